```python
import math
import jax, jax.numpy as jnp
from jax import lax
import numpy as np

D_MODEL = 1024
BATCH = 16
SEQ = 2048
DEPTH = 1

D_MIX = D_MODEL
D_SSM = D_MIX // 2
SSM_GROUP = 16
N_SSM_GROUPS = D_SSM // SSM_GROUP
SSM_STATE = 64
D_ATTN = D_MIX - D_SSM
HEAD_DIM = 64
N_HEADS = D_ATTN // HEAD_DIM
Q_BLOCK = 128
D_FF = 2752
CONV_WIDTH = 3
EPS = 1e-6
DT_MIN = 1e-3
DT_MAX = 1e-1
D_IN_PROJ = D_SSM + 3 * D_ATTN + N_HEADS

kernel_name = "hymba_s5_fox_convffn_block"


def rmsnorm(x, g):
    xf = x.astype(jnp.float32)
    y = xf * lax.rsqrt(jnp.mean(xf * xf, axis=-1, keepdims=True) + EPS)
    return (y * g.astype(jnp.float32)).astype(x.dtype)


def s5_mixer(u, lam_re, lam_im, b_re, b_im, c_re, c_im, d_skip, log_dt, w_glu, b_glu):
    f32 = jnp.float32
    bsz, L, _ = u.shape
    uf = u.astype(f32).reshape(bsz, L, N_SSM_GROUPS, SSM_GROUP)
    lr = lam_re.astype(f32)
    li = lam_im.astype(f32)
    dt = jnp.exp(log_dt.astype(f32))[:, None]
    mag = jnp.exp(lr * dt)
    ab_re = mag * jnp.cos(li * dt)
    ab_im = mag * jnp.sin(li * dt)
    nr = ab_re - 1.0
    ni = ab_im
    den = lr * lr + li * li
    q_re = (nr * lr + ni * li) / den
    q_im = (ni * lr - nr * li) / den
    br = b_re.astype(f32)
    bi = b_im.astype(f32)
    bb_re = q_re[..., None] * br - q_im[..., None] * bi
    bb_im = q_re[..., None] * bi + q_im[..., None] * br
    bu_re = jnp.einsum('blgh,gph->blgp', uf, bb_re)
    bu_im = jnp.einsum('blgh,gph->blgp', uf, bb_im)
    a_re = jnp.broadcast_to(ab_re, (1, L) + ab_re.shape)
    a_im = jnp.broadcast_to(ab_im, (1, L) + ab_im.shape)

    def combine(e1, e2):
        a1r, a1i, b1r, b1i = e1
        a2r, a2i, b2r, b2i = e2
        return (a2r * a1r - a2i * a1i,
                a2r * a1i + a2i * a1r,
                a2r * b1r - a2i * b1i + b2r,
                a2r * b1i + a2i * b1r + b2i)

    _, _, xr, xi = lax.associative_scan(combine, (a_re, a_im, bu_re, bu_im), axis=1)
    y = (jnp.einsum('blgp,ghp->blgh', xr, c_re.astype(f32))
         - jnp.einsum('blgp,ghp->blgh', xi, c_im.astype(f32))
         + d_skip.astype(f32).reshape(N_SSM_GROUPS, SSM_GROUP) * uf)
    y = jax.nn.gelu(y.reshape(bsz, L, D_SSM))
    return y * jax.nn.sigmoid(y @ w_glu.astype(f32) + b_glu.astype(f32))


def fox_attention(q, k, v, f_logit):
    f32 = jnp.float32
    L = q.shape[1]
    log_f = jax.nn.log_sigmoid(f_logit.astype(f32))
    cum = jnp.cumsum(log_f, axis=1).transpose(0, 2, 1)
    scale = HEAD_DIM ** -0.5
    outs = []
    for i in range(L // Q_BLOCK):
        q0 = i * Q_BLOCK
        kend = q0 + Q_BLOCK
        qb = q[:, q0:kend]
        kb = k[:, :kend]
        vb = v[:, :kend]
        s = jnp.einsum('bqhd,bkhd->bhqk', qb, kb).astype(f32) * scale
        s = s + cum[:, :, q0:kend, None] - cum[:, :, None, :kend]
        mask = (q0 + jnp.arange(Q_BLOCK))[:, None] >= jnp.arange(kend)[None, :]
        s = jnp.where(mask, s, -jnp.inf)
        p = jax.nn.softmax(s, axis=-1)
        outs.append(jnp.einsum('bhqk,bkhd->bqhd', p.astype(v.dtype), vb))
    return jnp.concatenate(outs, axis=1)


def conv_ffn(h, w_up, conv_w, conv_b, w_down):
    u = h @ w_up
    L = u.shape[1]
    up = jnp.pad(u, ((0, 0), (CONV_WIDTH - 1, 0), (0, 0)))
    c = up[:, 0:L] * conv_w[0]
    for kk in range(1, CONV_WIDTH):
        c = c + up[:, kk:kk + L] * conv_w[kk]
    c = c + conv_b
    gate, val = jnp.split(c, 2, axis=-1)
    return (jax.nn.silu(gate) * val) @ w_down


def _fwd_setup_inputs(seed: int = 0) -> dict:
    key = jax.random.key(seed)
    ks = jax.random.split(key, 24)
    f32 = jnp.float32
    nrm = lambda k, shape, s: jax.random.normal(k, shape, f32) * s
    G, P, H = N_SSM_GROUPS, SSM_STATE, SSM_GROUP
    x = jax.random.normal(ks[0], (BATCH, SEQ, D_MODEL), f32)
    norm_mix = 1.0 + nrm(ks[1], (DEPTH, D_MODEL), 0.02)
    w_in = nrm(ks[2], (DEPTH, D_MODEL, D_IN_PROJ), D_MODEL ** -0.5)
    b_forget = jax.random.uniform(ks[3], (DEPTH, N_HEADS), f32, 1.0, 5.0)
    lam_re = -0.5 + nrm(ks[4], (DEPTH, G, P), 0.01)
    lam_im = jnp.pi * jnp.arange(P, dtype=f32)[None, None, :] + nrm(ks[5], (DEPTH, G, P), 0.01)
    b_re = nrm(ks[6], (DEPTH, G, P, H), (2.0 * H) ** -0.5)
    b_im = nrm(ks[7], (DEPTH, G, P, H), (2.0 * H) ** -0.5)
    c_re = nrm(ks[8], (DEPTH, G, H, P), (2.0 * P) ** -0.5)
    c_im = nrm(ks[9], (DEPTH, G, H, P), (2.0 * P) ** -0.5)
    d_skip = nrm(ks[10], (DEPTH, D_SSM), 1.0)
    log_dt = jax.random.uniform(ks[11], (DEPTH, G), f32, math.log(DT_MIN), math.log(DT_MAX))
    w_glu = nrm(ks[12], (DEPTH, D_SSM, D_SSM), D_SSM ** -0.5)
    b_glu = nrm(ks[13], (DEPTH, D_SSM), 0.01)
    q_norm = 1.0 + nrm(ks[14], (DEPTH, HEAD_DIM), 0.02)
    k_norm = 1.0 + nrm(ks[15], (DEPTH, HEAD_DIM), 0.02)
    norm_out_ssm = 1.0 + nrm(ks[16], (DEPTH, D_SSM), 0.02)
    norm_out_attn = 1.0 + nrm(ks[17], (DEPTH, D_ATTN), 0.02)
    w_out = nrm(ks[18], (DEPTH, D_MIX, D_MODEL), D_MIX ** -0.5)
    norm_ffn = 1.0 + nrm(ks[19], (DEPTH, D_MODEL), 0.02)
    w_up = nrm(ks[20], (DEPTH, D_MODEL, 2 * D_FF), D_MODEL ** -0.5)
    conv_w = nrm(ks[21], (DEPTH, CONV_WIDTH, 2 * D_FF), CONV_WIDTH ** -0.5)
    conv_b = nrm(ks[22], (DEPTH, 2 * D_FF), 0.01)
    w_down = nrm(ks[23], (DEPTH, D_FF, D_MODEL), D_FF ** -0.5)
    return {"x": x, "norm_mix": norm_mix, "w_in": w_in, "b_forget": b_forget,
            "lam_re": lam_re, "lam_im": lam_im, "b_re": b_re, "b_im": b_im,
            "c_re": c_re, "c_im": c_im, "d_skip": d_skip, "log_dt": log_dt,
            "w_glu": w_glu, "b_glu": b_glu, "q_norm": q_norm, "k_norm": k_norm,
            "norm_out_ssm": norm_out_ssm, "norm_out_attn": norm_out_attn,
            "w_out": w_out, "norm_ffn": norm_ffn, "w_up": w_up, "conv_w": conv_w,
            "conv_b": conv_b, "w_down": w_down}


def _fwd_reference(x, norm_mix, w_in, b_forget, lam_re, lam_im, b_re, b_im, c_re, c_im,
              d_skip, log_dt, w_glu, b_glu, q_norm, k_norm, norm_out_ssm,
              norm_out_attn, w_out, norm_ffn, w_up, conv_w, conv_b, w_down):
    bsz, L, _ = x.shape
    h = x
    for l in range(DEPTH):
        hn = rmsnorm(h, norm_mix[l])
        proj = hn @ w_in[l]
        o = 0
        u_ssm = proj[..., o:o + D_SSM]; o += D_SSM
        q = proj[..., o:o + D_ATTN].reshape(bsz, L, N_HEADS, HEAD_DIM); o += D_ATTN
        k = proj[..., o:o + D_ATTN].reshape(bsz, L, N_HEADS, HEAD_DIM); o += D_ATTN
        v = proj[..., o:o + D_ATTN].reshape(bsz, L, N_HEADS, HEAD_DIM); o += D_ATTN
        f_logit = proj[..., o:o + N_HEADS] + b_forget[l]

        y_ssm = s5_mixer(u_ssm, lam_re[l], lam_im[l], b_re[l], b_im[l], c_re[l],
                         c_im[l], d_skip[l], log_dt[l], w_glu[l], b_glu[l]).astype(h.dtype)
        q = rmsnorm(q, q_norm[l])
        k = rmsnorm(k, k_norm[l])
        y_attn = fox_attention(q, k, v, f_logit).reshape(bsz, L, D_ATTN)

        mixed = jnp.concatenate([rmsnorm(y_ssm, norm_out_ssm[l]),
                                 rmsnorm(y_attn, norm_out_attn[l])], axis=-1)
        h = h + mixed @ w_out[l]
        h = h + conv_ffn(rmsnorm(h, norm_ffn[l]), w_up[l], conv_w[l], conv_b[l], w_down[l])
    return h


import jax as _jax
import jax.numpy as _jnp

TWIN_FORMAT = 'train_step'
FWD_PARAMS = ['x', 'norm_mix', 'w_in', 'b_forget', 'lam_re', 'lam_im', 'b_re', 'b_im', 'c_re', 'c_im', 'd_skip', 'log_dt', 'w_glu', 'b_glu', 'q_norm', 'k_norm', 'norm_out_ssm', 'norm_out_attn', 'w_out', 'norm_ffn', 'w_up', 'conv_w', 'conv_b', 'w_down']
TWIN_WEIGHTS = ['norm_mix', 'w_in', 'b_forget', 'lam_re', 'lam_im', 'b_re', 'b_im', 'c_re', 'c_im', 'd_skip', 'log_dt', 'w_glu', 'b_glu', 'q_norm', 'k_norm', 'norm_out_ssm', 'norm_out_attn', 'w_out', 'norm_ffn', 'w_up', 'conv_w', 'conv_b', 'w_down']
TWIN_DIFF_INPUT = 'x'
TWIN_INPUTS = ['x', 'norm_mix', 'w_in', 'b_forget', 'lam_re', 'lam_im', 'b_re', 'b_im', 'c_re', 'c_im', 'd_skip', 'log_dt', 'w_glu', 'b_glu', 'q_norm', 'k_norm', 'norm_out_ssm', 'norm_out_attn', 'w_out', 'norm_ffn', 'w_up', 'conv_w', 'conv_b', 'w_down', 'loss_target', 'm_norm_mix', 'm_w_in', 'm_b_forget', 'm_lam_re', 'm_lam_im', 'm_b_re', 'm_b_im', 'm_c_re', 'm_c_im', 'm_d_skip', 'm_log_dt', 'm_w_glu', 'm_b_glu', 'm_q_norm', 'm_k_norm', 'm_norm_out_ssm', 'm_norm_out_attn', 'm_w_out', 'm_norm_ffn', 'm_w_up', 'm_conv_w', 'm_conv_b', 'm_w_down', 'v_norm_mix', 'v_w_in', 'v_b_forget', 'v_lam_re', 'v_lam_im', 'v_b_re', 'v_b_im', 'v_c_re', 'v_c_im', 'v_d_skip', 'v_log_dt', 'v_w_glu', 'v_b_glu', 'v_q_norm', 'v_k_norm', 'v_norm_out_ssm', 'v_norm_out_attn', 'v_w_out', 'v_norm_ffn', 'v_w_up', 'v_conv_w', 'v_conv_b', 'v_w_down']
TWIN_OUTPUTS = ['loss', 'grad_x', 'grad_norm_mix', 'grad_w_in', 'grad_b_forget', 'grad_lam_re', 'grad_lam_im', 'grad_b_re', 'grad_b_im', 'grad_c_re', 'grad_c_im', 'grad_d_skip', 'grad_log_dt', 'grad_w_glu', 'grad_b_glu', 'grad_q_norm', 'grad_k_norm', 'grad_norm_out_ssm', 'grad_norm_out_attn', 'grad_w_out', 'grad_norm_ffn', 'grad_w_up', 'grad_conv_w', 'grad_conv_b', 'grad_w_down', 'delta_norm_mix', 'delta_w_in', 'delta_b_forget', 'delta_lam_re', 'delta_lam_im', 'delta_b_re', 'delta_b_im', 'delta_c_re', 'delta_c_im', 'delta_d_skip', 'delta_log_dt', 'delta_w_glu', 'delta_b_glu', 'delta_q_norm', 'delta_k_norm', 'delta_norm_out_ssm', 'delta_norm_out_attn', 'delta_w_out', 'delta_norm_ffn', 'delta_w_up', 'delta_conv_w', 'delta_conv_b', 'delta_w_down', 'new_m_norm_mix', 'new_m_w_in', 'new_m_b_forget', 'new_m_lam_re', 'new_m_lam_im', 'new_m_b_re', 'new_m_b_im', 'new_m_c_re', 'new_m_c_im', 'new_m_d_skip', 'new_m_log_dt', 'new_m_w_glu', 'new_m_b_glu', 'new_m_q_norm', 'new_m_k_norm', 'new_m_norm_out_ssm', 'new_m_norm_out_attn', 'new_m_w_out', 'new_m_norm_ffn', 'new_m_w_up', 'new_m_conv_w', 'new_m_conv_b', 'new_m_w_down', 'new_v_norm_mix', 'new_v_w_in', 'new_v_b_forget', 'new_v_lam_re', 'new_v_lam_im', 'new_v_b_re', 'new_v_b_im', 'new_v_c_re', 'new_v_c_im', 'new_v_d_skip', 'new_v_log_dt', 'new_v_w_glu', 'new_v_b_glu', 'new_v_q_norm', 'new_v_k_norm', 'new_v_norm_out_ssm', 'new_v_norm_out_attn', 'new_v_w_out', 'new_v_norm_ffn', 'new_v_w_up', 'new_v_conv_w', 'new_v_conv_b', 'new_v_w_down']
TWIN_LEAF_KINDS = {'loss': 'loss', 'grad_x': 'grad_x', 'grad_norm_mix': 'grad_w', 'grad_w_in': 'grad_w', 'grad_b_forget': 'grad_w', 'grad_lam_re': 'grad_w', 'grad_lam_im': 'grad_w', 'grad_b_re': 'grad_w', 'grad_b_im': 'grad_w', 'grad_c_re': 'grad_w', 'grad_c_im': 'grad_w', 'grad_d_skip': 'grad_w', 'grad_log_dt': 'grad_w', 'grad_w_glu': 'grad_w', 'grad_b_glu': 'grad_w', 'grad_q_norm': 'grad_w', 'grad_k_norm': 'grad_w', 'grad_norm_out_ssm': 'grad_w', 'grad_norm_out_attn': 'grad_w', 'grad_w_out': 'grad_w', 'grad_norm_ffn': 'grad_w', 'grad_w_up': 'grad_w', 'grad_conv_w': 'grad_w', 'grad_conv_b': 'grad_w', 'grad_w_down': 'grad_w', 'delta_norm_mix': 'delta_w', 'delta_w_in': 'delta_w', 'delta_b_forget': 'delta_w', 'delta_lam_re': 'delta_w', 'delta_lam_im': 'delta_w', 'delta_b_re': 'delta_w', 'delta_b_im': 'delta_w', 'delta_c_re': 'delta_w', 'delta_c_im': 'delta_w', 'delta_d_skip': 'delta_w', 'delta_log_dt': 'delta_w', 'delta_w_glu': 'delta_w', 'delta_b_glu': 'delta_w', 'delta_q_norm': 'delta_w', 'delta_k_norm': 'delta_w', 'delta_norm_out_ssm': 'delta_w', 'delta_norm_out_attn': 'delta_w', 'delta_w_out': 'delta_w', 'delta_norm_ffn': 'delta_w', 'delta_w_up': 'delta_w', 'delta_conv_w': 'delta_w', 'delta_conv_b': 'delta_w', 'delta_w_down': 'delta_w', 'new_m_norm_mix': 'new_m', 'new_m_w_in': 'new_m', 'new_m_b_forget': 'new_m', 'new_m_lam_re': 'new_m', 'new_m_lam_im': 'new_m', 'new_m_b_re': 'new_m', 'new_m_b_im': 'new_m', 'new_m_c_re': 'new_m', 'new_m_c_im': 'new_m', 'new_m_d_skip': 'new_m', 'new_m_log_dt': 'new_m', 'new_m_w_glu': 'new_m', 'new_m_b_glu': 'new_m', 'new_m_q_norm': 'new_m', 'new_m_k_norm': 'new_m', 'new_m_norm_out_ssm': 'new_m', 'new_m_norm_out_attn': 'new_m', 'new_m_w_out': 'new_m', 'new_m_norm_ffn': 'new_m', 'new_m_w_up': 'new_m', 'new_m_conv_w': 'new_m', 'new_m_conv_b': 'new_m', 'new_m_w_down': 'new_m', 'new_v_norm_mix': 'new_v', 'new_v_w_in': 'new_v', 'new_v_b_forget': 'new_v', 'new_v_lam_re': 'new_v', 'new_v_lam_im': 'new_v', 'new_v_b_re': 'new_v', 'new_v_b_im': 'new_v', 'new_v_c_re': 'new_v', 'new_v_c_im': 'new_v', 'new_v_d_skip': 'new_v', 'new_v_log_dt': 'new_v', 'new_v_w_glu': 'new_v', 'new_v_b_glu': 'new_v', 'new_v_q_norm': 'new_v', 'new_v_k_norm': 'new_v', 'new_v_norm_out_ssm': 'new_v', 'new_v_norm_out_attn': 'new_v', 'new_v_w_out': 'new_v', 'new_v_norm_ffn': 'new_v', 'new_v_w_up': 'new_v', 'new_v_conv_w': 'new_v', 'new_v_conv_b': 'new_v', 'new_v_w_down': 'new_v'}


def _forward(args):
    return _fwd_reference(*[args[k] for k in FWD_PARAMS])


def _output_shape():
    out = _jax.eval_shape(lambda: _forward(_fwd_setup_inputs(0)))
    return out.shape, out.dtype

N_MICROBATCH = 1
ADAM_LR = 0.001
ADAM_B1 = 0.9
ADAM_B2 = 0.999
ADAM_EPS = 1e-08
ADAM_WD = 0.01
ADAM_STEP = 10
PER_EXAMPLE_BATCH_AXIS = {'x': 0, 'loss_target': 0}
SHARED_INPUTS = []
_WEIGHT_DTYPES = {'norm_mix': _jnp.float32, 'w_in': _jnp.float32, 'b_forget': _jnp.float32, 'lam_re': _jnp.float32, 'lam_im': _jnp.float32, 'b_re': _jnp.float32, 'b_im': _jnp.float32, 'c_re': _jnp.float32, 'c_im': _jnp.float32, 'd_skip': _jnp.float32, 'log_dt': _jnp.float32, 'w_glu': _jnp.float32, 'b_glu': _jnp.float32, 'q_norm': _jnp.float32, 'k_norm': _jnp.float32, 'norm_out_ssm': _jnp.float32, 'norm_out_attn': _jnp.float32, 'w_out': _jnp.float32, 'norm_ffn': _jnp.float32, 'w_up': _jnp.float32, 'conv_w': _jnp.float32, 'conv_b': _jnp.float32, 'w_down': _jnp.float32}
MOMENT_SCALE = {'norm_mix': 7.986740e-01, 'w_in': 5.360242e-01, 'b_forget': 1.663315e+00, 'lam_re': 3.195620e-02, 'lam_im': 2.522228e-02, 'b_re': 1.811467e-02, 'b_im': 1.815537e-02, 'c_re': 3.784603e-02, 'c_im': 3.958424e-02, 'd_skip': 7.103849e+00, 'log_dt': 2.502735e+01, 'w_glu': 9.716212e-01, 'b_glu': 2.979239e+00, 'q_norm': 9.723075e-01, 'k_norm': 9.740266e-01, 'norm_out_ssm': 5.481924e+01, 'norm_out_attn': 3.986986e+01, 'w_out': 4.310731e+00, 'norm_ffn': 2.570746e+01, 'w_up': 1.108934e+00, 'conv_w': 4.024150e+00, 'conv_b': 3.719349e+00, 'w_down': 6.047253e-01}


def _to_microbatches(a, axis):
    t = _jnp.moveaxis(a, axis, 0)
    t = t.reshape((N_MICROBATCH, t.shape[0] // N_MICROBATCH) + t.shape[1:])
    return _jnp.moveaxis(t, 1, axis + 1)


def setup_inputs(seed: int = 0) -> dict:
    inp = _fwd_setup_inputs(seed)
    key = _jax.random.fold_in(_jax.random.key(seed), 7919)
    shape, _ = _output_shape()
    out = dict(inp)
    out["loss_target"] = _jax.random.normal(_jax.random.fold_in(key, 0), shape, _jnp.float32)
    for i, name in enumerate(TWIN_WEIGHTS):
        w = inp[name].astype(_jnp.float32)
        if MOMENT_SCALE is None:
            s = _jnp.sqrt(_jnp.mean(_jnp.square(w)) + 1e-30)
        else:
            s = MOMENT_SCALE[name]
        km, kv = _jax.random.split(_jax.random.fold_in(key, i + 1))
        out[name] = w
        out["m_" + name] = s * _jax.random.normal(km, w.shape, _jnp.float32)
        out["v_" + name] = (s * s) * _jax.random.uniform(kv, w.shape, _jnp.float32, 0.5, 1.5)
    if N_MICROBATCH > 1:
        for name, axis in PER_EXAMPLE_BATCH_AXIS.items():
            out[name] = _to_microbatches(out[name], axis)
    return {'x': out['x'], 'norm_mix': out['norm_mix'], 'w_in': out['w_in'], 'b_forget': out['b_forget'], 'lam_re': out['lam_re'], 'lam_im': out['lam_im'], 'b_re': out['b_re'], 'b_im': out['b_im'], 'c_re': out['c_re'], 'c_im': out['c_im'], 'd_skip': out['d_skip'], 'log_dt': out['log_dt'], 'w_glu': out['w_glu'], 'b_glu': out['b_glu'], 'q_norm': out['q_norm'], 'k_norm': out['k_norm'], 'norm_out_ssm': out['norm_out_ssm'], 'norm_out_attn': out['norm_out_attn'], 'w_out': out['w_out'], 'norm_ffn': out['norm_ffn'], 'w_up': out['w_up'], 'conv_w': out['conv_w'], 'conv_b': out['conv_b'], 'w_down': out['w_down'], 'loss_target': out['loss_target'], 'm_norm_mix': out['m_norm_mix'], 'm_w_in': out['m_w_in'], 'm_b_forget': out['m_b_forget'], 'm_lam_re': out['m_lam_re'], 'm_lam_im': out['m_lam_im'], 'm_b_re': out['m_b_re'], 'm_b_im': out['m_b_im'], 'm_c_re': out['m_c_re'], 'm_c_im': out['m_c_im'], 'm_d_skip': out['m_d_skip'], 'm_log_dt': out['m_log_dt'], 'm_w_glu': out['m_w_glu'], 'm_b_glu': out['m_b_glu'], 'm_q_norm': out['m_q_norm'], 'm_k_norm': out['m_k_norm'], 'm_norm_out_ssm': out['m_norm_out_ssm'], 'm_norm_out_attn': out['m_norm_out_attn'], 'm_w_out': out['m_w_out'], 'm_norm_ffn': out['m_norm_ffn'], 'm_w_up': out['m_w_up'], 'm_conv_w': out['m_conv_w'], 'm_conv_b': out['m_conv_b'], 'm_w_down': out['m_w_down'], 'v_norm_mix': out['v_norm_mix'], 'v_w_in': out['v_w_in'], 'v_b_forget': out['v_b_forget'], 'v_lam_re': out['v_lam_re'], 'v_lam_im': out['v_lam_im'], 'v_b_re': out['v_b_re'], 'v_b_im': out['v_b_im'], 'v_c_re': out['v_c_re'], 'v_c_im': out['v_c_im'], 'v_d_skip': out['v_d_skip'], 'v_log_dt': out['v_log_dt'], 'v_w_glu': out['v_w_glu'], 'v_b_glu': out['v_b_glu'], 'v_q_norm': out['v_q_norm'], 'v_k_norm': out['v_k_norm'], 'v_norm_out_ssm': out['v_norm_out_ssm'], 'v_norm_out_attn': out['v_norm_out_attn'], 'v_w_out': out['v_w_out'], 'v_norm_ffn': out['v_norm_ffn'], 'v_w_up': out['v_w_up'], 'v_conv_w': out['v_conv_w'], 'v_conv_b': out['v_conv_b'], 'v_w_down': out['v_w_down']}


def _loss(weights, diff, rest, loss_target):
    with _jax.named_scope("forward"):
        args = {**rest, TWIN_DIFF_INPUT: diff, **{k: w.astype(_WEIGHT_DTYPES[k]) for k, w in weights.items()}}
        y = _forward(args)
    with _jax.named_scope("loss_head"):
        err = _jnp.square(y.astype(_jnp.float32) - loss_target)
        return 0.5 * _jnp.sum(_jnp.mean(err, axis=-1)) if err.ndim else 0.5 * err


def _adamw(w, g, m, v):
    m = ADAM_B1 * m + (1.0 - ADAM_B1) * g
    v = ADAM_B2 * v + (1.0 - ADAM_B2) * _jnp.square(g)
    m_hat = m / (1.0 - ADAM_B1 ** ADAM_STEP)
    v_hat = v / (1.0 - ADAM_B2 ** ADAM_STEP)
    delta = -ADAM_LR * (m_hat / (_jnp.sqrt(v_hat) + ADAM_EPS) + ADAM_WD * w)
    return delta, m, v


def reference(x, norm_mix, w_in, b_forget, lam_re, lam_im, b_re, b_im, c_re, c_im, d_skip, log_dt, w_glu, b_glu, q_norm, k_norm, norm_out_ssm, norm_out_attn, w_out, norm_ffn, w_up, conv_w, conv_b, w_down, loss_target, m_norm_mix, m_w_in, m_b_forget, m_lam_re, m_lam_im, m_b_re, m_b_im, m_c_re, m_c_im, m_d_skip, m_log_dt, m_w_glu, m_b_glu, m_q_norm, m_k_norm, m_norm_out_ssm, m_norm_out_attn, m_w_out, m_norm_ffn, m_w_up, m_conv_w, m_conv_b, m_w_down, v_norm_mix, v_w_in, v_b_forget, v_lam_re, v_lam_im, v_b_re, v_b_im, v_c_re, v_c_im, v_d_skip, v_log_dt, v_w_glu, v_b_glu, v_q_norm, v_k_norm, v_norm_out_ssm, v_norm_out_attn, v_w_out, v_norm_ffn, v_w_up, v_conv_w, v_conv_b, v_w_down):
    given = dict(x=x, norm_mix=norm_mix, w_in=w_in, b_forget=b_forget, lam_re=lam_re, lam_im=lam_im, b_re=b_re, b_im=b_im, c_re=c_re, c_im=c_im, d_skip=d_skip, log_dt=log_dt, w_glu=w_glu, b_glu=b_glu, q_norm=q_norm, k_norm=k_norm, norm_out_ssm=norm_out_ssm, norm_out_attn=norm_out_attn, w_out=w_out, norm_ffn=norm_ffn, w_up=w_up, conv_w=conv_w, conv_b=conv_b, w_down=w_down, loss_target=loss_target, m_norm_mix=m_norm_mix, m_w_in=m_w_in, m_b_forget=m_b_forget, m_lam_re=m_lam_re, m_lam_im=m_lam_im, m_b_re=m_b_re, m_b_im=m_b_im, m_c_re=m_c_re, m_c_im=m_c_im, m_d_skip=m_d_skip, m_log_dt=m_log_dt, m_w_glu=m_w_glu, m_b_glu=m_b_glu, m_q_norm=m_q_norm, m_k_norm=m_k_norm, m_norm_out_ssm=m_norm_out_ssm, m_norm_out_attn=m_norm_out_attn, m_w_out=m_w_out, m_norm_ffn=m_norm_ffn, m_w_up=m_w_up, m_conv_w=m_conv_w, m_conv_b=m_conv_b, m_w_down=m_w_down, v_norm_mix=v_norm_mix, v_w_in=v_w_in, v_b_forget=v_b_forget, v_lam_re=v_lam_re, v_lam_im=v_lam_im, v_b_re=v_b_re, v_b_im=v_b_im, v_c_re=v_c_re, v_c_im=v_c_im, v_d_skip=v_d_skip, v_log_dt=v_log_dt, v_w_glu=v_w_glu, v_b_glu=v_b_glu, v_q_norm=v_q_norm, v_k_norm=v_k_norm, v_norm_out_ssm=v_norm_out_ssm, v_norm_out_attn=v_norm_out_attn, v_w_out=v_w_out, v_norm_ffn=v_norm_ffn, v_w_up=v_w_up, v_conv_w=v_conv_w, v_conv_b=v_conv_b, v_w_down=v_w_down)
    weights = {n: given[n] for n in TWIN_WEIGHTS}
    shared = {n: given[n] for n in SHARED_INPUTS}
    per_example = {n: given[n] for n in ['x']}
    grad_fn = _jax.value_and_grad(_loss, argnums=(0, 1))

    def one_microbatch(ex, loss_target):
        ex = dict(ex)
        diff = ex.pop(TWIN_DIFF_INPUT)
        return grad_fn(weights, diff, {**shared, **ex}, loss_target)

    if N_MICROBATCH == 1:
        loss, (grad_w, grad_x) = one_microbatch(per_example, given["loss_target"])
    else:
        def body(carry, xs):
            loss_sum, grad_sum = carry
            l_k, (gw_k, gx_k) = one_microbatch(xs[0], xs[1])
            with _jax.named_scope("update"):
                return (loss_sum + l_k, _jax.tree.map(_jnp.add, grad_sum, gw_k)), gx_k

        init = (_jnp.zeros((), _jnp.float32), _jax.tree.map(_jnp.zeros_like, weights))
        (loss, grad_w), grad_x = _jax.lax.scan(body, init, (per_example, given["loss_target"]))
    with _jax.named_scope("update"):
        delta_w, new_m, new_v = {}, {}, {}
        for n in TWIN_WEIGHTS:
            delta_w[n], new_m[n], new_v[n] = _adamw(weights[n], grad_w[n], given["m_" + n], given["v_" + n])
    return (loss, grad_x, *[grad_w[n] for n in TWIN_WEIGHTS], *[delta_w[n] for n in TWIN_WEIGHTS],
            *[new_m[n] for n in TWIN_WEIGHTS], *[new_v[n] for n in TWIN_WEIGHTS])
```

```python
import functools
import math

import jax
import jax.numpy as jnp
from jax import lax
from jax.experimental import pallas as pl
from jax.experimental.pallas import tpu as pltpu

F32 = jnp.float32
BF16 = jnp.bfloat16
HIGHEST = lax.Precision.HIGHEST

N_DEV = 8
D_MODEL = 1024
D_SSM = 512
D_ATTN = 512
N_HEADS = 8
HEAD_DIM = 64
N_GROUPS = 32
SSM_GROUP = 16
SSM_STATE = 64
D_FF = 2752
D_FF_PAD = 2816
D_IN = 2056
D_IN_PAD = 2176
EPS = 1e-6
LANES = 128
SUBLANES = 8
VMEM_LIMIT = 56 * 1024 * 1024

ADAM_LR = 0.001
ADAM_B1 = 0.9
ADAM_B2 = 0.999
ADAM_EPS = 1e-08
ADAM_WD = 0.01
ADAM_STEP = 10


def _cparams(*sem):
    return pltpu.CompilerParams(dimension_semantics=sem, vmem_limit_bytes=VMEM_LIMIT)


def _dot(a, b, **kw):
    return jnp.dot(a, b, preferred_element_type=F32, **kw)


def _dot_nt(a, b):
    return lax.dot_general(a, b, (((1,), (1,)), ((), ())), preferred_element_type=F32)


def _dot_tn(a, b):
    return lax.dot_general(a, b, (((0,), (0,)), ((), ())), preferred_element_type=F32)


def _rms(x, g):
    return x * lax.rsqrt(jnp.mean(x * x, axis=-1, keepdims=True) + EPS) * g


def _headnorm(q, avg, g):
    ms = jnp.dot(q * q, avg, preferred_element_type=F32, precision=HIGHEST)
    return q * lax.rsqrt(ms + EPS) * g


def _exchange(srcs, scatter_flags, name):
    n = len(srcs)
    out_shape = []
    for s, sc in zip(srcs, scatter_flags):
        shp = s.shape if sc else (N_DEV,) + s.shape
        out_shape.append(jax.ShapeDtypeStruct(shp, s.dtype))

    def body(*refs):
        src = refs[:n]
        dst = refs[n:2 * n]
        send_sems, recv_sems, loc_sems = refs[2 * n:]
        x, y, c = lax.axis_index("x"), lax.axis_index("y"), lax.axis_index("c")
        me = 4 * x + 2 * y + c
        peers = []
        for j in range(1, N_DEV):
            px = 1 - x if (j >> 2) & 1 else x
            py = 1 - y if (j >> 1) & 1 else y
            pc = 1 - c if j & 1 else c
            peers.append(((px, py, pc), 4 * px + 2 * py + pc))
        local, sends = [], []
        for k in range(n):
            own = src[k].at[me] if scatter_flags[k] else src[k]
            lc = pltpu.make_async_copy(own, dst[k].at[me], loc_sems.at[k])
            lc.start()
            local.append(lc)
            for j, (pid, pidx) in enumerate(peers):
                s = src[k].at[pidx] if scatter_flags[k] else src[k]
                cp = pltpu.make_async_remote_copy(
                    src_ref=s, dst_ref=dst[k].at[me], send_sem=send_sems.at[k, j], recv_sem=recv_sems.at[k, j],
                    device_id=pid, device_id_type=pl.DeviceIdType.MESH)
                cp.start()
                sends.append(cp)
        for k in range(n):
            for j, (pid, pidx) in enumerate(peers):
                s = src[k].at[pidx] if scatter_flags[k] else src[k]
                pltpu.make_async_remote_copy(
                    src_ref=s, dst_ref=dst[k].at[pidx], send_sem=send_sems.at[k, j], recv_sem=recv_sems.at[k, j],
                    device_id=pid, device_id_type=pl.DeviceIdType.MESH).wait_recv()
        for cp in sends:
            cp.wait_send()
        for lc in local:
            lc.wait()

    any_spec = pl.BlockSpec(memory_space=pl.ANY)
    return pl.pallas_call(
        body, name=name, out_shape=tuple(out_shape),
        in_specs=[any_spec] * n, out_specs=tuple([any_spec] * n),
        scratch_shapes=[pltpu.SemaphoreType.DMA((n, N_DEV - 1)), pltpu.SemaphoreType.DMA((n, N_DEV - 1)),
                        pltpu.SemaphoreType.DMA((n,))],
        compiler_params=pltpu.CompilerParams(has_side_effects=True),
    )(*srcs)


def _tn_matmul(a, b, name, tk, tm, tn=512):
    n_tok, k_dim = a.shape
    m_dim = b.shape[1]
    grid = (k_dim // tk, m_dim // tm, n_tok // tn)

    def body(a_ref, b_ref, o_ref):
        @pl.when(pl.program_id(2) == 0)
        def _():
            o_ref[...] = jnp.zeros_like(o_ref)
        o_ref[...] += _dot_tn(a_ref[...].astype(BF16), b_ref[...].astype(BF16))

    return pl.pallas_call(
        body, name=name, grid=grid,
        in_specs=[pl.BlockSpec((tn, tk), lambda i, j, k: (k, i)), pl.BlockSpec((tn, tm), lambda i, j, k: (k, j))],
        out_specs=pl.BlockSpec((tk, tm), lambda i, j, k: (i, j)),
        out_shape=jax.ShapeDtypeStruct((k_dim, m_dim), F32),
        compiler_params=_cparams("parallel", "parallel", "arbitrary"),
    )(a, b)


def _adam_math(g, w, m, v):
    m = ADAM_B1 * m + (1.0 - ADAM_B1) * g
    v = ADAM_B2 * v + (1.0 - ADAM_B2) * (g * g)
    m_hat = m / (1.0 - ADAM_B1 ** ADAM_STEP)
    v_hat = v / (1.0 - ADAM_B2 ** ADAM_STEP)
    delta = -ADAM_LR * (m_hat / (jnp.sqrt(v_hat) + ADAM_EPS) + ADAM_WD * w)
    return delta, m, v


def _adam_sharded(land, w, m, v, name, tr):
    r, c = w.shape

    def body(l_ref, w_ref, m_ref, v_ref, g_ref, d_ref, nm_ref, nv_ref):
        g = l_ref[0]
        for s in range(1, N_DEV):
            g = g + l_ref[s]
        d, nm, nv = _adam_math(g, w_ref[...], m_ref[...], v_ref[...])
        g_ref[...] = g
        d_ref[...] = d
        nm_ref[...] = nm
        nv_ref[...] = nv

    spec = pl.BlockSpec((tr, c), lambda i: (i, 0))
    return pl.pallas_call(
        body, name=name, grid=(r // tr,),
        in_specs=[pl.BlockSpec((N_DEV, tr, c), lambda i: (0, i, 0)), spec, spec, spec],
        out_specs=(spec, spec, spec, spec),
        out_shape=tuple(jax.ShapeDtypeStruct((r, c), F32) for _ in range(4)),
        compiler_params=_cparams("parallel"),
    )(land, w, m, v)


def _sum_partials(parts, name):
    _, r, c = parts.shape

    def body(p_ref, o_ref):
        g = p_ref[0]
        for s in range(1, N_DEV):
            g = g + p_ref[s]
        o_ref[...] = g

    return pl.pallas_call(body, name=name, out_shape=jax.ShapeDtypeStruct((r, c), F32),
                          compiler_params=pltpu.CompilerParams(vmem_limit_bytes=VMEM_LIMIT))(parts)


def _adam_packed(g, w, m, v, name):
    def body(g_ref, w_ref, m_ref, v_ref, d_ref, nm_ref, nv_ref):
        d, nm, nv = _adam_math(g_ref[...], w_ref[...], m_ref[...], v_ref[...])
        d_ref[...] = d
        nm_ref[...] = nm
        nv_ref[...] = nv

    return pl.pallas_call(body, name=name, out_shape=tuple(jax.ShapeDtypeStruct(g.shape, F32) for _ in range(3)),
                          compiler_params=pltpu.CompilerParams(vmem_limit_bytes=VMEM_LIMIT))(g, w, m, v)


def _inproj_fwd(x, g, w_in, avg, qg, kg, tm=512):
    n = x.shape[0]

    def body(x_ref, g_ref, w_ref, a_ref, qg_ref, kg_ref, hn_ref, u_ref, qkv_ref, raw_ref, fl_ref):
        hn = _rms(x_ref[...], g_ref[...]).astype(BF16)
        hn_ref[...] = hn
        proj = _dot(hn, w_ref[...])
        u_ref[...] = proj[:, 0:512]
        q = proj[:, 512:1024]
        k = proj[:, 1024:1536]
        raw_ref[:, 0:512] = q
        raw_ref[:, 512:1024] = k
        qkv_ref[:, 0:512] = _headnorm(q, a_ref[...], qg_ref[...])
        qkv_ref[:, 512:1024] = _headnorm(k, a_ref[...], kg_ref[...])
        qkv_ref[:, 1024:1536] = proj[:, 1536:2048]
        fl_ref[...] = proj[:, 2048:D_IN_PAD]

    row = lambda w: pl.BlockSpec((tm, w), lambda i: (i, 0))
    full = lambda a: pl.BlockSpec(a.shape, lambda i: (0,) * a.ndim)
    return pl.pallas_call(
        body, name="inproj_fwd", grid=(n // tm,),
        in_specs=[row(D_MODEL), full(g), full(w_in), full(avg), full(qg), full(kg)],
        out_specs=(row(D_MODEL), row(512), row(1536), row(1024), row(LANES)),
        out_shape=(jax.ShapeDtypeStruct((n, D_MODEL), BF16), jax.ShapeDtypeStruct((n, 512), F32),
                   jax.ShapeDtypeStruct((n, 1536), F32), jax.ShapeDtypeStruct((n, 1024), F32),
                   jax.ShapeDtypeStruct((n, LANES), F32)),
        compiler_params=_cparams("parallel"),
    )(x, g, w_in, avg, qg, kg)


def _inproj_bwd(x, g, w_in, avg, qg, kg, raw, du, dqn, dkn, dv, dfl, dres, tm=512):
    n = x.shape[0]

    def body(x_ref, g_ref, w_ref, a_ref, qg_ref, kg_ref, raw_ref, du_ref, dqn_ref, dkn_ref, dv_ref, dfl_ref, dres_ref,
             dx_ref, dproj_ref, dg_ref, dqg_ref, dkg_ref):
        @pl.when(pl.program_id(0) == 0)
        def _():
            dg_ref[...] = jnp.zeros_like(dg_ref)
            dqg_ref[...] = jnp.zeros_like(dqg_ref)
            dkg_ref[...] = jnp.zeros_like(dkg_ref)
        avg_m = a_ref[...]
        _, vjp_q = jax.vjp(lambda q, gg: _headnorm(q, avg_m, gg), raw_ref[:, 0:512], qg_ref[...])
        dq, dqg = vjp_q(dqn_ref[...])
        _, vjp_k = jax.vjp(lambda k, gg: _headnorm(k, avg_m, gg), raw_ref[:, 512:1024], kg_ref[...])
        dk, dkg = vjp_k(dkn_ref[...])
        dproj = jnp.concatenate([du_ref[...], dq, dk, dv_ref[...], dfl_ref[...]], axis=1).astype(BF16)
        dproj_ref[...] = dproj
        dhn = _dot_nt(dproj, w_ref[...])
        _, vjp_x = jax.vjp(_rms, x_ref[...], g_ref[...])
        dxn, dg = vjp_x(dhn)
        dx_ref[...] = dxn + dres_ref[...]
        dg_ref[...] += dg
        dqg_ref[...] += dqg
        dkg_ref[...] += dkg

    row = lambda w: pl.BlockSpec((tm, w), lambda i: (i, 0))
    full = lambda a: pl.BlockSpec(a.shape, lambda i: (0,) * a.ndim)
    vec = lambda w: pl.BlockSpec((1, w), lambda i: (0, 0))
    return pl.pallas_call(
        body, name="inproj_bwd", grid=(n // tm,),
        in_specs=[row(D_MODEL), full(g), full(w_in), full(avg), full(qg), full(kg), row(1024), row(512), row(512),
                  row(512), row(512), row(LANES), row(D_MODEL)],
        out_specs=(row(D_MODEL), row(D_IN_PAD), vec(D_MODEL), vec(512), vec(512)),
        out_shape=(jax.ShapeDtypeStruct((n, D_MODEL), F32), jax.ShapeDtypeStruct((n, D_IN_PAD), BF16),
                   jax.ShapeDtypeStruct((1, D_MODEL), F32), jax.ShapeDtypeStruct((1, 512), F32),
                   jax.ShapeDtypeStruct((1, 512), F32)),
        compiler_params=_cparams("arbitrary"),
    )(x, g, w_in, avg, qg, kg, raw, du, dqn, dkn, dv, dfl, dres)


def _glu_fwd(yc, wg, bg, tm=512):
    n = yc.shape[0]

    def body(yc_ref, w_ref, b_ref, ys_ref):
        gl = jax.nn.gelu(yc_ref[...])
        z = _dot(gl.astype(BF16), w_ref[...]) + b_ref[...]
        ys_ref[...] = gl * jax.nn.sigmoid(z)

    row = pl.BlockSpec((tm, 512), lambda i: (i, 0))
    full = lambda a: pl.BlockSpec(a.shape, lambda i: (0,) * a.ndim)
    return pl.pallas_call(
        body, name="glu_fwd", grid=(n // tm,), in_specs=[row, full(wg), full(bg)], out_specs=row,
        out_shape=jax.ShapeDtypeStruct((n, 512), F32), compiler_params=_cparams("parallel"),
    )(yc, wg, bg)


def _glu_bwd(yc, dys, wg, bg, tm=512):
    n = yc.shape[0]

    def body(yc_ref, dys_ref, w_ref, b_ref, dyc_ref, gl_ref, dz_ref, db_ref):
        @pl.when(pl.program_id(0) == 0)
        def _():
            db_ref[...] = jnp.zeros_like(db_ref)
        gl, vjp_gelu = jax.vjp(jax.nn.gelu, yc_ref[...])
        glb = gl.astype(BF16)
        z = _dot(glb, w_ref[...]) + b_ref[...]
        s = jax.nn.sigmoid(z)
        dys = dys_ref[...]
        dz = dys * gl * s * (1.0 - s)
        dzb = dz.astype(BF16)
        dgl = dys * s + _dot_nt(dzb, w_ref[...])
        dyc_ref[...] = vjp_gelu(dgl)[0]
        gl_ref[...] = glb
        dz_ref[...] = dzb
        db_ref[...] += jnp.sum(dz, axis=0, keepdims=True)

    row = pl.BlockSpec((tm, 512), lambda i: (i, 0))
    full = lambda a: pl.BlockSpec(a.shape, lambda i: (0,) * a.ndim)
    return pl.pallas_call(
        body, name="glu_bwd", grid=(n // tm,), in_specs=[row, row, full(wg), full(bg)],
        out_specs=(row, row, row, pl.BlockSpec((1, 512), lambda i: (0, 0))),
        out_shape=(jax.ShapeDtypeStruct((n, 512), F32), jax.ShapeDtypeStruct((n, 512), BF16),
                   jax.ShapeDtypeStruct((n, 512), BF16), jax.ShapeDtypeStruct((1, 512), F32)),
        compiler_params=_cparams("arbitrary"),
    )(yc, dys, wg, bg)


def _mix_fwd(x, ys, ya, gs, ga, wout, gf, tm=512):
    n = x.shape[0]

    def body(x_ref, ys_ref, ya_ref, gs_ref, ga_ref, w_ref, gf_ref, h1_ref, hn2_ref, mixed_ref):
        mixed = jnp.concatenate([_rms(ys_ref[...], gs_ref[...]), _rms(ya_ref[...], ga_ref[...])], axis=1).astype(BF16)
        mixed_ref[...] = mixed
        h1 = x_ref[...] + _dot(mixed, w_ref[...])
        h1_ref[...] = h1
        hn2_ref[...] = _rms(h1, gf_ref[...]).astype(BF16)

    row = lambda w: pl.BlockSpec((tm, w), lambda i: (i, 0))
    full = lambda a: pl.BlockSpec(a.shape, lambda i: (0,) * a.ndim)
    return pl.pallas_call(
        body, name="mix_fwd", grid=(n // tm,),
        in_specs=[row(D_MODEL), row(512), row(512), full(gs), full(ga), full(wout), full(gf)],
        out_specs=(row(D_MODEL), row(D_MODEL), row(D_MODEL)),
        out_shape=(jax.ShapeDtypeStruct((n, D_MODEL), F32), jax.ShapeDtypeStruct((n, D_MODEL), BF16),
                   jax.ShapeDtypeStruct((n, D_MODEL), BF16)),
        compiler_params=_cparams("parallel"),
    )(x, ys, ya, gs, ga, wout, gf)


def _mix_bwd(dy, dhn2_parts, h1, ys, ya, gs, ga, wout, gf, tm=512):
    n = dy.shape[0]
    n_parts = dhn2_parts.shape[0]

    def body(dy_ref, dp_ref, h1_ref, ys_ref, ya_ref, gs_ref, ga_ref, w_ref, gf_ref,
             dh1_ref, dys_ref, dya_ref, dgs_ref, dga_ref, dgf_ref):
        @pl.when(pl.program_id(0) == 0)
        def _():
            dgs_ref[...] = jnp.zeros_like(dgs_ref)
            dga_ref[...] = jnp.zeros_like(dga_ref)
            dgf_ref[...] = jnp.zeros_like(dgf_ref)
        dhn2 = dp_ref[0]
        for p in range(1, n_parts):
            dhn2 = dhn2 + dp_ref[p]
        _, vjp_f = jax.vjp(_rms, h1_ref[...], gf_ref[...])
        dh1n, dgf = vjp_f(dhn2)
        dh1 = dy_ref[...] + dh1n
        dh1_ref[...] = dh1
        dmixed = _dot_nt(dh1.astype(BF16), w_ref[...])
        _, vjp_s = jax.vjp(_rms, ys_ref[...], gs_ref[...])
        dys, dgs = vjp_s(dmixed[:, 0:512])
        _, vjp_a = jax.vjp(_rms, ya_ref[...], ga_ref[...])
        dya, dga = vjp_a(dmixed[:, 512:1024])
        dys_ref[...] = dys
        dya_ref[...] = dya
        dgs_ref[...] += dgs
        dga_ref[...] += dga
        dgf_ref[...] += dgf

    row = lambda w: pl.BlockSpec((tm, w), lambda i: (i, 0))
    full = lambda a: pl.BlockSpec(a.shape, lambda i: (0,) * a.ndim)
    vec = lambda w: pl.BlockSpec((1, w), lambda i: (0, 0))
    return pl.pallas_call(
        body, name="mix_bwd", grid=(n // tm,),
        in_specs=[row(D_MODEL), pl.BlockSpec((n_parts, tm, D_MODEL), lambda i: (0, i, 0)), row(D_MODEL), row(512),
                  row(512), full(gs), full(ga), full(wout), full(gf)],
        out_specs=(row(D_MODEL), row(512), row(512), vec(512), vec(512), vec(D_MODEL)),
        out_shape=(jax.ShapeDtypeStruct((n, D_MODEL), F32), jax.ShapeDtypeStruct((n, 512), F32),
                   jax.ShapeDtypeStruct((n, 512), F32), jax.ShapeDtypeStruct((1, 512), F32),
                   jax.ShapeDtypeStruct((1, 512), F32), jax.ShapeDtypeStruct((1, D_MODEL), F32)),
        compiler_params=_cparams("arbitrary"),
    )(dy, dhn2_parts, h1, ys, ya, gs, ga, wout, gf)


HALO = 16


def _conv3(ue, cw):
    return cw[2:3] * ue + cw[1:2] * pltpu.roll(ue, 1, 0) + cw[0:1] * pltpu.roll(ue, 2, 0) + cw[3:4]


def _ffn_fwd(hn2, h1, target, wup_g, wup_v, cw_g, cw_v, wdown, seq_len, tm=512, fb=256):
    n = hn2.shape[0]
    nj = D_FF_PAD // fb
    hb = tm // HALO

    def body(hn_ref, halo_ref, h1_ref, tgt_ref, wg_ref, wv_ref, cg_ref, cv_ref, wd_ref,
             ug_ref, uv_ref, dy_ref, loss_ref, acc):
        i, j = pl.program_id(0), pl.program_id(1)
        seq_start = (i * tm) % seq_len == 0
        halo = halo_ref[...]
        halo = jnp.where(seq_start, jnp.zeros_like(halo), halo)
        he = jnp.concatenate([halo, hn_ref[...]], axis=0)
        ueg = _dot(he, wg_ref[...])
        uev = _dot(he, wv_ref[...])
        ug_ref[...] = ueg[HALO:]
        uv_ref[...] = uev[HALO:]
        cg = _conv3(ueg, cg_ref[...])[HALO:]
        cv = _conv3(uev, cv_ref[...])[HALO:]
        act = (jax.nn.silu(cg) * cv).astype(BF16)
        part = _dot(act, wd_ref[...])

        @pl.when(j == 0)
        def _():
            acc[...] = part

        @pl.when(j > 0)
        def _():
            acc[...] += part

        @pl.when(j == nj - 1)
        def _():
            err = h1_ref[...] + acc[...] - tgt_ref[...]
            dy_ref[...] = err * (1.0 / D_MODEL)
            loss_ref[0] = jnp.sum(err * err, axis=0, keepdims=True)

    row = pl.BlockSpec((tm, D_MODEL), lambda i, j: (i, 0))
    return pl.pallas_call(
        body, name="ffn_fwd", grid=(n // tm, nj),
        in_specs=[row, pl.BlockSpec((HALO, D_MODEL), lambda i, j: (jnp.maximum(i * hb - 1, 0), 0)), row, row,
                  pl.BlockSpec((D_MODEL, fb), lambda i, j: (0, j)), pl.BlockSpec((D_MODEL, fb), lambda i, j: (0, j)),
                  pl.BlockSpec((8, fb), lambda i, j: (0, j)), pl.BlockSpec((8, fb), lambda i, j: (0, j)),
                  pl.BlockSpec((fb, D_MODEL), lambda i, j: (j, 0))],
        out_specs=(pl.BlockSpec((tm, fb), lambda i, j: (i, j)), pl.BlockSpec((tm, fb), lambda i, j: (i, j)), row,
                   pl.BlockSpec((1, 1, D_MODEL), lambda i, j: (i, 0, 0))),
        out_shape=(jax.ShapeDtypeStruct((n, D_FF_PAD), F32), jax.ShapeDtypeStruct((n, D_FF_PAD), F32),
                   jax.ShapeDtypeStruct((n, D_MODEL), F32), jax.ShapeDtypeStruct((n // tm, 1, D_MODEL), F32)),
        scratch_shapes=[pltpu.VMEM((tm, D_MODEL), F32)],
        compiler_params=_cparams("parallel", "arbitrary"),
    )(hn2, hn2, h1, target, wup_g, wup_v, cw_g, cw_v, wdown)


def _ffn_bwd(dy, ug, uv, wup_g, wup_v, cw_g, cw_v, wdown, seq_len, tm=512, fb=256):
    n = dy.shape[0]
    nj = D_FF_PAD // fb
    hb = tm // HALO
    last_hb = n // HALO - 1
    rows = tm + HALO

    def body(dy_ref, dyn_ref, ugp_ref, ugm_ref, ugn_ref, uvp_ref, uvm_ref, uvn_ref, wg_ref, wv_ref, cg_ref, cv_ref,
             wd_ref, dug_ref, duv_ref, act_ref, dhn_ref, dcg_ref, dcv_ref, acc):
        i, j = pl.program_id(0), pl.program_id(1)
        seq_start = (i * tm) % seq_len == 0
        seq_end = ((i + 1) * tm) % seq_len == 0
        dyn = dyn_ref[...]
        dyn = jnp.where(seq_end, jnp.zeros_like(dyn), dyn)
        d_out = jnp.concatenate([dy_ref[...], dyn], axis=0).astype(BF16)
        d_act = _dot_nt(d_out, wd_ref[...])

        def pre_act(up_ref, um_ref, un_ref, cw):
            up = up_ref[...]
            up = jnp.where(seq_start, jnp.zeros_like(up), up)
            ue = jnp.concatenate([up, um_ref[...], un_ref[...]], axis=0)
            return ue, _conv3(ue, cw)[HALO:]

        cwg, cwv = cg_ref[...], cv_ref[...]
        ueg, cge = pre_act(ugp_ref, ugm_ref, ugn_ref, cwg)
        uev, cve = pre_act(uvp_ref, uvm_ref, uvn_ref, cwv)
        act, vjp_act = jax.vjp(lambda g, v: jax.nn.silu(g) * v, cge, cve)
        dcge, dcve = vjp_act(d_act)
        act_ref[...] = act[:tm].astype(BF16)

        def conv_t(dc, cw):
            return (cw[2:3] * dc + cw[1:2] * pltpu.roll(dc, rows - 1, 0) + cw[0:1] * pltpu.roll(dc, rows - 2, 0))[:tm]

        dug = conv_t(dcge, cwg).astype(BF16)
        duv = conv_t(dcve, cwv).astype(BF16)
        dug_ref[...] = dug
        duv_ref[...] = duv
        part = _dot_nt(dug, wg_ref[...]) + _dot_nt(duv, wv_ref[...])

        @pl.when(j == 0)
        def _():
            acc[...] = part

        @pl.when(j > 0)
        def _():
            acc[...] += part

        @pl.when(j == nj - 1)
        def _():
            dhn_ref[...] = acc[...]

        def cw_grad(dc, ue):
            dcm = dc[:tm]
            taps = [jnp.sum(dcm * pltpu.roll(ue, 2 - k, 0)[HALO:HALO + tm], axis=0, keepdims=True) for k in (0, 1)]
            taps.append(jnp.sum(dcm * ue[HALO:HALO + tm], axis=0, keepdims=True))
            taps.append(jnp.sum(dcm, axis=0, keepdims=True))
            return jnp.concatenate(taps + [jnp.zeros((4, fb), F32)], axis=0)

        @pl.when(i == 0)
        def _():
            dcg_ref[j] = jnp.zeros((8, fb), F32)
            dcv_ref[j] = jnp.zeros((8, fb), F32)

        dcg_ref[j] += cw_grad(dcge, ueg)
        dcv_ref[j] += cw_grad(dcve, uev)

    row = pl.BlockSpec((tm, D_MODEL), lambda i, j: (i, 0))
    u_prev = pl.BlockSpec((HALO, fb), lambda i, j: (jnp.maximum(i * hb - 1, 0), j))
    u_main = pl.BlockSpec((tm, fb), lambda i, j: (i, j))
    u_next = pl.BlockSpec((HALO, fb), lambda i, j: (jnp.minimum((i + 1) * hb, last_hb), j))
    w_col = pl.BlockSpec((D_MODEL, fb), lambda i, j: (0, j))
    c_col = pl.BlockSpec((8, fb), lambda i, j: (0, j))
    dc_spec = pl.BlockSpec((nj, 8, fb), lambda i, j: (0, 0, 0))
    return pl.pallas_call(
        body, name="ffn_bwd", grid=(n // tm, nj),
        in_specs=[row, pl.BlockSpec((HALO, D_MODEL), lambda i, j: (jnp.minimum((i + 1) * hb, last_hb), 0)),
                  u_prev, u_main, u_next, u_prev, u_main, u_next, w_col, w_col, c_col, c_col,
                  pl.BlockSpec((fb, D_MODEL), lambda i, j: (j, 0))],
        out_specs=(u_main, u_main, u_main, row, dc_spec, dc_spec),
        out_shape=(jax.ShapeDtypeStruct((n, D_FF_PAD), BF16), jax.ShapeDtypeStruct((n, D_FF_PAD), BF16),
                   jax.ShapeDtypeStruct((n, D_FF_PAD), BF16), jax.ShapeDtypeStruct((n, D_MODEL), F32),
                   jax.ShapeDtypeStruct((nj, 8, fb), F32), jax.ShapeDtypeStruct((nj, 8, fb), F32)),
        scratch_shapes=[pltpu.VMEM((tm, D_MODEL), F32)],
        compiler_params=_cparams("arbitrary", "arbitrary"),
    )(dy, dy, ug, ug, ug, uv, uv, uv, wup_g, wup_v, cw_g, cw_v, wdown)


def _s5_param_fn(lr, li, ldt, br, bi):
    dt = jnp.exp(ldt)
    mag = jnp.exp(lr * dt)
    ab_re = mag * jnp.cos(li * dt)
    ab_im = mag * jnp.sin(li * dt)
    nr = ab_re - 1.0
    ni = ab_im
    den = lr * lr + li * li
    q_re = (nr * lr + ni * li) / den
    q_im = (ni * lr - nr * li) / den
    bb_re = q_re * br - q_im * bi
    bb_im = q_re * bi + q_im * br
    return ab_re, ab_im, bb_re, bb_im


def _s5_param_fwd(lr, li, ldt, br, bi):
    def body(lr_ref, li_ref, ldt_ref, br_ref, bi_ref, ar_ref, ai_ref, bbr_ref, bbi_ref):
        ar, ai, bbr, bbi = _s5_param_fn(lr_ref[...], li_ref[...], ldt_ref[...], br_ref[...], bi_ref[...])
        ar_ref[...] = ar
        ai_ref[...] = ai
        bbr_ref[...] = bbr
        bbi_ref[...] = bbi

    return pl.pallas_call(
        body, name="s5_param_fwd",
        out_shape=(jax.ShapeDtypeStruct(lr.shape, F32), jax.ShapeDtypeStruct(lr.shape, F32),
                   jax.ShapeDtypeStruct(br.shape, F32), jax.ShapeDtypeStruct(br.shape, F32)),
    )(lr, li, ldt, br, bi)


def _s5_param_bwd(lr, li, ldt, br, bi, dar, dai, dbbr, dbbi):
    def body(lr_ref, li_ref, ldt_ref, br_ref, bi_ref, dar_ref, dai_ref, dbbr_ref, dbbi_ref,
             dlr_ref, dli_ref, dldt_ref, dbr_ref, dbi_ref):
        _, vjp = jax.vjp(_s5_param_fn, lr_ref[...], li_ref[...], ldt_ref[...], br_ref[...], bi_ref[...])
        dlr, dli, dldt, dbr, dbi = vjp((dar_ref[...], dai_ref[...], dbbr_ref[...], dbbi_ref[...]))
        dlr_ref[...] = dlr
        dli_ref[...] = dli
        dldt_ref[...] = dldt
        dbr_ref[...] = dbr
        dbi_ref[...] = dbi

    return pl.pallas_call(
        body, name="s5_param_bwd",
        out_shape=(jax.ShapeDtypeStruct(lr.shape, F32), jax.ShapeDtypeStruct(lr.shape, F32),
                   jax.ShapeDtypeStruct(ldt.shape, F32), jax.ShapeDtypeStruct(br.shape, F32),
                   jax.ShapeDtypeStruct(br.shape, F32)),
    )(lr, li, ldt, br, bi, dar, dai, dbbr, dbbi)


S5_CHUNK = 256
S5_STATES = 512
S5_BLOCKS = 4


def _cpow_rows(ar, ai, count):
    rs, im = [ar], [ai]
    for _ in range(count - 1):
        pr, pi = rs[-1], im[-1]
        rs.append(pr * ar - pi * ai)
        im.append(pr * ai + pi * ar)
    return rs, im


def _scan_in_groups(vr, vi, pr, pi, rm, reverse):
    n = vr.shape[0]
    for k in (1, 2, 4):
        if reverse:
            sr, si, keep = pltpu.roll(vr, n - k, 0), pltpu.roll(vi, n - k, 0), rm < SUBLANES - k
        else:
            sr, si, keep = pltpu.roll(vr, k, 0), pltpu.roll(vi, k, 0), rm >= k
        sr = jnp.where(keep, sr, 0.0)
        si = jnp.where(keep, si, 0.0)
        kr, ki = pr[k - 1], pi[k - 1]
        vr, vi = vr + kr * sr - ki * si, vi + kr * si + ki * sr
    return vr, vi


def _carry_over_groups(xr_s, xi_s, wr, wi, c0r, c0i, reverse):
    groups = xr_s.shape[0] // SUBLANES
    pick = 0 if reverse else SUBLANES - 1

    def step(q, carry):
        cr, ci = carry
        r = groups - 1 - q if reverse else q
        o = pl.multiple_of(r * SUBLANES, SUBLANES)
        vr = xr_s[pl.ds(o, SUBLANES), :]
        vi = xi_s[pl.ds(o, SUBLANES), :]
        nr = vr + wr * cr - wi * ci
        ni = vi + wr * ci + wi * cr
        xr_s[pl.ds(o, SUBLANES), :] = nr
        xi_s[pl.ds(o, SUBLANES), :] = ni
        return (jnp.broadcast_to(nr[pick:pick + 1], nr.shape), jnp.broadcast_to(ni[pick:pick + 1], ni.shape))

    return lax.fori_loop(0, groups, step, (c0r, c0i))


def _s5_state_scan(u_b, bbr, bbi, pr, pi, rm, xr_s, xi_s, c0r, c0i):
    bur = _dot(u_b, bbr)
    bui = _dot(u_b, bbi)
    bur, bui = _scan_in_groups(bur, bui, pr, pi, rm, False)
    xr_s[...] = bur
    xi_s[...] = bui
    w8r = jnp.concatenate(pr, axis=0)
    w8i = jnp.concatenate(pi, axis=0)
    return _carry_over_groups(xr_s, xi_s, w8r, w8i, c0r, c0i, False)


def _s5_fwd(u, a_re, a_im, bbr, bbi, cr, ci, d_skip, n_seq):
    n = u.shape[0]
    seq_len = n // n_seq
    nt = seq_len // S5_CHUNK
    tc = S5_CHUNK

    def body(u_ref, ar_ref, ai_ref, bbr_ref, bbi_ref, cr_ref, ci_ref, d_ref, y_ref, str_ref, sti_ref,
             xr_s, xi_s, car_r, car_i):
        t = pl.program_id(2)

        @pl.when(t == 0)
        def _():
            car_r[...] = jnp.zeros_like(car_r)
            car_i[...] = jnp.zeros_like(car_i)
        pr, pi = _cpow_rows(ar_ref[0], ai_ref[0], SUBLANES)
        rm = lax.broadcasted_iota(jnp.int32, (tc, S5_STATES), 0) & (SUBLANES - 1)
        str_ref[0, 0] = car_r[...]
        sti_ref[0, 0] = car_i[...]
        u_t = u_ref[...]
        cfr, cfi = _s5_state_scan(u_t.astype(BF16), bbr_ref[0], bbi_ref[0], pr, pi, rm, xr_s, xi_s,
                                  car_r[...], car_i[...])
        car_r[...] = cfr
        car_i[...] = cfi
        y = _dot(xr_s[...].astype(BF16), cr_ref[0]) - _dot(xi_s[...].astype(BF16), ci_ref[0])
        y_ref[...] = y + d_ref[...] * u_t

    u_spec = pl.BlockSpec((tc, LANES), lambda cb, b, t: (b * nt + t, cb))
    a_spec = pl.BlockSpec((1, 1, S5_STATES), lambda cb, b, t: (cb, 0, 0))
    bb_spec = pl.BlockSpec((1, LANES, S5_STATES), lambda cb, b, t: (cb, 0, 0))
    c_spec = pl.BlockSpec((1, S5_STATES, LANES), lambda cb, b, t: (cb, 0, 0))
    st_spec = pl.BlockSpec((1, 1, SUBLANES, S5_STATES), lambda cb, b, t: (cb, b * nt + t, 0, 0))
    st_shape = jax.ShapeDtypeStruct((S5_BLOCKS, n_seq * nt, SUBLANES, S5_STATES), F32)
    return pl.pallas_call(
        body, name="s5_fwd", grid=(S5_BLOCKS, n_seq, nt),
        in_specs=[u_spec, a_spec, a_spec, bb_spec, bb_spec, c_spec, c_spec,
                  pl.BlockSpec((1, LANES), lambda cb, b, t: (0, cb))],
        out_specs=(u_spec, st_spec, st_spec),
        out_shape=(jax.ShapeDtypeStruct((n, D_SSM), F32), st_shape, st_shape),
        scratch_shapes=[pltpu.VMEM((tc, S5_STATES), F32), pltpu.VMEM((tc, S5_STATES), F32),
                        pltpu.VMEM((SUBLANES, S5_STATES), F32), pltpu.VMEM((SUBLANES, S5_STATES), F32)],
        compiler_params=_cparams("parallel", "arbitrary", "arbitrary"),
    )(u, a_re, a_im, bbr, bbi, cr, ci, d_skip)


def _s5_bwd(u, dy, st_r, st_i, a_re, a_im, bbr, bbi, cr, ci, d_skip, n_seq):
    n = u.shape[0]
    seq_len = n // n_seq
    nt = seq_len // S5_CHUNK
    tc = S5_CHUNK

    def body(u_ref, dy_ref, str_ref, sti_ref, ar_ref, ai_ref, bbr_ref, bbi_ref, cr_ref, ci_ref, d_ref,
             du_ref, dbbr_ref, dbbi_ref, dcr_ref, dci_ref, dar_ref, dai_ref, dd_ref,
             xr_s, xi_s, gr_s, gi_s, car_r, car_i):
        b, t = pl.program_id(1), pl.program_id(2)

        @pl.when((b == 0) & (t == 0))
        def _():
            for ref in (dbbr_ref, dbbi_ref, dcr_ref, dci_ref, dar_ref, dai_ref, dd_ref):
                ref[...] = jnp.zeros_like(ref)

        @pl.when(t == 0)
        def _():
            car_r[...] = jnp.zeros_like(car_r)
            car_i[...] = jnp.zeros_like(car_i)
        ar, ai = ar_ref[0], ai_ref[0]
        pr, pi = _cpow_rows(ar, ai, SUBLANES)
        row = lax.broadcasted_iota(jnp.int32, (tc, S5_STATES), 0)
        rm = row & (SUBLANES - 1)
        u_t = u_ref[...]
        u_b = u_t.astype(BF16)
        dy_t = dy_ref[...]
        dy_b = dy_t.astype(BF16)
        s0r, s0i = str_ref[0, 0], sti_ref[0, 0]
        _s5_state_scan(u_b, bbr_ref[0], bbi_ref[0], pr, pi, rm, xr_s, xi_s, s0r, s0i)
        xr, xi = xr_s[...], xi_s[...]
        gr = _dot_nt(dy_b, cr_ref[0])
        gi = -_dot_nt(dy_b, ci_ref[0])
        npi = [-v for v in pi]
        gr, gi = _scan_in_groups(gr, gi, pr, npi, rm, True)
        gr_s[...] = gr
        gi_s[...] = gi
        w8r = jnp.concatenate(pr[::-1], axis=0)
        w8i = jnp.concatenate(npi[::-1], axis=0)
        cfr, cfi = _carry_over_groups(gr_s, gi_s, w8r, w8i, car_r[...], car_i[...], True)
        car_r[...] = cfr
        car_i[...] = cfi
        gr, gi = gr_s[...], gi_s[...]
        gr_b, gi_b = gr.astype(BF16), gi.astype(BF16)
        du_ref[...] = _dot_nt(gr_b, bbr_ref[0]) + _dot_nt(gi_b, bbi_ref[0]) + d_ref[...] * dy_t
        dbbr_ref[0] += _dot_tn(u_b, gr_b)
        dbbi_ref[0] += _dot_tn(u_b, gi_b)
        dcr_ref[0] += _dot_tn(xr.astype(BF16), dy_b)
        dci_ref[0] -= _dot_tn(xi.astype(BF16), dy_b)
        dd_ref[0] += jnp.sum((dy_t * u_t).reshape(tc // SUBLANES, SUBLANES, LANES), axis=0)
        first = row == 0
        xpr = jnp.where(first, jnp.broadcast_to(s0r[0:1], xr.shape), pltpu.roll(xr, 1, 0))
        xpi = jnp.where(first, jnp.broadcast_to(s0i[0:1], xi.shape), pltpu.roll(xi, 1, 0))
        shp = (tc // SUBLANES, SUBLANES, S5_STATES)
        dar_ref[0] += jnp.sum((gr * xpr + gi * xpi).reshape(shp), axis=0)
        dai_ref[0] += jnp.sum((gi * xpr - gr * xpi).reshape(shp), axis=0)

    u_spec = pl.BlockSpec((tc, LANES), lambda cb, b, t: (b * nt + nt - 1 - t, cb))
    a_spec = pl.BlockSpec((1, 1, S5_STATES), lambda cb, b, t: (cb, 0, 0))
    bb_spec = pl.BlockSpec((1, LANES, S5_STATES), lambda cb, b, t: (cb, 0, 0))
    c_spec = pl.BlockSpec((1, S5_STATES, LANES), lambda cb, b, t: (cb, 0, 0))
    st_spec = pl.BlockSpec((1, 1, SUBLANES, S5_STATES), lambda cb, b, t: (cb, b * nt + nt - 1 - t, 0, 0))
    da_spec = pl.BlockSpec((1, SUBLANES, S5_STATES), lambda cb, b, t: (cb, 0, 0))
    dd_spec = pl.BlockSpec((1, SUBLANES, LANES), lambda cb, b, t: (cb, 0, 0))
    big = pltpu.VMEM((tc, S5_STATES), F32)
    small = pltpu.VMEM((SUBLANES, S5_STATES), F32)
    return pl.pallas_call(
        body, name="s5_bwd", grid=(S5_BLOCKS, n_seq, nt),
        in_specs=[u_spec, u_spec, st_spec, st_spec, a_spec, a_spec, bb_spec, bb_spec, c_spec, c_spec,
                  pl.BlockSpec((1, LANES), lambda cb, b, t: (0, cb))],
        out_specs=(u_spec, bb_spec, bb_spec, c_spec, c_spec, da_spec, da_spec, dd_spec),
        out_shape=(jax.ShapeDtypeStruct((n, D_SSM), F32),
                   jax.ShapeDtypeStruct((S5_BLOCKS, LANES, S5_STATES), F32),
                   jax.ShapeDtypeStruct((S5_BLOCKS, LANES, S5_STATES), F32),
                   jax.ShapeDtypeStruct((S5_BLOCKS, S5_STATES, LANES), F32),
                   jax.ShapeDtypeStruct((S5_BLOCKS, S5_STATES, LANES), F32),
                   jax.ShapeDtypeStruct((S5_BLOCKS, SUBLANES, S5_STATES), F32),
                   jax.ShapeDtypeStruct((S5_BLOCKS, SUBLANES, S5_STATES), F32),
                   jax.ShapeDtypeStruct((S5_BLOCKS, SUBLANES, LANES), F32)),
        scratch_shapes=[big, big, big, big, small, small],
        compiler_params=_cparams("parallel", "arbitrary", "arbitrary"),
    )(u, dy, st_r, st_i, a_re, a_im, bbr, bbi, cr, ci, d_skip)


CUM_BLOCK = 128


def _tri(lower):
    r = lax.broadcasted_iota(jnp.int32, (CUM_BLOCK, CUM_BLOCK), 0)
    c = lax.broadcasted_iota(jnp.int32, (CUM_BLOCK, CUM_BLOCK), 1)
    return jnp.where(r >= c if lower else r <= c, 1.0, 0.0).astype(F32)


def _fprep_fwd(fl, bf, n_seq):
    n = fl.shape[0]
    seq_len = n // n_seq
    nb = seq_len // CUM_BLOCK

    def body(fl_ref, bf_ref, cum_ref):
        tril = _tri(True)
        carry = jnp.zeros((1, LANES), F32)
        for blk in range(nb):
            rows = slice(blk * CUM_BLOCK, (blk + 1) * CUM_BLOCK)
            lf = jax.nn.log_sigmoid(fl_ref[rows, :] + bf_ref[...])
            cs = jnp.dot(tril, lf, preferred_element_type=F32, precision=HIGHEST) + carry
            cum_ref[rows, :] = cs
            carry = cs[CUM_BLOCK - 1:CUM_BLOCK, :]

    spec = pl.BlockSpec((seq_len, LANES), lambda b: (b, 0))
    return pl.pallas_call(
        body, name="fprep_fwd", grid=(n_seq,), in_specs=[spec, pl.BlockSpec((1, LANES), lambda b: (0, 0))],
        out_specs=spec, out_shape=jax.ShapeDtypeStruct((n, LANES), F32), compiler_params=_cparams("parallel"),
    )(fl, bf)


def _fprep_bwd(dcum, fl, bf, n_seq):
    n = fl.shape[0]
    seq_len = n // n_seq
    nb = seq_len // CUM_BLOCK

    def body(dcum_ref, fl_ref, bf_ref, dfl_ref, dbf_ref):
        triu = _tri(False)
        lane = lax.broadcasted_iota(jnp.int32, (CUM_BLOCK, LANES), 1)
        carry = jnp.zeros((1, LANES), F32)
        total = jnp.zeros((1, LANES), F32)
        for blk in reversed(range(nb)):
            rows = slice(blk * CUM_BLOCK, (blk + 1) * CUM_BLOCK)
            rs = jnp.dot(triu, dcum_ref[rows, :], preferred_element_type=F32, precision=HIGHEST) + carry
            carry = rs[0:1, :]
            _, vjp = jax.vjp(jax.nn.log_sigmoid, fl_ref[rows, :] + bf_ref[...])
            dz = jnp.where(lane < N_HEADS, vjp(rs)[0], 0.0)
            dfl_ref[rows, :] = dz
            total = total + jnp.sum(dz, axis=0, keepdims=True)
        dbf_ref[0] = total

    spec = pl.BlockSpec((seq_len, LANES), lambda b: (b, 0))
    return pl.pallas_call(
        body, name="fprep_bwd", grid=(n_seq,), in_specs=[spec, spec, pl.BlockSpec((1, LANES), lambda b: (0, 0))],
        out_specs=(spec, pl.BlockSpec((1, 1, LANES), lambda b: (b, 0, 0))),
        out_shape=(jax.ShapeDtypeStruct((n, LANES), F32), jax.ShapeDtypeStruct((n_seq, 1, LANES), F32)),
        compiler_params=_cparams("parallel"),
    )(dcum, fl, bf)


ATT_TQ = 256
ATT_SCALE = HEAD_DIM ** -0.5
NEG_BIG = -1e30


def _attn_scores(qe, kb, cq, ck, causal):
    s = _dot_nt(qe, kb) * ATT_SCALE + cq - ck
    return jnp.where(causal, s, NEG_BIG)


def _causal_mask(qi, seq_len):
    r = lax.broadcasted_iota(jnp.int32, (ATT_TQ, seq_len), 0) + qi * ATT_TQ
    c = lax.broadcasted_iota(jnp.int32, (ATT_TQ, seq_len), 1)
    return r >= c


def _attn_specs(n_seq, seq_len):
    nq = seq_len // ATT_TQ
    q_spec = pl.BlockSpec((ATT_TQ, LANES), lambda b, h, q: (b * nq + q, h))
    k_spec = pl.BlockSpec((seq_len, LANES), lambda b, h, q: (b, N_HEADS // 2 + h))
    v_spec = pl.BlockSpec((seq_len, LANES), lambda b, h, q: (b, N_HEADS + h))
    cq_spec = pl.BlockSpec((1, 2, ATT_TQ, 1), lambda b, h, q: (b, h, q, 0))
    ck_spec = pl.BlockSpec((1, 2, 1, seq_len), lambda b, h, q: (b, h, 0, 0))
    return nq, q_spec, k_spec, v_spec, cq_spec, ck_spec


def _attn_fwd(qkv, cq, ck, n_seq):
    n = qkv.shape[0]
    seq_len = n // n_seq
    nq, q_spec, k_spec, v_spec, cq_spec, ck_spec = _attn_specs(n_seq, seq_len)

    def body(q_ref, k_ref, v_ref, cq_ref, ck_ref, o_ref, lse_ref):
        q2 = q_ref[...]
        kb = k_ref[...].astype(BF16)
        vb = v_ref[...].astype(BF16)
        head0 = lax.broadcasted_iota(jnp.int32, (1, LANES), 1) < HEAD_DIM
        causal = _causal_mask(pl.program_id(2), seq_len)
        outs = []
        for e in range(2):
            sel = head0 if e == 0 else jnp.logical_not(head0)
            qe = jnp.where(sel, q2, 0.0).astype(BF16)
            s = _attn_scores(qe, kb, cq_ref[0, e], ck_ref[0, e], causal)
            mx = jnp.max(s, axis=1, keepdims=True)
            p = jnp.exp(s - mx)
            den = jnp.sum(p, axis=1, keepdims=True)
            outs.append(_dot(p.astype(BF16), vb) / den)
            lse_ref[0, e] = mx + jnp.log(den)
        o_ref[...] = jnp.where(head0, outs[0], outs[1])

    return pl.pallas_call(
        body, name="attn_fwd", grid=(n_seq, N_HEADS // 2, nq),
        in_specs=[q_spec, k_spec, v_spec, cq_spec, ck_spec],
        out_specs=(q_spec, cq_spec),
        out_shape=(jax.ShapeDtypeStruct((n, D_ATTN), F32), jax.ShapeDtypeStruct((n_seq, N_HEADS, seq_len, 1), F32)),
        compiler_params=_cparams("parallel", "parallel", "parallel"),
    )(qkv, qkv, qkv, cq, ck)


def _attn_bwd(qkv, cq, ck, o, do, lse, n_seq):
    n = qkv.shape[0]
    seq_len = n // n_seq
    nq, q_spec, k_spec, v_spec, cq_spec, ck_spec = _attn_specs(n_seq, seq_len)
    kv_out = pl.BlockSpec((seq_len, LANES), lambda b, h, q: (b, h))

    def body(q_ref, k_ref, v_ref, cq_ref, ck_ref, o_ref, do_ref, lse_ref, dq_ref, dk_ref, dv_ref, dcq_ref, dck_ref):
        @pl.when(pl.program_id(2) == 0)
        def _():
            dk_ref[...] = jnp.zeros_like(dk_ref)
            dv_ref[...] = jnp.zeros_like(dv_ref)
            dck_ref[...] = jnp.zeros_like(dck_ref)
        q2 = q_ref[...]
        kb = k_ref[...].astype(BF16)
        vb = v_ref[...].astype(BF16)
        do2 = do_ref[...]
        o2 = o_ref[...]
        head0 = lax.broadcasted_iota(jnp.int32, (1, LANES), 1) < HEAD_DIM
        causal = _causal_mask(pl.program_id(2), seq_len)
        dqs = []
        for e in range(2):
            sel = head0 if e == 0 else jnp.logical_not(head0)
            qe = jnp.where(sel, q2, 0.0).astype(BF16)
            doe = jnp.where(sel, do2, 0.0)
            doe_b = doe.astype(BF16)
            s = _attn_scores(qe, kb, cq_ref[0, e], ck_ref[0, e], causal)
            p = jnp.exp(s - lse_ref[0, e])
            delta = jnp.sum(doe * o2, axis=1, keepdims=True)
            ds = p * (_dot_nt(doe_b, vb) - delta)
            ds_b = ds.astype(BF16)
            dqs.append(_dot(ds_b, kb) * ATT_SCALE)
            dk_ref[...] += _dot_tn(ds_b, qe) * ATT_SCALE
            dv_ref[...] += _dot_tn(p.astype(BF16), doe_b)
            dcq_ref[0, e] = jnp.sum(ds, axis=1, keepdims=True)
            dck_ref[0, e] -= jnp.sum(ds, axis=0, keepdims=True)
        dq_ref[...] = jnp.where(head0, dqs[0], dqs[1])

    return pl.pallas_call(
        body, name="attn_bwd", grid=(n_seq, N_HEADS // 2, nq),
        in_specs=[q_spec, k_spec, v_spec, cq_spec, ck_spec, q_spec, q_spec, cq_spec],
        out_specs=(q_spec, kv_out, kv_out, cq_spec, ck_spec),
        out_shape=(jax.ShapeDtypeStruct((n, D_ATTN), F32), jax.ShapeDtypeStruct((n, D_ATTN), F32),
                   jax.ShapeDtypeStruct((n, D_ATTN), F32),
                   jax.ShapeDtypeStruct((n_seq, N_HEADS, seq_len, 1), F32),
                   jax.ShapeDtypeStruct((n_seq, N_HEADS, 1, seq_len), F32)),
        compiler_params=_cparams("parallel", "parallel", "arbitrary"),
    )(qkv, qkv, qkv, cq, ck, o, do, lse)


WEIGHT_NAMES = ("norm_mix", "w_in", "b_forget", "lam_re", "lam_im", "b_re", "b_im", "c_re", "c_im", "d_skip", "log_dt",
                "w_glu", "b_glu", "q_norm", "k_norm", "norm_out_ssm", "norm_out_attn", "w_out", "norm_ffn", "w_up",
                "conv_w", "conv_b", "w_down")
SHARDED = ("w_in", "w_glu", "w_out", "w_up", "conv_w", "w_down")
ADAM_ROWS = {"w_in": 256, "w_glu": 64, "w_out": 128, "w_up": 128, "conv_w": 3, "w_down": 344}
PACK_ROWS = SUBLANES * LANES


def _pad_to(a, axis, size):
    pad = [(0, 0)] * a.ndim
    pad[axis] = (0, size - a.shape[axis])
    return jnp.pad(a, pad)


def _block_diag(t, transpose):
    t4 = t.reshape(S5_BLOCKS, 8, SSM_GROUP, SSM_STATE)
    eye = jnp.eye(8, dtype=t.dtype)
    if transpose:
        e = jnp.swapaxes(t4, 2, 3)[:, :, :, None, :] * eye[None, :, None, :, None]
        return e.reshape(S5_BLOCKS, S5_STATES, LANES)
    e = t4[:, :, :, None, :] * eye[None, :, None, :, None]
    return e.reshape(S5_BLOCKS, LANES, S5_STATES)


def _block_diag_extract(m, transpose):
    if transpose:
        m5 = m.reshape(S5_BLOCKS, 8, SSM_STATE, 8, SSM_GROUP)
        d = jnp.stack([m5[:, i, :, i, :] for i in range(8)], axis=1)
        return jnp.swapaxes(d, 2, 3).reshape(N_GROUPS, SSM_GROUP, SSM_STATE)
    m5 = m.reshape(S5_BLOCKS, 8, SSM_GROUP, 8, SSM_STATE)
    d = jnp.stack([m5[:, i, :, i, :] for i in range(8)], axis=1)
    return d.reshape(N_GROUPS, SSM_GROUP, SSM_STATE)


def _pack(pieces):
    flat = jnp.concatenate([p.reshape(-1).astype(F32) for p in pieces])
    size = -(-flat.shape[0] // PACK_ROWS) * PACK_ROWS
    return _pad_to(flat, 0, size).reshape(-1, LANES)


def _unpack(packed, shapes):
    flat = packed.reshape(-1)
    out, off = [], 0
    for shp in shapes:
        size = math.prod(shp)
        out.append(flat[off:off + size].reshape(shp))
        off += size
    return out


def kernel(x, norm_mix, w_in, b_forget, lam_re, lam_im, b_re, b_im, c_re, c_im, d_skip, log_dt, w_glu, b_glu, q_norm, k_norm, norm_out_ssm, norm_out_attn, w_out, norm_ffn, w_up, conv_w, conv_b, w_down, loss_target, m_norm_mix, m_w_in, m_b_forget, m_lam_re, m_lam_im, m_b_re, m_b_im, m_c_re, m_c_im, m_d_skip, m_log_dt, m_w_glu, m_b_glu, m_q_norm, m_k_norm, m_norm_out_ssm, m_norm_out_attn, m_w_out, m_norm_ffn, m_w_up, m_conv_w, m_conv_b, m_w_down, v_norm_mix, v_w_in, v_b_forget, v_lam_re, v_lam_im, v_b_re, v_b_im, v_c_re, v_c_im, v_d_skip, v_log_dt, v_w_glu, v_b_glu, v_q_norm, v_k_norm, v_norm_out_ssm, v_norm_out_attn, v_w_out, v_norm_ffn, v_w_up, v_conv_w, v_conv_b, v_w_down):
    given = dict(locals())
    weights = {k: given[k] for k in WEIGHT_NAMES}
    mom1 = {k: given["m_" + k] for k in WEIGHT_NAMES}
    mom2 = {k: given["v_" + k] for k in WEIGHT_NAMES}
    n_seq, seq_len, _ = x.shape
    n = n_seq * seq_len
    xf = x.reshape(n, D_MODEL)
    target = loss_target.reshape(n, D_MODEL)

    g_in, g_glu, g_out, g_up, g_cw, g_down = _exchange(
        [w_in[0].astype(BF16), w_glu[0].astype(BF16), w_out[0].astype(BF16), w_up[0].astype(BF16), conv_w[0],
         w_down[0].astype(BF16)], [False] * 6, "weight_gather")
    w_in_p = _pad_to(jnp.swapaxes(g_in, 0, 1).reshape(D_MODEL, D_IN), 1, D_IN_PAD)
    w_glu_f = g_glu.reshape(D_SSM, D_SSM)
    w_out_f = g_out.reshape(D_MODEL, D_MODEL)
    w_up_f = jnp.swapaxes(g_up, 0, 1).reshape(D_MODEL, 2 * D_FF)
    wup_g = _pad_to(w_up_f[:, :D_FF], 1, D_FF_PAD)
    wup_v = _pad_to(w_up_f[:, D_FF:], 1, D_FF_PAD)
    cw_f = jnp.swapaxes(g_cw, 0, 1).reshape(3, 2 * D_FF)
    cw4 = jnp.concatenate([cw_f, conv_b], axis=0)
    cw_g = _pad_to(_pad_to(cw4[:, :D_FF], 1, D_FF_PAD), 0, SUBLANES)
    cw_v = _pad_to(_pad_to(cw4[:, D_FF:], 1, D_FF_PAD), 0, SUBLANES)
    w_down_p = _pad_to(g_down.reshape(D_FF, D_MODEL), 0, D_FF_PAD)

    lr3 = lam_re[0].reshape(N_GROUPS, 1, SSM_STATE)
    li3 = lam_im[0].reshape(N_GROUPS, 1, SSM_STATE)
    ldt3 = log_dt[0].reshape(N_GROUPS, 1, 1)
    br_t = jnp.swapaxes(b_re[0], 1, 2)
    bi_t = jnp.swapaxes(b_im[0], 1, 2)
    ab_re, ab_im, bb_re, bb_im = _s5_param_fwd(lr3, li3, ldt3, br_t, bi_t)
    a_re = ab_re.reshape(S5_BLOCKS, 1, S5_STATES)
    a_im = ab_im.reshape(S5_BLOCKS, 1, S5_STATES)
    bbr = _block_diag(bb_re, False).astype(BF16)
    bbi = _block_diag(bb_im, False).astype(BF16)
    cr = _block_diag(c_re[0], True).astype(BF16)
    ci = _block_diag(c_im[0], True).astype(BF16)

    avg = jnp.kron(jnp.eye(N_HEADS, dtype=F32), jnp.full((HEAD_DIM, HEAD_DIM), 1.0 / HEAD_DIM, F32))
    qg = jnp.tile(q_norm, (1, N_HEADS))
    kg = jnp.tile(k_norm, (1, N_HEADS))
    hn, u, qkv, raw, fl = _inproj_fwd(xf, norm_mix, w_in_p, avg, qg, kg)
    yc, st_r, st_i = _s5_fwd(u, a_re, a_im, bbr, bbi, cr, ci, d_skip, n_seq)
    ys = _glu_fwd(yc, w_glu_f, b_glu)
    bf = _pad_to(b_forget, 1, LANES)
    cum = _fprep_fwd(fl, bf, n_seq)
    cum8 = jnp.swapaxes(cum[:, :N_HEADS].reshape(n_seq, seq_len, N_HEADS), 1, 2)
    cq = cum8[:, :, :, None]
    ck = cum8[:, :, None, :]
    ya, lse = _attn_fwd(qkv, cq, ck, n_seq)
    h1, hn2, mixed = _mix_fwd(xf, ys, ya, norm_out_ssm, norm_out_attn, w_out_f, norm_ffn)
    ug, uv, dy, loss_part = _ffn_fwd(hn2, h1, target, wup_g, wup_v, cw_g, cw_v, w_down_p, seq_len)
    loss = lax.psum(0.5 * jnp.sum(loss_part) / D_MODEL, ("x", "y", "c"))

    dug, duv, act, dhn2, dcg, dcv = _ffn_bwd(dy, ug, uv, wup_g, wup_v, cw_g, cw_v, w_down_p, seq_len)
    dh1, dys, dya, d_gs, d_ga, d_gf = _mix_bwd(dy, dhn2[None], h1, ys, ya, norm_out_ssm, norm_out_attn, w_out_f, norm_ffn)
    dyc, gl_b, dz_b, d_bglu = _glu_bwd(yc, dys, w_glu_f, b_glu)
    du, dbbr, dbbi, dcr, dci, dar, dai, ddk = _s5_bwd(u, dyc, st_r, st_i, a_re, a_im, bbr, bbi, cr, ci, d_skip, n_seq)
    dqn, dkn, dv, dcq, dck = _attn_bwd(qkv, cq, ck, ya, dya, lse, n_seq)
    dcum8 = dcq[:, :, :, 0] + dck[:, :, 0, :]
    dcum = _pad_to(jnp.swapaxes(dcum8, 1, 2).reshape(n, N_HEADS), 1, LANES)
    dfl, dbf = _fprep_bwd(dcum, fl, bf, n_seq)
    dx, dproj, d_gmix, d_qg, d_kg = _inproj_bwd(xf, norm_mix, w_in_p, avg, qg, kg, raw, du, dqn, dkn, dv, dfl, dh1)

    gw_in = _tn_matmul(hn, dproj, "dw_in", D_MODEL, D_IN_PAD)[:, :D_IN]
    gw_glu = _tn_matmul(gl_b, dz_b, "dw_glu", D_SSM, D_SSM)
    gw_out = _tn_matmul(mixed, dh1, "dw_out", D_MODEL, D_MODEL)
    gw_up = jnp.concatenate([_tn_matmul(hn2, dug, "dw_up_gate", D_MODEL, D_FF_PAD // 2)[:, :D_FF],
                             _tn_matmul(hn2, duv, "dw_up_val", D_MODEL, D_FF_PAD // 2)[:, :D_FF]], axis=1)
    gw_down = _tn_matmul(act, dy, "dw_down", D_FF_PAD // 2, D_MODEL)[:D_FF]
    dcg2 = jnp.swapaxes(dcg, 0, 1).reshape(SUBLANES, D_FF_PAD)[:, :D_FF]
    dcv2 = jnp.swapaxes(dcv, 0, 1).reshape(SUBLANES, D_FF_PAD)[:, :D_FF]
    g_conv = jnp.concatenate([dcg2, dcv2], axis=1)
    by_cols = lambda g, c: jnp.swapaxes(g.reshape(g.shape[0], N_DEV, c), 0, 1)
    partial_small = {
        "norm_mix": d_gmix, "b_forget": jnp.sum(dbf, axis=(0, 1))[:N_HEADS],
        "ab_re": jnp.sum(dar, axis=1), "ab_im": jnp.sum(dai, axis=1),
        "bb_re": _block_diag_extract(dbbr, False), "bb_im": _block_diag_extract(dbbi, False),
        "c_re": _block_diag_extract(dcr, True), "c_im": _block_diag_extract(dci, True),
        "d_skip": jnp.sum(ddk, axis=1), "b_glu": d_bglu,
        "q_norm": jnp.sum(d_qg.reshape(N_HEADS, HEAD_DIM), axis=0),
        "k_norm": jnp.sum(d_kg.reshape(N_HEADS, HEAD_DIM), axis=0),
        "norm_out_ssm": d_gs, "norm_out_attn": d_ga, "norm_ffn": d_gf, "conv_b": g_conv[3],
    }
    small_keys = tuple(partial_small)
    small_shapes = [partial_small[k].shape for k in small_keys]

    landed = _exchange(
        [by_cols(gw_in, D_IN // N_DEV), gw_glu.reshape(N_DEV, -1, D_SSM), gw_out.reshape(N_DEV, -1, D_MODEL),
         by_cols(gw_up, 2 * D_FF // N_DEV), by_cols(g_conv[:3], 2 * D_FF // N_DEV),
         gw_down.reshape(N_DEV, -1, D_MODEL), _pack([partial_small[k] for k in small_keys])],
        [True] * 6 + [False], "grad_exchange")
    grads, deltas, new_m, new_v = {}, {}, {}, {}
    for name, land in zip(SHARDED, landed[:6]):
        g, d, nm, nv = _adam_sharded(land, weights[name][0], mom1[name][0], mom2[name][0], "adam_" + name, ADAM_ROWS[name])
        grads[name], deltas[name], new_m[name], new_v[name] = g[None], d[None], nm[None], nv[None]

    summed = dict(zip(small_keys, _unpack(_sum_partials(landed[6], "sum_small_grads"), small_shapes)))
    dlr, dli, dldt, dbr_t, dbi_t = _s5_param_bwd(
        lr3, li3, ldt3, br_t, bi_t, summed["ab_re"].reshape(lr3.shape), summed["ab_im"].reshape(lr3.shape),
        summed["bb_re"], summed["bb_im"])
    small_grads = {
        "norm_mix": summed["norm_mix"], "b_forget": summed["b_forget"], "lam_re": dlr, "lam_im": dli,
        "b_re": jnp.swapaxes(dbr_t, 1, 2), "b_im": jnp.swapaxes(dbi_t, 1, 2), "c_re": summed["c_re"], "c_im": summed["c_im"],
        "d_skip": summed["d_skip"], "log_dt": dldt, "b_glu": summed["b_glu"], "q_norm": summed["q_norm"],
        "k_norm": summed["k_norm"], "norm_out_ssm": summed["norm_out_ssm"], "norm_out_attn": summed["norm_out_attn"],
        "norm_ffn": summed["norm_ffn"], "conv_b": summed["conv_b"],
    }
    repl = tuple(k for k in WEIGHT_NAMES if k not in SHARDED)
    shapes = [weights[k].shape for k in repl]
    gp = _pack([small_grads[k] for k in repl])
    dp, mp, vp = _adam_packed(gp, _pack([weights[k] for k in repl]), _pack([mom1[k] for k in repl]),
                              _pack([mom2[k] for k in repl]), "adam_replicated")
    for k, g, d, nm, nv in zip(repl, _unpack(gp, shapes), _unpack(dp, shapes), _unpack(mp, shapes), _unpack(vp, shapes)):
        grads[k], deltas[k], new_m[k], new_v[k] = g, d, nm, nv

    grad_x = dx.reshape(x.shape)
    return (loss, grad_x, *[grads[k] for k in WEIGHT_NAMES], *[deltas[k] for k in WEIGHT_NAMES],
            *[new_m[k] for k in WEIGHT_NAMES], *[new_v[k] for k in WEIGHT_NAMES])
```

```python
import functools
import math

import jax
import jax.numpy as jnp
from jax import lax
from jax.experimental import pallas as pl
from jax.experimental.pallas import tpu as pltpu

F32 = jnp.float32
BF16 = jnp.bfloat16
HIGHEST = lax.Precision.HIGHEST

N_DEV = 8
D_MODEL = 1024
D_SSM = 512
D_ATTN = 512
N_HEADS = 8
HEAD_DIM = 64
N_GROUPS = 32
SSM_GROUP = 16
SSM_STATE = 64
D_FF = 2752
D_FF_PAD = 2816
D_IN = 2056
D_IN_PAD = 2176
EPS = 1e-6
LANES = 128
SUBLANES = 8
VMEM_LIMIT = 56 * 1024 * 1024

ADAM_LR = 0.001
ADAM_B1 = 0.9
ADAM_B2 = 0.999
ADAM_EPS = 1e-08
ADAM_WD = 0.01
ADAM_STEP = 10


def _cparams(*sem):
    return pltpu.CompilerParams(dimension_semantics=sem, vmem_limit_bytes=VMEM_LIMIT)


def _dot(a, b, **kw):
    return jnp.dot(a, b, preferred_element_type=F32, **kw)


def _dot_nt(a, b):
    return lax.dot_general(a, b, (((1,), (1,)), ((), ())), preferred_element_type=F32)


def _dot_tn(a, b):
    return lax.dot_general(a, b, (((0,), (0,)), ((), ())), preferred_element_type=F32)


def _rms(x, g):
    return x * lax.rsqrt(jnp.mean(x * x, axis=-1, keepdims=True) + EPS) * g


def _headnorm(q, avg, g):
    ms = jnp.dot(q * q, avg, preferred_element_type=F32, precision=HIGHEST)
    return q * lax.rsqrt(ms + EPS) * g


def _exchange(srcs, scatter_flags, name):
    n = len(srcs)
    out_shape = []
    for s, sc in zip(srcs, scatter_flags):
        shp = s.shape if sc else (N_DEV,) + s.shape
        out_shape.append(jax.ShapeDtypeStruct(shp, s.dtype))

    def body(*refs):
        src = refs[:n]
        dst = refs[n:2 * n]
        send_sems, recv_sems, loc_sems = refs[2 * n:]
        x, y, c = lax.axis_index("x"), lax.axis_index("y"), lax.axis_index("c")
        me = 4 * x + 2 * y + c
        peers = []
        for j in range(1, N_DEV):
            px = 1 - x if (j >> 2) & 1 else x
            py = 1 - y if (j >> 1) & 1 else y
            pc = 1 - c if j & 1 else c
            peers.append(((px, py, pc), 4 * px + 2 * py + pc))
        local, sends = [], []
        for k in range(n):
            own = src[k].at[me] if scatter_flags[k] else src[k]
            lc = pltpu.make_async_copy(own, dst[k].at[me], loc_sems.at[k])
            lc.start()
            local.append(lc)
            for j, (pid, pidx) in enumerate(peers):
                s = src[k].at[pidx] if scatter_flags[k] else src[k]
                cp = pltpu.make_async_remote_copy(
                    src_ref=s, dst_ref=dst[k].at[me], send_sem=send_sems.at[k, j], recv_sem=recv_sems.at[k, j],
                    device_id=pid, device_id_type=pl.DeviceIdType.MESH)
                cp.start()
                sends.append(cp)
        for k in range(n):
            for j, (pid, pidx) in enumerate(peers):
                s = src[k].at[pidx] if scatter_flags[k] else src[k]
                pltpu.make_async_remote_copy(
                    src_ref=s, dst_ref=dst[k].at[pidx], send_sem=send_sems.at[k, j], recv_sem=recv_sems.at[k, j],
                    device_id=pid, device_id_type=pl.DeviceIdType.MESH).wait_recv()
        for cp in sends:
            cp.wait_send()
        for lc in local:
            lc.wait()

    any_spec = pl.BlockSpec(memory_space=pl.ANY)
    return pl.pallas_call(
        body, name=name, out_shape=tuple(out_shape),
        in_specs=[any_spec] * n, out_specs=tuple([any_spec] * n),
        scratch_shapes=[pltpu.SemaphoreType.DMA((n, N_DEV - 1)), pltpu.SemaphoreType.DMA((n, N_DEV - 1)),
                        pltpu.SemaphoreType.DMA((n,))],
        compiler_params=pltpu.CompilerParams(has_side_effects=True),
    )(*srcs)


def _peer_list():
    x, y, c = lax.axis_index("x"), lax.axis_index("y"), lax.axis_index("c")
    peers = []
    for j in range(1, N_DEV):
        px = 1 - x if (j >> 2) & 1 else x
        py = 1 - y if (j >> 1) & 1 else y
        pc = 1 - c if j & 1 else c
        peers.append(((px, py, pc), 4 * px + 2 * py + pc))
    return 4 * x + 2 * y + c, peers


def _split_copies(src, land, send_sems, recv_sems, scatter_flags, me, peers, incoming):
    copies = []
    for k in range(len(src)):
        for j, (pid, pidx) in enumerate(peers):
            s = src[k].at[pidx] if scatter_flags[k] else src[k]
            i = k * (N_DEV - 1) + j
            copies.append(pltpu.make_async_remote_copy(
                src_ref=s, dst_ref=land[k].at[pidx if incoming else me], send_sem=send_sems[i],
                recv_sem=recv_sems[i], device_id=pid, device_id_type=pl.DeviceIdType.MESH))
    return copies


def _exchange_start(srcs, scatter_flags, name, collective_id):
    n = len(srcs)
    ns = n * (N_DEV - 1)
    hbm = pl.BlockSpec(memory_space=pltpu.HBM)
    sem = pl.BlockSpec(memory_space=pltpu.SEMAPHORE)
    land_shapes = [s.shape if sc else (N_DEV,) + s.shape for s, sc in zip(srcs, scatter_flags)]

    def body(*refs):
        src, land = refs[:n], refs[n:2 * n]
        send_sems = refs[2 * n:2 * n + ns]
        recv_sems = refs[2 * n + ns:2 * n + 2 * ns]
        token = refs[4 * n + 2 * ns]
        local_sems = refs[4 * n + 2 * ns + 1]
        me, peers = _peer_list()
        barrier = pltpu.get_barrier_semaphore()
        for pid, _ in peers:
            pl.semaphore_signal(barrier, inc=1, device_id=pid, device_id_type=pl.DeviceIdType.MESH)
        pl.semaphore_wait(barrier, N_DEV - 1)
        for cp in _split_copies(src, land, send_sems, recv_sems, scatter_flags, me, peers, False):
            cp.start()
        local = []
        for k in range(n):
            own = src[k].at[me] if scatter_flags[k] else src[k]
            local.append(pltpu.make_async_copy(own, land[k].at[me], local_sems.at[k]))
            local[-1].start()
        for lc in local:
            lc.wait()
        token[...] = jnp.zeros_like(token)

    outs = pl.pallas_call(
        body, name=name,
        out_shape=(*[pltpu.SemaphoreType.DMA(())] * (2 * ns), *[pltpu.HBM(s.shape, s.dtype) for s in srcs],
                   *[pltpu.HBM(shp, s.dtype) for shp, s in zip(land_shapes, srcs)],
                   jax.ShapeDtypeStruct((SUBLANES, LANES), F32)),
        in_specs=[hbm] * (2 * n),
        out_specs=(*[sem] * (2 * ns), *[hbm] * (2 * n), pl.BlockSpec(memory_space=pltpu.VMEM)),
        input_output_aliases={i: 2 * ns + i for i in range(2 * n)},
        scratch_shapes=[pltpu.SemaphoreType.DMA((n,))],
        compiler_params=pltpu.CompilerParams(has_side_effects=pltpu.SideEffectType.DATAFLOW_SIDE_EFFECTING,
                                             collective_id=collective_id),
    )(*[pltpu.with_memory_space_constraint(s, pltpu.HBM) for s in srcs],
      *[pltpu.with_memory_space_constraint(lax.empty(shp, s.dtype), pltpu.HBM) for shp, s in zip(land_shapes, srcs)])
    return (outs[:ns], outs[ns:2 * ns], outs[2 * ns:2 * ns + n], outs[2 * ns + n:2 * ns + 2 * n], outs[2 * ns + 2 * n])


def _exchange_wait(send_sems, recv_sems, srcs, lands, scatter_flags, after, name):
    n = len(srcs)
    ns = n * (N_DEV - 1)
    hbm = pl.BlockSpec(memory_space=pltpu.HBM)
    sem = pl.BlockSpec(memory_space=pltpu.SEMAPHORE)

    def body(*refs):
        src, land = refs[:n], refs[n:2 * n]
        s_sems = refs[2 * n:2 * n + ns]
        r_sems = refs[2 * n + ns:2 * n + 2 * ns]
        me, peers = _peer_list()
        for cp in _split_copies(src, land, s_sems, r_sems, scatter_flags, me, peers, True):
            cp.wait_send()
            cp.wait_recv()

    outs = pl.pallas_call(
        body, name=name,
        out_shape=tuple(pltpu.HBM(a.shape, a.dtype) for a in (*srcs, *lands)),
        in_specs=[hbm] * (2 * n) + [sem] * (2 * ns) + [pl.BlockSpec(memory_space=pl.ANY)],
        out_specs=tuple([hbm] * (2 * n)),
        input_output_aliases={i: i for i in range(2 * n)},
        compiler_params=pltpu.CompilerParams(has_side_effects=pltpu.SideEffectType.DATAFLOW_SIDE_EFFECTING),
    )(*srcs, *lands, *send_sems, *recv_sems, after)
    return outs[n:]


def _tn_matmul(a, b, name, tk, tm, tn=512):
    n_tok, k_dim = a.shape
    m_dim = b.shape[1]
    grid = (k_dim // tk, m_dim // tm, n_tok // tn)

    def body(a_ref, b_ref, o_ref):
        @pl.when(pl.program_id(2) == 0)
        def _():
            o_ref[...] = jnp.zeros_like(o_ref)
        o_ref[...] += _dot_tn(a_ref[...].astype(BF16), b_ref[...].astype(BF16))

    return pl.pallas_call(
        body, name=name, grid=grid,
        in_specs=[pl.BlockSpec((tn, tk), lambda i, j, k: (k, i)), pl.BlockSpec((tn, tm), lambda i, j, k: (k, j))],
        out_specs=pl.BlockSpec((tk, tm), lambda i, j, k: (i, j)),
        out_shape=jax.ShapeDtypeStruct((k_dim, m_dim), F32),
        compiler_params=_cparams("parallel", "parallel", "arbitrary"),
    )(a, b)


def _adam_math(g, w, m, v):
    m = ADAM_B1 * m + (1.0 - ADAM_B1) * g
    v = ADAM_B2 * v + (1.0 - ADAM_B2) * (g * g)
    m_hat = m / (1.0 - ADAM_B1 ** ADAM_STEP)
    v_hat = v / (1.0 - ADAM_B2 ** ADAM_STEP)
    delta = -ADAM_LR * (m_hat / (jnp.sqrt(v_hat) + ADAM_EPS) + ADAM_WD * w)
    return delta, m, v


def _adam_sharded(land, w, m, v, name, tr):
    r, c = w.shape

    def body(l_ref, w_ref, m_ref, v_ref, g_ref, d_ref, nm_ref, nv_ref):
        g = l_ref[0]
        for s in range(1, N_DEV):
            g = g + l_ref[s]
        d, nm, nv = _adam_math(g, w_ref[...], m_ref[...], v_ref[...])
        g_ref[...] = g
        d_ref[...] = d
        nm_ref[...] = nm
        nv_ref[...] = nv

    spec = pl.BlockSpec((tr, c), lambda i: (i, 0))
    return pl.pallas_call(
        body, name=name, grid=(r // tr,),
        in_specs=[pl.BlockSpec((N_DEV, tr, c), lambda i: (0, i, 0)), spec, spec, spec],
        out_specs=(spec, spec, spec, spec),
        out_shape=tuple(jax.ShapeDtypeStruct((r, c), F32) for _ in range(4)),
        compiler_params=_cparams("parallel"),
    )(land, w, m, v)


def _sum_partials(parts, name):
    _, r, c = parts.shape

    def body(p_ref, o_ref):
        g = p_ref[0]
        for s in range(1, N_DEV):
            g = g + p_ref[s]
        o_ref[...] = g

    return pl.pallas_call(body, name=name, out_shape=jax.ShapeDtypeStruct((r, c), F32),
                          compiler_params=pltpu.CompilerParams(vmem_limit_bytes=VMEM_LIMIT))(parts)


def _adam_packed(g, w, m, v, name):
    def body(g_ref, w_ref, m_ref, v_ref, d_ref, nm_ref, nv_ref):
        d, nm, nv = _adam_math(g_ref[...], w_ref[...], m_ref[...], v_ref[...])
        d_ref[...] = d
        nm_ref[...] = nm
        nv_ref[...] = nv

    return pl.pallas_call(body, name=name, out_shape=tuple(jax.ShapeDtypeStruct(g.shape, F32) for _ in range(3)),
                          compiler_params=pltpu.CompilerParams(vmem_limit_bytes=VMEM_LIMIT))(g, w, m, v)


def _inproj_fwd(x, g, w_in, avg, qg, kg, tm=512):
    n = x.shape[0]

    def body(x_ref, g_ref, w_ref, a_ref, qg_ref, kg_ref, hn_ref, u_ref, qkv_ref, raw_ref, fl_ref):
        hn = _rms(x_ref[...], g_ref[...]).astype(BF16)
        hn_ref[...] = hn
        proj = _dot(hn, w_ref[...])
        u_ref[...] = proj[:, 0:512]
        q = proj[:, 512:1024]
        k = proj[:, 1024:1536]
        raw_ref[:, 0:512] = q
        raw_ref[:, 512:1024] = k
        qkv_ref[:, 0:512] = _headnorm(q, a_ref[...], qg_ref[...])
        qkv_ref[:, 512:1024] = _headnorm(k, a_ref[...], kg_ref[...])
        qkv_ref[:, 1024:1536] = proj[:, 1536:2048]
        fl_ref[...] = proj[:, 2048:D_IN_PAD]

    row = lambda w: pl.BlockSpec((tm, w), lambda i: (i, 0))
    full = lambda a: pl.BlockSpec(a.shape, lambda i: (0,) * a.ndim)
    return pl.pallas_call(
        body, name="inproj_fwd", grid=(n // tm,),
        in_specs=[row(D_MODEL), full(g), full(w_in), full(avg), full(qg), full(kg)],
        out_specs=(row(D_MODEL), row(512), row(1536), row(1024), row(LANES)),
        out_shape=(jax.ShapeDtypeStruct((n, D_MODEL), BF16), jax.ShapeDtypeStruct((n, 512), F32),
                   jax.ShapeDtypeStruct((n, 1536), F32), jax.ShapeDtypeStruct((n, 1024), F32),
                   jax.ShapeDtypeStruct((n, LANES), F32)),
        compiler_params=_cparams("parallel"),
    )(x, g, w_in, avg, qg, kg)


def _inproj_bwd(x, g, w_in, avg, qg, kg, raw, du, dqn, dkn, dv, dfl, dres, tm=512):
    n = x.shape[0]

    def body(x_ref, g_ref, w_ref, a_ref, qg_ref, kg_ref, raw_ref, du_ref, dqn_ref, dkn_ref, dv_ref, dfl_ref, dres_ref,
             dx_ref, dproj_ref, dg_ref, dqg_ref, dkg_ref):
        @pl.when(pl.program_id(0) == 0)
        def _():
            dg_ref[...] = jnp.zeros_like(dg_ref)
            dqg_ref[...] = jnp.zeros_like(dqg_ref)
            dkg_ref[...] = jnp.zeros_like(dkg_ref)
        avg_m = a_ref[...]
        _, vjp_q = jax.vjp(lambda q, gg: _headnorm(q, avg_m, gg), raw_ref[:, 0:512], qg_ref[...])
        dq, dqg = vjp_q(dqn_ref[...])
        _, vjp_k = jax.vjp(lambda k, gg: _headnorm(k, avg_m, gg), raw_ref[:, 512:1024], kg_ref[...])
        dk, dkg = vjp_k(dkn_ref[...])
        dproj = jnp.concatenate([du_ref[...], dq, dk, dv_ref[...], dfl_ref[...]], axis=1).astype(BF16)
        dproj_ref[...] = dproj
        dhn = _dot_nt(dproj, w_ref[...])
        _, vjp_x = jax.vjp(_rms, x_ref[...], g_ref[...])
        dxn, dg = vjp_x(dhn)
        dx_ref[...] = dxn + dres_ref[...]
        dg_ref[...] += dg
        dqg_ref[...] += dqg
        dkg_ref[...] += dkg

    row = lambda w: pl.BlockSpec((tm, w), lambda i: (i, 0))
    full = lambda a: pl.BlockSpec(a.shape, lambda i: (0,) * a.ndim)
    vec = lambda w: pl.BlockSpec((1, w), lambda i: (0, 0))
    return pl.pallas_call(
        body, name="inproj_bwd", grid=(n // tm,),
        in_specs=[row(D_MODEL), full(g), full(w_in), full(avg), full(qg), full(kg), row(1024), row(512), row(512),
                  row(512), row(512), row(LANES), row(D_MODEL)],
        out_specs=(row(D_MODEL), row(D_IN_PAD), vec(D_MODEL), vec(512), vec(512)),
        out_shape=(jax.ShapeDtypeStruct((n, D_MODEL), F32), jax.ShapeDtypeStruct((n, D_IN_PAD), BF16),
                   jax.ShapeDtypeStruct((1, D_MODEL), F32), jax.ShapeDtypeStruct((1, 512), F32),
                   jax.ShapeDtypeStruct((1, 512), F32)),
        compiler_params=_cparams("arbitrary"),
    )(x, g, w_in, avg, qg, kg, raw, du, dqn, dkn, dv, dfl, dres)


def _glu_fwd(yc, wg, bg, tm=512):
    n = yc.shape[0]

    def body(yc_ref, w_ref, b_ref, ys_ref):
        gl = jax.nn.gelu(yc_ref[...])
        z = _dot(gl.astype(BF16), w_ref[...]) + b_ref[...]
        ys_ref[...] = gl * jax.nn.sigmoid(z)

    row = pl.BlockSpec((tm, 512), lambda i: (i, 0))
    full = lambda a: pl.BlockSpec(a.shape, lambda i: (0,) * a.ndim)
    return pl.pallas_call(
        body, name="glu_fwd", grid=(n // tm,), in_specs=[row, full(wg), full(bg)], out_specs=row,
        out_shape=jax.ShapeDtypeStruct((n, 512), F32), compiler_params=_cparams("parallel"),
    )(yc, wg, bg)


def _glu_bwd(yc, dys, wg, bg, tm=512):
    n = yc.shape[0]

    def body(yc_ref, dys_ref, w_ref, b_ref, dyc_ref, gl_ref, dz_ref, db_ref):
        @pl.when(pl.program_id(0) == 0)
        def _():
            db_ref[...] = jnp.zeros_like(db_ref)
        gl, vjp_gelu = jax.vjp(jax.nn.gelu, yc_ref[...])
        glb = gl.astype(BF16)
        z = _dot(glb, w_ref[...]) + b_ref[...]
        s = jax.nn.sigmoid(z)
        dys = dys_ref[...]
        dz = dys * gl * s * (1.0 - s)
        dzb = dz.astype(BF16)
        dgl = dys * s + _dot_nt(dzb, w_ref[...])
        dyc_ref[...] = vjp_gelu(dgl)[0]
        gl_ref[...] = glb
        dz_ref[...] = dzb
        db_ref[...] += jnp.sum(dz, axis=0, keepdims=True)

    row = pl.BlockSpec((tm, 512), lambda i: (i, 0))
    full = lambda a: pl.BlockSpec(a.shape, lambda i: (0,) * a.ndim)
    return pl.pallas_call(
        body, name="glu_bwd", grid=(n // tm,), in_specs=[row, row, full(wg), full(bg)],
        out_specs=(row, row, row, pl.BlockSpec((1, 512), lambda i: (0, 0))),
        out_shape=(jax.ShapeDtypeStruct((n, 512), F32), jax.ShapeDtypeStruct((n, 512), BF16),
                   jax.ShapeDtypeStruct((n, 512), BF16), jax.ShapeDtypeStruct((1, 512), F32)),
        compiler_params=_cparams("arbitrary"),
    )(yc, dys, wg, bg)


def _mix_fwd(x, ys, ya, gs, ga, wout, gf, tm=512):
    n = x.shape[0]

    def body(x_ref, ys_ref, ya_ref, gs_ref, ga_ref, w_ref, gf_ref, h1_ref, hn2_ref, mixed_ref):
        mixed = jnp.concatenate([_rms(ys_ref[...], gs_ref[...]), _rms(ya_ref[...], ga_ref[...])], axis=1).astype(BF16)
        mixed_ref[...] = mixed
        h1 = x_ref[...] + _dot(mixed, w_ref[...])
        h1_ref[...] = h1
        hn2_ref[...] = _rms(h1, gf_ref[...]).astype(BF16)

    row = lambda w: pl.BlockSpec((tm, w), lambda i: (i, 0))
    full = lambda a: pl.BlockSpec(a.shape, lambda i: (0,) * a.ndim)
    return pl.pallas_call(
        body, name="mix_fwd", grid=(n // tm,),
        in_specs=[row(D_MODEL), row(512), row(512), full(gs), full(ga), full(wout), full(gf)],
        out_specs=(row(D_MODEL), row(D_MODEL), row(D_MODEL)),
        out_shape=(jax.ShapeDtypeStruct((n, D_MODEL), F32), jax.ShapeDtypeStruct((n, D_MODEL), BF16),
                   jax.ShapeDtypeStruct((n, D_MODEL), BF16)),
        compiler_params=_cparams("parallel"),
    )(x, ys, ya, gs, ga, wout, gf)


def _mix_bwd(dy, dhn2_parts, h1, ys, ya, gs, ga, wout, gf, tm=512):
    n = dy.shape[0]
    n_parts = dhn2_parts.shape[0]

    def body(dy_ref, dp_ref, h1_ref, ys_ref, ya_ref, gs_ref, ga_ref, w_ref, gf_ref,
             dh1_ref, dys_ref, dya_ref, dgs_ref, dga_ref, dgf_ref):
        @pl.when(pl.program_id(0) == 0)
        def _():
            dgs_ref[...] = jnp.zeros_like(dgs_ref)
            dga_ref[...] = jnp.zeros_like(dga_ref)
            dgf_ref[...] = jnp.zeros_like(dgf_ref)
        dhn2 = dp_ref[0]
        for p in range(1, n_parts):
            dhn2 = dhn2 + dp_ref[p]
        _, vjp_f = jax.vjp(_rms, h1_ref[...], gf_ref[...])
        dh1n, dgf = vjp_f(dhn2)
        dh1 = dy_ref[...] + dh1n
        dh1_ref[...] = dh1
        dmixed = _dot_nt(dh1.astype(BF16), w_ref[...])
        _, vjp_s = jax.vjp(_rms, ys_ref[...], gs_ref[...])
        dys, dgs = vjp_s(dmixed[:, 0:512])
        _, vjp_a = jax.vjp(_rms, ya_ref[...], ga_ref[...])
        dya, dga = vjp_a(dmixed[:, 512:1024])
        dys_ref[...] = dys
        dya_ref[...] = dya
        dgs_ref[...] += dgs
        dga_ref[...] += dga
        dgf_ref[...] += dgf

    row = lambda w: pl.BlockSpec((tm, w), lambda i: (i, 0))
    full = lambda a: pl.BlockSpec(a.shape, lambda i: (0,) * a.ndim)
    vec = lambda w: pl.BlockSpec((1, w), lambda i: (0, 0))
    return pl.pallas_call(
        body, name="mix_bwd", grid=(n // tm,),
        in_specs=[row(D_MODEL), pl.BlockSpec((n_parts, tm, D_MODEL), lambda i: (0, i, 0)), row(D_MODEL), row(512),
                  row(512), full(gs), full(ga), full(wout), full(gf)],
        out_specs=(row(D_MODEL), row(512), row(512), vec(512), vec(512), vec(D_MODEL)),
        out_shape=(jax.ShapeDtypeStruct((n, D_MODEL), F32), jax.ShapeDtypeStruct((n, 512), F32),
                   jax.ShapeDtypeStruct((n, 512), F32), jax.ShapeDtypeStruct((1, 512), F32),
                   jax.ShapeDtypeStruct((1, 512), F32), jax.ShapeDtypeStruct((1, D_MODEL), F32)),
        compiler_params=_cparams("arbitrary"),
    )(dy, dhn2_parts, h1, ys, ya, gs, ga, wout, gf)


HALO = 16


def _conv3(ue, cw):
    return cw[2:3] * ue + cw[1:2] * pltpu.roll(ue, 1, 0) + cw[0:1] * pltpu.roll(ue, 2, 0) + cw[3:4]


def _ffn_fwd(hn2, h1, target, wup_g, wup_v, cw_g, cw_v, wdown, seq_len, tm=512, fb=256):
    n = hn2.shape[0]
    nj = D_FF_PAD // fb
    hb = tm // HALO

    def body(hn_ref, halo_ref, h1_ref, tgt_ref, wg_ref, wv_ref, cg_ref, cv_ref, wd_ref,
             ug_ref, uv_ref, dy_ref, loss_ref, acc):
        i, j = pl.program_id(0), pl.program_id(1)
        seq_start = (i * tm) % seq_len == 0
        halo = halo_ref[...]
        halo = jnp.where(seq_start, jnp.zeros_like(halo), halo)
        he = jnp.concatenate([halo, hn_ref[...]], axis=0)
        ueg = _dot(he, wg_ref[...])
        uev = _dot(he, wv_ref[...])
        ug_ref[...] = ueg[HALO:]
        uv_ref[...] = uev[HALO:]
        cg = _conv3(ueg, cg_ref[...])[HALO:]
        cv = _conv3(uev, cv_ref[...])[HALO:]
        act = (jax.nn.silu(cg) * cv).astype(BF16)
        part = _dot(act, wd_ref[...])

        @pl.when(j == 0)
        def _():
            acc[...] = part

        @pl.when(j > 0)
        def _():
            acc[...] += part

        @pl.when(j == nj - 1)
        def _():
            err = h1_ref[...] + acc[...] - tgt_ref[...]
            dy_ref[...] = err * (1.0 / D_MODEL)
            loss_ref[0] = jnp.sum(err * err, axis=0, keepdims=True)

    row = pl.BlockSpec((tm, D_MODEL), lambda i, j: (i, 0))
    return pl.pallas_call(
        body, name="ffn_fwd", grid=(n // tm, nj),
        in_specs=[row, pl.BlockSpec((HALO, D_MODEL), lambda i, j: (jnp.maximum(i * hb - 1, 0), 0)), row, row,
                  pl.BlockSpec((D_MODEL, fb), lambda i, j: (0, j)), pl.BlockSpec((D_MODEL, fb), lambda i, j: (0, j)),
                  pl.BlockSpec((8, fb), lambda i, j: (0, j)), pl.BlockSpec((8, fb), lambda i, j: (0, j)),
                  pl.BlockSpec((fb, D_MODEL), lambda i, j: (j, 0))],
        out_specs=(pl.BlockSpec((tm, fb), lambda i, j: (i, j)), pl.BlockSpec((tm, fb), lambda i, j: (i, j)), row,
                   pl.BlockSpec((1, 1, D_MODEL), lambda i, j: (i, 0, 0))),
        out_shape=(jax.ShapeDtypeStruct((n, D_FF_PAD), F32), jax.ShapeDtypeStruct((n, D_FF_PAD), F32),
                   jax.ShapeDtypeStruct((n, D_MODEL), F32), jax.ShapeDtypeStruct((n // tm, 1, D_MODEL), F32)),
        scratch_shapes=[pltpu.VMEM((tm, D_MODEL), F32)],
        compiler_params=_cparams("parallel", "arbitrary"),
    )(hn2, hn2, h1, target, wup_g, wup_v, cw_g, cw_v, wdown)


def _ffn_bwd(dy, ug, uv, wup_g, wup_v, cw_g, cw_v, wdown, seq_len, tm=512, fb=256):
    n = dy.shape[0]
    nj = D_FF_PAD // fb
    hb = tm // HALO
    last_hb = n // HALO - 1
    rows = tm + HALO

    def body(dy_ref, dyn_ref, ugp_ref, ugm_ref, ugn_ref, uvp_ref, uvm_ref, uvn_ref, wg_ref, wv_ref, cg_ref, cv_ref,
             wd_ref, dug_ref, duv_ref, act_ref, dhn_ref, dcg_ref, dcv_ref, acc):
        i, j = pl.program_id(0), pl.program_id(1)
        seq_start = (i * tm) % seq_len == 0
        seq_end = ((i + 1) * tm) % seq_len == 0
        dyn = dyn_ref[...]
        dyn = jnp.where(seq_end, jnp.zeros_like(dyn), dyn)
        d_out = jnp.concatenate([dy_ref[...], dyn], axis=0).astype(BF16)
        d_act = _dot_nt(d_out, wd_ref[...])

        def pre_act(up_ref, um_ref, un_ref, cw):
            up = up_ref[...]
            up = jnp.where(seq_start, jnp.zeros_like(up), up)
            ue = jnp.concatenate([up, um_ref[...], un_ref[...]], axis=0)
            return ue, _conv3(ue, cw)[HALO:]

        cwg, cwv = cg_ref[...], cv_ref[...]
        ueg, cge = pre_act(ugp_ref, ugm_ref, ugn_ref, cwg)
        uev, cve = pre_act(uvp_ref, uvm_ref, uvn_ref, cwv)
        act, vjp_act = jax.vjp(lambda g, v: jax.nn.silu(g) * v, cge, cve)
        dcge, dcve = vjp_act(d_act)
        act_ref[...] = act[:tm].astype(BF16)

        def conv_t(dc, cw):
            return (cw[2:3] * dc + cw[1:2] * pltpu.roll(dc, rows - 1, 0) + cw[0:1] * pltpu.roll(dc, rows - 2, 0))[:tm]

        dug = conv_t(dcge, cwg).astype(BF16)
        duv = conv_t(dcve, cwv).astype(BF16)
        dug_ref[...] = dug
        duv_ref[...] = duv
        part = _dot_nt(dug, wg_ref[...]) + _dot_nt(duv, wv_ref[...])

        @pl.when(j == 0)
        def _():
            acc[...] = part

        @pl.when(j > 0)
        def _():
            acc[...] += part

        @pl.when(j == nj - 1)
        def _():
            dhn_ref[...] = acc[...]

        def cw_grad(dc, ue):
            dcm = dc[:tm]
            taps = [jnp.sum(dcm * pltpu.roll(ue, 2 - k, 0)[HALO:HALO + tm], axis=0, keepdims=True) for k in (0, 1)]
            taps.append(jnp.sum(dcm * ue[HALO:HALO + tm], axis=0, keepdims=True))
            taps.append(jnp.sum(dcm, axis=0, keepdims=True))
            return jnp.concatenate(taps + [jnp.zeros((4, fb), F32)], axis=0)

        @pl.when(i == 0)
        def _():
            dcg_ref[j] = jnp.zeros((8, fb), F32)
            dcv_ref[j] = jnp.zeros((8, fb), F32)

        dcg_ref[j] += cw_grad(dcge, ueg)
        dcv_ref[j] += cw_grad(dcve, uev)

    row = pl.BlockSpec((tm, D_MODEL), lambda i, j: (i, 0))
    u_prev = pl.BlockSpec((HALO, fb), lambda i, j: (jnp.maximum(i * hb - 1, 0), j))
    u_main = pl.BlockSpec((tm, fb), lambda i, j: (i, j))
    u_next = pl.BlockSpec((HALO, fb), lambda i, j: (jnp.minimum((i + 1) * hb, last_hb), j))
    w_col = pl.BlockSpec((D_MODEL, fb), lambda i, j: (0, j))
    c_col = pl.BlockSpec((8, fb), lambda i, j: (0, j))
    dc_spec = pl.BlockSpec((nj, 8, fb), lambda i, j: (0, 0, 0))
    return pl.pallas_call(
        body, name="ffn_bwd", grid=(n // tm, nj),
        in_specs=[row, pl.BlockSpec((HALO, D_MODEL), lambda i, j: (jnp.minimum((i + 1) * hb, last_hb), 0)),
                  u_prev, u_main, u_next, u_prev, u_main, u_next, w_col, w_col, c_col, c_col,
                  pl.BlockSpec((fb, D_MODEL), lambda i, j: (j, 0))],
        out_specs=(u_main, u_main, u_main, row, dc_spec, dc_spec),
        out_shape=(jax.ShapeDtypeStruct((n, D_FF_PAD), BF16), jax.ShapeDtypeStruct((n, D_FF_PAD), BF16),
                   jax.ShapeDtypeStruct((n, D_FF_PAD), BF16), jax.ShapeDtypeStruct((n, D_MODEL), F32),
                   jax.ShapeDtypeStruct((nj, 8, fb), F32), jax.ShapeDtypeStruct((nj, 8, fb), F32)),
        scratch_shapes=[pltpu.VMEM((tm, D_MODEL), F32)],
        compiler_params=_cparams("arbitrary", "arbitrary"),
    )(dy, dy, ug, ug, ug, uv, uv, uv, wup_g, wup_v, cw_g, cw_v, wdown)


def _s5_param_fn(lr, li, ldt, br, bi):
    dt = jnp.exp(ldt)
    mag = jnp.exp(lr * dt)
    ab_re = mag * jnp.cos(li * dt)
    ab_im = mag * jnp.sin(li * dt)
    nr = ab_re - 1.0
    ni = ab_im
    den = lr * lr + li * li
    q_re = (nr * lr + ni * li) / den
    q_im = (ni * lr - nr * li) / den
    bb_re = q_re * br - q_im * bi
    bb_im = q_re * bi + q_im * br
    return ab_re, ab_im, bb_re, bb_im


def _s5_param_fwd(lr, li, ldt, br, bi):
    def body(lr_ref, li_ref, ldt_ref, br_ref, bi_ref, ar_ref, ai_ref, bbr_ref, bbi_ref):
        ar, ai, bbr, bbi = _s5_param_fn(lr_ref[...], li_ref[...], ldt_ref[...], br_ref[...], bi_ref[...])
        ar_ref[...] = ar
        ai_ref[...] = ai
        bbr_ref[...] = bbr
        bbi_ref[...] = bbi

    return pl.pallas_call(
        body, name="s5_param_fwd",
        out_shape=(jax.ShapeDtypeStruct(lr.shape, F32), jax.ShapeDtypeStruct(lr.shape, F32),
                   jax.ShapeDtypeStruct(br.shape, F32), jax.ShapeDtypeStruct(br.shape, F32)),
    )(lr, li, ldt, br, bi)


def _s5_param_bwd(lr, li, ldt, br, bi, dar, dai, dbbr, dbbi):
    def body(lr_ref, li_ref, ldt_ref, br_ref, bi_ref, dar_ref, dai_ref, dbbr_ref, dbbi_ref,
             dlr_ref, dli_ref, dldt_ref, dbr_ref, dbi_ref):
        _, vjp = jax.vjp(_s5_param_fn, lr_ref[...], li_ref[...], ldt_ref[...], br_ref[...], bi_ref[...])
        dlr, dli, dldt, dbr, dbi = vjp((dar_ref[...], dai_ref[...], dbbr_ref[...], dbbi_ref[...]))
        dlr_ref[...] = dlr
        dli_ref[...] = dli
        dldt_ref[...] = dldt
        dbr_ref[...] = dbr
        dbi_ref[...] = dbi

    return pl.pallas_call(
        body, name="s5_param_bwd",
        out_shape=(jax.ShapeDtypeStruct(lr.shape, F32), jax.ShapeDtypeStruct(lr.shape, F32),
                   jax.ShapeDtypeStruct(ldt.shape, F32), jax.ShapeDtypeStruct(br.shape, F32),
                   jax.ShapeDtypeStruct(br.shape, F32)),
    )(lr, li, ldt, br, bi, dar, dai, dbbr, dbbi)


S5_CHUNK = 256
S5_STATES = 512
S5_BLOCKS = 4


def _cpow_rows(ar, ai, count):
    rs, im = [ar], [ai]
    for _ in range(count - 1):
        pr, pi = rs[-1], im[-1]
        rs.append(pr * ar - pi * ai)
        im.append(pr * ai + pi * ar)
    return rs, im


def _scan_in_groups(vr, vi, pr, pi, rm, reverse):
    n = vr.shape[0]
    for k in (1, 2, 4):
        if reverse:
            sr, si, keep = pltpu.roll(vr, n - k, 0), pltpu.roll(vi, n - k, 0), rm < SUBLANES - k
        else:
            sr, si, keep = pltpu.roll(vr, k, 0), pltpu.roll(vi, k, 0), rm >= k
        sr = jnp.where(keep, sr, 0.0)
        si = jnp.where(keep, si, 0.0)
        kr, ki = pr[k - 1], pi[k - 1]
        vr, vi = vr + kr * sr - ki * si, vi + kr * si + ki * sr
    return vr, vi


def _carry_over_groups(xr_s, xi_s, wr, wi, c0r, c0i, reverse):
    groups = xr_s.shape[0] // SUBLANES
    pick = 0 if reverse else SUBLANES - 1

    def step(q, carry):
        cr, ci = carry
        r = groups - 1 - q if reverse else q
        o = pl.multiple_of(r * SUBLANES, SUBLANES)
        vr = xr_s[pl.ds(o, SUBLANES), :]
        vi = xi_s[pl.ds(o, SUBLANES), :]
        nr = vr + wr * cr - wi * ci
        ni = vi + wr * ci + wi * cr
        xr_s[pl.ds(o, SUBLANES), :] = nr
        xi_s[pl.ds(o, SUBLANES), :] = ni
        return (jnp.broadcast_to(nr[pick:pick + 1], nr.shape), jnp.broadcast_to(ni[pick:pick + 1], ni.shape))

    return lax.fori_loop(0, groups, step, (c0r, c0i))


def _s5_state_scan(u_b, bbr, bbi, pr, pi, rm, xr_s, xi_s, c0r, c0i):
    bur = _dot(u_b, bbr)
    bui = _dot(u_b, bbi)
    bur, bui = _scan_in_groups(bur, bui, pr, pi, rm, False)
    xr_s[...] = bur
    xi_s[...] = bui
    w8r = jnp.concatenate(pr, axis=0)
    w8i = jnp.concatenate(pi, axis=0)
    return _carry_over_groups(xr_s, xi_s, w8r, w8i, c0r, c0i, False)


def _s5_fwd(u, a_re, a_im, bbr, bbi, cr, ci, d_skip, n_seq):
    n = u.shape[0]
    seq_len = n // n_seq
    nt = seq_len // S5_CHUNK
    tc = S5_CHUNK

    def body(u_ref, ar_ref, ai_ref, bbr_ref, bbi_ref, cr_ref, ci_ref, d_ref, y_ref, str_ref, sti_ref,
             xr_s, xi_s, car_r, car_i):
        t = pl.program_id(2)

        @pl.when(t == 0)
        def _():
            car_r[...] = jnp.zeros_like(car_r)
            car_i[...] = jnp.zeros_like(car_i)
        pr, pi = _cpow_rows(ar_ref[0], ai_ref[0], SUBLANES)
        rm = lax.broadcasted_iota(jnp.int32, (tc, S5_STATES), 0) & (SUBLANES - 1)
        str_ref[0, 0] = car_r[...]
        sti_ref[0, 0] = car_i[...]
        u_t = u_ref[...]
        cfr, cfi = _s5_state_scan(u_t.astype(BF16), bbr_ref[0], bbi_ref[0], pr, pi, rm, xr_s, xi_s,
                                  car_r[...], car_i[...])
        car_r[...] = cfr
        car_i[...] = cfi
        y = _dot(xr_s[...].astype(BF16), cr_ref[0]) - _dot(xi_s[...].astype(BF16), ci_ref[0])
        y_ref[...] = y + d_ref[...] * u_t

    u_spec = pl.BlockSpec((tc, LANES), lambda cb, b, t: (b * nt + t, cb))
    a_spec = pl.BlockSpec((1, 1, S5_STATES), lambda cb, b, t: (cb, 0, 0))
    bb_spec = pl.BlockSpec((1, LANES, S5_STATES), lambda cb, b, t: (cb, 0, 0))
    c_spec = pl.BlockSpec((1, S5_STATES, LANES), lambda cb, b, t: (cb, 0, 0))
    st_spec = pl.BlockSpec((1, 1, SUBLANES, S5_STATES), lambda cb, b, t: (cb, b * nt + t, 0, 0))
    st_shape = jax.ShapeDtypeStruct((S5_BLOCKS, n_seq * nt, SUBLANES, S5_STATES), F32)
    return pl.pallas_call(
        body, name="s5_fwd", grid=(S5_BLOCKS, n_seq, nt),
        in_specs=[u_spec, a_spec, a_spec, bb_spec, bb_spec, c_spec, c_spec,
                  pl.BlockSpec((1, LANES), lambda cb, b, t: (0, cb))],
        out_specs=(u_spec, st_spec, st_spec),
        out_shape=(jax.ShapeDtypeStruct((n, D_SSM), F32), st_shape, st_shape),
        scratch_shapes=[pltpu.VMEM((tc, S5_STATES), F32), pltpu.VMEM((tc, S5_STATES), F32),
                        pltpu.VMEM((SUBLANES, S5_STATES), F32), pltpu.VMEM((SUBLANES, S5_STATES), F32)],
        compiler_params=_cparams("parallel", "arbitrary", "arbitrary"),
    )(u, a_re, a_im, bbr, bbi, cr, ci, d_skip)


def _s5_bwd(u, dy, st_r, st_i, a_re, a_im, bbr, bbi, cr, ci, d_skip, n_seq):
    n = u.shape[0]
    seq_len = n // n_seq
    nt = seq_len // S5_CHUNK
    tc = S5_CHUNK

    def body(u_ref, dy_ref, str_ref, sti_ref, ar_ref, ai_ref, bbr_ref, bbi_ref, cr_ref, ci_ref, d_ref,
             du_ref, dbbr_ref, dbbi_ref, dcr_ref, dci_ref, dar_ref, dai_ref, dd_ref,
             xr_s, xi_s, gr_s, gi_s, car_r, car_i):
        b, t = pl.program_id(1), pl.program_id(2)

        @pl.when((b == 0) & (t == 0))
        def _():
            for ref in (dbbr_ref, dbbi_ref, dcr_ref, dci_ref, dar_ref, dai_ref, dd_ref):
                ref[...] = jnp.zeros_like(ref)

        @pl.when(t == 0)
        def _():
            car_r[...] = jnp.zeros_like(car_r)
            car_i[...] = jnp.zeros_like(car_i)
        ar, ai = ar_ref[0], ai_ref[0]
        pr, pi = _cpow_rows(ar, ai, SUBLANES)
        row = lax.broadcasted_iota(jnp.int32, (tc, S5_STATES), 0)
        rm = row & (SUBLANES - 1)
        u_t = u_ref[...]
        u_b = u_t.astype(BF16)
        dy_t = dy_ref[...]
        dy_b = dy_t.astype(BF16)
        s0r, s0i = str_ref[0, 0], sti_ref[0, 0]
        _s5_state_scan(u_b, bbr_ref[0], bbi_ref[0], pr, pi, rm, xr_s, xi_s, s0r, s0i)
        xr, xi = xr_s[...], xi_s[...]
        gr = _dot_nt(dy_b, cr_ref[0])
        gi = -_dot_nt(dy_b, ci_ref[0])
        npi = [-v for v in pi]
        gr, gi = _scan_in_groups(gr, gi, pr, npi, rm, True)
        gr_s[...] = gr
        gi_s[...] = gi
        w8r = jnp.concatenate(pr[::-1], axis=0)
        w8i = jnp.concatenate(npi[::-1], axis=0)
        cfr, cfi = _carry_over_groups(gr_s, gi_s, w8r, w8i, car_r[...], car_i[...], True)
        car_r[...] = cfr
        car_i[...] = cfi
        gr, gi = gr_s[...], gi_s[...]
        gr_b, gi_b = gr.astype(BF16), gi.astype(BF16)
        du_ref[...] = _dot_nt(gr_b, bbr_ref[0]) + _dot_nt(gi_b, bbi_ref[0]) + d_ref[...] * dy_t
        dbbr_ref[0] += _dot_tn(u_b, gr_b)
        dbbi_ref[0] += _dot_tn(u_b, gi_b)
        dcr_ref[0] += _dot_tn(xr.astype(BF16), dy_b)
        dci_ref[0] -= _dot_tn(xi.astype(BF16), dy_b)
        dd_ref[0] += jnp.sum((dy_t * u_t).reshape(tc // SUBLANES, SUBLANES, LANES), axis=0)
        first = row == 0
        xpr = jnp.where(first, jnp.broadcast_to(s0r[0:1], xr.shape), pltpu.roll(xr, 1, 0))
        xpi = jnp.where(first, jnp.broadcast_to(s0i[0:1], xi.shape), pltpu.roll(xi, 1, 0))
        shp = (tc // SUBLANES, SUBLANES, S5_STATES)
        dar_ref[0] += jnp.sum((gr * xpr + gi * xpi).reshape(shp), axis=0)
        dai_ref[0] += jnp.sum((gi * xpr - gr * xpi).reshape(shp), axis=0)

    u_spec = pl.BlockSpec((tc, LANES), lambda cb, b, t: (b * nt + nt - 1 - t, cb))
    a_spec = pl.BlockSpec((1, 1, S5_STATES), lambda cb, b, t: (cb, 0, 0))
    bb_spec = pl.BlockSpec((1, LANES, S5_STATES), lambda cb, b, t: (cb, 0, 0))
    c_spec = pl.BlockSpec((1, S5_STATES, LANES), lambda cb, b, t: (cb, 0, 0))
    st_spec = pl.BlockSpec((1, 1, SUBLANES, S5_STATES), lambda cb, b, t: (cb, b * nt + nt - 1 - t, 0, 0))
    da_spec = pl.BlockSpec((1, SUBLANES, S5_STATES), lambda cb, b, t: (cb, 0, 0))
    dd_spec = pl.BlockSpec((1, SUBLANES, LANES), lambda cb, b, t: (cb, 0, 0))
    big = pltpu.VMEM((tc, S5_STATES), F32)
    small = pltpu.VMEM((SUBLANES, S5_STATES), F32)
    return pl.pallas_call(
        body, name="s5_bwd", grid=(S5_BLOCKS, n_seq, nt),
        in_specs=[u_spec, u_spec, st_spec, st_spec, a_spec, a_spec, bb_spec, bb_spec, c_spec, c_spec,
                  pl.BlockSpec((1, LANES), lambda cb, b, t: (0, cb))],
        out_specs=(u_spec, bb_spec, bb_spec, c_spec, c_spec, da_spec, da_spec, dd_spec),
        out_shape=(jax.ShapeDtypeStruct((n, D_SSM), F32),
                   jax.ShapeDtypeStruct((S5_BLOCKS, LANES, S5_STATES), F32),
                   jax.ShapeDtypeStruct((S5_BLOCKS, LANES, S5_STATES), F32),
                   jax.ShapeDtypeStruct((S5_BLOCKS, S5_STATES, LANES), F32),
                   jax.ShapeDtypeStruct((S5_BLOCKS, S5_STATES, LANES), F32),
                   jax.ShapeDtypeStruct((S5_BLOCKS, SUBLANES, S5_STATES), F32),
                   jax.ShapeDtypeStruct((S5_BLOCKS, SUBLANES, S5_STATES), F32),
                   jax.ShapeDtypeStruct((S5_BLOCKS, SUBLANES, LANES), F32)),
        scratch_shapes=[big, big, big, big, small, small],
        compiler_params=_cparams("parallel", "arbitrary", "arbitrary"),
    )(u, dy, st_r, st_i, a_re, a_im, bbr, bbi, cr, ci, d_skip)


CUM_BLOCK = 128


def _tri(lower):
    r = lax.broadcasted_iota(jnp.int32, (CUM_BLOCK, CUM_BLOCK), 0)
    c = lax.broadcasted_iota(jnp.int32, (CUM_BLOCK, CUM_BLOCK), 1)
    return jnp.where(r >= c if lower else r <= c, 1.0, 0.0).astype(F32)


def _fprep_fwd(fl, bf, n_seq):
    n = fl.shape[0]
    seq_len = n // n_seq
    nb = seq_len // CUM_BLOCK

    def body(fl_ref, bf_ref, cum_ref):
        tril = _tri(True)
        carry = jnp.zeros((1, LANES), F32)
        for blk in range(nb):
            rows = slice(blk * CUM_BLOCK, (blk + 1) * CUM_BLOCK)
            lf = jax.nn.log_sigmoid(fl_ref[rows, :] + bf_ref[...])
            cs = jnp.dot(tril, lf, preferred_element_type=F32, precision=HIGHEST) + carry
            cum_ref[rows, :] = cs
            carry = cs[CUM_BLOCK - 1:CUM_BLOCK, :]

    spec = pl.BlockSpec((seq_len, LANES), lambda b: (b, 0))
    return pl.pallas_call(
        body, name="fprep_fwd", grid=(n_seq,), in_specs=[spec, pl.BlockSpec((1, LANES), lambda b: (0, 0))],
        out_specs=spec, out_shape=jax.ShapeDtypeStruct((n, LANES), F32), compiler_params=_cparams("parallel"),
    )(fl, bf)


def _fprep_bwd(dcum, fl, bf, n_seq):
    n = fl.shape[0]
    seq_len = n // n_seq
    nb = seq_len // CUM_BLOCK

    def body(dcum_ref, fl_ref, bf_ref, dfl_ref, dbf_ref):
        triu = _tri(False)
        lane = lax.broadcasted_iota(jnp.int32, (CUM_BLOCK, LANES), 1)
        carry = jnp.zeros((1, LANES), F32)
        total = jnp.zeros((1, LANES), F32)
        for blk in reversed(range(nb)):
            rows = slice(blk * CUM_BLOCK, (blk + 1) * CUM_BLOCK)
            rs = jnp.dot(triu, dcum_ref[rows, :], preferred_element_type=F32, precision=HIGHEST) + carry
            carry = rs[0:1, :]
            _, vjp = jax.vjp(jax.nn.log_sigmoid, fl_ref[rows, :] + bf_ref[...])
            dz = jnp.where(lane < N_HEADS, vjp(rs)[0], 0.0)
            dfl_ref[rows, :] = dz
            total = total + jnp.sum(dz, axis=0, keepdims=True)
        dbf_ref[0] = total

    spec = pl.BlockSpec((seq_len, LANES), lambda b: (b, 0))
    return pl.pallas_call(
        body, name="fprep_bwd", grid=(n_seq,), in_specs=[spec, spec, pl.BlockSpec((1, LANES), lambda b: (0, 0))],
        out_specs=(spec, pl.BlockSpec((1, 1, LANES), lambda b: (b, 0, 0))),
        out_shape=(jax.ShapeDtypeStruct((n, LANES), F32), jax.ShapeDtypeStruct((n_seq, 1, LANES), F32)),
        compiler_params=_cparams("parallel"),
    )(dcum, fl, bf)


ATT_TQ = 256
ATT_SCALE = HEAD_DIM ** -0.5
NEG_BIG = -1e30


def _attn_scores(qe, kb, cq, ck, causal):
    s = _dot_nt(qe, kb) * ATT_SCALE + cq - ck
    return jnp.where(causal, s, NEG_BIG)


def _causal_mask(qi, seq_len):
    r = lax.broadcasted_iota(jnp.int32, (ATT_TQ, seq_len), 0) + qi * ATT_TQ
    c = lax.broadcasted_iota(jnp.int32, (ATT_TQ, seq_len), 1)
    return r >= c


def _attn_specs(n_seq, seq_len):
    nq = seq_len // ATT_TQ
    q_spec = pl.BlockSpec((ATT_TQ, LANES), lambda b, h, q: (b * nq + q, h))
    k_spec = pl.BlockSpec((seq_len, LANES), lambda b, h, q: (b, N_HEADS // 2 + h))
    v_spec = pl.BlockSpec((seq_len, LANES), lambda b, h, q: (b, N_HEADS + h))
    cq_spec = pl.BlockSpec((1, 2, ATT_TQ, 1), lambda b, h, q: (b, h, q, 0))
    ck_spec = pl.BlockSpec((1, 2, 1, seq_len), lambda b, h, q: (b, h, 0, 0))
    return nq, q_spec, k_spec, v_spec, cq_spec, ck_spec


def _attn_fwd(qkv, cq, ck, n_seq):
    n = qkv.shape[0]
    seq_len = n // n_seq
    nq, q_spec, k_spec, v_spec, cq_spec, ck_spec = _attn_specs(n_seq, seq_len)

    def body(q_ref, k_ref, v_ref, cq_ref, ck_ref, o_ref, lse_ref):
        q2 = q_ref[...]
        kb = k_ref[...].astype(BF16)
        vb = v_ref[...].astype(BF16)
        head0 = lax.broadcasted_iota(jnp.int32, (1, LANES), 1) < HEAD_DIM
        causal = _causal_mask(pl.program_id(2), seq_len)
        outs = []
        for e in range(2):
            sel = head0 if e == 0 else jnp.logical_not(head0)
            qe = jnp.where(sel, q2, 0.0).astype(BF16)
            s = _attn_scores(qe, kb, cq_ref[0, e], ck_ref[0, e], causal)
            mx = jnp.max(s, axis=1, keepdims=True)
            p = jnp.exp(s - mx)
            den = jnp.sum(p, axis=1, keepdims=True)
            outs.append(_dot(p.astype(BF16), vb) / den)
            lse_ref[0, e] = mx + jnp.log(den)
        o_ref[...] = jnp.where(head0, outs[0], outs[1])

    return pl.pallas_call(
        body, name="attn_fwd", grid=(n_seq, N_HEADS // 2, nq),
        in_specs=[q_spec, k_spec, v_spec, cq_spec, ck_spec],
        out_specs=(q_spec, cq_spec),
        out_shape=(jax.ShapeDtypeStruct((n, D_ATTN), F32), jax.ShapeDtypeStruct((n_seq, N_HEADS, seq_len, 1), F32)),
        compiler_params=_cparams("parallel", "parallel", "parallel"),
    )(qkv, qkv, qkv, cq, ck)


def _attn_bwd(qkv, cq, ck, o, do, lse, n_seq):
    n = qkv.shape[0]
    seq_len = n // n_seq
    nq, q_spec, k_spec, v_spec, cq_spec, ck_spec = _attn_specs(n_seq, seq_len)
    kv_out = pl.BlockSpec((seq_len, LANES), lambda b, h, q: (b, h))

    def body(q_ref, k_ref, v_ref, cq_ref, ck_ref, o_ref, do_ref, lse_ref, dq_ref, dk_ref, dv_ref, dcq_ref, dck_ref):
        @pl.when(pl.program_id(2) == 0)
        def _():
            dk_ref[...] = jnp.zeros_like(dk_ref)
            dv_ref[...] = jnp.zeros_like(dv_ref)
            dck_ref[...] = jnp.zeros_like(dck_ref)
        q2 = q_ref[...]
        kb = k_ref[...].astype(BF16)
        vb = v_ref[...].astype(BF16)
        do2 = do_ref[...]
        o2 = o_ref[...]
        head0 = lax.broadcasted_iota(jnp.int32, (1, LANES), 1) < HEAD_DIM
        causal = _causal_mask(pl.program_id(2), seq_len)
        dqs = []
        for e in range(2):
            sel = head0 if e == 0 else jnp.logical_not(head0)
            qe = jnp.where(sel, q2, 0.0).astype(BF16)
            doe = jnp.where(sel, do2, 0.0)
            doe_b = doe.astype(BF16)
            s = _attn_scores(qe, kb, cq_ref[0, e], ck_ref[0, e], causal)
            p = jnp.exp(s - lse_ref[0, e])
            delta = jnp.sum(doe * o2, axis=1, keepdims=True)
            ds = p * (_dot_nt(doe_b, vb) - delta)
            ds_b = ds.astype(BF16)
            dqs.append(_dot(ds_b, kb) * ATT_SCALE)
            dk_ref[...] += _dot_tn(ds_b, qe) * ATT_SCALE
            dv_ref[...] += _dot_tn(p.astype(BF16), doe_b)
            dcq_ref[0, e] = jnp.sum(ds, axis=1, keepdims=True)
            dck_ref[0, e] -= jnp.sum(ds, axis=0, keepdims=True)
        dq_ref[...] = jnp.where(head0, dqs[0], dqs[1])

    return pl.pallas_call(
        body, name="attn_bwd", grid=(n_seq, N_HEADS // 2, nq),
        in_specs=[q_spec, k_spec, v_spec, cq_spec, ck_spec, q_spec, q_spec, cq_spec],
        out_specs=(q_spec, kv_out, kv_out, cq_spec, ck_spec),
        out_shape=(jax.ShapeDtypeStruct((n, D_ATTN), F32), jax.ShapeDtypeStruct((n, D_ATTN), F32),
                   jax.ShapeDtypeStruct((n, D_ATTN), F32),
                   jax.ShapeDtypeStruct((n_seq, N_HEADS, seq_len, 1), F32),
                   jax.ShapeDtypeStruct((n_seq, N_HEADS, 1, seq_len), F32)),
        compiler_params=_cparams("parallel", "parallel", "arbitrary"),
    )(qkv, qkv, qkv, cq, ck, o, do, lse)


WEIGHT_NAMES = ("norm_mix", "w_in", "b_forget", "lam_re", "lam_im", "b_re", "b_im", "c_re", "c_im", "d_skip", "log_dt",
                "w_glu", "b_glu", "q_norm", "k_norm", "norm_out_ssm", "norm_out_attn", "w_out", "norm_ffn", "w_up",
                "conv_w", "conv_b", "w_down")
SHARDED = ("w_in", "w_glu", "w_out", "w_up", "conv_w", "w_down")
ADAM_ROWS = {"w_in": 256, "w_glu": 64, "w_out": 128, "w_up": 128, "conv_w": 3, "w_down": 344}
PACK_ROWS = SUBLANES * LANES


def _pad_to(a, axis, size):
    pad = [(0, 0)] * a.ndim
    pad[axis] = (0, size - a.shape[axis])
    return jnp.pad(a, pad)


def _block_diag(t, transpose):
    t4 = t.reshape(S5_BLOCKS, 8, SSM_GROUP, SSM_STATE)
    eye = jnp.eye(8, dtype=t.dtype)
    if transpose:
        e = jnp.swapaxes(t4, 2, 3)[:, :, :, None, :] * eye[None, :, None, :, None]
        return e.reshape(S5_BLOCKS, S5_STATES, LANES)
    e = t4[:, :, :, None, :] * eye[None, :, None, :, None]
    return e.reshape(S5_BLOCKS, LANES, S5_STATES)


def _block_diag_extract(m, transpose):
    if transpose:
        m5 = m.reshape(S5_BLOCKS, 8, SSM_STATE, 8, SSM_GROUP)
        d = jnp.stack([m5[:, i, :, i, :] for i in range(8)], axis=1)
        return jnp.swapaxes(d, 2, 3).reshape(N_GROUPS, SSM_GROUP, SSM_STATE)
    m5 = m.reshape(S5_BLOCKS, 8, SSM_GROUP, 8, SSM_STATE)
    d = jnp.stack([m5[:, i, :, i, :] for i in range(8)], axis=1)
    return d.reshape(N_GROUPS, SSM_GROUP, SSM_STATE)


def _pack(pieces):
    flat = jnp.concatenate([p.reshape(-1).astype(F32) for p in pieces])
    size = -(-flat.shape[0] // PACK_ROWS) * PACK_ROWS
    return _pad_to(flat, 0, size).reshape(-1, LANES)


def _unpack(packed, shapes):
    flat = packed.reshape(-1)
    out, off = [], 0
    for shp in shapes:
        size = math.prod(shp)
        out.append(flat[off:off + size].reshape(shp))
        off += size
    return out


def kernel(x, norm_mix, w_in, b_forget, lam_re, lam_im, b_re, b_im, c_re, c_im, d_skip, log_dt, w_glu, b_glu, q_norm, k_norm, norm_out_ssm, norm_out_attn, w_out, norm_ffn, w_up, conv_w, conv_b, w_down, loss_target, m_norm_mix, m_w_in, m_b_forget, m_lam_re, m_lam_im, m_b_re, m_b_im, m_c_re, m_c_im, m_d_skip, m_log_dt, m_w_glu, m_b_glu, m_q_norm, m_k_norm, m_norm_out_ssm, m_norm_out_attn, m_w_out, m_norm_ffn, m_w_up, m_conv_w, m_conv_b, m_w_down, v_norm_mix, v_w_in, v_b_forget, v_lam_re, v_lam_im, v_b_re, v_b_im, v_c_re, v_c_im, v_d_skip, v_log_dt, v_w_glu, v_b_glu, v_q_norm, v_k_norm, v_norm_out_ssm, v_norm_out_attn, v_w_out, v_norm_ffn, v_w_up, v_conv_w, v_conv_b, v_w_down):
    given = dict(locals())
    weights = {k: given[k] for k in WEIGHT_NAMES}
    mom1 = {k: given["m_" + k] for k in WEIGHT_NAMES}
    mom2 = {k: given["v_" + k] for k in WEIGHT_NAMES}
    n_seq, seq_len, _ = x.shape
    n = n_seq * seq_len
    xf = x.reshape(n, D_MODEL)
    target = loss_target.reshape(n, D_MODEL)

    (g_in,) = _exchange([w_in[0].astype(BF16)], [False], "gather_w_in")
    rest_flags = [False] * 5
    w_sems = _exchange_start(
        [w_glu[0].astype(BF16), w_out[0].astype(BF16), w_up[0].astype(BF16), conv_w[0], w_down[0].astype(BF16)],
        rest_flags, "gather_rest_start", 0)
    norm_mix = norm_mix + w_sems[4][0, 0]
    w_in_p = _pad_to(jnp.swapaxes(g_in, 0, 1).reshape(D_MODEL, D_IN), 1, D_IN_PAD)

    lr3 = lam_re[0].reshape(N_GROUPS, 1, SSM_STATE)
    li3 = lam_im[0].reshape(N_GROUPS, 1, SSM_STATE)
    ldt3 = log_dt[0].reshape(N_GROUPS, 1, 1)
    br_t = jnp.swapaxes(b_re[0], 1, 2)
    bi_t = jnp.swapaxes(b_im[0], 1, 2)
    ab_re, ab_im, bb_re, bb_im = _s5_param_fwd(lr3, li3, ldt3, br_t, bi_t)
    a_re = ab_re.reshape(S5_BLOCKS, 1, S5_STATES)
    a_im = ab_im.reshape(S5_BLOCKS, 1, S5_STATES)
    bbr = _block_diag(bb_re, False).astype(BF16)
    bbi = _block_diag(bb_im, False).astype(BF16)
    cr = _block_diag(c_re[0], True).astype(BF16)
    ci = _block_diag(c_im[0], True).astype(BF16)

    avg = jnp.kron(jnp.eye(N_HEADS, dtype=F32), jnp.full((HEAD_DIM, HEAD_DIM), 1.0 / HEAD_DIM, F32))
    qg = jnp.tile(q_norm, (1, N_HEADS))
    kg = jnp.tile(k_norm, (1, N_HEADS))
    hn, u, qkv, raw, fl = _inproj_fwd(xf, norm_mix, w_in_p, avg, qg, kg)
    yc, st_r, st_i = _s5_fwd(u, a_re, a_im, bbr, bbi, cr, ci, d_skip, n_seq)
    bf = _pad_to(b_forget, 1, LANES)
    cum = _fprep_fwd(fl, bf, n_seq)
    cum8 = jnp.swapaxes(cum[:, :N_HEADS].reshape(n_seq, seq_len, N_HEADS), 1, 2)
    cq = cum8[:, :, :, None]
    ck = cum8[:, :, None, :]
    ya, lse = _attn_fwd(qkv, cq, ck, n_seq)
    g_glu, g_out, g_up, g_cw, g_down = _exchange_wait(w_sems[0], w_sems[1], w_sems[2], w_sems[3], rest_flags, ya,
                                                      "gather_rest_wait")
    w_glu_f = g_glu.reshape(D_SSM, D_SSM)
    w_out_f = g_out.reshape(D_MODEL, D_MODEL)
    w_up_f = jnp.swapaxes(g_up, 0, 1).reshape(D_MODEL, 2 * D_FF)
    wup_g = _pad_to(w_up_f[:, :D_FF], 1, D_FF_PAD)
    wup_v = _pad_to(w_up_f[:, D_FF:], 1, D_FF_PAD)
    cw_f = jnp.swapaxes(g_cw, 0, 1).reshape(3, 2 * D_FF)
    cw4 = jnp.concatenate([cw_f, conv_b], axis=0)
    cw_g = _pad_to(_pad_to(cw4[:, :D_FF], 1, D_FF_PAD), 0, SUBLANES)
    cw_v = _pad_to(_pad_to(cw4[:, D_FF:], 1, D_FF_PAD), 0, SUBLANES)
    w_down_p = _pad_to(g_down.reshape(D_FF, D_MODEL), 0, D_FF_PAD)
    ys = _glu_fwd(yc, w_glu_f, b_glu)
    h1, hn2, mixed = _mix_fwd(xf, ys, ya, norm_out_ssm, norm_out_attn, w_out_f, norm_ffn)
    ug, uv, dy, loss_part = _ffn_fwd(hn2, h1, target, wup_g, wup_v, cw_g, cw_v, w_down_p, seq_len)
    loss = lax.psum(0.5 * jnp.sum(loss_part) / D_MODEL, ("x", "y", "c"))

    dug, duv, act, dhn2, dcg, dcv = _ffn_bwd(dy, ug, uv, wup_g, wup_v, cw_g, cw_v, w_down_p, seq_len)
    dh1, dys, dya, d_gs, d_ga, d_gf = _mix_bwd(dy, dhn2[None], h1, ys, ya, norm_out_ssm, norm_out_attn, w_out_f, norm_ffn)
    dyc, gl_b, dz_b, d_bglu = _glu_bwd(yc, dys, w_glu_f, b_glu)

    gw_glu = _tn_matmul(gl_b, dz_b, "dw_glu", D_SSM, D_SSM)
    gw_out = _tn_matmul(mixed, dh1, "dw_out", D_MODEL, D_MODEL)
    gw_up = jnp.concatenate([_tn_matmul(hn2, dug, "dw_up_gate", D_MODEL, D_FF_PAD // 2)[:, :D_FF],
                             _tn_matmul(hn2, duv, "dw_up_val", D_MODEL, D_FF_PAD // 2)[:, :D_FF]], axis=1)
    gw_down = _tn_matmul(act, dy, "dw_down", D_FF_PAD // 2, D_MODEL)[:D_FF]
    dcg2 = jnp.swapaxes(dcg, 0, 1).reshape(SUBLANES, D_FF_PAD)[:, :D_FF]
    dcv2 = jnp.swapaxes(dcv, 0, 1).reshape(SUBLANES, D_FF_PAD)[:, :D_FF]
    g_conv = jnp.concatenate([dcg2, dcv2], axis=1)
    by_cols = lambda g, c: jnp.swapaxes(g.reshape(g.shape[0], N_DEV, c), 0, 1)
    early_flags = [True] * 5
    g_sems = _exchange_start(
        [gw_glu.reshape(N_DEV, -1, D_SSM), gw_out.reshape(N_DEV, -1, D_MODEL), by_cols(gw_up, 2 * D_FF // N_DEV),
         by_cols(g_conv[:3], 2 * D_FF // N_DEV), gw_down.reshape(N_DEV, -1, D_MODEL)],
        early_flags, "grad_early_start", 1)
    started = g_sems[4][0, 0]

    du, dbbr, dbbi, dcr, dci, dar, dai, ddk = _s5_bwd(u, dyc, st_r, st_i, a_re, a_im, bbr, bbi, cr, ci,
                                                      d_skip + started, n_seq)
    dqn, dkn, dv, dcq, dck = _attn_bwd(qkv, cq, ck, ya, dya, lse + started, n_seq)
    dcum8 = dcq[:, :, :, 0] + dck[:, :, 0, :]
    dcum = _pad_to(jnp.swapaxes(dcum8, 1, 2).reshape(n, N_HEADS), 1, LANES)
    dfl, dbf = _fprep_bwd(dcum, fl, bf, n_seq)
    dx, dproj, d_gmix, d_qg, d_kg = _inproj_bwd(xf, norm_mix, w_in_p, avg, qg, kg, raw, du, dqn, dkn, dv, dfl, dh1)

    gw_in = _tn_matmul(hn, dproj, "dw_in", D_MODEL, D_IN_PAD)[:, :D_IN]
    partial_small = {
        "norm_mix": d_gmix, "b_forget": jnp.sum(dbf, axis=(0, 1))[:N_HEADS],
        "ab_re": jnp.sum(dar, axis=1), "ab_im": jnp.sum(dai, axis=1),
        "bb_re": _block_diag_extract(dbbr, False), "bb_im": _block_diag_extract(dbbi, False),
        "c_re": _block_diag_extract(dcr, True), "c_im": _block_diag_extract(dci, True),
        "d_skip": jnp.sum(ddk, axis=1), "b_glu": d_bglu,
        "q_norm": jnp.sum(d_qg.reshape(N_HEADS, HEAD_DIM), axis=0),
        "k_norm": jnp.sum(d_kg.reshape(N_HEADS, HEAD_DIM), axis=0),
        "norm_out_ssm": d_gs, "norm_out_attn": d_ga, "norm_ffn": d_gf, "conv_b": g_conv[3],
    }
    small_keys = tuple(partial_small)
    small_shapes = [partial_small[k].shape for k in small_keys]

    land_in, small_parts = _exchange(
        [by_cols(gw_in, D_IN // N_DEV), _pack([partial_small[k] for k in small_keys])], [True, False], "grad_late_exchange")
    land_glu, land_out, land_up, land_cw, land_down = _exchange_wait(
        g_sems[0], g_sems[1], g_sems[2], g_sems[3], early_flags, land_in, "grad_early_wait")
    grads, deltas, new_m, new_v = {}, {}, {}, {}
    for name, land in zip(SHARDED, (land_in, land_glu, land_out, land_up, land_cw, land_down)):
        g, d, nm, nv = _adam_sharded(land, weights[name][0], mom1[name][0], mom2[name][0], "adam_" + name, ADAM_ROWS[name])
        grads[name], deltas[name], new_m[name], new_v[name] = g[None], d[None], nm[None], nv[None]

    summed = dict(zip(small_keys, _unpack(_sum_partials(small_parts, "sum_small_grads"), small_shapes)))
    dlr, dli, dldt, dbr_t, dbi_t = _s5_param_bwd(
        lr3, li3, ldt3, br_t, bi_t, summed["ab_re"].reshape(lr3.shape), summed["ab_im"].reshape(lr3.shape),
        summed["bb_re"], summed["bb_im"])
    small_grads = {
        "norm_mix": summed["norm_mix"], "b_forget": summed["b_forget"], "lam_re": dlr, "lam_im": dli,
        "b_re": jnp.swapaxes(dbr_t, 1, 2), "b_im": jnp.swapaxes(dbi_t, 1, 2), "c_re": summed["c_re"], "c_im": summed["c_im"],
        "d_skip": summed["d_skip"], "log_dt": dldt, "b_glu": summed["b_glu"], "q_norm": summed["q_norm"],
        "k_norm": summed["k_norm"], "norm_out_ssm": summed["norm_out_ssm"], "norm_out_attn": summed["norm_out_attn"],
        "norm_ffn": summed["norm_ffn"], "conv_b": summed["conv_b"],
    }
    repl = tuple(k for k in WEIGHT_NAMES if k not in SHARDED)
    shapes = [weights[k].shape for k in repl]
    gp = _pack([small_grads[k] for k in repl])
    dp, mp, vp = _adam_packed(gp, _pack([weights[k] for k in repl]), _pack([mom1[k] for k in repl]),
                              _pack([mom2[k] for k in repl]), "adam_replicated")
    for k, g, d, nm, nv in zip(repl, _unpack(gp, shapes), _unpack(dp, shapes), _unpack(mp, shapes), _unpack(vp, shapes)):
        grads[k], deltas[k], new_m[k], new_v[k] = g, d, nm, nv

    grad_x = dx.reshape(x.shape)
    return (loss, grad_x, *[grads[k] for k in WEIGHT_NAMES], *[deltas[k] for k in WEIGHT_NAMES],
            *[new_m[k] for k in WEIGHT_NAMES], *[new_v[k] for k in WEIGHT_NAMES])
```

```python
import functools
import math

import jax
import jax.numpy as jnp
from jax import lax
from jax.experimental import pallas as pl
from jax.experimental.pallas import tpu as pltpu

F32 = jnp.float32
BF16 = jnp.bfloat16
HIGHEST = lax.Precision.HIGHEST

N_DEV = 8
D_MODEL = 1024
D_SSM = 512
D_ATTN = 512
N_HEADS = 8
HEAD_DIM = 64
N_GROUPS = 32
SSM_GROUP = 16
SSM_STATE = 64
D_FF = 2752
D_FF_PAD = 2816
D_IN = 2056
D_IN_PAD = 2176
EPS = 1e-6
LANES = 128
SUBLANES = 8
VMEM_LIMIT = 56 * 1024 * 1024

ADAM_LR = 0.001
ADAM_B1 = 0.9
ADAM_B2 = 0.999
ADAM_EPS = 1e-08
ADAM_WD = 0.01
ADAM_STEP = 10


def _cparams(*sem):
    return pltpu.CompilerParams(dimension_semantics=sem, vmem_limit_bytes=VMEM_LIMIT)


def _dot(a, b, **kw):
    return jnp.dot(a, b, preferred_element_type=F32, **kw)


def _dot_nt(a, b):
    return lax.dot_general(a, b, (((1,), (1,)), ((), ())), preferred_element_type=F32)


def _dot_tn(a, b):
    return lax.dot_general(a, b, (((0,), (0,)), ((), ())), preferred_element_type=F32)


def _rms(x, g):
    return x * lax.rsqrt(jnp.mean(x * x, axis=-1, keepdims=True) + EPS) * g


def _headnorm(q, avg, g):
    ms = jnp.dot(q * q, avg, preferred_element_type=F32, precision=HIGHEST)
    return q * lax.rsqrt(ms + EPS) * g


def _exchange(srcs, scatter_flags, name):
    n = len(srcs)
    out_shape = []
    for s, sc in zip(srcs, scatter_flags):
        shp = s.shape if sc else (N_DEV,) + s.shape
        out_shape.append(jax.ShapeDtypeStruct(shp, s.dtype))

    def body(*refs):
        src = refs[:n]
        dst = refs[n:2 * n]
        send_sems, recv_sems, loc_sems = refs[2 * n:]
        x, y, c = lax.axis_index("x"), lax.axis_index("y"), lax.axis_index("c")
        me = 4 * x + 2 * y + c
        peers = []
        for j in range(1, N_DEV):
            px = 1 - x if (j >> 2) & 1 else x
            py = 1 - y if (j >> 1) & 1 else y
            pc = 1 - c if j & 1 else c
            peers.append(((px, py, pc), 4 * px + 2 * py + pc))
        local, sends = [], []
        for k in range(n):
            own = src[k].at[me] if scatter_flags[k] else src[k]
            lc = pltpu.make_async_copy(own, dst[k].at[me], loc_sems.at[k])
            lc.start()
            local.append(lc)
            for j, (pid, pidx) in enumerate(peers):
                s = src[k].at[pidx] if scatter_flags[k] else src[k]
                cp = pltpu.make_async_remote_copy(
                    src_ref=s, dst_ref=dst[k].at[me], send_sem=send_sems.at[k, j], recv_sem=recv_sems.at[k, j],
                    device_id=pid, device_id_type=pl.DeviceIdType.MESH)
                cp.start()
                sends.append(cp)
        for k in range(n):
            for j, (pid, pidx) in enumerate(peers):
                s = src[k].at[pidx] if scatter_flags[k] else src[k]
                pltpu.make_async_remote_copy(
                    src_ref=s, dst_ref=dst[k].at[pidx], send_sem=send_sems.at[k, j], recv_sem=recv_sems.at[k, j],
                    device_id=pid, device_id_type=pl.DeviceIdType.MESH).wait_recv()
        for cp in sends:
            cp.wait_send()
        for lc in local:
            lc.wait()

    any_spec = pl.BlockSpec(memory_space=pl.ANY)
    return pl.pallas_call(
        body, name=name, out_shape=tuple(out_shape),
        in_specs=[any_spec] * n, out_specs=tuple([any_spec] * n),
        scratch_shapes=[pltpu.SemaphoreType.DMA((n, N_DEV - 1)), pltpu.SemaphoreType.DMA((n, N_DEV - 1)),
                        pltpu.SemaphoreType.DMA((n,))],
        compiler_params=pltpu.CompilerParams(has_side_effects=True),
    )(*srcs)


def _peer_list():
    x, y, c = lax.axis_index("x"), lax.axis_index("y"), lax.axis_index("c")
    peers = []
    for j in range(1, N_DEV):
        px = 1 - x if (j >> 2) & 1 else x
        py = 1 - y if (j >> 1) & 1 else y
        pc = 1 - c if j & 1 else c
        peers.append(((px, py, pc), 4 * px + 2 * py + pc))
    return 4 * x + 2 * y + c, peers


def _split_copies(src, land, send_sems, recv_sems, scatter_flags, me, peers, incoming):
    copies = []
    for k in range(len(src)):
        for j, (pid, pidx) in enumerate(peers):
            s = src[k].at[pidx] if scatter_flags[k] else src[k]
            i = k * (N_DEV - 1) + j
            copies.append(pltpu.make_async_remote_copy(
                src_ref=s, dst_ref=land[k].at[pidx if incoming else me], send_sem=send_sems[i],
                recv_sem=recv_sems[i], device_id=pid, device_id_type=pl.DeviceIdType.MESH))
    return copies


def _exchange_start(srcs, scatter_flags, after, name, collective_id):
    n = len(srcs)
    ns = n * (N_DEV - 1)
    hbm = pl.BlockSpec(memory_space=pltpu.HBM)
    sem = pl.BlockSpec(memory_space=pltpu.SEMAPHORE)
    land_shapes = [s.shape if sc else (N_DEV,) + s.shape for s, sc in zip(srcs, scatter_flags)]

    def body(*refs):
        src, land = refs[:n], refs[n:2 * n]
        send_sems = refs[2 * n + 1:2 * n + 1 + ns]
        recv_sems = refs[2 * n + 1 + ns:2 * n + 1 + 2 * ns]
        token = refs[4 * n + 1 + 2 * ns]
        local_sems = refs[4 * n + 2 + 2 * ns]
        me, peers = _peer_list()
        barrier = pltpu.get_barrier_semaphore()
        for pid, _ in peers:
            pl.semaphore_signal(barrier, inc=1, device_id=pid, device_id_type=pl.DeviceIdType.MESH)
        pl.semaphore_wait(barrier, N_DEV - 1)
        local = []
        for k in range(n):
            own = src[k].at[me] if scatter_flags[k] else src[k]
            local.append(pltpu.make_async_copy(own, land[k].at[me], local_sems.at[k]))
            local[-1].start()
        for lc in local:
            lc.wait()
        for cp in _split_copies(src, land, send_sems, recv_sems, scatter_flags, me, peers, False):
            cp.start()
        token[...] = jnp.zeros_like(token)

    outs = pl.pallas_call(
        body, name=name,
        out_shape=(*[pltpu.SemaphoreType.DMA(())] * (2 * ns), *[pltpu.HBM(s.shape, s.dtype) for s in srcs],
                   *[pltpu.HBM(shp, s.dtype) for shp, s in zip(land_shapes, srcs)],
                   jax.ShapeDtypeStruct((SUBLANES, LANES), F32)),
        in_specs=[hbm] * (2 * n) + [pl.BlockSpec(memory_space=pl.ANY)],
        out_specs=(*[sem] * (2 * ns), *[hbm] * (2 * n), pl.BlockSpec(memory_space=pltpu.VMEM)),
        input_output_aliases={i: 2 * ns + i for i in range(2 * n)},
        scratch_shapes=[pltpu.SemaphoreType.DMA((n,))],
        compiler_params=pltpu.CompilerParams(has_side_effects=pltpu.SideEffectType.DATAFLOW_SIDE_EFFECTING,
                                             collective_id=collective_id),
    )(*[pltpu.with_memory_space_constraint(s, pltpu.HBM) for s in srcs],
      *[pltpu.with_memory_space_constraint(lax.empty(shp, s.dtype), pltpu.HBM) for shp, s in zip(land_shapes, srcs)],
      after)
    return (outs[:ns], outs[ns:2 * ns], outs[2 * ns:2 * ns + n], outs[2 * ns + n:2 * ns + 2 * n], outs[2 * ns + 2 * n])


def _exchange_wait(send_sems, recv_sems, srcs, lands, scatter_flags, after, name):
    n = len(srcs)
    ns = n * (N_DEV - 1)
    hbm = pl.BlockSpec(memory_space=pltpu.HBM)
    sem = pl.BlockSpec(memory_space=pltpu.SEMAPHORE)

    def body(*refs):
        src, land = refs[:n], refs[n:2 * n]
        s_sems = refs[2 * n:2 * n + ns]
        r_sems = refs[2 * n + ns:2 * n + 2 * ns]
        me, peers = _peer_list()
        for cp in _split_copies(src, land, s_sems, r_sems, scatter_flags, me, peers, True):
            cp.wait_send()
            cp.wait_recv()

    outs = pl.pallas_call(
        body, name=name,
        out_shape=tuple(pltpu.HBM(a.shape, a.dtype) for a in (*srcs, *lands)),
        in_specs=[hbm] * (2 * n) + [sem] * (2 * ns) + [pl.BlockSpec(memory_space=pl.ANY)],
        out_specs=tuple([hbm] * (2 * n)),
        input_output_aliases={i: i for i in range(2 * n)},
        compiler_params=pltpu.CompilerParams(has_side_effects=pltpu.SideEffectType.DATAFLOW_SIDE_EFFECTING),
    )(*srcs, *lands, *send_sems, *recv_sems, after)
    return outs[n:]


def _tn_matmul(a, b, name, tk, tm, tn=512):
    n_tok, k_dim = a.shape
    m_dim = b.shape[1]
    grid = (k_dim // tk, m_dim // tm, n_tok // tn)

    def body(a_ref, b_ref, o_ref):
        @pl.when(pl.program_id(2) == 0)
        def _():
            o_ref[...] = jnp.zeros_like(o_ref)
        o_ref[...] += _dot_tn(a_ref[...].astype(BF16), b_ref[...].astype(BF16))

    return pl.pallas_call(
        body, name=name, grid=grid,
        in_specs=[pl.BlockSpec((tn, tk), lambda i, j, k: (k, i)), pl.BlockSpec((tn, tm), lambda i, j, k: (k, j))],
        out_specs=pl.BlockSpec((tk, tm), lambda i, j, k: (i, j)),
        out_shape=jax.ShapeDtypeStruct((k_dim, m_dim), F32),
        compiler_params=_cparams("parallel", "parallel", "arbitrary"),
    )(a, b)


def _adam_math(g, w, m, v):
    m = ADAM_B1 * m + (1.0 - ADAM_B1) * g
    v = ADAM_B2 * v + (1.0 - ADAM_B2) * (g * g)
    m_hat = m / (1.0 - ADAM_B1 ** ADAM_STEP)
    v_hat = v / (1.0 - ADAM_B2 ** ADAM_STEP)
    delta = -ADAM_LR * (m_hat / (jnp.sqrt(v_hat) + ADAM_EPS) + ADAM_WD * w)
    return delta, m, v


def _adam_sharded(land, w, m, v, name, tr):
    r, c = w.shape

    def body(l_ref, w_ref, m_ref, v_ref, g_ref, d_ref, nm_ref, nv_ref):
        g = l_ref[0]
        for s in range(1, N_DEV):
            g = g + l_ref[s]
        d, nm, nv = _adam_math(g, w_ref[...], m_ref[...], v_ref[...])
        g_ref[...] = g
        d_ref[...] = d
        nm_ref[...] = nm
        nv_ref[...] = nv

    spec = pl.BlockSpec((tr, c), lambda i: (i, 0))
    return pl.pallas_call(
        body, name=name, grid=(r // tr,),
        in_specs=[pl.BlockSpec((N_DEV, tr, c), lambda i: (0, i, 0)), spec, spec, spec],
        out_specs=(spec, spec, spec, spec),
        out_shape=tuple(jax.ShapeDtypeStruct((r, c), F32) for _ in range(4)),
        compiler_params=_cparams("parallel"),
    )(land, w, m, v)


def _sum_partials(parts, name):
    _, r, c = parts.shape

    def body(p_ref, o_ref):
        g = p_ref[0]
        for s in range(1, N_DEV):
            g = g + p_ref[s]
        o_ref[...] = g

    return pl.pallas_call(body, name=name, out_shape=jax.ShapeDtypeStruct((r, c), F32),
                          compiler_params=pltpu.CompilerParams(vmem_limit_bytes=VMEM_LIMIT))(parts)


def _adam_replicated(gs, ws, ms, vs, name):
    k = len(ws)

    def body(*refs):
        outs = refs[4 * k:]
        for i in range(k):
            d, nm, nv = _adam_math(refs[i][...], refs[k + i][...], refs[2 * k + i][...], refs[3 * k + i][...])
            outs[i][...] = d
            outs[k + i][...] = nm
            outs[2 * k + i][...] = nv

    outs = pl.pallas_call(body, name=name, out_shape=tuple(jax.ShapeDtypeStruct(w.shape, F32) for w in ws) * 3,
                          compiler_params=pltpu.CompilerParams(vmem_limit_bytes=VMEM_LIMIT))(*gs, *ws, *ms, *vs)
    return outs[:k], outs[k:2 * k], outs[2 * k:]


def _inproj_fwd(x, g, w_in, avg, qg, kg, tm=512):
    n = x.shape[0]

    def body(x_ref, g_ref, w_ref, a_ref, qg_ref, kg_ref, hn_ref, u_ref, qkv_ref, raw_ref, fl_ref):
        hn = _rms(x_ref[...], g_ref[...]).astype(BF16)
        hn_ref[...] = hn
        proj = _dot(hn, w_ref[...])
        u_ref[...] = proj[:, 0:512]
        q = proj[:, 512:1024]
        k = proj[:, 1024:1536]
        raw_ref[:, 0:512] = q
        raw_ref[:, 512:1024] = k
        qkv_ref[:, 0:512] = _headnorm(q, a_ref[...], qg_ref[...])
        qkv_ref[:, 512:1024] = _headnorm(k, a_ref[...], kg_ref[...])
        qkv_ref[:, 1024:1536] = proj[:, 1536:2048]
        fl_ref[...] = proj[:, 2048:D_IN_PAD]

    row = lambda w: pl.BlockSpec((tm, w), lambda i: (i, 0))
    full = lambda a: pl.BlockSpec(a.shape, lambda i: (0,) * a.ndim)
    return pl.pallas_call(
        body, name="inproj_fwd", grid=(n // tm,),
        in_specs=[row(D_MODEL), full(g), full(w_in), full(avg), full(qg), full(kg)],
        out_specs=(row(D_MODEL), row(512), row(1536), row(1024), row(LANES)),
        out_shape=(jax.ShapeDtypeStruct((n, D_MODEL), BF16), jax.ShapeDtypeStruct((n, 512), F32),
                   jax.ShapeDtypeStruct((n, 1536), F32), jax.ShapeDtypeStruct((n, 1024), F32),
                   jax.ShapeDtypeStruct((n, LANES), F32)),
        compiler_params=_cparams("parallel"),
    )(x, g, w_in, avg, qg, kg)


def _inproj_bwd(x, g, w_in, avg, qg, kg, raw, du, dqn, dkn, dv, dfl, dres, tm=512):
    n = x.shape[0]

    def body(x_ref, g_ref, w_ref, a_ref, qg_ref, kg_ref, raw_ref, du_ref, dqn_ref, dkn_ref, dv_ref, dfl_ref, dres_ref,
             dx_ref, dproj_ref, dg_ref, dqg_ref, dkg_ref):
        @pl.when(pl.program_id(0) == 0)
        def _():
            dg_ref[...] = jnp.zeros_like(dg_ref)
            dqg_ref[...] = jnp.zeros_like(dqg_ref)
            dkg_ref[...] = jnp.zeros_like(dkg_ref)
        avg_m = a_ref[...]
        _, vjp_q = jax.vjp(lambda q, gg: _headnorm(q, avg_m, gg), raw_ref[:, 0:512], qg_ref[...])
        dq, dqg = vjp_q(dqn_ref[...])
        _, vjp_k = jax.vjp(lambda k, gg: _headnorm(k, avg_m, gg), raw_ref[:, 512:1024], kg_ref[...])
        dk, dkg = vjp_k(dkn_ref[...])
        dproj = jnp.concatenate([du_ref[...], dq, dk, dv_ref[...], dfl_ref[...]], axis=1).astype(BF16)
        dproj_ref[...] = dproj
        dhn = _dot_nt(dproj, w_ref[...])
        _, vjp_x = jax.vjp(_rms, x_ref[...], g_ref[...])
        dxn, dg = vjp_x(dhn)
        dx_ref[...] = dxn + dres_ref[...]
        dg_ref[...] += dg
        dqg_ref[...] += dqg
        dkg_ref[...] += dkg

    row = lambda w: pl.BlockSpec((tm, w), lambda i: (i, 0))
    full = lambda a: pl.BlockSpec(a.shape, lambda i: (0,) * a.ndim)
    vec = lambda w: pl.BlockSpec((1, w), lambda i: (0, 0))
    return pl.pallas_call(
        body, name="inproj_bwd", grid=(n // tm,),
        in_specs=[row(D_MODEL), full(g), full(w_in), full(avg), full(qg), full(kg), row(1024), row(512), row(512),
                  row(512), row(512), row(LANES), row(D_MODEL)],
        out_specs=(row(D_MODEL), row(D_IN_PAD), vec(D_MODEL), vec(512), vec(512)),
        out_shape=(jax.ShapeDtypeStruct((n, D_MODEL), F32), jax.ShapeDtypeStruct((n, D_IN_PAD), BF16),
                   jax.ShapeDtypeStruct((1, D_MODEL), F32), jax.ShapeDtypeStruct((1, 512), F32),
                   jax.ShapeDtypeStruct((1, 512), F32)),
        compiler_params=_cparams("arbitrary"),
    )(x, g, w_in, avg, qg, kg, raw, du, dqn, dkn, dv, dfl, dres)


def _glu_fwd(yc, wg, bg, tm=512):
    n = yc.shape[0]

    def body(yc_ref, w_ref, b_ref, ys_ref):
        gl = jax.nn.gelu(yc_ref[...])
        z = _dot(gl.astype(BF16), w_ref[...]) + b_ref[...]
        ys_ref[...] = gl * jax.nn.sigmoid(z)

    row = pl.BlockSpec((tm, 512), lambda i: (i, 0))
    full = lambda a: pl.BlockSpec(a.shape, lambda i: (0,) * a.ndim)
    return pl.pallas_call(
        body, name="glu_fwd", grid=(n // tm,), in_specs=[row, full(wg), full(bg)], out_specs=row,
        out_shape=jax.ShapeDtypeStruct((n, 512), F32), compiler_params=_cparams("parallel"),
    )(yc, wg, bg)


def _glu_bwd(yc, dys, wg, bg, tm=512):
    n = yc.shape[0]

    def body(yc_ref, dys_ref, w_ref, b_ref, dyc_ref, gl_ref, dz_ref, db_ref):
        @pl.when(pl.program_id(0) == 0)
        def _():
            db_ref[...] = jnp.zeros_like(db_ref)
        gl, vjp_gelu = jax.vjp(jax.nn.gelu, yc_ref[...])
        glb = gl.astype(BF16)
        z = _dot(glb, w_ref[...]) + b_ref[...]
        s = jax.nn.sigmoid(z)
        dys = dys_ref[...]
        dz = dys * gl * s * (1.0 - s)
        dzb = dz.astype(BF16)
        dgl = dys * s + _dot_nt(dzb, w_ref[...])
        dyc_ref[...] = vjp_gelu(dgl)[0]
        gl_ref[...] = glb
        dz_ref[...] = dzb
        db_ref[...] += jnp.sum(dz, axis=0, keepdims=True)

    row = pl.BlockSpec((tm, 512), lambda i: (i, 0))
    full = lambda a: pl.BlockSpec(a.shape, lambda i: (0,) * a.ndim)
    return pl.pallas_call(
        body, name="glu_bwd", grid=(n // tm,), in_specs=[row, row, full(wg), full(bg)],
        out_specs=(row, row, row, pl.BlockSpec((1, 512), lambda i: (0, 0))),
        out_shape=(jax.ShapeDtypeStruct((n, 512), F32), jax.ShapeDtypeStruct((n, 512), BF16),
                   jax.ShapeDtypeStruct((n, 512), BF16), jax.ShapeDtypeStruct((1, 512), F32)),
        compiler_params=_cparams("arbitrary"),
    )(yc, dys, wg, bg)


def _mix_fwd(x, ys, ya, gs, ga, wout, gf, tm=512):
    n = x.shape[0]

    def body(x_ref, ys_ref, ya_ref, gs_ref, ga_ref, w_ref, gf_ref, h1_ref, hn2_ref, mixed_ref):
        mixed = jnp.concatenate([_rms(ys_ref[...], gs_ref[...]), _rms(ya_ref[...], ga_ref[...])], axis=1).astype(BF16)
        mixed_ref[...] = mixed
        h1 = x_ref[...] + _dot(mixed, w_ref[...])
        h1_ref[...] = h1
        hn2_ref[...] = _rms(h1, gf_ref[...]).astype(BF16)

    row = lambda w: pl.BlockSpec((tm, w), lambda i: (i, 0))
    full = lambda a: pl.BlockSpec(a.shape, lambda i: (0,) * a.ndim)
    return pl.pallas_call(
        body, name="mix_fwd", grid=(n // tm,),
        in_specs=[row(D_MODEL), row(512), row(512), full(gs), full(ga), full(wout), full(gf)],
        out_specs=(row(D_MODEL), row(D_MODEL), row(D_MODEL)),
        out_shape=(jax.ShapeDtypeStruct((n, D_MODEL), F32), jax.ShapeDtypeStruct((n, D_MODEL), BF16),
                   jax.ShapeDtypeStruct((n, D_MODEL), BF16)),
        compiler_params=_cparams("parallel"),
    )(x, ys, ya, gs, ga, wout, gf)


def _mix_bwd(dy, dhn2_parts, h1, ys, ya, gs, ga, wout, gf, tm=512):
    n = dy.shape[0]
    n_parts = dhn2_parts.shape[0]

    def body(dy_ref, dp_ref, h1_ref, ys_ref, ya_ref, gs_ref, ga_ref, w_ref, gf_ref,
             dh1_ref, dys_ref, dya_ref, dgs_ref, dga_ref, dgf_ref):
        @pl.when(pl.program_id(0) == 0)
        def _():
            dgs_ref[...] = jnp.zeros_like(dgs_ref)
            dga_ref[...] = jnp.zeros_like(dga_ref)
            dgf_ref[...] = jnp.zeros_like(dgf_ref)
        dhn2 = dp_ref[0]
        for p in range(1, n_parts):
            dhn2 = dhn2 + dp_ref[p]
        _, vjp_f = jax.vjp(_rms, h1_ref[...], gf_ref[...])
        dh1n, dgf = vjp_f(dhn2)
        dh1 = dy_ref[...] + dh1n
        dh1_ref[...] = dh1
        dmixed = _dot_nt(dh1.astype(BF16), w_ref[...])
        _, vjp_s = jax.vjp(_rms, ys_ref[...], gs_ref[...])
        dys, dgs = vjp_s(dmixed[:, 0:512])
        _, vjp_a = jax.vjp(_rms, ya_ref[...], ga_ref[...])
        dya, dga = vjp_a(dmixed[:, 512:1024])
        dys_ref[...] = dys
        dya_ref[...] = dya
        dgs_ref[...] += dgs
        dga_ref[...] += dga
        dgf_ref[...] += dgf

    row = lambda w: pl.BlockSpec((tm, w), lambda i: (i, 0))
    full = lambda a: pl.BlockSpec(a.shape, lambda i: (0,) * a.ndim)
    vec = lambda w: pl.BlockSpec((1, w), lambda i: (0, 0))
    return pl.pallas_call(
        body, name="mix_bwd", grid=(n // tm,),
        in_specs=[row(D_MODEL), pl.BlockSpec((n_parts, tm, D_MODEL), lambda i: (0, i, 0)), row(D_MODEL), row(512),
                  row(512), full(gs), full(ga), full(wout), full(gf)],
        out_specs=(row(D_MODEL), row(512), row(512), vec(512), vec(512), vec(D_MODEL)),
        out_shape=(jax.ShapeDtypeStruct((n, D_MODEL), F32), jax.ShapeDtypeStruct((n, 512), F32),
                   jax.ShapeDtypeStruct((n, 512), F32), jax.ShapeDtypeStruct((1, 512), F32),
                   jax.ShapeDtypeStruct((1, 512), F32), jax.ShapeDtypeStruct((1, D_MODEL), F32)),
        compiler_params=_cparams("arbitrary"),
    )(dy, dhn2_parts, h1, ys, ya, gs, ga, wout, gf)


HALO = 16


def _conv3(ue, cw):
    return cw[2:3] * ue + cw[1:2] * pltpu.roll(ue, 1, 0) + cw[0:1] * pltpu.roll(ue, 2, 0) + cw[3:4]


def _ffn_fwd(hn2, h1, target, wup_g, wup_v, cw_g, cw_v, wdown, seq_len, tm=512, fb=256):
    n = hn2.shape[0]
    nj = D_FF_PAD // fb
    hb = tm // HALO

    def body(hn_ref, halo_ref, h1_ref, tgt_ref, wg_ref, wv_ref, cg_ref, cv_ref, wd_ref,
             ug_ref, uv_ref, dy_ref, loss_ref, acc):
        i, j = pl.program_id(0), pl.program_id(1)
        seq_start = (i * tm) % seq_len == 0
        halo = halo_ref[...]
        halo = jnp.where(seq_start, jnp.zeros_like(halo), halo)
        he = jnp.concatenate([halo, hn_ref[...]], axis=0)
        ueg = _dot(he, wg_ref[...])
        uev = _dot(he, wv_ref[...])
        ug_ref[...] = ueg[HALO:]
        uv_ref[...] = uev[HALO:]
        cg = _conv3(ueg, cg_ref[...])[HALO:]
        cv = _conv3(uev, cv_ref[...])[HALO:]
        act = (jax.nn.silu(cg) * cv).astype(BF16)
        part = _dot(act, wd_ref[...])

        @pl.when(j == 0)
        def _():
            acc[...] = part

        @pl.when(j > 0)
        def _():
            acc[...] += part

        @pl.when(j == nj - 1)
        def _():
            err = h1_ref[...] + acc[...] - tgt_ref[...]
            dy_ref[...] = err * (1.0 / D_MODEL)
            loss_ref[0] = jnp.sum(err * err, axis=0, keepdims=True)

    row = pl.BlockSpec((tm, D_MODEL), lambda i, j: (i, 0))
    return pl.pallas_call(
        body, name="ffn_fwd", grid=(n // tm, nj),
        in_specs=[row, pl.BlockSpec((HALO, D_MODEL), lambda i, j: (jnp.maximum(i * hb - 1, 0), 0)), row, row,
                  pl.BlockSpec((D_MODEL, fb), lambda i, j: (0, j)), pl.BlockSpec((D_MODEL, fb), lambda i, j: (0, j)),
                  pl.BlockSpec((8, fb), lambda i, j: (0, j)), pl.BlockSpec((8, fb), lambda i, j: (0, j)),
                  pl.BlockSpec((fb, D_MODEL), lambda i, j: (j, 0))],
        out_specs=(pl.BlockSpec((tm, fb), lambda i, j: (i, j)), pl.BlockSpec((tm, fb), lambda i, j: (i, j)), row,
                   pl.BlockSpec((1, 1, D_MODEL), lambda i, j: (i, 0, 0))),
        out_shape=(jax.ShapeDtypeStruct((n, D_FF_PAD), F32), jax.ShapeDtypeStruct((n, D_FF_PAD), F32),
                   jax.ShapeDtypeStruct((n, D_MODEL), F32), jax.ShapeDtypeStruct((n // tm, 1, D_MODEL), F32)),
        scratch_shapes=[pltpu.VMEM((tm, D_MODEL), F32)],
        compiler_params=_cparams("parallel", "arbitrary"),
    )(hn2, hn2, h1, target, wup_g, wup_v, cw_g, cw_v, wdown)


def _ffn_bwd(dy, ug, uv, wup_g, wup_v, cw_g, cw_v, wdown, seq_len, tm=512, fb=256):
    n = dy.shape[0]
    nj = D_FF_PAD // fb
    hb = tm // HALO
    last_hb = n // HALO - 1
    rows = tm + HALO

    def body(dy_ref, dyn_ref, ugp_ref, ugm_ref, ugn_ref, uvp_ref, uvm_ref, uvn_ref, wg_ref, wv_ref, cg_ref, cv_ref,
             wd_ref, dug_ref, duv_ref, act_ref, dhn_ref, dcg_ref, dcv_ref, acc):
        i, j = pl.program_id(0), pl.program_id(1)
        seq_start = (i * tm) % seq_len == 0
        seq_end = ((i + 1) * tm) % seq_len == 0
        dyn = dyn_ref[...]
        dyn = jnp.where(seq_end, jnp.zeros_like(dyn), dyn)
        d_out = jnp.concatenate([dy_ref[...], dyn], axis=0).astype(BF16)
        d_act = _dot_nt(d_out, wd_ref[...])

        def pre_act(up_ref, um_ref, un_ref, cw):
            up = up_ref[...]
            up = jnp.where(seq_start, jnp.zeros_like(up), up)
            ue = jnp.concatenate([up, um_ref[...], un_ref[...]], axis=0)
            return ue, _conv3(ue, cw)[HALO:]

        cwg, cwv = cg_ref[...], cv_ref[...]
        ueg, cge = pre_act(ugp_ref, ugm_ref, ugn_ref, cwg)
        uev, cve = pre_act(uvp_ref, uvm_ref, uvn_ref, cwv)
        act, vjp_act = jax.vjp(lambda g, v: jax.nn.silu(g) * v, cge, cve)
        dcge, dcve = vjp_act(d_act)
        act_ref[...] = act[:tm].astype(BF16)

        def conv_t(dc, cw):
            return (cw[2:3] * dc + cw[1:2] * pltpu.roll(dc, rows - 1, 0) + cw[0:1] * pltpu.roll(dc, rows - 2, 0))[:tm]

        dug = conv_t(dcge, cwg).astype(BF16)
        duv = conv_t(dcve, cwv).astype(BF16)
        dug_ref[...] = dug
        duv_ref[...] = duv
        part = _dot_nt(dug, wg_ref[...]) + _dot_nt(duv, wv_ref[...])

        @pl.when(j == 0)
        def _():
            acc[...] = part

        @pl.when(j > 0)
        def _():
            acc[...] += part

        @pl.when(j == nj - 1)
        def _():
            dhn_ref[...] = acc[...]

        def cw_grad(dc, ue):
            dcm = dc[:tm]
            taps = [jnp.sum(dcm * pltpu.roll(ue, 2 - k, 0)[HALO:HALO + tm], axis=0, keepdims=True) for k in (0, 1)]
            taps.append(jnp.sum(dcm * ue[HALO:HALO + tm], axis=0, keepdims=True))
            taps.append(jnp.sum(dcm, axis=0, keepdims=True))
            return jnp.concatenate(taps + [jnp.zeros((4, fb), F32)], axis=0)

        @pl.when(i == 0)
        def _():
            dcg_ref[j] = jnp.zeros((8, fb), F32)
            dcv_ref[j] = jnp.zeros((8, fb), F32)

        dcg_ref[j] += cw_grad(dcge, ueg)
        dcv_ref[j] += cw_grad(dcve, uev)

    row = pl.BlockSpec((tm, D_MODEL), lambda i, j: (i, 0))
    u_prev = pl.BlockSpec((HALO, fb), lambda i, j: (jnp.maximum(i * hb - 1, 0), j))
    u_main = pl.BlockSpec((tm, fb), lambda i, j: (i, j))
    u_next = pl.BlockSpec((HALO, fb), lambda i, j: (jnp.minimum((i + 1) * hb, last_hb), j))
    w_col = pl.BlockSpec((D_MODEL, fb), lambda i, j: (0, j))
    c_col = pl.BlockSpec((8, fb), lambda i, j: (0, j))
    dc_spec = pl.BlockSpec((nj, 8, fb), lambda i, j: (0, 0, 0))
    return pl.pallas_call(
        body, name="ffn_bwd", grid=(n // tm, nj),
        in_specs=[row, pl.BlockSpec((HALO, D_MODEL), lambda i, j: (jnp.minimum((i + 1) * hb, last_hb), 0)),
                  u_prev, u_main, u_next, u_prev, u_main, u_next, w_col, w_col, c_col, c_col,
                  pl.BlockSpec((fb, D_MODEL), lambda i, j: (j, 0))],
        out_specs=(u_main, u_main, u_main, row, dc_spec, dc_spec),
        out_shape=(jax.ShapeDtypeStruct((n, D_FF_PAD), BF16), jax.ShapeDtypeStruct((n, D_FF_PAD), BF16),
                   jax.ShapeDtypeStruct((n, D_FF_PAD), BF16), jax.ShapeDtypeStruct((n, D_MODEL), F32),
                   jax.ShapeDtypeStruct((nj, 8, fb), F32), jax.ShapeDtypeStruct((nj, 8, fb), F32)),
        scratch_shapes=[pltpu.VMEM((tm, D_MODEL), F32)],
        compiler_params=_cparams("arbitrary", "arbitrary"),
    )(dy, dy, ug, ug, ug, uv, uv, uv, wup_g, wup_v, cw_g, cw_v, wdown)


def _s5_param_fn(lr, li, ldt, br, bi):
    dt = jnp.exp(ldt)
    mag = jnp.exp(lr * dt)
    ab_re = mag * jnp.cos(li * dt)
    ab_im = mag * jnp.sin(li * dt)
    nr = ab_re - 1.0
    ni = ab_im
    den = lr * lr + li * li
    q_re = (nr * lr + ni * li) / den
    q_im = (ni * lr - nr * li) / den
    bb_re = q_re * br - q_im * bi
    bb_im = q_re * bi + q_im * br
    return ab_re, ab_im, bb_re, bb_im


def _s5_param_fwd(lr, li, ldt, br, bi):
    def body(lr_ref, li_ref, ldt_ref, br_ref, bi_ref, ar_ref, ai_ref, bbr_ref, bbi_ref):
        ar, ai, bbr, bbi = _s5_param_fn(lr_ref[...], li_ref[...], ldt_ref[...], br_ref[...], bi_ref[...])
        ar_ref[...] = ar
        ai_ref[...] = ai
        bbr_ref[...] = bbr
        bbi_ref[...] = bbi

    return pl.pallas_call(
        body, name="s5_param_fwd",
        out_shape=(jax.ShapeDtypeStruct(lr.shape, F32), jax.ShapeDtypeStruct(lr.shape, F32),
                   jax.ShapeDtypeStruct(br.shape, F32), jax.ShapeDtypeStruct(br.shape, F32)),
    )(lr, li, ldt, br, bi)


def _s5_param_bwd(lr, li, ldt, br, bi, dar, dai, dbbr, dbbi):
    def body(lr_ref, li_ref, ldt_ref, br_ref, bi_ref, dar_ref, dai_ref, dbbr_ref, dbbi_ref,
             dlr_ref, dli_ref, dldt_ref, dbr_ref, dbi_ref):
        _, vjp = jax.vjp(_s5_param_fn, lr_ref[...], li_ref[...], ldt_ref[...], br_ref[...], bi_ref[...])
        dlr, dli, dldt, dbr, dbi = vjp((dar_ref[...], dai_ref[...], dbbr_ref[...], dbbi_ref[...]))
        dlr_ref[...] = dlr
        dli_ref[...] = dli
        dldt_ref[...] = dldt
        dbr_ref[...] = dbr
        dbi_ref[...] = dbi

    return pl.pallas_call(
        body, name="s5_param_bwd",
        out_shape=(jax.ShapeDtypeStruct(lr.shape, F32), jax.ShapeDtypeStruct(lr.shape, F32),
                   jax.ShapeDtypeStruct(ldt.shape, F32), jax.ShapeDtypeStruct(br.shape, F32),
                   jax.ShapeDtypeStruct(br.shape, F32)),
    )(lr, li, ldt, br, bi, dar, dai, dbbr, dbbi)


S5_CHUNK = 256
S5_STATES = 512
S5_BLOCKS = 4


def _cpow_rows(ar, ai, count):
    rs, im = [ar], [ai]
    for _ in range(count - 1):
        pr, pi = rs[-1], im[-1]
        rs.append(pr * ar - pi * ai)
        im.append(pr * ai + pi * ar)
    return rs, im


def _scan_in_groups(vr, vi, pr, pi, rm, reverse):
    n = vr.shape[0]
    for k in (1, 2, 4):
        if reverse:
            sr, si, keep = pltpu.roll(vr, n - k, 0), pltpu.roll(vi, n - k, 0), rm < SUBLANES - k
        else:
            sr, si, keep = pltpu.roll(vr, k, 0), pltpu.roll(vi, k, 0), rm >= k
        sr = jnp.where(keep, sr, 0.0)
        si = jnp.where(keep, si, 0.0)
        kr, ki = pr[k - 1], pi[k - 1]
        vr, vi = vr + kr * sr - ki * si, vi + kr * si + ki * sr
    return vr, vi


def _carry_over_groups(xr_s, xi_s, wr, wi, c0r, c0i, reverse):
    groups = xr_s.shape[0] // SUBLANES
    pick = 0 if reverse else SUBLANES - 1

    def step(q, carry):
        cr, ci = carry
        r = groups - 1 - q if reverse else q
        o = pl.multiple_of(r * SUBLANES, SUBLANES)
        vr = xr_s[pl.ds(o, SUBLANES), :]
        vi = xi_s[pl.ds(o, SUBLANES), :]
        nr = vr + wr * cr - wi * ci
        ni = vi + wr * ci + wi * cr
        xr_s[pl.ds(o, SUBLANES), :] = nr
        xi_s[pl.ds(o, SUBLANES), :] = ni
        return (jnp.broadcast_to(nr[pick:pick + 1], nr.shape), jnp.broadcast_to(ni[pick:pick + 1], ni.shape))

    return lax.fori_loop(0, groups, step, (c0r, c0i))


def _s5_state_scan(u_b, bbr, bbi, pr, pi, rm, xr_s, xi_s, c0r, c0i):
    bur = _dot(u_b, bbr)
    bui = _dot(u_b, bbi)
    bur, bui = _scan_in_groups(bur, bui, pr, pi, rm, False)
    xr_s[...] = bur
    xi_s[...] = bui
    w8r = jnp.concatenate(pr, axis=0)
    w8i = jnp.concatenate(pi, axis=0)
    return _carry_over_groups(xr_s, xi_s, w8r, w8i, c0r, c0i, False)


def _s5_fwd(u, a_re, a_im, bbr, bbi, cr, ci, d_skip, n_seq):
    n = u.shape[0]
    seq_len = n // n_seq
    nt = seq_len // S5_CHUNK
    tc = S5_CHUNK

    def body(u_ref, ar_ref, ai_ref, bbr_ref, bbi_ref, cr_ref, ci_ref, d_ref, y_ref, str_ref, sti_ref,
             xr_s, xi_s, car_r, car_i):
        t = pl.program_id(2)

        @pl.when(t == 0)
        def _():
            car_r[...] = jnp.zeros_like(car_r)
            car_i[...] = jnp.zeros_like(car_i)
        pr, pi = _cpow_rows(ar_ref[0], ai_ref[0], SUBLANES)
        rm = lax.broadcasted_iota(jnp.int32, (tc, S5_STATES), 0) & (SUBLANES - 1)
        str_ref[0, 0] = car_r[...]
        sti_ref[0, 0] = car_i[...]
        u_t = u_ref[...]
        cfr, cfi = _s5_state_scan(u_t.astype(BF16), bbr_ref[0], bbi_ref[0], pr, pi, rm, xr_s, xi_s,
                                  car_r[...], car_i[...])
        car_r[...] = cfr
        car_i[...] = cfi
        y = _dot(xr_s[...].astype(BF16), cr_ref[0]) - _dot(xi_s[...].astype(BF16), ci_ref[0])
        y_ref[...] = y + d_ref[...] * u_t

    u_spec = pl.BlockSpec((tc, LANES), lambda cb, b, t: (b * nt + t, cb))
    a_spec = pl.BlockSpec((1, 1, S5_STATES), lambda cb, b, t: (cb, 0, 0))
    bb_spec = pl.BlockSpec((1, LANES, S5_STATES), lambda cb, b, t: (cb, 0, 0))
    c_spec = pl.BlockSpec((1, S5_STATES, LANES), lambda cb, b, t: (cb, 0, 0))
    st_spec = pl.BlockSpec((1, 1, SUBLANES, S5_STATES), lambda cb, b, t: (cb, b * nt + t, 0, 0))
    st_shape = jax.ShapeDtypeStruct((S5_BLOCKS, n_seq * nt, SUBLANES, S5_STATES), F32)
    return pl.pallas_call(
        body, name="s5_fwd", grid=(S5_BLOCKS, n_seq, nt),
        in_specs=[u_spec, a_spec, a_spec, bb_spec, bb_spec, c_spec, c_spec,
                  pl.BlockSpec((1, LANES), lambda cb, b, t: (0, cb))],
        out_specs=(u_spec, st_spec, st_spec),
        out_shape=(jax.ShapeDtypeStruct((n, D_SSM), F32), st_shape, st_shape),
        scratch_shapes=[pltpu.VMEM((tc, S5_STATES), F32), pltpu.VMEM((tc, S5_STATES), F32),
                        pltpu.VMEM((SUBLANES, S5_STATES), F32), pltpu.VMEM((SUBLANES, S5_STATES), F32)],
        compiler_params=_cparams("parallel", "arbitrary", "arbitrary"),
    )(u, a_re, a_im, bbr, bbi, cr, ci, d_skip)


def _s5_bwd(u, dy, st_r, st_i, a_re, a_im, bbr, bbi, cr, ci, d_skip, n_seq):
    n = u.shape[0]
    seq_len = n // n_seq
    nt = seq_len // S5_CHUNK
    tc = S5_CHUNK

    def body(u_ref, dy_ref, str_ref, sti_ref, ar_ref, ai_ref, bbr_ref, bbi_ref, cr_ref, ci_ref, d_ref,
             du_ref, dbbr_ref, dbbi_ref, dcr_ref, dci_ref, dar_ref, dai_ref, dd_ref,
             xr_s, xi_s, gr_s, gi_s, car_r, car_i):
        b, t = pl.program_id(1), pl.program_id(2)

        @pl.when((b == 0) & (t == 0))
        def _():
            for ref in (dbbr_ref, dbbi_ref, dcr_ref, dci_ref, dar_ref, dai_ref, dd_ref):
                ref[...] = jnp.zeros_like(ref)

        @pl.when(t == 0)
        def _():
            car_r[...] = jnp.zeros_like(car_r)
            car_i[...] = jnp.zeros_like(car_i)
        ar, ai = ar_ref[0], ai_ref[0]
        pr, pi = _cpow_rows(ar, ai, SUBLANES)
        row = lax.broadcasted_iota(jnp.int32, (tc, S5_STATES), 0)
        rm = row & (SUBLANES - 1)
        u_t = u_ref[...]
        u_b = u_t.astype(BF16)
        dy_t = dy_ref[...]
        dy_b = dy_t.astype(BF16)
        s0r, s0i = str_ref[0, 0], sti_ref[0, 0]
        _s5_state_scan(u_b, bbr_ref[0], bbi_ref[0], pr, pi, rm, xr_s, xi_s, s0r, s0i)
        xr, xi = xr_s[...], xi_s[...]
        gr = _dot_nt(dy_b, cr_ref[0])
        gi = -_dot_nt(dy_b, ci_ref[0])
        npi = [-v for v in pi]
        gr, gi = _scan_in_groups(gr, gi, pr, npi, rm, True)
        gr_s[...] = gr
        gi_s[...] = gi
        w8r = jnp.concatenate(pr[::-1], axis=0)
        w8i = jnp.concatenate(npi[::-1], axis=0)
        cfr, cfi = _carry_over_groups(gr_s, gi_s, w8r, w8i, car_r[...], car_i[...], True)
        car_r[...] = cfr
        car_i[...] = cfi
        gr, gi = gr_s[...], gi_s[...]
        gr_b, gi_b = gr.astype(BF16), gi.astype(BF16)
        du_ref[...] = _dot_nt(gr_b, bbr_ref[0]) + _dot_nt(gi_b, bbi_ref[0]) + d_ref[...] * dy_t
        dbbr_ref[0] += _dot_tn(u_b, gr_b)
        dbbi_ref[0] += _dot_tn(u_b, gi_b)
        dcr_ref[0] += _dot_tn(xr.astype(BF16), dy_b)
        dci_ref[0] -= _dot_tn(xi.astype(BF16), dy_b)
        dd_ref[0] += jnp.sum((dy_t * u_t).reshape(tc // SUBLANES, SUBLANES, LANES), axis=0)
        first = row == 0
        xpr = jnp.where(first, jnp.broadcast_to(s0r[0:1], xr.shape), pltpu.roll(xr, 1, 0))
        xpi = jnp.where(first, jnp.broadcast_to(s0i[0:1], xi.shape), pltpu.roll(xi, 1, 0))
        shp = (tc // SUBLANES, SUBLANES, S5_STATES)
        dar_ref[0] += jnp.sum((gr * xpr + gi * xpi).reshape(shp), axis=0)
        dai_ref[0] += jnp.sum((gi * xpr - gr * xpi).reshape(shp), axis=0)

    u_spec = pl.BlockSpec((tc, LANES), lambda cb, b, t: (b * nt + nt - 1 - t, cb))
    a_spec = pl.BlockSpec((1, 1, S5_STATES), lambda cb, b, t: (cb, 0, 0))
    bb_spec = pl.BlockSpec((1, LANES, S5_STATES), lambda cb, b, t: (cb, 0, 0))
    c_spec = pl.BlockSpec((1, S5_STATES, LANES), lambda cb, b, t: (cb, 0, 0))
    st_spec = pl.BlockSpec((1, 1, SUBLANES, S5_STATES), lambda cb, b, t: (cb, b * nt + nt - 1 - t, 0, 0))
    da_spec = pl.BlockSpec((1, SUBLANES, S5_STATES), lambda cb, b, t: (cb, 0, 0))
    dd_spec = pl.BlockSpec((1, SUBLANES, LANES), lambda cb, b, t: (cb, 0, 0))
    big = pltpu.VMEM((tc, S5_STATES), F32)
    small = pltpu.VMEM((SUBLANES, S5_STATES), F32)
    return pl.pallas_call(
        body, name="s5_bwd", grid=(S5_BLOCKS, n_seq, nt),
        in_specs=[u_spec, u_spec, st_spec, st_spec, a_spec, a_spec, bb_spec, bb_spec, c_spec, c_spec,
                  pl.BlockSpec((1, LANES), lambda cb, b, t: (0, cb))],
        out_specs=(u_spec, bb_spec, bb_spec, c_spec, c_spec, da_spec, da_spec, dd_spec),
        out_shape=(jax.ShapeDtypeStruct((n, D_SSM), F32),
                   jax.ShapeDtypeStruct((S5_BLOCKS, LANES, S5_STATES), F32),
                   jax.ShapeDtypeStruct((S5_BLOCKS, LANES, S5_STATES), F32),
                   jax.ShapeDtypeStruct((S5_BLOCKS, S5_STATES, LANES), F32),
                   jax.ShapeDtypeStruct((S5_BLOCKS, S5_STATES, LANES), F32),
                   jax.ShapeDtypeStruct((S5_BLOCKS, SUBLANES, S5_STATES), F32),
                   jax.ShapeDtypeStruct((S5_BLOCKS, SUBLANES, S5_STATES), F32),
                   jax.ShapeDtypeStruct((S5_BLOCKS, SUBLANES, LANES), F32)),
        scratch_shapes=[big, big, big, big, small, small],
        compiler_params=_cparams("parallel", "arbitrary", "arbitrary"),
    )(u, dy, st_r, st_i, a_re, a_im, bbr, bbi, cr, ci, d_skip)


CUM_BLOCK = 128


def _tri(lower):
    r = lax.broadcasted_iota(jnp.int32, (CUM_BLOCK, CUM_BLOCK), 0)
    c = lax.broadcasted_iota(jnp.int32, (CUM_BLOCK, CUM_BLOCK), 1)
    return jnp.where(r >= c if lower else r <= c, 1.0, 0.0).astype(F32)


def _fprep_fwd(fl, bf, n_seq):
    n = fl.shape[0]
    seq_len = n // n_seq
    nb = seq_len // CUM_BLOCK

    def body(fl_ref, bf_ref, cum_ref):
        tril = _tri(True)
        carry = jnp.zeros((1, LANES), F32)
        for blk in range(nb):
            rows = slice(blk * CUM_BLOCK, (blk + 1) * CUM_BLOCK)
            lf = jax.nn.log_sigmoid(fl_ref[rows, :] + bf_ref[...])
            cs = jnp.dot(tril, lf, preferred_element_type=F32, precision=HIGHEST) + carry
            cum_ref[rows, :] = cs
            carry = cs[CUM_BLOCK - 1:CUM_BLOCK, :]

    spec = pl.BlockSpec((seq_len, LANES), lambda b: (b, 0))
    return pl.pallas_call(
        body, name="fprep_fwd", grid=(n_seq,), in_specs=[spec, pl.BlockSpec((1, LANES), lambda b: (0, 0))],
        out_specs=spec, out_shape=jax.ShapeDtypeStruct((n, LANES), F32), compiler_params=_cparams("parallel"),
    )(fl, bf)


def _fprep_bwd(dcum, fl, bf, n_seq):
    n = fl.shape[0]
    seq_len = n // n_seq
    nb = seq_len // CUM_BLOCK

    def body(dcum_ref, fl_ref, bf_ref, dfl_ref, dbf_ref):
        triu = _tri(False)
        lane = lax.broadcasted_iota(jnp.int32, (CUM_BLOCK, LANES), 1)
        carry = jnp.zeros((1, LANES), F32)
        total = jnp.zeros((1, LANES), F32)
        for blk in reversed(range(nb)):
            rows = slice(blk * CUM_BLOCK, (blk + 1) * CUM_BLOCK)
            rs = jnp.dot(triu, dcum_ref[rows, :], preferred_element_type=F32, precision=HIGHEST) + carry
            carry = rs[0:1, :]
            _, vjp = jax.vjp(jax.nn.log_sigmoid, fl_ref[rows, :] + bf_ref[...])
            dz = jnp.where(lane < N_HEADS, vjp(rs)[0], 0.0)
            dfl_ref[rows, :] = dz
            total = total + jnp.sum(dz, axis=0, keepdims=True)
        dbf_ref[0] = total

    spec = pl.BlockSpec((seq_len, LANES), lambda b: (b, 0))
    return pl.pallas_call(
        body, name="fprep_bwd", grid=(n_seq,), in_specs=[spec, spec, pl.BlockSpec((1, LANES), lambda b: (0, 0))],
        out_specs=(spec, pl.BlockSpec((1, 1, LANES), lambda b: (b, 0, 0))),
        out_shape=(jax.ShapeDtypeStruct((n, LANES), F32), jax.ShapeDtypeStruct((n_seq, 1, LANES), F32)),
        compiler_params=_cparams("parallel"),
    )(dcum, fl, bf)


ATT_TQ = 256
ATT_SCALE = HEAD_DIM ** -0.5
NEG_BIG = -1e30


def _attn_scores(qe, kb, cq, ck, causal):
    s = _dot_nt(qe, kb) * ATT_SCALE + cq - ck
    return jnp.where(causal, s, NEG_BIG)


def _causal_mask(qi, kblk):
    r = lax.broadcasted_iota(jnp.int32, (ATT_TQ, ATT_TQ), 0) + qi * ATT_TQ
    c = lax.broadcasted_iota(jnp.int32, (ATT_TQ, ATT_TQ), 1) + kblk * ATT_TQ
    return r >= c


def _attn_specs(n_seq, seq_len):
    nq = seq_len // ATT_TQ
    q_spec = pl.BlockSpec((ATT_TQ, LANES), lambda b, h, q: (b * nq + q, h))
    k_spec = pl.BlockSpec((seq_len, LANES), lambda b, h, q: (b, N_HEADS // 2 + h))
    v_spec = pl.BlockSpec((seq_len, LANES), lambda b, h, q: (b, N_HEADS + h))
    cq_spec = pl.BlockSpec((1, 2, ATT_TQ, 1), lambda b, h, q: (b, h, q, 0))
    ck_spec = pl.BlockSpec((1, 2, nq, 1, ATT_TQ), lambda b, h, q: (b, h, 0, 0, 0))
    return nq, q_spec, k_spec, v_spec, cq_spec, ck_spec


def _head_selectors():
    head0 = lax.broadcasted_iota(jnp.int32, (1, LANES), 1) < HEAD_DIM
    return head0, (head0, jnp.logical_not(head0))


def _attn_fwd(qkv, cq, ck, n_seq):
    n = qkv.shape[0]
    seq_len = n // n_seq
    nq, q_spec, k_spec, v_spec, cq_spec, ck_spec = _attn_specs(n_seq, seq_len)

    def body(q_ref, k_ref, v_ref, cq_ref, ck_ref, o_ref, lse_ref):
        qi = pl.program_id(2)
        q2 = q_ref[...]
        head0, sels = _head_selectors()
        qe = [jnp.where(sel, q2, 0.0).astype(BF16) for sel in sels]
        cqe = [cq_ref[0, e] for e in range(2)]

        def step(kblk, carry):
            off = pl.multiple_of(kblk * ATT_TQ, ATT_TQ)
            kb = k_ref[pl.ds(off, ATT_TQ), :].astype(BF16)
            vb = v_ref[pl.ds(off, ATT_TQ), :].astype(BF16)
            causal = _causal_mask(qi, kblk)
            new = []
            for e in range(2):
                mx, den, acc = carry[e]
                s = _attn_scores(qe[e], kb, cqe[e], ck_ref[0, e, kblk], causal)
                mx_new = jnp.maximum(mx, jnp.max(s, axis=1, keepdims=True))
                alpha = jnp.exp(mx - mx_new)
                p = jnp.exp(s - mx_new)
                den = alpha * den + jnp.sum(p, axis=1, keepdims=True)
                acc = alpha * acc + _dot(p.astype(BF16), vb)
                new.append((mx_new, den, acc))
            return tuple(new)

        zero = jnp.zeros((ATT_TQ, 1), F32)
        init = (jnp.full((ATT_TQ, 1), NEG_BIG, F32), zero, jnp.zeros((ATT_TQ, LANES), F32))
        res = lax.fori_loop(0, qi + 1, step, (init, init))
        for e in range(2):
            lse_ref[0, e] = res[e][0] + jnp.log(res[e][1])
        o_ref[...] = jnp.where(head0, res[0][2] / res[0][1], res[1][2] / res[1][1])

    return pl.pallas_call(
        body, name="attn_fwd", grid=(n_seq, N_HEADS // 2, nq),
        in_specs=[q_spec, k_spec, v_spec, cq_spec, ck_spec],
        out_specs=(q_spec, cq_spec),
        out_shape=(jax.ShapeDtypeStruct((n, D_ATTN), F32), jax.ShapeDtypeStruct((n_seq, N_HEADS, seq_len, 1), F32)),
        compiler_params=_cparams("parallel", "parallel", "parallel"),
    )(qkv, qkv, qkv, cq, ck)


def _attn_bwd(qkv, cq, ck, o, do, lse, n_seq):
    n = qkv.shape[0]
    seq_len = n // n_seq
    nq, q_spec, k_spec, v_spec, cq_spec, ck_spec = _attn_specs(n_seq, seq_len)
    kv_out = pl.BlockSpec((seq_len, LANES), lambda b, h, q: (b, h))

    def body(q_ref, k_ref, v_ref, cq_ref, ck_ref, o_ref, do_ref, lse_ref, dq_ref, dk_ref, dv_ref, dcq_ref, dck_ref):
        @pl.when(pl.program_id(2) == 0)
        def _():
            dk_ref[...] = jnp.zeros_like(dk_ref)
            dv_ref[...] = jnp.zeros_like(dv_ref)
            dck_ref[...] = jnp.zeros_like(dck_ref)
        qi = pl.program_id(2)
        q2 = q_ref[...]
        do2 = do_ref[...]
        o2 = o_ref[...]
        head0, sels = _head_selectors()
        qe = [jnp.where(sel, q2, 0.0).astype(BF16) for sel in sels]
        doe = [jnp.where(sel, do2, 0.0) for sel in sels]
        doe_b = [d.astype(BF16) for d in doe]
        delta = [jnp.sum(d * o2, axis=1, keepdims=True) for d in doe]
        cqe = [cq_ref[0, e] for e in range(2)]
        lse = [lse_ref[0, e] for e in range(2)]

        def step(kblk, carry):
            off = pl.multiple_of(kblk * ATT_TQ, ATT_TQ)
            kb = k_ref[pl.ds(off, ATT_TQ), :].astype(BF16)
            vb = v_ref[pl.ds(off, ATT_TQ), :].astype(BF16)
            causal = _causal_mask(qi, kblk)
            new = []
            dk = jnp.zeros((ATT_TQ, LANES), F32)
            dv = jnp.zeros((ATT_TQ, LANES), F32)
            for e in range(2):
                dq, dcq = carry[e]
                s = _attn_scores(qe[e], kb, cqe[e], ck_ref[0, e, kblk], causal)
                p = jnp.exp(s - lse[e])
                ds = p * (_dot_nt(doe_b[e], vb) - delta[e])
                ds_b = ds.astype(BF16)
                dk = dk + _dot_tn(ds_b, qe[e])
                dv = dv + _dot_tn(p.astype(BF16), doe_b[e])
                dck_ref[0, e, kblk] -= jnp.sum(ds, axis=0, keepdims=True)
                new.append((dq + _dot(ds_b, kb), dcq + jnp.sum(ds, axis=1, keepdims=True)))
            dk_ref[pl.ds(off, ATT_TQ), :] += dk * ATT_SCALE
            dv_ref[pl.ds(off, ATT_TQ), :] += dv
            return tuple(new)

        init = (jnp.zeros((ATT_TQ, LANES), F32), jnp.zeros((ATT_TQ, 1), F32))
        res = lax.fori_loop(0, qi + 1, step, (init, init))
        for e in range(2):
            dcq_ref[0, e] = res[e][1]
        dq_ref[...] = jnp.where(head0, res[0][0], res[1][0]) * ATT_SCALE

    return pl.pallas_call(
        body, name="attn_bwd", grid=(n_seq, N_HEADS // 2, nq),
        in_specs=[q_spec, k_spec, v_spec, cq_spec, ck_spec, q_spec, q_spec, cq_spec],
        out_specs=(q_spec, kv_out, kv_out, cq_spec, ck_spec),
        out_shape=(jax.ShapeDtypeStruct((n, D_ATTN), F32), jax.ShapeDtypeStruct((n, D_ATTN), F32),
                   jax.ShapeDtypeStruct((n, D_ATTN), F32),
                   jax.ShapeDtypeStruct((n_seq, N_HEADS, seq_len, 1), F32),
                   jax.ShapeDtypeStruct((n_seq, N_HEADS, nq, 1, ATT_TQ), F32)),
        compiler_params=_cparams("parallel", "parallel", "arbitrary"),
    )(qkv, qkv, qkv, cq, ck, o, do, lse)


WEIGHT_NAMES = ("norm_mix", "w_in", "b_forget", "lam_re", "lam_im", "b_re", "b_im", "c_re", "c_im", "d_skip", "log_dt",
                "w_glu", "b_glu", "q_norm", "k_norm", "norm_out_ssm", "norm_out_attn", "w_out", "norm_ffn", "w_up",
                "conv_w", "conv_b", "w_down")
SHARDED = ("w_in", "w_glu", "w_out", "w_up", "conv_w", "w_down")
ADAM_ROWS = {"w_in": 256, "w_glu": 64, "w_out": 128, "w_up": 128, "conv_w": 3, "w_down": 344}
PACK_ROWS = SUBLANES * LANES


def _pad_to(a, axis, size):
    pad = [(0, 0)] * a.ndim
    pad[axis] = (0, size - a.shape[axis])
    return jnp.pad(a, pad)


def _block_diag(t, transpose):
    t4 = t.reshape(S5_BLOCKS, 8, SSM_GROUP, SSM_STATE)
    eye = jnp.eye(8, dtype=t.dtype)
    if transpose:
        e = jnp.swapaxes(t4, 2, 3)[:, :, :, None, :] * eye[None, :, None, :, None]
        return e.reshape(S5_BLOCKS, S5_STATES, LANES)
    e = t4[:, :, :, None, :] * eye[None, :, None, :, None]
    return e.reshape(S5_BLOCKS, LANES, S5_STATES)


def _block_diag_extract(m, transpose):
    if transpose:
        m5 = m.reshape(S5_BLOCKS, 8, SSM_STATE, 8, SSM_GROUP)
        d = jnp.stack([m5[:, i, :, i, :] for i in range(8)], axis=1)
        return jnp.swapaxes(d, 2, 3).reshape(N_GROUPS, SSM_GROUP, SSM_STATE)
    m5 = m.reshape(S5_BLOCKS, 8, SSM_GROUP, 8, SSM_STATE)
    d = jnp.stack([m5[:, i, :, i, :] for i in range(8)], axis=1)
    return d.reshape(N_GROUPS, SSM_GROUP, SSM_STATE)


def _pack(pieces):
    flat = jnp.concatenate([p.reshape(-1).astype(F32) for p in pieces])
    size = -(-flat.shape[0] // PACK_ROWS) * PACK_ROWS
    return _pad_to(flat, 0, size).reshape(-1, LANES)


def _unpack(packed, shapes):
    flat = packed.reshape(-1)
    out, off = [], 0
    for shp in shapes:
        size = math.prod(shp)
        out.append(flat[off:off + size].reshape(shp))
        off += size
    return out


def kernel(x, norm_mix, w_in, b_forget, lam_re, lam_im, b_re, b_im, c_re, c_im, d_skip, log_dt, w_glu, b_glu, q_norm, k_norm, norm_out_ssm, norm_out_attn, w_out, norm_ffn, w_up, conv_w, conv_b, w_down, loss_target, m_norm_mix, m_w_in, m_b_forget, m_lam_re, m_lam_im, m_b_re, m_b_im, m_c_re, m_c_im, m_d_skip, m_log_dt, m_w_glu, m_b_glu, m_q_norm, m_k_norm, m_norm_out_ssm, m_norm_out_attn, m_w_out, m_norm_ffn, m_w_up, m_conv_w, m_conv_b, m_w_down, v_norm_mix, v_w_in, v_b_forget, v_lam_re, v_lam_im, v_b_re, v_b_im, v_c_re, v_c_im, v_d_skip, v_log_dt, v_w_glu, v_b_glu, v_q_norm, v_k_norm, v_norm_out_ssm, v_norm_out_attn, v_w_out, v_norm_ffn, v_w_up, v_conv_w, v_conv_b, v_w_down):
    given = dict(locals())
    weights = {k: given[k] for k in WEIGHT_NAMES}
    mom1 = {k: given["m_" + k] for k in WEIGHT_NAMES}
    mom2 = {k: given["v_" + k] for k in WEIGHT_NAMES}
    n_seq, seq_len, _ = x.shape
    n = n_seq * seq_len
    xf = x.reshape(n, D_MODEL)
    target = loss_target.reshape(n, D_MODEL)

    (g_in,) = _exchange([w_in[0].astype(BF16)], [False], "gather_w_in")
    rest_flags = [False] * 5
    w_sems = _exchange_start(
        [w_glu[0].astype(BF16), w_out[0].astype(BF16), w_up[0].astype(BF16), conv_w[0], w_down[0].astype(BF16)],
        rest_flags, g_in, "gather_rest_start", 0)
    norm_mix = norm_mix + w_sems[4][0, 0]
    w_in_p = _pad_to(jnp.swapaxes(g_in, 0, 1).reshape(D_MODEL, D_IN), 1, D_IN_PAD)

    lr3 = lam_re[0].reshape(N_GROUPS, 1, SSM_STATE)
    li3 = lam_im[0].reshape(N_GROUPS, 1, SSM_STATE)
    ldt3 = log_dt[0].reshape(N_GROUPS, 1, 1)
    br_t = jnp.swapaxes(b_re[0], 1, 2)
    bi_t = jnp.swapaxes(b_im[0], 1, 2)
    ab_re, ab_im, bb_re, bb_im = _s5_param_fwd(lr3, li3, ldt3, br_t, bi_t)
    a_re = ab_re.reshape(S5_BLOCKS, 1, S5_STATES)
    a_im = ab_im.reshape(S5_BLOCKS, 1, S5_STATES)
    bbr = _block_diag(bb_re, False).astype(BF16)
    bbi = _block_diag(bb_im, False).astype(BF16)
    cr = _block_diag(c_re[0], True).astype(BF16)
    ci = _block_diag(c_im[0], True).astype(BF16)

    avg = jnp.kron(jnp.eye(N_HEADS, dtype=F32), jnp.full((HEAD_DIM, HEAD_DIM), 1.0 / HEAD_DIM, F32))
    qg = jnp.tile(q_norm, (1, N_HEADS))
    kg = jnp.tile(k_norm, (1, N_HEADS))
    hn, u, qkv, raw, fl = _inproj_fwd(xf, norm_mix, w_in_p, avg, qg, kg)
    yc, st_r, st_i = _s5_fwd(u, a_re, a_im, bbr, bbi, cr, ci, d_skip, n_seq)
    bf = _pad_to(b_forget, 1, LANES)
    cum = _fprep_fwd(fl, bf, n_seq)
    cum8 = jnp.swapaxes(cum[:, :N_HEADS].reshape(n_seq, seq_len, N_HEADS), 1, 2)
    cq = cum8[:, :, :, None]
    ck = cum8.reshape(n_seq, N_HEADS, seq_len // ATT_TQ, 1, ATT_TQ)
    ya, lse = _attn_fwd(qkv, cq, ck, n_seq)
    g_glu, g_out, g_up, g_cw, g_down = _exchange_wait(w_sems[0], w_sems[1], w_sems[2], w_sems[3], rest_flags, ya,
                                                      "gather_rest_wait")
    w_glu_f = g_glu.reshape(D_SSM, D_SSM)
    w_out_f = g_out.reshape(D_MODEL, D_MODEL)
    w_up_f = jnp.swapaxes(g_up, 0, 1).reshape(D_MODEL, 2 * D_FF)
    wup_g = _pad_to(w_up_f[:, :D_FF], 1, D_FF_PAD)
    wup_v = _pad_to(w_up_f[:, D_FF:], 1, D_FF_PAD)
    cw_f = jnp.swapaxes(g_cw, 0, 1).reshape(3, 2 * D_FF)
    cw4 = jnp.concatenate([cw_f, conv_b], axis=0)
    cw_g = _pad_to(_pad_to(cw4[:, :D_FF], 1, D_FF_PAD), 0, SUBLANES)
    cw_v = _pad_to(_pad_to(cw4[:, D_FF:], 1, D_FF_PAD), 0, SUBLANES)
    w_down_p = _pad_to(g_down.reshape(D_FF, D_MODEL), 0, D_FF_PAD)
    ys = _glu_fwd(yc, w_glu_f, b_glu)
    h1, hn2, mixed = _mix_fwd(xf, ys, ya, norm_out_ssm, norm_out_attn, w_out_f, norm_ffn)
    ug, uv, dy, loss_part = _ffn_fwd(hn2, h1, target, wup_g, wup_v, cw_g, cw_v, w_down_p, seq_len)
    loss = lax.psum(0.5 * jnp.sum(loss_part) / D_MODEL, ("x", "y", "c"))

    dug, duv, act, dhn2, dcg, dcv = _ffn_bwd(dy, ug, uv, wup_g, wup_v, cw_g, cw_v, w_down_p, seq_len)
    dh1, dys, dya, d_gs, d_ga, d_gf = _mix_bwd(dy, dhn2[None], h1, ys, ya, norm_out_ssm, norm_out_attn, w_out_f, norm_ffn)
    dyc, gl_b, dz_b, d_bglu = _glu_bwd(yc, dys, w_glu_f, b_glu)

    gw_glu = _tn_matmul(gl_b, dz_b, "dw_glu", D_SSM, D_SSM)
    gw_out = _tn_matmul(mixed, dh1, "dw_out", D_MODEL, D_MODEL)
    gw_up = jnp.concatenate([_tn_matmul(hn2, dug, "dw_up_gate", D_MODEL, D_FF_PAD // 2)[:, :D_FF],
                             _tn_matmul(hn2, duv, "dw_up_val", D_MODEL, D_FF_PAD // 2)[:, :D_FF]], axis=1)
    gw_down = _tn_matmul(act, dy, "dw_down", D_FF_PAD // 2, D_MODEL)[:D_FF]
    dcg2 = jnp.swapaxes(dcg, 0, 1).reshape(SUBLANES, D_FF_PAD)[:, :D_FF]
    dcv2 = jnp.swapaxes(dcv, 0, 1).reshape(SUBLANES, D_FF_PAD)[:, :D_FF]
    g_conv = jnp.concatenate([dcg2, dcv2], axis=1)
    by_cols = lambda g, c: jnp.swapaxes(g.reshape(g.shape[0], N_DEV, c), 0, 1)
    early_flags = [True] * 5
    g_sems = _exchange_start(
        [gw_glu.reshape(N_DEV, -1, D_SSM), gw_out.reshape(N_DEV, -1, D_MODEL), by_cols(gw_up, 2 * D_FF // N_DEV),
         by_cols(g_conv[:3], 2 * D_FF // N_DEV), gw_down.reshape(N_DEV, -1, D_MODEL)],
        early_flags, dyc, "grad_early_start", 1)
    started = g_sems[4][0, 0]

    du, dbbr, dbbi, dcr, dci, dar, dai, ddk = _s5_bwd(u, dyc, st_r, st_i, a_re, a_im, bbr, bbi, cr, ci,
                                                      d_skip + started, n_seq)
    dqn, dkn, dv, dcq, dck = _attn_bwd(qkv, cq, ck, ya, dya, lse + started, n_seq)
    dcum8 = dcq[:, :, :, 0] + dck.reshape(n_seq, N_HEADS, seq_len)
    dcum = _pad_to(jnp.swapaxes(dcum8, 1, 2).reshape(n, N_HEADS), 1, LANES)
    dfl, dbf = _fprep_bwd(dcum, fl, bf, n_seq)
    dx, dproj, d_gmix, d_qg, d_kg = _inproj_bwd(xf, norm_mix, w_in_p, avg, qg, kg, raw, du, dqn, dkn, dv, dfl, dh1)

    gw_in = _tn_matmul(hn, dproj, "dw_in", D_MODEL, D_IN_PAD)[:, :D_IN]
    partial_small = {
        "norm_mix": d_gmix, "b_forget": jnp.sum(dbf, axis=(0, 1))[:N_HEADS],
        "ab_re": jnp.sum(dar, axis=1), "ab_im": jnp.sum(dai, axis=1),
        "bb_re": _block_diag_extract(dbbr, False), "bb_im": _block_diag_extract(dbbi, False),
        "c_re": _block_diag_extract(dcr, True), "c_im": _block_diag_extract(dci, True),
        "d_skip": jnp.sum(ddk, axis=1), "b_glu": d_bglu,
        "q_norm": jnp.sum(d_qg.reshape(N_HEADS, HEAD_DIM), axis=0),
        "k_norm": jnp.sum(d_kg.reshape(N_HEADS, HEAD_DIM), axis=0),
        "norm_out_ssm": d_gs, "norm_out_attn": d_ga, "norm_ffn": d_gf, "conv_b": g_conv[3],
    }
    small_keys = tuple(partial_small)
    small_shapes = [partial_small[k].shape for k in small_keys]

    land_in, small_parts = _exchange(
        [by_cols(gw_in, D_IN // N_DEV), _pack([partial_small[k] for k in small_keys])], [True, False], "grad_late_exchange")
    land_glu, land_out, land_up, land_cw, land_down = _exchange_wait(
        g_sems[0], g_sems[1], g_sems[2], g_sems[3], early_flags, land_in, "grad_early_wait")
    grads, deltas, new_m, new_v = {}, {}, {}, {}
    for name, land in zip(SHARDED, (land_in, land_glu, land_out, land_up, land_cw, land_down)):
        g, d, nm, nv = _adam_sharded(land, weights[name][0], mom1[name][0], mom2[name][0], "adam_" + name, ADAM_ROWS[name])
        grads[name], deltas[name], new_m[name], new_v[name] = g[None], d[None], nm[None], nv[None]

    summed = dict(zip(small_keys, _unpack(_sum_partials(small_parts, "sum_small_grads"), small_shapes)))
    dlr, dli, dldt, dbr_t, dbi_t = _s5_param_bwd(
        lr3, li3, ldt3, br_t, bi_t, summed["ab_re"].reshape(lr3.shape), summed["ab_im"].reshape(lr3.shape),
        summed["bb_re"], summed["bb_im"])
    small_grads = {
        "norm_mix": summed["norm_mix"], "b_forget": summed["b_forget"], "lam_re": dlr, "lam_im": dli,
        "b_re": jnp.swapaxes(dbr_t, 1, 2), "b_im": jnp.swapaxes(dbi_t, 1, 2), "c_re": summed["c_re"], "c_im": summed["c_im"],
        "d_skip": summed["d_skip"], "log_dt": dldt, "b_glu": summed["b_glu"], "q_norm": summed["q_norm"],
        "k_norm": summed["k_norm"], "norm_out_ssm": summed["norm_out_ssm"], "norm_out_attn": summed["norm_out_attn"],
        "norm_ffn": summed["norm_ffn"], "conv_b": summed["conv_b"],
    }
    repl = tuple(k for k in WEIGHT_NAMES if k not in SHARDED)
    g_list = [small_grads[k].reshape(weights[k].shape) for k in repl]
    d_list, m_list, v_list = _adam_replicated(g_list, [weights[k] for k in repl], [mom1[k] for k in repl],
                                              [mom2[k] for k in repl], "adam_replicated")
    for k, g, d, nm, nv in zip(repl, g_list, d_list, m_list, v_list):
        grads[k], deltas[k], new_m[k], new_v[k] = g, d, nm, nv

    grad_x = dx.reshape(x.shape)
    return (loss, grad_x, *[grads[k] for k in WEIGHT_NAMES], *[deltas[k] for k in WEIGHT_NAMES],
            *[new_m[k] for k in WEIGHT_NAMES], *[new_v[k] for k in WEIGHT_NAMES])
```

```python
import functools
import math

import jax
import jax.numpy as jnp
from jax import lax
from jax.experimental import pallas as pl
from jax.experimental.pallas import tpu as pltpu

F32 = jnp.float32
BF16 = jnp.bfloat16
HIGHEST = lax.Precision.HIGHEST

N_DEV = 8
D_MODEL = 1024
D_SSM = 512
D_ATTN = 512
N_HEADS = 8
HEAD_DIM = 64
N_GROUPS = 32
SSM_GROUP = 16
SSM_STATE = 64
D_FF = 2752
D_FF_PAD = 2816
D_IN = 2056
D_IN_PAD = 2176
EPS = 1e-6
LANES = 128
SUBLANES = 8
VMEM_LIMIT = 56 * 1024 * 1024

ADAM_LR = 0.001
ADAM_B1 = 0.9
ADAM_B2 = 0.999
ADAM_EPS = 1e-08
ADAM_WD = 0.01
ADAM_STEP = 10


def _cparams(*sem):
    return pltpu.CompilerParams(dimension_semantics=sem, vmem_limit_bytes=VMEM_LIMIT)


def _dot(a, b, **kw):
    return jnp.dot(a, b, preferred_element_type=F32, **kw)


def _dot_nt(a, b):
    return lax.dot_general(a, b, (((1,), (1,)), ((), ())), preferred_element_type=F32)


def _dot_tn(a, b):
    return lax.dot_general(a, b, (((0,), (0,)), ((), ())), preferred_element_type=F32)


def _rms(x, g):
    return x * lax.rsqrt(jnp.mean(x * x, axis=-1, keepdims=True) + EPS) * g


def _headnorm(q, avg, g):
    ms = jnp.dot(q * q, avg, preferred_element_type=F32, precision=HIGHEST)
    return q * lax.rsqrt(ms + EPS) * g


def _exchange(srcs, scatter_flags, name):
    n = len(srcs)
    out_shape = []
    for s, sc in zip(srcs, scatter_flags):
        shp = s.shape if sc else (N_DEV,) + s.shape
        out_shape.append(jax.ShapeDtypeStruct(shp, s.dtype))

    def body(*refs):
        src = refs[:n]
        dst = refs[n:2 * n]
        send_sems, recv_sems, loc_sems = refs[2 * n:]
        x, y, c = lax.axis_index("x"), lax.axis_index("y"), lax.axis_index("c")
        me = 4 * x + 2 * y + c
        peers = []
        for j in range(1, N_DEV):
            px = 1 - x if (j >> 2) & 1 else x
            py = 1 - y if (j >> 1) & 1 else y
            pc = 1 - c if j & 1 else c
            peers.append(((px, py, pc), 4 * px + 2 * py + pc))
        local, sends = [], []
        for k in range(n):
            own = src[k].at[me] if scatter_flags[k] else src[k]
            lc = pltpu.make_async_copy(own, dst[k].at[me], loc_sems.at[k])
            lc.start()
            local.append(lc)
            for j, (pid, pidx) in enumerate(peers):
                s = src[k].at[pidx] if scatter_flags[k] else src[k]
                cp = pltpu.make_async_remote_copy(
                    src_ref=s, dst_ref=dst[k].at[me], send_sem=send_sems.at[k, j], recv_sem=recv_sems.at[k, j],
                    device_id=pid, device_id_type=pl.DeviceIdType.MESH)
                cp.start()
                sends.append(cp)
        for k in range(n):
            for j, (pid, pidx) in enumerate(peers):
                s = src[k].at[pidx] if scatter_flags[k] else src[k]
                pltpu.make_async_remote_copy(
                    src_ref=s, dst_ref=dst[k].at[pidx], send_sem=send_sems.at[k, j], recv_sem=recv_sems.at[k, j],
                    device_id=pid, device_id_type=pl.DeviceIdType.MESH).wait_recv()
        for cp in sends:
            cp.wait_send()
        for lc in local:
            lc.wait()

    any_spec = pl.BlockSpec(memory_space=pl.ANY)
    return pl.pallas_call(
        body, name=name, out_shape=tuple(out_shape),
        in_specs=[any_spec] * n, out_specs=tuple([any_spec] * n),
        scratch_shapes=[pltpu.SemaphoreType.DMA((n, N_DEV - 1)), pltpu.SemaphoreType.DMA((n, N_DEV - 1)),
                        pltpu.SemaphoreType.DMA((n,))],
        compiler_params=pltpu.CompilerParams(has_side_effects=True),
    )(*srcs)


def _peer_list():
    x, y, c = lax.axis_index("x"), lax.axis_index("y"), lax.axis_index("c")
    peers = []
    for j in range(1, N_DEV):
        px = 1 - x if (j >> 2) & 1 else x
        py = 1 - y if (j >> 1) & 1 else y
        pc = 1 - c if j & 1 else c
        peers.append(((px, py, pc), 4 * px + 2 * py + pc))
    return 4 * x + 2 * y + c, peers


def _split_copies(src, land, send_sems, recv_sems, scatter_flags, me, peers, incoming):
    copies = []
    for k in range(len(src)):
        for j, (pid, pidx) in enumerate(peers):
            s = src[k].at[pidx] if scatter_flags[k] else src[k]
            i = k * (N_DEV - 1) + j
            copies.append(pltpu.make_async_remote_copy(
                src_ref=s, dst_ref=land[k].at[pidx if incoming else me], send_sem=send_sems[i],
                recv_sem=recv_sems[i], device_id=pid, device_id_type=pl.DeviceIdType.MESH))
    return copies


def _exchange_start(srcs, scatter_flags, after, name, collective_id):
    n = len(srcs)
    ns = n * (N_DEV - 1)
    hbm = pl.BlockSpec(memory_space=pltpu.HBM)
    sem = pl.BlockSpec(memory_space=pltpu.SEMAPHORE)
    land_shapes = [s.shape if sc else (N_DEV,) + s.shape for s, sc in zip(srcs, scatter_flags)]

    def body(*refs):
        src, land = refs[:n], refs[n:2 * n]
        send_sems = refs[2 * n + 1:2 * n + 1 + ns]
        recv_sems = refs[2 * n + 1 + ns:2 * n + 1 + 2 * ns]
        token = refs[4 * n + 1 + 2 * ns]
        local_sems = refs[4 * n + 2 + 2 * ns]
        me, peers = _peer_list()
        barrier = pltpu.get_barrier_semaphore()
        for pid, _ in peers:
            pl.semaphore_signal(barrier, inc=1, device_id=pid, device_id_type=pl.DeviceIdType.MESH)
        pl.semaphore_wait(barrier, N_DEV - 1)
        local = []
        for k in range(n):
            own = src[k].at[me] if scatter_flags[k] else src[k]
            local.append(pltpu.make_async_copy(own, land[k].at[me], local_sems.at[k]))
            local[-1].start()
        for lc in local:
            lc.wait()
        for cp in _split_copies(src, land, send_sems, recv_sems, scatter_flags, me, peers, False):
            cp.start()
        token[...] = jnp.zeros_like(token)

    outs = pl.pallas_call(
        body, name=name,
        out_shape=(*[pltpu.SemaphoreType.DMA(())] * (2 * ns), *[pltpu.HBM(s.shape, s.dtype) for s in srcs],
                   *[pltpu.HBM(shp, s.dtype) for shp, s in zip(land_shapes, srcs)],
                   jax.ShapeDtypeStruct((SUBLANES, LANES), F32)),
        in_specs=[hbm] * (2 * n) + [pl.BlockSpec(memory_space=pl.ANY)],
        out_specs=(*[sem] * (2 * ns), *[hbm] * (2 * n), pl.BlockSpec(memory_space=pltpu.VMEM)),
        input_output_aliases={i: 2 * ns + i for i in range(2 * n)},
        scratch_shapes=[pltpu.SemaphoreType.DMA((n,))],
        compiler_params=pltpu.CompilerParams(has_side_effects=pltpu.SideEffectType.DATAFLOW_SIDE_EFFECTING,
                                             collective_id=collective_id),
    )(*[pltpu.with_memory_space_constraint(s, pltpu.HBM) for s in srcs],
      *[pltpu.with_memory_space_constraint(lax.empty(shp, s.dtype), pltpu.HBM) for shp, s in zip(land_shapes, srcs)],
      after)
    return (outs[:ns], outs[ns:2 * ns], outs[2 * ns:2 * ns + n], outs[2 * ns + n:2 * ns + 2 * n], outs[2 * ns + 2 * n])


def _exchange_wait(send_sems, recv_sems, srcs, lands, scatter_flags, after, name):
    n = len(srcs)
    ns = n * (N_DEV - 1)
    hbm = pl.BlockSpec(memory_space=pltpu.HBM)
    sem = pl.BlockSpec(memory_space=pltpu.SEMAPHORE)

    def body(*refs):
        src, land = refs[:n], refs[n:2 * n]
        s_sems = refs[2 * n:2 * n + ns]
        r_sems = refs[2 * n + ns:2 * n + 2 * ns]
        me, peers = _peer_list()
        for cp in _split_copies(src, land, s_sems, r_sems, scatter_flags, me, peers, True):
            cp.wait_send()
            cp.wait_recv()

    outs = pl.pallas_call(
        body, name=name,
        out_shape=tuple(pltpu.HBM(a.shape, a.dtype) for a in (*srcs, *lands)),
        in_specs=[hbm] * (2 * n) + [sem] * (2 * ns) + [pl.BlockSpec(memory_space=pl.ANY)],
        out_specs=tuple([hbm] * (2 * n)),
        input_output_aliases={i: i for i in range(2 * n)},
        compiler_params=pltpu.CompilerParams(has_side_effects=pltpu.SideEffectType.DATAFLOW_SIDE_EFFECTING),
    )(*srcs, *lands, *send_sems, *recv_sems, after)
    return outs[n:]


def _tn_matmul(a, b, name, tk, tm, tn=512):
    n_tok, k_dim = a.shape
    m_dim = b.shape[1]
    grid = (k_dim // tk, m_dim // tm, n_tok // tn)

    def body(a_ref, b_ref, o_ref):
        @pl.when(pl.program_id(2) == 0)
        def _():
            o_ref[...] = jnp.zeros_like(o_ref)
        o_ref[...] += _dot_tn(a_ref[...].astype(BF16), b_ref[...].astype(BF16))

    return pl.pallas_call(
        body, name=name, grid=grid,
        in_specs=[pl.BlockSpec((tn, tk), lambda i, j, k: (k, i)), pl.BlockSpec((tn, tm), lambda i, j, k: (k, j))],
        out_specs=pl.BlockSpec((tk, tm), lambda i, j, k: (i, j)),
        out_shape=jax.ShapeDtypeStruct((k_dim, m_dim), F32),
        compiler_params=_cparams("parallel", "parallel", "arbitrary"),
    )(a, b)


def _adam_math(g, w, m, v):
    m = ADAM_B1 * m + (1.0 - ADAM_B1) * g
    v = ADAM_B2 * v + (1.0 - ADAM_B2) * (g * g)
    m_hat = m / (1.0 - ADAM_B1 ** ADAM_STEP)
    v_hat = v / (1.0 - ADAM_B2 ** ADAM_STEP)
    delta = -ADAM_LR * (m_hat / (jnp.sqrt(v_hat) + ADAM_EPS) + ADAM_WD * w)
    return delta, m, v


def _adam_sharded(land, w, m, v, name, tr):
    r, c = w.shape

    def body(l_ref, w_ref, m_ref, v_ref, g_ref, d_ref, nm_ref, nv_ref):
        g = l_ref[0].astype(F32)
        for s in range(1, N_DEV):
            g = g + l_ref[s].astype(F32)
        d, nm, nv = _adam_math(g, w_ref[...], m_ref[...], v_ref[...])
        g_ref[...] = g
        d_ref[...] = d
        nm_ref[...] = nm
        nv_ref[...] = nv

    spec = pl.BlockSpec((tr, c), lambda i: (i, 0))
    return pl.pallas_call(
        body, name=name, grid=(r // tr,),
        in_specs=[pl.BlockSpec((N_DEV, tr, c), lambda i: (0, i, 0)), spec, spec, spec],
        out_specs=(spec, spec, spec, spec),
        out_shape=tuple(jax.ShapeDtypeStruct((r, c), F32) for _ in range(4)),
        compiler_params=_cparams("parallel"),
    )(land, w, m, v)


def _sum_partials(parts, name):
    _, r, c = parts.shape

    def body(p_ref, o_ref):
        g = p_ref[0]
        for s in range(1, N_DEV):
            g = g + p_ref[s]
        o_ref[...] = g

    return pl.pallas_call(body, name=name, out_shape=jax.ShapeDtypeStruct((r, c), F32),
                          compiler_params=pltpu.CompilerParams(vmem_limit_bytes=VMEM_LIMIT))(parts)


def _adam_replicated(gs, ws, ms, vs, name):
    k = len(ws)

    def body(*refs):
        outs = refs[4 * k:]
        for i in range(k):
            d, nm, nv = _adam_math(refs[i][...], refs[k + i][...], refs[2 * k + i][...], refs[3 * k + i][...])
            outs[i][...] = d
            outs[k + i][...] = nm
            outs[2 * k + i][...] = nv

    outs = pl.pallas_call(body, name=name, out_shape=tuple(jax.ShapeDtypeStruct(w.shape, F32) for w in ws) * 3,
                          compiler_params=pltpu.CompilerParams(vmem_limit_bytes=VMEM_LIMIT))(*gs, *ws, *ms, *vs)
    return outs[:k], outs[k:2 * k], outs[2 * k:]


def _inproj_fwd(x, g, w_in, avg, qg, kg, tm=512):
    n = x.shape[0]

    def body(x_ref, g_ref, w_ref, a_ref, qg_ref, kg_ref, hn_ref, u_ref, qkv_ref, raw_ref, fl_ref):
        hn = _rms(x_ref[...], g_ref[...]).astype(BF16)
        hn_ref[...] = hn
        proj = _dot(hn, w_ref[...])
        u_ref[...] = proj[:, 0:512]
        q = proj[:, 512:1024]
        k = proj[:, 1024:1536]
        raw_ref[:, 0:512] = q
        raw_ref[:, 512:1024] = k
        qkv_ref[:, 0:512] = _headnorm(q, a_ref[...], qg_ref[...])
        qkv_ref[:, 512:1024] = _headnorm(k, a_ref[...], kg_ref[...])
        qkv_ref[:, 1024:1536] = proj[:, 1536:2048]
        fl_ref[...] = proj[:, 2048:D_IN_PAD]

    row = lambda w: pl.BlockSpec((tm, w), lambda i: (i, 0))
    full = lambda a: pl.BlockSpec(a.shape, lambda i: (0,) * a.ndim)
    return pl.pallas_call(
        body, name="inproj_fwd", grid=(n // tm,),
        in_specs=[row(D_MODEL), full(g), full(w_in), full(avg), full(qg), full(kg)],
        out_specs=(row(D_MODEL), row(512), row(1536), row(1024), row(LANES)),
        out_shape=(jax.ShapeDtypeStruct((n, D_MODEL), BF16), jax.ShapeDtypeStruct((n, 512), F32),
                   jax.ShapeDtypeStruct((n, 1536), F32), jax.ShapeDtypeStruct((n, 1024), F32),
                   jax.ShapeDtypeStruct((n, LANES), F32)),
        compiler_params=_cparams("parallel"),
    )(x, g, w_in, avg, qg, kg)


def _inproj_bwd(x, g, w_in, avg, qg, kg, raw, du, dqn, dkn, dv, dfl, dres, tm=512):
    n = x.shape[0]

    def body(x_ref, g_ref, w_ref, a_ref, qg_ref, kg_ref, raw_ref, du_ref, dqn_ref, dkn_ref, dv_ref, dfl_ref, dres_ref,
             dx_ref, dproj_ref, dg_ref, dqg_ref, dkg_ref):
        @pl.when(pl.program_id(0) == 0)
        def _():
            dg_ref[...] = jnp.zeros_like(dg_ref)
            dqg_ref[...] = jnp.zeros_like(dqg_ref)
            dkg_ref[...] = jnp.zeros_like(dkg_ref)
        avg_m = a_ref[...]
        _, vjp_q = jax.vjp(lambda q, gg: _headnorm(q, avg_m, gg), raw_ref[:, 0:512], qg_ref[...])
        dq, dqg = vjp_q(dqn_ref[...])
        _, vjp_k = jax.vjp(lambda k, gg: _headnorm(k, avg_m, gg), raw_ref[:, 512:1024], kg_ref[...])
        dk, dkg = vjp_k(dkn_ref[...])
        dproj = jnp.concatenate([du_ref[...], dq, dk, dv_ref[...], dfl_ref[...]], axis=1).astype(BF16)
        dproj_ref[...] = dproj
        dhn = _dot_nt(dproj, w_ref[...])
        _, vjp_x = jax.vjp(_rms, x_ref[...], g_ref[...])
        dxn, dg = vjp_x(dhn)
        dx_ref[...] = dxn + dres_ref[...]
        dg_ref[...] += dg
        dqg_ref[...] += dqg
        dkg_ref[...] += dkg

    row = lambda w: pl.BlockSpec((tm, w), lambda i: (i, 0))
    full = lambda a: pl.BlockSpec(a.shape, lambda i: (0,) * a.ndim)
    vec = lambda w: pl.BlockSpec((1, w), lambda i: (0, 0))
    return pl.pallas_call(
        body, name="inproj_bwd", grid=(n // tm,),
        in_specs=[row(D_MODEL), full(g), full(w_in), full(avg), full(qg), full(kg), row(1024), row(512), row(512),
                  row(512), row(512), row(LANES), row(D_MODEL)],
        out_specs=(row(D_MODEL), row(D_IN_PAD), vec(D_MODEL), vec(512), vec(512)),
        out_shape=(jax.ShapeDtypeStruct((n, D_MODEL), F32), jax.ShapeDtypeStruct((n, D_IN_PAD), BF16),
                   jax.ShapeDtypeStruct((1, D_MODEL), F32), jax.ShapeDtypeStruct((1, 512), F32),
                   jax.ShapeDtypeStruct((1, 512), F32)),
        compiler_params=_cparams("arbitrary"),
    )(x, g, w_in, avg, qg, kg, raw, du, dqn, dkn, dv, dfl, dres)


def _glu_fwd(yc, wg, bg, tm=512):
    n = yc.shape[0]

    def body(yc_ref, w_ref, b_ref, ys_ref):
        gl = jax.nn.gelu(yc_ref[...])
        z = _dot(gl.astype(BF16), w_ref[...]) + b_ref[...]
        ys_ref[...] = gl * jax.nn.sigmoid(z)

    row = pl.BlockSpec((tm, 512), lambda i: (i, 0))
    full = lambda a: pl.BlockSpec(a.shape, lambda i: (0,) * a.ndim)
    return pl.pallas_call(
        body, name="glu_fwd", grid=(n // tm,), in_specs=[row, full(wg), full(bg)], out_specs=row,
        out_shape=jax.ShapeDtypeStruct((n, 512), F32), compiler_params=_cparams("parallel"),
    )(yc, wg, bg)


def _glu_bwd(yc, dys, wg, bg, tm=512):
    n = yc.shape[0]

    def body(yc_ref, dys_ref, w_ref, b_ref, dyc_ref, gl_ref, dz_ref, db_ref):
        @pl.when(pl.program_id(0) == 0)
        def _():
            db_ref[...] = jnp.zeros_like(db_ref)
        gl, vjp_gelu = jax.vjp(jax.nn.gelu, yc_ref[...])
        glb = gl.astype(BF16)
        z = _dot(glb, w_ref[...]) + b_ref[...]
        s = jax.nn.sigmoid(z)
        dys = dys_ref[...]
        dz = dys * gl * s * (1.0 - s)
        dzb = dz.astype(BF16)
        dgl = dys * s + _dot_nt(dzb, w_ref[...])
        dyc_ref[...] = vjp_gelu(dgl)[0]
        gl_ref[...] = glb
        dz_ref[...] = dzb
        db_ref[...] += jnp.sum(dz, axis=0, keepdims=True)

    row = pl.BlockSpec((tm, 512), lambda i: (i, 0))
    full = lambda a: pl.BlockSpec(a.shape, lambda i: (0,) * a.ndim)
    return pl.pallas_call(
        body, name="glu_bwd", grid=(n // tm,), in_specs=[row, row, full(wg), full(bg)],
        out_specs=(row, row, row, pl.BlockSpec((1, 512), lambda i: (0, 0))),
        out_shape=(jax.ShapeDtypeStruct((n, 512), F32), jax.ShapeDtypeStruct((n, 512), BF16),
                   jax.ShapeDtypeStruct((n, 512), BF16), jax.ShapeDtypeStruct((1, 512), F32)),
        compiler_params=_cparams("arbitrary"),
    )(yc, dys, wg, bg)


def _mix_fwd(x, ys, ya, gs, ga, wout, gf, tm=512):
    n = x.shape[0]

    def body(x_ref, ys_ref, ya_ref, gs_ref, ga_ref, w_ref, gf_ref, h1_ref, hn2_ref, mixed_ref):
        mixed = jnp.concatenate([_rms(ys_ref[...], gs_ref[...]), _rms(ya_ref[...], ga_ref[...])], axis=1).astype(BF16)
        mixed_ref[...] = mixed
        h1 = x_ref[...] + _dot(mixed, w_ref[...])
        h1_ref[...] = h1
        hn2_ref[...] = _rms(h1, gf_ref[...]).astype(BF16)

    row = lambda w: pl.BlockSpec((tm, w), lambda i: (i, 0))
    full = lambda a: pl.BlockSpec(a.shape, lambda i: (0,) * a.ndim)
    return pl.pallas_call(
        body, name="mix_fwd", grid=(n // tm,),
        in_specs=[row(D_MODEL), row(512), row(512), full(gs), full(ga), full(wout), full(gf)],
        out_specs=(row(D_MODEL), row(D_MODEL), row(D_MODEL)),
        out_shape=(jax.ShapeDtypeStruct((n, D_MODEL), F32), jax.ShapeDtypeStruct((n, D_MODEL), BF16),
                   jax.ShapeDtypeStruct((n, D_MODEL), BF16)),
        compiler_params=_cparams("parallel"),
    )(x, ys, ya, gs, ga, wout, gf)


def _mix_bwd(dy, dhn2_parts, h1, ys, ya, gs, ga, wout, gf, tm=512):
    n = dy.shape[0]
    n_parts = dhn2_parts.shape[0]

    def body(dy_ref, dp_ref, h1_ref, ys_ref, ya_ref, gs_ref, ga_ref, w_ref, gf_ref,
             dh1_ref, dys_ref, dya_ref, dgs_ref, dga_ref, dgf_ref):
        @pl.when(pl.program_id(0) == 0)
        def _():
            dgs_ref[...] = jnp.zeros_like(dgs_ref)
            dga_ref[...] = jnp.zeros_like(dga_ref)
            dgf_ref[...] = jnp.zeros_like(dgf_ref)
        dhn2 = dp_ref[0]
        for p in range(1, n_parts):
            dhn2 = dhn2 + dp_ref[p]
        _, vjp_f = jax.vjp(_rms, h1_ref[...], gf_ref[...])
        dh1n, dgf = vjp_f(dhn2)
        dh1 = dy_ref[...] + dh1n
        dh1_ref[...] = dh1
        dmixed = _dot_nt(dh1.astype(BF16), w_ref[...])
        _, vjp_s = jax.vjp(_rms, ys_ref[...], gs_ref[...])
        dys, dgs = vjp_s(dmixed[:, 0:512])
        _, vjp_a = jax.vjp(_rms, ya_ref[...], ga_ref[...])
        dya, dga = vjp_a(dmixed[:, 512:1024])
        dys_ref[...] = dys
        dya_ref[...] = dya
        dgs_ref[...] += dgs
        dga_ref[...] += dga
        dgf_ref[...] += dgf

    row = lambda w: pl.BlockSpec((tm, w), lambda i: (i, 0))
    full = lambda a: pl.BlockSpec(a.shape, lambda i: (0,) * a.ndim)
    vec = lambda w: pl.BlockSpec((1, w), lambda i: (0, 0))
    return pl.pallas_call(
        body, name="mix_bwd", grid=(n // tm,),
        in_specs=[row(D_MODEL), pl.BlockSpec((n_parts, tm, D_MODEL), lambda i: (0, i, 0)), row(D_MODEL), row(512),
                  row(512), full(gs), full(ga), full(wout), full(gf)],
        out_specs=(row(D_MODEL), row(512), row(512), vec(512), vec(512), vec(D_MODEL)),
        out_shape=(jax.ShapeDtypeStruct((n, D_MODEL), F32), jax.ShapeDtypeStruct((n, 512), F32),
                   jax.ShapeDtypeStruct((n, 512), F32), jax.ShapeDtypeStruct((1, 512), F32),
                   jax.ShapeDtypeStruct((1, 512), F32), jax.ShapeDtypeStruct((1, D_MODEL), F32)),
        compiler_params=_cparams("arbitrary"),
    )(dy, dhn2_parts, h1, ys, ya, gs, ga, wout, gf)


HALO = 16


def _conv3(ue, cw):
    return cw[2:3] * ue + cw[1:2] * pltpu.roll(ue, 1, 0) + cw[0:1] * pltpu.roll(ue, 2, 0) + cw[3:4]


def _ffn_fwd(hn2, h1, target, wup_g, wup_v, cw_g, cw_v, wdown, seq_len, tm=512, fb=256):
    n = hn2.shape[0]
    nj = D_FF_PAD // fb
    hb = tm // HALO

    def body(hn_ref, halo_ref, h1_ref, tgt_ref, wg_ref, wv_ref, cg_ref, cv_ref, wd_ref,
             ug_ref, uv_ref, dy_ref, loss_ref, acc):
        i, j = pl.program_id(0), pl.program_id(1)
        seq_start = (i * tm) % seq_len == 0
        halo = halo_ref[...]
        halo = jnp.where(seq_start, jnp.zeros_like(halo), halo)
        he = jnp.concatenate([halo, hn_ref[...]], axis=0)
        ueg = _dot(he, wg_ref[...])
        uev = _dot(he, wv_ref[...])
        ug_ref[...] = ueg[HALO:]
        uv_ref[...] = uev[HALO:]
        cg = _conv3(ueg, cg_ref[...])[HALO:]
        cv = _conv3(uev, cv_ref[...])[HALO:]
        act = (jax.nn.silu(cg) * cv).astype(BF16)
        part = _dot(act, wd_ref[...])

        @pl.when(j == 0)
        def _():
            acc[...] = part

        @pl.when(j > 0)
        def _():
            acc[...] += part

        @pl.when(j == nj - 1)
        def _():
            err = h1_ref[...] + acc[...] - tgt_ref[...]
            dy_ref[...] = err * (1.0 / D_MODEL)
            loss_ref[0] = jnp.sum(err * err, axis=0, keepdims=True)

    row = pl.BlockSpec((tm, D_MODEL), lambda i, j: (i, 0))
    return pl.pallas_call(
        body, name="ffn_fwd", grid=(n // tm, nj),
        in_specs=[row, pl.BlockSpec((HALO, D_MODEL), lambda i, j: (jnp.maximum(i * hb - 1, 0), 0)), row, row,
                  pl.BlockSpec((D_MODEL, fb), lambda i, j: (0, j)), pl.BlockSpec((D_MODEL, fb), lambda i, j: (0, j)),
                  pl.BlockSpec((8, fb), lambda i, j: (0, j)), pl.BlockSpec((8, fb), lambda i, j: (0, j)),
                  pl.BlockSpec((fb, D_MODEL), lambda i, j: (j, 0))],
        out_specs=(pl.BlockSpec((tm, fb), lambda i, j: (i, j)), pl.BlockSpec((tm, fb), lambda i, j: (i, j)), row,
                   pl.BlockSpec((1, 1, D_MODEL), lambda i, j: (i, 0, 0))),
        out_shape=(jax.ShapeDtypeStruct((n, D_FF_PAD), F32), jax.ShapeDtypeStruct((n, D_FF_PAD), F32),
                   jax.ShapeDtypeStruct((n, D_MODEL), F32), jax.ShapeDtypeStruct((n // tm, 1, D_MODEL), F32)),
        scratch_shapes=[pltpu.VMEM((tm, D_MODEL), F32)],
        compiler_params=_cparams("parallel", "arbitrary"),
    )(hn2, hn2, h1, target, wup_g, wup_v, cw_g, cw_v, wdown)


def _ffn_bwd(dy, ug, uv, wup_g, wup_v, cw_g, cw_v, wdown, seq_len, tm=512, fb=256):
    n = dy.shape[0]
    nj = D_FF_PAD // fb
    hb = tm // HALO
    last_hb = n // HALO - 1
    rows = tm + HALO

    def body(dy_ref, dyn_ref, ugp_ref, ugm_ref, ugn_ref, uvp_ref, uvm_ref, uvn_ref, wg_ref, wv_ref, cg_ref, cv_ref,
             wd_ref, dug_ref, duv_ref, act_ref, dhn_ref, dcg_ref, dcv_ref, acc):
        i, j = pl.program_id(0), pl.program_id(1)
        seq_start = (i * tm) % seq_len == 0
        seq_end = ((i + 1) * tm) % seq_len == 0
        dyn = dyn_ref[...]
        dyn = jnp.where(seq_end, jnp.zeros_like(dyn), dyn)
        d_out = jnp.concatenate([dy_ref[...], dyn], axis=0).astype(BF16)
        d_act = _dot_nt(d_out, wd_ref[...])

        def pre_act(up_ref, um_ref, un_ref, cw):
            up = up_ref[...]
            up = jnp.where(seq_start, jnp.zeros_like(up), up)
            ue = jnp.concatenate([up, um_ref[...], un_ref[...]], axis=0)
            return ue, _conv3(ue, cw)[HALO:]

        cwg, cwv = cg_ref[...], cv_ref[...]
        ueg, cge = pre_act(ugp_ref, ugm_ref, ugn_ref, cwg)
        uev, cve = pre_act(uvp_ref, uvm_ref, uvn_ref, cwv)
        act, vjp_act = jax.vjp(lambda g, v: jax.nn.silu(g) * v, cge, cve)
        dcge, dcve = vjp_act(d_act)
        act_ref[...] = act[:tm].astype(BF16)

        def conv_t(dc, cw):
            return (cw[2:3] * dc + cw[1:2] * pltpu.roll(dc, rows - 1, 0) + cw[0:1] * pltpu.roll(dc, rows - 2, 0))[:tm]

        dug = conv_t(dcge, cwg).astype(BF16)
        duv = conv_t(dcve, cwv).astype(BF16)
        dug_ref[...] = dug
        duv_ref[...] = duv
        part = _dot_nt(dug, wg_ref[...]) + _dot_nt(duv, wv_ref[...])

        @pl.when(j == 0)
        def _():
            acc[...] = part

        @pl.when(j > 0)
        def _():
            acc[...] += part

        @pl.when(j == nj - 1)
        def _():
            dhn_ref[...] = acc[...]

        def cw_grad(dc, ue):
            dcm = dc[:tm]
            taps = [jnp.sum(dcm * pltpu.roll(ue, 2 - k, 0)[HALO:HALO + tm], axis=0, keepdims=True) for k in (0, 1)]
            taps.append(jnp.sum(dcm * ue[HALO:HALO + tm], axis=0, keepdims=True))
            taps.append(jnp.sum(dcm, axis=0, keepdims=True))
            return jnp.concatenate(taps + [jnp.zeros((4, fb), F32)], axis=0)

        @pl.when(i == 0)
        def _():
            dcg_ref[j] = jnp.zeros((8, fb), F32)
            dcv_ref[j] = jnp.zeros((8, fb), F32)

        dcg_ref[j] += cw_grad(dcge, ueg)
        dcv_ref[j] += cw_grad(dcve, uev)

    row = pl.BlockSpec((tm, D_MODEL), lambda i, j: (i, 0))
    u_prev = pl.BlockSpec((HALO, fb), lambda i, j: (jnp.maximum(i * hb - 1, 0), j))
    u_main = pl.BlockSpec((tm, fb), lambda i, j: (i, j))
    u_next = pl.BlockSpec((HALO, fb), lambda i, j: (jnp.minimum((i + 1) * hb, last_hb), j))
    w_col = pl.BlockSpec((D_MODEL, fb), lambda i, j: (0, j))
    c_col = pl.BlockSpec((8, fb), lambda i, j: (0, j))
    dc_spec = pl.BlockSpec((nj, 8, fb), lambda i, j: (0, 0, 0))
    return pl.pallas_call(
        body, name="ffn_bwd", grid=(n // tm, nj),
        in_specs=[row, pl.BlockSpec((HALO, D_MODEL), lambda i, j: (jnp.minimum((i + 1) * hb, last_hb), 0)),
                  u_prev, u_main, u_next, u_prev, u_main, u_next, w_col, w_col, c_col, c_col,
                  pl.BlockSpec((fb, D_MODEL), lambda i, j: (j, 0))],
        out_specs=(u_main, u_main, u_main, row, dc_spec, dc_spec),
        out_shape=(jax.ShapeDtypeStruct((n, D_FF_PAD), BF16), jax.ShapeDtypeStruct((n, D_FF_PAD), BF16),
                   jax.ShapeDtypeStruct((n, D_FF_PAD), BF16), jax.ShapeDtypeStruct((n, D_MODEL), F32),
                   jax.ShapeDtypeStruct((nj, 8, fb), F32), jax.ShapeDtypeStruct((nj, 8, fb), F32)),
        scratch_shapes=[pltpu.VMEM((tm, D_MODEL), F32)],
        compiler_params=_cparams("arbitrary", "arbitrary"),
    )(dy, dy, ug, ug, ug, uv, uv, uv, wup_g, wup_v, cw_g, cw_v, wdown)


def _s5_param_fn(lr, li, ldt, br, bi):
    dt = jnp.exp(ldt)
    mag = jnp.exp(lr * dt)
    ab_re = mag * jnp.cos(li * dt)
    ab_im = mag * jnp.sin(li * dt)
    nr = ab_re - 1.0
    ni = ab_im
    den = lr * lr + li * li
    q_re = (nr * lr + ni * li) / den
    q_im = (ni * lr - nr * li) / den
    bb_re = q_re * br - q_im * bi
    bb_im = q_re * bi + q_im * br
    return ab_re, ab_im, bb_re, bb_im


def _s5_param_fwd(lr, li, ldt, br, bi):
    def body(lr_ref, li_ref, ldt_ref, br_ref, bi_ref, ar_ref, ai_ref, bbr_ref, bbi_ref):
        ar, ai, bbr, bbi = _s5_param_fn(lr_ref[...], li_ref[...], ldt_ref[...], br_ref[...], bi_ref[...])
        ar_ref[...] = ar
        ai_ref[...] = ai
        bbr_ref[...] = bbr
        bbi_ref[...] = bbi

    return pl.pallas_call(
        body, name="s5_param_fwd",
        out_shape=(jax.ShapeDtypeStruct(lr.shape, F32), jax.ShapeDtypeStruct(lr.shape, F32),
                   jax.ShapeDtypeStruct(br.shape, F32), jax.ShapeDtypeStruct(br.shape, F32)),
    )(lr, li, ldt, br, bi)


def _s5_param_bwd(lr, li, ldt, br, bi, dar, dai, dbbr, dbbi):
    def body(lr_ref, li_ref, ldt_ref, br_ref, bi_ref, dar_ref, dai_ref, dbbr_ref, dbbi_ref,
             dlr_ref, dli_ref, dldt_ref, dbr_ref, dbi_ref):
        _, vjp = jax.vjp(_s5_param_fn, lr_ref[...], li_ref[...], ldt_ref[...], br_ref[...], bi_ref[...])
        dlr, dli, dldt, dbr, dbi = vjp((dar_ref[...], dai_ref[...], dbbr_ref[...], dbbi_ref[...]))
        dlr_ref[...] = dlr
        dli_ref[...] = dli
        dldt_ref[...] = dldt
        dbr_ref[...] = dbr
        dbi_ref[...] = dbi

    return pl.pallas_call(
        body, name="s5_param_bwd",
        out_shape=(jax.ShapeDtypeStruct(lr.shape, F32), jax.ShapeDtypeStruct(lr.shape, F32),
                   jax.ShapeDtypeStruct(ldt.shape, F32), jax.ShapeDtypeStruct(br.shape, F32),
                   jax.ShapeDtypeStruct(br.shape, F32)),
    )(lr, li, ldt, br, bi, dar, dai, dbbr, dbbi)


S5_CHUNK = 256
S5_STATES = 512
S5_BLOCKS = 4


def _cpow_rows(ar, ai, count):
    rs, im = [ar], [ai]
    for _ in range(count - 1):
        pr, pi = rs[-1], im[-1]
        rs.append(pr * ar - pi * ai)
        im.append(pr * ai + pi * ar)
    return rs, im


def _scan_in_groups(vr, vi, pr, pi, rm, reverse):
    n = vr.shape[0]
    for k in (1, 2, 4):
        if reverse:
            sr, si, keep = pltpu.roll(vr, n - k, 0), pltpu.roll(vi, n - k, 0), rm < SUBLANES - k
        else:
            sr, si, keep = pltpu.roll(vr, k, 0), pltpu.roll(vi, k, 0), rm >= k
        sr = jnp.where(keep, sr, 0.0)
        si = jnp.where(keep, si, 0.0)
        kr, ki = pr[k - 1], pi[k - 1]
        vr, vi = vr + kr * sr - ki * si, vi + kr * si + ki * sr
    return vr, vi


def _carry_over_groups(xr_s, xi_s, wr, wi, c0r, c0i, reverse):
    groups = xr_s.shape[0] // SUBLANES
    pick = 0 if reverse else SUBLANES - 1

    def step(q, carry):
        cr, ci = carry
        r = groups - 1 - q if reverse else q
        o = pl.multiple_of(r * SUBLANES, SUBLANES)
        vr = xr_s[pl.ds(o, SUBLANES), :]
        vi = xi_s[pl.ds(o, SUBLANES), :]
        nr = vr + wr * cr - wi * ci
        ni = vi + wr * ci + wi * cr
        xr_s[pl.ds(o, SUBLANES), :] = nr
        xi_s[pl.ds(o, SUBLANES), :] = ni
        return (jnp.broadcast_to(nr[pick:pick + 1], nr.shape), jnp.broadcast_to(ni[pick:pick + 1], ni.shape))

    return lax.fori_loop(0, groups, step, (c0r, c0i))


def _s5_state_scan(u_b, bbr, bbi, pr, pi, rm, xr_s, xi_s, c0r, c0i):
    bur = _dot(u_b, bbr)
    bui = _dot(u_b, bbi)
    bur, bui = _scan_in_groups(bur, bui, pr, pi, rm, False)
    xr_s[...] = bur
    xi_s[...] = bui
    w8r = jnp.concatenate(pr, axis=0)
    w8i = jnp.concatenate(pi, axis=0)
    return _carry_over_groups(xr_s, xi_s, w8r, w8i, c0r, c0i, False)


def _s5_fwd(u, a_re, a_im, bbr, bbi, cr, ci, d_skip, n_seq):
    n = u.shape[0]
    seq_len = n // n_seq
    nt = seq_len // S5_CHUNK
    tc = S5_CHUNK

    def body(u_ref, ar_ref, ai_ref, bbr_ref, bbi_ref, cr_ref, ci_ref, d_ref, y_ref, str_ref, sti_ref,
             xr_s, xi_s, car_r, car_i):
        t = pl.program_id(2)

        @pl.when(t == 0)
        def _():
            car_r[...] = jnp.zeros_like(car_r)
            car_i[...] = jnp.zeros_like(car_i)
        pr, pi = _cpow_rows(ar_ref[0], ai_ref[0], SUBLANES)
        rm = lax.broadcasted_iota(jnp.int32, (tc, S5_STATES), 0) & (SUBLANES - 1)
        str_ref[0, 0] = car_r[...]
        sti_ref[0, 0] = car_i[...]
        u_t = u_ref[...]
        cfr, cfi = _s5_state_scan(u_t.astype(BF16), bbr_ref[0], bbi_ref[0], pr, pi, rm, xr_s, xi_s,
                                  car_r[...], car_i[...])
        car_r[...] = cfr
        car_i[...] = cfi
        y = _dot(xr_s[...].astype(BF16), cr_ref[0]) - _dot(xi_s[...].astype(BF16), ci_ref[0])
        y_ref[...] = y + d_ref[...] * u_t

    u_spec = pl.BlockSpec((tc, LANES), lambda cb, b, t: (b * nt + t, cb))
    a_spec = pl.BlockSpec((1, 1, S5_STATES), lambda cb, b, t: (cb, 0, 0))
    bb_spec = pl.BlockSpec((1, LANES, S5_STATES), lambda cb, b, t: (cb, 0, 0))
    c_spec = pl.BlockSpec((1, S5_STATES, LANES), lambda cb, b, t: (cb, 0, 0))
    st_spec = pl.BlockSpec((1, 1, SUBLANES, S5_STATES), lambda cb, b, t: (cb, b * nt + t, 0, 0))
    st_shape = jax.ShapeDtypeStruct((S5_BLOCKS, n_seq * nt, SUBLANES, S5_STATES), F32)
    return pl.pallas_call(
        body, name="s5_fwd", grid=(S5_BLOCKS, n_seq, nt),
        in_specs=[u_spec, a_spec, a_spec, bb_spec, bb_spec, c_spec, c_spec,
                  pl.BlockSpec((1, LANES), lambda cb, b, t: (0, cb))],
        out_specs=(u_spec, st_spec, st_spec),
        out_shape=(jax.ShapeDtypeStruct((n, D_SSM), F32), st_shape, st_shape),
        scratch_shapes=[pltpu.VMEM((tc, S5_STATES), F32), pltpu.VMEM((tc, S5_STATES), F32),
                        pltpu.VMEM((SUBLANES, S5_STATES), F32), pltpu.VMEM((SUBLANES, S5_STATES), F32)],
        compiler_params=_cparams("parallel", "arbitrary", "arbitrary"),
    )(u, a_re, a_im, bbr, bbi, cr, ci, d_skip)


def _s5_bwd(u, dy, st_r, st_i, a_re, a_im, bbr, bbi, cr, ci, d_skip, n_seq):
    n = u.shape[0]
    seq_len = n // n_seq
    nt = seq_len // S5_CHUNK
    tc = S5_CHUNK

    def body(u_ref, dy_ref, str_ref, sti_ref, ar_ref, ai_ref, bbr_ref, bbi_ref, cr_ref, ci_ref, d_ref,
             du_ref, dbbr_ref, dbbi_ref, dcr_ref, dci_ref, dar_ref, dai_ref, dd_ref,
             xr_s, xi_s, gr_s, gi_s, car_r, car_i):
        b, t = pl.program_id(1), pl.program_id(2)

        @pl.when((b == 0) & (t == 0))
        def _():
            for ref in (dbbr_ref, dbbi_ref, dcr_ref, dci_ref, dar_ref, dai_ref, dd_ref):
                ref[...] = jnp.zeros_like(ref)

        @pl.when(t == 0)
        def _():
            car_r[...] = jnp.zeros_like(car_r)
            car_i[...] = jnp.zeros_like(car_i)
        ar, ai = ar_ref[0], ai_ref[0]
        pr, pi = _cpow_rows(ar, ai, SUBLANES)
        row = lax.broadcasted_iota(jnp.int32, (tc, S5_STATES), 0)
        rm = row & (SUBLANES - 1)
        u_t = u_ref[...]
        u_b = u_t.astype(BF16)
        dy_t = dy_ref[...]
        dy_b = dy_t.astype(BF16)
        s0r, s0i = str_ref[0, 0], sti_ref[0, 0]
        _s5_state_scan(u_b, bbr_ref[0], bbi_ref[0], pr, pi, rm, xr_s, xi_s, s0r, s0i)
        xr, xi = xr_s[...], xi_s[...]
        gr = _dot_nt(dy_b, cr_ref[0])
        gi = -_dot_nt(dy_b, ci_ref[0])
        npi = [-v for v in pi]
        gr, gi = _scan_in_groups(gr, gi, pr, npi, rm, True)
        gr_s[...] = gr
        gi_s[...] = gi
        w8r = jnp.concatenate(pr[::-1], axis=0)
        w8i = jnp.concatenate(npi[::-1], axis=0)
        cfr, cfi = _carry_over_groups(gr_s, gi_s, w8r, w8i, car_r[...], car_i[...], True)
        car_r[...] = cfr
        car_i[...] = cfi
        gr, gi = gr_s[...], gi_s[...]
        gr_b, gi_b = gr.astype(BF16), gi.astype(BF16)
        du_ref[...] = _dot_nt(gr_b, bbr_ref[0]) + _dot_nt(gi_b, bbi_ref[0]) + d_ref[...] * dy_t
        dbbr_ref[0] += _dot_tn(u_b, gr_b)
        dbbi_ref[0] += _dot_tn(u_b, gi_b)
        dcr_ref[0] += _dot_tn(xr.astype(BF16), dy_b)
        dci_ref[0] -= _dot_tn(xi.astype(BF16), dy_b)
        dd_ref[0] += jnp.sum((dy_t * u_t).reshape(tc // SUBLANES, SUBLANES, LANES), axis=0)
        first = row == 0
        xpr = jnp.where(first, jnp.broadcast_to(s0r[0:1], xr.shape), pltpu.roll(xr, 1, 0))
        xpi = jnp.where(first, jnp.broadcast_to(s0i[0:1], xi.shape), pltpu.roll(xi, 1, 0))
        shp = (tc // SUBLANES, SUBLANES, S5_STATES)
        dar_ref[0] += jnp.sum((gr * xpr + gi * xpi).reshape(shp), axis=0)
        dai_ref[0] += jnp.sum((gi * xpr - gr * xpi).reshape(shp), axis=0)

    u_spec = pl.BlockSpec((tc, LANES), lambda cb, b, t: (b * nt + nt - 1 - t, cb))
    a_spec = pl.BlockSpec((1, 1, S5_STATES), lambda cb, b, t: (cb, 0, 0))
    bb_spec = pl.BlockSpec((1, LANES, S5_STATES), lambda cb, b, t: (cb, 0, 0))
    c_spec = pl.BlockSpec((1, S5_STATES, LANES), lambda cb, b, t: (cb, 0, 0))
    st_spec = pl.BlockSpec((1, 1, SUBLANES, S5_STATES), lambda cb, b, t: (cb, b * nt + nt - 1 - t, 0, 0))
    da_spec = pl.BlockSpec((1, SUBLANES, S5_STATES), lambda cb, b, t: (cb, 0, 0))
    dd_spec = pl.BlockSpec((1, SUBLANES, LANES), lambda cb, b, t: (cb, 0, 0))
    big = pltpu.VMEM((tc, S5_STATES), F32)
    small = pltpu.VMEM((SUBLANES, S5_STATES), F32)
    return pl.pallas_call(
        body, name="s5_bwd", grid=(S5_BLOCKS, n_seq, nt),
        in_specs=[u_spec, u_spec, st_spec, st_spec, a_spec, a_spec, bb_spec, bb_spec, c_spec, c_spec,
                  pl.BlockSpec((1, LANES), lambda cb, b, t: (0, cb))],
        out_specs=(u_spec, bb_spec, bb_spec, c_spec, c_spec, da_spec, da_spec, dd_spec),
        out_shape=(jax.ShapeDtypeStruct((n, D_SSM), F32),
                   jax.ShapeDtypeStruct((S5_BLOCKS, LANES, S5_STATES), F32),
                   jax.ShapeDtypeStruct((S5_BLOCKS, LANES, S5_STATES), F32),
                   jax.ShapeDtypeStruct((S5_BLOCKS, S5_STATES, LANES), F32),
                   jax.ShapeDtypeStruct((S5_BLOCKS, S5_STATES, LANES), F32),
                   jax.ShapeDtypeStruct((S5_BLOCKS, SUBLANES, S5_STATES), F32),
                   jax.ShapeDtypeStruct((S5_BLOCKS, SUBLANES, S5_STATES), F32),
                   jax.ShapeDtypeStruct((S5_BLOCKS, SUBLANES, LANES), F32)),
        scratch_shapes=[big, big, big, big, small, small],
        compiler_params=_cparams("parallel", "arbitrary", "arbitrary"),
    )(u, dy, st_r, st_i, a_re, a_im, bbr, bbi, cr, ci, d_skip)


CUM_BLOCK = 128


def _tri(lower):
    r = lax.broadcasted_iota(jnp.int32, (CUM_BLOCK, CUM_BLOCK), 0)
    c = lax.broadcasted_iota(jnp.int32, (CUM_BLOCK, CUM_BLOCK), 1)
    return jnp.where(r >= c if lower else r <= c, 1.0, 0.0).astype(F32)


def _fprep_fwd(fl, bf, n_seq):
    n = fl.shape[0]
    seq_len = n // n_seq
    nb = seq_len // CUM_BLOCK

    def body(fl_ref, bf_ref, cum_ref):
        tril = _tri(True)
        carry = jnp.zeros((1, LANES), F32)
        for blk in range(nb):
            rows = slice(blk * CUM_BLOCK, (blk + 1) * CUM_BLOCK)
            lf = jax.nn.log_sigmoid(fl_ref[rows, :] + bf_ref[...])
            cs = jnp.dot(tril, lf, preferred_element_type=F32, precision=HIGHEST) + carry
            cum_ref[rows, :] = cs
            carry = cs[CUM_BLOCK - 1:CUM_BLOCK, :]

    spec = pl.BlockSpec((seq_len, LANES), lambda b: (b, 0))
    return pl.pallas_call(
        body, name="fprep_fwd", grid=(n_seq,), in_specs=[spec, pl.BlockSpec((1, LANES), lambda b: (0, 0))],
        out_specs=spec, out_shape=jax.ShapeDtypeStruct((n, LANES), F32), compiler_params=_cparams("parallel"),
    )(fl, bf)


def _fprep_bwd(dcum, fl, bf, n_seq):
    n = fl.shape[0]
    seq_len = n // n_seq
    nb = seq_len // CUM_BLOCK

    def body(dcum_ref, fl_ref, bf_ref, dfl_ref, dbf_ref):
        triu = _tri(False)
        lane = lax.broadcasted_iota(jnp.int32, (CUM_BLOCK, LANES), 1)
        carry = jnp.zeros((1, LANES), F32)
        total = jnp.zeros((1, LANES), F32)
        for blk in reversed(range(nb)):
            rows = slice(blk * CUM_BLOCK, (blk + 1) * CUM_BLOCK)
            rs = jnp.dot(triu, dcum_ref[rows, :], preferred_element_type=F32, precision=HIGHEST) + carry
            carry = rs[0:1, :]
            _, vjp = jax.vjp(jax.nn.log_sigmoid, fl_ref[rows, :] + bf_ref[...])
            dz = jnp.where(lane < N_HEADS, vjp(rs)[0], 0.0)
            dfl_ref[rows, :] = dz
            total = total + jnp.sum(dz, axis=0, keepdims=True)
        dbf_ref[0] = total

    spec = pl.BlockSpec((seq_len, LANES), lambda b: (b, 0))
    return pl.pallas_call(
        body, name="fprep_bwd", grid=(n_seq,), in_specs=[spec, spec, pl.BlockSpec((1, LANES), lambda b: (0, 0))],
        out_specs=(spec, pl.BlockSpec((1, 1, LANES), lambda b: (b, 0, 0))),
        out_shape=(jax.ShapeDtypeStruct((n, LANES), F32), jax.ShapeDtypeStruct((n_seq, 1, LANES), F32)),
        compiler_params=_cparams("parallel"),
    )(dcum, fl, bf)


ATT_TQ = 256
ATT_KSTEP = 512
ATT_SCALE = HEAD_DIM ** -0.5
NEG_BIG = -1e30


def _attn_scores(qe, kb, cq, ck, causal):
    s = _dot_nt(qe, kb) * ATT_SCALE + cq - ck
    return jnp.where(causal, s, NEG_BIG)


def _causal_mask(qi, kend):
    r = lax.broadcasted_iota(jnp.int32, (ATT_TQ, kend), 0) + qi * ATT_TQ
    c = lax.broadcasted_iota(jnp.int32, (ATT_TQ, kend), 1)
    return r >= c


def _attn_specs(n_seq, seq_len):
    nq = seq_len // ATT_TQ
    q_spec = pl.BlockSpec((ATT_TQ, LANES), lambda b, h, q: (b * nq + q, h))
    k_spec = pl.BlockSpec((seq_len, LANES), lambda b, h, q: (b, N_HEADS // 2 + h))
    v_spec = pl.BlockSpec((seq_len, LANES), lambda b, h, q: (b, N_HEADS + h))
    cq_spec = pl.BlockSpec((1, 2, ATT_TQ, 1), lambda b, h, q: (b, h, q, 0))
    ck_spec = pl.BlockSpec((1, 2, 1, seq_len), lambda b, h, q: (b, h, 0, 0))
    return nq, q_spec, k_spec, v_spec, cq_spec, ck_spec


def _head_selectors():
    head0 = lax.broadcasted_iota(jnp.int32, (1, LANES), 1) < HEAD_DIM
    return head0, (head0, jnp.logical_not(head0))


def _for_key_range(qi, seq_len, run):
    per = ATT_KSTEP // ATT_TQ
    for g in range(seq_len // ATT_KSTEP):
        pl.when(qi // per == g)(functools.partial(run, (g + 1) * ATT_KSTEP))


def _attn_fwd(qkv, cq, ck, n_seq):
    n = qkv.shape[0]
    seq_len = n // n_seq
    nq, q_spec, k_spec, v_spec, cq_spec, ck_spec = _attn_specs(n_seq, seq_len)

    def body(q_ref, k_ref, v_ref, cq_ref, ck_ref, o_ref, lse_ref):
        qi = pl.program_id(2)
        q2 = q_ref[...]
        head0, sels = _head_selectors()
        qe = [jnp.where(sel, q2, 0.0).astype(BF16) for sel in sels]

        def run(kend):
            kb = k_ref[0:kend, :].astype(BF16)
            vb = v_ref[0:kend, :].astype(BF16)
            causal = _causal_mask(qi, kend)
            outs = []
            for e in range(2):
                s = _attn_scores(qe[e], kb, cq_ref[0, e], ck_ref[0, e, :, 0:kend], causal)
                mx = jnp.max(s, axis=1, keepdims=True)
                p = jnp.exp(s - mx)
                den = jnp.sum(p, axis=1, keepdims=True)
                outs.append(_dot(p.astype(BF16), vb) / den)
                lse_ref[0, e] = mx + jnp.log(den)
            o_ref[...] = jnp.where(head0, outs[0], outs[1])

        _for_key_range(qi, seq_len, run)

    return pl.pallas_call(
        body, name="attn_fwd", grid=(n_seq, N_HEADS // 2, nq),
        in_specs=[q_spec, k_spec, v_spec, cq_spec, ck_spec],
        out_specs=(q_spec, cq_spec),
        out_shape=(jax.ShapeDtypeStruct((n, D_ATTN), F32), jax.ShapeDtypeStruct((n_seq, N_HEADS, seq_len, 1), F32)),
        compiler_params=_cparams("parallel", "parallel", "parallel"),
    )(qkv, qkv, qkv, cq, ck)


def _attn_bwd(qkv, cq, ck, o, do, lse, n_seq):
    n = qkv.shape[0]
    seq_len = n // n_seq
    nq, q_spec, k_spec, v_spec, cq_spec, ck_spec = _attn_specs(n_seq, seq_len)
    kv_out = pl.BlockSpec((seq_len, LANES), lambda b, h, q: (b, h))

    def body(q_ref, k_ref, v_ref, cq_ref, ck_ref, o_ref, do_ref, lse_ref, dq_ref, dk_ref, dv_ref, dcq_ref, dck_ref):
        qi = pl.program_id(2)

        @pl.when(qi == 0)
        def _():
            dk_ref[...] = jnp.zeros_like(dk_ref)
            dv_ref[...] = jnp.zeros_like(dv_ref)
            dck_ref[...] = jnp.zeros_like(dck_ref)
        q2 = q_ref[...]
        do2 = do_ref[...]
        o2 = o_ref[...]
        head0, sels = _head_selectors()
        qe = [jnp.where(sel, q2, 0.0).astype(BF16) for sel in sels]
        doe = [jnp.where(sel, do2, 0.0) for sel in sels]
        doe_b = [d.astype(BF16) for d in doe]
        delta = [jnp.sum(d * o2, axis=1, keepdims=True) for d in doe]

        def run(kend):
            kb = k_ref[0:kend, :].astype(BF16)
            vb = v_ref[0:kend, :].astype(BF16)
            causal = _causal_mask(qi, kend)
            dqs = []
            dk = jnp.zeros((kend, LANES), F32)
            dv = jnp.zeros((kend, LANES), F32)
            for e in range(2):
                s = _attn_scores(qe[e], kb, cq_ref[0, e], ck_ref[0, e, :, 0:kend], causal)
                p = jnp.exp(s - lse_ref[0, e])
                ds = p * (_dot_nt(doe_b[e], vb) - delta[e])
                ds_b = ds.astype(BF16)
                dqs.append(_dot(ds_b, kb))
                dk = dk + _dot_tn(ds_b, qe[e])
                dv = dv + _dot_tn(p.astype(BF16), doe_b[e])
                dcq_ref[0, e] = jnp.sum(ds, axis=1, keepdims=True)
                dck_ref[0, e, :, 0:kend] -= jnp.sum(ds, axis=0, keepdims=True)
            dk_ref[0:kend, :] += dk * ATT_SCALE
            dv_ref[0:kend, :] += dv
            dq_ref[...] = jnp.where(head0, dqs[0], dqs[1]) * ATT_SCALE

        _for_key_range(qi, seq_len, run)

    return pl.pallas_call(
        body, name="attn_bwd", grid=(n_seq, N_HEADS // 2, nq),
        in_specs=[q_spec, k_spec, v_spec, cq_spec, ck_spec, q_spec, q_spec, cq_spec],
        out_specs=(q_spec, kv_out, kv_out, cq_spec, ck_spec),
        out_shape=(jax.ShapeDtypeStruct((n, D_ATTN), F32), jax.ShapeDtypeStruct((n, D_ATTN), F32),
                   jax.ShapeDtypeStruct((n, D_ATTN), F32),
                   jax.ShapeDtypeStruct((n_seq, N_HEADS, seq_len, 1), F32),
                   jax.ShapeDtypeStruct((n_seq, N_HEADS, 1, seq_len), F32)),
        compiler_params=_cparams("parallel", "parallel", "arbitrary"),
    )(qkv, qkv, qkv, cq, ck, o, do, lse)


WEIGHT_NAMES = ("norm_mix", "w_in", "b_forget", "lam_re", "lam_im", "b_re", "b_im", "c_re", "c_im", "d_skip", "log_dt",
                "w_glu", "b_glu", "q_norm", "k_norm", "norm_out_ssm", "norm_out_attn", "w_out", "norm_ffn", "w_up",
                "conv_w", "conv_b", "w_down")
SHARDED = ("w_in", "w_glu", "w_out", "w_up", "conv_w", "w_down")
ADAM_ROWS = {"w_in": 256, "w_glu": 64, "w_out": 128, "w_up": 128, "conv_w": 3, "w_down": 344}
PACK_ROWS = SUBLANES * LANES
ROWS_DOWN = D_FF // N_DEV
ROWS_OUT = D_MODEL // N_DEV
ROWS_GLU = D_SSM * D_SSM // N_DEV // D_MODEL


def _pad_to(a, axis, size):
    pad = [(0, 0)] * a.ndim
    pad[axis] = (0, size - a.shape[axis])
    return jnp.pad(a, pad)


def _block_diag(t, transpose):
    t4 = t.reshape(S5_BLOCKS, 8, SSM_GROUP, SSM_STATE)
    eye = jnp.eye(8, dtype=t.dtype)
    if transpose:
        e = jnp.swapaxes(t4, 2, 3)[:, :, :, None, :] * eye[None, :, None, :, None]
        return e.reshape(S5_BLOCKS, S5_STATES, LANES)
    e = t4[:, :, :, None, :] * eye[None, :, None, :, None]
    return e.reshape(S5_BLOCKS, LANES, S5_STATES)


def _block_diag_extract(m, transpose):
    if transpose:
        m5 = m.reshape(S5_BLOCKS, 8, SSM_STATE, 8, SSM_GROUP)
        d = jnp.stack([m5[:, i, :, i, :] for i in range(8)], axis=1)
        return jnp.swapaxes(d, 2, 3).reshape(N_GROUPS, SSM_GROUP, SSM_STATE)
    m5 = m.reshape(S5_BLOCKS, 8, SSM_GROUP, 8, SSM_STATE)
    d = jnp.stack([m5[:, i, :, i, :] for i in range(8)], axis=1)
    return d.reshape(N_GROUPS, SSM_GROUP, SSM_STATE)


def _pack(pieces):
    flat = jnp.concatenate([p.reshape(-1).astype(F32) for p in pieces])
    size = -(-flat.shape[0] // PACK_ROWS) * PACK_ROWS
    return _pad_to(flat, 0, size).reshape(-1, LANES)


def _unpack(packed, shapes):
    flat = packed.reshape(-1)
    out, off = [], 0
    for shp in shapes:
        size = math.prod(shp)
        out.append(flat[off:off + size].reshape(shp))
        off += size
    return out


def kernel(x, norm_mix, w_in, b_forget, lam_re, lam_im, b_re, b_im, c_re, c_im, d_skip, log_dt, w_glu, b_glu, q_norm, k_norm, norm_out_ssm, norm_out_attn, w_out, norm_ffn, w_up, conv_w, conv_b, w_down, loss_target, m_norm_mix, m_w_in, m_b_forget, m_lam_re, m_lam_im, m_b_re, m_b_im, m_c_re, m_c_im, m_d_skip, m_log_dt, m_w_glu, m_b_glu, m_q_norm, m_k_norm, m_norm_out_ssm, m_norm_out_attn, m_w_out, m_norm_ffn, m_w_up, m_conv_w, m_conv_b, m_w_down, v_norm_mix, v_w_in, v_b_forget, v_lam_re, v_lam_im, v_b_re, v_b_im, v_c_re, v_c_im, v_d_skip, v_log_dt, v_w_glu, v_b_glu, v_q_norm, v_k_norm, v_norm_out_ssm, v_norm_out_attn, v_w_out, v_norm_ffn, v_w_up, v_conv_w, v_conv_b, v_w_down):
    given = dict(locals())
    weights = {k: given[k] for k in WEIGHT_NAMES}
    mom1 = {k: given["m_" + k] for k in WEIGHT_NAMES}
    mom2 = {k: given["v_" + k] for k in WEIGHT_NAMES}
    n_seq, seq_len, _ = x.shape
    n = n_seq * seq_len
    xf = x.reshape(n, D_MODEL)
    target = loss_target.reshape(n, D_MODEL)

    g_in, g_cw = _exchange([w_in[0].astype(BF16), conv_w[0]], [False, False], "gather_w_in")
    rest_flags = [False] * 2
    rows_w = jnp.concatenate([w_down[0], w_out[0], w_glu[0].reshape(ROWS_GLU, D_MODEL)], axis=0).astype(BF16)
    w_sems = _exchange_start([rows_w, w_up[0].astype(BF16)], rest_flags, g_in, "gather_rest_start", 0)
    norm_mix = norm_mix + w_sems[4][0, 0]
    w_in_p = _pad_to(jnp.swapaxes(g_in, 0, 1).reshape(D_MODEL, D_IN), 1, D_IN_PAD)

    lr3 = lam_re[0].reshape(N_GROUPS, 1, SSM_STATE)
    li3 = lam_im[0].reshape(N_GROUPS, 1, SSM_STATE)
    ldt3 = log_dt[0].reshape(N_GROUPS, 1, 1)
    br_t = jnp.swapaxes(b_re[0], 1, 2)
    bi_t = jnp.swapaxes(b_im[0], 1, 2)
    ab_re, ab_im, bb_re, bb_im = _s5_param_fwd(lr3, li3, ldt3, br_t, bi_t)
    a_re = ab_re.reshape(S5_BLOCKS, 1, S5_STATES)
    a_im = ab_im.reshape(S5_BLOCKS, 1, S5_STATES)
    bbr = _block_diag(bb_re, False).astype(BF16)
    bbi = _block_diag(bb_im, False).astype(BF16)
    cr = _block_diag(c_re[0], True).astype(BF16)
    ci = _block_diag(c_im[0], True).astype(BF16)

    avg = jnp.kron(jnp.eye(N_HEADS, dtype=F32), jnp.full((HEAD_DIM, HEAD_DIM), 1.0 / HEAD_DIM, F32))
    qg = jnp.tile(q_norm, (1, N_HEADS))
    kg = jnp.tile(k_norm, (1, N_HEADS))
    hn, u, qkv, raw, fl = _inproj_fwd(xf, norm_mix, w_in_p, avg, qg, kg)
    yc, st_r, st_i = _s5_fwd(u, a_re, a_im, bbr, bbi, cr, ci, d_skip, n_seq)
    bf = _pad_to(b_forget, 1, LANES)
    cum = _fprep_fwd(fl, bf, n_seq)
    cum8 = jnp.swapaxes(cum[:, :N_HEADS].reshape(n_seq, seq_len, N_HEADS), 1, 2)
    cq = cum8[:, :, :, None]
    ck = cum8[:, :, None, :]
    ya, lse = _attn_fwd(qkv, cq, ck, n_seq)
    g_rows, g_up = _exchange_wait(w_sems[0], w_sems[1], w_sems[2], w_sems[3], rest_flags, ya, "gather_rest_wait")
    g_down = g_rows[:, :ROWS_DOWN]
    g_out = g_rows[:, ROWS_DOWN:ROWS_DOWN + ROWS_OUT]
    g_glu = g_rows[:, ROWS_DOWN + ROWS_OUT:]
    w_glu_f = g_glu.reshape(D_SSM, D_SSM)
    w_out_f = g_out.reshape(D_MODEL, D_MODEL)
    w_up_f = jnp.swapaxes(g_up, 0, 1).reshape(D_MODEL, 2 * D_FF)
    wup_g = _pad_to(w_up_f[:, :D_FF], 1, D_FF_PAD)
    wup_v = _pad_to(w_up_f[:, D_FF:], 1, D_FF_PAD)
    cw_f = jnp.swapaxes(g_cw, 0, 1).reshape(3, 2 * D_FF)
    cw4 = jnp.concatenate([cw_f, conv_b], axis=0)
    cw_g = _pad_to(_pad_to(cw4[:, :D_FF], 1, D_FF_PAD), 0, SUBLANES)
    cw_v = _pad_to(_pad_to(cw4[:, D_FF:], 1, D_FF_PAD), 0, SUBLANES)
    w_down_p = _pad_to(g_down.reshape(D_FF, D_MODEL), 0, D_FF_PAD)
    ys = _glu_fwd(yc, w_glu_f, b_glu)
    h1, hn2, mixed = _mix_fwd(xf, ys, ya, norm_out_ssm, norm_out_attn, w_out_f, norm_ffn)
    ug, uv, dy, loss_part = _ffn_fwd(hn2, h1, target, wup_g, wup_v, cw_g, cw_v, w_down_p, seq_len)
    loss = lax.psum(0.5 * jnp.sum(loss_part) / D_MODEL, ("x", "y", "c"))

    dug, duv, act, dhn2, dcg, dcv = _ffn_bwd(dy, ug, uv, wup_g, wup_v, cw_g, cw_v, w_down_p, seq_len)
    dh1, dys, dya, d_gs, d_ga, d_gf = _mix_bwd(dy, dhn2[None], h1, ys, ya, norm_out_ssm, norm_out_attn, w_out_f, norm_ffn)
    dyc, gl_b, dz_b, d_bglu = _glu_bwd(yc, dys, w_glu_f, b_glu)

    gw_glu = _tn_matmul(gl_b, dz_b, "dw_glu", D_SSM, D_SSM)
    gw_out = _tn_matmul(mixed, dh1, "dw_out", D_MODEL, D_MODEL)
    gw_up = jnp.concatenate([_tn_matmul(hn2, dug, "dw_up_gate", D_MODEL, D_FF_PAD // 2)[:, :D_FF],
                             _tn_matmul(hn2, duv, "dw_up_val", D_MODEL, D_FF_PAD // 2)[:, :D_FF]], axis=1)
    gw_down = _tn_matmul(act, dy, "dw_down", D_FF_PAD // 2, D_MODEL)[:D_FF]
    dcg2 = jnp.swapaxes(dcg, 0, 1).reshape(SUBLANES, D_FF_PAD)[:, :D_FF]
    dcv2 = jnp.swapaxes(dcv, 0, 1).reshape(SUBLANES, D_FF_PAD)[:, :D_FF]
    g_conv = jnp.concatenate([dcg2, dcv2], axis=1)
    by_cols = lambda g, c: jnp.swapaxes(g.reshape(g.shape[0], N_DEV, c), 0, 1)
    early_flags = [True] * 2
    rows_g = jnp.concatenate([gw_down.reshape(N_DEV, ROWS_DOWN, D_MODEL), gw_out.reshape(N_DEV, ROWS_OUT, D_MODEL),
                              gw_glu.reshape(N_DEV, ROWS_GLU, D_MODEL)], axis=1)
    g_sems = _exchange_start([rows_g, by_cols(gw_up, 2 * D_FF // N_DEV)], early_flags, dyc, "grad_early_start", 1)
    started = g_sems[4][0, 0]

    du, dbbr, dbbi, dcr, dci, dar, dai, ddk = _s5_bwd(u, dyc, st_r, st_i, a_re, a_im, bbr, bbi, cr, ci,
                                                      d_skip + started, n_seq)
    dqn, dkn, dv, dcq, dck = _attn_bwd(qkv, cq, ck, ya, dya, lse + started, n_seq)
    dcum8 = dcq[:, :, :, 0] + dck.reshape(n_seq, N_HEADS, seq_len)
    dcum = _pad_to(jnp.swapaxes(dcum8, 1, 2).reshape(n, N_HEADS), 1, LANES)
    dfl, dbf = _fprep_bwd(dcum, fl, bf, n_seq)
    dx, dproj, d_gmix, d_qg, d_kg = _inproj_bwd(xf, norm_mix, w_in_p, avg, qg, kg, raw, du, dqn, dkn, dv, dfl, dh1)

    gw_in = _tn_matmul(hn, dproj, "dw_in", D_MODEL, D_IN_PAD)[:, :D_IN]
    partial_small = {
        "norm_mix": d_gmix, "b_forget": jnp.sum(dbf, axis=(0, 1))[:N_HEADS],
        "ab_re": jnp.sum(dar, axis=1), "ab_im": jnp.sum(dai, axis=1),
        "bb_re": _block_diag_extract(dbbr, False), "bb_im": _block_diag_extract(dbbi, False),
        "c_re": _block_diag_extract(dcr, True), "c_im": _block_diag_extract(dci, True),
        "d_skip": jnp.sum(ddk, axis=1), "b_glu": d_bglu,
        "q_norm": jnp.sum(d_qg.reshape(N_HEADS, HEAD_DIM), axis=0),
        "k_norm": jnp.sum(d_kg.reshape(N_HEADS, HEAD_DIM), axis=0),
        "norm_out_ssm": d_gs, "norm_out_attn": d_ga, "norm_ffn": d_gf, "conv_b": g_conv[3],
    }
    small_keys = tuple(partial_small)
    small_shapes = [partial_small[k].shape for k in small_keys]

    land_in, land_cw, small_parts = _exchange(
        [by_cols(gw_in, D_IN // N_DEV).astype(BF16), by_cols(g_conv[:3], 2 * D_FF // N_DEV),
         _pack([partial_small[k] for k in small_keys])], [True, True, False], "grad_late_exchange")
    land_rows, land_up = _exchange_wait(g_sems[0], g_sems[1], g_sems[2], g_sems[3], early_flags, land_in,
                                        "grad_early_wait")
    land_out = land_rows[:, ROWS_DOWN:ROWS_DOWN + ROWS_OUT]
    land_glu = land_rows[:, ROWS_DOWN + ROWS_OUT:].reshape(N_DEV, -1, D_SSM)
    grads, deltas, new_m, new_v = {}, {}, {}, {}
    for name, land in zip(SHARDED, (land_in, land_glu, land_out, land_up, land_cw, land_rows)):
        g, d, nm, nv = _adam_sharded(land, weights[name][0], mom1[name][0], mom2[name][0], "adam_" + name, ADAM_ROWS[name])
        grads[name], deltas[name], new_m[name], new_v[name] = g[None], d[None], nm[None], nv[None]

    summed = dict(zip(small_keys, _unpack(_sum_partials(small_parts, "sum_small_grads"), small_shapes)))
    dlr, dli, dldt, dbr_t, dbi_t = _s5_param_bwd(
        lr3, li3, ldt3, br_t, bi_t, summed["ab_re"].reshape(lr3.shape), summed["ab_im"].reshape(lr3.shape),
        summed["bb_re"], summed["bb_im"])
    small_grads = {
        "norm_mix": summed["norm_mix"], "b_forget": summed["b_forget"], "lam_re": dlr, "lam_im": dli,
        "b_re": jnp.swapaxes(dbr_t, 1, 2), "b_im": jnp.swapaxes(dbi_t, 1, 2), "c_re": summed["c_re"], "c_im": summed["c_im"],
        "d_skip": summed["d_skip"], "log_dt": dldt, "b_glu": summed["b_glu"], "q_norm": summed["q_norm"],
        "k_norm": summed["k_norm"], "norm_out_ssm": summed["norm_out_ssm"], "norm_out_attn": summed["norm_out_attn"],
        "norm_ffn": summed["norm_ffn"], "conv_b": summed["conv_b"],
    }
    repl = tuple(k for k in WEIGHT_NAMES if k not in SHARDED)
    g_list = [small_grads[k].reshape(weights[k].shape) for k in repl]
    d_list, m_list, v_list = _adam_replicated(g_list, [weights[k] for k in repl], [mom1[k] for k in repl],
                                              [mom2[k] for k in repl], "adam_replicated")
    for k, g, d, nm, nv in zip(repl, g_list, d_list, m_list, v_list):
        grads[k], deltas[k], new_m[k], new_v[k] = g, d, nm, nv

    grad_x = dx.reshape(x.shape)
    return (loss, grad_x, *[grads[k] for k in WEIGHT_NAMES], *[deltas[k] for k in WEIGHT_NAMES],
            *[new_m[k] for k in WEIGHT_NAMES], *[new_v[k] for k in WEIGHT_NAMES])
```

```python
import functools
import math

import jax
import jax.numpy as jnp
from jax import lax
from jax.experimental import pallas as pl
from jax.experimental.pallas import tpu as pltpu

F32 = jnp.float32
BF16 = jnp.bfloat16
HIGHEST = lax.Precision.HIGHEST

N_DEV = 8
D_MODEL = 1024
D_SSM = 512
D_ATTN = 512
N_HEADS = 8
HEAD_DIM = 64
N_GROUPS = 32
SSM_GROUP = 16
SSM_STATE = 64
D_FF = 2752
D_FF_PAD = 2816
D_IN = 2056
D_IN_PAD = 2176
EPS = 1e-6
LANES = 128
SUBLANES = 8
VMEM_LIMIT = 56 * 1024 * 1024

ADAM_LR = 0.001
ADAM_B1 = 0.9
ADAM_B2 = 0.999
ADAM_EPS = 1e-08
ADAM_WD = 0.01
ADAM_STEP = 10


def _cparams(*sem):
    return pltpu.CompilerParams(dimension_semantics=sem, vmem_limit_bytes=VMEM_LIMIT)


def _dot(a, b, **kw):
    return jnp.dot(a, b, preferred_element_type=F32, **kw)


def _dot_nt(a, b):
    return lax.dot_general(a, b, (((1,), (1,)), ((), ())), preferred_element_type=F32)


def _dot_tn(a, b):
    return lax.dot_general(a, b, (((0,), (0,)), ((), ())), preferred_element_type=F32)


def _rms(x, g):
    return x * lax.rsqrt(jnp.mean(x * x, axis=-1, keepdims=True) + EPS) * g


def _headnorm(q, avg, g):
    ms = jnp.dot(q * q, avg, preferred_element_type=F32, precision=HIGHEST)
    return q * lax.rsqrt(ms + EPS) * g


def _exchange(srcs, scatter_flags, name):
    n = len(srcs)
    out_shape = []
    for s, sc in zip(srcs, scatter_flags):
        shp = s.shape if sc else (N_DEV,) + s.shape
        out_shape.append(jax.ShapeDtypeStruct(shp, s.dtype))

    def body(*refs):
        src = refs[:n]
        dst = refs[n:2 * n]
        send_sems, recv_sems, loc_sems = refs[2 * n:]
        x, y, c = lax.axis_index("x"), lax.axis_index("y"), lax.axis_index("c")
        me = 4 * x + 2 * y + c
        peers = []
        for j in range(1, N_DEV):
            px = 1 - x if (j >> 2) & 1 else x
            py = 1 - y if (j >> 1) & 1 else y
            pc = 1 - c if j & 1 else c
            peers.append(((px, py, pc), 4 * px + 2 * py + pc))
        local, sends = [], []
        for k in range(n):
            own = src[k].at[me] if scatter_flags[k] else src[k]
            lc = pltpu.make_async_copy(own, dst[k].at[me], loc_sems.at[k])
            lc.start()
            local.append(lc)
            for j, (pid, pidx) in enumerate(peers):
                s = src[k].at[pidx] if scatter_flags[k] else src[k]
                cp = pltpu.make_async_remote_copy(
                    src_ref=s, dst_ref=dst[k].at[me], send_sem=send_sems.at[k, j], recv_sem=recv_sems.at[k, j],
                    device_id=pid, device_id_type=pl.DeviceIdType.MESH)
                cp.start()
                sends.append(cp)
        for k in range(n):
            for j, (pid, pidx) in enumerate(peers):
                s = src[k].at[pidx] if scatter_flags[k] else src[k]
                pltpu.make_async_remote_copy(
                    src_ref=s, dst_ref=dst[k].at[pidx], send_sem=send_sems.at[k, j], recv_sem=recv_sems.at[k, j],
                    device_id=pid, device_id_type=pl.DeviceIdType.MESH).wait_recv()
        for cp in sends:
            cp.wait_send()
        for lc in local:
            lc.wait()

    any_spec = pl.BlockSpec(memory_space=pl.ANY)
    return pl.pallas_call(
        body, name=name, out_shape=tuple(out_shape),
        in_specs=[any_spec] * n, out_specs=tuple([any_spec] * n),
        scratch_shapes=[pltpu.SemaphoreType.DMA((n, N_DEV - 1)), pltpu.SemaphoreType.DMA((n, N_DEV - 1)),
                        pltpu.SemaphoreType.DMA((n,))],
        compiler_params=pltpu.CompilerParams(has_side_effects=True),
    )(*srcs)


def _peer_list():
    x, y, c = lax.axis_index("x"), lax.axis_index("y"), lax.axis_index("c")
    peers = []
    for j in range(1, N_DEV):
        px = 1 - x if (j >> 2) & 1 else x
        py = 1 - y if (j >> 1) & 1 else y
        pc = 1 - c if j & 1 else c
        peers.append(((px, py, pc), 4 * px + 2 * py + pc))
    return 4 * x + 2 * y + c, peers


def _split_copies(src, land, send_sems, recv_sems, scatter_flags, me, peers, incoming):
    copies = []
    for k in range(len(src)):
        for j, (pid, pidx) in enumerate(peers):
            s = src[k].at[pidx] if scatter_flags[k] else src[k]
            i = k * (N_DEV - 1) + j
            copies.append(pltpu.make_async_remote_copy(
                src_ref=s, dst_ref=land[k].at[pidx if incoming else me], send_sem=send_sems[i],
                recv_sem=recv_sems[i], device_id=pid, device_id_type=pl.DeviceIdType.MESH))
    return copies


def _exchange_start(srcs, scatter_flags, after, name, collective_id):
    n = len(srcs)
    ns = n * (N_DEV - 1)
    hbm = pl.BlockSpec(memory_space=pltpu.HBM)
    sem = pl.BlockSpec(memory_space=pltpu.SEMAPHORE)
    land_shapes = [s.shape if sc else (N_DEV,) + s.shape for s, sc in zip(srcs, scatter_flags)]

    def body(*refs):
        src, land = refs[:n], refs[n:2 * n]
        send_sems = refs[2 * n + 1:2 * n + 1 + ns]
        recv_sems = refs[2 * n + 1 + ns:2 * n + 1 + 2 * ns]
        token = refs[4 * n + 1 + 2 * ns]
        me, peers = _peer_list()
        barrier = pltpu.get_barrier_semaphore()
        for pid, _ in peers:
            pl.semaphore_signal(barrier, inc=1, device_id=pid, device_id_type=pl.DeviceIdType.MESH)
        pl.semaphore_wait(barrier, N_DEV - 1)
        for cp in _split_copies(src, land, send_sems, recv_sems, scatter_flags, me, peers, False):
            cp.start()
        token[...] = jnp.zeros_like(token)

    outs = pl.pallas_call(
        body, name=name,
        out_shape=(*[pltpu.SemaphoreType.DMA(())] * (2 * ns), *[pltpu.HBM(s.shape, s.dtype) for s in srcs],
                   *[pltpu.HBM(shp, s.dtype) for shp, s in zip(land_shapes, srcs)],
                   jax.ShapeDtypeStruct((SUBLANES, LANES), F32)),
        in_specs=[hbm] * (2 * n) + [pl.BlockSpec(memory_space=pl.ANY)],
        out_specs=(*[sem] * (2 * ns), *[hbm] * (2 * n), pl.BlockSpec(memory_space=pltpu.VMEM)),
        input_output_aliases={i: 2 * ns + i for i in range(2 * n)},
        compiler_params=pltpu.CompilerParams(has_side_effects=pltpu.SideEffectType.DATAFLOW_SIDE_EFFECTING,
                                             collective_id=collective_id),
    )(*[pltpu.with_memory_space_constraint(s, pltpu.HBM) for s in srcs],
      *[pltpu.with_memory_space_constraint(lax.empty(shp, s.dtype), pltpu.HBM) for shp, s in zip(land_shapes, srcs)],
      after)
    return (outs[:ns], outs[ns:2 * ns], outs[2 * ns:2 * ns + n], outs[2 * ns + n:2 * ns + 2 * n], outs[2 * ns + 2 * n])


def _exchange_wait(send_sems, recv_sems, srcs, lands, scatter_flags, after, name):
    n = len(srcs)
    ns = n * (N_DEV - 1)
    hbm = pl.BlockSpec(memory_space=pltpu.HBM)
    sem = pl.BlockSpec(memory_space=pltpu.SEMAPHORE)

    def body(*refs):
        src, land = refs[:n], refs[n:2 * n]
        s_sems = refs[2 * n:2 * n + ns]
        r_sems = refs[2 * n + ns:2 * n + 2 * ns]
        me, peers = _peer_list()
        for cp in _split_copies(src, land, s_sems, r_sems, scatter_flags, me, peers, True):
            cp.wait_send()
            cp.wait_recv()

    outs = pl.pallas_call(
        body, name=name,
        out_shape=tuple(pltpu.HBM(a.shape, a.dtype) for a in (*srcs, *lands)),
        in_specs=[hbm] * (2 * n) + [sem] * (2 * ns) + [pl.BlockSpec(memory_space=pl.ANY)],
        out_specs=tuple([hbm] * (2 * n)),
        input_output_aliases={i: i for i in range(2 * n)},
        compiler_params=pltpu.CompilerParams(has_side_effects=pltpu.SideEffectType.DATAFLOW_SIDE_EFFECTING),
    )(*srcs, *lands, *send_sems, *recv_sems, after)
    return outs[:n], outs[n:]


def _tn_matmul(a, b, name, tk, tm, tn=512):
    n_tok, k_dim = a.shape
    m_dim = b.shape[1]
    grid = (k_dim // tk, m_dim // tm, n_tok // tn)

    def body(a_ref, b_ref, o_ref):
        @pl.when(pl.program_id(2) == 0)
        def _():
            o_ref[...] = jnp.zeros_like(o_ref)
        o_ref[...] += _dot_tn(a_ref[...].astype(BF16), b_ref[...].astype(BF16))

    return pl.pallas_call(
        body, name=name, grid=grid,
        in_specs=[pl.BlockSpec((tn, tk), lambda i, j, k: (k, i)), pl.BlockSpec((tn, tm), lambda i, j, k: (k, j))],
        out_specs=pl.BlockSpec((tk, tm), lambda i, j, k: (i, j)),
        out_shape=jax.ShapeDtypeStruct((k_dim, m_dim), F32),
        compiler_params=_cparams("parallel", "parallel", "arbitrary"),
    )(a, b)


def _adam_math(g, w, m, v):
    m = ADAM_B1 * m + (1.0 - ADAM_B1) * g
    v = ADAM_B2 * v + (1.0 - ADAM_B2) * (g * g)
    m_hat = m / (1.0 - ADAM_B1 ** ADAM_STEP)
    v_hat = v / (1.0 - ADAM_B2 ** ADAM_STEP)
    delta = -ADAM_LR * (m_hat / (jnp.sqrt(v_hat) + ADAM_EPS) + ADAM_WD * w)
    return delta, m, v


def _adam_sharded(land, own, w, m, v, name, tr):
    r, c = w.shape

    def body(*refs):
        l_ref = refs[0]
        own_ref = refs[1] if own is not None else None
        w_ref, m_ref, v_ref, g_ref, d_ref, nm_ref, nv_ref = refs[-7:]
        if own_ref is not None:
            x, y, z = lax.axis_index("x"), lax.axis_index("y"), lax.axis_index("c")
            me = 4 * x + 2 * y + z
            mine = own_ref[...].astype(F32)
        g = None
        for s in range(N_DEV):
            part = l_ref[s].astype(F32)
            if own_ref is not None:
                part = jnp.where(me == s, mine, part)
            g = part if g is None else g + part
        d, nm, nv = _adam_math(g, w_ref[...], m_ref[...], v_ref[...])
        g_ref[...] = g
        d_ref[...] = d
        nm_ref[...] = nm
        nv_ref[...] = nv

    spec = pl.BlockSpec((tr, c), lambda i: (i, 0))
    own_specs, own_args = ([spec], [own]) if own is not None else ([], [])
    return pl.pallas_call(
        body, name=name, grid=(r // tr,),
        in_specs=[pl.BlockSpec((N_DEV, tr, c), lambda i: (0, i, 0)), *own_specs, spec, spec, spec],
        out_specs=(spec, spec, spec, spec),
        out_shape=tuple(jax.ShapeDtypeStruct((r, c), F32) for _ in range(4)),
        compiler_params=_cparams("parallel"),
    )(land, *own_args, w, m, v)


def _sum_partials(parts, name):
    _, r, c = parts.shape

    def body(p_ref, o_ref):
        g = p_ref[0]
        for s in range(1, N_DEV):
            g = g + p_ref[s]
        o_ref[...] = g

    return pl.pallas_call(body, name=name, out_shape=jax.ShapeDtypeStruct((r, c), F32),
                          compiler_params=pltpu.CompilerParams(vmem_limit_bytes=VMEM_LIMIT))(parts)


def _adam_replicated(gs, ws, ms, vs, name):
    k = len(ws)

    def body(*refs):
        outs = refs[4 * k:]
        for i in range(k):
            d, nm, nv = _adam_math(refs[i][...], refs[k + i][...], refs[2 * k + i][...], refs[3 * k + i][...])
            outs[i][...] = d
            outs[k + i][...] = nm
            outs[2 * k + i][...] = nv

    outs = pl.pallas_call(body, name=name, out_shape=tuple(jax.ShapeDtypeStruct(w.shape, F32) for w in ws) * 3,
                          compiler_params=pltpu.CompilerParams(vmem_limit_bytes=VMEM_LIMIT))(*gs, *ws, *ms, *vs)
    return outs[:k], outs[k:2 * k], outs[2 * k:]


def _inproj_fwd(x, g, w_in, avg, qg, kg, tm=512):
    n = x.shape[0]

    def body(x_ref, g_ref, w_ref, a_ref, qg_ref, kg_ref, hn_ref, u_ref, qkv_ref, raw_ref, fl_ref):
        hn = _rms(x_ref[...], g_ref[...]).astype(BF16)
        hn_ref[...] = hn
        proj = _dot(hn, w_ref[...])
        u_ref[...] = proj[:, 0:512]
        q = proj[:, 512:1024]
        k = proj[:, 1024:1536]
        raw_ref[:, 0:512] = q
        raw_ref[:, 512:1024] = k
        qkv_ref[:, 0:512] = _headnorm(q, a_ref[...], qg_ref[...])
        qkv_ref[:, 512:1024] = _headnorm(k, a_ref[...], kg_ref[...])
        qkv_ref[:, 1024:1536] = proj[:, 1536:2048]
        fl_ref[...] = proj[:, 2048:D_IN_PAD]

    row = lambda w: pl.BlockSpec((tm, w), lambda i: (i, 0))
    full = lambda a: pl.BlockSpec(a.shape, lambda i: (0,) * a.ndim)
    return pl.pallas_call(
        body, name="inproj_fwd", grid=(n // tm,),
        in_specs=[row(D_MODEL), full(g), full(w_in), full(avg), full(qg), full(kg)],
        out_specs=(row(D_MODEL), row(512), row(1536), row(1024), row(LANES)),
        out_shape=(jax.ShapeDtypeStruct((n, D_MODEL), BF16), jax.ShapeDtypeStruct((n, 512), F32),
                   jax.ShapeDtypeStruct((n, 1536), F32), jax.ShapeDtypeStruct((n, 1024), F32),
                   jax.ShapeDtypeStruct((n, LANES), F32)),
        compiler_params=_cparams("parallel"),
    )(x, g, w_in, avg, qg, kg)


def _inproj_bwd(x, g, w_in, avg, qg, kg, raw, du, dqn, dkn, dv, dfl, dres, tm=512):
    n = x.shape[0]

    def body(x_ref, g_ref, w_ref, a_ref, qg_ref, kg_ref, raw_ref, du_ref, dqn_ref, dkn_ref, dv_ref, dfl_ref, dres_ref,
             dx_ref, dproj_ref, dg_ref, dqg_ref, dkg_ref):
        @pl.when(pl.program_id(0) == 0)
        def _():
            dg_ref[...] = jnp.zeros_like(dg_ref)
            dqg_ref[...] = jnp.zeros_like(dqg_ref)
            dkg_ref[...] = jnp.zeros_like(dkg_ref)
        avg_m = a_ref[...]
        _, vjp_q = jax.vjp(lambda q, gg: _headnorm(q, avg_m, gg), raw_ref[:, 0:512], qg_ref[...])
        dq, dqg = vjp_q(dqn_ref[...])
        _, vjp_k = jax.vjp(lambda k, gg: _headnorm(k, avg_m, gg), raw_ref[:, 512:1024], kg_ref[...])
        dk, dkg = vjp_k(dkn_ref[...])
        dproj = jnp.concatenate([du_ref[...], dq, dk, dv_ref[...], dfl_ref[...]], axis=1).astype(BF16)
        dproj_ref[...] = dproj
        dhn = _dot_nt(dproj, w_ref[...])
        _, vjp_x = jax.vjp(_rms, x_ref[...], g_ref[...])
        dxn, dg = vjp_x(dhn)
        dx_ref[...] = dxn + dres_ref[...]
        dg_ref[...] += dg
        dqg_ref[...] += dqg
        dkg_ref[...] += dkg

    row = lambda w: pl.BlockSpec((tm, w), lambda i: (i, 0))
    full = lambda a: pl.BlockSpec(a.shape, lambda i: (0,) * a.ndim)
    vec = lambda w: pl.BlockSpec((1, w), lambda i: (0, 0))
    return pl.pallas_call(
        body, name="inproj_bwd", grid=(n // tm,),
        in_specs=[row(D_MODEL), full(g), full(w_in), full(avg), full(qg), full(kg), row(1024), row(512), row(512),
                  row(512), row(512), row(LANES), row(D_MODEL)],
        out_specs=(row(D_MODEL), row(D_IN_PAD), vec(D_MODEL), vec(512), vec(512)),
        out_shape=(jax.ShapeDtypeStruct((n, D_MODEL), F32), jax.ShapeDtypeStruct((n, D_IN_PAD), BF16),
                   jax.ShapeDtypeStruct((1, D_MODEL), F32), jax.ShapeDtypeStruct((1, 512), F32),
                   jax.ShapeDtypeStruct((1, 512), F32)),
        compiler_params=_cparams("arbitrary"),
    )(x, g, w_in, avg, qg, kg, raw, du, dqn, dkn, dv, dfl, dres)


def _glu_fwd(yc, wg, bg, tm=512):
    n = yc.shape[0]

    def body(yc_ref, w_ref, b_ref, ys_ref):
        gl = jax.nn.gelu(yc_ref[...])
        z = _dot(gl.astype(BF16), w_ref[...]) + b_ref[...]
        ys_ref[...] = gl * jax.nn.sigmoid(z)

    row = pl.BlockSpec((tm, 512), lambda i: (i, 0))
    full = lambda a: pl.BlockSpec(a.shape, lambda i: (0,) * a.ndim)
    return pl.pallas_call(
        body, name="glu_fwd", grid=(n // tm,), in_specs=[row, full(wg), full(bg)], out_specs=row,
        out_shape=jax.ShapeDtypeStruct((n, 512), F32), compiler_params=_cparams("parallel"),
    )(yc, wg, bg)


def _glu_bwd(yc, dys, wg, bg, tm=512):
    n = yc.shape[0]

    def body(yc_ref, dys_ref, w_ref, b_ref, dyc_ref, gl_ref, dz_ref, db_ref):
        @pl.when(pl.program_id(0) == 0)
        def _():
            db_ref[...] = jnp.zeros_like(db_ref)
        gl, vjp_gelu = jax.vjp(jax.nn.gelu, yc_ref[...])
        glb = gl.astype(BF16)
        z = _dot(glb, w_ref[...]) + b_ref[...]
        s = jax.nn.sigmoid(z)
        dys = dys_ref[...]
        dz = dys * gl * s * (1.0 - s)
        dzb = dz.astype(BF16)
        dgl = dys * s + _dot_nt(dzb, w_ref[...])
        dyc_ref[...] = vjp_gelu(dgl)[0]
        gl_ref[...] = glb
        dz_ref[...] = dzb
        db_ref[...] += jnp.sum(dz, axis=0, keepdims=True)

    row = pl.BlockSpec((tm, 512), lambda i: (i, 0))
    full = lambda a: pl.BlockSpec(a.shape, lambda i: (0,) * a.ndim)
    return pl.pallas_call(
        body, name="glu_bwd", grid=(n // tm,), in_specs=[row, row, full(wg), full(bg)],
        out_specs=(row, row, row, pl.BlockSpec((1, 512), lambda i: (0, 0))),
        out_shape=(jax.ShapeDtypeStruct((n, 512), F32), jax.ShapeDtypeStruct((n, 512), BF16),
                   jax.ShapeDtypeStruct((n, 512), BF16), jax.ShapeDtypeStruct((1, 512), F32)),
        compiler_params=_cparams("arbitrary"),
    )(yc, dys, wg, bg)


def _mix_fwd(x, ys, ya, gs, ga, wout, gf, tm=512):
    n = x.shape[0]

    def body(x_ref, ys_ref, ya_ref, gs_ref, ga_ref, w_ref, gf_ref, h1_ref, hn2_ref, mixed_ref):
        mixed = jnp.concatenate([_rms(ys_ref[...], gs_ref[...]), _rms(ya_ref[...], ga_ref[...])], axis=1).astype(BF16)
        mixed_ref[...] = mixed
        h1 = x_ref[...] + _dot(mixed, w_ref[...])
        h1_ref[...] = h1
        hn2_ref[...] = _rms(h1, gf_ref[...]).astype(BF16)

    row = lambda w: pl.BlockSpec((tm, w), lambda i: (i, 0))
    full = lambda a: pl.BlockSpec(a.shape, lambda i: (0,) * a.ndim)
    return pl.pallas_call(
        body, name="mix_fwd", grid=(n // tm,),
        in_specs=[row(D_MODEL), row(512), row(512), full(gs), full(ga), full(wout), full(gf)],
        out_specs=(row(D_MODEL), row(D_MODEL), row(D_MODEL)),
        out_shape=(jax.ShapeDtypeStruct((n, D_MODEL), F32), jax.ShapeDtypeStruct((n, D_MODEL), BF16),
                   jax.ShapeDtypeStruct((n, D_MODEL), BF16)),
        compiler_params=_cparams("parallel"),
    )(x, ys, ya, gs, ga, wout, gf)


def _mix_bwd(dy, dhn2_parts, h1, ys, ya, gs, ga, wout, gf, tm=512):
    n = dy.shape[0]
    n_parts = dhn2_parts.shape[0]

    def body(dy_ref, dp_ref, h1_ref, ys_ref, ya_ref, gs_ref, ga_ref, w_ref, gf_ref,
             dh1_ref, dys_ref, dya_ref, dgs_ref, dga_ref, dgf_ref):
        @pl.when(pl.program_id(0) == 0)
        def _():
            dgs_ref[...] = jnp.zeros_like(dgs_ref)
            dga_ref[...] = jnp.zeros_like(dga_ref)
            dgf_ref[...] = jnp.zeros_like(dgf_ref)
        dhn2 = dp_ref[0]
        for p in range(1, n_parts):
            dhn2 = dhn2 + dp_ref[p]
        _, vjp_f = jax.vjp(_rms, h1_ref[...], gf_ref[...])
        dh1n, dgf = vjp_f(dhn2)
        dh1 = dy_ref[...] + dh1n
        dh1_ref[...] = dh1
        dmixed = _dot_nt(dh1.astype(BF16), w_ref[...])
        _, vjp_s = jax.vjp(_rms, ys_ref[...], gs_ref[...])
        dys, dgs = vjp_s(dmixed[:, 0:512])
        _, vjp_a = jax.vjp(_rms, ya_ref[...], ga_ref[...])
        dya, dga = vjp_a(dmixed[:, 512:1024])
        dys_ref[...] = dys
        dya_ref[...] = dya
        dgs_ref[...] += dgs
        dga_ref[...] += dga
        dgf_ref[...] += dgf

    row = lambda w: pl.BlockSpec((tm, w), lambda i: (i, 0))
    full = lambda a: pl.BlockSpec(a.shape, lambda i: (0,) * a.ndim)
    vec = lambda w: pl.BlockSpec((1, w), lambda i: (0, 0))
    return pl.pallas_call(
        body, name="mix_bwd", grid=(n // tm,),
        in_specs=[row(D_MODEL), pl.BlockSpec((n_parts, tm, D_MODEL), lambda i: (0, i, 0)), row(D_MODEL), row(512),
                  row(512), full(gs), full(ga), full(wout), full(gf)],
        out_specs=(row(D_MODEL), row(512), row(512), vec(512), vec(512), vec(D_MODEL)),
        out_shape=(jax.ShapeDtypeStruct((n, D_MODEL), F32), jax.ShapeDtypeStruct((n, 512), F32),
                   jax.ShapeDtypeStruct((n, 512), F32), jax.ShapeDtypeStruct((1, 512), F32),
                   jax.ShapeDtypeStruct((1, 512), F32), jax.ShapeDtypeStruct((1, D_MODEL), F32)),
        compiler_params=_cparams("arbitrary"),
    )(dy, dhn2_parts, h1, ys, ya, gs, ga, wout, gf)


HALO = 16


def _conv3(ue, cw):
    return cw[2:3] * ue + cw[1:2] * pltpu.roll(ue, 1, 0) + cw[0:1] * pltpu.roll(ue, 2, 0) + cw[3:4]


def _ffn_fwd(hn2, h1, target, wup_g, wup_v, cw_g, cw_v, wdown, seq_len, tm=512, fb=256):
    n = hn2.shape[0]
    nj = D_FF_PAD // fb
    hb = tm // HALO

    def body(hn_ref, halo_ref, h1_ref, tgt_ref, wg_ref, wv_ref, cg_ref, cv_ref, wd_ref,
             ug_ref, uv_ref, dy_ref, loss_ref, acc):
        i, j = pl.program_id(0), pl.program_id(1)
        seq_start = (i * tm) % seq_len == 0
        halo = halo_ref[...]
        halo = jnp.where(seq_start, jnp.zeros_like(halo), halo)
        he = jnp.concatenate([halo, hn_ref[...]], axis=0)
        ueg = _dot(he, wg_ref[...])
        uev = _dot(he, wv_ref[...])
        ug_ref[...] = ueg[HALO:]
        uv_ref[...] = uev[HALO:]
        cg = _conv3(ueg, cg_ref[...])[HALO:]
        cv = _conv3(uev, cv_ref[...])[HALO:]
        act = (jax.nn.silu(cg) * cv).astype(BF16)
        part = _dot(act, wd_ref[...])

        @pl.when(j == 0)
        def _():
            acc[...] = part

        @pl.when(j > 0)
        def _():
            acc[...] += part

        @pl.when(j == nj - 1)
        def _():
            err = h1_ref[...] + acc[...] - tgt_ref[...]
            dy_ref[...] = err * (1.0 / D_MODEL)
            loss_ref[0] = jnp.sum(err * err, axis=0, keepdims=True)

    row = pl.BlockSpec((tm, D_MODEL), lambda i, j: (i, 0))
    return pl.pallas_call(
        body, name="ffn_fwd", grid=(n // tm, nj),
        in_specs=[row, pl.BlockSpec((HALO, D_MODEL), lambda i, j: (jnp.maximum(i * hb - 1, 0), 0)), row, row,
                  pl.BlockSpec((D_MODEL, fb), lambda i, j: (0, j)), pl.BlockSpec((D_MODEL, fb), lambda i, j: (0, j)),
                  pl.BlockSpec((8, fb), lambda i, j: (0, j)), pl.BlockSpec((8, fb), lambda i, j: (0, j)),
                  pl.BlockSpec((fb, D_MODEL), lambda i, j: (j, 0))],
        out_specs=(pl.BlockSpec((tm, fb), lambda i, j: (i, j)), pl.BlockSpec((tm, fb), lambda i, j: (i, j)), row,
                   pl.BlockSpec((1, 1, D_MODEL), lambda i, j: (i, 0, 0))),
        out_shape=(jax.ShapeDtypeStruct((n, D_FF_PAD), F32), jax.ShapeDtypeStruct((n, D_FF_PAD), F32),
                   jax.ShapeDtypeStruct((n, D_MODEL), F32), jax.ShapeDtypeStruct((n // tm, 1, D_MODEL), F32)),
        scratch_shapes=[pltpu.VMEM((tm, D_MODEL), F32)],
        compiler_params=_cparams("parallel", "arbitrary"),
    )(hn2, hn2, h1, target, wup_g, wup_v, cw_g, cw_v, wdown)


def _ffn_bwd(dy, ug, uv, wup_g, wup_v, cw_g, cw_v, wdown, seq_len, tm=512, fb=256):
    n = dy.shape[0]
    nj = D_FF_PAD // fb
    hb = tm // HALO
    last_hb = n // HALO - 1
    rows = tm + HALO

    def body(dy_ref, dyn_ref, ugp_ref, ugm_ref, ugn_ref, uvp_ref, uvm_ref, uvn_ref, wg_ref, wv_ref, cg_ref, cv_ref,
             wd_ref, dug_ref, duv_ref, act_ref, dhn_ref, dcg_ref, dcv_ref, acc):
        i, j = pl.program_id(0), pl.program_id(1)
        seq_start = (i * tm) % seq_len == 0
        seq_end = ((i + 1) * tm) % seq_len == 0
        dyn = dyn_ref[...]
        dyn = jnp.where(seq_end, jnp.zeros_like(dyn), dyn)
        d_out = jnp.concatenate([dy_ref[...], dyn], axis=0).astype(BF16)
        d_act = _dot_nt(d_out, wd_ref[...])

        def pre_act(up_ref, um_ref, un_ref, cw):
            up = up_ref[...]
            up = jnp.where(seq_start, jnp.zeros_like(up), up)
            ue = jnp.concatenate([up, um_ref[...], un_ref[...]], axis=0)
            return ue, _conv3(ue, cw)[HALO:]

        cwg, cwv = cg_ref[...], cv_ref[...]
        ueg, cge = pre_act(ugp_ref, ugm_ref, ugn_ref, cwg)
        uev, cve = pre_act(uvp_ref, uvm_ref, uvn_ref, cwv)
        act, vjp_act = jax.vjp(lambda g, v: jax.nn.silu(g) * v, cge, cve)
        dcge, dcve = vjp_act(d_act)
        act_ref[...] = act[:tm].astype(BF16)

        def conv_t(dc, cw):
            return (cw[2:3] * dc + cw[1:2] * pltpu.roll(dc, rows - 1, 0) + cw[0:1] * pltpu.roll(dc, rows - 2, 0))[:tm]

        dug = conv_t(dcge, cwg).astype(BF16)
        duv = conv_t(dcve, cwv).astype(BF16)
        dug_ref[...] = dug
        duv_ref[...] = duv
        part = _dot_nt(dug, wg_ref[...]) + _dot_nt(duv, wv_ref[...])

        @pl.when(j == 0)
        def _():
            acc[...] = part

        @pl.when(j > 0)
        def _():
            acc[...] += part

        @pl.when(j == nj - 1)
        def _():
            dhn_ref[...] = acc[...]

        def cw_grad(dc, ue):
            dcm = dc[:tm]
            taps = [jnp.sum(dcm * pltpu.roll(ue, 2 - k, 0)[HALO:HALO + tm], axis=0, keepdims=True) for k in (0, 1)]
            taps.append(jnp.sum(dcm * ue[HALO:HALO + tm], axis=0, keepdims=True))
            taps.append(jnp.sum(dcm, axis=0, keepdims=True))
            return jnp.concatenate(taps + [jnp.zeros((4, fb), F32)], axis=0)

        @pl.when(i == 0)
        def _():
            dcg_ref[j] = jnp.zeros((8, fb), F32)
            dcv_ref[j] = jnp.zeros((8, fb), F32)

        dcg_ref[j] += cw_grad(dcge, ueg)
        dcv_ref[j] += cw_grad(dcve, uev)

    row = pl.BlockSpec((tm, D_MODEL), lambda i, j: (i, 0))
    u_prev = pl.BlockSpec((HALO, fb), lambda i, j: (jnp.maximum(i * hb - 1, 0), j))
    u_main = pl.BlockSpec((tm, fb), lambda i, j: (i, j))
    u_next = pl.BlockSpec((HALO, fb), lambda i, j: (jnp.minimum((i + 1) * hb, last_hb), j))
    w_col = pl.BlockSpec((D_MODEL, fb), lambda i, j: (0, j))
    c_col = pl.BlockSpec((8, fb), lambda i, j: (0, j))
    dc_spec = pl.BlockSpec((nj, 8, fb), lambda i, j: (0, 0, 0))
    return pl.pallas_call(
        body, name="ffn_bwd", grid=(n // tm, nj),
        in_specs=[row, pl.BlockSpec((HALO, D_MODEL), lambda i, j: (jnp.minimum((i + 1) * hb, last_hb), 0)),
                  u_prev, u_main, u_next, u_prev, u_main, u_next, w_col, w_col, c_col, c_col,
                  pl.BlockSpec((fb, D_MODEL), lambda i, j: (j, 0))],
        out_specs=(u_main, u_main, u_main, row, dc_spec, dc_spec),
        out_shape=(jax.ShapeDtypeStruct((n, D_FF_PAD), BF16), jax.ShapeDtypeStruct((n, D_FF_PAD), BF16),
                   jax.ShapeDtypeStruct((n, D_FF_PAD), BF16), jax.ShapeDtypeStruct((n, D_MODEL), F32),
                   jax.ShapeDtypeStruct((nj, 8, fb), F32), jax.ShapeDtypeStruct((nj, 8, fb), F32)),
        scratch_shapes=[pltpu.VMEM((tm, D_MODEL), F32)],
        compiler_params=_cparams("arbitrary", "arbitrary"),
    )(dy, dy, ug, ug, ug, uv, uv, uv, wup_g, wup_v, cw_g, cw_v, wdown)


def _s5_param_fn(lr, li, ldt, br, bi):
    dt = jnp.exp(ldt)
    mag = jnp.exp(lr * dt)
    ab_re = mag * jnp.cos(li * dt)
    ab_im = mag * jnp.sin(li * dt)
    nr = ab_re - 1.0
    ni = ab_im
    den = lr * lr + li * li
    q_re = (nr * lr + ni * li) / den
    q_im = (ni * lr - nr * li) / den
    bb_re = q_re * br - q_im * bi
    bb_im = q_re * bi + q_im * br
    return ab_re, ab_im, bb_re, bb_im


def _s5_param_fwd(lr, li, ldt, br, bi):
    def body(lr_ref, li_ref, ldt_ref, br_ref, bi_ref, ar_ref, ai_ref, bbr_ref, bbi_ref):
        ar, ai, bbr, bbi = _s5_param_fn(lr_ref[...], li_ref[...], ldt_ref[...], br_ref[...], bi_ref[...])
        ar_ref[...] = ar
        ai_ref[...] = ai
        bbr_ref[...] = bbr
        bbi_ref[...] = bbi

    return pl.pallas_call(
        body, name="s5_param_fwd",
        out_shape=(jax.ShapeDtypeStruct(lr.shape, F32), jax.ShapeDtypeStruct(lr.shape, F32),
                   jax.ShapeDtypeStruct(br.shape, F32), jax.ShapeDtypeStruct(br.shape, F32)),
    )(lr, li, ldt, br, bi)


def _s5_param_bwd(lr, li, ldt, br, bi, dar, dai, dbbr, dbbi):
    def body(lr_ref, li_ref, ldt_ref, br_ref, bi_ref, dar_ref, dai_ref, dbbr_ref, dbbi_ref,
             dlr_ref, dli_ref, dldt_ref, dbr_ref, dbi_ref):
        _, vjp = jax.vjp(_s5_param_fn, lr_ref[...], li_ref[...], ldt_ref[...], br_ref[...], bi_ref[...])
        dlr, dli, dldt, dbr, dbi = vjp((dar_ref[...], dai_ref[...], dbbr_ref[...], dbbi_ref[...]))
        dlr_ref[...] = dlr
        dli_ref[...] = dli
        dldt_ref[...] = dldt
        dbr_ref[...] = dbr
        dbi_ref[...] = dbi

    return pl.pallas_call(
        body, name="s5_param_bwd",
        out_shape=(jax.ShapeDtypeStruct(lr.shape, F32), jax.ShapeDtypeStruct(lr.shape, F32),
                   jax.ShapeDtypeStruct(ldt.shape, F32), jax.ShapeDtypeStruct(br.shape, F32),
                   jax.ShapeDtypeStruct(br.shape, F32)),
    )(lr, li, ldt, br, bi, dar, dai, dbbr, dbbi)


S5_CHUNK = 256
S5_STATES = 512
S5_BLOCKS = 4


def _cpow_rows(ar, ai, count):
    rs, im = [ar], [ai]
    for _ in range(count - 1):
        pr, pi = rs[-1], im[-1]
        rs.append(pr * ar - pi * ai)
        im.append(pr * ai + pi * ar)
    return rs, im


def _scan_in_groups(vr, vi, pr, pi, rm, reverse):
    n = vr.shape[0]
    for k in (1, 2, 4):
        if reverse:
            sr, si, keep = pltpu.roll(vr, n - k, 0), pltpu.roll(vi, n - k, 0), rm < SUBLANES - k
        else:
            sr, si, keep = pltpu.roll(vr, k, 0), pltpu.roll(vi, k, 0), rm >= k
        sr = jnp.where(keep, sr, 0.0)
        si = jnp.where(keep, si, 0.0)
        kr, ki = pr[k - 1], pi[k - 1]
        vr, vi = vr + kr * sr - ki * si, vi + kr * si + ki * sr
    return vr, vi


def _carry_over_groups(xr_s, xi_s, wr, wi, c0r, c0i, reverse):
    groups = xr_s.shape[0] // SUBLANES
    pick = 0 if reverse else SUBLANES - 1

    def step(q, carry):
        cr, ci = carry
        r = groups - 1 - q if reverse else q
        o = pl.multiple_of(r * SUBLANES, SUBLANES)
        vr = xr_s[pl.ds(o, SUBLANES), :]
        vi = xi_s[pl.ds(o, SUBLANES), :]
        nr = vr + wr * cr - wi * ci
        ni = vi + wr * ci + wi * cr
        xr_s[pl.ds(o, SUBLANES), :] = nr
        xi_s[pl.ds(o, SUBLANES), :] = ni
        return (jnp.broadcast_to(nr[pick:pick + 1], nr.shape), jnp.broadcast_to(ni[pick:pick + 1], ni.shape))

    return lax.fori_loop(0, groups, step, (c0r, c0i))


def _s5_state_scan(u_b, bbr, bbi, pr, pi, rm, xr_s, xi_s, c0r, c0i):
    bur = _dot(u_b, bbr)
    bui = _dot(u_b, bbi)
    bur, bui = _scan_in_groups(bur, bui, pr, pi, rm, False)
    xr_s[...] = bur
    xi_s[...] = bui
    w8r = jnp.concatenate(pr, axis=0)
    w8i = jnp.concatenate(pi, axis=0)
    return _carry_over_groups(xr_s, xi_s, w8r, w8i, c0r, c0i, False)


def _s5_fwd(u, a_re, a_im, bbr, bbi, cr, ci, d_skip, n_seq):
    n = u.shape[0]
    seq_len = n // n_seq
    nt = seq_len // S5_CHUNK
    tc = S5_CHUNK

    def body(u_ref, ar_ref, ai_ref, bbr_ref, bbi_ref, cr_ref, ci_ref, d_ref, y_ref, str_ref, sti_ref,
             xr_s, xi_s, car_r, car_i):
        t = pl.program_id(2)

        @pl.when(t == 0)
        def _():
            car_r[...] = jnp.zeros_like(car_r)
            car_i[...] = jnp.zeros_like(car_i)
        pr, pi = _cpow_rows(ar_ref[0], ai_ref[0], SUBLANES)
        rm = lax.broadcasted_iota(jnp.int32, (tc, S5_STATES), 0) & (SUBLANES - 1)
        str_ref[0, 0] = car_r[...]
        sti_ref[0, 0] = car_i[...]
        u_t = u_ref[...]
        cfr, cfi = _s5_state_scan(u_t.astype(BF16), bbr_ref[0], bbi_ref[0], pr, pi, rm, xr_s, xi_s,
                                  car_r[...], car_i[...])
        car_r[...] = cfr
        car_i[...] = cfi
        y = _dot(xr_s[...].astype(BF16), cr_ref[0]) - _dot(xi_s[...].astype(BF16), ci_ref[0])
        y_ref[...] = y + d_ref[...] * u_t

    u_spec = pl.BlockSpec((tc, LANES), lambda cb, b, t: (b * nt + t, cb))
    a_spec = pl.BlockSpec((1, 1, S5_STATES), lambda cb, b, t: (cb, 0, 0))
    bb_spec = pl.BlockSpec((1, LANES, S5_STATES), lambda cb, b, t: (cb, 0, 0))
    c_spec = pl.BlockSpec((1, S5_STATES, LANES), lambda cb, b, t: (cb, 0, 0))
    st_spec = pl.BlockSpec((1, 1, SUBLANES, S5_STATES), lambda cb, b, t: (cb, b * nt + t, 0, 0))
    st_shape = jax.ShapeDtypeStruct((S5_BLOCKS, n_seq * nt, SUBLANES, S5_STATES), F32)
    return pl.pallas_call(
        body, name="s5_fwd", grid=(S5_BLOCKS, n_seq, nt),
        in_specs=[u_spec, a_spec, a_spec, bb_spec, bb_spec, c_spec, c_spec,
                  pl.BlockSpec((1, LANES), lambda cb, b, t: (0, cb))],
        out_specs=(u_spec, st_spec, st_spec),
        out_shape=(jax.ShapeDtypeStruct((n, D_SSM), F32), st_shape, st_shape),
        scratch_shapes=[pltpu.VMEM((tc, S5_STATES), F32), pltpu.VMEM((tc, S5_STATES), F32),
                        pltpu.VMEM((SUBLANES, S5_STATES), F32), pltpu.VMEM((SUBLANES, S5_STATES), F32)],
        compiler_params=_cparams("parallel", "arbitrary", "arbitrary"),
    )(u, a_re, a_im, bbr, bbi, cr, ci, d_skip)


def _s5_bwd(u, dy, st_r, st_i, a_re, a_im, bbr, bbi, cr, ci, d_skip, n_seq):
    n = u.shape[0]
    seq_len = n // n_seq
    nt = seq_len // S5_CHUNK
    tc = S5_CHUNK

    def body(u_ref, dy_ref, str_ref, sti_ref, ar_ref, ai_ref, bbr_ref, bbi_ref, cr_ref, ci_ref, d_ref,
             du_ref, dbbr_ref, dbbi_ref, dcr_ref, dci_ref, dar_ref, dai_ref, dd_ref,
             xr_s, xi_s, gr_s, gi_s, car_r, car_i):
        b, t = pl.program_id(1), pl.program_id(2)

        @pl.when((b == 0) & (t == 0))
        def _():
            for ref in (dbbr_ref, dbbi_ref, dcr_ref, dci_ref, dar_ref, dai_ref, dd_ref):
                ref[...] = jnp.zeros_like(ref)

        @pl.when(t == 0)
        def _():
            car_r[...] = jnp.zeros_like(car_r)
            car_i[...] = jnp.zeros_like(car_i)
        ar, ai = ar_ref[0], ai_ref[0]
        pr, pi = _cpow_rows(ar, ai, SUBLANES)
        row = lax.broadcasted_iota(jnp.int32, (tc, S5_STATES), 0)
        rm = row & (SUBLANES - 1)
        u_t = u_ref[...]
        u_b = u_t.astype(BF16)
        dy_t = dy_ref[...]
        dy_b = dy_t.astype(BF16)
        s0r, s0i = str_ref[0, 0], sti_ref[0, 0]
        _s5_state_scan(u_b, bbr_ref[0], bbi_ref[0], pr, pi, rm, xr_s, xi_s, s0r, s0i)
        xr, xi = xr_s[...], xi_s[...]
        gr = _dot_nt(dy_b, cr_ref[0])
        gi = -_dot_nt(dy_b, ci_ref[0])
        npi = [-v for v in pi]
        gr, gi = _scan_in_groups(gr, gi, pr, npi, rm, True)
        gr_s[...] = gr
        gi_s[...] = gi
        w8r = jnp.concatenate(pr[::-1], axis=0)
        w8i = jnp.concatenate(npi[::-1], axis=0)
        cfr, cfi = _carry_over_groups(gr_s, gi_s, w8r, w8i, car_r[...], car_i[...], True)
        car_r[...] = cfr
        car_i[...] = cfi
        gr, gi = gr_s[...], gi_s[...]
        gr_b, gi_b = gr.astype(BF16), gi.astype(BF16)
        du_ref[...] = _dot_nt(gr_b, bbr_ref[0]) + _dot_nt(gi_b, bbi_ref[0]) + d_ref[...] * dy_t
        dbbr_ref[0] += _dot_tn(u_b, gr_b)
        dbbi_ref[0] += _dot_tn(u_b, gi_b)
        dcr_ref[0] += _dot_tn(xr.astype(BF16), dy_b)
        dci_ref[0] -= _dot_tn(xi.astype(BF16), dy_b)
        dd_ref[0] += jnp.sum((dy_t * u_t).reshape(tc // SUBLANES, SUBLANES, LANES), axis=0)
        first = row == 0
        xpr = jnp.where(first, jnp.broadcast_to(s0r[0:1], xr.shape), pltpu.roll(xr, 1, 0))
        xpi = jnp.where(first, jnp.broadcast_to(s0i[0:1], xi.shape), pltpu.roll(xi, 1, 0))
        shp = (tc // SUBLANES, SUBLANES, S5_STATES)
        dar_ref[0] += jnp.sum((gr * xpr + gi * xpi).reshape(shp), axis=0)
        dai_ref[0] += jnp.sum((gi * xpr - gr * xpi).reshape(shp), axis=0)

    u_spec = pl.BlockSpec((tc, LANES), lambda cb, b, t: (b * nt + nt - 1 - t, cb))
    a_spec = pl.BlockSpec((1, 1, S5_STATES), lambda cb, b, t: (cb, 0, 0))
    bb_spec = pl.BlockSpec((1, LANES, S5_STATES), lambda cb, b, t: (cb, 0, 0))
    c_spec = pl.BlockSpec((1, S5_STATES, LANES), lambda cb, b, t: (cb, 0, 0))
    st_spec = pl.BlockSpec((1, 1, SUBLANES, S5_STATES), lambda cb, b, t: (cb, b * nt + nt - 1 - t, 0, 0))
    da_spec = pl.BlockSpec((1, SUBLANES, S5_STATES), lambda cb, b, t: (cb, 0, 0))
    dd_spec = pl.BlockSpec((1, SUBLANES, LANES), lambda cb, b, t: (cb, 0, 0))
    big = pltpu.VMEM((tc, S5_STATES), F32)
    small = pltpu.VMEM((SUBLANES, S5_STATES), F32)
    return pl.pallas_call(
        body, name="s5_bwd", grid=(S5_BLOCKS, n_seq, nt),
        in_specs=[u_spec, u_spec, st_spec, st_spec, a_spec, a_spec, bb_spec, bb_spec, c_spec, c_spec,
                  pl.BlockSpec((1, LANES), lambda cb, b, t: (0, cb))],
        out_specs=(u_spec, bb_spec, bb_spec, c_spec, c_spec, da_spec, da_spec, dd_spec),
        out_shape=(jax.ShapeDtypeStruct((n, D_SSM), F32),
                   jax.ShapeDtypeStruct((S5_BLOCKS, LANES, S5_STATES), F32),
                   jax.ShapeDtypeStruct((S5_BLOCKS, LANES, S5_STATES), F32),
                   jax.ShapeDtypeStruct((S5_BLOCKS, S5_STATES, LANES), F32),
                   jax.ShapeDtypeStruct((S5_BLOCKS, S5_STATES, LANES), F32),
                   jax.ShapeDtypeStruct((S5_BLOCKS, SUBLANES, S5_STATES), F32),
                   jax.ShapeDtypeStruct((S5_BLOCKS, SUBLANES, S5_STATES), F32),
                   jax.ShapeDtypeStruct((S5_BLOCKS, SUBLANES, LANES), F32)),
        scratch_shapes=[big, big, big, big, small, small],
        compiler_params=_cparams("parallel", "arbitrary", "arbitrary"),
    )(u, dy, st_r, st_i, a_re, a_im, bbr, bbi, cr, ci, d_skip)


CUM_BLOCK = 128


def _tri(lower):
    r = lax.broadcasted_iota(jnp.int32, (CUM_BLOCK, CUM_BLOCK), 0)
    c = lax.broadcasted_iota(jnp.int32, (CUM_BLOCK, CUM_BLOCK), 1)
    return jnp.where(r >= c if lower else r <= c, 1.0, 0.0).astype(F32)


def _fprep_fwd(fl, bf, n_seq):
    n = fl.shape[0]
    seq_len = n // n_seq
    nb = seq_len // CUM_BLOCK

    def body(fl_ref, bf_ref, cum_ref):
        tril = _tri(True)
        carry = jnp.zeros((1, LANES), F32)
        for blk in range(nb):
            rows = slice(blk * CUM_BLOCK, (blk + 1) * CUM_BLOCK)
            lf = jax.nn.log_sigmoid(fl_ref[rows, :] + bf_ref[...])
            cs = jnp.dot(tril, lf, preferred_element_type=F32, precision=HIGHEST) + carry
            cum_ref[rows, :] = cs
            carry = cs[CUM_BLOCK - 1:CUM_BLOCK, :]

    spec = pl.BlockSpec((seq_len, LANES), lambda b: (b, 0))
    return pl.pallas_call(
        body, name="fprep_fwd", grid=(n_seq,), in_specs=[spec, pl.BlockSpec((1, LANES), lambda b: (0, 0))],
        out_specs=spec, out_shape=jax.ShapeDtypeStruct((n, LANES), F32), compiler_params=_cparams("parallel"),
    )(fl, bf)


def _fprep_bwd(dcum, fl, bf, n_seq):
    n = fl.shape[0]
    seq_len = n // n_seq
    nb = seq_len // CUM_BLOCK

    def body(dcum_ref, fl_ref, bf_ref, dfl_ref, dbf_ref):
        triu = _tri(False)
        lane = lax.broadcasted_iota(jnp.int32, (CUM_BLOCK, LANES), 1)
        carry = jnp.zeros((1, LANES), F32)
        total = jnp.zeros((1, LANES), F32)
        for blk in reversed(range(nb)):
            rows = slice(blk * CUM_BLOCK, (blk + 1) * CUM_BLOCK)
            rs = jnp.dot(triu, dcum_ref[rows, :], preferred_element_type=F32, precision=HIGHEST) + carry
            carry = rs[0:1, :]
            _, vjp = jax.vjp(jax.nn.log_sigmoid, fl_ref[rows, :] + bf_ref[...])
            dz = jnp.where(lane < N_HEADS, vjp(rs)[0], 0.0)
            dfl_ref[rows, :] = dz
            total = total + jnp.sum(dz, axis=0, keepdims=True)
        dbf_ref[0] = total

    spec = pl.BlockSpec((seq_len, LANES), lambda b: (b, 0))
    return pl.pallas_call(
        body, name="fprep_bwd", grid=(n_seq,), in_specs=[spec, spec, pl.BlockSpec((1, LANES), lambda b: (0, 0))],
        out_specs=(spec, pl.BlockSpec((1, 1, LANES), lambda b: (b, 0, 0))),
        out_shape=(jax.ShapeDtypeStruct((n, LANES), F32), jax.ShapeDtypeStruct((n_seq, 1, LANES), F32)),
        compiler_params=_cparams("parallel"),
    )(dcum, fl, bf)


ATT_TQ = 256
ATT_KSTEP = 512
ATT_SCALE = HEAD_DIM ** -0.5
NEG_BIG = -1e30


def _attn_scores(qe, kb, cq, ck, causal):
    s = _dot_nt(qe, kb) * ATT_SCALE + cq - ck
    return jnp.where(causal, s, NEG_BIG)


def _causal_mask(qi, kend):
    r = lax.broadcasted_iota(jnp.int32, (ATT_TQ, kend), 0) + qi * ATT_TQ
    c = lax.broadcasted_iota(jnp.int32, (ATT_TQ, kend), 1)
    return r >= c


def _attn_specs(n_seq, seq_len):
    nq = seq_len // ATT_TQ
    q_spec = pl.BlockSpec((ATT_TQ, LANES), lambda b, h, q: (b * nq + q, h))
    k_spec = pl.BlockSpec((seq_len, LANES), lambda b, h, q: (b, N_HEADS // 2 + h))
    v_spec = pl.BlockSpec((seq_len, LANES), lambda b, h, q: (b, N_HEADS + h))
    cq_spec = pl.BlockSpec((1, 2, ATT_TQ, 1), lambda b, h, q: (b, h, q, 0))
    ck_spec = pl.BlockSpec((1, 2, 1, seq_len), lambda b, h, q: (b, h, 0, 0))
    return nq, q_spec, k_spec, v_spec, cq_spec, ck_spec


def _head_selectors():
    head0 = lax.broadcasted_iota(jnp.int32, (1, LANES), 1) < HEAD_DIM
    return head0, (head0, jnp.logical_not(head0))


def _for_key_range(qi, seq_len, run):
    per = ATT_KSTEP // ATT_TQ
    for g in range(seq_len // ATT_KSTEP):
        pl.when(qi // per == g)(functools.partial(run, (g + 1) * ATT_KSTEP))


def _attn_fwd(qkv, cq, ck, n_seq):
    n = qkv.shape[0]
    seq_len = n // n_seq
    nq, q_spec, k_spec, v_spec, cq_spec, ck_spec = _attn_specs(n_seq, seq_len)

    def body(q_ref, k_ref, v_ref, cq_ref, ck_ref, o_ref, lse_ref):
        qi = pl.program_id(2)
        q2 = q_ref[...]
        head0, sels = _head_selectors()
        qe = [jnp.where(sel, q2, 0.0).astype(BF16) for sel in sels]

        def run(kend):
            kb = k_ref[0:kend, :].astype(BF16)
            vb = v_ref[0:kend, :].astype(BF16)
            causal = _causal_mask(qi, kend)
            outs = []
            for e in range(2):
                s = _attn_scores(qe[e], kb, cq_ref[0, e], ck_ref[0, e, :, 0:kend], causal)
                mx = jnp.max(s, axis=1, keepdims=True)
                p = jnp.exp(s - mx)
                den = jnp.sum(p, axis=1, keepdims=True)
                outs.append(_dot(p.astype(BF16), vb) / den)
                lse_ref[0, e] = mx + jnp.log(den)
            o_ref[...] = jnp.where(head0, outs[0], outs[1])

        _for_key_range(qi, seq_len, run)

    return pl.pallas_call(
        body, name="attn_fwd", grid=(n_seq, N_HEADS // 2, nq),
        in_specs=[q_spec, k_spec, v_spec, cq_spec, ck_spec],
        out_specs=(q_spec, cq_spec),
        out_shape=(jax.ShapeDtypeStruct((n, D_ATTN), F32), jax.ShapeDtypeStruct((n_seq, N_HEADS, seq_len, 1), F32)),
        compiler_params=_cparams("parallel", "parallel", "parallel"),
    )(qkv, qkv, qkv, cq, ck)


def _attn_bwd(qkv, cq, ck, o, do, lse, n_seq):
    n = qkv.shape[0]
    seq_len = n // n_seq
    nq, q_spec, k_spec, v_spec, cq_spec, ck_spec = _attn_specs(n_seq, seq_len)
    kv_out = pl.BlockSpec((seq_len, LANES), lambda b, h, q: (b, h))

    def body(q_ref, k_ref, v_ref, cq_ref, ck_ref, o_ref, do_ref, lse_ref, dq_ref, dk_ref, dv_ref, dcq_ref, dck_ref):
        qi = pl.program_id(2)

        @pl.when(qi == 0)
        def _():
            dk_ref[...] = jnp.zeros_like(dk_ref)
            dv_ref[...] = jnp.zeros_like(dv_ref)
            dck_ref[...] = jnp.zeros_like(dck_ref)
        q2 = q_ref[...]
        do2 = do_ref[...]
        o2 = o_ref[...]
        head0, sels = _head_selectors()
        qe = [jnp.where(sel, q2, 0.0).astype(BF16) for sel in sels]
        doe = [jnp.where(sel, do2, 0.0) for sel in sels]
        doe_b = [d.astype(BF16) for d in doe]
        delta = [jnp.sum(d * o2, axis=1, keepdims=True) for d in doe]

        def run(kend):
            kb = k_ref[0:kend, :].astype(BF16)
            vb = v_ref[0:kend, :].astype(BF16)
            causal = _causal_mask(qi, kend)
            dqs = []
            dk = jnp.zeros((kend, LANES), F32)
            dv = jnp.zeros((kend, LANES), F32)
            for e in range(2):
                s = _attn_scores(qe[e], kb, cq_ref[0, e], ck_ref[0, e, :, 0:kend], causal)
                p = jnp.exp(s - lse_ref[0, e])
                ds = p * (_dot_nt(doe_b[e], vb) - delta[e])
                ds_b = ds.astype(BF16)
                dqs.append(_dot(ds_b, kb))
                dk = dk + _dot_tn(ds_b, qe[e])
                dv = dv + _dot_tn(p.astype(BF16), doe_b[e])
                dcq_ref[0, e] = jnp.sum(ds, axis=1, keepdims=True)
                dck_ref[0, e, :, 0:kend] -= jnp.sum(ds, axis=0, keepdims=True)
            dk_ref[0:kend, :] += dk * ATT_SCALE
            dv_ref[0:kend, :] += dv
            dq_ref[...] = jnp.where(head0, dqs[0], dqs[1]) * ATT_SCALE

        _for_key_range(qi, seq_len, run)

    return pl.pallas_call(
        body, name="attn_bwd", grid=(n_seq, N_HEADS // 2, nq),
        in_specs=[q_spec, k_spec, v_spec, cq_spec, ck_spec, q_spec, q_spec, cq_spec],
        out_specs=(q_spec, kv_out, kv_out, cq_spec, ck_spec),
        out_shape=(jax.ShapeDtypeStruct((n, D_ATTN), F32), jax.ShapeDtypeStruct((n, D_ATTN), F32),
                   jax.ShapeDtypeStruct((n, D_ATTN), F32),
                   jax.ShapeDtypeStruct((n_seq, N_HEADS, seq_len, 1), F32),
                   jax.ShapeDtypeStruct((n_seq, N_HEADS, 1, seq_len), F32)),
        compiler_params=_cparams("parallel", "parallel", "arbitrary"),
    )(qkv, qkv, qkv, cq, ck, o, do, lse)


WEIGHT_NAMES = ("norm_mix", "w_in", "b_forget", "lam_re", "lam_im", "b_re", "b_im", "c_re", "c_im", "d_skip", "log_dt",
                "w_glu", "b_glu", "q_norm", "k_norm", "norm_out_ssm", "norm_out_attn", "w_out", "norm_ffn", "w_up",
                "conv_w", "conv_b", "w_down")
SHARDED = ("w_in", "w_glu", "w_out", "w_up", "conv_w", "w_down")
ADAM_ROWS = {"w_in": 256, "w_glu": 64, "w_out": 128, "w_up": 128, "conv_w": 3, "w_down": 344}
PACK_ROWS = SUBLANES * LANES
ROWS_DOWN = D_FF // N_DEV
ROWS_OUT = D_MODEL // N_DEV
ROWS_GLU = D_SSM * D_SSM // N_DEV // D_MODEL


def _pad_to(a, axis, size):
    pad = [(0, 0)] * a.ndim
    pad[axis] = (0, size - a.shape[axis])
    return jnp.pad(a, pad)


def _block_diag(t, transpose):
    t4 = t.reshape(S5_BLOCKS, 8, SSM_GROUP, SSM_STATE)
    eye = jnp.eye(8, dtype=t.dtype)
    if transpose:
        e = jnp.swapaxes(t4, 2, 3)[:, :, :, None, :] * eye[None, :, None, :, None]
        return e.reshape(S5_BLOCKS, S5_STATES, LANES)
    e = t4[:, :, :, None, :] * eye[None, :, None, :, None]
    return e.reshape(S5_BLOCKS, LANES, S5_STATES)


def _block_diag_extract(m, transpose):
    if transpose:
        m5 = m.reshape(S5_BLOCKS, 8, SSM_STATE, 8, SSM_GROUP)
        d = jnp.stack([m5[:, i, :, i, :] for i in range(8)], axis=1)
        return jnp.swapaxes(d, 2, 3).reshape(N_GROUPS, SSM_GROUP, SSM_STATE)
    m5 = m.reshape(S5_BLOCKS, 8, SSM_GROUP, 8, SSM_STATE)
    d = jnp.stack([m5[:, i, :, i, :] for i in range(8)], axis=1)
    return d.reshape(N_GROUPS, SSM_GROUP, SSM_STATE)


def _pack(pieces):
    flat = jnp.concatenate([p.reshape(-1).astype(F32) for p in pieces])
    size = -(-flat.shape[0] // PACK_ROWS) * PACK_ROWS
    return _pad_to(flat, 0, size).reshape(-1, LANES)


def _unpack(packed, shapes):
    flat = packed.reshape(-1)
    out, off = [], 0
    for shp in shapes:
        size = math.prod(shp)
        out.append(flat[off:off + size].reshape(shp))
        off += size
    return out


def kernel(x, norm_mix, w_in, b_forget, lam_re, lam_im, b_re, b_im, c_re, c_im, d_skip, log_dt, w_glu, b_glu, q_norm, k_norm, norm_out_ssm, norm_out_attn, w_out, norm_ffn, w_up, conv_w, conv_b, w_down, loss_target, m_norm_mix, m_w_in, m_b_forget, m_lam_re, m_lam_im, m_b_re, m_b_im, m_c_re, m_c_im, m_d_skip, m_log_dt, m_w_glu, m_b_glu, m_q_norm, m_k_norm, m_norm_out_ssm, m_norm_out_attn, m_w_out, m_norm_ffn, m_w_up, m_conv_w, m_conv_b, m_w_down, v_norm_mix, v_w_in, v_b_forget, v_lam_re, v_lam_im, v_b_re, v_b_im, v_c_re, v_c_im, v_d_skip, v_log_dt, v_w_glu, v_b_glu, v_q_norm, v_k_norm, v_norm_out_ssm, v_norm_out_attn, v_w_out, v_norm_ffn, v_w_up, v_conv_w, v_conv_b, v_w_down):
    given = dict(locals())
    weights = {k: given[k] for k in WEIGHT_NAMES}
    mom1 = {k: given["m_" + k] for k in WEIGHT_NAMES}
    mom2 = {k: given["v_" + k] for k in WEIGHT_NAMES}
    n_seq, seq_len, _ = x.shape
    n = n_seq * seq_len
    xf = x.reshape(n, D_MODEL)
    target = loss_target.reshape(n, D_MODEL)
    me_idx = 4 * lax.axis_index("x") + 2 * lax.axis_index("y") + lax.axis_index("c")

    g_in, g_cw = _exchange([w_in[0].astype(BF16), conv_w[0]], [False, False], "gather_w_in")
    rest_flags = [False] * 2
    rows_w = jnp.concatenate([w_down[0], w_out[0], w_glu[0].reshape(ROWS_GLU, D_MODEL)], axis=0).astype(BF16)
    w_sems = _exchange_start([rows_w, w_up[0].astype(BF16)], rest_flags, g_in, "gather_rest_start", 0)
    norm_mix = norm_mix + w_sems[4][0, 0]
    w_in_p = _pad_to(jnp.swapaxes(g_in, 0, 1).reshape(D_MODEL, D_IN), 1, D_IN_PAD)

    lr3 = lam_re[0].reshape(N_GROUPS, 1, SSM_STATE)
    li3 = lam_im[0].reshape(N_GROUPS, 1, SSM_STATE)
    ldt3 = log_dt[0].reshape(N_GROUPS, 1, 1)
    br_t = jnp.swapaxes(b_re[0], 1, 2)
    bi_t = jnp.swapaxes(b_im[0], 1, 2)
    ab_re, ab_im, bb_re, bb_im = _s5_param_fwd(lr3, li3, ldt3, br_t, bi_t)
    a_re = ab_re.reshape(S5_BLOCKS, 1, S5_STATES)
    a_im = ab_im.reshape(S5_BLOCKS, 1, S5_STATES)
    bbr = _block_diag(bb_re, False).astype(BF16)
    bbi = _block_diag(bb_im, False).astype(BF16)
    cr = _block_diag(c_re[0], True).astype(BF16)
    ci = _block_diag(c_im[0], True).astype(BF16)

    avg = jnp.kron(jnp.eye(N_HEADS, dtype=F32), jnp.full((HEAD_DIM, HEAD_DIM), 1.0 / HEAD_DIM, F32))
    qg = jnp.tile(q_norm, (1, N_HEADS))
    kg = jnp.tile(k_norm, (1, N_HEADS))
    hn, u, qkv, raw, fl = _inproj_fwd(xf, norm_mix, w_in_p, avg, qg, kg)
    yc, st_r, st_i = _s5_fwd(u, a_re, a_im, bbr, bbi, cr, ci, d_skip, n_seq)
    bf = _pad_to(b_forget, 1, LANES)
    cum = _fprep_fwd(fl, bf, n_seq)
    cum8 = jnp.swapaxes(cum[:, :N_HEADS].reshape(n_seq, seq_len, N_HEADS), 1, 2)
    cq = cum8[:, :, :, None]
    ck = cum8[:, :, None, :]
    ya, lse = _attn_fwd(qkv, cq, ck, n_seq)
    (own_rows, own_up), (g_rows, g_up) = _exchange_wait(w_sems[0], w_sems[1], w_sems[2], w_sems[3], rest_flags, ya,
                                                        "gather_rest_wait")
    my_slot = lax.broadcasted_iota(jnp.int32, (N_DEV, 1, 1), 0) == me_idx
    g_rows = jnp.where(my_slot, own_rows[None], g_rows)
    g_up = jnp.where(my_slot, own_up[None], g_up)
    g_down = g_rows[:, :ROWS_DOWN]
    g_out = g_rows[:, ROWS_DOWN:ROWS_DOWN + ROWS_OUT]
    g_glu = g_rows[:, ROWS_DOWN + ROWS_OUT:]
    w_glu_f = g_glu.reshape(D_SSM, D_SSM)
    w_out_f = g_out.reshape(D_MODEL, D_MODEL)
    w_up_f = jnp.swapaxes(g_up, 0, 1).reshape(D_MODEL, 2 * D_FF)
    wup_g = _pad_to(w_up_f[:, :D_FF], 1, D_FF_PAD)
    wup_v = _pad_to(w_up_f[:, D_FF:], 1, D_FF_PAD)
    cw_f = jnp.swapaxes(g_cw, 0, 1).reshape(3, 2 * D_FF)
    cw4 = jnp.concatenate([cw_f, conv_b], axis=0)
    cw_g = _pad_to(_pad_to(cw4[:, :D_FF], 1, D_FF_PAD), 0, SUBLANES)
    cw_v = _pad_to(_pad_to(cw4[:, D_FF:], 1, D_FF_PAD), 0, SUBLANES)
    w_down_p = _pad_to(g_down.reshape(D_FF, D_MODEL), 0, D_FF_PAD)
    ys = _glu_fwd(yc, w_glu_f, b_glu)
    h1, hn2, mixed = _mix_fwd(xf, ys, ya, norm_out_ssm, norm_out_attn, w_out_f, norm_ffn)
    ug, uv, dy, loss_part = _ffn_fwd(hn2, h1, target, wup_g, wup_v, cw_g, cw_v, w_down_p, seq_len)
    loss = lax.psum(0.5 * jnp.sum(loss_part) / D_MODEL, ("x", "y", "c"))

    dug, duv, act, dhn2, dcg, dcv = _ffn_bwd(dy, ug, uv, wup_g, wup_v, cw_g, cw_v, w_down_p, seq_len)
    dh1, dys, dya, d_gs, d_ga, d_gf = _mix_bwd(dy, dhn2[None], h1, ys, ya, norm_out_ssm, norm_out_attn, w_out_f, norm_ffn)
    dyc, gl_b, dz_b, d_bglu = _glu_bwd(yc, dys, w_glu_f, b_glu)

    gw_glu = _tn_matmul(gl_b, dz_b, "dw_glu", D_SSM, D_SSM)
    gw_out = _tn_matmul(mixed, dh1, "dw_out", D_MODEL, D_MODEL)
    gw_up = jnp.concatenate([_tn_matmul(hn2, dug, "dw_up_gate", D_MODEL, D_FF_PAD // 2)[:, :D_FF],
                             _tn_matmul(hn2, duv, "dw_up_val", D_MODEL, D_FF_PAD // 2)[:, :D_FF]], axis=1)
    gw_down = _tn_matmul(act, dy, "dw_down", D_FF_PAD // 2, D_MODEL)[:D_FF]
    dcg2 = jnp.swapaxes(dcg, 0, 1).reshape(SUBLANES, D_FF_PAD)[:, :D_FF]
    dcv2 = jnp.swapaxes(dcv, 0, 1).reshape(SUBLANES, D_FF_PAD)[:, :D_FF]
    g_conv = jnp.concatenate([dcg2, dcv2], axis=1)
    by_cols = lambda g, c: jnp.swapaxes(g.reshape(g.shape[0], N_DEV, c), 0, 1)
    early_flags = [True] * 2
    rows_g = jnp.concatenate([gw_down.reshape(N_DEV, ROWS_DOWN, D_MODEL), gw_out.reshape(N_DEV, ROWS_OUT, D_MODEL),
                              gw_glu.reshape(N_DEV, ROWS_GLU, D_MODEL)], axis=1)
    g_sems = _exchange_start([rows_g, by_cols(gw_up, 2 * D_FF // N_DEV)], early_flags, dyc, "grad_early_start", 1)
    started = g_sems[4][0, 0]

    du, dbbr, dbbi, dcr, dci, dar, dai, ddk = _s5_bwd(u, dyc, st_r, st_i, a_re, a_im, bbr, bbi, cr, ci,
                                                      d_skip + started, n_seq)
    dqn, dkn, dv, dcq, dck = _attn_bwd(qkv, cq, ck, ya, dya, lse + started, n_seq)
    dcum8 = dcq[:, :, :, 0] + dck.reshape(n_seq, N_HEADS, seq_len)
    dcum = _pad_to(jnp.swapaxes(dcum8, 1, 2).reshape(n, N_HEADS), 1, LANES)
    dfl, dbf = _fprep_bwd(dcum, fl, bf, n_seq)
    dx, dproj, d_gmix, d_qg, d_kg = _inproj_bwd(xf, norm_mix, w_in_p, avg, qg, kg, raw, du, dqn, dkn, dv, dfl, dh1)

    gw_in = _tn_matmul(hn, dproj, "dw_in", D_MODEL, D_IN_PAD)[:, :D_IN]
    partial_small = {
        "norm_mix": d_gmix, "b_forget": jnp.sum(dbf, axis=(0, 1))[:N_HEADS],
        "ab_re": jnp.sum(dar, axis=1), "ab_im": jnp.sum(dai, axis=1),
        "bb_re": _block_diag_extract(dbbr, False), "bb_im": _block_diag_extract(dbbi, False),
        "c_re": _block_diag_extract(dcr, True), "c_im": _block_diag_extract(dci, True),
        "d_skip": jnp.sum(ddk, axis=1), "b_glu": d_bglu,
        "q_norm": jnp.sum(d_qg.reshape(N_HEADS, HEAD_DIM), axis=0),
        "k_norm": jnp.sum(d_kg.reshape(N_HEADS, HEAD_DIM), axis=0),
        "norm_out_ssm": d_gs, "norm_out_attn": d_ga, "norm_ffn": d_gf, "conv_b": g_conv[3],
    }
    small_keys = tuple(partial_small)
    small_shapes = [partial_small[k].shape for k in small_keys]

    land_in, land_cw, small_parts = _exchange(
        [by_cols(gw_in, D_IN // N_DEV).astype(BF16), by_cols(g_conv[:3], 2 * D_FF // N_DEV),
         _pack([partial_small[k] for k in small_keys])], [True, True, False], "grad_late_exchange")
    (src_rows, src_up), (land_rows, land_up) = _exchange_wait(g_sems[0], g_sems[1], g_sems[2], g_sems[3], early_flags,
                                                              land_in, "grad_early_wait")
    land_out = land_rows[:, ROWS_DOWN:ROWS_DOWN + ROWS_OUT]
    land_glu = land_rows[:, ROWS_DOWN + ROWS_OUT:].reshape(N_DEV, -1, D_SSM)
    own_rows = lax.dynamic_index_in_dim(src_rows, me_idx, 0, keepdims=False)
    own_up = lax.dynamic_index_in_dim(src_up, me_idx, 0, keepdims=False)
    own = {"w_in": None, "conv_w": None, "w_up": own_up, "w_down": own_rows[:ROWS_DOWN],
           "w_out": own_rows[ROWS_DOWN:ROWS_DOWN + ROWS_OUT], "w_glu": own_rows[ROWS_DOWN + ROWS_OUT:].reshape(-1, D_SSM)}
    grads, deltas, new_m, new_v = {}, {}, {}, {}
    for name, land in zip(SHARDED, (land_in, land_glu, land_out, land_up, land_cw, land_rows)):
        g, d, nm, nv = _adam_sharded(land, own[name], weights[name][0], mom1[name][0], mom2[name][0], "adam_" + name,
                                     ADAM_ROWS[name])
        grads[name], deltas[name], new_m[name], new_v[name] = g[None], d[None], nm[None], nv[None]

    summed = dict(zip(small_keys, _unpack(_sum_partials(small_parts, "sum_small_grads"), small_shapes)))
    dlr, dli, dldt, dbr_t, dbi_t = _s5_param_bwd(
        lr3, li3, ldt3, br_t, bi_t, summed["ab_re"].reshape(lr3.shape), summed["ab_im"].reshape(lr3.shape),
        summed["bb_re"], summed["bb_im"])
    small_grads = {
        "norm_mix": summed["norm_mix"], "b_forget": summed["b_forget"], "lam_re": dlr, "lam_im": dli,
        "b_re": jnp.swapaxes(dbr_t, 1, 2), "b_im": jnp.swapaxes(dbi_t, 1, 2), "c_re": summed["c_re"], "c_im": summed["c_im"],
        "d_skip": summed["d_skip"], "log_dt": dldt, "b_glu": summed["b_glu"], "q_norm": summed["q_norm"],
        "k_norm": summed["k_norm"], "norm_out_ssm": summed["norm_out_ssm"], "norm_out_attn": summed["norm_out_attn"],
        "norm_ffn": summed["norm_ffn"], "conv_b": summed["conv_b"],
    }
    repl = tuple(k for k in WEIGHT_NAMES if k not in SHARDED)
    g_list = [small_grads[k].reshape(weights[k].shape) for k in repl]
    d_list, m_list, v_list = _adam_replicated(g_list, [weights[k] for k in repl], [mom1[k] for k in repl],
                                              [mom2[k] for k in repl], "adam_replicated")
    for k, g, d, nm, nv in zip(repl, g_list, d_list, m_list, v_list):
        grads[k], deltas[k], new_m[k], new_v[k] = g, d, nm, nv

    grad_x = dx.reshape(x.shape)
    return (loss, grad_x, *[grads[k] for k in WEIGHT_NAMES], *[deltas[k] for k in WEIGHT_NAMES],
            *[new_m[k] for k in WEIGHT_NAMES], *[new_v[k] for k in WEIGHT_NAMES])
```

```python
import functools
import math

import jax
import jax.numpy as jnp
from jax import lax
from jax.experimental import pallas as pl
from jax.experimental.pallas import tpu as pltpu

F32 = jnp.float32
BF16 = jnp.bfloat16
HIGHEST = lax.Precision.HIGHEST

N_DEV = 8
D_MODEL = 1024
D_SSM = 512
D_ATTN = 512
N_HEADS = 8
HEAD_DIM = 64
N_GROUPS = 32
SSM_GROUP = 16
SSM_STATE = 64
D_FF = 2752
D_FF_PAD = 2816
D_IN = 2056
D_IN_PAD = 2176
EPS = 1e-6
LANES = 128
SUBLANES = 8
VMEM_LIMIT = 56 * 1024 * 1024

ADAM_LR = 0.001
ADAM_B1 = 0.9
ADAM_B2 = 0.999
ADAM_EPS = 1e-08
ADAM_WD = 0.01
ADAM_STEP = 10


def _cparams(*sem):
    return pltpu.CompilerParams(dimension_semantics=sem, vmem_limit_bytes=VMEM_LIMIT)


def _dot(a, b, **kw):
    return jnp.dot(a, b, preferred_element_type=F32, **kw)


def _dot_nt(a, b):
    return lax.dot_general(a, b, (((1,), (1,)), ((), ())), preferred_element_type=F32)


def _dot_tn(a, b):
    return lax.dot_general(a, b, (((0,), (0,)), ((), ())), preferred_element_type=F32)


def _rms(x, g):
    return x * lax.rsqrt(jnp.mean(x * x, axis=-1, keepdims=True) + EPS) * g


def _split_dot(x, avg):
    hi = x.astype(BF16)
    lo = (x - hi.astype(F32)).astype(BF16)
    return _dot(hi, avg) + _dot(lo, avg)


@jax.custom_vjp
def _group_mean(x, avg):
    return _split_dot(x, avg)


def _group_mean_fwd(x, avg):
    return _split_dot(x, avg), avg


def _group_mean_bwd(avg, ct):
    return _split_dot(ct, avg), jnp.zeros_like(avg)


_group_mean.defvjp(_group_mean_fwd, _group_mean_bwd)


def _headnorm(q, avg, g):
    return q * lax.rsqrt(_group_mean(q * q, avg) + EPS) * g


def _exchange(srcs, scatter_flags, name):
    n = len(srcs)
    out_shape = []
    for s, sc in zip(srcs, scatter_flags):
        shp = s.shape if sc else (N_DEV,) + s.shape
        out_shape.append(jax.ShapeDtypeStruct(shp, s.dtype))

    def body(*refs):
        src = refs[:n]
        dst = refs[n:2 * n]
        send_sems, recv_sems, loc_sems = refs[2 * n:]
        x, y, c = lax.axis_index("x"), lax.axis_index("y"), lax.axis_index("c")
        me = 4 * x + 2 * y + c
        peers = []
        for j in range(1, N_DEV):
            px = 1 - x if (j >> 2) & 1 else x
            py = 1 - y if (j >> 1) & 1 else y
            pc = 1 - c if j & 1 else c
            peers.append(((px, py, pc), 4 * px + 2 * py + pc))
        local, sends = [], []
        for k in range(n):
            own = src[k].at[me] if scatter_flags[k] else src[k]
            lc = pltpu.make_async_copy(own, dst[k].at[me], loc_sems.at[k])
            lc.start()
            local.append(lc)
            for j, (pid, pidx) in enumerate(peers):
                s = src[k].at[pidx] if scatter_flags[k] else src[k]
                cp = pltpu.make_async_remote_copy(
                    src_ref=s, dst_ref=dst[k].at[me], send_sem=send_sems.at[k, j], recv_sem=recv_sems.at[k, j],
                    device_id=pid, device_id_type=pl.DeviceIdType.MESH)
                cp.start()
                sends.append(cp)
        for k in range(n):
            for j, (pid, pidx) in enumerate(peers):
                s = src[k].at[pidx] if scatter_flags[k] else src[k]
                pltpu.make_async_remote_copy(
                    src_ref=s, dst_ref=dst[k].at[pidx], send_sem=send_sems.at[k, j], recv_sem=recv_sems.at[k, j],
                    device_id=pid, device_id_type=pl.DeviceIdType.MESH).wait_recv()
        for cp in sends:
            cp.wait_send()
        for lc in local:
            lc.wait()

    any_spec = pl.BlockSpec(memory_space=pl.ANY)
    return pl.pallas_call(
        body, name=name, out_shape=tuple(out_shape),
        in_specs=[any_spec] * n, out_specs=tuple([any_spec] * n),
        scratch_shapes=[pltpu.SemaphoreType.DMA((n, N_DEV - 1)), pltpu.SemaphoreType.DMA((n, N_DEV - 1)),
                        pltpu.SemaphoreType.DMA((n,))],
        compiler_params=pltpu.CompilerParams(has_side_effects=True),
    )(*srcs)


def _peer_list():
    x, y, c = lax.axis_index("x"), lax.axis_index("y"), lax.axis_index("c")
    peers = []
    for j in range(1, N_DEV):
        px = 1 - x if (j >> 2) & 1 else x
        py = 1 - y if (j >> 1) & 1 else y
        pc = 1 - c if j & 1 else c
        peers.append(((px, py, pc), 4 * px + 2 * py + pc))
    return 4 * x + 2 * y + c, peers


def _split_copies(src, land, send_sems, recv_sems, scatter_flags, me, peers, incoming):
    copies = []
    for k in range(len(src)):
        for j, (pid, pidx) in enumerate(peers):
            s = src[k].at[pidx] if scatter_flags[k] else src[k]
            i = k * (N_DEV - 1) + j
            copies.append(pltpu.make_async_remote_copy(
                src_ref=s, dst_ref=land[k].at[pidx if incoming else me], send_sem=send_sems[i],
                recv_sem=recv_sems[i], device_id=pid, device_id_type=pl.DeviceIdType.MESH))
    return copies


def _exchange_start(srcs, scatter_flags, after, name, collective_id):
    n = len(srcs)
    ns = n * (N_DEV - 1)
    hbm = pl.BlockSpec(memory_space=pltpu.HBM)
    sem = pl.BlockSpec(memory_space=pltpu.SEMAPHORE)
    land_shapes = [s.shape if sc else (N_DEV,) + s.shape for s, sc in zip(srcs, scatter_flags)]

    def body(*refs):
        src, land = refs[:n], refs[n:2 * n]
        send_sems = refs[2 * n + 1:2 * n + 1 + ns]
        recv_sems = refs[2 * n + 1 + ns:2 * n + 1 + 2 * ns]
        token = refs[4 * n + 1 + 2 * ns]
        me, peers = _peer_list()
        barrier = pltpu.get_barrier_semaphore()
        for pid, _ in peers:
            pl.semaphore_signal(barrier, inc=1, device_id=pid, device_id_type=pl.DeviceIdType.MESH)
        pl.semaphore_wait(barrier, N_DEV - 1)
        for cp in _split_copies(src, land, send_sems, recv_sems, scatter_flags, me, peers, False):
            cp.start()
        token[...] = jnp.zeros_like(token)

    outs = pl.pallas_call(
        body, name=name,
        out_shape=(*[pltpu.SemaphoreType.DMA(())] * (2 * ns), *[pltpu.HBM(s.shape, s.dtype) for s in srcs],
                   *[pltpu.HBM(shp, s.dtype) for shp, s in zip(land_shapes, srcs)],
                   jax.ShapeDtypeStruct((SUBLANES, LANES), F32)),
        in_specs=[hbm] * (2 * n) + [pl.BlockSpec(memory_space=pl.ANY)],
        out_specs=(*[sem] * (2 * ns), *[hbm] * (2 * n), pl.BlockSpec(memory_space=pltpu.VMEM)),
        input_output_aliases={i: 2 * ns + i for i in range(2 * n)},
        compiler_params=pltpu.CompilerParams(has_side_effects=pltpu.SideEffectType.DATAFLOW_SIDE_EFFECTING,
                                             collective_id=collective_id),
    )(*[pltpu.with_memory_space_constraint(s, pltpu.HBM) for s in srcs],
      *[pltpu.with_memory_space_constraint(lax.empty(shp, s.dtype), pltpu.HBM) for shp, s in zip(land_shapes, srcs)],
      after)
    return (outs[:ns], outs[ns:2 * ns], outs[2 * ns:2 * ns + n], outs[2 * ns + n:2 * ns + 2 * n], outs[2 * ns + 2 * n])


def _exchange_wait(send_sems, recv_sems, srcs, lands, scatter_flags, after, name):
    n = len(srcs)
    ns = n * (N_DEV - 1)
    hbm = pl.BlockSpec(memory_space=pltpu.HBM)
    sem = pl.BlockSpec(memory_space=pltpu.SEMAPHORE)

    def body(*refs):
        src, land = refs[:n], refs[n:2 * n]
        s_sems = refs[2 * n:2 * n + ns]
        r_sems = refs[2 * n + ns:2 * n + 2 * ns]
        me, peers = _peer_list()
        for cp in _split_copies(src, land, s_sems, r_sems, scatter_flags, me, peers, True):
            cp.wait_send()
            cp.wait_recv()

    outs = pl.pallas_call(
        body, name=name,
        out_shape=tuple(pltpu.HBM(a.shape, a.dtype) for a in (*srcs, *lands)),
        in_specs=[hbm] * (2 * n) + [sem] * (2 * ns) + [pl.BlockSpec(memory_space=pl.ANY)],
        out_specs=tuple([hbm] * (2 * n)),
        input_output_aliases={i: i for i in range(2 * n)},
        compiler_params=pltpu.CompilerParams(has_side_effects=pltpu.SideEffectType.DATAFLOW_SIDE_EFFECTING),
    )(*srcs, *lands, *send_sems, *recv_sems, after)
    return outs[:n], outs[n:]


def _tn_matmul(a, b, name, tk, tm, out_rows=None, out_cols=None, tn=512):
    n_tok, k_dim = a.shape
    m_dim = b.shape[1]
    grid = (k_dim // tk, m_dim // tm, n_tok // tn)

    def body(a_ref, b_ref, o_ref):
        @pl.when(pl.program_id(2) == 0)
        def _():
            o_ref[...] = jnp.zeros_like(o_ref)
        o_ref[...] += _dot_tn(a_ref[...].astype(BF16), b_ref[...].astype(BF16))

    return pl.pallas_call(
        body, name=name, grid=grid,
        in_specs=[pl.BlockSpec((tn, tk), lambda i, j, k: (k, i)), pl.BlockSpec((tn, tm), lambda i, j, k: (k, j))],
        out_specs=pl.BlockSpec((tk, tm), lambda i, j, k: (i, j)),
        out_shape=jax.ShapeDtypeStruct((out_rows or k_dim, out_cols or m_dim), F32),
        compiler_params=_cparams("parallel", "parallel", "arbitrary"),
    )(a, b)


def _adam_math(g, w, m, v):
    m = ADAM_B1 * m + (1.0 - ADAM_B1) * g
    v = ADAM_B2 * v + (1.0 - ADAM_B2) * (g * g)
    m_hat = m / (1.0 - ADAM_B1 ** ADAM_STEP)
    v_hat = v / (1.0 - ADAM_B2 ** ADAM_STEP)
    delta = -ADAM_LR * (m_hat / (jnp.sqrt(v_hat) + ADAM_EPS) + ADAM_WD * w)
    return delta, m, v


def _adam_sharded(land, own, w, m, v, name, tr):
    _, r, c = w.shape

    def body(*refs):
        l_ref = refs[0]
        own_ref = refs[1] if own is not None else None
        w_ref, m_ref, v_ref, g_ref, d_ref, nm_ref, nv_ref = [ref.at[0] for ref in refs[-7:]]
        if own_ref is not None:
            x, y, z = lax.axis_index("x"), lax.axis_index("y"), lax.axis_index("c")
            me = 4 * x + 2 * y + z
            mine = own_ref[...].astype(F32)
        g = None
        for s in range(N_DEV):
            part = l_ref[s].astype(F32)
            if own_ref is not None:
                part = jnp.where(me == s, mine, part)
            g = part if g is None else g + part
        d, nm, nv = _adam_math(g, w_ref[...], m_ref[...], v_ref[...])
        g_ref[...] = g
        d_ref[...] = d
        nm_ref[...] = nm
        nv_ref[...] = nv

    spec = pl.BlockSpec((1, tr, c), lambda i: (0, i, 0))
    own_specs, own_args = ([pl.BlockSpec((tr, c), lambda i: (i, 0))], [own]) if own is not None else ([], [])
    return pl.pallas_call(
        body, name=name, grid=(r // tr,),
        in_specs=[pl.BlockSpec((N_DEV, tr, c), lambda i: (0, i, 0)), *own_specs, spec, spec, spec],
        out_specs=(spec, spec, spec, spec),
        out_shape=tuple(jax.ShapeDtypeStruct((1, r, c), F32) for _ in range(4)),
        compiler_params=_cparams("parallel"),
    )(land, *own_args, w, m, v)


def _sum_partials(parts, name):
    _, r, c = parts.shape

    def body(p_ref, o_ref):
        g = p_ref[0]
        for s in range(1, N_DEV):
            g = g + p_ref[s]
        o_ref[...] = g

    return pl.pallas_call(body, name=name, out_shape=jax.ShapeDtypeStruct((r, c), F32),
                          compiler_params=pltpu.CompilerParams(vmem_limit_bytes=VMEM_LIMIT))(parts)


def _adam_replicated(gs, ws, ms, vs, name):
    k = len(ws)

    def body(*refs):
        outs = refs[4 * k:]
        for i in range(k):
            d, nm, nv = _adam_math(refs[i][...], refs[k + i][...], refs[2 * k + i][...], refs[3 * k + i][...])
            outs[i][...] = d
            outs[k + i][...] = nm
            outs[2 * k + i][...] = nv

    outs = pl.pallas_call(body, name=name, out_shape=tuple(jax.ShapeDtypeStruct(w.shape, F32) for w in ws) * 3,
                          compiler_params=pltpu.CompilerParams(vmem_limit_bytes=VMEM_LIMIT))(*gs, *ws, *ms, *vs)
    return outs[:k], outs[k:2 * k], outs[2 * k:]


def _inproj_fwd(x, g, w_in, avg, qg, kg, tm=512):
    n = x.shape[0]

    def body(x_ref, g_ref, w_ref, a_ref, qg_ref, kg_ref, hn_ref, u_ref, qkv_ref, raw_ref, fl_ref):
        hn = _rms(x_ref[...], g_ref[...]).astype(BF16)
        hn_ref[...] = hn
        proj = _dot(hn, w_ref[...])
        u_ref[...] = proj[:, 0:512]
        q = proj[:, 512:1024]
        k = proj[:, 1024:1536]
        raw_ref[:, 0:512] = q
        raw_ref[:, 512:1024] = k
        qkv_ref[:, 0:512] = _headnorm(q, a_ref[...], qg_ref[...])
        qkv_ref[:, 512:1024] = _headnorm(k, a_ref[...], kg_ref[...])
        qkv_ref[:, 1024:1536] = proj[:, 1536:2048]
        fl_ref[...] = proj[:, 2048:D_IN_PAD]

    row = lambda w: pl.BlockSpec((tm, w), lambda i: (i, 0))
    full = lambda a: pl.BlockSpec(a.shape, lambda i: (0,) * a.ndim)
    return pl.pallas_call(
        body, name="inproj_fwd", grid=(n // tm,),
        in_specs=[row(D_MODEL), full(g), full(w_in), full(avg), full(qg), full(kg)],
        out_specs=(row(D_MODEL), row(512), row(1536), row(1024), row(LANES)),
        out_shape=(jax.ShapeDtypeStruct((n, D_MODEL), BF16), jax.ShapeDtypeStruct((n, 512), F32),
                   jax.ShapeDtypeStruct((n, 1536), F32), jax.ShapeDtypeStruct((n, 1024), F32),
                   jax.ShapeDtypeStruct((n, LANES), F32)),
        compiler_params=_cparams("parallel"),
    )(x, g, w_in, avg, qg, kg)


def _inproj_bwd(x, g, w_in, avg, qg, kg, raw, du, dqn, dkn, dv, dfl, dres, tm=512):
    n = x.shape[0]

    def body(x_ref, g_ref, w_ref, a_ref, qg_ref, kg_ref, raw_ref, du_ref, dqn_ref, dkn_ref, dv_ref, dfl_ref, dres_ref,
             dx_ref, dproj_ref, dg_ref, dqg_ref, dkg_ref):
        @pl.when(pl.program_id(0) == 0)
        def _():
            dg_ref[...] = jnp.zeros_like(dg_ref)
            dqg_ref[...] = jnp.zeros_like(dqg_ref)
            dkg_ref[...] = jnp.zeros_like(dkg_ref)
        avg_m = a_ref[...]
        _, vjp_q = jax.vjp(lambda q, gg: _headnorm(q, avg_m, gg), raw_ref[:, 0:512], qg_ref[...])
        dq, dqg = vjp_q(dqn_ref[...])
        _, vjp_k = jax.vjp(lambda k, gg: _headnorm(k, avg_m, gg), raw_ref[:, 512:1024], kg_ref[...])
        dk, dkg = vjp_k(dkn_ref[...])
        dproj = jnp.concatenate([du_ref[...], dq, dk, dv_ref[...], dfl_ref[...]], axis=1).astype(BF16)
        dproj_ref[...] = dproj
        dhn = _dot_nt(dproj, w_ref[...])
        _, vjp_x = jax.vjp(_rms, x_ref[...], g_ref[...])
        dxn, dg = vjp_x(dhn)
        dx_ref[...] = dxn + dres_ref[...]
        dg_ref[...] += dg
        dqg_ref[...] += dqg
        dkg_ref[...] += dkg

    row = lambda w: pl.BlockSpec((tm, w), lambda i: (i, 0))
    full = lambda a: pl.BlockSpec(a.shape, lambda i: (0,) * a.ndim)
    vec = lambda w: pl.BlockSpec((1, w), lambda i: (0, 0))
    return pl.pallas_call(
        body, name="inproj_bwd", grid=(n // tm,),
        in_specs=[row(D_MODEL), full(g), full(w_in), full(avg), full(qg), full(kg), row(1024), row(512), row(512),
                  row(512), row(512), row(LANES), row(D_MODEL)],
        out_specs=(row(D_MODEL), row(D_IN_PAD), vec(D_MODEL), vec(512), vec(512)),
        out_shape=(jax.ShapeDtypeStruct((n, D_MODEL), F32), jax.ShapeDtypeStruct((n, D_IN_PAD), BF16),
                   jax.ShapeDtypeStruct((1, D_MODEL), F32), jax.ShapeDtypeStruct((1, 512), F32),
                   jax.ShapeDtypeStruct((1, 512), F32)),
        compiler_params=_cparams("arbitrary"),
    )(x, g, w_in, avg, qg, kg, raw, du, dqn, dkn, dv, dfl, dres)


def _glu_fwd(yc, wg, bg, tm=512):
    n = yc.shape[0]

    def body(yc_ref, w_ref, b_ref, ys_ref):
        gl = jax.nn.gelu(yc_ref[...])
        z = _dot(gl.astype(BF16), w_ref[...]) + b_ref[...]
        ys_ref[...] = gl * jax.nn.sigmoid(z)

    row = pl.BlockSpec((tm, 512), lambda i: (i, 0))
    full = lambda a: pl.BlockSpec(a.shape, lambda i: (0,) * a.ndim)
    return pl.pallas_call(
        body, name="glu_fwd", grid=(n // tm,), in_specs=[row, full(wg), full(bg)], out_specs=row,
        out_shape=jax.ShapeDtypeStruct((n, 512), F32), compiler_params=_cparams("parallel"),
    )(yc, wg, bg)


def _glu_bwd(yc, dys, wg, bg, tm=512):
    n = yc.shape[0]

    def body(yc_ref, dys_ref, w_ref, b_ref, dyc_ref, gl_ref, dz_ref, db_ref):
        @pl.when(pl.program_id(0) == 0)
        def _():
            db_ref[...] = jnp.zeros_like(db_ref)
        gl, vjp_gelu = jax.vjp(jax.nn.gelu, yc_ref[...])
        glb = gl.astype(BF16)
        z = _dot(glb, w_ref[...]) + b_ref[...]
        s = jax.nn.sigmoid(z)
        dys = dys_ref[...]
        dz = dys * gl * s * (1.0 - s)
        dzb = dz.astype(BF16)
        dgl = dys * s + _dot_nt(dzb, w_ref[...])
        dyc_ref[...] = vjp_gelu(dgl)[0]
        gl_ref[...] = glb
        dz_ref[...] = dzb
        db_ref[...] += jnp.sum(dz, axis=0, keepdims=True)

    row = pl.BlockSpec((tm, 512), lambda i: (i, 0))
    full = lambda a: pl.BlockSpec(a.shape, lambda i: (0,) * a.ndim)
    return pl.pallas_call(
        body, name="glu_bwd", grid=(n // tm,), in_specs=[row, row, full(wg), full(bg)],
        out_specs=(row, row, row, pl.BlockSpec((1, 512), lambda i: (0, 0))),
        out_shape=(jax.ShapeDtypeStruct((n, 512), F32), jax.ShapeDtypeStruct((n, 512), BF16),
                   jax.ShapeDtypeStruct((n, 512), BF16), jax.ShapeDtypeStruct((1, 512), F32)),
        compiler_params=_cparams("arbitrary"),
    )(yc, dys, wg, bg)


def _mix_fwd(x, ys, ya, gs, ga, wout, gf, tm=512):
    n = x.shape[0]

    def body(x_ref, ys_ref, ya_ref, gs_ref, ga_ref, w_ref, gf_ref, h1_ref, hn2_ref, mixed_ref):
        mixed = jnp.concatenate([_rms(ys_ref[...], gs_ref[...]), _rms(ya_ref[...], ga_ref[...])], axis=1).astype(BF16)
        mixed_ref[...] = mixed
        h1 = x_ref[...] + _dot(mixed, w_ref[...])
        h1_ref[...] = h1
        hn2_ref[...] = _rms(h1, gf_ref[...]).astype(BF16)

    row = lambda w: pl.BlockSpec((tm, w), lambda i: (i, 0))
    full = lambda a: pl.BlockSpec(a.shape, lambda i: (0,) * a.ndim)
    return pl.pallas_call(
        body, name="mix_fwd", grid=(n // tm,),
        in_specs=[row(D_MODEL), row(512), row(512), full(gs), full(ga), full(wout), full(gf)],
        out_specs=(row(D_MODEL), row(D_MODEL), row(D_MODEL)),
        out_shape=(jax.ShapeDtypeStruct((n, D_MODEL), F32), jax.ShapeDtypeStruct((n, D_MODEL), BF16),
                   jax.ShapeDtypeStruct((n, D_MODEL), BF16)),
        compiler_params=_cparams("parallel"),
    )(x, ys, ya, gs, ga, wout, gf)


def _mix_bwd(dy, dhn2_parts, h1, ys, ya, gs, ga, wout, gf, tm=512):
    n = dy.shape[0]
    n_parts = dhn2_parts.shape[0]

    def body(dy_ref, dp_ref, h1_ref, ys_ref, ya_ref, gs_ref, ga_ref, w_ref, gf_ref,
             dh1_ref, dys_ref, dya_ref, dgs_ref, dga_ref, dgf_ref):
        @pl.when(pl.program_id(0) == 0)
        def _():
            dgs_ref[...] = jnp.zeros_like(dgs_ref)
            dga_ref[...] = jnp.zeros_like(dga_ref)
            dgf_ref[...] = jnp.zeros_like(dgf_ref)
        dhn2 = dp_ref[0]
        for p in range(1, n_parts):
            dhn2 = dhn2 + dp_ref[p]
        _, vjp_f = jax.vjp(_rms, h1_ref[...], gf_ref[...])
        dh1n, dgf = vjp_f(dhn2)
        dh1 = dy_ref[...] + dh1n
        dh1_ref[...] = dh1
        dmixed = _dot_nt(dh1.astype(BF16), w_ref[...])
        _, vjp_s = jax.vjp(_rms, ys_ref[...], gs_ref[...])
        dys, dgs = vjp_s(dmixed[:, 0:512])
        _, vjp_a = jax.vjp(_rms, ya_ref[...], ga_ref[...])
        dya, dga = vjp_a(dmixed[:, 512:1024])
        dys_ref[...] = dys
        dya_ref[...] = dya
        dgs_ref[...] += dgs
        dga_ref[...] += dga
        dgf_ref[...] += dgf

    row = lambda w: pl.BlockSpec((tm, w), lambda i: (i, 0))
    full = lambda a: pl.BlockSpec(a.shape, lambda i: (0,) * a.ndim)
    vec = lambda w: pl.BlockSpec((1, w), lambda i: (0, 0))
    return pl.pallas_call(
        body, name="mix_bwd", grid=(n // tm,),
        in_specs=[row(D_MODEL), pl.BlockSpec((n_parts, tm, D_MODEL), lambda i: (0, i, 0)), row(D_MODEL), row(512),
                  row(512), full(gs), full(ga), full(wout), full(gf)],
        out_specs=(row(D_MODEL), row(512), row(512), vec(512), vec(512), vec(D_MODEL)),
        out_shape=(jax.ShapeDtypeStruct((n, D_MODEL), F32), jax.ShapeDtypeStruct((n, 512), F32),
                   jax.ShapeDtypeStruct((n, 512), F32), jax.ShapeDtypeStruct((1, 512), F32),
                   jax.ShapeDtypeStruct((1, 512), F32), jax.ShapeDtypeStruct((1, D_MODEL), F32)),
        compiler_params=_cparams("arbitrary"),
    )(dy, dhn2_parts, h1, ys, ya, gs, ga, wout, gf)


HALO = 16


def _conv3(ue, cw):
    return cw[2:3] * ue + cw[1:2] * pltpu.roll(ue, 1, 0) + cw[0:1] * pltpu.roll(ue, 2, 0) + cw[3:4]


def _ffn_fwd(hn2, h1, target, wup_g, wup_v, cw_g, cw_v, wdown, seq_len, tm=512, fb=256):
    n = hn2.shape[0]
    nj = D_FF_PAD // fb
    hb = tm // HALO

    def body(hn_ref, halo_ref, h1_ref, tgt_ref, wg_ref, wv_ref, cg_ref, cv_ref, wd_ref,
             ug_ref, uv_ref, dy_ref, loss_ref, acc):
        i, j = pl.program_id(0), pl.program_id(1)
        seq_start = (i * tm) % seq_len == 0
        halo = halo_ref[...]
        halo = jnp.where(seq_start, jnp.zeros_like(halo), halo)
        he = jnp.concatenate([halo, hn_ref[...]], axis=0)
        ueg = _dot(he, wg_ref[...])
        uev = _dot(he, wv_ref[...])
        ug_ref[...] = ueg[HALO:]
        uv_ref[...] = uev[HALO:]
        cg = _conv3(ueg, cg_ref[...])[HALO:]
        cv = _conv3(uev, cv_ref[...])[HALO:]
        act = (jax.nn.silu(cg) * cv).astype(BF16)
        part = _dot(act, wd_ref[...])

        @pl.when(j == 0)
        def _():
            acc[...] = part

        @pl.when(j > 0)
        def _():
            acc[...] += part

        @pl.when(j == nj - 1)
        def _():
            err = h1_ref[...] + acc[...] - tgt_ref[...]
            dy_ref[...] = err * (1.0 / D_MODEL)
            loss_ref[0] = jnp.sum(err * err, axis=0, keepdims=True)

    row = pl.BlockSpec((tm, D_MODEL), lambda i, j: (i, 0))
    return pl.pallas_call(
        body, name="ffn_fwd", grid=(n // tm, nj),
        in_specs=[row, pl.BlockSpec((HALO, D_MODEL), lambda i, j: (jnp.maximum(i * hb - 1, 0), 0)), row, row,
                  pl.BlockSpec((D_MODEL, fb), lambda i, j: (0, j)), pl.BlockSpec((D_MODEL, fb), lambda i, j: (0, j)),
                  pl.BlockSpec((8, fb), lambda i, j: (0, j)), pl.BlockSpec((8, fb), lambda i, j: (0, j)),
                  pl.BlockSpec((fb, D_MODEL), lambda i, j: (j, 0))],
        out_specs=(pl.BlockSpec((tm, fb), lambda i, j: (i, j)), pl.BlockSpec((tm, fb), lambda i, j: (i, j)), row,
                   pl.BlockSpec((1, 1, D_MODEL), lambda i, j: (i, 0, 0))),
        out_shape=(jax.ShapeDtypeStruct((n, D_FF_PAD), F32), jax.ShapeDtypeStruct((n, D_FF_PAD), F32),
                   jax.ShapeDtypeStruct((n, D_MODEL), F32), jax.ShapeDtypeStruct((n // tm, 1, D_MODEL), F32)),
        scratch_shapes=[pltpu.VMEM((tm, D_MODEL), F32)],
        compiler_params=_cparams("parallel", "arbitrary"),
    )(hn2, hn2, h1, target, wup_g, wup_v, cw_g, cw_v, wdown)


def _ffn_bwd(dy, ug, uv, wup_g, wup_v, cw_g, cw_v, wdown, seq_len, tm=512, fb=256):
    n = dy.shape[0]
    nj = D_FF_PAD // fb
    hb = tm // HALO
    last_hb = n // HALO - 1
    rows = tm + HALO

    def body(dy_ref, dyn_ref, ugp_ref, ugm_ref, ugn_ref, uvp_ref, uvm_ref, uvn_ref, wg_ref, wv_ref, cg_ref, cv_ref,
             wd_ref, dug_ref, duv_ref, act_ref, dhn_ref, dcg_ref, dcv_ref, acc):
        i, j = pl.program_id(0), pl.program_id(1)
        seq_start = (i * tm) % seq_len == 0
        seq_end = ((i + 1) * tm) % seq_len == 0
        dyn = dyn_ref[...]
        dyn = jnp.where(seq_end, jnp.zeros_like(dyn), dyn)
        d_out = jnp.concatenate([dy_ref[...], dyn], axis=0).astype(BF16)
        d_act = _dot_nt(d_out, wd_ref[...])

        def pre_act(up_ref, um_ref, un_ref, cw):
            up = up_ref[...]
            up = jnp.where(seq_start, jnp.zeros_like(up), up)
            ue = jnp.concatenate([up, um_ref[...], un_ref[...]], axis=0)
            return ue, _conv3(ue, cw)[HALO:]

        cwg, cwv = cg_ref[...], cv_ref[...]
        ueg, cge = pre_act(ugp_ref, ugm_ref, ugn_ref, cwg)
        uev, cve = pre_act(uvp_ref, uvm_ref, uvn_ref, cwv)
        act, vjp_act = jax.vjp(lambda g, v: jax.nn.silu(g) * v, cge, cve)
        dcge, dcve = vjp_act(d_act)
        act_ref[...] = act[:tm].astype(BF16)

        def conv_t(dc, cw):
            return (cw[2:3] * dc + cw[1:2] * pltpu.roll(dc, rows - 1, 0) + cw[0:1] * pltpu.roll(dc, rows - 2, 0))[:tm]

        dug = conv_t(dcge, cwg).astype(BF16)
        duv = conv_t(dcve, cwv).astype(BF16)
        dug_ref[...] = dug
        duv_ref[...] = duv
        part = _dot_nt(dug, wg_ref[...]) + _dot_nt(duv, wv_ref[...])

        @pl.when(j == 0)
        def _():
            acc[...] = part

        @pl.when(j > 0)
        def _():
            acc[...] += part

        @pl.when(j == nj - 1)
        def _():
            dhn_ref[...] = acc[...]

        def cw_grad(dc, ue):
            dcm = dc[:tm]
            taps = [jnp.sum(dcm * pltpu.roll(ue, 2 - k, 0)[HALO:HALO + tm], axis=0, keepdims=True) for k in (0, 1)]
            taps.append(jnp.sum(dcm * ue[HALO:HALO + tm], axis=0, keepdims=True))
            taps.append(jnp.sum(dcm, axis=0, keepdims=True))
            return jnp.concatenate(taps + [jnp.zeros((4, fb), F32)], axis=0)

        @pl.when(i == 0)
        def _():
            dcg_ref[j] = jnp.zeros((8, fb), F32)
            dcv_ref[j] = jnp.zeros((8, fb), F32)

        dcg_ref[j] += cw_grad(dcge, ueg)
        dcv_ref[j] += cw_grad(dcve, uev)

    row = pl.BlockSpec((tm, D_MODEL), lambda i, j: (i, 0))
    u_prev = pl.BlockSpec((HALO, fb), lambda i, j: (jnp.maximum(i * hb - 1, 0), j))
    u_main = pl.BlockSpec((tm, fb), lambda i, j: (i, j))
    u_next = pl.BlockSpec((HALO, fb), lambda i, j: (jnp.minimum((i + 1) * hb, last_hb), j))
    w_col = pl.BlockSpec((D_MODEL, fb), lambda i, j: (0, j))
    c_col = pl.BlockSpec((8, fb), lambda i, j: (0, j))
    dc_spec = pl.BlockSpec((nj, 8, fb), lambda i, j: (0, 0, 0))
    return pl.pallas_call(
        body, name="ffn_bwd", grid=(n // tm, nj),
        in_specs=[row, pl.BlockSpec((HALO, D_MODEL), lambda i, j: (jnp.minimum((i + 1) * hb, last_hb), 0)),
                  u_prev, u_main, u_next, u_prev, u_main, u_next, w_col, w_col, c_col, c_col,
                  pl.BlockSpec((fb, D_MODEL), lambda i, j: (j, 0))],
        out_specs=(u_main, u_main, u_main, row, dc_spec, dc_spec),
        out_shape=(jax.ShapeDtypeStruct((n, D_FF_PAD), BF16), jax.ShapeDtypeStruct((n, D_FF_PAD), BF16),
                   jax.ShapeDtypeStruct((n, D_FF_PAD), BF16), jax.ShapeDtypeStruct((n, D_MODEL), F32),
                   jax.ShapeDtypeStruct((nj, 8, fb), F32), jax.ShapeDtypeStruct((nj, 8, fb), F32)),
        scratch_shapes=[pltpu.VMEM((tm, D_MODEL), F32)],
        compiler_params=_cparams("arbitrary", "arbitrary"),
    )(dy, dy, ug, ug, ug, uv, uv, uv, wup_g, wup_v, cw_g, cw_v, wdown)


def _s5_param_fn(lr, li, ldt, br, bi):
    dt = jnp.exp(ldt)
    mag = jnp.exp(lr * dt)
    ab_re = mag * jnp.cos(li * dt)
    ab_im = mag * jnp.sin(li * dt)
    nr = ab_re - 1.0
    ni = ab_im
    den = lr * lr + li * li
    q_re = (nr * lr + ni * li) / den
    q_im = (ni * lr - nr * li) / den
    bb_re = q_re * br - q_im * bi
    bb_im = q_re * bi + q_im * br
    return ab_re, ab_im, bb_re, bb_im


def _s5_param_fwd(lr, li, ldt, br, bi):
    def body(lr_ref, li_ref, ldt_ref, br_ref, bi_ref, ar_ref, ai_ref, bbr_ref, bbi_ref):
        ar, ai, bbr, bbi = _s5_param_fn(lr_ref[...], li_ref[...], ldt_ref[...], br_ref[...], bi_ref[...])
        ar_ref[...] = ar
        ai_ref[...] = ai
        bbr_ref[...] = bbr
        bbi_ref[...] = bbi

    return pl.pallas_call(
        body, name="s5_param_fwd",
        out_shape=(jax.ShapeDtypeStruct(lr.shape, F32), jax.ShapeDtypeStruct(lr.shape, F32),
                   jax.ShapeDtypeStruct(br.shape, F32), jax.ShapeDtypeStruct(br.shape, F32)),
    )(lr, li, ldt, br, bi)


def _s5_param_bwd(lr, li, ldt, br, bi, dar, dai, dbbr, dbbi):
    def body(lr_ref, li_ref, ldt_ref, br_ref, bi_ref, dar_ref, dai_ref, dbbr_ref, dbbi_ref,
             dlr_ref, dli_ref, dldt_ref, dbr_ref, dbi_ref):
        _, vjp = jax.vjp(_s5_param_fn, lr_ref[...], li_ref[...], ldt_ref[...], br_ref[...], bi_ref[...])
        dlr, dli, dldt, dbr, dbi = vjp((dar_ref[...], dai_ref[...], dbbr_ref[...], dbbi_ref[...]))
        dlr_ref[...] = dlr
        dli_ref[...] = dli
        dldt_ref[...] = dldt
        dbr_ref[...] = dbr
        dbi_ref[...] = dbi

    return pl.pallas_call(
        body, name="s5_param_bwd",
        out_shape=(jax.ShapeDtypeStruct(lr.shape, F32), jax.ShapeDtypeStruct(lr.shape, F32),
                   jax.ShapeDtypeStruct(ldt.shape, F32), jax.ShapeDtypeStruct(br.shape, F32),
                   jax.ShapeDtypeStruct(br.shape, F32)),
    )(lr, li, ldt, br, bi, dar, dai, dbbr, dbbi)


S5_CHUNK = 256
S5_STATES = 512
S5_BLOCKS = 4


def _cpow_rows(ar, ai, count):
    rs, im = [ar], [ai]
    for _ in range(count - 1):
        pr, pi = rs[-1], im[-1]
        rs.append(pr * ar - pi * ai)
        im.append(pr * ai + pi * ar)
    return rs, im


def _scan_in_groups(vr, vi, pr, pi, rm, reverse):
    n = vr.shape[0]
    for k in (1, 2, 4):
        if reverse:
            sr, si, keep = pltpu.roll(vr, n - k, 0), pltpu.roll(vi, n - k, 0), rm < SUBLANES - k
        else:
            sr, si, keep = pltpu.roll(vr, k, 0), pltpu.roll(vi, k, 0), rm >= k
        sr = jnp.where(keep, sr, 0.0)
        si = jnp.where(keep, si, 0.0)
        kr, ki = pr[k - 1], pi[k - 1]
        vr, vi = vr + kr * sr - ki * si, vi + kr * si + ki * sr
    return vr, vi


def _carry_over_groups(xr_s, xi_s, wr, wi, c0r, c0i, reverse):
    groups = xr_s.shape[0] // SUBLANES
    pick = 0 if reverse else SUBLANES - 1

    def step(q, carry):
        cr, ci = carry
        r = groups - 1 - q if reverse else q
        o = pl.multiple_of(r * SUBLANES, SUBLANES)
        vr = xr_s[pl.ds(o, SUBLANES), :]
        vi = xi_s[pl.ds(o, SUBLANES), :]
        nr = vr + wr * cr - wi * ci
        ni = vi + wr * ci + wi * cr
        xr_s[pl.ds(o, SUBLANES), :] = nr
        xi_s[pl.ds(o, SUBLANES), :] = ni
        return (jnp.broadcast_to(nr[pick:pick + 1], nr.shape), jnp.broadcast_to(ni[pick:pick + 1], ni.shape))

    return lax.fori_loop(0, groups, step, (c0r, c0i))


def _s5_state_scan(u_b, bbr, bbi, pr, pi, rm, xr_s, xi_s, c0r, c0i):
    bur = _dot(u_b, bbr)
    bui = _dot(u_b, bbi)
    bur, bui = _scan_in_groups(bur, bui, pr, pi, rm, False)
    xr_s[...] = bur
    xi_s[...] = bui
    w8r = jnp.concatenate(pr, axis=0)
    w8i = jnp.concatenate(pi, axis=0)
    return _carry_over_groups(xr_s, xi_s, w8r, w8i, c0r, c0i, False)


def _s5_fwd(u, a_re, a_im, bbr, bbi, cr, ci, d_skip, n_seq):
    n = u.shape[0]
    seq_len = n // n_seq
    nt = seq_len // S5_CHUNK
    tc = S5_CHUNK

    def body(u_ref, ar_ref, ai_ref, bbr_ref, bbi_ref, cr_ref, ci_ref, d_ref, y_ref, str_ref, sti_ref,
             xr_s, xi_s, car_r, car_i):
        t = pl.program_id(2)

        @pl.when(t == 0)
        def _():
            car_r[...] = jnp.zeros_like(car_r)
            car_i[...] = jnp.zeros_like(car_i)
        pr, pi = _cpow_rows(ar_ref[0], ai_ref[0], SUBLANES)
        rm = lax.broadcasted_iota(jnp.int32, (tc, S5_STATES), 0) & (SUBLANES - 1)
        str_ref[0, 0] = car_r[...]
        sti_ref[0, 0] = car_i[...]
        u_t = u_ref[...]
        cfr, cfi = _s5_state_scan(u_t.astype(BF16), bbr_ref[0], bbi_ref[0], pr, pi, rm, xr_s, xi_s,
                                  car_r[...], car_i[...])
        car_r[...] = cfr
        car_i[...] = cfi
        y = _dot(xr_s[...].astype(BF16), cr_ref[0]) - _dot(xi_s[...].astype(BF16), ci_ref[0])
        y_ref[...] = y + d_ref[...] * u_t

    u_spec = pl.BlockSpec((tc, LANES), lambda cb, b, t: (b * nt + t, cb))
    a_spec = pl.BlockSpec((1, 1, S5_STATES), lambda cb, b, t: (cb, 0, 0))
    bb_spec = pl.BlockSpec((1, LANES, S5_STATES), lambda cb, b, t: (cb, 0, 0))
    c_spec = pl.BlockSpec((1, S5_STATES, LANES), lambda cb, b, t: (cb, 0, 0))
    st_spec = pl.BlockSpec((1, 1, SUBLANES, S5_STATES), lambda cb, b, t: (cb, b * nt + t, 0, 0))
    st_shape = jax.ShapeDtypeStruct((S5_BLOCKS, n_seq * nt, SUBLANES, S5_STATES), F32)
    return pl.pallas_call(
        body, name="s5_fwd", grid=(S5_BLOCKS, n_seq, nt),
        in_specs=[u_spec, a_spec, a_spec, bb_spec, bb_spec, c_spec, c_spec,
                  pl.BlockSpec((1, LANES), lambda cb, b, t: (0, cb))],
        out_specs=(u_spec, st_spec, st_spec),
        out_shape=(jax.ShapeDtypeStruct((n, D_SSM), F32), st_shape, st_shape),
        scratch_shapes=[pltpu.VMEM((tc, S5_STATES), F32), pltpu.VMEM((tc, S5_STATES), F32),
                        pltpu.VMEM((SUBLANES, S5_STATES), F32), pltpu.VMEM((SUBLANES, S5_STATES), F32)],
        compiler_params=_cparams("parallel", "arbitrary", "arbitrary"),
    )(u, a_re, a_im, bbr, bbi, cr, ci, d_skip)


def _s5_bwd(u, dy, st_r, st_i, a_re, a_im, bbr, bbi, cr, ci, d_skip, n_seq):
    n = u.shape[0]
    seq_len = n // n_seq
    nt = seq_len // S5_CHUNK
    tc = S5_CHUNK

    def body(u_ref, dy_ref, str_ref, sti_ref, ar_ref, ai_ref, bbr_ref, bbi_ref, cr_ref, ci_ref, d_ref,
             du_ref, dbbr_ref, dbbi_ref, dcr_ref, dci_ref, dar_ref, dai_ref, dd_ref,
             xr_s, xi_s, gr_s, gi_s, car_r, car_i):
        b, t = pl.program_id(1), pl.program_id(2)

        @pl.when((b == 0) & (t == 0))
        def _():
            for ref in (dbbr_ref, dbbi_ref, dcr_ref, dci_ref, dar_ref, dai_ref, dd_ref):
                ref[...] = jnp.zeros_like(ref)

        @pl.when(t == 0)
        def _():
            car_r[...] = jnp.zeros_like(car_r)
            car_i[...] = jnp.zeros_like(car_i)
        ar, ai = ar_ref[0], ai_ref[0]
        pr, pi = _cpow_rows(ar, ai, SUBLANES)
        row = lax.broadcasted_iota(jnp.int32, (tc, S5_STATES), 0)
        rm = row & (SUBLANES - 1)
        u_t = u_ref[...]
        u_b = u_t.astype(BF16)
        dy_t = dy_ref[...]
        dy_b = dy_t.astype(BF16)
        s0r, s0i = str_ref[0, 0], sti_ref[0, 0]
        _s5_state_scan(u_b, bbr_ref[0], bbi_ref[0], pr, pi, rm, xr_s, xi_s, s0r, s0i)
        xr, xi = xr_s[...], xi_s[...]
        gr = _dot_nt(dy_b, cr_ref[0])
        gi = -_dot_nt(dy_b, ci_ref[0])
        npi = [-v for v in pi]
        gr, gi = _scan_in_groups(gr, gi, pr, npi, rm, True)
        gr_s[...] = gr
        gi_s[...] = gi
        w8r = jnp.concatenate(pr[::-1], axis=0)
        w8i = jnp.concatenate(npi[::-1], axis=0)
        cfr, cfi = _carry_over_groups(gr_s, gi_s, w8r, w8i, car_r[...], car_i[...], True)
        car_r[...] = cfr
        car_i[...] = cfi
        gr, gi = gr_s[...], gi_s[...]
        gr_b, gi_b = gr.astype(BF16), gi.astype(BF16)
        du_ref[...] = _dot_nt(gr_b, bbr_ref[0]) + _dot_nt(gi_b, bbi_ref[0]) + d_ref[...] * dy_t
        dbbr_ref[0] += _dot_tn(u_b, gr_b)
        dbbi_ref[0] += _dot_tn(u_b, gi_b)
        dcr_ref[0] += _dot_tn(xr.astype(BF16), dy_b)
        dci_ref[0] -= _dot_tn(xi.astype(BF16), dy_b)
        dd_ref[0] += jnp.sum((dy_t * u_t).reshape(tc // SUBLANES, SUBLANES, LANES), axis=0)
        first = row == 0
        xpr = jnp.where(first, jnp.broadcast_to(s0r[0:1], xr.shape), pltpu.roll(xr, 1, 0))
        xpi = jnp.where(first, jnp.broadcast_to(s0i[0:1], xi.shape), pltpu.roll(xi, 1, 0))
        shp = (tc // SUBLANES, SUBLANES, S5_STATES)
        dar_ref[0] += jnp.sum((gr * xpr + gi * xpi).reshape(shp), axis=0)
        dai_ref[0] += jnp.sum((gi * xpr - gr * xpi).reshape(shp), axis=0)

    u_spec = pl.BlockSpec((tc, LANES), lambda cb, b, t: (b * nt + nt - 1 - t, cb))
    a_spec = pl.BlockSpec((1, 1, S5_STATES), lambda cb, b, t: (cb, 0, 0))
    bb_spec = pl.BlockSpec((1, LANES, S5_STATES), lambda cb, b, t: (cb, 0, 0))
    c_spec = pl.BlockSpec((1, S5_STATES, LANES), lambda cb, b, t: (cb, 0, 0))
    st_spec = pl.BlockSpec((1, 1, SUBLANES, S5_STATES), lambda cb, b, t: (cb, b * nt + nt - 1 - t, 0, 0))
    da_spec = pl.BlockSpec((1, SUBLANES, S5_STATES), lambda cb, b, t: (cb, 0, 0))
    dd_spec = pl.BlockSpec((1, SUBLANES, LANES), lambda cb, b, t: (cb, 0, 0))
    big = pltpu.VMEM((tc, S5_STATES), F32)
    small = pltpu.VMEM((SUBLANES, S5_STATES), F32)
    return pl.pallas_call(
        body, name="s5_bwd", grid=(S5_BLOCKS, n_seq, nt),
        in_specs=[u_spec, u_spec, st_spec, st_spec, a_spec, a_spec, bb_spec, bb_spec, c_spec, c_spec,
                  pl.BlockSpec((1, LANES), lambda cb, b, t: (0, cb))],
        out_specs=(u_spec, bb_spec, bb_spec, c_spec, c_spec, da_spec, da_spec, dd_spec),
        out_shape=(jax.ShapeDtypeStruct((n, D_SSM), F32),
                   jax.ShapeDtypeStruct((S5_BLOCKS, LANES, S5_STATES), F32),
                   jax.ShapeDtypeStruct((S5_BLOCKS, LANES, S5_STATES), F32),
                   jax.ShapeDtypeStruct((S5_BLOCKS, S5_STATES, LANES), F32),
                   jax.ShapeDtypeStruct((S5_BLOCKS, S5_STATES, LANES), F32),
                   jax.ShapeDtypeStruct((S5_BLOCKS, SUBLANES, S5_STATES), F32),
                   jax.ShapeDtypeStruct((S5_BLOCKS, SUBLANES, S5_STATES), F32),
                   jax.ShapeDtypeStruct((S5_BLOCKS, SUBLANES, LANES), F32)),
        scratch_shapes=[big, big, big, big, small, small],
        compiler_params=_cparams("parallel", "arbitrary", "arbitrary"),
    )(u, dy, st_r, st_i, a_re, a_im, bbr, bbi, cr, ci, d_skip)


CUM_BLOCK = 128


def _tri(lower):
    r = lax.broadcasted_iota(jnp.int32, (CUM_BLOCK, CUM_BLOCK), 0)
    c = lax.broadcasted_iota(jnp.int32, (CUM_BLOCK, CUM_BLOCK), 1)
    return jnp.where(r >= c if lower else r <= c, 1.0, 0.0).astype(F32)


def _fprep_fwd(fl, bf, n_seq):
    n = fl.shape[0]
    seq_len = n // n_seq
    nb = seq_len // CUM_BLOCK

    def body(fl_ref, bf_ref, cum_ref):
        tril = _tri(True)
        carry = jnp.zeros((1, LANES), F32)
        for blk in range(nb):
            rows = slice(blk * CUM_BLOCK, (blk + 1) * CUM_BLOCK)
            lf = jax.nn.log_sigmoid(fl_ref[rows, :] + bf_ref[...])
            cs = jnp.dot(tril, lf, preferred_element_type=F32, precision=HIGHEST) + carry
            cum_ref[rows, :] = cs
            carry = cs[CUM_BLOCK - 1:CUM_BLOCK, :]

    spec = pl.BlockSpec((seq_len, LANES), lambda b: (b, 0))
    return pl.pallas_call(
        body, name="fprep_fwd", grid=(n_seq,), in_specs=[spec, pl.BlockSpec((1, LANES), lambda b: (0, 0))],
        out_specs=spec, out_shape=jax.ShapeDtypeStruct((n, LANES), F32), compiler_params=_cparams("parallel"),
    )(fl, bf)


def _fprep_bwd(dcum, fl, bf, n_seq):
    n = fl.shape[0]
    seq_len = n // n_seq
    nb = seq_len // CUM_BLOCK

    def body(dcum_ref, fl_ref, bf_ref, dfl_ref, dbf_ref):
        triu = _tri(False)
        lane = lax.broadcasted_iota(jnp.int32, (CUM_BLOCK, LANES), 1)
        carry = jnp.zeros((1, LANES), F32)
        total = jnp.zeros((1, LANES), F32)
        for blk in reversed(range(nb)):
            rows = slice(blk * CUM_BLOCK, (blk + 1) * CUM_BLOCK)
            rs = jnp.dot(triu, dcum_ref[rows, :], preferred_element_type=F32, precision=HIGHEST) + carry
            carry = rs[0:1, :]
            _, vjp = jax.vjp(jax.nn.log_sigmoid, fl_ref[rows, :] + bf_ref[...])
            dz = jnp.where(lane < N_HEADS, vjp(rs)[0], 0.0)
            dfl_ref[rows, :] = dz
            total = total + jnp.sum(dz, axis=0, keepdims=True)
        dbf_ref[0] = total

    spec = pl.BlockSpec((seq_len, LANES), lambda b: (b, 0))
    return pl.pallas_call(
        body, name="fprep_bwd", grid=(n_seq,), in_specs=[spec, spec, pl.BlockSpec((1, LANES), lambda b: (0, 0))],
        out_specs=(spec, pl.BlockSpec((1, 1, LANES), lambda b: (b, 0, 0))),
        out_shape=(jax.ShapeDtypeStruct((n, LANES), F32), jax.ShapeDtypeStruct((n_seq, 1, LANES), F32)),
        compiler_params=_cparams("parallel"),
    )(dcum, fl, bf)


ATT_TQ = 256
ATT_KSTEP = 512
ATT_SCALE = HEAD_DIM ** -0.5
NEG_BIG = -1e30


def _attn_scores(qe, kb, cq, ck, causal):
    s = _dot_nt(qe, kb) * ATT_SCALE + cq - ck
    return jnp.where(causal, s, NEG_BIG)


def _causal_mask(qi, kend):
    r = lax.broadcasted_iota(jnp.int32, (ATT_TQ, kend), 0) + qi * ATT_TQ
    c = lax.broadcasted_iota(jnp.int32, (ATT_TQ, kend), 1)
    return r >= c


def _attn_specs(n_seq, seq_len):
    nq = seq_len // ATT_TQ
    q_spec = pl.BlockSpec((ATT_TQ, LANES), lambda b, h, q: (b * nq + q, h))
    k_spec = pl.BlockSpec((seq_len, LANES), lambda b, h, q: (b, N_HEADS // 2 + h))
    v_spec = pl.BlockSpec((seq_len, LANES), lambda b, h, q: (b, N_HEADS + h))
    cq_spec = pl.BlockSpec((1, 2, ATT_TQ, 1), lambda b, h, q: (b, h, q, 0))
    ck_spec = pl.BlockSpec((1, 2, 1, seq_len), lambda b, h, q: (b, h, 0, 0))
    return nq, q_spec, k_spec, v_spec, cq_spec, ck_spec


def _head_selectors():
    head0 = lax.broadcasted_iota(jnp.int32, (1, LANES), 1) < HEAD_DIM
    return head0, (head0, jnp.logical_not(head0))


def _for_key_range(qi, seq_len, run):
    per = ATT_KSTEP // ATT_TQ
    for g in range(seq_len // ATT_KSTEP):
        pl.when(qi // per == g)(functools.partial(run, (g + 1) * ATT_KSTEP))


def _attn_fwd(qkv, cq, ck, n_seq):
    n = qkv.shape[0]
    seq_len = n // n_seq
    nq, q_spec, k_spec, v_spec, cq_spec, ck_spec = _attn_specs(n_seq, seq_len)

    def body(q_ref, k_ref, v_ref, cq_ref, ck_ref, o_ref, lse_ref):
        qi = pl.program_id(2)
        q2 = q_ref[...]
        head0, sels = _head_selectors()
        qe = [jnp.where(sel, q2, 0.0).astype(BF16) for sel in sels]

        def run(kend):
            kb = k_ref[0:kend, :].astype(BF16)
            vb = v_ref[0:kend, :].astype(BF16)
            causal = _causal_mask(qi, kend)
            outs = []
            for e in range(2):
                s = _attn_scores(qe[e], kb, cq_ref[0, e], ck_ref[0, e, :, 0:kend], causal)
                mx = jnp.max(s, axis=1, keepdims=True)
                p = jnp.exp(s - mx)
                den = jnp.sum(p, axis=1, keepdims=True)
                outs.append(_dot(p.astype(BF16), vb) / den)
                lse_ref[0, e] = mx + jnp.log(den)
            o_ref[...] = jnp.where(head0, outs[0], outs[1])

        _for_key_range(qi, seq_len, run)

    return pl.pallas_call(
        body, name="attn_fwd", grid=(n_seq, N_HEADS // 2, nq),
        in_specs=[q_spec, k_spec, v_spec, cq_spec, ck_spec],
        out_specs=(q_spec, cq_spec),
        out_shape=(jax.ShapeDtypeStruct((n, D_ATTN), F32), jax.ShapeDtypeStruct((n_seq, N_HEADS, seq_len, 1), F32)),
        compiler_params=_cparams("parallel", "parallel", "parallel"),
    )(qkv, qkv, qkv, cq, ck)


def _attn_bwd(qkv, cq, ck, o, do, lse, n_seq):
    n = qkv.shape[0]
    seq_len = n // n_seq
    nq, q_spec, k_spec, v_spec, cq_spec, ck_spec = _attn_specs(n_seq, seq_len)
    kv_out = pl.BlockSpec((seq_len, LANES), lambda b, h, q: (b, h))

    def body(q_ref, k_ref, v_ref, cq_ref, ck_ref, o_ref, do_ref, lse_ref, dq_ref, dk_ref, dv_ref, dcq_ref, dck_ref):
        qi = pl.program_id(2)

        @pl.when(qi == 0)
        def _():
            dk_ref[...] = jnp.zeros_like(dk_ref)
            dv_ref[...] = jnp.zeros_like(dv_ref)
            dck_ref[...] = jnp.zeros_like(dck_ref)
        q2 = q_ref[...]
        do2 = do_ref[...]
        o2 = o_ref[...]
        head0, sels = _head_selectors()
        qe = [jnp.where(sel, q2, 0.0).astype(BF16) for sel in sels]
        doe = [jnp.where(sel, do2, 0.0) for sel in sels]
        doe_b = [d.astype(BF16) for d in doe]
        delta = [jnp.sum(d * o2, axis=1, keepdims=True) for d in doe]

        def run(kend):
            kb = k_ref[0:kend, :].astype(BF16)
            vb = v_ref[0:kend, :].astype(BF16)
            causal = _causal_mask(qi, kend)
            dqs = []
            dk = jnp.zeros((kend, LANES), F32)
            dv = jnp.zeros((kend, LANES), F32)
            for e in range(2):
                s = _attn_scores(qe[e], kb, cq_ref[0, e], ck_ref[0, e, :, 0:kend], causal)
                p = jnp.exp(s - lse_ref[0, e])
                ds = p * (_dot_nt(doe_b[e], vb) - delta[e])
                ds_b = ds.astype(BF16)
                dqs.append(_dot(ds_b, kb))
                dk = dk + _dot_tn(ds_b, qe[e])
                dv = dv + _dot_tn(p.astype(BF16), doe_b[e])
                dcq_ref[0, e] = jnp.sum(ds, axis=1, keepdims=True)
                dck_ref[0, e, :, 0:kend] -= jnp.sum(ds, axis=0, keepdims=True)
            dk_ref[0:kend, :] += dk * ATT_SCALE
            dv_ref[0:kend, :] += dv
            dq_ref[...] = jnp.where(head0, dqs[0], dqs[1]) * ATT_SCALE

        _for_key_range(qi, seq_len, run)

    return pl.pallas_call(
        body, name="attn_bwd", grid=(n_seq, N_HEADS // 2, nq),
        in_specs=[q_spec, k_spec, v_spec, cq_spec, ck_spec, q_spec, q_spec, cq_spec],
        out_specs=(q_spec, kv_out, kv_out, cq_spec, ck_spec),
        out_shape=(jax.ShapeDtypeStruct((n, D_ATTN), F32), jax.ShapeDtypeStruct((n, D_ATTN), F32),
                   jax.ShapeDtypeStruct((n, D_ATTN), F32),
                   jax.ShapeDtypeStruct((n_seq, N_HEADS, seq_len, 1), F32),
                   jax.ShapeDtypeStruct((n_seq, N_HEADS, 1, seq_len), F32)),
        compiler_params=_cparams("parallel", "parallel", "arbitrary"),
    )(qkv, qkv, qkv, cq, ck, o, do, lse)


WEIGHT_NAMES = ("norm_mix", "w_in", "b_forget", "lam_re", "lam_im", "b_re", "b_im", "c_re", "c_im", "d_skip", "log_dt",
                "w_glu", "b_glu", "q_norm", "k_norm", "norm_out_ssm", "norm_out_attn", "w_out", "norm_ffn", "w_up",
                "conv_w", "conv_b", "w_down")
SHARDED = ("w_in", "w_glu", "w_out", "w_up", "conv_w", "w_down")
ADAM_ROWS = {"w_in": 256, "w_glu": 64, "w_out": 128, "w_up": 128, "conv_w": 3, "w_down": 344}
PACK_ROWS = SUBLANES * LANES
ROWS_DOWN = D_FF // N_DEV
ROWS_OUT = D_MODEL // N_DEV
ROWS_GLU = D_SSM * D_SSM // N_DEV // D_MODEL


def _pad_to(a, axis, size):
    pad = [(0, 0)] * a.ndim
    pad[axis] = (0, size - a.shape[axis])
    return jnp.pad(a, pad)


def _block_diag(t, transpose):
    t4 = t.reshape(S5_BLOCKS, 8, SSM_GROUP, SSM_STATE)
    eye = jnp.eye(8, dtype=t.dtype)
    if transpose:
        e = jnp.swapaxes(t4, 2, 3)[:, :, :, None, :] * eye[None, :, None, :, None]
        return e.reshape(S5_BLOCKS, S5_STATES, LANES)
    e = t4[:, :, :, None, :] * eye[None, :, None, :, None]
    return e.reshape(S5_BLOCKS, LANES, S5_STATES)


def _block_diag_extract(m, transpose):
    if transpose:
        m5 = m.reshape(S5_BLOCKS, 8, SSM_STATE, 8, SSM_GROUP)
        d = jnp.stack([m5[:, i, :, i, :] for i in range(8)], axis=1)
        return jnp.swapaxes(d, 2, 3).reshape(N_GROUPS, SSM_GROUP, SSM_STATE)
    m5 = m.reshape(S5_BLOCKS, 8, SSM_GROUP, 8, SSM_STATE)
    d = jnp.stack([m5[:, i, :, i, :] for i in range(8)], axis=1)
    return d.reshape(N_GROUPS, SSM_GROUP, SSM_STATE)


def _pack(pieces):
    flat = jnp.concatenate([p.reshape(-1).astype(F32) for p in pieces])
    size = -(-flat.shape[0] // PACK_ROWS) * PACK_ROWS
    return _pad_to(flat, 0, size).reshape(-1, LANES)


def _unpack(packed, shapes):
    flat = packed.reshape(-1)
    out, off = [], 0
    for shp in shapes:
        size = math.prod(shp)
        out.append(flat[off:off + size].reshape(shp))
        off += size
    return out


def kernel(x, norm_mix, w_in, b_forget, lam_re, lam_im, b_re, b_im, c_re, c_im, d_skip, log_dt, w_glu, b_glu, q_norm, k_norm, norm_out_ssm, norm_out_attn, w_out, norm_ffn, w_up, conv_w, conv_b, w_down, loss_target, m_norm_mix, m_w_in, m_b_forget, m_lam_re, m_lam_im, m_b_re, m_b_im, m_c_re, m_c_im, m_d_skip, m_log_dt, m_w_glu, m_b_glu, m_q_norm, m_k_norm, m_norm_out_ssm, m_norm_out_attn, m_w_out, m_norm_ffn, m_w_up, m_conv_w, m_conv_b, m_w_down, v_norm_mix, v_w_in, v_b_forget, v_lam_re, v_lam_im, v_b_re, v_b_im, v_c_re, v_c_im, v_d_skip, v_log_dt, v_w_glu, v_b_glu, v_q_norm, v_k_norm, v_norm_out_ssm, v_norm_out_attn, v_w_out, v_norm_ffn, v_w_up, v_conv_w, v_conv_b, v_w_down):
    given = dict(locals())
    weights = {k: given[k] for k in WEIGHT_NAMES}
    mom1 = {k: given["m_" + k] for k in WEIGHT_NAMES}
    mom2 = {k: given["v_" + k] for k in WEIGHT_NAMES}
    n_seq, seq_len, _ = x.shape
    n = n_seq * seq_len
    xf = x.reshape(n, D_MODEL)
    target = loss_target.reshape(n, D_MODEL)
    me_idx = 4 * lax.axis_index("x") + 2 * lax.axis_index("y") + lax.axis_index("c")

    g_in, g_cw = _exchange([w_in[0].astype(BF16), conv_w[0]], [False, False], "gather_w_in")
    rest_flags = [False] * 2
    rows_w = jnp.concatenate([w_down[0], w_out[0], w_glu[0].reshape(ROWS_GLU, D_MODEL)], axis=0).astype(BF16)
    w_sems = _exchange_start([rows_w, w_up[0].astype(BF16)], rest_flags, g_in, "gather_rest_start", 0)
    norm_mix = norm_mix + w_sems[4][0, 0]
    w_in_p = _pad_to(jnp.swapaxes(g_in, 0, 1).reshape(D_MODEL, D_IN), 1, D_IN_PAD)

    lr3 = lam_re[0].reshape(N_GROUPS, 1, SSM_STATE)
    li3 = lam_im[0].reshape(N_GROUPS, 1, SSM_STATE)
    ldt3 = log_dt[0].reshape(N_GROUPS, 1, 1)
    br_t = jnp.swapaxes(b_re[0], 1, 2)
    bi_t = jnp.swapaxes(b_im[0], 1, 2)
    ab_re, ab_im, bb_re, bb_im = _s5_param_fwd(lr3, li3, ldt3, br_t, bi_t)
    a_re = ab_re.reshape(S5_BLOCKS, 1, S5_STATES)
    a_im = ab_im.reshape(S5_BLOCKS, 1, S5_STATES)
    bbr = _block_diag(bb_re, False).astype(BF16)
    bbi = _block_diag(bb_im, False).astype(BF16)
    cr = _block_diag(c_re[0], True).astype(BF16)
    ci = _block_diag(c_im[0], True).astype(BF16)

    avg = jnp.kron(jnp.eye(N_HEADS, dtype=F32), jnp.full((HEAD_DIM, HEAD_DIM), 1.0 / HEAD_DIM, F32)).astype(BF16)
    qg = jnp.tile(q_norm, (1, N_HEADS))
    kg = jnp.tile(k_norm, (1, N_HEADS))
    hn, u, qkv, raw, fl = _inproj_fwd(xf, norm_mix, w_in_p, avg, qg, kg)
    yc, st_r, st_i = _s5_fwd(u, a_re, a_im, bbr, bbi, cr, ci, d_skip, n_seq)
    bf = _pad_to(b_forget, 1, LANES)
    cum = _fprep_fwd(fl, bf, n_seq)
    cum8 = jnp.swapaxes(cum[:, :N_HEADS].reshape(n_seq, seq_len, N_HEADS), 1, 2)
    cq = cum8[:, :, :, None]
    ck = cum8[:, :, None, :]
    ya, lse = _attn_fwd(qkv, cq, ck, n_seq)
    (own_rows, own_up), (g_rows, g_up) = _exchange_wait(w_sems[0], w_sems[1], w_sems[2], w_sems[3], rest_flags, ya,
                                                        "gather_rest_wait")
    my_slot = lax.broadcasted_iota(jnp.int32, (N_DEV, 1, 1), 0) == me_idx
    g_rows = jnp.where(my_slot, own_rows[None], g_rows)
    g_up = jnp.where(my_slot, own_up[None], g_up)
    g_down = g_rows[:, :ROWS_DOWN]
    g_out = g_rows[:, ROWS_DOWN:ROWS_DOWN + ROWS_OUT]
    g_glu = g_rows[:, ROWS_DOWN + ROWS_OUT:]
    w_glu_f = g_glu.reshape(D_SSM, D_SSM)
    w_out_f = g_out.reshape(D_MODEL, D_MODEL)
    w_up_f = jnp.swapaxes(g_up, 0, 1).reshape(D_MODEL, 2 * D_FF)
    wup_g = _pad_to(w_up_f[:, :D_FF], 1, D_FF_PAD)
    wup_v = _pad_to(w_up_f[:, D_FF:], 1, D_FF_PAD)
    cw_f = jnp.swapaxes(g_cw, 0, 1).reshape(3, 2 * D_FF)
    cw4 = jnp.concatenate([cw_f, conv_b], axis=0)
    cw_g = _pad_to(_pad_to(cw4[:, :D_FF], 1, D_FF_PAD), 0, SUBLANES)
    cw_v = _pad_to(_pad_to(cw4[:, D_FF:], 1, D_FF_PAD), 0, SUBLANES)
    w_down_p = _pad_to(g_down.reshape(D_FF, D_MODEL), 0, D_FF_PAD)
    ys = _glu_fwd(yc, w_glu_f, b_glu)
    h1, hn2, mixed = _mix_fwd(xf, ys, ya, norm_out_ssm, norm_out_attn, w_out_f, norm_ffn)
    ug, uv, dy, loss_part = _ffn_fwd(hn2, h1, target, wup_g, wup_v, cw_g, cw_v, w_down_p, seq_len)
    loss = lax.psum(0.5 * jnp.sum(loss_part) / D_MODEL, ("x", "y", "c"))

    dug, duv, act, dhn2, dcg, dcv = _ffn_bwd(dy, ug, uv, wup_g, wup_v, cw_g, cw_v, w_down_p, seq_len)
    dh1, dys, dya, d_gs, d_ga, d_gf = _mix_bwd(dy, dhn2[None], h1, ys, ya, norm_out_ssm, norm_out_attn, w_out_f, norm_ffn)
    dyc, gl_b, dz_b, d_bglu = _glu_bwd(yc, dys, w_glu_f, b_glu)

    gw_glu = _tn_matmul(gl_b, dz_b, "dw_glu", D_SSM, D_SSM)
    gw_out = _tn_matmul(mixed, dh1, "dw_out", D_MODEL, D_MODEL)
    gw_up = jnp.concatenate([_tn_matmul(hn2, dug, "dw_up_gate", D_MODEL, D_FF_PAD // 2, out_cols=D_FF),
                             _tn_matmul(hn2, duv, "dw_up_val", D_MODEL, D_FF_PAD // 2, out_cols=D_FF)], axis=1)
    gw_down = _tn_matmul(act, dy, "dw_down", D_FF_PAD // 2, D_MODEL, out_rows=D_FF)
    dcg2 = jnp.swapaxes(dcg, 0, 1).reshape(SUBLANES, D_FF_PAD)[:, :D_FF]
    dcv2 = jnp.swapaxes(dcv, 0, 1).reshape(SUBLANES, D_FF_PAD)[:, :D_FF]
    g_conv = jnp.concatenate([dcg2, dcv2], axis=1)
    by_cols = lambda g, c: jnp.swapaxes(g.reshape(g.shape[0], N_DEV, c), 0, 1)
    early_flags = [True] * 2
    rows_g = jnp.concatenate([gw_down.reshape(N_DEV, ROWS_DOWN, D_MODEL), gw_out.reshape(N_DEV, ROWS_OUT, D_MODEL),
                              gw_glu.reshape(N_DEV, ROWS_GLU, D_MODEL)], axis=1).astype(BF16)
    g_sems = _exchange_start([rows_g, by_cols(gw_up, 2 * D_FF // N_DEV).astype(BF16)], early_flags, dyc,
                             "grad_early_start", 1)
    started = g_sems[4][0, 0]

    du, dbbr, dbbi, dcr, dci, dar, dai, ddk = _s5_bwd(u, dyc, st_r, st_i, a_re, a_im, bbr, bbi, cr, ci,
                                                      d_skip + started, n_seq)
    dqn, dkn, dv, dcq, dck = _attn_bwd(qkv, cq, ck + started, ya, dya, lse, n_seq)
    dcum8 = dcq[:, :, :, 0] + dck.reshape(n_seq, N_HEADS, seq_len)
    dcum = _pad_to(jnp.swapaxes(dcum8, 1, 2).reshape(n, N_HEADS), 1, LANES)
    dfl, dbf = _fprep_bwd(dcum, fl, bf, n_seq)
    dx, dproj, d_gmix, d_qg, d_kg = _inproj_bwd(xf, norm_mix, w_in_p, avg, qg, kg, raw, du, dqn, dkn, dv, dfl, dh1)

    gw_in = _tn_matmul(hn, dproj, "dw_in", D_MODEL, D_IN_PAD, out_cols=D_IN)
    partial_small = {
        "norm_mix": d_gmix, "b_forget": jnp.sum(dbf, axis=(0, 1))[:N_HEADS],
        "ab_re": jnp.sum(dar, axis=1), "ab_im": jnp.sum(dai, axis=1),
        "bb_re": _block_diag_extract(dbbr, False), "bb_im": _block_diag_extract(dbbi, False),
        "c_re": _block_diag_extract(dcr, True), "c_im": _block_diag_extract(dci, True),
        "d_skip": jnp.sum(ddk, axis=1), "b_glu": d_bglu,
        "q_norm": jnp.sum(d_qg.reshape(N_HEADS, HEAD_DIM), axis=0),
        "k_norm": jnp.sum(d_kg.reshape(N_HEADS, HEAD_DIM), axis=0),
        "norm_out_ssm": d_gs, "norm_out_attn": d_ga, "norm_ffn": d_gf, "conv_b": g_conv[3],
    }
    small_keys = tuple(partial_small)
    small_shapes = [partial_small[k].shape for k in small_keys]

    land_in, land_cw, small_parts = _exchange(
        [by_cols(gw_in, D_IN // N_DEV).astype(BF16), by_cols(g_conv[:3], 2 * D_FF // N_DEV),
         _pack([partial_small[k] for k in small_keys])], [True, True, False], "grad_late_exchange")
    (src_rows, src_up), (land_rows, land_up) = _exchange_wait(g_sems[0], g_sems[1], g_sems[2], g_sems[3], early_flags,
                                                              land_in, "grad_early_wait")
    land_down = land_rows[:, :ROWS_DOWN]
    land_out = land_rows[:, ROWS_DOWN:ROWS_DOWN + ROWS_OUT]
    land_glu = land_rows[:, ROWS_DOWN + ROWS_OUT:].reshape(N_DEV, -1, D_SSM)
    own_rows = lax.dynamic_index_in_dim(src_rows, me_idx, 0, keepdims=False)
    own_up = lax.dynamic_index_in_dim(src_up, me_idx, 0, keepdims=False)
    own = {"w_in": None, "conv_w": None, "w_up": own_up, "w_down": own_rows[:ROWS_DOWN],
           "w_out": own_rows[ROWS_DOWN:ROWS_DOWN + ROWS_OUT], "w_glu": own_rows[ROWS_DOWN + ROWS_OUT:].reshape(-1, D_SSM)}
    grads, deltas, new_m, new_v = {}, {}, {}, {}
    for name, land in zip(SHARDED, (land_in, land_glu, land_out, land_up, land_cw, land_down)):
        grads[name], deltas[name], new_m[name], new_v[name] = _adam_sharded(
            land, own[name], weights[name], mom1[name], mom2[name], "adam_" + name, ADAM_ROWS[name])

    summed = dict(zip(small_keys, _unpack(_sum_partials(small_parts, "sum_small_grads"), small_shapes)))
    dlr, dli, dldt, dbr_t, dbi_t = _s5_param_bwd(
        lr3, li3, ldt3, br_t, bi_t, summed["ab_re"].reshape(lr3.shape), summed["ab_im"].reshape(lr3.shape),
        summed["bb_re"], summed["bb_im"])
    small_grads = {
        "norm_mix": summed["norm_mix"], "b_forget": summed["b_forget"], "lam_re": dlr, "lam_im": dli,
        "b_re": jnp.swapaxes(dbr_t, 1, 2), "b_im": jnp.swapaxes(dbi_t, 1, 2), "c_re": summed["c_re"], "c_im": summed["c_im"],
        "d_skip": summed["d_skip"], "log_dt": dldt, "b_glu": summed["b_glu"], "q_norm": summed["q_norm"],
        "k_norm": summed["k_norm"], "norm_out_ssm": summed["norm_out_ssm"], "norm_out_attn": summed["norm_out_attn"],
        "norm_ffn": summed["norm_ffn"], "conv_b": summed["conv_b"],
    }
    repl = tuple(k for k in WEIGHT_NAMES if k not in SHARDED)
    g_list = [small_grads[k].reshape(weights[k].shape) for k in repl]
    d_list, m_list, v_list = _adam_replicated(g_list, [weights[k] for k in repl], [mom1[k] for k in repl],
                                              [mom2[k] for k in repl], "adam_replicated")
    for k, g, d, nm, nv in zip(repl, g_list, d_list, m_list, v_list):
        grads[k], deltas[k], new_m[k], new_v[k] = g, d, nm, nv

    grad_x = dx.reshape(x.shape)
    return (loss, grad_x, *[grads[k] for k in WEIGHT_NAMES], *[deltas[k] for k in WEIGHT_NAMES],
            *[new_m[k] for k in WEIGHT_NAMES], *[new_v[k] for k in WEIGHT_NAMES])
```

```python
import functools
import math

import jax
import jax.numpy as jnp
from jax import lax
from jax.experimental import pallas as pl
from jax.experimental.pallas import tpu as pltpu

F32 = jnp.float32
BF16 = jnp.bfloat16
HIGHEST = lax.Precision.HIGHEST

N_DEV = 8
D_MODEL = 1024
D_SSM = 512
D_ATTN = 512
N_HEADS = 8
HEAD_DIM = 64
N_GROUPS = 32
SSM_GROUP = 16
SSM_STATE = 64
D_FF = 2752
D_FF_PAD = 2816
D_IN = 2056
D_IN_PAD = 2176
EPS = 1e-6
LANES = 128
SUBLANES = 8
VMEM_LIMIT = 56 * 1024 * 1024

ADAM_LR = 0.001
ADAM_B1 = 0.9
ADAM_B2 = 0.999
ADAM_EPS = 1e-08
ADAM_WD = 0.01
ADAM_STEP = 10


def _cparams(*sem):
    return pltpu.CompilerParams(dimension_semantics=sem, vmem_limit_bytes=VMEM_LIMIT)


def _dot(a, b, **kw):
    return jnp.dot(a, b, preferred_element_type=F32, **kw)


def _dot_nt(a, b):
    return lax.dot_general(a, b, (((1,), (1,)), ((), ())), preferred_element_type=F32)


def _dot_tn(a, b):
    return lax.dot_general(a, b, (((0,), (0,)), ((), ())), preferred_element_type=F32)


def _rms(x, g):
    return x * lax.rsqrt(jnp.mean(x * x, axis=-1, keepdims=True) + EPS) * g


def _split_dot(x, avg):
    hi = x.astype(BF16)
    lo = (x - hi.astype(F32)).astype(BF16)
    return _dot(hi, avg) + _dot(lo, avg)


@jax.custom_vjp
def _group_mean(x, avg):
    return _split_dot(x, avg)


def _group_mean_fwd(x, avg):
    return _split_dot(x, avg), avg


def _group_mean_bwd(avg, ct):
    return _split_dot(ct, avg), jnp.zeros_like(avg)


_group_mean.defvjp(_group_mean_fwd, _group_mean_bwd)


def _headnorm(q, avg, g):
    return q * lax.rsqrt(_group_mean(q * q, avg) + EPS) * g


def _exchange(srcs, scatter_flags, name):
    n = len(srcs)
    out_shape = []
    for s, sc in zip(srcs, scatter_flags):
        shp = s.shape if sc else (N_DEV,) + s.shape
        out_shape.append(jax.ShapeDtypeStruct(shp, s.dtype))

    def body(*refs):
        src = refs[:n]
        dst = refs[n:2 * n]
        send_sems, recv_sems, loc_sems = refs[2 * n:]
        x, y, c = lax.axis_index("x"), lax.axis_index("y"), lax.axis_index("c")
        me = 4 * x + 2 * y + c
        peers = []
        for j in range(1, N_DEV):
            px = 1 - x if (j >> 2) & 1 else x
            py = 1 - y if (j >> 1) & 1 else y
            pc = 1 - c if j & 1 else c
            peers.append(((px, py, pc), 4 * px + 2 * py + pc))
        local, sends = [], []
        for k in range(n):
            own = src[k].at[me] if scatter_flags[k] else src[k]
            lc = pltpu.make_async_copy(own, dst[k].at[me], loc_sems.at[k])
            lc.start()
            local.append(lc)
            for j, (pid, pidx) in enumerate(peers):
                s = src[k].at[pidx] if scatter_flags[k] else src[k]
                cp = pltpu.make_async_remote_copy(
                    src_ref=s, dst_ref=dst[k].at[me], send_sem=send_sems.at[k, j], recv_sem=recv_sems.at[k, j],
                    device_id=pid, device_id_type=pl.DeviceIdType.MESH)
                cp.start()
                sends.append(cp)
        for k in range(n):
            for j, (pid, pidx) in enumerate(peers):
                s = src[k].at[pidx] if scatter_flags[k] else src[k]
                pltpu.make_async_remote_copy(
                    src_ref=s, dst_ref=dst[k].at[pidx], send_sem=send_sems.at[k, j], recv_sem=recv_sems.at[k, j],
                    device_id=pid, device_id_type=pl.DeviceIdType.MESH).wait_recv()
        for cp in sends:
            cp.wait_send()
        for lc in local:
            lc.wait()

    any_spec = pl.BlockSpec(memory_space=pl.ANY)
    return pl.pallas_call(
        body, name=name, out_shape=tuple(out_shape),
        in_specs=[any_spec] * n, out_specs=tuple([any_spec] * n),
        scratch_shapes=[pltpu.SemaphoreType.DMA((n, N_DEV - 1)), pltpu.SemaphoreType.DMA((n, N_DEV - 1)),
                        pltpu.SemaphoreType.DMA((n,))],
        compiler_params=pltpu.CompilerParams(has_side_effects=True),
    )(*srcs)


def _peer_list():
    x, y, c = lax.axis_index("x"), lax.axis_index("y"), lax.axis_index("c")
    peers = []
    for j in range(1, N_DEV):
        px = 1 - x if (j >> 2) & 1 else x
        py = 1 - y if (j >> 1) & 1 else y
        pc = 1 - c if j & 1 else c
        peers.append(((px, py, pc), 4 * px + 2 * py + pc))
    return 4 * x + 2 * y + c, peers


def _split_copies(src, land, send_sems, recv_sems, scatter_flags, me, peers, incoming):
    copies = []
    for k in range(len(src)):
        for j, (pid, pidx) in enumerate(peers):
            s = src[k].at[pidx] if scatter_flags[k] else src[k]
            i = k * (N_DEV - 1) + j
            copies.append(pltpu.make_async_remote_copy(
                src_ref=s, dst_ref=land[k].at[pidx if incoming else me], send_sem=send_sems[i],
                recv_sem=recv_sems[i], device_id=pid, device_id_type=pl.DeviceIdType.MESH))
    return copies


def _exchange_start(srcs, scatter_flags, after, name, collective_id):
    n = len(srcs)
    ns = n * (N_DEV - 1)
    hbm = pl.BlockSpec(memory_space=pltpu.HBM)
    sem = pl.BlockSpec(memory_space=pltpu.SEMAPHORE)
    land_shapes = [s.shape if sc else (N_DEV,) + s.shape for s, sc in zip(srcs, scatter_flags)]

    def body(*refs):
        src, land = refs[:n], refs[n:2 * n]
        send_sems = refs[2 * n + 1:2 * n + 1 + ns]
        recv_sems = refs[2 * n + 1 + ns:2 * n + 1 + 2 * ns]
        token = refs[4 * n + 1 + 2 * ns]
        me, peers = _peer_list()
        barrier = pltpu.get_barrier_semaphore()
        for pid, _ in peers:
            pl.semaphore_signal(barrier, inc=1, device_id=pid, device_id_type=pl.DeviceIdType.MESH)
        pl.semaphore_wait(barrier, N_DEV - 1)
        for cp in _split_copies(src, land, send_sems, recv_sems, scatter_flags, me, peers, False):
            cp.start()
        token[...] = jnp.zeros_like(token)

    outs = pl.pallas_call(
        body, name=name,
        out_shape=(*[pltpu.SemaphoreType.DMA(())] * (2 * ns), *[pltpu.HBM(s.shape, s.dtype) for s in srcs],
                   *[pltpu.HBM(shp, s.dtype) for shp, s in zip(land_shapes, srcs)],
                   jax.ShapeDtypeStruct((SUBLANES, LANES), F32)),
        in_specs=[hbm] * (2 * n) + [pl.BlockSpec(memory_space=pl.ANY)],
        out_specs=(*[sem] * (2 * ns), *[hbm] * (2 * n), pl.BlockSpec(memory_space=pltpu.VMEM)),
        input_output_aliases={i: 2 * ns + i for i in range(2 * n)},
        compiler_params=pltpu.CompilerParams(has_side_effects=pltpu.SideEffectType.DATAFLOW_SIDE_EFFECTING,
                                             collective_id=collective_id),
    )(*[pltpu.with_memory_space_constraint(s, pltpu.HBM) for s in srcs],
      *[pltpu.with_memory_space_constraint(lax.empty(shp, s.dtype), pltpu.HBM) for shp, s in zip(land_shapes, srcs)],
      after)
    return (outs[:ns], outs[ns:2 * ns], outs[2 * ns:2 * ns + n], outs[2 * ns + n:2 * ns + 2 * n], outs[2 * ns + 2 * n])


def _exchange_wait(send_sems, recv_sems, srcs, lands, scatter_flags, after, name):
    n = len(srcs)
    ns = n * (N_DEV - 1)
    hbm = pl.BlockSpec(memory_space=pltpu.HBM)
    sem = pl.BlockSpec(memory_space=pltpu.SEMAPHORE)

    def body(*refs):
        src, land = refs[:n], refs[n:2 * n]
        s_sems = refs[2 * n:2 * n + ns]
        r_sems = refs[2 * n + ns:2 * n + 2 * ns]
        me, peers = _peer_list()
        for cp in _split_copies(src, land, s_sems, r_sems, scatter_flags, me, peers, True):
            cp.wait_send()
            cp.wait_recv()

    outs = pl.pallas_call(
        body, name=name,
        out_shape=tuple(pltpu.HBM(a.shape, a.dtype) for a in (*srcs, *lands)),
        in_specs=[hbm] * (2 * n) + [sem] * (2 * ns) + [pl.BlockSpec(memory_space=pl.ANY)],
        out_specs=tuple([hbm] * (2 * n)),
        input_output_aliases={i: i for i in range(2 * n)},
        compiler_params=pltpu.CompilerParams(has_side_effects=pltpu.SideEffectType.DATAFLOW_SIDE_EFFECTING),
    )(*srcs, *lands, *send_sems, *recv_sems, after)
    return outs[:n], outs[n:]


def _tn_matmul(a, b, name, tk, tm, out_rows=None, out_cols=None, tn=512):
    n_tok, k_dim = a.shape
    m_dim = b.shape[1]
    grid = (k_dim // tk, m_dim // tm, n_tok // tn)

    def body(a_ref, b_ref, o_ref):
        @pl.when(pl.program_id(2) == 0)
        def _():
            o_ref[...] = jnp.zeros_like(o_ref)
        o_ref[...] += _dot_tn(a_ref[...].astype(BF16), b_ref[...].astype(BF16))

    return pl.pallas_call(
        body, name=name, grid=grid,
        in_specs=[pl.BlockSpec((tn, tk), lambda i, j, k: (k, i)), pl.BlockSpec((tn, tm), lambda i, j, k: (k, j))],
        out_specs=pl.BlockSpec((tk, tm), lambda i, j, k: (i, j)),
        out_shape=jax.ShapeDtypeStruct((out_rows or k_dim, out_cols or m_dim), F32),
        compiler_params=_cparams("parallel", "parallel", "arbitrary"),
    )(a, b)


def _adam_math(g, w, m, v):
    m = ADAM_B1 * m + (1.0 - ADAM_B1) * g
    v = ADAM_B2 * v + (1.0 - ADAM_B2) * (g * g)
    m_hat = m / (1.0 - ADAM_B1 ** ADAM_STEP)
    v_hat = v / (1.0 - ADAM_B2 ** ADAM_STEP)
    delta = -ADAM_LR * (m_hat / (jnp.sqrt(v_hat) + ADAM_EPS) + ADAM_WD * w)
    return delta, m, v


def _adam_sharded(land, own, w, m, v, name, tr):
    _, r, c = w.shape

    def body(*refs):
        l_ref = refs[0]
        own_ref = refs[1] if own is not None else None
        w_ref, m_ref, v_ref, g_ref, d_ref, nm_ref, nv_ref = [ref.at[0] for ref in refs[-7:]]
        if own_ref is not None:
            x, y, z = lax.axis_index("x"), lax.axis_index("y"), lax.axis_index("c")
            me = 4 * x + 2 * y + z
            mine = own_ref[...].astype(F32)
        g = None
        for s in range(N_DEV):
            part = l_ref[s].astype(F32)
            if own_ref is not None:
                part = jnp.where(me == s, mine, part)
            g = part if g is None else g + part
        d, nm, nv = _adam_math(g, w_ref[...], m_ref[...], v_ref[...])
        g_ref[...] = g
        d_ref[...] = d
        nm_ref[...] = nm
        nv_ref[...] = nv

    spec = pl.BlockSpec((1, tr, c), lambda i: (0, i, 0))
    own_specs, own_args = ([pl.BlockSpec((tr, c), lambda i: (i, 0))], [own]) if own is not None else ([], [])
    return pl.pallas_call(
        body, name=name, grid=(r // tr,),
        in_specs=[pl.BlockSpec((N_DEV, tr, c), lambda i: (0, i, 0)), *own_specs, spec, spec, spec],
        out_specs=(spec, spec, spec, spec),
        out_shape=tuple(jax.ShapeDtypeStruct((1, r, c), F32) for _ in range(4)),
        compiler_params=_cparams("parallel"),
    )(land, *own_args, w, m, v)


def _sum_partials(parts, own, name):
    _, r, c = parts.shape

    def body(*refs):
        p_ref, o_ref = refs[0], refs[-1]
        if own is not None:
            x, y, z = lax.axis_index("x"), lax.axis_index("y"), lax.axis_index("c")
            me = 4 * x + 2 * y + z
            mine = refs[1][...]
        g = None
        for s in range(N_DEV):
            part = p_ref[s]
            if own is not None:
                part = jnp.where(me == s, mine, part)
            g = part if g is None else g + part
        o_ref[...] = g

    args = (parts,) if own is None else (parts, own)
    return pl.pallas_call(body, name=name, out_shape=jax.ShapeDtypeStruct((r, c), F32),
                          compiler_params=pltpu.CompilerParams(vmem_limit_bytes=VMEM_LIMIT))(*args)


def _adam_replicated(gs, ws, ms, vs, name):
    k = len(ws)

    def body(*refs):
        outs = refs[4 * k:]
        for i in range(k):
            d, nm, nv = _adam_math(refs[i][...], refs[k + i][...], refs[2 * k + i][...], refs[3 * k + i][...])
            outs[i][...] = d
            outs[k + i][...] = nm
            outs[2 * k + i][...] = nv

    outs = pl.pallas_call(body, name=name, out_shape=tuple(jax.ShapeDtypeStruct(w.shape, F32) for w in ws) * 3,
                          compiler_params=pltpu.CompilerParams(vmem_limit_bytes=VMEM_LIMIT))(*gs, *ws, *ms, *vs)
    return outs[:k], outs[k:2 * k], outs[2 * k:]


def _inproj_fwd(x, g, w_in, avg, qg, kg, tm=512):
    n = x.shape[0]

    def body(x_ref, g_ref, w_ref, a_ref, qg_ref, kg_ref, hn_ref, u_ref, qkv_ref, raw_ref, fl_ref):
        hn = _rms(x_ref[...], g_ref[...]).astype(BF16)
        hn_ref[...] = hn
        proj = _dot(hn, w_ref[...])
        u_ref[...] = proj[:, 0:512]
        q = proj[:, 512:1024]
        k = proj[:, 1024:1536]
        raw_ref[:, 0:512] = q
        raw_ref[:, 512:1024] = k
        qkv_ref[:, 0:512] = _headnorm(q, a_ref[...], qg_ref[...])
        qkv_ref[:, 512:1024] = _headnorm(k, a_ref[...], kg_ref[...])
        qkv_ref[:, 1024:1536] = proj[:, 1536:2048]
        fl_ref[...] = proj[:, 2048:D_IN_PAD]

    row = lambda w: pl.BlockSpec((tm, w), lambda i: (i, 0))
    full = lambda a: pl.BlockSpec(a.shape, lambda i: (0,) * a.ndim)
    return pl.pallas_call(
        body, name="inproj_fwd", grid=(n // tm,),
        in_specs=[row(D_MODEL), full(g), full(w_in), full(avg), full(qg), full(kg)],
        out_specs=(row(D_MODEL), row(512), row(1536), row(1024), row(LANES)),
        out_shape=(jax.ShapeDtypeStruct((n, D_MODEL), BF16), jax.ShapeDtypeStruct((n, 512), F32),
                   jax.ShapeDtypeStruct((n, 1536), F32), jax.ShapeDtypeStruct((n, 1024), F32),
                   jax.ShapeDtypeStruct((n, LANES), F32)),
        compiler_params=_cparams("parallel"),
    )(x, g, w_in, avg, qg, kg)


def _inproj_bwd(x, g, w_in, avg, qg, kg, raw, du, dqn, dkn, dv, dfl, dres, tm=512):
    n = x.shape[0]

    def body(x_ref, g_ref, w_ref, a_ref, qg_ref, kg_ref, raw_ref, du_ref, dqn_ref, dkn_ref, dv_ref, dfl_ref, dres_ref,
             dx_ref, dproj_ref, dg_ref, dqg_ref, dkg_ref):
        @pl.when(pl.program_id(0) == 0)
        def _():
            dg_ref[...] = jnp.zeros_like(dg_ref)
            dqg_ref[...] = jnp.zeros_like(dqg_ref)
            dkg_ref[...] = jnp.zeros_like(dkg_ref)
        avg_m = a_ref[...]
        _, vjp_q = jax.vjp(lambda q, gg: _headnorm(q, avg_m, gg), raw_ref[:, 0:512], qg_ref[...])
        dq, dqg = vjp_q(dqn_ref[...])
        _, vjp_k = jax.vjp(lambda k, gg: _headnorm(k, avg_m, gg), raw_ref[:, 512:1024], kg_ref[...])
        dk, dkg = vjp_k(dkn_ref[...])
        dproj = jnp.concatenate([du_ref[...], dq, dk, dv_ref[...], dfl_ref[...]], axis=1).astype(BF16)
        dproj_ref[...] = dproj
        dhn = _dot_nt(dproj, w_ref[...])
        _, vjp_x = jax.vjp(_rms, x_ref[...], g_ref[...])
        dxn, dg = vjp_x(dhn)
        dx_ref[...] = dxn + dres_ref[...]
        dg_ref[...] += dg
        dqg_ref[...] += dqg
        dkg_ref[...] += dkg

    row = lambda w: pl.BlockSpec((tm, w), lambda i: (i, 0))
    full = lambda a: pl.BlockSpec(a.shape, lambda i: (0,) * a.ndim)
    vec = lambda w: pl.BlockSpec((1, w), lambda i: (0, 0))
    return pl.pallas_call(
        body, name="inproj_bwd", grid=(n // tm,),
        in_specs=[row(D_MODEL), full(g), full(w_in), full(avg), full(qg), full(kg), row(1024), row(512), row(512),
                  row(512), row(512), row(LANES), row(D_MODEL)],
        out_specs=(row(D_MODEL), row(D_IN_PAD), vec(D_MODEL), vec(512), vec(512)),
        out_shape=(jax.ShapeDtypeStruct((n, D_MODEL), F32), jax.ShapeDtypeStruct((n, D_IN_PAD), BF16),
                   jax.ShapeDtypeStruct((1, D_MODEL), F32), jax.ShapeDtypeStruct((1, 512), F32),
                   jax.ShapeDtypeStruct((1, 512), F32)),
        compiler_params=_cparams("arbitrary"),
    )(x, g, w_in, avg, qg, kg, raw, du, dqn, dkn, dv, dfl, dres)


def _glu_fwd(yc, wg, bg, tm=512):
    n = yc.shape[0]

    def body(yc_ref, w_ref, b_ref, ys_ref):
        gl = jax.nn.gelu(yc_ref[...])
        z = _dot(gl.astype(BF16), w_ref[...]) + b_ref[...]
        ys_ref[...] = gl * jax.nn.sigmoid(z)

    row = pl.BlockSpec((tm, 512), lambda i: (i, 0))
    full = lambda a: pl.BlockSpec(a.shape, lambda i: (0,) * a.ndim)
    return pl.pallas_call(
        body, name="glu_fwd", grid=(n // tm,), in_specs=[row, full(wg), full(bg)], out_specs=row,
        out_shape=jax.ShapeDtypeStruct((n, 512), F32), compiler_params=_cparams("parallel"),
    )(yc, wg, bg)


def _glu_bwd(yc, dys, wg, bg, tm=512):
    n = yc.shape[0]

    def body(yc_ref, dys_ref, w_ref, b_ref, dyc_ref, gl_ref, dz_ref, db_ref):
        @pl.when(pl.program_id(0) == 0)
        def _():
            db_ref[...] = jnp.zeros_like(db_ref)
        gl, vjp_gelu = jax.vjp(jax.nn.gelu, yc_ref[...])
        glb = gl.astype(BF16)
        z = _dot(glb, w_ref[...]) + b_ref[...]
        s = jax.nn.sigmoid(z)
        dys = dys_ref[...]
        dz = dys * gl * s * (1.0 - s)
        dzb = dz.astype(BF16)
        dgl = dys * s + _dot_nt(dzb, w_ref[...])
        dyc_ref[...] = vjp_gelu(dgl)[0]
        gl_ref[...] = glb
        dz_ref[...] = dzb
        db_ref[...] += jnp.sum(dz, axis=0, keepdims=True)

    row = pl.BlockSpec((tm, 512), lambda i: (i, 0))
    full = lambda a: pl.BlockSpec(a.shape, lambda i: (0,) * a.ndim)
    return pl.pallas_call(
        body, name="glu_bwd", grid=(n // tm,), in_specs=[row, row, full(wg), full(bg)],
        out_specs=(row, row, row, pl.BlockSpec((1, 512), lambda i: (0, 0))),
        out_shape=(jax.ShapeDtypeStruct((n, 512), F32), jax.ShapeDtypeStruct((n, 512), BF16),
                   jax.ShapeDtypeStruct((n, 512), BF16), jax.ShapeDtypeStruct((1, 512), F32)),
        compiler_params=_cparams("arbitrary"),
    )(yc, dys, wg, bg)


def _mix_fwd(x, ys, ya, gs, ga, wout, gf, tm=512):
    n = x.shape[0]

    def body(x_ref, ys_ref, ya_ref, gs_ref, ga_ref, w_ref, gf_ref, h1_ref, hn2_ref, mixed_ref):
        mixed = jnp.concatenate([_rms(ys_ref[...], gs_ref[...]), _rms(ya_ref[...], ga_ref[...])], axis=1).astype(BF16)
        mixed_ref[...] = mixed
        h1 = x_ref[...] + _dot(mixed, w_ref[...])
        h1_ref[...] = h1
        hn2_ref[...] = _rms(h1, gf_ref[...]).astype(BF16)

    row = lambda w: pl.BlockSpec((tm, w), lambda i: (i, 0))
    full = lambda a: pl.BlockSpec(a.shape, lambda i: (0,) * a.ndim)
    return pl.pallas_call(
        body, name="mix_fwd", grid=(n // tm,),
        in_specs=[row(D_MODEL), row(512), row(512), full(gs), full(ga), full(wout), full(gf)],
        out_specs=(row(D_MODEL), row(D_MODEL), row(D_MODEL)),
        out_shape=(jax.ShapeDtypeStruct((n, D_MODEL), F32), jax.ShapeDtypeStruct((n, D_MODEL), BF16),
                   jax.ShapeDtypeStruct((n, D_MODEL), BF16)),
        compiler_params=_cparams("parallel"),
    )(x, ys, ya, gs, ga, wout, gf)


def _mix_bwd(dy, dhn2_parts, h1, ys, ya, gs, ga, wout, gf, tm=512):
    n = dy.shape[0]
    n_parts = dhn2_parts.shape[0]

    def body(dy_ref, dp_ref, h1_ref, ys_ref, ya_ref, gs_ref, ga_ref, w_ref, gf_ref,
             dh1_ref, dys_ref, dya_ref, dgs_ref, dga_ref, dgf_ref):
        @pl.when(pl.program_id(0) == 0)
        def _():
            dgs_ref[...] = jnp.zeros_like(dgs_ref)
            dga_ref[...] = jnp.zeros_like(dga_ref)
            dgf_ref[...] = jnp.zeros_like(dgf_ref)
        dhn2 = dp_ref[0]
        for p in range(1, n_parts):
            dhn2 = dhn2 + dp_ref[p]
        _, vjp_f = jax.vjp(_rms, h1_ref[...], gf_ref[...])
        dh1n, dgf = vjp_f(dhn2)
        dh1 = dy_ref[...] + dh1n
        dh1_ref[...] = dh1
        dmixed = _dot_nt(dh1.astype(BF16), w_ref[...])
        _, vjp_s = jax.vjp(_rms, ys_ref[...], gs_ref[...])
        dys, dgs = vjp_s(dmixed[:, 0:512])
        _, vjp_a = jax.vjp(_rms, ya_ref[...], ga_ref[...])
        dya, dga = vjp_a(dmixed[:, 512:1024])
        dys_ref[...] = dys
        dya_ref[...] = dya
        dgs_ref[...] += dgs
        dga_ref[...] += dga
        dgf_ref[...] += dgf

    row = lambda w: pl.BlockSpec((tm, w), lambda i: (i, 0))
    full = lambda a: pl.BlockSpec(a.shape, lambda i: (0,) * a.ndim)
    vec = lambda w: pl.BlockSpec((1, w), lambda i: (0, 0))
    return pl.pallas_call(
        body, name="mix_bwd", grid=(n // tm,),
        in_specs=[row(D_MODEL), pl.BlockSpec((n_parts, tm, D_MODEL), lambda i: (0, i, 0)), row(D_MODEL), row(512),
                  row(512), full(gs), full(ga), full(wout), full(gf)],
        out_specs=(row(D_MODEL), row(512), row(512), vec(512), vec(512), vec(D_MODEL)),
        out_shape=(jax.ShapeDtypeStruct((n, D_MODEL), F32), jax.ShapeDtypeStruct((n, 512), F32),
                   jax.ShapeDtypeStruct((n, 512), F32), jax.ShapeDtypeStruct((1, 512), F32),
                   jax.ShapeDtypeStruct((1, 512), F32), jax.ShapeDtypeStruct((1, D_MODEL), F32)),
        compiler_params=_cparams("arbitrary"),
    )(dy, dhn2_parts, h1, ys, ya, gs, ga, wout, gf)


HALO = 16


def _conv3(ue, cw):
    return cw[2:3] * ue + cw[1:2] * pltpu.roll(ue, 1, 0) + cw[0:1] * pltpu.roll(ue, 2, 0) + cw[3:4]


def _ffn_fwd(hn2, h1, target, wup_g, wup_v, cw_g, cw_v, wdown, seq_len, tm=512, fb=256):
    n = hn2.shape[0]
    nj = D_FF_PAD // fb
    hb = tm // HALO

    def body(hn_ref, halo_ref, h1_ref, tgt_ref, wg_ref, wv_ref, cg_ref, cv_ref, wd_ref,
             ug_ref, uv_ref, dy_ref, loss_ref, acc):
        i, j = pl.program_id(0), pl.program_id(1)
        seq_start = (i * tm) % seq_len == 0
        halo = halo_ref[...]
        halo = jnp.where(seq_start, jnp.zeros_like(halo), halo)
        he = jnp.concatenate([halo, hn_ref[...]], axis=0)
        ueg = _dot(he, wg_ref[...])
        uev = _dot(he, wv_ref[...])
        ug_ref[...] = ueg[HALO:]
        uv_ref[...] = uev[HALO:]
        cg = _conv3(ueg, cg_ref[...])[HALO:]
        cv = _conv3(uev, cv_ref[...])[HALO:]
        act = (jax.nn.silu(cg) * cv).astype(BF16)
        part = _dot(act, wd_ref[...])

        @pl.when(j == 0)
        def _():
            acc[...] = part

        @pl.when(j > 0)
        def _():
            acc[...] += part

        @pl.when(j == nj - 1)
        def _():
            err = h1_ref[...] + acc[...] - tgt_ref[...]
            dy_ref[...] = err * (1.0 / D_MODEL)
            loss_ref[0] = jnp.sum(err * err, axis=0, keepdims=True)

    row = pl.BlockSpec((tm, D_MODEL), lambda i, j: (i, 0))
    return pl.pallas_call(
        body, name="ffn_fwd", grid=(n // tm, nj),
        in_specs=[row, pl.BlockSpec((HALO, D_MODEL), lambda i, j: (jnp.maximum(i * hb - 1, 0), 0)), row, row,
                  pl.BlockSpec((D_MODEL, fb), lambda i, j: (0, j)), pl.BlockSpec((D_MODEL, fb), lambda i, j: (0, j)),
                  pl.BlockSpec((8, fb), lambda i, j: (0, j)), pl.BlockSpec((8, fb), lambda i, j: (0, j)),
                  pl.BlockSpec((fb, D_MODEL), lambda i, j: (j, 0))],
        out_specs=(pl.BlockSpec((tm, fb), lambda i, j: (i, j)), pl.BlockSpec((tm, fb), lambda i, j: (i, j)), row,
                   pl.BlockSpec((1, 1, D_MODEL), lambda i, j: (i, 0, 0))),
        out_shape=(jax.ShapeDtypeStruct((n, D_FF_PAD), F32), jax.ShapeDtypeStruct((n, D_FF_PAD), F32),
                   jax.ShapeDtypeStruct((n, D_MODEL), F32), jax.ShapeDtypeStruct((n // tm, 1, D_MODEL), F32)),
        scratch_shapes=[pltpu.VMEM((tm, D_MODEL), F32)],
        compiler_params=_cparams("parallel", "arbitrary"),
    )(hn2, hn2, h1, target, wup_g, wup_v, cw_g, cw_v, wdown)


def _ffn_bwd(dy, ug, uv, wup_g, wup_v, cw_g, cw_v, wdown, seq_len, tm=512, fb=256):
    n = dy.shape[0]
    nj = D_FF_PAD // fb
    hb = tm // HALO
    last_hb = n // HALO - 1
    rows = tm + HALO

    def body(dy_ref, dyn_ref, ugp_ref, ugm_ref, ugn_ref, uvp_ref, uvm_ref, uvn_ref, wg_ref, wv_ref, cg_ref, cv_ref,
             wd_ref, dug_ref, duv_ref, act_ref, dhn_ref, dcg_ref, dcv_ref, acc):
        i, j = pl.program_id(0), pl.program_id(1)
        seq_start = (i * tm) % seq_len == 0
        seq_end = ((i + 1) * tm) % seq_len == 0
        dyn = dyn_ref[...]
        dyn = jnp.where(seq_end, jnp.zeros_like(dyn), dyn)
        d_out = jnp.concatenate([dy_ref[...], dyn], axis=0).astype(BF16)
        d_act = _dot_nt(d_out, wd_ref[...])

        def pre_act(up_ref, um_ref, un_ref, cw):
            up = up_ref[...]
            up = jnp.where(seq_start, jnp.zeros_like(up), up)
            ue = jnp.concatenate([up, um_ref[...], un_ref[...]], axis=0)
            return ue, _conv3(ue, cw)[HALO:]

        cwg, cwv = cg_ref[...], cv_ref[...]
        ueg, cge = pre_act(ugp_ref, ugm_ref, ugn_ref, cwg)
        uev, cve = pre_act(uvp_ref, uvm_ref, uvn_ref, cwv)
        act, vjp_act = jax.vjp(lambda g, v: jax.nn.silu(g) * v, cge, cve)
        dcge, dcve = vjp_act(d_act)
        act_ref[...] = act[:tm].astype(BF16)

        def conv_t(dc, cw):
            return (cw[2:3] * dc + cw[1:2] * pltpu.roll(dc, rows - 1, 0) + cw[0:1] * pltpu.roll(dc, rows - 2, 0))[:tm]

        dug = conv_t(dcge, cwg).astype(BF16)
        duv = conv_t(dcve, cwv).astype(BF16)
        dug_ref[...] = dug
        duv_ref[...] = duv
        part = _dot_nt(dug, wg_ref[...]) + _dot_nt(duv, wv_ref[...])

        @pl.when(j == 0)
        def _():
            acc[...] = part

        @pl.when(j > 0)
        def _():
            acc[...] += part

        @pl.when(j == nj - 1)
        def _():
            dhn_ref[...] = acc[...]

        def cw_grad(dc, ue):
            dcm = dc[:tm]
            taps = [jnp.sum(dcm * pltpu.roll(ue, 2 - k, 0)[HALO:HALO + tm], axis=0, keepdims=True) for k in (0, 1)]
            taps.append(jnp.sum(dcm * ue[HALO:HALO + tm], axis=0, keepdims=True))
            taps.append(jnp.sum(dcm, axis=0, keepdims=True))
            return jnp.concatenate(taps + [jnp.zeros((4, fb), F32)], axis=0)

        @pl.when(i == 0)
        def _():
            dcg_ref[j] = jnp.zeros((8, fb), F32)
            dcv_ref[j] = jnp.zeros((8, fb), F32)

        dcg_ref[j] += cw_grad(dcge, ueg)
        dcv_ref[j] += cw_grad(dcve, uev)

    row = pl.BlockSpec((tm, D_MODEL), lambda i, j: (i, 0))
    u_prev = pl.BlockSpec((HALO, fb), lambda i, j: (jnp.maximum(i * hb - 1, 0), j))
    u_main = pl.BlockSpec((tm, fb), lambda i, j: (i, j))
    u_next = pl.BlockSpec((HALO, fb), lambda i, j: (jnp.minimum((i + 1) * hb, last_hb), j))
    w_col = pl.BlockSpec((D_MODEL, fb), lambda i, j: (0, j))
    c_col = pl.BlockSpec((8, fb), lambda i, j: (0, j))
    dc_spec = pl.BlockSpec((nj, 8, fb), lambda i, j: (0, 0, 0))
    return pl.pallas_call(
        body, name="ffn_bwd", grid=(n // tm, nj),
        in_specs=[row, pl.BlockSpec((HALO, D_MODEL), lambda i, j: (jnp.minimum((i + 1) * hb, last_hb), 0)),
                  u_prev, u_main, u_next, u_prev, u_main, u_next, w_col, w_col, c_col, c_col,
                  pl.BlockSpec((fb, D_MODEL), lambda i, j: (j, 0))],
        out_specs=(u_main, u_main, u_main, row, dc_spec, dc_spec),
        out_shape=(jax.ShapeDtypeStruct((n, D_FF_PAD), BF16), jax.ShapeDtypeStruct((n, D_FF_PAD), BF16),
                   jax.ShapeDtypeStruct((n, D_FF_PAD), BF16), jax.ShapeDtypeStruct((n, D_MODEL), F32),
                   jax.ShapeDtypeStruct((nj, 8, fb), F32), jax.ShapeDtypeStruct((nj, 8, fb), F32)),
        scratch_shapes=[pltpu.VMEM((tm, D_MODEL), F32)],
        compiler_params=_cparams("arbitrary", "arbitrary"),
    )(dy, dy, ug, ug, ug, uv, uv, uv, wup_g, wup_v, cw_g, cw_v, wdown)


def _s5_param_fn(lr, li, ldt, br, bi):
    dt = jnp.exp(ldt)
    mag = jnp.exp(lr * dt)
    ab_re = mag * jnp.cos(li * dt)
    ab_im = mag * jnp.sin(li * dt)
    nr = ab_re - 1.0
    ni = ab_im
    den = lr * lr + li * li
    q_re = (nr * lr + ni * li) / den
    q_im = (ni * lr - nr * li) / den
    bb_re = q_re * br - q_im * bi
    bb_im = q_re * bi + q_im * br
    return ab_re, ab_im, bb_re, bb_im


def _s5_param_fwd(lr, li, ldt, br, bi):
    def body(lr_ref, li_ref, ldt_ref, br_ref, bi_ref, ar_ref, ai_ref, bbr_ref, bbi_ref):
        ar, ai, bbr, bbi = _s5_param_fn(lr_ref[...], li_ref[...], ldt_ref[...], br_ref[...], bi_ref[...])
        ar_ref[...] = ar
        ai_ref[...] = ai
        bbr_ref[...] = bbr
        bbi_ref[...] = bbi

    return pl.pallas_call(
        body, name="s5_param_fwd",
        out_shape=(jax.ShapeDtypeStruct(lr.shape, F32), jax.ShapeDtypeStruct(lr.shape, F32),
                   jax.ShapeDtypeStruct(br.shape, F32), jax.ShapeDtypeStruct(br.shape, F32)),
    )(lr, li, ldt, br, bi)


def _s5_param_bwd(lr, li, ldt, br, bi, dar, dai, dbbr, dbbi):
    def body(lr_ref, li_ref, ldt_ref, br_ref, bi_ref, dar_ref, dai_ref, dbbr_ref, dbbi_ref,
             dlr_ref, dli_ref, dldt_ref, dbr_ref, dbi_ref):
        _, vjp = jax.vjp(_s5_param_fn, lr_ref[...], li_ref[...], ldt_ref[...], br_ref[...], bi_ref[...])
        dlr, dli, dldt, dbr, dbi = vjp((dar_ref[...], dai_ref[...], dbbr_ref[...], dbbi_ref[...]))
        dlr_ref[...] = dlr
        dli_ref[...] = dli
        dldt_ref[...] = dldt
        dbr_ref[...] = dbr
        dbi_ref[...] = dbi

    return pl.pallas_call(
        body, name="s5_param_bwd",
        out_shape=(jax.ShapeDtypeStruct(lr.shape, F32), jax.ShapeDtypeStruct(lr.shape, F32),
                   jax.ShapeDtypeStruct(ldt.shape, F32), jax.ShapeDtypeStruct(br.shape, F32),
                   jax.ShapeDtypeStruct(br.shape, F32)),
    )(lr, li, ldt, br, bi, dar, dai, dbbr, dbbi)


S5_CHUNK = 256
S5_STATES = 512
S5_BLOCKS = 4


def _cpow_rows(ar, ai, count):
    rs, im = [ar], [ai]
    for _ in range(count - 1):
        pr, pi = rs[-1], im[-1]
        rs.append(pr * ar - pi * ai)
        im.append(pr * ai + pi * ar)
    return rs, im


def _scan_in_groups(vr, vi, pr, pi, rm, reverse):
    n = vr.shape[0]
    for k in (1, 2, 4):
        if reverse:
            sr, si, keep = pltpu.roll(vr, n - k, 0), pltpu.roll(vi, n - k, 0), rm < SUBLANES - k
        else:
            sr, si, keep = pltpu.roll(vr, k, 0), pltpu.roll(vi, k, 0), rm >= k
        sr = jnp.where(keep, sr, 0.0)
        si = jnp.where(keep, si, 0.0)
        kr, ki = pr[k - 1], pi[k - 1]
        vr, vi = vr + kr * sr - ki * si, vi + kr * si + ki * sr
    return vr, vi


def _carry_over_groups(xr_s, xi_s, wr, wi, c0r, c0i, reverse):
    groups = xr_s.shape[0] // SUBLANES
    pick = 0 if reverse else SUBLANES - 1

    def step(q, carry):
        cr, ci = carry
        r = groups - 1 - q if reverse else q
        o = pl.multiple_of(r * SUBLANES, SUBLANES)
        vr = xr_s[pl.ds(o, SUBLANES), :]
        vi = xi_s[pl.ds(o, SUBLANES), :]
        nr = vr + wr * cr - wi * ci
        ni = vi + wr * ci + wi * cr
        xr_s[pl.ds(o, SUBLANES), :] = nr
        xi_s[pl.ds(o, SUBLANES), :] = ni
        return (jnp.broadcast_to(nr[pick:pick + 1], nr.shape), jnp.broadcast_to(ni[pick:pick + 1], ni.shape))

    return lax.fori_loop(0, groups, step, (c0r, c0i))


def _s5_state_scan(u_b, bbr, bbi, pr, pi, rm, xr_s, xi_s, c0r, c0i):
    bur = _dot(u_b, bbr)
    bui = _dot(u_b, bbi)
    bur, bui = _scan_in_groups(bur, bui, pr, pi, rm, False)
    xr_s[...] = bur
    xi_s[...] = bui
    w8r = jnp.concatenate(pr, axis=0)
    w8i = jnp.concatenate(pi, axis=0)
    return _carry_over_groups(xr_s, xi_s, w8r, w8i, c0r, c0i, False)


def _s5_fwd(u, a_re, a_im, bbr, bbi, cr, ci, d_skip, n_seq):
    n = u.shape[0]
    seq_len = n // n_seq
    nt = seq_len // S5_CHUNK
    tc = S5_CHUNK

    def body(u_ref, ar_ref, ai_ref, bbr_ref, bbi_ref, cr_ref, ci_ref, d_ref, y_ref, str_ref, sti_ref,
             xr_s, xi_s, car_r, car_i):
        t = pl.program_id(2)

        @pl.when(t == 0)
        def _():
            car_r[...] = jnp.zeros_like(car_r)
            car_i[...] = jnp.zeros_like(car_i)
        pr, pi = _cpow_rows(ar_ref[0], ai_ref[0], SUBLANES)
        rm = lax.broadcasted_iota(jnp.int32, (tc, S5_STATES), 0) & (SUBLANES - 1)
        str_ref[0, 0] = car_r[...]
        sti_ref[0, 0] = car_i[...]
        u_t = u_ref[...]
        cfr, cfi = _s5_state_scan(u_t.astype(BF16), bbr_ref[0], bbi_ref[0], pr, pi, rm, xr_s, xi_s,
                                  car_r[...], car_i[...])
        car_r[...] = cfr
        car_i[...] = cfi
        y = _dot(xr_s[...].astype(BF16), cr_ref[0]) - _dot(xi_s[...].astype(BF16), ci_ref[0])
        y_ref[...] = y + d_ref[...] * u_t

    u_spec = pl.BlockSpec((tc, LANES), lambda cb, b, t: (b * nt + t, cb))
    a_spec = pl.BlockSpec((1, 1, S5_STATES), lambda cb, b, t: (cb, 0, 0))
    bb_spec = pl.BlockSpec((1, LANES, S5_STATES), lambda cb, b, t: (cb, 0, 0))
    c_spec = pl.BlockSpec((1, S5_STATES, LANES), lambda cb, b, t: (cb, 0, 0))
    st_spec = pl.BlockSpec((1, 1, SUBLANES, S5_STATES), lambda cb, b, t: (cb, b * nt + t, 0, 0))
    st_shape = jax.ShapeDtypeStruct((S5_BLOCKS, n_seq * nt, SUBLANES, S5_STATES), F32)
    return pl.pallas_call(
        body, name="s5_fwd", grid=(S5_BLOCKS, n_seq, nt),
        in_specs=[u_spec, a_spec, a_spec, bb_spec, bb_spec, c_spec, c_spec,
                  pl.BlockSpec((1, LANES), lambda cb, b, t: (0, cb))],
        out_specs=(u_spec, st_spec, st_spec),
        out_shape=(jax.ShapeDtypeStruct((n, D_SSM), F32), st_shape, st_shape),
        scratch_shapes=[pltpu.VMEM((tc, S5_STATES), F32), pltpu.VMEM((tc, S5_STATES), F32),
                        pltpu.VMEM((SUBLANES, S5_STATES), F32), pltpu.VMEM((SUBLANES, S5_STATES), F32)],
        compiler_params=_cparams("parallel", "arbitrary", "arbitrary"),
    )(u, a_re, a_im, bbr, bbi, cr, ci, d_skip)


def _s5_bwd(u, dy, st_r, st_i, a_re, a_im, bbr, bbi, cr, ci, d_skip, n_seq):
    n = u.shape[0]
    seq_len = n // n_seq
    nt = seq_len // S5_CHUNK
    tc = S5_CHUNK

    def body(u_ref, dy_ref, str_ref, sti_ref, ar_ref, ai_ref, bbr_ref, bbi_ref, cr_ref, ci_ref, d_ref,
             du_ref, dbbr_ref, dbbi_ref, dcr_ref, dci_ref, dar_ref, dai_ref, dd_ref,
             xr_s, xi_s, gr_s, gi_s, car_r, car_i):
        b, t = pl.program_id(1), pl.program_id(2)

        @pl.when((b == 0) & (t == 0))
        def _():
            for ref in (dbbr_ref, dbbi_ref, dcr_ref, dci_ref, dar_ref, dai_ref, dd_ref):
                ref[...] = jnp.zeros_like(ref)

        @pl.when(t == 0)
        def _():
            car_r[...] = jnp.zeros_like(car_r)
            car_i[...] = jnp.zeros_like(car_i)
        ar, ai = ar_ref[0], ai_ref[0]
        pr, pi = _cpow_rows(ar, ai, SUBLANES)
        row = lax.broadcasted_iota(jnp.int32, (tc, S5_STATES), 0)
        rm = row & (SUBLANES - 1)
        u_t = u_ref[...]
        u_b = u_t.astype(BF16)
        dy_t = dy_ref[...]
        dy_b = dy_t.astype(BF16)
        s0r, s0i = str_ref[0, 0], sti_ref[0, 0]
        _s5_state_scan(u_b, bbr_ref[0], bbi_ref[0], pr, pi, rm, xr_s, xi_s, s0r, s0i)
        xr, xi = xr_s[...], xi_s[...]
        gr = _dot_nt(dy_b, cr_ref[0])
        gi = -_dot_nt(dy_b, ci_ref[0])
        npi = [-v for v in pi]
        gr, gi = _scan_in_groups(gr, gi, pr, npi, rm, True)
        gr_s[...] = gr
        gi_s[...] = gi
        w8r = jnp.concatenate(pr[::-1], axis=0)
        w8i = jnp.concatenate(npi[::-1], axis=0)
        cfr, cfi = _carry_over_groups(gr_s, gi_s, w8r, w8i, car_r[...], car_i[...], True)
        car_r[...] = cfr
        car_i[...] = cfi
        gr, gi = gr_s[...], gi_s[...]
        gr_b, gi_b = gr.astype(BF16), gi.astype(BF16)
        du_ref[...] = _dot_nt(gr_b, bbr_ref[0]) + _dot_nt(gi_b, bbi_ref[0]) + d_ref[...] * dy_t
        dbbr_ref[0] += _dot_tn(u_b, gr_b)
        dbbi_ref[0] += _dot_tn(u_b, gi_b)
        dcr_ref[0] += _dot_tn(xr.astype(BF16), dy_b)
        dci_ref[0] -= _dot_tn(xi.astype(BF16), dy_b)
        dd_ref[0] += jnp.sum((dy_t * u_t).reshape(tc // SUBLANES, SUBLANES, LANES), axis=0)
        first = row == 0
        xpr = jnp.where(first, jnp.broadcast_to(s0r[0:1], xr.shape), pltpu.roll(xr, 1, 0))
        xpi = jnp.where(first, jnp.broadcast_to(s0i[0:1], xi.shape), pltpu.roll(xi, 1, 0))
        shp = (tc // SUBLANES, SUBLANES, S5_STATES)
        dar_ref[0] += jnp.sum((gr * xpr + gi * xpi).reshape(shp), axis=0)
        dai_ref[0] += jnp.sum((gi * xpr - gr * xpi).reshape(shp), axis=0)

    u_spec = pl.BlockSpec((tc, LANES), lambda cb, b, t: (b * nt + nt - 1 - t, cb))
    a_spec = pl.BlockSpec((1, 1, S5_STATES), lambda cb, b, t: (cb, 0, 0))
    bb_spec = pl.BlockSpec((1, LANES, S5_STATES), lambda cb, b, t: (cb, 0, 0))
    c_spec = pl.BlockSpec((1, S5_STATES, LANES), lambda cb, b, t: (cb, 0, 0))
    st_spec = pl.BlockSpec((1, 1, SUBLANES, S5_STATES), lambda cb, b, t: (cb, b * nt + nt - 1 - t, 0, 0))
    da_spec = pl.BlockSpec((1, SUBLANES, S5_STATES), lambda cb, b, t: (cb, 0, 0))
    dd_spec = pl.BlockSpec((1, SUBLANES, LANES), lambda cb, b, t: (cb, 0, 0))
    big = pltpu.VMEM((tc, S5_STATES), F32)
    small = pltpu.VMEM((SUBLANES, S5_STATES), F32)
    return pl.pallas_call(
        body, name="s5_bwd", grid=(S5_BLOCKS, n_seq, nt),
        in_specs=[u_spec, u_spec, st_spec, st_spec, a_spec, a_spec, bb_spec, bb_spec, c_spec, c_spec,
                  pl.BlockSpec((1, LANES), lambda cb, b, t: (0, cb))],
        out_specs=(u_spec, bb_spec, bb_spec, c_spec, c_spec, da_spec, da_spec, dd_spec),
        out_shape=(jax.ShapeDtypeStruct((n, D_SSM), F32),
                   jax.ShapeDtypeStruct((S5_BLOCKS, LANES, S5_STATES), F32),
                   jax.ShapeDtypeStruct((S5_BLOCKS, LANES, S5_STATES), F32),
                   jax.ShapeDtypeStruct((S5_BLOCKS, S5_STATES, LANES), F32),
                   jax.ShapeDtypeStruct((S5_BLOCKS, S5_STATES, LANES), F32),
                   jax.ShapeDtypeStruct((S5_BLOCKS, SUBLANES, S5_STATES), F32),
                   jax.ShapeDtypeStruct((S5_BLOCKS, SUBLANES, S5_STATES), F32),
                   jax.ShapeDtypeStruct((S5_BLOCKS, SUBLANES, LANES), F32)),
        scratch_shapes=[big, big, big, big, small, small],
        compiler_params=_cparams("parallel", "arbitrary", "arbitrary"),
    )(u, dy, st_r, st_i, a_re, a_im, bbr, bbi, cr, ci, d_skip)


CUM_BLOCK = 128


def _tri(lower):
    r = lax.broadcasted_iota(jnp.int32, (CUM_BLOCK, CUM_BLOCK), 0)
    c = lax.broadcasted_iota(jnp.int32, (CUM_BLOCK, CUM_BLOCK), 1)
    return jnp.where(r >= c if lower else r <= c, 1.0, 0.0).astype(F32)


def _fprep_fwd(fl, bf, n_seq):
    n = fl.shape[0]
    seq_len = n // n_seq
    nb = seq_len // CUM_BLOCK

    def body(fl_ref, bf_ref, cum_ref):
        tril = _tri(True)
        carry = jnp.zeros((1, LANES), F32)
        for blk in range(nb):
            rows = slice(blk * CUM_BLOCK, (blk + 1) * CUM_BLOCK)
            lf = jax.nn.log_sigmoid(fl_ref[rows, :] + bf_ref[...])
            cs = jnp.dot(tril, lf, preferred_element_type=F32, precision=HIGHEST) + carry
            cum_ref[rows, :] = cs
            carry = cs[CUM_BLOCK - 1:CUM_BLOCK, :]

    spec = pl.BlockSpec((seq_len, LANES), lambda b: (b, 0))
    return pl.pallas_call(
        body, name="fprep_fwd", grid=(n_seq,), in_specs=[spec, pl.BlockSpec((1, LANES), lambda b: (0, 0))],
        out_specs=spec, out_shape=jax.ShapeDtypeStruct((n, LANES), F32), compiler_params=_cparams("parallel"),
    )(fl, bf)


def _fprep_bwd(dcum, fl, bf, n_seq):
    n = fl.shape[0]
    seq_len = n // n_seq
    nb = seq_len // CUM_BLOCK

    def body(dcum_ref, fl_ref, bf_ref, dfl_ref, dbf_ref):
        triu = _tri(False)
        lane = lax.broadcasted_iota(jnp.int32, (CUM_BLOCK, LANES), 1)
        carry = jnp.zeros((1, LANES), F32)
        total = jnp.zeros((1, LANES), F32)
        for blk in reversed(range(nb)):
            rows = slice(blk * CUM_BLOCK, (blk + 1) * CUM_BLOCK)
            rs = jnp.dot(triu, dcum_ref[rows, :], preferred_element_type=F32, precision=HIGHEST) + carry
            carry = rs[0:1, :]
            _, vjp = jax.vjp(jax.nn.log_sigmoid, fl_ref[rows, :] + bf_ref[...])
            dz = jnp.where(lane < N_HEADS, vjp(rs)[0], 0.0)
            dfl_ref[rows, :] = dz
            total = total + jnp.sum(dz, axis=0, keepdims=True)
        dbf_ref[0] = total

    spec = pl.BlockSpec((seq_len, LANES), lambda b: (b, 0))
    return pl.pallas_call(
        body, name="fprep_bwd", grid=(n_seq,), in_specs=[spec, spec, pl.BlockSpec((1, LANES), lambda b: (0, 0))],
        out_specs=(spec, pl.BlockSpec((1, 1, LANES), lambda b: (b, 0, 0))),
        out_shape=(jax.ShapeDtypeStruct((n, LANES), F32), jax.ShapeDtypeStruct((n_seq, 1, LANES), F32)),
        compiler_params=_cparams("parallel"),
    )(dcum, fl, bf)


ATT_TQ = 256
ATT_KSTEP = 512
ATT_SCALE = HEAD_DIM ** -0.5
NEG_BIG = -1e30


def _attn_scores(qe, kb, cq, ck, causal):
    s = _dot_nt(qe, kb) * ATT_SCALE + cq - ck
    return jnp.where(causal, s, NEG_BIG)


def _causal_mask(qi, kend):
    r = lax.broadcasted_iota(jnp.int32, (ATT_TQ, kend), 0) + qi * ATT_TQ
    c = lax.broadcasted_iota(jnp.int32, (ATT_TQ, kend), 1)
    return r >= c


def _attn_specs(n_seq, seq_len):
    nq = seq_len // ATT_TQ
    q_spec = pl.BlockSpec((ATT_TQ, LANES), lambda b, h, q: (b * nq + q, h))
    k_spec = pl.BlockSpec((seq_len, LANES), lambda b, h, q: (b, N_HEADS // 2 + h))
    v_spec = pl.BlockSpec((seq_len, LANES), lambda b, h, q: (b, N_HEADS + h))
    cq_spec = pl.BlockSpec((1, 2, ATT_TQ, 1), lambda b, h, q: (b, h, q, 0))
    ck_spec = pl.BlockSpec((1, 2, 1, seq_len), lambda b, h, q: (b, h, 0, 0))
    return nq, q_spec, k_spec, v_spec, cq_spec, ck_spec


def _head_selectors():
    head0 = lax.broadcasted_iota(jnp.int32, (1, LANES), 1) < HEAD_DIM
    return head0, (head0, jnp.logical_not(head0))


def _for_key_range(qi, seq_len, run):
    per = ATT_KSTEP // ATT_TQ
    for g in range(seq_len // ATT_KSTEP):
        pl.when(qi // per == g)(functools.partial(run, (g + 1) * ATT_KSTEP))


def _attn_fwd(qkv, cq, ck, n_seq):
    n = qkv.shape[0]
    seq_len = n // n_seq
    nq, q_spec, k_spec, v_spec, cq_spec, ck_spec = _attn_specs(n_seq, seq_len)

    def body(q_ref, k_ref, v_ref, cq_ref, ck_ref, o_ref, lse_ref):
        qi = pl.program_id(2)
        q2 = q_ref[...]
        head0, sels = _head_selectors()
        qe = [jnp.where(sel, q2, 0.0).astype(BF16) for sel in sels]

        def run(kend):
            kb = k_ref[0:kend, :].astype(BF16)
            vb = v_ref[0:kend, :].astype(BF16)
            causal = _causal_mask(qi, kend)
            outs = []
            for e in range(2):
                s = _attn_scores(qe[e], kb, cq_ref[0, e], ck_ref[0, e, :, 0:kend], causal)
                mx = jnp.max(s, axis=1, keepdims=True)
                p = jnp.exp(s - mx)
                den = jnp.sum(p, axis=1, keepdims=True)
                outs.append(_dot(p.astype(BF16), vb) / den)
                lse_ref[0, e] = mx + jnp.log(den)
            o_ref[...] = jnp.where(head0, outs[0], outs[1])

        _for_key_range(qi, seq_len, run)

    return pl.pallas_call(
        body, name="attn_fwd", grid=(n_seq, N_HEADS // 2, nq),
        in_specs=[q_spec, k_spec, v_spec, cq_spec, ck_spec],
        out_specs=(q_spec, cq_spec),
        out_shape=(jax.ShapeDtypeStruct((n, D_ATTN), F32), jax.ShapeDtypeStruct((n_seq, N_HEADS, seq_len, 1), F32)),
        compiler_params=_cparams("parallel", "parallel", "parallel"),
    )(qkv, qkv, qkv, cq, ck)


def _attn_bwd(qkv, cq, ck, o, do, lse, n_seq):
    n = qkv.shape[0]
    seq_len = n // n_seq
    nq, q_spec, k_spec, v_spec, cq_spec, ck_spec = _attn_specs(n_seq, seq_len)
    kv_out = pl.BlockSpec((seq_len, LANES), lambda b, h, q: (b, h))

    def body(q_ref, k_ref, v_ref, cq_ref, ck_ref, o_ref, do_ref, lse_ref, dq_ref, dk_ref, dv_ref, dcq_ref, dck_ref):
        qi = pl.program_id(2)

        @pl.when(qi == 0)
        def _():
            dk_ref[...] = jnp.zeros_like(dk_ref)
            dv_ref[...] = jnp.zeros_like(dv_ref)
            dck_ref[...] = jnp.zeros_like(dck_ref)
        q2 = q_ref[...]
        do2 = do_ref[...]
        o2 = o_ref[...]
        head0, sels = _head_selectors()
        qe = [jnp.where(sel, q2, 0.0).astype(BF16) for sel in sels]
        doe = [jnp.where(sel, do2, 0.0) for sel in sels]
        doe_b = [d.astype(BF16) for d in doe]
        delta = [jnp.sum(d * o2, axis=1, keepdims=True) for d in doe]

        def run(kend):
            kb = k_ref[0:kend, :].astype(BF16)
            vb = v_ref[0:kend, :].astype(BF16)
            causal = _causal_mask(qi, kend)
            dqs = []
            dk = jnp.zeros((kend, LANES), F32)
            dv = jnp.zeros((kend, LANES), F32)
            for e in range(2):
                s = _attn_scores(qe[e], kb, cq_ref[0, e], ck_ref[0, e, :, 0:kend], causal)
                p = jnp.exp(s - lse_ref[0, e])
                ds = p * (_dot_nt(doe_b[e], vb) - delta[e])
                ds_b = ds.astype(BF16)
                dqs.append(_dot(ds_b, kb))
                dk = dk + _dot_tn(ds_b, qe[e])
                dv = dv + _dot_tn(p.astype(BF16), doe_b[e])
                dcq_ref[0, e] = jnp.sum(ds, axis=1, keepdims=True)
                dck_ref[0, e, :, 0:kend] -= jnp.sum(ds, axis=0, keepdims=True)
            dk_ref[0:kend, :] += dk * ATT_SCALE
            dv_ref[0:kend, :] += dv
            dq_ref[...] = jnp.where(head0, dqs[0], dqs[1]) * ATT_SCALE

        _for_key_range(qi, seq_len, run)

    return pl.pallas_call(
        body, name="attn_bwd", grid=(n_seq, N_HEADS // 2, nq),
        in_specs=[q_spec, k_spec, v_spec, cq_spec, ck_spec, q_spec, q_spec, cq_spec],
        out_specs=(q_spec, kv_out, kv_out, cq_spec, ck_spec),
        out_shape=(jax.ShapeDtypeStruct((n, D_ATTN), F32), jax.ShapeDtypeStruct((n, D_ATTN), F32),
                   jax.ShapeDtypeStruct((n, D_ATTN), F32),
                   jax.ShapeDtypeStruct((n_seq, N_HEADS, seq_len, 1), F32),
                   jax.ShapeDtypeStruct((n_seq, N_HEADS, 1, seq_len), F32)),
        compiler_params=_cparams("parallel", "parallel", "arbitrary"),
    )(qkv, qkv, qkv, cq, ck, o, do, lse)


WEIGHT_NAMES = ("norm_mix", "w_in", "b_forget", "lam_re", "lam_im", "b_re", "b_im", "c_re", "c_im", "d_skip", "log_dt",
                "w_glu", "b_glu", "q_norm", "k_norm", "norm_out_ssm", "norm_out_attn", "w_out", "norm_ffn", "w_up",
                "conv_w", "conv_b", "w_down")
SHARDED = ("w_in", "w_glu", "w_out", "w_up", "conv_w", "w_down")
ADAM_ROWS = {"w_in": 256, "w_glu": 64, "w_out": 128, "w_up": 128, "conv_w": 3, "w_down": 344}
PACK_ROWS = SUBLANES * LANES
ROWS_DOWN = D_FF // N_DEV
ROWS_OUT = D_MODEL // N_DEV
ROWS_GLU = D_SSM * D_SSM // N_DEV // D_MODEL


def _pad_to(a, axis, size):
    pad = [(0, 0)] * a.ndim
    pad[axis] = (0, size - a.shape[axis])
    return jnp.pad(a, pad)


def _block_diag(t, transpose):
    t4 = t.reshape(S5_BLOCKS, 8, SSM_GROUP, SSM_STATE)
    eye = jnp.eye(8, dtype=t.dtype)
    if transpose:
        e = jnp.swapaxes(t4, 2, 3)[:, :, :, None, :] * eye[None, :, None, :, None]
        return e.reshape(S5_BLOCKS, S5_STATES, LANES)
    e = t4[:, :, :, None, :] * eye[None, :, None, :, None]
    return e.reshape(S5_BLOCKS, LANES, S5_STATES)


def _block_diag_extract(m, transpose):
    if transpose:
        m5 = m.reshape(S5_BLOCKS, 8, SSM_STATE, 8, SSM_GROUP)
        d = jnp.stack([m5[:, i, :, i, :] for i in range(8)], axis=1)
        return jnp.swapaxes(d, 2, 3).reshape(N_GROUPS, SSM_GROUP, SSM_STATE)
    m5 = m.reshape(S5_BLOCKS, 8, SSM_GROUP, 8, SSM_STATE)
    d = jnp.stack([m5[:, i, :, i, :] for i in range(8)], axis=1)
    return d.reshape(N_GROUPS, SSM_GROUP, SSM_STATE)


def _pack(pieces):
    flat = jnp.concatenate([p.reshape(-1).astype(F32) for p in pieces])
    size = -(-flat.shape[0] // PACK_ROWS) * PACK_ROWS
    return _pad_to(flat, 0, size).reshape(-1, LANES)


def _unpack(packed, shapes):
    flat = packed.reshape(-1)
    out, off = [], 0
    for shp in shapes:
        size = math.prod(shp)
        out.append(flat[off:off + size].reshape(shp))
        off += size
    return out


def kernel(x, norm_mix, w_in, b_forget, lam_re, lam_im, b_re, b_im, c_re, c_im, d_skip, log_dt, w_glu, b_glu, q_norm, k_norm, norm_out_ssm, norm_out_attn, w_out, norm_ffn, w_up, conv_w, conv_b, w_down, loss_target, m_norm_mix, m_w_in, m_b_forget, m_lam_re, m_lam_im, m_b_re, m_b_im, m_c_re, m_c_im, m_d_skip, m_log_dt, m_w_glu, m_b_glu, m_q_norm, m_k_norm, m_norm_out_ssm, m_norm_out_attn, m_w_out, m_norm_ffn, m_w_up, m_conv_w, m_conv_b, m_w_down, v_norm_mix, v_w_in, v_b_forget, v_lam_re, v_lam_im, v_b_re, v_b_im, v_c_re, v_c_im, v_d_skip, v_log_dt, v_w_glu, v_b_glu, v_q_norm, v_k_norm, v_norm_out_ssm, v_norm_out_attn, v_w_out, v_norm_ffn, v_w_up, v_conv_w, v_conv_b, v_w_down):
    given = dict(locals())
    weights = {k: given[k] for k in WEIGHT_NAMES}
    mom1 = {k: given["m_" + k] for k in WEIGHT_NAMES}
    mom2 = {k: given["v_" + k] for k in WEIGHT_NAMES}
    n_seq, seq_len, _ = x.shape
    n = n_seq * seq_len
    xf = x.reshape(n, D_MODEL)
    target = loss_target.reshape(n, D_MODEL)
    me_idx = 4 * lax.axis_index("x") + 2 * lax.axis_index("y") + lax.axis_index("c")

    g_in, g_cw = _exchange([w_in[0].astype(BF16), conv_w[0]], [False, False], "gather_w_in")
    rest_flags = [False] * 2
    rows_w = jnp.concatenate([w_down[0], w_out[0], w_glu[0].reshape(ROWS_GLU, D_MODEL)], axis=0).astype(BF16)
    w_sems = _exchange_start([rows_w, w_up[0].astype(BF16)], rest_flags, g_in, "gather_rest_start", 0)
    norm_mix = norm_mix + w_sems[4][0, 0]
    w_in_p = _pad_to(jnp.swapaxes(g_in, 0, 1).reshape(D_MODEL, D_IN), 1, D_IN_PAD)

    lr3 = lam_re[0].reshape(N_GROUPS, 1, SSM_STATE)
    li3 = lam_im[0].reshape(N_GROUPS, 1, SSM_STATE)
    ldt3 = log_dt[0].reshape(N_GROUPS, 1, 1)
    br_t = jnp.swapaxes(b_re[0], 1, 2)
    bi_t = jnp.swapaxes(b_im[0], 1, 2)
    ab_re, ab_im, bb_re, bb_im = _s5_param_fwd(lr3, li3, ldt3, br_t, bi_t)
    a_re = ab_re.reshape(S5_BLOCKS, 1, S5_STATES)
    a_im = ab_im.reshape(S5_BLOCKS, 1, S5_STATES)
    bbr = _block_diag(bb_re, False).astype(BF16)
    bbi = _block_diag(bb_im, False).astype(BF16)
    cr = _block_diag(c_re[0], True).astype(BF16)
    ci = _block_diag(c_im[0], True).astype(BF16)

    avg = jnp.kron(jnp.eye(N_HEADS, dtype=F32), jnp.full((HEAD_DIM, HEAD_DIM), 1.0 / HEAD_DIM, F32)).astype(BF16)
    qg = jnp.tile(q_norm, (1, N_HEADS))
    kg = jnp.tile(k_norm, (1, N_HEADS))
    hn, u, qkv, raw, fl = _inproj_fwd(xf, norm_mix, w_in_p, avg, qg, kg)
    yc, st_r, st_i = _s5_fwd(u, a_re, a_im, bbr, bbi, cr, ci, d_skip, n_seq)
    bf = _pad_to(b_forget, 1, LANES)
    cum = _fprep_fwd(fl, bf, n_seq)
    cum8 = jnp.swapaxes(cum[:, :N_HEADS].reshape(n_seq, seq_len, N_HEADS), 1, 2)
    cq = cum8[:, :, :, None]
    ck = cum8[:, :, None, :]
    ya, lse = _attn_fwd(qkv, cq, ck, n_seq)
    (own_rows, own_up), (g_rows, g_up) = _exchange_wait(w_sems[0], w_sems[1], w_sems[2], w_sems[3], rest_flags, ya,
                                                        "gather_rest_wait")
    my_slot = lax.broadcasted_iota(jnp.int32, (N_DEV, 1, 1), 0) == me_idx
    g_rows = jnp.where(my_slot, own_rows[None], g_rows)
    g_up = jnp.where(my_slot, own_up[None], g_up)
    g_down = g_rows[:, :ROWS_DOWN]
    g_out = g_rows[:, ROWS_DOWN:ROWS_DOWN + ROWS_OUT]
    g_glu = g_rows[:, ROWS_DOWN + ROWS_OUT:]
    w_glu_f = g_glu.reshape(D_SSM, D_SSM)
    w_out_f = g_out.reshape(D_MODEL, D_MODEL)
    w_up_f = jnp.swapaxes(g_up, 0, 1).reshape(D_MODEL, 2 * D_FF)
    wup_g = _pad_to(w_up_f[:, :D_FF], 1, D_FF_PAD)
    wup_v = _pad_to(w_up_f[:, D_FF:], 1, D_FF_PAD)
    cw_f = jnp.swapaxes(g_cw, 0, 1).reshape(3, 2 * D_FF)
    cw4 = jnp.concatenate([cw_f, conv_b], axis=0)
    cw_g = _pad_to(_pad_to(cw4[:, :D_FF], 1, D_FF_PAD), 0, SUBLANES)
    cw_v = _pad_to(_pad_to(cw4[:, D_FF:], 1, D_FF_PAD), 0, SUBLANES)
    w_down_p = _pad_to(g_down.reshape(D_FF, D_MODEL), 0, D_FF_PAD)
    ys = _glu_fwd(yc, w_glu_f, b_glu)
    h1, hn2, mixed = _mix_fwd(xf, ys, ya, norm_out_ssm, norm_out_attn, w_out_f, norm_ffn)
    ug, uv, dy, loss_part = _ffn_fwd(hn2, h1, target, wup_g, wup_v, cw_g, cw_v, w_down_p, seq_len)
    loss = lax.psum(0.5 * jnp.sum(loss_part) / D_MODEL, ("x", "y", "c"))

    dug, duv, act, dhn2, dcg, dcv = _ffn_bwd(dy, ug, uv, wup_g, wup_v, cw_g, cw_v, w_down_p, seq_len)
    dh1, dys, dya, d_gs, d_ga, d_gf = _mix_bwd(dy, dhn2[None], h1, ys, ya, norm_out_ssm, norm_out_attn, w_out_f, norm_ffn)
    dyc, gl_b, dz_b, d_bglu = _glu_bwd(yc, dys, w_glu_f, b_glu)

    gw_glu = _tn_matmul(gl_b, dz_b, "dw_glu", D_SSM, D_SSM)
    gw_out = _tn_matmul(mixed, dh1, "dw_out", D_MODEL, D_MODEL)
    gw_up = jnp.concatenate([_tn_matmul(hn2, dug, "dw_up_gate", D_MODEL, D_FF_PAD // 2, out_cols=D_FF),
                             _tn_matmul(hn2, duv, "dw_up_val", D_MODEL, D_FF_PAD // 2, out_cols=D_FF)], axis=1)
    gw_down = _tn_matmul(act, dy, "dw_down", D_FF_PAD // 2, D_MODEL, out_rows=D_FF)
    dcg2 = jnp.swapaxes(dcg, 0, 1).reshape(SUBLANES, D_FF_PAD)[:, :D_FF]
    dcv2 = jnp.swapaxes(dcv, 0, 1).reshape(SUBLANES, D_FF_PAD)[:, :D_FF]
    g_conv = jnp.concatenate([dcg2, dcv2], axis=1)
    by_cols = lambda g, c: jnp.swapaxes(g.reshape(g.shape[0], N_DEV, c), 0, 1)
    early_flags = [True] * 2
    rows_g = jnp.concatenate([gw_down.reshape(N_DEV, ROWS_DOWN, D_MODEL), gw_out.reshape(N_DEV, ROWS_OUT, D_MODEL),
                              gw_glu.reshape(N_DEV, ROWS_GLU, D_MODEL)], axis=1).astype(BF16)
    g_sems = _exchange_start([rows_g, by_cols(gw_up, 2 * D_FF // N_DEV).astype(BF16)], early_flags, dyc,
                             "grad_early_start", 1)
    started = g_sems[4][0, 0]

    du, dbbr, dbbi, dcr, dci, dar, dai, ddk = _s5_bwd(u, dyc, st_r, st_i, a_re, a_im, bbr, bbi, cr, ci,
                                                      d_skip + started, n_seq)
    partial_early = {
        "ab_re": jnp.sum(dar, axis=1), "ab_im": jnp.sum(dai, axis=1),
        "bb_re": _block_diag_extract(dbbr, False), "bb_im": _block_diag_extract(dbbi, False),
        "c_re": _block_diag_extract(dcr, True), "c_im": _block_diag_extract(dci, True),
        "d_skip": jnp.sum(ddk, axis=1), "b_glu": d_bglu,
        "norm_out_ssm": d_gs, "norm_out_attn": d_ga, "norm_ffn": d_gf, "conv_b": g_conv[3],
    }
    early_keys = tuple(partial_early)
    early_shapes = [partial_early[k].shape for k in early_keys]
    p_sems = _exchange_start([_pack([partial_early[k] for k in early_keys])], [False], du, "small_early_start", 2)
    started = started + p_sems[4][0, 0]

    dqn, dkn, dv, dcq, dck = _attn_bwd(qkv, cq, ck + started, ya, dya, lse, n_seq)
    dcum8 = dcq[:, :, :, 0] + dck.reshape(n_seq, N_HEADS, seq_len)
    dcum = _pad_to(jnp.swapaxes(dcum8, 1, 2).reshape(n, N_HEADS), 1, LANES)
    dfl, dbf = _fprep_bwd(dcum, fl, bf, n_seq)
    dx, dproj, d_gmix, d_qg, d_kg = _inproj_bwd(xf, norm_mix, w_in_p, avg, qg, kg, raw, du, dqn, dkn, dv, dfl, dh1)

    gw_in = _tn_matmul(hn, dproj, "dw_in", D_MODEL, D_IN_PAD, out_cols=D_IN)
    partial_late = {
        "norm_mix": d_gmix, "b_forget": jnp.sum(dbf, axis=(0, 1))[:N_HEADS],
        "q_norm": jnp.sum(d_qg.reshape(N_HEADS, HEAD_DIM), axis=0),
        "k_norm": jnp.sum(d_kg.reshape(N_HEADS, HEAD_DIM), axis=0),
    }
    late_keys = tuple(partial_late)
    late_shapes = [partial_late[k].shape for k in late_keys]

    land_in, land_cw, small_parts = _exchange(
        [by_cols(gw_in, D_IN // N_DEV).astype(BF16), by_cols(g_conv[:3], 2 * D_FF // N_DEV),
         _pack([partial_late[k] for k in late_keys])], [True, True, False], "grad_late_exchange")
    (src_rows, src_up), (land_rows, land_up) = _exchange_wait(g_sems[0], g_sems[1], g_sems[2], g_sems[3], early_flags,
                                                              land_in, "grad_early_wait")
    land_down = land_rows[:, :ROWS_DOWN]
    land_out = land_rows[:, ROWS_DOWN:ROWS_DOWN + ROWS_OUT]
    land_glu = land_rows[:, ROWS_DOWN + ROWS_OUT:].reshape(N_DEV, -1, D_SSM)
    own_rows = lax.dynamic_index_in_dim(src_rows, me_idx, 0, keepdims=False)
    own_up = lax.dynamic_index_in_dim(src_up, me_idx, 0, keepdims=False)
    own = {"w_in": None, "conv_w": None, "w_up": own_up, "w_down": own_rows[:ROWS_DOWN],
           "w_out": own_rows[ROWS_DOWN:ROWS_DOWN + ROWS_OUT], "w_glu": own_rows[ROWS_DOWN + ROWS_OUT:].reshape(-1, D_SSM)}
    grads, deltas, new_m, new_v = {}, {}, {}, {}
    for name, land in zip(SHARDED, (land_in, land_glu, land_out, land_up, land_cw, land_down)):
        grads[name], deltas[name], new_m[name], new_v[name] = _adam_sharded(
            land, own[name], weights[name], mom1[name], mom2[name], "adam_" + name, ADAM_ROWS[name])

    (own_pack,), (early_parts,) = _exchange_wait(p_sems[0], p_sems[1], p_sems[2], p_sems[3], [False], land_up,
                                                 "small_early_wait")
    summed = dict(zip(late_keys, _unpack(_sum_partials(small_parts, None, "sum_late_partials"), late_shapes)))
    summed.update(zip(early_keys, _unpack(_sum_partials(early_parts, own_pack, "sum_early_partials"), early_shapes)))
    dlr, dli, dldt, dbr_t, dbi_t = _s5_param_bwd(
        lr3, li3, ldt3, br_t, bi_t, summed["ab_re"].reshape(lr3.shape), summed["ab_im"].reshape(lr3.shape),
        summed["bb_re"], summed["bb_im"])
    small_grads = {
        "norm_mix": summed["norm_mix"], "b_forget": summed["b_forget"], "lam_re": dlr, "lam_im": dli,
        "b_re": jnp.swapaxes(dbr_t, 1, 2), "b_im": jnp.swapaxes(dbi_t, 1, 2), "c_re": summed["c_re"], "c_im": summed["c_im"],
        "d_skip": summed["d_skip"], "log_dt": dldt, "b_glu": summed["b_glu"], "q_norm": summed["q_norm"],
        "k_norm": summed["k_norm"], "norm_out_ssm": summed["norm_out_ssm"], "norm_out_attn": summed["norm_out_attn"],
        "norm_ffn": summed["norm_ffn"], "conv_b": summed["conv_b"],
    }
    repl = tuple(k for k in WEIGHT_NAMES if k not in SHARDED)
    g_list = [small_grads[k].reshape(weights[k].shape) for k in repl]
    d_list, m_list, v_list = _adam_replicated(g_list, [weights[k] for k in repl], [mom1[k] for k in repl],
                                              [mom2[k] for k in repl], "adam_replicated")
    for k, g, d, nm, nv in zip(repl, g_list, d_list, m_list, v_list):
        grads[k], deltas[k], new_m[k], new_v[k] = g, d, nm, nv

    grad_x = dx.reshape(x.shape)
    return (loss, grad_x, *[grads[k] for k in WEIGHT_NAMES], *[deltas[k] for k in WEIGHT_NAMES],
            *[new_m[k] for k in WEIGHT_NAMES], *[new_v[k] for k in WEIGHT_NAMES])
```

```python
import functools
import math

import jax
import jax.numpy as jnp
from jax import lax
from jax.experimental import pallas as pl
from jax.experimental.pallas import tpu as pltpu

F32 = jnp.float32
BF16 = jnp.bfloat16
HIGHEST = lax.Precision.HIGHEST

N_DEV = 8
D_MODEL = 1024
D_SSM = 512
D_ATTN = 512
N_HEADS = 8
HEAD_DIM = 64
N_GROUPS = 32
SSM_GROUP = 16
SSM_STATE = 64
D_FF = 2752
D_FF_PAD = 2816
D_IN = 2056
D_IN_PAD = 2176
EPS = 1e-6
LANES = 128
SUBLANES = 8
VMEM_LIMIT = 56 * 1024 * 1024

ADAM_LR = 0.001
ADAM_B1 = 0.9
ADAM_B2 = 0.999
ADAM_EPS = 1e-08
ADAM_WD = 0.01
ADAM_STEP = 10


def _cparams(*sem):
    return pltpu.CompilerParams(dimension_semantics=sem, vmem_limit_bytes=VMEM_LIMIT)


def _dot(a, b, **kw):
    return jnp.dot(a, b, preferred_element_type=F32, **kw)


def _dot_nt(a, b):
    return lax.dot_general(a, b, (((1,), (1,)), ((), ())), preferred_element_type=F32)


def _dot_tn(a, b):
    return lax.dot_general(a, b, (((0,), (0,)), ((), ())), preferred_element_type=F32)


def _rms(x, g):
    return x * lax.rsqrt(jnp.mean(x * x, axis=-1, keepdims=True) + EPS) * g


def _split_dot(x, avg):
    hi = x.astype(BF16)
    lo = (x - hi.astype(F32)).astype(BF16)
    return _dot(hi, avg) + _dot(lo, avg)


@jax.custom_vjp
def _group_mean(x, avg):
    return _split_dot(x, avg)


def _group_mean_fwd(x, avg):
    return _split_dot(x, avg), avg


def _group_mean_bwd(avg, ct):
    return _split_dot(ct, avg), jnp.zeros_like(avg)


_group_mean.defvjp(_group_mean_fwd, _group_mean_bwd)


def _headnorm(q, avg, g):
    return q * lax.rsqrt(_group_mean(q * q, avg) + EPS) * g


def _exchange(srcs, scatter_flags, name):
    n = len(srcs)
    out_shape = []
    for s, sc in zip(srcs, scatter_flags):
        shp = s.shape if sc else (N_DEV,) + s.shape
        out_shape.append(jax.ShapeDtypeStruct(shp, s.dtype))

    def body(*refs):
        src = refs[:n]
        dst = refs[n:2 * n]
        send_sems, recv_sems, loc_sems = refs[2 * n:]
        x, y, c = lax.axis_index("x"), lax.axis_index("y"), lax.axis_index("c")
        me = 4 * x + 2 * y + c
        peers = []
        for j in range(1, N_DEV):
            px = 1 - x if (j >> 2) & 1 else x
            py = 1 - y if (j >> 1) & 1 else y
            pc = 1 - c if j & 1 else c
            peers.append(((px, py, pc), 4 * px + 2 * py + pc))
        local, sends = [], []
        for k in range(n):
            own = src[k].at[me] if scatter_flags[k] else src[k]
            lc = pltpu.make_async_copy(own, dst[k].at[me], loc_sems.at[k])
            lc.start()
            local.append(lc)
            for j, (pid, pidx) in enumerate(peers):
                s = src[k].at[pidx] if scatter_flags[k] else src[k]
                cp = pltpu.make_async_remote_copy(
                    src_ref=s, dst_ref=dst[k].at[me], send_sem=send_sems.at[k, j], recv_sem=recv_sems.at[k, j],
                    device_id=pid, device_id_type=pl.DeviceIdType.MESH)
                cp.start()
                sends.append(cp)
        for k in range(n):
            for j, (pid, pidx) in enumerate(peers):
                s = src[k].at[pidx] if scatter_flags[k] else src[k]
                pltpu.make_async_remote_copy(
                    src_ref=s, dst_ref=dst[k].at[pidx], send_sem=send_sems.at[k, j], recv_sem=recv_sems.at[k, j],
                    device_id=pid, device_id_type=pl.DeviceIdType.MESH).wait_recv()
        for cp in sends:
            cp.wait_send()
        for lc in local:
            lc.wait()

    any_spec = pl.BlockSpec(memory_space=pl.ANY)
    return pl.pallas_call(
        body, name=name, out_shape=tuple(out_shape),
        in_specs=[any_spec] * n, out_specs=tuple([any_spec] * n),
        scratch_shapes=[pltpu.SemaphoreType.DMA((n, N_DEV - 1)), pltpu.SemaphoreType.DMA((n, N_DEV - 1)),
                        pltpu.SemaphoreType.DMA((n,))],
        compiler_params=pltpu.CompilerParams(has_side_effects=True),
    )(*srcs)


def _peer_list():
    x, y, c = lax.axis_index("x"), lax.axis_index("y"), lax.axis_index("c")
    peers = []
    for j in range(1, N_DEV):
        px = 1 - x if (j >> 2) & 1 else x
        py = 1 - y if (j >> 1) & 1 else y
        pc = 1 - c if j & 1 else c
        peers.append(((px, py, pc), 4 * px + 2 * py + pc))
    return 4 * x + 2 * y + c, peers


def _split_copies(src, land, send_sems, recv_sems, scatter_flags, me, peers, incoming):
    copies = []
    for k in range(len(src)):
        for j, (pid, pidx) in enumerate(peers):
            s = src[k].at[pidx] if scatter_flags[k] else src[k]
            i = k * (N_DEV - 1) + j
            copies.append(pltpu.make_async_remote_copy(
                src_ref=s, dst_ref=land[k].at[pidx if incoming else me], send_sem=send_sems[i],
                recv_sem=recv_sems[i], device_id=pid, device_id_type=pl.DeviceIdType.MESH))
    return copies


def _exchange_start(srcs, scatter_flags, after, name, collective_id):
    n = len(srcs)
    ns = n * (N_DEV - 1)
    hbm = pl.BlockSpec(memory_space=pltpu.HBM)
    sem = pl.BlockSpec(memory_space=pltpu.SEMAPHORE)
    land_shapes = [s.shape if sc else (N_DEV,) + s.shape for s, sc in zip(srcs, scatter_flags)]

    def body(*refs):
        src, land = refs[:n], refs[n:2 * n]
        send_sems = refs[2 * n + 1:2 * n + 1 + ns]
        recv_sems = refs[2 * n + 1 + ns:2 * n + 1 + 2 * ns]
        token = refs[4 * n + 1 + 2 * ns]
        me, peers = _peer_list()
        barrier = pltpu.get_barrier_semaphore()
        for pid, _ in peers:
            pl.semaphore_signal(barrier, inc=1, device_id=pid, device_id_type=pl.DeviceIdType.MESH)
        pl.semaphore_wait(barrier, N_DEV - 1)
        for cp in _split_copies(src, land, send_sems, recv_sems, scatter_flags, me, peers, False):
            cp.start()
        token[...] = jnp.zeros_like(token)

    outs = pl.pallas_call(
        body, name=name,
        out_shape=(*[pltpu.SemaphoreType.DMA(())] * (2 * ns), *[pltpu.HBM(s.shape, s.dtype) for s in srcs],
                   *[pltpu.HBM(shp, s.dtype) for shp, s in zip(land_shapes, srcs)],
                   jax.ShapeDtypeStruct((SUBLANES, LANES), F32)),
        in_specs=[hbm] * (2 * n) + [pl.BlockSpec(memory_space=pl.ANY)],
        out_specs=(*[sem] * (2 * ns), *[hbm] * (2 * n), pl.BlockSpec(memory_space=pltpu.VMEM)),
        input_output_aliases={i: 2 * ns + i for i in range(2 * n)},
        compiler_params=pltpu.CompilerParams(has_side_effects=pltpu.SideEffectType.DATAFLOW_SIDE_EFFECTING,
                                             collective_id=collective_id),
    )(*[pltpu.with_memory_space_constraint(s, pltpu.HBM) for s in srcs],
      *[pltpu.with_memory_space_constraint(lax.empty(shp, s.dtype), pltpu.HBM) for shp, s in zip(land_shapes, srcs)],
      after)
    return (outs[:ns], outs[ns:2 * ns], outs[2 * ns:2 * ns + n], outs[2 * ns + n:2 * ns + 2 * n], outs[2 * ns + 2 * n])


def _exchange_wait(send_sems, recv_sems, srcs, lands, scatter_flags, after, name):
    n = len(srcs)
    ns = n * (N_DEV - 1)
    hbm = pl.BlockSpec(memory_space=pltpu.HBM)
    sem = pl.BlockSpec(memory_space=pltpu.SEMAPHORE)

    def body(*refs):
        src, land = refs[:n], refs[n:2 * n]
        s_sems = refs[2 * n:2 * n + ns]
        r_sems = refs[2 * n + ns:2 * n + 2 * ns]
        me, peers = _peer_list()
        for cp in _split_copies(src, land, s_sems, r_sems, scatter_flags, me, peers, True):
            cp.wait_send()
            cp.wait_recv()

    outs = pl.pallas_call(
        body, name=name,
        out_shape=tuple(pltpu.HBM(a.shape, a.dtype) for a in (*srcs, *lands)),
        in_specs=[hbm] * (2 * n) + [sem] * (2 * ns) + [pl.BlockSpec(memory_space=pl.ANY)],
        out_specs=tuple([hbm] * (2 * n)),
        input_output_aliases={i: i for i in range(2 * n)},
        compiler_params=pltpu.CompilerParams(has_side_effects=pltpu.SideEffectType.DATAFLOW_SIDE_EFFECTING),
    )(*srcs, *lands, *send_sems, *recv_sems, after)
    return outs[:n], outs[n:]


def _tn_matmul(a, b, name, tk, tm, out_rows=None, out_cols=None, tn=512):
    n_tok, k_dim = a.shape
    m_dim = b.shape[1]
    grid = (k_dim // tk, m_dim // tm, n_tok // tn)

    def body(a_ref, b_ref, o_ref):
        @pl.when(pl.program_id(2) == 0)
        def _():
            o_ref[...] = jnp.zeros_like(o_ref)
        o_ref[...] += _dot_tn(a_ref[...].astype(BF16), b_ref[...].astype(BF16))

    return pl.pallas_call(
        body, name=name, grid=grid,
        in_specs=[pl.BlockSpec((tn, tk), lambda i, j, k: (k, i)), pl.BlockSpec((tn, tm), lambda i, j, k: (k, j))],
        out_specs=pl.BlockSpec((tk, tm), lambda i, j, k: (i, j)),
        out_shape=jax.ShapeDtypeStruct((out_rows or k_dim, out_cols or m_dim), F32),
        compiler_params=_cparams("parallel", "parallel", "arbitrary"),
    )(a, b)


def _adam_math(g, w, m, v):
    m = ADAM_B1 * m + (1.0 - ADAM_B1) * g
    v = ADAM_B2 * v + (1.0 - ADAM_B2) * (g * g)
    m_hat = m / (1.0 - ADAM_B1 ** ADAM_STEP)
    v_hat = v / (1.0 - ADAM_B2 ** ADAM_STEP)
    delta = -ADAM_LR * (m_hat / (jnp.sqrt(v_hat) + ADAM_EPS) + ADAM_WD * w)
    return delta, m, v


def _adam_sharded(land, own, w, m, v, name, tr):
    _, r, c = w.shape

    def body(*refs):
        l_ref = refs[0]
        own_ref = refs[1] if own is not None else None
        w_ref, m_ref, v_ref, g_ref, d_ref, nm_ref, nv_ref = [ref.at[0] for ref in refs[-7:]]
        if own_ref is not None:
            x, y, z = lax.axis_index("x"), lax.axis_index("y"), lax.axis_index("c")
            me = 4 * x + 2 * y + z
            mine = own_ref[...].astype(F32)
        g = None
        for s in range(N_DEV):
            part = l_ref[s].astype(F32)
            if own_ref is not None:
                part = jnp.where(me == s, mine, part)
            g = part if g is None else g + part
        d, nm, nv = _adam_math(g, w_ref[...], m_ref[...], v_ref[...])
        g_ref[...] = g
        d_ref[...] = d
        nm_ref[...] = nm
        nv_ref[...] = nv

    spec = pl.BlockSpec((1, tr, c), lambda i: (0, i, 0))
    own_specs, own_args = ([pl.BlockSpec((tr, c), lambda i: (i, 0))], [own]) if own is not None else ([], [])
    return pl.pallas_call(
        body, name=name, grid=(r // tr,),
        in_specs=[pl.BlockSpec((N_DEV, tr, c), lambda i: (0, i, 0)), *own_specs, spec, spec, spec],
        out_specs=(spec, spec, spec, spec),
        out_shape=tuple(jax.ShapeDtypeStruct((1, r, c), F32) for _ in range(4)),
        compiler_params=_cparams("parallel"),
    )(land, *own_args, w, m, v)


def _sum_partials(parts, own, name):
    _, r, c = parts.shape

    def body(*refs):
        p_ref, o_ref = refs[0], refs[-1]
        if own is not None:
            x, y, z = lax.axis_index("x"), lax.axis_index("y"), lax.axis_index("c")
            me = 4 * x + 2 * y + z
            mine = refs[1][...]
        g = None
        for s in range(N_DEV):
            part = p_ref[s]
            if own is not None:
                part = jnp.where(me == s, mine, part)
            g = part if g is None else g + part
        o_ref[...] = g

    args = (parts,) if own is None else (parts, own)
    return pl.pallas_call(body, name=name, out_shape=jax.ShapeDtypeStruct((r, c), F32),
                          compiler_params=pltpu.CompilerParams(vmem_limit_bytes=VMEM_LIMIT))(*args)


def _adam_replicated(gs, ws, ms, vs, name):
    k = len(ws)

    def body(*refs):
        outs = refs[4 * k:]
        for i in range(k):
            d, nm, nv = _adam_math(refs[i][...], refs[k + i][...], refs[2 * k + i][...], refs[3 * k + i][...])
            outs[i][...] = d
            outs[k + i][...] = nm
            outs[2 * k + i][...] = nv

    outs = pl.pallas_call(body, name=name, out_shape=tuple(jax.ShapeDtypeStruct(w.shape, F32) for w in ws) * 3,
                          compiler_params=pltpu.CompilerParams(vmem_limit_bytes=VMEM_LIMIT))(*gs, *ws, *ms, *vs)
    return outs[:k], outs[k:2 * k], outs[2 * k:]


def _inproj_fwd(x, g, w_in, avg, qg, kg, tm=512):
    n = x.shape[0]

    def body(x_ref, g_ref, w_ref, a_ref, qg_ref, kg_ref, hn_ref, u_ref, qkv_ref, raw_ref, fl_ref):
        hn = _rms(x_ref[...], g_ref[...]).astype(BF16)
        hn_ref[...] = hn
        proj = _dot(hn, w_ref[...])
        u_ref[...] = proj[:, 0:512]
        q = proj[:, 512:1024]
        k = proj[:, 1024:1536]
        raw_ref[:, 0:512] = q
        raw_ref[:, 512:1024] = k
        qkv_ref[:, 0:512] = _headnorm(q, a_ref[...], qg_ref[...])
        qkv_ref[:, 512:1024] = _headnorm(k, a_ref[...], kg_ref[...])
        qkv_ref[:, 1024:1536] = proj[:, 1536:2048]
        fl_ref[...] = proj[:, 2048:D_IN_PAD]

    row = lambda w: pl.BlockSpec((tm, w), lambda i: (i, 0))
    full = lambda a: pl.BlockSpec(a.shape, lambda i: (0,) * a.ndim)
    return pl.pallas_call(
        body, name="inproj_fwd", grid=(n // tm,),
        in_specs=[row(D_MODEL), full(g), full(w_in), full(avg), full(qg), full(kg)],
        out_specs=(row(D_MODEL), row(512), row(1536), row(1024), row(LANES)),
        out_shape=(jax.ShapeDtypeStruct((n, D_MODEL), BF16), jax.ShapeDtypeStruct((n, 512), F32),
                   jax.ShapeDtypeStruct((n, 1536), F32), jax.ShapeDtypeStruct((n, 1024), F32),
                   jax.ShapeDtypeStruct((n, LANES), F32)),
        compiler_params=_cparams("parallel"),
    )(x, g, w_in, avg, qg, kg)


def _inproj_bwd(x, g, w_in, avg, qg, kg, raw, du, dqn, dkn, dv, dfl, dres, tm=512):
    n = x.shape[0]

    def body(x_ref, g_ref, w_ref, a_ref, qg_ref, kg_ref, raw_ref, du_ref, dqn_ref, dkn_ref, dv_ref, dfl_ref, dres_ref,
             dx_ref, dproj_ref, dg_ref, dqg_ref, dkg_ref):
        @pl.when(pl.program_id(0) == 0)
        def _():
            dg_ref[...] = jnp.zeros_like(dg_ref)
            dqg_ref[...] = jnp.zeros_like(dqg_ref)
            dkg_ref[...] = jnp.zeros_like(dkg_ref)
        avg_m = a_ref[...]
        _, vjp_q = jax.vjp(lambda q, gg: _headnorm(q, avg_m, gg), raw_ref[:, 0:512], qg_ref[...])
        dq, dqg = vjp_q(dqn_ref[...])
        _, vjp_k = jax.vjp(lambda k, gg: _headnorm(k, avg_m, gg), raw_ref[:, 512:1024], kg_ref[...])
        dk, dkg = vjp_k(dkn_ref[...])
        dproj = jnp.concatenate([du_ref[...], dq, dk, dv_ref[...], dfl_ref[...]], axis=1).astype(BF16)
        dproj_ref[...] = dproj
        dhn = _dot_nt(dproj, w_ref[...])
        _, vjp_x = jax.vjp(_rms, x_ref[...], g_ref[...])
        dxn, dg = vjp_x(dhn)
        dx_ref[...] = dxn + dres_ref[...]
        dg_ref[...] += dg
        dqg_ref[...] += dqg
        dkg_ref[...] += dkg

    row = lambda w: pl.BlockSpec((tm, w), lambda i: (i, 0))
    full = lambda a: pl.BlockSpec(a.shape, lambda i: (0,) * a.ndim)
    vec = lambda w: pl.BlockSpec((1, w), lambda i: (0, 0))
    return pl.pallas_call(
        body, name="inproj_bwd", grid=(n // tm,),
        in_specs=[row(D_MODEL), full(g), full(w_in), full(avg), full(qg), full(kg), row(1024), row(512), row(512),
                  row(512), row(512), row(LANES), row(D_MODEL)],
        out_specs=(row(D_MODEL), row(D_IN_PAD), vec(D_MODEL), vec(512), vec(512)),
        out_shape=(jax.ShapeDtypeStruct((n, D_MODEL), F32), jax.ShapeDtypeStruct((n, D_IN_PAD), BF16),
                   jax.ShapeDtypeStruct((1, D_MODEL), F32), jax.ShapeDtypeStruct((1, 512), F32),
                   jax.ShapeDtypeStruct((1, 512), F32)),
        compiler_params=_cparams("arbitrary"),
    )(x, g, w_in, avg, qg, kg, raw, du, dqn, dkn, dv, dfl, dres)


def _glu_fwd(yc, wg, bg, tm=512):
    n = yc.shape[0]

    def body(yc_ref, w_ref, b_ref, ys_ref):
        gl = jax.nn.gelu(yc_ref[...])
        z = _dot(gl.astype(BF16), w_ref[...]) + b_ref[...]
        ys_ref[...] = gl * jax.nn.sigmoid(z)

    row = pl.BlockSpec((tm, 512), lambda i: (i, 0))
    full = lambda a: pl.BlockSpec(a.shape, lambda i: (0,) * a.ndim)
    return pl.pallas_call(
        body, name="glu_fwd", grid=(n // tm,), in_specs=[row, full(wg), full(bg)], out_specs=row,
        out_shape=jax.ShapeDtypeStruct((n, 512), F32), compiler_params=_cparams("parallel"),
    )(yc, wg, bg)


def _glu_bwd(yc, dys, wg, bg, tm=512):
    n = yc.shape[0]

    def body(yc_ref, dys_ref, w_ref, b_ref, dyc_ref, gl_ref, dz_ref, db_ref):
        @pl.when(pl.program_id(0) == 0)
        def _():
            db_ref[...] = jnp.zeros_like(db_ref)
        gl, vjp_gelu = jax.vjp(jax.nn.gelu, yc_ref[...])
        glb = gl.astype(BF16)
        z = _dot(glb, w_ref[...]) + b_ref[...]
        s = jax.nn.sigmoid(z)
        dys = dys_ref[...]
        dz = dys * gl * s * (1.0 - s)
        dzb = dz.astype(BF16)
        dgl = dys * s + _dot_nt(dzb, w_ref[...])
        dyc_ref[...] = vjp_gelu(dgl)[0]
        gl_ref[...] = glb
        dz_ref[...] = dzb
        db_ref[...] += jnp.sum(dz, axis=0, keepdims=True)

    row = pl.BlockSpec((tm, 512), lambda i: (i, 0))
    full = lambda a: pl.BlockSpec(a.shape, lambda i: (0,) * a.ndim)
    return pl.pallas_call(
        body, name="glu_bwd", grid=(n // tm,), in_specs=[row, row, full(wg), full(bg)],
        out_specs=(row, row, row, pl.BlockSpec((1, 512), lambda i: (0, 0))),
        out_shape=(jax.ShapeDtypeStruct((n, 512), F32), jax.ShapeDtypeStruct((n, 512), BF16),
                   jax.ShapeDtypeStruct((n, 512), BF16), jax.ShapeDtypeStruct((1, 512), F32)),
        compiler_params=_cparams("arbitrary"),
    )(yc, dys, wg, bg)


def _mix_fwd(x, ys, ya, gs, ga, wout, gf, tm=512):
    n = x.shape[0]

    def body(x_ref, ys_ref, ya_ref, gs_ref, ga_ref, w_ref, gf_ref, h1_ref, hn2_ref, mixed_ref):
        mixed = jnp.concatenate([_rms(ys_ref[...], gs_ref[...]), _rms(ya_ref[...], ga_ref[...])], axis=1).astype(BF16)
        mixed_ref[...] = mixed
        h1 = x_ref[...] + _dot(mixed, w_ref[...])
        h1_ref[...] = h1
        hn2_ref[...] = _rms(h1, gf_ref[...]).astype(BF16)

    row = lambda w: pl.BlockSpec((tm, w), lambda i: (i, 0))
    full = lambda a: pl.BlockSpec(a.shape, lambda i: (0,) * a.ndim)
    return pl.pallas_call(
        body, name="mix_fwd", grid=(n // tm,),
        in_specs=[row(D_MODEL), row(512), row(512), full(gs), full(ga), full(wout), full(gf)],
        out_specs=(row(D_MODEL), row(D_MODEL), row(D_MODEL)),
        out_shape=(jax.ShapeDtypeStruct((n, D_MODEL), F32), jax.ShapeDtypeStruct((n, D_MODEL), BF16),
                   jax.ShapeDtypeStruct((n, D_MODEL), BF16)),
        compiler_params=_cparams("parallel"),
    )(x, ys, ya, gs, ga, wout, gf)


def _mix_bwd(dy, dhn2_parts, h1, ys, ya, gs, ga, wout, gf, tm=512):
    n = dy.shape[0]
    n_parts = dhn2_parts.shape[0]

    def body(dy_ref, dp_ref, h1_ref, ys_ref, ya_ref, gs_ref, ga_ref, w_ref, gf_ref,
             dh1_ref, dys_ref, dya_ref, dgs_ref, dga_ref, dgf_ref):
        @pl.when(pl.program_id(0) == 0)
        def _():
            dgs_ref[...] = jnp.zeros_like(dgs_ref)
            dga_ref[...] = jnp.zeros_like(dga_ref)
            dgf_ref[...] = jnp.zeros_like(dgf_ref)
        dhn2 = dp_ref[0]
        for p in range(1, n_parts):
            dhn2 = dhn2 + dp_ref[p]
        _, vjp_f = jax.vjp(_rms, h1_ref[...], gf_ref[...])
        dh1n, dgf = vjp_f(dhn2)
        dh1 = dy_ref[...] + dh1n
        dh1_ref[...] = dh1
        dmixed = _dot_nt(dh1.astype(BF16), w_ref[...])
        _, vjp_s = jax.vjp(_rms, ys_ref[...], gs_ref[...])
        dys, dgs = vjp_s(dmixed[:, 0:512])
        _, vjp_a = jax.vjp(_rms, ya_ref[...], ga_ref[...])
        dya, dga = vjp_a(dmixed[:, 512:1024])
        dys_ref[...] = dys
        dya_ref[...] = dya
        dgs_ref[...] += dgs
        dga_ref[...] += dga
        dgf_ref[...] += dgf

    row = lambda w: pl.BlockSpec((tm, w), lambda i: (i, 0))
    full = lambda a: pl.BlockSpec(a.shape, lambda i: (0,) * a.ndim)
    vec = lambda w: pl.BlockSpec((1, w), lambda i: (0, 0))
    return pl.pallas_call(
        body, name="mix_bwd", grid=(n // tm,),
        in_specs=[row(D_MODEL), pl.BlockSpec((n_parts, tm, D_MODEL), lambda i: (0, i, 0)), row(D_MODEL), row(512),
                  row(512), full(gs), full(ga), full(wout), full(gf)],
        out_specs=(row(D_MODEL), row(512), row(512), vec(512), vec(512), vec(D_MODEL)),
        out_shape=(jax.ShapeDtypeStruct((n, D_MODEL), F32), jax.ShapeDtypeStruct((n, 512), F32),
                   jax.ShapeDtypeStruct((n, 512), F32), jax.ShapeDtypeStruct((1, 512), F32),
                   jax.ShapeDtypeStruct((1, 512), F32), jax.ShapeDtypeStruct((1, D_MODEL), F32)),
        compiler_params=_cparams("arbitrary"),
    )(dy, dhn2_parts, h1, ys, ya, gs, ga, wout, gf)


HALO = 16


def _conv3(ue, cw):
    return cw[2:3] * ue + cw[1:2] * pltpu.roll(ue, 1, 0) + cw[0:1] * pltpu.roll(ue, 2, 0) + cw[3:4]


def _ffn_fwd(hn2, h1, target, wup_g, wup_v, cw_g, cw_v, wdown, seq_len, tm=512, fb=256):
    n = hn2.shape[0]
    nj = D_FF_PAD // fb
    hb = tm // HALO

    def body(hn_ref, halo_ref, h1_ref, tgt_ref, wg_ref, wv_ref, cg_ref, cv_ref, wd_ref,
             ug_ref, uv_ref, dy_ref, loss_ref, acc):
        i, j = pl.program_id(0), pl.program_id(1)
        seq_start = (i * tm) % seq_len == 0
        halo = halo_ref[...]
        halo = jnp.where(seq_start, jnp.zeros_like(halo), halo)
        he = jnp.concatenate([halo, hn_ref[...]], axis=0)
        ueg = _dot(he, wg_ref[...])
        uev = _dot(he, wv_ref[...])
        ug_ref[...] = ueg[HALO:].astype(BF16)
        uv_ref[...] = uev[HALO:].astype(BF16)
        cg = _conv3(ueg, cg_ref[...])[HALO:]
        cv = _conv3(uev, cv_ref[...])[HALO:]
        act = (jax.nn.silu(cg) * cv).astype(BF16)
        part = _dot(act, wd_ref[...])

        @pl.when(j == 0)
        def _():
            acc[...] = part

        @pl.when(j > 0)
        def _():
            acc[...] += part

        @pl.when(j == nj - 1)
        def _():
            err = h1_ref[...] + acc[...] - tgt_ref[...]
            dy_ref[...] = err * (1.0 / D_MODEL)
            loss_ref[0] = jnp.sum(err * err, axis=0, keepdims=True)

    row = pl.BlockSpec((tm, D_MODEL), lambda i, j: (i, 0))
    return pl.pallas_call(
        body, name="ffn_fwd", grid=(n // tm, nj),
        in_specs=[row, pl.BlockSpec((HALO, D_MODEL), lambda i, j: (jnp.maximum(i * hb - 1, 0), 0)), row, row,
                  pl.BlockSpec((D_MODEL, fb), lambda i, j: (0, j)), pl.BlockSpec((D_MODEL, fb), lambda i, j: (0, j)),
                  pl.BlockSpec((8, fb), lambda i, j: (0, j)), pl.BlockSpec((8, fb), lambda i, j: (0, j)),
                  pl.BlockSpec((fb, D_MODEL), lambda i, j: (j, 0))],
        out_specs=(pl.BlockSpec((tm, fb), lambda i, j: (i, j)), pl.BlockSpec((tm, fb), lambda i, j: (i, j)), row,
                   pl.BlockSpec((1, 1, D_MODEL), lambda i, j: (i, 0, 0))),
        out_shape=(jax.ShapeDtypeStruct((n, D_FF_PAD), BF16), jax.ShapeDtypeStruct((n, D_FF_PAD), BF16),
                   jax.ShapeDtypeStruct((n, D_MODEL), F32), jax.ShapeDtypeStruct((n // tm, 1, D_MODEL), F32)),
        scratch_shapes=[pltpu.VMEM((tm, D_MODEL), F32)],
        compiler_params=_cparams("parallel", "arbitrary"),
    )(hn2, hn2, h1, target, wup_g, wup_v, cw_g, cw_v, wdown)


def _ffn_bwd(dy, ug, uv, wup_g, wup_v, cw_g, cw_v, wdown, seq_len, tm=512, fb=256):
    n = dy.shape[0]
    nj = D_FF_PAD // fb
    hb = tm // HALO
    last_hb = n // HALO - 1
    rows = tm + HALO

    def body(dy_ref, dyn_ref, ugp_ref, ugm_ref, ugn_ref, uvp_ref, uvm_ref, uvn_ref, wg_ref, wv_ref, cg_ref, cv_ref,
             wd_ref, dug_ref, duv_ref, act_ref, dhn_ref, dcg_ref, dcv_ref, acc):
        i, j = pl.program_id(0), pl.program_id(1)
        seq_start = (i * tm) % seq_len == 0
        seq_end = ((i + 1) * tm) % seq_len == 0
        dyn = dyn_ref[...]
        dyn = jnp.where(seq_end, jnp.zeros_like(dyn), dyn)
        d_out = jnp.concatenate([dy_ref[...], dyn], axis=0).astype(BF16)
        d_act = _dot_nt(d_out, wd_ref[...])

        def pre_act(up_ref, um_ref, un_ref, cw):
            up = up_ref[...]
            up = jnp.where(seq_start, jnp.zeros_like(up), up)
            ue = jnp.concatenate([up, um_ref[...], un_ref[...]], axis=0).astype(F32)
            return ue, _conv3(ue, cw)[HALO:]

        cwg, cwv = cg_ref[...], cv_ref[...]
        ueg, cge = pre_act(ugp_ref, ugm_ref, ugn_ref, cwg)
        uev, cve = pre_act(uvp_ref, uvm_ref, uvn_ref, cwv)
        act, vjp_act = jax.vjp(lambda g, v: jax.nn.silu(g) * v, cge, cve)
        dcge, dcve = vjp_act(d_act)
        act_ref[...] = act[:tm].astype(BF16)

        def conv_t(dc, cw):
            return (cw[2:3] * dc + cw[1:2] * pltpu.roll(dc, rows - 1, 0) + cw[0:1] * pltpu.roll(dc, rows - 2, 0))[:tm]

        dug = conv_t(dcge, cwg).astype(BF16)
        duv = conv_t(dcve, cwv).astype(BF16)
        dug_ref[...] = dug
        duv_ref[...] = duv
        part = _dot_nt(dug, wg_ref[...]) + _dot_nt(duv, wv_ref[...])

        @pl.when(j == 0)
        def _():
            acc[...] = part

        @pl.when(j > 0)
        def _():
            acc[...] += part

        @pl.when(j == nj - 1)
        def _():
            dhn_ref[...] = acc[...]

        def cw_grad(dc, ue):
            dcm = dc[:tm]
            taps = [jnp.sum(dcm * pltpu.roll(ue, 2 - k, 0)[HALO:HALO + tm], axis=0, keepdims=True) for k in (0, 1)]
            taps.append(jnp.sum(dcm * ue[HALO:HALO + tm], axis=0, keepdims=True))
            taps.append(jnp.sum(dcm, axis=0, keepdims=True))
            return jnp.concatenate(taps + [jnp.zeros((4, fb), F32)], axis=0)

        @pl.when(i == 0)
        def _():
            dcg_ref[j] = jnp.zeros((8, fb), F32)
            dcv_ref[j] = jnp.zeros((8, fb), F32)

        dcg_ref[j] += cw_grad(dcge, ueg)
        dcv_ref[j] += cw_grad(dcve, uev)

    row = pl.BlockSpec((tm, D_MODEL), lambda i, j: (i, 0))
    u_prev = pl.BlockSpec((HALO, fb), lambda i, j: (jnp.maximum(i * hb - 1, 0), j))
    u_main = pl.BlockSpec((tm, fb), lambda i, j: (i, j))
    u_next = pl.BlockSpec((HALO, fb), lambda i, j: (jnp.minimum((i + 1) * hb, last_hb), j))
    w_col = pl.BlockSpec((D_MODEL, fb), lambda i, j: (0, j))
    c_col = pl.BlockSpec((8, fb), lambda i, j: (0, j))
    dc_spec = pl.BlockSpec((nj, 8, fb), lambda i, j: (0, 0, 0))
    return pl.pallas_call(
        body, name="ffn_bwd", grid=(n // tm, nj),
        in_specs=[row, pl.BlockSpec((HALO, D_MODEL), lambda i, j: (jnp.minimum((i + 1) * hb, last_hb), 0)),
                  u_prev, u_main, u_next, u_prev, u_main, u_next, w_col, w_col, c_col, c_col,
                  pl.BlockSpec((fb, D_MODEL), lambda i, j: (j, 0))],
        out_specs=(u_main, u_main, u_main, row, dc_spec, dc_spec),
        out_shape=(jax.ShapeDtypeStruct((n, D_FF_PAD), BF16), jax.ShapeDtypeStruct((n, D_FF_PAD), BF16),
                   jax.ShapeDtypeStruct((n, D_FF_PAD), BF16), jax.ShapeDtypeStruct((n, D_MODEL), F32),
                   jax.ShapeDtypeStruct((nj, 8, fb), F32), jax.ShapeDtypeStruct((nj, 8, fb), F32)),
        scratch_shapes=[pltpu.VMEM((tm, D_MODEL), F32)],
        compiler_params=_cparams("arbitrary", "arbitrary"),
    )(dy, dy, ug, ug, ug, uv, uv, uv, wup_g, wup_v, cw_g, cw_v, wdown)


def _s5_param_fn(lr, li, ldt, br, bi):
    dt = jnp.exp(ldt)
    mag = jnp.exp(lr * dt)
    ab_re = mag * jnp.cos(li * dt)
    ab_im = mag * jnp.sin(li * dt)
    nr = ab_re - 1.0
    ni = ab_im
    den = lr * lr + li * li
    q_re = (nr * lr + ni * li) / den
    q_im = (ni * lr - nr * li) / den
    bb_re = q_re * br - q_im * bi
    bb_im = q_re * bi + q_im * br
    return ab_re, ab_im, bb_re, bb_im


def _s5_param_fwd(lr, li, ldt, br, bi):
    def body(lr_ref, li_ref, ldt_ref, br_ref, bi_ref, ar_ref, ai_ref, bbr_ref, bbi_ref):
        ar, ai, bbr, bbi = _s5_param_fn(lr_ref[...], li_ref[...], ldt_ref[...], br_ref[...], bi_ref[...])
        ar_ref[...] = ar
        ai_ref[...] = ai
        bbr_ref[...] = bbr
        bbi_ref[...] = bbi

    return pl.pallas_call(
        body, name="s5_param_fwd",
        out_shape=(jax.ShapeDtypeStruct(lr.shape, F32), jax.ShapeDtypeStruct(lr.shape, F32),
                   jax.ShapeDtypeStruct(br.shape, F32), jax.ShapeDtypeStruct(br.shape, F32)),
    )(lr, li, ldt, br, bi)


def _s5_param_bwd(lr, li, ldt, br, bi, dar, dai, dbbr, dbbi):
    def body(lr_ref, li_ref, ldt_ref, br_ref, bi_ref, dar_ref, dai_ref, dbbr_ref, dbbi_ref,
             dlr_ref, dli_ref, dldt_ref, dbr_ref, dbi_ref):
        _, vjp = jax.vjp(_s5_param_fn, lr_ref[...], li_ref[...], ldt_ref[...], br_ref[...], bi_ref[...])
        dlr, dli, dldt, dbr, dbi = vjp((dar_ref[...], dai_ref[...], dbbr_ref[...], dbbi_ref[...]))
        dlr_ref[...] = dlr
        dli_ref[...] = dli
        dldt_ref[...] = dldt
        dbr_ref[...] = dbr
        dbi_ref[...] = dbi

    return pl.pallas_call(
        body, name="s5_param_bwd",
        out_shape=(jax.ShapeDtypeStruct(lr.shape, F32), jax.ShapeDtypeStruct(lr.shape, F32),
                   jax.ShapeDtypeStruct(ldt.shape, F32), jax.ShapeDtypeStruct(br.shape, F32),
                   jax.ShapeDtypeStruct(br.shape, F32)),
    )(lr, li, ldt, br, bi, dar, dai, dbbr, dbbi)


S5_CHUNK = 256
S5_STATES = 512
S5_BLOCKS = 4


def _cpow_rows(ar, ai, count):
    rs, im = [ar], [ai]
    for _ in range(count - 1):
        pr, pi = rs[-1], im[-1]
        rs.append(pr * ar - pi * ai)
        im.append(pr * ai + pi * ar)
    return rs, im


def _scan_in_groups(vr, vi, pr, pi, rm, reverse):
    n, width = vr.shape
    vr = vr.reshape(n // SUBLANES, SUBLANES, width)
    vi = vi.reshape(n // SUBLANES, SUBLANES, width)
    row = rm[0:SUBLANES]
    for k in (1, 2, 4):
        shift = SUBLANES - k if reverse else k
        keep = row < SUBLANES - k if reverse else row >= k
        kr = jnp.where(keep, pr[k - 1], 0.0)
        ki = jnp.where(keep, pi[k - 1], 0.0)
        sr, si = pltpu.roll(vr, shift, 1), pltpu.roll(vi, shift, 1)
        vr, vi = vr + kr * sr - ki * si, vi + kr * si + ki * sr
    return vr.reshape(n, width), vi.reshape(n, width)


def _carry_over_groups(xr_s, xi_s, wr, wi, c0r, c0i, reverse):
    groups = xr_s.shape[0] // SUBLANES
    pick = 0 if reverse else SUBLANES - 1

    def step(q, carry):
        cr, ci = carry
        r = groups - 1 - q if reverse else q
        o = pl.multiple_of(r * SUBLANES, SUBLANES)
        vr = xr_s[pl.ds(o, SUBLANES), :]
        vi = xi_s[pl.ds(o, SUBLANES), :]
        nr = vr + wr * cr - wi * ci
        ni = vi + wr * ci + wi * cr
        xr_s[pl.ds(o, SUBLANES), :] = nr
        xi_s[pl.ds(o, SUBLANES), :] = ni
        return (jnp.broadcast_to(nr[pick:pick + 1], nr.shape), jnp.broadcast_to(ni[pick:pick + 1], ni.shape))

    return lax.fori_loop(0, groups, step, (c0r, c0i))


def _s5_state_scan(u_b, bbr, bbi, pr, pi, rm, xr_s, xi_s, c0r, c0i):
    bur = _dot(u_b, bbr)
    bui = _dot(u_b, bbi)
    bur, bui = _scan_in_groups(bur, bui, pr, pi, rm, False)
    xr_s[...] = bur
    xi_s[...] = bui
    w8r = jnp.concatenate(pr, axis=0)
    w8i = jnp.concatenate(pi, axis=0)
    return _carry_over_groups(xr_s, xi_s, w8r, w8i, c0r, c0i, False)


def _s5_fwd(u, a_re, a_im, bbr, bbi, cr, ci, d_skip, n_seq):
    n = u.shape[0]
    seq_len = n // n_seq
    nt = seq_len // S5_CHUNK
    tc = S5_CHUNK

    def body(u_ref, ar_ref, ai_ref, bbr_ref, bbi_ref, cr_ref, ci_ref, d_ref, y_ref, str_ref, sti_ref,
             xr_s, xi_s, car_r, car_i):
        t = pl.program_id(2)

        @pl.when(t == 0)
        def _():
            car_r[...] = jnp.zeros_like(car_r)
            car_i[...] = jnp.zeros_like(car_i)
        pr, pi = _cpow_rows(ar_ref[0], ai_ref[0], SUBLANES)
        rm = lax.broadcasted_iota(jnp.int32, (tc, S5_STATES), 0) & (SUBLANES - 1)
        str_ref[0, 0] = car_r[...]
        sti_ref[0, 0] = car_i[...]
        u_t = u_ref[...]
        cfr, cfi = _s5_state_scan(u_t.astype(BF16), bbr_ref[0], bbi_ref[0], pr, pi, rm, xr_s, xi_s,
                                  car_r[...], car_i[...])
        car_r[...] = cfr
        car_i[...] = cfi
        y = _dot(xr_s[...].astype(BF16), cr_ref[0]) - _dot(xi_s[...].astype(BF16), ci_ref[0])
        y_ref[...] = y + d_ref[...] * u_t

    u_spec = pl.BlockSpec((tc, LANES), lambda cb, b, t: (b * nt + t, cb))
    a_spec = pl.BlockSpec((1, 1, S5_STATES), lambda cb, b, t: (cb, 0, 0))
    bb_spec = pl.BlockSpec((1, LANES, S5_STATES), lambda cb, b, t: (cb, 0, 0))
    c_spec = pl.BlockSpec((1, S5_STATES, LANES), lambda cb, b, t: (cb, 0, 0))
    st_spec = pl.BlockSpec((1, 1, SUBLANES, S5_STATES), lambda cb, b, t: (cb, b * nt + t, 0, 0))
    st_shape = jax.ShapeDtypeStruct((S5_BLOCKS, n_seq * nt, SUBLANES, S5_STATES), F32)
    return pl.pallas_call(
        body, name="s5_fwd", grid=(S5_BLOCKS, n_seq, nt),
        in_specs=[u_spec, a_spec, a_spec, bb_spec, bb_spec, c_spec, c_spec,
                  pl.BlockSpec((1, LANES), lambda cb, b, t: (0, cb))],
        out_specs=(u_spec, st_spec, st_spec),
        out_shape=(jax.ShapeDtypeStruct((n, D_SSM), F32), st_shape, st_shape),
        scratch_shapes=[pltpu.VMEM((tc, S5_STATES), F32), pltpu.VMEM((tc, S5_STATES), F32),
                        pltpu.VMEM((SUBLANES, S5_STATES), F32), pltpu.VMEM((SUBLANES, S5_STATES), F32)],
        compiler_params=_cparams("parallel", "arbitrary", "arbitrary"),
    )(u, a_re, a_im, bbr, bbi, cr, ci, d_skip)


def _s5_bwd(u, dy, st_r, st_i, a_re, a_im, bbr, bbi, cr, ci, d_skip, n_seq):
    n = u.shape[0]
    seq_len = n // n_seq
    nt = seq_len // S5_CHUNK
    tc = S5_CHUNK

    def body(u_ref, dy_ref, str_ref, sti_ref, ar_ref, ai_ref, bbr_ref, bbi_ref, cr_ref, ci_ref, d_ref,
             du_ref, dbbr_ref, dbbi_ref, dcr_ref, dci_ref, dar_ref, dai_ref, dd_ref,
             xr_s, xi_s, gr_s, gi_s, car_r, car_i):
        b, t = pl.program_id(1), pl.program_id(2)

        @pl.when((b == 0) & (t == 0))
        def _():
            for ref in (dbbr_ref, dbbi_ref, dcr_ref, dci_ref, dar_ref, dai_ref, dd_ref):
                ref[...] = jnp.zeros_like(ref)

        @pl.when(t == 0)
        def _():
            car_r[...] = jnp.zeros_like(car_r)
            car_i[...] = jnp.zeros_like(car_i)
        ar, ai = ar_ref[0], ai_ref[0]
        pr, pi = _cpow_rows(ar, ai, SUBLANES)
        row = lax.broadcasted_iota(jnp.int32, (tc, S5_STATES), 0)
        rm = row & (SUBLANES - 1)
        u_t = u_ref[...]
        u_b = u_t.astype(BF16)
        dy_t = dy_ref[...]
        dy_b = dy_t.astype(BF16)
        s0r, s0i = str_ref[0, 0], sti_ref[0, 0]
        _s5_state_scan(u_b, bbr_ref[0], bbi_ref[0], pr, pi, rm, xr_s, xi_s, s0r, s0i)
        xr, xi = xr_s[...], xi_s[...]
        gr = _dot_nt(dy_b, cr_ref[0])
        gi = -_dot_nt(dy_b, ci_ref[0])
        npi = [-v for v in pi]
        gr, gi = _scan_in_groups(gr, gi, pr, npi, rm, True)
        gr_s[...] = gr
        gi_s[...] = gi
        w8r = jnp.concatenate(pr[::-1], axis=0)
        w8i = jnp.concatenate(npi[::-1], axis=0)
        cfr, cfi = _carry_over_groups(gr_s, gi_s, w8r, w8i, car_r[...], car_i[...], True)
        car_r[...] = cfr
        car_i[...] = cfi
        gr, gi = gr_s[...], gi_s[...]
        gr_b, gi_b = gr.astype(BF16), gi.astype(BF16)
        du_ref[...] = _dot_nt(gr_b, bbr_ref[0]) + _dot_nt(gi_b, bbi_ref[0]) + d_ref[...] * dy_t
        dbbr_ref[0] += _dot_tn(u_b, gr_b)
        dbbi_ref[0] += _dot_tn(u_b, gi_b)
        dcr_ref[0] += _dot_tn(xr.astype(BF16), dy_b)
        dci_ref[0] -= _dot_tn(xi.astype(BF16), dy_b)
        dd_ref[0] += jnp.sum((dy_t * u_t).reshape(tc // SUBLANES, SUBLANES, LANES), axis=0)
        first = row == 0
        xpr = jnp.where(first, jnp.broadcast_to(s0r[0:1], xr.shape), pltpu.roll(xr, 1, 0))
        xpi = jnp.where(first, jnp.broadcast_to(s0i[0:1], xi.shape), pltpu.roll(xi, 1, 0))
        shp = (tc // SUBLANES, SUBLANES, S5_STATES)
        dar_ref[0] += jnp.sum((gr * xpr + gi * xpi).reshape(shp), axis=0)
        dai_ref[0] += jnp.sum((gi * xpr - gr * xpi).reshape(shp), axis=0)

    u_spec = pl.BlockSpec((tc, LANES), lambda cb, b, t: (b * nt + nt - 1 - t, cb))
    a_spec = pl.BlockSpec((1, 1, S5_STATES), lambda cb, b, t: (cb, 0, 0))
    bb_spec = pl.BlockSpec((1, LANES, S5_STATES), lambda cb, b, t: (cb, 0, 0))
    c_spec = pl.BlockSpec((1, S5_STATES, LANES), lambda cb, b, t: (cb, 0, 0))
    st_spec = pl.BlockSpec((1, 1, SUBLANES, S5_STATES), lambda cb, b, t: (cb, b * nt + nt - 1 - t, 0, 0))
    da_spec = pl.BlockSpec((1, SUBLANES, S5_STATES), lambda cb, b, t: (cb, 0, 0))
    dd_spec = pl.BlockSpec((1, SUBLANES, LANES), lambda cb, b, t: (cb, 0, 0))
    big = pltpu.VMEM((tc, S5_STATES), F32)
    small = pltpu.VMEM((SUBLANES, S5_STATES), F32)
    return pl.pallas_call(
        body, name="s5_bwd", grid=(S5_BLOCKS, n_seq, nt),
        in_specs=[u_spec, u_spec, st_spec, st_spec, a_spec, a_spec, bb_spec, bb_spec, c_spec, c_spec,
                  pl.BlockSpec((1, LANES), lambda cb, b, t: (0, cb))],
        out_specs=(u_spec, bb_spec, bb_spec, c_spec, c_spec, da_spec, da_spec, dd_spec),
        out_shape=(jax.ShapeDtypeStruct((n, D_SSM), F32),
                   jax.ShapeDtypeStruct((S5_BLOCKS, LANES, S5_STATES), F32),
                   jax.ShapeDtypeStruct((S5_BLOCKS, LANES, S5_STATES), F32),
                   jax.ShapeDtypeStruct((S5_BLOCKS, S5_STATES, LANES), F32),
                   jax.ShapeDtypeStruct((S5_BLOCKS, S5_STATES, LANES), F32),
                   jax.ShapeDtypeStruct((S5_BLOCKS, SUBLANES, S5_STATES), F32),
                   jax.ShapeDtypeStruct((S5_BLOCKS, SUBLANES, S5_STATES), F32),
                   jax.ShapeDtypeStruct((S5_BLOCKS, SUBLANES, LANES), F32)),
        scratch_shapes=[big, big, big, big, small, small],
        compiler_params=_cparams("parallel", "arbitrary", "arbitrary"),
    )(u, dy, st_r, st_i, a_re, a_im, bbr, bbi, cr, ci, d_skip)


CUM_BLOCK = 128


def _tri(lower):
    r = lax.broadcasted_iota(jnp.int32, (CUM_BLOCK, CUM_BLOCK), 0)
    c = lax.broadcasted_iota(jnp.int32, (CUM_BLOCK, CUM_BLOCK), 1)
    return jnp.where(r >= c if lower else r <= c, 1.0, 0.0).astype(F32)


def _fprep_fwd(fl, bf, n_seq):
    n = fl.shape[0]
    seq_len = n // n_seq
    nb = seq_len // CUM_BLOCK

    def body(fl_ref, bf_ref, cum_ref):
        tril = _tri(True)
        carry = jnp.zeros((1, LANES), F32)
        for blk in range(nb):
            rows = slice(blk * CUM_BLOCK, (blk + 1) * CUM_BLOCK)
            lf = jax.nn.log_sigmoid(fl_ref[rows, :] + bf_ref[...])
            cs = jnp.dot(tril, lf, preferred_element_type=F32, precision=HIGHEST) + carry
            cum_ref[rows, :] = cs
            carry = cs[CUM_BLOCK - 1:CUM_BLOCK, :]

    spec = pl.BlockSpec((seq_len, LANES), lambda b: (b, 0))
    return pl.pallas_call(
        body, name="fprep_fwd", grid=(n_seq,), in_specs=[spec, pl.BlockSpec((1, LANES), lambda b: (0, 0))],
        out_specs=spec, out_shape=jax.ShapeDtypeStruct((n, LANES), F32), compiler_params=_cparams("parallel"),
    )(fl, bf)


def _fprep_bwd(dcum, fl, bf, n_seq):
    n = fl.shape[0]
    seq_len = n // n_seq
    nb = seq_len // CUM_BLOCK

    def body(dcum_ref, fl_ref, bf_ref, dfl_ref, dbf_ref):
        triu = _tri(False)
        lane = lax.broadcasted_iota(jnp.int32, (CUM_BLOCK, LANES), 1)
        carry = jnp.zeros((1, LANES), F32)
        total = jnp.zeros((1, LANES), F32)
        for blk in reversed(range(nb)):
            rows = slice(blk * CUM_BLOCK, (blk + 1) * CUM_BLOCK)
            rs = jnp.dot(triu, dcum_ref[rows, :], preferred_element_type=F32, precision=HIGHEST) + carry
            carry = rs[0:1, :]
            _, vjp = jax.vjp(jax.nn.log_sigmoid, fl_ref[rows, :] + bf_ref[...])
            dz = jnp.where(lane < N_HEADS, vjp(rs)[0], 0.0)
            dfl_ref[rows, :] = dz
            total = total + jnp.sum(dz, axis=0, keepdims=True)
        dbf_ref[0] = total

    spec = pl.BlockSpec((seq_len, LANES), lambda b: (b, 0))
    return pl.pallas_call(
        body, name="fprep_bwd", grid=(n_seq,), in_specs=[spec, spec, pl.BlockSpec((1, LANES), lambda b: (0, 0))],
        out_specs=(spec, pl.BlockSpec((1, 1, LANES), lambda b: (b, 0, 0))),
        out_shape=(jax.ShapeDtypeStruct((n, LANES), F32), jax.ShapeDtypeStruct((n_seq, 1, LANES), F32)),
        compiler_params=_cparams("parallel"),
    )(dcum, fl, bf)


ATT_TQ = 256
ATT_KSTEP = 256
ATT_SCALE = HEAD_DIM ** -0.5
NEG_BIG = -1e30


def _attn_scores(qe, kb, cq, ck, causal):
    s = _dot_nt(qe, kb) * ATT_SCALE + cq - ck
    return jnp.where(causal, s, NEG_BIG)


def _causal_mask(qi, kend):
    r = lax.broadcasted_iota(jnp.int32, (ATT_TQ, kend), 0) + qi * ATT_TQ
    c = lax.broadcasted_iota(jnp.int32, (ATT_TQ, kend), 1)
    return r >= c


def _attn_specs(n_seq, seq_len):
    nq = seq_len // ATT_TQ
    q_spec = pl.BlockSpec((ATT_TQ, LANES), lambda b, h, q: (b * nq + q, h))
    k_spec = pl.BlockSpec((seq_len, LANES), lambda b, h, q: (b, N_HEADS // 2 + h))
    v_spec = pl.BlockSpec((seq_len, LANES), lambda b, h, q: (b, N_HEADS + h))
    cq_spec = pl.BlockSpec((1, 2, ATT_TQ, 1), lambda b, h, q: (b, h, q, 0))
    ck_spec = pl.BlockSpec((1, 2, 1, seq_len), lambda b, h, q: (b, h, 0, 0))
    return nq, q_spec, k_spec, v_spec, cq_spec, ck_spec


def _head_selectors():
    head0 = lax.broadcasted_iota(jnp.int32, (1, LANES), 1) < HEAD_DIM
    return head0, (head0, jnp.logical_not(head0))


def _for_key_range(qi, seq_len, run):
    per = ATT_KSTEP // ATT_TQ
    for g in range(seq_len // ATT_KSTEP):
        pl.when(qi // per == g)(functools.partial(run, (g + 1) * ATT_KSTEP))


def _attn_fwd(qkv, cq, ck, n_seq):
    n = qkv.shape[0]
    seq_len = n // n_seq
    nq, q_spec, k_spec, v_spec, cq_spec, ck_spec = _attn_specs(n_seq, seq_len)

    def body(q_ref, k_ref, v_ref, cq_ref, ck_ref, o_ref, lse_ref):
        qi = pl.program_id(2)
        q2 = q_ref[...]
        head0, sels = _head_selectors()
        qe = [jnp.where(sel, q2, 0.0).astype(BF16) for sel in sels]

        def run(kend):
            kb = k_ref[0:kend, :].astype(BF16)
            vb = v_ref[0:kend, :].astype(BF16)
            causal = _causal_mask(qi, kend)
            outs = []
            for e in range(2):
                s = _attn_scores(qe[e], kb, cq_ref[0, e], ck_ref[0, e, :, 0:kend], causal)
                mx = jnp.max(s, axis=1, keepdims=True)
                p = jnp.exp(s - mx)
                den = jnp.sum(p, axis=1, keepdims=True)
                outs.append(_dot(p.astype(BF16), vb) / den)
                lse_ref[0, e] = mx + jnp.log(den)
            o_ref[...] = jnp.where(head0, outs[0], outs[1])

        _for_key_range(qi, seq_len, run)

    return pl.pallas_call(
        body, name="attn_fwd", grid=(n_seq, N_HEADS // 2, nq),
        in_specs=[q_spec, k_spec, v_spec, cq_spec, ck_spec],
        out_specs=(q_spec, cq_spec),
        out_shape=(jax.ShapeDtypeStruct((n, D_ATTN), F32), jax.ShapeDtypeStruct((n_seq, N_HEADS, seq_len, 1), F32)),
        compiler_params=_cparams("parallel", "parallel", "parallel"),
    )(qkv, qkv, qkv, cq, ck)


def _attn_bwd(qkv, cq, ck, o, do, lse, n_seq):
    n = qkv.shape[0]
    seq_len = n // n_seq
    nq, q_spec, k_spec, v_spec, cq_spec, ck_spec = _attn_specs(n_seq, seq_len)
    kv_out = pl.BlockSpec((seq_len, LANES), lambda b, h, q: (b, h))

    def body(q_ref, k_ref, v_ref, cq_ref, ck_ref, o_ref, do_ref, lse_ref, dq_ref, dk_ref, dv_ref, dcq_ref, dck_ref):
        qi = pl.program_id(2)

        @pl.when(qi == 0)
        def _():
            dk_ref[...] = jnp.zeros_like(dk_ref)
            dv_ref[...] = jnp.zeros_like(dv_ref)
            dck_ref[...] = jnp.zeros_like(dck_ref)
        q2 = q_ref[...]
        do2 = do_ref[...]
        o2 = o_ref[...]
        head0, sels = _head_selectors()
        qe = [jnp.where(sel, q2, 0.0).astype(BF16) for sel in sels]
        doe = [jnp.where(sel, do2, 0.0) for sel in sels]
        doe_b = [d.astype(BF16) for d in doe]
        delta = [jnp.sum(d * o2, axis=1, keepdims=True) for d in doe]

        def run(kend):
            kb = k_ref[0:kend, :].astype(BF16)
            vb = v_ref[0:kend, :].astype(BF16)
            causal = _causal_mask(qi, kend)
            dqs = []
            dk = jnp.zeros((kend, LANES), F32)
            dv = jnp.zeros((kend, LANES), F32)
            for e in range(2):
                s = _attn_scores(qe[e], kb, cq_ref[0, e], ck_ref[0, e, :, 0:kend], causal)
                p = jnp.exp(s - lse_ref[0, e])
                ds = p * (_dot_nt(doe_b[e], vb) - delta[e])
                ds_b = ds.astype(BF16)
                dqs.append(_dot(ds_b, kb))
                dk = dk + _dot_tn(ds_b, qe[e])
                dv = dv + _dot_tn(p.astype(BF16), doe_b[e])
                dcq_ref[0, e] = jnp.sum(ds, axis=1, keepdims=True)
                dck_ref[0, e, :, 0:kend] -= jnp.sum(ds, axis=0, keepdims=True)
            dk_ref[0:kend, :] += dk * ATT_SCALE
            dv_ref[0:kend, :] += dv
            dq_ref[...] = jnp.where(head0, dqs[0], dqs[1]) * ATT_SCALE

        _for_key_range(qi, seq_len, run)

    return pl.pallas_call(
        body, name="attn_bwd", grid=(n_seq, N_HEADS // 2, nq),
        in_specs=[q_spec, k_spec, v_spec, cq_spec, ck_spec, q_spec, q_spec, cq_spec],
        out_specs=(q_spec, kv_out, kv_out, cq_spec, ck_spec),
        out_shape=(jax.ShapeDtypeStruct((n, D_ATTN), F32), jax.ShapeDtypeStruct((n, D_ATTN), F32),
                   jax.ShapeDtypeStruct((n, D_ATTN), F32),
                   jax.ShapeDtypeStruct((n_seq, N_HEADS, seq_len, 1), F32),
                   jax.ShapeDtypeStruct((n_seq, N_HEADS, 1, seq_len), F32)),
        compiler_params=_cparams("parallel", "parallel", "arbitrary"),
    )(qkv, qkv, qkv, cq, ck, o, do, lse)


WEIGHT_NAMES = ("norm_mix", "w_in", "b_forget", "lam_re", "lam_im", "b_re", "b_im", "c_re", "c_im", "d_skip", "log_dt",
                "w_glu", "b_glu", "q_norm", "k_norm", "norm_out_ssm", "norm_out_attn", "w_out", "norm_ffn", "w_up",
                "conv_w", "conv_b", "w_down")
SHARDED = ("w_in", "w_glu", "w_out", "w_up", "conv_w", "w_down")
ADAM_ROWS = {"w_in": 256, "w_glu": 64, "w_out": 128, "w_up": 128, "conv_w": 3, "w_down": 344}
PACK_ROWS = SUBLANES * LANES
ROWS_DOWN = D_FF // N_DEV
ROWS_OUT = D_MODEL // N_DEV
ROWS_GLU = D_SSM * D_SSM // N_DEV // D_MODEL


def _pad_to(a, axis, size):
    pad = [(0, 0)] * a.ndim
    pad[axis] = (0, size - a.shape[axis])
    return jnp.pad(a, pad)


def _block_diag(t, transpose):
    t4 = t.reshape(S5_BLOCKS, 8, SSM_GROUP, SSM_STATE)
    eye = jnp.eye(8, dtype=t.dtype)
    if transpose:
        e = jnp.swapaxes(t4, 2, 3)[:, :, :, None, :] * eye[None, :, None, :, None]
        return e.reshape(S5_BLOCKS, S5_STATES, LANES)
    e = t4[:, :, :, None, :] * eye[None, :, None, :, None]
    return e.reshape(S5_BLOCKS, LANES, S5_STATES)


def _block_diag_extract(m, transpose):
    if transpose:
        m5 = m.reshape(S5_BLOCKS, 8, SSM_STATE, 8, SSM_GROUP)
        d = jnp.stack([m5[:, i, :, i, :] for i in range(8)], axis=1)
        return jnp.swapaxes(d, 2, 3).reshape(N_GROUPS, SSM_GROUP, SSM_STATE)
    m5 = m.reshape(S5_BLOCKS, 8, SSM_GROUP, 8, SSM_STATE)
    d = jnp.stack([m5[:, i, :, i, :] for i in range(8)], axis=1)
    return d.reshape(N_GROUPS, SSM_GROUP, SSM_STATE)


def _pack(pieces):
    flat = jnp.concatenate([p.reshape(-1).astype(F32) for p in pieces])
    size = -(-flat.shape[0] // PACK_ROWS) * PACK_ROWS
    return _pad_to(flat, 0, size).reshape(-1, LANES)


def _unpack(packed, shapes):
    flat = packed.reshape(-1)
    out, off = [], 0
    for shp in shapes:
        size = math.prod(shp)
        out.append(flat[off:off + size].reshape(shp))
        off += size
    return out


def kernel(x, norm_mix, w_in, b_forget, lam_re, lam_im, b_re, b_im, c_re, c_im, d_skip, log_dt, w_glu, b_glu, q_norm, k_norm, norm_out_ssm, norm_out_attn, w_out, norm_ffn, w_up, conv_w, conv_b, w_down, loss_target, m_norm_mix, m_w_in, m_b_forget, m_lam_re, m_lam_im, m_b_re, m_b_im, m_c_re, m_c_im, m_d_skip, m_log_dt, m_w_glu, m_b_glu, m_q_norm, m_k_norm, m_norm_out_ssm, m_norm_out_attn, m_w_out, m_norm_ffn, m_w_up, m_conv_w, m_conv_b, m_w_down, v_norm_mix, v_w_in, v_b_forget, v_lam_re, v_lam_im, v_b_re, v_b_im, v_c_re, v_c_im, v_d_skip, v_log_dt, v_w_glu, v_b_glu, v_q_norm, v_k_norm, v_norm_out_ssm, v_norm_out_attn, v_w_out, v_norm_ffn, v_w_up, v_conv_w, v_conv_b, v_w_down):
    given = dict(locals())
    weights = {k: given[k] for k in WEIGHT_NAMES}
    mom1 = {k: given["m_" + k] for k in WEIGHT_NAMES}
    mom2 = {k: given["v_" + k] for k in WEIGHT_NAMES}
    n_seq, seq_len, _ = x.shape
    n = n_seq * seq_len
    xf = x.reshape(n, D_MODEL)
    target = loss_target.reshape(n, D_MODEL)
    me_idx = 4 * lax.axis_index("x") + 2 * lax.axis_index("y") + lax.axis_index("c")

    g_in, g_cw = _exchange([w_in[0].astype(BF16), conv_w[0]], [False, False], "gather_w_in")
    rest_flags = [False] * 2
    rows_w = jnp.concatenate([w_down[0], w_out[0], w_glu[0].reshape(ROWS_GLU, D_MODEL)], axis=0).astype(BF16)
    w_sems = _exchange_start([rows_w, w_up[0].astype(BF16)], rest_flags, g_in, "gather_rest_start", 0)
    norm_mix = norm_mix + w_sems[4][0, 0]
    w_in_p = _pad_to(jnp.swapaxes(g_in, 0, 1).reshape(D_MODEL, D_IN), 1, D_IN_PAD)

    lr3 = lam_re[0].reshape(N_GROUPS, 1, SSM_STATE)
    li3 = lam_im[0].reshape(N_GROUPS, 1, SSM_STATE)
    ldt3 = log_dt[0].reshape(N_GROUPS, 1, 1)
    br_t = jnp.swapaxes(b_re[0], 1, 2)
    bi_t = jnp.swapaxes(b_im[0], 1, 2)
    ab_re, ab_im, bb_re, bb_im = _s5_param_fwd(lr3, li3, ldt3, br_t, bi_t)
    a_re = ab_re.reshape(S5_BLOCKS, 1, S5_STATES)
    a_im = ab_im.reshape(S5_BLOCKS, 1, S5_STATES)
    bbr = _block_diag(bb_re, False).astype(BF16)
    bbi = _block_diag(bb_im, False).astype(BF16)
    cr = _block_diag(c_re[0], True).astype(BF16)
    ci = _block_diag(c_im[0], True).astype(BF16)

    avg = jnp.kron(jnp.eye(N_HEADS, dtype=F32), jnp.full((HEAD_DIM, HEAD_DIM), 1.0 / HEAD_DIM, F32)).astype(BF16)
    qg = jnp.tile(q_norm, (1, N_HEADS))
    kg = jnp.tile(k_norm, (1, N_HEADS))
    hn, u, qkv, raw, fl = _inproj_fwd(xf, norm_mix, w_in_p, avg, qg, kg)
    yc, st_r, st_i = _s5_fwd(u, a_re, a_im, bbr, bbi, cr, ci, d_skip, n_seq)
    bf = _pad_to(b_forget, 1, LANES)
    cum = _fprep_fwd(fl, bf, n_seq)
    cum8 = jnp.swapaxes(cum[:, :N_HEADS].reshape(n_seq, seq_len, N_HEADS), 1, 2)
    cq = cum8[:, :, :, None]
    ck = cum8[:, :, None, :]
    ya, lse = _attn_fwd(qkv, cq, ck, n_seq)
    (own_rows, own_up), (g_rows, g_up) = _exchange_wait(w_sems[0], w_sems[1], w_sems[2], w_sems[3], rest_flags, ya,
                                                        "gather_rest_wait")
    my_slot = lax.broadcasted_iota(jnp.int32, (N_DEV, 1, 1), 0) == me_idx
    g_rows = jnp.where(my_slot, own_rows[None], g_rows)
    g_up = jnp.where(my_slot, own_up[None], g_up)
    g_down = g_rows[:, :ROWS_DOWN]
    g_out = g_rows[:, ROWS_DOWN:ROWS_DOWN + ROWS_OUT]
    g_glu = g_rows[:, ROWS_DOWN + ROWS_OUT:]
    w_glu_f = g_glu.reshape(D_SSM, D_SSM)
    w_out_f = g_out.reshape(D_MODEL, D_MODEL)
    w_up_f = jnp.swapaxes(g_up, 0, 1).reshape(D_MODEL, 2 * D_FF)
    wup_g = _pad_to(w_up_f[:, :D_FF], 1, D_FF_PAD)
    wup_v = _pad_to(w_up_f[:, D_FF:], 1, D_FF_PAD)
    cw_f = jnp.swapaxes(g_cw, 0, 1).reshape(3, 2 * D_FF)
    cw4 = jnp.concatenate([cw_f, conv_b], axis=0)
    cw_g = _pad_to(_pad_to(cw4[:, :D_FF], 1, D_FF_PAD), 0, SUBLANES)
    cw_v = _pad_to(_pad_to(cw4[:, D_FF:], 1, D_FF_PAD), 0, SUBLANES)
    w_down_p = _pad_to(g_down.reshape(D_FF, D_MODEL), 0, D_FF_PAD)
    ys = _glu_fwd(yc, w_glu_f, b_glu)
    h1, hn2, mixed = _mix_fwd(xf, ys, ya, norm_out_ssm, norm_out_attn, w_out_f, norm_ffn)
    ug, uv, dy, loss_part = _ffn_fwd(hn2, h1, target, wup_g, wup_v, cw_g, cw_v, w_down_p, seq_len)
    loss = lax.psum(0.5 * jnp.sum(loss_part) / D_MODEL, ("x", "y", "c"))

    dug, duv, act, dhn2, dcg, dcv = _ffn_bwd(dy, ug, uv, wup_g, wup_v, cw_g, cw_v, w_down_p, seq_len)
    dh1, dys, dya, d_gs, d_ga, d_gf = _mix_bwd(dy, dhn2[None], h1, ys, ya, norm_out_ssm, norm_out_attn, w_out_f, norm_ffn)
    dyc, gl_b, dz_b, d_bglu = _glu_bwd(yc, dys, w_glu_f, b_glu)

    gw_glu = _tn_matmul(gl_b, dz_b, "dw_glu", D_SSM, D_SSM)
    gw_out = _tn_matmul(mixed, dh1, "dw_out", D_MODEL, D_MODEL)
    gw_up = jnp.concatenate([_tn_matmul(hn2, dug, "dw_up_gate", D_MODEL, D_FF_PAD // 2, out_cols=D_FF),
                             _tn_matmul(hn2, duv, "dw_up_val", D_MODEL, D_FF_PAD // 2, out_cols=D_FF)], axis=1)
    gw_down = _tn_matmul(act, dy, "dw_down", D_FF_PAD // 2, D_MODEL, out_rows=D_FF)
    dcg2 = jnp.swapaxes(dcg, 0, 1).reshape(SUBLANES, D_FF_PAD)[:, :D_FF]
    dcv2 = jnp.swapaxes(dcv, 0, 1).reshape(SUBLANES, D_FF_PAD)[:, :D_FF]
    g_conv = jnp.concatenate([dcg2, dcv2], axis=1)
    by_cols = lambda g, c: jnp.swapaxes(g.reshape(g.shape[0], N_DEV, c), 0, 1)
    early_flags = [True] * 2
    rows_g = jnp.concatenate([gw_down.reshape(N_DEV, ROWS_DOWN, D_MODEL), gw_out.reshape(N_DEV, ROWS_OUT, D_MODEL),
                              gw_glu.reshape(N_DEV, ROWS_GLU, D_MODEL)], axis=1).astype(BF16)
    g_sems = _exchange_start([rows_g, by_cols(gw_up, 2 * D_FF // N_DEV).astype(BF16)], early_flags, dyc,
                             "grad_early_start", 1)
    started = g_sems[4][0, 0]

    du, dbbr, dbbi, dcr, dci, dar, dai, ddk = _s5_bwd(u, dyc, st_r, st_i, a_re, a_im, bbr, bbi, cr, ci,
                                                      d_skip + started, n_seq)
    partial_early = {
        "ab_re": jnp.sum(dar, axis=1), "ab_im": jnp.sum(dai, axis=1),
        "bb_re": _block_diag_extract(dbbr, False), "bb_im": _block_diag_extract(dbbi, False),
        "c_re": _block_diag_extract(dcr, True), "c_im": _block_diag_extract(dci, True),
        "d_skip": jnp.sum(ddk, axis=1), "b_glu": d_bglu,
        "norm_out_ssm": d_gs, "norm_out_attn": d_ga, "norm_ffn": d_gf, "conv_b": g_conv[3],
    }
    early_keys = tuple(partial_early)
    early_shapes = [partial_early[k].shape for k in early_keys]
    p_sems = _exchange_start([_pack([partial_early[k] for k in early_keys])], [False], du, "small_early_start", 2)
    started = started + p_sems[4][0, 0]

    dqn, dkn, dv, dcq, dck = _attn_bwd(qkv, cq, ck + started, ya, dya, lse, n_seq)
    dcum8 = dcq[:, :, :, 0] + dck.reshape(n_seq, N_HEADS, seq_len)
    dcum = _pad_to(jnp.swapaxes(dcum8, 1, 2).reshape(n, N_HEADS), 1, LANES)
    dfl, dbf = _fprep_bwd(dcum, fl, bf, n_seq)
    dx, dproj, d_gmix, d_qg, d_kg = _inproj_bwd(xf, norm_mix, w_in_p, avg, qg, kg, raw, du, dqn, dkn, dv, dfl, dh1)

    gw_in = _tn_matmul(hn, dproj, "dw_in", D_MODEL, D_IN_PAD, out_cols=D_IN)
    partial_late = {
        "norm_mix": d_gmix, "b_forget": jnp.sum(dbf, axis=(0, 1))[:N_HEADS],
        "q_norm": jnp.sum(d_qg.reshape(N_HEADS, HEAD_DIM), axis=0),
        "k_norm": jnp.sum(d_kg.reshape(N_HEADS, HEAD_DIM), axis=0),
    }
    late_keys = tuple(partial_late)
    late_shapes = [partial_late[k].shape for k in late_keys]

    land_in, land_cw, small_parts = _exchange(
        [by_cols(gw_in, D_IN // N_DEV).astype(BF16), by_cols(g_conv[:3], 2 * D_FF // N_DEV),
         _pack([partial_late[k] for k in late_keys])], [True, True, False], "grad_late_exchange")
    (src_rows, src_up), (land_rows, land_up) = _exchange_wait(g_sems[0], g_sems[1], g_sems[2], g_sems[3], early_flags,
                                                              land_in, "grad_early_wait")
    land_down = land_rows[:, :ROWS_DOWN]
    land_out = land_rows[:, ROWS_DOWN:ROWS_DOWN + ROWS_OUT]
    land_glu = land_rows[:, ROWS_DOWN + ROWS_OUT:].reshape(N_DEV, -1, D_SSM)
    own_rows = lax.dynamic_index_in_dim(src_rows, me_idx, 0, keepdims=False)
    own_up = lax.dynamic_index_in_dim(src_up, me_idx, 0, keepdims=False)
    own = {"w_in": None, "conv_w": None, "w_up": own_up, "w_down": own_rows[:ROWS_DOWN],
           "w_out": own_rows[ROWS_DOWN:ROWS_DOWN + ROWS_OUT], "w_glu": own_rows[ROWS_DOWN + ROWS_OUT:].reshape(-1, D_SSM)}
    grads, deltas, new_m, new_v = {}, {}, {}, {}
    for name, land in zip(SHARDED, (land_in, land_glu, land_out, land_up, land_cw, land_down)):
        grads[name], deltas[name], new_m[name], new_v[name] = _adam_sharded(
            land, own[name], weights[name], mom1[name], mom2[name], "adam_" + name, ADAM_ROWS[name])

    (own_pack,), (early_parts,) = _exchange_wait(p_sems[0], p_sems[1], p_sems[2], p_sems[3], [False], land_up,
                                                 "small_early_wait")
    summed = dict(zip(late_keys, _unpack(_sum_partials(small_parts, None, "sum_late_partials"), late_shapes)))
    summed.update(zip(early_keys, _unpack(_sum_partials(early_parts, own_pack, "sum_early_partials"), early_shapes)))
    dlr, dli, dldt, dbr_t, dbi_t = _s5_param_bwd(
        lr3, li3, ldt3, br_t, bi_t, summed["ab_re"].reshape(lr3.shape), summed["ab_im"].reshape(lr3.shape),
        summed["bb_re"], summed["bb_im"])
    small_grads = {
        "norm_mix": summed["norm_mix"], "b_forget": summed["b_forget"], "lam_re": dlr, "lam_im": dli,
        "b_re": jnp.swapaxes(dbr_t, 1, 2), "b_im": jnp.swapaxes(dbi_t, 1, 2), "c_re": summed["c_re"], "c_im": summed["c_im"],
        "d_skip": summed["d_skip"], "log_dt": dldt, "b_glu": summed["b_glu"], "q_norm": summed["q_norm"],
        "k_norm": summed["k_norm"], "norm_out_ssm": summed["norm_out_ssm"], "norm_out_attn": summed["norm_out_attn"],
        "norm_ffn": summed["norm_ffn"], "conv_b": summed["conv_b"],
    }
    repl = tuple(k for k in WEIGHT_NAMES if k not in SHARDED)
    g_list = [small_grads[k].reshape(weights[k].shape) for k in repl]
    d_list, m_list, v_list = _adam_replicated(g_list, [weights[k] for k in repl], [mom1[k] for k in repl],
                                              [mom2[k] for k in repl], "adam_replicated")
    for k, g, d, nm, nv in zip(repl, g_list, d_list, m_list, v_list):
        grads[k], deltas[k], new_m[k], new_v[k] = g, d, nm, nv

    grad_x = dx.reshape(x.shape)
    return (loss, grad_x, *[grads[k] for k in WEIGHT_NAMES], *[deltas[k] for k in WEIGHT_NAMES],
            *[new_m[k] for k in WEIGHT_NAMES], *[new_v[k] for k in WEIGHT_NAMES])
```

```python
import functools
import math

import jax
import jax.numpy as jnp
from jax import lax
from jax.experimental import pallas as pl
from jax.experimental.pallas import tpu as pltpu

F32 = jnp.float32
BF16 = jnp.bfloat16
HIGHEST = lax.Precision.HIGHEST

N_DEV = 8
D_MODEL = 1024
D_SSM = 512
D_ATTN = 512
N_HEADS = 8
HEAD_DIM = 64
N_GROUPS = 32
SSM_GROUP = 16
SSM_STATE = 64
D_FF = 2752
D_FF_PAD = 2816
D_IN = 2056
D_IN_PAD = 2176
EPS = 1e-6
LANES = 128
SUBLANES = 8
VMEM_LIMIT = 56 * 1024 * 1024

ADAM_LR = 0.001
ADAM_B1 = 0.9
ADAM_B2 = 0.999
ADAM_EPS = 1e-08
ADAM_WD = 0.01
ADAM_STEP = 10


def _cparams(*sem):
    return pltpu.CompilerParams(dimension_semantics=sem, vmem_limit_bytes=VMEM_LIMIT)


def _dot(a, b, **kw):
    return jnp.dot(a, b, preferred_element_type=F32, **kw)


def _dot_nt(a, b):
    return lax.dot_general(a, b, (((1,), (1,)), ((), ())), preferred_element_type=F32)


def _dot_tn(a, b):
    return lax.dot_general(a, b, (((0,), (0,)), ((), ())), preferred_element_type=F32)


def _rms(x, g):
    return x * lax.rsqrt(jnp.mean(x * x, axis=-1, keepdims=True) + EPS) * g


def _split_dot(x, avg):
    hi = x.astype(BF16)
    lo = (x - hi.astype(F32)).astype(BF16)
    return _dot(hi, avg) + _dot(lo, avg)


@jax.custom_vjp
def _group_mean(x, avg):
    return _split_dot(x, avg)


def _group_mean_fwd(x, avg):
    return _split_dot(x, avg), avg


def _group_mean_bwd(avg, ct):
    return _split_dot(ct, avg), jnp.zeros_like(avg)


_group_mean.defvjp(_group_mean_fwd, _group_mean_bwd)


def _headnorm(q, avg, g):
    return q * lax.rsqrt(_group_mean(q * q, avg) + EPS) * g


def _exchange(srcs, scatter_flags, name):
    n = len(srcs)
    out_shape = []
    for s, sc in zip(srcs, scatter_flags):
        shp = s.shape if sc else (N_DEV,) + s.shape
        out_shape.append(jax.ShapeDtypeStruct(shp, s.dtype))

    def body(*refs):
        src = refs[:n]
        dst = refs[n:2 * n]
        send_sems, recv_sems, loc_sems = refs[2 * n:]
        x, y, c = lax.axis_index("x"), lax.axis_index("y"), lax.axis_index("c")
        me = 4 * x + 2 * y + c
        peers = []
        for j in range(1, N_DEV):
            px = 1 - x if (j >> 2) & 1 else x
            py = 1 - y if (j >> 1) & 1 else y
            pc = 1 - c if j & 1 else c
            peers.append(((px, py, pc), 4 * px + 2 * py + pc))
        local, sends = [], []
        for k in range(n):
            own = src[k].at[me] if scatter_flags[k] else src[k]
            lc = pltpu.make_async_copy(own, dst[k].at[me], loc_sems.at[k])
            lc.start()
            local.append(lc)
            for j, (pid, pidx) in enumerate(peers):
                s = src[k].at[pidx] if scatter_flags[k] else src[k]
                cp = pltpu.make_async_remote_copy(
                    src_ref=s, dst_ref=dst[k].at[me], send_sem=send_sems.at[k, j], recv_sem=recv_sems.at[k, j],
                    device_id=pid, device_id_type=pl.DeviceIdType.MESH)
                cp.start()
                sends.append(cp)
        for k in range(n):
            for j, (pid, pidx) in enumerate(peers):
                s = src[k].at[pidx] if scatter_flags[k] else src[k]
                pltpu.make_async_remote_copy(
                    src_ref=s, dst_ref=dst[k].at[pidx], send_sem=send_sems.at[k, j], recv_sem=recv_sems.at[k, j],
                    device_id=pid, device_id_type=pl.DeviceIdType.MESH).wait_recv()
        for cp in sends:
            cp.wait_send()
        for lc in local:
            lc.wait()

    any_spec = pl.BlockSpec(memory_space=pl.ANY)
    return pl.pallas_call(
        body, name=name, out_shape=tuple(out_shape),
        in_specs=[any_spec] * n, out_specs=tuple([any_spec] * n),
        scratch_shapes=[pltpu.SemaphoreType.DMA((n, N_DEV - 1)), pltpu.SemaphoreType.DMA((n, N_DEV - 1)),
                        pltpu.SemaphoreType.DMA((n,))],
        compiler_params=pltpu.CompilerParams(has_side_effects=True),
    )(*srcs)


def _peer_list():
    x, y, c = lax.axis_index("x"), lax.axis_index("y"), lax.axis_index("c")
    peers = []
    for j in range(1, N_DEV):
        px = 1 - x if (j >> 2) & 1 else x
        py = 1 - y if (j >> 1) & 1 else y
        pc = 1 - c if j & 1 else c
        peers.append(((px, py, pc), 4 * px + 2 * py + pc))
    return 4 * x + 2 * y + c, peers


def _split_copies(src, land, send_sems, recv_sems, scatter_flags, me, peers, incoming):
    copies = []
    for k in range(len(src)):
        for j, (pid, pidx) in enumerate(peers):
            s = src[k].at[pidx] if scatter_flags[k] else src[k]
            i = k * (N_DEV - 1) + j
            copies.append(pltpu.make_async_remote_copy(
                src_ref=s, dst_ref=land[k].at[pidx if incoming else me], send_sem=send_sems[i],
                recv_sem=recv_sems[i], device_id=pid, device_id_type=pl.DeviceIdType.MESH))
    return copies


def _exchange_start(srcs, scatter_flags, after, name, collective_id):
    n = len(srcs)
    ns = n * (N_DEV - 1)
    hbm = pl.BlockSpec(memory_space=pltpu.HBM)
    sem = pl.BlockSpec(memory_space=pltpu.SEMAPHORE)
    land_shapes = [s.shape if sc else (N_DEV,) + s.shape for s, sc in zip(srcs, scatter_flags)]

    def body(*refs):
        src, land = refs[:n], refs[n:2 * n]
        send_sems = refs[2 * n + 1:2 * n + 1 + ns]
        recv_sems = refs[2 * n + 1 + ns:2 * n + 1 + 2 * ns]
        token = refs[4 * n + 1 + 2 * ns]
        me, peers = _peer_list()
        barrier = pltpu.get_barrier_semaphore()
        for pid, _ in peers:
            pl.semaphore_signal(barrier, inc=1, device_id=pid, device_id_type=pl.DeviceIdType.MESH)
        pl.semaphore_wait(barrier, N_DEV - 1)
        for cp in _split_copies(src, land, send_sems, recv_sems, scatter_flags, me, peers, False):
            cp.start()
        token[...] = jnp.zeros_like(token)

    outs = pl.pallas_call(
        body, name=name,
        out_shape=(*[pltpu.SemaphoreType.DMA(())] * (2 * ns), *[pltpu.HBM(s.shape, s.dtype) for s in srcs],
                   *[pltpu.HBM(shp, s.dtype) for shp, s in zip(land_shapes, srcs)],
                   jax.ShapeDtypeStruct((SUBLANES, LANES), F32)),
        in_specs=[hbm] * (2 * n) + [pl.BlockSpec(memory_space=pl.ANY)],
        out_specs=(*[sem] * (2 * ns), *[hbm] * (2 * n), pl.BlockSpec(memory_space=pltpu.VMEM)),
        input_output_aliases={i: 2 * ns + i for i in range(2 * n)},
        compiler_params=pltpu.CompilerParams(has_side_effects=pltpu.SideEffectType.DATAFLOW_SIDE_EFFECTING,
                                             collective_id=collective_id),
    )(*[pltpu.with_memory_space_constraint(s, pltpu.HBM) for s in srcs],
      *[pltpu.with_memory_space_constraint(lax.empty(shp, s.dtype), pltpu.HBM) for shp, s in zip(land_shapes, srcs)],
      after)
    return (outs[:ns], outs[ns:2 * ns], outs[2 * ns:2 * ns + n], outs[2 * ns + n:2 * ns + 2 * n], outs[2 * ns + 2 * n])


def _exchange_wait(send_sems, recv_sems, srcs, lands, scatter_flags, after, name):
    n = len(srcs)
    ns = n * (N_DEV - 1)
    hbm = pl.BlockSpec(memory_space=pltpu.HBM)
    sem = pl.BlockSpec(memory_space=pltpu.SEMAPHORE)

    def body(*refs):
        src, land = refs[:n], refs[n:2 * n]
        s_sems = refs[2 * n:2 * n + ns]
        r_sems = refs[2 * n + ns:2 * n + 2 * ns]
        me, peers = _peer_list()
        for cp in _split_copies(src, land, s_sems, r_sems, scatter_flags, me, peers, True):
            cp.wait_send()
            cp.wait_recv()

    outs = pl.pallas_call(
        body, name=name,
        out_shape=tuple(pltpu.HBM(a.shape, a.dtype) for a in (*srcs, *lands)),
        in_specs=[hbm] * (2 * n) + [sem] * (2 * ns) + [pl.BlockSpec(memory_space=pl.ANY)],
        out_specs=tuple([hbm] * (2 * n)),
        input_output_aliases={i: i for i in range(2 * n)},
        compiler_params=pltpu.CompilerParams(has_side_effects=pltpu.SideEffectType.DATAFLOW_SIDE_EFFECTING),
    )(*srcs, *lands, *send_sems, *recv_sems, after)
    return outs[:n], outs[n:]


def _tn_matmul(a, b, name, tk, tm, out_rows=None, out_cols=None, tn=512):
    n_tok, k_dim = a.shape
    m_dim = b.shape[1]
    grid = (k_dim // tk, m_dim // tm, n_tok // tn)

    def body(a_ref, b_ref, o_ref):
        @pl.when(pl.program_id(2) == 0)
        def _():
            o_ref[...] = jnp.zeros_like(o_ref)
        o_ref[...] += _dot_tn(a_ref[...].astype(BF16), b_ref[...].astype(BF16))

    return pl.pallas_call(
        body, name=name, grid=grid,
        in_specs=[pl.BlockSpec((tn, tk), lambda i, j, k: (k, i)), pl.BlockSpec((tn, tm), lambda i, j, k: (k, j))],
        out_specs=pl.BlockSpec((tk, tm), lambda i, j, k: (i, j)),
        out_shape=jax.ShapeDtypeStruct((out_rows or k_dim, out_cols or m_dim), F32),
        compiler_params=_cparams("parallel", "parallel", "arbitrary"),
    )(a, b)


def _adam_math(g, w, m, v):
    m = ADAM_B1 * m + (1.0 - ADAM_B1) * g
    v = ADAM_B2 * v + (1.0 - ADAM_B2) * (g * g)
    m_hat = m / (1.0 - ADAM_B1 ** ADAM_STEP)
    v_hat = v / (1.0 - ADAM_B2 ** ADAM_STEP)
    delta = -ADAM_LR * (m_hat / (jnp.sqrt(v_hat) + ADAM_EPS) + ADAM_WD * w)
    return delta, m, v


def _adam_sharded(land, own, w, m, v, name, tr):
    _, r, c = w.shape

    def body(*refs):
        l_ref = refs[0]
        own_ref = refs[1] if own is not None else None
        w_ref, m_ref, v_ref, g_ref, d_ref, nm_ref, nv_ref = [ref.at[0] for ref in refs[-7:]]
        if own_ref is not None:
            x, y, z = lax.axis_index("x"), lax.axis_index("y"), lax.axis_index("c")
            me = 4 * x + 2 * y + z
            mine = own_ref[...].astype(F32)
        g = None
        for s in range(N_DEV):
            part = l_ref[s].astype(F32)
            if own_ref is not None:
                part = jnp.where(me == s, mine, part)
            g = part if g is None else g + part
        d, nm, nv = _adam_math(g, w_ref[...], m_ref[...], v_ref[...])
        g_ref[...] = g
        d_ref[...] = d
        nm_ref[...] = nm
        nv_ref[...] = nv

    spec = pl.BlockSpec((1, tr, c), lambda i: (0, i, 0))
    own_specs, own_args = ([pl.BlockSpec((tr, c), lambda i: (i, 0))], [own]) if own is not None else ([], [])
    return pl.pallas_call(
        body, name=name, grid=(r // tr,),
        in_specs=[pl.BlockSpec((N_DEV, tr, c), lambda i: (0, i, 0)), *own_specs, spec, spec, spec],
        out_specs=(spec, spec, spec, spec),
        out_shape=tuple(jax.ShapeDtypeStruct((1, r, c), F32) for _ in range(4)),
        compiler_params=_cparams("parallel"),
    )(land, *own_args, w, m, v)


def _sum_partials(parts, own, name):
    _, r, c = parts.shape

    def body(*refs):
        p_ref, o_ref = refs[0], refs[-1]
        if own is not None:
            x, y, z = lax.axis_index("x"), lax.axis_index("y"), lax.axis_index("c")
            me = 4 * x + 2 * y + z
            mine = refs[1][...]
        g = None
        for s in range(N_DEV):
            part = p_ref[s]
            if own is not None:
                part = jnp.where(me == s, mine, part)
            g = part if g is None else g + part
        o_ref[...] = g

    args = (parts,) if own is None else (parts, own)
    return pl.pallas_call(body, name=name, out_shape=jax.ShapeDtypeStruct((r, c), F32),
                          compiler_params=pltpu.CompilerParams(vmem_limit_bytes=VMEM_LIMIT))(*args)


def _adam_replicated(gs, ws, ms, vs, name):
    k = len(ws)

    def body(*refs):
        outs = refs[4 * k:]
        for i in range(k):
            d, nm, nv = _adam_math(refs[i][...], refs[k + i][...], refs[2 * k + i][...], refs[3 * k + i][...])
            outs[i][...] = d
            outs[k + i][...] = nm
            outs[2 * k + i][...] = nv

    outs = pl.pallas_call(body, name=name, out_shape=tuple(jax.ShapeDtypeStruct(w.shape, F32) for w in ws) * 3,
                          compiler_params=pltpu.CompilerParams(vmem_limit_bytes=VMEM_LIMIT))(*gs, *ws, *ms, *vs)
    return outs[:k], outs[k:2 * k], outs[2 * k:]


def _inproj_fwd(x, g, w_in, avg, qg, kg, tm=512):
    n = x.shape[0]

    def body(x_ref, g_ref, w_ref, a_ref, qg_ref, kg_ref, hn_ref, u_ref, qkv_ref, raw_ref, fl_ref):
        hn = _rms(x_ref[...], g_ref[...]).astype(BF16)
        hn_ref[...] = hn
        proj = _dot(hn, w_ref[...])
        u_ref[...] = proj[:, 0:512]
        q = proj[:, 512:1024]
        k = proj[:, 1024:1536]
        raw_ref[:, 0:512] = q
        raw_ref[:, 512:1024] = k
        qkv_ref[:, 0:512] = _headnorm(q, a_ref[...], qg_ref[...])
        qkv_ref[:, 512:1024] = _headnorm(k, a_ref[...], kg_ref[...])
        qkv_ref[:, 1024:1536] = proj[:, 1536:2048]
        fl_ref[...] = proj[:, 2048:D_IN_PAD]

    row = lambda w: pl.BlockSpec((tm, w), lambda i: (i, 0))
    full = lambda a: pl.BlockSpec(a.shape, lambda i: (0,) * a.ndim)
    return pl.pallas_call(
        body, name="inproj_fwd", grid=(n // tm,),
        in_specs=[row(D_MODEL), full(g), full(w_in), full(avg), full(qg), full(kg)],
        out_specs=(row(D_MODEL), row(512), row(1536), row(1024), row(LANES)),
        out_shape=(jax.ShapeDtypeStruct((n, D_MODEL), BF16), jax.ShapeDtypeStruct((n, 512), F32),
                   jax.ShapeDtypeStruct((n, 1536), F32), jax.ShapeDtypeStruct((n, 1024), F32),
                   jax.ShapeDtypeStruct((n, LANES), F32)),
        compiler_params=_cparams("parallel"),
    )(x, g, w_in, avg, qg, kg)


def _inproj_bwd(x, g, w_in, avg, qg, kg, raw, du, dqn, dkn, dv, dfl, dres, tm=512):
    n = x.shape[0]

    def body(x_ref, g_ref, w_ref, a_ref, qg_ref, kg_ref, raw_ref, du_ref, dqn_ref, dkn_ref, dv_ref, dfl_ref, dres_ref,
             dx_ref, dproj_ref, dg_ref, dqg_ref, dkg_ref):
        @pl.when(pl.program_id(0) == 0)
        def _():
            dg_ref[...] = jnp.zeros_like(dg_ref)
            dqg_ref[...] = jnp.zeros_like(dqg_ref)
            dkg_ref[...] = jnp.zeros_like(dkg_ref)
        avg_m = a_ref[...]
        _, vjp_q = jax.vjp(lambda q, gg: _headnorm(q, avg_m, gg), raw_ref[:, 0:512], qg_ref[...])
        dq, dqg = vjp_q(dqn_ref[...])
        _, vjp_k = jax.vjp(lambda k, gg: _headnorm(k, avg_m, gg), raw_ref[:, 512:1024], kg_ref[...])
        dk, dkg = vjp_k(dkn_ref[...])
        dproj = jnp.concatenate([du_ref[...], dq, dk, dv_ref[...], dfl_ref[...]], axis=1).astype(BF16)
        dproj_ref[...] = dproj
        dhn = _dot_nt(dproj, w_ref[...])
        _, vjp_x = jax.vjp(_rms, x_ref[...], g_ref[...])
        dxn, dg = vjp_x(dhn)
        dx_ref[...] = dxn + dres_ref[...]
        dg_ref[...] += dg
        dqg_ref[...] += dqg
        dkg_ref[...] += dkg

    row = lambda w: pl.BlockSpec((tm, w), lambda i: (i, 0))
    full = lambda a: pl.BlockSpec(a.shape, lambda i: (0,) * a.ndim)
    vec = lambda w: pl.BlockSpec((1, w), lambda i: (0, 0))
    return pl.pallas_call(
        body, name="inproj_bwd", grid=(n // tm,),
        in_specs=[row(D_MODEL), full(g), full(w_in), full(avg), full(qg), full(kg), row(1024), row(512), row(512),
                  row(512), row(512), row(LANES), row(D_MODEL)],
        out_specs=(row(D_MODEL), row(D_IN_PAD), vec(D_MODEL), vec(512), vec(512)),
        out_shape=(jax.ShapeDtypeStruct((n, D_MODEL), F32), jax.ShapeDtypeStruct((n, D_IN_PAD), BF16),
                   jax.ShapeDtypeStruct((1, D_MODEL), F32), jax.ShapeDtypeStruct((1, 512), F32),
                   jax.ShapeDtypeStruct((1, 512), F32)),
        compiler_params=_cparams("arbitrary"),
    )(x, g, w_in, avg, qg, kg, raw, du, dqn, dkn, dv, dfl, dres)


def _glu_fwd(yc, wg, bg, tm=512):
    n = yc.shape[0]

    def body(yc_ref, w_ref, b_ref, ys_ref):
        gl = jax.nn.gelu(yc_ref[...])
        z = _dot(gl.astype(BF16), w_ref[...]) + b_ref[...]
        ys_ref[...] = gl * jax.nn.sigmoid(z)

    row = pl.BlockSpec((tm, 512), lambda i: (i, 0))
    full = lambda a: pl.BlockSpec(a.shape, lambda i: (0,) * a.ndim)
    return pl.pallas_call(
        body, name="glu_fwd", grid=(n // tm,), in_specs=[row, full(wg), full(bg)], out_specs=row,
        out_shape=jax.ShapeDtypeStruct((n, 512), F32), compiler_params=_cparams("parallel"),
    )(yc, wg, bg)


def _glu_bwd(yc, dys, wg, bg, tm=512):
    n = yc.shape[0]

    def body(yc_ref, dys_ref, w_ref, b_ref, dyc_ref, gl_ref, dz_ref, db_ref):
        @pl.when(pl.program_id(0) == 0)
        def _():
            db_ref[...] = jnp.zeros_like(db_ref)
        gl, vjp_gelu = jax.vjp(jax.nn.gelu, yc_ref[...])
        glb = gl.astype(BF16)
        z = _dot(glb, w_ref[...]) + b_ref[...]
        s = jax.nn.sigmoid(z)
        dys = dys_ref[...]
        dz = dys * gl * s * (1.0 - s)
        dzb = dz.astype(BF16)
        dgl = dys * s + _dot_nt(dzb, w_ref[...])
        dyc_ref[...] = vjp_gelu(dgl)[0]
        gl_ref[...] = glb
        dz_ref[...] = dzb
        db_ref[...] += jnp.sum(dz, axis=0, keepdims=True)

    row = pl.BlockSpec((tm, 512), lambda i: (i, 0))
    full = lambda a: pl.BlockSpec(a.shape, lambda i: (0,) * a.ndim)
    return pl.pallas_call(
        body, name="glu_bwd", grid=(n // tm,), in_specs=[row, row, full(wg), full(bg)],
        out_specs=(row, row, row, pl.BlockSpec((1, 512), lambda i: (0, 0))),
        out_shape=(jax.ShapeDtypeStruct((n, 512), F32), jax.ShapeDtypeStruct((n, 512), BF16),
                   jax.ShapeDtypeStruct((n, 512), BF16), jax.ShapeDtypeStruct((1, 512), F32)),
        compiler_params=_cparams("arbitrary"),
    )(yc, dys, wg, bg)


def _mix_fwd(x, ys, ya, gs, ga, wout, gf, tm=512):
    n = x.shape[0]

    def body(x_ref, ys_ref, ya_ref, gs_ref, ga_ref, w_ref, gf_ref, h1_ref, hn2_ref, mixed_ref):
        mixed = jnp.concatenate([_rms(ys_ref[...], gs_ref[...]), _rms(ya_ref[...], ga_ref[...])], axis=1).astype(BF16)
        mixed_ref[...] = mixed
        h1 = x_ref[...] + _dot(mixed, w_ref[...])
        h1_ref[...] = h1
        hn2_ref[...] = _rms(h1, gf_ref[...]).astype(BF16)

    row = lambda w: pl.BlockSpec((tm, w), lambda i: (i, 0))
    full = lambda a: pl.BlockSpec(a.shape, lambda i: (0,) * a.ndim)
    return pl.pallas_call(
        body, name="mix_fwd", grid=(n // tm,),
        in_specs=[row(D_MODEL), row(512), row(512), full(gs), full(ga), full(wout), full(gf)],
        out_specs=(row(D_MODEL), row(D_MODEL), row(D_MODEL)),
        out_shape=(jax.ShapeDtypeStruct((n, D_MODEL), F32), jax.ShapeDtypeStruct((n, D_MODEL), BF16),
                   jax.ShapeDtypeStruct((n, D_MODEL), BF16)),
        compiler_params=_cparams("parallel"),
    )(x, ys, ya, gs, ga, wout, gf)


def _mix_bwd(dy, dhn2_parts, h1, ys, ya, gs, ga, wout, gf, tm=512):
    n = dy.shape[0]
    n_parts = dhn2_parts.shape[0]

    def body(dy_ref, dp_ref, h1_ref, ys_ref, ya_ref, gs_ref, ga_ref, w_ref, gf_ref,
             dh1_ref, dys_ref, dya_ref, dgs_ref, dga_ref, dgf_ref):
        @pl.when(pl.program_id(0) == 0)
        def _():
            dgs_ref[...] = jnp.zeros_like(dgs_ref)
            dga_ref[...] = jnp.zeros_like(dga_ref)
            dgf_ref[...] = jnp.zeros_like(dgf_ref)
        dhn2 = dp_ref[0]
        for p in range(1, n_parts):
            dhn2 = dhn2 + dp_ref[p]
        _, vjp_f = jax.vjp(_rms, h1_ref[...], gf_ref[...])
        dh1n, dgf = vjp_f(dhn2)
        dh1 = dy_ref[...] + dh1n
        dh1_ref[...] = dh1
        dmixed = _dot_nt(dh1.astype(BF16), w_ref[...])
        _, vjp_s = jax.vjp(_rms, ys_ref[...], gs_ref[...])
        dys, dgs = vjp_s(dmixed[:, 0:512])
        _, vjp_a = jax.vjp(_rms, ya_ref[...], ga_ref[...])
        dya, dga = vjp_a(dmixed[:, 512:1024])
        dys_ref[...] = dys
        dya_ref[...] = dya
        dgs_ref[...] += dgs
        dga_ref[...] += dga
        dgf_ref[...] += dgf

    row = lambda w: pl.BlockSpec((tm, w), lambda i: (i, 0))
    full = lambda a: pl.BlockSpec(a.shape, lambda i: (0,) * a.ndim)
    vec = lambda w: pl.BlockSpec((1, w), lambda i: (0, 0))
    return pl.pallas_call(
        body, name="mix_bwd", grid=(n // tm,),
        in_specs=[row(D_MODEL), pl.BlockSpec((n_parts, tm, D_MODEL), lambda i: (0, i, 0)), row(D_MODEL), row(512),
                  row(512), full(gs), full(ga), full(wout), full(gf)],
        out_specs=(row(D_MODEL), row(512), row(512), vec(512), vec(512), vec(D_MODEL)),
        out_shape=(jax.ShapeDtypeStruct((n, D_MODEL), F32), jax.ShapeDtypeStruct((n, 512), F32),
                   jax.ShapeDtypeStruct((n, 512), F32), jax.ShapeDtypeStruct((1, 512), F32),
                   jax.ShapeDtypeStruct((1, 512), F32), jax.ShapeDtypeStruct((1, D_MODEL), F32)),
        compiler_params=_cparams("arbitrary"),
    )(dy, dhn2_parts, h1, ys, ya, gs, ga, wout, gf)


HALO = 16
FFN_GROUPS = 4
FFN_GROUP = D_FF // FFN_GROUPS


def _conv3(ue, cw):
    return cw[2:3] * ue + cw[1:2] * pltpu.roll(ue, 1, 0) + cw[0:1] * pltpu.roll(ue, 2, 0) + cw[3:4]


def _ffn_weight_specs():
    gate = lambda i, j: (j, 0, 0)
    val = lambda i, j: (j + FFN_GROUPS, 0, 0)
    w_blk, c_blk = (1, D_MODEL, FFN_GROUP), (1, SUBLANES, FFN_GROUP)
    return [pl.BlockSpec(w_blk, gate), pl.BlockSpec(w_blk, val), pl.BlockSpec(c_blk, gate), pl.BlockSpec(c_blk, val),
            pl.BlockSpec((1, FFN_GROUP, D_MODEL), gate)]


def _ffn_fwd(hn2, h1, target, w_up, conv, w_down, seq_len, tm=512):
    n = hn2.shape[0]
    nj = FFN_GROUPS
    hb = tm // HALO

    def body(hn_ref, halo_ref, h1_ref, tgt_ref, wg_ref, wv_ref, cg_ref, cv_ref, wd_ref,
             ug_ref, uv_ref, dy_ref, loss_ref, acc):
        i, j = pl.program_id(0), pl.program_id(1)
        seq_start = (i * tm) % seq_len == 0
        halo = halo_ref[...]
        halo = jnp.where(seq_start, jnp.zeros_like(halo), halo)
        he = jnp.concatenate([halo, hn_ref[...]], axis=0)
        ueg = _dot(he, wg_ref[0])
        uev = _dot(he, wv_ref[0])
        ug_ref[0] = ueg[HALO:].astype(BF16)
        uv_ref[0] = uev[HALO:].astype(BF16)
        cg = _conv3(ueg, cg_ref[0])[HALO:]
        cv = _conv3(uev, cv_ref[0])[HALO:]
        act = (jax.nn.silu(cg) * cv).astype(BF16)
        part = _dot(act, wd_ref[0])

        @pl.when(j == 0)
        def _():
            acc[...] = part

        @pl.when(j > 0)
        def _():
            acc[...] += part

        @pl.when(j == nj - 1)
        def _():
            err = h1_ref[...] + acc[...] - tgt_ref[...]
            dy_ref[...] = err * (1.0 / D_MODEL)
            loss_ref[0] = jnp.sum(err * err, axis=0, keepdims=True)

    row = pl.BlockSpec((tm, D_MODEL), lambda i, j: (i, 0))
    u_main = pl.BlockSpec((1, tm, FFN_GROUP), lambda i, j: (j, i, 0))
    u_shape = jax.ShapeDtypeStruct((FFN_GROUPS, n, FFN_GROUP), BF16)
    return pl.pallas_call(
        body, name="ffn_fwd", grid=(n // tm, nj),
        in_specs=[row, pl.BlockSpec((HALO, D_MODEL), lambda i, j: (jnp.maximum(i * hb - 1, 0), 0)), row, row,
                  *_ffn_weight_specs()],
        out_specs=(u_main, u_main, row, pl.BlockSpec((1, 1, D_MODEL), lambda i, j: (i, 0, 0))),
        out_shape=(u_shape, u_shape, jax.ShapeDtypeStruct((n, D_MODEL), F32),
                   jax.ShapeDtypeStruct((n // tm, 1, D_MODEL), F32)),
        scratch_shapes=[pltpu.VMEM((tm, D_MODEL), F32)],
        compiler_params=_cparams("parallel", "arbitrary"),
    )(hn2, hn2, h1, target, w_up, w_up, conv, conv, w_down)


def _ffn_bwd(dy, ug, uv, w_up, conv, w_down, seq_len, tm=512):
    n = dy.shape[0]
    nj = FFN_GROUPS
    fb = FFN_GROUP
    hb = tm // HALO
    last_hb = n // HALO - 1
    rows = tm + HALO

    def body(dy_ref, dyn_ref, ugp_ref, ugm_ref, ugn_ref, uvp_ref, uvm_ref, uvn_ref, wg_ref, wv_ref, cg_ref, cv_ref,
             wd_ref, dug_ref, duv_ref, act_ref, dhn_ref, dcg_ref, dcv_ref, acc):
        i, j = pl.program_id(0), pl.program_id(1)
        seq_start = (i * tm) % seq_len == 0
        seq_end = ((i + 1) * tm) % seq_len == 0
        dyn = dyn_ref[...]
        dyn = jnp.where(seq_end, jnp.zeros_like(dyn), dyn)
        d_out = jnp.concatenate([dy_ref[...], dyn], axis=0).astype(BF16)
        d_act = _dot_nt(d_out, wd_ref[0])

        def pre_act(up_ref, um_ref, un_ref, cw):
            up = up_ref[0]
            up = jnp.where(seq_start, jnp.zeros_like(up), up)
            ue = jnp.concatenate([up, um_ref[0], un_ref[0]], axis=0).astype(F32)
            return ue, _conv3(ue, cw)[HALO:]

        cwg, cwv = cg_ref[0], cv_ref[0]
        ueg, cge = pre_act(ugp_ref, ugm_ref, ugn_ref, cwg)
        uev, cve = pre_act(uvp_ref, uvm_ref, uvn_ref, cwv)
        act, vjp_act = jax.vjp(lambda g, v: jax.nn.silu(g) * v, cge, cve)
        dcge, dcve = vjp_act(d_act)
        act_ref[0] = act[:tm].astype(BF16)

        def conv_t(dc, cw):
            return (cw[2:3] * dc + cw[1:2] * pltpu.roll(dc, rows - 1, 0) + cw[0:1] * pltpu.roll(dc, rows - 2, 0))[:tm]

        dug = conv_t(dcge, cwg).astype(BF16)
        duv = conv_t(dcve, cwv).astype(BF16)
        dug_ref[0] = dug
        duv_ref[0] = duv
        part = _dot_nt(dug, wg_ref[0]) + _dot_nt(duv, wv_ref[0])

        @pl.when(j == 0)
        def _():
            acc[...] = part

        @pl.when(j > 0)
        def _():
            acc[...] += part

        @pl.when(j == nj - 1)
        def _():
            dhn_ref[...] = acc[...]

        def cw_grad(dc, ue):
            dcm = dc[:tm]
            taps = [jnp.sum(dcm * pltpu.roll(ue, 2 - k, 0)[HALO:HALO + tm], axis=0, keepdims=True) for k in (0, 1)]
            taps.append(jnp.sum(dcm * ue[HALO:HALO + tm], axis=0, keepdims=True))
            taps.append(jnp.sum(dcm, axis=0, keepdims=True))
            return jnp.concatenate(taps + [jnp.zeros((4, fb), F32)], axis=0)

        @pl.when(i == 0)
        def _():
            dcg_ref[j] = jnp.zeros((8, fb), F32)
            dcv_ref[j] = jnp.zeros((8, fb), F32)

        dcg_ref[j] += cw_grad(dcge, ueg)
        dcv_ref[j] += cw_grad(dcve, uev)

    row = pl.BlockSpec((tm, D_MODEL), lambda i, j: (i, 0))
    u_prev = pl.BlockSpec((1, HALO, fb), lambda i, j: (j, jnp.maximum(i * hb - 1, 0), 0))
    u_main = pl.BlockSpec((1, tm, fb), lambda i, j: (j, i, 0))
    u_next = pl.BlockSpec((1, HALO, fb), lambda i, j: (j, jnp.minimum((i + 1) * hb, last_hb), 0))
    dc_spec = pl.BlockSpec((nj, 8, fb), lambda i, j: (0, 0, 0))
    u_shape = jax.ShapeDtypeStruct((FFN_GROUPS, n, fb), BF16)
    return pl.pallas_call(
        body, name="ffn_bwd", grid=(n // tm, nj),
        in_specs=[row, pl.BlockSpec((HALO, D_MODEL), lambda i, j: (jnp.minimum((i + 1) * hb, last_hb), 0)),
                  u_prev, u_main, u_next, u_prev, u_main, u_next, *_ffn_weight_specs()],
        out_specs=(u_main, u_main, u_main, row, dc_spec, dc_spec),
        out_shape=(u_shape, u_shape, u_shape, jax.ShapeDtypeStruct((n, D_MODEL), F32),
                   jax.ShapeDtypeStruct((nj, 8, fb), F32), jax.ShapeDtypeStruct((nj, 8, fb), F32)),
        scratch_shapes=[pltpu.VMEM((tm, D_MODEL), F32)],
        compiler_params=_cparams("arbitrary", "arbitrary"),
    )(dy, dy, ug, ug, ug, uv, uv, uv, w_up, w_up, conv, conv, w_down)


def _tn_grouped(a, b, name, shared_a, out_dtype=F32, tn=512):
    groups = b.shape[0] if shared_a else a.shape[0]
    n_tok = a.shape[0] if shared_a else b.shape[0]
    k_dim, m_dim = a.shape[-1], b.shape[-1]

    def body(a_ref, b_ref, o_ref, acc):
        k = pl.program_id(1)
        a_t = a_ref[...] if shared_a else a_ref[0]
        b_t = b_ref[0] if shared_a else b_ref[...]
        part = _dot_tn(a_t.astype(BF16), b_t.astype(BF16))

        @pl.when(k == 0)
        def _():
            acc[...] = part

        @pl.when(k > 0)
        def _():
            acc[...] += part

        @pl.when(k == n_tok // tn - 1)
        def _():
            o_ref[0] = acc[...].astype(out_dtype)

    plain = lambda w: pl.BlockSpec((tn, w), lambda g, k: (k, 0))
    grouped = lambda w: pl.BlockSpec((1, tn, w), lambda g, k: (g, k, 0))
    return pl.pallas_call(
        body, name=name, grid=(groups, n_tok // tn),
        in_specs=[plain(k_dim), grouped(m_dim)] if shared_a else [grouped(k_dim), plain(m_dim)],
        out_specs=pl.BlockSpec((1, k_dim, m_dim), lambda g, k: (g, 0, 0)),
        out_shape=jax.ShapeDtypeStruct((groups, k_dim, m_dim), out_dtype),
        scratch_shapes=[pltpu.VMEM((k_dim, m_dim), F32)],
        compiler_params=_cparams("parallel", "arbitrary"),
    )(a, b)


def _s5_param_fn(lr, li, ldt, br, bi):
    dt = jnp.exp(ldt)
    mag = jnp.exp(lr * dt)
    ab_re = mag * jnp.cos(li * dt)
    ab_im = mag * jnp.sin(li * dt)
    nr = ab_re - 1.0
    ni = ab_im
    den = lr * lr + li * li
    q_re = (nr * lr + ni * li) / den
    q_im = (ni * lr - nr * li) / den
    bb_re = q_re * br - q_im * bi
    bb_im = q_re * bi + q_im * br
    return ab_re, ab_im, bb_re, bb_im


def _s5_param_fwd(lr, li, ldt, br, bi):
    def body(lr_ref, li_ref, ldt_ref, br_ref, bi_ref, ar_ref, ai_ref, bbr_ref, bbi_ref):
        ar, ai, bbr, bbi = _s5_param_fn(lr_ref[...], li_ref[...], ldt_ref[...], br_ref[...], bi_ref[...])
        ar_ref[...] = ar
        ai_ref[...] = ai
        bbr_ref[...] = bbr
        bbi_ref[...] = bbi

    return pl.pallas_call(
        body, name="s5_param_fwd",
        out_shape=(jax.ShapeDtypeStruct(lr.shape, F32), jax.ShapeDtypeStruct(lr.shape, F32),
                   jax.ShapeDtypeStruct(br.shape, F32), jax.ShapeDtypeStruct(br.shape, F32)),
    )(lr, li, ldt, br, bi)


def _s5_param_bwd(lr, li, ldt, br, bi, dar, dai, dbbr, dbbi):
    def body(lr_ref, li_ref, ldt_ref, br_ref, bi_ref, dar_ref, dai_ref, dbbr_ref, dbbi_ref,
             dlr_ref, dli_ref, dldt_ref, dbr_ref, dbi_ref):
        _, vjp = jax.vjp(_s5_param_fn, lr_ref[...], li_ref[...], ldt_ref[...], br_ref[...], bi_ref[...])
        dlr, dli, dldt, dbr, dbi = vjp((dar_ref[...], dai_ref[...], dbbr_ref[...], dbbi_ref[...]))
        dlr_ref[...] = dlr
        dli_ref[...] = dli
        dldt_ref[...] = dldt
        dbr_ref[...] = dbr
        dbi_ref[...] = dbi

    return pl.pallas_call(
        body, name="s5_param_bwd",
        out_shape=(jax.ShapeDtypeStruct(lr.shape, F32), jax.ShapeDtypeStruct(lr.shape, F32),
                   jax.ShapeDtypeStruct(ldt.shape, F32), jax.ShapeDtypeStruct(br.shape, F32),
                   jax.ShapeDtypeStruct(br.shape, F32)),
    )(lr, li, ldt, br, bi, dar, dai, dbbr, dbbi)


S5_CHUNK = 256
S5_STATES = 512
S5_BLOCKS = 4


def _cpow_rows(ar, ai, count):
    rs, im = [ar], [ai]
    for _ in range(count - 1):
        pr, pi = rs[-1], im[-1]
        rs.append(pr * ar - pi * ai)
        im.append(pr * ai + pi * ar)
    return rs, im


def _scan_in_groups(vr, vi, pr, pi, rm, reverse):
    n, width = vr.shape
    vr = vr.reshape(n // SUBLANES, SUBLANES, width)
    vi = vi.reshape(n // SUBLANES, SUBLANES, width)
    row = rm[0:SUBLANES]
    for k in (1, 2, 4):
        shift = SUBLANES - k if reverse else k
        keep = row < SUBLANES - k if reverse else row >= k
        kr = jnp.where(keep, pr[k - 1], 0.0)
        ki = jnp.where(keep, pi[k - 1], 0.0)
        sr, si = pltpu.roll(vr, shift, 1), pltpu.roll(vi, shift, 1)
        vr, vi = vr + kr * sr - ki * si, vi + kr * si + ki * sr
    return vr.reshape(n, width), vi.reshape(n, width)


def _carry_over_groups(xr_s, xi_s, wr, wi, c0r, c0i, reverse):
    groups = xr_s.shape[0] // SUBLANES
    pick = 0 if reverse else SUBLANES - 1

    def step(q, carry):
        cr, ci = carry
        r = groups - 1 - q if reverse else q
        o = pl.multiple_of(r * SUBLANES, SUBLANES)
        vr = xr_s[pl.ds(o, SUBLANES), :]
        vi = xi_s[pl.ds(o, SUBLANES), :]
        nr = vr + wr * cr - wi * ci
        ni = vi + wr * ci + wi * cr
        xr_s[pl.ds(o, SUBLANES), :] = nr
        xi_s[pl.ds(o, SUBLANES), :] = ni
        return (jnp.broadcast_to(nr[pick:pick + 1], nr.shape), jnp.broadcast_to(ni[pick:pick + 1], ni.shape))

    return lax.fori_loop(0, groups, step, (c0r, c0i))


def _s5_state_scan(u_b, bbr, bbi, pr, pi, rm, xr_s, xi_s, c0r, c0i):
    bur = _dot(u_b, bbr)
    bui = _dot(u_b, bbi)
    bur, bui = _scan_in_groups(bur, bui, pr, pi, rm, False)
    xr_s[...] = bur
    xi_s[...] = bui
    w8r = jnp.concatenate(pr, axis=0)
    w8i = jnp.concatenate(pi, axis=0)
    return _carry_over_groups(xr_s, xi_s, w8r, w8i, c0r, c0i, False)


def _s5_fwd(u, a_re, a_im, bbr, bbi, cr, ci, d_skip, n_seq):
    n = u.shape[0]
    seq_len = n // n_seq
    nt = seq_len // S5_CHUNK
    tc = S5_CHUNK

    def body(u_ref, ar_ref, ai_ref, bbr_ref, bbi_ref, cr_ref, ci_ref, d_ref, y_ref, str_ref, sti_ref,
             xr_s, xi_s, car_r, car_i):
        t = pl.program_id(2)

        @pl.when(t == 0)
        def _():
            car_r[...] = jnp.zeros_like(car_r)
            car_i[...] = jnp.zeros_like(car_i)
        pr, pi = _cpow_rows(ar_ref[0], ai_ref[0], SUBLANES)
        rm = lax.broadcasted_iota(jnp.int32, (tc, S5_STATES), 0) & (SUBLANES - 1)
        str_ref[0, 0] = car_r[...]
        sti_ref[0, 0] = car_i[...]
        u_t = u_ref[...]
        cfr, cfi = _s5_state_scan(u_t.astype(BF16), bbr_ref[0], bbi_ref[0], pr, pi, rm, xr_s, xi_s,
                                  car_r[...], car_i[...])
        car_r[...] = cfr
        car_i[...] = cfi
        y = _dot(xr_s[...].astype(BF16), cr_ref[0]) - _dot(xi_s[...].astype(BF16), ci_ref[0])
        y_ref[...] = y + d_ref[...] * u_t

    u_spec = pl.BlockSpec((tc, LANES), lambda cb, b, t: (b * nt + t, cb))
    a_spec = pl.BlockSpec((1, 1, S5_STATES), lambda cb, b, t: (cb, 0, 0))
    bb_spec = pl.BlockSpec((1, LANES, S5_STATES), lambda cb, b, t: (cb, 0, 0))
    c_spec = pl.BlockSpec((1, S5_STATES, LANES), lambda cb, b, t: (cb, 0, 0))
    st_spec = pl.BlockSpec((1, 1, SUBLANES, S5_STATES), lambda cb, b, t: (cb, b * nt + t, 0, 0))
    st_shape = jax.ShapeDtypeStruct((S5_BLOCKS, n_seq * nt, SUBLANES, S5_STATES), F32)
    return pl.pallas_call(
        body, name="s5_fwd", grid=(S5_BLOCKS, n_seq, nt),
        in_specs=[u_spec, a_spec, a_spec, bb_spec, bb_spec, c_spec, c_spec,
                  pl.BlockSpec((1, LANES), lambda cb, b, t: (0, cb))],
        out_specs=(u_spec, st_spec, st_spec),
        out_shape=(jax.ShapeDtypeStruct((n, D_SSM), F32), st_shape, st_shape),
        scratch_shapes=[pltpu.VMEM((tc, S5_STATES), F32), pltpu.VMEM((tc, S5_STATES), F32),
                        pltpu.VMEM((SUBLANES, S5_STATES), F32), pltpu.VMEM((SUBLANES, S5_STATES), F32)],
        compiler_params=_cparams("parallel", "arbitrary", "arbitrary"),
    )(u, a_re, a_im, bbr, bbi, cr, ci, d_skip)


def _s5_bwd(u, dy, st_r, st_i, a_re, a_im, bbr, bbi, cr, ci, d_skip, n_seq):
    n = u.shape[0]
    seq_len = n // n_seq
    nt = seq_len // S5_CHUNK
    tc = S5_CHUNK

    def body(u_ref, dy_ref, str_ref, sti_ref, ar_ref, ai_ref, bbr_ref, bbi_ref, cr_ref, ci_ref, d_ref,
             du_ref, dbbr_ref, dbbi_ref, dcr_ref, dci_ref, dar_ref, dai_ref, dd_ref,
             xr_s, xi_s, gr_s, gi_s, car_r, car_i):
        b, t = pl.program_id(1), pl.program_id(2)

        @pl.when((b == 0) & (t == 0))
        def _():
            for ref in (dbbr_ref, dbbi_ref, dcr_ref, dci_ref, dar_ref, dai_ref, dd_ref):
                ref[...] = jnp.zeros_like(ref)

        @pl.when(t == 0)
        def _():
            car_r[...] = jnp.zeros_like(car_r)
            car_i[...] = jnp.zeros_like(car_i)
        ar, ai = ar_ref[0], ai_ref[0]
        pr, pi = _cpow_rows(ar, ai, SUBLANES)
        row = lax.broadcasted_iota(jnp.int32, (tc, S5_STATES), 0)
        rm = row & (SUBLANES - 1)
        u_t = u_ref[...]
        u_b = u_t.astype(BF16)
        dy_t = dy_ref[...]
        dy_b = dy_t.astype(BF16)
        s0r, s0i = str_ref[0, 0], sti_ref[0, 0]
        _s5_state_scan(u_b, bbr_ref[0], bbi_ref[0], pr, pi, rm, xr_s, xi_s, s0r, s0i)
        xr, xi = xr_s[...], xi_s[...]
        gr = _dot_nt(dy_b, cr_ref[0])
        gi = -_dot_nt(dy_b, ci_ref[0])
        npi = [-v for v in pi]
        gr, gi = _scan_in_groups(gr, gi, pr, npi, rm, True)
        gr_s[...] = gr
        gi_s[...] = gi
        w8r = jnp.concatenate(pr[::-1], axis=0)
        w8i = jnp.concatenate(npi[::-1], axis=0)
        cfr, cfi = _carry_over_groups(gr_s, gi_s, w8r, w8i, car_r[...], car_i[...], True)
        car_r[...] = cfr
        car_i[...] = cfi
        gr, gi = gr_s[...], gi_s[...]
        gr_b, gi_b = gr.astype(BF16), gi.astype(BF16)
        du_ref[...] = _dot_nt(gr_b, bbr_ref[0]) + _dot_nt(gi_b, bbi_ref[0]) + d_ref[...] * dy_t
        dbbr_ref[0] += _dot_tn(u_b, gr_b)
        dbbi_ref[0] += _dot_tn(u_b, gi_b)
        dcr_ref[0] += _dot_tn(xr.astype(BF16), dy_b)
        dci_ref[0] -= _dot_tn(xi.astype(BF16), dy_b)
        dd_ref[0] += jnp.sum((dy_t * u_t).reshape(tc // SUBLANES, SUBLANES, LANES), axis=0)
        first = row == 0
        xpr = jnp.where(first, jnp.broadcast_to(s0r[0:1], xr.shape), pltpu.roll(xr, 1, 0))
        xpi = jnp.where(first, jnp.broadcast_to(s0i[0:1], xi.shape), pltpu.roll(xi, 1, 0))
        shp = (tc // SUBLANES, SUBLANES, S5_STATES)
        dar_ref[0] += jnp.sum((gr * xpr + gi * xpi).reshape(shp), axis=0)
        dai_ref[0] += jnp.sum((gi * xpr - gr * xpi).reshape(shp), axis=0)

    u_spec = pl.BlockSpec((tc, LANES), lambda cb, b, t: (b * nt + nt - 1 - t, cb))
    a_spec = pl.BlockSpec((1, 1, S5_STATES), lambda cb, b, t: (cb, 0, 0))
    bb_spec = pl.BlockSpec((1, LANES, S5_STATES), lambda cb, b, t: (cb, 0, 0))
    c_spec = pl.BlockSpec((1, S5_STATES, LANES), lambda cb, b, t: (cb, 0, 0))
    st_spec = pl.BlockSpec((1, 1, SUBLANES, S5_STATES), lambda cb, b, t: (cb, b * nt + nt - 1 - t, 0, 0))
    da_spec = pl.BlockSpec((1, SUBLANES, S5_STATES), lambda cb, b, t: (cb, 0, 0))
    dd_spec = pl.BlockSpec((1, SUBLANES, LANES), lambda cb, b, t: (cb, 0, 0))
    big = pltpu.VMEM((tc, S5_STATES), F32)
    small = pltpu.VMEM((SUBLANES, S5_STATES), F32)
    return pl.pallas_call(
        body, name="s5_bwd", grid=(S5_BLOCKS, n_seq, nt),
        in_specs=[u_spec, u_spec, st_spec, st_spec, a_spec, a_spec, bb_spec, bb_spec, c_spec, c_spec,
                  pl.BlockSpec((1, LANES), lambda cb, b, t: (0, cb))],
        out_specs=(u_spec, bb_spec, bb_spec, c_spec, c_spec, da_spec, da_spec, dd_spec),
        out_shape=(jax.ShapeDtypeStruct((n, D_SSM), F32),
                   jax.ShapeDtypeStruct((S5_BLOCKS, LANES, S5_STATES), F32),
                   jax.ShapeDtypeStruct((S5_BLOCKS, LANES, S5_STATES), F32),
                   jax.ShapeDtypeStruct((S5_BLOCKS, S5_STATES, LANES), F32),
                   jax.ShapeDtypeStruct((S5_BLOCKS, S5_STATES, LANES), F32),
                   jax.ShapeDtypeStruct((S5_BLOCKS, SUBLANES, S5_STATES), F32),
                   jax.ShapeDtypeStruct((S5_BLOCKS, SUBLANES, S5_STATES), F32),
                   jax.ShapeDtypeStruct((S5_BLOCKS, SUBLANES, LANES), F32)),
        scratch_shapes=[big, big, big, big, small, small],
        compiler_params=_cparams("parallel", "arbitrary", "arbitrary"),
    )(u, dy, st_r, st_i, a_re, a_im, bbr, bbi, cr, ci, d_skip)


CUM_BLOCK = 128


def _tri(lower):
    r = lax.broadcasted_iota(jnp.int32, (CUM_BLOCK, CUM_BLOCK), 0)
    c = lax.broadcasted_iota(jnp.int32, (CUM_BLOCK, CUM_BLOCK), 1)
    return jnp.where(r >= c if lower else r <= c, 1.0, 0.0).astype(F32)


def _fprep_fwd(fl, bf, n_seq):
    n = fl.shape[0]
    seq_len = n // n_seq
    nb = seq_len // CUM_BLOCK

    def body(fl_ref, bf_ref, cum_ref):
        tril = _tri(True)
        carry = jnp.zeros((1, LANES), F32)
        for blk in range(nb):
            rows = slice(blk * CUM_BLOCK, (blk + 1) * CUM_BLOCK)
            lf = jax.nn.log_sigmoid(fl_ref[rows, :] + bf_ref[...])
            cs = jnp.dot(tril, lf, preferred_element_type=F32, precision=HIGHEST) + carry
            cum_ref[rows, :] = cs
            carry = cs[CUM_BLOCK - 1:CUM_BLOCK, :]

    spec = pl.BlockSpec((seq_len, LANES), lambda b: (b, 0))
    return pl.pallas_call(
        body, name="fprep_fwd", grid=(n_seq,), in_specs=[spec, pl.BlockSpec((1, LANES), lambda b: (0, 0))],
        out_specs=spec, out_shape=jax.ShapeDtypeStruct((n, LANES), F32), compiler_params=_cparams("parallel"),
    )(fl, bf)


def _fprep_bwd(dcum, fl, bf, n_seq):
    n = fl.shape[0]
    seq_len = n // n_seq
    nb = seq_len // CUM_BLOCK

    def body(dcum_ref, fl_ref, bf_ref, dfl_ref, dbf_ref):
        triu = _tri(False)
        lane = lax.broadcasted_iota(jnp.int32, (CUM_BLOCK, LANES), 1)
        carry = jnp.zeros((1, LANES), F32)
        total = jnp.zeros((1, LANES), F32)
        for blk in reversed(range(nb)):
            rows = slice(blk * CUM_BLOCK, (blk + 1) * CUM_BLOCK)
            rs = jnp.dot(triu, dcum_ref[rows, :], preferred_element_type=F32, precision=HIGHEST) + carry
            carry = rs[0:1, :]
            _, vjp = jax.vjp(jax.nn.log_sigmoid, fl_ref[rows, :] + bf_ref[...])
            dz = jnp.where(lane < N_HEADS, vjp(rs)[0], 0.0)
            dfl_ref[rows, :] = dz
            total = total + jnp.sum(dz, axis=0, keepdims=True)
        dbf_ref[0] = total

    spec = pl.BlockSpec((seq_len, LANES), lambda b: (b, 0))
    return pl.pallas_call(
        body, name="fprep_bwd", grid=(n_seq,), in_specs=[spec, spec, pl.BlockSpec((1, LANES), lambda b: (0, 0))],
        out_specs=(spec, pl.BlockSpec((1, 1, LANES), lambda b: (b, 0, 0))),
        out_shape=(jax.ShapeDtypeStruct((n, LANES), F32), jax.ShapeDtypeStruct((n_seq, 1, LANES), F32)),
        compiler_params=_cparams("parallel"),
    )(dcum, fl, bf)


ATT_TQ = 256
ATT_KSTEP = 256
ATT_SCALE = HEAD_DIM ** -0.5
NEG_BIG = -1e30


def _attn_scores(qe, kb, cq, ck, causal):
    s = _dot_nt(qe, kb) * ATT_SCALE + cq - ck
    return jnp.where(causal, s, NEG_BIG)


def _causal_mask(qi, kend):
    r = lax.broadcasted_iota(jnp.int32, (ATT_TQ, kend), 0) + qi * ATT_TQ
    c = lax.broadcasted_iota(jnp.int32, (ATT_TQ, kend), 1)
    return r >= c


def _attn_specs(n_seq, seq_len):
    nq = seq_len // ATT_TQ
    q_spec = pl.BlockSpec((ATT_TQ, LANES), lambda b, h, q: (b * nq + q, h))
    k_spec = pl.BlockSpec((seq_len, LANES), lambda b, h, q: (b, N_HEADS // 2 + h))
    v_spec = pl.BlockSpec((seq_len, LANES), lambda b, h, q: (b, N_HEADS + h))
    cq_spec = pl.BlockSpec((1, 2, ATT_TQ, 1), lambda b, h, q: (b, h, q, 0))
    ck_spec = pl.BlockSpec((1, 2, 1, seq_len), lambda b, h, q: (b, h, 0, 0))
    return nq, q_spec, k_spec, v_spec, cq_spec, ck_spec


def _head_selectors():
    head0 = lax.broadcasted_iota(jnp.int32, (1, LANES), 1) < HEAD_DIM
    return head0, (head0, jnp.logical_not(head0))


def _for_key_range(qi, seq_len, run):
    per = ATT_KSTEP // ATT_TQ
    for g in range(seq_len // ATT_KSTEP):
        pl.when(qi // per == g)(functools.partial(run, (g + 1) * ATT_KSTEP))


def _attn_fwd(qkv, cq, ck, n_seq):
    n = qkv.shape[0]
    seq_len = n // n_seq
    nq, q_spec, k_spec, v_spec, cq_spec, ck_spec = _attn_specs(n_seq, seq_len)

    def body(q_ref, k_ref, v_ref, cq_ref, ck_ref, o_ref, lse_ref):
        qi = pl.program_id(2)
        q2 = q_ref[...]
        head0, sels = _head_selectors()
        qe = [jnp.where(sel, q2, 0.0).astype(BF16) for sel in sels]

        def run(kend):
            kb = k_ref[0:kend, :].astype(BF16)
            vb = v_ref[0:kend, :].astype(BF16)
            causal = _causal_mask(qi, kend)
            outs = []
            for e in range(2):
                s = _attn_scores(qe[e], kb, cq_ref[0, e], ck_ref[0, e, :, 0:kend], causal)
                mx = jnp.max(s, axis=1, keepdims=True)
                p = jnp.exp(s - mx)
                den = jnp.sum(p, axis=1, keepdims=True)
                outs.append(_dot(p.astype(BF16), vb) / den)
                lse_ref[0, e] = mx + jnp.log(den)
            o_ref[...] = jnp.where(head0, outs[0], outs[1])

        _for_key_range(qi, seq_len, run)

    return pl.pallas_call(
        body, name="attn_fwd", grid=(n_seq, N_HEADS // 2, nq),
        in_specs=[q_spec, k_spec, v_spec, cq_spec, ck_spec],
        out_specs=(q_spec, cq_spec),
        out_shape=(jax.ShapeDtypeStruct((n, D_ATTN), F32), jax.ShapeDtypeStruct((n_seq, N_HEADS, seq_len, 1), F32)),
        compiler_params=_cparams("parallel", "parallel", "parallel"),
    )(qkv, qkv, qkv, cq, ck)


def _attn_bwd(qkv, cq, ck, o, do, lse, n_seq):
    n = qkv.shape[0]
    seq_len = n // n_seq
    nq, q_spec, k_spec, v_spec, cq_spec, ck_spec = _attn_specs(n_seq, seq_len)
    kv_out = pl.BlockSpec((seq_len, LANES), lambda b, h, q: (b, h))

    def body(q_ref, k_ref, v_ref, cq_ref, ck_ref, o_ref, do_ref, lse_ref, dq_ref, dk_ref, dv_ref, dcq_ref, dck_ref):
        qi = pl.program_id(2)

        @pl.when(qi == 0)
        def _():
            dk_ref[...] = jnp.zeros_like(dk_ref)
            dv_ref[...] = jnp.zeros_like(dv_ref)
            dck_ref[...] = jnp.zeros_like(dck_ref)
        q2 = q_ref[...]
        do2 = do_ref[...]
        o2 = o_ref[...]
        head0, sels = _head_selectors()
        qe = [jnp.where(sel, q2, 0.0).astype(BF16) for sel in sels]
        doe = [jnp.where(sel, do2, 0.0) for sel in sels]
        doe_b = [d.astype(BF16) for d in doe]
        delta = [jnp.sum(d * o2, axis=1, keepdims=True) for d in doe]

        def run(kend):
            kb = k_ref[0:kend, :].astype(BF16)
            vb = v_ref[0:kend, :].astype(BF16)
            causal = _causal_mask(qi, kend)
            dqs = []
            dk = jnp.zeros((kend, LANES), F32)
            dv = jnp.zeros((kend, LANES), F32)
            for e in range(2):
                s = _attn_scores(qe[e], kb, cq_ref[0, e], ck_ref[0, e, :, 0:kend], causal)
                p = jnp.exp(s - lse_ref[0, e])
                ds = p * (_dot_nt(doe_b[e], vb) - delta[e])
                ds_b = ds.astype(BF16)
                dqs.append(_dot(ds_b, kb))
                dk = dk + _dot_tn(ds_b, qe[e])
                dv = dv + _dot_tn(p.astype(BF16), doe_b[e])
                dcq_ref[0, e] = jnp.sum(ds, axis=1, keepdims=True)
                dck_ref[0, e, :, 0:kend] -= jnp.sum(ds, axis=0, keepdims=True)
            dk_ref[0:kend, :] += dk * ATT_SCALE
            dv_ref[0:kend, :] += dv
            dq_ref[...] = jnp.where(head0, dqs[0], dqs[1]) * ATT_SCALE

        _for_key_range(qi, seq_len, run)

    return pl.pallas_call(
        body, name="attn_bwd", grid=(n_seq, N_HEADS // 2, nq),
        in_specs=[q_spec, k_spec, v_spec, cq_spec, ck_spec, q_spec, q_spec, cq_spec],
        out_specs=(q_spec, kv_out, kv_out, cq_spec, ck_spec),
        out_shape=(jax.ShapeDtypeStruct((n, D_ATTN), F32), jax.ShapeDtypeStruct((n, D_ATTN), F32),
                   jax.ShapeDtypeStruct((n, D_ATTN), F32),
                   jax.ShapeDtypeStruct((n_seq, N_HEADS, seq_len, 1), F32),
                   jax.ShapeDtypeStruct((n_seq, N_HEADS, 1, seq_len), F32)),
        compiler_params=_cparams("parallel", "parallel", "arbitrary"),
    )(qkv, qkv, qkv, cq, ck, o, do, lse)


WEIGHT_NAMES = ("norm_mix", "w_in", "b_forget", "lam_re", "lam_im", "b_re", "b_im", "c_re", "c_im", "d_skip", "log_dt",
                "w_glu", "b_glu", "q_norm", "k_norm", "norm_out_ssm", "norm_out_attn", "w_out", "norm_ffn", "w_up",
                "conv_w", "conv_b", "w_down")
SHARDED = ("w_in", "w_glu", "w_out", "w_up", "conv_w", "w_down")
ADAM_ROWS = {"w_in": 256, "w_glu": 64, "w_out": 128, "w_up": 128, "conv_w": 3, "w_down": 344}
PACK_ROWS = SUBLANES * LANES
ROWS_DOWN = D_FF // N_DEV
ROWS_OUT = D_MODEL // N_DEV
ROWS_GLU = D_SSM * D_SSM // N_DEV // D_MODEL


def _pad_to(a, axis, size):
    pad = [(0, 0)] * a.ndim
    pad[axis] = (0, size - a.shape[axis])
    return jnp.pad(a, pad)


def _block_diag(t, transpose):
    t4 = t.reshape(S5_BLOCKS, 8, SSM_GROUP, SSM_STATE)
    eye = jnp.eye(8, dtype=t.dtype)
    if transpose:
        e = jnp.swapaxes(t4, 2, 3)[:, :, :, None, :] * eye[None, :, None, :, None]
        return e.reshape(S5_BLOCKS, S5_STATES, LANES)
    e = t4[:, :, :, None, :] * eye[None, :, None, :, None]
    return e.reshape(S5_BLOCKS, LANES, S5_STATES)


def _block_diag_extract(m, transpose):
    if transpose:
        m5 = m.reshape(S5_BLOCKS, 8, SSM_STATE, 8, SSM_GROUP)
        d = jnp.stack([m5[:, i, :, i, :] for i in range(8)], axis=1)
        return jnp.swapaxes(d, 2, 3).reshape(N_GROUPS, SSM_GROUP, SSM_STATE)
    m5 = m.reshape(S5_BLOCKS, 8, SSM_GROUP, 8, SSM_STATE)
    d = jnp.stack([m5[:, i, :, i, :] for i in range(8)], axis=1)
    return d.reshape(N_GROUPS, SSM_GROUP, SSM_STATE)


def _pack(pieces):
    flat = jnp.concatenate([p.reshape(-1).astype(F32) for p in pieces])
    size = -(-flat.shape[0] // PACK_ROWS) * PACK_ROWS
    return _pad_to(flat, 0, size).reshape(-1, LANES)


def _unpack(packed, shapes):
    flat = packed.reshape(-1)
    out, off = [], 0
    for shp in shapes:
        size = math.prod(shp)
        out.append(flat[off:off + size].reshape(shp))
        off += size
    return out


def kernel(x, norm_mix, w_in, b_forget, lam_re, lam_im, b_re, b_im, c_re, c_im, d_skip, log_dt, w_glu, b_glu, q_norm, k_norm, norm_out_ssm, norm_out_attn, w_out, norm_ffn, w_up, conv_w, conv_b, w_down, loss_target, m_norm_mix, m_w_in, m_b_forget, m_lam_re, m_lam_im, m_b_re, m_b_im, m_c_re, m_c_im, m_d_skip, m_log_dt, m_w_glu, m_b_glu, m_q_norm, m_k_norm, m_norm_out_ssm, m_norm_out_attn, m_w_out, m_norm_ffn, m_w_up, m_conv_w, m_conv_b, m_w_down, v_norm_mix, v_w_in, v_b_forget, v_lam_re, v_lam_im, v_b_re, v_b_im, v_c_re, v_c_im, v_d_skip, v_log_dt, v_w_glu, v_b_glu, v_q_norm, v_k_norm, v_norm_out_ssm, v_norm_out_attn, v_w_out, v_norm_ffn, v_w_up, v_conv_w, v_conv_b, v_w_down):
    given = dict(locals())
    weights = {k: given[k] for k in WEIGHT_NAMES}
    mom1 = {k: given["m_" + k] for k in WEIGHT_NAMES}
    mom2 = {k: given["v_" + k] for k in WEIGHT_NAMES}
    n_seq, seq_len, _ = x.shape
    n = n_seq * seq_len
    xf = x.reshape(n, D_MODEL)
    target = loss_target.reshape(n, D_MODEL)
    me_idx = 4 * lax.axis_index("x") + 2 * lax.axis_index("y") + lax.axis_index("c")

    g_in, g_cw = _exchange([w_in[0].astype(BF16), conv_w[0]], [False, False], "gather_w_in")
    rest_flags = [False] * 2
    rows_w = jnp.concatenate([w_down[0], w_out[0], w_glu[0].reshape(ROWS_GLU, D_MODEL)], axis=0).astype(BF16)
    w_sems = _exchange_start([rows_w, w_up[0].astype(BF16)], rest_flags, g_in, "gather_rest_start", 0)
    norm_mix = norm_mix + w_sems[4][0, 0]
    w_in_p = _pad_to(jnp.swapaxes(g_in, 0, 1).reshape(D_MODEL, D_IN), 1, D_IN_PAD)

    lr3 = lam_re[0].reshape(N_GROUPS, 1, SSM_STATE)
    li3 = lam_im[0].reshape(N_GROUPS, 1, SSM_STATE)
    ldt3 = log_dt[0].reshape(N_GROUPS, 1, 1)
    br_t = jnp.swapaxes(b_re[0], 1, 2)
    bi_t = jnp.swapaxes(b_im[0], 1, 2)
    ab_re, ab_im, bb_re, bb_im = _s5_param_fwd(lr3, li3, ldt3, br_t, bi_t)
    a_re = ab_re.reshape(S5_BLOCKS, 1, S5_STATES)
    a_im = ab_im.reshape(S5_BLOCKS, 1, S5_STATES)
    bbr = _block_diag(bb_re, False).astype(BF16)
    bbi = _block_diag(bb_im, False).astype(BF16)
    cr = _block_diag(c_re[0], True).astype(BF16)
    ci = _block_diag(c_im[0], True).astype(BF16)

    avg = jnp.kron(jnp.eye(N_HEADS, dtype=F32), jnp.full((HEAD_DIM, HEAD_DIM), 1.0 / HEAD_DIM, F32)).astype(BF16)
    qg = jnp.tile(q_norm, (1, N_HEADS))
    kg = jnp.tile(k_norm, (1, N_HEADS))
    hn, u, qkv, raw, fl = _inproj_fwd(xf, norm_mix, w_in_p, avg, qg, kg)
    yc, st_r, st_i = _s5_fwd(u, a_re, a_im, bbr, bbi, cr, ci, d_skip, n_seq)
    bf = _pad_to(b_forget, 1, LANES)
    cum = _fprep_fwd(fl, bf, n_seq)
    cum8 = jnp.swapaxes(cum[:, :N_HEADS].reshape(n_seq, seq_len, N_HEADS), 1, 2)
    cq = cum8[:, :, :, None]
    ck = cum8[:, :, None, :]
    ya, lse = _attn_fwd(qkv, cq, ck, n_seq)
    (own_rows, own_up), (g_rows, g_up) = _exchange_wait(w_sems[0], w_sems[1], w_sems[2], w_sems[3], rest_flags, ya,
                                                        "gather_rest_wait")
    my_slot = lax.broadcasted_iota(jnp.int32, (N_DEV, 1, 1), 0) == me_idx
    g_rows = jnp.where(my_slot, own_rows[None], g_rows)
    g_up = jnp.where(my_slot, own_up[None], g_up)
    g_down = g_rows[:, :ROWS_DOWN]
    g_out = g_rows[:, ROWS_DOWN:ROWS_DOWN + ROWS_OUT]
    g_glu = g_rows[:, ROWS_DOWN + ROWS_OUT:]
    w_glu_f = g_glu.reshape(D_SSM, D_SSM)
    w_out_f = g_out.reshape(D_MODEL, D_MODEL)
    conv_st = _pad_to(jnp.concatenate([g_cw, conv_b.reshape(N_DEV, 1, -1)], axis=1), 1, SUBLANES)
    w_down4 = g_down.reshape(FFN_GROUPS, FFN_GROUP, D_MODEL)
    ys = _glu_fwd(yc, w_glu_f, b_glu)
    h1, hn2, mixed = _mix_fwd(xf, ys, ya, norm_out_ssm, norm_out_attn, w_out_f, norm_ffn)
    ug, uv, dy, loss_part = _ffn_fwd(hn2, h1, target, g_up, conv_st, w_down4, seq_len)
    loss = lax.psum(0.5 * jnp.sum(loss_part) / D_MODEL, ("x", "y", "c"))

    dug, duv, act, dhn2, dcg, dcv = _ffn_bwd(dy, ug, uv, g_up, conv_st, w_down4, seq_len)
    dh1, dys, dya, d_gs, d_ga, d_gf = _mix_bwd(dy, dhn2[None], h1, ys, ya, norm_out_ssm, norm_out_attn, w_out_f, norm_ffn)
    dyc, gl_b, dz_b, d_bglu = _glu_bwd(yc, dys, w_glu_f, b_glu)

    gw_glu = _tn_matmul(gl_b, dz_b, "dw_glu", D_SSM, D_SSM)
    gw_out = _tn_matmul(mixed, dh1, "dw_out", D_MODEL, D_MODEL)
    gw_up = jnp.concatenate([_tn_grouped(hn2, dug, "dw_up_gate", True, BF16),
                             _tn_grouped(hn2, duv, "dw_up_val", True, BF16)], axis=0)
    gw_down = _tn_grouped(act, dy, "dw_down", False)
    g_conv = jnp.concatenate([dcg, dcv], axis=0)
    by_cols = lambda g, c: jnp.swapaxes(g.reshape(g.shape[0], N_DEV, c), 0, 1)
    early_flags = [True] * 2
    rows_g = jnp.concatenate([gw_down.reshape(N_DEV, ROWS_DOWN, D_MODEL), gw_out.reshape(N_DEV, ROWS_OUT, D_MODEL),
                              gw_glu.reshape(N_DEV, ROWS_GLU, D_MODEL)], axis=1).astype(BF16)
    g_sems = _exchange_start([rows_g, gw_up], early_flags, dyc, "grad_early_start", 1)
    started = g_sems[4][0, 0]

    du, dbbr, dbbi, dcr, dci, dar, dai, ddk = _s5_bwd(u, dyc, st_r, st_i, a_re, a_im, bbr, bbi, cr, ci,
                                                      d_skip + started, n_seq)
    partial_early = {
        "ab_re": jnp.sum(dar, axis=1), "ab_im": jnp.sum(dai, axis=1),
        "bb_re": _block_diag_extract(dbbr, False), "bb_im": _block_diag_extract(dbbi, False),
        "c_re": _block_diag_extract(dcr, True), "c_im": _block_diag_extract(dci, True),
        "d_skip": jnp.sum(ddk, axis=1), "b_glu": d_bglu,
        "norm_out_ssm": d_gs, "norm_out_attn": d_ga, "norm_ffn": d_gf, "conv_b": g_conv[:, 3],
    }
    early_keys = tuple(partial_early)
    early_shapes = [partial_early[k].shape for k in early_keys]
    p_sems = _exchange_start([_pack([partial_early[k] for k in early_keys])], [False], du, "small_early_start", 2)
    started = started + p_sems[4][0, 0]

    dqn, dkn, dv, dcq, dck = _attn_bwd(qkv, cq, ck + started, ya, dya, lse, n_seq)
    dcum8 = dcq[:, :, :, 0] + dck.reshape(n_seq, N_HEADS, seq_len)
    dcum = _pad_to(jnp.swapaxes(dcum8, 1, 2).reshape(n, N_HEADS), 1, LANES)
    dfl, dbf = _fprep_bwd(dcum, fl, bf, n_seq)
    dx, dproj, d_gmix, d_qg, d_kg = _inproj_bwd(xf, norm_mix, w_in_p, avg, qg, kg, raw, du, dqn, dkn, dv, dfl, dh1)

    gw_in = _tn_matmul(hn, dproj, "dw_in", D_MODEL, D_IN_PAD, out_cols=D_IN)
    partial_late = {
        "norm_mix": d_gmix, "b_forget": jnp.sum(dbf, axis=(0, 1))[:N_HEADS],
        "q_norm": jnp.sum(d_qg.reshape(N_HEADS, HEAD_DIM), axis=0),
        "k_norm": jnp.sum(d_kg.reshape(N_HEADS, HEAD_DIM), axis=0),
    }
    late_keys = tuple(partial_late)
    late_shapes = [partial_late[k].shape for k in late_keys]

    land_in, land_cw, small_parts = _exchange(
        [by_cols(gw_in, D_IN // N_DEV).astype(BF16), g_conv[:, :3],
         _pack([partial_late[k] for k in late_keys])], [True, True, False], "grad_late_exchange")
    (src_rows, src_up), (land_rows, land_up) = _exchange_wait(g_sems[0], g_sems[1], g_sems[2], g_sems[3], early_flags,
                                                              land_in, "grad_early_wait")
    land_down = land_rows[:, :ROWS_DOWN]
    land_out = land_rows[:, ROWS_DOWN:ROWS_DOWN + ROWS_OUT]
    land_glu = land_rows[:, ROWS_DOWN + ROWS_OUT:].reshape(N_DEV, -1, D_SSM)
    own_rows = lax.dynamic_index_in_dim(src_rows, me_idx, 0, keepdims=False)
    own_up = lax.dynamic_index_in_dim(src_up, me_idx, 0, keepdims=False)
    own = {"w_in": None, "conv_w": None, "w_up": own_up, "w_down": own_rows[:ROWS_DOWN],
           "w_out": own_rows[ROWS_DOWN:ROWS_DOWN + ROWS_OUT], "w_glu": own_rows[ROWS_DOWN + ROWS_OUT:].reshape(-1, D_SSM)}
    grads, deltas, new_m, new_v = {}, {}, {}, {}
    for name, land in zip(SHARDED, (land_in, land_glu, land_out, land_up, land_cw, land_down)):
        grads[name], deltas[name], new_m[name], new_v[name] = _adam_sharded(
            land, own[name], weights[name], mom1[name], mom2[name], "adam_" + name, ADAM_ROWS[name])

    (own_pack,), (early_parts,) = _exchange_wait(p_sems[0], p_sems[1], p_sems[2], p_sems[3], [False], land_up,
                                                 "small_early_wait")
    summed = dict(zip(late_keys, _unpack(_sum_partials(small_parts, None, "sum_late_partials"), late_shapes)))
    summed.update(zip(early_keys, _unpack(_sum_partials(early_parts, own_pack, "sum_early_partials"), early_shapes)))
    dlr, dli, dldt, dbr_t, dbi_t = _s5_param_bwd(
        lr3, li3, ldt3, br_t, bi_t, summed["ab_re"].reshape(lr3.shape), summed["ab_im"].reshape(lr3.shape),
        summed["bb_re"], summed["bb_im"])
    small_grads = {
        "norm_mix": summed["norm_mix"], "b_forget": summed["b_forget"], "lam_re": dlr, "lam_im": dli,
        "b_re": jnp.swapaxes(dbr_t, 1, 2), "b_im": jnp.swapaxes(dbi_t, 1, 2), "c_re": summed["c_re"], "c_im": summed["c_im"],
        "d_skip": summed["d_skip"], "log_dt": dldt, "b_glu": summed["b_glu"], "q_norm": summed["q_norm"],
        "k_norm": summed["k_norm"], "norm_out_ssm": summed["norm_out_ssm"], "norm_out_attn": summed["norm_out_attn"],
        "norm_ffn": summed["norm_ffn"], "conv_b": summed["conv_b"],
    }
    repl = tuple(k for k in WEIGHT_NAMES if k not in SHARDED)
    g_list = [small_grads[k].reshape(weights[k].shape) for k in repl]
    d_list, m_list, v_list = _adam_replicated(g_list, [weights[k] for k in repl], [mom1[k] for k in repl],
                                              [mom2[k] for k in repl], "adam_replicated")
    for k, g, d, nm, nv in zip(repl, g_list, d_list, m_list, v_list):
        grads[k], deltas[k], new_m[k], new_v[k] = g, d, nm, nv

    grad_x = dx.reshape(x.shape)
    return (loss, grad_x, *[grads[k] for k in WEIGHT_NAMES], *[deltas[k] for k in WEIGHT_NAMES],
            *[new_m[k] for k in WEIGHT_NAMES], *[new_v[k] for k in WEIGHT_NAMES])
```

```python
import functools
import math

import jax
import jax.numpy as jnp
from jax import lax
from jax.experimental import pallas as pl
from jax.experimental.pallas import tpu as pltpu

F32 = jnp.float32
BF16 = jnp.bfloat16
HIGHEST = lax.Precision.HIGHEST

N_DEV = 8
D_MODEL = 1024
D_SSM = 512
D_ATTN = 512
N_HEADS = 8
HEAD_DIM = 64
N_GROUPS = 32
SSM_GROUP = 16
SSM_STATE = 64
D_FF = 2752
D_FF_PAD = 2816
D_IN = 2056
D_IN_PAD = 2176
EPS = 1e-6
LANES = 128
SUBLANES = 8
VMEM_LIMIT = 56 * 1024 * 1024

ADAM_LR = 0.001
ADAM_B1 = 0.9
ADAM_B2 = 0.999
ADAM_EPS = 1e-08
ADAM_WD = 0.01
ADAM_STEP = 10


def _cparams(*sem):
    return pltpu.CompilerParams(dimension_semantics=sem, vmem_limit_bytes=VMEM_LIMIT)


def _dot(a, b, **kw):
    return jnp.dot(a, b, preferred_element_type=F32, **kw)


def _dot_nt(a, b):
    return lax.dot_general(a, b, (((1,), (1,)), ((), ())), preferred_element_type=F32)


def _dot_tn(a, b):
    return lax.dot_general(a, b, (((0,), (0,)), ((), ())), preferred_element_type=F32)


def _rms(x, g):
    return x * lax.rsqrt(jnp.mean(x * x, axis=-1, keepdims=True) + EPS) * g


def _split_dot(x, avg):
    hi = x.astype(BF16)
    lo = (x - hi.astype(F32)).astype(BF16)
    return _dot(hi, avg) + _dot(lo, avg)


@jax.custom_vjp
def _group_mean(x, avg):
    return _split_dot(x, avg)


def _group_mean_fwd(x, avg):
    return _split_dot(x, avg), avg


def _group_mean_bwd(avg, ct):
    return _split_dot(ct, avg), jnp.zeros_like(avg)


_group_mean.defvjp(_group_mean_fwd, _group_mean_bwd)


def _headnorm(q, avg, g):
    return q * lax.rsqrt(_group_mean(q * q, avg) + EPS) * g


def _exchange(srcs, scatter_flags, name):
    n = len(srcs)
    out_shape = []
    for s, sc in zip(srcs, scatter_flags):
        shp = s.shape if sc else (N_DEV,) + s.shape
        out_shape.append(jax.ShapeDtypeStruct(shp, s.dtype))

    def body(*refs):
        src = refs[:n]
        dst = refs[n:2 * n]
        send_sems, recv_sems, loc_sems = refs[2 * n:]
        x, y, c = lax.axis_index("x"), lax.axis_index("y"), lax.axis_index("c")
        me = 4 * x + 2 * y + c
        peers = []
        for j in range(1, N_DEV):
            px = 1 - x if (j >> 2) & 1 else x
            py = 1 - y if (j >> 1) & 1 else y
            pc = 1 - c if j & 1 else c
            peers.append(((px, py, pc), 4 * px + 2 * py + pc))
        local, sends = [], []
        for k in range(n):
            own = src[k].at[me] if scatter_flags[k] else src[k]
            lc = pltpu.make_async_copy(own, dst[k].at[me], loc_sems.at[k])
            lc.start()
            local.append(lc)
            for j, (pid, pidx) in enumerate(peers):
                s = src[k].at[pidx] if scatter_flags[k] else src[k]
                cp = pltpu.make_async_remote_copy(
                    src_ref=s, dst_ref=dst[k].at[me], send_sem=send_sems.at[k, j], recv_sem=recv_sems.at[k, j],
                    device_id=pid, device_id_type=pl.DeviceIdType.MESH)
                cp.start()
                sends.append(cp)
        for k in range(n):
            for j, (pid, pidx) in enumerate(peers):
                s = src[k].at[pidx] if scatter_flags[k] else src[k]
                pltpu.make_async_remote_copy(
                    src_ref=s, dst_ref=dst[k].at[pidx], send_sem=send_sems.at[k, j], recv_sem=recv_sems.at[k, j],
                    device_id=pid, device_id_type=pl.DeviceIdType.MESH).wait_recv()
        for cp in sends:
            cp.wait_send()
        for lc in local:
            lc.wait()

    any_spec = pl.BlockSpec(memory_space=pl.ANY)
    return pl.pallas_call(
        body, name=name, out_shape=tuple(out_shape),
        in_specs=[any_spec] * n, out_specs=tuple([any_spec] * n),
        scratch_shapes=[pltpu.SemaphoreType.DMA((n, N_DEV - 1)), pltpu.SemaphoreType.DMA((n, N_DEV - 1)),
                        pltpu.SemaphoreType.DMA((n,))],
        compiler_params=pltpu.CompilerParams(has_side_effects=True),
    )(*srcs)


def _peer_list():
    x, y, c = lax.axis_index("x"), lax.axis_index("y"), lax.axis_index("c")
    peers = []
    for j in range(1, N_DEV):
        px = 1 - x if (j >> 2) & 1 else x
        py = 1 - y if (j >> 1) & 1 else y
        pc = 1 - c if j & 1 else c
        peers.append(((px, py, pc), 4 * px + 2 * py + pc))
    return 4 * x + 2 * y + c, peers


def _split_copies(src, land, send_sems, recv_sems, scatter_flags, me, peers, incoming):
    copies = []
    for k in range(len(src)):
        for j, (pid, pidx) in enumerate(peers):
            s = src[k].at[pidx] if scatter_flags[k] else src[k]
            i = k * (N_DEV - 1) + j
            copies.append(pltpu.make_async_remote_copy(
                src_ref=s, dst_ref=land[k].at[pidx if incoming else me], send_sem=send_sems[i],
                recv_sem=recv_sems[i], device_id=pid, device_id_type=pl.DeviceIdType.MESH))
    return copies


def _exchange_start(srcs, scatter_flags, after, name, collective_id):
    n = len(srcs)
    ns = n * (N_DEV - 1)
    hbm = pl.BlockSpec(memory_space=pltpu.HBM)
    sem = pl.BlockSpec(memory_space=pltpu.SEMAPHORE)
    land_shapes = [s.shape if sc else (N_DEV,) + s.shape for s, sc in zip(srcs, scatter_flags)]

    def body(*refs):
        src, land = refs[:n], refs[n:2 * n]
        send_sems = refs[2 * n + 1:2 * n + 1 + ns]
        recv_sems = refs[2 * n + 1 + ns:2 * n + 1 + 2 * ns]
        token = refs[4 * n + 1 + 2 * ns]
        me, peers = _peer_list()
        barrier = pltpu.get_barrier_semaphore()
        for pid, _ in peers:
            pl.semaphore_signal(barrier, inc=1, device_id=pid, device_id_type=pl.DeviceIdType.MESH)
        pl.semaphore_wait(barrier, N_DEV - 1)
        for cp in _split_copies(src, land, send_sems, recv_sems, scatter_flags, me, peers, False):
            cp.start()
        token[...] = jnp.zeros_like(token)

    outs = pl.pallas_call(
        body, name=name,
        out_shape=(*[pltpu.SemaphoreType.DMA(())] * (2 * ns), *[pltpu.HBM(s.shape, s.dtype) for s in srcs],
                   *[pltpu.HBM(shp, s.dtype) for shp, s in zip(land_shapes, srcs)],
                   jax.ShapeDtypeStruct((SUBLANES, LANES), F32)),
        in_specs=[hbm] * (2 * n) + [pl.BlockSpec(memory_space=pl.ANY)],
        out_specs=(*[sem] * (2 * ns), *[hbm] * (2 * n), pl.BlockSpec(memory_space=pltpu.VMEM)),
        input_output_aliases={i: 2 * ns + i for i in range(2 * n)},
        compiler_params=pltpu.CompilerParams(has_side_effects=pltpu.SideEffectType.DATAFLOW_SIDE_EFFECTING,
                                             collective_id=collective_id),
    )(*[pltpu.with_memory_space_constraint(s, pltpu.HBM) for s in srcs],
      *[pltpu.with_memory_space_constraint(lax.empty(shp, s.dtype), pltpu.HBM) for shp, s in zip(land_shapes, srcs)],
      after)
    return (outs[:ns], outs[ns:2 * ns], outs[2 * ns:2 * ns + n], outs[2 * ns + n:2 * ns + 2 * n], outs[2 * ns + 2 * n])


def _exchange_wait(send_sems, recv_sems, srcs, lands, scatter_flags, after, name):
    n = len(srcs)
    ns = n * (N_DEV - 1)
    hbm = pl.BlockSpec(memory_space=pltpu.HBM)
    sem = pl.BlockSpec(memory_space=pltpu.SEMAPHORE)

    def body(*refs):
        src, land = refs[:n], refs[n:2 * n]
        s_sems = refs[2 * n:2 * n + ns]
        r_sems = refs[2 * n + ns:2 * n + 2 * ns]
        me, peers = _peer_list()
        for cp in _split_copies(src, land, s_sems, r_sems, scatter_flags, me, peers, True):
            cp.wait_send()
            cp.wait_recv()

    outs = pl.pallas_call(
        body, name=name,
        out_shape=tuple(pltpu.HBM(a.shape, a.dtype) for a in (*srcs, *lands)),
        in_specs=[hbm] * (2 * n) + [sem] * (2 * ns) + [pl.BlockSpec(memory_space=pl.ANY)],
        out_specs=tuple([hbm] * (2 * n)),
        input_output_aliases={i: i for i in range(2 * n)},
        compiler_params=pltpu.CompilerParams(has_side_effects=pltpu.SideEffectType.DATAFLOW_SIDE_EFFECTING),
    )(*srcs, *lands, *send_sems, *recv_sems, after)
    return outs[:n], outs[n:]


def _tn_matmul(a, b, name, tk, tm, out_rows=None, out_cols=None, tn=512):
    n_tok, k_dim = a.shape
    m_dim = b.shape[1]
    grid = (k_dim // tk, m_dim // tm, n_tok // tn)

    def body(a_ref, b_ref, o_ref):
        @pl.when(pl.program_id(2) == 0)
        def _():
            o_ref[...] = jnp.zeros_like(o_ref)
        o_ref[...] += _dot_tn(a_ref[...].astype(BF16), b_ref[...].astype(BF16))

    return pl.pallas_call(
        body, name=name, grid=grid,
        in_specs=[pl.BlockSpec((tn, tk), lambda i, j, k: (k, i)), pl.BlockSpec((tn, tm), lambda i, j, k: (k, j))],
        out_specs=pl.BlockSpec((tk, tm), lambda i, j, k: (i, j)),
        out_shape=jax.ShapeDtypeStruct((out_rows or k_dim, out_cols or m_dim), F32),
        compiler_params=_cparams("parallel", "parallel", "arbitrary"),
    )(a, b)


def _adam_math(g, w, m, v):
    m = ADAM_B1 * m + (1.0 - ADAM_B1) * g
    v = ADAM_B2 * v + (1.0 - ADAM_B2) * (g * g)
    m_hat = m / (1.0 - ADAM_B1 ** ADAM_STEP)
    v_hat = v / (1.0 - ADAM_B2 ** ADAM_STEP)
    delta = -ADAM_LR * (m_hat / (jnp.sqrt(v_hat) + ADAM_EPS) + ADAM_WD * w)
    return delta, m, v


def _adam_sharded(land, own, w, m, v, name, tr):
    _, r, c = w.shape

    def body(*refs):
        l_ref = refs[0]
        own_ref = refs[1] if own is not None else None
        w_ref, m_ref, v_ref, g_ref, d_ref, nm_ref, nv_ref = [ref.at[0] for ref in refs[-7:]]
        if own_ref is not None:
            x, y, z = lax.axis_index("x"), lax.axis_index("y"), lax.axis_index("c")
            me = 4 * x + 2 * y + z
            mine = own_ref[...].astype(F32)
        g = None
        for s in range(N_DEV):
            part = l_ref[s].astype(F32)
            if own_ref is not None:
                part = jnp.where(me == s, mine, part)
            g = part if g is None else g + part
        d, nm, nv = _adam_math(g, w_ref[...], m_ref[...], v_ref[...])
        g_ref[...] = g
        d_ref[...] = d
        nm_ref[...] = nm
        nv_ref[...] = nv

    spec = pl.BlockSpec((1, tr, c), lambda i: (0, i, 0))
    own_specs, own_args = ([pl.BlockSpec((tr, c), lambda i: (i, 0))], [own]) if own is not None else ([], [])
    return pl.pallas_call(
        body, name=name, grid=(r // tr,),
        in_specs=[pl.BlockSpec((N_DEV, tr, c), lambda i: (0, i, 0)), *own_specs, spec, spec, spec],
        out_specs=(spec, spec, spec, spec),
        out_shape=tuple(jax.ShapeDtypeStruct((1, r, c), F32) for _ in range(4)),
        compiler_params=_cparams("parallel"),
    )(land, *own_args, w, m, v)


def _sum_partials(parts, own, name):
    _, r, c = parts.shape

    def body(*refs):
        p_ref, o_ref = refs[0], refs[-1]
        if own is not None:
            x, y, z = lax.axis_index("x"), lax.axis_index("y"), lax.axis_index("c")
            me = 4 * x + 2 * y + z
            mine = refs[1][...]
        g = None
        for s in range(N_DEV):
            part = p_ref[s]
            if own is not None:
                part = jnp.where(me == s, mine, part)
            g = part if g is None else g + part
        o_ref[...] = g

    args = (parts,) if own is None else (parts, own)
    return pl.pallas_call(body, name=name, out_shape=jax.ShapeDtypeStruct((r, c), F32),
                          compiler_params=pltpu.CompilerParams(vmem_limit_bytes=VMEM_LIMIT))(*args)


def _adam_replicated(gs, ws, ms, vs, name):
    k = len(ws)

    def body(*refs):
        outs = refs[4 * k:]
        for i in range(k):
            d, nm, nv = _adam_math(refs[i][...], refs[k + i][...], refs[2 * k + i][...], refs[3 * k + i][...])
            outs[i][...] = d
            outs[k + i][...] = nm
            outs[2 * k + i][...] = nv

    outs = pl.pallas_call(body, name=name, out_shape=tuple(jax.ShapeDtypeStruct(w.shape, F32) for w in ws) * 3,
                          compiler_params=pltpu.CompilerParams(vmem_limit_bytes=VMEM_LIMIT))(*gs, *ws, *ms, *vs)
    return outs[:k], outs[k:2 * k], outs[2 * k:]


def _inproj_fwd(x, g, w_in, avg, qg, kg, tm=512):
    n = x.shape[0]

    def body(x_ref, g_ref, w_ref, a_ref, qg_ref, kg_ref, hn_ref, u_ref, qkv_ref, raw_ref, fl_ref):
        hn = _rms(x_ref[...], g_ref[...]).astype(BF16)
        hn_ref[...] = hn
        proj = _dot(hn, w_ref[...])
        u_ref[...] = proj[:, 0:512]
        q = proj[:, 512:1024]
        k = proj[:, 1024:1536]
        raw_ref[:, 0:512] = q
        raw_ref[:, 512:1024] = k
        qkv_ref[:, 0:512] = _headnorm(q, a_ref[...], qg_ref[...])
        qkv_ref[:, 512:1024] = _headnorm(k, a_ref[...], kg_ref[...])
        qkv_ref[:, 1024:1536] = proj[:, 1536:2048]
        fl_ref[...] = proj[:, 2048:D_IN_PAD]

    row = lambda w: pl.BlockSpec((tm, w), lambda i: (i, 0))
    full = lambda a: pl.BlockSpec(a.shape, lambda i: (0,) * a.ndim)
    return pl.pallas_call(
        body, name="inproj_fwd", grid=(n // tm,),
        in_specs=[row(D_MODEL), full(g), full(w_in), full(avg), full(qg), full(kg)],
        out_specs=(row(D_MODEL), row(512), row(1536), row(1024), row(LANES)),
        out_shape=(jax.ShapeDtypeStruct((n, D_MODEL), BF16), jax.ShapeDtypeStruct((n, 512), F32),
                   jax.ShapeDtypeStruct((n, 1536), F32), jax.ShapeDtypeStruct((n, 1024), F32),
                   jax.ShapeDtypeStruct((n, LANES), F32)),
        compiler_params=_cparams("parallel"),
    )(x, g, w_in, avg, qg, kg)


def _inproj_bwd(x, g, w_in, avg, qg, kg, raw, du, dqn, dkn, dv, dfl, dres, tm=512):
    n = x.shape[0]

    def body(x_ref, g_ref, w_ref, a_ref, qg_ref, kg_ref, raw_ref, du_ref, dqn_ref, dkn_ref, dv_ref, dfl_ref, dres_ref,
             dx_ref, dproj_ref, dg_ref, dqg_ref, dkg_ref):
        @pl.when(pl.program_id(0) == 0)
        def _():
            dg_ref[...] = jnp.zeros_like(dg_ref)
            dqg_ref[...] = jnp.zeros_like(dqg_ref)
            dkg_ref[...] = jnp.zeros_like(dkg_ref)
        avg_m = a_ref[...]
        _, vjp_q = jax.vjp(lambda q, gg: _headnorm(q, avg_m, gg), raw_ref[:, 0:512], qg_ref[...])
        dq, dqg = vjp_q(dqn_ref[...])
        _, vjp_k = jax.vjp(lambda k, gg: _headnorm(k, avg_m, gg), raw_ref[:, 512:1024], kg_ref[...])
        dk, dkg = vjp_k(dkn_ref[...])
        dproj = jnp.concatenate([du_ref[...], dq, dk, dv_ref[...], dfl_ref[...]], axis=1).astype(BF16)
        dproj_ref[...] = dproj
        dhn = _dot_nt(dproj, w_ref[...])
        _, vjp_x = jax.vjp(_rms, x_ref[...], g_ref[...])
        dxn, dg = vjp_x(dhn)
        dx_ref[...] = dxn + dres_ref[...]
        dg_ref[...] += dg
        dqg_ref[...] += dqg
        dkg_ref[...] += dkg

    row = lambda w: pl.BlockSpec((tm, w), lambda i: (i, 0))
    full = lambda a: pl.BlockSpec(a.shape, lambda i: (0,) * a.ndim)
    vec = lambda w: pl.BlockSpec((1, w), lambda i: (0, 0))
    return pl.pallas_call(
        body, name="inproj_bwd", grid=(n // tm,),
        in_specs=[row(D_MODEL), full(g), full(w_in), full(avg), full(qg), full(kg), row(1024), row(512), row(512),
                  row(512), row(512), row(LANES), row(D_MODEL)],
        out_specs=(row(D_MODEL), row(D_IN_PAD), vec(D_MODEL), vec(512), vec(512)),
        out_shape=(jax.ShapeDtypeStruct((n, D_MODEL), F32), jax.ShapeDtypeStruct((n, D_IN_PAD), BF16),
                   jax.ShapeDtypeStruct((1, D_MODEL), F32), jax.ShapeDtypeStruct((1, 512), F32),
                   jax.ShapeDtypeStruct((1, 512), F32)),
        compiler_params=_cparams("arbitrary"),
    )(x, g, w_in, avg, qg, kg, raw, du, dqn, dkn, dv, dfl, dres)


def _glu_fwd(yc, wg, bg, tm=512):
    n = yc.shape[0]

    def body(yc_ref, w_ref, b_ref, ys_ref):
        gl = jax.nn.gelu(yc_ref[...])
        z = _dot(gl.astype(BF16), w_ref[...]) + b_ref[...]
        ys_ref[...] = gl * jax.nn.sigmoid(z)

    row = pl.BlockSpec((tm, 512), lambda i: (i, 0))
    full = lambda a: pl.BlockSpec(a.shape, lambda i: (0,) * a.ndim)
    return pl.pallas_call(
        body, name="glu_fwd", grid=(n // tm,), in_specs=[row, full(wg), full(bg)], out_specs=row,
        out_shape=jax.ShapeDtypeStruct((n, 512), F32), compiler_params=_cparams("parallel"),
    )(yc, wg, bg)


def _glu_bwd(yc, dys, wg, bg, tm=512):
    n = yc.shape[0]

    def body(yc_ref, dys_ref, w_ref, b_ref, dyc_ref, gl_ref, dz_ref, db_ref):
        @pl.when(pl.program_id(0) == 0)
        def _():
            db_ref[...] = jnp.zeros_like(db_ref)
        gl, vjp_gelu = jax.vjp(jax.nn.gelu, yc_ref[...])
        glb = gl.astype(BF16)
        z = _dot(glb, w_ref[...]) + b_ref[...]
        s = jax.nn.sigmoid(z)
        dys = dys_ref[...]
        dz = dys * gl * s * (1.0 - s)
        dzb = dz.astype(BF16)
        dgl = dys * s + _dot_nt(dzb, w_ref[...])
        dyc_ref[...] = vjp_gelu(dgl)[0]
        gl_ref[...] = glb
        dz_ref[...] = dzb
        db_ref[...] += jnp.sum(dz, axis=0, keepdims=True)

    row = pl.BlockSpec((tm, 512), lambda i: (i, 0))
    full = lambda a: pl.BlockSpec(a.shape, lambda i: (0,) * a.ndim)
    return pl.pallas_call(
        body, name="glu_bwd", grid=(n // tm,), in_specs=[row, row, full(wg), full(bg)],
        out_specs=(row, row, row, pl.BlockSpec((1, 512), lambda i: (0, 0))),
        out_shape=(jax.ShapeDtypeStruct((n, 512), F32), jax.ShapeDtypeStruct((n, 512), BF16),
                   jax.ShapeDtypeStruct((n, 512), BF16), jax.ShapeDtypeStruct((1, 512), F32)),
        compiler_params=_cparams("arbitrary"),
    )(yc, dys, wg, bg)


def _mix_fwd(x, ys, ya, gs, ga, wout, gf, tm=512):
    n = x.shape[0]

    def body(x_ref, ys_ref, ya_ref, gs_ref, ga_ref, w_ref, gf_ref, h1_ref, hn2_ref, mixed_ref):
        mixed = jnp.concatenate([_rms(ys_ref[...], gs_ref[...]), _rms(ya_ref[...], ga_ref[...])], axis=1).astype(BF16)
        mixed_ref[...] = mixed
        h1 = x_ref[...] + _dot(mixed, w_ref[...])
        h1_ref[...] = h1
        hn2_ref[...] = _rms(h1, gf_ref[...]).astype(BF16)

    row = lambda w: pl.BlockSpec((tm, w), lambda i: (i, 0))
    full = lambda a: pl.BlockSpec(a.shape, lambda i: (0,) * a.ndim)
    return pl.pallas_call(
        body, name="mix_fwd", grid=(n // tm,),
        in_specs=[row(D_MODEL), row(512), row(512), full(gs), full(ga), full(wout), full(gf)],
        out_specs=(row(D_MODEL), row(D_MODEL), row(D_MODEL)),
        out_shape=(jax.ShapeDtypeStruct((n, D_MODEL), F32), jax.ShapeDtypeStruct((n, D_MODEL), BF16),
                   jax.ShapeDtypeStruct((n, D_MODEL), BF16)),
        compiler_params=_cparams("parallel"),
    )(x, ys, ya, gs, ga, wout, gf)


def _mix_bwd(dy, dhn2_parts, h1, ys, ya, gs, ga, wout, gf, tm=512):
    n = dy.shape[0]
    n_parts = dhn2_parts.shape[0]

    def body(dy_ref, dp_ref, h1_ref, ys_ref, ya_ref, gs_ref, ga_ref, w_ref, gf_ref,
             dh1_ref, dys_ref, dya_ref, dgs_ref, dga_ref, dgf_ref):
        @pl.when(pl.program_id(0) == 0)
        def _():
            dgs_ref[...] = jnp.zeros_like(dgs_ref)
            dga_ref[...] = jnp.zeros_like(dga_ref)
            dgf_ref[...] = jnp.zeros_like(dgf_ref)
        dhn2 = dp_ref[0]
        for p in range(1, n_parts):
            dhn2 = dhn2 + dp_ref[p]
        _, vjp_f = jax.vjp(_rms, h1_ref[...], gf_ref[...])
        dh1n, dgf = vjp_f(dhn2)
        dh1 = dy_ref[...] + dh1n
        dh1_ref[...] = dh1
        dmixed = _dot_nt(dh1.astype(BF16), w_ref[...])
        _, vjp_s = jax.vjp(_rms, ys_ref[...], gs_ref[...])
        dys, dgs = vjp_s(dmixed[:, 0:512])
        _, vjp_a = jax.vjp(_rms, ya_ref[...], ga_ref[...])
        dya, dga = vjp_a(dmixed[:, 512:1024])
        dys_ref[...] = dys
        dya_ref[...] = dya
        dgs_ref[...] += dgs
        dga_ref[...] += dga
        dgf_ref[...] += dgf

    row = lambda w: pl.BlockSpec((tm, w), lambda i: (i, 0))
    full = lambda a: pl.BlockSpec(a.shape, lambda i: (0,) * a.ndim)
    vec = lambda w: pl.BlockSpec((1, w), lambda i: (0, 0))
    return pl.pallas_call(
        body, name="mix_bwd", grid=(n // tm,),
        in_specs=[row(D_MODEL), pl.BlockSpec((n_parts, tm, D_MODEL), lambda i: (0, i, 0)), row(D_MODEL), row(512),
                  row(512), full(gs), full(ga), full(wout), full(gf)],
        out_specs=(row(D_MODEL), row(512), row(512), vec(512), vec(512), vec(D_MODEL)),
        out_shape=(jax.ShapeDtypeStruct((n, D_MODEL), F32), jax.ShapeDtypeStruct((n, 512), F32),
                   jax.ShapeDtypeStruct((n, 512), F32), jax.ShapeDtypeStruct((1, 512), F32),
                   jax.ShapeDtypeStruct((1, 512), F32), jax.ShapeDtypeStruct((1, D_MODEL), F32)),
        compiler_params=_cparams("arbitrary"),
    )(dy, dhn2_parts, h1, ys, ya, gs, ga, wout, gf)


HALO = 16
FFN_GROUPS = 4
FFN_GROUP = D_FF // FFN_GROUPS


def _conv3(ue, cw):
    return cw[2:3] * ue + cw[1:2] * pltpu.roll(ue, 1, 0) + cw[0:1] * pltpu.roll(ue, 2, 0) + cw[3:4]


def _ffn_weight_specs():
    gate = lambda i, j: (j, 0, 0)
    val = lambda i, j: (j + FFN_GROUPS, 0, 0)
    w_blk, c_blk = (1, D_MODEL, FFN_GROUP), (1, SUBLANES, FFN_GROUP)
    return [pl.BlockSpec(w_blk, gate), pl.BlockSpec(w_blk, val), pl.BlockSpec(c_blk, gate), pl.BlockSpec(c_blk, val),
            pl.BlockSpec((1, FFN_GROUP, D_MODEL), gate)]


def _ffn_fwd(hn2, h1, target, w_up, conv, w_down, seq_len, tm=512):
    n = hn2.shape[0]
    nj = FFN_GROUPS
    hb = tm // HALO

    def body(hn_ref, halo_ref, h1_ref, tgt_ref, wg_ref, wv_ref, cg_ref, cv_ref, wd_ref,
             ug_ref, uv_ref, dy_ref, loss_ref, acc):
        i, j = pl.program_id(0), pl.program_id(1)
        seq_start = (i * tm) % seq_len == 0
        halo = halo_ref[...]
        halo = jnp.where(seq_start, jnp.zeros_like(halo), halo)
        he = jnp.concatenate([halo, hn_ref[...]], axis=0)
        ueg = _dot(he, wg_ref[0])
        uev = _dot(he, wv_ref[0])
        ug_ref[0] = ueg[HALO:].astype(BF16)
        uv_ref[0] = uev[HALO:].astype(BF16)
        cg = _conv3(ueg, cg_ref[0])[HALO:]
        cv = _conv3(uev, cv_ref[0])[HALO:]
        act = (jax.nn.silu(cg) * cv).astype(BF16)
        part = _dot(act, wd_ref[0])

        @pl.when(j == 0)
        def _():
            acc[...] = part

        @pl.when(j > 0)
        def _():
            acc[...] += part

        @pl.when(j == nj - 1)
        def _():
            err = h1_ref[...] + acc[...] - tgt_ref[...]
            dy_ref[...] = err * (1.0 / D_MODEL)
            loss_ref[0] = jnp.sum(err * err, axis=0, keepdims=True)

    row = pl.BlockSpec((tm, D_MODEL), lambda i, j: (i, 0))
    u_main = pl.BlockSpec((1, tm, FFN_GROUP), lambda i, j: (j, i, 0))
    u_shape = jax.ShapeDtypeStruct((FFN_GROUPS, n, FFN_GROUP), BF16)
    return pl.pallas_call(
        body, name="ffn_fwd", grid=(n // tm, nj),
        in_specs=[row, pl.BlockSpec((HALO, D_MODEL), lambda i, j: (jnp.maximum(i * hb - 1, 0), 0)), row, row,
                  *_ffn_weight_specs()],
        out_specs=(u_main, u_main, row, pl.BlockSpec((1, 1, D_MODEL), lambda i, j: (i, 0, 0))),
        out_shape=(u_shape, u_shape, jax.ShapeDtypeStruct((n, D_MODEL), F32),
                   jax.ShapeDtypeStruct((n // tm, 1, D_MODEL), F32)),
        scratch_shapes=[pltpu.VMEM((tm, D_MODEL), F32)],
        compiler_params=_cparams("parallel", "arbitrary"),
    )(hn2, hn2, h1, target, w_up, w_up, conv, conv, w_down)


def _ffn_bwd(dy, ug, uv, w_up, conv, w_down, seq_len, tm=512):
    n = dy.shape[0]
    nj = FFN_GROUPS
    fb = FFN_GROUP
    hb = tm // HALO
    last_hb = n // HALO - 1
    rows = tm + HALO

    def body(dy_ref, dyn_ref, ugp_ref, ugm_ref, ugn_ref, uvp_ref, uvm_ref, uvn_ref, wg_ref, wv_ref, cg_ref, cv_ref,
             wd_ref, dug_ref, duv_ref, act_ref, dhn_ref, dcg_ref, dcv_ref, acc):
        i, j = pl.program_id(0), pl.program_id(1)
        seq_start = (i * tm) % seq_len == 0
        seq_end = ((i + 1) * tm) % seq_len == 0
        dyn = dyn_ref[...]
        dyn = jnp.where(seq_end, jnp.zeros_like(dyn), dyn)
        d_out = jnp.concatenate([dy_ref[...], dyn], axis=0).astype(BF16)
        d_act = _dot_nt(d_out, wd_ref[0])

        def pre_act(up_ref, um_ref, un_ref, cw):
            up = up_ref[0]
            up = jnp.where(seq_start, jnp.zeros_like(up), up)
            ue = jnp.concatenate([up, um_ref[0], un_ref[0]], axis=0).astype(F32)
            return ue, _conv3(ue, cw)[HALO:]

        cwg, cwv = cg_ref[0], cv_ref[0]
        ueg, cge = pre_act(ugp_ref, ugm_ref, ugn_ref, cwg)
        uev, cve = pre_act(uvp_ref, uvm_ref, uvn_ref, cwv)
        act, vjp_act = jax.vjp(lambda g, v: jax.nn.silu(g) * v, cge, cve)
        dcge, dcve = vjp_act(d_act)
        act_ref[0] = act[:tm].astype(BF16)

        def conv_t(dc, cw):
            return (cw[2:3] * dc + cw[1:2] * pltpu.roll(dc, rows - 1, 0) + cw[0:1] * pltpu.roll(dc, rows - 2, 0))[:tm]

        dug = conv_t(dcge, cwg).astype(BF16)
        duv = conv_t(dcve, cwv).astype(BF16)
        dug_ref[0] = dug
        duv_ref[0] = duv
        part = _dot_nt(dug, wg_ref[0]) + _dot_nt(duv, wv_ref[0])

        @pl.when(j == 0)
        def _():
            acc[...] = part

        @pl.when(j > 0)
        def _():
            acc[...] += part

        @pl.when(j == nj - 1)
        def _():
            dhn_ref[...] = acc[...]

        def cw_grad(dc, ue):
            dcm = dc[:tm]
            taps = [jnp.sum(dcm * pltpu.roll(ue, 2 - k, 0)[HALO:HALO + tm], axis=0, keepdims=True) for k in (0, 1)]
            taps.append(jnp.sum(dcm * ue[HALO:HALO + tm], axis=0, keepdims=True))
            taps.append(jnp.sum(dcm, axis=0, keepdims=True))
            return jnp.concatenate(taps + [jnp.zeros((4, fb), F32)], axis=0)

        @pl.when(i == 0)
        def _():
            dcg_ref[j] = jnp.zeros((8, fb), F32)
            dcv_ref[j] = jnp.zeros((8, fb), F32)

        dcg_ref[j] += cw_grad(dcge, ueg)
        dcv_ref[j] += cw_grad(dcve, uev)

    row = pl.BlockSpec((tm, D_MODEL), lambda i, j: (i, 0))
    u_prev = pl.BlockSpec((1, HALO, fb), lambda i, j: (j, jnp.maximum(i * hb - 1, 0), 0))
    u_main = pl.BlockSpec((1, tm, fb), lambda i, j: (j, i, 0))
    u_next = pl.BlockSpec((1, HALO, fb), lambda i, j: (j, jnp.minimum((i + 1) * hb, last_hb), 0))
    dc_spec = pl.BlockSpec((nj, 8, fb), lambda i, j: (0, 0, 0))
    u_shape = jax.ShapeDtypeStruct((FFN_GROUPS, n, fb), BF16)
    return pl.pallas_call(
        body, name="ffn_bwd", grid=(n // tm, nj),
        in_specs=[row, pl.BlockSpec((HALO, D_MODEL), lambda i, j: (jnp.minimum((i + 1) * hb, last_hb), 0)),
                  u_prev, u_main, u_next, u_prev, u_main, u_next, *_ffn_weight_specs()],
        out_specs=(u_main, u_main, u_main, row, dc_spec, dc_spec),
        out_shape=(u_shape, u_shape, u_shape, jax.ShapeDtypeStruct((n, D_MODEL), F32),
                   jax.ShapeDtypeStruct((nj, 8, fb), F32), jax.ShapeDtypeStruct((nj, 8, fb), F32)),
        scratch_shapes=[pltpu.VMEM((tm, D_MODEL), F32)],
        compiler_params=_cparams("arbitrary", "arbitrary"),
    )(dy, dy, ug, ug, ug, uv, uv, uv, w_up, w_up, conv, conv, w_down)


def _tn_grouped(a, b, name, shared_a, out_dtype=F32, tn=1024):
    groups = b.shape[0] if shared_a else a.shape[0]
    n_tok = a.shape[0] if shared_a else b.shape[0]
    k_dim, m_dim = a.shape[-1], b.shape[-1]

    def body(a_ref, b_ref, o_ref, acc):
        k = pl.program_id(1)
        a_t = a_ref[...] if shared_a else a_ref[0]
        b_t = b_ref[0] if shared_a else b_ref[...]
        part = _dot_tn(a_t.astype(BF16), b_t.astype(BF16))

        @pl.when(k == 0)
        def _():
            acc[...] = part

        @pl.when(k > 0)
        def _():
            acc[...] += part

        @pl.when(k == n_tok // tn - 1)
        def _():
            o_ref[0] = acc[...].astype(out_dtype)

    plain = lambda w: pl.BlockSpec((tn, w), lambda g, k: (k, 0))
    grouped = lambda w: pl.BlockSpec((1, tn, w), lambda g, k: (g, k, 0))
    return pl.pallas_call(
        body, name=name, grid=(groups, n_tok // tn),
        in_specs=[plain(k_dim), grouped(m_dim)] if shared_a else [grouped(k_dim), plain(m_dim)],
        out_specs=pl.BlockSpec((1, k_dim, m_dim), lambda g, k: (g, 0, 0)),
        out_shape=jax.ShapeDtypeStruct((groups, k_dim, m_dim), out_dtype),
        scratch_shapes=[pltpu.VMEM((k_dim, m_dim), F32)],
        compiler_params=_cparams("parallel", "arbitrary"),
    )(a, b)


def _s5_param_fn(lr, li, ldt, br, bi):
    dt = jnp.exp(ldt)
    mag = jnp.exp(lr * dt)
    ab_re = mag * jnp.cos(li * dt)
    ab_im = mag * jnp.sin(li * dt)
    nr = ab_re - 1.0
    ni = ab_im
    den = lr * lr + li * li
    q_re = (nr * lr + ni * li) / den
    q_im = (ni * lr - nr * li) / den
    bb_re = q_re * br - q_im * bi
    bb_im = q_re * bi + q_im * br
    return ab_re, ab_im, bb_re, bb_im


def _s5_param_fwd(lr, li, ldt, br, bi):
    def body(lr_ref, li_ref, ldt_ref, br_ref, bi_ref, ar_ref, ai_ref, bbr_ref, bbi_ref):
        ar, ai, bbr, bbi = _s5_param_fn(lr_ref[...], li_ref[...], ldt_ref[...], br_ref[...], bi_ref[...])
        ar_ref[...] = ar
        ai_ref[...] = ai
        bbr_ref[...] = bbr
        bbi_ref[...] = bbi

    return pl.pallas_call(
        body, name="s5_param_fwd",
        out_shape=(jax.ShapeDtypeStruct(lr.shape, F32), jax.ShapeDtypeStruct(lr.shape, F32),
                   jax.ShapeDtypeStruct(br.shape, F32), jax.ShapeDtypeStruct(br.shape, F32)),
    )(lr, li, ldt, br, bi)


def _s5_param_bwd(lr, li, ldt, br, bi, dar, dai, dbbr, dbbi):
    def body(lr_ref, li_ref, ldt_ref, br_ref, bi_ref, dar_ref, dai_ref, dbbr_ref, dbbi_ref,
             dlr_ref, dli_ref, dldt_ref, dbr_ref, dbi_ref):
        _, vjp = jax.vjp(_s5_param_fn, lr_ref[...], li_ref[...], ldt_ref[...], br_ref[...], bi_ref[...])
        dlr, dli, dldt, dbr, dbi = vjp((dar_ref[...], dai_ref[...], dbbr_ref[...], dbbi_ref[...]))
        dlr_ref[...] = dlr
        dli_ref[...] = dli
        dldt_ref[...] = dldt
        dbr_ref[...] = dbr
        dbi_ref[...] = dbi

    return pl.pallas_call(
        body, name="s5_param_bwd",
        out_shape=(jax.ShapeDtypeStruct(lr.shape, F32), jax.ShapeDtypeStruct(lr.shape, F32),
                   jax.ShapeDtypeStruct(ldt.shape, F32), jax.ShapeDtypeStruct(br.shape, F32),
                   jax.ShapeDtypeStruct(br.shape, F32)),
    )(lr, li, ldt, br, bi, dar, dai, dbbr, dbbi)


S5_CHUNK = 256
S5_STATES = 512
S5_BLOCKS = 4


def _cpow_rows(ar, ai, count):
    rs, im = [ar], [ai]
    for _ in range(count - 1):
        pr, pi = rs[-1], im[-1]
        rs.append(pr * ar - pi * ai)
        im.append(pr * ai + pi * ar)
    return rs, im


def _scan_in_groups(vr, vi, pr, pi, rm, reverse):
    n, width = vr.shape
    vr = vr.reshape(n // SUBLANES, SUBLANES, width)
    vi = vi.reshape(n // SUBLANES, SUBLANES, width)
    row = rm[0:SUBLANES]
    for k in (1, 2, 4):
        shift = SUBLANES - k if reverse else k
        keep = row < SUBLANES - k if reverse else row >= k
        kr = jnp.where(keep, pr[k - 1], 0.0)
        ki = jnp.where(keep, pi[k - 1], 0.0)
        sr, si = pltpu.roll(vr, shift, 1), pltpu.roll(vi, shift, 1)
        vr, vi = vr + kr * sr - ki * si, vi + kr * si + ki * sr
    return vr.reshape(n, width), vi.reshape(n, width)


def _carry_over_groups(xr_s, xi_s, wr, wi, c0r, c0i, reverse):
    groups = xr_s.shape[0] // SUBLANES
    pick = 0 if reverse else SUBLANES - 1

    def step(q, carry):
        cr, ci = carry
        r = groups - 1 - q if reverse else q
        o = pl.multiple_of(r * SUBLANES, SUBLANES)
        vr = xr_s[pl.ds(o, SUBLANES), :]
        vi = xi_s[pl.ds(o, SUBLANES), :]
        nr = vr + wr * cr - wi * ci
        ni = vi + wr * ci + wi * cr
        xr_s[pl.ds(o, SUBLANES), :] = nr
        xi_s[pl.ds(o, SUBLANES), :] = ni
        return (jnp.broadcast_to(nr[pick:pick + 1], nr.shape), jnp.broadcast_to(ni[pick:pick + 1], ni.shape))

    return lax.fori_loop(0, groups, step, (c0r, c0i))


def _s5_state_scan(u_b, bbr, bbi, pr, pi, rm, xr_s, xi_s, c0r, c0i):
    bur = _dot(u_b, bbr)
    bui = _dot(u_b, bbi)
    bur, bui = _scan_in_groups(bur, bui, pr, pi, rm, False)
    xr_s[...] = bur
    xi_s[...] = bui
    w8r = jnp.concatenate(pr, axis=0)
    w8i = jnp.concatenate(pi, axis=0)
    return _carry_over_groups(xr_s, xi_s, w8r, w8i, c0r, c0i, False)


def _s5_fwd(u, a_re, a_im, bbr, bbi, cr, ci, d_skip, n_seq):
    n = u.shape[0]
    seq_len = n // n_seq
    nt = seq_len // S5_CHUNK
    tc = S5_CHUNK

    def body(u_ref, ar_ref, ai_ref, bbr_ref, bbi_ref, cr_ref, ci_ref, d_ref, y_ref, str_ref, sti_ref,
             xr_s, xi_s, car_r, car_i):
        t = pl.program_id(2)

        @pl.when(t == 0)
        def _():
            car_r[...] = jnp.zeros_like(car_r)
            car_i[...] = jnp.zeros_like(car_i)
        pr, pi = _cpow_rows(ar_ref[0], ai_ref[0], SUBLANES)
        rm = lax.broadcasted_iota(jnp.int32, (tc, S5_STATES), 0) & (SUBLANES - 1)
        str_ref[0, 0] = car_r[...]
        sti_ref[0, 0] = car_i[...]
        u_t = u_ref[...]
        cfr, cfi = _s5_state_scan(u_t.astype(BF16), bbr_ref[0], bbi_ref[0], pr, pi, rm, xr_s, xi_s,
                                  car_r[...], car_i[...])
        car_r[...] = cfr
        car_i[...] = cfi
        y = _dot(xr_s[...].astype(BF16), cr_ref[0]) - _dot(xi_s[...].astype(BF16), ci_ref[0])
        y_ref[...] = y + d_ref[...] * u_t

    u_spec = pl.BlockSpec((tc, LANES), lambda cb, b, t: (b * nt + t, cb))
    a_spec = pl.BlockSpec((1, 1, S5_STATES), lambda cb, b, t: (cb, 0, 0))
    bb_spec = pl.BlockSpec((1, LANES, S5_STATES), lambda cb, b, t: (cb, 0, 0))
    c_spec = pl.BlockSpec((1, S5_STATES, LANES), lambda cb, b, t: (cb, 0, 0))
    st_spec = pl.BlockSpec((1, 1, SUBLANES, S5_STATES), lambda cb, b, t: (cb, b * nt + t, 0, 0))
    st_shape = jax.ShapeDtypeStruct((S5_BLOCKS, n_seq * nt, SUBLANES, S5_STATES), F32)
    return pl.pallas_call(
        body, name="s5_fwd", grid=(S5_BLOCKS, n_seq, nt),
        in_specs=[u_spec, a_spec, a_spec, bb_spec, bb_spec, c_spec, c_spec,
                  pl.BlockSpec((1, LANES), lambda cb, b, t: (0, cb))],
        out_specs=(u_spec, st_spec, st_spec),
        out_shape=(jax.ShapeDtypeStruct((n, D_SSM), F32), st_shape, st_shape),
        scratch_shapes=[pltpu.VMEM((tc, S5_STATES), F32), pltpu.VMEM((tc, S5_STATES), F32),
                        pltpu.VMEM((SUBLANES, S5_STATES), F32), pltpu.VMEM((SUBLANES, S5_STATES), F32)],
        compiler_params=_cparams("parallel", "arbitrary", "arbitrary"),
    )(u, a_re, a_im, bbr, bbi, cr, ci, d_skip)


def _s5_bwd(u, dy, st_r, st_i, a_re, a_im, bbr, bbi, cr, ci, d_skip, n_seq):
    n = u.shape[0]
    seq_len = n // n_seq
    nt = seq_len // S5_CHUNK
    tc = S5_CHUNK

    def body(u_ref, dy_ref, str_ref, sti_ref, ar_ref, ai_ref, bbr_ref, bbi_ref, cr_ref, ci_ref, d_ref,
             du_ref, dbbr_ref, dbbi_ref, dcr_ref, dci_ref, dar_ref, dai_ref, dd_ref,
             xr_s, xi_s, gr_s, gi_s, car_r, car_i):
        b, t = pl.program_id(1), pl.program_id(2)

        @pl.when((b == 0) & (t == 0))
        def _():
            for ref in (dbbr_ref, dbbi_ref, dcr_ref, dci_ref, dar_ref, dai_ref, dd_ref):
                ref[...] = jnp.zeros_like(ref)

        @pl.when(t == 0)
        def _():
            car_r[...] = jnp.zeros_like(car_r)
            car_i[...] = jnp.zeros_like(car_i)
        ar, ai = ar_ref[0], ai_ref[0]
        pr, pi = _cpow_rows(ar, ai, SUBLANES)
        row = lax.broadcasted_iota(jnp.int32, (tc, S5_STATES), 0)
        rm = row & (SUBLANES - 1)
        u_t = u_ref[...]
        u_b = u_t.astype(BF16)
        dy_t = dy_ref[...]
        dy_b = dy_t.astype(BF16)
        s0r, s0i = str_ref[0, 0], sti_ref[0, 0]
        _s5_state_scan(u_b, bbr_ref[0], bbi_ref[0], pr, pi, rm, xr_s, xi_s, s0r, s0i)
        xr, xi = xr_s[...], xi_s[...]
        gr = _dot_nt(dy_b, cr_ref[0])
        gi = -_dot_nt(dy_b, ci_ref[0])
        npi = [-v for v in pi]
        gr, gi = _scan_in_groups(gr, gi, pr, npi, rm, True)
        gr_s[...] = gr
        gi_s[...] = gi
        w8r = jnp.concatenate(pr[::-1], axis=0)
        w8i = jnp.concatenate(npi[::-1], axis=0)
        cfr, cfi = _carry_over_groups(gr_s, gi_s, w8r, w8i, car_r[...], car_i[...], True)
        car_r[...] = cfr
        car_i[...] = cfi
        gr, gi = gr_s[...], gi_s[...]
        gr_b, gi_b = gr.astype(BF16), gi.astype(BF16)
        du_ref[...] = _dot_nt(gr_b, bbr_ref[0]) + _dot_nt(gi_b, bbi_ref[0]) + d_ref[...] * dy_t
        dbbr_ref[0] += _dot_tn(u_b, gr_b)
        dbbi_ref[0] += _dot_tn(u_b, gi_b)
        dcr_ref[0] += _dot_tn(xr.astype(BF16), dy_b)
        dci_ref[0] -= _dot_tn(xi.astype(BF16), dy_b)
        dd_ref[0] += jnp.sum((dy_t * u_t).reshape(tc // SUBLANES, SUBLANES, LANES), axis=0)
        first = row == 0
        xpr = jnp.where(first, jnp.broadcast_to(s0r[0:1], xr.shape), pltpu.roll(xr, 1, 0))
        xpi = jnp.where(first, jnp.broadcast_to(s0i[0:1], xi.shape), pltpu.roll(xi, 1, 0))
        shp = (tc // SUBLANES, SUBLANES, S5_STATES)
        dar_ref[0] += jnp.sum((gr * xpr + gi * xpi).reshape(shp), axis=0)
        dai_ref[0] += jnp.sum((gi * xpr - gr * xpi).reshape(shp), axis=0)

    u_spec = pl.BlockSpec((tc, LANES), lambda cb, b, t: (b * nt + nt - 1 - t, cb))
    a_spec = pl.BlockSpec((1, 1, S5_STATES), lambda cb, b, t: (cb, 0, 0))
    bb_spec = pl.BlockSpec((1, LANES, S5_STATES), lambda cb, b, t: (cb, 0, 0))
    c_spec = pl.BlockSpec((1, S5_STATES, LANES), lambda cb, b, t: (cb, 0, 0))
    st_spec = pl.BlockSpec((1, 1, SUBLANES, S5_STATES), lambda cb, b, t: (cb, b * nt + nt - 1 - t, 0, 0))
    da_spec = pl.BlockSpec((1, SUBLANES, S5_STATES), lambda cb, b, t: (cb, 0, 0))
    dd_spec = pl.BlockSpec((1, SUBLANES, LANES), lambda cb, b, t: (cb, 0, 0))
    big = pltpu.VMEM((tc, S5_STATES), F32)
    small = pltpu.VMEM((SUBLANES, S5_STATES), F32)
    return pl.pallas_call(
        body, name="s5_bwd", grid=(S5_BLOCKS, n_seq, nt),
        in_specs=[u_spec, u_spec, st_spec, st_spec, a_spec, a_spec, bb_spec, bb_spec, c_spec, c_spec,
                  pl.BlockSpec((1, LANES), lambda cb, b, t: (0, cb))],
        out_specs=(u_spec, bb_spec, bb_spec, c_spec, c_spec, da_spec, da_spec, dd_spec),
        out_shape=(jax.ShapeDtypeStruct((n, D_SSM), F32),
                   jax.ShapeDtypeStruct((S5_BLOCKS, LANES, S5_STATES), F32),
                   jax.ShapeDtypeStruct((S5_BLOCKS, LANES, S5_STATES), F32),
                   jax.ShapeDtypeStruct((S5_BLOCKS, S5_STATES, LANES), F32),
                   jax.ShapeDtypeStruct((S5_BLOCKS, S5_STATES, LANES), F32),
                   jax.ShapeDtypeStruct((S5_BLOCKS, SUBLANES, S5_STATES), F32),
                   jax.ShapeDtypeStruct((S5_BLOCKS, SUBLANES, S5_STATES), F32),
                   jax.ShapeDtypeStruct((S5_BLOCKS, SUBLANES, LANES), F32)),
        scratch_shapes=[big, big, big, big, small, small],
        compiler_params=_cparams("parallel", "arbitrary", "arbitrary"),
    )(u, dy, st_r, st_i, a_re, a_im, bbr, bbi, cr, ci, d_skip)


CUM_BLOCK = 128


def _tri(lower):
    r = lax.broadcasted_iota(jnp.int32, (CUM_BLOCK, CUM_BLOCK), 0)
    c = lax.broadcasted_iota(jnp.int32, (CUM_BLOCK, CUM_BLOCK), 1)
    return jnp.where(r >= c if lower else r <= c, 1.0, 0.0).astype(F32)


def _fprep_fwd(fl, bf, n_seq):
    n = fl.shape[0]
    seq_len = n // n_seq
    nb = seq_len // CUM_BLOCK

    def body(fl_ref, bf_ref, cum_ref):
        tril = _tri(True)
        carry = jnp.zeros((1, LANES), F32)
        for blk in range(nb):
            rows = slice(blk * CUM_BLOCK, (blk + 1) * CUM_BLOCK)
            lf = jax.nn.log_sigmoid(fl_ref[rows, :] + bf_ref[...])
            cs = jnp.dot(tril, lf, preferred_element_type=F32, precision=HIGHEST) + carry
            cum_ref[rows, :] = cs
            carry = cs[CUM_BLOCK - 1:CUM_BLOCK, :]

    spec = pl.BlockSpec((seq_len, LANES), lambda b: (b, 0))
    return pl.pallas_call(
        body, name="fprep_fwd", grid=(n_seq,), in_specs=[spec, pl.BlockSpec((1, LANES), lambda b: (0, 0))],
        out_specs=spec, out_shape=jax.ShapeDtypeStruct((n, LANES), F32), compiler_params=_cparams("parallel"),
    )(fl, bf)


def _fprep_bwd(dcum, fl, bf, n_seq):
    n = fl.shape[0]
    seq_len = n // n_seq
    nb = seq_len // CUM_BLOCK

    def body(dcum_ref, fl_ref, bf_ref, dfl_ref, dbf_ref):
        triu = _tri(False)
        lane = lax.broadcasted_iota(jnp.int32, (CUM_BLOCK, LANES), 1)
        carry = jnp.zeros((1, LANES), F32)
        total = jnp.zeros((1, LANES), F32)
        for blk in reversed(range(nb)):
            rows = slice(blk * CUM_BLOCK, (blk + 1) * CUM_BLOCK)
            rs = jnp.dot(triu, dcum_ref[rows, :], preferred_element_type=F32, precision=HIGHEST) + carry
            carry = rs[0:1, :]
            _, vjp = jax.vjp(jax.nn.log_sigmoid, fl_ref[rows, :] + bf_ref[...])
            dz = jnp.where(lane < N_HEADS, vjp(rs)[0], 0.0)
            dfl_ref[rows, :] = dz
            total = total + jnp.sum(dz, axis=0, keepdims=True)
        dbf_ref[0] = total

    spec = pl.BlockSpec((seq_len, LANES), lambda b: (b, 0))
    return pl.pallas_call(
        body, name="fprep_bwd", grid=(n_seq,), in_specs=[spec, spec, pl.BlockSpec((1, LANES), lambda b: (0, 0))],
        out_specs=(spec, pl.BlockSpec((1, 1, LANES), lambda b: (b, 0, 0))),
        out_shape=(jax.ShapeDtypeStruct((n, LANES), F32), jax.ShapeDtypeStruct((n_seq, 1, LANES), F32)),
        compiler_params=_cparams("parallel"),
    )(dcum, fl, bf)


ATT_TQ = 256
ATT_KSTEP = 256
ATT_SCALE = HEAD_DIM ** -0.5
NEG_BIG = -1e30


assert ATT_KSTEP == ATT_TQ


def _scores(q_scaled, kb, cq, ck, kend):
    s = _dot_nt(q_scaled, kb) + cq - ck
    r = lax.broadcasted_iota(jnp.int32, (ATT_TQ, ATT_TQ), 0)
    c = lax.broadcasted_iota(jnp.int32, (ATT_TQ, ATT_TQ), 1)
    diag = jnp.where(r >= c, s[:, kend - ATT_TQ:], NEG_BIG)
    return diag if kend == ATT_TQ else jnp.concatenate([s[:, :kend - ATT_TQ], diag], axis=1)


def _attn_specs(n_seq, seq_len):
    nq = seq_len // ATT_TQ
    q_spec = pl.BlockSpec((ATT_TQ, LANES), lambda b, h, q: (b * nq + q, h))
    k_spec = pl.BlockSpec((seq_len, LANES), lambda b, h, q: (b, N_HEADS // 2 + h))
    v_spec = pl.BlockSpec((seq_len, LANES), lambda b, h, q: (b, N_HEADS + h))
    cq_spec = pl.BlockSpec((1, 2, ATT_TQ, 1), lambda b, h, q: (b, h, q, 0))
    ck_spec = pl.BlockSpec((1, 2, 1, seq_len), lambda b, h, q: (b, h, 0, 0))
    return nq, q_spec, k_spec, v_spec, cq_spec, ck_spec


def _head_selectors():
    head0 = lax.broadcasted_iota(jnp.int32, (1, LANES), 1) < HEAD_DIM
    return head0, (head0, jnp.logical_not(head0))


def _for_key_range(qi, seq_len, run):
    per = ATT_KSTEP // ATT_TQ
    for g in range(seq_len // ATT_KSTEP):
        pl.when(qi // per == g)(functools.partial(run, (g + 1) * ATT_KSTEP))


def _attn_fwd(qkv, cq, ck, n_seq):
    n = qkv.shape[0]
    seq_len = n // n_seq
    nq, q_spec, k_spec, v_spec, cq_spec, ck_spec = _attn_specs(n_seq, seq_len)

    def body(q_ref, k_ref, v_ref, cq_ref, ck_ref, o_ref, lse_ref):
        qi = pl.program_id(2)
        q2 = q_ref[...]
        head0, sels = _head_selectors()
        qe = [jnp.where(sel, q2 * ATT_SCALE, 0.0).astype(BF16) for sel in sels]

        def run(kend):
            kb = k_ref[0:kend, :].astype(BF16)
            vb = v_ref[0:kend, :].astype(BF16)
            outs = []
            for e in range(2):
                s = _scores(qe[e], kb, cq_ref[0, e], ck_ref[0, e, :, 0:kend], kend)
                mx = jnp.max(s, axis=1, keepdims=True)
                p = jnp.exp(s - mx)
                den = jnp.sum(p, axis=1, keepdims=True)
                outs.append(_dot(p.astype(BF16), vb) / den)
                lse_ref[0, e] = mx + jnp.log(den)
            o_ref[...] = jnp.where(head0, outs[0], outs[1])

        _for_key_range(qi, seq_len, run)

    return pl.pallas_call(
        body, name="attn_fwd", grid=(n_seq, N_HEADS // 2, nq),
        in_specs=[q_spec, k_spec, v_spec, cq_spec, ck_spec],
        out_specs=(q_spec, cq_spec),
        out_shape=(jax.ShapeDtypeStruct((n, D_ATTN), F32), jax.ShapeDtypeStruct((n_seq, N_HEADS, seq_len, 1), F32)),
        compiler_params=_cparams("parallel", "parallel", "parallel"),
    )(qkv, qkv, qkv, cq, ck)


def _attn_bwd(qkv, cq, ck, o, do, lse, n_seq):
    n = qkv.shape[0]
    seq_len = n // n_seq
    nq, q_spec, k_spec, v_spec, cq_spec, ck_spec = _attn_specs(n_seq, seq_len)
    kv_out = pl.BlockSpec((seq_len, LANES), lambda b, h, q: (b, h))

    def body(q_ref, k_ref, v_ref, cq_ref, ck_ref, o_ref, do_ref, lse_ref, dq_ref, dk_ref, dv_ref, dcq_ref, dck_ref):
        qi = pl.program_id(2)

        @pl.when(qi == 0)
        def _():
            dk_ref[...] = jnp.zeros_like(dk_ref)
            dv_ref[...] = jnp.zeros_like(dv_ref)
            dck_ref[...] = jnp.zeros_like(dck_ref)
        q2 = q_ref[...]
        do2 = do_ref[...]
        o2 = o_ref[...]
        head0, sels = _head_selectors()
        qe = [jnp.where(sel, q2 * ATT_SCALE, 0.0).astype(BF16) for sel in sels]
        doe = [jnp.where(sel, do2, 0.0) for sel in sels]
        doe_b = [d.astype(BF16) for d in doe]
        delta = [jnp.sum(d * o2, axis=1, keepdims=True) for d in doe]

        def run(kend):
            kb = k_ref[0:kend, :].astype(BF16)
            vb = v_ref[0:kend, :].astype(BF16)
            dqs = []
            dk = jnp.zeros((kend, LANES), F32)
            dv = jnp.zeros((kend, LANES), F32)
            for e in range(2):
                s = _scores(qe[e], kb, cq_ref[0, e], ck_ref[0, e, :, 0:kend], kend)
                p = jnp.exp(s - lse_ref[0, e])
                ds = p * (_dot_nt(doe_b[e], vb) - delta[e])
                ds_b = ds.astype(BF16)
                dqs.append(_dot(ds_b, kb))
                dk = dk + _dot_tn(ds_b, qe[e])
                dv = dv + _dot_tn(p.astype(BF16), doe_b[e])
                dcq_ref[0, e] = jnp.sum(ds, axis=1, keepdims=True)
                dck_ref[0, e, :, 0:kend] -= jnp.sum(ds, axis=0, keepdims=True)
            dk_ref[0:kend, :] += dk
            dv_ref[0:kend, :] += dv
            dq_ref[...] = jnp.where(head0, dqs[0], dqs[1]) * ATT_SCALE

        _for_key_range(qi, seq_len, run)

    return pl.pallas_call(
        body, name="attn_bwd", grid=(n_seq, N_HEADS // 2, nq),
        in_specs=[q_spec, k_spec, v_spec, cq_spec, ck_spec, q_spec, q_spec, cq_spec],
        out_specs=(q_spec, kv_out, kv_out, cq_spec, ck_spec),
        out_shape=(jax.ShapeDtypeStruct((n, D_ATTN), F32), jax.ShapeDtypeStruct((n, D_ATTN), F32),
                   jax.ShapeDtypeStruct((n, D_ATTN), F32),
                   jax.ShapeDtypeStruct((n_seq, N_HEADS, seq_len, 1), F32),
                   jax.ShapeDtypeStruct((n_seq, N_HEADS, 1, seq_len), F32)),
        compiler_params=_cparams("parallel", "parallel", "arbitrary"),
    )(qkv, qkv, qkv, cq, ck, o, do, lse)


WEIGHT_NAMES = ("norm_mix", "w_in", "b_forget", "lam_re", "lam_im", "b_re", "b_im", "c_re", "c_im", "d_skip", "log_dt",
                "w_glu", "b_glu", "q_norm", "k_norm", "norm_out_ssm", "norm_out_attn", "w_out", "norm_ffn", "w_up",
                "conv_w", "conv_b", "w_down")
SHARDED = ("w_in", "w_glu", "w_out", "w_up", "conv_w", "w_down")
ADAM_ROWS = {"w_in": 256, "w_glu": 64, "w_out": 128, "w_up": 128, "conv_w": 3, "w_down": 344}
PACK_ROWS = SUBLANES * LANES
ROWS_DOWN = D_FF // N_DEV
ROWS_OUT = D_MODEL // N_DEV
ROWS_GLU = D_SSM * D_SSM // N_DEV // D_MODEL


def _pad_to(a, axis, size):
    pad = [(0, 0)] * a.ndim
    pad[axis] = (0, size - a.shape[axis])
    return jnp.pad(a, pad)


def _block_diag(t, transpose):
    t4 = t.reshape(S5_BLOCKS, 8, SSM_GROUP, SSM_STATE)
    eye = jnp.eye(8, dtype=t.dtype)
    if transpose:
        e = jnp.swapaxes(t4, 2, 3)[:, :, :, None, :] * eye[None, :, None, :, None]
        return e.reshape(S5_BLOCKS, S5_STATES, LANES)
    e = t4[:, :, :, None, :] * eye[None, :, None, :, None]
    return e.reshape(S5_BLOCKS, LANES, S5_STATES)


def _block_diag_extract(m, transpose):
    if transpose:
        m5 = m.reshape(S5_BLOCKS, 8, SSM_STATE, 8, SSM_GROUP)
        d = jnp.stack([m5[:, i, :, i, :] for i in range(8)], axis=1)
        return jnp.swapaxes(d, 2, 3).reshape(N_GROUPS, SSM_GROUP, SSM_STATE)
    m5 = m.reshape(S5_BLOCKS, 8, SSM_GROUP, 8, SSM_STATE)
    d = jnp.stack([m5[:, i, :, i, :] for i in range(8)], axis=1)
    return d.reshape(N_GROUPS, SSM_GROUP, SSM_STATE)


def _pack(pieces):
    flat = jnp.concatenate([p.reshape(-1).astype(F32) for p in pieces])
    size = -(-flat.shape[0] // PACK_ROWS) * PACK_ROWS
    return _pad_to(flat, 0, size).reshape(-1, LANES)


def _unpack(packed, shapes):
    flat = packed.reshape(-1)
    out, off = [], 0
    for shp in shapes:
        size = math.prod(shp)
        out.append(flat[off:off + size].reshape(shp))
        off += size
    return out


def kernel(x, norm_mix, w_in, b_forget, lam_re, lam_im, b_re, b_im, c_re, c_im, d_skip, log_dt, w_glu, b_glu, q_norm, k_norm, norm_out_ssm, norm_out_attn, w_out, norm_ffn, w_up, conv_w, conv_b, w_down, loss_target, m_norm_mix, m_w_in, m_b_forget, m_lam_re, m_lam_im, m_b_re, m_b_im, m_c_re, m_c_im, m_d_skip, m_log_dt, m_w_glu, m_b_glu, m_q_norm, m_k_norm, m_norm_out_ssm, m_norm_out_attn, m_w_out, m_norm_ffn, m_w_up, m_conv_w, m_conv_b, m_w_down, v_norm_mix, v_w_in, v_b_forget, v_lam_re, v_lam_im, v_b_re, v_b_im, v_c_re, v_c_im, v_d_skip, v_log_dt, v_w_glu, v_b_glu, v_q_norm, v_k_norm, v_norm_out_ssm, v_norm_out_attn, v_w_out, v_norm_ffn, v_w_up, v_conv_w, v_conv_b, v_w_down):
    given = dict(locals())
    weights = {k: given[k] for k in WEIGHT_NAMES}
    mom1 = {k: given["m_" + k] for k in WEIGHT_NAMES}
    mom2 = {k: given["v_" + k] for k in WEIGHT_NAMES}
    n_seq, seq_len, _ = x.shape
    n = n_seq * seq_len
    xf = x.reshape(n, D_MODEL)
    target = loss_target.reshape(n, D_MODEL)
    me_idx = 4 * lax.axis_index("x") + 2 * lax.axis_index("y") + lax.axis_index("c")

    g_in, g_cw = _exchange([w_in[0].astype(BF16), conv_w[0]], [False, False], "gather_w_in")
    rest_flags = [False] * 2
    rows_w = jnp.concatenate([w_down[0], w_out[0], w_glu[0].reshape(ROWS_GLU, D_MODEL)], axis=0).astype(BF16)
    w_sems = _exchange_start([rows_w, w_up[0].astype(BF16)], rest_flags, g_in, "gather_rest_start", 0)
    norm_mix = norm_mix + w_sems[4][0, 0]
    w_in_p = _pad_to(jnp.swapaxes(g_in, 0, 1).reshape(D_MODEL, D_IN), 1, D_IN_PAD)

    lr3 = lam_re[0].reshape(N_GROUPS, 1, SSM_STATE)
    li3 = lam_im[0].reshape(N_GROUPS, 1, SSM_STATE)
    ldt3 = log_dt[0].reshape(N_GROUPS, 1, 1)
    br_t = jnp.swapaxes(b_re[0], 1, 2)
    bi_t = jnp.swapaxes(b_im[0], 1, 2)
    ab_re, ab_im, bb_re, bb_im = _s5_param_fwd(lr3, li3, ldt3, br_t, bi_t)
    a_re = ab_re.reshape(S5_BLOCKS, 1, S5_STATES)
    a_im = ab_im.reshape(S5_BLOCKS, 1, S5_STATES)
    bbr = _block_diag(bb_re, False).astype(BF16)
    bbi = _block_diag(bb_im, False).astype(BF16)
    cr = _block_diag(c_re[0], True).astype(BF16)
    ci = _block_diag(c_im[0], True).astype(BF16)

    avg = jnp.kron(jnp.eye(N_HEADS, dtype=F32), jnp.full((HEAD_DIM, HEAD_DIM), 1.0 / HEAD_DIM, F32)).astype(BF16)
    qg = jnp.tile(q_norm, (1, N_HEADS))
    kg = jnp.tile(k_norm, (1, N_HEADS))
    hn, u, qkv, raw, fl = _inproj_fwd(xf, norm_mix, w_in_p, avg, qg, kg)
    yc, st_r, st_i = _s5_fwd(u, a_re, a_im, bbr, bbi, cr, ci, d_skip, n_seq)
    bf = _pad_to(b_forget, 1, LANES)
    cum = _fprep_fwd(fl, bf, n_seq)
    cum8 = jnp.swapaxes(cum[:, :N_HEADS].reshape(n_seq, seq_len, N_HEADS), 1, 2)
    cq = cum8[:, :, :, None]
    ck = cum8[:, :, None, :]
    ya, lse = _attn_fwd(qkv, cq, ck, n_seq)
    (own_rows, own_up), (g_rows, g_up) = _exchange_wait(w_sems[0], w_sems[1], w_sems[2], w_sems[3], rest_flags, ya,
                                                        "gather_rest_wait")
    my_slot = lax.broadcasted_iota(jnp.int32, (N_DEV, 1, 1), 0) == me_idx
    g_rows = jnp.where(my_slot, own_rows[None], g_rows)
    g_up = jnp.where(my_slot, own_up[None], g_up)
    g_down = g_rows[:, :ROWS_DOWN]
    g_out = g_rows[:, ROWS_DOWN:ROWS_DOWN + ROWS_OUT]
    g_glu = g_rows[:, ROWS_DOWN + ROWS_OUT:]
    w_glu_f = g_glu.reshape(D_SSM, D_SSM)
    w_out_f = g_out.reshape(D_MODEL, D_MODEL)
    conv_st = _pad_to(jnp.concatenate([g_cw, conv_b.reshape(N_DEV, 1, -1)], axis=1), 1, SUBLANES)
    w_down4 = g_down.reshape(FFN_GROUPS, FFN_GROUP, D_MODEL)
    ys = _glu_fwd(yc, w_glu_f, b_glu)
    h1, hn2, mixed = _mix_fwd(xf, ys, ya, norm_out_ssm, norm_out_attn, w_out_f, norm_ffn)
    ug, uv, dy, loss_part = _ffn_fwd(hn2, h1, target, g_up, conv_st, w_down4, seq_len)
    loss_local = 0.5 * jnp.sum(loss_part) / D_MODEL

    dug, duv, act, dhn2, dcg, dcv = _ffn_bwd(dy, ug, uv, g_up, conv_st, w_down4, seq_len)
    dh1, dys, dya, d_gs, d_ga, d_gf = _mix_bwd(dy, dhn2[None], h1, ys, ya, norm_out_ssm, norm_out_attn, w_out_f, norm_ffn)
    dyc, gl_b, dz_b, d_bglu = _glu_bwd(yc, dys, w_glu_f, b_glu)

    gw_glu = _tn_matmul(gl_b, dz_b, "dw_glu", D_SSM, D_SSM)
    gw_out = _tn_matmul(mixed, dh1, "dw_out", D_MODEL, D_MODEL)
    gw_up = jnp.concatenate([_tn_grouped(hn2, dug, "dw_up_gate", True, BF16),
                             _tn_grouped(hn2, duv, "dw_up_val", True, BF16)], axis=0)
    gw_down = _tn_grouped(act, dy, "dw_down", False)
    g_conv = jnp.concatenate([dcg, dcv], axis=0)
    by_cols = lambda g, c: jnp.swapaxes(g.reshape(g.shape[0], N_DEV, c), 0, 1)
    early_flags = [True] * 2
    rows_g = jnp.concatenate([gw_down.reshape(N_DEV, ROWS_DOWN, D_MODEL), gw_out.reshape(N_DEV, ROWS_OUT, D_MODEL),
                              gw_glu.reshape(N_DEV, ROWS_GLU, D_MODEL)], axis=1).astype(BF16)
    g_sems = _exchange_start([rows_g, gw_up], early_flags, dyc, "grad_early_start", 1)
    started = g_sems[4][0, 0]

    du, dbbr, dbbi, dcr, dci, dar, dai, ddk = _s5_bwd(u, dyc, st_r, st_i, a_re, a_im, bbr, bbi, cr, ci,
                                                      d_skip + started, n_seq)
    partial_early = {
        "ab_re": jnp.sum(dar, axis=1), "ab_im": jnp.sum(dai, axis=1),
        "bb_re": _block_diag_extract(dbbr, False), "bb_im": _block_diag_extract(dbbi, False),
        "c_re": _block_diag_extract(dcr, True), "c_im": _block_diag_extract(dci, True),
        "d_skip": jnp.sum(ddk, axis=1), "b_glu": d_bglu,
        "norm_out_ssm": d_gs, "norm_out_attn": d_ga, "norm_ffn": d_gf, "conv_b": g_conv[:, 3],
    }
    early_keys = tuple(partial_early)
    early_shapes = [partial_early[k].shape for k in early_keys]
    p_sems = _exchange_start([_pack([partial_early[k] for k in early_keys])], [False], du, "small_early_start", 2)
    started = started + p_sems[4][0, 0]

    dqn, dkn, dv, dcq, dck = _attn_bwd(qkv, cq, ck + started, ya, dya, lse, n_seq)
    dcum8 = dcq[:, :, :, 0] + dck.reshape(n_seq, N_HEADS, seq_len)
    dcum = _pad_to(jnp.swapaxes(dcum8, 1, 2).reshape(n, N_HEADS), 1, LANES)
    dfl, dbf = _fprep_bwd(dcum, fl, bf, n_seq)
    dx, dproj, d_gmix, d_qg, d_kg = _inproj_bwd(xf, norm_mix, w_in_p, avg, qg, kg, raw, du, dqn, dkn, dv, dfl, dh1)

    gw_in = _tn_matmul(hn, dproj, "dw_in", D_MODEL, D_IN_PAD, out_cols=D_IN)
    partial_late = {
        "norm_mix": d_gmix, "b_forget": jnp.sum(dbf, axis=(0, 1))[:N_HEADS],
        "q_norm": jnp.sum(d_qg.reshape(N_HEADS, HEAD_DIM), axis=0),
        "k_norm": jnp.sum(d_kg.reshape(N_HEADS, HEAD_DIM), axis=0), "loss": loss_local.reshape(1),
    }
    late_keys = tuple(partial_late)
    late_shapes = [partial_late[k].shape for k in late_keys]

    land_in, land_cw, small_parts = _exchange(
        [by_cols(gw_in, D_IN // N_DEV).astype(BF16), g_conv[:, :3],
         _pack([partial_late[k] for k in late_keys])], [True, True, False], "grad_late_exchange")
    (src_rows, src_up), (land_rows, land_up) = _exchange_wait(g_sems[0], g_sems[1], g_sems[2], g_sems[3], early_flags,
                                                              land_in, "grad_early_wait")
    land_down = land_rows[:, :ROWS_DOWN]
    land_out = land_rows[:, ROWS_DOWN:ROWS_DOWN + ROWS_OUT]
    land_glu = land_rows[:, ROWS_DOWN + ROWS_OUT:].reshape(N_DEV, -1, D_SSM)
    own_rows = lax.dynamic_index_in_dim(src_rows, me_idx, 0, keepdims=False)
    own_up = lax.dynamic_index_in_dim(src_up, me_idx, 0, keepdims=False)
    own = {"w_in": None, "conv_w": None, "w_up": own_up, "w_down": own_rows[:ROWS_DOWN],
           "w_out": own_rows[ROWS_DOWN:ROWS_DOWN + ROWS_OUT], "w_glu": own_rows[ROWS_DOWN + ROWS_OUT:].reshape(-1, D_SSM)}
    grads, deltas, new_m, new_v = {}, {}, {}, {}
    for name, land in zip(SHARDED, (land_in, land_glu, land_out, land_up, land_cw, land_down)):
        grads[name], deltas[name], new_m[name], new_v[name] = _adam_sharded(
            land, own[name], weights[name], mom1[name], mom2[name], "adam_" + name, ADAM_ROWS[name])

    (own_pack,), (early_parts,) = _exchange_wait(p_sems[0], p_sems[1], p_sems[2], p_sems[3], [False], land_up,
                                                 "small_early_wait")
    summed = dict(zip(late_keys, _unpack(_sum_partials(small_parts, None, "sum_late_partials"), late_shapes)))
    summed.update(zip(early_keys, _unpack(_sum_partials(early_parts, own_pack, "sum_early_partials"), early_shapes)))
    dlr, dli, dldt, dbr_t, dbi_t = _s5_param_bwd(
        lr3, li3, ldt3, br_t, bi_t, summed["ab_re"].reshape(lr3.shape), summed["ab_im"].reshape(lr3.shape),
        summed["bb_re"], summed["bb_im"])
    small_grads = {
        "norm_mix": summed["norm_mix"], "b_forget": summed["b_forget"], "lam_re": dlr, "lam_im": dli,
        "b_re": jnp.swapaxes(dbr_t, 1, 2), "b_im": jnp.swapaxes(dbi_t, 1, 2), "c_re": summed["c_re"], "c_im": summed["c_im"],
        "d_skip": summed["d_skip"], "log_dt": dldt, "b_glu": summed["b_glu"], "q_norm": summed["q_norm"],
        "k_norm": summed["k_norm"], "norm_out_ssm": summed["norm_out_ssm"], "norm_out_attn": summed["norm_out_attn"],
        "norm_ffn": summed["norm_ffn"], "conv_b": summed["conv_b"],
    }
    repl = tuple(k for k in WEIGHT_NAMES if k not in SHARDED)
    g_list = [small_grads[k].reshape(weights[k].shape) for k in repl]
    d_list, m_list, v_list = _adam_replicated(g_list, [weights[k] for k in repl], [mom1[k] for k in repl],
                                              [mom2[k] for k in repl], "adam_replicated")
    for k, g, d, nm, nv in zip(repl, g_list, d_list, m_list, v_list):
        grads[k], deltas[k], new_m[k], new_v[k] = g, d, nm, nv

    grad_x = dx.reshape(x.shape)
    loss = summed["loss"].reshape(())
    return (loss, grad_x, *[grads[k] for k in WEIGHT_NAMES], *[deltas[k] for k in WEIGHT_NAMES],
            *[new_m[k] for k in WEIGHT_NAMES], *[new_v[k] for k in WEIGHT_NAMES])
```

```python
import functools
import math

import jax
import jax.numpy as jnp
from jax import lax
from jax.experimental import pallas as pl
from jax.experimental.pallas import tpu as pltpu

F32 = jnp.float32
BF16 = jnp.bfloat16
HIGHEST = lax.Precision.HIGHEST

N_DEV = 8
D_MODEL = 1024
D_SSM = 512
D_ATTN = 512
N_HEADS = 8
HEAD_DIM = 64
N_GROUPS = 32
SSM_GROUP = 16
SSM_STATE = 64
D_FF = 2752
D_FF_PAD = 2816
D_IN = 2056
D_IN_PAD = 2176
EPS = 1e-6
LANES = 128
SUBLANES = 8
VMEM_LIMIT = 56 * 1024 * 1024

ADAM_LR = 0.001
ADAM_B1 = 0.9
ADAM_B2 = 0.999
ADAM_EPS = 1e-08
ADAM_WD = 0.01
ADAM_STEP = 10


def _cparams(*sem):
    return pltpu.CompilerParams(dimension_semantics=sem, vmem_limit_bytes=VMEM_LIMIT)


def _dot(a, b, **kw):
    return jnp.dot(a, b, preferred_element_type=F32, **kw)


def _dot_nt(a, b):
    return lax.dot_general(a, b, (((1,), (1,)), ((), ())), preferred_element_type=F32)


def _dot_tn(a, b):
    return lax.dot_general(a, b, (((0,), (0,)), ((), ())), preferred_element_type=F32)


def _rms(x, g):
    return x * lax.rsqrt(jnp.mean(x * x, axis=-1, keepdims=True) + EPS) * g


def _split_dot(x, avg):
    hi = x.astype(BF16)
    lo = (x - hi.astype(F32)).astype(BF16)
    return _dot(hi, avg) + _dot(lo, avg)


@jax.custom_vjp
def _group_mean(x, avg):
    return _split_dot(x, avg)


def _group_mean_fwd(x, avg):
    return _split_dot(x, avg), avg


def _group_mean_bwd(avg, ct):
    return _split_dot(ct, avg), jnp.zeros_like(avg)


_group_mean.defvjp(_group_mean_fwd, _group_mean_bwd)


def _headnorm(q, avg, g):
    return q * lax.rsqrt(_group_mean(q * q, avg) + EPS) * g


def _exchange(srcs, scatter_flags, name):
    n = len(srcs)
    out_shape = []
    for s, sc in zip(srcs, scatter_flags):
        shp = s.shape if sc else (N_DEV,) + s.shape
        out_shape.append(jax.ShapeDtypeStruct(shp, s.dtype))

    def body(*refs):
        src = refs[:n]
        dst = refs[n:2 * n]
        send_sems, recv_sems, loc_sems = refs[2 * n:]
        x, y, c = lax.axis_index("x"), lax.axis_index("y"), lax.axis_index("c")
        me = 4 * x + 2 * y + c
        peers = []
        for j in range(1, N_DEV):
            px = 1 - x if (j >> 2) & 1 else x
            py = 1 - y if (j >> 1) & 1 else y
            pc = 1 - c if j & 1 else c
            peers.append(((px, py, pc), 4 * px + 2 * py + pc))
        local, sends = [], []
        for k in range(n):
            own = src[k].at[me] if scatter_flags[k] else src[k]
            lc = pltpu.make_async_copy(own, dst[k].at[me], loc_sems.at[k])
            lc.start()
            local.append(lc)
            for j, (pid, pidx) in enumerate(peers):
                s = src[k].at[pidx] if scatter_flags[k] else src[k]
                cp = pltpu.make_async_remote_copy(
                    src_ref=s, dst_ref=dst[k].at[me], send_sem=send_sems.at[k, j], recv_sem=recv_sems.at[k, j],
                    device_id=pid, device_id_type=pl.DeviceIdType.MESH)
                cp.start()
                sends.append(cp)
        for k in range(n):
            for j, (pid, pidx) in enumerate(peers):
                s = src[k].at[pidx] if scatter_flags[k] else src[k]
                pltpu.make_async_remote_copy(
                    src_ref=s, dst_ref=dst[k].at[pidx], send_sem=send_sems.at[k, j], recv_sem=recv_sems.at[k, j],
                    device_id=pid, device_id_type=pl.DeviceIdType.MESH).wait_recv()
        for cp in sends:
            cp.wait_send()
        for lc in local:
            lc.wait()

    any_spec = pl.BlockSpec(memory_space=pl.ANY)
    return pl.pallas_call(
        body, name=name, out_shape=tuple(out_shape),
        in_specs=[any_spec] * n, out_specs=tuple([any_spec] * n),
        scratch_shapes=[pltpu.SemaphoreType.DMA((n, N_DEV - 1)), pltpu.SemaphoreType.DMA((n, N_DEV - 1)),
                        pltpu.SemaphoreType.DMA((n,))],
        compiler_params=pltpu.CompilerParams(has_side_effects=True),
    )(*srcs)


def _peer_list():
    x, y, c = lax.axis_index("x"), lax.axis_index("y"), lax.axis_index("c")
    peers = []
    for j in range(1, N_DEV):
        px = 1 - x if (j >> 2) & 1 else x
        py = 1 - y if (j >> 1) & 1 else y
        pc = 1 - c if j & 1 else c
        peers.append(((px, py, pc), 4 * px + 2 * py + pc))
    return 4 * x + 2 * y + c, peers


def _split_copies(src, land, send_sems, recv_sems, scatter_flags, me, peers, incoming):
    copies = []
    for k in range(len(src)):
        for j, (pid, pidx) in enumerate(peers):
            s = src[k].at[pidx] if scatter_flags[k] else src[k]
            i = k * (N_DEV - 1) + j
            copies.append(pltpu.make_async_remote_copy(
                src_ref=s, dst_ref=land[k].at[pidx if incoming else me], send_sem=send_sems[i],
                recv_sem=recv_sems[i], device_id=pid, device_id_type=pl.DeviceIdType.MESH))
    return copies


def _exchange_start(srcs, scatter_flags, after, name, collective_id):
    n = len(srcs)
    ns = n * (N_DEV - 1)
    hbm = pl.BlockSpec(memory_space=pltpu.HBM)
    sem = pl.BlockSpec(memory_space=pltpu.SEMAPHORE)
    land_shapes = [s.shape if sc else (N_DEV,) + s.shape for s, sc in zip(srcs, scatter_flags)]

    def body(*refs):
        src, land = refs[:n], refs[n:2 * n]
        send_sems = refs[2 * n + 1:2 * n + 1 + ns]
        recv_sems = refs[2 * n + 1 + ns:2 * n + 1 + 2 * ns]
        token = refs[4 * n + 1 + 2 * ns]
        me, peers = _peer_list()
        barrier = pltpu.get_barrier_semaphore()
        for pid, _ in peers:
            pl.semaphore_signal(barrier, inc=1, device_id=pid, device_id_type=pl.DeviceIdType.MESH)
        pl.semaphore_wait(barrier, N_DEV - 1)
        for cp in _split_copies(src, land, send_sems, recv_sems, scatter_flags, me, peers, False):
            cp.start()
        token[...] = jnp.zeros_like(token)

    outs = pl.pallas_call(
        body, name=name,
        out_shape=(*[pltpu.SemaphoreType.DMA(())] * (2 * ns), *[pltpu.HBM(s.shape, s.dtype) for s in srcs],
                   *[pltpu.HBM(shp, s.dtype) for shp, s in zip(land_shapes, srcs)],
                   jax.ShapeDtypeStruct((SUBLANES, LANES), F32)),
        in_specs=[hbm] * (2 * n) + [pl.BlockSpec(memory_space=pl.ANY)],
        out_specs=(*[sem] * (2 * ns), *[hbm] * (2 * n), pl.BlockSpec(memory_space=pltpu.VMEM)),
        input_output_aliases={i: 2 * ns + i for i in range(2 * n)},
        compiler_params=pltpu.CompilerParams(has_side_effects=pltpu.SideEffectType.DATAFLOW_SIDE_EFFECTING,
                                             collective_id=collective_id),
    )(*[pltpu.with_memory_space_constraint(s, pltpu.HBM) for s in srcs],
      *[pltpu.with_memory_space_constraint(lax.empty(shp, s.dtype), pltpu.HBM) for shp, s in zip(land_shapes, srcs)],
      after)
    return (outs[:ns], outs[ns:2 * ns], outs[2 * ns:2 * ns + n], outs[2 * ns + n:2 * ns + 2 * n], outs[2 * ns + 2 * n])


def _exchange_wait(send_sems, recv_sems, srcs, lands, scatter_flags, after, name):
    n = len(srcs)
    ns = n * (N_DEV - 1)
    hbm = pl.BlockSpec(memory_space=pltpu.HBM)
    sem = pl.BlockSpec(memory_space=pltpu.SEMAPHORE)

    def body(*refs):
        src, land = refs[:n], refs[n:2 * n]
        s_sems = refs[2 * n:2 * n + ns]
        r_sems = refs[2 * n + ns:2 * n + 2 * ns]
        me, peers = _peer_list()
        for cp in _split_copies(src, land, s_sems, r_sems, scatter_flags, me, peers, True):
            cp.wait_send()
            cp.wait_recv()

    outs = pl.pallas_call(
        body, name=name,
        out_shape=tuple(pltpu.HBM(a.shape, a.dtype) for a in (*srcs, *lands)),
        in_specs=[hbm] * (2 * n) + [sem] * (2 * ns) + [pl.BlockSpec(memory_space=pl.ANY)],
        out_specs=tuple([hbm] * (2 * n)),
        input_output_aliases={i: i for i in range(2 * n)},
        compiler_params=pltpu.CompilerParams(has_side_effects=pltpu.SideEffectType.DATAFLOW_SIDE_EFFECTING),
    )(*srcs, *lands, *send_sems, *recv_sems, after)
    return outs[:n], outs[n:]


def _tn_matmul(a, b, name, tk, tm, out_rows=None, out_cols=None, tn=512):
    n_tok, k_dim = a.shape
    m_dim = b.shape[1]
    grid = (k_dim // tk, m_dim // tm, n_tok // tn)

    def body(a_ref, b_ref, o_ref):
        @pl.when(pl.program_id(2) == 0)
        def _():
            o_ref[...] = jnp.zeros_like(o_ref)
        o_ref[...] += _dot_tn(a_ref[...].astype(BF16), b_ref[...].astype(BF16))

    return pl.pallas_call(
        body, name=name, grid=grid,
        in_specs=[pl.BlockSpec((tn, tk), lambda i, j, k: (k, i)), pl.BlockSpec((tn, tm), lambda i, j, k: (k, j))],
        out_specs=pl.BlockSpec((tk, tm), lambda i, j, k: (i, j)),
        out_shape=jax.ShapeDtypeStruct((out_rows or k_dim, out_cols or m_dim), F32),
        compiler_params=_cparams("parallel", "parallel", "arbitrary"),
    )(a, b)


def _adam_math(g, w, m, v):
    m = ADAM_B1 * m + (1.0 - ADAM_B1) * g
    v = ADAM_B2 * v + (1.0 - ADAM_B2) * (g * g)
    m_hat = m / (1.0 - ADAM_B1 ** ADAM_STEP)
    v_hat = v / (1.0 - ADAM_B2 ** ADAM_STEP)
    delta = -ADAM_LR * (m_hat / (jnp.sqrt(v_hat) + ADAM_EPS) + ADAM_WD * w)
    return delta, m, v


def _adam_sharded(land, own, w, m, v, name, tr):
    _, r, c = w.shape

    def body(*refs):
        l_ref = refs[0]
        own_ref = refs[1] if own is not None else None
        w_ref, m_ref, v_ref, g_ref, d_ref, nm_ref, nv_ref = [ref.at[0] for ref in refs[-7:]]
        if own_ref is not None:
            x, y, z = lax.axis_index("x"), lax.axis_index("y"), lax.axis_index("c")
            me = 4 * x + 2 * y + z
            mine = own_ref[...].astype(F32)
        g = None
        for s in range(N_DEV):
            part = l_ref[s].astype(F32)
            if own_ref is not None:
                part = jnp.where(me == s, mine, part)
            g = part if g is None else g + part
        d, nm, nv = _adam_math(g, w_ref[...], m_ref[...], v_ref[...])
        g_ref[...] = g
        d_ref[...] = d
        nm_ref[...] = nm
        nv_ref[...] = nv

    spec = pl.BlockSpec((1, tr, c), lambda i: (0, i, 0))
    own_specs, own_args = ([pl.BlockSpec((tr, c), lambda i: (i, 0))], [own]) if own is not None else ([], [])
    return pl.pallas_call(
        body, name=name, grid=(r // tr,),
        in_specs=[pl.BlockSpec((N_DEV, tr, c), lambda i: (0, i, 0)), *own_specs, spec, spec, spec],
        out_specs=(spec, spec, spec, spec),
        out_shape=tuple(jax.ShapeDtypeStruct((1, r, c), F32) for _ in range(4)),
        compiler_params=_cparams("parallel"),
    )(land, *own_args, w, m, v)


def _sum_partials(parts, own, name):
    _, r, c = parts.shape

    def body(*refs):
        p_ref, o_ref = refs[0], refs[-1]
        if own is not None:
            x, y, z = lax.axis_index("x"), lax.axis_index("y"), lax.axis_index("c")
            me = 4 * x + 2 * y + z
            mine = refs[1][...]
        g = None
        for s in range(N_DEV):
            part = p_ref[s]
            if own is not None:
                part = jnp.where(me == s, mine, part)
            g = part if g is None else g + part
        o_ref[...] = g

    args = (parts,) if own is None else (parts, own)
    return pl.pallas_call(body, name=name, out_shape=jax.ShapeDtypeStruct((r, c), F32),
                          compiler_params=pltpu.CompilerParams(vmem_limit_bytes=VMEM_LIMIT))(*args)


def _adam_replicated(gs, ws, ms, vs, name):
    k = len(ws)

    def body(*refs):
        outs = refs[4 * k:]
        for i in range(k):
            d, nm, nv = _adam_math(refs[i][...], refs[k + i][...], refs[2 * k + i][...], refs[3 * k + i][...])
            outs[i][...] = d
            outs[k + i][...] = nm
            outs[2 * k + i][...] = nv

    outs = pl.pallas_call(body, name=name, out_shape=tuple(jax.ShapeDtypeStruct(w.shape, F32) for w in ws) * 3,
                          compiler_params=pltpu.CompilerParams(vmem_limit_bytes=VMEM_LIMIT))(*gs, *ws, *ms, *vs)
    return outs[:k], outs[k:2 * k], outs[2 * k:]


def _inproj_fwd(x, g, w_in, avg, qg, kg, tm=512):
    n = x.shape[0]

    def body(x_ref, g_ref, w_ref, a_ref, qg_ref, kg_ref, hn_ref, u_ref, qkv_ref, raw_ref, fl_ref):
        hn = _rms(x_ref[...], g_ref[...]).astype(BF16)
        hn_ref[...] = hn
        proj = _dot(hn, w_ref[...])
        u_ref[...] = proj[:, 0:512]
        q = proj[:, 512:1024]
        k = proj[:, 1024:1536]
        raw_ref[:, 0:512] = q
        raw_ref[:, 512:1024] = k
        qkv_ref[:, 0:512] = _headnorm(q, a_ref[...], qg_ref[...])
        qkv_ref[:, 512:1024] = _headnorm(k, a_ref[...], kg_ref[...])
        qkv_ref[:, 1024:1536] = proj[:, 1536:2048]
        fl_ref[...] = proj[:, 2048:D_IN_PAD]

    row = lambda w: pl.BlockSpec((tm, w), lambda i: (i, 0))
    full = lambda a: pl.BlockSpec(a.shape, lambda i: (0,) * a.ndim)
    return pl.pallas_call(
        body, name="inproj_fwd", grid=(n // tm,),
        in_specs=[row(D_MODEL), full(g), full(w_in), full(avg), full(qg), full(kg)],
        out_specs=(row(D_MODEL), row(512), row(1536), row(1024), row(LANES)),
        out_shape=(jax.ShapeDtypeStruct((n, D_MODEL), BF16), jax.ShapeDtypeStruct((n, 512), F32),
                   jax.ShapeDtypeStruct((n, 1536), F32), jax.ShapeDtypeStruct((n, 1024), F32),
                   jax.ShapeDtypeStruct((n, LANES), F32)),
        compiler_params=_cparams("parallel"),
    )(x, g, w_in, avg, qg, kg)


def _inproj_bwd(x, g, w_in, avg, qg, kg, raw, du, dqn, dkn, dv, dfl, dres, tm=512):
    n = x.shape[0]

    def body(x_ref, g_ref, w_ref, a_ref, qg_ref, kg_ref, raw_ref, du_ref, dqn_ref, dkn_ref, dv_ref, dfl_ref, dres_ref,
             dx_ref, dproj_ref, dg_ref, dqg_ref, dkg_ref):
        @pl.when(pl.program_id(0) == 0)
        def _():
            dg_ref[...] = jnp.zeros_like(dg_ref)
            dqg_ref[...] = jnp.zeros_like(dqg_ref)
            dkg_ref[...] = jnp.zeros_like(dkg_ref)
        avg_m = a_ref[...]
        _, vjp_q = jax.vjp(lambda q, gg: _headnorm(q, avg_m, gg), raw_ref[:, 0:512], qg_ref[...])
        dq, dqg = vjp_q(dqn_ref[...])
        _, vjp_k = jax.vjp(lambda k, gg: _headnorm(k, avg_m, gg), raw_ref[:, 512:1024], kg_ref[...])
        dk, dkg = vjp_k(dkn_ref[...])
        dproj = jnp.concatenate([du_ref[...], dq, dk, dv_ref[...], dfl_ref[...]], axis=1).astype(BF16)
        dproj_ref[...] = dproj
        dhn = _dot_nt(dproj, w_ref[...])
        _, vjp_x = jax.vjp(_rms, x_ref[...], g_ref[...])
        dxn, dg = vjp_x(dhn)
        dx_ref[...] = dxn + dres_ref[...]
        dg_ref[...] += dg
        dqg_ref[...] += dqg
        dkg_ref[...] += dkg

    row = lambda w: pl.BlockSpec((tm, w), lambda i: (i, 0))
    full = lambda a: pl.BlockSpec(a.shape, lambda i: (0,) * a.ndim)
    vec = lambda w: pl.BlockSpec((1, w), lambda i: (0, 0))
    return pl.pallas_call(
        body, name="inproj_bwd", grid=(n // tm,),
        in_specs=[row(D_MODEL), full(g), full(w_in), full(avg), full(qg), full(kg), row(1024), row(512), row(512),
                  row(512), row(512), row(LANES), row(D_MODEL)],
        out_specs=(row(D_MODEL), row(D_IN_PAD), vec(D_MODEL), vec(512), vec(512)),
        out_shape=(jax.ShapeDtypeStruct((n, D_MODEL), F32), jax.ShapeDtypeStruct((n, D_IN_PAD), BF16),
                   jax.ShapeDtypeStruct((1, D_MODEL), F32), jax.ShapeDtypeStruct((1, 512), F32),
                   jax.ShapeDtypeStruct((1, 512), F32)),
        compiler_params=_cparams("arbitrary"),
    )(x, g, w_in, avg, qg, kg, raw, du, dqn, dkn, dv, dfl, dres)


def _glu_fwd(yc, wg, bg, tm=512):
    n = yc.shape[0]

    def body(yc_ref, w_ref, b_ref, ys_ref):
        gl = jax.nn.gelu(yc_ref[...])
        z = _dot(gl.astype(BF16), w_ref[...]) + b_ref[...]
        ys_ref[...] = gl * jax.nn.sigmoid(z)

    row = pl.BlockSpec((tm, 512), lambda i: (i, 0))
    full = lambda a: pl.BlockSpec(a.shape, lambda i: (0,) * a.ndim)
    return pl.pallas_call(
        body, name="glu_fwd", grid=(n // tm,), in_specs=[row, full(wg), full(bg)], out_specs=row,
        out_shape=jax.ShapeDtypeStruct((n, 512), F32), compiler_params=_cparams("parallel"),
    )(yc, wg, bg)


def _glu_bwd(yc, dys, wg, bg, tm=512):
    n = yc.shape[0]

    def body(yc_ref, dys_ref, w_ref, b_ref, dyc_ref, gl_ref, dz_ref, db_ref):
        @pl.when(pl.program_id(0) == 0)
        def _():
            db_ref[...] = jnp.zeros_like(db_ref)
        gl, vjp_gelu = jax.vjp(jax.nn.gelu, yc_ref[...])
        glb = gl.astype(BF16)
        z = _dot(glb, w_ref[...]) + b_ref[...]
        s = jax.nn.sigmoid(z)
        dys = dys_ref[...]
        dz = dys * gl * s * (1.0 - s)
        dzb = dz.astype(BF16)
        dgl = dys * s + _dot_nt(dzb, w_ref[...])
        dyc_ref[...] = vjp_gelu(dgl)[0]
        gl_ref[...] = glb
        dz_ref[...] = dzb
        db_ref[...] += jnp.sum(dz, axis=0, keepdims=True)

    row = pl.BlockSpec((tm, 512), lambda i: (i, 0))
    full = lambda a: pl.BlockSpec(a.shape, lambda i: (0,) * a.ndim)
    return pl.pallas_call(
        body, name="glu_bwd", grid=(n // tm,), in_specs=[row, row, full(wg), full(bg)],
        out_specs=(row, row, row, pl.BlockSpec((1, 512), lambda i: (0, 0))),
        out_shape=(jax.ShapeDtypeStruct((n, 512), F32), jax.ShapeDtypeStruct((n, 512), BF16),
                   jax.ShapeDtypeStruct((n, 512), BF16), jax.ShapeDtypeStruct((1, 512), F32)),
        compiler_params=_cparams("arbitrary"),
    )(yc, dys, wg, bg)


def _mix_fwd(x, ys, ya, gs, ga, wout, gf, tm=512):
    n = x.shape[0]

    def body(x_ref, ys_ref, ya_ref, gs_ref, ga_ref, w_ref, gf_ref, h1_ref, hn2_ref, mixed_ref):
        mixed = jnp.concatenate([_rms(ys_ref[...], gs_ref[...]), _rms(ya_ref[...], ga_ref[...])], axis=1).astype(BF16)
        mixed_ref[...] = mixed
        h1 = x_ref[...] + _dot(mixed, w_ref[...])
        h1_ref[...] = h1
        hn2_ref[...] = _rms(h1, gf_ref[...]).astype(BF16)

    row = lambda w: pl.BlockSpec((tm, w), lambda i: (i, 0))
    full = lambda a: pl.BlockSpec(a.shape, lambda i: (0,) * a.ndim)
    return pl.pallas_call(
        body, name="mix_fwd", grid=(n // tm,),
        in_specs=[row(D_MODEL), row(512), row(512), full(gs), full(ga), full(wout), full(gf)],
        out_specs=(row(D_MODEL), row(D_MODEL), row(D_MODEL)),
        out_shape=(jax.ShapeDtypeStruct((n, D_MODEL), F32), jax.ShapeDtypeStruct((n, D_MODEL), BF16),
                   jax.ShapeDtypeStruct((n, D_MODEL), BF16)),
        compiler_params=_cparams("parallel"),
    )(x, ys, ya, gs, ga, wout, gf)


def _mix_bwd(dy, dhn2_parts, h1, ys, ya, gs, ga, wout, gf, tm=512):
    n = dy.shape[0]
    n_parts = dhn2_parts.shape[0]

    def body(dy_ref, dp_ref, h1_ref, ys_ref, ya_ref, gs_ref, ga_ref, w_ref, gf_ref,
             dh1_ref, dys_ref, dya_ref, dgs_ref, dga_ref, dgf_ref):
        @pl.when(pl.program_id(0) == 0)
        def _():
            dgs_ref[...] = jnp.zeros_like(dgs_ref)
            dga_ref[...] = jnp.zeros_like(dga_ref)
            dgf_ref[...] = jnp.zeros_like(dgf_ref)
        dhn2 = dp_ref[0]
        for p in range(1, n_parts):
            dhn2 = dhn2 + dp_ref[p]
        _, vjp_f = jax.vjp(_rms, h1_ref[...], gf_ref[...])
        dh1n, dgf = vjp_f(dhn2)
        dh1 = dy_ref[...] + dh1n
        dh1_ref[...] = dh1
        dmixed = _dot_nt(dh1.astype(BF16), w_ref[...])
        _, vjp_s = jax.vjp(_rms, ys_ref[...], gs_ref[...])
        dys, dgs = vjp_s(dmixed[:, 0:512])
        _, vjp_a = jax.vjp(_rms, ya_ref[...], ga_ref[...])
        dya, dga = vjp_a(dmixed[:, 512:1024])
        dys_ref[...] = dys
        dya_ref[...] = dya
        dgs_ref[...] += dgs
        dga_ref[...] += dga
        dgf_ref[...] += dgf

    row = lambda w: pl.BlockSpec((tm, w), lambda i: (i, 0))
    full = lambda a: pl.BlockSpec(a.shape, lambda i: (0,) * a.ndim)
    vec = lambda w: pl.BlockSpec((1, w), lambda i: (0, 0))
    return pl.pallas_call(
        body, name="mix_bwd", grid=(n // tm,),
        in_specs=[row(D_MODEL), pl.BlockSpec((n_parts, tm, D_MODEL), lambda i: (0, i, 0)), row(D_MODEL), row(512),
                  row(512), full(gs), full(ga), full(wout), full(gf)],
        out_specs=(row(D_MODEL), row(512), row(512), vec(512), vec(512), vec(D_MODEL)),
        out_shape=(jax.ShapeDtypeStruct((n, D_MODEL), F32), jax.ShapeDtypeStruct((n, 512), F32),
                   jax.ShapeDtypeStruct((n, 512), F32), jax.ShapeDtypeStruct((1, 512), F32),
                   jax.ShapeDtypeStruct((1, 512), F32), jax.ShapeDtypeStruct((1, D_MODEL), F32)),
        compiler_params=_cparams("arbitrary"),
    )(dy, dhn2_parts, h1, ys, ya, gs, ga, wout, gf)


HALO = 16
FFN_GROUPS = 4
FFN_GROUP = D_FF // FFN_GROUPS


def _conv3(ue, cw):
    return cw[2:3] * ue + cw[1:2] * pltpu.roll(ue, 1, 0) + cw[0:1] * pltpu.roll(ue, 2, 0) + cw[3:4]


def _ffn_weight_specs():
    gate = lambda i, j: (j, 0, 0)
    val = lambda i, j: (j + FFN_GROUPS, 0, 0)
    w_blk, c_blk = (1, D_MODEL, FFN_GROUP), (1, SUBLANES, FFN_GROUP)
    return [pl.BlockSpec(w_blk, gate), pl.BlockSpec(w_blk, val), pl.BlockSpec(c_blk, gate), pl.BlockSpec(c_blk, val),
            pl.BlockSpec((1, FFN_GROUP, D_MODEL), gate)]


def _ffn_fwd(hn2, h1, target, w_up, conv, w_down, seq_len, tm=512):
    n = hn2.shape[0]
    nj = FFN_GROUPS
    hb = tm // HALO

    def body(hn_ref, halo_ref, h1_ref, tgt_ref, wg_ref, wv_ref, cg_ref, cv_ref, wd_ref,
             ug_ref, uv_ref, pg_ref, pv_ref, dy_ref, loss_ref, acc):
        i, j = pl.program_id(0), pl.program_id(1)
        seq_start = (i * tm) % seq_len == 0
        halo = halo_ref[...]
        halo = jnp.where(seq_start, jnp.zeros_like(halo), halo)
        he = jnp.concatenate([halo, hn_ref[...]], axis=0)
        ueg = _dot(he, wg_ref[0])
        uev = _dot(he, wv_ref[0])
        ug_ref[0] = ueg[HALO:].astype(BF16)
        uv_ref[0] = uev[HALO:].astype(BF16)
        cg = _conv3(ueg, cg_ref[0])[HALO:]
        cv = _conv3(uev, cv_ref[0])[HALO:]
        pg_ref[0] = cg.astype(BF16)
        pv_ref[0] = cv.astype(BF16)
        act = (jax.nn.silu(cg) * cv).astype(BF16)
        part = _dot(act, wd_ref[0])

        @pl.when(j == 0)
        def _():
            acc[...] = part

        @pl.when(j > 0)
        def _():
            acc[...] += part

        @pl.when(j == nj - 1)
        def _():
            err = h1_ref[...] + acc[...] - tgt_ref[...]
            dy_ref[...] = err * (1.0 / D_MODEL)
            loss_ref[0] = jnp.sum(err * err, axis=0, keepdims=True)

    row = pl.BlockSpec((tm, D_MODEL), lambda i, j: (i, 0))
    u_main = pl.BlockSpec((1, tm, FFN_GROUP), lambda i, j: (j, i, 0))
    u_shape = jax.ShapeDtypeStruct((FFN_GROUPS, n, FFN_GROUP), BF16)
    return pl.pallas_call(
        body, name="ffn_fwd", grid=(n // tm, nj),
        in_specs=[row, pl.BlockSpec((HALO, D_MODEL), lambda i, j: (jnp.maximum(i * hb - 1, 0), 0)), row, row,
                  *_ffn_weight_specs()],
        out_specs=(u_main, u_main, u_main, u_main, row, pl.BlockSpec((1, 1, D_MODEL), lambda i, j: (i, 0, 0))),
        out_shape=(u_shape, u_shape, u_shape, u_shape, jax.ShapeDtypeStruct((n, D_MODEL), F32),
                   jax.ShapeDtypeStruct((n // tm, 1, D_MODEL), F32)),
        scratch_shapes=[pltpu.VMEM((tm, D_MODEL), F32)],
        compiler_params=_cparams("parallel", "arbitrary"),
    )(hn2, hn2, h1, target, w_up, w_up, conv, conv, w_down)


def _ffn_bwd(dy, ug, uv, pg, pv, w_up, conv, w_down, seq_len, tm=512):
    n = dy.shape[0]
    nj = FFN_GROUPS
    fb = FFN_GROUP
    hb = tm // HALO
    last_hb = n // HALO - 1
    rows = tm + HALO

    def body(dy_ref, dyn_ref, ug_ref, uv_ref, pgm_ref, pgn_ref, pvm_ref, pvn_ref, wg_ref, wv_ref, cg_ref, cv_ref,
             wd_ref, dug_ref, duv_ref, act_ref, dhn_ref, dcg_ref, dcv_ref, acc):
        i, j = pl.program_id(0), pl.program_id(1)
        seq_end = ((i + 1) * tm) % seq_len == 0
        dyn = dyn_ref[...]
        dyn = jnp.where(seq_end, jnp.zeros_like(dyn), dyn)
        d_out = jnp.concatenate([dy_ref[...], dyn], axis=0).astype(BF16)
        d_act = _dot_nt(d_out, wd_ref[0])
        cge = jnp.concatenate([pgm_ref[0], pgn_ref[0]], axis=0).astype(F32)
        cve = jnp.concatenate([pvm_ref[0], pvn_ref[0]], axis=0).astype(F32)
        act, vjp_act = jax.vjp(lambda g, v: jax.nn.silu(g) * v, cge, cve)
        dcge, dcve = vjp_act(d_act)
        act_ref[0] = act[:tm].astype(BF16)

        def conv_t(dc, u_ref, cw):
            ahead1 = pltpu.roll(dc, rows - 1, 0)[:tm]
            ahead2 = pltpu.roll(dc, rows - 2, 0)[:tm]
            here = dc[:tm]
            du = cw[2:3] * here + cw[1:2] * ahead1 + cw[0:1] * ahead2
            u = u_ref[0].astype(F32)
            col = lambda x: jnp.sum(x, axis=0, keepdims=True)
            grad = jnp.concatenate([col(ahead2 * u), col(ahead1 * u), col(here * u), col(here),
                                    jnp.zeros((4, fb), F32)], axis=0)
            return du.astype(BF16), grad

        cwg, cwv = cg_ref[0], cv_ref[0]
        dug, grad_g = conv_t(dcge, ug_ref, cwg)
        duv, grad_v = conv_t(dcve, uv_ref, cwv)
        dug_ref[0] = dug
        duv_ref[0] = duv
        part = _dot_nt(dug, wg_ref[0]) + _dot_nt(duv, wv_ref[0])

        @pl.when(j == 0)
        def _():
            acc[...] = part

        @pl.when(j > 0)
        def _():
            acc[...] += part

        @pl.when(j == nj - 1)
        def _():
            dhn_ref[...] = acc[...]

        @pl.when(i == 0)
        def _():
            dcg_ref[j] = jnp.zeros((8, fb), F32)
            dcv_ref[j] = jnp.zeros((8, fb), F32)

        dcg_ref[j] += grad_g
        dcv_ref[j] += grad_v

    row = pl.BlockSpec((tm, D_MODEL), lambda i, j: (i, 0))
    u_main = pl.BlockSpec((1, tm, fb), lambda i, j: (j, i, 0))
    u_next = pl.BlockSpec((1, HALO, fb), lambda i, j: (j, jnp.minimum((i + 1) * hb, last_hb), 0))
    dc_spec = pl.BlockSpec((nj, 8, fb), lambda i, j: (0, 0, 0))
    u_shape = jax.ShapeDtypeStruct((FFN_GROUPS, n, fb), BF16)
    return pl.pallas_call(
        body, name="ffn_bwd", grid=(n // tm, nj),
        in_specs=[row, pl.BlockSpec((HALO, D_MODEL), lambda i, j: (jnp.minimum((i + 1) * hb, last_hb), 0)),
                  u_main, u_main, u_main, u_next, u_main, u_next, *_ffn_weight_specs()],
        out_specs=(u_main, u_main, u_main, row, dc_spec, dc_spec),
        out_shape=(u_shape, u_shape, u_shape, jax.ShapeDtypeStruct((n, D_MODEL), F32),
                   jax.ShapeDtypeStruct((nj, 8, fb), F32), jax.ShapeDtypeStruct((nj, 8, fb), F32)),
        scratch_shapes=[pltpu.VMEM((tm, D_MODEL), F32)],
        compiler_params=_cparams("arbitrary", "arbitrary"),
    )(dy, dy, ug, uv, pg, pg, pv, pv, w_up, w_up, conv, conv, w_down)


def _tn_grouped(a, b, name, shared_a, out_dtype=F32, tn=1024):
    groups = b.shape[0] if shared_a else a.shape[0]
    n_tok = a.shape[0] if shared_a else b.shape[0]
    k_dim, m_dim = a.shape[-1], b.shape[-1]

    def body(a_ref, b_ref, o_ref, acc):
        k = pl.program_id(1)
        a_t = a_ref[...] if shared_a else a_ref[0]
        b_t = b_ref[0] if shared_a else b_ref[...]
        part = _dot_tn(a_t.astype(BF16), b_t.astype(BF16))

        @pl.when(k == 0)
        def _():
            acc[...] = part

        @pl.when(k > 0)
        def _():
            acc[...] += part

        @pl.when(k == n_tok // tn - 1)
        def _():
            o_ref[0] = acc[...].astype(out_dtype)

    plain = lambda w: pl.BlockSpec((tn, w), lambda g, k: (k, 0))
    grouped = lambda w: pl.BlockSpec((1, tn, w), lambda g, k: (g, k, 0))
    return pl.pallas_call(
        body, name=name, grid=(groups, n_tok // tn),
        in_specs=[plain(k_dim), grouped(m_dim)] if shared_a else [grouped(k_dim), plain(m_dim)],
        out_specs=pl.BlockSpec((1, k_dim, m_dim), lambda g, k: (g, 0, 0)),
        out_shape=jax.ShapeDtypeStruct((groups, k_dim, m_dim), out_dtype),
        scratch_shapes=[pltpu.VMEM((k_dim, m_dim), F32)],
        compiler_params=_cparams("parallel", "arbitrary"),
    )(a, b)


def _s5_param_fn(lr, li, ldt, br, bi):
    dt = jnp.exp(ldt)
    mag = jnp.exp(lr * dt)
    ab_re = mag * jnp.cos(li * dt)
    ab_im = mag * jnp.sin(li * dt)
    nr = ab_re - 1.0
    ni = ab_im
    den = lr * lr + li * li
    q_re = (nr * lr + ni * li) / den
    q_im = (ni * lr - nr * li) / den
    bb_re = q_re * br - q_im * bi
    bb_im = q_re * bi + q_im * br
    return ab_re, ab_im, bb_re, bb_im


def _s5_param_fwd(lr, li, ldt, br, bi):
    def body(lr_ref, li_ref, ldt_ref, br_ref, bi_ref, ar_ref, ai_ref, bbr_ref, bbi_ref):
        ar, ai, bbr, bbi = _s5_param_fn(lr_ref[...], li_ref[...], ldt_ref[...], br_ref[...], bi_ref[...])
        ar_ref[...] = ar
        ai_ref[...] = ai
        bbr_ref[...] = bbr
        bbi_ref[...] = bbi

    return pl.pallas_call(
        body, name="s5_param_fwd",
        out_shape=(jax.ShapeDtypeStruct(lr.shape, F32), jax.ShapeDtypeStruct(lr.shape, F32),
                   jax.ShapeDtypeStruct(br.shape, F32), jax.ShapeDtypeStruct(br.shape, F32)),
    )(lr, li, ldt, br, bi)


def _s5_param_bwd(lr, li, ldt, br, bi, dar, dai, dbbr, dbbi):
    def body(lr_ref, li_ref, ldt_ref, br_ref, bi_ref, dar_ref, dai_ref, dbbr_ref, dbbi_ref,
             dlr_ref, dli_ref, dldt_ref, dbr_ref, dbi_ref):
        _, vjp = jax.vjp(_s5_param_fn, lr_ref[...], li_ref[...], ldt_ref[...], br_ref[...], bi_ref[...])
        dlr, dli, dldt, dbr, dbi = vjp((dar_ref[...], dai_ref[...], dbbr_ref[...], dbbi_ref[...]))
        dlr_ref[...] = dlr
        dli_ref[...] = dli
        dldt_ref[...] = dldt
        dbr_ref[...] = dbr
        dbi_ref[...] = dbi

    return pl.pallas_call(
        body, name="s5_param_bwd",
        out_shape=(jax.ShapeDtypeStruct(lr.shape, F32), jax.ShapeDtypeStruct(lr.shape, F32),
                   jax.ShapeDtypeStruct(ldt.shape, F32), jax.ShapeDtypeStruct(br.shape, F32),
                   jax.ShapeDtypeStruct(br.shape, F32)),
    )(lr, li, ldt, br, bi, dar, dai, dbbr, dbbi)


S5_CHUNK = 256
S5_STATES = 512
S5_BLOCKS = 4


def _cpow_rows(ar, ai, count):
    rs, im = [ar], [ai]
    for _ in range(count - 1):
        pr, pi = rs[-1], im[-1]
        rs.append(pr * ar - pi * ai)
        im.append(pr * ai + pi * ar)
    return rs, im


def _scan_in_groups(vr, vi, pr, pi, rm, reverse):
    n, width = vr.shape
    vr = vr.reshape(n // SUBLANES, SUBLANES, width)
    vi = vi.reshape(n // SUBLANES, SUBLANES, width)
    row = rm[0:SUBLANES]
    for k in (1, 2, 4):
        shift = SUBLANES - k if reverse else k
        keep = row < SUBLANES - k if reverse else row >= k
        kr = jnp.where(keep, pr[k - 1], 0.0)
        ki = jnp.where(keep, pi[k - 1], 0.0)
        sr, si = pltpu.roll(vr, shift, 1), pltpu.roll(vi, shift, 1)
        vr, vi = vr + kr * sr - ki * si, vi + kr * si + ki * sr
    return vr.reshape(n, width), vi.reshape(n, width)


def _carry_over_groups(xr_s, xi_s, wr, wi, c0r, c0i, reverse):
    groups = xr_s.shape[0] // SUBLANES
    pick = 0 if reverse else SUBLANES - 1

    def step(q, carry):
        cr, ci = carry
        r = groups - 1 - q if reverse else q
        o = pl.multiple_of(r * SUBLANES, SUBLANES)
        vr = xr_s[pl.ds(o, SUBLANES), :]
        vi = xi_s[pl.ds(o, SUBLANES), :]
        nr = vr + wr * cr - wi * ci
        ni = vi + wr * ci + wi * cr
        xr_s[pl.ds(o, SUBLANES), :] = nr
        xi_s[pl.ds(o, SUBLANES), :] = ni
        return (jnp.broadcast_to(nr[pick:pick + 1], nr.shape), jnp.broadcast_to(ni[pick:pick + 1], ni.shape))

    return lax.fori_loop(0, groups, step, (c0r, c0i))


def _s5_state_scan(u_b, bbr, bbi, pr, pi, rm, xr_s, xi_s, c0r, c0i):
    bur = _dot(u_b, bbr)
    bui = _dot(u_b, bbi)
    bur, bui = _scan_in_groups(bur, bui, pr, pi, rm, False)
    xr_s[...] = bur
    xi_s[...] = bui
    w8r = jnp.concatenate(pr, axis=0)
    w8i = jnp.concatenate(pi, axis=0)
    return _carry_over_groups(xr_s, xi_s, w8r, w8i, c0r, c0i, False)


def _s5_fwd(u, a_re, a_im, bbr, bbi, cr, ci, d_skip, n_seq):
    n = u.shape[0]
    seq_len = n // n_seq
    nt = seq_len // S5_CHUNK
    tc = S5_CHUNK

    def body(u_ref, ar_ref, ai_ref, bbr_ref, bbi_ref, cr_ref, ci_ref, d_ref, y_ref, str_ref, sti_ref,
             xr_s, xi_s, car_r, car_i):
        t = pl.program_id(2)

        @pl.when(t == 0)
        def _():
            car_r[...] = jnp.zeros_like(car_r)
            car_i[...] = jnp.zeros_like(car_i)
        pr, pi = _cpow_rows(ar_ref[0], ai_ref[0], SUBLANES)
        rm = lax.broadcasted_iota(jnp.int32, (tc, S5_STATES), 0) & (SUBLANES - 1)
        str_ref[0, 0] = car_r[...]
        sti_ref[0, 0] = car_i[...]
        u_t = u_ref[...]
        cfr, cfi = _s5_state_scan(u_t.astype(BF16), bbr_ref[0], bbi_ref[0], pr, pi, rm, xr_s, xi_s,
                                  car_r[...], car_i[...])
        car_r[...] = cfr
        car_i[...] = cfi
        y = _dot(xr_s[...].astype(BF16), cr_ref[0]) - _dot(xi_s[...].astype(BF16), ci_ref[0])
        y_ref[...] = y + d_ref[...] * u_t

    u_spec = pl.BlockSpec((tc, LANES), lambda cb, b, t: (b * nt + t, cb))
    a_spec = pl.BlockSpec((1, 1, S5_STATES), lambda cb, b, t: (cb, 0, 0))
    bb_spec = pl.BlockSpec((1, LANES, S5_STATES), lambda cb, b, t: (cb, 0, 0))
    c_spec = pl.BlockSpec((1, S5_STATES, LANES), lambda cb, b, t: (cb, 0, 0))
    st_spec = pl.BlockSpec((1, 1, SUBLANES, S5_STATES), lambda cb, b, t: (cb, b * nt + t, 0, 0))
    st_shape = jax.ShapeDtypeStruct((S5_BLOCKS, n_seq * nt, SUBLANES, S5_STATES), F32)
    return pl.pallas_call(
        body, name="s5_fwd", grid=(S5_BLOCKS, n_seq, nt),
        in_specs=[u_spec, a_spec, a_spec, bb_spec, bb_spec, c_spec, c_spec,
                  pl.BlockSpec((1, LANES), lambda cb, b, t: (0, cb))],
        out_specs=(u_spec, st_spec, st_spec),
        out_shape=(jax.ShapeDtypeStruct((n, D_SSM), F32), st_shape, st_shape),
        scratch_shapes=[pltpu.VMEM((tc, S5_STATES), F32), pltpu.VMEM((tc, S5_STATES), F32),
                        pltpu.VMEM((SUBLANES, S5_STATES), F32), pltpu.VMEM((SUBLANES, S5_STATES), F32)],
        compiler_params=_cparams("parallel", "arbitrary", "arbitrary"),
    )(u, a_re, a_im, bbr, bbi, cr, ci, d_skip)


def _s5_bwd(u, dy, st_r, st_i, a_re, a_im, bbr, bbi, cr, ci, d_skip, n_seq):
    n = u.shape[0]
    seq_len = n // n_seq
    nt = seq_len // S5_CHUNK
    tc = S5_CHUNK

    def body(u_ref, dy_ref, str_ref, sti_ref, ar_ref, ai_ref, bbr_ref, bbi_ref, cr_ref, ci_ref, d_ref,
             du_ref, dbbr_ref, dbbi_ref, dcr_ref, dci_ref, dar_ref, dai_ref, dd_ref,
             xr_s, xi_s, gr_s, gi_s, car_r, car_i):
        b, t = pl.program_id(1), pl.program_id(2)

        @pl.when((b == 0) & (t == 0))
        def _():
            for ref in (dbbr_ref, dbbi_ref, dcr_ref, dci_ref, dar_ref, dai_ref, dd_ref):
                ref[...] = jnp.zeros_like(ref)

        @pl.when(t == 0)
        def _():
            car_r[...] = jnp.zeros_like(car_r)
            car_i[...] = jnp.zeros_like(car_i)
        ar, ai = ar_ref[0], ai_ref[0]
        pr, pi = _cpow_rows(ar, ai, SUBLANES)
        row = lax.broadcasted_iota(jnp.int32, (tc, S5_STATES), 0)
        rm = row & (SUBLANES - 1)
        u_t = u_ref[...]
        u_b = u_t.astype(BF16)
        dy_t = dy_ref[...]
        dy_b = dy_t.astype(BF16)
        s0r, s0i = str_ref[0, 0], sti_ref[0, 0]
        _s5_state_scan(u_b, bbr_ref[0], bbi_ref[0], pr, pi, rm, xr_s, xi_s, s0r, s0i)
        xr, xi = xr_s[...], xi_s[...]
        gr = _dot_nt(dy_b, cr_ref[0])
        gi = -_dot_nt(dy_b, ci_ref[0])
        npi = [-v for v in pi]
        gr, gi = _scan_in_groups(gr, gi, pr, npi, rm, True)
        gr_s[...] = gr
        gi_s[...] = gi
        w8r = jnp.concatenate(pr[::-1], axis=0)
        w8i = jnp.concatenate(npi[::-1], axis=0)
        cfr, cfi = _carry_over_groups(gr_s, gi_s, w8r, w8i, car_r[...], car_i[...], True)
        car_r[...] = cfr
        car_i[...] = cfi
        gr, gi = gr_s[...], gi_s[...]
        gr_b, gi_b = gr.astype(BF16), gi.astype(BF16)
        du_ref[...] = _dot_nt(gr_b, bbr_ref[0]) + _dot_nt(gi_b, bbi_ref[0]) + d_ref[...] * dy_t
        dbbr_ref[0] += _dot_tn(u_b, gr_b)
        dbbi_ref[0] += _dot_tn(u_b, gi_b)
        dcr_ref[0] += _dot_tn(xr.astype(BF16), dy_b)
        dci_ref[0] -= _dot_tn(xi.astype(BF16), dy_b)
        dd_ref[0] += jnp.sum((dy_t * u_t).reshape(tc // SUBLANES, SUBLANES, LANES), axis=0)
        first = row == 0
        xpr = jnp.where(first, jnp.broadcast_to(s0r[0:1], xr.shape), pltpu.roll(xr, 1, 0))
        xpi = jnp.where(first, jnp.broadcast_to(s0i[0:1], xi.shape), pltpu.roll(xi, 1, 0))
        shp = (tc // SUBLANES, SUBLANES, S5_STATES)
        dar_ref[0] += jnp.sum((gr * xpr + gi * xpi).reshape(shp), axis=0)
        dai_ref[0] += jnp.sum((gi * xpr - gr * xpi).reshape(shp), axis=0)

    u_spec = pl.BlockSpec((tc, LANES), lambda cb, b, t: (b * nt + nt - 1 - t, cb))
    a_spec = pl.BlockSpec((1, 1, S5_STATES), lambda cb, b, t: (cb, 0, 0))
    bb_spec = pl.BlockSpec((1, LANES, S5_STATES), lambda cb, b, t: (cb, 0, 0))
    c_spec = pl.BlockSpec((1, S5_STATES, LANES), lambda cb, b, t: (cb, 0, 0))
    st_spec = pl.BlockSpec((1, 1, SUBLANES, S5_STATES), lambda cb, b, t: (cb, b * nt + nt - 1 - t, 0, 0))
    da_spec = pl.BlockSpec((1, SUBLANES, S5_STATES), lambda cb, b, t: (cb, 0, 0))
    dd_spec = pl.BlockSpec((1, SUBLANES, LANES), lambda cb, b, t: (cb, 0, 0))
    big = pltpu.VMEM((tc, S5_STATES), F32)
    small = pltpu.VMEM((SUBLANES, S5_STATES), F32)
    return pl.pallas_call(
        body, name="s5_bwd", grid=(S5_BLOCKS, n_seq, nt),
        in_specs=[u_spec, u_spec, st_spec, st_spec, a_spec, a_spec, bb_spec, bb_spec, c_spec, c_spec,
                  pl.BlockSpec((1, LANES), lambda cb, b, t: (0, cb))],
        out_specs=(u_spec, bb_spec, bb_spec, c_spec, c_spec, da_spec, da_spec, dd_spec),
        out_shape=(jax.ShapeDtypeStruct((n, D_SSM), F32),
                   jax.ShapeDtypeStruct((S5_BLOCKS, LANES, S5_STATES), F32),
                   jax.ShapeDtypeStruct((S5_BLOCKS, LANES, S5_STATES), F32),
                   jax.ShapeDtypeStruct((S5_BLOCKS, S5_STATES, LANES), F32),
                   jax.ShapeDtypeStruct((S5_BLOCKS, S5_STATES, LANES), F32),
                   jax.ShapeDtypeStruct((S5_BLOCKS, SUBLANES, S5_STATES), F32),
                   jax.ShapeDtypeStruct((S5_BLOCKS, SUBLANES, S5_STATES), F32),
                   jax.ShapeDtypeStruct((S5_BLOCKS, SUBLANES, LANES), F32)),
        scratch_shapes=[big, big, big, big, small, small],
        compiler_params=_cparams("parallel", "arbitrary", "arbitrary"),
    )(u, dy, st_r, st_i, a_re, a_im, bbr, bbi, cr, ci, d_skip)


CUM_BLOCK = 128


def _tri(lower):
    r = lax.broadcasted_iota(jnp.int32, (CUM_BLOCK, CUM_BLOCK), 0)
    c = lax.broadcasted_iota(jnp.int32, (CUM_BLOCK, CUM_BLOCK), 1)
    return jnp.where(r >= c if lower else r <= c, 1.0, 0.0).astype(F32)


def _fprep_fwd(fl, bf, n_seq):
    n = fl.shape[0]
    seq_len = n // n_seq
    nb = seq_len // CUM_BLOCK

    def body(fl_ref, bf_ref, cum_ref):
        tril = _tri(True)
        carry = jnp.zeros((1, LANES), F32)
        for blk in range(nb):
            rows = slice(blk * CUM_BLOCK, (blk + 1) * CUM_BLOCK)
            lf = jax.nn.log_sigmoid(fl_ref[rows, :] + bf_ref[...])
            cs = jnp.dot(tril, lf, preferred_element_type=F32, precision=HIGHEST) + carry
            cum_ref[rows, :] = cs
            carry = cs[CUM_BLOCK - 1:CUM_BLOCK, :]

    spec = pl.BlockSpec((seq_len, LANES), lambda b: (b, 0))
    return pl.pallas_call(
        body, name="fprep_fwd", grid=(n_seq,), in_specs=[spec, pl.BlockSpec((1, LANES), lambda b: (0, 0))],
        out_specs=spec, out_shape=jax.ShapeDtypeStruct((n, LANES), F32), compiler_params=_cparams("parallel"),
    )(fl, bf)


def _fprep_bwd(dcum, fl, bf, n_seq):
    n = fl.shape[0]
    seq_len = n // n_seq
    nb = seq_len // CUM_BLOCK

    def body(dcum_ref, fl_ref, bf_ref, dfl_ref, dbf_ref):
        triu = _tri(False)
        lane = lax.broadcasted_iota(jnp.int32, (CUM_BLOCK, LANES), 1)
        carry = jnp.zeros((1, LANES), F32)
        total = jnp.zeros((1, LANES), F32)
        for blk in reversed(range(nb)):
            rows = slice(blk * CUM_BLOCK, (blk + 1) * CUM_BLOCK)
            rs = jnp.dot(triu, dcum_ref[rows, :], preferred_element_type=F32, precision=HIGHEST) + carry
            carry = rs[0:1, :]
            _, vjp = jax.vjp(jax.nn.log_sigmoid, fl_ref[rows, :] + bf_ref[...])
            dz = jnp.where(lane < N_HEADS, vjp(rs)[0], 0.0)
            dfl_ref[rows, :] = dz
            total = total + jnp.sum(dz, axis=0, keepdims=True)
        dbf_ref[0] = total

    spec = pl.BlockSpec((seq_len, LANES), lambda b: (b, 0))
    return pl.pallas_call(
        body, name="fprep_bwd", grid=(n_seq,), in_specs=[spec, spec, pl.BlockSpec((1, LANES), lambda b: (0, 0))],
        out_specs=(spec, pl.BlockSpec((1, 1, LANES), lambda b: (b, 0, 0))),
        out_shape=(jax.ShapeDtypeStruct((n, LANES), F32), jax.ShapeDtypeStruct((n_seq, 1, LANES), F32)),
        compiler_params=_cparams("parallel"),
    )(dcum, fl, bf)


ATT_TQ = 256
ATT_KSTEP = 256
ATT_SCALE = HEAD_DIM ** -0.5
NEG_BIG = -1e30


assert ATT_KSTEP == ATT_TQ


def _scores(q_scaled, kb, cq, ck, kend):
    s = _dot_nt(q_scaled, kb) + cq - ck
    r = lax.broadcasted_iota(jnp.int32, (ATT_TQ, ATT_TQ), 0)
    c = lax.broadcasted_iota(jnp.int32, (ATT_TQ, ATT_TQ), 1)
    diag = jnp.where(r >= c, s[:, kend - ATT_TQ:], NEG_BIG)
    return diag if kend == ATT_TQ else jnp.concatenate([s[:, :kend - ATT_TQ], diag], axis=1)


def _attn_specs(n_seq, seq_len):
    nq = seq_len // ATT_TQ
    q_spec = pl.BlockSpec((ATT_TQ, LANES), lambda b, h, q: (b * nq + q, h))
    k_spec = pl.BlockSpec((seq_len, LANES), lambda b, h, q: (b, N_HEADS // 2 + h))
    v_spec = pl.BlockSpec((seq_len, LANES), lambda b, h, q: (b, N_HEADS + h))
    cq_spec = pl.BlockSpec((1, 2, ATT_TQ, 1), lambda b, h, q: (b, h, q, 0))
    ck_spec = pl.BlockSpec((1, 2, 1, seq_len), lambda b, h, q: (b, h, 0, 0))
    return nq, q_spec, k_spec, v_spec, cq_spec, ck_spec


def _head_selectors():
    head0 = lax.broadcasted_iota(jnp.int32, (1, LANES), 1) < HEAD_DIM
    return head0, (head0, jnp.logical_not(head0))


def _for_key_range(qi, seq_len, run):
    per = ATT_KSTEP // ATT_TQ
    for g in range(seq_len // ATT_KSTEP):
        pl.when(qi // per == g)(functools.partial(run, (g + 1) * ATT_KSTEP))


def _attn_fwd(qkv, cq, ck, n_seq):
    n = qkv.shape[0]
    seq_len = n // n_seq
    nq, q_spec, k_spec, v_spec, cq_spec, ck_spec = _attn_specs(n_seq, seq_len)

    def body(q_ref, k_ref, v_ref, cq_ref, ck_ref, o_ref, lse_ref):
        qi = pl.program_id(2)
        q2 = q_ref[...]
        head0, sels = _head_selectors()
        qe = [jnp.where(sel, q2 * ATT_SCALE, 0.0).astype(BF16) for sel in sels]

        def run(kend):
            kb = k_ref[0:kend, :].astype(BF16)
            vb = v_ref[0:kend, :].astype(BF16)
            outs = []
            for e in range(2):
                s = _scores(qe[e], kb, cq_ref[0, e], ck_ref[0, e, :, 0:kend], kend)
                mx = jnp.max(s, axis=1, keepdims=True)
                p = jnp.exp(s - mx)
                den = jnp.sum(p, axis=1, keepdims=True)
                outs.append(_dot(p.astype(BF16), vb) / den)
                lse_ref[0, e] = mx + jnp.log(den)
            o_ref[...] = jnp.where(head0, outs[0], outs[1])

        _for_key_range(qi, seq_len, run)

    return pl.pallas_call(
        body, name="attn_fwd", grid=(n_seq, N_HEADS // 2, nq),
        in_specs=[q_spec, k_spec, v_spec, cq_spec, ck_spec],
        out_specs=(q_spec, cq_spec),
        out_shape=(jax.ShapeDtypeStruct((n, D_ATTN), F32), jax.ShapeDtypeStruct((n_seq, N_HEADS, seq_len, 1), F32)),
        compiler_params=_cparams("parallel", "parallel", "parallel"),
    )(qkv, qkv, qkv, cq, ck)


def _attn_bwd(qkv, cq, ck, o, do, lse, n_seq):
    n = qkv.shape[0]
    seq_len = n // n_seq
    nq, q_spec, k_spec, v_spec, cq_spec, ck_spec = _attn_specs(n_seq, seq_len)
    kv_out = pl.BlockSpec((seq_len, LANES), lambda b, h, q: (b, h))

    def body(q_ref, k_ref, v_ref, cq_ref, ck_ref, o_ref, do_ref, lse_ref, dq_ref, dk_ref, dv_ref, dcq_ref, dck_ref):
        qi = pl.program_id(2)

        @pl.when(qi == 0)
        def _():
            dk_ref[...] = jnp.zeros_like(dk_ref)
            dv_ref[...] = jnp.zeros_like(dv_ref)
            dck_ref[...] = jnp.zeros_like(dck_ref)
        q2 = q_ref[...]
        do2 = do_ref[...]
        o2 = o_ref[...]
        head0, sels = _head_selectors()
        qe = [jnp.where(sel, q2 * ATT_SCALE, 0.0).astype(BF16) for sel in sels]
        doe = [jnp.where(sel, do2, 0.0) for sel in sels]
        doe_b = [d.astype(BF16) for d in doe]
        delta = [jnp.sum(d * o2, axis=1, keepdims=True) for d in doe]

        def run(kend):
            kb = k_ref[0:kend, :].astype(BF16)
            vb = v_ref[0:kend, :].astype(BF16)
            dqs = []
            dk = jnp.zeros((kend, LANES), F32)
            dv = jnp.zeros((kend, LANES), F32)
            for e in range(2):
                s = _scores(qe[e], kb, cq_ref[0, e], ck_ref[0, e, :, 0:kend], kend)
                p = jnp.exp(s - lse_ref[0, e])
                ds = p * (_dot_nt(doe_b[e], vb) - delta[e])
                ds_b = ds.astype(BF16)
                dqs.append(_dot(ds_b, kb))
                dk = dk + _dot_tn(ds_b, qe[e])
                dv = dv + _dot_tn(p.astype(BF16), doe_b[e])
                dcq_ref[0, e] = jnp.sum(ds, axis=1, keepdims=True)
                dck_ref[0, e, :, 0:kend] -= jnp.sum(ds, axis=0, keepdims=True)
            dk_ref[0:kend, :] += dk
            dv_ref[0:kend, :] += dv
            dq_ref[...] = jnp.where(head0, dqs[0], dqs[1]) * ATT_SCALE

        _for_key_range(qi, seq_len, run)

    return pl.pallas_call(
        body, name="attn_bwd", grid=(n_seq, N_HEADS // 2, nq),
        in_specs=[q_spec, k_spec, v_spec, cq_spec, ck_spec, q_spec, q_spec, cq_spec],
        out_specs=(q_spec, kv_out, kv_out, cq_spec, ck_spec),
        out_shape=(jax.ShapeDtypeStruct((n, D_ATTN), F32), jax.ShapeDtypeStruct((n, D_ATTN), F32),
                   jax.ShapeDtypeStruct((n, D_ATTN), F32),
                   jax.ShapeDtypeStruct((n_seq, N_HEADS, seq_len, 1), F32),
                   jax.ShapeDtypeStruct((n_seq, N_HEADS, 1, seq_len), F32)),
        compiler_params=_cparams("parallel", "parallel", "arbitrary"),
    )(qkv, qkv, qkv, cq, ck, o, do, lse)


WEIGHT_NAMES = ("norm_mix", "w_in", "b_forget", "lam_re", "lam_im", "b_re", "b_im", "c_re", "c_im", "d_skip", "log_dt",
                "w_glu", "b_glu", "q_norm", "k_norm", "norm_out_ssm", "norm_out_attn", "w_out", "norm_ffn", "w_up",
                "conv_w", "conv_b", "w_down")
SHARDED = ("w_in", "w_glu", "w_out", "w_up", "conv_w", "w_down")
ADAM_ROWS = {"w_in": 256, "w_glu": 64, "w_out": 128, "w_up": 128, "conv_w": 3, "w_down": 344}
PACK_ROWS = SUBLANES * LANES
ROWS_DOWN = D_FF // N_DEV
ROWS_OUT = D_MODEL // N_DEV
ROWS_GLU = D_SSM * D_SSM // N_DEV // D_MODEL


def _pad_to(a, axis, size):
    pad = [(0, 0)] * a.ndim
    pad[axis] = (0, size - a.shape[axis])
    return jnp.pad(a, pad)


def _block_diag(t, transpose):
    t4 = t.reshape(S5_BLOCKS, 8, SSM_GROUP, SSM_STATE)
    eye = jnp.eye(8, dtype=t.dtype)
    if transpose:
        e = jnp.swapaxes(t4, 2, 3)[:, :, :, None, :] * eye[None, :, None, :, None]
        return e.reshape(S5_BLOCKS, S5_STATES, LANES)
    e = t4[:, :, :, None, :] * eye[None, :, None, :, None]
    return e.reshape(S5_BLOCKS, LANES, S5_STATES)


def _block_diag_extract(m, transpose):
    if transpose:
        m5 = m.reshape(S5_BLOCKS, 8, SSM_STATE, 8, SSM_GROUP)
        d = jnp.stack([m5[:, i, :, i, :] for i in range(8)], axis=1)
        return jnp.swapaxes(d, 2, 3).reshape(N_GROUPS, SSM_GROUP, SSM_STATE)
    m5 = m.reshape(S5_BLOCKS, 8, SSM_GROUP, 8, SSM_STATE)
    d = jnp.stack([m5[:, i, :, i, :] for i in range(8)], axis=1)
    return d.reshape(N_GROUPS, SSM_GROUP, SSM_STATE)


def _pack(pieces):
    flat = jnp.concatenate([p.reshape(-1).astype(F32) for p in pieces])
    size = -(-flat.shape[0] // PACK_ROWS) * PACK_ROWS
    return _pad_to(flat, 0, size).reshape(-1, LANES)


def _unpack(packed, shapes):
    flat = packed.reshape(-1)
    out, off = [], 0
    for shp in shapes:
        size = math.prod(shp)
        out.append(flat[off:off + size].reshape(shp))
        off += size
    return out


def kernel(x, norm_mix, w_in, b_forget, lam_re, lam_im, b_re, b_im, c_re, c_im, d_skip, log_dt, w_glu, b_glu, q_norm, k_norm, norm_out_ssm, norm_out_attn, w_out, norm_ffn, w_up, conv_w, conv_b, w_down, loss_target, m_norm_mix, m_w_in, m_b_forget, m_lam_re, m_lam_im, m_b_re, m_b_im, m_c_re, m_c_im, m_d_skip, m_log_dt, m_w_glu, m_b_glu, m_q_norm, m_k_norm, m_norm_out_ssm, m_norm_out_attn, m_w_out, m_norm_ffn, m_w_up, m_conv_w, m_conv_b, m_w_down, v_norm_mix, v_w_in, v_b_forget, v_lam_re, v_lam_im, v_b_re, v_b_im, v_c_re, v_c_im, v_d_skip, v_log_dt, v_w_glu, v_b_glu, v_q_norm, v_k_norm, v_norm_out_ssm, v_norm_out_attn, v_w_out, v_norm_ffn, v_w_up, v_conv_w, v_conv_b, v_w_down):
    given = dict(locals())
    weights = {k: given[k] for k in WEIGHT_NAMES}
    mom1 = {k: given["m_" + k] for k in WEIGHT_NAMES}
    mom2 = {k: given["v_" + k] for k in WEIGHT_NAMES}
    n_seq, seq_len, _ = x.shape
    n = n_seq * seq_len
    xf = x.reshape(n, D_MODEL)
    target = loss_target.reshape(n, D_MODEL)
    me_idx = 4 * lax.axis_index("x") + 2 * lax.axis_index("y") + lax.axis_index("c")

    g_in, g_cw = _exchange([w_in[0].astype(BF16), conv_w[0]], [False, False], "gather_w_in")
    rest_flags = [False] * 2
    rows_w = jnp.concatenate([w_down[0], w_out[0], w_glu[0].reshape(ROWS_GLU, D_MODEL)], axis=0).astype(BF16)
    w_sems = _exchange_start([rows_w, w_up[0].astype(BF16)], rest_flags, g_in, "gather_rest_start", 0)
    norm_mix = norm_mix + w_sems[4][0, 0]
    w_in_p = _pad_to(jnp.swapaxes(g_in, 0, 1).reshape(D_MODEL, D_IN), 1, D_IN_PAD)

    lr3 = lam_re[0].reshape(N_GROUPS, 1, SSM_STATE)
    li3 = lam_im[0].reshape(N_GROUPS, 1, SSM_STATE)
    ldt3 = log_dt[0].reshape(N_GROUPS, 1, 1)
    br_t = jnp.swapaxes(b_re[0], 1, 2)
    bi_t = jnp.swapaxes(b_im[0], 1, 2)
    ab_re, ab_im, bb_re, bb_im = _s5_param_fwd(lr3, li3, ldt3, br_t, bi_t)
    a_re = ab_re.reshape(S5_BLOCKS, 1, S5_STATES)
    a_im = ab_im.reshape(S5_BLOCKS, 1, S5_STATES)
    bbr = _block_diag(bb_re, False).astype(BF16)
    bbi = _block_diag(bb_im, False).astype(BF16)
    cr = _block_diag(c_re[0], True).astype(BF16)
    ci = _block_diag(c_im[0], True).astype(BF16)

    avg = jnp.kron(jnp.eye(N_HEADS, dtype=F32), jnp.full((HEAD_DIM, HEAD_DIM), 1.0 / HEAD_DIM, F32)).astype(BF16)
    qg = jnp.tile(q_norm, (1, N_HEADS))
    kg = jnp.tile(k_norm, (1, N_HEADS))
    hn, u, qkv, raw, fl = _inproj_fwd(xf, norm_mix, w_in_p, avg, qg, kg)
    yc, st_r, st_i = _s5_fwd(u, a_re, a_im, bbr, bbi, cr, ci, d_skip, n_seq)
    bf = _pad_to(b_forget, 1, LANES)
    cum = _fprep_fwd(fl, bf, n_seq)
    cum8 = jnp.swapaxes(cum[:, :N_HEADS].reshape(n_seq, seq_len, N_HEADS), 1, 2)
    cq = cum8[:, :, :, None]
    ck = cum8[:, :, None, :]
    ya, lse = _attn_fwd(qkv, cq, ck, n_seq)
    (own_rows, own_up), (g_rows, g_up) = _exchange_wait(w_sems[0], w_sems[1], w_sems[2], w_sems[3], rest_flags, ya,
                                                        "gather_rest_wait")
    my_slot = lax.broadcasted_iota(jnp.int32, (N_DEV, 1, 1), 0) == me_idx
    g_rows = jnp.where(my_slot, own_rows[None], g_rows)
    g_up = jnp.where(my_slot, own_up[None], g_up)
    g_down = g_rows[:, :ROWS_DOWN]
    g_out = g_rows[:, ROWS_DOWN:ROWS_DOWN + ROWS_OUT]
    g_glu = g_rows[:, ROWS_DOWN + ROWS_OUT:]
    w_glu_f = g_glu.reshape(D_SSM, D_SSM)
    w_out_f = g_out.reshape(D_MODEL, D_MODEL)
    conv_st = _pad_to(jnp.concatenate([g_cw, conv_b.reshape(N_DEV, 1, -1)], axis=1), 1, SUBLANES)
    w_down4 = g_down.reshape(FFN_GROUPS, FFN_GROUP, D_MODEL)
    ys = _glu_fwd(yc, w_glu_f, b_glu)
    h1, hn2, mixed = _mix_fwd(xf, ys, ya, norm_out_ssm, norm_out_attn, w_out_f, norm_ffn)
    ug, uv, pg, pv, dy, loss_part = _ffn_fwd(hn2, h1, target, g_up, conv_st, w_down4, seq_len)
    loss_local = 0.5 * jnp.sum(loss_part) / D_MODEL

    dug, duv, act, dhn2, dcg, dcv = _ffn_bwd(dy, ug, uv, pg, pv, g_up, conv_st, w_down4, seq_len)
    dh1, dys, dya, d_gs, d_ga, d_gf = _mix_bwd(dy, dhn2[None], h1, ys, ya, norm_out_ssm, norm_out_attn, w_out_f, norm_ffn)
    dyc, gl_b, dz_b, d_bglu = _glu_bwd(yc, dys, w_glu_f, b_glu)

    gw_glu = _tn_matmul(gl_b, dz_b, "dw_glu", D_SSM, D_SSM)
    gw_out = _tn_matmul(mixed, dh1, "dw_out", D_MODEL, D_MODEL)
    gw_up = jnp.concatenate([_tn_grouped(hn2, dug, "dw_up_gate", True, BF16),
                             _tn_grouped(hn2, duv, "dw_up_val", True, BF16)], axis=0)
    gw_down = _tn_grouped(act, dy, "dw_down", False)
    g_conv = jnp.concatenate([dcg, dcv], axis=0)
    by_cols = lambda g, c: jnp.swapaxes(g.reshape(g.shape[0], N_DEV, c), 0, 1)
    early_flags = [True] * 2
    rows_g = jnp.concatenate([gw_down.reshape(N_DEV, ROWS_DOWN, D_MODEL), gw_out.reshape(N_DEV, ROWS_OUT, D_MODEL),
                              gw_glu.reshape(N_DEV, ROWS_GLU, D_MODEL)], axis=1).astype(BF16)
    g_sems = _exchange_start([rows_g, gw_up], early_flags, dyc, "grad_early_start", 1)
    started = g_sems[4][0, 0]

    du, dbbr, dbbi, dcr, dci, dar, dai, ddk = _s5_bwd(u, dyc, st_r, st_i, a_re, a_im, bbr, bbi, cr, ci,
                                                      d_skip + started, n_seq)
    partial_early = {
        "ab_re": jnp.sum(dar, axis=1), "ab_im": jnp.sum(dai, axis=1),
        "bb_re": _block_diag_extract(dbbr, False), "bb_im": _block_diag_extract(dbbi, False),
        "c_re": _block_diag_extract(dcr, True), "c_im": _block_diag_extract(dci, True),
        "d_skip": jnp.sum(ddk, axis=1), "b_glu": d_bglu,
        "norm_out_ssm": d_gs, "norm_out_attn": d_ga, "norm_ffn": d_gf, "conv_b": g_conv[:, 3],
    }
    early_keys = tuple(partial_early)
    early_shapes = [partial_early[k].shape for k in early_keys]
    p_sems = _exchange_start([_pack([partial_early[k] for k in early_keys])], [False], du, "small_early_start", 2)
    started = started + p_sems[4][0, 0]

    dqn, dkn, dv, dcq, dck = _attn_bwd(qkv, cq, ck + started, ya, dya, lse, n_seq)
    dcum8 = dcq[:, :, :, 0] + dck.reshape(n_seq, N_HEADS, seq_len)
    dcum = _pad_to(jnp.swapaxes(dcum8, 1, 2).reshape(n, N_HEADS), 1, LANES)
    dfl, dbf = _fprep_bwd(dcum, fl, bf, n_seq)
    dx, dproj, d_gmix, d_qg, d_kg = _inproj_bwd(xf, norm_mix, w_in_p, avg, qg, kg, raw, du, dqn, dkn, dv, dfl, dh1)

    gw_in = _tn_matmul(hn, dproj, "dw_in", D_MODEL, D_IN_PAD, out_cols=D_IN)
    partial_late = {
        "norm_mix": d_gmix, "b_forget": jnp.sum(dbf, axis=(0, 1))[:N_HEADS],
        "q_norm": jnp.sum(d_qg.reshape(N_HEADS, HEAD_DIM), axis=0),
        "k_norm": jnp.sum(d_kg.reshape(N_HEADS, HEAD_DIM), axis=0), "loss": loss_local.reshape(1),
    }
    late_keys = tuple(partial_late)
    late_shapes = [partial_late[k].shape for k in late_keys]

    land_in, land_cw, small_parts = _exchange(
        [by_cols(gw_in, D_IN // N_DEV).astype(BF16), g_conv[:, :3],
         _pack([partial_late[k] for k in late_keys])], [True, True, False], "grad_late_exchange")
    (src_rows, src_up), (land_rows, land_up) = _exchange_wait(g_sems[0], g_sems[1], g_sems[2], g_sems[3], early_flags,
                                                              land_in, "grad_early_wait")
    land_down = land_rows[:, :ROWS_DOWN]
    land_out = land_rows[:, ROWS_DOWN:ROWS_DOWN + ROWS_OUT]
    land_glu = land_rows[:, ROWS_DOWN + ROWS_OUT:].reshape(N_DEV, -1, D_SSM)
    own_rows = lax.dynamic_index_in_dim(src_rows, me_idx, 0, keepdims=False)
    own_up = lax.dynamic_index_in_dim(src_up, me_idx, 0, keepdims=False)
    own = {"w_in": None, "conv_w": None, "w_up": own_up, "w_down": own_rows[:ROWS_DOWN],
           "w_out": own_rows[ROWS_DOWN:ROWS_DOWN + ROWS_OUT], "w_glu": own_rows[ROWS_DOWN + ROWS_OUT:].reshape(-1, D_SSM)}
    grads, deltas, new_m, new_v = {}, {}, {}, {}
    for name, land in zip(SHARDED, (land_in, land_glu, land_out, land_up, land_cw, land_down)):
        grads[name], deltas[name], new_m[name], new_v[name] = _adam_sharded(
            land, own[name], weights[name], mom1[name], mom2[name], "adam_" + name, ADAM_ROWS[name])

    (own_pack,), (early_parts,) = _exchange_wait(p_sems[0], p_sems[1], p_sems[2], p_sems[3], [False], land_up,
                                                 "small_early_wait")
    summed = dict(zip(late_keys, _unpack(_sum_partials(small_parts, None, "sum_late_partials"), late_shapes)))
    summed.update(zip(early_keys, _unpack(_sum_partials(early_parts, own_pack, "sum_early_partials"), early_shapes)))
    dlr, dli, dldt, dbr_t, dbi_t = _s5_param_bwd(
        lr3, li3, ldt3, br_t, bi_t, summed["ab_re"].reshape(lr3.shape), summed["ab_im"].reshape(lr3.shape),
        summed["bb_re"], summed["bb_im"])
    small_grads = {
        "norm_mix": summed["norm_mix"], "b_forget": summed["b_forget"], "lam_re": dlr, "lam_im": dli,
        "b_re": jnp.swapaxes(dbr_t, 1, 2), "b_im": jnp.swapaxes(dbi_t, 1, 2), "c_re": summed["c_re"], "c_im": summed["c_im"],
        "d_skip": summed["d_skip"], "log_dt": dldt, "b_glu": summed["b_glu"], "q_norm": summed["q_norm"],
        "k_norm": summed["k_norm"], "norm_out_ssm": summed["norm_out_ssm"], "norm_out_attn": summed["norm_out_attn"],
        "norm_ffn": summed["norm_ffn"], "conv_b": summed["conv_b"],
    }
    repl = tuple(k for k in WEIGHT_NAMES if k not in SHARDED)
    g_list = [small_grads[k].reshape(weights[k].shape) for k in repl]
    d_list, m_list, v_list = _adam_replicated(g_list, [weights[k] for k in repl], [mom1[k] for k in repl],
                                              [mom2[k] for k in repl], "adam_replicated")
    for k, g, d, nm, nv in zip(repl, g_list, d_list, m_list, v_list):
        grads[k], deltas[k], new_m[k], new_v[k] = g, d, nm, nv

    grad_x = dx.reshape(x.shape)
    loss = summed["loss"].reshape(())
    return (loss, grad_x, *[grads[k] for k in WEIGHT_NAMES], *[deltas[k] for k in WEIGHT_NAMES],
            *[new_m[k] for k in WEIGHT_NAMES], *[new_v[k] for k in WEIGHT_NAMES])
```

```python
import functools
import math

import jax
import jax.numpy as jnp
from jax import lax
from jax.experimental import pallas as pl
from jax.experimental.pallas import tpu as pltpu

F32 = jnp.float32
BF16 = jnp.bfloat16
HIGHEST = lax.Precision.HIGHEST

N_DEV = 8
D_MODEL = 1024
D_SSM = 512
D_ATTN = 512
N_HEADS = 8
HEAD_DIM = 64
N_GROUPS = 32
SSM_GROUP = 16
SSM_STATE = 64
D_FF = 2752
D_FF_PAD = 2816
D_IN = 2056
D_IN_PAD = 2176
EPS = 1e-6
LANES = 128
SUBLANES = 8
VMEM_LIMIT = 56 * 1024 * 1024

ADAM_LR = 0.001
ADAM_B1 = 0.9
ADAM_B2 = 0.999
ADAM_EPS = 1e-08
ADAM_WD = 0.01
ADAM_STEP = 10


def _cparams(*sem):
    return pltpu.CompilerParams(dimension_semantics=sem, vmem_limit_bytes=VMEM_LIMIT)


def _dot(a, b, **kw):
    return jnp.dot(a, b, preferred_element_type=F32, **kw)


def _dot_nt(a, b):
    return lax.dot_general(a, b, (((1,), (1,)), ((), ())), preferred_element_type=F32)


def _dot_tn(a, b):
    return lax.dot_general(a, b, (((0,), (0,)), ((), ())), preferred_element_type=F32)


def _rms(x, g):
    return x * lax.rsqrt(jnp.mean(x * x, axis=-1, keepdims=True) + EPS) * g


def _split_dot(x, avg):
    hi = x.astype(BF16)
    lo = (x - hi.astype(F32)).astype(BF16)
    return _dot(hi, avg) + _dot(lo, avg)


@jax.custom_vjp
def _group_mean(x, avg):
    return _split_dot(x, avg)


def _group_mean_fwd(x, avg):
    return _split_dot(x, avg), avg


def _group_mean_bwd(avg, ct):
    return _split_dot(ct, avg), jnp.zeros_like(avg)


_group_mean.defvjp(_group_mean_fwd, _group_mean_bwd)


def _headnorm(q, avg, g):
    return q * lax.rsqrt(_group_mean(q * q, avg) + EPS) * g


def _exchange(srcs, scatter_flags, name):
    n = len(srcs)
    out_shape = []
    for s, sc in zip(srcs, scatter_flags):
        shp = s.shape if sc else (N_DEV,) + s.shape
        out_shape.append(jax.ShapeDtypeStruct(shp, s.dtype))

    def body(*refs):
        src = refs[:n]
        dst = refs[n:2 * n]
        send_sems, recv_sems, loc_sems = refs[2 * n:]
        x, y, c = lax.axis_index("x"), lax.axis_index("y"), lax.axis_index("c")
        me = 4 * x + 2 * y + c
        peers = []
        for j in range(1, N_DEV):
            px = 1 - x if (j >> 2) & 1 else x
            py = 1 - y if (j >> 1) & 1 else y
            pc = 1 - c if j & 1 else c
            peers.append(((px, py, pc), 4 * px + 2 * py + pc))
        local, sends = [], []
        for k in range(n):
            own = src[k].at[me] if scatter_flags[k] else src[k]
            lc = pltpu.make_async_copy(own, dst[k].at[me], loc_sems.at[k])
            lc.start()
            local.append(lc)
            for j, (pid, pidx) in enumerate(peers):
                s = src[k].at[pidx] if scatter_flags[k] else src[k]
                cp = pltpu.make_async_remote_copy(
                    src_ref=s, dst_ref=dst[k].at[me], send_sem=send_sems.at[k, j], recv_sem=recv_sems.at[k, j],
                    device_id=pid, device_id_type=pl.DeviceIdType.MESH)
                cp.start()
                sends.append(cp)
        for k in range(n):
            for j, (pid, pidx) in enumerate(peers):
                s = src[k].at[pidx] if scatter_flags[k] else src[k]
                pltpu.make_async_remote_copy(
                    src_ref=s, dst_ref=dst[k].at[pidx], send_sem=send_sems.at[k, j], recv_sem=recv_sems.at[k, j],
                    device_id=pid, device_id_type=pl.DeviceIdType.MESH).wait_recv()
        for cp in sends:
            cp.wait_send()
        for lc in local:
            lc.wait()

    any_spec = pl.BlockSpec(memory_space=pl.ANY)
    return pl.pallas_call(
        body, name=name, out_shape=tuple(out_shape),
        in_specs=[any_spec] * n, out_specs=tuple([any_spec] * n),
        scratch_shapes=[pltpu.SemaphoreType.DMA((n, N_DEV - 1)), pltpu.SemaphoreType.DMA((n, N_DEV - 1)),
                        pltpu.SemaphoreType.DMA((n,))],
        compiler_params=pltpu.CompilerParams(has_side_effects=True),
    )(*srcs)


def _peer_list():
    x, y, c = lax.axis_index("x"), lax.axis_index("y"), lax.axis_index("c")
    peers = []
    for j in range(1, N_DEV):
        px = 1 - x if (j >> 2) & 1 else x
        py = 1 - y if (j >> 1) & 1 else y
        pc = 1 - c if j & 1 else c
        peers.append(((px, py, pc), 4 * px + 2 * py + pc))
    return 4 * x + 2 * y + c, peers


def _split_copies(src, land, send_sems, recv_sems, scatter_flags, me, peers, incoming):
    copies = []
    for k in range(len(src)):
        for j, (pid, pidx) in enumerate(peers):
            s = src[k].at[pidx] if scatter_flags[k] else src[k]
            i = k * (N_DEV - 1) + j
            copies.append(pltpu.make_async_remote_copy(
                src_ref=s, dst_ref=land[k].at[pidx if incoming else me], send_sem=send_sems[i],
                recv_sem=recv_sems[i], device_id=pid, device_id_type=pl.DeviceIdType.MESH))
    return copies


def _exchange_start(srcs, scatter_flags, after, name, collective_id):
    n = len(srcs)
    ns = n * (N_DEV - 1)
    hbm = pl.BlockSpec(memory_space=pltpu.HBM)
    sem = pl.BlockSpec(memory_space=pltpu.SEMAPHORE)
    land_shapes = [s.shape if sc else (N_DEV,) + s.shape for s, sc in zip(srcs, scatter_flags)]

    def body(*refs):
        src, land = refs[:n], refs[n:2 * n]
        send_sems = refs[2 * n + 1:2 * n + 1 + ns]
        recv_sems = refs[2 * n + 1 + ns:2 * n + 1 + 2 * ns]
        token = refs[4 * n + 1 + 2 * ns]
        me, peers = _peer_list()
        barrier = pltpu.get_barrier_semaphore()
        for pid, _ in peers:
            pl.semaphore_signal(barrier, inc=1, device_id=pid, device_id_type=pl.DeviceIdType.MESH)
        pl.semaphore_wait(barrier, N_DEV - 1)
        for cp in _split_copies(src, land, send_sems, recv_sems, scatter_flags, me, peers, False):
            cp.start()
        token[...] = jnp.zeros_like(token)

    outs = pl.pallas_call(
        body, name=name,
        out_shape=(*[pltpu.SemaphoreType.DMA(())] * (2 * ns), *[pltpu.HBM(s.shape, s.dtype) for s in srcs],
                   *[pltpu.HBM(shp, s.dtype) for shp, s in zip(land_shapes, srcs)],
                   jax.ShapeDtypeStruct((SUBLANES, LANES), F32)),
        in_specs=[hbm] * (2 * n) + [pl.BlockSpec(memory_space=pl.ANY)],
        out_specs=(*[sem] * (2 * ns), *[hbm] * (2 * n), pl.BlockSpec(memory_space=pltpu.VMEM)),
        input_output_aliases={i: 2 * ns + i for i in range(2 * n)},
        compiler_params=pltpu.CompilerParams(has_side_effects=pltpu.SideEffectType.DATAFLOW_SIDE_EFFECTING,
                                             collective_id=collective_id),
    )(*[pltpu.with_memory_space_constraint(s, pltpu.HBM) for s in srcs],
      *[pltpu.with_memory_space_constraint(lax.empty(shp, s.dtype), pltpu.HBM) for shp, s in zip(land_shapes, srcs)],
      after)
    return (outs[:ns], outs[ns:2 * ns], outs[2 * ns:2 * ns + n], outs[2 * ns + n:2 * ns + 2 * n], outs[2 * ns + 2 * n])


def _exchange_wait(send_sems, recv_sems, srcs, lands, scatter_flags, after, name):
    n = len(srcs)
    ns = n * (N_DEV - 1)
    hbm = pl.BlockSpec(memory_space=pltpu.HBM)
    sem = pl.BlockSpec(memory_space=pltpu.SEMAPHORE)

    def body(*refs):
        src, land = refs[:n], refs[n:2 * n]
        s_sems = refs[2 * n:2 * n + ns]
        r_sems = refs[2 * n + ns:2 * n + 2 * ns]
        me, peers = _peer_list()
        for cp in _split_copies(src, land, s_sems, r_sems, scatter_flags, me, peers, True):
            cp.wait_send()
            cp.wait_recv()

    outs = pl.pallas_call(
        body, name=name,
        out_shape=tuple(pltpu.HBM(a.shape, a.dtype) for a in (*srcs, *lands)),
        in_specs=[hbm] * (2 * n) + [sem] * (2 * ns) + [pl.BlockSpec(memory_space=pl.ANY)],
        out_specs=tuple([hbm] * (2 * n)),
        input_output_aliases={i: i for i in range(2 * n)},
        compiler_params=pltpu.CompilerParams(has_side_effects=pltpu.SideEffectType.DATAFLOW_SIDE_EFFECTING),
    )(*srcs, *lands, *send_sems, *recv_sems, after)
    return outs[:n], outs[n:]


def _tn_matmul(a, b, name, tk, tm, out_rows=None, out_cols=None, tn=512):
    n_tok, k_dim = a.shape
    m_dim = b.shape[1]
    grid = (k_dim // tk, m_dim // tm, n_tok // tn)

    def body(a_ref, b_ref, o_ref):
        @pl.when(pl.program_id(2) == 0)
        def _():
            o_ref[...] = jnp.zeros_like(o_ref)
        o_ref[...] += _dot_tn(a_ref[...].astype(BF16), b_ref[...].astype(BF16))

    return pl.pallas_call(
        body, name=name, grid=grid,
        in_specs=[pl.BlockSpec((tn, tk), lambda i, j, k: (k, i)), pl.BlockSpec((tn, tm), lambda i, j, k: (k, j))],
        out_specs=pl.BlockSpec((tk, tm), lambda i, j, k: (i, j)),
        out_shape=jax.ShapeDtypeStruct((out_rows or k_dim, out_cols or m_dim), F32),
        compiler_params=_cparams("parallel", "parallel", "arbitrary"),
    )(a, b)


def _adam_math(g, w, m, v):
    m = ADAM_B1 * m + (1.0 - ADAM_B1) * g
    v = ADAM_B2 * v + (1.0 - ADAM_B2) * (g * g)
    m_hat = m / (1.0 - ADAM_B1 ** ADAM_STEP)
    v_hat = v / (1.0 - ADAM_B2 ** ADAM_STEP)
    delta = -ADAM_LR * (m_hat / (jnp.sqrt(v_hat) + ADAM_EPS) + ADAM_WD * w)
    return delta, m, v


def _adam_sharded(land, own, w, m, v, name, tr):
    _, r, c = w.shape

    def body(*refs):
        l_ref = refs[0]
        own_ref = refs[1] if own is not None else None
        w_ref, m_ref, v_ref, g_ref, d_ref, nm_ref, nv_ref = [ref.at[0] for ref in refs[-7:]]
        if own_ref is not None:
            x, y, z = lax.axis_index("x"), lax.axis_index("y"), lax.axis_index("c")
            me = 4 * x + 2 * y + z
            mine = own_ref[...].astype(F32)
        g = None
        for s in range(N_DEV):
            part = l_ref[s].astype(F32)
            if own_ref is not None:
                part = jnp.where(me == s, mine, part)
            g = part if g is None else g + part
        d, nm, nv = _adam_math(g, w_ref[...], m_ref[...], v_ref[...])
        g_ref[...] = g
        d_ref[...] = d
        nm_ref[...] = nm
        nv_ref[...] = nv

    spec = pl.BlockSpec((1, tr, c), lambda i: (0, i, 0))
    own_specs, own_args = ([pl.BlockSpec((tr, c), lambda i: (i, 0))], [own]) if own is not None else ([], [])
    return pl.pallas_call(
        body, name=name, grid=(r // tr,),
        in_specs=[pl.BlockSpec((N_DEV, tr, c), lambda i: (0, i, 0)), *own_specs, spec, spec, spec],
        out_specs=(spec, spec, spec, spec),
        out_shape=tuple(jax.ShapeDtypeStruct((1, r, c), F32) for _ in range(4)),
        compiler_params=_cparams("parallel"),
    )(land, *own_args, w, m, v)


def _sum_partials(parts, own, name):
    _, r, c = parts.shape

    def body(*refs):
        p_ref, o_ref = refs[0], refs[-1]
        if own is not None:
            x, y, z = lax.axis_index("x"), lax.axis_index("y"), lax.axis_index("c")
            me = 4 * x + 2 * y + z
            mine = refs[1][...]
        g = None
        for s in range(N_DEV):
            part = p_ref[s]
            if own is not None:
                part = jnp.where(me == s, mine, part)
            g = part if g is None else g + part
        o_ref[...] = g

    args = (parts,) if own is None else (parts, own)
    return pl.pallas_call(body, name=name, out_shape=jax.ShapeDtypeStruct((r, c), F32),
                          compiler_params=pltpu.CompilerParams(vmem_limit_bytes=VMEM_LIMIT))(*args)


def _adam_replicated(gs, ws, ms, vs, name):
    k = len(ws)

    def body(*refs):
        outs = refs[4 * k:]
        for i in range(k):
            d, nm, nv = _adam_math(refs[i][...], refs[k + i][...], refs[2 * k + i][...], refs[3 * k + i][...])
            outs[i][...] = d
            outs[k + i][...] = nm
            outs[2 * k + i][...] = nv

    outs = pl.pallas_call(body, name=name, out_shape=tuple(jax.ShapeDtypeStruct(w.shape, F32) for w in ws) * 3,
                          compiler_params=pltpu.CompilerParams(vmem_limit_bytes=VMEM_LIMIT))(*gs, *ws, *ms, *vs)
    return outs[:k], outs[k:2 * k], outs[2 * k:]


def _inproj_fwd(x, g, w_in, avg, qg, kg, tm=512):
    n = x.shape[0]

    def body(x_ref, g_ref, w_ref, a_ref, qg_ref, kg_ref, hn_ref, u_ref, qkv_ref, raw_ref, fl_ref):
        hn = _rms(x_ref[...], g_ref[...]).astype(BF16)
        hn_ref[...] = hn
        proj = _dot(hn, w_ref[...])
        u_ref[...] = proj[:, 0:512]
        q = proj[:, 512:1024]
        k = proj[:, 1024:1536]
        raw_ref[:, 0:512] = q
        raw_ref[:, 512:1024] = k
        qkv_ref[:, 0:512] = _headnorm(q, a_ref[...], qg_ref[...])
        qkv_ref[:, 512:1024] = _headnorm(k, a_ref[...], kg_ref[...])
        qkv_ref[:, 1024:1536] = proj[:, 1536:2048]
        fl_ref[...] = proj[:, 2048:D_IN_PAD]

    row = lambda w: pl.BlockSpec((tm, w), lambda i: (i, 0))
    full = lambda a: pl.BlockSpec(a.shape, lambda i: (0,) * a.ndim)
    return pl.pallas_call(
        body, name="inproj_fwd", grid=(n // tm,),
        in_specs=[row(D_MODEL), full(g), full(w_in), full(avg), full(qg), full(kg)],
        out_specs=(row(D_MODEL), row(512), row(1536), row(1024), row(LANES)),
        out_shape=(jax.ShapeDtypeStruct((n, D_MODEL), BF16), jax.ShapeDtypeStruct((n, 512), F32),
                   jax.ShapeDtypeStruct((n, 1536), F32), jax.ShapeDtypeStruct((n, 1024), F32),
                   jax.ShapeDtypeStruct((n, LANES), F32)),
        compiler_params=_cparams("parallel"),
    )(x, g, w_in, avg, qg, kg)


def _inproj_bwd(x, g, w_in, avg, qg, kg, raw, du, dqn, dkn, dv, dfl, dres, tm=512):
    n = x.shape[0]

    def body(x_ref, g_ref, w_ref, a_ref, qg_ref, kg_ref, raw_ref, du_ref, dqn_ref, dkn_ref, dv_ref, dfl_ref, dres_ref,
             dx_ref, dproj_ref, dg_ref, dqg_ref, dkg_ref):
        @pl.when(pl.program_id(0) == 0)
        def _():
            dg_ref[...] = jnp.zeros_like(dg_ref)
            dqg_ref[...] = jnp.zeros_like(dqg_ref)
            dkg_ref[...] = jnp.zeros_like(dkg_ref)
        avg_m = a_ref[...]
        _, vjp_q = jax.vjp(lambda q, gg: _headnorm(q, avg_m, gg), raw_ref[:, 0:512], qg_ref[...])
        dq, dqg = vjp_q(dqn_ref[...])
        _, vjp_k = jax.vjp(lambda k, gg: _headnorm(k, avg_m, gg), raw_ref[:, 512:1024], kg_ref[...])
        dk, dkg = vjp_k(dkn_ref[...])
        dproj = jnp.concatenate([du_ref[...], dq, dk, dv_ref[...], dfl_ref[...]], axis=1).astype(BF16)
        dproj_ref[...] = dproj
        dhn = _dot_nt(dproj, w_ref[...])
        _, vjp_x = jax.vjp(_rms, x_ref[...], g_ref[...])
        dxn, dg = vjp_x(dhn)
        dx_ref[...] = dxn + dres_ref[...]
        dg_ref[...] += dg
        dqg_ref[...] += dqg
        dkg_ref[...] += dkg

    row = lambda w: pl.BlockSpec((tm, w), lambda i: (i, 0))
    full = lambda a: pl.BlockSpec(a.shape, lambda i: (0,) * a.ndim)
    vec = lambda w: pl.BlockSpec((1, w), lambda i: (0, 0))
    return pl.pallas_call(
        body, name="inproj_bwd", grid=(n // tm,),
        in_specs=[row(D_MODEL), full(g), full(w_in), full(avg), full(qg), full(kg), row(1024), row(512), row(512),
                  row(512), row(512), row(LANES), row(D_MODEL)],
        out_specs=(row(D_MODEL), row(D_IN_PAD), vec(D_MODEL), vec(512), vec(512)),
        out_shape=(jax.ShapeDtypeStruct((n, D_MODEL), F32), jax.ShapeDtypeStruct((n, D_IN_PAD), BF16),
                   jax.ShapeDtypeStruct((1, D_MODEL), F32), jax.ShapeDtypeStruct((1, 512), F32),
                   jax.ShapeDtypeStruct((1, 512), F32)),
        compiler_params=_cparams("arbitrary"),
    )(x, g, w_in, avg, qg, kg, raw, du, dqn, dkn, dv, dfl, dres)


def _glu_fwd(yc, wg, bg, tm=512):
    n = yc.shape[0]

    def body(yc_ref, w_ref, b_ref, ys_ref):
        gl = jax.nn.gelu(yc_ref[...])
        z = _dot(gl.astype(BF16), w_ref[...]) + b_ref[...]
        ys_ref[...] = gl * jax.nn.sigmoid(z)

    row = pl.BlockSpec((tm, 512), lambda i: (i, 0))
    full = lambda a: pl.BlockSpec(a.shape, lambda i: (0,) * a.ndim)
    return pl.pallas_call(
        body, name="glu_fwd", grid=(n // tm,), in_specs=[row, full(wg), full(bg)], out_specs=row,
        out_shape=jax.ShapeDtypeStruct((n, 512), F32), compiler_params=_cparams("parallel"),
    )(yc, wg, bg)


def _glu_bwd(yc, dys, wg, bg, tm=512):
    n = yc.shape[0]

    def body(yc_ref, dys_ref, w_ref, b_ref, dyc_ref, gl_ref, dz_ref, db_ref):
        @pl.when(pl.program_id(0) == 0)
        def _():
            db_ref[...] = jnp.zeros_like(db_ref)
        gl, vjp_gelu = jax.vjp(jax.nn.gelu, yc_ref[...])
        glb = gl.astype(BF16)
        z = _dot(glb, w_ref[...]) + b_ref[...]
        s = jax.nn.sigmoid(z)
        dys = dys_ref[...]
        dz = dys * gl * s * (1.0 - s)
        dzb = dz.astype(BF16)
        dgl = dys * s + _dot_nt(dzb, w_ref[...])
        dyc_ref[...] = vjp_gelu(dgl)[0]
        gl_ref[...] = glb
        dz_ref[...] = dzb
        db_ref[...] += jnp.sum(dz, axis=0, keepdims=True)

    row = pl.BlockSpec((tm, 512), lambda i: (i, 0))
    full = lambda a: pl.BlockSpec(a.shape, lambda i: (0,) * a.ndim)
    return pl.pallas_call(
        body, name="glu_bwd", grid=(n // tm,), in_specs=[row, row, full(wg), full(bg)],
        out_specs=(row, row, row, pl.BlockSpec((1, 512), lambda i: (0, 0))),
        out_shape=(jax.ShapeDtypeStruct((n, 512), F32), jax.ShapeDtypeStruct((n, 512), BF16),
                   jax.ShapeDtypeStruct((n, 512), BF16), jax.ShapeDtypeStruct((1, 512), F32)),
        compiler_params=_cparams("arbitrary"),
    )(yc, dys, wg, bg)


def _mix_fwd(x, ys, ya, gs, ga, wout, gf, tm=512):
    n = x.shape[0]

    def body(x_ref, ys_ref, ya_ref, gs_ref, ga_ref, w_ref, gf_ref, h1_ref, hn2_ref, mixed_ref):
        mixed = jnp.concatenate([_rms(ys_ref[...], gs_ref[...]), _rms(ya_ref[...], ga_ref[...])], axis=1).astype(BF16)
        mixed_ref[...] = mixed
        h1 = x_ref[...] + _dot(mixed, w_ref[...])
        h1_ref[...] = h1
        hn2_ref[...] = _rms(h1, gf_ref[...]).astype(BF16)

    row = lambda w: pl.BlockSpec((tm, w), lambda i: (i, 0))
    full = lambda a: pl.BlockSpec(a.shape, lambda i: (0,) * a.ndim)
    return pl.pallas_call(
        body, name="mix_fwd", grid=(n // tm,),
        in_specs=[row(D_MODEL), row(512), row(512), full(gs), full(ga), full(wout), full(gf)],
        out_specs=(row(D_MODEL), row(D_MODEL), row(D_MODEL)),
        out_shape=(jax.ShapeDtypeStruct((n, D_MODEL), F32), jax.ShapeDtypeStruct((n, D_MODEL), BF16),
                   jax.ShapeDtypeStruct((n, D_MODEL), BF16)),
        compiler_params=_cparams("parallel"),
    )(x, ys, ya, gs, ga, wout, gf)


def _mix_bwd(dy, dhn2_parts, h1, ys, ya, gs, ga, wout, gf, tm=512):
    n = dy.shape[0]
    n_parts = dhn2_parts.shape[0]

    def body(dy_ref, dp_ref, h1_ref, ys_ref, ya_ref, gs_ref, ga_ref, w_ref, gf_ref,
             dh1_ref, dys_ref, dya_ref, dgs_ref, dga_ref, dgf_ref):
        @pl.when(pl.program_id(0) == 0)
        def _():
            dgs_ref[...] = jnp.zeros_like(dgs_ref)
            dga_ref[...] = jnp.zeros_like(dga_ref)
            dgf_ref[...] = jnp.zeros_like(dgf_ref)
        dhn2 = dp_ref[0]
        for p in range(1, n_parts):
            dhn2 = dhn2 + dp_ref[p]
        _, vjp_f = jax.vjp(_rms, h1_ref[...], gf_ref[...])
        dh1n, dgf = vjp_f(dhn2)
        dh1 = dy_ref[...] + dh1n
        dh1_ref[...] = dh1
        dmixed = _dot_nt(dh1.astype(BF16), w_ref[...])
        _, vjp_s = jax.vjp(_rms, ys_ref[...], gs_ref[...])
        dys, dgs = vjp_s(dmixed[:, 0:512])
        _, vjp_a = jax.vjp(_rms, ya_ref[...], ga_ref[...])
        dya, dga = vjp_a(dmixed[:, 512:1024])
        dys_ref[...] = dys
        dya_ref[...] = dya
        dgs_ref[...] += dgs
        dga_ref[...] += dga
        dgf_ref[...] += dgf

    row = lambda w: pl.BlockSpec((tm, w), lambda i: (i, 0))
    full = lambda a: pl.BlockSpec(a.shape, lambda i: (0,) * a.ndim)
    vec = lambda w: pl.BlockSpec((1, w), lambda i: (0, 0))
    return pl.pallas_call(
        body, name="mix_bwd", grid=(n // tm,),
        in_specs=[row(D_MODEL), pl.BlockSpec((n_parts, tm, D_MODEL), lambda i: (0, i, 0)), row(D_MODEL), row(512),
                  row(512), full(gs), full(ga), full(wout), full(gf)],
        out_specs=(row(D_MODEL), row(512), row(512), vec(512), vec(512), vec(D_MODEL)),
        out_shape=(jax.ShapeDtypeStruct((n, D_MODEL), F32), jax.ShapeDtypeStruct((n, 512), F32),
                   jax.ShapeDtypeStruct((n, 512), F32), jax.ShapeDtypeStruct((1, 512), F32),
                   jax.ShapeDtypeStruct((1, 512), F32), jax.ShapeDtypeStruct((1, D_MODEL), F32)),
        compiler_params=_cparams("arbitrary"),
    )(dy, dhn2_parts, h1, ys, ya, gs, ga, wout, gf)


HALO = 16
FFN_GROUPS = 4
FFN_GROUP = D_FF // FFN_GROUPS


def _conv3(ue, cw):
    return cw[2:3] * ue + cw[1:2] * pltpu.roll(ue, 1, 0) + cw[0:1] * pltpu.roll(ue, 2, 0) + cw[3:4]


def _ffn_weight_specs():
    gate = lambda i, j: (j, 0, 0)
    val = lambda i, j: (j + FFN_GROUPS, 0, 0)
    w_blk, c_blk = (1, D_MODEL, FFN_GROUP), (1, SUBLANES, FFN_GROUP)
    return [pl.BlockSpec(w_blk, gate), pl.BlockSpec(w_blk, val), pl.BlockSpec(c_blk, gate), pl.BlockSpec(c_blk, val),
            pl.BlockSpec((1, FFN_GROUP, D_MODEL), gate)]


def _ffn_fwd(hn2, h1, target, w_up, conv, w_down, seq_len, tm=512):
    n = hn2.shape[0]
    nj = FFN_GROUPS
    hb = tm // HALO

    def body(hn_ref, halo_ref, h1_ref, tgt_ref, wg_ref, wv_ref, cg_ref, cv_ref, wd_ref,
             ug_ref, uv_ref, pg_ref, pv_ref, dy_ref, loss_ref, acc):
        i, j = pl.program_id(0), pl.program_id(1)
        seq_start = (i * tm) % seq_len == 0
        halo = halo_ref[...]
        halo = jnp.where(seq_start, jnp.zeros_like(halo), halo)
        he = jnp.concatenate([halo, hn_ref[...]], axis=0)
        ueg = _dot(he, wg_ref[0])
        uev = _dot(he, wv_ref[0])
        ug_ref[0] = ueg[HALO:].astype(BF16)
        uv_ref[0] = uev[HALO:].astype(BF16)
        cg = _conv3(ueg, cg_ref[0])[HALO:]
        cv = _conv3(uev, cv_ref[0])[HALO:]
        pg_ref[0] = cg.astype(BF16)
        pv_ref[0] = cv.astype(BF16)
        act = (jax.nn.silu(cg) * cv).astype(BF16)
        part = _dot(act, wd_ref[0])

        @pl.when(j == 0)
        def _():
            acc[...] = part

        @pl.when(j > 0)
        def _():
            acc[...] += part

        @pl.when(j == nj - 1)
        def _():
            err = h1_ref[...] + acc[...] - tgt_ref[...]
            dy_ref[...] = err * (1.0 / D_MODEL)
            loss_ref[0] = jnp.sum(err * err, axis=0, keepdims=True)

    row = pl.BlockSpec((tm, D_MODEL), lambda i, j: (i, 0))
    u_main = pl.BlockSpec((1, tm, FFN_GROUP), lambda i, j: (j, i, 0))
    u_shape = jax.ShapeDtypeStruct((FFN_GROUPS, n, FFN_GROUP), BF16)
    return pl.pallas_call(
        body, name="ffn_fwd", grid=(n // tm, nj),
        in_specs=[row, pl.BlockSpec((HALO, D_MODEL), lambda i, j: (jnp.maximum(i * hb - 1, 0), 0)), row, row,
                  *_ffn_weight_specs()],
        out_specs=(u_main, u_main, u_main, u_main, row, pl.BlockSpec((1, 1, D_MODEL), lambda i, j: (i, 0, 0))),
        out_shape=(u_shape, u_shape, u_shape, u_shape, jax.ShapeDtypeStruct((n, D_MODEL), F32),
                   jax.ShapeDtypeStruct((n // tm, 1, D_MODEL), F32)),
        scratch_shapes=[pltpu.VMEM((tm, D_MODEL), F32)],
        compiler_params=_cparams("parallel", "arbitrary"),
    )(hn2, hn2, h1, target, w_up, w_up, conv, conv, w_down)


def _ffn_bwd(dy, ug, uv, pg, pv, w_up, conv, w_down, seq_len, tm=512):
    n = dy.shape[0]
    nj = FFN_GROUPS
    fb = FFN_GROUP
    hb = tm // HALO
    last_hb = n // HALO - 1
    rows = tm + HALO

    def body(dy_ref, dyn_ref, ug_ref, uv_ref, pgm_ref, pgn_ref, pvm_ref, pvn_ref, wg_ref, wv_ref, cg_ref, cv_ref,
             wd_ref, dug_ref, duv_ref, act_ref, dhn_ref, dcg_ref, dcv_ref, acc):
        i, j = pl.program_id(0), pl.program_id(1)
        seq_end = ((i + 1) * tm) % seq_len == 0
        dyn = dyn_ref[...]
        dyn = jnp.where(seq_end, jnp.zeros_like(dyn), dyn)
        d_out = jnp.concatenate([dy_ref[...], dyn], axis=0).astype(BF16)
        d_act = _dot_nt(d_out, wd_ref[0])
        cge = jnp.concatenate([pgm_ref[0], pgn_ref[0]], axis=0).astype(F32)
        cve = jnp.concatenate([pvm_ref[0], pvn_ref[0]], axis=0).astype(F32)
        act, vjp_act = jax.vjp(lambda g, v: jax.nn.silu(g) * v, cge, cve)
        dcge, dcve = vjp_act(d_act)
        act_ref[0] = act[:tm].astype(BF16)

        def conv_t(dc, u_ref, cw):
            ahead1 = pltpu.roll(dc, rows - 1, 0)[:tm]
            ahead2 = pltpu.roll(dc, rows - 2, 0)[:tm]
            here = dc[:tm]
            du = cw[2:3] * here + cw[1:2] * ahead1 + cw[0:1] * ahead2
            u = u_ref[0].astype(F32)
            col = lambda x: jnp.sum(x, axis=0, keepdims=True)
            grad = jnp.concatenate([col(ahead2 * u), col(ahead1 * u), col(here * u), col(here),
                                    jnp.zeros((4, fb), F32)], axis=0)
            return du.astype(BF16), grad

        cwg, cwv = cg_ref[0], cv_ref[0]
        dug, grad_g = conv_t(dcge, ug_ref, cwg)
        duv, grad_v = conv_t(dcve, uv_ref, cwv)
        dug_ref[0] = dug
        duv_ref[0] = duv
        part = _dot_nt(dug, wg_ref[0]) + _dot_nt(duv, wv_ref[0])

        @pl.when(j == 0)
        def _():
            acc[...] = part

        @pl.when(j > 0)
        def _():
            acc[...] += part

        @pl.when(j == nj - 1)
        def _():
            dhn_ref[...] = acc[...]

        @pl.when(i == 0)
        def _():
            dcg_ref[j] = jnp.zeros((8, fb), F32)
            dcv_ref[j] = jnp.zeros((8, fb), F32)

        dcg_ref[j] += grad_g
        dcv_ref[j] += grad_v

    row = pl.BlockSpec((tm, D_MODEL), lambda i, j: (i, 0))
    u_main = pl.BlockSpec((1, tm, fb), lambda i, j: (j, i, 0))
    u_next = pl.BlockSpec((1, HALO, fb), lambda i, j: (j, jnp.minimum((i + 1) * hb, last_hb), 0))
    dc_spec = pl.BlockSpec((nj, 8, fb), lambda i, j: (0, 0, 0))
    u_shape = jax.ShapeDtypeStruct((FFN_GROUPS, n, fb), BF16)
    return pl.pallas_call(
        body, name="ffn_bwd", grid=(n // tm, nj),
        in_specs=[row, pl.BlockSpec((HALO, D_MODEL), lambda i, j: (jnp.minimum((i + 1) * hb, last_hb), 0)),
                  u_main, u_main, u_main, u_next, u_main, u_next, *_ffn_weight_specs()],
        out_specs=(u_main, u_main, u_main, row, dc_spec, dc_spec),
        out_shape=(u_shape, u_shape, u_shape, jax.ShapeDtypeStruct((n, D_MODEL), F32),
                   jax.ShapeDtypeStruct((nj, 8, fb), F32), jax.ShapeDtypeStruct((nj, 8, fb), F32)),
        scratch_shapes=[pltpu.VMEM((tm, D_MODEL), F32)],
        compiler_params=_cparams("arbitrary", "arbitrary"),
    )(dy, dy, ug, uv, pg, pg, pv, pv, w_up, w_up, conv, conv, w_down)


def _tn_grouped(a, b, name, shared_a, out_dtype=F32, tn=1024):
    groups = b.shape[0] if shared_a else a.shape[0]
    n_tok = a.shape[0] if shared_a else b.shape[0]
    k_dim, m_dim = a.shape[-1], b.shape[-1]

    def body(a_ref, b_ref, o_ref, acc):
        k = pl.program_id(1)
        a_t = a_ref[...] if shared_a else a_ref[0]
        b_t = b_ref[0] if shared_a else b_ref[...]
        part = _dot_tn(a_t.astype(BF16), b_t.astype(BF16))

        @pl.when(k == 0)
        def _():
            acc[...] = part

        @pl.when(k > 0)
        def _():
            acc[...] += part

        @pl.when(k == n_tok // tn - 1)
        def _():
            o_ref[0] = acc[...].astype(out_dtype)

    plain = lambda w: pl.BlockSpec((tn, w), lambda g, k: (k, 0))
    grouped = lambda w: pl.BlockSpec((1, tn, w), lambda g, k: (g, k, 0))
    return pl.pallas_call(
        body, name=name, grid=(groups, n_tok // tn),
        in_specs=[plain(k_dim), grouped(m_dim)] if shared_a else [grouped(k_dim), plain(m_dim)],
        out_specs=pl.BlockSpec((1, k_dim, m_dim), lambda g, k: (g, 0, 0)),
        out_shape=jax.ShapeDtypeStruct((groups, k_dim, m_dim), out_dtype),
        scratch_shapes=[pltpu.VMEM((k_dim, m_dim), F32)],
        compiler_params=_cparams("parallel", "arbitrary"),
    )(a, b)


def _s5_param_fn(lr, li, ldt, br, bi):
    dt = jnp.exp(ldt)
    mag = jnp.exp(lr * dt)
    ab_re = mag * jnp.cos(li * dt)
    ab_im = mag * jnp.sin(li * dt)
    nr = ab_re - 1.0
    ni = ab_im
    den = lr * lr + li * li
    q_re = (nr * lr + ni * li) / den
    q_im = (ni * lr - nr * li) / den
    bb_re = q_re * br - q_im * bi
    bb_im = q_re * bi + q_im * br
    return ab_re, ab_im, bb_re, bb_im


def _s5_param_fwd(lr, li, ldt, br, bi):
    def body(lr_ref, li_ref, ldt_ref, br_ref, bi_ref, ar_ref, ai_ref, bbr_ref, bbi_ref):
        ar, ai, bbr, bbi = _s5_param_fn(lr_ref[...], li_ref[...], ldt_ref[...], br_ref[...], bi_ref[...])
        ar_ref[...] = ar
        ai_ref[...] = ai
        bbr_ref[...] = bbr
        bbi_ref[...] = bbi

    return pl.pallas_call(
        body, name="s5_param_fwd",
        out_shape=(jax.ShapeDtypeStruct(lr.shape, F32), jax.ShapeDtypeStruct(lr.shape, F32),
                   jax.ShapeDtypeStruct(br.shape, F32), jax.ShapeDtypeStruct(br.shape, F32)),
    )(lr, li, ldt, br, bi)


def _s5_param_bwd(lr, li, ldt, br, bi, dar, dai, dbbr, dbbi):
    def body(lr_ref, li_ref, ldt_ref, br_ref, bi_ref, dar_ref, dai_ref, dbbr_ref, dbbi_ref,
             dlr_ref, dli_ref, dldt_ref, dbr_ref, dbi_ref):
        _, vjp = jax.vjp(_s5_param_fn, lr_ref[...], li_ref[...], ldt_ref[...], br_ref[...], bi_ref[...])
        dlr, dli, dldt, dbr, dbi = vjp((dar_ref[...], dai_ref[...], dbbr_ref[...], dbbi_ref[...]))
        dlr_ref[...] = dlr
        dli_ref[...] = dli
        dldt_ref[...] = dldt
        dbr_ref[...] = dbr
        dbi_ref[...] = dbi

    return pl.pallas_call(
        body, name="s5_param_bwd",
        out_shape=(jax.ShapeDtypeStruct(lr.shape, F32), jax.ShapeDtypeStruct(lr.shape, F32),
                   jax.ShapeDtypeStruct(ldt.shape, F32), jax.ShapeDtypeStruct(br.shape, F32),
                   jax.ShapeDtypeStruct(br.shape, F32)),
    )(lr, li, ldt, br, bi, dar, dai, dbbr, dbbi)


S5_CHUNK = 256
S5_STATES = 512
S5_BLOCKS = 4


def _cpow_rows(ar, ai, count):
    rs, im = [ar], [ai]
    for _ in range(count - 1):
        pr, pi = rs[-1], im[-1]
        rs.append(pr * ar - pi * ai)
        im.append(pr * ai + pi * ar)
    return rs, im


def _scan_in_groups(vr, vi, pr, pi, rm, reverse):
    n, width = vr.shape
    vr = vr.reshape(n // SUBLANES, SUBLANES, width)
    vi = vi.reshape(n // SUBLANES, SUBLANES, width)
    row = rm[0:SUBLANES]
    for k in (1, 2, 4):
        shift = SUBLANES - k if reverse else k
        keep = row < SUBLANES - k if reverse else row >= k
        kr = jnp.where(keep, pr[k - 1], 0.0)
        ki = jnp.where(keep, pi[k - 1], 0.0)
        sr, si = pltpu.roll(vr, shift, 1), pltpu.roll(vi, shift, 1)
        vr, vi = vr + kr * sr - ki * si, vi + kr * si + ki * sr
    return vr.reshape(n, width), vi.reshape(n, width)


def _carry_over_groups(xr_s, xi_s, wr, wi, c0r, c0i, reverse):
    groups = xr_s.shape[0] // SUBLANES
    pick = 0 if reverse else SUBLANES - 1

    def step(q, carry):
        cr, ci = carry
        r = groups - 1 - q if reverse else q
        o = pl.multiple_of(r * SUBLANES, SUBLANES)
        vr = xr_s[pl.ds(o, SUBLANES), :]
        vi = xi_s[pl.ds(o, SUBLANES), :]
        nr = vr + wr * cr - wi * ci
        ni = vi + wr * ci + wi * cr
        xr_s[pl.ds(o, SUBLANES), :] = nr
        xi_s[pl.ds(o, SUBLANES), :] = ni
        return (jnp.broadcast_to(nr[pick:pick + 1], nr.shape), jnp.broadcast_to(ni[pick:pick + 1], ni.shape))

    return lax.fori_loop(0, groups, step, (c0r, c0i))


def _s5_state_scan(u_b, bbr, bbi, pr, pi, rm, xr_s, xi_s, c0r, c0i):
    bur = _dot(u_b, bbr)
    bui = _dot(u_b, bbi)
    bur, bui = _scan_in_groups(bur, bui, pr, pi, rm, False)
    xr_s[...] = bur
    xi_s[...] = bui
    w8r = jnp.concatenate(pr, axis=0)
    w8i = jnp.concatenate(pi, axis=0)
    return _carry_over_groups(xr_s, xi_s, w8r, w8i, c0r, c0i, False)


def _s5_fwd(u, a_re, a_im, bbr, bbi, cr, ci, d_skip, n_seq):
    n = u.shape[0]
    seq_len = n // n_seq
    nt = seq_len // S5_CHUNK
    tc = S5_CHUNK

    def body(u_ref, ar_ref, ai_ref, bbr_ref, bbi_ref, cr_ref, ci_ref, d_ref, y_ref, str_ref, sti_ref,
             xr_s, xi_s, car_r, car_i):
        t = pl.program_id(2)

        @pl.when(t == 0)
        def _():
            car_r[...] = jnp.zeros_like(car_r)
            car_i[...] = jnp.zeros_like(car_i)
        pr, pi = _cpow_rows(ar_ref[0], ai_ref[0], SUBLANES)
        rm = lax.broadcasted_iota(jnp.int32, (tc, S5_STATES), 0) & (SUBLANES - 1)
        str_ref[0, 0] = car_r[...]
        sti_ref[0, 0] = car_i[...]
        u_t = u_ref[...]
        cfr, cfi = _s5_state_scan(u_t.astype(BF16), bbr_ref[0], bbi_ref[0], pr, pi, rm, xr_s, xi_s,
                                  car_r[...], car_i[...])
        car_r[...] = cfr
        car_i[...] = cfi
        y = _dot(xr_s[...].astype(BF16), cr_ref[0]) - _dot(xi_s[...].astype(BF16), ci_ref[0])
        y_ref[...] = y + d_ref[...] * u_t

    u_spec = pl.BlockSpec((tc, LANES), lambda cb, b, t: (b * nt + t, cb))
    a_spec = pl.BlockSpec((1, 1, S5_STATES), lambda cb, b, t: (cb, 0, 0))
    bb_spec = pl.BlockSpec((1, LANES, S5_STATES), lambda cb, b, t: (cb, 0, 0))
    c_spec = pl.BlockSpec((1, S5_STATES, LANES), lambda cb, b, t: (cb, 0, 0))
    st_spec = pl.BlockSpec((1, 1, SUBLANES, S5_STATES), lambda cb, b, t: (cb, b * nt + t, 0, 0))
    st_shape = jax.ShapeDtypeStruct((S5_BLOCKS, n_seq * nt, SUBLANES, S5_STATES), F32)
    return pl.pallas_call(
        body, name="s5_fwd", grid=(S5_BLOCKS, n_seq, nt),
        in_specs=[u_spec, a_spec, a_spec, bb_spec, bb_spec, c_spec, c_spec,
                  pl.BlockSpec((1, LANES), lambda cb, b, t: (0, cb))],
        out_specs=(u_spec, st_spec, st_spec),
        out_shape=(jax.ShapeDtypeStruct((n, D_SSM), F32), st_shape, st_shape),
        scratch_shapes=[pltpu.VMEM((tc, S5_STATES), F32), pltpu.VMEM((tc, S5_STATES), F32),
                        pltpu.VMEM((SUBLANES, S5_STATES), F32), pltpu.VMEM((SUBLANES, S5_STATES), F32)],
        compiler_params=_cparams("parallel", "arbitrary", "arbitrary"),
    )(u, a_re, a_im, bbr, bbi, cr, ci, d_skip)


def _s5_bwd(u, dy, st_r, st_i, a_re, a_im, bbr, bbi, cr, ci, d_skip, n_seq):
    n = u.shape[0]
    seq_len = n // n_seq
    nt = seq_len // S5_CHUNK
    tc = S5_CHUNK

    def body(u_ref, dy_ref, str_ref, sti_ref, ar_ref, ai_ref, bbr_ref, bbi_ref, cr_ref, ci_ref, d_ref,
             du_ref, dbbr_ref, dbbi_ref, dcr_ref, dci_ref, dar_ref, dai_ref, dd_ref,
             xr_s, xi_s, gr_s, gi_s, car_r, car_i):
        b, t = pl.program_id(1), pl.program_id(2)

        @pl.when((b == 0) & (t == 0))
        def _():
            for ref in (dbbr_ref, dbbi_ref, dcr_ref, dci_ref, dar_ref, dai_ref, dd_ref):
                ref[...] = jnp.zeros_like(ref)

        @pl.when(t == 0)
        def _():
            car_r[...] = jnp.zeros_like(car_r)
            car_i[...] = jnp.zeros_like(car_i)
        ar, ai = ar_ref[0], ai_ref[0]
        pr, pi = _cpow_rows(ar, ai, SUBLANES)
        row = lax.broadcasted_iota(jnp.int32, (tc, S5_STATES), 0)
        rm = row & (SUBLANES - 1)
        u_t = u_ref[...]
        u_b = u_t.astype(BF16)
        dy_t = dy_ref[...]
        dy_b = dy_t.astype(BF16)
        s0r, s0i = str_ref[0, 0], sti_ref[0, 0]
        _s5_state_scan(u_b, bbr_ref[0], bbi_ref[0], pr, pi, rm, xr_s, xi_s, s0r, s0i)
        xr, xi = xr_s[...], xi_s[...]
        gr = _dot_nt(dy_b, cr_ref[0])
        gi = -_dot_nt(dy_b, ci_ref[0])
        npi = [-v for v in pi]
        gr, gi = _scan_in_groups(gr, gi, pr, npi, rm, True)
        gr_s[...] = gr
        gi_s[...] = gi
        w8r = jnp.concatenate(pr[::-1], axis=0)
        w8i = jnp.concatenate(npi[::-1], axis=0)
        cfr, cfi = _carry_over_groups(gr_s, gi_s, w8r, w8i, car_r[...], car_i[...], True)
        car_r[...] = cfr
        car_i[...] = cfi
        gr, gi = gr_s[...], gi_s[...]
        gr_b, gi_b = gr.astype(BF16), gi.astype(BF16)
        du_ref[...] = _dot_nt(gr_b, bbr_ref[0]) + _dot_nt(gi_b, bbi_ref[0]) + d_ref[...] * dy_t
        dbbr_ref[0] += _dot_tn(u_b, gr_b)
        dbbi_ref[0] += _dot_tn(u_b, gi_b)
        dcr_ref[0] += _dot_tn(xr.astype(BF16), dy_b)
        dci_ref[0] -= _dot_tn(xi.astype(BF16), dy_b)
        dd_ref[0] += jnp.sum((dy_t * u_t).reshape(tc // SUBLANES, SUBLANES, LANES), axis=0)
        first = row == 0
        xpr = jnp.where(first, jnp.broadcast_to(s0r[0:1], xr.shape), pltpu.roll(xr, 1, 0))
        xpi = jnp.where(first, jnp.broadcast_to(s0i[0:1], xi.shape), pltpu.roll(xi, 1, 0))
        shp = (tc // SUBLANES, SUBLANES, S5_STATES)
        dar_ref[0] += jnp.sum((gr * xpr + gi * xpi).reshape(shp), axis=0)
        dai_ref[0] += jnp.sum((gi * xpr - gr * xpi).reshape(shp), axis=0)

    u_spec = pl.BlockSpec((tc, LANES), lambda cb, b, t: (b * nt + nt - 1 - t, cb))
    a_spec = pl.BlockSpec((1, 1, S5_STATES), lambda cb, b, t: (cb, 0, 0))
    bb_spec = pl.BlockSpec((1, LANES, S5_STATES), lambda cb, b, t: (cb, 0, 0))
    c_spec = pl.BlockSpec((1, S5_STATES, LANES), lambda cb, b, t: (cb, 0, 0))
    st_spec = pl.BlockSpec((1, 1, SUBLANES, S5_STATES), lambda cb, b, t: (cb, b * nt + nt - 1 - t, 0, 0))
    da_spec = pl.BlockSpec((1, SUBLANES, S5_STATES), lambda cb, b, t: (cb, 0, 0))
    dd_spec = pl.BlockSpec((1, SUBLANES, LANES), lambda cb, b, t: (cb, 0, 0))
    big = pltpu.VMEM((tc, S5_STATES), F32)
    small = pltpu.VMEM((SUBLANES, S5_STATES), F32)
    return pl.pallas_call(
        body, name="s5_bwd", grid=(S5_BLOCKS, n_seq, nt),
        in_specs=[u_spec, u_spec, st_spec, st_spec, a_spec, a_spec, bb_spec, bb_spec, c_spec, c_spec,
                  pl.BlockSpec((1, LANES), lambda cb, b, t: (0, cb))],
        out_specs=(u_spec, bb_spec, bb_spec, c_spec, c_spec, da_spec, da_spec, dd_spec),
        out_shape=(jax.ShapeDtypeStruct((n, D_SSM), F32),
                   jax.ShapeDtypeStruct((S5_BLOCKS, LANES, S5_STATES), F32),
                   jax.ShapeDtypeStruct((S5_BLOCKS, LANES, S5_STATES), F32),
                   jax.ShapeDtypeStruct((S5_BLOCKS, S5_STATES, LANES), F32),
                   jax.ShapeDtypeStruct((S5_BLOCKS, S5_STATES, LANES), F32),
                   jax.ShapeDtypeStruct((S5_BLOCKS, SUBLANES, S5_STATES), F32),
                   jax.ShapeDtypeStruct((S5_BLOCKS, SUBLANES, S5_STATES), F32),
                   jax.ShapeDtypeStruct((S5_BLOCKS, SUBLANES, LANES), F32)),
        scratch_shapes=[big, big, big, big, small, small],
        compiler_params=_cparams("parallel", "arbitrary", "arbitrary"),
    )(u, dy, st_r, st_i, a_re, a_im, bbr, bbi, cr, ci, d_skip)


CUM_BLOCK = 128


def _tri(lower):
    r = lax.broadcasted_iota(jnp.int32, (CUM_BLOCK, CUM_BLOCK), 0)
    c = lax.broadcasted_iota(jnp.int32, (CUM_BLOCK, CUM_BLOCK), 1)
    return jnp.where(r >= c if lower else r <= c, 1.0, 0.0).astype(F32)


def _fprep_fwd(fl, bf, n_seq):
    n = fl.shape[0]
    seq_len = n // n_seq
    nb = seq_len // CUM_BLOCK

    def body(fl_ref, bf_ref, cum_ref):
        tril = _tri(True)
        carry = jnp.zeros((1, LANES), F32)
        for blk in range(nb):
            rows = slice(blk * CUM_BLOCK, (blk + 1) * CUM_BLOCK)
            lf = jax.nn.log_sigmoid(fl_ref[rows, :] + bf_ref[...])
            cs = jnp.dot(tril, lf, preferred_element_type=F32, precision=HIGHEST) + carry
            cum_ref[rows, :] = cs
            carry = cs[CUM_BLOCK - 1:CUM_BLOCK, :]

    spec = pl.BlockSpec((seq_len, LANES), lambda b: (b, 0))
    return pl.pallas_call(
        body, name="fprep_fwd", grid=(n_seq,), in_specs=[spec, pl.BlockSpec((1, LANES), lambda b: (0, 0))],
        out_specs=spec, out_shape=jax.ShapeDtypeStruct((n, LANES), F32), compiler_params=_cparams("parallel"),
    )(fl, bf)


def _fprep_bwd(dcum, fl, bf, n_seq):
    n = fl.shape[0]
    seq_len = n // n_seq
    nb = seq_len // CUM_BLOCK

    def body(dcum_ref, fl_ref, bf_ref, dfl_ref, dbf_ref):
        triu = _tri(False)
        lane = lax.broadcasted_iota(jnp.int32, (CUM_BLOCK, LANES), 1)
        carry = jnp.zeros((1, LANES), F32)
        total = jnp.zeros((1, LANES), F32)
        for blk in reversed(range(nb)):
            rows = slice(blk * CUM_BLOCK, (blk + 1) * CUM_BLOCK)
            rs = jnp.dot(triu, dcum_ref[rows, :], preferred_element_type=F32, precision=HIGHEST) + carry
            carry = rs[0:1, :]
            _, vjp = jax.vjp(jax.nn.log_sigmoid, fl_ref[rows, :] + bf_ref[...])
            dz = jnp.where(lane < N_HEADS, vjp(rs)[0], 0.0)
            dfl_ref[rows, :] = dz
            total = total + jnp.sum(dz, axis=0, keepdims=True)
        dbf_ref[0] = total

    spec = pl.BlockSpec((seq_len, LANES), lambda b: (b, 0))
    return pl.pallas_call(
        body, name="fprep_bwd", grid=(n_seq,), in_specs=[spec, spec, pl.BlockSpec((1, LANES), lambda b: (0, 0))],
        out_specs=(spec, pl.BlockSpec((1, 1, LANES), lambda b: (b, 0, 0))),
        out_shape=(jax.ShapeDtypeStruct((n, LANES), F32), jax.ShapeDtypeStruct((n_seq, 1, LANES), F32)),
        compiler_params=_cparams("parallel"),
    )(dcum, fl, bf)


ATT_TQ = 256
ATT_KSTEP = 256
ATT_SCALE = HEAD_DIM ** -0.5
NEG_BIG = -1e30


assert ATT_KSTEP == ATT_TQ


def _scores(q_scaled, kb, cq, ck, kend):
    s = _dot_nt(q_scaled, kb) + cq - ck
    r = lax.broadcasted_iota(jnp.int32, (ATT_TQ, ATT_TQ), 0)
    c = lax.broadcasted_iota(jnp.int32, (ATT_TQ, ATT_TQ), 1)
    diag = jnp.where(r >= c, s[:, kend - ATT_TQ:], NEG_BIG)
    return diag if kend == ATT_TQ else jnp.concatenate([s[:, :kend - ATT_TQ], diag], axis=1)


def _attn_specs(n_seq, seq_len):
    nq = seq_len // ATT_TQ
    q_spec = pl.BlockSpec((ATT_TQ, LANES), lambda b, h, q: (b * nq + q, h))
    k_spec = pl.BlockSpec((seq_len, LANES), lambda b, h, q: (b, N_HEADS // 2 + h))
    v_spec = pl.BlockSpec((seq_len, LANES), lambda b, h, q: (b, N_HEADS + h))
    cq_spec = pl.BlockSpec((1, 2, ATT_TQ, 1), lambda b, h, q: (b, h, q, 0))
    ck_spec = pl.BlockSpec((1, 2, 1, seq_len), lambda b, h, q: (b, h, 0, 0))
    return nq, q_spec, k_spec, v_spec, cq_spec, ck_spec


def _head_selectors():
    head0 = lax.broadcasted_iota(jnp.int32, (1, LANES), 1) < HEAD_DIM
    return head0, (head0, jnp.logical_not(head0))


def _for_key_range(qi, seq_len, run):
    per = ATT_KSTEP // ATT_TQ
    for g in range(seq_len // ATT_KSTEP):
        pl.when(qi // per == g)(functools.partial(run, (g + 1) * ATT_KSTEP))


def _attn_fwd(qkv, cq, ck, n_seq):
    n = qkv.shape[0]
    seq_len = n // n_seq
    nq, q_spec, k_spec, v_spec, cq_spec, ck_spec = _attn_specs(n_seq, seq_len)

    def body(q_ref, k_ref, v_ref, cq_ref, ck_ref, o_ref, lse_ref):
        qi = pl.program_id(2)
        q2 = q_ref[...]
        head0, sels = _head_selectors()
        qe = [jnp.where(sel, q2 * ATT_SCALE, 0.0).astype(BF16) for sel in sels]

        def run(kend):
            kb = k_ref[0:kend, :].astype(BF16)
            vb = v_ref[0:kend, :].astype(BF16)
            outs = []
            for e in range(2):
                s = _scores(qe[e], kb, cq_ref[0, e], ck_ref[0, e, :, 0:kend], kend)
                mx = jnp.max(s, axis=1, keepdims=True)
                p = jnp.exp(s - mx)
                den = jnp.sum(p, axis=1, keepdims=True)
                outs.append(_dot(p.astype(BF16), vb) / den)
                lse_ref[0, e] = mx + jnp.log(den)
            o_ref[...] = jnp.where(head0, outs[0], outs[1])

        _for_key_range(qi, seq_len, run)

    return pl.pallas_call(
        body, name="attn_fwd", grid=(n_seq, N_HEADS // 2, nq),
        in_specs=[q_spec, k_spec, v_spec, cq_spec, ck_spec],
        out_specs=(q_spec, cq_spec),
        out_shape=(jax.ShapeDtypeStruct((n, D_ATTN), F32), jax.ShapeDtypeStruct((n_seq, N_HEADS, seq_len, 1), F32)),
        compiler_params=_cparams("parallel", "parallel", "parallel"),
    )(qkv, qkv, qkv, cq, ck)


def _attn_bwd(qkv, cq, ck, o, do, lse, n_seq):
    n = qkv.shape[0]
    seq_len = n // n_seq
    nq, q_spec, k_spec, v_spec, cq_spec, ck_spec = _attn_specs(n_seq, seq_len)
    kv_out = pl.BlockSpec((seq_len, LANES), lambda b, h, q: (b, h))

    def body(q_ref, k_ref, v_ref, cq_ref, ck_ref, o_ref, do_ref, lse_ref, dq_ref, dk_ref, dv_ref, dcq_ref, dck_ref):
        qi = pl.program_id(2)

        @pl.when(qi == 0)
        def _():
            dk_ref[...] = jnp.zeros_like(dk_ref)
            dv_ref[...] = jnp.zeros_like(dv_ref)
            dck_ref[...] = jnp.zeros_like(dck_ref)
        q2 = q_ref[...]
        do2 = do_ref[...]
        o2 = o_ref[...]
        head0, sels = _head_selectors()
        qe = [jnp.where(sel, q2 * ATT_SCALE, 0.0).astype(BF16) for sel in sels]
        doe = [jnp.where(sel, do2, 0.0) for sel in sels]
        doe_b = [d.astype(BF16) for d in doe]
        delta = [jnp.sum(d * o2, axis=1, keepdims=True) for d in doe]

        def run(kend):
            kb = k_ref[0:kend, :].astype(BF16)
            vb = v_ref[0:kend, :].astype(BF16)
            dqs = []
            dk = jnp.zeros((kend, LANES), F32)
            dv = jnp.zeros((kend, LANES), F32)
            for e in range(2):
                s = _scores(qe[e], kb, cq_ref[0, e], ck_ref[0, e, :, 0:kend], kend)
                p = jnp.exp(s - lse_ref[0, e])
                ds = p * (_dot_nt(doe_b[e], vb) - delta[e])
                ds_b = ds.astype(BF16)
                dqs.append(_dot(ds_b, kb))
                dk = dk + _dot_tn(ds_b, qe[e])
                dv = dv + _dot_tn(p.astype(BF16), doe_b[e])
                dcq_ref[0, e] = jnp.sum(ds, axis=1, keepdims=True)
                dck_ref[0, e, :, 0:kend] -= jnp.sum(ds, axis=0, keepdims=True)
            dk_ref[0:kend, :] += dk
            dv_ref[0:kend, :] += dv
            dq_ref[...] = jnp.where(head0, dqs[0], dqs[1]) * ATT_SCALE

        _for_key_range(qi, seq_len, run)

    return pl.pallas_call(
        body, name="attn_bwd", grid=(n_seq, N_HEADS // 2, nq),
        in_specs=[q_spec, k_spec, v_spec, cq_spec, ck_spec, q_spec, q_spec, cq_spec],
        out_specs=(q_spec, kv_out, kv_out, cq_spec, ck_spec),
        out_shape=(jax.ShapeDtypeStruct((n, D_ATTN), F32), jax.ShapeDtypeStruct((n, D_ATTN), F32),
                   jax.ShapeDtypeStruct((n, D_ATTN), F32),
                   jax.ShapeDtypeStruct((n_seq, N_HEADS, seq_len, 1), F32),
                   jax.ShapeDtypeStruct((n_seq, N_HEADS, 1, seq_len), F32)),
        compiler_params=_cparams("parallel", "parallel", "arbitrary"),
    )(qkv, qkv, qkv, cq, ck, o, do, lse)


WEIGHT_NAMES = ("norm_mix", "w_in", "b_forget", "lam_re", "lam_im", "b_re", "b_im", "c_re", "c_im", "d_skip", "log_dt",
                "w_glu", "b_glu", "q_norm", "k_norm", "norm_out_ssm", "norm_out_attn", "w_out", "norm_ffn", "w_up",
                "conv_w", "conv_b", "w_down")
SHARDED = ("w_in", "w_glu", "w_out", "w_up", "conv_w", "w_down")
ADAM_ROWS = {"w_in": 256, "w_glu": 64, "w_out": 128, "w_up": 128, "conv_w": 3, "w_down": 344}
PACK_ROWS = SUBLANES * LANES
ROWS_DOWN = D_FF // N_DEV
ROWS_OUT = D_MODEL // N_DEV
ROWS_GLU = D_SSM * D_SSM // N_DEV // D_MODEL


def _pad_to(a, axis, size):
    pad = [(0, 0)] * a.ndim
    pad[axis] = (0, size - a.shape[axis])
    return jnp.pad(a, pad)


def _block_diag(t, transpose):
    t4 = t.reshape(S5_BLOCKS, 8, SSM_GROUP, SSM_STATE)
    eye = jnp.eye(8, dtype=t.dtype)
    if transpose:
        e = jnp.swapaxes(t4, 2, 3)[:, :, :, None, :] * eye[None, :, None, :, None]
        return e.reshape(S5_BLOCKS, S5_STATES, LANES)
    e = t4[:, :, :, None, :] * eye[None, :, None, :, None]
    return e.reshape(S5_BLOCKS, LANES, S5_STATES)


def _block_diag_extract(m, transpose):
    if transpose:
        m5 = m.reshape(S5_BLOCKS, 8, SSM_STATE, 8, SSM_GROUP)
        d = jnp.stack([m5[:, i, :, i, :] for i in range(8)], axis=1)
        return jnp.swapaxes(d, 2, 3).reshape(N_GROUPS, SSM_GROUP, SSM_STATE)
    m5 = m.reshape(S5_BLOCKS, 8, SSM_GROUP, 8, SSM_STATE)
    d = jnp.stack([m5[:, i, :, i, :] for i in range(8)], axis=1)
    return d.reshape(N_GROUPS, SSM_GROUP, SSM_STATE)


def _pack(pieces):
    flat = jnp.concatenate([p.reshape(-1).astype(F32) for p in pieces])
    size = -(-flat.shape[0] // PACK_ROWS) * PACK_ROWS
    return _pad_to(flat, 0, size).reshape(-1, LANES)


def _unpack(packed, shapes):
    flat = packed.reshape(-1)
    out, off = [], 0
    for shp in shapes:
        size = math.prod(shp)
        out.append(flat[off:off + size].reshape(shp))
        off += size
    return out


def kernel(x, norm_mix, w_in, b_forget, lam_re, lam_im, b_re, b_im, c_re, c_im, d_skip, log_dt, w_glu, b_glu, q_norm, k_norm, norm_out_ssm, norm_out_attn, w_out, norm_ffn, w_up, conv_w, conv_b, w_down, loss_target, m_norm_mix, m_w_in, m_b_forget, m_lam_re, m_lam_im, m_b_re, m_b_im, m_c_re, m_c_im, m_d_skip, m_log_dt, m_w_glu, m_b_glu, m_q_norm, m_k_norm, m_norm_out_ssm, m_norm_out_attn, m_w_out, m_norm_ffn, m_w_up, m_conv_w, m_conv_b, m_w_down, v_norm_mix, v_w_in, v_b_forget, v_lam_re, v_lam_im, v_b_re, v_b_im, v_c_re, v_c_im, v_d_skip, v_log_dt, v_w_glu, v_b_glu, v_q_norm, v_k_norm, v_norm_out_ssm, v_norm_out_attn, v_w_out, v_norm_ffn, v_w_up, v_conv_w, v_conv_b, v_w_down):
    given = dict(locals())
    weights = {k: given[k] for k in WEIGHT_NAMES}
    mom1 = {k: given["m_" + k] for k in WEIGHT_NAMES}
    mom2 = {k: given["v_" + k] for k in WEIGHT_NAMES}
    n_seq, seq_len, _ = x.shape
    n = n_seq * seq_len
    xf = x.reshape(n, D_MODEL)
    target = loss_target.reshape(n, D_MODEL)
    me_idx = 4 * lax.axis_index("x") + 2 * lax.axis_index("y") + lax.axis_index("c")

    in_flags = [False] * 2
    in_sems = _exchange_start([w_in[0].astype(BF16), conv_w[0]], in_flags, norm_mix, "gather_in_start", 3)
    my_slot = lax.broadcasted_iota(jnp.int32, (N_DEV, 1, 1), 0) == me_idx

    lr3 = lam_re[0].reshape(N_GROUPS, 1, SSM_STATE)
    li3 = lam_im[0].reshape(N_GROUPS, 1, SSM_STATE)
    ldt3 = log_dt[0].reshape(N_GROUPS, 1, 1)
    br_t = jnp.swapaxes(b_re[0], 1, 2)
    bi_t = jnp.swapaxes(b_im[0], 1, 2)
    ab_re, ab_im, bb_re, bb_im = _s5_param_fwd(lr3, li3, ldt3, br_t, bi_t)
    a_re = ab_re.reshape(S5_BLOCKS, 1, S5_STATES)
    a_im = ab_im.reshape(S5_BLOCKS, 1, S5_STATES)
    bbr = _block_diag(bb_re, False).astype(BF16)
    bbi = _block_diag(bb_im, False).astype(BF16)
    cr = _block_diag(c_re[0], True).astype(BF16)
    ci = _block_diag(c_im[0], True).astype(BF16)
    avg = jnp.kron(jnp.eye(N_HEADS, dtype=F32), jnp.full((HEAD_DIM, HEAD_DIM), 1.0 / HEAD_DIM, F32)).astype(BF16)
    qg = jnp.tile(q_norm, (1, N_HEADS))
    kg = jnp.tile(k_norm, (1, N_HEADS))
    rows_w = jnp.concatenate([w_down[0], w_out[0], w_glu[0].reshape(ROWS_GLU, D_MODEL)], axis=0).astype(BF16)

    (own_in, own_cw), (g_in, g_cw) = _exchange_wait(in_sems[0], in_sems[1], in_sems[2], in_sems[3], in_flags, ci,
                                                    "gather_in_wait")
    g_in = jnp.where(my_slot, own_in[None], g_in)
    g_cw = jnp.where(my_slot, own_cw[None], g_cw)
    rest_flags = [False] * 2
    w_sems = _exchange_start([rows_w, w_up[0].astype(BF16)], rest_flags, g_in, "gather_rest_start", 0)
    norm_mix = norm_mix + w_sems[4][0, 0]
    w_in_p = _pad_to(jnp.swapaxes(g_in, 0, 1).reshape(D_MODEL, D_IN), 1, D_IN_PAD)

    hn, u, qkv, raw, fl = _inproj_fwd(xf, norm_mix, w_in_p, avg, qg, kg)
    yc, st_r, st_i = _s5_fwd(u, a_re, a_im, bbr, bbi, cr, ci, d_skip, n_seq)
    bf = _pad_to(b_forget, 1, LANES)
    cum = _fprep_fwd(fl, bf, n_seq)
    cum8 = jnp.swapaxes(cum[:, :N_HEADS].reshape(n_seq, seq_len, N_HEADS), 1, 2)
    cq = cum8[:, :, :, None]
    ck = cum8[:, :, None, :]
    ya, lse = _attn_fwd(qkv, cq, ck, n_seq)
    (own_rows, own_up), (g_rows, g_up) = _exchange_wait(w_sems[0], w_sems[1], w_sems[2], w_sems[3], rest_flags, ya,
                                                        "gather_rest_wait")
    g_rows = jnp.where(my_slot, own_rows[None], g_rows)
    g_up = jnp.where(my_slot, own_up[None], g_up)
    g_down = g_rows[:, :ROWS_DOWN]
    g_out = g_rows[:, ROWS_DOWN:ROWS_DOWN + ROWS_OUT]
    g_glu = g_rows[:, ROWS_DOWN + ROWS_OUT:]
    w_glu_f = g_glu.reshape(D_SSM, D_SSM)
    w_out_f = g_out.reshape(D_MODEL, D_MODEL)
    conv_st = _pad_to(jnp.concatenate([g_cw, conv_b.reshape(N_DEV, 1, -1)], axis=1), 1, SUBLANES)
    w_down4 = g_down.reshape(FFN_GROUPS, FFN_GROUP, D_MODEL)
    ys = _glu_fwd(yc, w_glu_f, b_glu)
    h1, hn2, mixed = _mix_fwd(xf, ys, ya, norm_out_ssm, norm_out_attn, w_out_f, norm_ffn)
    ug, uv, pg, pv, dy, loss_part = _ffn_fwd(hn2, h1, target, g_up, conv_st, w_down4, seq_len)
    loss_local = 0.5 * jnp.sum(loss_part) / D_MODEL

    dug, duv, act, dhn2, dcg, dcv = _ffn_bwd(dy, ug, uv, pg, pv, g_up, conv_st, w_down4, seq_len)
    dh1, dys, dya, d_gs, d_ga, d_gf = _mix_bwd(dy, dhn2[None], h1, ys, ya, norm_out_ssm, norm_out_attn, w_out_f, norm_ffn)
    dyc, gl_b, dz_b, d_bglu = _glu_bwd(yc, dys, w_glu_f, b_glu)

    gw_glu = _tn_matmul(gl_b, dz_b, "dw_glu", D_SSM, D_SSM)
    gw_out = _tn_matmul(mixed, dh1, "dw_out", D_MODEL, D_MODEL)
    gw_up = jnp.concatenate([_tn_grouped(hn2, dug, "dw_up_gate", True, BF16),
                             _tn_grouped(hn2, duv, "dw_up_val", True, BF16)], axis=0)
    gw_down = _tn_grouped(act, dy, "dw_down", False)
    g_conv = jnp.concatenate([dcg, dcv], axis=0)
    by_cols = lambda g, c: jnp.swapaxes(g.reshape(g.shape[0], N_DEV, c), 0, 1)
    early_flags = [True] * 2
    rows_g = jnp.concatenate([gw_down.reshape(N_DEV, ROWS_DOWN, D_MODEL), gw_out.reshape(N_DEV, ROWS_OUT, D_MODEL),
                              gw_glu.reshape(N_DEV, ROWS_GLU, D_MODEL)], axis=1).astype(BF16)
    g_sems = _exchange_start([rows_g, gw_up], early_flags, dyc, "grad_early_start", 1)
    started = g_sems[4][0, 0]

    du, dbbr, dbbi, dcr, dci, dar, dai, ddk = _s5_bwd(u, dyc, st_r, st_i, a_re, a_im, bbr, bbi, cr, ci,
                                                      d_skip + started, n_seq)
    partial_early = {
        "ab_re": jnp.sum(dar, axis=1), "ab_im": jnp.sum(dai, axis=1),
        "bb_re": _block_diag_extract(dbbr, False), "bb_im": _block_diag_extract(dbbi, False),
        "c_re": _block_diag_extract(dcr, True), "c_im": _block_diag_extract(dci, True),
        "d_skip": jnp.sum(ddk, axis=1), "b_glu": d_bglu,
        "norm_out_ssm": d_gs, "norm_out_attn": d_ga, "norm_ffn": d_gf, "conv_b": g_conv[:, 3],
    }
    early_keys = tuple(partial_early)
    early_shapes = [partial_early[k].shape for k in early_keys]
    p_sems = _exchange_start([_pack([partial_early[k] for k in early_keys])], [False], du, "small_early_start", 2)
    started = started + p_sems[4][0, 0]

    dqn, dkn, dv, dcq, dck = _attn_bwd(qkv, cq, ck + started, ya, dya, lse, n_seq)
    dcum8 = dcq[:, :, :, 0] + dck.reshape(n_seq, N_HEADS, seq_len)
    dcum = _pad_to(jnp.swapaxes(dcum8, 1, 2).reshape(n, N_HEADS), 1, LANES)
    dfl, dbf = _fprep_bwd(dcum, fl, bf, n_seq)
    dx, dproj, d_gmix, d_qg, d_kg = _inproj_bwd(xf, norm_mix, w_in_p, avg, qg, kg, raw, du, dqn, dkn, dv, dfl, dh1)

    gw_in = _tn_matmul(hn, dproj, "dw_in", D_MODEL, D_IN_PAD, out_cols=D_IN)
    partial_late = {
        "norm_mix": d_gmix, "b_forget": jnp.sum(dbf, axis=(0, 1))[:N_HEADS],
        "q_norm": jnp.sum(d_qg.reshape(N_HEADS, HEAD_DIM), axis=0),
        "k_norm": jnp.sum(d_kg.reshape(N_HEADS, HEAD_DIM), axis=0), "loss": loss_local.reshape(1),
    }
    late_keys = tuple(partial_late)
    late_shapes = [partial_late[k].shape for k in late_keys]

    late_flags = [True, True, False]
    l_sems = _exchange_start(
        [by_cols(gw_in, D_IN // N_DEV).astype(BF16), g_conv[:, :3], _pack([partial_late[k] for k in late_keys])],
        late_flags, dx, "grad_late_start", 4)
    (src_rows, src_up), (land_rows, land_up) = _exchange_wait(g_sems[0], g_sems[1], g_sems[2], g_sems[3], early_flags,
                                                              l_sems[4], "grad_early_wait")
    own_rows = lax.dynamic_index_in_dim(src_rows, me_idx, 0, keepdims=False)
    land = {"w_up": land_up, "w_down": land_rows[:, :ROWS_DOWN], "w_out": land_rows[:, ROWS_DOWN:ROWS_DOWN + ROWS_OUT],
            "w_glu": land_rows[:, ROWS_DOWN + ROWS_OUT:].reshape(N_DEV, -1, D_SSM)}
    own = {"w_up": lax.dynamic_index_in_dim(src_up, me_idx, 0, keepdims=False), "w_down": own_rows[:ROWS_DOWN],
           "w_out": own_rows[ROWS_DOWN:ROWS_DOWN + ROWS_OUT], "w_glu": own_rows[ROWS_DOWN + ROWS_OUT:].reshape(-1, D_SSM)}
    grads, deltas, new_m, new_v = {}, {}, {}, {}

    def adam_shard(name):
        grads[name], deltas[name], new_m[name], new_v[name] = _adam_sharded(
            land[name], own[name], weights[name], mom1[name], mom2[name], "adam_" + name, ADAM_ROWS[name])

    for name in ("w_up", "w_down", "w_out", "w_glu"):
        adam_shard(name)
    (own_pack,), (early_parts,) = _exchange_wait(p_sems[0], p_sems[1], p_sems[2], p_sems[3], [False], land_up,
                                                 "small_early_wait")
    early_sum = _sum_partials(early_parts, own_pack, "sum_early_partials")
    (src_in, src_cw, own_late), (land["w_in"], land["conv_w"], late_parts) = _exchange_wait(
        l_sems[0], l_sems[1], l_sems[2], l_sems[3], late_flags, early_sum, "grad_late_wait")
    own["w_in"] = lax.dynamic_index_in_dim(src_in, me_idx, 0, keepdims=False)
    own["conv_w"] = lax.dynamic_index_in_dim(src_cw, me_idx, 0, keepdims=False)
    for name in ("w_in", "conv_w"):
        adam_shard(name)

    summed = dict(zip(late_keys, _unpack(_sum_partials(late_parts, own_late, "sum_late_partials"), late_shapes)))
    summed.update(zip(early_keys, _unpack(early_sum, early_shapes)))
    dlr, dli, dldt, dbr_t, dbi_t = _s5_param_bwd(
        lr3, li3, ldt3, br_t, bi_t, summed["ab_re"].reshape(lr3.shape), summed["ab_im"].reshape(lr3.shape),
        summed["bb_re"], summed["bb_im"])
    small_grads = {
        "norm_mix": summed["norm_mix"], "b_forget": summed["b_forget"], "lam_re": dlr, "lam_im": dli,
        "b_re": jnp.swapaxes(dbr_t, 1, 2), "b_im": jnp.swapaxes(dbi_t, 1, 2), "c_re": summed["c_re"], "c_im": summed["c_im"],
        "d_skip": summed["d_skip"], "log_dt": dldt, "b_glu": summed["b_glu"], "q_norm": summed["q_norm"],
        "k_norm": summed["k_norm"], "norm_out_ssm": summed["norm_out_ssm"], "norm_out_attn": summed["norm_out_attn"],
        "norm_ffn": summed["norm_ffn"], "conv_b": summed["conv_b"],
    }
    repl = tuple(k for k in WEIGHT_NAMES if k not in SHARDED)
    g_list = [small_grads[k].reshape(weights[k].shape) for k in repl]
    d_list, m_list, v_list = _adam_replicated(g_list, [weights[k] for k in repl], [mom1[k] for k in repl],
                                              [mom2[k] for k in repl], "adam_replicated")
    for k, g, d, nm, nv in zip(repl, g_list, d_list, m_list, v_list):
        grads[k], deltas[k], new_m[k], new_v[k] = g, d, nm, nv

    grad_x = dx.reshape(x.shape)
    loss = summed["loss"].reshape(())
    return (loss, grad_x, *[grads[k] for k in WEIGHT_NAMES], *[deltas[k] for k in WEIGHT_NAMES],
            *[new_m[k] for k in WEIGHT_NAMES], *[new_v[k] for k in WEIGHT_NAMES])
```

```python
import functools
import math

import jax
import jax.numpy as jnp
from jax import lax
from jax.experimental import pallas as pl
from jax.experimental.pallas import tpu as pltpu

F32 = jnp.float32
BF16 = jnp.bfloat16
HIGHEST = lax.Precision.HIGHEST

N_DEV = 8
D_MODEL = 1024
D_SSM = 512
D_ATTN = 512
N_HEADS = 8
HEAD_DIM = 64
N_GROUPS = 32
SSM_GROUP = 16
SSM_STATE = 64
D_FF = 2752
D_FF_PAD = 2816
D_IN = 2056
D_IN_PAD = 2176
EPS = 1e-6
LANES = 128
SUBLANES = 8
VMEM_LIMIT = 56 * 1024 * 1024

ADAM_LR = 0.001
ADAM_B1 = 0.9
ADAM_B2 = 0.999
ADAM_EPS = 1e-08
ADAM_WD = 0.01
ADAM_STEP = 10


def _cparams(*sem):
    return pltpu.CompilerParams(dimension_semantics=sem, vmem_limit_bytes=VMEM_LIMIT)


def _dot(a, b, **kw):
    return jnp.dot(a, b, preferred_element_type=F32, **kw)


def _dot_nt(a, b):
    return lax.dot_general(a, b, (((1,), (1,)), ((), ())), preferred_element_type=F32)


def _dot_tn(a, b):
    return lax.dot_general(a, b, (((0,), (0,)), ((), ())), preferred_element_type=F32)


def _rms(x, g):
    return x * lax.rsqrt(jnp.mean(x * x, axis=-1, keepdims=True) + EPS) * g


def _split_dot(x, avg):
    hi = x.astype(BF16)
    lo = (x - hi.astype(F32)).astype(BF16)
    return _dot(hi, avg) + _dot(lo, avg)


@jax.custom_vjp
def _group_mean(x, avg):
    return _split_dot(x, avg)


def _group_mean_fwd(x, avg):
    return _split_dot(x, avg), avg


def _group_mean_bwd(avg, ct):
    return _split_dot(ct, avg), jnp.zeros_like(avg)


_group_mean.defvjp(_group_mean_fwd, _group_mean_bwd)


def _headnorm(q, avg, g):
    return q * lax.rsqrt(_group_mean(q * q, avg) + EPS) * g


def _exchange(srcs, scatter_flags, name):
    n = len(srcs)
    out_shape = []
    for s, sc in zip(srcs, scatter_flags):
        shp = s.shape if sc else (N_DEV,) + s.shape
        out_shape.append(jax.ShapeDtypeStruct(shp, s.dtype))

    def body(*refs):
        src = refs[:n]
        dst = refs[n:2 * n]
        send_sems, recv_sems, loc_sems = refs[2 * n:]
        x, y, c = lax.axis_index("x"), lax.axis_index("y"), lax.axis_index("c")
        me = 4 * x + 2 * y + c
        peers = []
        for j in range(1, N_DEV):
            px = 1 - x if (j >> 2) & 1 else x
            py = 1 - y if (j >> 1) & 1 else y
            pc = 1 - c if j & 1 else c
            peers.append(((px, py, pc), 4 * px + 2 * py + pc))
        local, sends = [], []
        for k in range(n):
            own = src[k].at[me] if scatter_flags[k] else src[k]
            lc = pltpu.make_async_copy(own, dst[k].at[me], loc_sems.at[k])
            lc.start()
            local.append(lc)
            for j, (pid, pidx) in enumerate(peers):
                s = src[k].at[pidx] if scatter_flags[k] else src[k]
                cp = pltpu.make_async_remote_copy(
                    src_ref=s, dst_ref=dst[k].at[me], send_sem=send_sems.at[k, j], recv_sem=recv_sems.at[k, j],
                    device_id=pid, device_id_type=pl.DeviceIdType.MESH)
                cp.start()
                sends.append(cp)
        for k in range(n):
            for j, (pid, pidx) in enumerate(peers):
                s = src[k].at[pidx] if scatter_flags[k] else src[k]
                pltpu.make_async_remote_copy(
                    src_ref=s, dst_ref=dst[k].at[pidx], send_sem=send_sems.at[k, j], recv_sem=recv_sems.at[k, j],
                    device_id=pid, device_id_type=pl.DeviceIdType.MESH).wait_recv()
        for cp in sends:
            cp.wait_send()
        for lc in local:
            lc.wait()

    any_spec = pl.BlockSpec(memory_space=pl.ANY)
    return pl.pallas_call(
        body, name=name, out_shape=tuple(out_shape),
        in_specs=[any_spec] * n, out_specs=tuple([any_spec] * n),
        scratch_shapes=[pltpu.SemaphoreType.DMA((n, N_DEV - 1)), pltpu.SemaphoreType.DMA((n, N_DEV - 1)),
                        pltpu.SemaphoreType.DMA((n,))],
        compiler_params=pltpu.CompilerParams(has_side_effects=True),
    )(*srcs)


def _peer_list():
    x, y, c = lax.axis_index("x"), lax.axis_index("y"), lax.axis_index("c")
    peers = []
    for j in range(1, N_DEV):
        px = 1 - x if (j >> 2) & 1 else x
        py = 1 - y if (j >> 1) & 1 else y
        pc = 1 - c if j & 1 else c
        peers.append(((px, py, pc), 4 * px + 2 * py + pc))
    return 4 * x + 2 * y + c, peers


def _split_copies(src, land, send_sems, recv_sems, scatter_flags, me, peers, incoming):
    copies = []
    for k in range(len(src)):
        for j, (pid, pidx) in enumerate(peers):
            s = src[k].at[pidx] if scatter_flags[k] else src[k]
            i = k * (N_DEV - 1) + j
            copies.append(pltpu.make_async_remote_copy(
                src_ref=s, dst_ref=land[k].at[pidx if incoming else me], send_sem=send_sems[i],
                recv_sem=recv_sems[i], device_id=pid, device_id_type=pl.DeviceIdType.MESH))
    return copies


def _exchange_start(srcs, scatter_flags, after, name, collective_id):
    n = len(srcs)
    ns = n * (N_DEV - 1)
    hbm = pl.BlockSpec(memory_space=pltpu.HBM)
    sem = pl.BlockSpec(memory_space=pltpu.SEMAPHORE)
    land_shapes = [s.shape if sc else (N_DEV,) + s.shape for s, sc in zip(srcs, scatter_flags)]

    def body(*refs):
        src, land = refs[:n], refs[n:2 * n]
        send_sems = refs[2 * n + 1:2 * n + 1 + ns]
        recv_sems = refs[2 * n + 1 + ns:2 * n + 1 + 2 * ns]
        token = refs[4 * n + 1 + 2 * ns]
        me, peers = _peer_list()
        barrier = pltpu.get_barrier_semaphore()
        for pid, _ in peers:
            pl.semaphore_signal(barrier, inc=1, device_id=pid, device_id_type=pl.DeviceIdType.MESH)
        pl.semaphore_wait(barrier, N_DEV - 1)
        for cp in _split_copies(src, land, send_sems, recv_sems, scatter_flags, me, peers, False):
            cp.start()
        token[...] = jnp.zeros_like(token)

    outs = pl.pallas_call(
        body, name=name,
        out_shape=(*[pltpu.SemaphoreType.DMA(())] * (2 * ns), *[pltpu.HBM(s.shape, s.dtype) for s in srcs],
                   *[pltpu.HBM(shp, s.dtype) for shp, s in zip(land_shapes, srcs)],
                   jax.ShapeDtypeStruct((SUBLANES, LANES), F32)),
        in_specs=[hbm] * (2 * n) + [pl.BlockSpec(memory_space=pl.ANY)],
        out_specs=(*[sem] * (2 * ns), *[hbm] * (2 * n), pl.BlockSpec(memory_space=pltpu.VMEM)),
        input_output_aliases={i: 2 * ns + i for i in range(2 * n)},
        compiler_params=pltpu.CompilerParams(has_side_effects=pltpu.SideEffectType.DATAFLOW_SIDE_EFFECTING,
                                             collective_id=collective_id),
    )(*[pltpu.with_memory_space_constraint(s, pltpu.HBM) for s in srcs],
      *[pltpu.with_memory_space_constraint(lax.empty(shp, s.dtype), pltpu.HBM) for shp, s in zip(land_shapes, srcs)],
      after)
    return (outs[:ns], outs[ns:2 * ns], outs[2 * ns:2 * ns + n], outs[2 * ns + n:2 * ns + 2 * n], outs[2 * ns + 2 * n])


def _exchange_wait(send_sems, recv_sems, srcs, lands, scatter_flags, after, name):
    n = len(srcs)
    ns = n * (N_DEV - 1)
    hbm = pl.BlockSpec(memory_space=pltpu.HBM)
    sem = pl.BlockSpec(memory_space=pltpu.SEMAPHORE)

    def body(*refs):
        src, land = refs[:n], refs[n:2 * n]
        s_sems = refs[2 * n:2 * n + ns]
        r_sems = refs[2 * n + ns:2 * n + 2 * ns]
        me, peers = _peer_list()
        for cp in _split_copies(src, land, s_sems, r_sems, scatter_flags, me, peers, True):
            cp.wait_send()
            cp.wait_recv()

    outs = pl.pallas_call(
        body, name=name,
        out_shape=tuple(pltpu.HBM(a.shape, a.dtype) for a in (*srcs, *lands)),
        in_specs=[hbm] * (2 * n) + [sem] * (2 * ns) + [pl.BlockSpec(memory_space=pl.ANY)],
        out_specs=tuple([hbm] * (2 * n)),
        input_output_aliases={i: i for i in range(2 * n)},
        compiler_params=pltpu.CompilerParams(has_side_effects=pltpu.SideEffectType.DATAFLOW_SIDE_EFFECTING),
    )(*srcs, *lands, *send_sems, *recv_sems, after)
    return outs[:n], outs[n:]


def _tn_matmul(a, b, name, tk, tm, out_rows=None, out_cols=None, tn=512):
    n_tok, k_dim = a.shape
    m_dim = b.shape[1]
    grid = (k_dim // tk, m_dim // tm, n_tok // tn)

    def body(a_ref, b_ref, o_ref):
        @pl.when(pl.program_id(2) == 0)
        def _():
            o_ref[...] = jnp.zeros_like(o_ref)
        o_ref[...] += _dot_tn(a_ref[...].astype(BF16), b_ref[...].astype(BF16))

    return pl.pallas_call(
        body, name=name, grid=grid,
        in_specs=[pl.BlockSpec((tn, tk), lambda i, j, k: (k, i)), pl.BlockSpec((tn, tm), lambda i, j, k: (k, j))],
        out_specs=pl.BlockSpec((tk, tm), lambda i, j, k: (i, j)),
        out_shape=jax.ShapeDtypeStruct((out_rows or k_dim, out_cols or m_dim), F32),
        compiler_params=_cparams("parallel", "parallel", "arbitrary"),
    )(a, b)


def _adam_math(g, w, m, v):
    m = ADAM_B1 * m + (1.0 - ADAM_B1) * g
    v = ADAM_B2 * v + (1.0 - ADAM_B2) * (g * g)
    m_hat = m / (1.0 - ADAM_B1 ** ADAM_STEP)
    v_hat = v / (1.0 - ADAM_B2 ** ADAM_STEP)
    delta = -ADAM_LR * (m_hat / (jnp.sqrt(v_hat) + ADAM_EPS) + ADAM_WD * w)
    return delta, m, v


def _adam_sharded(land, own, w, m, v, name, tr):
    _, r, c = w.shape

    def body(*refs):
        l_ref = refs[0]
        own_ref = refs[1] if own is not None else None
        w_ref, m_ref, v_ref, g_ref, d_ref, nm_ref, nv_ref = [ref.at[0] for ref in refs[-7:]]
        if own_ref is not None:
            x, y, z = lax.axis_index("x"), lax.axis_index("y"), lax.axis_index("c")
            me = 4 * x + 2 * y + z
            mine = own_ref[...].astype(F32)
        g = None
        for s in range(N_DEV):
            part = l_ref[s].astype(F32)
            if own_ref is not None:
                part = jnp.where(me == s, mine, part)
            g = part if g is None else g + part
        d, nm, nv = _adam_math(g, w_ref[...], m_ref[...], v_ref[...])
        g_ref[...] = g
        d_ref[...] = d
        nm_ref[...] = nm
        nv_ref[...] = nv

    spec = pl.BlockSpec((1, tr, c), lambda i: (0, i, 0))
    own_specs, own_args = ([pl.BlockSpec((tr, c), lambda i: (i, 0))], [own]) if own is not None else ([], [])
    return pl.pallas_call(
        body, name=name, grid=(r // tr,),
        in_specs=[pl.BlockSpec((N_DEV, tr, c), lambda i: (0, i, 0)), *own_specs, spec, spec, spec],
        out_specs=(spec, spec, spec, spec),
        out_shape=tuple(jax.ShapeDtypeStruct((1, r, c), F32) for _ in range(4)),
        compiler_params=_cparams("parallel"),
    )(land, *own_args, w, m, v)


def _sum_partials(parts, own, name):
    _, r, c = parts.shape

    def body(*refs):
        p_ref, o_ref = refs[0], refs[-1]
        if own is not None:
            x, y, z = lax.axis_index("x"), lax.axis_index("y"), lax.axis_index("c")
            me = 4 * x + 2 * y + z
            mine = refs[1][...]
        g = None
        for s in range(N_DEV):
            part = p_ref[s]
            if own is not None:
                part = jnp.where(me == s, mine, part)
            g = part if g is None else g + part
        o_ref[...] = g

    args = (parts,) if own is None else (parts, own)
    return pl.pallas_call(body, name=name, out_shape=jax.ShapeDtypeStruct((r, c), F32),
                          compiler_params=pltpu.CompilerParams(vmem_limit_bytes=VMEM_LIMIT))(*args)


def _adam_replicated(gs, ws, ms, vs, name):
    k = len(ws)

    def body(*refs):
        outs = refs[4 * k:]
        for i in range(k):
            d, nm, nv = _adam_math(refs[i][...], refs[k + i][...], refs[2 * k + i][...], refs[3 * k + i][...])
            outs[i][...] = d
            outs[k + i][...] = nm
            outs[2 * k + i][...] = nv

    outs = pl.pallas_call(body, name=name, out_shape=tuple(jax.ShapeDtypeStruct(w.shape, F32) for w in ws) * 3,
                          compiler_params=pltpu.CompilerParams(vmem_limit_bytes=VMEM_LIMIT))(*gs, *ws, *ms, *vs)
    return outs[:k], outs[k:2 * k], outs[2 * k:]


def _inproj_fwd(x, g, w_in, avg, qg, kg, tm=512):
    n = x.shape[0]

    def body(x_ref, g_ref, w_ref, a_ref, qg_ref, kg_ref, hn_ref, u_ref, qkv_ref, raw_ref, fl_ref):
        hn = _rms(x_ref[...], g_ref[...]).astype(BF16)
        hn_ref[...] = hn
        proj = _dot(hn, w_ref[...])
        u_ref[...] = proj[:, 0:512]
        q = proj[:, 512:1024]
        k = proj[:, 1024:1536]
        raw_ref[:, 0:512] = q
        raw_ref[:, 512:1024] = k
        qkv_ref[:, 0:512] = _headnorm(q, a_ref[...], qg_ref[...])
        qkv_ref[:, 512:1024] = _headnorm(k, a_ref[...], kg_ref[...])
        qkv_ref[:, 1024:1536] = proj[:, 1536:2048]
        fl_ref[...] = proj[:, 2048:D_IN_PAD]

    row = lambda w: pl.BlockSpec((tm, w), lambda i: (i, 0))
    full = lambda a: pl.BlockSpec(a.shape, lambda i: (0,) * a.ndim)
    return pl.pallas_call(
        body, name="inproj_fwd", grid=(n // tm,),
        in_specs=[row(D_MODEL), full(g), full(w_in), full(avg), full(qg), full(kg)],
        out_specs=(row(D_MODEL), row(512), row(1536), row(1024), row(LANES)),
        out_shape=(jax.ShapeDtypeStruct((n, D_MODEL), BF16), jax.ShapeDtypeStruct((n, 512), F32),
                   jax.ShapeDtypeStruct((n, 1536), F32), jax.ShapeDtypeStruct((n, 1024), F32),
                   jax.ShapeDtypeStruct((n, LANES), F32)),
        compiler_params=_cparams("parallel"),
    )(x, g, w_in, avg, qg, kg)


def _inproj_bwd(x, g, w_in, avg, qg, kg, raw, du, dqn, dkn, dv, dfl, dres, tm=512):
    n = x.shape[0]

    def body(x_ref, g_ref, w_ref, a_ref, qg_ref, kg_ref, raw_ref, du_ref, dqn_ref, dkn_ref, dv_ref, dfl_ref, dres_ref,
             dx_ref, dproj_ref, dg_ref, dqg_ref, dkg_ref):
        @pl.when(pl.program_id(0) == 0)
        def _():
            dg_ref[...] = jnp.zeros_like(dg_ref)
            dqg_ref[...] = jnp.zeros_like(dqg_ref)
            dkg_ref[...] = jnp.zeros_like(dkg_ref)
        avg_m = a_ref[...]
        _, vjp_q = jax.vjp(lambda q, gg: _headnorm(q, avg_m, gg), raw_ref[:, 0:512], qg_ref[...])
        dq, dqg = vjp_q(dqn_ref[...])
        _, vjp_k = jax.vjp(lambda k, gg: _headnorm(k, avg_m, gg), raw_ref[:, 512:1024], kg_ref[...])
        dk, dkg = vjp_k(dkn_ref[...])
        dproj = jnp.concatenate([du_ref[...], dq, dk, dv_ref[...], dfl_ref[...]], axis=1).astype(BF16)
        dproj_ref[...] = dproj
        dhn = _dot_nt(dproj, w_ref[...])
        _, vjp_x = jax.vjp(_rms, x_ref[...], g_ref[...])
        dxn, dg = vjp_x(dhn)
        dx_ref[...] = dxn + dres_ref[...]
        dg_ref[...] += dg
        dqg_ref[...] += dqg
        dkg_ref[...] += dkg

    row = lambda w: pl.BlockSpec((tm, w), lambda i: (i, 0))
    full = lambda a: pl.BlockSpec(a.shape, lambda i: (0,) * a.ndim)
    vec = lambda w: pl.BlockSpec((1, w), lambda i: (0, 0))
    return pl.pallas_call(
        body, name="inproj_bwd", grid=(n // tm,),
        in_specs=[row(D_MODEL), full(g), full(w_in), full(avg), full(qg), full(kg), row(1024), row(512), row(512),
                  row(512), row(512), row(LANES), row(D_MODEL)],
        out_specs=(row(D_MODEL), row(D_IN_PAD), vec(D_MODEL), vec(512), vec(512)),
        out_shape=(jax.ShapeDtypeStruct((n, D_MODEL), F32), jax.ShapeDtypeStruct((n, D_IN_PAD), BF16),
                   jax.ShapeDtypeStruct((1, D_MODEL), F32), jax.ShapeDtypeStruct((1, 512), F32),
                   jax.ShapeDtypeStruct((1, 512), F32)),
        compiler_params=_cparams("arbitrary"),
    )(x, g, w_in, avg, qg, kg, raw, du, dqn, dkn, dv, dfl, dres)


def _glu_fwd(yc, wg, bg, tm=512):
    n = yc.shape[0]

    def body(yc_ref, w_ref, b_ref, ys_ref):
        gl = jax.nn.gelu(yc_ref[...])
        z = _dot(gl.astype(BF16), w_ref[...]) + b_ref[...]
        ys_ref[...] = gl * jax.nn.sigmoid(z)

    row = pl.BlockSpec((tm, 512), lambda i: (i, 0))
    full = lambda a: pl.BlockSpec(a.shape, lambda i: (0,) * a.ndim)
    return pl.pallas_call(
        body, name="glu_fwd", grid=(n // tm,), in_specs=[row, full(wg), full(bg)], out_specs=row,
        out_shape=jax.ShapeDtypeStruct((n, 512), F32), compiler_params=_cparams("parallel"),
    )(yc, wg, bg)


def _glu_bwd(yc, dys, wg, bg, tm=512):
    n = yc.shape[0]

    def body(yc_ref, dys_ref, w_ref, b_ref, dyc_ref, gl_ref, dz_ref, db_ref):
        @pl.when(pl.program_id(0) == 0)
        def _():
            db_ref[...] = jnp.zeros_like(db_ref)
        gl, vjp_gelu = jax.vjp(jax.nn.gelu, yc_ref[...])
        glb = gl.astype(BF16)
        z = _dot(glb, w_ref[...]) + b_ref[...]
        s = jax.nn.sigmoid(z)
        dys = dys_ref[...]
        dz = dys * gl * s * (1.0 - s)
        dzb = dz.astype(BF16)
        dgl = dys * s + _dot_nt(dzb, w_ref[...])
        dyc_ref[...] = vjp_gelu(dgl)[0]
        gl_ref[...] = glb
        dz_ref[...] = dzb
        db_ref[...] += jnp.sum(dz, axis=0, keepdims=True)

    row = pl.BlockSpec((tm, 512), lambda i: (i, 0))
    full = lambda a: pl.BlockSpec(a.shape, lambda i: (0,) * a.ndim)
    return pl.pallas_call(
        body, name="glu_bwd", grid=(n // tm,), in_specs=[row, row, full(wg), full(bg)],
        out_specs=(row, row, row, pl.BlockSpec((1, 512), lambda i: (0, 0))),
        out_shape=(jax.ShapeDtypeStruct((n, 512), F32), jax.ShapeDtypeStruct((n, 512), BF16),
                   jax.ShapeDtypeStruct((n, 512), BF16), jax.ShapeDtypeStruct((1, 512), F32)),
        compiler_params=_cparams("arbitrary"),
    )(yc, dys, wg, bg)


def _mix_fwd(x, ys, ya, gs, ga, wout, gf, tm=512):
    n = x.shape[0]

    def body(x_ref, ys_ref, ya_ref, gs_ref, ga_ref, w_ref, gf_ref, h1_ref, hn2_ref, mixed_ref):
        mixed = jnp.concatenate([_rms(ys_ref[...], gs_ref[...]), _rms(ya_ref[...], ga_ref[...])], axis=1).astype(BF16)
        mixed_ref[...] = mixed
        h1 = x_ref[...] + _dot(mixed, w_ref[...])
        h1_ref[...] = h1
        hn2_ref[...] = _rms(h1, gf_ref[...]).astype(BF16)

    row = lambda w: pl.BlockSpec((tm, w), lambda i: (i, 0))
    full = lambda a: pl.BlockSpec(a.shape, lambda i: (0,) * a.ndim)
    return pl.pallas_call(
        body, name="mix_fwd", grid=(n // tm,),
        in_specs=[row(D_MODEL), row(512), row(512), full(gs), full(ga), full(wout), full(gf)],
        out_specs=(row(D_MODEL), row(D_MODEL), row(D_MODEL)),
        out_shape=(jax.ShapeDtypeStruct((n, D_MODEL), F32), jax.ShapeDtypeStruct((n, D_MODEL), BF16),
                   jax.ShapeDtypeStruct((n, D_MODEL), BF16)),
        compiler_params=_cparams("parallel"),
    )(x, ys, ya, gs, ga, wout, gf)


def _mix_bwd(dy, dhn2_parts, h1, ys, ya, gs, ga, wout, gf, tm=512):
    n = dy.shape[0]
    n_parts = dhn2_parts.shape[0]

    def body(dy_ref, dp_ref, h1_ref, ys_ref, ya_ref, gs_ref, ga_ref, w_ref, gf_ref,
             dh1_ref, dys_ref, dya_ref, dgs_ref, dga_ref, dgf_ref):
        @pl.when(pl.program_id(0) == 0)
        def _():
            dgs_ref[...] = jnp.zeros_like(dgs_ref)
            dga_ref[...] = jnp.zeros_like(dga_ref)
            dgf_ref[...] = jnp.zeros_like(dgf_ref)
        dhn2 = dp_ref[0]
        for p in range(1, n_parts):
            dhn2 = dhn2 + dp_ref[p]
        _, vjp_f = jax.vjp(_rms, h1_ref[...], gf_ref[...])
        dh1n, dgf = vjp_f(dhn2)
        dh1 = dy_ref[...] + dh1n
        dh1_ref[...] = dh1
        dmixed = _dot_nt(dh1.astype(BF16), w_ref[...])
        _, vjp_s = jax.vjp(_rms, ys_ref[...], gs_ref[...])
        dys, dgs = vjp_s(dmixed[:, 0:512])
        _, vjp_a = jax.vjp(_rms, ya_ref[...], ga_ref[...])
        dya, dga = vjp_a(dmixed[:, 512:1024])
        dys_ref[...] = dys
        dya_ref[...] = dya
        dgs_ref[...] += dgs
        dga_ref[...] += dga
        dgf_ref[...] += dgf

    row = lambda w: pl.BlockSpec((tm, w), lambda i: (i, 0))
    full = lambda a: pl.BlockSpec(a.shape, lambda i: (0,) * a.ndim)
    vec = lambda w: pl.BlockSpec((1, w), lambda i: (0, 0))
    return pl.pallas_call(
        body, name="mix_bwd", grid=(n // tm,),
        in_specs=[row(D_MODEL), pl.BlockSpec((n_parts, tm, D_MODEL), lambda i: (0, i, 0)), row(D_MODEL), row(512),
                  row(512), full(gs), full(ga), full(wout), full(gf)],
        out_specs=(row(D_MODEL), row(512), row(512), vec(512), vec(512), vec(D_MODEL)),
        out_shape=(jax.ShapeDtypeStruct((n, D_MODEL), F32), jax.ShapeDtypeStruct((n, 512), F32),
                   jax.ShapeDtypeStruct((n, 512), F32), jax.ShapeDtypeStruct((1, 512), F32),
                   jax.ShapeDtypeStruct((1, 512), F32), jax.ShapeDtypeStruct((1, D_MODEL), F32)),
        compiler_params=_cparams("arbitrary"),
    )(dy, dhn2_parts, h1, ys, ya, gs, ga, wout, gf)


HALO = 16
FFN_GROUPS = 4
FFN_GROUP = D_FF // FFN_GROUPS


def _conv3(ue, cw):
    return cw[2:3] * ue + cw[1:2] * pltpu.roll(ue, 1, 0) + cw[0:1] * pltpu.roll(ue, 2, 0) + cw[3:4]


def _ffn_weight_specs():
    gate = lambda i, j: (j, 0, 0)
    val = lambda i, j: (j + FFN_GROUPS, 0, 0)
    w_blk, c_blk = (1, D_MODEL, FFN_GROUP), (1, SUBLANES, FFN_GROUP)
    return [pl.BlockSpec(w_blk, gate), pl.BlockSpec(w_blk, val), pl.BlockSpec(c_blk, gate), pl.BlockSpec(c_blk, val),
            pl.BlockSpec((1, FFN_GROUP, D_MODEL), gate)]


def _ffn_fwd(hn2, h1, target, w_up, conv, w_down, seq_len, tm=512):
    n = hn2.shape[0]
    nj = FFN_GROUPS
    hb = tm // HALO

    def body(hn_ref, halo_ref, h1_ref, tgt_ref, wg_ref, wv_ref, cg_ref, cv_ref, wd_ref,
             ug_ref, uv_ref, pg_ref, pv_ref, dy_ref, loss_ref, acc):
        i, j = pl.program_id(0), pl.program_id(1)
        seq_start = (i * tm) % seq_len == 0
        halo = halo_ref[...]
        halo = jnp.where(seq_start, jnp.zeros_like(halo), halo)
        he = jnp.concatenate([halo, hn_ref[...]], axis=0)
        ueg = _dot(he, wg_ref[0])
        uev = _dot(he, wv_ref[0])
        ug_ref[0] = ueg[HALO:].astype(BF16)
        uv_ref[0] = uev[HALO:].astype(BF16)
        cg = _conv3(ueg, cg_ref[0])[HALO:]
        cv = _conv3(uev, cv_ref[0])[HALO:]
        pg_ref[0] = cg.astype(BF16)
        pv_ref[0] = cv.astype(BF16)
        act = (jax.nn.silu(cg) * cv).astype(BF16)
        part = _dot(act, wd_ref[0])

        @pl.when(j == 0)
        def _():
            acc[...] = part

        @pl.when(j > 0)
        def _():
            acc[...] += part

        @pl.when(j == nj - 1)
        def _():
            err = h1_ref[...] + acc[...] - tgt_ref[...]
            dy_ref[...] = err * (1.0 / D_MODEL)
            loss_ref[0] = jnp.sum(err * err, axis=0, keepdims=True)

    row = pl.BlockSpec((tm, D_MODEL), lambda i, j: (i, 0))
    u_main = pl.BlockSpec((1, tm, FFN_GROUP), lambda i, j: (j, i, 0))
    u_shape = jax.ShapeDtypeStruct((FFN_GROUPS, n, FFN_GROUP), BF16)
    return pl.pallas_call(
        body, name="ffn_fwd", grid=(n // tm, nj),
        in_specs=[row, pl.BlockSpec((HALO, D_MODEL), lambda i, j: (jnp.maximum(i * hb - 1, 0), 0)), row, row,
                  *_ffn_weight_specs()],
        out_specs=(u_main, u_main, u_main, u_main, row, pl.BlockSpec((1, 1, D_MODEL), lambda i, j: (i, 0, 0))),
        out_shape=(u_shape, u_shape, u_shape, u_shape, jax.ShapeDtypeStruct((n, D_MODEL), F32),
                   jax.ShapeDtypeStruct((n // tm, 1, D_MODEL), F32)),
        scratch_shapes=[pltpu.VMEM((tm, D_MODEL), F32)],
        compiler_params=_cparams("parallel", "arbitrary"),
    )(hn2, hn2, h1, target, w_up, w_up, conv, conv, w_down)


def _ffn_bwd(dy, ug, uv, pg, pv, w_up, conv, w_down, seq_len, tm=512):
    n = dy.shape[0]
    nj = FFN_GROUPS
    fb = FFN_GROUP
    hb = tm // HALO
    last_hb = n // HALO - 1
    rows = tm + HALO

    def body(dy_ref, dyn_ref, ug_ref, uv_ref, pgm_ref, pgn_ref, pvm_ref, pvn_ref, wg_ref, wv_ref, cg_ref, cv_ref,
             wd_ref, dug_ref, duv_ref, act_ref, dhn_ref, dcg_ref, dcv_ref, acc):
        i, j = pl.program_id(0), pl.program_id(1)
        seq_end = ((i + 1) * tm) % seq_len == 0
        dyn = dyn_ref[...]
        dyn = jnp.where(seq_end, jnp.zeros_like(dyn), dyn)
        d_out = jnp.concatenate([dy_ref[...], dyn], axis=0).astype(BF16)
        d_act = _dot_nt(d_out, wd_ref[0])
        cge = jnp.concatenate([pgm_ref[0], pgn_ref[0]], axis=0).astype(F32)
        cve = jnp.concatenate([pvm_ref[0], pvn_ref[0]], axis=0).astype(F32)
        act, vjp_act = jax.vjp(lambda g, v: jax.nn.silu(g) * v, cge, cve)
        dcge, dcve = vjp_act(d_act)
        act_ref[0] = act[:tm].astype(BF16)

        def conv_t(dc, u_ref, cw):
            ahead1 = pltpu.roll(dc, rows - 1, 0)[:tm]
            ahead2 = pltpu.roll(dc, rows - 2, 0)[:tm]
            here = dc[:tm]
            du = cw[2:3] * here + cw[1:2] * ahead1 + cw[0:1] * ahead2
            u = u_ref[0].astype(F32)
            col = lambda x: jnp.sum(x, axis=0, keepdims=True)
            grad = jnp.concatenate([col(ahead2 * u), col(ahead1 * u), col(here * u), col(here),
                                    jnp.zeros((4, fb), F32)], axis=0)
            return du.astype(BF16), grad

        cwg, cwv = cg_ref[0], cv_ref[0]
        dug, grad_g = conv_t(dcge, ug_ref, cwg)
        duv, grad_v = conv_t(dcve, uv_ref, cwv)
        dug_ref[0] = dug
        duv_ref[0] = duv
        part = _dot_nt(dug, wg_ref[0]) + _dot_nt(duv, wv_ref[0])

        @pl.when(j == 0)
        def _():
            acc[...] = part

        @pl.when(j > 0)
        def _():
            acc[...] += part

        @pl.when(j == nj - 1)
        def _():
            dhn_ref[...] = acc[...]

        @pl.when(i == 0)
        def _():
            dcg_ref[j] = jnp.zeros((8, fb), F32)
            dcv_ref[j] = jnp.zeros((8, fb), F32)

        dcg_ref[j] += grad_g
        dcv_ref[j] += grad_v

    row = pl.BlockSpec((tm, D_MODEL), lambda i, j: (i, 0))
    u_main = pl.BlockSpec((1, tm, fb), lambda i, j: (j, i, 0))
    u_next = pl.BlockSpec((1, HALO, fb), lambda i, j: (j, jnp.minimum((i + 1) * hb, last_hb), 0))
    dc_spec = pl.BlockSpec((nj, 8, fb), lambda i, j: (0, 0, 0))
    u_shape = jax.ShapeDtypeStruct((FFN_GROUPS, n, fb), BF16)
    return pl.pallas_call(
        body, name="ffn_bwd", grid=(n // tm, nj),
        in_specs=[row, pl.BlockSpec((HALO, D_MODEL), lambda i, j: (jnp.minimum((i + 1) * hb, last_hb), 0)),
                  u_main, u_main, u_main, u_next, u_main, u_next, *_ffn_weight_specs()],
        out_specs=(u_main, u_main, u_main, row, dc_spec, dc_spec),
        out_shape=(u_shape, u_shape, u_shape, jax.ShapeDtypeStruct((n, D_MODEL), F32),
                   jax.ShapeDtypeStruct((nj, 8, fb), F32), jax.ShapeDtypeStruct((nj, 8, fb), F32)),
        scratch_shapes=[pltpu.VMEM((tm, D_MODEL), F32)],
        compiler_params=_cparams("arbitrary", "arbitrary"),
    )(dy, dy, ug, uv, pg, pg, pv, pv, w_up, w_up, conv, conv, w_down)


def _tn_grouped(a, b, name, shared_a, out_dtype=F32, tn=1024):
    groups = b.shape[0] if shared_a else a.shape[0]
    n_tok = a.shape[0] if shared_a else b.shape[0]
    k_dim, m_dim = a.shape[-1], b.shape[-1]

    def body(a_ref, b_ref, o_ref, acc):
        k = pl.program_id(1)
        a_t = a_ref[...] if shared_a else a_ref[0]
        b_t = b_ref[0] if shared_a else b_ref[...]
        part = _dot_tn(a_t.astype(BF16), b_t.astype(BF16))

        @pl.when(k == 0)
        def _():
            acc[...] = part

        @pl.when(k > 0)
        def _():
            acc[...] += part

        @pl.when(k == n_tok // tn - 1)
        def _():
            o_ref[0] = acc[...].astype(out_dtype)

    plain = lambda w: pl.BlockSpec((tn, w), lambda g, k: (k, 0))
    grouped = lambda w: pl.BlockSpec((1, tn, w), lambda g, k: (g, k, 0))
    return pl.pallas_call(
        body, name=name, grid=(groups, n_tok // tn),
        in_specs=[plain(k_dim), grouped(m_dim)] if shared_a else [grouped(k_dim), plain(m_dim)],
        out_specs=pl.BlockSpec((1, k_dim, m_dim), lambda g, k: (g, 0, 0)),
        out_shape=jax.ShapeDtypeStruct((groups, k_dim, m_dim), out_dtype),
        scratch_shapes=[pltpu.VMEM((k_dim, m_dim), F32)],
        compiler_params=_cparams("parallel", "arbitrary"),
    )(a, b)


def _s5_param_fn(lr, li, ldt, br, bi):
    dt = jnp.exp(ldt)
    mag = jnp.exp(lr * dt)
    ab_re = mag * jnp.cos(li * dt)
    ab_im = mag * jnp.sin(li * dt)
    nr = ab_re - 1.0
    ni = ab_im
    den = lr * lr + li * li
    q_re = (nr * lr + ni * li) / den
    q_im = (ni * lr - nr * li) / den
    bb_re = q_re * br - q_im * bi
    bb_im = q_re * bi + q_im * br
    return ab_re, ab_im, bb_re, bb_im


def _s5_param_fwd(lr, li, ldt, br, bi):
    def body(lr_ref, li_ref, ldt_ref, br_ref, bi_ref, ar_ref, ai_ref, bbr_ref, bbi_ref):
        ar, ai, bbr, bbi = _s5_param_fn(lr_ref[...], li_ref[...], ldt_ref[...], br_ref[...], bi_ref[...])
        ar_ref[...] = ar
        ai_ref[...] = ai
        bbr_ref[...] = bbr
        bbi_ref[...] = bbi

    return pl.pallas_call(
        body, name="s5_param_fwd",
        out_shape=(jax.ShapeDtypeStruct(lr.shape, F32), jax.ShapeDtypeStruct(lr.shape, F32),
                   jax.ShapeDtypeStruct(br.shape, F32), jax.ShapeDtypeStruct(br.shape, F32)),
    )(lr, li, ldt, br, bi)


def _s5_param_bwd(lr, li, ldt, br, bi, dar, dai, dbbr, dbbi):
    def body(lr_ref, li_ref, ldt_ref, br_ref, bi_ref, dar_ref, dai_ref, dbbr_ref, dbbi_ref,
             dlr_ref, dli_ref, dldt_ref, dbr_ref, dbi_ref):
        _, vjp = jax.vjp(_s5_param_fn, lr_ref[...], li_ref[...], ldt_ref[...], br_ref[...], bi_ref[...])
        dlr, dli, dldt, dbr, dbi = vjp((dar_ref[...], dai_ref[...], dbbr_ref[...], dbbi_ref[...]))
        dlr_ref[...] = dlr
        dli_ref[...] = dli
        dldt_ref[...] = dldt
        dbr_ref[...] = dbr
        dbi_ref[...] = dbi

    return pl.pallas_call(
        body, name="s5_param_bwd",
        out_shape=(jax.ShapeDtypeStruct(lr.shape, F32), jax.ShapeDtypeStruct(lr.shape, F32),
                   jax.ShapeDtypeStruct(ldt.shape, F32), jax.ShapeDtypeStruct(br.shape, F32),
                   jax.ShapeDtypeStruct(br.shape, F32)),
    )(lr, li, ldt, br, bi, dar, dai, dbbr, dbbi)


S5_CHUNK = 256
S5_STATES = 512
S5_BLOCKS = 4


def _cpow_rows(ar, ai, count):
    rs, im = [ar], [ai]
    for _ in range(count - 1):
        pr, pi = rs[-1], im[-1]
        rs.append(pr * ar - pi * ai)
        im.append(pr * ai + pi * ar)
    return rs, im


def _scan_in_groups(vr, vi, pr, pi, rm, reverse):
    n, width = vr.shape
    vr = vr.reshape(n // SUBLANES, SUBLANES, width)
    vi = vi.reshape(n // SUBLANES, SUBLANES, width)
    row = rm[0:SUBLANES]
    for k in (1, 2, 4):
        shift = SUBLANES - k if reverse else k
        keep = row < SUBLANES - k if reverse else row >= k
        kr = jnp.where(keep, pr[k - 1], 0.0)
        ki = jnp.where(keep, pi[k - 1], 0.0)
        sr, si = pltpu.roll(vr, shift, 1), pltpu.roll(vi, shift, 1)
        vr, vi = vr + kr * sr - ki * si, vi + kr * si + ki * sr
    return vr.reshape(n, width), vi.reshape(n, width)


def _carry_over_groups(xr_s, xi_s, wr, wi, c0r, c0i, reverse):
    groups = xr_s.shape[0] // SUBLANES
    pick = 0 if reverse else SUBLANES - 1

    def step(q, carry):
        cr, ci = carry
        r = groups - 1 - q if reverse else q
        o = pl.multiple_of(r * SUBLANES, SUBLANES)
        vr = xr_s[pl.ds(o, SUBLANES), :]
        vi = xi_s[pl.ds(o, SUBLANES), :]
        nr = vr + wr * cr - wi * ci
        ni = vi + wr * ci + wi * cr
        xr_s[pl.ds(o, SUBLANES), :] = nr
        xi_s[pl.ds(o, SUBLANES), :] = ni
        return (jnp.broadcast_to(nr[pick:pick + 1], nr.shape), jnp.broadcast_to(ni[pick:pick + 1], ni.shape))

    return lax.fori_loop(0, groups, step, (c0r, c0i))


def _s5_state_scan(u_b, bbr, bbi, pr, pi, rm, xr_s, xi_s, c0r, c0i):
    bur = _dot(u_b, bbr)
    bui = _dot(u_b, bbi)
    bur, bui = _scan_in_groups(bur, bui, pr, pi, rm, False)
    xr_s[...] = bur
    xi_s[...] = bui
    w8r = jnp.concatenate(pr, axis=0)
    w8i = jnp.concatenate(pi, axis=0)
    return _carry_over_groups(xr_s, xi_s, w8r, w8i, c0r, c0i, False)


def _s5_fwd(u, a_re, a_im, bbr, bbi, cr, ci, d_skip, n_seq):
    n = u.shape[0]
    seq_len = n // n_seq
    nt = seq_len // S5_CHUNK
    tc = S5_CHUNK

    def body(u_ref, ar_ref, ai_ref, bbr_ref, bbi_ref, cr_ref, ci_ref, d_ref, y_ref, str_ref, sti_ref,
             xr_s, xi_s, car_r, car_i):
        t = pl.program_id(2)

        @pl.when(t == 0)
        def _():
            car_r[...] = jnp.zeros_like(car_r)
            car_i[...] = jnp.zeros_like(car_i)
        pr, pi = _cpow_rows(ar_ref[0], ai_ref[0], SUBLANES)
        rm = lax.broadcasted_iota(jnp.int32, (tc, S5_STATES), 0) & (SUBLANES - 1)
        str_ref[0, 0] = car_r[...]
        sti_ref[0, 0] = car_i[...]
        u_t = u_ref[...]
        cfr, cfi = _s5_state_scan(u_t.astype(BF16), bbr_ref[0], bbi_ref[0], pr, pi, rm, xr_s, xi_s,
                                  car_r[...], car_i[...])
        car_r[...] = cfr
        car_i[...] = cfi
        y = _dot(xr_s[...].astype(BF16), cr_ref[0]) - _dot(xi_s[...].astype(BF16), ci_ref[0])
        y_ref[...] = y + d_ref[...] * u_t

    u_spec = pl.BlockSpec((tc, LANES), lambda cb, b, t: (b * nt + t, cb))
    a_spec = pl.BlockSpec((1, 1, S5_STATES), lambda cb, b, t: (cb, 0, 0))
    bb_spec = pl.BlockSpec((1, LANES, S5_STATES), lambda cb, b, t: (cb, 0, 0))
    c_spec = pl.BlockSpec((1, S5_STATES, LANES), lambda cb, b, t: (cb, 0, 0))
    st_spec = pl.BlockSpec((1, 1, SUBLANES, S5_STATES), lambda cb, b, t: (cb, b * nt + t, 0, 0))
    st_shape = jax.ShapeDtypeStruct((S5_BLOCKS, n_seq * nt, SUBLANES, S5_STATES), F32)
    return pl.pallas_call(
        body, name="s5_fwd", grid=(S5_BLOCKS, n_seq, nt),
        in_specs=[u_spec, a_spec, a_spec, bb_spec, bb_spec, c_spec, c_spec,
                  pl.BlockSpec((1, LANES), lambda cb, b, t: (0, cb))],
        out_specs=(u_spec, st_spec, st_spec),
        out_shape=(jax.ShapeDtypeStruct((n, D_SSM), F32), st_shape, st_shape),
        scratch_shapes=[pltpu.VMEM((tc, S5_STATES), F32), pltpu.VMEM((tc, S5_STATES), F32),
                        pltpu.VMEM((SUBLANES, S5_STATES), F32), pltpu.VMEM((SUBLANES, S5_STATES), F32)],
        compiler_params=_cparams("parallel", "arbitrary", "arbitrary"),
    )(u, a_re, a_im, bbr, bbi, cr, ci, d_skip)


def _s5_bwd(u, dy, st_r, st_i, a_re, a_im, bbr, bbi, cr, ci, d_skip, n_seq):
    n = u.shape[0]
    seq_len = n // n_seq
    nt = seq_len // S5_CHUNK
    tc = S5_CHUNK

    def body(u_ref, dy_ref, str_ref, sti_ref, ar_ref, ai_ref, bbr_ref, bbi_ref, cr_ref, ci_ref, d_ref,
             du_ref, dbbr_ref, dbbi_ref, dcr_ref, dci_ref, dar_ref, dai_ref, dd_ref,
             xr_s, xi_s, gr_s, gi_s, car_r, car_i):
        b, t = pl.program_id(1), pl.program_id(2)

        @pl.when((b == 0) & (t == 0))
        def _():
            for ref in (dbbr_ref, dbbi_ref, dcr_ref, dci_ref, dar_ref, dai_ref, dd_ref):
                ref[...] = jnp.zeros_like(ref)

        @pl.when(t == 0)
        def _():
            car_r[...] = jnp.zeros_like(car_r)
            car_i[...] = jnp.zeros_like(car_i)
        ar, ai = ar_ref[0], ai_ref[0]
        pr, pi = _cpow_rows(ar, ai, SUBLANES)
        row = lax.broadcasted_iota(jnp.int32, (tc, S5_STATES), 0)
        rm = row & (SUBLANES - 1)
        u_t = u_ref[...]
        u_b = u_t.astype(BF16)
        dy_t = dy_ref[...]
        dy_b = dy_t.astype(BF16)
        s0r, s0i = str_ref[0, 0], sti_ref[0, 0]
        _s5_state_scan(u_b, bbr_ref[0], bbi_ref[0], pr, pi, rm, xr_s, xi_s, s0r, s0i)
        xr, xi = xr_s[...], xi_s[...]
        gr = _dot_nt(dy_b, cr_ref[0])
        gi = -_dot_nt(dy_b, ci_ref[0])
        npi = [-v for v in pi]
        gr, gi = _scan_in_groups(gr, gi, pr, npi, rm, True)
        gr_s[...] = gr
        gi_s[...] = gi
        w8r = jnp.concatenate(pr[::-1], axis=0)
        w8i = jnp.concatenate(npi[::-1], axis=0)
        cfr, cfi = _carry_over_groups(gr_s, gi_s, w8r, w8i, car_r[...], car_i[...], True)
        car_r[...] = cfr
        car_i[...] = cfi
        gr, gi = gr_s[...], gi_s[...]
        gr_b, gi_b = gr.astype(BF16), gi.astype(BF16)
        du_ref[...] = _dot_nt(gr_b, bbr_ref[0]) + _dot_nt(gi_b, bbi_ref[0]) + d_ref[...] * dy_t
        dbbr_ref[0] += _dot_tn(u_b, gr_b)
        dbbi_ref[0] += _dot_tn(u_b, gi_b)
        dcr_ref[0] += _dot_tn(xr.astype(BF16), dy_b)
        dci_ref[0] -= _dot_tn(xi.astype(BF16), dy_b)
        dd_ref[0] += jnp.sum((dy_t * u_t).reshape(tc // SUBLANES, SUBLANES, LANES), axis=0)
        first = row == 0
        xpr = jnp.where(first, jnp.broadcast_to(s0r[0:1], xr.shape), pltpu.roll(xr, 1, 0))
        xpi = jnp.where(first, jnp.broadcast_to(s0i[0:1], xi.shape), pltpu.roll(xi, 1, 0))
        shp = (tc // SUBLANES, SUBLANES, S5_STATES)
        dar_ref[0] += jnp.sum((gr * xpr + gi * xpi).reshape(shp), axis=0)
        dai_ref[0] += jnp.sum((gi * xpr - gr * xpi).reshape(shp), axis=0)

    u_spec = pl.BlockSpec((tc, LANES), lambda cb, b, t: (b * nt + nt - 1 - t, cb))
    a_spec = pl.BlockSpec((1, 1, S5_STATES), lambda cb, b, t: (cb, 0, 0))
    bb_spec = pl.BlockSpec((1, LANES, S5_STATES), lambda cb, b, t: (cb, 0, 0))
    c_spec = pl.BlockSpec((1, S5_STATES, LANES), lambda cb, b, t: (cb, 0, 0))
    st_spec = pl.BlockSpec((1, 1, SUBLANES, S5_STATES), lambda cb, b, t: (cb, b * nt + nt - 1 - t, 0, 0))
    da_spec = pl.BlockSpec((1, SUBLANES, S5_STATES), lambda cb, b, t: (cb, 0, 0))
    dd_spec = pl.BlockSpec((1, SUBLANES, LANES), lambda cb, b, t: (cb, 0, 0))
    big = pltpu.VMEM((tc, S5_STATES), F32)
    small = pltpu.VMEM((SUBLANES, S5_STATES), F32)
    return pl.pallas_call(
        body, name="s5_bwd", grid=(S5_BLOCKS, n_seq, nt),
        in_specs=[u_spec, u_spec, st_spec, st_spec, a_spec, a_spec, bb_spec, bb_spec, c_spec, c_spec,
                  pl.BlockSpec((1, LANES), lambda cb, b, t: (0, cb))],
        out_specs=(u_spec, bb_spec, bb_spec, c_spec, c_spec, da_spec, da_spec, dd_spec),
        out_shape=(jax.ShapeDtypeStruct((n, D_SSM), F32),
                   jax.ShapeDtypeStruct((S5_BLOCKS, LANES, S5_STATES), F32),
                   jax.ShapeDtypeStruct((S5_BLOCKS, LANES, S5_STATES), F32),
                   jax.ShapeDtypeStruct((S5_BLOCKS, S5_STATES, LANES), F32),
                   jax.ShapeDtypeStruct((S5_BLOCKS, S5_STATES, LANES), F32),
                   jax.ShapeDtypeStruct((S5_BLOCKS, SUBLANES, S5_STATES), F32),
                   jax.ShapeDtypeStruct((S5_BLOCKS, SUBLANES, S5_STATES), F32),
                   jax.ShapeDtypeStruct((S5_BLOCKS, SUBLANES, LANES), F32)),
        scratch_shapes=[big, big, big, big, small, small],
        compiler_params=_cparams("parallel", "arbitrary", "arbitrary"),
    )(u, dy, st_r, st_i, a_re, a_im, bbr, bbi, cr, ci, d_skip)


CUM_BLOCK = 128


def _tri(lower):
    r = lax.broadcasted_iota(jnp.int32, (CUM_BLOCK, CUM_BLOCK), 0)
    c = lax.broadcasted_iota(jnp.int32, (CUM_BLOCK, CUM_BLOCK), 1)
    return jnp.where(r >= c if lower else r <= c, 1.0, 0.0).astype(F32)


def _fprep_fwd(fl, bf, n_seq):
    n = fl.shape[0]
    seq_len = n // n_seq
    nb = seq_len // CUM_BLOCK

    def body(fl_ref, bf_ref, cum_ref):
        tril = _tri(True)
        carry = jnp.zeros((1, LANES), F32)
        for blk in range(nb):
            rows = slice(blk * CUM_BLOCK, (blk + 1) * CUM_BLOCK)
            lf = jax.nn.log_sigmoid(fl_ref[rows, :] + bf_ref[...])
            cs = jnp.dot(tril, lf, preferred_element_type=F32, precision=HIGHEST) + carry
            cum_ref[rows, :] = cs
            carry = cs[CUM_BLOCK - 1:CUM_BLOCK, :]

    spec = pl.BlockSpec((seq_len, LANES), lambda b: (b, 0))
    return pl.pallas_call(
        body, name="fprep_fwd", grid=(n_seq,), in_specs=[spec, pl.BlockSpec((1, LANES), lambda b: (0, 0))],
        out_specs=spec, out_shape=jax.ShapeDtypeStruct((n, LANES), F32), compiler_params=_cparams("parallel"),
    )(fl, bf)


def _fprep_bwd(dcum, fl, bf, n_seq):
    n = fl.shape[0]
    seq_len = n // n_seq
    nb = seq_len // CUM_BLOCK

    def body(dcum_ref, fl_ref, bf_ref, dfl_ref, dbf_ref):
        triu = _tri(False)
        lane = lax.broadcasted_iota(jnp.int32, (CUM_BLOCK, LANES), 1)
        carry = jnp.zeros((1, LANES), F32)
        total = jnp.zeros((1, LANES), F32)
        for blk in reversed(range(nb)):
            rows = slice(blk * CUM_BLOCK, (blk + 1) * CUM_BLOCK)
            rs = jnp.dot(triu, dcum_ref[rows, :], preferred_element_type=F32, precision=HIGHEST) + carry
            carry = rs[0:1, :]
            _, vjp = jax.vjp(jax.nn.log_sigmoid, fl_ref[rows, :] + bf_ref[...])
            dz = jnp.where(lane < N_HEADS, vjp(rs)[0], 0.0)
            dfl_ref[rows, :] = dz
            total = total + jnp.sum(dz, axis=0, keepdims=True)
        dbf_ref[0] = total

    spec = pl.BlockSpec((seq_len, LANES), lambda b: (b, 0))
    return pl.pallas_call(
        body, name="fprep_bwd", grid=(n_seq,), in_specs=[spec, spec, pl.BlockSpec((1, LANES), lambda b: (0, 0))],
        out_specs=(spec, pl.BlockSpec((1, 1, LANES), lambda b: (b, 0, 0))),
        out_shape=(jax.ShapeDtypeStruct((n, LANES), F32), jax.ShapeDtypeStruct((n_seq, 1, LANES), F32)),
        compiler_params=_cparams("parallel"),
    )(dcum, fl, bf)


ATT_TQ = 256
ATT_KSTEP = 256
ATT_SCALE = HEAD_DIM ** -0.5
NEG_BIG = -1e30


assert ATT_KSTEP == ATT_TQ


def _scores(q_scaled, kb, cq, ck, kend):
    s = _dot_nt(q_scaled, kb) + cq - ck
    r = lax.broadcasted_iota(jnp.int32, (ATT_TQ, ATT_TQ), 0)
    c = lax.broadcasted_iota(jnp.int32, (ATT_TQ, ATT_TQ), 1)
    diag = jnp.where(r >= c, s[:, kend - ATT_TQ:], NEG_BIG)
    return diag if kend == ATT_TQ else jnp.concatenate([s[:, :kend - ATT_TQ], diag], axis=1)


def _attn_specs(n_seq, seq_len):
    nq = seq_len // ATT_TQ
    q_spec = pl.BlockSpec((ATT_TQ, LANES), lambda b, h, q: (b * nq + q, h))
    k_spec = pl.BlockSpec((seq_len, LANES), lambda b, h, q: (b, N_HEADS // 2 + h))
    v_spec = pl.BlockSpec((seq_len, LANES), lambda b, h, q: (b, N_HEADS + h))
    cq_spec = pl.BlockSpec((1, 2, ATT_TQ, 1), lambda b, h, q: (b, h, q, 0))
    ck_spec = pl.BlockSpec((1, 2, 1, seq_len), lambda b, h, q: (b, h, 0, 0))
    return nq, q_spec, k_spec, v_spec, cq_spec, ck_spec


def _head_selectors():
    head0 = lax.broadcasted_iota(jnp.int32, (1, LANES), 1) < HEAD_DIM
    return head0, (head0, jnp.logical_not(head0))


def _for_key_range(qi, seq_len, run):
    per = ATT_KSTEP // ATT_TQ
    for g in range(seq_len // ATT_KSTEP):
        pl.when(qi // per == g)(functools.partial(run, (g + 1) * ATT_KSTEP))


def _attn_fwd(qkv, cq, ck, n_seq):
    n = qkv.shape[0]
    seq_len = n // n_seq
    nq, q_spec, k_spec, v_spec, cq_spec, ck_spec = _attn_specs(n_seq, seq_len)

    def body(q_ref, k_ref, v_ref, cq_ref, ck_ref, o_ref, lse_ref):
        qi = pl.program_id(2)
        q2 = q_ref[...]
        head0, sels = _head_selectors()
        qe = [jnp.where(sel, q2 * ATT_SCALE, 0.0).astype(BF16) for sel in sels]

        def run(kend):
            kb = k_ref[0:kend, :].astype(BF16)
            vb = v_ref[0:kend, :].astype(BF16)
            outs = []
            for e in range(2):
                s = _scores(qe[e], kb, cq_ref[0, e], ck_ref[0, e, :, 0:kend], kend)
                mx = jnp.max(s, axis=1, keepdims=True)
                p = jnp.exp(s - mx)
                den = jnp.sum(p, axis=1, keepdims=True)
                outs.append(_dot(p.astype(BF16), vb) / den)
                lse_ref[0, e] = mx + jnp.log(den)
            o_ref[...] = jnp.where(head0, outs[0], outs[1])

        _for_key_range(qi, seq_len, run)

    return pl.pallas_call(
        body, name="attn_fwd", grid=(n_seq, N_HEADS // 2, nq),
        in_specs=[q_spec, k_spec, v_spec, cq_spec, ck_spec],
        out_specs=(q_spec, cq_spec),
        out_shape=(jax.ShapeDtypeStruct((n, D_ATTN), F32), jax.ShapeDtypeStruct((n_seq, N_HEADS, seq_len, 1), F32)),
        compiler_params=_cparams("parallel", "parallel", "parallel"),
    )(qkv, qkv, qkv, cq, ck)


def _attn_bwd(qkv, cq, ck, o, do, lse, n_seq):
    n = qkv.shape[0]
    seq_len = n // n_seq
    nq, q_spec, k_spec, v_spec, cq_spec, ck_spec = _attn_specs(n_seq, seq_len)
    kv_out = pl.BlockSpec((seq_len, LANES), lambda b, h, q: (b, h))

    def body(q_ref, k_ref, v_ref, cq_ref, ck_ref, o_ref, do_ref, lse_ref, dq_ref, dk_ref, dv_ref, dcq_ref, dck_ref):
        qi = pl.program_id(2)

        @pl.when(qi == 0)
        def _():
            dk_ref[...] = jnp.zeros_like(dk_ref)
            dv_ref[...] = jnp.zeros_like(dv_ref)
            dck_ref[...] = jnp.zeros_like(dck_ref)
        q2 = q_ref[...]
        do2 = do_ref[...]
        o2 = o_ref[...]
        head0, sels = _head_selectors()
        qe = [jnp.where(sel, q2 * ATT_SCALE, 0.0).astype(BF16) for sel in sels]
        doe = [jnp.where(sel, do2, 0.0) for sel in sels]
        doe_b = [d.astype(BF16) for d in doe]
        delta = [jnp.sum(d * o2, axis=1, keepdims=True) for d in doe]

        def run(kend):
            kb = k_ref[0:kend, :].astype(BF16)
            vb = v_ref[0:kend, :].astype(BF16)
            dqs = []
            dk = jnp.zeros((kend, LANES), F32)
            dv = jnp.zeros((kend, LANES), F32)
            for e in range(2):
                s = _scores(qe[e], kb, cq_ref[0, e], ck_ref[0, e, :, 0:kend], kend)
                p = jnp.exp(s - lse_ref[0, e])
                ds = p * (_dot_nt(doe_b[e], vb) - delta[e])
                ds_b = ds.astype(BF16)
                dqs.append(_dot(ds_b, kb))
                dk = dk + _dot_tn(ds_b, qe[e])
                dv = dv + _dot_tn(p.astype(BF16), doe_b[e])
                dcq_ref[0, e] = jnp.sum(ds, axis=1, keepdims=True)
                dck_ref[0, e, :, 0:kend] -= jnp.sum(ds, axis=0, keepdims=True)
            dk_ref[0:kend, :] += dk
            dv_ref[0:kend, :] += dv
            dq_ref[...] = jnp.where(head0, dqs[0], dqs[1]) * ATT_SCALE

        _for_key_range(qi, seq_len, run)

    return pl.pallas_call(
        body, name="attn_bwd", grid=(n_seq, N_HEADS // 2, nq),
        in_specs=[q_spec, k_spec, v_spec, cq_spec, ck_spec, q_spec, q_spec, cq_spec],
        out_specs=(q_spec, kv_out, kv_out, cq_spec, ck_spec),
        out_shape=(jax.ShapeDtypeStruct((n, D_ATTN), F32), jax.ShapeDtypeStruct((n, D_ATTN), F32),
                   jax.ShapeDtypeStruct((n, D_ATTN), F32),
                   jax.ShapeDtypeStruct((n_seq, N_HEADS, seq_len, 1), F32),
                   jax.ShapeDtypeStruct((n_seq, N_HEADS, 1, seq_len), F32)),
        compiler_params=_cparams("parallel", "parallel", "arbitrary"),
    )(qkv, qkv, qkv, cq, ck, o, do, lse)


WEIGHT_NAMES = ("norm_mix", "w_in", "b_forget", "lam_re", "lam_im", "b_re", "b_im", "c_re", "c_im", "d_skip", "log_dt",
                "w_glu", "b_glu", "q_norm", "k_norm", "norm_out_ssm", "norm_out_attn", "w_out", "norm_ffn", "w_up",
                "conv_w", "conv_b", "w_down")
SHARDED = ("w_in", "w_glu", "w_out", "w_up", "conv_w", "w_down")
ADAM_ROWS = {"w_in": 256, "w_glu": 64, "w_out": 128, "w_up": 128, "conv_w": 3, "w_down": 344}
PACK_ROWS = SUBLANES * LANES
ROWS_DOWN = D_FF // N_DEV
ROWS_OUT = D_MODEL // N_DEV
ROWS_GLU = D_SSM * D_SSM // N_DEV // D_MODEL


def _after_all(*arrays):
    return sum(a.reshape(-1)[0].astype(F32) for a in arrays).reshape(1, 1)


def _pad_to(a, axis, size):
    pad = [(0, 0)] * a.ndim
    pad[axis] = (0, size - a.shape[axis])
    return jnp.pad(a, pad)


def _block_diag(t, transpose):
    t4 = t.reshape(S5_BLOCKS, 8, SSM_GROUP, SSM_STATE)
    eye = jnp.eye(8, dtype=t.dtype)
    if transpose:
        e = jnp.swapaxes(t4, 2, 3)[:, :, :, None, :] * eye[None, :, None, :, None]
        return e.reshape(S5_BLOCKS, S5_STATES, LANES)
    e = t4[:, :, :, None, :] * eye[None, :, None, :, None]
    return e.reshape(S5_BLOCKS, LANES, S5_STATES)


def _block_diag_extract(m, transpose):
    if transpose:
        m5 = m.reshape(S5_BLOCKS, 8, SSM_STATE, 8, SSM_GROUP)
        d = jnp.stack([m5[:, i, :, i, :] for i in range(8)], axis=1)
        return jnp.swapaxes(d, 2, 3).reshape(N_GROUPS, SSM_GROUP, SSM_STATE)
    m5 = m.reshape(S5_BLOCKS, 8, SSM_GROUP, 8, SSM_STATE)
    d = jnp.stack([m5[:, i, :, i, :] for i in range(8)], axis=1)
    return d.reshape(N_GROUPS, SSM_GROUP, SSM_STATE)


def _pack(pieces):
    flat = jnp.concatenate([p.reshape(-1).astype(F32) for p in pieces])
    size = -(-flat.shape[0] // PACK_ROWS) * PACK_ROWS
    return _pad_to(flat, 0, size).reshape(-1, LANES)


def _unpack(packed, shapes):
    flat = packed.reshape(-1)
    out, off = [], 0
    for shp in shapes:
        size = math.prod(shp)
        out.append(flat[off:off + size].reshape(shp))
        off += size
    return out


def kernel(x, norm_mix, w_in, b_forget, lam_re, lam_im, b_re, b_im, c_re, c_im, d_skip, log_dt, w_glu, b_glu, q_norm, k_norm, norm_out_ssm, norm_out_attn, w_out, norm_ffn, w_up, conv_w, conv_b, w_down, loss_target, m_norm_mix, m_w_in, m_b_forget, m_lam_re, m_lam_im, m_b_re, m_b_im, m_c_re, m_c_im, m_d_skip, m_log_dt, m_w_glu, m_b_glu, m_q_norm, m_k_norm, m_norm_out_ssm, m_norm_out_attn, m_w_out, m_norm_ffn, m_w_up, m_conv_w, m_conv_b, m_w_down, v_norm_mix, v_w_in, v_b_forget, v_lam_re, v_lam_im, v_b_re, v_b_im, v_c_re, v_c_im, v_d_skip, v_log_dt, v_w_glu, v_b_glu, v_q_norm, v_k_norm, v_norm_out_ssm, v_norm_out_attn, v_w_out, v_norm_ffn, v_w_up, v_conv_w, v_conv_b, v_w_down):
    given = dict(locals())
    weights = {k: given[k] for k in WEIGHT_NAMES}
    mom1 = {k: given["m_" + k] for k in WEIGHT_NAMES}
    mom2 = {k: given["v_" + k] for k in WEIGHT_NAMES}
    n_seq, seq_len, _ = x.shape
    n = n_seq * seq_len
    xf = x.reshape(n, D_MODEL)
    target = loss_target.reshape(n, D_MODEL)
    me_idx = 4 * lax.axis_index("x") + 2 * lax.axis_index("y") + lax.axis_index("c")

    in_flags = [False] * 2
    in_sems = _exchange_start([w_in[0].astype(BF16), conv_w[0]], in_flags, norm_mix, "gather_in_start", 3)
    my_slot = lax.broadcasted_iota(jnp.int32, (N_DEV, 1, 1), 0) == me_idx

    lr3 = lam_re[0].reshape(N_GROUPS, 1, SSM_STATE)
    li3 = lam_im[0].reshape(N_GROUPS, 1, SSM_STATE)
    ldt3 = log_dt[0].reshape(N_GROUPS, 1, 1)
    br_t = jnp.swapaxes(b_re[0], 1, 2)
    bi_t = jnp.swapaxes(b_im[0], 1, 2)
    ab_re, ab_im, bb_re, bb_im = _s5_param_fwd(lr3, li3, ldt3, br_t, bi_t)
    a_re = ab_re.reshape(S5_BLOCKS, 1, S5_STATES)
    a_im = ab_im.reshape(S5_BLOCKS, 1, S5_STATES)
    bbr = _block_diag(bb_re, False).astype(BF16)
    bbi = _block_diag(bb_im, False).astype(BF16)
    cr = _block_diag(c_re[0], True).astype(BF16)
    ci = _block_diag(c_im[0], True).astype(BF16)
    avg = jnp.kron(jnp.eye(N_HEADS, dtype=F32), jnp.full((HEAD_DIM, HEAD_DIM), 1.0 / HEAD_DIM, F32)).astype(BF16)
    qg = jnp.tile(q_norm, (1, N_HEADS))
    kg = jnp.tile(k_norm, (1, N_HEADS))
    rows_w = jnp.concatenate([w_down[0], w_out[0], w_glu[0].reshape(ROWS_GLU, D_MODEL)], axis=0).astype(BF16)

    (own_in, own_cw), (g_in, g_cw) = _exchange_wait(in_sems[0], in_sems[1], in_sems[2], in_sems[3], in_flags,
                                                    _after_all(a_re, a_im, bbr, bbi, cr, ci, avg, qg, kg, rows_w),
                                                    "gather_in_wait")
    g_in = jnp.where(my_slot, own_in[None], g_in)
    g_cw = jnp.where(my_slot, own_cw[None], g_cw)
    rest_flags = [False] * 2
    w_sems = _exchange_start([rows_w, w_up[0].astype(BF16)], rest_flags, g_in, "gather_rest_start", 0)
    norm_mix = norm_mix + w_sems[4][0, 0]
    w_in_p = _pad_to(jnp.swapaxes(g_in, 0, 1).reshape(D_MODEL, D_IN), 1, D_IN_PAD)

    hn, u, qkv, raw, fl = _inproj_fwd(xf, norm_mix, w_in_p, avg, qg, kg)
    yc, st_r, st_i = _s5_fwd(u, a_re, a_im, bbr, bbi, cr, ci, d_skip, n_seq)
    bf = _pad_to(b_forget, 1, LANES)
    cum = _fprep_fwd(fl, bf, n_seq)
    cum8 = jnp.swapaxes(cum[:, :N_HEADS].reshape(n_seq, seq_len, N_HEADS), 1, 2)
    cq = cum8[:, :, :, None]
    ck = cum8[:, :, None, :]
    ya, lse = _attn_fwd(qkv, cq, ck, n_seq)
    (own_rows, own_up), (g_rows, g_up) = _exchange_wait(w_sems[0], w_sems[1], w_sems[2], w_sems[3], rest_flags, ya,
                                                        "gather_rest_wait")
    g_rows = jnp.where(my_slot, own_rows[None], g_rows)
    g_up = jnp.where(my_slot, own_up[None], g_up)
    g_down = g_rows[:, :ROWS_DOWN]
    g_out = g_rows[:, ROWS_DOWN:ROWS_DOWN + ROWS_OUT]
    g_glu = g_rows[:, ROWS_DOWN + ROWS_OUT:]
    w_glu_f = g_glu.reshape(D_SSM, D_SSM)
    w_out_f = g_out.reshape(D_MODEL, D_MODEL)
    conv_st = _pad_to(jnp.concatenate([g_cw, conv_b.reshape(N_DEV, 1, -1)], axis=1), 1, SUBLANES)
    w_down4 = g_down.reshape(FFN_GROUPS, FFN_GROUP, D_MODEL)
    ys = _glu_fwd(yc, w_glu_f, b_glu)
    h1, hn2, mixed = _mix_fwd(xf, ys, ya, norm_out_ssm, norm_out_attn, w_out_f, norm_ffn)
    ug, uv, pg, pv, dy, loss_part = _ffn_fwd(hn2, h1, target, g_up, conv_st, w_down4, seq_len)
    loss_local = 0.5 * jnp.sum(loss_part) / D_MODEL

    dug, duv, act, dhn2, dcg, dcv = _ffn_bwd(dy, ug, uv, pg, pv, g_up, conv_st, w_down4, seq_len)
    dh1, dys, dya, d_gs, d_ga, d_gf = _mix_bwd(dy, dhn2[None], h1, ys, ya, norm_out_ssm, norm_out_attn, w_out_f, norm_ffn)
    dyc, gl_b, dz_b, d_bglu = _glu_bwd(yc, dys, w_glu_f, b_glu)

    gw_glu = _tn_matmul(gl_b, dz_b, "dw_glu", D_SSM, D_SSM)
    gw_out = _tn_matmul(mixed, dh1, "dw_out", D_MODEL, D_MODEL)
    gw_up = jnp.concatenate([_tn_grouped(hn2, dug, "dw_up_gate", True, BF16),
                             _tn_grouped(hn2, duv, "dw_up_val", True, BF16)], axis=0)
    gw_down = _tn_grouped(act, dy, "dw_down", False)
    g_conv = jnp.concatenate([dcg, dcv], axis=0)
    by_cols = lambda g, c: jnp.swapaxes(g.reshape(g.shape[0], N_DEV, c), 0, 1)
    early_flags = [True] * 2
    rows_g = jnp.concatenate([gw_down.reshape(N_DEV, ROWS_DOWN, D_MODEL), gw_out.reshape(N_DEV, ROWS_OUT, D_MODEL),
                              gw_glu.reshape(N_DEV, ROWS_GLU, D_MODEL)], axis=1).astype(BF16)
    g_sems = _exchange_start([rows_g, gw_up], early_flags, dyc, "grad_early_start", 1)
    started = g_sems[4][0, 0]

    du, dbbr, dbbi, dcr, dci, dar, dai, ddk = _s5_bwd(u, dyc, st_r, st_i, a_re, a_im, bbr, bbi, cr, ci,
                                                      d_skip + started, n_seq)
    partial_early = {
        "ab_re": jnp.sum(dar, axis=1), "ab_im": jnp.sum(dai, axis=1),
        "bb_re": _block_diag_extract(dbbr, False), "bb_im": _block_diag_extract(dbbi, False),
        "c_re": _block_diag_extract(dcr, True), "c_im": _block_diag_extract(dci, True),
        "d_skip": jnp.sum(ddk, axis=1), "b_glu": d_bglu,
        "norm_out_ssm": d_gs, "norm_out_attn": d_ga, "norm_ffn": d_gf, "conv_b": g_conv[:, 3],
    }
    early_keys = tuple(partial_early)
    early_shapes = [partial_early[k].shape for k in early_keys]
    p_sems = _exchange_start([_pack([partial_early[k] for k in early_keys])], [False], du, "small_early_start", 2)
    started = started + p_sems[4][0, 0]

    dqn, dkn, dv, dcq, dck = _attn_bwd(qkv, cq, ck + started, ya, dya, lse, n_seq)
    dcum8 = dcq[:, :, :, 0] + dck.reshape(n_seq, N_HEADS, seq_len)
    dcum = _pad_to(jnp.swapaxes(dcum8, 1, 2).reshape(n, N_HEADS), 1, LANES)
    dfl, dbf = _fprep_bwd(dcum, fl, bf, n_seq)
    dx, dproj, d_gmix, d_qg, d_kg = _inproj_bwd(xf, norm_mix, w_in_p, avg, qg, kg, raw, du, dqn, dkn, dv, dfl, dh1)

    gw_in = _tn_matmul(hn, dproj, "dw_in", D_MODEL, D_IN_PAD, out_cols=D_IN)
    partial_late = {
        "norm_mix": d_gmix, "b_forget": jnp.sum(dbf, axis=(0, 1))[:N_HEADS],
        "q_norm": jnp.sum(d_qg.reshape(N_HEADS, HEAD_DIM), axis=0),
        "k_norm": jnp.sum(d_kg.reshape(N_HEADS, HEAD_DIM), axis=0), "loss": loss_local.reshape(1),
    }
    late_keys = tuple(partial_late)
    late_shapes = [partial_late[k].shape for k in late_keys]

    late_flags = [True, True, False]
    l_sems = _exchange_start(
        [by_cols(gw_in, D_IN // N_DEV).astype(BF16), g_conv[:, :3], _pack([partial_late[k] for k in late_keys])],
        late_flags, dx, "grad_late_start", 4)
    (src_rows, src_up), (land_rows, land_up) = _exchange_wait(g_sems[0], g_sems[1], g_sems[2], g_sems[3], early_flags,
                                                              l_sems[4], "grad_early_wait")
    own_rows = lax.dynamic_index_in_dim(src_rows, me_idx, 0, keepdims=False)
    land = {"w_up": land_up, "w_down": land_rows[:, :ROWS_DOWN], "w_out": land_rows[:, ROWS_DOWN:ROWS_DOWN + ROWS_OUT],
            "w_glu": land_rows[:, ROWS_DOWN + ROWS_OUT:].reshape(N_DEV, -1, D_SSM)}
    own = {"w_up": lax.dynamic_index_in_dim(src_up, me_idx, 0, keepdims=False), "w_down": own_rows[:ROWS_DOWN],
           "w_out": own_rows[ROWS_DOWN:ROWS_DOWN + ROWS_OUT], "w_glu": own_rows[ROWS_DOWN + ROWS_OUT:].reshape(-1, D_SSM)}
    grads, deltas, new_m, new_v = {}, {}, {}, {}

    def adam_shard(name):
        grads[name], deltas[name], new_m[name], new_v[name] = _adam_sharded(
            land[name], own[name], weights[name], mom1[name], mom2[name], "adam_" + name, ADAM_ROWS[name])

    for name in ("w_up", "w_down", "w_out", "w_glu"):
        adam_shard(name)
    (own_pack,), (early_parts,) = _exchange_wait(p_sems[0], p_sems[1], p_sems[2], p_sems[3], [False], land_up,
                                                 "small_early_wait")
    early_sum = _sum_partials(early_parts, own_pack, "sum_early_partials")
    (src_in, src_cw, own_late), (land["w_in"], land["conv_w"], late_parts) = _exchange_wait(
        l_sems[0], l_sems[1], l_sems[2], l_sems[3], late_flags,
        _after_all(early_sum, *[new_v[k] for k in ("w_up", "w_down", "w_out", "w_glu")]), "grad_late_wait")
    own["w_in"] = lax.dynamic_index_in_dim(src_in, me_idx, 0, keepdims=False)
    own["conv_w"] = lax.dynamic_index_in_dim(src_cw, me_idx, 0, keepdims=False)
    for name in ("w_in", "conv_w"):
        adam_shard(name)

    summed = dict(zip(late_keys, _unpack(_sum_partials(late_parts, own_late, "sum_late_partials"), late_shapes)))
    summed.update(zip(early_keys, _unpack(early_sum, early_shapes)))
    dlr, dli, dldt, dbr_t, dbi_t = _s5_param_bwd(
        lr3, li3, ldt3, br_t, bi_t, summed["ab_re"].reshape(lr3.shape), summed["ab_im"].reshape(lr3.shape),
        summed["bb_re"], summed["bb_im"])
    small_grads = {
        "norm_mix": summed["norm_mix"], "b_forget": summed["b_forget"], "lam_re": dlr, "lam_im": dli,
        "b_re": jnp.swapaxes(dbr_t, 1, 2), "b_im": jnp.swapaxes(dbi_t, 1, 2), "c_re": summed["c_re"], "c_im": summed["c_im"],
        "d_skip": summed["d_skip"], "log_dt": dldt, "b_glu": summed["b_glu"], "q_norm": summed["q_norm"],
        "k_norm": summed["k_norm"], "norm_out_ssm": summed["norm_out_ssm"], "norm_out_attn": summed["norm_out_attn"],
        "norm_ffn": summed["norm_ffn"], "conv_b": summed["conv_b"],
    }
    repl = tuple(k for k in WEIGHT_NAMES if k not in SHARDED)
    g_list = [small_grads[k].reshape(weights[k].shape) for k in repl]
    d_list, m_list, v_list = _adam_replicated(g_list, [weights[k] for k in repl], [mom1[k] for k in repl],
                                              [mom2[k] for k in repl], "adam_replicated")
    for k, g, d, nm, nv in zip(repl, g_list, d_list, m_list, v_list):
        grads[k], deltas[k], new_m[k], new_v[k] = g, d, nm, nv

    grad_x = dx.reshape(x.shape)
    loss = summed["loss"].reshape(())
    return (loss, grad_x, *[grads[k] for k in WEIGHT_NAMES], *[deltas[k] for k in WEIGHT_NAMES],
            *[new_m[k] for k in WEIGHT_NAMES], *[new_v[k] for k in WEIGHT_NAMES])
```

```python
import functools
import math

import jax
import jax.numpy as jnp
from jax import lax
from jax.experimental import pallas as pl
from jax.experimental.pallas import tpu as pltpu

F32 = jnp.float32
BF16 = jnp.bfloat16
HIGHEST = lax.Precision.HIGHEST

N_DEV = 8
D_MODEL = 1024
D_SSM = 512
D_ATTN = 512
N_HEADS = 8
HEAD_DIM = 64
N_GROUPS = 32
SSM_GROUP = 16
SSM_STATE = 64
D_FF = 2752
D_FF_PAD = 2816
D_IN = 2056
D_IN_PAD = 2176
EPS = 1e-6
LANES = 128
SUBLANES = 8
VMEM_LIMIT = 56 * 1024 * 1024

ADAM_LR = 0.001
ADAM_B1 = 0.9
ADAM_B2 = 0.999
ADAM_EPS = 1e-08
ADAM_WD = 0.01
ADAM_STEP = 10


def _cparams(*sem):
    return pltpu.CompilerParams(dimension_semantics=sem, vmem_limit_bytes=VMEM_LIMIT)


def _dot(a, b, **kw):
    return jnp.dot(a, b, preferred_element_type=F32, **kw)


def _dot_nt(a, b):
    return lax.dot_general(a, b, (((1,), (1,)), ((), ())), preferred_element_type=F32)


def _dot_tn(a, b):
    return lax.dot_general(a, b, (((0,), (0,)), ((), ())), preferred_element_type=F32)


def _rms(x, g):
    return x * lax.rsqrt(jnp.mean(x * x, axis=-1, keepdims=True) + EPS) * g


def _split_dot(x, avg):
    hi = x.astype(BF16)
    lo = (x - hi.astype(F32)).astype(BF16)
    return _dot(hi, avg) + _dot(lo, avg)


@jax.custom_vjp
def _group_mean(x, avg):
    return _split_dot(x, avg)


def _group_mean_fwd(x, avg):
    return _split_dot(x, avg), avg


def _group_mean_bwd(avg, ct):
    return _split_dot(ct, avg), jnp.zeros_like(avg)


_group_mean.defvjp(_group_mean_fwd, _group_mean_bwd)


def _headnorm(q, avg, g):
    return q * lax.rsqrt(_group_mean(q * q, avg) + EPS) * g


def _exchange(srcs, scatter_flags, name):
    n = len(srcs)
    out_shape = []
    for s, sc in zip(srcs, scatter_flags):
        shp = s.shape if sc else (N_DEV,) + s.shape
        out_shape.append(jax.ShapeDtypeStruct(shp, s.dtype))

    def body(*refs):
        src = refs[:n]
        dst = refs[n:2 * n]
        send_sems, recv_sems, loc_sems = refs[2 * n:]
        x, y, c = lax.axis_index("x"), lax.axis_index("y"), lax.axis_index("c")
        me = 4 * x + 2 * y + c
        peers = []
        for j in range(1, N_DEV):
            px = 1 - x if (j >> 2) & 1 else x
            py = 1 - y if (j >> 1) & 1 else y
            pc = 1 - c if j & 1 else c
            peers.append(((px, py, pc), 4 * px + 2 * py + pc))
        local, sends = [], []
        for k in range(n):
            own = src[k].at[me] if scatter_flags[k] else src[k]
            lc = pltpu.make_async_copy(own, dst[k].at[me], loc_sems.at[k])
            lc.start()
            local.append(lc)
            for j, (pid, pidx) in enumerate(peers):
                s = src[k].at[pidx] if scatter_flags[k] else src[k]
                cp = pltpu.make_async_remote_copy(
                    src_ref=s, dst_ref=dst[k].at[me], send_sem=send_sems.at[k, j], recv_sem=recv_sems.at[k, j],
                    device_id=pid, device_id_type=pl.DeviceIdType.MESH)
                cp.start()
                sends.append(cp)
        for k in range(n):
            for j, (pid, pidx) in enumerate(peers):
                s = src[k].at[pidx] if scatter_flags[k] else src[k]
                pltpu.make_async_remote_copy(
                    src_ref=s, dst_ref=dst[k].at[pidx], send_sem=send_sems.at[k, j], recv_sem=recv_sems.at[k, j],
                    device_id=pid, device_id_type=pl.DeviceIdType.MESH).wait_recv()
        for cp in sends:
            cp.wait_send()
        for lc in local:
            lc.wait()

    any_spec = pl.BlockSpec(memory_space=pl.ANY)
    return pl.pallas_call(
        body, name=name, out_shape=tuple(out_shape),
        in_specs=[any_spec] * n, out_specs=tuple([any_spec] * n),
        scratch_shapes=[pltpu.SemaphoreType.DMA((n, N_DEV - 1)), pltpu.SemaphoreType.DMA((n, N_DEV - 1)),
                        pltpu.SemaphoreType.DMA((n,))],
        compiler_params=pltpu.CompilerParams(has_side_effects=True),
    )(*srcs)


def _peer_list():
    x, y, c = lax.axis_index("x"), lax.axis_index("y"), lax.axis_index("c")
    peers = []
    for j in range(1, N_DEV):
        px = 1 - x if (j >> 2) & 1 else x
        py = 1 - y if (j >> 1) & 1 else y
        pc = 1 - c if j & 1 else c
        peers.append(((px, py, pc), 4 * px + 2 * py + pc))
    return 4 * x + 2 * y + c, peers


def _split_copies(src, land, send_sems, recv_sems, scatter_flags, me, peers, incoming):
    copies = []
    for k in range(len(src)):
        for j, (pid, pidx) in enumerate(peers):
            s = src[k].at[pidx] if scatter_flags[k] else src[k]
            i = k * (N_DEV - 1) + j
            copies.append(pltpu.make_async_remote_copy(
                src_ref=s, dst_ref=land[k].at[pidx if incoming else me], send_sem=send_sems[i],
                recv_sem=recv_sems[i], device_id=pid, device_id_type=pl.DeviceIdType.MESH))
    return copies


def _exchange_start(srcs, scatter_flags, after, name, collective_id):
    n = len(srcs)
    ns = n * (N_DEV - 1)
    hbm = pl.BlockSpec(memory_space=pltpu.HBM)
    sem = pl.BlockSpec(memory_space=pltpu.SEMAPHORE)
    land_shapes = [s.shape if sc else (N_DEV,) + s.shape for s, sc in zip(srcs, scatter_flags)]

    def body(*refs):
        src, land = refs[:n], refs[n:2 * n]
        send_sems = refs[2 * n + 1:2 * n + 1 + ns]
        recv_sems = refs[2 * n + 1 + ns:2 * n + 1 + 2 * ns]
        token = refs[4 * n + 1 + 2 * ns]
        me, peers = _peer_list()
        barrier = pltpu.get_barrier_semaphore()
        for pid, _ in peers:
            pl.semaphore_signal(barrier, inc=1, device_id=pid, device_id_type=pl.DeviceIdType.MESH)
        pl.semaphore_wait(barrier, N_DEV - 1)
        for cp in _split_copies(src, land, send_sems, recv_sems, scatter_flags, me, peers, False):
            cp.start()
        token[...] = jnp.zeros_like(token)

    outs = pl.pallas_call(
        body, name=name,
        out_shape=(*[pltpu.SemaphoreType.DMA(())] * (2 * ns), *[pltpu.HBM(s.shape, s.dtype) for s in srcs],
                   *[pltpu.HBM(shp, s.dtype) for shp, s in zip(land_shapes, srcs)],
                   jax.ShapeDtypeStruct((SUBLANES, LANES), F32)),
        in_specs=[hbm] * (2 * n) + [pl.BlockSpec(memory_space=pl.ANY)],
        out_specs=(*[sem] * (2 * ns), *[hbm] * (2 * n), pl.BlockSpec(memory_space=pltpu.VMEM)),
        input_output_aliases={i: 2 * ns + i for i in range(2 * n)},
        compiler_params=pltpu.CompilerParams(has_side_effects=pltpu.SideEffectType.DATAFLOW_SIDE_EFFECTING,
                                             collective_id=collective_id),
    )(*[pltpu.with_memory_space_constraint(s, pltpu.HBM) for s in srcs],
      *[pltpu.with_memory_space_constraint(lax.empty(shp, s.dtype), pltpu.HBM) for shp, s in zip(land_shapes, srcs)],
      after)
    return (outs[:ns], outs[ns:2 * ns], outs[2 * ns:2 * ns + n], outs[2 * ns + n:2 * ns + 2 * n], outs[2 * ns + 2 * n])


def _exchange_wait(send_sems, recv_sems, srcs, lands, scatter_flags, after, name):
    n = len(srcs)
    ns = n * (N_DEV - 1)
    hbm = pl.BlockSpec(memory_space=pltpu.HBM)
    sem = pl.BlockSpec(memory_space=pltpu.SEMAPHORE)

    def body(*refs):
        src, land = refs[:n], refs[n:2 * n]
        s_sems = refs[2 * n:2 * n + ns]
        r_sems = refs[2 * n + ns:2 * n + 2 * ns]
        me, peers = _peer_list()
        for cp in _split_copies(src, land, s_sems, r_sems, scatter_flags, me, peers, True):
            cp.wait_send()
            cp.wait_recv()

    outs = pl.pallas_call(
        body, name=name,
        out_shape=tuple(pltpu.HBM(a.shape, a.dtype) for a in (*srcs, *lands)),
        in_specs=[hbm] * (2 * n) + [sem] * (2 * ns) + [pl.BlockSpec(memory_space=pl.ANY)],
        out_specs=tuple([hbm] * (2 * n)),
        input_output_aliases={i: i for i in range(2 * n)},
        compiler_params=pltpu.CompilerParams(has_side_effects=pltpu.SideEffectType.DATAFLOW_SIDE_EFFECTING),
    )(*srcs, *lands, *send_sems, *recv_sems, after)
    return outs[:n], outs[n:]


def _tn_matmul(a, b, name, tk, tm, out_rows=None, out_cols=None, tn=512):
    n_tok, k_dim = a.shape
    m_dim = b.shape[1]
    grid = (k_dim // tk, m_dim // tm, n_tok // tn)

    def body(a_ref, b_ref, o_ref):
        @pl.when(pl.program_id(2) == 0)
        def _():
            o_ref[...] = jnp.zeros_like(o_ref)
        o_ref[...] += _dot_tn(a_ref[...].astype(BF16), b_ref[...].astype(BF16))

    return pl.pallas_call(
        body, name=name, grid=grid,
        in_specs=[pl.BlockSpec((tn, tk), lambda i, j, k: (k, i)), pl.BlockSpec((tn, tm), lambda i, j, k: (k, j))],
        out_specs=pl.BlockSpec((tk, tm), lambda i, j, k: (i, j)),
        out_shape=jax.ShapeDtypeStruct((out_rows or k_dim, out_cols or m_dim), F32),
        compiler_params=_cparams("parallel", "parallel", "arbitrary"),
    )(a, b)


def _adam_math(g, w, m, v):
    m = ADAM_B1 * m + (1.0 - ADAM_B1) * g
    v = ADAM_B2 * v + (1.0 - ADAM_B2) * (g * g)
    m_hat = m / (1.0 - ADAM_B1 ** ADAM_STEP)
    v_hat = v / (1.0 - ADAM_B2 ** ADAM_STEP)
    delta = -ADAM_LR * (m_hat / (jnp.sqrt(v_hat) + ADAM_EPS) + ADAM_WD * w)
    return delta, m, v


def _adam_sharded(land, own, w, m, v, name, tile):
    _, r, c = w.shape

    def body(*refs):
        l_ref = refs[0]
        own_ref = refs[1] if own is not None else None
        w_ref, m_ref, v_ref, g_ref, d_ref, nm_ref, nv_ref = [ref.at[0] for ref in refs[-7:]]
        if own_ref is not None:
            x, y, z = lax.axis_index("x"), lax.axis_index("y"), lax.axis_index("c")
            me = 4 * x + 2 * y + z
            mine = own_ref[...].astype(F32)
        g = None
        for s in range(N_DEV):
            part = l_ref[s].astype(F32)
            if own_ref is not None:
                part = jnp.where(me == s, mine, part)
            g = part if g is None else g + part
        d, nm, nv = _adam_math(g, w_ref[...], m_ref[...], v_ref[...])
        g_ref[...] = g
        d_ref[...] = d
        nm_ref[...] = nm
        nv_ref[...] = nv

    tr, tc = tile
    spec = pl.BlockSpec((1, tr, tc), lambda i, j: (0, i, j))
    own_specs, own_args = ([pl.BlockSpec((tr, tc), lambda i, j: (i, j))], [own]) if own is not None else ([], [])
    return pl.pallas_call(
        body, name=name, grid=(r // tr, c // tc),
        in_specs=[pl.BlockSpec((N_DEV, tr, tc), lambda i, j: (0, i, j)), *own_specs, spec, spec, spec],
        out_specs=(spec, spec, spec, spec),
        out_shape=tuple(jax.ShapeDtypeStruct((1, r, c), F32) for _ in range(4)),
        compiler_params=_cparams("parallel", "parallel"),
    )(land, *own_args, w, m, v)


def _sum_partials(parts, own, name):
    _, r, c = parts.shape

    def body(*refs):
        p_ref, o_ref = refs[0], refs[-1]
        if own is not None:
            x, y, z = lax.axis_index("x"), lax.axis_index("y"), lax.axis_index("c")
            me = 4 * x + 2 * y + z
            mine = refs[1][...]
        g = None
        for s in range(N_DEV):
            part = p_ref[s]
            if own is not None:
                part = jnp.where(me == s, mine, part)
            g = part if g is None else g + part
        o_ref[...] = g

    args = (parts,) if own is None else (parts, own)
    return pl.pallas_call(body, name=name, out_shape=jax.ShapeDtypeStruct((r, c), F32),
                          compiler_params=pltpu.CompilerParams(vmem_limit_bytes=VMEM_LIMIT))(*args)


def _adam_replicated(gs, ws, ms, vs, name):
    k = len(ws)

    def body(*refs):
        outs = refs[4 * k:]
        for i in range(k):
            d, nm, nv = _adam_math(refs[i][...], refs[k + i][...], refs[2 * k + i][...], refs[3 * k + i][...])
            outs[i][...] = d
            outs[k + i][...] = nm
            outs[2 * k + i][...] = nv

    outs = pl.pallas_call(body, name=name, out_shape=tuple(jax.ShapeDtypeStruct(w.shape, F32) for w in ws) * 3,
                          compiler_params=pltpu.CompilerParams(vmem_limit_bytes=VMEM_LIMIT))(*gs, *ws, *ms, *vs)
    return outs[:k], outs[k:2 * k], outs[2 * k:]


def _inproj_fwd(x, g, w_in, avg, qg, kg, tm=512):
    n = x.shape[0]

    def body(x_ref, g_ref, w_ref, a_ref, qg_ref, kg_ref, hn_ref, u_ref, qkv_ref, raw_ref, fl_ref):
        hn = _rms(x_ref[...], g_ref[...]).astype(BF16)
        hn_ref[...] = hn
        proj = _dot_nt(hn, w_ref[...])
        u_ref[...] = proj[:, 0:512]
        q = proj[:, 512:1024]
        k = proj[:, 1024:1536]
        raw_ref[:, 0:512] = q
        raw_ref[:, 512:1024] = k
        qkv_ref[:, 0:512] = _headnorm(q, a_ref[...], qg_ref[...])
        qkv_ref[:, 512:1024] = _headnorm(k, a_ref[...], kg_ref[...])
        qkv_ref[:, 1024:1536] = proj[:, 1536:2048]
        fl_ref[...] = proj[:, 2048:D_IN_PAD]

    row = lambda w: pl.BlockSpec((tm, w), lambda i: (i, 0))
    full = lambda a: pl.BlockSpec(a.shape, lambda i: (0,) * a.ndim)
    return pl.pallas_call(
        body, name="inproj_fwd", grid=(n // tm,),
        in_specs=[row(D_MODEL), full(g), full(w_in), full(avg), full(qg), full(kg)],
        out_specs=(row(D_MODEL), row(512), row(1536), row(1024), row(LANES)),
        out_shape=(jax.ShapeDtypeStruct((n, D_MODEL), BF16), jax.ShapeDtypeStruct((n, 512), F32),
                   jax.ShapeDtypeStruct((n, 1536), F32), jax.ShapeDtypeStruct((n, 1024), F32),
                   jax.ShapeDtypeStruct((n, LANES), F32)),
        compiler_params=_cparams("parallel"),
    )(x, g, w_in, avg, qg, kg)


def _inproj_bwd(x, g, w_in, avg, qg, kg, raw, du, dqn, dkn, dv, dfl, dres, tm=512):
    n = x.shape[0]

    def body(x_ref, g_ref, w_ref, a_ref, qg_ref, kg_ref, raw_ref, du_ref, dqn_ref, dkn_ref, dv_ref, dfl_ref, dres_ref,
             dx_ref, dproj_ref, dg_ref, dqg_ref, dkg_ref):
        @pl.when(pl.program_id(0) == 0)
        def _():
            dg_ref[...] = jnp.zeros_like(dg_ref)
            dqg_ref[...] = jnp.zeros_like(dqg_ref)
            dkg_ref[...] = jnp.zeros_like(dkg_ref)
        avg_m = a_ref[...]
        _, vjp_q = jax.vjp(lambda q, gg: _headnorm(q, avg_m, gg), raw_ref[:, 0:512], qg_ref[...])
        dq, dqg = vjp_q(dqn_ref[...])
        _, vjp_k = jax.vjp(lambda k, gg: _headnorm(k, avg_m, gg), raw_ref[:, 512:1024], kg_ref[...])
        dk, dkg = vjp_k(dkn_ref[...])
        dproj = jnp.concatenate([du_ref[...], dq, dk, dv_ref[...], dfl_ref[...]], axis=1).astype(BF16)
        dproj_ref[...] = dproj
        dhn = _dot(dproj, w_ref[...])
        _, vjp_x = jax.vjp(_rms, x_ref[...], g_ref[...])
        dxn, dg = vjp_x(dhn)
        dx_ref[...] = dxn + dres_ref[...]
        dg_ref[...] += dg
        dqg_ref[...] += dqg
        dkg_ref[...] += dkg

    row = lambda w: pl.BlockSpec((tm, w), lambda i: (i, 0))
    full = lambda a: pl.BlockSpec(a.shape, lambda i: (0,) * a.ndim)
    vec = lambda w: pl.BlockSpec((1, w), lambda i: (0, 0))
    return pl.pallas_call(
        body, name="inproj_bwd", grid=(n // tm,),
        in_specs=[row(D_MODEL), full(g), full(w_in), full(avg), full(qg), full(kg), row(1024), row(512), row(512),
                  row(512), row(512), row(LANES), row(D_MODEL)],
        out_specs=(row(D_MODEL), row(D_IN_PAD), vec(D_MODEL), vec(512), vec(512)),
        out_shape=(jax.ShapeDtypeStruct((n, D_MODEL), F32), jax.ShapeDtypeStruct((n, D_IN_PAD), BF16),
                   jax.ShapeDtypeStruct((1, D_MODEL), F32), jax.ShapeDtypeStruct((1, 512), F32),
                   jax.ShapeDtypeStruct((1, 512), F32)),
        compiler_params=_cparams("arbitrary"),
    )(x, g, w_in, avg, qg, kg, raw, du, dqn, dkn, dv, dfl, dres)


def _glu_fwd(yc, wg, bg, tm=512):
    n = yc.shape[0]

    def body(yc_ref, w_ref, b_ref, ys_ref):
        gl = jax.nn.gelu(yc_ref[...])
        z = _dot(gl.astype(BF16), w_ref[...]) + b_ref[...]
        ys_ref[...] = gl * jax.nn.sigmoid(z)

    row = pl.BlockSpec((tm, 512), lambda i: (i, 0))
    full = lambda a: pl.BlockSpec(a.shape, lambda i: (0,) * a.ndim)
    return pl.pallas_call(
        body, name="glu_fwd", grid=(n // tm,), in_specs=[row, full(wg), full(bg)], out_specs=row,
        out_shape=jax.ShapeDtypeStruct((n, 512), F32), compiler_params=_cparams("parallel"),
    )(yc, wg, bg)


def _glu_bwd(yc, dys, wg, bg, tm=512):
    n = yc.shape[0]

    def body(yc_ref, dys_ref, w_ref, b_ref, dyc_ref, gl_ref, dz_ref, db_ref):
        @pl.when(pl.program_id(0) == 0)
        def _():
            db_ref[...] = jnp.zeros_like(db_ref)
        gl, vjp_gelu = jax.vjp(jax.nn.gelu, yc_ref[...])
        glb = gl.astype(BF16)
        z = _dot(glb, w_ref[...]) + b_ref[...]
        s = jax.nn.sigmoid(z)
        dys = dys_ref[...]
        dz = dys * gl * s * (1.0 - s)
        dzb = dz.astype(BF16)
        dgl = dys * s + _dot_nt(dzb, w_ref[...])
        dyc_ref[...] = vjp_gelu(dgl)[0]
        gl_ref[...] = glb
        dz_ref[...] = dzb
        db_ref[...] += jnp.sum(dz, axis=0, keepdims=True)

    row = pl.BlockSpec((tm, 512), lambda i: (i, 0))
    full = lambda a: pl.BlockSpec(a.shape, lambda i: (0,) * a.ndim)
    return pl.pallas_call(
        body, name="glu_bwd", grid=(n // tm,), in_specs=[row, row, full(wg), full(bg)],
        out_specs=(row, row, row, pl.BlockSpec((1, 512), lambda i: (0, 0))),
        out_shape=(jax.ShapeDtypeStruct((n, 512), F32), jax.ShapeDtypeStruct((n, 512), BF16),
                   jax.ShapeDtypeStruct((n, 512), BF16), jax.ShapeDtypeStruct((1, 512), F32)),
        compiler_params=_cparams("arbitrary"),
    )(yc, dys, wg, bg)


def _mix_fwd(x, ys, ya, gs, ga, wout, gf, tm=512):
    n = x.shape[0]

    def body(x_ref, ys_ref, ya_ref, gs_ref, ga_ref, w_ref, gf_ref, h1_ref, hn2_ref, mixed_ref):
        mixed = jnp.concatenate([_rms(ys_ref[...], gs_ref[...]), _rms(ya_ref[...], ga_ref[...])], axis=1).astype(BF16)
        mixed_ref[...] = mixed
        h1 = x_ref[...] + _dot(mixed, w_ref[...])
        h1_ref[...] = h1
        hn2_ref[...] = _rms(h1, gf_ref[...]).astype(BF16)

    row = lambda w: pl.BlockSpec((tm, w), lambda i: (i, 0))
    full = lambda a: pl.BlockSpec(a.shape, lambda i: (0,) * a.ndim)
    return pl.pallas_call(
        body, name="mix_fwd", grid=(n // tm,),
        in_specs=[row(D_MODEL), row(512), row(512), full(gs), full(ga), full(wout), full(gf)],
        out_specs=(row(D_MODEL), row(D_MODEL), row(D_MODEL)),
        out_shape=(jax.ShapeDtypeStruct((n, D_MODEL), F32), jax.ShapeDtypeStruct((n, D_MODEL), BF16),
                   jax.ShapeDtypeStruct((n, D_MODEL), BF16)),
        compiler_params=_cparams("parallel"),
    )(x, ys, ya, gs, ga, wout, gf)


def _mix_bwd(dy, dhn2_parts, h1, ys, ya, gs, ga, wout, gf, tm=512):
    n = dy.shape[0]
    n_parts = dhn2_parts.shape[0]

    def body(dy_ref, dp_ref, h1_ref, ys_ref, ya_ref, gs_ref, ga_ref, w_ref, gf_ref,
             dh1_ref, dys_ref, dya_ref, dgs_ref, dga_ref, dgf_ref):
        @pl.when(pl.program_id(0) == 0)
        def _():
            dgs_ref[...] = jnp.zeros_like(dgs_ref)
            dga_ref[...] = jnp.zeros_like(dga_ref)
            dgf_ref[...] = jnp.zeros_like(dgf_ref)
        dhn2 = dp_ref[0]
        for p in range(1, n_parts):
            dhn2 = dhn2 + dp_ref[p]
        _, vjp_f = jax.vjp(_rms, h1_ref[...], gf_ref[...])
        dh1n, dgf = vjp_f(dhn2)
        dh1 = dy_ref[...] + dh1n
        dh1_ref[...] = dh1
        dmixed = _dot_nt(dh1.astype(BF16), w_ref[...])
        _, vjp_s = jax.vjp(_rms, ys_ref[...], gs_ref[...])
        dys, dgs = vjp_s(dmixed[:, 0:512])
        _, vjp_a = jax.vjp(_rms, ya_ref[...], ga_ref[...])
        dya, dga = vjp_a(dmixed[:, 512:1024])
        dys_ref[...] = dys
        dya_ref[...] = dya
        dgs_ref[...] += dgs
        dga_ref[...] += dga
        dgf_ref[...] += dgf

    row = lambda w: pl.BlockSpec((tm, w), lambda i: (i, 0))
    full = lambda a: pl.BlockSpec(a.shape, lambda i: (0,) * a.ndim)
    vec = lambda w: pl.BlockSpec((1, w), lambda i: (0, 0))
    return pl.pallas_call(
        body, name="mix_bwd", grid=(n // tm,),
        in_specs=[row(D_MODEL), pl.BlockSpec((n_parts, tm, D_MODEL), lambda i: (0, i, 0)), row(D_MODEL), row(512),
                  row(512), full(gs), full(ga), full(wout), full(gf)],
        out_specs=(row(D_MODEL), row(512), row(512), vec(512), vec(512), vec(D_MODEL)),
        out_shape=(jax.ShapeDtypeStruct((n, D_MODEL), F32), jax.ShapeDtypeStruct((n, 512), F32),
                   jax.ShapeDtypeStruct((n, 512), F32), jax.ShapeDtypeStruct((1, 512), F32),
                   jax.ShapeDtypeStruct((1, 512), F32), jax.ShapeDtypeStruct((1, D_MODEL), F32)),
        compiler_params=_cparams("arbitrary"),
    )(dy, dhn2_parts, h1, ys, ya, gs, ga, wout, gf)


HALO = 16
FFN_GROUPS = 4
FFN_GROUP = D_FF // FFN_GROUPS


def _conv3(ue, cw):
    return cw[2:3] * ue + cw[1:2] * pltpu.roll(ue, 1, 0) + cw[0:1] * pltpu.roll(ue, 2, 0) + cw[3:4]


def _ffn_weight_specs():
    gate = lambda i, j: (j, 0, 0)
    val = lambda i, j: (j + FFN_GROUPS, 0, 0)
    w_blk, c_blk = (1, FFN_GROUP, D_MODEL), (1, SUBLANES, FFN_GROUP)
    return [pl.BlockSpec(w_blk, gate), pl.BlockSpec(w_blk, val), pl.BlockSpec(c_blk, gate), pl.BlockSpec(c_blk, val),
            pl.BlockSpec((1, FFN_GROUP, D_MODEL), gate)]


def _ffn_fwd(hn2, h1, target, w_up, conv, w_down, seq_len, tm=512):
    n = hn2.shape[0]
    nj = FFN_GROUPS
    hb = tm // HALO

    def body(hn_ref, halo_ref, h1_ref, tgt_ref, wg_ref, wv_ref, cg_ref, cv_ref, wd_ref,
             ug_ref, uv_ref, pg_ref, pv_ref, dy_ref, loss_ref, acc):
        i, j = pl.program_id(0), pl.program_id(1)
        seq_start = (i * tm) % seq_len == 0
        halo = halo_ref[...]
        halo = jnp.where(seq_start, jnp.zeros_like(halo), halo)
        he = jnp.concatenate([halo, hn_ref[...]], axis=0)
        ueg = _dot_nt(he, wg_ref[0])
        uev = _dot_nt(he, wv_ref[0])
        ug_ref[0] = ueg[HALO:].astype(BF16)
        uv_ref[0] = uev[HALO:].astype(BF16)
        cg = _conv3(ueg, cg_ref[0])[HALO:]
        cv = _conv3(uev, cv_ref[0])[HALO:]
        pg_ref[0] = cg.astype(BF16)
        pv_ref[0] = cv.astype(BF16)
        act = (jax.nn.silu(cg) * cv).astype(BF16)
        part = _dot(act, wd_ref[0])

        @pl.when(j == 0)
        def _():
            acc[...] = part

        @pl.when(j > 0)
        def _():
            acc[...] += part

        @pl.when(j == nj - 1)
        def _():
            err = h1_ref[...] + acc[...] - tgt_ref[...]
            dy_ref[...] = err * (1.0 / D_MODEL)
            loss_ref[0] = jnp.sum(err * err, axis=0, keepdims=True)

    row = pl.BlockSpec((tm, D_MODEL), lambda i, j: (i, 0))
    u_main = pl.BlockSpec((1, tm, FFN_GROUP), lambda i, j: (j, i, 0))
    u_shape = jax.ShapeDtypeStruct((FFN_GROUPS, n, FFN_GROUP), BF16)
    return pl.pallas_call(
        body, name="ffn_fwd", grid=(n // tm, nj),
        in_specs=[row, pl.BlockSpec((HALO, D_MODEL), lambda i, j: (jnp.maximum(i * hb - 1, 0), 0)), row, row,
                  *_ffn_weight_specs()],
        out_specs=(u_main, u_main, u_main, u_main, row, pl.BlockSpec((1, 1, D_MODEL), lambda i, j: (i, 0, 0))),
        out_shape=(u_shape, u_shape, u_shape, u_shape, jax.ShapeDtypeStruct((n, D_MODEL), F32),
                   jax.ShapeDtypeStruct((n // tm, 1, D_MODEL), F32)),
        scratch_shapes=[pltpu.VMEM((tm, D_MODEL), F32)],
        compiler_params=_cparams("parallel", "arbitrary"),
    )(hn2, hn2, h1, target, w_up, w_up, conv, conv, w_down)


def _ffn_bwd(dy, ug, uv, pg, pv, w_up, conv, w_down, seq_len, tm=512):
    n = dy.shape[0]
    nj = FFN_GROUPS
    fb = FFN_GROUP
    hb = tm // HALO
    last_hb = n // HALO - 1
    rows = tm + HALO

    def body(dy_ref, dyn_ref, ug_ref, uv_ref, pgm_ref, pgn_ref, pvm_ref, pvn_ref, wg_ref, wv_ref, cg_ref, cv_ref,
             wd_ref, dug_ref, duv_ref, act_ref, dhn_ref, dcg_ref, dcv_ref, acc):
        i, j = pl.program_id(0), pl.program_id(1)
        seq_end = ((i + 1) * tm) % seq_len == 0
        dyn = dyn_ref[...]
        dyn = jnp.where(seq_end, jnp.zeros_like(dyn), dyn)
        d_out = jnp.concatenate([dy_ref[...], dyn], axis=0).astype(BF16)
        d_act = _dot_nt(d_out, wd_ref[0])
        cge = jnp.concatenate([pgm_ref[0], pgn_ref[0]], axis=0).astype(F32)
        cve = jnp.concatenate([pvm_ref[0], pvn_ref[0]], axis=0).astype(F32)
        act, vjp_act = jax.vjp(lambda g, v: jax.nn.silu(g) * v, cge, cve)
        dcge, dcve = vjp_act(d_act)
        act_ref[0] = act[:tm].astype(BF16)

        def conv_t(dc, u_ref, cw):
            ahead1 = pltpu.roll(dc, rows - 1, 0)[:tm]
            ahead2 = pltpu.roll(dc, rows - 2, 0)[:tm]
            here = dc[:tm]
            du = cw[2:3] * here + cw[1:2] * ahead1 + cw[0:1] * ahead2
            u = u_ref[0].astype(F32)
            col = lambda x: jnp.sum(x, axis=0, keepdims=True)
            grad = jnp.concatenate([col(ahead2 * u), col(ahead1 * u), col(here * u), col(here),
                                    jnp.zeros((4, fb), F32)], axis=0)
            return du.astype(BF16), grad

        cwg, cwv = cg_ref[0], cv_ref[0]
        dug, grad_g = conv_t(dcge, ug_ref, cwg)
        duv, grad_v = conv_t(dcve, uv_ref, cwv)
        dug_ref[0] = dug
        duv_ref[0] = duv
        part = _dot(dug, wg_ref[0]) + _dot(duv, wv_ref[0])

        @pl.when(j == 0)
        def _():
            acc[...] = part

        @pl.when(j > 0)
        def _():
            acc[...] += part

        @pl.when(j == nj - 1)
        def _():
            dhn_ref[...] = acc[...]

        @pl.when(i == 0)
        def _():
            dcg_ref[j] = jnp.zeros((8, fb), F32)
            dcv_ref[j] = jnp.zeros((8, fb), F32)

        dcg_ref[j] += grad_g
        dcv_ref[j] += grad_v

    row = pl.BlockSpec((tm, D_MODEL), lambda i, j: (i, 0))
    u_main = pl.BlockSpec((1, tm, fb), lambda i, j: (j, i, 0))
    u_next = pl.BlockSpec((1, HALO, fb), lambda i, j: (j, jnp.minimum((i + 1) * hb, last_hb), 0))
    dc_spec = pl.BlockSpec((nj, 8, fb), lambda i, j: (0, 0, 0))
    u_shape = jax.ShapeDtypeStruct((FFN_GROUPS, n, fb), BF16)
    return pl.pallas_call(
        body, name="ffn_bwd", grid=(n // tm, nj),
        in_specs=[row, pl.BlockSpec((HALO, D_MODEL), lambda i, j: (jnp.minimum((i + 1) * hb, last_hb), 0)),
                  u_main, u_main, u_main, u_next, u_main, u_next, *_ffn_weight_specs()],
        out_specs=(u_main, u_main, u_main, row, dc_spec, dc_spec),
        out_shape=(u_shape, u_shape, u_shape, jax.ShapeDtypeStruct((n, D_MODEL), F32),
                   jax.ShapeDtypeStruct((nj, 8, fb), F32), jax.ShapeDtypeStruct((nj, 8, fb), F32)),
        scratch_shapes=[pltpu.VMEM((tm, D_MODEL), F32)],
        compiler_params=_cparams("arbitrary", "arbitrary"),
    )(dy, dy, ug, uv, pg, pg, pv, pv, w_up, w_up, conv, conv, w_down)


def _tn_grouped(a, b, name, shared_a, out_dtype=F32, tn=1024):
    groups = b.shape[0] if shared_a else a.shape[0]
    n_tok = a.shape[0] if shared_a else b.shape[0]
    k_dim, m_dim = a.shape[-1], b.shape[-1]

    def body(a_ref, b_ref, o_ref, acc):
        k = pl.program_id(1)
        a_t = a_ref[...] if shared_a else a_ref[0]
        b_t = b_ref[0] if shared_a else b_ref[...]
        part = _dot_tn(a_t.astype(BF16), b_t.astype(BF16))

        @pl.when(k == 0)
        def _():
            acc[...] = part

        @pl.when(k > 0)
        def _():
            acc[...] += part

        @pl.when(k == n_tok // tn - 1)
        def _():
            o_ref[0] = acc[...].astype(out_dtype)

    plain = lambda w: pl.BlockSpec((tn, w), lambda g, k: (k, 0))
    grouped = lambda w: pl.BlockSpec((1, tn, w), lambda g, k: (g, k, 0))
    return pl.pallas_call(
        body, name=name, grid=(groups, n_tok // tn),
        in_specs=[plain(k_dim), grouped(m_dim)] if shared_a else [grouped(k_dim), plain(m_dim)],
        out_specs=pl.BlockSpec((1, k_dim, m_dim), lambda g, k: (g, 0, 0)),
        out_shape=jax.ShapeDtypeStruct((groups, k_dim, m_dim), out_dtype),
        scratch_shapes=[pltpu.VMEM((k_dim, m_dim), F32)],
        compiler_params=_cparams("parallel", "arbitrary"),
    )(a, b)


def _s5_param_fn(lr, li, ldt, br, bi):
    dt = jnp.exp(ldt)
    mag = jnp.exp(lr * dt)
    ab_re = mag * jnp.cos(li * dt)
    ab_im = mag * jnp.sin(li * dt)
    nr = ab_re - 1.0
    ni = ab_im
    den = lr * lr + li * li
    q_re = (nr * lr + ni * li) / den
    q_im = (ni * lr - nr * li) / den
    bb_re = q_re * br - q_im * bi
    bb_im = q_re * bi + q_im * br
    return ab_re, ab_im, bb_re, bb_im


def _s5_param_fwd(lr, li, ldt, br, bi):
    def body(lr_ref, li_ref, ldt_ref, br_ref, bi_ref, ar_ref, ai_ref, bbr_ref, bbi_ref):
        ar, ai, bbr, bbi = _s5_param_fn(lr_ref[...], li_ref[...], ldt_ref[...], br_ref[...], bi_ref[...])
        ar_ref[...] = ar
        ai_ref[...] = ai
        bbr_ref[...] = bbr
        bbi_ref[...] = bbi

    return pl.pallas_call(
        body, name="s5_param_fwd",
        out_shape=(jax.ShapeDtypeStruct(lr.shape, F32), jax.ShapeDtypeStruct(lr.shape, F32),
                   jax.ShapeDtypeStruct(br.shape, F32), jax.ShapeDtypeStruct(br.shape, F32)),
    )(lr, li, ldt, br, bi)


def _s5_param_bwd(lr, li, ldt, br, bi, dar, dai, dbbr, dbbi):
    def body(lr_ref, li_ref, ldt_ref, br_ref, bi_ref, dar_ref, dai_ref, dbbr_ref, dbbi_ref,
             dlr_ref, dli_ref, dldt_ref, dbr_ref, dbi_ref):
        _, vjp = jax.vjp(_s5_param_fn, lr_ref[...], li_ref[...], ldt_ref[...], br_ref[...], bi_ref[...])
        dlr, dli, dldt, dbr, dbi = vjp((dar_ref[...], dai_ref[...], dbbr_ref[...], dbbi_ref[...]))
        dlr_ref[...] = dlr
        dli_ref[...] = dli
        dldt_ref[...] = dldt
        dbr_ref[...] = dbr
        dbi_ref[...] = dbi

    return pl.pallas_call(
        body, name="s5_param_bwd",
        out_shape=(jax.ShapeDtypeStruct(lr.shape, F32), jax.ShapeDtypeStruct(lr.shape, F32),
                   jax.ShapeDtypeStruct(ldt.shape, F32), jax.ShapeDtypeStruct(br.shape, F32),
                   jax.ShapeDtypeStruct(br.shape, F32)),
    )(lr, li, ldt, br, bi, dar, dai, dbbr, dbbi)


S5_CHUNK = 256
S5_STATES = 512
S5_BLOCKS = 4


def _cpow_rows(ar, ai, count):
    rs, im = [ar], [ai]
    for _ in range(count - 1):
        pr, pi = rs[-1], im[-1]
        rs.append(pr * ar - pi * ai)
        im.append(pr * ai + pi * ar)
    return rs, im


def _scan_in_groups(vr, vi, pr, pi, rm, reverse):
    n, width = vr.shape
    vr = vr.reshape(n // SUBLANES, SUBLANES, width)
    vi = vi.reshape(n // SUBLANES, SUBLANES, width)
    row = rm[0:SUBLANES]
    for k in (1, 2, 4):
        shift = SUBLANES - k if reverse else k
        keep = row < SUBLANES - k if reverse else row >= k
        kr = jnp.where(keep, pr[k - 1], 0.0)
        ki = jnp.where(keep, pi[k - 1], 0.0)
        sr, si = pltpu.roll(vr, shift, 1), pltpu.roll(vi, shift, 1)
        vr, vi = vr + kr * sr - ki * si, vi + kr * si + ki * sr
    return vr.reshape(n, width), vi.reshape(n, width)


def _carry_over_groups(xr_s, xi_s, wr, wi, c0r, c0i, reverse):
    groups = xr_s.shape[0] // SUBLANES
    pick = 0 if reverse else SUBLANES - 1

    def step(q, carry):
        cr, ci = carry
        r = groups - 1 - q if reverse else q
        o = pl.multiple_of(r * SUBLANES, SUBLANES)
        vr = xr_s[pl.ds(o, SUBLANES), :]
        vi = xi_s[pl.ds(o, SUBLANES), :]
        nr = vr + wr * cr - wi * ci
        ni = vi + wr * ci + wi * cr
        xr_s[pl.ds(o, SUBLANES), :] = nr
        xi_s[pl.ds(o, SUBLANES), :] = ni
        return (jnp.broadcast_to(nr[pick:pick + 1], nr.shape), jnp.broadcast_to(ni[pick:pick + 1], ni.shape))

    return lax.fori_loop(0, groups, step, (c0r, c0i))


def _s5_state_scan(u_b, bbr, bbi, pr, pi, rm, xr_s, xi_s, c0r, c0i):
    bur = _dot(u_b, bbr)
    bui = _dot(u_b, bbi)
    bur, bui = _scan_in_groups(bur, bui, pr, pi, rm, False)
    xr_s[...] = bur
    xi_s[...] = bui
    w8r = jnp.concatenate(pr, axis=0)
    w8i = jnp.concatenate(pi, axis=0)
    return _carry_over_groups(xr_s, xi_s, w8r, w8i, c0r, c0i, False)


def _s5_fwd(u, a_re, a_im, bbr, bbi, cr, ci, d_skip, n_seq):
    n = u.shape[0]
    seq_len = n // n_seq
    nt = seq_len // S5_CHUNK
    tc = S5_CHUNK

    def body(u_ref, ar_ref, ai_ref, bbr_ref, bbi_ref, cr_ref, ci_ref, d_ref, y_ref, str_ref, sti_ref,
             xr_s, xi_s, car_r, car_i):
        t = pl.program_id(2)

        @pl.when(t == 0)
        def _():
            car_r[...] = jnp.zeros_like(car_r)
            car_i[...] = jnp.zeros_like(car_i)
        pr, pi = _cpow_rows(ar_ref[0], ai_ref[0], SUBLANES)
        rm = lax.broadcasted_iota(jnp.int32, (tc, S5_STATES), 0) & (SUBLANES - 1)
        str_ref[0, 0] = car_r[...]
        sti_ref[0, 0] = car_i[...]
        u_t = u_ref[...]
        cfr, cfi = _s5_state_scan(u_t.astype(BF16), bbr_ref[0], bbi_ref[0], pr, pi, rm, xr_s, xi_s,
                                  car_r[...], car_i[...])
        car_r[...] = cfr
        car_i[...] = cfi
        y = _dot(xr_s[...].astype(BF16), cr_ref[0]) - _dot(xi_s[...].astype(BF16), ci_ref[0])
        y_ref[...] = y + d_ref[...] * u_t

    u_spec = pl.BlockSpec((tc, LANES), lambda cb, b, t: (b * nt + t, cb))
    a_spec = pl.BlockSpec((1, 1, S5_STATES), lambda cb, b, t: (cb, 0, 0))
    bb_spec = pl.BlockSpec((1, LANES, S5_STATES), lambda cb, b, t: (cb, 0, 0))
    c_spec = pl.BlockSpec((1, S5_STATES, LANES), lambda cb, b, t: (cb, 0, 0))
    st_spec = pl.BlockSpec((1, 1, SUBLANES, S5_STATES), lambda cb, b, t: (cb, b * nt + t, 0, 0))
    st_shape = jax.ShapeDtypeStruct((S5_BLOCKS, n_seq * nt, SUBLANES, S5_STATES), F32)
    return pl.pallas_call(
        body, name="s5_fwd", grid=(S5_BLOCKS, n_seq, nt),
        in_specs=[u_spec, a_spec, a_spec, bb_spec, bb_spec, c_spec, c_spec,
                  pl.BlockSpec((1, LANES), lambda cb, b, t: (0, cb))],
        out_specs=(u_spec, st_spec, st_spec),
        out_shape=(jax.ShapeDtypeStruct((n, D_SSM), F32), st_shape, st_shape),
        scratch_shapes=[pltpu.VMEM((tc, S5_STATES), F32), pltpu.VMEM((tc, S5_STATES), F32),
                        pltpu.VMEM((SUBLANES, S5_STATES), F32), pltpu.VMEM((SUBLANES, S5_STATES), F32)],
        compiler_params=_cparams("parallel", "arbitrary", "arbitrary"),
    )(u, a_re, a_im, bbr, bbi, cr, ci, d_skip)


def _s5_bwd(u, dy, st_r, st_i, a_re, a_im, bbr, bbi, cr, ci, d_skip, n_seq):
    n = u.shape[0]
    seq_len = n // n_seq
    nt = seq_len // S5_CHUNK
    tc = S5_CHUNK

    def body(u_ref, dy_ref, str_ref, sti_ref, ar_ref, ai_ref, bbr_ref, bbi_ref, cr_ref, ci_ref, d_ref,
             du_ref, dbbr_ref, dbbi_ref, dcr_ref, dci_ref, dar_ref, dai_ref, dd_ref,
             xr_s, xi_s, gr_s, gi_s, car_r, car_i):
        b, t = pl.program_id(1), pl.program_id(2)

        @pl.when((b == 0) & (t == 0))
        def _():
            for ref in (dbbr_ref, dbbi_ref, dcr_ref, dci_ref, dar_ref, dai_ref, dd_ref):
                ref[...] = jnp.zeros_like(ref)

        @pl.when(t == 0)
        def _():
            car_r[...] = jnp.zeros_like(car_r)
            car_i[...] = jnp.zeros_like(car_i)
        ar, ai = ar_ref[0], ai_ref[0]
        pr, pi = _cpow_rows(ar, ai, SUBLANES)
        row = lax.broadcasted_iota(jnp.int32, (tc, S5_STATES), 0)
        rm = row & (SUBLANES - 1)
        u_t = u_ref[...]
        u_b = u_t.astype(BF16)
        dy_t = dy_ref[...]
        dy_b = dy_t.astype(BF16)
        s0r, s0i = str_ref[0, 0], sti_ref[0, 0]
        _s5_state_scan(u_b, bbr_ref[0], bbi_ref[0], pr, pi, rm, xr_s, xi_s, s0r, s0i)
        xr, xi = xr_s[...], xi_s[...]
        gr = _dot_nt(dy_b, cr_ref[0])
        gi = -_dot_nt(dy_b, ci_ref[0])
        npi = [-v for v in pi]
        gr, gi = _scan_in_groups(gr, gi, pr, npi, rm, True)
        gr_s[...] = gr
        gi_s[...] = gi
        w8r = jnp.concatenate(pr[::-1], axis=0)
        w8i = jnp.concatenate(npi[::-1], axis=0)
        cfr, cfi = _carry_over_groups(gr_s, gi_s, w8r, w8i, car_r[...], car_i[...], True)
        car_r[...] = cfr
        car_i[...] = cfi
        gr, gi = gr_s[...], gi_s[...]
        gr_b, gi_b = gr.astype(BF16), gi.astype(BF16)
        du_ref[...] = _dot_nt(gr_b, bbr_ref[0]) + _dot_nt(gi_b, bbi_ref[0]) + d_ref[...] * dy_t
        dbbr_ref[0] += _dot_tn(u_b, gr_b)
        dbbi_ref[0] += _dot_tn(u_b, gi_b)
        dcr_ref[0] += _dot_tn(xr.astype(BF16), dy_b)
        dci_ref[0] -= _dot_tn(xi.astype(BF16), dy_b)
        dd_ref[0] += jnp.sum((dy_t * u_t).reshape(tc // SUBLANES, SUBLANES, LANES), axis=0)
        first = row == 0
        xpr = jnp.where(first, jnp.broadcast_to(s0r[0:1], xr.shape), pltpu.roll(xr, 1, 0))
        xpi = jnp.where(first, jnp.broadcast_to(s0i[0:1], xi.shape), pltpu.roll(xi, 1, 0))
        shp = (tc // SUBLANES, SUBLANES, S5_STATES)
        dar_ref[0] += jnp.sum((gr * xpr + gi * xpi).reshape(shp), axis=0)
        dai_ref[0] += jnp.sum((gi * xpr - gr * xpi).reshape(shp), axis=0)

    u_spec = pl.BlockSpec((tc, LANES), lambda cb, b, t: (b * nt + nt - 1 - t, cb))
    a_spec = pl.BlockSpec((1, 1, S5_STATES), lambda cb, b, t: (cb, 0, 0))
    bb_spec = pl.BlockSpec((1, LANES, S5_STATES), lambda cb, b, t: (cb, 0, 0))
    c_spec = pl.BlockSpec((1, S5_STATES, LANES), lambda cb, b, t: (cb, 0, 0))
    st_spec = pl.BlockSpec((1, 1, SUBLANES, S5_STATES), lambda cb, b, t: (cb, b * nt + nt - 1 - t, 0, 0))
    da_spec = pl.BlockSpec((1, SUBLANES, S5_STATES), lambda cb, b, t: (cb, 0, 0))
    dd_spec = pl.BlockSpec((1, SUBLANES, LANES), lambda cb, b, t: (cb, 0, 0))
    big = pltpu.VMEM((tc, S5_STATES), F32)
    small = pltpu.VMEM((SUBLANES, S5_STATES), F32)
    return pl.pallas_call(
        body, name="s5_bwd", grid=(S5_BLOCKS, n_seq, nt),
        in_specs=[u_spec, u_spec, st_spec, st_spec, a_spec, a_spec, bb_spec, bb_spec, c_spec, c_spec,
                  pl.BlockSpec((1, LANES), lambda cb, b, t: (0, cb))],
        out_specs=(u_spec, bb_spec, bb_spec, c_spec, c_spec, da_spec, da_spec, dd_spec),
        out_shape=(jax.ShapeDtypeStruct((n, D_SSM), F32),
                   jax.ShapeDtypeStruct((S5_BLOCKS, LANES, S5_STATES), F32),
                   jax.ShapeDtypeStruct((S5_BLOCKS, LANES, S5_STATES), F32),
                   jax.ShapeDtypeStruct((S5_BLOCKS, S5_STATES, LANES), F32),
                   jax.ShapeDtypeStruct((S5_BLOCKS, S5_STATES, LANES), F32),
                   jax.ShapeDtypeStruct((S5_BLOCKS, SUBLANES, S5_STATES), F32),
                   jax.ShapeDtypeStruct((S5_BLOCKS, SUBLANES, S5_STATES), F32),
                   jax.ShapeDtypeStruct((S5_BLOCKS, SUBLANES, LANES), F32)),
        scratch_shapes=[big, big, big, big, small, small],
        compiler_params=_cparams("parallel", "arbitrary", "arbitrary"),
    )(u, dy, st_r, st_i, a_re, a_im, bbr, bbi, cr, ci, d_skip)


CUM_BLOCK = 128


def _tri(lower):
    r = lax.broadcasted_iota(jnp.int32, (CUM_BLOCK, CUM_BLOCK), 0)
    c = lax.broadcasted_iota(jnp.int32, (CUM_BLOCK, CUM_BLOCK), 1)
    return jnp.where(r >= c if lower else r <= c, 1.0, 0.0).astype(F32)


def _fprep_fwd(fl, bf, n_seq):
    n = fl.shape[0]
    seq_len = n // n_seq
    nb = seq_len // CUM_BLOCK

    def body(fl_ref, bf_ref, cum_ref):
        tril = _tri(True)
        carry = jnp.zeros((1, LANES), F32)
        for blk in range(nb):
            rows = slice(blk * CUM_BLOCK, (blk + 1) * CUM_BLOCK)
            lf = jax.nn.log_sigmoid(fl_ref[rows, :] + bf_ref[...])
            cs = jnp.dot(tril, lf, preferred_element_type=F32, precision=HIGHEST) + carry
            cum_ref[rows, :] = cs
            carry = cs[CUM_BLOCK - 1:CUM_BLOCK, :]

    spec = pl.BlockSpec((seq_len, LANES), lambda b: (b, 0))
    return pl.pallas_call(
        body, name="fprep_fwd", grid=(n_seq,), in_specs=[spec, pl.BlockSpec((1, LANES), lambda b: (0, 0))],
        out_specs=spec, out_shape=jax.ShapeDtypeStruct((n, LANES), F32), compiler_params=_cparams("parallel"),
    )(fl, bf)


def _fprep_bwd(dcum, fl, bf, n_seq):
    n = fl.shape[0]
    seq_len = n // n_seq
    nb = seq_len // CUM_BLOCK

    def body(dcum_ref, fl_ref, bf_ref, dfl_ref, dbf_ref):
        triu = _tri(False)
        lane = lax.broadcasted_iota(jnp.int32, (CUM_BLOCK, LANES), 1)
        carry = jnp.zeros((1, LANES), F32)
        total = jnp.zeros((1, LANES), F32)
        for blk in reversed(range(nb)):
            rows = slice(blk * CUM_BLOCK, (blk + 1) * CUM_BLOCK)
            rs = jnp.dot(triu, dcum_ref[rows, :], preferred_element_type=F32, precision=HIGHEST) + carry
            carry = rs[0:1, :]
            _, vjp = jax.vjp(jax.nn.log_sigmoid, fl_ref[rows, :] + bf_ref[...])
            dz = jnp.where(lane < N_HEADS, vjp(rs)[0], 0.0)
            dfl_ref[rows, :] = dz
            total = total + jnp.sum(dz, axis=0, keepdims=True)
        dbf_ref[0] = total

    spec = pl.BlockSpec((seq_len, LANES), lambda b: (b, 0))
    return pl.pallas_call(
        body, name="fprep_bwd", grid=(n_seq,), in_specs=[spec, spec, pl.BlockSpec((1, LANES), lambda b: (0, 0))],
        out_specs=(spec, pl.BlockSpec((1, 1, LANES), lambda b: (b, 0, 0))),
        out_shape=(jax.ShapeDtypeStruct((n, LANES), F32), jax.ShapeDtypeStruct((n_seq, 1, LANES), F32)),
        compiler_params=_cparams("parallel"),
    )(dcum, fl, bf)


ATT_TQ = 256
ATT_KSTEP = 256
ATT_SCALE = HEAD_DIM ** -0.5
NEG_BIG = -1e30


assert ATT_KSTEP == ATT_TQ


def _scores(q_scaled, kb, cq, ck, kend):
    s = _dot_nt(q_scaled, kb) + cq - ck
    r = lax.broadcasted_iota(jnp.int32, (ATT_TQ, ATT_TQ), 0)
    c = lax.broadcasted_iota(jnp.int32, (ATT_TQ, ATT_TQ), 1)
    diag = jnp.where(r >= c, s[:, kend - ATT_TQ:], NEG_BIG)
    return diag if kend == ATT_TQ else jnp.concatenate([s[:, :kend - ATT_TQ], diag], axis=1)


def _attn_specs(n_seq, seq_len):
    nq = seq_len // ATT_TQ
    q_spec = pl.BlockSpec((ATT_TQ, LANES), lambda b, h, q: (b * nq + q, h))
    k_spec = pl.BlockSpec((seq_len, LANES), lambda b, h, q: (b, N_HEADS // 2 + h))
    v_spec = pl.BlockSpec((seq_len, LANES), lambda b, h, q: (b, N_HEADS + h))
    cq_spec = pl.BlockSpec((1, 2, ATT_TQ, 1), lambda b, h, q: (b, h, q, 0))
    ck_spec = pl.BlockSpec((1, 2, 1, seq_len), lambda b, h, q: (b, h, 0, 0))
    return nq, q_spec, k_spec, v_spec, cq_spec, ck_spec


def _head_selectors():
    head0 = lax.broadcasted_iota(jnp.int32, (1, LANES), 1) < HEAD_DIM
    return head0, (head0, jnp.logical_not(head0))


def _for_key_range(qi, seq_len, run):
    per = ATT_KSTEP // ATT_TQ
    for g in range(seq_len // ATT_KSTEP):
        pl.when(qi // per == g)(functools.partial(run, (g + 1) * ATT_KSTEP))


def _attn_fwd(qkv, cq, ck, n_seq):
    n = qkv.shape[0]
    seq_len = n // n_seq
    nq, q_spec, k_spec, v_spec, cq_spec, ck_spec = _attn_specs(n_seq, seq_len)

    def body(q_ref, k_ref, v_ref, cq_ref, ck_ref, o_ref, lse_ref):
        qi = pl.program_id(2)
        q2 = q_ref[...]
        head0, sels = _head_selectors()
        qe = [jnp.where(sel, q2 * ATT_SCALE, 0.0).astype(BF16) for sel in sels]

        def run(kend):
            kb = k_ref[0:kend, :].astype(BF16)
            vb = v_ref[0:kend, :].astype(BF16)
            outs = []
            for e in range(2):
                s = _scores(qe[e], kb, cq_ref[0, e], ck_ref[0, e, :, 0:kend], kend)
                mx = jnp.max(s, axis=1, keepdims=True)
                p = jnp.exp(s - mx)
                den = jnp.sum(p, axis=1, keepdims=True)
                outs.append(_dot(p.astype(BF16), vb) / den)
                lse_ref[0, e] = mx + jnp.log(den)
            o_ref[...] = jnp.where(head0, outs[0], outs[1])

        _for_key_range(qi, seq_len, run)

    return pl.pallas_call(
        body, name="attn_fwd", grid=(n_seq, N_HEADS // 2, nq),
        in_specs=[q_spec, k_spec, v_spec, cq_spec, ck_spec],
        out_specs=(q_spec, cq_spec),
        out_shape=(jax.ShapeDtypeStruct((n, D_ATTN), F32), jax.ShapeDtypeStruct((n_seq, N_HEADS, seq_len, 1), F32)),
        compiler_params=_cparams("parallel", "parallel", "parallel"),
    )(qkv, qkv, qkv, cq, ck)


def _attn_bwd(qkv, cq, ck, o, do, lse, n_seq):
    n = qkv.shape[0]
    seq_len = n // n_seq
    nq, q_spec, k_spec, v_spec, cq_spec, ck_spec = _attn_specs(n_seq, seq_len)
    kv_out = pl.BlockSpec((seq_len, LANES), lambda b, h, q: (b, h))

    def body(q_ref, k_ref, v_ref, cq_ref, ck_ref, o_ref, do_ref, lse_ref, dq_ref, dk_ref, dv_ref, dcq_ref, dck_ref):
        qi = pl.program_id(2)

        @pl.when(qi == 0)
        def _():
            dk_ref[...] = jnp.zeros_like(dk_ref)
            dv_ref[...] = jnp.zeros_like(dv_ref)
            dck_ref[...] = jnp.zeros_like(dck_ref)
        q2 = q_ref[...]
        do2 = do_ref[...]
        o2 = o_ref[...]
        head0, sels = _head_selectors()
        qe = [jnp.where(sel, q2 * ATT_SCALE, 0.0).astype(BF16) for sel in sels]
        doe = [jnp.where(sel, do2, 0.0) for sel in sels]
        doe_b = [d.astype(BF16) for d in doe]
        delta = [jnp.sum(d * o2, axis=1, keepdims=True) for d in doe]

        def run(kend):
            kb = k_ref[0:kend, :].astype(BF16)
            vb = v_ref[0:kend, :].astype(BF16)
            dqs = []
            dk = jnp.zeros((kend, LANES), F32)
            dv = jnp.zeros((kend, LANES), F32)
            for e in range(2):
                s = _scores(qe[e], kb, cq_ref[0, e], ck_ref[0, e, :, 0:kend], kend)
                p = jnp.exp(s - lse_ref[0, e])
                ds = p * (_dot_nt(doe_b[e], vb) - delta[e])
                ds_b = ds.astype(BF16)
                dqs.append(_dot(ds_b, kb))
                dk = dk + _dot_tn(ds_b, qe[e])
                dv = dv + _dot_tn(p.astype(BF16), doe_b[e])
                dcq_ref[0, e] = jnp.sum(ds, axis=1, keepdims=True)
                dck_ref[0, e, :, 0:kend] -= jnp.sum(ds, axis=0, keepdims=True)
            dk_ref[0:kend, :] += dk
            dv_ref[0:kend, :] += dv
            dq_ref[...] = jnp.where(head0, dqs[0], dqs[1]) * ATT_SCALE

        _for_key_range(qi, seq_len, run)

    return pl.pallas_call(
        body, name="attn_bwd", grid=(n_seq, N_HEADS // 2, nq),
        in_specs=[q_spec, k_spec, v_spec, cq_spec, ck_spec, q_spec, q_spec, cq_spec],
        out_specs=(q_spec, kv_out, kv_out, cq_spec, ck_spec),
        out_shape=(jax.ShapeDtypeStruct((n, D_ATTN), F32), jax.ShapeDtypeStruct((n, D_ATTN), F32),
                   jax.ShapeDtypeStruct((n, D_ATTN), F32),
                   jax.ShapeDtypeStruct((n_seq, N_HEADS, seq_len, 1), F32),
                   jax.ShapeDtypeStruct((n_seq, N_HEADS, 1, seq_len), F32)),
        compiler_params=_cparams("parallel", "parallel", "arbitrary"),
    )(qkv, qkv, qkv, cq, ck, o, do, lse)


WEIGHT_NAMES = ("norm_mix", "w_in", "b_forget", "lam_re", "lam_im", "b_re", "b_im", "c_re", "c_im", "d_skip", "log_dt",
                "w_glu", "b_glu", "q_norm", "k_norm", "norm_out_ssm", "norm_out_attn", "w_out", "norm_ffn", "w_up",
                "conv_w", "conv_b", "w_down")
SHARDED = ("w_in", "w_glu", "w_out", "w_up", "conv_w", "w_down")
ADAM_TILE = {"w_in": (257, 256), "w_glu": (64, 512), "w_out": (128, 1024), "w_up": (688, 256), "conv_w": (3, 688),
             "w_down": (344, 1024)}
PACK_ROWS = SUBLANES * LANES
ROWS_DOWN = D_FF // N_DEV
ROWS_OUT = D_MODEL // N_DEV
ROWS_GLU = D_SSM * D_SSM // N_DEV // D_MODEL


def _after_all(*arrays):
    return sum(a.reshape(-1)[0].astype(F32) for a in arrays).reshape(1, 1)


def _pad_to(a, axis, size):
    pad = [(0, 0)] * a.ndim
    pad[axis] = (0, size - a.shape[axis])
    return jnp.pad(a, pad)


def _block_diag(t, transpose):
    t4 = t.reshape(S5_BLOCKS, 8, SSM_GROUP, SSM_STATE)
    eye = jnp.eye(8, dtype=t.dtype)
    if transpose:
        e = jnp.swapaxes(t4, 2, 3)[:, :, :, None, :] * eye[None, :, None, :, None]
        return e.reshape(S5_BLOCKS, S5_STATES, LANES)
    e = t4[:, :, :, None, :] * eye[None, :, None, :, None]
    return e.reshape(S5_BLOCKS, LANES, S5_STATES)


def _block_diag_extract(m, transpose):
    if transpose:
        m5 = m.reshape(S5_BLOCKS, 8, SSM_STATE, 8, SSM_GROUP)
        d = jnp.stack([m5[:, i, :, i, :] for i in range(8)], axis=1)
        return jnp.swapaxes(d, 2, 3).reshape(N_GROUPS, SSM_GROUP, SSM_STATE)
    m5 = m.reshape(S5_BLOCKS, 8, SSM_GROUP, 8, SSM_STATE)
    d = jnp.stack([m5[:, i, :, i, :] for i in range(8)], axis=1)
    return d.reshape(N_GROUPS, SSM_GROUP, SSM_STATE)


def _pack(pieces):
    flat = jnp.concatenate([p.reshape(-1).astype(F32) for p in pieces])
    size = -(-flat.shape[0] // PACK_ROWS) * PACK_ROWS
    return _pad_to(flat, 0, size).reshape(-1, LANES)


def _unpack(packed, shapes):
    flat = packed.reshape(-1)
    out, off = [], 0
    for shp in shapes:
        size = math.prod(shp)
        out.append(flat[off:off + size].reshape(shp))
        off += size
    return out


def kernel(x, norm_mix, w_in, b_forget, lam_re, lam_im, b_re, b_im, c_re, c_im, d_skip, log_dt, w_glu, b_glu, q_norm, k_norm, norm_out_ssm, norm_out_attn, w_out, norm_ffn, w_up, conv_w, conv_b, w_down, loss_target, m_norm_mix, m_w_in, m_b_forget, m_lam_re, m_lam_im, m_b_re, m_b_im, m_c_re, m_c_im, m_d_skip, m_log_dt, m_w_glu, m_b_glu, m_q_norm, m_k_norm, m_norm_out_ssm, m_norm_out_attn, m_w_out, m_norm_ffn, m_w_up, m_conv_w, m_conv_b, m_w_down, v_norm_mix, v_w_in, v_b_forget, v_lam_re, v_lam_im, v_b_re, v_b_im, v_c_re, v_c_im, v_d_skip, v_log_dt, v_w_glu, v_b_glu, v_q_norm, v_k_norm, v_norm_out_ssm, v_norm_out_attn, v_w_out, v_norm_ffn, v_w_up, v_conv_w, v_conv_b, v_w_down):
    given = dict(locals())
    weights = {k: given[k] for k in WEIGHT_NAMES}
    mom1 = {k: given["m_" + k] for k in WEIGHT_NAMES}
    mom2 = {k: given["v_" + k] for k in WEIGHT_NAMES}
    n_seq, seq_len, _ = x.shape
    n = n_seq * seq_len
    xf = x.reshape(n, D_MODEL)
    target = loss_target.reshape(n, D_MODEL)
    me_idx = 4 * lax.axis_index("x") + 2 * lax.axis_index("y") + lax.axis_index("c")

    in_flags = [False] * 2
    in_sems = _exchange_start([jnp.swapaxes(w_in[0], 0, 1).astype(BF16), conv_w[0]], in_flags, norm_mix,
                              "gather_in_start", 3)
    my_slot = lax.broadcasted_iota(jnp.int32, (N_DEV, 1, 1), 0) == me_idx

    lr3 = lam_re[0].reshape(N_GROUPS, 1, SSM_STATE)
    li3 = lam_im[0].reshape(N_GROUPS, 1, SSM_STATE)
    ldt3 = log_dt[0].reshape(N_GROUPS, 1, 1)
    br_t = jnp.swapaxes(b_re[0], 1, 2)
    bi_t = jnp.swapaxes(b_im[0], 1, 2)
    ab_re, ab_im, bb_re, bb_im = _s5_param_fwd(lr3, li3, ldt3, br_t, bi_t)
    a_re = ab_re.reshape(S5_BLOCKS, 1, S5_STATES)
    a_im = ab_im.reshape(S5_BLOCKS, 1, S5_STATES)
    bbr = _block_diag(bb_re, False).astype(BF16)
    bbi = _block_diag(bb_im, False).astype(BF16)
    cr = _block_diag(c_re[0], True).astype(BF16)
    ci = _block_diag(c_im[0], True).astype(BF16)
    avg = jnp.kron(jnp.eye(N_HEADS, dtype=F32), jnp.full((HEAD_DIM, HEAD_DIM), 1.0 / HEAD_DIM, F32)).astype(BF16)
    qg = jnp.tile(q_norm, (1, N_HEADS))
    kg = jnp.tile(k_norm, (1, N_HEADS))
    rows_w = jnp.concatenate([w_down[0], w_out[0], w_glu[0].reshape(ROWS_GLU, D_MODEL)], axis=0).astype(BF16)

    (own_in, own_cw), (g_in, g_cw) = _exchange_wait(in_sems[0], in_sems[1], in_sems[2], in_sems[3], in_flags,
                                                    _after_all(a_re, a_im, bbr, bbi, cr, ci, avg, qg, kg, rows_w),
                                                    "gather_in_wait")
    g_in = jnp.where(my_slot, own_in[None], g_in)
    g_cw = jnp.where(my_slot, own_cw[None], g_cw)
    rest_flags = [False] * 2
    w_sems = _exchange_start([rows_w, jnp.swapaxes(w_up[0], 0, 1).astype(BF16)], rest_flags, g_in,
                             "gather_rest_start", 0)
    norm_mix = norm_mix + w_sems[4][0, 0]
    w_in_p = _pad_to(g_in.reshape(D_IN, D_MODEL), 0, D_IN_PAD)

    hn, u, qkv, raw, fl = _inproj_fwd(xf, norm_mix, w_in_p, avg, qg, kg)
    yc, st_r, st_i = _s5_fwd(u, a_re, a_im, bbr, bbi, cr, ci, d_skip, n_seq)
    bf = _pad_to(b_forget, 1, LANES)
    cum = _fprep_fwd(fl, bf, n_seq)
    cum8 = jnp.swapaxes(cum[:, :N_HEADS].reshape(n_seq, seq_len, N_HEADS), 1, 2)
    cq = cum8[:, :, :, None]
    ck = cum8[:, :, None, :]
    ya, lse = _attn_fwd(qkv, cq, ck, n_seq)
    (own_rows, own_up), (g_rows, g_up) = _exchange_wait(w_sems[0], w_sems[1], w_sems[2], w_sems[3], rest_flags, ya,
                                                        "gather_rest_wait")
    g_rows = jnp.where(my_slot, own_rows[None], g_rows)
    g_up = jnp.where(my_slot, own_up[None], g_up)
    g_down = g_rows[:, :ROWS_DOWN]
    g_out = g_rows[:, ROWS_DOWN:ROWS_DOWN + ROWS_OUT]
    g_glu = g_rows[:, ROWS_DOWN + ROWS_OUT:]
    w_glu_f = g_glu.reshape(D_SSM, D_SSM)
    w_out_f = g_out.reshape(D_MODEL, D_MODEL)
    conv_st = _pad_to(jnp.concatenate([g_cw, conv_b.reshape(N_DEV, 1, -1)], axis=1), 1, SUBLANES)
    w_down4 = g_down.reshape(FFN_GROUPS, FFN_GROUP, D_MODEL)
    ys = _glu_fwd(yc, w_glu_f, b_glu)
    h1, hn2, mixed = _mix_fwd(xf, ys, ya, norm_out_ssm, norm_out_attn, w_out_f, norm_ffn)
    ug, uv, pg, pv, dy, loss_part = _ffn_fwd(hn2, h1, target, g_up, conv_st, w_down4, seq_len)
    loss_local = 0.5 * jnp.sum(loss_part) / D_MODEL

    dug, duv, act, dhn2, dcg, dcv = _ffn_bwd(dy, ug, uv, pg, pv, g_up, conv_st, w_down4, seq_len)
    dh1, dys, dya, d_gs, d_ga, d_gf = _mix_bwd(dy, dhn2[None], h1, ys, ya, norm_out_ssm, norm_out_attn, w_out_f, norm_ffn)
    dyc, gl_b, dz_b, d_bglu = _glu_bwd(yc, dys, w_glu_f, b_glu)

    gw_glu = _tn_matmul(gl_b, dz_b, "dw_glu", D_SSM, D_SSM)
    gw_out = _tn_matmul(mixed, dh1, "dw_out", D_MODEL, D_MODEL)
    gw_up = jnp.concatenate([_tn_grouped(dug, hn2, "dw_up_gate", False, BF16),
                             _tn_grouped(duv, hn2, "dw_up_val", False, BF16)], axis=0)
    gw_down = _tn_grouped(act, dy, "dw_down", False)
    g_conv = jnp.concatenate([dcg, dcv], axis=0)
    by_cols = lambda g, c: jnp.swapaxes(g.reshape(g.shape[0], N_DEV, c), 0, 1)
    early_flags = [True] * 2
    rows_g = jnp.concatenate([gw_down.reshape(N_DEV, ROWS_DOWN, D_MODEL), gw_out.reshape(N_DEV, ROWS_OUT, D_MODEL),
                              gw_glu.reshape(N_DEV, ROWS_GLU, D_MODEL)], axis=1).astype(BF16)
    g_sems = _exchange_start([rows_g, gw_up], early_flags, dyc, "grad_early_start", 1)
    started = g_sems[4][0, 0]

    du, dbbr, dbbi, dcr, dci, dar, dai, ddk = _s5_bwd(u, dyc, st_r, st_i, a_re, a_im, bbr, bbi, cr, ci,
                                                      d_skip + started, n_seq)
    partial_early = {
        "ab_re": jnp.sum(dar, axis=1), "ab_im": jnp.sum(dai, axis=1),
        "bb_re": _block_diag_extract(dbbr, False), "bb_im": _block_diag_extract(dbbi, False),
        "c_re": _block_diag_extract(dcr, True), "c_im": _block_diag_extract(dci, True),
        "d_skip": jnp.sum(ddk, axis=1), "b_glu": d_bglu,
        "norm_out_ssm": d_gs, "norm_out_attn": d_ga, "norm_ffn": d_gf, "conv_b": g_conv[:, 3],
    }
    early_keys = tuple(partial_early)
    early_shapes = [partial_early[k].shape for k in early_keys]
    p_sems = _exchange_start([_pack([partial_early[k] for k in early_keys])], [False], du, "small_early_start", 2)
    started = started + p_sems[4][0, 0]

    dqn, dkn, dv, dcq, dck = _attn_bwd(qkv, cq, ck + started, ya, dya, lse, n_seq)
    dcum8 = dcq[:, :, :, 0] + dck.reshape(n_seq, N_HEADS, seq_len)
    dcum = _pad_to(jnp.swapaxes(dcum8, 1, 2).reshape(n, N_HEADS), 1, LANES)
    dfl, dbf = _fprep_bwd(dcum, fl, bf, n_seq)
    dx, dproj, d_gmix, d_qg, d_kg = _inproj_bwd(xf, norm_mix, w_in_p, avg, qg, kg, raw, du, dqn, dkn, dv, dfl, dh1)

    gw_in = _tn_matmul(dproj, hn, "dw_in", D_IN_PAD, D_MODEL, out_rows=D_IN)
    partial_late = {
        "norm_mix": d_gmix, "b_forget": jnp.sum(dbf, axis=(0, 1))[:N_HEADS],
        "q_norm": jnp.sum(d_qg.reshape(N_HEADS, HEAD_DIM), axis=0),
        "k_norm": jnp.sum(d_kg.reshape(N_HEADS, HEAD_DIM), axis=0), "loss": loss_local.reshape(1),
    }
    late_keys = tuple(partial_late)
    late_shapes = [partial_late[k].shape for k in late_keys]

    late_flags = [True, True, False]
    l_sems = _exchange_start(
        [gw_in.reshape(N_DEV, D_IN // N_DEV, D_MODEL).astype(BF16), g_conv[:, :3],
         _pack([partial_late[k] for k in late_keys])],
        late_flags, dx, "grad_late_start", 4)
    (src_rows, src_up), (land_rows, land_up) = _exchange_wait(g_sems[0], g_sems[1], g_sems[2], g_sems[3], early_flags,
                                                              l_sems[4], "grad_early_wait")
    own_rows = lax.dynamic_index_in_dim(src_rows, me_idx, 0, keepdims=False)
    land = {"w_up": land_up, "w_down": land_rows[:, :ROWS_DOWN], "w_out": land_rows[:, ROWS_DOWN:ROWS_DOWN + ROWS_OUT],
            "w_glu": land_rows[:, ROWS_DOWN + ROWS_OUT:].reshape(N_DEV, -1, D_SSM)}
    own = {"w_up": lax.dynamic_index_in_dim(src_up, me_idx, 0, keepdims=False), "w_down": own_rows[:ROWS_DOWN],
           "w_out": own_rows[ROWS_DOWN:ROWS_DOWN + ROWS_OUT], "w_glu": own_rows[ROWS_DOWN + ROWS_OUT:].reshape(-1, D_SSM)}
    grads, deltas, new_m, new_v = {}, {}, {}, {}

    def adam_shard(name):
        flip = (lambda a: jnp.swapaxes(a, 1, 2)) if name in ("w_in", "w_up") else (lambda a: a)
        outs = _adam_sharded(land[name], own[name], flip(weights[name]), flip(mom1[name]), flip(mom2[name]),
                             "adam_" + name, ADAM_TILE[name])
        grads[name], deltas[name], new_m[name], new_v[name] = [flip(o) for o in outs]

    for name in ("w_up", "w_down", "w_out", "w_glu"):
        adam_shard(name)
    (own_pack,), (early_parts,) = _exchange_wait(p_sems[0], p_sems[1], p_sems[2], p_sems[3], [False], land_up,
                                                 "small_early_wait")
    early_sum = _sum_partials(early_parts, own_pack, "sum_early_partials")
    (src_in, src_cw, own_late), (land["w_in"], land["conv_w"], late_parts) = _exchange_wait(
        l_sems[0], l_sems[1], l_sems[2], l_sems[3], late_flags,
        _after_all(early_sum, *[new_v[k] for k in ("w_up", "w_down", "w_out", "w_glu")]), "grad_late_wait")
    own["w_in"] = lax.dynamic_index_in_dim(src_in, me_idx, 0, keepdims=False)
    own["conv_w"] = lax.dynamic_index_in_dim(src_cw, me_idx, 0, keepdims=False)
    for name in ("w_in", "conv_w"):
        adam_shard(name)

    summed = dict(zip(late_keys, _unpack(_sum_partials(late_parts, own_late, "sum_late_partials"), late_shapes)))
    summed.update(zip(early_keys, _unpack(early_sum, early_shapes)))
    dlr, dli, dldt, dbr_t, dbi_t = _s5_param_bwd(
        lr3, li3, ldt3, br_t, bi_t, summed["ab_re"].reshape(lr3.shape), summed["ab_im"].reshape(lr3.shape),
        summed["bb_re"], summed["bb_im"])
    small_grads = {
        "norm_mix": summed["norm_mix"], "b_forget": summed["b_forget"], "lam_re": dlr, "lam_im": dli,
        "b_re": dbr_t, "b_im": dbi_t, "c_re": summed["c_re"], "c_im": summed["c_im"],
        "d_skip": summed["d_skip"], "log_dt": dldt, "b_glu": summed["b_glu"], "q_norm": summed["q_norm"],
        "k_norm": summed["k_norm"], "norm_out_ssm": summed["norm_out_ssm"], "norm_out_attn": summed["norm_out_attn"],
        "norm_ffn": summed["norm_ffn"], "conv_b": summed["conv_b"],
    }
    repl = tuple(k for k in WEIGHT_NAMES if k not in SHARDED)
    turn = lambda k, a: jnp.swapaxes(a, 2, 3) if k in ("b_re", "b_im") else a
    w_list = [turn(k, weights[k]) for k in repl]
    g_list = [small_grads[k].reshape(w.shape) for k, w in zip(repl, w_list)]
    d_list, m_list, v_list = _adam_replicated(g_list, w_list, [turn(k, mom1[k]) for k in repl],
                                              [turn(k, mom2[k]) for k in repl], "adam_replicated")
    for k, g, d, nm, nv in zip(repl, g_list, d_list, m_list, v_list):
        grads[k], deltas[k], new_m[k], new_v[k] = turn(k, g), turn(k, d), turn(k, nm), turn(k, nv)

    grad_x = dx.reshape(x.shape)
    loss = summed["loss"].reshape(())
    return (loss, grad_x, *[grads[k] for k in WEIGHT_NAMES], *[deltas[k] for k in WEIGHT_NAMES],
            *[new_m[k] for k in WEIGHT_NAMES], *[new_v[k] for k in WEIGHT_NAMES])
```

```python
import functools
import math

import jax
import jax.numpy as jnp
from jax import lax
from jax.experimental import pallas as pl
from jax.experimental.pallas import tpu as pltpu

F32 = jnp.float32
BF16 = jnp.bfloat16
HIGHEST = lax.Precision.HIGHEST

N_DEV = 8
D_MODEL = 1024
D_SSM = 512
D_ATTN = 512
N_HEADS = 8
HEAD_DIM = 64
N_GROUPS = 32
SSM_GROUP = 16
SSM_STATE = 64
D_FF = 2752
D_FF_PAD = 2816
D_IN = 2056
D_IN_PAD = 2176
EPS = 1e-6
LANES = 128
SUBLANES = 8
VMEM_LIMIT = 56 * 1024 * 1024

ADAM_LR = 0.001
ADAM_B1 = 0.9
ADAM_B2 = 0.999
ADAM_EPS = 1e-08
ADAM_WD = 0.01
ADAM_STEP = 10


def _cparams(*sem):
    return pltpu.CompilerParams(dimension_semantics=sem, vmem_limit_bytes=VMEM_LIMIT)


def _dot(a, b, **kw):
    return jnp.dot(a, b, preferred_element_type=F32, **kw)


def _dot_nt(a, b):
    return lax.dot_general(a, b, (((1,), (1,)), ((), ())), preferred_element_type=F32)


def _dot_tn(a, b):
    return lax.dot_general(a, b, (((0,), (0,)), ((), ())), preferred_element_type=F32)


def _rms(x, g):
    return x * lax.rsqrt(jnp.mean(x * x, axis=-1, keepdims=True) + EPS) * g


def _split_dot(x, avg):
    hi = x.astype(BF16)
    lo = (x - hi.astype(F32)).astype(BF16)
    return _dot(hi, avg) + _dot(lo, avg)


@jax.custom_vjp
def _group_mean(x, avg):
    return _split_dot(x, avg)


def _group_mean_fwd(x, avg):
    return _split_dot(x, avg), avg


def _group_mean_bwd(avg, ct):
    return _split_dot(ct, avg), jnp.zeros_like(avg)


_group_mean.defvjp(_group_mean_fwd, _group_mean_bwd)


def _headnorm(q, avg, g):
    return q * lax.rsqrt(_group_mean(q * q, avg) + EPS) * g


def _exchange(srcs, scatter_flags, name):
    n = len(srcs)
    out_shape = []
    for s, sc in zip(srcs, scatter_flags):
        shp = s.shape if sc else (N_DEV,) + s.shape
        out_shape.append(jax.ShapeDtypeStruct(shp, s.dtype))

    def body(*refs):
        src = refs[:n]
        dst = refs[n:2 * n]
        send_sems, recv_sems, loc_sems = refs[2 * n:]
        x, y, c = lax.axis_index("x"), lax.axis_index("y"), lax.axis_index("c")
        me = 4 * x + 2 * y + c
        peers = []
        for j in range(1, N_DEV):
            px = 1 - x if (j >> 2) & 1 else x
            py = 1 - y if (j >> 1) & 1 else y
            pc = 1 - c if j & 1 else c
            peers.append(((px, py, pc), 4 * px + 2 * py + pc))
        local, sends = [], []
        for k in range(n):
            own = src[k].at[me] if scatter_flags[k] else src[k]
            lc = pltpu.make_async_copy(own, dst[k].at[me], loc_sems.at[k])
            lc.start()
            local.append(lc)
            for j, (pid, pidx) in enumerate(peers):
                s = src[k].at[pidx] if scatter_flags[k] else src[k]
                cp = pltpu.make_async_remote_copy(
                    src_ref=s, dst_ref=dst[k].at[me], send_sem=send_sems.at[k, j], recv_sem=recv_sems.at[k, j],
                    device_id=pid, device_id_type=pl.DeviceIdType.MESH)
                cp.start()
                sends.append(cp)
        for k in range(n):
            for j, (pid, pidx) in enumerate(peers):
                s = src[k].at[pidx] if scatter_flags[k] else src[k]
                pltpu.make_async_remote_copy(
                    src_ref=s, dst_ref=dst[k].at[pidx], send_sem=send_sems.at[k, j], recv_sem=recv_sems.at[k, j],
                    device_id=pid, device_id_type=pl.DeviceIdType.MESH).wait_recv()
        for cp in sends:
            cp.wait_send()
        for lc in local:
            lc.wait()

    any_spec = pl.BlockSpec(memory_space=pl.ANY)
    return pl.pallas_call(
        body, name=name, out_shape=tuple(out_shape),
        in_specs=[any_spec] * n, out_specs=tuple([any_spec] * n),
        scratch_shapes=[pltpu.SemaphoreType.DMA((n, N_DEV - 1)), pltpu.SemaphoreType.DMA((n, N_DEV - 1)),
                        pltpu.SemaphoreType.DMA((n,))],
        compiler_params=pltpu.CompilerParams(has_side_effects=True),
    )(*srcs)


def _peer_list():
    x, y, c = lax.axis_index("x"), lax.axis_index("y"), lax.axis_index("c")
    peers = []
    for j in range(1, N_DEV):
        px = 1 - x if (j >> 2) & 1 else x
        py = 1 - y if (j >> 1) & 1 else y
        pc = 1 - c if j & 1 else c
        peers.append(((px, py, pc), 4 * px + 2 * py + pc))
    return 4 * x + 2 * y + c, peers


def _split_copies(src, land, send_sems, recv_sems, scatter_flags, me, peers, incoming):
    copies = []
    for k in range(len(src)):
        for j, (pid, pidx) in enumerate(peers):
            s = src[k].at[pidx] if scatter_flags[k] else src[k]
            i = k * (N_DEV - 1) + j
            copies.append(pltpu.make_async_remote_copy(
                src_ref=s, dst_ref=land[k].at[pidx if incoming else me], send_sem=send_sems[i],
                recv_sem=recv_sems[i], device_id=pid, device_id_type=pl.DeviceIdType.MESH))
    return copies


def _exchange_start(srcs, scatter_flags, after, name, collective_id):
    n = len(srcs)
    ns = n * (N_DEV - 1)
    hbm = pl.BlockSpec(memory_space=pltpu.HBM)
    sem = pl.BlockSpec(memory_space=pltpu.SEMAPHORE)
    land_shapes = [s.shape if sc else (N_DEV,) + s.shape for s, sc in zip(srcs, scatter_flags)]

    def body(*refs):
        src, land = refs[:n], refs[n:2 * n]
        send_sems = refs[2 * n + 1:2 * n + 1 + ns]
        recv_sems = refs[2 * n + 1 + ns:2 * n + 1 + 2 * ns]
        token = refs[4 * n + 1 + 2 * ns]
        me, peers = _peer_list()
        barrier = pltpu.get_barrier_semaphore()
        for pid, _ in peers:
            pl.semaphore_signal(barrier, inc=1, device_id=pid, device_id_type=pl.DeviceIdType.MESH)
        pl.semaphore_wait(barrier, N_DEV - 1)
        for cp in _split_copies(src, land, send_sems, recv_sems, scatter_flags, me, peers, False):
            cp.start()
        token[...] = jnp.zeros_like(token)

    outs = pl.pallas_call(
        body, name=name,
        out_shape=(*[pltpu.SemaphoreType.DMA(())] * (2 * ns), *[pltpu.HBM(s.shape, s.dtype) for s in srcs],
                   *[pltpu.HBM(shp, s.dtype) for shp, s in zip(land_shapes, srcs)],
                   jax.ShapeDtypeStruct((SUBLANES, LANES), F32)),
        in_specs=[hbm] * (2 * n) + [pl.BlockSpec(memory_space=pl.ANY)],
        out_specs=(*[sem] * (2 * ns), *[hbm] * (2 * n), pl.BlockSpec(memory_space=pltpu.VMEM)),
        input_output_aliases={i: 2 * ns + i for i in range(2 * n)},
        compiler_params=pltpu.CompilerParams(has_side_effects=pltpu.SideEffectType.DATAFLOW_SIDE_EFFECTING,
                                             collective_id=collective_id),
    )(*[pltpu.with_memory_space_constraint(s, pltpu.HBM) for s in srcs],
      *[pltpu.with_memory_space_constraint(lax.empty(shp, s.dtype), pltpu.HBM) for shp, s in zip(land_shapes, srcs)],
      after)
    return (outs[:ns], outs[ns:2 * ns], outs[2 * ns:2 * ns + n], outs[2 * ns + n:2 * ns + 2 * n], outs[2 * ns + 2 * n])


def _exchange_wait(send_sems, recv_sems, srcs, lands, scatter_flags, after, name):
    n = len(srcs)
    ns = n * (N_DEV - 1)
    hbm = pl.BlockSpec(memory_space=pltpu.HBM)
    sem = pl.BlockSpec(memory_space=pltpu.SEMAPHORE)

    def body(*refs):
        src, land = refs[:n], refs[n:2 * n]
        s_sems = refs[2 * n:2 * n + ns]
        r_sems = refs[2 * n + ns:2 * n + 2 * ns]
        me, peers = _peer_list()
        for cp in _split_copies(src, land, s_sems, r_sems, scatter_flags, me, peers, True):
            cp.wait_send()
            cp.wait_recv()

    outs = pl.pallas_call(
        body, name=name,
        out_shape=tuple(pltpu.HBM(a.shape, a.dtype) for a in (*srcs, *lands)),
        in_specs=[hbm] * (2 * n) + [sem] * (2 * ns) + [pl.BlockSpec(memory_space=pl.ANY)],
        out_specs=tuple([hbm] * (2 * n)),
        input_output_aliases={i: i for i in range(2 * n)},
        compiler_params=pltpu.CompilerParams(has_side_effects=pltpu.SideEffectType.DATAFLOW_SIDE_EFFECTING),
    )(*srcs, *lands, *send_sems, *recv_sems, after)
    return outs[:n], outs[n:]


def _tn_matmul(a, b, name, tk, tm, out_rows=None, out_cols=None, tn=512):
    n_tok, k_dim = a.shape
    m_dim = b.shape[1]
    grid = (k_dim // tk, m_dim // tm, n_tok // tn)

    def body(a_ref, b_ref, o_ref):
        @pl.when(pl.program_id(2) == 0)
        def _():
            o_ref[...] = jnp.zeros_like(o_ref)
        o_ref[...] += _dot_tn(a_ref[...].astype(BF16), b_ref[...].astype(BF16))

    return pl.pallas_call(
        body, name=name, grid=grid,
        in_specs=[pl.BlockSpec((tn, tk), lambda i, j, k: (k, i)), pl.BlockSpec((tn, tm), lambda i, j, k: (k, j))],
        out_specs=pl.BlockSpec((tk, tm), lambda i, j, k: (i, j)),
        out_shape=jax.ShapeDtypeStruct((out_rows or k_dim, out_cols or m_dim), F32),
        compiler_params=_cparams("parallel", "parallel", "arbitrary"),
    )(a, b)


def _adam_math(g, w, m, v):
    m = ADAM_B1 * m + (1.0 - ADAM_B1) * g
    v = ADAM_B2 * v + (1.0 - ADAM_B2) * (g * g)
    m_hat = m / (1.0 - ADAM_B1 ** ADAM_STEP)
    v_hat = v / (1.0 - ADAM_B2 ** ADAM_STEP)
    delta = -ADAM_LR * (m_hat / (jnp.sqrt(v_hat) + ADAM_EPS) + ADAM_WD * w)
    return delta, m, v


def _adam_sharded(land, own, w, m, v, name, tile):
    _, r, c = w.shape

    def body(*refs):
        l_ref = refs[0]
        own_ref = refs[1] if own is not None else None
        w_ref, m_ref, v_ref, g_ref, d_ref, nm_ref, nv_ref = [ref.at[0] for ref in refs[-7:]]
        if own_ref is not None:
            x, y, z = lax.axis_index("x"), lax.axis_index("y"), lax.axis_index("c")
            me = 4 * x + 2 * y + z
            mine = own_ref[...].astype(F32)
        g = None
        for s in range(N_DEV):
            part = l_ref[s].astype(F32)
            if own_ref is not None:
                part = jnp.where(me == s, mine, part)
            g = part if g is None else g + part
        d, nm, nv = _adam_math(g, w_ref[...], m_ref[...], v_ref[...])
        g_ref[...] = g
        d_ref[...] = d
        nm_ref[...] = nm
        nv_ref[...] = nv

    tr, tc = tile
    spec = pl.BlockSpec((1, tr, tc), lambda i, j: (0, i, j))
    own_specs, own_args = ([pl.BlockSpec((tr, tc), lambda i, j: (i, j))], [own]) if own is not None else ([], [])
    return pl.pallas_call(
        body, name=name, grid=(r // tr, c // tc),
        in_specs=[pl.BlockSpec((N_DEV, tr, tc), lambda i, j: (0, i, j)), *own_specs, spec, spec, spec],
        out_specs=(spec, spec, spec, spec),
        out_shape=tuple(jax.ShapeDtypeStruct((1, r, c), F32) for _ in range(4)),
        compiler_params=_cparams("parallel", "parallel"),
    )(land, *own_args, w, m, v)


def _sum_partials(parts, own, name):
    _, r, c = parts.shape

    def body(*refs):
        p_ref, o_ref = refs[0], refs[-1]
        if own is not None:
            x, y, z = lax.axis_index("x"), lax.axis_index("y"), lax.axis_index("c")
            me = 4 * x + 2 * y + z
            mine = refs[1][...]
        g = None
        for s in range(N_DEV):
            part = p_ref[s]
            if own is not None:
                part = jnp.where(me == s, mine, part)
            g = part if g is None else g + part
        o_ref[...] = g

    args = (parts,) if own is None else (parts, own)
    return pl.pallas_call(body, name=name, out_shape=jax.ShapeDtypeStruct((r, c), F32),
                          compiler_params=pltpu.CompilerParams(vmem_limit_bytes=VMEM_LIMIT))(*args)


def _adam_replicated(gs, ws, ms, vs, name):
    k = len(ws)

    def body(*refs):
        outs = refs[4 * k:]
        for i in range(k):
            d, nm, nv = _adam_math(refs[i][...], refs[k + i][...], refs[2 * k + i][...], refs[3 * k + i][...])
            outs[i][...] = d
            outs[k + i][...] = nm
            outs[2 * k + i][...] = nv

    outs = pl.pallas_call(body, name=name, out_shape=tuple(jax.ShapeDtypeStruct(w.shape, F32) for w in ws) * 3,
                          compiler_params=pltpu.CompilerParams(vmem_limit_bytes=VMEM_LIMIT))(*gs, *ws, *ms, *vs)
    return outs[:k], outs[k:2 * k], outs[2 * k:]


def _inproj_fwd(x, g, w_in, avg, qg, kg, tm=512):
    n = x.shape[0]

    def body(x_ref, g_ref, w_ref, a_ref, qg_ref, kg_ref, hn_ref, u_ref, qkv_ref, raw_ref, fl_ref):
        hn = _rms(x_ref[...], g_ref[...]).astype(BF16)
        hn_ref[...] = hn
        proj = _dot_nt(hn, w_ref[...])
        u_ref[...] = proj[:, 0:512]
        q = proj[:, 512:1024]
        k = proj[:, 1024:1536]
        raw_ref[:, 0:512] = q
        raw_ref[:, 512:1024] = k
        qkv_ref[:, 0:512] = _headnorm(q, a_ref[...], qg_ref[...])
        qkv_ref[:, 512:1024] = _headnorm(k, a_ref[...], kg_ref[...])
        qkv_ref[:, 1024:1536] = proj[:, 1536:2048]
        fl_ref[...] = proj[:, 2048:D_IN_PAD]

    row = lambda w: pl.BlockSpec((tm, w), lambda i: (i, 0))
    full = lambda a: pl.BlockSpec(a.shape, lambda i: (0,) * a.ndim)
    return pl.pallas_call(
        body, name="inproj_fwd", grid=(n // tm,),
        in_specs=[row(D_MODEL), full(g), full(w_in), full(avg), full(qg), full(kg)],
        out_specs=(row(D_MODEL), row(512), row(1536), row(1024), row(LANES)),
        out_shape=(jax.ShapeDtypeStruct((n, D_MODEL), BF16), jax.ShapeDtypeStruct((n, 512), F32),
                   jax.ShapeDtypeStruct((n, 1536), F32), jax.ShapeDtypeStruct((n, 1024), F32),
                   jax.ShapeDtypeStruct((n, LANES), F32)),
        compiler_params=_cparams("parallel"),
    )(x, g, w_in, avg, qg, kg)


def _inproj_bwd(x, g, w_in, avg, qg, kg, raw, du, dqn, dkn, dv, dfl, dres, tm=512):
    n = x.shape[0]

    def body(x_ref, g_ref, w_ref, a_ref, qg_ref, kg_ref, raw_ref, du_ref, dqn_ref, dkn_ref, dv_ref, dfl_ref, dres_ref,
             dx_ref, dproj_ref, dg_ref, dqg_ref, dkg_ref):
        @pl.when(pl.program_id(0) == 0)
        def _():
            dg_ref[...] = jnp.zeros_like(dg_ref)
            dqg_ref[...] = jnp.zeros_like(dqg_ref)
            dkg_ref[...] = jnp.zeros_like(dkg_ref)
        avg_m = a_ref[...]
        _, vjp_q = jax.vjp(lambda q, gg: _headnorm(q, avg_m, gg), raw_ref[:, 0:512], qg_ref[...])
        dq, dqg = vjp_q(dqn_ref[...])
        _, vjp_k = jax.vjp(lambda k, gg: _headnorm(k, avg_m, gg), raw_ref[:, 512:1024], kg_ref[...])
        dk, dkg = vjp_k(dkn_ref[...])
        dproj = jnp.concatenate([du_ref[...], dq, dk, dv_ref[...], dfl_ref[...]], axis=1).astype(BF16)
        dproj_ref[...] = dproj
        dhn = _dot(dproj, w_ref[...])
        _, vjp_x = jax.vjp(_rms, x_ref[...], g_ref[...])
        dxn, dg = vjp_x(dhn)
        dx_ref[...] = dxn + dres_ref[...]
        dg_ref[...] += dg
        dqg_ref[...] += dqg
        dkg_ref[...] += dkg

    row = lambda w: pl.BlockSpec((tm, w), lambda i: (i, 0))
    full = lambda a: pl.BlockSpec(a.shape, lambda i: (0,) * a.ndim)
    vec = lambda w: pl.BlockSpec((1, w), lambda i: (0, 0))
    return pl.pallas_call(
        body, name="inproj_bwd", grid=(n // tm,),
        in_specs=[row(D_MODEL), full(g), full(w_in), full(avg), full(qg), full(kg), row(1024), row(512), row(512),
                  row(512), row(512), row(LANES), row(D_MODEL)],
        out_specs=(row(D_MODEL), row(D_IN_PAD), vec(D_MODEL), vec(512), vec(512)),
        out_shape=(jax.ShapeDtypeStruct((n, D_MODEL), F32), jax.ShapeDtypeStruct((n, D_IN_PAD), BF16),
                   jax.ShapeDtypeStruct((1, D_MODEL), F32), jax.ShapeDtypeStruct((1, 512), F32),
                   jax.ShapeDtypeStruct((1, 512), F32)),
        compiler_params=_cparams("arbitrary"),
    )(x, g, w_in, avg, qg, kg, raw, du, dqn, dkn, dv, dfl, dres)


def _glu_fwd(yc, wg, bg, tm=512):
    n = yc.shape[0]

    def body(yc_ref, w_ref, b_ref, ys_ref):
        gl = jax.nn.gelu(yc_ref[...])
        z = _dot(gl.astype(BF16), w_ref[...]) + b_ref[...]
        ys_ref[...] = gl * jax.nn.sigmoid(z)

    row = pl.BlockSpec((tm, 512), lambda i: (i, 0))
    full = lambda a: pl.BlockSpec(a.shape, lambda i: (0,) * a.ndim)
    return pl.pallas_call(
        body, name="glu_fwd", grid=(n // tm,), in_specs=[row, full(wg), full(bg)], out_specs=row,
        out_shape=jax.ShapeDtypeStruct((n, 512), F32), compiler_params=_cparams("parallel"),
    )(yc, wg, bg)


def _glu_bwd(yc, dys, wg, bg, tm=512):
    n = yc.shape[0]

    def body(yc_ref, dys_ref, w_ref, b_ref, dyc_ref, gl_ref, dz_ref, db_ref):
        @pl.when(pl.program_id(0) == 0)
        def _():
            db_ref[...] = jnp.zeros_like(db_ref)
        gl, vjp_gelu = jax.vjp(jax.nn.gelu, yc_ref[...])
        glb = gl.astype(BF16)
        z = _dot(glb, w_ref[...]) + b_ref[...]
        s = jax.nn.sigmoid(z)
        dys = dys_ref[...]
        dz = dys * gl * s * (1.0 - s)
        dzb = dz.astype(BF16)
        dgl = dys * s + _dot_nt(dzb, w_ref[...])
        dyc_ref[...] = vjp_gelu(dgl)[0]
        gl_ref[...] = glb
        dz_ref[...] = dzb
        db_ref[...] += jnp.sum(dz, axis=0, keepdims=True)

    row = pl.BlockSpec((tm, 512), lambda i: (i, 0))
    full = lambda a: pl.BlockSpec(a.shape, lambda i: (0,) * a.ndim)
    return pl.pallas_call(
        body, name="glu_bwd", grid=(n // tm,), in_specs=[row, row, full(wg), full(bg)],
        out_specs=(row, row, row, pl.BlockSpec((1, 512), lambda i: (0, 0))),
        out_shape=(jax.ShapeDtypeStruct((n, 512), F32), jax.ShapeDtypeStruct((n, 512), BF16),
                   jax.ShapeDtypeStruct((n, 512), BF16), jax.ShapeDtypeStruct((1, 512), F32)),
        compiler_params=_cparams("arbitrary"),
    )(yc, dys, wg, bg)


def _mix_fwd(x, ys, ya, gs, ga, wout, gf, tm=512):
    n = x.shape[0]

    def body(x_ref, ys_ref, ya_ref, gs_ref, ga_ref, w_ref, gf_ref, h1_ref, hn2_ref, mixed_ref):
        mixed = jnp.concatenate([_rms(ys_ref[...], gs_ref[...]), _rms(ya_ref[...], ga_ref[...])], axis=1).astype(BF16)
        mixed_ref[...] = mixed
        h1 = x_ref[...] + _dot(mixed, w_ref[...])
        h1_ref[...] = h1
        hn2_ref[...] = _rms(h1, gf_ref[...]).astype(BF16)

    row = lambda w: pl.BlockSpec((tm, w), lambda i: (i, 0))
    full = lambda a: pl.BlockSpec(a.shape, lambda i: (0,) * a.ndim)
    return pl.pallas_call(
        body, name="mix_fwd", grid=(n // tm,),
        in_specs=[row(D_MODEL), row(512), row(512), full(gs), full(ga), full(wout), full(gf)],
        out_specs=(row(D_MODEL), row(D_MODEL), row(D_MODEL)),
        out_shape=(jax.ShapeDtypeStruct((n, D_MODEL), F32), jax.ShapeDtypeStruct((n, D_MODEL), BF16),
                   jax.ShapeDtypeStruct((n, D_MODEL), BF16)),
        compiler_params=_cparams("parallel"),
    )(x, ys, ya, gs, ga, wout, gf)


def _mix_bwd(dy, dhn2_parts, h1, ys, ya, gs, ga, wout, gf, tm=512):
    n = dy.shape[0]
    n_parts = dhn2_parts.shape[0]

    def body(dy_ref, dp_ref, h1_ref, ys_ref, ya_ref, gs_ref, ga_ref, w_ref, gf_ref,
             dh1_ref, dys_ref, dya_ref, dgs_ref, dga_ref, dgf_ref):
        @pl.when(pl.program_id(0) == 0)
        def _():
            dgs_ref[...] = jnp.zeros_like(dgs_ref)
            dga_ref[...] = jnp.zeros_like(dga_ref)
            dgf_ref[...] = jnp.zeros_like(dgf_ref)
        dhn2 = dp_ref[0]
        for p in range(1, n_parts):
            dhn2 = dhn2 + dp_ref[p]
        _, vjp_f = jax.vjp(_rms, h1_ref[...], gf_ref[...])
        dh1n, dgf = vjp_f(dhn2)
        dh1 = dy_ref[...] + dh1n
        dh1_ref[...] = dh1
        dmixed = _dot_nt(dh1.astype(BF16), w_ref[...])
        _, vjp_s = jax.vjp(_rms, ys_ref[...], gs_ref[...])
        dys, dgs = vjp_s(dmixed[:, 0:512])
        _, vjp_a = jax.vjp(_rms, ya_ref[...], ga_ref[...])
        dya, dga = vjp_a(dmixed[:, 512:1024])
        dys_ref[...] = dys
        dya_ref[...] = dya
        dgs_ref[...] += dgs
        dga_ref[...] += dga
        dgf_ref[...] += dgf

    row = lambda w: pl.BlockSpec((tm, w), lambda i: (i, 0))
    full = lambda a: pl.BlockSpec(a.shape, lambda i: (0,) * a.ndim)
    vec = lambda w: pl.BlockSpec((1, w), lambda i: (0, 0))
    return pl.pallas_call(
        body, name="mix_bwd", grid=(n // tm,),
        in_specs=[row(D_MODEL), pl.BlockSpec((n_parts, tm, D_MODEL), lambda i: (0, i, 0)), row(D_MODEL), row(512),
                  row(512), full(gs), full(ga), full(wout), full(gf)],
        out_specs=(row(D_MODEL), row(512), row(512), vec(512), vec(512), vec(D_MODEL)),
        out_shape=(jax.ShapeDtypeStruct((n, D_MODEL), F32), jax.ShapeDtypeStruct((n, 512), F32),
                   jax.ShapeDtypeStruct((n, 512), F32), jax.ShapeDtypeStruct((1, 512), F32),
                   jax.ShapeDtypeStruct((1, 512), F32), jax.ShapeDtypeStruct((1, D_MODEL), F32)),
        compiler_params=_cparams("arbitrary"),
    )(dy, dhn2_parts, h1, ys, ya, gs, ga, wout, gf)


HALO = 16
FFN_GROUPS = 4
FFN_GROUP = D_FF // FFN_GROUPS


def _conv3(ue, cw):
    return cw[2:3] * ue + cw[1:2] * pltpu.roll(ue, 1, 0) + cw[0:1] * pltpu.roll(ue, 2, 0) + cw[3:4]


def _ffn_weight_specs():
    gate = lambda i, j: (j, 0, 0)
    val = lambda i, j: (j + FFN_GROUPS, 0, 0)
    w_blk, c_blk = (1, FFN_GROUP, D_MODEL), (1, SUBLANES, FFN_GROUP)
    return [pl.BlockSpec(w_blk, gate), pl.BlockSpec(w_blk, val), pl.BlockSpec(c_blk, gate), pl.BlockSpec(c_blk, val),
            pl.BlockSpec((1, FFN_GROUP, D_MODEL), gate)]


def _ffn_fwd(hn2, h1, target, w_up, conv, w_down, seq_len, tm=512):
    n = hn2.shape[0]
    nj = FFN_GROUPS
    hb = tm // HALO

    def body(hn_ref, halo_ref, h1_ref, tgt_ref, wg_ref, wv_ref, cg_ref, cv_ref, wd_ref,
             ug_ref, uv_ref, pg_ref, pv_ref, dy_ref, loss_ref, acc):
        i, j = pl.program_id(0), pl.program_id(1)
        seq_start = (i * tm) % seq_len == 0
        halo = halo_ref[...]
        halo = jnp.where(seq_start, jnp.zeros_like(halo), halo)
        he = jnp.concatenate([halo, hn_ref[...]], axis=0)
        ueg = _dot_nt(he, wg_ref[0])
        uev = _dot_nt(he, wv_ref[0])
        ug_ref[0] = ueg[HALO:].astype(BF16)
        uv_ref[0] = uev[HALO:].astype(BF16)
        cg = _conv3(ueg, cg_ref[0])[HALO:]
        cv = _conv3(uev, cv_ref[0])[HALO:]
        pg_ref[0] = cg.astype(BF16)
        pv_ref[0] = cv.astype(BF16)
        act = (jax.nn.silu(cg) * cv).astype(BF16)
        part = _dot(act, wd_ref[0])

        @pl.when(j == 0)
        def _():
            acc[...] = part

        @pl.when(j > 0)
        def _():
            acc[...] += part

        @pl.when(j == nj - 1)
        def _():
            err = h1_ref[...] + acc[...] - tgt_ref[...]
            dy_ref[...] = err * (1.0 / D_MODEL)
            loss_ref[0] = jnp.sum(err * err, axis=0, keepdims=True)

    row = pl.BlockSpec((tm, D_MODEL), lambda i, j: (i, 0))
    u_main = pl.BlockSpec((1, tm, FFN_GROUP), lambda i, j: (j, i, 0))
    u_shape = jax.ShapeDtypeStruct((FFN_GROUPS, n, FFN_GROUP), BF16)
    return pl.pallas_call(
        body, name="ffn_fwd", grid=(n // tm, nj),
        in_specs=[row, pl.BlockSpec((HALO, D_MODEL), lambda i, j: (jnp.maximum(i * hb - 1, 0), 0)), row, row,
                  *_ffn_weight_specs()],
        out_specs=(u_main, u_main, u_main, u_main, row, pl.BlockSpec((1, 1, D_MODEL), lambda i, j: (i, 0, 0))),
        out_shape=(u_shape, u_shape, u_shape, u_shape, jax.ShapeDtypeStruct((n, D_MODEL), F32),
                   jax.ShapeDtypeStruct((n // tm, 1, D_MODEL), F32)),
        scratch_shapes=[pltpu.VMEM((tm, D_MODEL), F32)],
        compiler_params=_cparams("parallel", "arbitrary"),
    )(hn2, hn2, h1, target, w_up, w_up, conv, conv, w_down)


def _ffn_bwd(dy, ug, uv, pg, pv, w_up, conv, w_down, seq_len, tm=512):
    n = dy.shape[0]
    nj = FFN_GROUPS
    fb = FFN_GROUP
    hb = tm // HALO
    last_hb = n // HALO - 1
    rows = tm + HALO

    def body(dy_ref, dyn_ref, ug_ref, uv_ref, pgm_ref, pgn_ref, pvm_ref, pvn_ref, wg_ref, wv_ref, cg_ref, cv_ref,
             wd_ref, dug_ref, duv_ref, act_ref, dhn_ref, dcg_ref, dcv_ref, acc):
        i, j = pl.program_id(0), pl.program_id(1)
        seq_end = ((i + 1) * tm) % seq_len == 0
        dyn = dyn_ref[...]
        dyn = jnp.where(seq_end, jnp.zeros_like(dyn), dyn)
        d_out = jnp.concatenate([dy_ref[...], dyn], axis=0).astype(BF16)
        d_act = _dot_nt(d_out, wd_ref[0])
        cge = jnp.concatenate([pgm_ref[0], pgn_ref[0]], axis=0).astype(F32)
        cve = jnp.concatenate([pvm_ref[0], pvn_ref[0]], axis=0).astype(F32)
        act, vjp_act = jax.vjp(lambda g, v: jax.nn.silu(g) * v, cge, cve)
        dcge, dcve = vjp_act(d_act)
        act_ref[0] = act[:tm].astype(BF16)

        def conv_t(dc, u_ref, cw):
            ahead1 = pltpu.roll(dc, rows - 1, 0)[:tm]
            ahead2 = pltpu.roll(dc, rows - 2, 0)[:tm]
            here = dc[:tm]
            du = cw[2:3] * here + cw[1:2] * ahead1 + cw[0:1] * ahead2
            u = u_ref[0].astype(F32)
            col = lambda x: jnp.sum(x, axis=0, keepdims=True)
            grad = jnp.concatenate([col(ahead2 * u), col(ahead1 * u), col(here * u), col(here),
                                    jnp.zeros((4, fb), F32)], axis=0)
            return du.astype(BF16), grad

        cwg, cwv = cg_ref[0], cv_ref[0]
        dug, grad_g = conv_t(dcge, ug_ref, cwg)
        duv, grad_v = conv_t(dcve, uv_ref, cwv)
        dug_ref[0] = dug
        duv_ref[0] = duv
        part = _dot(dug, wg_ref[0]) + _dot(duv, wv_ref[0])

        @pl.when(j == 0)
        def _():
            acc[...] = part

        @pl.when(j > 0)
        def _():
            acc[...] += part

        @pl.when(j == nj - 1)
        def _():
            dhn_ref[...] = acc[...]

        @pl.when(i == 0)
        def _():
            dcg_ref[j] = jnp.zeros((8, fb), F32)
            dcv_ref[j] = jnp.zeros((8, fb), F32)

        dcg_ref[j] += grad_g
        dcv_ref[j] += grad_v

    row = pl.BlockSpec((tm, D_MODEL), lambda i, j: (i, 0))
    u_main = pl.BlockSpec((1, tm, fb), lambda i, j: (j, i, 0))
    u_next = pl.BlockSpec((1, HALO, fb), lambda i, j: (j, jnp.minimum((i + 1) * hb, last_hb), 0))
    dc_spec = pl.BlockSpec((nj, 8, fb), lambda i, j: (0, 0, 0))
    u_shape = jax.ShapeDtypeStruct((FFN_GROUPS, n, fb), BF16)
    return pl.pallas_call(
        body, name="ffn_bwd", grid=(n // tm, nj),
        in_specs=[row, pl.BlockSpec((HALO, D_MODEL), lambda i, j: (jnp.minimum((i + 1) * hb, last_hb), 0)),
                  u_main, u_main, u_main, u_next, u_main, u_next, *_ffn_weight_specs()],
        out_specs=(u_main, u_main, u_main, row, dc_spec, dc_spec),
        out_shape=(u_shape, u_shape, u_shape, jax.ShapeDtypeStruct((n, D_MODEL), F32),
                   jax.ShapeDtypeStruct((nj, 8, fb), F32), jax.ShapeDtypeStruct((nj, 8, fb), F32)),
        scratch_shapes=[pltpu.VMEM((tm, D_MODEL), F32)],
        compiler_params=_cparams("arbitrary", "arbitrary"),
    )(dy, dy, ug, uv, pg, pg, pv, pv, w_up, w_up, conv, conv, w_down)


def _tn_grouped(a, b, name, shared_a, out_dtype=F32, tn=1024):
    groups = b.shape[0] if shared_a else a.shape[0]
    n_tok = a.shape[0] if shared_a else b.shape[0]
    k_dim, m_dim = a.shape[-1], b.shape[-1]

    def body(a_ref, b_ref, o_ref, acc):
        k = pl.program_id(1)
        a_t = a_ref[...] if shared_a else a_ref[0]
        b_t = b_ref[0] if shared_a else b_ref[...]
        part = _dot_tn(a_t.astype(BF16), b_t.astype(BF16))

        @pl.when(k == 0)
        def _():
            acc[...] = part

        @pl.when(k > 0)
        def _():
            acc[...] += part

        @pl.when(k == n_tok // tn - 1)
        def _():
            o_ref[0] = acc[...].astype(out_dtype)

    plain = lambda w: pl.BlockSpec((tn, w), lambda g, k: (k, 0))
    grouped = lambda w: pl.BlockSpec((1, tn, w), lambda g, k: (g, k, 0))
    return pl.pallas_call(
        body, name=name, grid=(groups, n_tok // tn),
        in_specs=[plain(k_dim), grouped(m_dim)] if shared_a else [grouped(k_dim), plain(m_dim)],
        out_specs=pl.BlockSpec((1, k_dim, m_dim), lambda g, k: (g, 0, 0)),
        out_shape=jax.ShapeDtypeStruct((groups, k_dim, m_dim), out_dtype),
        scratch_shapes=[pltpu.VMEM((k_dim, m_dim), F32)],
        compiler_params=_cparams("parallel", "arbitrary"),
    )(a, b)


def _s5_param_fn(lr, li, ldt, br, bi):
    dt = jnp.exp(ldt)
    mag = jnp.exp(lr * dt)
    ab_re = mag * jnp.cos(li * dt)
    ab_im = mag * jnp.sin(li * dt)
    nr = ab_re - 1.0
    ni = ab_im
    den = lr * lr + li * li
    q_re = (nr * lr + ni * li) / den
    q_im = (ni * lr - nr * li) / den
    bb_re = q_re * br - q_im * bi
    bb_im = q_re * bi + q_im * br
    return ab_re, ab_im, bb_re, bb_im


def _s5_param_fwd(lr, li, ldt, br, bi):
    def body(lr_ref, li_ref, ldt_ref, br_ref, bi_ref, ar_ref, ai_ref, bbr_ref, bbi_ref):
        ar, ai, bbr, bbi = _s5_param_fn(lr_ref[...], li_ref[...], ldt_ref[...], br_ref[...], bi_ref[...])
        ar_ref[...] = ar
        ai_ref[...] = ai
        bbr_ref[...] = bbr
        bbi_ref[...] = bbi

    return pl.pallas_call(
        body, name="s5_param_fwd",
        out_shape=(jax.ShapeDtypeStruct(lr.shape, F32), jax.ShapeDtypeStruct(lr.shape, F32),
                   jax.ShapeDtypeStruct(br.shape, F32), jax.ShapeDtypeStruct(br.shape, F32)),
    )(lr, li, ldt, br, bi)


def _s5_param_bwd(lr, li, ldt, br, bi, dar, dai, dbbr, dbbi):
    def body(lr_ref, li_ref, ldt_ref, br_ref, bi_ref, dar_ref, dai_ref, dbbr_ref, dbbi_ref,
             dlr_ref, dli_ref, dldt_ref, dbr_ref, dbi_ref):
        _, vjp = jax.vjp(_s5_param_fn, lr_ref[...], li_ref[...], ldt_ref[...], br_ref[...], bi_ref[...])
        dlr, dli, dldt, dbr, dbi = vjp((dar_ref[...], dai_ref[...], dbbr_ref[...], dbbi_ref[...]))
        dlr_ref[...] = dlr
        dli_ref[...] = dli
        dldt_ref[...] = dldt
        dbr_ref[...] = dbr
        dbi_ref[...] = dbi

    return pl.pallas_call(
        body, name="s5_param_bwd",
        out_shape=(jax.ShapeDtypeStruct(lr.shape, F32), jax.ShapeDtypeStruct(lr.shape, F32),
                   jax.ShapeDtypeStruct(ldt.shape, F32), jax.ShapeDtypeStruct(br.shape, F32),
                   jax.ShapeDtypeStruct(br.shape, F32)),
    )(lr, li, ldt, br, bi, dar, dai, dbbr, dbbi)


S5_CHUNK = 256
S5_STATES = 512
S5_BLOCKS = 4


def _cpow_rows(ar, ai, count):
    rs, im = [ar], [ai]
    for _ in range(count - 1):
        pr, pi = rs[-1], im[-1]
        rs.append(pr * ar - pi * ai)
        im.append(pr * ai + pi * ar)
    return rs, im


def _scan_in_groups(vr, vi, pr, pi, rm, reverse):
    n, width = vr.shape
    vr = vr.reshape(n // SUBLANES, SUBLANES, width)
    vi = vi.reshape(n // SUBLANES, SUBLANES, width)
    row = rm[0:SUBLANES]
    for k in (1, 2, 4):
        shift = SUBLANES - k if reverse else k
        keep = row < SUBLANES - k if reverse else row >= k
        kr = jnp.where(keep, pr[k - 1], 0.0)
        ki = jnp.where(keep, pi[k - 1], 0.0)
        sr, si = pltpu.roll(vr, shift, 1), pltpu.roll(vi, shift, 1)
        vr, vi = vr + kr * sr - ki * si, vi + kr * si + ki * sr
    return vr.reshape(n, width), vi.reshape(n, width)


def _carry_over_groups(xr_s, xi_s, wr, wi, c0r, c0i, reverse):
    groups = xr_s.shape[0] // SUBLANES
    pick = 0 if reverse else SUBLANES - 1

    def step(q, carry):
        cr, ci = carry
        r = groups - 1 - q if reverse else q
        o = pl.multiple_of(r * SUBLANES, SUBLANES)
        vr = xr_s[pl.ds(o, SUBLANES), :]
        vi = xi_s[pl.ds(o, SUBLANES), :]
        nr = vr + wr * cr - wi * ci
        ni = vi + wr * ci + wi * cr
        xr_s[pl.ds(o, SUBLANES), :] = nr
        xi_s[pl.ds(o, SUBLANES), :] = ni
        return (jnp.broadcast_to(nr[pick:pick + 1], nr.shape), jnp.broadcast_to(ni[pick:pick + 1], ni.shape))

    return lax.fori_loop(0, groups, step, (c0r, c0i))


def _s5_state_scan(u_b, bbr, bbi, pr, pi, rm, xr_s, xi_s, c0r, c0i):
    bur = _dot(u_b, bbr)
    bui = _dot(u_b, bbi)
    bur, bui = _scan_in_groups(bur, bui, pr, pi, rm, False)
    xr_s[...] = bur
    xi_s[...] = bui
    w8r = jnp.concatenate(pr, axis=0)
    w8i = jnp.concatenate(pi, axis=0)
    return _carry_over_groups(xr_s, xi_s, w8r, w8i, c0r, c0i, False)


def _s5_fwd(u, a_re, a_im, bbr, bbi, cr, ci, d_skip, n_seq):
    n = u.shape[0]
    seq_len = n // n_seq
    nt = seq_len // S5_CHUNK
    tc = S5_CHUNK

    def body(u_ref, ar_ref, ai_ref, bbr_ref, bbi_ref, cr_ref, ci_ref, d_ref, y_ref, str_ref, sti_ref,
             xr_s, xi_s, car_r, car_i):
        t = pl.program_id(2)

        @pl.when(t == 0)
        def _():
            car_r[...] = jnp.zeros_like(car_r)
            car_i[...] = jnp.zeros_like(car_i)
        pr, pi = _cpow_rows(ar_ref[0], ai_ref[0], SUBLANES)
        rm = lax.broadcasted_iota(jnp.int32, (tc, S5_STATES), 0) & (SUBLANES - 1)
        str_ref[0, 0] = car_r[...]
        sti_ref[0, 0] = car_i[...]
        u_t = u_ref[...]
        cfr, cfi = _s5_state_scan(u_t.astype(BF16), bbr_ref[0], bbi_ref[0], pr, pi, rm, xr_s, xi_s,
                                  car_r[...], car_i[...])
        car_r[...] = cfr
        car_i[...] = cfi
        y = _dot(xr_s[...].astype(BF16), cr_ref[0]) - _dot(xi_s[...].astype(BF16), ci_ref[0])
        y_ref[...] = y + d_ref[...] * u_t

    u_spec = pl.BlockSpec((tc, LANES), lambda cb, b, t: (b * nt + t, cb))
    a_spec = pl.BlockSpec((1, 1, S5_STATES), lambda cb, b, t: (cb, 0, 0))
    bb_spec = pl.BlockSpec((1, LANES, S5_STATES), lambda cb, b, t: (cb, 0, 0))
    c_spec = pl.BlockSpec((1, S5_STATES, LANES), lambda cb, b, t: (cb, 0, 0))
    st_spec = pl.BlockSpec((1, 1, SUBLANES, S5_STATES), lambda cb, b, t: (cb, b * nt + t, 0, 0))
    st_shape = jax.ShapeDtypeStruct((S5_BLOCKS, n_seq * nt, SUBLANES, S5_STATES), F32)
    return pl.pallas_call(
        body, name="s5_fwd", grid=(S5_BLOCKS, n_seq, nt),
        in_specs=[u_spec, a_spec, a_spec, bb_spec, bb_spec, c_spec, c_spec,
                  pl.BlockSpec((1, LANES), lambda cb, b, t: (0, cb))],
        out_specs=(u_spec, st_spec, st_spec),
        out_shape=(jax.ShapeDtypeStruct((n, D_SSM), F32), st_shape, st_shape),
        scratch_shapes=[pltpu.VMEM((tc, S5_STATES), F32), pltpu.VMEM((tc, S5_STATES), F32),
                        pltpu.VMEM((SUBLANES, S5_STATES), F32), pltpu.VMEM((SUBLANES, S5_STATES), F32)],
        compiler_params=_cparams("parallel", "arbitrary", "arbitrary"),
    )(u, a_re, a_im, bbr, bbi, cr, ci, d_skip)


def _s5_bwd(u, dy, st_r, st_i, a_re, a_im, bbr, bbi, cr, ci, d_skip, n_seq):
    n = u.shape[0]
    seq_len = n // n_seq
    nt = seq_len // S5_CHUNK
    tc = S5_CHUNK

    def body(u_ref, dy_ref, str_ref, sti_ref, ar_ref, ai_ref, bbr_ref, bbi_ref, cr_ref, ci_ref, d_ref,
             du_ref, dbbr_ref, dbbi_ref, dcr_ref, dci_ref, dar_ref, dai_ref, dd_ref,
             xr_s, xi_s, gr_s, gi_s, car_r, car_i):
        b, t = pl.program_id(1), pl.program_id(2)

        @pl.when((b == 0) & (t == 0))
        def _():
            for ref in (dbbr_ref, dbbi_ref, dcr_ref, dci_ref, dar_ref, dai_ref, dd_ref):
                ref[...] = jnp.zeros_like(ref)

        @pl.when(t == 0)
        def _():
            car_r[...] = jnp.zeros_like(car_r)
            car_i[...] = jnp.zeros_like(car_i)
        ar, ai = ar_ref[0], ai_ref[0]
        pr, pi = _cpow_rows(ar, ai, SUBLANES)
        row = lax.broadcasted_iota(jnp.int32, (tc, S5_STATES), 0)
        rm = row & (SUBLANES - 1)
        u_t = u_ref[...]
        u_b = u_t.astype(BF16)
        dy_t = dy_ref[...]
        dy_b = dy_t.astype(BF16)
        s0r, s0i = str_ref[0, 0], sti_ref[0, 0]
        _s5_state_scan(u_b, bbr_ref[0], bbi_ref[0], pr, pi, rm, xr_s, xi_s, s0r, s0i)
        xr, xi = xr_s[...], xi_s[...]
        gr = _dot_nt(dy_b, cr_ref[0])
        gi = -_dot_nt(dy_b, ci_ref[0])
        npi = [-v for v in pi]
        gr, gi = _scan_in_groups(gr, gi, pr, npi, rm, True)
        gr_s[...] = gr
        gi_s[...] = gi
        w8r = jnp.concatenate(pr[::-1], axis=0)
        w8i = jnp.concatenate(npi[::-1], axis=0)
        cfr, cfi = _carry_over_groups(gr_s, gi_s, w8r, w8i, car_r[...], car_i[...], True)
        car_r[...] = cfr
        car_i[...] = cfi
        gr, gi = gr_s[...], gi_s[...]
        gr_b, gi_b = gr.astype(BF16), gi.astype(BF16)
        du_ref[...] = _dot_nt(gr_b, bbr_ref[0]) + _dot_nt(gi_b, bbi_ref[0]) + d_ref[...] * dy_t
        dbbr_ref[0] += _dot_tn(u_b, gr_b)
        dbbi_ref[0] += _dot_tn(u_b, gi_b)
        dcr_ref[0] += _dot_tn(xr.astype(BF16), dy_b)
        dci_ref[0] -= _dot_tn(xi.astype(BF16), dy_b)
        dd_ref[0] += jnp.sum((dy_t * u_t).reshape(tc // SUBLANES, SUBLANES, LANES), axis=0)
        first = row == 0
        xpr = jnp.where(first, jnp.broadcast_to(s0r[0:1], xr.shape), pltpu.roll(xr, 1, 0))
        xpi = jnp.where(first, jnp.broadcast_to(s0i[0:1], xi.shape), pltpu.roll(xi, 1, 0))
        shp = (tc // SUBLANES, SUBLANES, S5_STATES)
        dar_ref[0] += jnp.sum((gr * xpr + gi * xpi).reshape(shp), axis=0)
        dai_ref[0] += jnp.sum((gi * xpr - gr * xpi).reshape(shp), axis=0)

    u_spec = pl.BlockSpec((tc, LANES), lambda cb, b, t: (b * nt + nt - 1 - t, cb))
    a_spec = pl.BlockSpec((1, 1, S5_STATES), lambda cb, b, t: (cb, 0, 0))
    bb_spec = pl.BlockSpec((1, LANES, S5_STATES), lambda cb, b, t: (cb, 0, 0))
    c_spec = pl.BlockSpec((1, S5_STATES, LANES), lambda cb, b, t: (cb, 0, 0))
    st_spec = pl.BlockSpec((1, 1, SUBLANES, S5_STATES), lambda cb, b, t: (cb, b * nt + nt - 1 - t, 0, 0))
    da_spec = pl.BlockSpec((1, SUBLANES, S5_STATES), lambda cb, b, t: (cb, 0, 0))
    dd_spec = pl.BlockSpec((1, SUBLANES, LANES), lambda cb, b, t: (cb, 0, 0))
    big = pltpu.VMEM((tc, S5_STATES), F32)
    small = pltpu.VMEM((SUBLANES, S5_STATES), F32)
    return pl.pallas_call(
        body, name="s5_bwd", grid=(S5_BLOCKS, n_seq, nt),
        in_specs=[u_spec, u_spec, st_spec, st_spec, a_spec, a_spec, bb_spec, bb_spec, c_spec, c_spec,
                  pl.BlockSpec((1, LANES), lambda cb, b, t: (0, cb))],
        out_specs=(u_spec, bb_spec, bb_spec, c_spec, c_spec, da_spec, da_spec, dd_spec),
        out_shape=(jax.ShapeDtypeStruct((n, D_SSM), F32),
                   jax.ShapeDtypeStruct((S5_BLOCKS, LANES, S5_STATES), F32),
                   jax.ShapeDtypeStruct((S5_BLOCKS, LANES, S5_STATES), F32),
                   jax.ShapeDtypeStruct((S5_BLOCKS, S5_STATES, LANES), F32),
                   jax.ShapeDtypeStruct((S5_BLOCKS, S5_STATES, LANES), F32),
                   jax.ShapeDtypeStruct((S5_BLOCKS, SUBLANES, S5_STATES), F32),
                   jax.ShapeDtypeStruct((S5_BLOCKS, SUBLANES, S5_STATES), F32),
                   jax.ShapeDtypeStruct((S5_BLOCKS, SUBLANES, LANES), F32)),
        scratch_shapes=[big, big, big, big, small, small],
        compiler_params=_cparams("parallel", "arbitrary", "arbitrary"),
    )(u, dy, st_r, st_i, a_re, a_im, bbr, bbi, cr, ci, d_skip)


CUM_BLOCK = 128


def _tri(lower):
    r = lax.broadcasted_iota(jnp.int32, (CUM_BLOCK, CUM_BLOCK), 0)
    c = lax.broadcasted_iota(jnp.int32, (CUM_BLOCK, CUM_BLOCK), 1)
    return jnp.where(r >= c if lower else r <= c, 1.0, 0.0).astype(F32)


def _fprep_fwd(fl, bf, n_seq):
    n = fl.shape[0]
    seq_len = n // n_seq
    nb = seq_len // CUM_BLOCK

    def body(fl_ref, bf_ref, cum_ref):
        tril = _tri(True)
        carry = jnp.zeros((1, LANES), F32)
        for blk in range(nb):
            rows = slice(blk * CUM_BLOCK, (blk + 1) * CUM_BLOCK)
            lf = jax.nn.log_sigmoid(fl_ref[rows, :] + bf_ref[...])
            cs = jnp.dot(tril, lf, preferred_element_type=F32, precision=HIGHEST) + carry
            cum_ref[rows, :] = cs
            carry = cs[CUM_BLOCK - 1:CUM_BLOCK, :]

    spec = pl.BlockSpec((seq_len, LANES), lambda b: (b, 0))
    return pl.pallas_call(
        body, name="fprep_fwd", grid=(n_seq,), in_specs=[spec, pl.BlockSpec((1, LANES), lambda b: (0, 0))],
        out_specs=spec, out_shape=jax.ShapeDtypeStruct((n, LANES), F32), compiler_params=_cparams("parallel"),
    )(fl, bf)


def _fprep_bwd(dcum, fl, bf, n_seq):
    n = fl.shape[0]
    seq_len = n // n_seq
    nb = seq_len // CUM_BLOCK

    def body(dcum_ref, fl_ref, bf_ref, dfl_ref, dbf_ref):
        triu = _tri(False)
        lane = lax.broadcasted_iota(jnp.int32, (CUM_BLOCK, LANES), 1)
        carry = jnp.zeros((1, LANES), F32)
        total = jnp.zeros((1, LANES), F32)
        for blk in reversed(range(nb)):
            rows = slice(blk * CUM_BLOCK, (blk + 1) * CUM_BLOCK)
            rs = jnp.dot(triu, dcum_ref[rows, :], preferred_element_type=F32, precision=HIGHEST) + carry
            carry = rs[0:1, :]
            _, vjp = jax.vjp(jax.nn.log_sigmoid, fl_ref[rows, :] + bf_ref[...])
            dz = jnp.where(lane < N_HEADS, vjp(rs)[0], 0.0)
            dfl_ref[rows, :] = dz
            total = total + jnp.sum(dz, axis=0, keepdims=True)
        dbf_ref[0] = total

    spec = pl.BlockSpec((seq_len, LANES), lambda b: (b, 0))
    return pl.pallas_call(
        body, name="fprep_bwd", grid=(n_seq,), in_specs=[spec, spec, pl.BlockSpec((1, LANES), lambda b: (0, 0))],
        out_specs=(spec, pl.BlockSpec((1, 1, LANES), lambda b: (b, 0, 0))),
        out_shape=(jax.ShapeDtypeStruct((n, LANES), F32), jax.ShapeDtypeStruct((n_seq, 1, LANES), F32)),
        compiler_params=_cparams("parallel"),
    )(dcum, fl, bf)


ATT_TQ = 256
ATT_KSTEP = 256
ATT_SCALE = HEAD_DIM ** -0.5
NEG_BIG = -1e30


assert ATT_KSTEP == ATT_TQ


def _scores(q_scaled, kb, row_bias, ck, kend):
    s = _dot_nt(q_scaled, kb) - ck
    if row_bias is not None:
        s = s + row_bias
    r = lax.broadcasted_iota(jnp.int32, (ATT_TQ, ATT_TQ), 0)
    c = lax.broadcasted_iota(jnp.int32, (ATT_TQ, ATT_TQ), 1)
    diag = jnp.where(r >= c, s[:, kend - ATT_TQ:], NEG_BIG)
    return diag if kend == ATT_TQ else jnp.concatenate([s[:, :kend - ATT_TQ], diag], axis=1)


def _attn_specs(n_seq, seq_len):
    nq = seq_len // ATT_TQ
    q_spec = pl.BlockSpec((ATT_TQ, LANES), lambda b, h, q: (b * nq + q, h))
    k_spec = pl.BlockSpec((seq_len, LANES), lambda b, h, q: (b, N_HEADS // 2 + h))
    v_spec = pl.BlockSpec((seq_len, LANES), lambda b, h, q: (b, N_HEADS + h))
    cq_spec = pl.BlockSpec((1, 2, ATT_TQ, 1), lambda b, h, q: (b, h, q, 0))
    ck_spec = pl.BlockSpec((1, 2, 1, seq_len), lambda b, h, q: (b, h, 0, 0))
    return nq, q_spec, k_spec, v_spec, cq_spec, ck_spec


def _head_selectors():
    head0 = lax.broadcasted_iota(jnp.int32, (1, LANES), 1) < HEAD_DIM
    return head0, (head0, jnp.logical_not(head0))


def _for_key_range(qi, seq_len, run):
    per = ATT_KSTEP // ATT_TQ
    for g in range(seq_len // ATT_KSTEP):
        pl.when(qi // per == g)(functools.partial(run, (g + 1) * ATT_KSTEP))


def _attn_fwd(qkv, cq, ck, n_seq):
    n = qkv.shape[0]
    seq_len = n // n_seq
    nq, q_spec, k_spec, v_spec, cq_spec, ck_spec = _attn_specs(n_seq, seq_len)

    def body(q_ref, k_ref, v_ref, cq_ref, ck_ref, o_ref, lse_ref):
        qi = pl.program_id(2)
        q2 = q_ref[...]
        head0, sels = _head_selectors()
        qe = [jnp.where(sel, q2 * ATT_SCALE, 0.0).astype(BF16) for sel in sels]

        def run(kend):
            kb = k_ref[0:kend, :].astype(BF16)
            vb = v_ref[0:kend, :].astype(BF16)
            outs = []
            ones = jnp.ones((kend, LANES), BF16)
            for e in range(2):
                s = _scores(qe[e], kb, None, ck_ref[0, e, :, 0:kend], kend)
                mx = jnp.max(s, axis=1, keepdims=True)
                p = jnp.exp(s - mx).astype(BF16)
                den = _dot(p, ones)[:, 0:1]
                outs.append(_dot(p, vb) / den)
                lse_ref[0, e] = cq_ref[0, e] + mx + jnp.log(den)
            o_ref[...] = jnp.where(head0, outs[0], outs[1])

        _for_key_range(qi, seq_len, run)

    return pl.pallas_call(
        body, name="attn_fwd", grid=(n_seq, N_HEADS // 2, nq),
        in_specs=[q_spec, k_spec, v_spec, cq_spec, ck_spec],
        out_specs=(q_spec, cq_spec),
        out_shape=(jax.ShapeDtypeStruct((n, D_ATTN), F32), jax.ShapeDtypeStruct((n_seq, N_HEADS, seq_len, 1), F32)),
        compiler_params=_cparams("parallel", "parallel", "parallel"),
    )(qkv, qkv, qkv, cq, ck)


def _attn_bwd(qkv, cq, ck, o, do, lse, n_seq):
    n = qkv.shape[0]
    seq_len = n // n_seq
    nq, q_spec, k_spec, v_spec, cq_spec, ck_spec = _attn_specs(n_seq, seq_len)
    kv_out = pl.BlockSpec((seq_len, LANES), lambda b, h, q: (b, h))

    def body(q_ref, k_ref, v_ref, cq_ref, ck_ref, o_ref, do_ref, lse_ref, dq_ref, dk_ref, dv_ref, dcq_ref, dck_ref):
        qi = pl.program_id(2)

        @pl.when(qi == 0)
        def _():
            dk_ref[...] = jnp.zeros_like(dk_ref)
            dv_ref[...] = jnp.zeros_like(dv_ref)
            dck_ref[...] = jnp.zeros_like(dck_ref)
        q2 = q_ref[...]
        do2 = do_ref[...]
        o2 = o_ref[...]
        head0, sels = _head_selectors()
        qe = [jnp.where(sel, q2 * ATT_SCALE, 0.0).astype(BF16) for sel in sels]
        doe = [jnp.where(sel, do2, 0.0) for sel in sels]
        doe_b = [d.astype(BF16) for d in doe]
        delta = [jnp.sum(d * o2, axis=1, keepdims=True) for d in doe]

        def run(kend):
            kb = k_ref[0:kend, :].astype(BF16)
            vb = v_ref[0:kend, :].astype(BF16)
            dqs = []
            dk = jnp.zeros((kend, LANES), F32)
            dv = jnp.zeros((kend, LANES), F32)
            ones_k = jnp.ones((kend, LANES), BF16)
            ones_q = jnp.ones((SUBLANES, ATT_TQ), BF16)
            for e in range(2):
                p = jnp.exp(_scores(qe[e], kb, cq_ref[0, e] - lse_ref[0, e], ck_ref[0, e, :, 0:kend], kend))
                ds = p * (_dot_nt(doe_b[e], vb) - delta[e])
                ds_b = ds.astype(BF16)
                dqs.append(_dot(ds_b, kb))
                dk = dk + _dot_tn(ds_b, qe[e])
                dv = dv + _dot_tn(p.astype(BF16), doe_b[e])
                dcq_ref[0, e] = _dot(ds_b, ones_k)[:, 0:1]
                dck_ref[0, e, :, 0:kend] -= _dot(ones_q, ds_b)[0:1]
            dk_ref[0:kend, :] += dk
            dv_ref[0:kend, :] += dv
            dq_ref[...] = jnp.where(head0, dqs[0], dqs[1]) * ATT_SCALE

        _for_key_range(qi, seq_len, run)

    return pl.pallas_call(
        body, name="attn_bwd", grid=(n_seq, N_HEADS // 2, nq),
        in_specs=[q_spec, k_spec, v_spec, cq_spec, ck_spec, q_spec, q_spec, cq_spec],
        out_specs=(q_spec, kv_out, kv_out, cq_spec, ck_spec),
        out_shape=(jax.ShapeDtypeStruct((n, D_ATTN), F32), jax.ShapeDtypeStruct((n, D_ATTN), F32),
                   jax.ShapeDtypeStruct((n, D_ATTN), F32),
                   jax.ShapeDtypeStruct((n_seq, N_HEADS, seq_len, 1), F32),
                   jax.ShapeDtypeStruct((n_seq, N_HEADS, 1, seq_len), F32)),
        compiler_params=_cparams("parallel", "parallel", "arbitrary"),
    )(qkv, qkv, qkv, cq, ck, o, do, lse)


WEIGHT_NAMES = ("norm_mix", "w_in", "b_forget", "lam_re", "lam_im", "b_re", "b_im", "c_re", "c_im", "d_skip", "log_dt",
                "w_glu", "b_glu", "q_norm", "k_norm", "norm_out_ssm", "norm_out_attn", "w_out", "norm_ffn", "w_up",
                "conv_w", "conv_b", "w_down")
SHARDED = ("w_in", "w_glu", "w_out", "w_up", "conv_w", "w_down")
ADAM_TILE = {"w_in": (257, 256), "w_glu": (64, 512), "w_out": (128, 1024), "w_up": (688, 256), "conv_w": (3, 688),
             "w_down": (344, 1024)}
PACK_ROWS = SUBLANES * LANES
ROWS_DOWN = D_FF // N_DEV
ROWS_OUT = D_MODEL // N_DEV
ROWS_GLU = D_SSM * D_SSM // N_DEV // D_MODEL


def _after_all(*arrays):
    return sum(a.reshape(-1)[0].astype(F32) for a in arrays).reshape(1, 1)


def _pad_to(a, axis, size):
    pad = [(0, 0)] * a.ndim
    pad[axis] = (0, size - a.shape[axis])
    return jnp.pad(a, pad)


def _block_diag(t, transpose):
    t4 = t.reshape(S5_BLOCKS, 8, SSM_GROUP, SSM_STATE)
    eye = jnp.eye(8, dtype=t.dtype)
    if transpose:
        e = jnp.swapaxes(t4, 2, 3)[:, :, :, None, :] * eye[None, :, None, :, None]
        return e.reshape(S5_BLOCKS, S5_STATES, LANES)
    e = t4[:, :, :, None, :] * eye[None, :, None, :, None]
    return e.reshape(S5_BLOCKS, LANES, S5_STATES)


def _block_diag_extract(m, transpose):
    if transpose:
        m5 = m.reshape(S5_BLOCKS, 8, SSM_STATE, 8, SSM_GROUP)
        d = jnp.stack([m5[:, i, :, i, :] for i in range(8)], axis=1)
        return jnp.swapaxes(d, 2, 3).reshape(N_GROUPS, SSM_GROUP, SSM_STATE)
    m5 = m.reshape(S5_BLOCKS, 8, SSM_GROUP, 8, SSM_STATE)
    d = jnp.stack([m5[:, i, :, i, :] for i in range(8)], axis=1)
    return d.reshape(N_GROUPS, SSM_GROUP, SSM_STATE)


def _pack(pieces):
    flat = jnp.concatenate([p.reshape(-1).astype(F32) for p in pieces])
    size = -(-flat.shape[0] // PACK_ROWS) * PACK_ROWS
    return _pad_to(flat, 0, size).reshape(-1, LANES)


def _unpack(packed, shapes):
    flat = packed.reshape(-1)
    out, off = [], 0
    for shp in shapes:
        size = math.prod(shp)
        out.append(flat[off:off + size].reshape(shp))
        off += size
    return out


def kernel(x, norm_mix, w_in, b_forget, lam_re, lam_im, b_re, b_im, c_re, c_im, d_skip, log_dt, w_glu, b_glu, q_norm, k_norm, norm_out_ssm, norm_out_attn, w_out, norm_ffn, w_up, conv_w, conv_b, w_down, loss_target, m_norm_mix, m_w_in, m_b_forget, m_lam_re, m_lam_im, m_b_re, m_b_im, m_c_re, m_c_im, m_d_skip, m_log_dt, m_w_glu, m_b_glu, m_q_norm, m_k_norm, m_norm_out_ssm, m_norm_out_attn, m_w_out, m_norm_ffn, m_w_up, m_conv_w, m_conv_b, m_w_down, v_norm_mix, v_w_in, v_b_forget, v_lam_re, v_lam_im, v_b_re, v_b_im, v_c_re, v_c_im, v_d_skip, v_log_dt, v_w_glu, v_b_glu, v_q_norm, v_k_norm, v_norm_out_ssm, v_norm_out_attn, v_w_out, v_norm_ffn, v_w_up, v_conv_w, v_conv_b, v_w_down):
    given = dict(locals())
    weights = {k: given[k] for k in WEIGHT_NAMES}
    mom1 = {k: given["m_" + k] for k in WEIGHT_NAMES}
    mom2 = {k: given["v_" + k] for k in WEIGHT_NAMES}
    n_seq, seq_len, _ = x.shape
    n = n_seq * seq_len
    xf = x.reshape(n, D_MODEL)
    target = loss_target.reshape(n, D_MODEL)
    me_idx = 4 * lax.axis_index("x") + 2 * lax.axis_index("y") + lax.axis_index("c")

    in_flags = [False] * 2
    in_sems = _exchange_start([jnp.swapaxes(w_in[0], 0, 1).astype(BF16), conv_w[0]], in_flags, norm_mix,
                              "gather_in_start", 3)
    my_slot = lax.broadcasted_iota(jnp.int32, (N_DEV, 1, 1), 0) == me_idx

    lr3 = lam_re[0].reshape(N_GROUPS, 1, SSM_STATE)
    li3 = lam_im[0].reshape(N_GROUPS, 1, SSM_STATE)
    ldt3 = log_dt[0].reshape(N_GROUPS, 1, 1)
    br_t = jnp.swapaxes(b_re[0], 1, 2)
    bi_t = jnp.swapaxes(b_im[0], 1, 2)
    ab_re, ab_im, bb_re, bb_im = _s5_param_fwd(lr3, li3, ldt3, br_t, bi_t)
    a_re = ab_re.reshape(S5_BLOCKS, 1, S5_STATES)
    a_im = ab_im.reshape(S5_BLOCKS, 1, S5_STATES)
    bbr = _block_diag(bb_re, False).astype(BF16)
    bbi = _block_diag(bb_im, False).astype(BF16)
    cr = _block_diag(c_re[0], True).astype(BF16)
    ci = _block_diag(c_im[0], True).astype(BF16)
    avg = jnp.kron(jnp.eye(N_HEADS, dtype=F32), jnp.full((HEAD_DIM, HEAD_DIM), 1.0 / HEAD_DIM, F32)).astype(BF16)
    qg = jnp.tile(q_norm, (1, N_HEADS))
    kg = jnp.tile(k_norm, (1, N_HEADS))
    rows_w = jnp.concatenate([w_down[0], w_out[0], w_glu[0].reshape(ROWS_GLU, D_MODEL)], axis=0).astype(BF16)

    (own_in, own_cw), (g_in, g_cw) = _exchange_wait(in_sems[0], in_sems[1], in_sems[2], in_sems[3], in_flags,
                                                    _after_all(a_re, a_im, bbr, bbi, cr, ci, avg, qg, kg, rows_w),
                                                    "gather_in_wait")
    g_in = jnp.where(my_slot, own_in[None], g_in)
    g_cw = jnp.where(my_slot, own_cw[None], g_cw)
    rest_flags = [False] * 2
    w_sems = _exchange_start([rows_w, jnp.swapaxes(w_up[0], 0, 1).astype(BF16)], rest_flags, g_in,
                             "gather_rest_start", 0)
    norm_mix = norm_mix + w_sems[4][0, 0]
    w_in_p = _pad_to(g_in.reshape(D_IN, D_MODEL), 0, D_IN_PAD)

    hn, u, qkv, raw, fl = _inproj_fwd(xf, norm_mix, w_in_p, avg, qg, kg)
    yc, st_r, st_i = _s5_fwd(u, a_re, a_im, bbr, bbi, cr, ci, d_skip, n_seq)
    bf = _pad_to(b_forget, 1, LANES)
    cum = _fprep_fwd(fl, bf, n_seq)
    cum8 = jnp.swapaxes(cum[:, :N_HEADS].reshape(n_seq, seq_len, N_HEADS), 1, 2)
    cq = cum8[:, :, :, None]
    ck = cum8[:, :, None, :]
    ya, lse = _attn_fwd(qkv, cq, ck, n_seq)
    (own_rows, own_up), (g_rows, g_up) = _exchange_wait(w_sems[0], w_sems[1], w_sems[2], w_sems[3], rest_flags, ya,
                                                        "gather_rest_wait")
    g_rows = jnp.where(my_slot, own_rows[None], g_rows)
    g_up = jnp.where(my_slot, own_up[None], g_up)
    g_down = g_rows[:, :ROWS_DOWN]
    g_out = g_rows[:, ROWS_DOWN:ROWS_DOWN + ROWS_OUT]
    g_glu = g_rows[:, ROWS_DOWN + ROWS_OUT:]
    w_glu_f = g_glu.reshape(D_SSM, D_SSM)
    w_out_f = g_out.reshape(D_MODEL, D_MODEL)
    conv_st = _pad_to(jnp.concatenate([g_cw, conv_b.reshape(N_DEV, 1, -1)], axis=1), 1, SUBLANES)
    w_down4 = g_down.reshape(FFN_GROUPS, FFN_GROUP, D_MODEL)
    ys = _glu_fwd(yc, w_glu_f, b_glu)
    h1, hn2, mixed = _mix_fwd(xf, ys, ya, norm_out_ssm, norm_out_attn, w_out_f, norm_ffn)
    ug, uv, pg, pv, dy, loss_part = _ffn_fwd(hn2, h1, target, g_up, conv_st, w_down4, seq_len)
    loss_local = 0.5 * jnp.sum(loss_part) / D_MODEL

    dug, duv, act, dhn2, dcg, dcv = _ffn_bwd(dy, ug, uv, pg, pv, g_up, conv_st, w_down4, seq_len)
    dh1, dys, dya, d_gs, d_ga, d_gf = _mix_bwd(dy, dhn2[None], h1, ys, ya, norm_out_ssm, norm_out_attn, w_out_f, norm_ffn)
    dyc, gl_b, dz_b, d_bglu = _glu_bwd(yc, dys, w_glu_f, b_glu)

    gw_glu = _tn_matmul(gl_b, dz_b, "dw_glu", D_SSM, D_SSM)
    gw_out = _tn_matmul(mixed, dh1, "dw_out", D_MODEL, D_MODEL)
    gw_up = jnp.concatenate([_tn_grouped(dug, hn2, "dw_up_gate", False, BF16),
                             _tn_grouped(duv, hn2, "dw_up_val", False, BF16)], axis=0)
    gw_down = _tn_grouped(act, dy, "dw_down", False)
    g_conv = jnp.concatenate([dcg, dcv], axis=0)
    by_cols = lambda g, c: jnp.swapaxes(g.reshape(g.shape[0], N_DEV, c), 0, 1)
    early_flags = [True] * 2
    rows_g = jnp.concatenate([gw_down.reshape(N_DEV, ROWS_DOWN, D_MODEL), gw_out.reshape(N_DEV, ROWS_OUT, D_MODEL),
                              gw_glu.reshape(N_DEV, ROWS_GLU, D_MODEL)], axis=1).astype(BF16)
    g_sems = _exchange_start([rows_g, gw_up], early_flags, dyc, "grad_early_start", 1)
    started = g_sems[4][0, 0]

    du, dbbr, dbbi, dcr, dci, dar, dai, ddk = _s5_bwd(u, dyc, st_r, st_i, a_re, a_im, bbr, bbi, cr, ci,
                                                      d_skip + started, n_seq)
    partial_early = {
        "ab_re": jnp.sum(dar, axis=1), "ab_im": jnp.sum(dai, axis=1),
        "bb_re": _block_diag_extract(dbbr, False), "bb_im": _block_diag_extract(dbbi, False),
        "c_re": _block_diag_extract(dcr, True), "c_im": _block_diag_extract(dci, True),
        "d_skip": jnp.sum(ddk, axis=1), "b_glu": d_bglu,
        "norm_out_ssm": d_gs, "norm_out_attn": d_ga, "norm_ffn": d_gf, "conv_b": g_conv[:, 3],
    }
    early_keys = tuple(partial_early)
    early_shapes = [partial_early[k].shape for k in early_keys]
    p_sems = _exchange_start([_pack([partial_early[k] for k in early_keys])], [False], du, "small_early_start", 2)
    started = started + p_sems[4][0, 0]

    dqn, dkn, dv, dcq, dck = _attn_bwd(qkv, cq, ck + started, ya, dya, lse, n_seq)
    dcum8 = dcq[:, :, :, 0] + dck.reshape(n_seq, N_HEADS, seq_len)
    dcum = _pad_to(jnp.swapaxes(dcum8, 1, 2).reshape(n, N_HEADS), 1, LANES)
    dfl, dbf = _fprep_bwd(dcum, fl, bf, n_seq)
    dx, dproj, d_gmix, d_qg, d_kg = _inproj_bwd(xf, norm_mix, w_in_p, avg, qg, kg, raw, du, dqn, dkn, dv, dfl, dh1)

    gw_in = _tn_matmul(dproj, hn, "dw_in", D_IN_PAD, D_MODEL, out_rows=D_IN)
    partial_late = {
        "norm_mix": d_gmix, "b_forget": jnp.sum(dbf, axis=(0, 1))[:N_HEADS],
        "q_norm": jnp.sum(d_qg.reshape(N_HEADS, HEAD_DIM), axis=0),
        "k_norm": jnp.sum(d_kg.reshape(N_HEADS, HEAD_DIM), axis=0), "loss": loss_local.reshape(1),
    }
    late_keys = tuple(partial_late)
    late_shapes = [partial_late[k].shape for k in late_keys]

    late_flags = [True, True, False]
    l_sems = _exchange_start(
        [gw_in.reshape(N_DEV, D_IN // N_DEV, D_MODEL).astype(BF16), g_conv[:, :3],
         _pack([partial_late[k] for k in late_keys])],
        late_flags, dx, "grad_late_start", 4)
    (src_rows, src_up), (land_rows, land_up) = _exchange_wait(g_sems[0], g_sems[1], g_sems[2], g_sems[3], early_flags,
                                                              l_sems[4], "grad_early_wait")
    own_rows = lax.dynamic_index_in_dim(src_rows, me_idx, 0, keepdims=False)
    land = {"w_up": land_up, "w_down": land_rows[:, :ROWS_DOWN], "w_out": land_rows[:, ROWS_DOWN:ROWS_DOWN + ROWS_OUT],
            "w_glu": land_rows[:, ROWS_DOWN + ROWS_OUT:].reshape(N_DEV, -1, D_SSM)}
    own = {"w_up": lax.dynamic_index_in_dim(src_up, me_idx, 0, keepdims=False), "w_down": own_rows[:ROWS_DOWN],
           "w_out": own_rows[ROWS_DOWN:ROWS_DOWN + ROWS_OUT], "w_glu": own_rows[ROWS_DOWN + ROWS_OUT:].reshape(-1, D_SSM)}
    grads, deltas, new_m, new_v = {}, {}, {}, {}

    def adam_shard(name):
        flip = (lambda a: jnp.swapaxes(a, 1, 2)) if name in ("w_in", "w_up") else (lambda a: a)
        outs = _adam_sharded(land[name], own[name], flip(weights[name]), flip(mom1[name]), flip(mom2[name]),
                             "adam_" + name, ADAM_TILE[name])
        grads[name], deltas[name], new_m[name], new_v[name] = [flip(o) for o in outs]

    for name in ("w_up", "w_down", "w_out", "w_glu"):
        adam_shard(name)
    (own_pack,), (early_parts,) = _exchange_wait(p_sems[0], p_sems[1], p_sems[2], p_sems[3], [False], land_up,
                                                 "small_early_wait")
    early_sum = _sum_partials(early_parts, own_pack, "sum_early_partials")
    (src_in, src_cw, own_late), (land["w_in"], land["conv_w"], late_parts) = _exchange_wait(
        l_sems[0], l_sems[1], l_sems[2], l_sems[3], late_flags,
        _after_all(early_sum, *[new_v[k] for k in ("w_up", "w_down", "w_out", "w_glu")]), "grad_late_wait")
    own["w_in"] = lax.dynamic_index_in_dim(src_in, me_idx, 0, keepdims=False)
    own["conv_w"] = lax.dynamic_index_in_dim(src_cw, me_idx, 0, keepdims=False)
    for name in ("w_in", "conv_w"):
        adam_shard(name)

    summed = dict(zip(late_keys, _unpack(_sum_partials(late_parts, own_late, "sum_late_partials"), late_shapes)))
    summed.update(zip(early_keys, _unpack(early_sum, early_shapes)))
    dlr, dli, dldt, dbr_t, dbi_t = _s5_param_bwd(
        lr3, li3, ldt3, br_t, bi_t, summed["ab_re"].reshape(lr3.shape), summed["ab_im"].reshape(lr3.shape),
        summed["bb_re"], summed["bb_im"])
    small_grads = {
        "norm_mix": summed["norm_mix"], "b_forget": summed["b_forget"], "lam_re": dlr, "lam_im": dli,
        "b_re": dbr_t, "b_im": dbi_t, "c_re": summed["c_re"], "c_im": summed["c_im"],
        "d_skip": summed["d_skip"], "log_dt": dldt, "b_glu": summed["b_glu"], "q_norm": summed["q_norm"],
        "k_norm": summed["k_norm"], "norm_out_ssm": summed["norm_out_ssm"], "norm_out_attn": summed["norm_out_attn"],
        "norm_ffn": summed["norm_ffn"], "conv_b": summed["conv_b"],
    }
    repl = tuple(k for k in WEIGHT_NAMES if k not in SHARDED)
    turn = lambda k, a: jnp.swapaxes(a, 2, 3) if k in ("b_re", "b_im") else a
    w_list = [turn(k, weights[k]) for k in repl]
    g_list = [small_grads[k].reshape(w.shape) for k, w in zip(repl, w_list)]
    d_list, m_list, v_list = _adam_replicated(g_list, w_list, [turn(k, mom1[k]) for k in repl],
                                              [turn(k, mom2[k]) for k in repl], "adam_replicated")
    for k, g, d, nm, nv in zip(repl, g_list, d_list, m_list, v_list):
        grads[k], deltas[k], new_m[k], new_v[k] = turn(k, g), turn(k, d), turn(k, nm), turn(k, nv)

    grad_x = dx.reshape(x.shape)
    loss = summed["loss"].reshape(())
    return (loss, grad_x, *[grads[k] for k in WEIGHT_NAMES], *[deltas[k] for k in WEIGHT_NAMES],
            *[new_m[k] for k in WEIGHT_NAMES], *[new_v[k] for k in WEIGHT_NAMES])
```

```python
import functools
import math

import jax
import jax.numpy as jnp
from jax import lax
from jax.experimental import pallas as pl
from jax.experimental.pallas import tpu as pltpu

F32 = jnp.float32
BF16 = jnp.bfloat16
HIGHEST = lax.Precision.HIGHEST

N_DEV = 8
D_MODEL = 1024
D_SSM = 512
D_ATTN = 512
N_HEADS = 8
HEAD_DIM = 64
N_GROUPS = 32
SSM_GROUP = 16
SSM_STATE = 64
D_FF = 2752
D_FF_PAD = 2816
D_IN = 2056
D_IN_PAD = 2176
EPS = 1e-6
LANES = 128
SUBLANES = 8
VMEM_LIMIT = 56 * 1024 * 1024

ADAM_LR = 0.001
ADAM_B1 = 0.9
ADAM_B2 = 0.999
ADAM_EPS = 1e-08
ADAM_WD = 0.01
ADAM_STEP = 10


def _cparams(*sem):
    return pltpu.CompilerParams(dimension_semantics=sem, vmem_limit_bytes=VMEM_LIMIT)


def _dot(a, b, **kw):
    return jnp.dot(a, b, preferred_element_type=F32, **kw)


def _dot_nt(a, b):
    return lax.dot_general(a, b, (((1,), (1,)), ((), ())), preferred_element_type=F32)


def _dot_tn(a, b):
    return lax.dot_general(a, b, (((0,), (0,)), ((), ())), preferred_element_type=F32)


def _rms(x, g):
    return x * lax.rsqrt(jnp.mean(x * x, axis=-1, keepdims=True) + EPS) * g


def _split_dot(x, avg):
    hi = x.astype(BF16)
    lo = (x - hi.astype(F32)).astype(BF16)
    return _dot(hi, avg) + _dot(lo, avg)


@jax.custom_vjp
def _group_mean(x, avg):
    return _split_dot(x, avg)


def _group_mean_fwd(x, avg):
    return _split_dot(x, avg), avg


def _group_mean_bwd(avg, ct):
    return _split_dot(ct, avg), jnp.zeros_like(avg)


_group_mean.defvjp(_group_mean_fwd, _group_mean_bwd)


def _headnorm(q, avg, g):
    return q * lax.rsqrt(_group_mean(q * q, avg) + EPS) * g


def _exchange(srcs, scatter_flags, name):
    n = len(srcs)
    out_shape = []
    for s, sc in zip(srcs, scatter_flags):
        shp = s.shape if sc else (N_DEV,) + s.shape
        out_shape.append(jax.ShapeDtypeStruct(shp, s.dtype))

    def body(*refs):
        src = refs[:n]
        dst = refs[n:2 * n]
        send_sems, recv_sems, loc_sems = refs[2 * n:]
        x, y, c = lax.axis_index("x"), lax.axis_index("y"), lax.axis_index("c")
        me = 4 * x + 2 * y + c
        peers = []
        for j in range(1, N_DEV):
            px = 1 - x if (j >> 2) & 1 else x
            py = 1 - y if (j >> 1) & 1 else y
            pc = 1 - c if j & 1 else c
            peers.append(((px, py, pc), 4 * px + 2 * py + pc))
        local, sends = [], []
        for k in range(n):
            own = src[k].at[me] if scatter_flags[k] else src[k]
            lc = pltpu.make_async_copy(own, dst[k].at[me], loc_sems.at[k])
            lc.start()
            local.append(lc)
            for j, (pid, pidx) in enumerate(peers):
                s = src[k].at[pidx] if scatter_flags[k] else src[k]
                cp = pltpu.make_async_remote_copy(
                    src_ref=s, dst_ref=dst[k].at[me], send_sem=send_sems.at[k, j], recv_sem=recv_sems.at[k, j],
                    device_id=pid, device_id_type=pl.DeviceIdType.MESH)
                cp.start()
                sends.append(cp)
        for k in range(n):
            for j, (pid, pidx) in enumerate(peers):
                s = src[k].at[pidx] if scatter_flags[k] else src[k]
                pltpu.make_async_remote_copy(
                    src_ref=s, dst_ref=dst[k].at[pidx], send_sem=send_sems.at[k, j], recv_sem=recv_sems.at[k, j],
                    device_id=pid, device_id_type=pl.DeviceIdType.MESH).wait_recv()
        for cp in sends:
            cp.wait_send()
        for lc in local:
            lc.wait()

    any_spec = pl.BlockSpec(memory_space=pl.ANY)
    return pl.pallas_call(
        body, name=name, out_shape=tuple(out_shape),
        in_specs=[any_spec] * n, out_specs=tuple([any_spec] * n),
        scratch_shapes=[pltpu.SemaphoreType.DMA((n, N_DEV - 1)), pltpu.SemaphoreType.DMA((n, N_DEV - 1)),
                        pltpu.SemaphoreType.DMA((n,))],
        compiler_params=pltpu.CompilerParams(has_side_effects=True),
    )(*srcs)


def _peer_list():
    x, y, c = lax.axis_index("x"), lax.axis_index("y"), lax.axis_index("c")
    peers = []
    for j in range(1, N_DEV):
        px = 1 - x if (j >> 2) & 1 else x
        py = 1 - y if (j >> 1) & 1 else y
        pc = 1 - c if j & 1 else c
        peers.append(((px, py, pc), 4 * px + 2 * py + pc))
    return 4 * x + 2 * y + c, peers


def _split_copies(src, land, send_sems, recv_sems, scatter_flags, me, peers, incoming):
    copies = []
    for k in range(len(src)):
        for j, (pid, pidx) in enumerate(peers):
            s = src[k].at[pidx] if scatter_flags[k] else src[k]
            i = k * (N_DEV - 1) + j
            copies.append(pltpu.make_async_remote_copy(
                src_ref=s, dst_ref=land[k].at[pidx if incoming else me], send_sem=send_sems[i],
                recv_sem=recv_sems[i], device_id=pid, device_id_type=pl.DeviceIdType.MESH))
    return copies


def _exchange_start(srcs, scatter_flags, after, name, collective_id):
    n = len(srcs)
    ns = n * (N_DEV - 1)
    hbm = pl.BlockSpec(memory_space=pltpu.HBM)
    sem = pl.BlockSpec(memory_space=pltpu.SEMAPHORE)
    land_shapes = [s.shape if sc else (N_DEV,) + s.shape for s, sc in zip(srcs, scatter_flags)]

    def body(*refs):
        src, land = refs[:n], refs[n:2 * n]
        send_sems = refs[2 * n + 1:2 * n + 1 + ns]
        recv_sems = refs[2 * n + 1 + ns:2 * n + 1 + 2 * ns]
        token = refs[4 * n + 1 + 2 * ns]
        me, peers = _peer_list()
        barrier = pltpu.get_barrier_semaphore()
        for pid, _ in peers:
            pl.semaphore_signal(barrier, inc=1, device_id=pid, device_id_type=pl.DeviceIdType.MESH)
        pl.semaphore_wait(barrier, N_DEV - 1)
        for cp in _split_copies(src, land, send_sems, recv_sems, scatter_flags, me, peers, False):
            cp.start()
        token[...] = jnp.zeros_like(token)

    outs = pl.pallas_call(
        body, name=name,
        out_shape=(*[pltpu.SemaphoreType.DMA(())] * (2 * ns), *[pltpu.HBM(s.shape, s.dtype) for s in srcs],
                   *[pltpu.HBM(shp, s.dtype) for shp, s in zip(land_shapes, srcs)],
                   jax.ShapeDtypeStruct((SUBLANES, LANES), F32)),
        in_specs=[hbm] * (2 * n) + [pl.BlockSpec(memory_space=pl.ANY)],
        out_specs=(*[sem] * (2 * ns), *[hbm] * (2 * n), pl.BlockSpec(memory_space=pltpu.VMEM)),
        input_output_aliases={i: 2 * ns + i for i in range(2 * n)},
        compiler_params=pltpu.CompilerParams(has_side_effects=pltpu.SideEffectType.DATAFLOW_SIDE_EFFECTING,
                                             collective_id=collective_id),
    )(*[pltpu.with_memory_space_constraint(s, pltpu.HBM) for s in srcs],
      *[pltpu.with_memory_space_constraint(lax.empty(shp, s.dtype), pltpu.HBM) for shp, s in zip(land_shapes, srcs)],
      after)
    return (outs[:ns], outs[ns:2 * ns], outs[2 * ns:2 * ns + n], outs[2 * ns + n:2 * ns + 2 * n], outs[2 * ns + 2 * n])


def _exchange_wait(send_sems, recv_sems, srcs, lands, scatter_flags, after, name):
    n = len(srcs)
    ns = n * (N_DEV - 1)
    hbm = pl.BlockSpec(memory_space=pltpu.HBM)
    sem = pl.BlockSpec(memory_space=pltpu.SEMAPHORE)

    def body(*refs):
        src, land = refs[:n], refs[n:2 * n]
        s_sems = refs[2 * n:2 * n + ns]
        r_sems = refs[2 * n + ns:2 * n + 2 * ns]
        me, peers = _peer_list()
        for cp in _split_copies(src, land, s_sems, r_sems, scatter_flags, me, peers, True):
            cp.wait_send()
            cp.wait_recv()

    outs = pl.pallas_call(
        body, name=name,
        out_shape=tuple(pltpu.HBM(a.shape, a.dtype) for a in (*srcs, *lands)),
        in_specs=[hbm] * (2 * n) + [sem] * (2 * ns) + [pl.BlockSpec(memory_space=pl.ANY)],
        out_specs=tuple([hbm] * (2 * n)),
        input_output_aliases={i: i for i in range(2 * n)},
        compiler_params=pltpu.CompilerParams(has_side_effects=pltpu.SideEffectType.DATAFLOW_SIDE_EFFECTING),
    )(*srcs, *lands, *send_sems, *recv_sems, after)
    return outs[:n], outs[n:]


def _tn_matmul(a, b, name, tk, tm, out_rows=None, out_cols=None, tn=512):
    n_tok, k_dim = a.shape
    m_dim = b.shape[1]
    grid = (k_dim // tk, m_dim // tm, n_tok // tn)

    def body(a_ref, b_ref, o_ref):
        @pl.when(pl.program_id(2) == 0)
        def _():
            o_ref[...] = jnp.zeros_like(o_ref)
        o_ref[...] += _dot_tn(a_ref[...].astype(BF16), b_ref[...].astype(BF16))

    return pl.pallas_call(
        body, name=name, grid=grid,
        in_specs=[pl.BlockSpec((tn, tk), lambda i, j, k: (k, i)), pl.BlockSpec((tn, tm), lambda i, j, k: (k, j))],
        out_specs=pl.BlockSpec((tk, tm), lambda i, j, k: (i, j)),
        out_shape=jax.ShapeDtypeStruct((out_rows or k_dim, out_cols or m_dim), F32),
        compiler_params=_cparams("parallel", "parallel", "arbitrary"),
    )(a, b)


def _adam_math(g, w, m, v):
    m = ADAM_B1 * m + (1.0 - ADAM_B1) * g
    v = ADAM_B2 * v + (1.0 - ADAM_B2) * (g * g)
    m_hat = m / (1.0 - ADAM_B1 ** ADAM_STEP)
    v_hat = v / (1.0 - ADAM_B2 ** ADAM_STEP)
    delta = -ADAM_LR * (m_hat / (jnp.sqrt(v_hat) + ADAM_EPS) + ADAM_WD * w)
    return delta, m, v


def _adam_sharded(land, own, w, m, v, name, tile):
    _, r, c = w.shape

    def body(*refs):
        l_ref = refs[0]
        own_ref = refs[1] if own is not None else None
        w_ref, m_ref, v_ref, g_ref, d_ref, nm_ref, nv_ref = [ref.at[0] for ref in refs[-7:]]
        if own_ref is not None:
            x, y, z = lax.axis_index("x"), lax.axis_index("y"), lax.axis_index("c")
            me = 4 * x + 2 * y + z
            mine = own_ref[...].astype(F32)
        g = None
        for s in range(N_DEV):
            part = l_ref[s].astype(F32)
            if own_ref is not None:
                part = jnp.where(me == s, mine, part)
            g = part if g is None else g + part
        d, nm, nv = _adam_math(g, w_ref[...], m_ref[...], v_ref[...])
        g_ref[...] = g
        d_ref[...] = d
        nm_ref[...] = nm
        nv_ref[...] = nv

    tr, tc = tile
    spec = pl.BlockSpec((1, tr, tc), lambda i, j: (0, i, j))
    own_specs, own_args = ([pl.BlockSpec((tr, tc), lambda i, j: (i, j))], [own]) if own is not None else ([], [])
    return pl.pallas_call(
        body, name=name, grid=(r // tr, c // tc),
        in_specs=[pl.BlockSpec((N_DEV, tr, tc), lambda i, j: (0, i, j)), *own_specs, spec, spec, spec],
        out_specs=(spec, spec, spec, spec),
        out_shape=tuple(jax.ShapeDtypeStruct((1, r, c), F32) for _ in range(4)),
        compiler_params=_cparams("parallel", "parallel"),
    )(land, *own_args, w, m, v)


def _sum_partials(parts, own, name):
    _, r, c = parts.shape

    def body(*refs):
        p_ref, o_ref = refs[0], refs[-1]
        if own is not None:
            x, y, z = lax.axis_index("x"), lax.axis_index("y"), lax.axis_index("c")
            me = 4 * x + 2 * y + z
            mine = refs[1][...]
        g = None
        for s in range(N_DEV):
            part = p_ref[s]
            if own is not None:
                part = jnp.where(me == s, mine, part)
            g = part if g is None else g + part
        o_ref[...] = g

    args = (parts,) if own is None else (parts, own)
    return pl.pallas_call(body, name=name, out_shape=jax.ShapeDtypeStruct((r, c), F32),
                          compiler_params=pltpu.CompilerParams(vmem_limit_bytes=VMEM_LIMIT))(*args)


def _adam_replicated(gs, ws, ms, vs, name):
    k = len(ws)

    def body(*refs):
        outs = refs[4 * k:]
        for i in range(k):
            d, nm, nv = _adam_math(refs[i][...], refs[k + i][...], refs[2 * k + i][...], refs[3 * k + i][...])
            outs[i][...] = d
            outs[k + i][...] = nm
            outs[2 * k + i][...] = nv

    outs = pl.pallas_call(body, name=name, out_shape=tuple(jax.ShapeDtypeStruct(w.shape, F32) for w in ws) * 3,
                          compiler_params=pltpu.CompilerParams(vmem_limit_bytes=VMEM_LIMIT))(*gs, *ws, *ms, *vs)
    return outs[:k], outs[k:2 * k], outs[2 * k:]


def _inproj_fwd(x, g, w_in, avg, qg, kg, tm=512):
    n = x.shape[0]

    def body(x_ref, g_ref, w_ref, a_ref, qg_ref, kg_ref, hn_ref, u_ref, qkv_ref, raw_ref, fl_ref):
        hn = _rms(x_ref[...], g_ref[...]).astype(BF16)
        hn_ref[...] = hn
        proj = _dot_nt(hn, w_ref[...])
        u_ref[...] = proj[:, 0:512]
        q = proj[:, 512:1024]
        k = proj[:, 1024:1536]
        raw_ref[:, 0:512] = q
        raw_ref[:, 512:1024] = k
        qkv_ref[:, 0:512] = _headnorm(q, a_ref[...], qg_ref[...])
        qkv_ref[:, 512:1024] = _headnorm(k, a_ref[...], kg_ref[...])
        qkv_ref[:, 1024:1536] = proj[:, 1536:2048]
        fl_ref[...] = proj[:, 2048:D_IN_PAD]

    row = lambda w: pl.BlockSpec((tm, w), lambda i: (i, 0))
    full = lambda a: pl.BlockSpec(a.shape, lambda i: (0,) * a.ndim)
    return pl.pallas_call(
        body, name="inproj_fwd", grid=(n // tm,),
        in_specs=[row(D_MODEL), full(g), full(w_in), full(avg), full(qg), full(kg)],
        out_specs=(row(D_MODEL), row(512), row(1536), row(1024), row(LANES)),
        out_shape=(jax.ShapeDtypeStruct((n, D_MODEL), BF16), jax.ShapeDtypeStruct((n, 512), F32),
                   jax.ShapeDtypeStruct((n, 1536), F32), jax.ShapeDtypeStruct((n, 1024), F32),
                   jax.ShapeDtypeStruct((n, LANES), F32)),
        compiler_params=_cparams("parallel"),
    )(x, g, w_in, avg, qg, kg)


def _inproj_bwd(x, g, w_in, avg, qg, kg, raw, du, dqn, dkn, dv, dfl, dres, tm=512):
    n = x.shape[0]

    def body(x_ref, g_ref, w_ref, a_ref, qg_ref, kg_ref, raw_ref, du_ref, dqn_ref, dkn_ref, dv_ref, dfl_ref, dres_ref,
             dx_ref, dproj_ref, dg_ref, dqg_ref, dkg_ref):
        @pl.when(pl.program_id(0) == 0)
        def _():
            dg_ref[...] = jnp.zeros_like(dg_ref)
            dqg_ref[...] = jnp.zeros_like(dqg_ref)
            dkg_ref[...] = jnp.zeros_like(dkg_ref)
        avg_m = a_ref[...]
        _, vjp_q = jax.vjp(lambda q, gg: _headnorm(q, avg_m, gg), raw_ref[:, 0:512], qg_ref[...])
        dq, dqg = vjp_q(dqn_ref[...])
        _, vjp_k = jax.vjp(lambda k, gg: _headnorm(k, avg_m, gg), raw_ref[:, 512:1024], kg_ref[...])
        dk, dkg = vjp_k(dkn_ref[...])
        dproj = jnp.concatenate([du_ref[...], dq, dk, dv_ref[...], dfl_ref[...]], axis=1).astype(BF16)
        dproj_ref[...] = dproj
        dhn = _dot(dproj, w_ref[...])
        _, vjp_x = jax.vjp(_rms, x_ref[...], g_ref[...])
        dxn, dg = vjp_x(dhn)
        dx_ref[...] = dxn + dres_ref[...]
        dg_ref[...] += dg
        dqg_ref[...] += dqg
        dkg_ref[...] += dkg

    row = lambda w: pl.BlockSpec((tm, w), lambda i: (i, 0))
    full = lambda a: pl.BlockSpec(a.shape, lambda i: (0,) * a.ndim)
    vec = lambda w: pl.BlockSpec((1, w), lambda i: (0, 0))
    return pl.pallas_call(
        body, name="inproj_bwd", grid=(n // tm,),
        in_specs=[row(D_MODEL), full(g), full(w_in), full(avg), full(qg), full(kg), row(1024), row(512), row(512),
                  row(512), row(512), row(LANES), row(D_MODEL)],
        out_specs=(row(D_MODEL), row(D_IN_PAD), vec(D_MODEL), vec(512), vec(512)),
        out_shape=(jax.ShapeDtypeStruct((n, D_MODEL), F32), jax.ShapeDtypeStruct((n, D_IN_PAD), BF16),
                   jax.ShapeDtypeStruct((1, D_MODEL), F32), jax.ShapeDtypeStruct((1, 512), F32),
                   jax.ShapeDtypeStruct((1, 512), F32)),
        compiler_params=_cparams("arbitrary"),
    )(x, g, w_in, avg, qg, kg, raw, du, dqn, dkn, dv, dfl, dres)


def _glu_fwd(yc, wg, bg, tm=512):
    n = yc.shape[0]

    def body(yc_ref, w_ref, b_ref, ys_ref):
        gl = jax.nn.gelu(yc_ref[...])
        z = _dot(gl.astype(BF16), w_ref[...]) + b_ref[...]
        ys_ref[...] = gl * jax.nn.sigmoid(z)

    row = pl.BlockSpec((tm, 512), lambda i: (i, 0))
    full = lambda a: pl.BlockSpec(a.shape, lambda i: (0,) * a.ndim)
    return pl.pallas_call(
        body, name="glu_fwd", grid=(n // tm,), in_specs=[row, full(wg), full(bg)], out_specs=row,
        out_shape=jax.ShapeDtypeStruct((n, 512), F32), compiler_params=_cparams("parallel"),
    )(yc, wg, bg)


def _glu_bwd(yc, dys, wg, bg, tm=512):
    n = yc.shape[0]

    def body(yc_ref, dys_ref, w_ref, b_ref, dyc_ref, gl_ref, dz_ref, db_ref):
        @pl.when(pl.program_id(0) == 0)
        def _():
            db_ref[...] = jnp.zeros_like(db_ref)
        gl, vjp_gelu = jax.vjp(jax.nn.gelu, yc_ref[...])
        glb = gl.astype(BF16)
        z = _dot(glb, w_ref[...]) + b_ref[...]
        s = jax.nn.sigmoid(z)
        dys = dys_ref[...]
        dz = dys * gl * s * (1.0 - s)
        dzb = dz.astype(BF16)
        dgl = dys * s + _dot_nt(dzb, w_ref[...])
        dyc_ref[...] = vjp_gelu(dgl)[0]
        gl_ref[...] = glb
        dz_ref[...] = dzb
        db_ref[...] += jnp.sum(dz, axis=0, keepdims=True)

    row = pl.BlockSpec((tm, 512), lambda i: (i, 0))
    full = lambda a: pl.BlockSpec(a.shape, lambda i: (0,) * a.ndim)
    return pl.pallas_call(
        body, name="glu_bwd", grid=(n // tm,), in_specs=[row, row, full(wg), full(bg)],
        out_specs=(row, row, row, pl.BlockSpec((1, 512), lambda i: (0, 0))),
        out_shape=(jax.ShapeDtypeStruct((n, 512), F32), jax.ShapeDtypeStruct((n, 512), BF16),
                   jax.ShapeDtypeStruct((n, 512), BF16), jax.ShapeDtypeStruct((1, 512), F32)),
        compiler_params=_cparams("arbitrary"),
    )(yc, dys, wg, bg)


def _mix_fwd(x, ys, ya, gs, ga, wout, gf, tm=512):
    n = x.shape[0]

    def body(x_ref, ys_ref, ya_ref, gs_ref, ga_ref, w_ref, gf_ref, h1_ref, hn2_ref, mixed_ref):
        mixed = jnp.concatenate([_rms(ys_ref[...], gs_ref[...]), _rms(ya_ref[...], ga_ref[...])], axis=1).astype(BF16)
        mixed_ref[...] = mixed
        h1 = x_ref[...] + _dot(mixed, w_ref[...])
        h1_ref[...] = h1
        hn2_ref[...] = _rms(h1, gf_ref[...]).astype(BF16)

    row = lambda w: pl.BlockSpec((tm, w), lambda i: (i, 0))
    full = lambda a: pl.BlockSpec(a.shape, lambda i: (0,) * a.ndim)
    return pl.pallas_call(
        body, name="mix_fwd", grid=(n // tm,),
        in_specs=[row(D_MODEL), row(512), row(512), full(gs), full(ga), full(wout), full(gf)],
        out_specs=(row(D_MODEL), row(D_MODEL), row(D_MODEL)),
        out_shape=(jax.ShapeDtypeStruct((n, D_MODEL), F32), jax.ShapeDtypeStruct((n, D_MODEL), BF16),
                   jax.ShapeDtypeStruct((n, D_MODEL), BF16)),
        compiler_params=_cparams("parallel"),
    )(x, ys, ya, gs, ga, wout, gf)


def _mix_bwd(dy, dhn2_parts, h1, ys, ya, gs, ga, wout, gf, tm=512):
    n = dy.shape[0]
    n_parts = dhn2_parts.shape[0]

    def body(dy_ref, dp_ref, h1_ref, ys_ref, ya_ref, gs_ref, ga_ref, w_ref, gf_ref,
             dh1_ref, dys_ref, dya_ref, dgs_ref, dga_ref, dgf_ref):
        @pl.when(pl.program_id(0) == 0)
        def _():
            dgs_ref[...] = jnp.zeros_like(dgs_ref)
            dga_ref[...] = jnp.zeros_like(dga_ref)
            dgf_ref[...] = jnp.zeros_like(dgf_ref)
        dhn2 = dp_ref[0]
        for p in range(1, n_parts):
            dhn2 = dhn2 + dp_ref[p]
        _, vjp_f = jax.vjp(_rms, h1_ref[...], gf_ref[...])
        dh1n, dgf = vjp_f(dhn2)
        dh1 = dy_ref[...] + dh1n
        dh1_ref[...] = dh1
        dmixed = _dot_nt(dh1.astype(BF16), w_ref[...])
        _, vjp_s = jax.vjp(_rms, ys_ref[...], gs_ref[...])
        dys, dgs = vjp_s(dmixed[:, 0:512])
        _, vjp_a = jax.vjp(_rms, ya_ref[...], ga_ref[...])
        dya, dga = vjp_a(dmixed[:, 512:1024])
        dys_ref[...] = dys
        dya_ref[...] = dya
        dgs_ref[...] += dgs
        dga_ref[...] += dga
        dgf_ref[...] += dgf

    row = lambda w: pl.BlockSpec((tm, w), lambda i: (i, 0))
    full = lambda a: pl.BlockSpec(a.shape, lambda i: (0,) * a.ndim)
    vec = lambda w: pl.BlockSpec((1, w), lambda i: (0, 0))
    return pl.pallas_call(
        body, name="mix_bwd", grid=(n // tm,),
        in_specs=[row(D_MODEL), pl.BlockSpec((n_parts, tm, D_MODEL), lambda i: (0, i, 0)), row(D_MODEL), row(512),
                  row(512), full(gs), full(ga), full(wout), full(gf)],
        out_specs=(row(D_MODEL), row(512), row(512), vec(512), vec(512), vec(D_MODEL)),
        out_shape=(jax.ShapeDtypeStruct((n, D_MODEL), F32), jax.ShapeDtypeStruct((n, 512), F32),
                   jax.ShapeDtypeStruct((n, 512), F32), jax.ShapeDtypeStruct((1, 512), F32),
                   jax.ShapeDtypeStruct((1, 512), F32), jax.ShapeDtypeStruct((1, D_MODEL), F32)),
        compiler_params=_cparams("arbitrary"),
    )(dy, dhn2_parts, h1, ys, ya, gs, ga, wout, gf)


HALO = 16
FFN_GROUPS = 4
FFN_GROUP = D_FF // FFN_GROUPS


def _conv3(ue, cw):
    return cw[2:3] * ue + cw[1:2] * pltpu.roll(ue, 1, 0) + cw[0:1] * pltpu.roll(ue, 2, 0) + cw[3:4]


def _ffn_weight_specs():
    gate = lambda i, j: (j, 0, 0)
    val = lambda i, j: (j + FFN_GROUPS, 0, 0)
    w_blk, c_blk = (1, FFN_GROUP, D_MODEL), (1, SUBLANES, FFN_GROUP)
    return [pl.BlockSpec(w_blk, gate), pl.BlockSpec(w_blk, val), pl.BlockSpec(c_blk, gate), pl.BlockSpec(c_blk, val),
            pl.BlockSpec((1, FFN_GROUP, D_MODEL), gate)]


def _ffn_fwd(hn2, h1, target, w_up, conv, w_down, seq_len, tm=512):
    n = hn2.shape[0]
    nj = FFN_GROUPS
    hb = tm // HALO

    def body(hn_ref, halo_ref, h1_ref, tgt_ref, wg_ref, wv_ref, cg_ref, cv_ref, wd_ref,
             ug_ref, uv_ref, pg_ref, pv_ref, dy_ref, loss_ref, acc):
        i, j = pl.program_id(0), pl.program_id(1)
        seq_start = (i * tm) % seq_len == 0
        halo = halo_ref[...]
        halo = jnp.where(seq_start, jnp.zeros_like(halo), halo)
        he = jnp.concatenate([halo, hn_ref[...]], axis=0)
        ueg = _dot_nt(he, wg_ref[0])
        uev = _dot_nt(he, wv_ref[0])
        ug_ref[0] = ueg[HALO:].astype(BF16)
        uv_ref[0] = uev[HALO:].astype(BF16)
        cg = _conv3(ueg, cg_ref[0])[HALO:]
        cv = _conv3(uev, cv_ref[0])[HALO:]
        pg_ref[0] = cg.astype(BF16)
        pv_ref[0] = cv.astype(BF16)
        act = (jax.nn.silu(cg) * cv).astype(BF16)
        part = _dot(act, wd_ref[0])

        @pl.when(j == 0)
        def _():
            acc[...] = part

        @pl.when(j > 0)
        def _():
            acc[...] += part

        @pl.when(j == nj - 1)
        def _():
            err = h1_ref[...] + acc[...] - tgt_ref[...]
            dy_ref[...] = err * (1.0 / D_MODEL)
            loss_ref[0] = jnp.sum(err * err, axis=0, keepdims=True)

    row = pl.BlockSpec((tm, D_MODEL), lambda i, j: (i, 0))
    u_main = pl.BlockSpec((1, tm, FFN_GROUP), lambda i, j: (j, i, 0))
    u_shape = jax.ShapeDtypeStruct((FFN_GROUPS, n, FFN_GROUP), BF16)
    return pl.pallas_call(
        body, name="ffn_fwd", grid=(n // tm, nj),
        in_specs=[row, pl.BlockSpec((HALO, D_MODEL), lambda i, j: (jnp.maximum(i * hb - 1, 0), 0)), row, row,
                  *_ffn_weight_specs()],
        out_specs=(u_main, u_main, u_main, u_main, row, pl.BlockSpec((1, 1, D_MODEL), lambda i, j: (i, 0, 0))),
        out_shape=(u_shape, u_shape, u_shape, u_shape, jax.ShapeDtypeStruct((n, D_MODEL), F32),
                   jax.ShapeDtypeStruct((n // tm, 1, D_MODEL), F32)),
        scratch_shapes=[pltpu.VMEM((tm, D_MODEL), F32)],
        compiler_params=_cparams("parallel", "arbitrary"),
    )(hn2, hn2, h1, target, w_up, w_up, conv, conv, w_down)


def _ffn_bwd(dy, ug, uv, pg, pv, w_up, conv, w_down, seq_len, tm=512):
    n = dy.shape[0]
    nj = FFN_GROUPS
    fb = FFN_GROUP
    hb = tm // HALO
    last_hb = n // HALO - 1
    rows = tm + HALO

    def body(dy_ref, dyn_ref, ug_ref, uv_ref, pgm_ref, pgn_ref, pvm_ref, pvn_ref, wg_ref, wv_ref, cg_ref, cv_ref,
             wd_ref, dug_ref, duv_ref, act_ref, dhn_ref, dcg_ref, dcv_ref, acc):
        i, j = pl.program_id(0), pl.program_id(1)
        seq_end = ((i + 1) * tm) % seq_len == 0
        dyn = dyn_ref[...]
        dyn = jnp.where(seq_end, jnp.zeros_like(dyn), dyn)
        d_out = jnp.concatenate([dy_ref[...], dyn], axis=0).astype(BF16)
        d_act = _dot_nt(d_out, wd_ref[0])
        cge = jnp.concatenate([pgm_ref[0], pgn_ref[0]], axis=0).astype(F32)
        cve = jnp.concatenate([pvm_ref[0], pvn_ref[0]], axis=0).astype(F32)
        act, vjp_act = jax.vjp(lambda g, v: jax.nn.silu(g) * v, cge, cve)
        dcge, dcve = vjp_act(d_act)
        act_ref[0] = act[:tm].astype(BF16)

        def conv_t(dc, u_ref, cw):
            ahead1 = pltpu.roll(dc, rows - 1, 0)[:tm]
            ahead2 = pltpu.roll(dc, rows - 2, 0)[:tm]
            here = dc[:tm]
            du = cw[2:3] * here + cw[1:2] * ahead1 + cw[0:1] * ahead2
            u = u_ref[0].astype(F32)
            col = lambda x: jnp.sum(x, axis=0, keepdims=True)
            grad = jnp.concatenate([col(ahead2 * u), col(ahead1 * u), col(here * u), col(here),
                                    jnp.zeros((4, fb), F32)], axis=0)
            return du.astype(BF16), grad

        cwg, cwv = cg_ref[0], cv_ref[0]
        dug, grad_g = conv_t(dcge, ug_ref, cwg)
        duv, grad_v = conv_t(dcve, uv_ref, cwv)
        dug_ref[0] = dug
        duv_ref[0] = duv
        part = _dot(dug, wg_ref[0]) + _dot(duv, wv_ref[0])

        @pl.when(j == 0)
        def _():
            acc[...] = part

        @pl.when(j > 0)
        def _():
            acc[...] += part

        @pl.when(j == nj - 1)
        def _():
            dhn_ref[...] = acc[...]

        @pl.when(i == 0)
        def _():
            dcg_ref[j] = jnp.zeros((8, fb), F32)
            dcv_ref[j] = jnp.zeros((8, fb), F32)

        dcg_ref[j] += grad_g
        dcv_ref[j] += grad_v

    row = pl.BlockSpec((tm, D_MODEL), lambda i, j: (i, 0))
    u_main = pl.BlockSpec((1, tm, fb), lambda i, j: (j, i, 0))
    u_next = pl.BlockSpec((1, HALO, fb), lambda i, j: (j, jnp.minimum((i + 1) * hb, last_hb), 0))
    dc_spec = pl.BlockSpec((nj, 8, fb), lambda i, j: (0, 0, 0))
    u_shape = jax.ShapeDtypeStruct((FFN_GROUPS, n, fb), BF16)
    return pl.pallas_call(
        body, name="ffn_bwd", grid=(n // tm, nj),
        in_specs=[row, pl.BlockSpec((HALO, D_MODEL), lambda i, j: (jnp.minimum((i + 1) * hb, last_hb), 0)),
                  u_main, u_main, u_main, u_next, u_main, u_next, *_ffn_weight_specs()],
        out_specs=(u_main, u_main, u_main, row, dc_spec, dc_spec),
        out_shape=(u_shape, u_shape, u_shape, jax.ShapeDtypeStruct((n, D_MODEL), F32),
                   jax.ShapeDtypeStruct((nj, 8, fb), F32), jax.ShapeDtypeStruct((nj, 8, fb), F32)),
        scratch_shapes=[pltpu.VMEM((tm, D_MODEL), F32)],
        compiler_params=_cparams("arbitrary", "arbitrary"),
    )(dy, dy, ug, uv, pg, pg, pv, pv, w_up, w_up, conv, conv, w_down)


def _tn_grouped(a, b, name, shared_a, out_dtype=F32, tn=1024):
    groups = b.shape[0] if shared_a else a.shape[0]
    n_tok = a.shape[0] if shared_a else b.shape[0]
    k_dim, m_dim = a.shape[-1], b.shape[-1]

    def body(a_ref, b_ref, o_ref, acc):
        k = pl.program_id(1)
        a_t = a_ref[...] if shared_a else a_ref[0]
        b_t = b_ref[0] if shared_a else b_ref[...]
        part = _dot_tn(a_t.astype(BF16), b_t.astype(BF16))

        @pl.when(k == 0)
        def _():
            acc[...] = part

        @pl.when(k > 0)
        def _():
            acc[...] += part

        @pl.when(k == n_tok // tn - 1)
        def _():
            o_ref[0] = acc[...].astype(out_dtype)

    plain = lambda w: pl.BlockSpec((tn, w), lambda g, k: (k, 0))
    grouped = lambda w: pl.BlockSpec((1, tn, w), lambda g, k: (g, k, 0))
    return pl.pallas_call(
        body, name=name, grid=(groups, n_tok // tn),
        in_specs=[plain(k_dim), grouped(m_dim)] if shared_a else [grouped(k_dim), plain(m_dim)],
        out_specs=pl.BlockSpec((1, k_dim, m_dim), lambda g, k: (g, 0, 0)),
        out_shape=jax.ShapeDtypeStruct((groups, k_dim, m_dim), out_dtype),
        scratch_shapes=[pltpu.VMEM((k_dim, m_dim), F32)],
        compiler_params=_cparams("parallel", "arbitrary"),
    )(a, b)


def _s5_param_fn(lr, li, ldt, br, bi):
    dt = jnp.exp(ldt)
    mag = jnp.exp(lr * dt)
    ab_re = mag * jnp.cos(li * dt)
    ab_im = mag * jnp.sin(li * dt)
    nr = ab_re - 1.0
    ni = ab_im
    den = lr * lr + li * li
    q_re = (nr * lr + ni * li) / den
    q_im = (ni * lr - nr * li) / den
    bb_re = q_re * br - q_im * bi
    bb_im = q_re * bi + q_im * br
    return ab_re, ab_im, bb_re, bb_im


def _s5_param_fwd(lr, li, ldt, br, bi):
    def body(lr_ref, li_ref, ldt_ref, br_ref, bi_ref, ar_ref, ai_ref, bbr_ref, bbi_ref):
        ar, ai, bbr, bbi = _s5_param_fn(lr_ref[...], li_ref[...], ldt_ref[...], br_ref[...], bi_ref[...])
        ar_ref[...] = ar
        ai_ref[...] = ai
        bbr_ref[...] = bbr
        bbi_ref[...] = bbi

    return pl.pallas_call(
        body, name="s5_param_fwd",
        out_shape=(jax.ShapeDtypeStruct(lr.shape, F32), jax.ShapeDtypeStruct(lr.shape, F32),
                   jax.ShapeDtypeStruct(br.shape, F32), jax.ShapeDtypeStruct(br.shape, F32)),
    )(lr, li, ldt, br, bi)


def _s5_param_bwd(lr, li, ldt, br, bi, dar, dai, dbbr, dbbi):
    def body(lr_ref, li_ref, ldt_ref, br_ref, bi_ref, dar_ref, dai_ref, dbbr_ref, dbbi_ref,
             dlr_ref, dli_ref, dldt_ref, dbr_ref, dbi_ref):
        _, vjp = jax.vjp(_s5_param_fn, lr_ref[...], li_ref[...], ldt_ref[...], br_ref[...], bi_ref[...])
        dlr, dli, dldt, dbr, dbi = vjp((dar_ref[...], dai_ref[...], dbbr_ref[...], dbbi_ref[...]))
        dlr_ref[...] = dlr
        dli_ref[...] = dli
        dldt_ref[...] = dldt
        dbr_ref[...] = dbr
        dbi_ref[...] = dbi

    return pl.pallas_call(
        body, name="s5_param_bwd",
        out_shape=(jax.ShapeDtypeStruct(lr.shape, F32), jax.ShapeDtypeStruct(lr.shape, F32),
                   jax.ShapeDtypeStruct(ldt.shape, F32), jax.ShapeDtypeStruct(br.shape, F32),
                   jax.ShapeDtypeStruct(br.shape, F32)),
    )(lr, li, ldt, br, bi, dar, dai, dbbr, dbbi)


S5_CHUNK = 256
S5_STATES = 512
S5_BLOCKS = 4


def _cpow_rows(ar, ai, count):
    rs, im = [ar], [ai]
    for _ in range(count - 1):
        pr, pi = rs[-1], im[-1]
        rs.append(pr * ar - pi * ai)
        im.append(pr * ai + pi * ar)
    return rs, im


def _scan_in_groups(vr, vi, pr, pi, rm, reverse):
    n, width = vr.shape
    vr = vr.reshape(n // SUBLANES, SUBLANES, width)
    vi = vi.reshape(n // SUBLANES, SUBLANES, width)
    row = rm[0:SUBLANES]
    for k in (1, 2, 4):
        shift = SUBLANES - k if reverse else k
        keep = row < SUBLANES - k if reverse else row >= k
        kr = jnp.where(keep, pr[k - 1], 0.0)
        ki = jnp.where(keep, pi[k - 1], 0.0)
        sr, si = pltpu.roll(vr, shift, 1), pltpu.roll(vi, shift, 1)
        vr, vi = vr + kr * sr - ki * si, vi + kr * si + ki * sr
    return vr.reshape(n, width), vi.reshape(n, width)


def _carry_over_groups(xr_s, xi_s, wr, wi, c0r, c0i, reverse):
    groups = xr_s.shape[0] // SUBLANES
    pick = 0 if reverse else SUBLANES - 1

    def step(q, carry):
        cr, ci = carry
        r = groups - 1 - q if reverse else q
        o = pl.multiple_of(r * SUBLANES, SUBLANES)
        vr = xr_s[pl.ds(o, SUBLANES), :]
        vi = xi_s[pl.ds(o, SUBLANES), :]
        nr = vr + wr * cr - wi * ci
        ni = vi + wr * ci + wi * cr
        xr_s[pl.ds(o, SUBLANES), :] = nr
        xi_s[pl.ds(o, SUBLANES), :] = ni
        return (jnp.broadcast_to(nr[pick:pick + 1], nr.shape), jnp.broadcast_to(ni[pick:pick + 1], ni.shape))

    return lax.fori_loop(0, groups, step, (c0r, c0i), unroll=4)


def _s5_state_scan(u_b, bbr, bbi, pr, pi, rm, xr_s, xi_s, c0r, c0i):
    bur = _dot(u_b, bbr)
    bui = _dot(u_b, bbi)
    bur, bui = _scan_in_groups(bur, bui, pr, pi, rm, False)
    xr_s[...] = bur
    xi_s[...] = bui
    w8r = jnp.concatenate(pr, axis=0)
    w8i = jnp.concatenate(pi, axis=0)
    return _carry_over_groups(xr_s, xi_s, w8r, w8i, c0r, c0i, False)


def _s5_fwd(u, a_re, a_im, bbr, bbi, cr, ci, d_skip, n_seq):
    n = u.shape[0]
    seq_len = n // n_seq
    nt = seq_len // S5_CHUNK
    tc = S5_CHUNK

    def body(u_ref, ar_ref, ai_ref, bbr_ref, bbi_ref, cr_ref, ci_ref, d_ref, y_ref, str_ref, sti_ref,
             xr_s, xi_s, car_r, car_i):
        t = pl.program_id(2)

        @pl.when(t == 0)
        def _():
            car_r[...] = jnp.zeros_like(car_r)
            car_i[...] = jnp.zeros_like(car_i)
        pr, pi = _cpow_rows(ar_ref[0], ai_ref[0], SUBLANES)
        rm = lax.broadcasted_iota(jnp.int32, (tc, S5_STATES), 0) & (SUBLANES - 1)
        str_ref[0, 0] = car_r[...]
        sti_ref[0, 0] = car_i[...]
        u_t = u_ref[...]
        cfr, cfi = _s5_state_scan(u_t.astype(BF16), bbr_ref[0], bbi_ref[0], pr, pi, rm, xr_s, xi_s,
                                  car_r[...], car_i[...])
        car_r[...] = cfr
        car_i[...] = cfi
        y = _dot(xr_s[...].astype(BF16), cr_ref[0]) - _dot(xi_s[...].astype(BF16), ci_ref[0])
        y_ref[...] = y + d_ref[...] * u_t

    u_spec = pl.BlockSpec((tc, LANES), lambda cb, b, t: (b * nt + t, cb))
    a_spec = pl.BlockSpec((1, 1, S5_STATES), lambda cb, b, t: (cb, 0, 0))
    bb_spec = pl.BlockSpec((1, LANES, S5_STATES), lambda cb, b, t: (cb, 0, 0))
    c_spec = pl.BlockSpec((1, S5_STATES, LANES), lambda cb, b, t: (cb, 0, 0))
    st_spec = pl.BlockSpec((1, 1, SUBLANES, S5_STATES), lambda cb, b, t: (cb, b * nt + t, 0, 0))
    st_shape = jax.ShapeDtypeStruct((S5_BLOCKS, n_seq * nt, SUBLANES, S5_STATES), F32)
    return pl.pallas_call(
        body, name="s5_fwd", grid=(S5_BLOCKS, n_seq, nt),
        in_specs=[u_spec, a_spec, a_spec, bb_spec, bb_spec, c_spec, c_spec,
                  pl.BlockSpec((1, LANES), lambda cb, b, t: (0, cb))],
        out_specs=(u_spec, st_spec, st_spec),
        out_shape=(jax.ShapeDtypeStruct((n, D_SSM), F32), st_shape, st_shape),
        scratch_shapes=[pltpu.VMEM((tc, S5_STATES), F32), pltpu.VMEM((tc, S5_STATES), F32),
                        pltpu.VMEM((SUBLANES, S5_STATES), F32), pltpu.VMEM((SUBLANES, S5_STATES), F32)],
        compiler_params=_cparams("parallel", "arbitrary", "arbitrary"),
    )(u, a_re, a_im, bbr, bbi, cr, ci, d_skip)


def _s5_bwd(u, dy, st_r, st_i, a_re, a_im, bbr, bbi, cr, ci, d_skip, n_seq):
    n = u.shape[0]
    seq_len = n // n_seq
    nt = seq_len // S5_CHUNK
    tc = S5_CHUNK

    def body(u_ref, dy_ref, str_ref, sti_ref, ar_ref, ai_ref, bbr_ref, bbi_ref, cr_ref, ci_ref, d_ref,
             du_ref, dbbr_ref, dbbi_ref, dcr_ref, dci_ref, dar_ref, dai_ref, dd_ref,
             xr_s, xi_s, gr_s, gi_s, car_r, car_i):
        b, t = pl.program_id(1), pl.program_id(2)

        @pl.when((b == 0) & (t == 0))
        def _():
            for ref in (dbbr_ref, dbbi_ref, dcr_ref, dci_ref, dar_ref, dai_ref, dd_ref):
                ref[...] = jnp.zeros_like(ref)

        @pl.when(t == 0)
        def _():
            car_r[...] = jnp.zeros_like(car_r)
            car_i[...] = jnp.zeros_like(car_i)
        ar, ai = ar_ref[0], ai_ref[0]
        pr, pi = _cpow_rows(ar, ai, SUBLANES)
        row = lax.broadcasted_iota(jnp.int32, (tc, S5_STATES), 0)
        rm = row & (SUBLANES - 1)
        u_t = u_ref[...]
        u_b = u_t.astype(BF16)
        dy_t = dy_ref[...]
        dy_b = dy_t.astype(BF16)
        s0r, s0i = str_ref[0, 0], sti_ref[0, 0]
        _s5_state_scan(u_b, bbr_ref[0], bbi_ref[0], pr, pi, rm, xr_s, xi_s, s0r, s0i)
        xr, xi = xr_s[...], xi_s[...]
        gr = _dot_nt(dy_b, cr_ref[0])
        gi = -_dot_nt(dy_b, ci_ref[0])
        npi = [-v for v in pi]
        gr, gi = _scan_in_groups(gr, gi, pr, npi, rm, True)
        gr_s[...] = gr
        gi_s[...] = gi
        w8r = jnp.concatenate(pr[::-1], axis=0)
        w8i = jnp.concatenate(npi[::-1], axis=0)
        cfr, cfi = _carry_over_groups(gr_s, gi_s, w8r, w8i, car_r[...], car_i[...], True)
        car_r[...] = cfr
        car_i[...] = cfi
        gr, gi = gr_s[...], gi_s[...]
        gr_b, gi_b = gr.astype(BF16), gi.astype(BF16)
        du_ref[...] = _dot_nt(gr_b, bbr_ref[0]) + _dot_nt(gi_b, bbi_ref[0]) + d_ref[...] * dy_t
        dbbr_ref[0] += _dot_tn(u_b, gr_b)
        dbbi_ref[0] += _dot_tn(u_b, gi_b)
        dcr_ref[0] += _dot_tn(xr.astype(BF16), dy_b)
        dci_ref[0] -= _dot_tn(xi.astype(BF16), dy_b)
        dd_ref[0] += jnp.sum((dy_t * u_t).reshape(tc // SUBLANES, SUBLANES, LANES), axis=0)
        first = row == 0
        xpr = jnp.where(first, jnp.broadcast_to(s0r[0:1], xr.shape), pltpu.roll(xr, 1, 0))
        xpi = jnp.where(first, jnp.broadcast_to(s0i[0:1], xi.shape), pltpu.roll(xi, 1, 0))
        shp = (tc // SUBLANES, SUBLANES, S5_STATES)
        dar_ref[0] += jnp.sum((gr * xpr + gi * xpi).reshape(shp), axis=0)
        dai_ref[0] += jnp.sum((gi * xpr - gr * xpi).reshape(shp), axis=0)

    u_spec = pl.BlockSpec((tc, LANES), lambda cb, b, t: (b * nt + nt - 1 - t, cb))
    a_spec = pl.BlockSpec((1, 1, S5_STATES), lambda cb, b, t: (cb, 0, 0))
    bb_spec = pl.BlockSpec((1, LANES, S5_STATES), lambda cb, b, t: (cb, 0, 0))
    c_spec = pl.BlockSpec((1, S5_STATES, LANES), lambda cb, b, t: (cb, 0, 0))
    st_spec = pl.BlockSpec((1, 1, SUBLANES, S5_STATES), lambda cb, b, t: (cb, b * nt + nt - 1 - t, 0, 0))
    da_spec = pl.BlockSpec((1, SUBLANES, S5_STATES), lambda cb, b, t: (cb, 0, 0))
    dd_spec = pl.BlockSpec((1, SUBLANES, LANES), lambda cb, b, t: (cb, 0, 0))
    big = pltpu.VMEM((tc, S5_STATES), F32)
    small = pltpu.VMEM((SUBLANES, S5_STATES), F32)
    return pl.pallas_call(
        body, name="s5_bwd", grid=(S5_BLOCKS, n_seq, nt),
        in_specs=[u_spec, u_spec, st_spec, st_spec, a_spec, a_spec, bb_spec, bb_spec, c_spec, c_spec,
                  pl.BlockSpec((1, LANES), lambda cb, b, t: (0, cb))],
        out_specs=(u_spec, bb_spec, bb_spec, c_spec, c_spec, da_spec, da_spec, dd_spec),
        out_shape=(jax.ShapeDtypeStruct((n, D_SSM), F32),
                   jax.ShapeDtypeStruct((S5_BLOCKS, LANES, S5_STATES), F32),
                   jax.ShapeDtypeStruct((S5_BLOCKS, LANES, S5_STATES), F32),
                   jax.ShapeDtypeStruct((S5_BLOCKS, S5_STATES, LANES), F32),
                   jax.ShapeDtypeStruct((S5_BLOCKS, S5_STATES, LANES), F32),
                   jax.ShapeDtypeStruct((S5_BLOCKS, SUBLANES, S5_STATES), F32),
                   jax.ShapeDtypeStruct((S5_BLOCKS, SUBLANES, S5_STATES), F32),
                   jax.ShapeDtypeStruct((S5_BLOCKS, SUBLANES, LANES), F32)),
        scratch_shapes=[big, big, big, big, small, small],
        compiler_params=_cparams("parallel", "arbitrary", "arbitrary"),
    )(u, dy, st_r, st_i, a_re, a_im, bbr, bbi, cr, ci, d_skip)


CUM_BLOCK = 128


def _tri(lower):
    r = lax.broadcasted_iota(jnp.int32, (CUM_BLOCK, CUM_BLOCK), 0)
    c = lax.broadcasted_iota(jnp.int32, (CUM_BLOCK, CUM_BLOCK), 1)
    return jnp.where(r >= c if lower else r <= c, 1.0, 0.0).astype(F32)


def _fprep_fwd(fl, bf, n_seq):
    n = fl.shape[0]
    seq_len = n // n_seq
    nb = seq_len // CUM_BLOCK

    def body(fl_ref, bf_ref, cum_ref):
        tril = _tri(True)
        carry = jnp.zeros((1, LANES), F32)
        for blk in range(nb):
            rows = slice(blk * CUM_BLOCK, (blk + 1) * CUM_BLOCK)
            lf = jax.nn.log_sigmoid(fl_ref[rows, :] + bf_ref[...])
            cs = jnp.dot(tril, lf, preferred_element_type=F32, precision=HIGHEST) + carry
            cum_ref[rows, :] = cs
            carry = cs[CUM_BLOCK - 1:CUM_BLOCK, :]

    spec = pl.BlockSpec((seq_len, LANES), lambda b: (b, 0))
    return pl.pallas_call(
        body, name="fprep_fwd", grid=(n_seq,), in_specs=[spec, pl.BlockSpec((1, LANES), lambda b: (0, 0))],
        out_specs=spec, out_shape=jax.ShapeDtypeStruct((n, LANES), F32), compiler_params=_cparams("parallel"),
    )(fl, bf)


def _fprep_bwd(dcum, fl, bf, n_seq):
    n = fl.shape[0]
    seq_len = n // n_seq
    nb = seq_len // CUM_BLOCK

    def body(dcum_ref, fl_ref, bf_ref, dfl_ref, dbf_ref):
        triu = _tri(False)
        lane = lax.broadcasted_iota(jnp.int32, (CUM_BLOCK, LANES), 1)
        carry = jnp.zeros((1, LANES), F32)
        total = jnp.zeros((1, LANES), F32)
        for blk in reversed(range(nb)):
            rows = slice(blk * CUM_BLOCK, (blk + 1) * CUM_BLOCK)
            rs = jnp.dot(triu, dcum_ref[rows, :], preferred_element_type=F32, precision=HIGHEST) + carry
            carry = rs[0:1, :]
            _, vjp = jax.vjp(jax.nn.log_sigmoid, fl_ref[rows, :] + bf_ref[...])
            dz = jnp.where(lane < N_HEADS, vjp(rs)[0], 0.0)
            dfl_ref[rows, :] = dz
            total = total + jnp.sum(dz, axis=0, keepdims=True)
        dbf_ref[0] = total

    spec = pl.BlockSpec((seq_len, LANES), lambda b: (b, 0))
    return pl.pallas_call(
        body, name="fprep_bwd", grid=(n_seq,), in_specs=[spec, spec, pl.BlockSpec((1, LANES), lambda b: (0, 0))],
        out_specs=(spec, pl.BlockSpec((1, 1, LANES), lambda b: (b, 0, 0))),
        out_shape=(jax.ShapeDtypeStruct((n, LANES), F32), jax.ShapeDtypeStruct((n_seq, 1, LANES), F32)),
        compiler_params=_cparams("parallel"),
    )(dcum, fl, bf)


ATT_TQ = 256
ATT_KSTEP = 256
ATT_SCALE = HEAD_DIM ** -0.5
NEG_BIG = -1e30


assert ATT_KSTEP == ATT_TQ


def _scores(q_scaled, kb, row_bias, ck, kend):
    s = _dot_nt(q_scaled, kb) - ck
    if row_bias is not None:
        s = s + row_bias
    r = lax.broadcasted_iota(jnp.int32, (ATT_TQ, ATT_TQ), 0)
    c = lax.broadcasted_iota(jnp.int32, (ATT_TQ, ATT_TQ), 1)
    diag = jnp.where(r >= c, s[:, kend - ATT_TQ:], NEG_BIG)
    return diag if kend == ATT_TQ else jnp.concatenate([s[:, :kend - ATT_TQ], diag], axis=1)


def _attn_specs(n_seq, seq_len):
    nq = seq_len // ATT_TQ
    q_spec = pl.BlockSpec((ATT_TQ, LANES), lambda b, h, q: (b * nq + q, h))
    k_spec = pl.BlockSpec((seq_len, LANES), lambda b, h, q: (b, N_HEADS // 2 + h))
    v_spec = pl.BlockSpec((seq_len, LANES), lambda b, h, q: (b, N_HEADS + h))
    cq_spec = pl.BlockSpec((1, 2, ATT_TQ, 1), lambda b, h, q: (b, h, q, 0))
    ck_spec = pl.BlockSpec((1, 2, 1, seq_len), lambda b, h, q: (b, h, 0, 0))
    return nq, q_spec, k_spec, v_spec, cq_spec, ck_spec


def _head_selectors():
    head0 = lax.broadcasted_iota(jnp.int32, (1, LANES), 1) < HEAD_DIM
    return head0, (head0, jnp.logical_not(head0))


def _for_key_range(qi, seq_len, run):
    per = ATT_KSTEP // ATT_TQ
    for g in range(seq_len // ATT_KSTEP):
        pl.when(qi // per == g)(functools.partial(run, (g + 1) * ATT_KSTEP))


def _attn_fwd(qkv, cq, ck, n_seq):
    n = qkv.shape[0]
    seq_len = n // n_seq
    nq, q_spec, k_spec, v_spec, cq_spec, ck_spec = _attn_specs(n_seq, seq_len)

    def body(q_ref, k_ref, v_ref, cq_ref, ck_ref, o_ref, lse_ref):
        qi = pl.program_id(2)
        q2 = q_ref[...]
        head0, sels = _head_selectors()
        qe = [jnp.where(sel, q2 * ATT_SCALE, 0.0).astype(BF16) for sel in sels]

        def run(kend):
            kb = k_ref[0:kend, :].astype(BF16)
            vb = v_ref[0:kend, :].astype(BF16)
            outs = []
            for e in range(2):
                s = _scores(qe[e], kb, None, ck_ref[0, e, :, 0:kend], kend)
                mx = jnp.max(s, axis=1, keepdims=True)
                p = jnp.exp(s - mx)
                den = jnp.sum(p, axis=1, keepdims=True)
                outs.append(_dot(p.astype(BF16), vb) / den)
                lse_ref[0, e] = cq_ref[0, e] + mx + jnp.log(den)
            o_ref[...] = jnp.where(head0, outs[0], outs[1])

        _for_key_range(qi, seq_len, run)

    return pl.pallas_call(
        body, name="attn_fwd", grid=(n_seq, N_HEADS // 2, nq),
        in_specs=[q_spec, k_spec, v_spec, cq_spec, ck_spec],
        out_specs=(q_spec, cq_spec),
        out_shape=(jax.ShapeDtypeStruct((n, D_ATTN), F32), jax.ShapeDtypeStruct((n_seq, N_HEADS, seq_len, 1), F32)),
        compiler_params=_cparams("parallel", "parallel", "parallel"),
    )(qkv, qkv, qkv, cq, ck)


def _attn_bwd(qkv, cq, ck, o, do, lse, n_seq):
    n = qkv.shape[0]
    seq_len = n // n_seq
    nq, q_spec, k_spec, v_spec, cq_spec, ck_spec = _attn_specs(n_seq, seq_len)
    kv_out = pl.BlockSpec((seq_len, LANES), lambda b, h, q: (b, h))

    def body(q_ref, k_ref, v_ref, cq_ref, ck_ref, o_ref, do_ref, lse_ref, dq_ref, dk_ref, dv_ref, dcq_ref, dck_ref):
        qi = pl.program_id(2)

        @pl.when(qi == 0)
        def _():
            dk_ref[...] = jnp.zeros_like(dk_ref)
            dv_ref[...] = jnp.zeros_like(dv_ref)
            dck_ref[...] = jnp.zeros_like(dck_ref)
        q2 = q_ref[...]
        do2 = do_ref[...]
        o2 = o_ref[...]
        head0, sels = _head_selectors()
        qe = [jnp.where(sel, q2 * ATT_SCALE, 0.0).astype(BF16) for sel in sels]
        doe = [jnp.where(sel, do2, 0.0) for sel in sels]
        doe_b = [d.astype(BF16) for d in doe]
        delta = [jnp.sum(d * o2, axis=1, keepdims=True) for d in doe]

        def run(kend):
            kb = k_ref[0:kend, :].astype(BF16)
            vb = v_ref[0:kend, :].astype(BF16)
            dqs = []
            dk = jnp.zeros((kend, LANES), F32)
            dv = jnp.zeros((kend, LANES), F32)
            for e in range(2):
                p = jnp.exp(_scores(qe[e], kb, cq_ref[0, e] - lse_ref[0, e], ck_ref[0, e, :, 0:kend], kend))
                ds = p * (_dot_nt(doe_b[e], vb) - delta[e])
                ds_b = ds.astype(BF16)
                dqs.append(_dot(ds_b, kb))
                dk = dk + _dot_tn(ds_b, qe[e])
                dv = dv + _dot_tn(p.astype(BF16), doe_b[e])
                dcq_ref[0, e] = jnp.sum(ds, axis=1, keepdims=True)
                dck_ref[0, e, :, 0:kend] -= jnp.sum(ds, axis=0, keepdims=True)
            dk_ref[0:kend, :] += dk
            dv_ref[0:kend, :] += dv
            dq_ref[...] = jnp.where(head0, dqs[0], dqs[1]) * ATT_SCALE

        _for_key_range(qi, seq_len, run)

    return pl.pallas_call(
        body, name="attn_bwd", grid=(n_seq, N_HEADS // 2, nq),
        in_specs=[q_spec, k_spec, v_spec, cq_spec, ck_spec, q_spec, q_spec, cq_spec],
        out_specs=(q_spec, kv_out, kv_out, cq_spec, ck_spec),
        out_shape=(jax.ShapeDtypeStruct((n, D_ATTN), F32), jax.ShapeDtypeStruct((n, D_ATTN), F32),
                   jax.ShapeDtypeStruct((n, D_ATTN), F32),
                   jax.ShapeDtypeStruct((n_seq, N_HEADS, seq_len, 1), F32),
                   jax.ShapeDtypeStruct((n_seq, N_HEADS, 1, seq_len), F32)),
        compiler_params=_cparams("parallel", "parallel", "arbitrary"),
    )(qkv, qkv, qkv, cq, ck, o, do, lse)


WEIGHT_NAMES = ("norm_mix", "w_in", "b_forget", "lam_re", "lam_im", "b_re", "b_im", "c_re", "c_im", "d_skip", "log_dt",
                "w_glu", "b_glu", "q_norm", "k_norm", "norm_out_ssm", "norm_out_attn", "w_out", "norm_ffn", "w_up",
                "conv_w", "conv_b", "w_down")
SHARDED = ("w_in", "w_glu", "w_out", "w_up", "conv_w", "w_down")
ADAM_TILE = {"w_in": (257, 256), "w_glu": (64, 512), "w_out": (128, 1024), "w_up": (688, 256), "conv_w": (3, 688),
             "w_down": (344, 1024)}
PACK_ROWS = SUBLANES * LANES
ROWS_DOWN = D_FF // N_DEV
ROWS_OUT = D_MODEL // N_DEV
ROWS_GLU = D_SSM * D_SSM // N_DEV // D_MODEL


def _after_all(*arrays):
    return sum(a.reshape(-1)[0].astype(F32) for a in arrays).reshape(1, 1)


def _pad_to(a, axis, size):
    pad = [(0, 0)] * a.ndim
    pad[axis] = (0, size - a.shape[axis])
    return jnp.pad(a, pad)


def _block_diag(t, transpose):
    t4 = t.reshape(S5_BLOCKS, 8, SSM_GROUP, SSM_STATE)
    eye = jnp.eye(8, dtype=t.dtype)
    if transpose:
        e = jnp.swapaxes(t4, 2, 3)[:, :, :, None, :] * eye[None, :, None, :, None]
        return e.reshape(S5_BLOCKS, S5_STATES, LANES)
    e = t4[:, :, :, None, :] * eye[None, :, None, :, None]
    return e.reshape(S5_BLOCKS, LANES, S5_STATES)


def _block_diag_extract(m, transpose):
    if transpose:
        m5 = m.reshape(S5_BLOCKS, 8, SSM_STATE, 8, SSM_GROUP)
        d = jnp.stack([m5[:, i, :, i, :] for i in range(8)], axis=1)
        return jnp.swapaxes(d, 2, 3).reshape(N_GROUPS, SSM_GROUP, SSM_STATE)
    m5 = m.reshape(S5_BLOCKS, 8, SSM_GROUP, 8, SSM_STATE)
    d = jnp.stack([m5[:, i, :, i, :] for i in range(8)], axis=1)
    return d.reshape(N_GROUPS, SSM_GROUP, SSM_STATE)


def _pack(pieces):
    flat = jnp.concatenate([p.reshape(-1).astype(F32) for p in pieces])
    size = -(-flat.shape[0] // PACK_ROWS) * PACK_ROWS
    return _pad_to(flat, 0, size).reshape(-1, LANES)


def _unpack(packed, shapes):
    flat = packed.reshape(-1)
    out, off = [], 0
    for shp in shapes:
        size = math.prod(shp)
        out.append(flat[off:off + size].reshape(shp))
        off += size
    return out


def kernel(x, norm_mix, w_in, b_forget, lam_re, lam_im, b_re, b_im, c_re, c_im, d_skip, log_dt, w_glu, b_glu, q_norm, k_norm, norm_out_ssm, norm_out_attn, w_out, norm_ffn, w_up, conv_w, conv_b, w_down, loss_target, m_norm_mix, m_w_in, m_b_forget, m_lam_re, m_lam_im, m_b_re, m_b_im, m_c_re, m_c_im, m_d_skip, m_log_dt, m_w_glu, m_b_glu, m_q_norm, m_k_norm, m_norm_out_ssm, m_norm_out_attn, m_w_out, m_norm_ffn, m_w_up, m_conv_w, m_conv_b, m_w_down, v_norm_mix, v_w_in, v_b_forget, v_lam_re, v_lam_im, v_b_re, v_b_im, v_c_re, v_c_im, v_d_skip, v_log_dt, v_w_glu, v_b_glu, v_q_norm, v_k_norm, v_norm_out_ssm, v_norm_out_attn, v_w_out, v_norm_ffn, v_w_up, v_conv_w, v_conv_b, v_w_down):
    given = dict(locals())
    weights = {k: given[k] for k in WEIGHT_NAMES}
    mom1 = {k: given["m_" + k] for k in WEIGHT_NAMES}
    mom2 = {k: given["v_" + k] for k in WEIGHT_NAMES}
    n_seq, seq_len, _ = x.shape
    n = n_seq * seq_len
    xf = x.reshape(n, D_MODEL)
    target = loss_target.reshape(n, D_MODEL)
    me_idx = 4 * lax.axis_index("x") + 2 * lax.axis_index("y") + lax.axis_index("c")

    in_flags = [False] * 2
    in_sems = _exchange_start([jnp.swapaxes(w_in[0], 0, 1).astype(BF16), conv_w[0]], in_flags, norm_mix,
                              "gather_in_start", 3)
    my_slot = lax.broadcasted_iota(jnp.int32, (N_DEV, 1, 1), 0) == me_idx

    lr3 = lam_re[0].reshape(N_GROUPS, 1, SSM_STATE)
    li3 = lam_im[0].reshape(N_GROUPS, 1, SSM_STATE)
    ldt3 = log_dt[0].reshape(N_GROUPS, 1, 1)
    br_t = jnp.swapaxes(b_re[0], 1, 2)
    bi_t = jnp.swapaxes(b_im[0], 1, 2)
    ab_re, ab_im, bb_re, bb_im = _s5_param_fwd(lr3, li3, ldt3, br_t, bi_t)
    a_re = ab_re.reshape(S5_BLOCKS, 1, S5_STATES)
    a_im = ab_im.reshape(S5_BLOCKS, 1, S5_STATES)
    bbr = _block_diag(bb_re, False).astype(BF16)
    bbi = _block_diag(bb_im, False).astype(BF16)
    cr = _block_diag(c_re[0], True).astype(BF16)
    ci = _block_diag(c_im[0], True).astype(BF16)
    avg = jnp.kron(jnp.eye(N_HEADS, dtype=F32), jnp.full((HEAD_DIM, HEAD_DIM), 1.0 / HEAD_DIM, F32)).astype(BF16)
    qg = jnp.tile(q_norm, (1, N_HEADS))
    kg = jnp.tile(k_norm, (1, N_HEADS))
    rows_w = jnp.concatenate([w_down[0], w_out[0], w_glu[0].reshape(ROWS_GLU, D_MODEL)], axis=0).astype(BF16)

    (own_in, own_cw), (g_in, g_cw) = _exchange_wait(in_sems[0], in_sems[1], in_sems[2], in_sems[3], in_flags,
                                                    _after_all(a_re, a_im, bbr, bbi, cr, ci, avg, qg, kg, rows_w),
                                                    "gather_in_wait")
    g_in = jnp.where(my_slot, own_in[None], g_in)
    g_cw = jnp.where(my_slot, own_cw[None], g_cw)
    r_sems = _exchange_start([rows_w], [False], g_in, "gather_rows_start", 0)
    u_sems = _exchange_start([jnp.swapaxes(w_up[0], 0, 1).astype(BF16)], [False], r_sems[4], "gather_up_start", 5)
    norm_mix = norm_mix + u_sems[4][0, 0]
    w_in_p = _pad_to(g_in.reshape(D_IN, D_MODEL), 0, D_IN_PAD)

    hn, u, qkv, raw, fl = _inproj_fwd(xf, norm_mix, w_in_p, avg, qg, kg)
    yc, st_r, st_i = _s5_fwd(u, a_re, a_im, bbr, bbi, cr, ci, d_skip, n_seq)
    bf = _pad_to(b_forget, 1, LANES)
    cum = _fprep_fwd(fl, bf, n_seq)
    cum8 = jnp.swapaxes(cum[:, :N_HEADS].reshape(n_seq, seq_len, N_HEADS), 1, 2)
    cq = cum8[:, :, :, None]
    ck = cum8[:, :, None, :]
    ya, lse = _attn_fwd(qkv, cq, ck, n_seq)
    (own_rows,), (g_rows,) = _exchange_wait(r_sems[0], r_sems[1], r_sems[2], r_sems[3], [False], ya, "gather_rows_wait")
    g_rows = jnp.where(my_slot, own_rows[None], g_rows)
    g_down = g_rows[:, :ROWS_DOWN]
    g_out = g_rows[:, ROWS_DOWN:ROWS_DOWN + ROWS_OUT]
    g_glu = g_rows[:, ROWS_DOWN + ROWS_OUT:]
    w_glu_f = g_glu.reshape(D_SSM, D_SSM)
    w_out_f = g_out.reshape(D_MODEL, D_MODEL)
    conv_st = _pad_to(jnp.concatenate([g_cw, conv_b.reshape(N_DEV, 1, -1)], axis=1), 1, SUBLANES)
    w_down4 = g_down.reshape(FFN_GROUPS, FFN_GROUP, D_MODEL)
    ys = _glu_fwd(yc, w_glu_f, b_glu)
    h1, hn2, mixed = _mix_fwd(xf, ys, ya, norm_out_ssm, norm_out_attn, w_out_f, norm_ffn)
    (own_up,), (g_up,) = _exchange_wait(u_sems[0], u_sems[1], u_sems[2], u_sems[3], [False], _after_all(h1, hn2),
                                        "gather_up_wait")
    g_up = jnp.where(my_slot, own_up[None], g_up)
    ug, uv, pg, pv, dy, loss_part = _ffn_fwd(hn2, h1, target, g_up, conv_st, w_down4, seq_len)
    loss_local = 0.5 * jnp.sum(loss_part) / D_MODEL

    dug, duv, act, dhn2, dcg, dcv = _ffn_bwd(dy, ug, uv, pg, pv, g_up, conv_st, w_down4, seq_len)
    dh1, dys, dya, d_gs, d_ga, d_gf = _mix_bwd(dy, dhn2[None], h1, ys, ya, norm_out_ssm, norm_out_attn, w_out_f, norm_ffn)
    dyc, gl_b, dz_b, d_bglu = _glu_bwd(yc, dys, w_glu_f, b_glu)

    gw_glu = _tn_matmul(gl_b, dz_b, "dw_glu", D_SSM, D_SSM)
    gw_out = _tn_matmul(mixed, dh1, "dw_out", D_MODEL, D_MODEL)
    gw_up = jnp.concatenate([_tn_grouped(dug, hn2, "dw_up_gate", False, BF16),
                             _tn_grouped(duv, hn2, "dw_up_val", False, BF16)], axis=0)
    gw_down = _tn_grouped(act, dy, "dw_down", False)
    g_conv = jnp.concatenate([dcg, dcv], axis=0)
    by_cols = lambda g, c: jnp.swapaxes(g.reshape(g.shape[0], N_DEV, c), 0, 1)
    early_flags = [True] * 2
    rows_g = jnp.concatenate([gw_down.reshape(N_DEV, ROWS_DOWN, D_MODEL), gw_out.reshape(N_DEV, ROWS_OUT, D_MODEL),
                              gw_glu.reshape(N_DEV, ROWS_GLU, D_MODEL)], axis=1).astype(BF16)
    g_sems = _exchange_start([rows_g, gw_up], early_flags, dyc, "grad_early_start", 1)
    started = g_sems[4][0, 0]

    du, dbbr, dbbi, dcr, dci, dar, dai, ddk = _s5_bwd(u, dyc, st_r, st_i, a_re, a_im, bbr, bbi, cr, ci,
                                                      d_skip + started, n_seq)
    partial_early = {
        "ab_re": jnp.sum(dar, axis=1), "ab_im": jnp.sum(dai, axis=1),
        "bb_re": _block_diag_extract(dbbr, False), "bb_im": _block_diag_extract(dbbi, False),
        "c_re": _block_diag_extract(dcr, True), "c_im": _block_diag_extract(dci, True),
        "d_skip": jnp.sum(ddk, axis=1), "b_glu": d_bglu,
        "norm_out_ssm": d_gs, "norm_out_attn": d_ga, "norm_ffn": d_gf, "conv_b": g_conv[:, 3],
    }
    early_keys = tuple(partial_early)
    early_shapes = [partial_early[k].shape for k in early_keys]
    p_sems = _exchange_start([_pack([partial_early[k] for k in early_keys])], [False], du, "small_early_start", 2)
    started = started + p_sems[4][0, 0]

    dqn, dkn, dv, dcq, dck = _attn_bwd(qkv, cq, ck + started, ya, dya, lse, n_seq)
    dcum8 = dcq[:, :, :, 0] + dck.reshape(n_seq, N_HEADS, seq_len)
    dcum = _pad_to(jnp.swapaxes(dcum8, 1, 2).reshape(n, N_HEADS), 1, LANES)
    dfl, dbf = _fprep_bwd(dcum, fl, bf, n_seq)
    dx, dproj, d_gmix, d_qg, d_kg = _inproj_bwd(xf, norm_mix, w_in_p, avg, qg, kg, raw, du, dqn, dkn, dv, dfl, dh1)

    gw_in = _tn_matmul(dproj, hn, "dw_in", D_IN_PAD, D_MODEL, out_rows=D_IN)
    partial_late = {
        "norm_mix": d_gmix, "b_forget": jnp.sum(dbf, axis=(0, 1))[:N_HEADS],
        "q_norm": jnp.sum(d_qg.reshape(N_HEADS, HEAD_DIM), axis=0),
        "k_norm": jnp.sum(d_kg.reshape(N_HEADS, HEAD_DIM), axis=0), "loss": loss_local.reshape(1),
    }
    late_keys = tuple(partial_late)
    late_shapes = [partial_late[k].shape for k in late_keys]

    late_flags = [True, True, False]
    l_sems = _exchange_start(
        [gw_in.reshape(N_DEV, D_IN // N_DEV, D_MODEL).astype(BF16), g_conv[:, :3],
         _pack([partial_late[k] for k in late_keys])],
        late_flags, dx, "grad_late_start", 4)
    (src_rows, src_up), (land_rows, land_up) = _exchange_wait(g_sems[0], g_sems[1], g_sems[2], g_sems[3], early_flags,
                                                              l_sems[4], "grad_early_wait")
    own_rows = lax.dynamic_index_in_dim(src_rows, me_idx, 0, keepdims=False)
    land = {"w_up": land_up, "w_down": land_rows[:, :ROWS_DOWN], "w_out": land_rows[:, ROWS_DOWN:ROWS_DOWN + ROWS_OUT],
            "w_glu": land_rows[:, ROWS_DOWN + ROWS_OUT:].reshape(N_DEV, -1, D_SSM)}
    own = {"w_up": lax.dynamic_index_in_dim(src_up, me_idx, 0, keepdims=False), "w_down": own_rows[:ROWS_DOWN],
           "w_out": own_rows[ROWS_DOWN:ROWS_DOWN + ROWS_OUT], "w_glu": own_rows[ROWS_DOWN + ROWS_OUT:].reshape(-1, D_SSM)}
    grads, deltas, new_m, new_v = {}, {}, {}, {}

    def adam_shard(name):
        flip = (lambda a: jnp.swapaxes(a, 1, 2)) if name in ("w_in", "w_up") else (lambda a: a)
        outs = _adam_sharded(land[name], own[name], flip(weights[name]), flip(mom1[name]), flip(mom2[name]),
                             "adam_" + name, ADAM_TILE[name])
        grads[name], deltas[name], new_m[name], new_v[name] = [flip(o) for o in outs]

    for name in ("w_up", "w_down", "w_out", "w_glu"):
        adam_shard(name)
    (own_pack,), (early_parts,) = _exchange_wait(p_sems[0], p_sems[1], p_sems[2], p_sems[3], [False], land_up,
                                                 "small_early_wait")
    early_sum = _sum_partials(early_parts, own_pack, "sum_early_partials")
    (src_in, src_cw, own_late), (land["w_in"], land["conv_w"], late_parts) = _exchange_wait(
        l_sems[0], l_sems[1], l_sems[2], l_sems[3], late_flags,
        _after_all(early_sum, *[new_v[k] for k in ("w_up", "w_down", "w_out", "w_glu")]), "grad_late_wait")
    own["w_in"] = lax.dynamic_index_in_dim(src_in, me_idx, 0, keepdims=False)
    own["conv_w"] = lax.dynamic_index_in_dim(src_cw, me_idx, 0, keepdims=False)
    for name in ("w_in", "conv_w"):
        adam_shard(name)

    summed = dict(zip(late_keys, _unpack(_sum_partials(late_parts, own_late, "sum_late_partials"), late_shapes)))
    summed.update(zip(early_keys, _unpack(early_sum, early_shapes)))
    dlr, dli, dldt, dbr_t, dbi_t = _s5_param_bwd(
        lr3, li3, ldt3, br_t, bi_t, summed["ab_re"].reshape(lr3.shape), summed["ab_im"].reshape(lr3.shape),
        summed["bb_re"], summed["bb_im"])
    small_grads = {
        "norm_mix": summed["norm_mix"], "b_forget": summed["b_forget"], "lam_re": dlr, "lam_im": dli,
        "b_re": dbr_t, "b_im": dbi_t, "c_re": summed["c_re"], "c_im": summed["c_im"],
        "d_skip": summed["d_skip"], "log_dt": dldt, "b_glu": summed["b_glu"], "q_norm": summed["q_norm"],
        "k_norm": summed["k_norm"], "norm_out_ssm": summed["norm_out_ssm"], "norm_out_attn": summed["norm_out_attn"],
        "norm_ffn": summed["norm_ffn"], "conv_b": summed["conv_b"],
    }
    repl = tuple(k for k in WEIGHT_NAMES if k not in SHARDED)
    turn = lambda k, a: jnp.swapaxes(a, 2, 3) if k in ("b_re", "b_im") else a
    w_list = [turn(k, weights[k]) for k in repl]
    g_list = [small_grads[k].reshape(w.shape) for k, w in zip(repl, w_list)]
    d_list, m_list, v_list = _adam_replicated(g_list, w_list, [turn(k, mom1[k]) for k in repl],
                                              [turn(k, mom2[k]) for k in repl], "adam_replicated")
    for k, g, d, nm, nv in zip(repl, g_list, d_list, m_list, v_list):
        grads[k], deltas[k], new_m[k], new_v[k] = turn(k, g), turn(k, d), turn(k, nm), turn(k, nv)

    grad_x = dx.reshape(x.shape)
    loss = summed["loss"].reshape(())
    return (loss, grad_x, *[grads[k] for k in WEIGHT_NAMES], *[deltas[k] for k in WEIGHT_NAMES],
            *[new_m[k] for k in WEIGHT_NAMES], *[new_v[k] for k in WEIGHT_NAMES])
```

```python
import functools
import math

import jax
import jax.numpy as jnp
from jax import lax
from jax.experimental import pallas as pl
from jax.experimental.pallas import tpu as pltpu

F32 = jnp.float32
BF16 = jnp.bfloat16
HIGHEST = lax.Precision.HIGHEST

N_DEV = 8
D_MODEL = 1024
D_SSM = 512
D_ATTN = 512
N_HEADS = 8
HEAD_DIM = 64
N_GROUPS = 32
SSM_GROUP = 16
SSM_STATE = 64
D_FF = 2752
D_FF_PAD = 2816
D_IN = 2056
D_IN_PAD = 2176
EPS = 1e-6
LANES = 128
SUBLANES = 8
VMEM_LIMIT = 56 * 1024 * 1024

ADAM_LR = 0.001
ADAM_B1 = 0.9
ADAM_B2 = 0.999
ADAM_EPS = 1e-08
ADAM_WD = 0.01
ADAM_STEP = 10


def _cparams(*sem):
    return pltpu.CompilerParams(dimension_semantics=sem, vmem_limit_bytes=VMEM_LIMIT)


def _dot(a, b, **kw):
    return jnp.dot(a, b, preferred_element_type=F32, **kw)


def _dot_nt(a, b):
    return lax.dot_general(a, b, (((1,), (1,)), ((), ())), preferred_element_type=F32)


def _dot_tn(a, b):
    return lax.dot_general(a, b, (((0,), (0,)), ((), ())), preferred_element_type=F32)


def _rms(x, g):
    return x * lax.rsqrt(jnp.mean(x * x, axis=-1, keepdims=True) + EPS) * g


def _split_dot(x, avg):
    hi = x.astype(BF16)
    lo = (x - hi.astype(F32)).astype(BF16)
    return _dot(hi, avg) + _dot(lo, avg)


@jax.custom_vjp
def _group_mean(x, avg):
    return _split_dot(x, avg)


def _group_mean_fwd(x, avg):
    return _split_dot(x, avg), avg


def _group_mean_bwd(avg, ct):
    return _split_dot(ct, avg), jnp.zeros_like(avg)


_group_mean.defvjp(_group_mean_fwd, _group_mean_bwd)


def _headnorm(q, avg, g):
    return q * lax.rsqrt(_group_mean(q * q, avg) + EPS) * g


def _exchange(srcs, scatter_flags, name):
    n = len(srcs)
    out_shape = []
    for s, sc in zip(srcs, scatter_flags):
        shp = s.shape if sc else (N_DEV,) + s.shape
        out_shape.append(jax.ShapeDtypeStruct(shp, s.dtype))

    def body(*refs):
        src = refs[:n]
        dst = refs[n:2 * n]
        send_sems, recv_sems, loc_sems = refs[2 * n:]
        x, y, c = lax.axis_index("x"), lax.axis_index("y"), lax.axis_index("c")
        me = 4 * x + 2 * y + c
        peers = []
        for j in range(1, N_DEV):
            px = 1 - x if (j >> 2) & 1 else x
            py = 1 - y if (j >> 1) & 1 else y
            pc = 1 - c if j & 1 else c
            peers.append(((px, py, pc), 4 * px + 2 * py + pc))
        local, sends = [], []
        for k in range(n):
            own = src[k].at[me] if scatter_flags[k] else src[k]
            lc = pltpu.make_async_copy(own, dst[k].at[me], loc_sems.at[k])
            lc.start()
            local.append(lc)
            for j, (pid, pidx) in enumerate(peers):
                s = src[k].at[pidx] if scatter_flags[k] else src[k]
                cp = pltpu.make_async_remote_copy(
                    src_ref=s, dst_ref=dst[k].at[me], send_sem=send_sems.at[k, j], recv_sem=recv_sems.at[k, j],
                    device_id=pid, device_id_type=pl.DeviceIdType.MESH)
                cp.start()
                sends.append(cp)
        for k in range(n):
            for j, (pid, pidx) in enumerate(peers):
                s = src[k].at[pidx] if scatter_flags[k] else src[k]
                pltpu.make_async_remote_copy(
                    src_ref=s, dst_ref=dst[k].at[pidx], send_sem=send_sems.at[k, j], recv_sem=recv_sems.at[k, j],
                    device_id=pid, device_id_type=pl.DeviceIdType.MESH).wait_recv()
        for cp in sends:
            cp.wait_send()
        for lc in local:
            lc.wait()

    any_spec = pl.BlockSpec(memory_space=pl.ANY)
    return pl.pallas_call(
        body, name=name, out_shape=tuple(out_shape),
        in_specs=[any_spec] * n, out_specs=tuple([any_spec] * n),
        scratch_shapes=[pltpu.SemaphoreType.DMA((n, N_DEV - 1)), pltpu.SemaphoreType.DMA((n, N_DEV - 1)),
                        pltpu.SemaphoreType.DMA((n,))],
        compiler_params=pltpu.CompilerParams(has_side_effects=True),
    )(*srcs)


def _peer_list():
    x, y, c = lax.axis_index("x"), lax.axis_index("y"), lax.axis_index("c")
    peers = []
    for j in range(1, N_DEV):
        px = 1 - x if (j >> 2) & 1 else x
        py = 1 - y if (j >> 1) & 1 else y
        pc = 1 - c if j & 1 else c
        peers.append(((px, py, pc), 4 * px + 2 * py + pc))
    return 4 * x + 2 * y + c, peers


def _split_copies(src, land, send_sems, recv_sems, scatter_flags, me, peers, incoming):
    copies = []
    for k in range(len(src)):
        for j, (pid, pidx) in enumerate(peers):
            s = src[k].at[pidx] if scatter_flags[k] else src[k]
            i = k * (N_DEV - 1) + j
            copies.append(pltpu.make_async_remote_copy(
                src_ref=s, dst_ref=land[k].at[pidx if incoming else me], send_sem=send_sems[i],
                recv_sem=recv_sems[i], device_id=pid, device_id_type=pl.DeviceIdType.MESH))
    return copies


def _exchange_start(srcs, scatter_flags, after, name, collective_id):
    n = len(srcs)
    ns = n * (N_DEV - 1)
    hbm = pl.BlockSpec(memory_space=pltpu.HBM)
    sem = pl.BlockSpec(memory_space=pltpu.SEMAPHORE)
    land_shapes = [s.shape if sc else (N_DEV,) + s.shape for s, sc in zip(srcs, scatter_flags)]

    def body(*refs):
        src, land = refs[:n], refs[n:2 * n]
        send_sems = refs[2 * n + 1:2 * n + 1 + ns]
        recv_sems = refs[2 * n + 1 + ns:2 * n + 1 + 2 * ns]
        token = refs[4 * n + 1 + 2 * ns]
        me, peers = _peer_list()
        barrier = pltpu.get_barrier_semaphore()
        for pid, _ in peers:
            pl.semaphore_signal(barrier, inc=1, device_id=pid, device_id_type=pl.DeviceIdType.MESH)
        pl.semaphore_wait(barrier, N_DEV - 1)
        for cp in _split_copies(src, land, send_sems, recv_sems, scatter_flags, me, peers, False):
            cp.start()
        token[...] = jnp.zeros_like(token)

    outs = pl.pallas_call(
        body, name=name,
        out_shape=(*[pltpu.SemaphoreType.DMA(())] * (2 * ns), *[pltpu.HBM(s.shape, s.dtype) for s in srcs],
                   *[pltpu.HBM(shp, s.dtype) for shp, s in zip(land_shapes, srcs)],
                   jax.ShapeDtypeStruct((SUBLANES, LANES), F32)),
        in_specs=[hbm] * (2 * n) + [pl.BlockSpec(memory_space=pl.ANY)],
        out_specs=(*[sem] * (2 * ns), *[hbm] * (2 * n), pl.BlockSpec(memory_space=pltpu.VMEM)),
        input_output_aliases={i: 2 * ns + i for i in range(2 * n)},
        compiler_params=pltpu.CompilerParams(has_side_effects=pltpu.SideEffectType.DATAFLOW_SIDE_EFFECTING,
                                             collective_id=collective_id),
    )(*[pltpu.with_memory_space_constraint(s, pltpu.HBM) for s in srcs],
      *[pltpu.with_memory_space_constraint(lax.empty(shp, s.dtype), pltpu.HBM) for shp, s in zip(land_shapes, srcs)],
      after)
    return (outs[:ns], outs[ns:2 * ns], outs[2 * ns:2 * ns + n], outs[2 * ns + n:2 * ns + 2 * n], outs[2 * ns + 2 * n])


def _exchange_wait(send_sems, recv_sems, srcs, lands, scatter_flags, after, name):
    n = len(srcs)
    ns = n * (N_DEV - 1)
    hbm = pl.BlockSpec(memory_space=pltpu.HBM)
    sem = pl.BlockSpec(memory_space=pltpu.SEMAPHORE)

    def body(*refs):
        src, land = refs[:n], refs[n:2 * n]
        s_sems = refs[2 * n:2 * n + ns]
        r_sems = refs[2 * n + ns:2 * n + 2 * ns]
        me, peers = _peer_list()
        for cp in _split_copies(src, land, s_sems, r_sems, scatter_flags, me, peers, True):
            cp.wait_send()
            cp.wait_recv()

    outs = pl.pallas_call(
        body, name=name,
        out_shape=tuple(pltpu.HBM(a.shape, a.dtype) for a in (*srcs, *lands)),
        in_specs=[hbm] * (2 * n) + [sem] * (2 * ns) + [pl.BlockSpec(memory_space=pl.ANY)],
        out_specs=tuple([hbm] * (2 * n)),
        input_output_aliases={i: i for i in range(2 * n)},
        compiler_params=pltpu.CompilerParams(has_side_effects=pltpu.SideEffectType.DATAFLOW_SIDE_EFFECTING),
    )(*srcs, *lands, *send_sems, *recv_sems, after)
    return outs[:n], outs[n:]


def _tn_matmul(a, b, name, tk, tm, out_rows=None, out_cols=None, out_dtype=F32, tn=512):
    n_tok, k_dim = a.shape
    m_dim = b.shape[1]
    grid = (k_dim // tk, m_dim // tm, n_tok // tn)

    def body(a_ref, b_ref, o_ref, acc):
        k = pl.program_id(2)
        part = _dot_tn(a_ref[...].astype(BF16), b_ref[...].astype(BF16))

        @pl.when(k == 0)
        def _():
            acc[...] = part

        @pl.when(k > 0)
        def _():
            acc[...] += part

        @pl.when(k == grid[2] - 1)
        def _():
            o_ref[...] = acc[...].astype(out_dtype)

    return pl.pallas_call(
        body, name=name, grid=grid,
        in_specs=[pl.BlockSpec((tn, tk), lambda i, j, k: (k, i)), pl.BlockSpec((tn, tm), lambda i, j, k: (k, j))],
        out_specs=pl.BlockSpec((tk, tm), lambda i, j, k: (i, j)),
        out_shape=jax.ShapeDtypeStruct((out_rows or k_dim, out_cols or m_dim), out_dtype),
        scratch_shapes=[pltpu.VMEM((tk, tm), F32)],
        compiler_params=_cparams("parallel", "parallel", "arbitrary"),
    )(a, b)


def _adam_math(g, w, m, v):
    m = ADAM_B1 * m + (1.0 - ADAM_B1) * g
    v = ADAM_B2 * v + (1.0 - ADAM_B2) * (g * g)
    m_hat = m / (1.0 - ADAM_B1 ** ADAM_STEP)
    v_hat = v / (1.0 - ADAM_B2 ** ADAM_STEP)
    delta = -ADAM_LR * (m_hat / (jnp.sqrt(v_hat) + ADAM_EPS) + ADAM_WD * w)
    return delta, m, v


def _adam_sharded(land, own, w, m, v, name, tile):
    _, r, c = w.shape

    def body(*refs):
        l_ref = refs[0]
        own_ref = refs[1] if own is not None else None
        w_ref, m_ref, v_ref, g_ref, d_ref, nm_ref, nv_ref = [ref.at[0] for ref in refs[-7:]]
        if own_ref is not None:
            x, y, z = lax.axis_index("x"), lax.axis_index("y"), lax.axis_index("c")
            me = 4 * x + 2 * y + z
            mine = own_ref[...].astype(F32)
        g = None
        for s in range(N_DEV):
            part = l_ref[s].astype(F32)
            if own_ref is not None:
                part = jnp.where(me == s, mine, part)
            g = part if g is None else g + part
        d, nm, nv = _adam_math(g, w_ref[...], m_ref[...], v_ref[...])
        g_ref[...] = g
        d_ref[...] = d
        nm_ref[...] = nm
        nv_ref[...] = nv

    tr, tc = tile
    spec = pl.BlockSpec((1, tr, tc), lambda i, j: (0, i, j))
    own_specs, own_args = ([pl.BlockSpec((tr, tc), lambda i, j: (i, j))], [own]) if own is not None else ([], [])
    return pl.pallas_call(
        body, name=name, grid=(r // tr, c // tc),
        in_specs=[pl.BlockSpec((N_DEV, tr, tc), lambda i, j: (0, i, j)), *own_specs, spec, spec, spec],
        out_specs=(spec, spec, spec, spec),
        out_shape=tuple(jax.ShapeDtypeStruct((1, r, c), F32) for _ in range(4)),
        compiler_params=_cparams("parallel", "parallel"),
    )(land, *own_args, w, m, v)


def _sum_partials(parts, own, name):
    _, r, c = parts.shape

    def body(*refs):
        p_ref, o_ref = refs[0], refs[-1]
        if own is not None:
            x, y, z = lax.axis_index("x"), lax.axis_index("y"), lax.axis_index("c")
            me = 4 * x + 2 * y + z
            mine = refs[1][...]
        g = None
        for s in range(N_DEV):
            part = p_ref[s]
            if own is not None:
                part = jnp.where(me == s, mine, part)
            g = part if g is None else g + part
        o_ref[...] = g

    args = (parts,) if own is None else (parts, own)
    return pl.pallas_call(body, name=name, out_shape=jax.ShapeDtypeStruct((r, c), F32),
                          compiler_params=pltpu.CompilerParams(vmem_limit_bytes=VMEM_LIMIT))(*args)


def _adam_replicated(gs, ws, ms, vs, name):
    k = len(ws)

    def body(*refs):
        outs = refs[4 * k:]
        for i in range(k):
            d, nm, nv = _adam_math(refs[i][...], refs[k + i][...], refs[2 * k + i][...], refs[3 * k + i][...])
            outs[i][...] = d
            outs[k + i][...] = nm
            outs[2 * k + i][...] = nv

    outs = pl.pallas_call(body, name=name, out_shape=tuple(jax.ShapeDtypeStruct(w.shape, F32) for w in ws) * 3,
                          compiler_params=pltpu.CompilerParams(vmem_limit_bytes=VMEM_LIMIT))(*gs, *ws, *ms, *vs)
    return outs[:k], outs[k:2 * k], outs[2 * k:]


def _inproj_fwd(x, g, w_in, avg, qg, kg, tm=512):
    n = x.shape[0]

    def body(x_ref, g_ref, w_ref, a_ref, qg_ref, kg_ref, hn_ref, u_ref, qkv_ref, raw_ref, fl_ref):
        hn = _rms(x_ref[...], g_ref[...]).astype(BF16)
        hn_ref[...] = hn
        proj = _dot_nt(hn, w_ref[...])
        u_ref[...] = proj[:, 0:512]
        q = proj[:, 512:1024]
        k = proj[:, 1024:1536]
        raw_ref[:, 0:512] = q
        raw_ref[:, 512:1024] = k
        qkv_ref[:, 0:512] = _headnorm(q, a_ref[...], qg_ref[...])
        qkv_ref[:, 512:1024] = _headnorm(k, a_ref[...], kg_ref[...])
        qkv_ref[:, 1024:1536] = proj[:, 1536:2048]
        fl_ref[...] = proj[:, 2048:D_IN_PAD]

    row = lambda w: pl.BlockSpec((tm, w), lambda i: (i, 0))
    full = lambda a: pl.BlockSpec(a.shape, lambda i: (0,) * a.ndim)
    return pl.pallas_call(
        body, name="inproj_fwd", grid=(n // tm,),
        in_specs=[row(D_MODEL), full(g), full(w_in), full(avg), full(qg), full(kg)],
        out_specs=(row(D_MODEL), row(512), row(1536), row(1024), row(LANES)),
        out_shape=(jax.ShapeDtypeStruct((n, D_MODEL), BF16), jax.ShapeDtypeStruct((n, 512), F32),
                   jax.ShapeDtypeStruct((n, 1536), F32), jax.ShapeDtypeStruct((n, 1024), F32),
                   jax.ShapeDtypeStruct((n, LANES), F32)),
        compiler_params=_cparams("parallel"),
    )(x, g, w_in, avg, qg, kg)


def _inproj_bwd(x, g, w_in, avg, qg, kg, raw, du, dqn, dkn, dv, dfl, dres, tm=512):
    n = x.shape[0]

    def body(x_ref, g_ref, w_ref, a_ref, qg_ref, kg_ref, raw_ref, du_ref, dqn_ref, dkn_ref, dv_ref, dfl_ref, dres_ref,
             dx_ref, dproj_ref, dg_ref, dqg_ref, dkg_ref):
        @pl.when(pl.program_id(0) == 0)
        def _():
            dg_ref[...] = jnp.zeros_like(dg_ref)
            dqg_ref[...] = jnp.zeros_like(dqg_ref)
            dkg_ref[...] = jnp.zeros_like(dkg_ref)
        avg_m = a_ref[...]
        _, vjp_q = jax.vjp(lambda q, gg: _headnorm(q, avg_m, gg), raw_ref[:, 0:512], qg_ref[...])
        dq, dqg = vjp_q(dqn_ref[...])
        _, vjp_k = jax.vjp(lambda k, gg: _headnorm(k, avg_m, gg), raw_ref[:, 512:1024], kg_ref[...])
        dk, dkg = vjp_k(dkn_ref[...])
        dproj = jnp.concatenate([du_ref[...], dq, dk, dv_ref[...], dfl_ref[...]], axis=1).astype(BF16)
        dproj_ref[...] = dproj
        dhn = _dot(dproj, w_ref[...])
        _, vjp_x = jax.vjp(_rms, x_ref[...], g_ref[...])
        dxn, dg = vjp_x(dhn)
        dx_ref[...] = dxn + dres_ref[...]
        dg_ref[...] += dg
        dqg_ref[...] += dqg
        dkg_ref[...] += dkg

    row = lambda w: pl.BlockSpec((tm, w), lambda i: (i, 0))
    full = lambda a: pl.BlockSpec(a.shape, lambda i: (0,) * a.ndim)
    vec = lambda w: pl.BlockSpec((1, w), lambda i: (0, 0))
    return pl.pallas_call(
        body, name="inproj_bwd", grid=(n // tm,),
        in_specs=[row(D_MODEL), full(g), full(w_in), full(avg), full(qg), full(kg), row(1024), row(512), row(512),
                  row(512), row(512), row(LANES), row(D_MODEL)],
        out_specs=(row(D_MODEL), row(D_IN_PAD), vec(D_MODEL), vec(512), vec(512)),
        out_shape=(jax.ShapeDtypeStruct((n, D_MODEL), F32), jax.ShapeDtypeStruct((n, D_IN_PAD), BF16),
                   jax.ShapeDtypeStruct((1, D_MODEL), F32), jax.ShapeDtypeStruct((1, 512), F32),
                   jax.ShapeDtypeStruct((1, 512), F32)),
        compiler_params=_cparams("arbitrary"),
    )(x, g, w_in, avg, qg, kg, raw, du, dqn, dkn, dv, dfl, dres)


def _glu_fwd(yc, wg, bg, tm=512):
    n = yc.shape[0]

    def body(yc_ref, w_ref, b_ref, ys_ref):
        gl = jax.nn.gelu(yc_ref[...])
        z = _dot(gl.astype(BF16), w_ref[...]) + b_ref[...]
        ys_ref[...] = gl * jax.nn.sigmoid(z)

    row = pl.BlockSpec((tm, 512), lambda i: (i, 0))
    full = lambda a: pl.BlockSpec(a.shape, lambda i: (0,) * a.ndim)
    return pl.pallas_call(
        body, name="glu_fwd", grid=(n // tm,), in_specs=[row, full(wg), full(bg)], out_specs=row,
        out_shape=jax.ShapeDtypeStruct((n, 512), F32), compiler_params=_cparams("parallel"),
    )(yc, wg, bg)


def _glu_bwd(yc, dys, wg, bg, tm=512):
    n = yc.shape[0]

    def body(yc_ref, dys_ref, w_ref, b_ref, dyc_ref, gl_ref, dz_ref, db_ref):
        @pl.when(pl.program_id(0) == 0)
        def _():
            db_ref[...] = jnp.zeros_like(db_ref)
        gl, vjp_gelu = jax.vjp(jax.nn.gelu, yc_ref[...])
        glb = gl.astype(BF16)
        z = _dot(glb, w_ref[...]) + b_ref[...]
        s = jax.nn.sigmoid(z)
        dys = dys_ref[...]
        dz = dys * gl * s * (1.0 - s)
        dzb = dz.astype(BF16)
        dgl = dys * s + _dot_nt(dzb, w_ref[...])
        dyc_ref[...] = vjp_gelu(dgl)[0]
        gl_ref[...] = glb
        dz_ref[...] = dzb
        db_ref[...] += jnp.sum(dz, axis=0, keepdims=True)

    row = pl.BlockSpec((tm, 512), lambda i: (i, 0))
    full = lambda a: pl.BlockSpec(a.shape, lambda i: (0,) * a.ndim)
    return pl.pallas_call(
        body, name="glu_bwd", grid=(n // tm,), in_specs=[row, row, full(wg), full(bg)],
        out_specs=(row, row, row, pl.BlockSpec((1, 512), lambda i: (0, 0))),
        out_shape=(jax.ShapeDtypeStruct((n, 512), F32), jax.ShapeDtypeStruct((n, 512), BF16),
                   jax.ShapeDtypeStruct((n, 512), BF16), jax.ShapeDtypeStruct((1, 512), F32)),
        compiler_params=_cparams("arbitrary"),
    )(yc, dys, wg, bg)


def _mix_fwd(x, ys, ya, gs, ga, wout, gf, tm=512):
    n = x.shape[0]

    def body(x_ref, ys_ref, ya_ref, gs_ref, ga_ref, w_ref, gf_ref, h1_ref, hn2_ref, mixed_ref):
        mixed = jnp.concatenate([_rms(ys_ref[...], gs_ref[...]), _rms(ya_ref[...], ga_ref[...])], axis=1).astype(BF16)
        mixed_ref[...] = mixed
        h1 = x_ref[...] + _dot(mixed, w_ref[...])
        h1_ref[...] = h1
        hn2_ref[...] = _rms(h1, gf_ref[...]).astype(BF16)

    row = lambda w: pl.BlockSpec((tm, w), lambda i: (i, 0))
    full = lambda a: pl.BlockSpec(a.shape, lambda i: (0,) * a.ndim)
    return pl.pallas_call(
        body, name="mix_fwd", grid=(n // tm,),
        in_specs=[row(D_MODEL), row(512), row(512), full(gs), full(ga), full(wout), full(gf)],
        out_specs=(row(D_MODEL), row(D_MODEL), row(D_MODEL)),
        out_shape=(jax.ShapeDtypeStruct((n, D_MODEL), F32), jax.ShapeDtypeStruct((n, D_MODEL), BF16),
                   jax.ShapeDtypeStruct((n, D_MODEL), BF16)),
        compiler_params=_cparams("parallel"),
    )(x, ys, ya, gs, ga, wout, gf)


def _mix_bwd(dy, dhn2_parts, h1, ys, ya, gs, ga, wout, gf, tm=512):
    n = dy.shape[0]
    n_parts = dhn2_parts.shape[0]

    def body(dy_ref, dp_ref, h1_ref, ys_ref, ya_ref, gs_ref, ga_ref, w_ref, gf_ref,
             dh1_ref, dys_ref, dya_ref, dgs_ref, dga_ref, dgf_ref):
        @pl.when(pl.program_id(0) == 0)
        def _():
            dgs_ref[...] = jnp.zeros_like(dgs_ref)
            dga_ref[...] = jnp.zeros_like(dga_ref)
            dgf_ref[...] = jnp.zeros_like(dgf_ref)
        dhn2 = dp_ref[0]
        for p in range(1, n_parts):
            dhn2 = dhn2 + dp_ref[p]
        _, vjp_f = jax.vjp(_rms, h1_ref[...], gf_ref[...])
        dh1n, dgf = vjp_f(dhn2)
        dh1 = dy_ref[...] + dh1n
        dh1_ref[...] = dh1
        dmixed = _dot_nt(dh1.astype(BF16), w_ref[...])
        _, vjp_s = jax.vjp(_rms, ys_ref[...], gs_ref[...])
        dys, dgs = vjp_s(dmixed[:, 0:512])
        _, vjp_a = jax.vjp(_rms, ya_ref[...], ga_ref[...])
        dya, dga = vjp_a(dmixed[:, 512:1024])
        dys_ref[...] = dys
        dya_ref[...] = dya
        dgs_ref[...] += dgs
        dga_ref[...] += dga
        dgf_ref[...] += dgf

    row = lambda w: pl.BlockSpec((tm, w), lambda i: (i, 0))
    full = lambda a: pl.BlockSpec(a.shape, lambda i: (0,) * a.ndim)
    vec = lambda w: pl.BlockSpec((1, w), lambda i: (0, 0))
    return pl.pallas_call(
        body, name="mix_bwd", grid=(n // tm,),
        in_specs=[row(D_MODEL), pl.BlockSpec((n_parts, tm, D_MODEL), lambda i: (0, i, 0)), row(D_MODEL), row(512),
                  row(512), full(gs), full(ga), full(wout), full(gf)],
        out_specs=(row(D_MODEL), row(512), row(512), vec(512), vec(512), vec(D_MODEL)),
        out_shape=(jax.ShapeDtypeStruct((n, D_MODEL), F32), jax.ShapeDtypeStruct((n, 512), F32),
                   jax.ShapeDtypeStruct((n, 512), F32), jax.ShapeDtypeStruct((1, 512), F32),
                   jax.ShapeDtypeStruct((1, 512), F32), jax.ShapeDtypeStruct((1, D_MODEL), F32)),
        compiler_params=_cparams("arbitrary"),
    )(dy, dhn2_parts, h1, ys, ya, gs, ga, wout, gf)


HALO = 16
FFN_GROUPS = 4
FFN_GROUP = D_FF // FFN_GROUPS


def _conv3(ue, cw):
    return cw[2:3] * ue + cw[1:2] * pltpu.roll(ue, 1, 0) + cw[0:1] * pltpu.roll(ue, 2, 0) + cw[3:4]


def _ffn_weight_specs():
    gate = lambda i, j: (j, 0, 0)
    val = lambda i, j: (j + FFN_GROUPS, 0, 0)
    w_blk, c_blk = (1, FFN_GROUP, D_MODEL), (1, SUBLANES, FFN_GROUP)
    return [pl.BlockSpec(w_blk, gate), pl.BlockSpec(w_blk, val), pl.BlockSpec(c_blk, gate), pl.BlockSpec(c_blk, val),
            pl.BlockSpec((1, FFN_GROUP, D_MODEL), gate)]


def _ffn_fwd(hn2, h1, target, w_up, conv, w_down, seq_len, tm=512):
    n = hn2.shape[0]
    nj = FFN_GROUPS
    hb = tm // HALO

    def body(hn_ref, halo_ref, h1_ref, tgt_ref, wg_ref, wv_ref, cg_ref, cv_ref, wd_ref,
             ug_ref, uv_ref, pg_ref, pv_ref, dy_ref, loss_ref, acc):
        i, j = pl.program_id(0), pl.program_id(1)
        seq_start = (i * tm) % seq_len == 0
        halo = halo_ref[...]
        halo = jnp.where(seq_start, jnp.zeros_like(halo), halo)
        he = jnp.concatenate([halo, hn_ref[...]], axis=0)
        ueg = _dot_nt(he, wg_ref[0])
        uev = _dot_nt(he, wv_ref[0])
        ug_ref[0] = ueg[HALO:].astype(BF16)
        uv_ref[0] = uev[HALO:].astype(BF16)
        cg = _conv3(ueg, cg_ref[0])[HALO:]
        cv = _conv3(uev, cv_ref[0])[HALO:]
        pg_ref[0] = cg.astype(BF16)
        pv_ref[0] = cv.astype(BF16)
        act = (jax.nn.silu(cg) * cv).astype(BF16)
        part = _dot(act, wd_ref[0])

        @pl.when(j == 0)
        def _():
            acc[...] = part

        @pl.when(j > 0)
        def _():
            acc[...] += part

        @pl.when(j == nj - 1)
        def _():
            err = h1_ref[...] + acc[...] - tgt_ref[...]
            dy_ref[...] = err * (1.0 / D_MODEL)
            loss_ref[0] = jnp.sum(err * err, axis=0, keepdims=True)

    row = pl.BlockSpec((tm, D_MODEL), lambda i, j: (i, 0))
    u_main = pl.BlockSpec((1, tm, FFN_GROUP), lambda i, j: (j, i, 0))
    u_shape = jax.ShapeDtypeStruct((FFN_GROUPS, n, FFN_GROUP), BF16)
    return pl.pallas_call(
        body, name="ffn_fwd", grid=(n // tm, nj),
        in_specs=[row, pl.BlockSpec((HALO, D_MODEL), lambda i, j: (jnp.maximum(i * hb - 1, 0), 0)), row, row,
                  *_ffn_weight_specs()],
        out_specs=(u_main, u_main, u_main, u_main, row, pl.BlockSpec((1, 1, D_MODEL), lambda i, j: (i, 0, 0))),
        out_shape=(u_shape, u_shape, u_shape, u_shape, jax.ShapeDtypeStruct((n, D_MODEL), F32),
                   jax.ShapeDtypeStruct((n // tm, 1, D_MODEL), F32)),
        scratch_shapes=[pltpu.VMEM((tm, D_MODEL), F32)],
        compiler_params=_cparams("parallel", "arbitrary"),
    )(hn2, hn2, h1, target, w_up, w_up, conv, conv, w_down)


def _ffn_bwd(dy, ug, uv, pg, pv, w_up, conv, w_down, seq_len, tm=512):
    n = dy.shape[0]
    nj = FFN_GROUPS
    fb = FFN_GROUP
    hb = tm // HALO
    last_hb = n // HALO - 1
    rows = tm + HALO

    def body(dy_ref, dyn_ref, ug_ref, uv_ref, pgm_ref, pgn_ref, pvm_ref, pvn_ref, wg_ref, wv_ref, cg_ref, cv_ref,
             wd_ref, dug_ref, duv_ref, act_ref, dhn_ref, dcg_ref, dcv_ref, acc):
        i, j = pl.program_id(0), pl.program_id(1)
        seq_end = ((i + 1) * tm) % seq_len == 0
        dyn = dyn_ref[...]
        dyn = jnp.where(seq_end, jnp.zeros_like(dyn), dyn)
        d_out = jnp.concatenate([dy_ref[...], dyn], axis=0).astype(BF16)
        d_act = _dot_nt(d_out, wd_ref[0])
        cge = jnp.concatenate([pgm_ref[0], pgn_ref[0]], axis=0).astype(F32)
        cve = jnp.concatenate([pvm_ref[0], pvn_ref[0]], axis=0).astype(F32)
        act, vjp_act = jax.vjp(lambda g, v: jax.nn.silu(g) * v, cge, cve)
        dcge, dcve = vjp_act(d_act)
        act_ref[0] = act[:tm].astype(BF16)

        def conv_t(dc, u_ref, cw):
            ahead1 = pltpu.roll(dc, rows - 1, 0)[:tm]
            ahead2 = pltpu.roll(dc, rows - 2, 0)[:tm]
            here = dc[:tm]
            du = cw[2:3] * here + cw[1:2] * ahead1 + cw[0:1] * ahead2
            u = u_ref[0].astype(F32)
            col = lambda x: jnp.sum(x, axis=0, keepdims=True)
            grad = jnp.concatenate([col(ahead2 * u), col(ahead1 * u), col(here * u), col(here),
                                    jnp.zeros((4, fb), F32)], axis=0)
            return du.astype(BF16), grad

        cwg, cwv = cg_ref[0], cv_ref[0]
        dug, grad_g = conv_t(dcge, ug_ref, cwg)
        duv, grad_v = conv_t(dcve, uv_ref, cwv)
        dug_ref[0] = dug
        duv_ref[0] = duv
        part = _dot(dug, wg_ref[0]) + _dot(duv, wv_ref[0])

        @pl.when(j == 0)
        def _():
            acc[...] = part

        @pl.when(j > 0)
        def _():
            acc[...] += part

        @pl.when(j == nj - 1)
        def _():
            dhn_ref[...] = acc[...]

        @pl.when(i == 0)
        def _():
            dcg_ref[j] = jnp.zeros((8, fb), F32)
            dcv_ref[j] = jnp.zeros((8, fb), F32)

        dcg_ref[j] += grad_g
        dcv_ref[j] += grad_v

    row = pl.BlockSpec((tm, D_MODEL), lambda i, j: (i, 0))
    u_main = pl.BlockSpec((1, tm, fb), lambda i, j: (j, i, 0))
    u_next = pl.BlockSpec((1, HALO, fb), lambda i, j: (j, jnp.minimum((i + 1) * hb, last_hb), 0))
    dc_spec = pl.BlockSpec((nj, 8, fb), lambda i, j: (0, 0, 0))
    u_shape = jax.ShapeDtypeStruct((FFN_GROUPS, n, fb), BF16)
    return pl.pallas_call(
        body, name="ffn_bwd", grid=(n // tm, nj),
        in_specs=[row, pl.BlockSpec((HALO, D_MODEL), lambda i, j: (jnp.minimum((i + 1) * hb, last_hb), 0)),
                  u_main, u_main, u_main, u_next, u_main, u_next, *_ffn_weight_specs()],
        out_specs=(u_main, u_main, u_main, row, dc_spec, dc_spec),
        out_shape=(u_shape, u_shape, u_shape, jax.ShapeDtypeStruct((n, D_MODEL), F32),
                   jax.ShapeDtypeStruct((nj, 8, fb), F32), jax.ShapeDtypeStruct((nj, 8, fb), F32)),
        scratch_shapes=[pltpu.VMEM((tm, D_MODEL), F32)],
        compiler_params=_cparams("arbitrary", "arbitrary"),
    )(dy, dy, ug, uv, pg, pg, pv, pv, w_up, w_up, conv, conv, w_down)


def _tn_grouped(a, b, name, shared_a, out_dtype=F32, tn=1024):
    groups = b.shape[0] if shared_a else a.shape[0]
    n_tok = a.shape[0] if shared_a else b.shape[0]
    k_dim, m_dim = a.shape[-1], b.shape[-1]

    def body(a_ref, b_ref, o_ref, acc):
        k = pl.program_id(1)
        a_t = a_ref[...] if shared_a else a_ref[0]
        b_t = b_ref[0] if shared_a else b_ref[...]
        part = _dot_tn(a_t.astype(BF16), b_t.astype(BF16))

        @pl.when(k == 0)
        def _():
            acc[...] = part

        @pl.when(k > 0)
        def _():
            acc[...] += part

        @pl.when(k == n_tok // tn - 1)
        def _():
            o_ref[0] = acc[...].astype(out_dtype)

    plain = lambda w: pl.BlockSpec((tn, w), lambda g, k: (k, 0))
    grouped = lambda w: pl.BlockSpec((1, tn, w), lambda g, k: (g, k, 0))
    return pl.pallas_call(
        body, name=name, grid=(groups, n_tok // tn),
        in_specs=[plain(k_dim), grouped(m_dim)] if shared_a else [grouped(k_dim), plain(m_dim)],
        out_specs=pl.BlockSpec((1, k_dim, m_dim), lambda g, k: (g, 0, 0)),
        out_shape=jax.ShapeDtypeStruct((groups, k_dim, m_dim), out_dtype),
        scratch_shapes=[pltpu.VMEM((k_dim, m_dim), F32)],
        compiler_params=_cparams("parallel", "arbitrary"),
    )(a, b)


def _s5_param_fn(lr, li, ldt, br, bi):
    dt = jnp.exp(ldt)
    mag = jnp.exp(lr * dt)
    ab_re = mag * jnp.cos(li * dt)
    ab_im = mag * jnp.sin(li * dt)
    nr = ab_re - 1.0
    ni = ab_im
    den = lr * lr + li * li
    q_re = (nr * lr + ni * li) / den
    q_im = (ni * lr - nr * li) / den
    bb_re = q_re * br - q_im * bi
    bb_im = q_re * bi + q_im * br
    return ab_re, ab_im, bb_re, bb_im


def _s5_param_fwd(lr, li, ldt, br, bi):
    def body(lr_ref, li_ref, ldt_ref, br_ref, bi_ref, ar_ref, ai_ref, bbr_ref, bbi_ref):
        ar, ai, bbr, bbi = _s5_param_fn(lr_ref[...], li_ref[...], ldt_ref[...], br_ref[...], bi_ref[...])
        ar_ref[...] = ar
        ai_ref[...] = ai
        bbr_ref[...] = bbr
        bbi_ref[...] = bbi

    return pl.pallas_call(
        body, name="s5_param_fwd",
        out_shape=(jax.ShapeDtypeStruct(lr.shape, F32), jax.ShapeDtypeStruct(lr.shape, F32),
                   jax.ShapeDtypeStruct(br.shape, F32), jax.ShapeDtypeStruct(br.shape, F32)),
    )(lr, li, ldt, br, bi)


def _s5_param_bwd(lr, li, ldt, br, bi, dar, dai, dbbr, dbbi):
    def body(lr_ref, li_ref, ldt_ref, br_ref, bi_ref, dar_ref, dai_ref, dbbr_ref, dbbi_ref,
             dlr_ref, dli_ref, dldt_ref, dbr_ref, dbi_ref):
        _, vjp = jax.vjp(_s5_param_fn, lr_ref[...], li_ref[...], ldt_ref[...], br_ref[...], bi_ref[...])
        dlr, dli, dldt, dbr, dbi = vjp((dar_ref[...], dai_ref[...], dbbr_ref[...], dbbi_ref[...]))
        dlr_ref[...] = dlr
        dli_ref[...] = dli
        dldt_ref[...] = dldt
        dbr_ref[...] = dbr
        dbi_ref[...] = dbi

    return pl.pallas_call(
        body, name="s5_param_bwd",
        out_shape=(jax.ShapeDtypeStruct(lr.shape, F32), jax.ShapeDtypeStruct(lr.shape, F32),
                   jax.ShapeDtypeStruct(ldt.shape, F32), jax.ShapeDtypeStruct(br.shape, F32),
                   jax.ShapeDtypeStruct(br.shape, F32)),
    )(lr, li, ldt, br, bi, dar, dai, dbbr, dbbi)


S5_CHUNK = 256
S5_STATES = 512
S5_BLOCKS = 4


def _cpow_rows(ar, ai, count):
    rs, im = [ar], [ai]
    for _ in range(count - 1):
        pr, pi = rs[-1], im[-1]
        rs.append(pr * ar - pi * ai)
        im.append(pr * ai + pi * ar)
    return rs, im


def _scan_in_groups(vr, vi, pr, pi, rm, reverse):
    n, width = vr.shape
    vr = vr.reshape(n // SUBLANES, SUBLANES, width)
    vi = vi.reshape(n // SUBLANES, SUBLANES, width)
    row = rm[0:SUBLANES]
    for k in (1, 2, 4):
        shift = SUBLANES - k if reverse else k
        keep = row < SUBLANES - k if reverse else row >= k
        kr = jnp.where(keep, pr[k - 1], 0.0)
        ki = jnp.where(keep, pi[k - 1], 0.0)
        sr, si = pltpu.roll(vr, shift, 1), pltpu.roll(vi, shift, 1)
        vr, vi = vr + kr * sr - ki * si, vi + kr * si + ki * sr
    return vr.reshape(n, width), vi.reshape(n, width)


def _carry_over_groups(xr_s, xi_s, wr, wi, c0r, c0i, reverse):
    groups = xr_s.shape[0] // SUBLANES
    pick = 0 if reverse else SUBLANES - 1

    def step(q, carry):
        cr, ci = carry
        r = groups - 1 - q if reverse else q
        o = pl.multiple_of(r * SUBLANES, SUBLANES)
        vr = xr_s[pl.ds(o, SUBLANES), :]
        vi = xi_s[pl.ds(o, SUBLANES), :]
        nr = vr + wr * cr - wi * ci
        ni = vi + wr * ci + wi * cr
        xr_s[pl.ds(o, SUBLANES), :] = nr
        xi_s[pl.ds(o, SUBLANES), :] = ni
        return (jnp.broadcast_to(nr[pick:pick + 1], nr.shape), jnp.broadcast_to(ni[pick:pick + 1], ni.shape))

    return lax.fori_loop(0, groups, step, (c0r, c0i), unroll=4)


def _s5_state_scan(u_b, bbr, bbi, pr, pi, rm, xr_s, xi_s, c0r, c0i):
    bur = _dot(u_b, bbr)
    bui = _dot(u_b, bbi)
    bur, bui = _scan_in_groups(bur, bui, pr, pi, rm, False)
    xr_s[...] = bur
    xi_s[...] = bui
    w8r = jnp.concatenate(pr, axis=0)
    w8i = jnp.concatenate(pi, axis=0)
    return _carry_over_groups(xr_s, xi_s, w8r, w8i, c0r, c0i, False)


def _s5_fwd(u, a_re, a_im, bbr, bbi, cr, ci, d_skip, n_seq):
    n = u.shape[0]
    seq_len = n // n_seq
    nt = seq_len // S5_CHUNK
    tc = S5_CHUNK

    def body(u_ref, ar_ref, ai_ref, bbr_ref, bbi_ref, cr_ref, ci_ref, d_ref, y_ref, str_ref, sti_ref,
             xr_s, xi_s, car_r, car_i):
        t = pl.program_id(2)

        @pl.when(t == 0)
        def _():
            car_r[...] = jnp.zeros_like(car_r)
            car_i[...] = jnp.zeros_like(car_i)
        pr, pi = _cpow_rows(ar_ref[0], ai_ref[0], SUBLANES)
        rm = lax.broadcasted_iota(jnp.int32, (tc, S5_STATES), 0) & (SUBLANES - 1)
        str_ref[0, 0] = car_r[...]
        sti_ref[0, 0] = car_i[...]
        u_t = u_ref[...]
        cfr, cfi = _s5_state_scan(u_t.astype(BF16), bbr_ref[0], bbi_ref[0], pr, pi, rm, xr_s, xi_s,
                                  car_r[...], car_i[...])
        car_r[...] = cfr
        car_i[...] = cfi
        y = _dot(xr_s[...].astype(BF16), cr_ref[0]) - _dot(xi_s[...].astype(BF16), ci_ref[0])
        y_ref[...] = y + d_ref[...] * u_t

    u_spec = pl.BlockSpec((tc, LANES), lambda cb, b, t: (b * nt + t, cb))
    a_spec = pl.BlockSpec((1, 1, S5_STATES), lambda cb, b, t: (cb, 0, 0))
    bb_spec = pl.BlockSpec((1, LANES, S5_STATES), lambda cb, b, t: (cb, 0, 0))
    c_spec = pl.BlockSpec((1, S5_STATES, LANES), lambda cb, b, t: (cb, 0, 0))
    st_spec = pl.BlockSpec((1, 1, SUBLANES, S5_STATES), lambda cb, b, t: (cb, b * nt + t, 0, 0))
    st_shape = jax.ShapeDtypeStruct((S5_BLOCKS, n_seq * nt, SUBLANES, S5_STATES), F32)
    return pl.pallas_call(
        body, name="s5_fwd", grid=(S5_BLOCKS, n_seq, nt),
        in_specs=[u_spec, a_spec, a_spec, bb_spec, bb_spec, c_spec, c_spec,
                  pl.BlockSpec((1, LANES), lambda cb, b, t: (0, cb))],
        out_specs=(u_spec, st_spec, st_spec),
        out_shape=(jax.ShapeDtypeStruct((n, D_SSM), F32), st_shape, st_shape),
        scratch_shapes=[pltpu.VMEM((tc, S5_STATES), F32), pltpu.VMEM((tc, S5_STATES), F32),
                        pltpu.VMEM((SUBLANES, S5_STATES), F32), pltpu.VMEM((SUBLANES, S5_STATES), F32)],
        compiler_params=_cparams("parallel", "arbitrary", "arbitrary"),
    )(u, a_re, a_im, bbr, bbi, cr, ci, d_skip)


def _s5_bwd(u, dy, st_r, st_i, a_re, a_im, bbr, bbi, cr, ci, d_skip, n_seq):
    n = u.shape[0]
    seq_len = n // n_seq
    nt = seq_len // S5_CHUNK
    tc = S5_CHUNK

    def body(u_ref, dy_ref, str_ref, sti_ref, ar_ref, ai_ref, bbr_ref, bbi_ref, cr_ref, ci_ref, d_ref,
             du_ref, dbbr_ref, dbbi_ref, dcr_ref, dci_ref, dar_ref, dai_ref, dd_ref,
             xr_s, xi_s, gr_s, gi_s, car_r, car_i):
        b, t = pl.program_id(1), pl.program_id(2)

        @pl.when((b == 0) & (t == 0))
        def _():
            for ref in (dbbr_ref, dbbi_ref, dcr_ref, dci_ref, dar_ref, dai_ref, dd_ref):
                ref[...] = jnp.zeros_like(ref)

        @pl.when(t == 0)
        def _():
            car_r[...] = jnp.zeros_like(car_r)
            car_i[...] = jnp.zeros_like(car_i)
        ar, ai = ar_ref[0], ai_ref[0]
        pr, pi = _cpow_rows(ar, ai, SUBLANES)
        row = lax.broadcasted_iota(jnp.int32, (tc, S5_STATES), 0)
        rm = row & (SUBLANES - 1)
        u_t = u_ref[...]
        u_b = u_t.astype(BF16)
        dy_t = dy_ref[...]
        dy_b = dy_t.astype(BF16)
        s0r, s0i = str_ref[0, 0], sti_ref[0, 0]
        _s5_state_scan(u_b, bbr_ref[0], bbi_ref[0], pr, pi, rm, xr_s, xi_s, s0r, s0i)
        xr, xi = xr_s[...], xi_s[...]
        gr = _dot_nt(dy_b, cr_ref[0])
        gi = -_dot_nt(dy_b, ci_ref[0])
        npi = [-v for v in pi]
        gr, gi = _scan_in_groups(gr, gi, pr, npi, rm, True)
        gr_s[...] = gr
        gi_s[...] = gi
        w8r = jnp.concatenate(pr[::-1], axis=0)
        w8i = jnp.concatenate(npi[::-1], axis=0)
        cfr, cfi = _carry_over_groups(gr_s, gi_s, w8r, w8i, car_r[...], car_i[...], True)
        car_r[...] = cfr
        car_i[...] = cfi
        gr, gi = gr_s[...], gi_s[...]
        gr_b, gi_b = gr.astype(BF16), gi.astype(BF16)
        du_ref[...] = _dot_nt(gr_b, bbr_ref[0]) + _dot_nt(gi_b, bbi_ref[0]) + d_ref[...] * dy_t
        dbbr_ref[0] += _dot_tn(u_b, gr_b)
        dbbi_ref[0] += _dot_tn(u_b, gi_b)
        dcr_ref[0] += _dot_tn(xr.astype(BF16), dy_b)
        dci_ref[0] -= _dot_tn(xi.astype(BF16), dy_b)
        dd_ref[0] += jnp.sum((dy_t * u_t).reshape(tc // SUBLANES, SUBLANES, LANES), axis=0)
        first = row == 0
        xpr = jnp.where(first, jnp.broadcast_to(s0r[0:1], xr.shape), pltpu.roll(xr, 1, 0))
        xpi = jnp.where(first, jnp.broadcast_to(s0i[0:1], xi.shape), pltpu.roll(xi, 1, 0))
        shp = (tc // SUBLANES, SUBLANES, S5_STATES)
        dar_ref[0] += jnp.sum((gr * xpr + gi * xpi).reshape(shp), axis=0)
        dai_ref[0] += jnp.sum((gi * xpr - gr * xpi).reshape(shp), axis=0)

    u_spec = pl.BlockSpec((tc, LANES), lambda cb, b, t: (b * nt + nt - 1 - t, cb))
    a_spec = pl.BlockSpec((1, 1, S5_STATES), lambda cb, b, t: (cb, 0, 0))
    bb_spec = pl.BlockSpec((1, LANES, S5_STATES), lambda cb, b, t: (cb, 0, 0))
    c_spec = pl.BlockSpec((1, S5_STATES, LANES), lambda cb, b, t: (cb, 0, 0))
    st_spec = pl.BlockSpec((1, 1, SUBLANES, S5_STATES), lambda cb, b, t: (cb, b * nt + nt - 1 - t, 0, 0))
    da_spec = pl.BlockSpec((1, SUBLANES, S5_STATES), lambda cb, b, t: (cb, 0, 0))
    dd_spec = pl.BlockSpec((1, SUBLANES, LANES), lambda cb, b, t: (cb, 0, 0))
    big = pltpu.VMEM((tc, S5_STATES), F32)
    small = pltpu.VMEM((SUBLANES, S5_STATES), F32)
    return pl.pallas_call(
        body, name="s5_bwd", grid=(S5_BLOCKS, n_seq, nt),
        in_specs=[u_spec, u_spec, st_spec, st_spec, a_spec, a_spec, bb_spec, bb_spec, c_spec, c_spec,
                  pl.BlockSpec((1, LANES), lambda cb, b, t: (0, cb))],
        out_specs=(u_spec, bb_spec, bb_spec, c_spec, c_spec, da_spec, da_spec, dd_spec),
        out_shape=(jax.ShapeDtypeStruct((n, D_SSM), F32),
                   jax.ShapeDtypeStruct((S5_BLOCKS, LANES, S5_STATES), F32),
                   jax.ShapeDtypeStruct((S5_BLOCKS, LANES, S5_STATES), F32),
                   jax.ShapeDtypeStruct((S5_BLOCKS, S5_STATES, LANES), F32),
                   jax.ShapeDtypeStruct((S5_BLOCKS, S5_STATES, LANES), F32),
                   jax.ShapeDtypeStruct((S5_BLOCKS, SUBLANES, S5_STATES), F32),
                   jax.ShapeDtypeStruct((S5_BLOCKS, SUBLANES, S5_STATES), F32),
                   jax.ShapeDtypeStruct((S5_BLOCKS, SUBLANES, LANES), F32)),
        scratch_shapes=[big, big, big, big, small, small],
        compiler_params=_cparams("parallel", "arbitrary", "arbitrary"),
    )(u, dy, st_r, st_i, a_re, a_im, bbr, bbi, cr, ci, d_skip)


CUM_BLOCK = 128


def _tri(lower):
    r = lax.broadcasted_iota(jnp.int32, (CUM_BLOCK, CUM_BLOCK), 0)
    c = lax.broadcasted_iota(jnp.int32, (CUM_BLOCK, CUM_BLOCK), 1)
    return jnp.where(r >= c if lower else r <= c, 1.0, 0.0).astype(F32)


def _fprep_fwd(fl, bf, n_seq):
    n = fl.shape[0]
    seq_len = n // n_seq
    nb = seq_len // CUM_BLOCK

    def body(fl_ref, bf_ref, cum_ref):
        tril = _tri(True)
        carry = jnp.zeros((1, LANES), F32)
        for blk in range(nb):
            rows = slice(blk * CUM_BLOCK, (blk + 1) * CUM_BLOCK)
            lf = jax.nn.log_sigmoid(fl_ref[rows, :] + bf_ref[...])
            cs = jnp.dot(tril, lf, preferred_element_type=F32, precision=HIGHEST) + carry
            cum_ref[rows, :] = cs
            carry = cs[CUM_BLOCK - 1:CUM_BLOCK, :]

    spec = pl.BlockSpec((seq_len, LANES), lambda b: (b, 0))
    return pl.pallas_call(
        body, name="fprep_fwd", grid=(n_seq,), in_specs=[spec, pl.BlockSpec((1, LANES), lambda b: (0, 0))],
        out_specs=spec, out_shape=jax.ShapeDtypeStruct((n, LANES), F32), compiler_params=_cparams("parallel"),
    )(fl, bf)


def _fprep_bwd(dcum, fl, bf, n_seq):
    n = fl.shape[0]
    seq_len = n // n_seq
    nb = seq_len // CUM_BLOCK

    def body(dcum_ref, fl_ref, bf_ref, dfl_ref, dbf_ref):
        triu = _tri(False)
        lane = lax.broadcasted_iota(jnp.int32, (CUM_BLOCK, LANES), 1)
        carry = jnp.zeros((1, LANES), F32)
        total = jnp.zeros((1, LANES), F32)
        for blk in reversed(range(nb)):
            rows = slice(blk * CUM_BLOCK, (blk + 1) * CUM_BLOCK)
            rs = jnp.dot(triu, dcum_ref[rows, :], preferred_element_type=F32, precision=HIGHEST) + carry
            carry = rs[0:1, :]
            _, vjp = jax.vjp(jax.nn.log_sigmoid, fl_ref[rows, :] + bf_ref[...])
            dz = jnp.where(lane < N_HEADS, vjp(rs)[0], 0.0)
            dfl_ref[rows, :] = dz
            total = total + jnp.sum(dz, axis=0, keepdims=True)
        dbf_ref[0] = total

    spec = pl.BlockSpec((seq_len, LANES), lambda b: (b, 0))
    return pl.pallas_call(
        body, name="fprep_bwd", grid=(n_seq,), in_specs=[spec, spec, pl.BlockSpec((1, LANES), lambda b: (0, 0))],
        out_specs=(spec, pl.BlockSpec((1, 1, LANES), lambda b: (b, 0, 0))),
        out_shape=(jax.ShapeDtypeStruct((n, LANES), F32), jax.ShapeDtypeStruct((n_seq, 1, LANES), F32)),
        compiler_params=_cparams("parallel"),
    )(dcum, fl, bf)


ATT_TQ = 256
ATT_KSTEP = 256
ATT_SCALE = HEAD_DIM ** -0.5
NEG_BIG = -1e30


assert ATT_KSTEP == ATT_TQ


def _scores(q_scaled, kb, row_bias, ck, kend):
    s = _dot_nt(q_scaled, kb) - ck
    if row_bias is not None:
        s = s + row_bias
    r = lax.broadcasted_iota(jnp.int32, (ATT_TQ, ATT_TQ), 0)
    c = lax.broadcasted_iota(jnp.int32, (ATT_TQ, ATT_TQ), 1)
    diag = jnp.where(r >= c, s[:, kend - ATT_TQ:], NEG_BIG)
    return diag if kend == ATT_TQ else jnp.concatenate([s[:, :kend - ATT_TQ], diag], axis=1)


def _attn_specs(n_seq, seq_len):
    nq = seq_len // ATT_TQ
    q_spec = pl.BlockSpec((ATT_TQ, LANES), lambda b, h, q: (b * nq + q, h))
    k_spec = pl.BlockSpec((seq_len, LANES), lambda b, h, q: (b, N_HEADS // 2 + h))
    v_spec = pl.BlockSpec((seq_len, LANES), lambda b, h, q: (b, N_HEADS + h))
    cq_spec = pl.BlockSpec((1, 2, ATT_TQ, 1), lambda b, h, q: (b, h, q, 0))
    ck_spec = pl.BlockSpec((1, 2, 1, seq_len), lambda b, h, q: (b, h, 0, 0))
    return nq, q_spec, k_spec, v_spec, cq_spec, ck_spec


def _head_selectors():
    head0 = lax.broadcasted_iota(jnp.int32, (1, LANES), 1) < HEAD_DIM
    return head0, (head0, jnp.logical_not(head0))


def _for_key_range(qi, seq_len, run):
    per = ATT_KSTEP // ATT_TQ
    for g in range(seq_len // ATT_KSTEP):
        pl.when(qi // per == g)(functools.partial(run, (g + 1) * ATT_KSTEP))


def _attn_fwd(qkv, cq, ck, n_seq):
    n = qkv.shape[0]
    seq_len = n // n_seq
    nq, q_spec, k_spec, v_spec, cq_spec, ck_spec = _attn_specs(n_seq, seq_len)

    def body(q_ref, k_ref, v_ref, cq_ref, ck_ref, o_ref, lse_ref):
        qi = pl.program_id(2)
        q2 = q_ref[...]
        head0, sels = _head_selectors()
        qe = [jnp.where(sel, q2 * ATT_SCALE, 0.0).astype(BF16) for sel in sels]

        def run(kend):
            kb = k_ref[0:kend, :].astype(BF16)
            vb = v_ref[0:kend, :].astype(BF16)
            outs = []
            for e in range(2):
                s = _scores(qe[e], kb, None, ck_ref[0, e, :, 0:kend], kend)
                mx = jnp.max(s, axis=1, keepdims=True)
                p = jnp.exp(s - mx)
                den = jnp.sum(p, axis=1, keepdims=True)
                outs.append(_dot(p.astype(BF16), vb) / den)
                lse_ref[0, e] = cq_ref[0, e] + mx + jnp.log(den)
            o_ref[...] = jnp.where(head0, outs[0], outs[1])

        _for_key_range(qi, seq_len, run)

    return pl.pallas_call(
        body, name="attn_fwd", grid=(n_seq, N_HEADS // 2, nq),
        in_specs=[q_spec, k_spec, v_spec, cq_spec, ck_spec],
        out_specs=(q_spec, cq_spec),
        out_shape=(jax.ShapeDtypeStruct((n, D_ATTN), F32), jax.ShapeDtypeStruct((n_seq, N_HEADS, seq_len, 1), F32)),
        compiler_params=_cparams("parallel", "parallel", "parallel"),
    )(qkv, qkv, qkv, cq, ck)


def _attn_bwd(qkv, cq, ck, o, do, lse, n_seq):
    n = qkv.shape[0]
    seq_len = n // n_seq
    nq, q_spec, k_spec, v_spec, cq_spec, ck_spec = _attn_specs(n_seq, seq_len)
    kv_out = pl.BlockSpec((seq_len, LANES), lambda b, h, q: (b, h))

    def body(q_ref, k_ref, v_ref, cq_ref, ck_ref, o_ref, do_ref, lse_ref, dq_ref, dk_ref, dv_ref, dcq_ref, dck_ref):
        qi = pl.program_id(2)

        @pl.when(qi == 0)
        def _():
            dk_ref[...] = jnp.zeros_like(dk_ref)
            dv_ref[...] = jnp.zeros_like(dv_ref)
            dck_ref[...] = jnp.zeros_like(dck_ref)
        q2 = q_ref[...]
        do2 = do_ref[...]
        o2 = o_ref[...]
        head0, sels = _head_selectors()
        qe = [jnp.where(sel, q2 * ATT_SCALE, 0.0).astype(BF16) for sel in sels]
        doe = [jnp.where(sel, do2, 0.0) for sel in sels]
        doe_b = [d.astype(BF16) for d in doe]
        delta = [jnp.sum(d * o2, axis=1, keepdims=True) for d in doe]

        def run(kend):
            kb = k_ref[0:kend, :].astype(BF16)
            vb = v_ref[0:kend, :].astype(BF16)
            dqs = []
            dk = jnp.zeros((kend, LANES), F32)
            dv = jnp.zeros((kend, LANES), F32)
            for e in range(2):
                p = jnp.exp(_scores(qe[e], kb, cq_ref[0, e] - lse_ref[0, e], ck_ref[0, e, :, 0:kend], kend))
                ds = p * (_dot_nt(doe_b[e], vb) - delta[e])
                ds_b = ds.astype(BF16)
                dqs.append(_dot(ds_b, kb))
                dk = dk + _dot_tn(ds_b, qe[e])
                dv = dv + _dot_tn(p.astype(BF16), doe_b[e])
                dcq_ref[0, e] = jnp.sum(ds, axis=1, keepdims=True)
                dck_ref[0, e, :, 0:kend] -= jnp.sum(ds, axis=0, keepdims=True)
            dk_ref[0:kend, :] += dk
            dv_ref[0:kend, :] += dv
            dq_ref[...] = jnp.where(head0, dqs[0], dqs[1]) * ATT_SCALE

        _for_key_range(qi, seq_len, run)

    return pl.pallas_call(
        body, name="attn_bwd", grid=(n_seq, N_HEADS // 2, nq),
        in_specs=[q_spec, k_spec, v_spec, cq_spec, ck_spec, q_spec, q_spec, cq_spec],
        out_specs=(q_spec, kv_out, kv_out, cq_spec, ck_spec),
        out_shape=(jax.ShapeDtypeStruct((n, D_ATTN), F32), jax.ShapeDtypeStruct((n, D_ATTN), F32),
                   jax.ShapeDtypeStruct((n, D_ATTN), F32),
                   jax.ShapeDtypeStruct((n_seq, N_HEADS, seq_len, 1), F32),
                   jax.ShapeDtypeStruct((n_seq, N_HEADS, 1, seq_len), F32)),
        compiler_params=_cparams("parallel", "parallel", "arbitrary"),
    )(qkv, qkv, qkv, cq, ck, o, do, lse)


WEIGHT_NAMES = ("norm_mix", "w_in", "b_forget", "lam_re", "lam_im", "b_re", "b_im", "c_re", "c_im", "d_skip", "log_dt",
                "w_glu", "b_glu", "q_norm", "k_norm", "norm_out_ssm", "norm_out_attn", "w_out", "norm_ffn", "w_up",
                "conv_w", "conv_b", "w_down")
SHARDED = ("w_in", "w_glu", "w_out", "w_up", "conv_w", "w_down")
ADAM_TILE = {"w_in": (257, 256), "w_glu": (64, 512), "w_out": (128, 1024), "w_up": (688, 256), "conv_w": (3, 688),
             "w_down": (344, 1024)}
PACK_ROWS = SUBLANES * LANES


def _after_all(*arrays):
    return sum(a[(0,) * a.ndim].astype(F32) for a in arrays).reshape(1, 1)


def _pad_to(a, axis, size):
    pad = [(0, 0)] * a.ndim
    pad[axis] = (0, size - a.shape[axis])
    return jnp.pad(a, pad)


def _block_diag(t, transpose):
    t4 = t.reshape(S5_BLOCKS, 8, SSM_GROUP, SSM_STATE)
    eye = jnp.eye(8, dtype=t.dtype)
    if transpose:
        e = jnp.swapaxes(t4, 2, 3)[:, :, :, None, :] * eye[None, :, None, :, None]
        return e.reshape(S5_BLOCKS, S5_STATES, LANES)
    e = t4[:, :, :, None, :] * eye[None, :, None, :, None]
    return e.reshape(S5_BLOCKS, LANES, S5_STATES)


def _block_diag_extract(m, transpose):
    if transpose:
        m5 = m.reshape(S5_BLOCKS, 8, SSM_STATE, 8, SSM_GROUP)
        d = jnp.stack([m5[:, i, :, i, :] for i in range(8)], axis=1)
        return jnp.swapaxes(d, 2, 3).reshape(N_GROUPS, SSM_GROUP, SSM_STATE)
    m5 = m.reshape(S5_BLOCKS, 8, SSM_GROUP, 8, SSM_STATE)
    d = jnp.stack([m5[:, i, :, i, :] for i in range(8)], axis=1)
    return d.reshape(N_GROUPS, SSM_GROUP, SSM_STATE)


def _pack(pieces):
    flat = jnp.concatenate([p.reshape(-1).astype(F32) for p in pieces])
    size = -(-flat.shape[0] // PACK_ROWS) * PACK_ROWS
    return _pad_to(flat, 0, size).reshape(-1, LANES)


def _unpack(packed, shapes):
    flat = packed.reshape(-1)
    out, off = [], 0
    for shp in shapes:
        size = math.prod(shp)
        out.append(flat[off:off + size].reshape(shp))
        off += size
    return out


def kernel(x, norm_mix, w_in, b_forget, lam_re, lam_im, b_re, b_im, c_re, c_im, d_skip, log_dt, w_glu, b_glu, q_norm, k_norm, norm_out_ssm, norm_out_attn, w_out, norm_ffn, w_up, conv_w, conv_b, w_down, loss_target, m_norm_mix, m_w_in, m_b_forget, m_lam_re, m_lam_im, m_b_re, m_b_im, m_c_re, m_c_im, m_d_skip, m_log_dt, m_w_glu, m_b_glu, m_q_norm, m_k_norm, m_norm_out_ssm, m_norm_out_attn, m_w_out, m_norm_ffn, m_w_up, m_conv_w, m_conv_b, m_w_down, v_norm_mix, v_w_in, v_b_forget, v_lam_re, v_lam_im, v_b_re, v_b_im, v_c_re, v_c_im, v_d_skip, v_log_dt, v_w_glu, v_b_glu, v_q_norm, v_k_norm, v_norm_out_ssm, v_norm_out_attn, v_w_out, v_norm_ffn, v_w_up, v_conv_w, v_conv_b, v_w_down):
    given = dict(locals())
    weights = {k: given[k] for k in WEIGHT_NAMES}
    mom1 = {k: given["m_" + k] for k in WEIGHT_NAMES}
    mom2 = {k: given["v_" + k] for k in WEIGHT_NAMES}
    n_seq, seq_len, _ = x.shape
    n = n_seq * seq_len
    xf = x.reshape(n, D_MODEL)
    target = loss_target.reshape(n, D_MODEL)
    me_idx = 4 * lax.axis_index("x") + 2 * lax.axis_index("y") + lax.axis_index("c")

    in_flags = [False] * 2
    in_sems = _exchange_start([jnp.swapaxes(w_in[0], 0, 1).astype(BF16), conv_w[0]], in_flags, norm_mix,
                              "gather_in_start", 3)
    my_slot = lax.broadcasted_iota(jnp.int32, (N_DEV, 1, 1), 0) == me_idx

    lr3 = lam_re[0].reshape(N_GROUPS, 1, SSM_STATE)
    li3 = lam_im[0].reshape(N_GROUPS, 1, SSM_STATE)
    ldt3 = log_dt[0].reshape(N_GROUPS, 1, 1)
    br_t = jnp.swapaxes(b_re[0], 1, 2)
    bi_t = jnp.swapaxes(b_im[0], 1, 2)
    ab_re, ab_im, bb_re, bb_im = _s5_param_fwd(lr3, li3, ldt3, br_t, bi_t)
    a_re = ab_re.reshape(S5_BLOCKS, 1, S5_STATES)
    a_im = ab_im.reshape(S5_BLOCKS, 1, S5_STATES)
    bbr = _block_diag(bb_re, False).astype(BF16)
    bbi = _block_diag(bb_im, False).astype(BF16)
    cr = _block_diag(c_re[0], True).astype(BF16)
    ci = _block_diag(c_im[0], True).astype(BF16)
    avg = jnp.kron(jnp.eye(N_HEADS, dtype=F32), jnp.full((HEAD_DIM, HEAD_DIM), 1.0 / HEAD_DIM, F32)).astype(BF16)
    qg = jnp.tile(q_norm, (1, N_HEADS))
    kg = jnp.tile(k_norm, (1, N_HEADS))
    row_shards = [w_down[0].astype(BF16), w_out[0].astype(BF16), w_glu[0].astype(BF16)]

    (own_in, own_cw), (g_in, g_cw) = _exchange_wait(in_sems[0], in_sems[1], in_sems[2], in_sems[3], in_flags,
                                                    _after_all(a_re, a_im, bbr, bbi, cr, ci, avg, qg, kg, *row_shards),
                                                    "gather_in_wait")
    g_in = jnp.where(my_slot, own_in[None], g_in)
    g_cw = jnp.where(my_slot, own_cw[None], g_cw)
    row_flags = [False] * 3
    r_sems = _exchange_start(row_shards, row_flags, g_in, "gather_rows_start", 0)
    u_sems = _exchange_start([jnp.swapaxes(w_up[0], 0, 1).astype(BF16)], [False], r_sems[4], "gather_up_start", 5)
    norm_mix = norm_mix + u_sems[4][0, 0]
    w_in_p = _pad_to(g_in.reshape(D_IN, D_MODEL), 0, D_IN_PAD)

    hn, u, qkv, raw, fl = _inproj_fwd(xf, norm_mix, w_in_p, avg, qg, kg)
    yc, st_r, st_i = _s5_fwd(u, a_re, a_im, bbr, bbi, cr, ci, d_skip, n_seq)
    bf = _pad_to(b_forget, 1, LANES)
    cum = _fprep_fwd(fl, bf, n_seq)
    cum8 = jnp.swapaxes(cum[:, :N_HEADS].reshape(n_seq, seq_len, N_HEADS), 1, 2)
    cq = cum8[:, :, :, None]
    ck = cum8[:, :, None, :]
    ya, lse = _attn_fwd(qkv, cq, ck, n_seq)
    own_rows, got_rows = _exchange_wait(r_sems[0], r_sems[1], r_sems[2], r_sems[3], row_flags, ya, "gather_rows_wait")
    g_down, g_out, g_glu = [jnp.where(my_slot, o[None], g) for o, g in zip(own_rows, got_rows)]
    w_glu_f = g_glu.reshape(D_SSM, D_SSM)
    w_out_f = g_out.reshape(D_MODEL, D_MODEL)
    conv_st = _pad_to(jnp.concatenate([g_cw, conv_b.reshape(N_DEV, 1, -1)], axis=1), 1, SUBLANES)
    w_down4 = g_down.reshape(FFN_GROUPS, FFN_GROUP, D_MODEL)
    ys = _glu_fwd(yc, w_glu_f, b_glu)
    h1, hn2, mixed = _mix_fwd(xf, ys, ya, norm_out_ssm, norm_out_attn, w_out_f, norm_ffn)
    (own_up,), (g_up,) = _exchange_wait(u_sems[0], u_sems[1], u_sems[2], u_sems[3], [False], _after_all(h1, hn2),
                                        "gather_up_wait")
    g_up = jnp.where(my_slot, own_up[None], g_up)
    ug, uv, pg, pv, dy, loss_part = _ffn_fwd(hn2, h1, target, g_up, conv_st, w_down4, seq_len)
    loss_local = 0.5 * jnp.sum(loss_part) / D_MODEL

    dug, duv, act, dhn2, dcg, dcv = _ffn_bwd(dy, ug, uv, pg, pv, g_up, conv_st, w_down4, seq_len)
    dh1, dys, dya, d_gs, d_ga, d_gf = _mix_bwd(dy, dhn2[None], h1, ys, ya, norm_out_ssm, norm_out_attn, w_out_f, norm_ffn)
    dyc, gl_b, dz_b, d_bglu = _glu_bwd(yc, dys, w_glu_f, b_glu)

    gw_glu = _tn_matmul(gl_b, dz_b, "dw_glu", D_SSM, D_SSM, out_dtype=BF16)
    gw_out = _tn_matmul(mixed, dh1, "dw_out", D_MODEL, D_MODEL, out_dtype=BF16)
    gw_up = jnp.concatenate([_tn_grouped(dug, hn2, "dw_up_gate", False, BF16),
                             _tn_grouped(duv, hn2, "dw_up_val", False, BF16)], axis=0)
    gw_down = _tn_grouped(act, dy, "dw_down", False, BF16)
    g_conv = jnp.concatenate([dcg, dcv], axis=0)
    by_cols = lambda g, c: jnp.swapaxes(g.reshape(g.shape[0], N_DEV, c), 0, 1)
    early_flags = [True] * 4
    early_names = ("w_down", "w_out", "w_glu", "w_up")
    g_sems = _exchange_start(
        [gw_down.reshape(N_DEV, -1, D_MODEL), gw_out.reshape(N_DEV, -1, D_MODEL), gw_glu.reshape(N_DEV, -1, D_SSM), gw_up],
        early_flags, dyc, "grad_early_start", 1)
    started = g_sems[4][0, 0]

    du, dbbr, dbbi, dcr, dci, dar, dai, ddk = _s5_bwd(u, dyc, st_r, st_i, a_re, a_im, bbr, bbi, cr, ci,
                                                      d_skip + started, n_seq)
    partial_early = {
        "ab_re": jnp.sum(dar, axis=1), "ab_im": jnp.sum(dai, axis=1),
        "bb_re": _block_diag_extract(dbbr, False), "bb_im": _block_diag_extract(dbbi, False),
        "c_re": _block_diag_extract(dcr, True), "c_im": _block_diag_extract(dci, True),
        "d_skip": jnp.sum(ddk, axis=1), "b_glu": d_bglu,
        "norm_out_ssm": d_gs, "norm_out_attn": d_ga, "norm_ffn": d_gf, "conv_b": g_conv[:, 3],
    }
    early_keys = tuple(partial_early)
    early_shapes = [partial_early[k].shape for k in early_keys]
    p_sems = _exchange_start([_pack([partial_early[k] for k in early_keys])], [False], du, "small_early_start", 2)
    started = started + p_sems[4][0, 0]

    dqn, dkn, dv, dcq, dck = _attn_bwd(qkv, cq, ck + started, ya, dya, lse, n_seq)
    dcum8 = dcq[:, :, :, 0] + dck.reshape(n_seq, N_HEADS, seq_len)
    dcum = _pad_to(jnp.swapaxes(dcum8, 1, 2).reshape(n, N_HEADS), 1, LANES)
    dfl, dbf = _fprep_bwd(dcum, fl, bf, n_seq)
    dx, dproj, d_gmix, d_qg, d_kg = _inproj_bwd(xf, norm_mix, w_in_p, avg, qg, kg, raw, du, dqn, dkn, dv, dfl, dh1)

    gw_in = _tn_matmul(dproj, hn, "dw_in", D_IN_PAD, D_MODEL, out_rows=D_IN, out_dtype=BF16)
    partial_late = {
        "norm_mix": d_gmix, "b_forget": jnp.sum(dbf, axis=(0, 1))[:N_HEADS],
        "q_norm": jnp.sum(d_qg.reshape(N_HEADS, HEAD_DIM), axis=0),
        "k_norm": jnp.sum(d_kg.reshape(N_HEADS, HEAD_DIM), axis=0), "loss": loss_local.reshape(1),
    }
    late_keys = tuple(partial_late)
    late_shapes = [partial_late[k].shape for k in late_keys]

    late_flags = [True, True, False]
    l_sems = _exchange_start(
        [gw_in.reshape(N_DEV, D_IN // N_DEV, D_MODEL).astype(BF16), g_conv[:, :3],
         _pack([partial_late[k] for k in late_keys])],
        late_flags, dx, "grad_late_start", 4)
    early_src, early_land = _exchange_wait(g_sems[0], g_sems[1], g_sems[2], g_sems[3], early_flags, l_sems[4],
                                           "grad_early_wait")
    land = dict(zip(early_names, early_land))
    land_up = land["w_up"]
    own = {k: lax.dynamic_index_in_dim(s, me_idx, 0, keepdims=False) for k, s in zip(early_names, early_src)}
    grads, deltas, new_m, new_v = {}, {}, {}, {}

    def adam_shard(name):
        flip = (lambda a: jnp.swapaxes(a, 1, 2)) if name in ("w_in", "w_up") else (lambda a: a)
        outs = _adam_sharded(land[name], own[name], flip(weights[name]), flip(mom1[name]), flip(mom2[name]),
                             "adam_" + name, ADAM_TILE[name])
        grads[name], deltas[name], new_m[name], new_v[name] = [flip(o) for o in outs]

    for name in ("w_up", "w_down", "w_out", "w_glu"):
        adam_shard(name)
    (own_pack,), (early_parts,) = _exchange_wait(p_sems[0], p_sems[1], p_sems[2], p_sems[3], [False], land_up,
                                                 "small_early_wait")
    early_sum = _sum_partials(early_parts, own_pack, "sum_early_partials")
    (src_in, src_cw, own_late), (land["w_in"], land["conv_w"], late_parts) = _exchange_wait(
        l_sems[0], l_sems[1], l_sems[2], l_sems[3], late_flags,
        _after_all(early_sum, *[new_v[k] for k in ("w_up", "w_down", "w_out", "w_glu")]), "grad_late_wait")
    own["w_in"] = lax.dynamic_index_in_dim(src_in, me_idx, 0, keepdims=False)
    own["conv_w"] = lax.dynamic_index_in_dim(src_cw, me_idx, 0, keepdims=False)
    for name in ("w_in", "conv_w"):
        adam_shard(name)

    summed = dict(zip(late_keys, _unpack(_sum_partials(late_parts, own_late, "sum_late_partials"), late_shapes)))
    summed.update(zip(early_keys, _unpack(early_sum, early_shapes)))
    dlr, dli, dldt, dbr_t, dbi_t = _s5_param_bwd(
        lr3, li3, ldt3, br_t, bi_t, summed["ab_re"].reshape(lr3.shape), summed["ab_im"].reshape(lr3.shape),
        summed["bb_re"], summed["bb_im"])
    small_grads = {
        "norm_mix": summed["norm_mix"], "b_forget": summed["b_forget"], "lam_re": dlr, "lam_im": dli,
        "b_re": dbr_t, "b_im": dbi_t, "c_re": summed["c_re"], "c_im": summed["c_im"],
        "d_skip": summed["d_skip"], "log_dt": dldt, "b_glu": summed["b_glu"], "q_norm": summed["q_norm"],
        "k_norm": summed["k_norm"], "norm_out_ssm": summed["norm_out_ssm"], "norm_out_attn": summed["norm_out_attn"],
        "norm_ffn": summed["norm_ffn"], "conv_b": summed["conv_b"],
    }
    repl = tuple(k for k in WEIGHT_NAMES if k not in SHARDED)
    turn = lambda k, a: jnp.swapaxes(a, 2, 3) if k in ("b_re", "b_im") else a
    w_list = [turn(k, weights[k]) for k in repl]
    g_list = [small_grads[k].reshape(w.shape) for k, w in zip(repl, w_list)]
    d_list, m_list, v_list = _adam_replicated(g_list, w_list, [turn(k, mom1[k]) for k in repl],
                                              [turn(k, mom2[k]) for k in repl], "adam_replicated")
    for k, g, d, nm, nv in zip(repl, g_list, d_list, m_list, v_list):
        grads[k], deltas[k], new_m[k], new_v[k] = turn(k, g), turn(k, d), turn(k, nm), turn(k, nv)

    grad_x = dx.reshape(x.shape)
    loss = summed["loss"].reshape(())
    return (loss, grad_x, *[grads[k] for k in WEIGHT_NAMES], *[deltas[k] for k in WEIGHT_NAMES],
            *[new_m[k] for k in WEIGHT_NAMES], *[new_v[k] for k in WEIGHT_NAMES])
```

```python
import functools
import math

import jax
import jax.numpy as jnp
from jax import lax
from jax.experimental import pallas as pl
from jax.experimental.pallas import tpu as pltpu

F32 = jnp.float32
BF16 = jnp.bfloat16
HIGHEST = lax.Precision.HIGHEST

N_DEV = 8
D_MODEL = 1024
D_SSM = 512
D_ATTN = 512
N_HEADS = 8
HEAD_DIM = 64
N_GROUPS = 32
SSM_GROUP = 16
SSM_STATE = 64
D_FF = 2752
D_FF_PAD = 2816
D_IN = 2056
D_IN_PAD = 2176
EPS = 1e-6
LANES = 128
SUBLANES = 8
VMEM_LIMIT = 56 * 1024 * 1024

ADAM_LR = 0.001
ADAM_B1 = 0.9
ADAM_B2 = 0.999
ADAM_EPS = 1e-08
ADAM_WD = 0.01
ADAM_STEP = 10


def _cparams(*sem):
    return pltpu.CompilerParams(dimension_semantics=sem, vmem_limit_bytes=VMEM_LIMIT)


def _dot(a, b, **kw):
    return jnp.dot(a, b, preferred_element_type=F32, **kw)


def _dot_nt(a, b):
    return lax.dot_general(a, b, (((1,), (1,)), ((), ())), preferred_element_type=F32)


def _dot_tn(a, b):
    return lax.dot_general(a, b, (((0,), (0,)), ((), ())), preferred_element_type=F32)


def _rms(x, g):
    return x * lax.rsqrt(jnp.mean(x * x, axis=-1, keepdims=True) + EPS) * g


def _split_dot(x, avg):
    hi = x.astype(BF16)
    lo = (x - hi.astype(F32)).astype(BF16)
    return _dot(hi, avg) + _dot(lo, avg)


@jax.custom_vjp
def _group_mean(x, avg):
    return _split_dot(x, avg)


def _group_mean_fwd(x, avg):
    return _split_dot(x, avg), avg


def _group_mean_bwd(avg, ct):
    return _split_dot(ct, avg), jnp.zeros_like(avg)


_group_mean.defvjp(_group_mean_fwd, _group_mean_bwd)


def _headnorm(q, avg, g):
    return q * lax.rsqrt(_group_mean(q * q, avg) + EPS) * g


def _exchange(srcs, scatter_flags, name):
    n = len(srcs)
    out_shape = []
    for s, sc in zip(srcs, scatter_flags):
        shp = s.shape if sc else (N_DEV,) + s.shape
        out_shape.append(jax.ShapeDtypeStruct(shp, s.dtype))

    def body(*refs):
        src = refs[:n]
        dst = refs[n:2 * n]
        send_sems, recv_sems, loc_sems = refs[2 * n:]
        x, y, c = lax.axis_index("x"), lax.axis_index("y"), lax.axis_index("c")
        me = 4 * x + 2 * y + c
        peers = []
        for j in range(1, N_DEV):
            px = 1 - x if (j >> 2) & 1 else x
            py = 1 - y if (j >> 1) & 1 else y
            pc = 1 - c if j & 1 else c
            peers.append(((px, py, pc), 4 * px + 2 * py + pc))
        local, sends = [], []
        for k in range(n):
            own = src[k].at[me] if scatter_flags[k] else src[k]
            lc = pltpu.make_async_copy(own, dst[k].at[me], loc_sems.at[k])
            lc.start()
            local.append(lc)
            for j, (pid, pidx) in enumerate(peers):
                s = src[k].at[pidx] if scatter_flags[k] else src[k]
                cp = pltpu.make_async_remote_copy(
                    src_ref=s, dst_ref=dst[k].at[me], send_sem=send_sems.at[k, j], recv_sem=recv_sems.at[k, j],
                    device_id=pid, device_id_type=pl.DeviceIdType.MESH)
                cp.start()
                sends.append(cp)
        for k in range(n):
            for j, (pid, pidx) in enumerate(peers):
                s = src[k].at[pidx] if scatter_flags[k] else src[k]
                pltpu.make_async_remote_copy(
                    src_ref=s, dst_ref=dst[k].at[pidx], send_sem=send_sems.at[k, j], recv_sem=recv_sems.at[k, j],
                    device_id=pid, device_id_type=pl.DeviceIdType.MESH).wait_recv()
        for cp in sends:
            cp.wait_send()
        for lc in local:
            lc.wait()

    any_spec = pl.BlockSpec(memory_space=pl.ANY)
    return pl.pallas_call(
        body, name=name, out_shape=tuple(out_shape),
        in_specs=[any_spec] * n, out_specs=tuple([any_spec] * n),
        scratch_shapes=[pltpu.SemaphoreType.DMA((n, N_DEV - 1)), pltpu.SemaphoreType.DMA((n, N_DEV - 1)),
                        pltpu.SemaphoreType.DMA((n,))],
        compiler_params=pltpu.CompilerParams(has_side_effects=True),
    )(*srcs)


def _peer_list():
    x, y, c = lax.axis_index("x"), lax.axis_index("y"), lax.axis_index("c")
    peers = []
    for j in range(1, N_DEV):
        px = 1 - x if (j >> 2) & 1 else x
        py = 1 - y if (j >> 1) & 1 else y
        pc = 1 - c if j & 1 else c
        peers.append(((px, py, pc), 4 * px + 2 * py + pc))
    return 4 * x + 2 * y + c, peers


def _split_copies(src, land, send_sems, recv_sems, scatter_flags, me, peers, incoming):
    copies = []
    for k in range(len(src)):
        for j, (pid, pidx) in enumerate(peers):
            s = src[k].at[pidx] if scatter_flags[k] else src[k]
            i = k * (N_DEV - 1) + j
            copies.append(pltpu.make_async_remote_copy(
                src_ref=s, dst_ref=land[k].at[pidx if incoming else me], send_sem=send_sems[i],
                recv_sem=recv_sems[i], device_id=pid, device_id_type=pl.DeviceIdType.MESH))
    return copies


def _exchange_start(srcs, scatter_flags, after, name, collective_id):
    n = len(srcs)
    ns = n * (N_DEV - 1)
    hbm = pl.BlockSpec(memory_space=pltpu.HBM)
    sem = pl.BlockSpec(memory_space=pltpu.SEMAPHORE)
    land_shapes = [s.shape if sc else (N_DEV,) + s.shape for s, sc in zip(srcs, scatter_flags)]

    def body(*refs):
        src, land = refs[:n], refs[n:2 * n]
        send_sems = refs[2 * n + 1:2 * n + 1 + ns]
        recv_sems = refs[2 * n + 1 + ns:2 * n + 1 + 2 * ns]
        token = refs[4 * n + 1 + 2 * ns]
        me, peers = _peer_list()
        barrier = pltpu.get_barrier_semaphore()
        for pid, _ in peers:
            pl.semaphore_signal(barrier, inc=1, device_id=pid, device_id_type=pl.DeviceIdType.MESH)
        pl.semaphore_wait(barrier, N_DEV - 1)
        for cp in _split_copies(src, land, send_sems, recv_sems, scatter_flags, me, peers, False):
            cp.start()
        token[...] = jnp.zeros_like(token)

    outs = pl.pallas_call(
        body, name=name,
        out_shape=(*[pltpu.SemaphoreType.DMA(())] * (2 * ns), *[pltpu.HBM(s.shape, s.dtype) for s in srcs],
                   *[pltpu.HBM(shp, s.dtype) for shp, s in zip(land_shapes, srcs)],
                   jax.ShapeDtypeStruct((SUBLANES, LANES), F32)),
        in_specs=[hbm] * (2 * n) + [pl.BlockSpec(memory_space=pl.ANY)],
        out_specs=(*[sem] * (2 * ns), *[hbm] * (2 * n), pl.BlockSpec(memory_space=pltpu.VMEM)),
        input_output_aliases={i: 2 * ns + i for i in range(2 * n)},
        compiler_params=pltpu.CompilerParams(has_side_effects=pltpu.SideEffectType.DATAFLOW_SIDE_EFFECTING,
                                             collective_id=collective_id),
    )(*[pltpu.with_memory_space_constraint(s, pltpu.HBM) for s in srcs],
      *[pltpu.with_memory_space_constraint(lax.empty(shp, s.dtype), pltpu.HBM) for shp, s in zip(land_shapes, srcs)],
      after)
    return (outs[:ns], outs[ns:2 * ns], outs[2 * ns:2 * ns + n], outs[2 * ns + n:2 * ns + 2 * n], outs[2 * ns + 2 * n])


def _exchange_wait(send_sems, recv_sems, srcs, lands, scatter_flags, after, name):
    n = len(srcs)
    ns = n * (N_DEV - 1)
    hbm = pl.BlockSpec(memory_space=pltpu.HBM)
    sem = pl.BlockSpec(memory_space=pltpu.SEMAPHORE)

    def body(*refs):
        src, land = refs[:n], refs[n:2 * n]
        s_sems = refs[2 * n:2 * n + ns]
        r_sems = refs[2 * n + ns:2 * n + 2 * ns]
        me, peers = _peer_list()
        for cp in _split_copies(src, land, s_sems, r_sems, scatter_flags, me, peers, True):
            cp.wait_send()
            cp.wait_recv()

    outs = pl.pallas_call(
        body, name=name,
        out_shape=tuple(pltpu.HBM(a.shape, a.dtype) for a in (*srcs, *lands)),
        in_specs=[hbm] * (2 * n) + [sem] * (2 * ns) + [pl.BlockSpec(memory_space=pl.ANY)],
        out_specs=tuple([hbm] * (2 * n)),
        input_output_aliases={i: i for i in range(2 * n)},
        compiler_params=pltpu.CompilerParams(has_side_effects=pltpu.SideEffectType.DATAFLOW_SIDE_EFFECTING),
    )(*srcs, *lands, *send_sems, *recv_sems, after)
    return outs[:n], outs[n:]


def _tn_matmul(a, b, name, tk, tm, out_rows=None, out_cols=None, out_dtype=F32, tn=512):
    n_tok, k_dim = a.shape
    m_dim = b.shape[1]
    grid = (k_dim // tk, m_dim // tm, n_tok // tn)

    def body(a_ref, b_ref, o_ref, acc):
        k = pl.program_id(2)
        part = _dot_tn(a_ref[...].astype(BF16), b_ref[...].astype(BF16))

        @pl.when(k == 0)
        def _():
            acc[...] = part

        @pl.when(k > 0)
        def _():
            acc[...] += part

        @pl.when(k == grid[2] - 1)
        def _():
            o_ref[...] = acc[...].astype(out_dtype)

    return pl.pallas_call(
        body, name=name, grid=grid,
        in_specs=[pl.BlockSpec((tn, tk), lambda i, j, k: (k, i)), pl.BlockSpec((tn, tm), lambda i, j, k: (k, j))],
        out_specs=pl.BlockSpec((tk, tm), lambda i, j, k: (i, j)),
        out_shape=jax.ShapeDtypeStruct((out_rows or k_dim, out_cols or m_dim), out_dtype),
        scratch_shapes=[pltpu.VMEM((tk, tm), F32)],
        compiler_params=_cparams("parallel", "parallel", "arbitrary"),
    )(a, b)


def _adam_math(g, w, m, v):
    m = ADAM_B1 * m + (1.0 - ADAM_B1) * g
    v = ADAM_B2 * v + (1.0 - ADAM_B2) * (g * g)
    m_hat = m / (1.0 - ADAM_B1 ** ADAM_STEP)
    v_hat = v / (1.0 - ADAM_B2 ** ADAM_STEP)
    delta = -ADAM_LR * (m_hat / (jnp.sqrt(v_hat) + ADAM_EPS) + ADAM_WD * w)
    return delta, m, v


def _adam_sharded(land, own, w, m, v, name, tile):
    _, r, c = w.shape

    def body(*refs):
        l_ref = refs[0]
        own_ref = refs[1] if own is not None else None
        w_ref, m_ref, v_ref, g_ref, d_ref, nm_ref, nv_ref = [ref.at[0] for ref in refs[-7:]]
        if own_ref is not None:
            x, y, z = lax.axis_index("x"), lax.axis_index("y"), lax.axis_index("c")
            me = 4 * x + 2 * y + z
            mine = own_ref[...].astype(F32)
        g = None
        for s in range(N_DEV):
            part = l_ref[s].astype(F32)
            if own_ref is not None:
                part = jnp.where(me == s, mine, part)
            g = part if g is None else g + part
        d, nm, nv = _adam_math(g, w_ref[...], m_ref[...], v_ref[...])
        g_ref[...] = g
        d_ref[...] = d
        nm_ref[...] = nm
        nv_ref[...] = nv

    tr, tc = tile
    spec = pl.BlockSpec((1, tr, tc), lambda i, j: (0, i, j))
    own_specs, own_args = ([pl.BlockSpec((tr, tc), lambda i, j: (i, j))], [own]) if own is not None else ([], [])
    return pl.pallas_call(
        body, name=name, grid=(r // tr, c // tc),
        in_specs=[pl.BlockSpec((N_DEV, tr, tc), lambda i, j: (0, i, j)), *own_specs, spec, spec, spec],
        out_specs=(spec, spec, spec, spec),
        out_shape=tuple(jax.ShapeDtypeStruct((1, r, c), F32) for _ in range(4)),
        compiler_params=_cparams("parallel", "parallel"),
    )(land, *own_args, w, m, v)


def _sum_partials(parts, own, name):
    _, r, c = parts.shape

    def body(*refs):
        p_ref, o_ref = refs[0], refs[-1]
        if own is not None:
            x, y, z = lax.axis_index("x"), lax.axis_index("y"), lax.axis_index("c")
            me = 4 * x + 2 * y + z
            mine = refs[1][...]
        g = None
        for s in range(N_DEV):
            part = p_ref[s]
            if own is not None:
                part = jnp.where(me == s, mine, part)
            g = part if g is None else g + part
        o_ref[...] = g

    args = (parts,) if own is None else (parts, own)
    return pl.pallas_call(body, name=name, out_shape=jax.ShapeDtypeStruct((r, c), F32),
                          compiler_params=pltpu.CompilerParams(vmem_limit_bytes=VMEM_LIMIT))(*args)


def _adam_replicated(gs, ws, ms, vs, name):
    k = len(ws)

    def body(*refs):
        outs = refs[4 * k:]
        for i in range(k):
            d, nm, nv = _adam_math(refs[i][...], refs[k + i][...], refs[2 * k + i][...], refs[3 * k + i][...])
            outs[i][...] = d
            outs[k + i][...] = nm
            outs[2 * k + i][...] = nv

    outs = pl.pallas_call(body, name=name, out_shape=tuple(jax.ShapeDtypeStruct(w.shape, F32) for w in ws) * 3,
                          compiler_params=pltpu.CompilerParams(vmem_limit_bytes=VMEM_LIMIT))(*gs, *ws, *ms, *vs)
    return outs[:k], outs[k:2 * k], outs[2 * k:]


def _inproj_fwd(x, g, w_in, avg, qg, kg, tm=512):
    n = x.shape[0]

    def body(x_ref, g_ref, w_ref, a_ref, qg_ref, kg_ref, hn_ref, u_ref, qkv_ref, raw_ref, fl_ref):
        hn = _rms(x_ref[...], g_ref[...]).astype(BF16)
        hn_ref[...] = hn
        proj = _dot_nt(hn, w_ref[...])
        u_ref[...] = proj[:, 0:512]
        q = proj[:, 512:1024]
        k = proj[:, 1024:1536]
        raw_ref[:, 0:512] = q
        raw_ref[:, 512:1024] = k
        qkv_ref[:, 0:512] = _headnorm(q, a_ref[...], qg_ref[...])
        qkv_ref[:, 512:1024] = _headnorm(k, a_ref[...], kg_ref[...])
        qkv_ref[:, 1024:1536] = proj[:, 1536:2048]
        fl_ref[...] = proj[:, 2048:D_IN_PAD]

    row = lambda w: pl.BlockSpec((tm, w), lambda i: (i, 0))
    full = lambda a: pl.BlockSpec(a.shape, lambda i: (0,) * a.ndim)
    return pl.pallas_call(
        body, name="inproj_fwd", grid=(n // tm,),
        in_specs=[row(D_MODEL), full(g), full(w_in), full(avg), full(qg), full(kg)],
        out_specs=(row(D_MODEL), row(512), row(1536), row(1024), row(LANES)),
        out_shape=(jax.ShapeDtypeStruct((n, D_MODEL), BF16), jax.ShapeDtypeStruct((n, 512), F32),
                   jax.ShapeDtypeStruct((n, 1536), F32), jax.ShapeDtypeStruct((n, 1024), F32),
                   jax.ShapeDtypeStruct((n, LANES), F32)),
        compiler_params=_cparams("parallel"),
    )(x, g, w_in, avg, qg, kg)


def _inproj_bwd(x, g, w_in, avg, qg, kg, raw, du, dqn, dkn, dv, dfl, dres, tm=512):
    n = x.shape[0]

    def body(x_ref, g_ref, w_ref, a_ref, qg_ref, kg_ref, raw_ref, du_ref, dqn_ref, dkn_ref, dv_ref, dfl_ref, dres_ref,
             dx_ref, dproj_ref, dg_ref, dqg_ref, dkg_ref):
        @pl.when(pl.program_id(0) == 0)
        def _():
            dg_ref[...] = jnp.zeros_like(dg_ref)
            dqg_ref[...] = jnp.zeros_like(dqg_ref)
            dkg_ref[...] = jnp.zeros_like(dkg_ref)
        avg_m = a_ref[...]
        _, vjp_q = jax.vjp(lambda q, gg: _headnorm(q, avg_m, gg), raw_ref[:, 0:512], qg_ref[...])
        dq, dqg = vjp_q(dqn_ref[...])
        _, vjp_k = jax.vjp(lambda k, gg: _headnorm(k, avg_m, gg), raw_ref[:, 512:1024], kg_ref[...])
        dk, dkg = vjp_k(dkn_ref[...])
        dproj = jnp.concatenate([du_ref[...], dq, dk, dv_ref[...], dfl_ref[...]], axis=1).astype(BF16)
        dproj_ref[...] = dproj
        dhn = _dot(dproj, w_ref[...])
        _, vjp_x = jax.vjp(_rms, x_ref[...], g_ref[...])
        dxn, dg = vjp_x(dhn)
        dx_ref[...] = dxn + dres_ref[...]
        dg_ref[...] += dg
        dqg_ref[...] += dqg
        dkg_ref[...] += dkg

    row = lambda w: pl.BlockSpec((tm, w), lambda i: (i, 0))
    full = lambda a: pl.BlockSpec(a.shape, lambda i: (0,) * a.ndim)
    vec = lambda w: pl.BlockSpec((1, w), lambda i: (0, 0))
    return pl.pallas_call(
        body, name="inproj_bwd", grid=(n // tm,),
        in_specs=[row(D_MODEL), full(g), full(w_in), full(avg), full(qg), full(kg), row(1024), row(512), row(512),
                  row(512), row(512), row(LANES), row(D_MODEL)],
        out_specs=(row(D_MODEL), row(D_IN_PAD), vec(D_MODEL), vec(512), vec(512)),
        out_shape=(jax.ShapeDtypeStruct((n, D_MODEL), F32), jax.ShapeDtypeStruct((n, D_IN_PAD), BF16),
                   jax.ShapeDtypeStruct((1, D_MODEL), F32), jax.ShapeDtypeStruct((1, 512), F32),
                   jax.ShapeDtypeStruct((1, 512), F32)),
        compiler_params=_cparams("arbitrary"),
    )(x, g, w_in, avg, qg, kg, raw, du, dqn, dkn, dv, dfl, dres)


def _glu_fwd(yc, wg, bg, tm=512):
    n = yc.shape[0]

    def body(yc_ref, w_ref, b_ref, ys_ref):
        gl = jax.nn.gelu(yc_ref[...])
        z = _dot(gl.astype(BF16), w_ref[...]) + b_ref[...]
        ys_ref[...] = gl * jax.nn.sigmoid(z)

    row = pl.BlockSpec((tm, 512), lambda i: (i, 0))
    full = lambda a: pl.BlockSpec(a.shape, lambda i: (0,) * a.ndim)
    return pl.pallas_call(
        body, name="glu_fwd", grid=(n // tm,), in_specs=[row, full(wg), full(bg)], out_specs=row,
        out_shape=jax.ShapeDtypeStruct((n, 512), F32), compiler_params=_cparams("parallel"),
    )(yc, wg, bg)


def _glu_bwd(yc, dys, wg, bg, tm=512):
    n = yc.shape[0]

    def body(yc_ref, dys_ref, w_ref, b_ref, dyc_ref, gl_ref, dz_ref, db_ref):
        @pl.when(pl.program_id(0) == 0)
        def _():
            db_ref[...] = jnp.zeros_like(db_ref)
        gl, vjp_gelu = jax.vjp(jax.nn.gelu, yc_ref[...])
        glb = gl.astype(BF16)
        z = _dot(glb, w_ref[...]) + b_ref[...]
        s = jax.nn.sigmoid(z)
        dys = dys_ref[...]
        dz = dys * gl * s * (1.0 - s)
        dzb = dz.astype(BF16)
        dgl = dys * s + _dot_nt(dzb, w_ref[...])
        dyc_ref[...] = vjp_gelu(dgl)[0]
        gl_ref[...] = glb
        dz_ref[...] = dzb
        db_ref[...] += jnp.sum(dz, axis=0, keepdims=True)

    row = pl.BlockSpec((tm, 512), lambda i: (i, 0))
    full = lambda a: pl.BlockSpec(a.shape, lambda i: (0,) * a.ndim)
    return pl.pallas_call(
        body, name="glu_bwd", grid=(n // tm,), in_specs=[row, row, full(wg), full(bg)],
        out_specs=(row, row, row, pl.BlockSpec((1, 512), lambda i: (0, 0))),
        out_shape=(jax.ShapeDtypeStruct((n, 512), F32), jax.ShapeDtypeStruct((n, 512), BF16),
                   jax.ShapeDtypeStruct((n, 512), BF16), jax.ShapeDtypeStruct((1, 512), F32)),
        compiler_params=_cparams("arbitrary"),
    )(yc, dys, wg, bg)


def _mix_fwd(x, ys, ya, gs, ga, wout, gf, tm=512):
    n = x.shape[0]

    def body(x_ref, ys_ref, ya_ref, gs_ref, ga_ref, w_ref, gf_ref, h1_ref, hn2_ref, mixed_ref):
        mixed = jnp.concatenate([_rms(ys_ref[...], gs_ref[...]), _rms(ya_ref[...], ga_ref[...])], axis=1).astype(BF16)
        mixed_ref[...] = mixed
        h1 = x_ref[...] + _dot(mixed, w_ref[...])
        h1_ref[...] = h1
        hn2_ref[...] = _rms(h1, gf_ref[...]).astype(BF16)

    row = lambda w: pl.BlockSpec((tm, w), lambda i: (i, 0))
    full = lambda a: pl.BlockSpec(a.shape, lambda i: (0,) * a.ndim)
    return pl.pallas_call(
        body, name="mix_fwd", grid=(n // tm,),
        in_specs=[row(D_MODEL), row(512), row(512), full(gs), full(ga), full(wout), full(gf)],
        out_specs=(row(D_MODEL), row(D_MODEL), row(D_MODEL)),
        out_shape=(jax.ShapeDtypeStruct((n, D_MODEL), F32), jax.ShapeDtypeStruct((n, D_MODEL), BF16),
                   jax.ShapeDtypeStruct((n, D_MODEL), BF16)),
        compiler_params=_cparams("parallel"),
    )(x, ys, ya, gs, ga, wout, gf)


def _mix_bwd(dy, dhn2_parts, h1, ys, ya, gs, ga, wout, gf, tm=512):
    n = dy.shape[0]
    n_parts = dhn2_parts.shape[0]

    def body(dy_ref, dp_ref, h1_ref, ys_ref, ya_ref, gs_ref, ga_ref, w_ref, gf_ref,
             dh1_ref, dys_ref, dya_ref, dgs_ref, dga_ref, dgf_ref):
        @pl.when(pl.program_id(0) == 0)
        def _():
            dgs_ref[...] = jnp.zeros_like(dgs_ref)
            dga_ref[...] = jnp.zeros_like(dga_ref)
            dgf_ref[...] = jnp.zeros_like(dgf_ref)
        dhn2 = dp_ref[0]
        for p in range(1, n_parts):
            dhn2 = dhn2 + dp_ref[p]
        _, vjp_f = jax.vjp(_rms, h1_ref[...], gf_ref[...])
        dh1n, dgf = vjp_f(dhn2)
        dh1 = dy_ref[...] + dh1n
        dh1_ref[...] = dh1
        dmixed = _dot_nt(dh1.astype(BF16), w_ref[...])
        _, vjp_s = jax.vjp(_rms, ys_ref[...], gs_ref[...])
        dys, dgs = vjp_s(dmixed[:, 0:512])
        _, vjp_a = jax.vjp(_rms, ya_ref[...], ga_ref[...])
        dya, dga = vjp_a(dmixed[:, 512:1024])
        dys_ref[...] = dys
        dya_ref[...] = dya
        dgs_ref[...] += dgs
        dga_ref[...] += dga
        dgf_ref[...] += dgf

    row = lambda w: pl.BlockSpec((tm, w), lambda i: (i, 0))
    full = lambda a: pl.BlockSpec(a.shape, lambda i: (0,) * a.ndim)
    vec = lambda w: pl.BlockSpec((1, w), lambda i: (0, 0))
    return pl.pallas_call(
        body, name="mix_bwd", grid=(n // tm,),
        in_specs=[row(D_MODEL), pl.BlockSpec((n_parts, tm, D_MODEL), lambda i: (0, i, 0)), row(D_MODEL), row(512),
                  row(512), full(gs), full(ga), full(wout), full(gf)],
        out_specs=(row(D_MODEL), row(512), row(512), vec(512), vec(512), vec(D_MODEL)),
        out_shape=(jax.ShapeDtypeStruct((n, D_MODEL), F32), jax.ShapeDtypeStruct((n, 512), F32),
                   jax.ShapeDtypeStruct((n, 512), F32), jax.ShapeDtypeStruct((1, 512), F32),
                   jax.ShapeDtypeStruct((1, 512), F32), jax.ShapeDtypeStruct((1, D_MODEL), F32)),
        compiler_params=_cparams("arbitrary"),
    )(dy, dhn2_parts, h1, ys, ya, gs, ga, wout, gf)


HALO = 16
FFN_GROUPS = 4
FFN_GROUP = D_FF // FFN_GROUPS


def _conv3(ue, cw):
    return cw[2:3] * ue + cw[1:2] * pltpu.roll(ue, 1, 0) + cw[0:1] * pltpu.roll(ue, 2, 0) + cw[3:4]


def _ffn_weight_specs():
    gate = lambda i, j: (j, 0, 0)
    val = lambda i, j: (j + FFN_GROUPS, 0, 0)
    w_blk, c_blk = (1, FFN_GROUP, D_MODEL), (1, SUBLANES, FFN_GROUP)
    return [pl.BlockSpec(w_blk, gate), pl.BlockSpec(w_blk, val), pl.BlockSpec(c_blk, gate), pl.BlockSpec(c_blk, val),
            pl.BlockSpec((1, FFN_GROUP, D_MODEL), gate)]


def _ffn_fwd(hn2, h1, target, w_up, conv, w_down, seq_len, tm=512):
    n = hn2.shape[0]
    nj = FFN_GROUPS
    hb = tm // HALO

    def body(hn_ref, halo_ref, h1_ref, tgt_ref, wg_ref, wv_ref, cg_ref, cv_ref, wd_ref,
             ug_ref, uv_ref, pg_ref, pv_ref, dy_ref, loss_ref, acc):
        i, j = pl.program_id(0), pl.program_id(1)
        seq_start = (i * tm) % seq_len == 0
        halo = halo_ref[...]
        halo = jnp.where(seq_start, jnp.zeros_like(halo), halo)
        he = jnp.concatenate([halo, hn_ref[...]], axis=0)
        ueg = _dot_nt(he, wg_ref[0])
        uev = _dot_nt(he, wv_ref[0])
        ug_ref[0] = ueg[HALO:].astype(BF16)
        uv_ref[0] = uev[HALO:].astype(BF16)
        cg = _conv3(ueg, cg_ref[0])[HALO:]
        cv = _conv3(uev, cv_ref[0])[HALO:]
        pg_ref[0] = cg.astype(BF16)
        pv_ref[0] = cv.astype(BF16)
        act = (jax.nn.silu(cg) * cv).astype(BF16)
        part = _dot(act, wd_ref[0])

        @pl.when(j == 0)
        def _():
            acc[...] = part

        @pl.when(j > 0)
        def _():
            acc[...] += part

        @pl.when(j == nj - 1)
        def _():
            err = h1_ref[...] + acc[...] - tgt_ref[...]
            dy_ref[...] = err * (1.0 / D_MODEL)
            loss_ref[0] = jnp.sum(err * err, axis=0, keepdims=True)

    row = pl.BlockSpec((tm, D_MODEL), lambda i, j: (i, 0))
    u_main = pl.BlockSpec((1, tm, FFN_GROUP), lambda i, j: (j, i, 0))
    u_shape = jax.ShapeDtypeStruct((FFN_GROUPS, n, FFN_GROUP), BF16)
    return pl.pallas_call(
        body, name="ffn_fwd", grid=(n // tm, nj),
        in_specs=[row, pl.BlockSpec((HALO, D_MODEL), lambda i, j: (jnp.maximum(i * hb - 1, 0), 0)), row, row,
                  *_ffn_weight_specs()],
        out_specs=(u_main, u_main, u_main, u_main, row, pl.BlockSpec((1, 1, D_MODEL), lambda i, j: (i, 0, 0))),
        out_shape=(u_shape, u_shape, u_shape, u_shape, jax.ShapeDtypeStruct((n, D_MODEL), F32),
                   jax.ShapeDtypeStruct((n // tm, 1, D_MODEL), F32)),
        scratch_shapes=[pltpu.VMEM((tm, D_MODEL), F32)],
        compiler_params=_cparams("parallel", "arbitrary"),
    )(hn2, hn2, h1, target, w_up, w_up, conv, conv, w_down)


def _ffn_bwd(dy, ug, uv, pg, pv, w_up, conv, w_down, seq_len, tm=512):
    n = dy.shape[0]
    nj = FFN_GROUPS
    fb = FFN_GROUP
    hb = tm // HALO
    last_hb = n // HALO - 1
    rows = tm + HALO

    def body(dy_ref, dyn_ref, ug_ref, uv_ref, pgm_ref, pgn_ref, pvm_ref, pvn_ref, wg_ref, wv_ref, cg_ref, cv_ref,
             wd_ref, dug_ref, duv_ref, act_ref, dhn_ref, dcg_ref, dcv_ref, acc):
        i, j = pl.program_id(0), pl.program_id(1)
        seq_end = ((i + 1) * tm) % seq_len == 0
        dyn = dyn_ref[...]
        dyn = jnp.where(seq_end, jnp.zeros_like(dyn), dyn)
        d_out = jnp.concatenate([dy_ref[...], dyn], axis=0).astype(BF16)
        d_act = _dot_nt(d_out, wd_ref[0])
        cge = jnp.concatenate([pgm_ref[0], pgn_ref[0]], axis=0).astype(F32)
        cve = jnp.concatenate([pvm_ref[0], pvn_ref[0]], axis=0).astype(F32)
        act, vjp_act = jax.vjp(lambda g, v: jax.nn.silu(g) * v, cge, cve)
        dcge, dcve = vjp_act(d_act)
        act_ref[0] = act[:tm].astype(BF16)

        def conv_t(dc, u_ref, cw):
            ahead1 = pltpu.roll(dc, rows - 1, 0)[:tm]
            ahead2 = pltpu.roll(dc, rows - 2, 0)[:tm]
            here = dc[:tm]
            du = cw[2:3] * here + cw[1:2] * ahead1 + cw[0:1] * ahead2
            u = u_ref[0].astype(F32)
            col = lambda x: jnp.sum(x, axis=0, keepdims=True)
            grad = jnp.concatenate([col(ahead2 * u), col(ahead1 * u), col(here * u), col(here),
                                    jnp.zeros((4, fb), F32)], axis=0)
            return du.astype(BF16), grad

        cwg, cwv = cg_ref[0], cv_ref[0]
        dug, grad_g = conv_t(dcge, ug_ref, cwg)
        duv, grad_v = conv_t(dcve, uv_ref, cwv)
        dug_ref[0] = dug
        duv_ref[0] = duv
        part = _dot(dug, wg_ref[0]) + _dot(duv, wv_ref[0])

        @pl.when(j == 0)
        def _():
            acc[...] = part

        @pl.when(j > 0)
        def _():
            acc[...] += part

        @pl.when(j == nj - 1)
        def _():
            dhn_ref[...] = acc[...]

        @pl.when(i == 0)
        def _():
            dcg_ref[j] = jnp.zeros((8, fb), F32)
            dcv_ref[j] = jnp.zeros((8, fb), F32)

        dcg_ref[j] += grad_g
        dcv_ref[j] += grad_v

    row = pl.BlockSpec((tm, D_MODEL), lambda i, j: (i, 0))
    u_main = pl.BlockSpec((1, tm, fb), lambda i, j: (j, i, 0))
    u_next = pl.BlockSpec((1, HALO, fb), lambda i, j: (j, jnp.minimum((i + 1) * hb, last_hb), 0))
    dc_spec = pl.BlockSpec((nj, 8, fb), lambda i, j: (0, 0, 0))
    u_shape = jax.ShapeDtypeStruct((FFN_GROUPS, n, fb), BF16)
    return pl.pallas_call(
        body, name="ffn_bwd", grid=(n // tm, nj),
        in_specs=[row, pl.BlockSpec((HALO, D_MODEL), lambda i, j: (jnp.minimum((i + 1) * hb, last_hb), 0)),
                  u_main, u_main, u_main, u_next, u_main, u_next, *_ffn_weight_specs()],
        out_specs=(u_main, u_main, u_main, row, dc_spec, dc_spec),
        out_shape=(u_shape, u_shape, u_shape, jax.ShapeDtypeStruct((n, D_MODEL), F32),
                   jax.ShapeDtypeStruct((nj, 8, fb), F32), jax.ShapeDtypeStruct((nj, 8, fb), F32)),
        scratch_shapes=[pltpu.VMEM((tm, D_MODEL), F32)],
        compiler_params=_cparams("arbitrary", "arbitrary"),
    )(dy, dy, ug, uv, pg, pg, pv, pv, w_up, w_up, conv, conv, w_down)


def _tn_grouped(a, b, name, shared_a, out_dtype=F32, tn=1024):
    groups = b.shape[0] if shared_a else a.shape[0]
    n_tok = a.shape[0] if shared_a else b.shape[0]
    k_dim, m_dim = a.shape[-1], b.shape[-1]

    def body(a_ref, b_ref, o_ref, acc):
        k = pl.program_id(1)
        a_t = a_ref[...] if shared_a else a_ref[0]
        b_t = b_ref[0] if shared_a else b_ref[...]
        part = _dot_tn(a_t.astype(BF16), b_t.astype(BF16))

        @pl.when(k == 0)
        def _():
            acc[...] = part

        @pl.when(k > 0)
        def _():
            acc[...] += part

        @pl.when(k == n_tok // tn - 1)
        def _():
            o_ref[0] = acc[...].astype(out_dtype)

    plain = lambda w: pl.BlockSpec((tn, w), lambda g, k: (k, 0))
    grouped = lambda w: pl.BlockSpec((1, tn, w), lambda g, k: (g, k, 0))
    return pl.pallas_call(
        body, name=name, grid=(groups, n_tok // tn),
        in_specs=[plain(k_dim), grouped(m_dim)] if shared_a else [grouped(k_dim), plain(m_dim)],
        out_specs=pl.BlockSpec((1, k_dim, m_dim), lambda g, k: (g, 0, 0)),
        out_shape=jax.ShapeDtypeStruct((groups, k_dim, m_dim), out_dtype),
        scratch_shapes=[pltpu.VMEM((k_dim, m_dim), F32)],
        compiler_params=_cparams("parallel", "arbitrary"),
    )(a, b)


def _s5_param_fn(lr, li, ldt, br, bi):
    dt = jnp.exp(ldt)
    mag = jnp.exp(lr * dt)
    ab_re = mag * jnp.cos(li * dt)
    ab_im = mag * jnp.sin(li * dt)
    nr = ab_re - 1.0
    ni = ab_im
    den = lr * lr + li * li
    q_re = (nr * lr + ni * li) / den
    q_im = (ni * lr - nr * li) / den
    bb_re = q_re * br - q_im * bi
    bb_im = q_re * bi + q_im * br
    return ab_re, ab_im, bb_re, bb_im


def _s5_param_fwd(lr, li, ldt, br, bi):
    def body(lr_ref, li_ref, ldt_ref, br_ref, bi_ref, ar_ref, ai_ref, bbr_ref, bbi_ref):
        ar, ai, bbr, bbi = _s5_param_fn(lr_ref[...], li_ref[...], ldt_ref[...], br_ref[...], bi_ref[...])
        ar_ref[...] = ar
        ai_ref[...] = ai
        bbr_ref[...] = bbr
        bbi_ref[...] = bbi

    return pl.pallas_call(
        body, name="s5_param_fwd",
        out_shape=(jax.ShapeDtypeStruct(lr.shape, F32), jax.ShapeDtypeStruct(lr.shape, F32),
                   jax.ShapeDtypeStruct(br.shape, F32), jax.ShapeDtypeStruct(br.shape, F32)),
    )(lr, li, ldt, br, bi)


def _s5_param_bwd(lr, li, ldt, br, bi, dar, dai, dbbr, dbbi):
    def body(lr_ref, li_ref, ldt_ref, br_ref, bi_ref, dar_ref, dai_ref, dbbr_ref, dbbi_ref,
             dlr_ref, dli_ref, dldt_ref, dbr_ref, dbi_ref):
        _, vjp = jax.vjp(_s5_param_fn, lr_ref[...], li_ref[...], ldt_ref[...], br_ref[...], bi_ref[...])
        dlr, dli, dldt, dbr, dbi = vjp((dar_ref[...], dai_ref[...], dbbr_ref[...], dbbi_ref[...]))
        dlr_ref[...] = dlr
        dli_ref[...] = dli
        dldt_ref[...] = dldt
        dbr_ref[...] = dbr
        dbi_ref[...] = dbi

    return pl.pallas_call(
        body, name="s5_param_bwd",
        out_shape=(jax.ShapeDtypeStruct(lr.shape, F32), jax.ShapeDtypeStruct(lr.shape, F32),
                   jax.ShapeDtypeStruct(ldt.shape, F32), jax.ShapeDtypeStruct(br.shape, F32),
                   jax.ShapeDtypeStruct(br.shape, F32)),
    )(lr, li, ldt, br, bi, dar, dai, dbbr, dbbi)


S5_CHUNK = 256
S5_STATES = 512
S5_BLOCKS = 4


def _cpow_rows(ar, ai, count):
    rs, im = [ar], [ai]
    for _ in range(count - 1):
        pr, pi = rs[-1], im[-1]
        rs.append(pr * ar - pi * ai)
        im.append(pr * ai + pi * ar)
    return rs, im


def _scan_in_groups(vr, vi, pr, pi, rm, reverse):
    n, width = vr.shape
    vr = vr.reshape(n // SUBLANES, SUBLANES, width)
    vi = vi.reshape(n // SUBLANES, SUBLANES, width)
    row = rm[0:SUBLANES]
    for k in (1, 2, 4):
        shift = SUBLANES - k if reverse else k
        keep = row < SUBLANES - k if reverse else row >= k
        kr = jnp.where(keep, pr[k - 1], 0.0)
        ki = jnp.where(keep, pi[k - 1], 0.0)
        sr, si = pltpu.roll(vr, shift, 1), pltpu.roll(vi, shift, 1)
        vr, vi = vr + kr * sr - ki * si, vi + kr * si + ki * sr
    return vr.reshape(n, width), vi.reshape(n, width)


def _carry_over_groups(xr_s, xi_s, wr, wi, c0r, c0i, reverse):
    groups = xr_s.shape[0] // SUBLANES
    pick = 0 if reverse else SUBLANES - 1

    def step(q, carry):
        cr, ci = carry
        r = groups - 1 - q if reverse else q
        o = pl.multiple_of(r * SUBLANES, SUBLANES)
        vr = xr_s[pl.ds(o, SUBLANES), :]
        vi = xi_s[pl.ds(o, SUBLANES), :]
        nr = vr + wr * cr - wi * ci
        ni = vi + wr * ci + wi * cr
        xr_s[pl.ds(o, SUBLANES), :] = nr
        xi_s[pl.ds(o, SUBLANES), :] = ni
        return (jnp.broadcast_to(nr[pick:pick + 1], nr.shape), jnp.broadcast_to(ni[pick:pick + 1], ni.shape))

    return lax.fori_loop(0, groups, step, (c0r, c0i), unroll=4)


def _s5_state_scan(u_b, bbr, bbi, pr, pi, rm, xr_s, xi_s, c0r, c0i):
    bur = _dot(u_b, bbr)
    bui = _dot(u_b, bbi)
    bur, bui = _scan_in_groups(bur, bui, pr, pi, rm, False)
    xr_s[...] = bur
    xi_s[...] = bui
    w8r = jnp.concatenate(pr, axis=0)
    w8i = jnp.concatenate(pi, axis=0)
    return _carry_over_groups(xr_s, xi_s, w8r, w8i, c0r, c0i, False)


def _s5_fwd(u, a_re, a_im, bbr, bbi, cr, ci, d_skip, n_seq):
    n = u.shape[0]
    seq_len = n // n_seq
    nt = seq_len // S5_CHUNK
    tc = S5_CHUNK

    def body(u_ref, ar_ref, ai_ref, bbr_ref, bbi_ref, cr_ref, ci_ref, d_ref, y_ref, str_ref, sti_ref,
             xr_s, xi_s, car_r, car_i):
        t = pl.program_id(2)

        @pl.when(t == 0)
        def _():
            car_r[...] = jnp.zeros_like(car_r)
            car_i[...] = jnp.zeros_like(car_i)
        pr, pi = _cpow_rows(ar_ref[0], ai_ref[0], SUBLANES)
        rm = lax.broadcasted_iota(jnp.int32, (tc, S5_STATES), 0) & (SUBLANES - 1)
        str_ref[0, 0] = car_r[...]
        sti_ref[0, 0] = car_i[...]
        u_t = u_ref[...]
        cfr, cfi = _s5_state_scan(u_t.astype(BF16), bbr_ref[0], bbi_ref[0], pr, pi, rm, xr_s, xi_s,
                                  car_r[...], car_i[...])
        car_r[...] = cfr
        car_i[...] = cfi
        y = _dot(xr_s[...].astype(BF16), cr_ref[0]) - _dot(xi_s[...].astype(BF16), ci_ref[0])
        y_ref[...] = y + d_ref[...] * u_t

    u_spec = pl.BlockSpec((tc, LANES), lambda cb, b, t: (b * nt + t, cb))
    a_spec = pl.BlockSpec((1, 1, S5_STATES), lambda cb, b, t: (cb, 0, 0))
    bb_spec = pl.BlockSpec((1, LANES, S5_STATES), lambda cb, b, t: (cb, 0, 0))
    c_spec = pl.BlockSpec((1, S5_STATES, LANES), lambda cb, b, t: (cb, 0, 0))
    st_spec = pl.BlockSpec((1, 1, SUBLANES, S5_STATES), lambda cb, b, t: (cb, b * nt + t, 0, 0))
    st_shape = jax.ShapeDtypeStruct((S5_BLOCKS, n_seq * nt, SUBLANES, S5_STATES), F32)
    return pl.pallas_call(
        body, name="s5_fwd", grid=(S5_BLOCKS, n_seq, nt),
        in_specs=[u_spec, a_spec, a_spec, bb_spec, bb_spec, c_spec, c_spec,
                  pl.BlockSpec((1, LANES), lambda cb, b, t: (0, cb))],
        out_specs=(u_spec, st_spec, st_spec),
        out_shape=(jax.ShapeDtypeStruct((n, D_SSM), F32), st_shape, st_shape),
        scratch_shapes=[pltpu.VMEM((tc, S5_STATES), F32), pltpu.VMEM((tc, S5_STATES), F32),
                        pltpu.VMEM((SUBLANES, S5_STATES), F32), pltpu.VMEM((SUBLANES, S5_STATES), F32)],
        compiler_params=_cparams("parallel", "arbitrary", "arbitrary"),
    )(u, a_re, a_im, bbr, bbi, cr, ci, d_skip)


def _s5_bwd(u, dy, st_r, st_i, a_re, a_im, bbr, bbi, cr, ci, d_skip, n_seq):
    n = u.shape[0]
    seq_len = n // n_seq
    nt = seq_len // S5_CHUNK
    tc = S5_CHUNK

    def body(u_ref, dy_ref, str_ref, sti_ref, ar_ref, ai_ref, bbr_ref, bbi_ref, cr_ref, ci_ref, d_ref,
             du_ref, dbbr_ref, dbbi_ref, dcr_ref, dci_ref, dar_ref, dai_ref, dd_ref,
             xr_s, xi_s, gr_s, gi_s, car_r, car_i):
        b, t = pl.program_id(1), pl.program_id(2)

        @pl.when((b == 0) & (t == 0))
        def _():
            for ref in (dbbr_ref, dbbi_ref, dcr_ref, dci_ref, dar_ref, dai_ref, dd_ref):
                ref[...] = jnp.zeros_like(ref)

        @pl.when(t == 0)
        def _():
            car_r[...] = jnp.zeros_like(car_r)
            car_i[...] = jnp.zeros_like(car_i)
        ar, ai = ar_ref[0], ai_ref[0]
        pr, pi = _cpow_rows(ar, ai, SUBLANES)
        row = lax.broadcasted_iota(jnp.int32, (tc, S5_STATES), 0)
        rm = row & (SUBLANES - 1)
        u_t = u_ref[...]
        u_b = u_t.astype(BF16)
        dy_t = dy_ref[...]
        dy_b = dy_t.astype(BF16)
        s0r, s0i = str_ref[0, 0], sti_ref[0, 0]
        _s5_state_scan(u_b, bbr_ref[0], bbi_ref[0], pr, pi, rm, xr_s, xi_s, s0r, s0i)
        xr, xi = xr_s[...], xi_s[...]
        gr = _dot_nt(dy_b, cr_ref[0])
        gi = -_dot_nt(dy_b, ci_ref[0])
        npi = [-v for v in pi]
        gr, gi = _scan_in_groups(gr, gi, pr, npi, rm, True)
        gr_s[...] = gr
        gi_s[...] = gi
        w8r = jnp.concatenate(pr[::-1], axis=0)
        w8i = jnp.concatenate(npi[::-1], axis=0)
        cfr, cfi = _carry_over_groups(gr_s, gi_s, w8r, w8i, car_r[...], car_i[...], True)
        car_r[...] = cfr
        car_i[...] = cfi
        gr, gi = gr_s[...], gi_s[...]
        gr_b, gi_b = gr.astype(BF16), gi.astype(BF16)
        du_ref[...] = _dot_nt(gr_b, bbr_ref[0]) + _dot_nt(gi_b, bbi_ref[0]) + d_ref[...] * dy_t
        dbbr_ref[0] += _dot_tn(u_b, gr_b)
        dbbi_ref[0] += _dot_tn(u_b, gi_b)
        dcr_ref[0] += _dot_tn(xr.astype(BF16), dy_b)
        dci_ref[0] -= _dot_tn(xi.astype(BF16), dy_b)
        dd_ref[0] += jnp.sum((dy_t * u_t).reshape(tc // SUBLANES, SUBLANES, LANES), axis=0)
        first = row == 0
        xpr = jnp.where(first, jnp.broadcast_to(s0r[0:1], xr.shape), pltpu.roll(xr, 1, 0))
        xpi = jnp.where(first, jnp.broadcast_to(s0i[0:1], xi.shape), pltpu.roll(xi, 1, 0))
        shp = (tc // SUBLANES, SUBLANES, S5_STATES)
        dar_ref[0] += jnp.sum((gr * xpr + gi * xpi).reshape(shp), axis=0)
        dai_ref[0] += jnp.sum((gi * xpr - gr * xpi).reshape(shp), axis=0)

    u_spec = pl.BlockSpec((tc, LANES), lambda cb, b, t: (b * nt + nt - 1 - t, cb))
    a_spec = pl.BlockSpec((1, 1, S5_STATES), lambda cb, b, t: (cb, 0, 0))
    bb_spec = pl.BlockSpec((1, LANES, S5_STATES), lambda cb, b, t: (cb, 0, 0))
    c_spec = pl.BlockSpec((1, S5_STATES, LANES), lambda cb, b, t: (cb, 0, 0))
    st_spec = pl.BlockSpec((1, 1, SUBLANES, S5_STATES), lambda cb, b, t: (cb, b * nt + nt - 1 - t, 0, 0))
    da_spec = pl.BlockSpec((1, SUBLANES, S5_STATES), lambda cb, b, t: (cb, 0, 0))
    dd_spec = pl.BlockSpec((1, SUBLANES, LANES), lambda cb, b, t: (cb, 0, 0))
    big = pltpu.VMEM((tc, S5_STATES), F32)
    small = pltpu.VMEM((SUBLANES, S5_STATES), F32)
    return pl.pallas_call(
        body, name="s5_bwd", grid=(S5_BLOCKS, n_seq, nt),
        in_specs=[u_spec, u_spec, st_spec, st_spec, a_spec, a_spec, bb_spec, bb_spec, c_spec, c_spec,
                  pl.BlockSpec((1, LANES), lambda cb, b, t: (0, cb))],
        out_specs=(u_spec, bb_spec, bb_spec, c_spec, c_spec, da_spec, da_spec, dd_spec),
        out_shape=(jax.ShapeDtypeStruct((n, D_SSM), F32),
                   jax.ShapeDtypeStruct((S5_BLOCKS, LANES, S5_STATES), F32),
                   jax.ShapeDtypeStruct((S5_BLOCKS, LANES, S5_STATES), F32),
                   jax.ShapeDtypeStruct((S5_BLOCKS, S5_STATES, LANES), F32),
                   jax.ShapeDtypeStruct((S5_BLOCKS, S5_STATES, LANES), F32),
                   jax.ShapeDtypeStruct((S5_BLOCKS, SUBLANES, S5_STATES), F32),
                   jax.ShapeDtypeStruct((S5_BLOCKS, SUBLANES, S5_STATES), F32),
                   jax.ShapeDtypeStruct((S5_BLOCKS, SUBLANES, LANES), F32)),
        scratch_shapes=[big, big, big, big, small, small],
        compiler_params=_cparams("parallel", "arbitrary", "arbitrary"),
    )(u, dy, st_r, st_i, a_re, a_im, bbr, bbi, cr, ci, d_skip)


CUM_BLOCK = 128


def _tri(lower):
    r = lax.broadcasted_iota(jnp.int32, (CUM_BLOCK, CUM_BLOCK), 0)
    c = lax.broadcasted_iota(jnp.int32, (CUM_BLOCK, CUM_BLOCK), 1)
    return jnp.where(r >= c if lower else r <= c, 1.0, 0.0).astype(F32)


def _fprep_fwd(fl, bf, n_seq):
    n = fl.shape[0]
    seq_len = n // n_seq
    nb = seq_len // CUM_BLOCK

    def body(fl_ref, bf_ref, cum_ref):
        tril = _tri(True)
        carry = jnp.zeros((1, LANES), F32)
        for blk in range(nb):
            rows = slice(blk * CUM_BLOCK, (blk + 1) * CUM_BLOCK)
            lf = jax.nn.log_sigmoid(fl_ref[rows, :] + bf_ref[...])
            cs = jnp.dot(tril, lf, preferred_element_type=F32, precision=HIGHEST) + carry
            cum_ref[rows, :] = cs
            carry = cs[CUM_BLOCK - 1:CUM_BLOCK, :]

    spec = pl.BlockSpec((seq_len, LANES), lambda b: (b, 0))
    return pl.pallas_call(
        body, name="fprep_fwd", grid=(n_seq,), in_specs=[spec, pl.BlockSpec((1, LANES), lambda b: (0, 0))],
        out_specs=spec, out_shape=jax.ShapeDtypeStruct((n, LANES), F32), compiler_params=_cparams("parallel"),
    )(fl, bf)


def _fprep_bwd(dcum, fl, bf, n_seq):
    n = fl.shape[0]
    seq_len = n // n_seq
    nb = seq_len // CUM_BLOCK

    def body(dcum_ref, fl_ref, bf_ref, dfl_ref, dbf_ref):
        triu = _tri(False)
        lane = lax.broadcasted_iota(jnp.int32, (CUM_BLOCK, LANES), 1)
        carry = jnp.zeros((1, LANES), F32)
        total = jnp.zeros((1, LANES), F32)
        for blk in reversed(range(nb)):
            rows = slice(blk * CUM_BLOCK, (blk + 1) * CUM_BLOCK)
            rs = jnp.dot(triu, dcum_ref[rows, :], preferred_element_type=F32, precision=HIGHEST) + carry
            carry = rs[0:1, :]
            _, vjp = jax.vjp(jax.nn.log_sigmoid, fl_ref[rows, :] + bf_ref[...])
            dz = jnp.where(lane < N_HEADS, vjp(rs)[0], 0.0)
            dfl_ref[rows, :] = dz
            total = total + jnp.sum(dz, axis=0, keepdims=True)
        dbf_ref[0] = total

    spec = pl.BlockSpec((seq_len, LANES), lambda b: (b, 0))
    return pl.pallas_call(
        body, name="fprep_bwd", grid=(n_seq,), in_specs=[spec, spec, pl.BlockSpec((1, LANES), lambda b: (0, 0))],
        out_specs=(spec, pl.BlockSpec((1, 1, LANES), lambda b: (b, 0, 0))),
        out_shape=(jax.ShapeDtypeStruct((n, LANES), F32), jax.ShapeDtypeStruct((n_seq, 1, LANES), F32)),
        compiler_params=_cparams("parallel"),
    )(dcum, fl, bf)


ATT_TQ = 256
ATT_KSTEP = 256
ATT_SCALE = HEAD_DIM ** -0.5
NEG_BIG = -1e30


assert ATT_KSTEP == ATT_TQ


def _scores(q_scaled, kb, row_bias, ck, kend):
    s = _dot_nt(q_scaled, kb) - ck
    if row_bias is not None:
        s = s + row_bias
    r = lax.broadcasted_iota(jnp.int32, (ATT_TQ, ATT_TQ), 0)
    c = lax.broadcasted_iota(jnp.int32, (ATT_TQ, ATT_TQ), 1)
    diag = jnp.where(r >= c, s[:, kend - ATT_TQ:], NEG_BIG)
    return diag if kend == ATT_TQ else jnp.concatenate([s[:, :kend - ATT_TQ], diag], axis=1)


def _attn_specs(n_seq, seq_len):
    nq = seq_len // ATT_TQ
    q_spec = pl.BlockSpec((ATT_TQ, LANES), lambda b, h, q: (b * nq + q, h))
    k_spec = pl.BlockSpec((seq_len, LANES), lambda b, h, q: (b, N_HEADS // 2 + h))
    v_spec = pl.BlockSpec((seq_len, LANES), lambda b, h, q: (b, N_HEADS + h))
    cq_spec = pl.BlockSpec((1, 2, ATT_TQ, 1), lambda b, h, q: (b, h, q, 0))
    ck_spec = pl.BlockSpec((1, 2, 1, seq_len), lambda b, h, q: (b, h, 0, 0))
    return nq, q_spec, k_spec, v_spec, cq_spec, ck_spec


def _own_cum(cum_ref, e):
    lane = lax.broadcasted_iota(jnp.int32, (1, LANES), 1)
    return jnp.sum(jnp.where(lane == 2 * pl.program_id(1) + e, cum_ref[...], 0.0), axis=1, keepdims=True)


def _head_selectors():
    head0 = lax.broadcasted_iota(jnp.int32, (1, LANES), 1) < HEAD_DIM
    return head0, (head0, jnp.logical_not(head0))


def _for_key_range(qi, seq_len, run):
    per = ATT_KSTEP // ATT_TQ
    for g in range(seq_len // ATT_KSTEP):
        pl.when(qi // per == g)(functools.partial(run, (g + 1) * ATT_KSTEP))


def _attn_fwd(qkv, cum, ck, n_seq):
    n = qkv.shape[0]
    seq_len = n // n_seq
    nq, q_spec, k_spec, v_spec, cq_spec, ck_spec = _attn_specs(n_seq, seq_len)
    cum_spec = pl.BlockSpec((ATT_TQ, LANES), lambda b, h, q: (b * nq + q, 0))

    def body(q_ref, k_ref, v_ref, cum_ref, ck_ref, o_ref, lse_ref):
        qi = pl.program_id(2)
        q2 = q_ref[...]
        head0, sels = _head_selectors()
        qe = [jnp.where(sel, q2 * ATT_SCALE, 0.0).astype(BF16) for sel in sels]

        def run(kend):
            kb = k_ref[0:kend, :].astype(BF16)
            vb = v_ref[0:kend, :].astype(BF16)
            outs = []
            for e in range(2):
                s = _scores(qe[e], kb, None, ck_ref[0, e, :, 0:kend], kend)
                mx = jnp.max(s, axis=1, keepdims=True)
                p = jnp.exp(s - mx)
                den = jnp.sum(p, axis=1, keepdims=True)
                outs.append(_dot(p.astype(BF16), vb) / den)
                lse_ref[0, e] = _own_cum(cum_ref, e) + mx + jnp.log(den)
            o_ref[...] = jnp.where(head0, outs[0], outs[1])

        _for_key_range(qi, seq_len, run)

    return pl.pallas_call(
        body, name="attn_fwd", grid=(n_seq, N_HEADS // 2, nq),
        in_specs=[q_spec, k_spec, v_spec, cum_spec, ck_spec],
        out_specs=(q_spec, cq_spec),
        out_shape=(jax.ShapeDtypeStruct((n, D_ATTN), F32), jax.ShapeDtypeStruct((n_seq, N_HEADS, seq_len, 1), F32)),
        compiler_params=_cparams("parallel", "parallel", "parallel"),
    )(qkv, qkv, qkv, cum, ck)


def _attn_bwd(qkv, cum, ck, o, do, lse, n_seq):
    n = qkv.shape[0]
    seq_len = n // n_seq
    nq, q_spec, k_spec, v_spec, cq_spec, ck_spec = _attn_specs(n_seq, seq_len)
    kv_out = pl.BlockSpec((seq_len, LANES), lambda b, h, q: (b, h))
    cum_spec = pl.BlockSpec((ATT_TQ, LANES), lambda b, h, q: (b * nq + q, 0))

    def body(q_ref, k_ref, v_ref, cum_ref, ck_ref, o_ref, do_ref, lse_ref, dq_ref, dk_ref, dv_ref, dcq_ref, dck_ref):
        qi = pl.program_id(2)

        @pl.when(qi == 0)
        def _():
            dk_ref[...] = jnp.zeros_like(dk_ref)
            dv_ref[...] = jnp.zeros_like(dv_ref)
            dck_ref[...] = jnp.zeros_like(dck_ref)
        q2 = q_ref[...]
        do2 = do_ref[...]
        o2 = o_ref[...]
        head0, sels = _head_selectors()
        qe = [jnp.where(sel, q2 * ATT_SCALE, 0.0).astype(BF16) for sel in sels]
        doe = [jnp.where(sel, do2, 0.0) for sel in sels]
        doe_b = [d.astype(BF16) for d in doe]
        delta = [jnp.sum(d * o2, axis=1, keepdims=True) for d in doe]

        def run(kend):
            kb = k_ref[0:kend, :].astype(BF16)
            vb = v_ref[0:kend, :].astype(BF16)
            dqs = []
            dk = jnp.zeros((kend, LANES), F32)
            dv = jnp.zeros((kend, LANES), F32)
            for e in range(2):
                p = jnp.exp(_scores(qe[e], kb, _own_cum(cum_ref, e) - lse_ref[0, e], ck_ref[0, e, :, 0:kend], kend))
                ds = p * (_dot_nt(doe_b[e], vb) - delta[e])
                ds_b = ds.astype(BF16)
                dqs.append(_dot(ds_b, kb))
                dk = dk + _dot_tn(ds_b, qe[e])
                dv = dv + _dot_tn(p.astype(BF16), doe_b[e])
                dcq_ref[0, e] = jnp.sum(ds, axis=1, keepdims=True)
                dck_ref[0, e, :, 0:kend] -= jnp.sum(ds, axis=0, keepdims=True)
            dk_ref[0:kend, :] += dk
            dv_ref[0:kend, :] += dv
            dq_ref[...] = jnp.where(head0, dqs[0], dqs[1]) * ATT_SCALE

        _for_key_range(qi, seq_len, run)

    return pl.pallas_call(
        body, name="attn_bwd", grid=(n_seq, N_HEADS // 2, nq),
        in_specs=[q_spec, k_spec, v_spec, cum_spec, ck_spec, q_spec, q_spec, cq_spec],
        out_specs=(q_spec, kv_out, kv_out, cq_spec, ck_spec),
        out_shape=(jax.ShapeDtypeStruct((n, D_ATTN), F32), jax.ShapeDtypeStruct((n, D_ATTN), F32),
                   jax.ShapeDtypeStruct((n, D_ATTN), F32),
                   jax.ShapeDtypeStruct((n_seq, N_HEADS, seq_len, 1), F32),
                   jax.ShapeDtypeStruct((n_seq, N_HEADS, 1, seq_len), F32)),
        compiler_params=_cparams("parallel", "parallel", "arbitrary"),
    )(qkv, qkv, qkv, cum, ck, o, do, lse)


WEIGHT_NAMES = ("norm_mix", "w_in", "b_forget", "lam_re", "lam_im", "b_re", "b_im", "c_re", "c_im", "d_skip", "log_dt",
                "w_glu", "b_glu", "q_norm", "k_norm", "norm_out_ssm", "norm_out_attn", "w_out", "norm_ffn", "w_up",
                "conv_w", "conv_b", "w_down")
SHARDED = ("w_in", "w_glu", "w_out", "w_up", "conv_w", "w_down")
ADAM_TILE = {"w_in": (257, 256), "w_glu": (64, 512), "w_out": (128, 1024), "w_up": (688, 256), "conv_w": (3, 688),
             "w_down": (344, 1024)}
PACK_ROWS = SUBLANES * LANES


def _after_all(*arrays):
    return sum(a[(0,) * a.ndim].astype(F32) for a in arrays).reshape(1, 1)


def _pad_to(a, axis, size):
    pad = [(0, 0)] * a.ndim
    pad[axis] = (0, size - a.shape[axis])
    return jnp.pad(a, pad)


def _block_diag(t, transpose):
    t4 = t.reshape(S5_BLOCKS, 8, SSM_GROUP, SSM_STATE)
    eye = jnp.eye(8, dtype=t.dtype)
    if transpose:
        e = jnp.swapaxes(t4, 2, 3)[:, :, :, None, :] * eye[None, :, None, :, None]
        return e.reshape(S5_BLOCKS, S5_STATES, LANES)
    e = t4[:, :, :, None, :] * eye[None, :, None, :, None]
    return e.reshape(S5_BLOCKS, LANES, S5_STATES)


def _block_diag_extract(m, transpose):
    if transpose:
        m5 = m.reshape(S5_BLOCKS, 8, SSM_STATE, 8, SSM_GROUP)
        d = jnp.stack([m5[:, i, :, i, :] for i in range(8)], axis=1)
        return jnp.swapaxes(d, 2, 3).reshape(N_GROUPS, SSM_GROUP, SSM_STATE)
    m5 = m.reshape(S5_BLOCKS, 8, SSM_GROUP, 8, SSM_STATE)
    d = jnp.stack([m5[:, i, :, i, :] for i in range(8)], axis=1)
    return d.reshape(N_GROUPS, SSM_GROUP, SSM_STATE)


def _pack(pieces):
    flat = jnp.concatenate([p.reshape(-1).astype(F32) for p in pieces])
    size = -(-flat.shape[0] // PACK_ROWS) * PACK_ROWS
    return _pad_to(flat, 0, size).reshape(-1, LANES)


def _unpack(packed, shapes):
    flat = packed.reshape(-1)
    out, off = [], 0
    for shp in shapes:
        size = math.prod(shp)
        out.append(flat[off:off + size].reshape(shp))
        off += size
    return out


def kernel(x, norm_mix, w_in, b_forget, lam_re, lam_im, b_re, b_im, c_re, c_im, d_skip, log_dt, w_glu, b_glu, q_norm, k_norm, norm_out_ssm, norm_out_attn, w_out, norm_ffn, w_up, conv_w, conv_b, w_down, loss_target, m_norm_mix, m_w_in, m_b_forget, m_lam_re, m_lam_im, m_b_re, m_b_im, m_c_re, m_c_im, m_d_skip, m_log_dt, m_w_glu, m_b_glu, m_q_norm, m_k_norm, m_norm_out_ssm, m_norm_out_attn, m_w_out, m_norm_ffn, m_w_up, m_conv_w, m_conv_b, m_w_down, v_norm_mix, v_w_in, v_b_forget, v_lam_re, v_lam_im, v_b_re, v_b_im, v_c_re, v_c_im, v_d_skip, v_log_dt, v_w_glu, v_b_glu, v_q_norm, v_k_norm, v_norm_out_ssm, v_norm_out_attn, v_w_out, v_norm_ffn, v_w_up, v_conv_w, v_conv_b, v_w_down):
    given = dict(locals())
    weights = {k: given[k] for k in WEIGHT_NAMES}
    mom1 = {k: given["m_" + k] for k in WEIGHT_NAMES}
    mom2 = {k: given["v_" + k] for k in WEIGHT_NAMES}
    n_seq, seq_len, _ = x.shape
    n = n_seq * seq_len
    xf = x.reshape(n, D_MODEL)
    target = loss_target.reshape(n, D_MODEL)
    me_idx = 4 * lax.axis_index("x") + 2 * lax.axis_index("y") + lax.axis_index("c")

    in_flags = [False] * 2
    in_sems = _exchange_start([jnp.swapaxes(w_in[0], 0, 1).astype(BF16), conv_w[0]], in_flags, norm_mix,
                              "gather_in_start", 3)
    fill_own = lambda got, mine: lax.dynamic_update_index_in_dim(got, mine, me_idx, 0)

    lr3 = lam_re[0].reshape(N_GROUPS, 1, SSM_STATE)
    li3 = lam_im[0].reshape(N_GROUPS, 1, SSM_STATE)
    ldt3 = log_dt[0].reshape(N_GROUPS, 1, 1)
    br_t = jnp.swapaxes(b_re[0], 1, 2)
    bi_t = jnp.swapaxes(b_im[0], 1, 2)
    ab_re, ab_im, bb_re, bb_im = _s5_param_fwd(lr3, li3, ldt3, br_t, bi_t)
    a_re = ab_re.reshape(S5_BLOCKS, 1, S5_STATES)
    a_im = ab_im.reshape(S5_BLOCKS, 1, S5_STATES)
    bbr = _block_diag(bb_re, False).astype(BF16)
    bbi = _block_diag(bb_im, False).astype(BF16)
    cr = _block_diag(c_re[0], True).astype(BF16)
    ci = _block_diag(c_im[0], True).astype(BF16)
    avg = jnp.kron(jnp.eye(N_HEADS, dtype=F32), jnp.full((HEAD_DIM, HEAD_DIM), 1.0 / HEAD_DIM, F32)).astype(BF16)
    qg = jnp.tile(q_norm, (1, N_HEADS))
    kg = jnp.tile(k_norm, (1, N_HEADS))
    row_shards = [w_down[0].astype(BF16), w_out[0].astype(BF16), w_glu[0].astype(BF16)]

    (own_in, own_cw), (g_in, g_cw) = _exchange_wait(in_sems[0], in_sems[1], in_sems[2], in_sems[3], in_flags,
                                                    _after_all(a_re, a_im, bbr, bbi, cr, ci, avg, qg, kg, *row_shards),
                                                    "gather_in_wait")
    g_in = fill_own(g_in, own_in)
    g_cw = fill_own(g_cw, own_cw)
    row_flags = [False] * 3
    r_sems = _exchange_start(row_shards, row_flags, g_in, "gather_rows_start", 0)
    u_sems = _exchange_start([jnp.swapaxes(w_up[0], 0, 1).astype(BF16)], [False], r_sems[4], "gather_up_start", 5)
    norm_mix = norm_mix + u_sems[4][0, 0]
    w_in_p = _pad_to(g_in.reshape(D_IN, D_MODEL), 0, D_IN_PAD)

    hn, u, qkv, raw, fl = _inproj_fwd(xf, norm_mix, w_in_p, avg, qg, kg)
    yc, st_r, st_i = _s5_fwd(u, a_re, a_im, bbr, bbi, cr, ci, d_skip, n_seq)
    bf = _pad_to(b_forget, 1, LANES)
    cum = _fprep_fwd(fl, bf, n_seq)
    cum8 = jnp.swapaxes(cum[:, :N_HEADS].reshape(n_seq, seq_len, N_HEADS), 1, 2)
    ck = cum8[:, :, None, :]
    ya, lse = _attn_fwd(qkv, cum, ck, n_seq)
    own_rows, got_rows = _exchange_wait(r_sems[0], r_sems[1], r_sems[2], r_sems[3], row_flags, ya, "gather_rows_wait")
    g_down, g_out, g_glu = [fill_own(g, o) for o, g in zip(own_rows, got_rows)]
    w_glu_f = g_glu.reshape(D_SSM, D_SSM)
    w_out_f = g_out.reshape(D_MODEL, D_MODEL)
    conv_st = _pad_to(jnp.concatenate([g_cw, conv_b.reshape(N_DEV, 1, -1)], axis=1), 1, SUBLANES)
    w_down4 = g_down.reshape(FFN_GROUPS, FFN_GROUP, D_MODEL)
    ys = _glu_fwd(yc, w_glu_f, b_glu)
    h1, hn2, mixed = _mix_fwd(xf, ys, ya, norm_out_ssm, norm_out_attn, w_out_f, norm_ffn)
    (own_up,), (g_up,) = _exchange_wait(u_sems[0], u_sems[1], u_sems[2], u_sems[3], [False], _after_all(h1, hn2),
                                        "gather_up_wait")
    g_up = fill_own(g_up, own_up)
    ug, uv, pg, pv, dy, loss_part = _ffn_fwd(hn2, h1, target, g_up, conv_st, w_down4, seq_len)
    loss_local = 0.5 * jnp.sum(loss_part) / D_MODEL

    dug, duv, act, dhn2, dcg, dcv = _ffn_bwd(dy, ug, uv, pg, pv, g_up, conv_st, w_down4, seq_len)
    dh1, dys, dya, d_gs, d_ga, d_gf = _mix_bwd(dy, dhn2[None], h1, ys, ya, norm_out_ssm, norm_out_attn, w_out_f, norm_ffn)
    dyc, gl_b, dz_b, d_bglu = _glu_bwd(yc, dys, w_glu_f, b_glu)

    gw_glu = _tn_matmul(gl_b, dz_b, "dw_glu", D_SSM, D_SSM, out_dtype=BF16)
    gw_out = _tn_matmul(mixed, dh1, "dw_out", D_MODEL, D_MODEL, out_dtype=BF16)
    gw_up = jnp.concatenate([_tn_grouped(dug, hn2, "dw_up_gate", False, BF16),
                             _tn_grouped(duv, hn2, "dw_up_val", False, BF16)], axis=0)
    gw_down = _tn_grouped(act, dy, "dw_down", False, BF16)
    g_conv = jnp.concatenate([dcg, dcv], axis=0)
    by_cols = lambda g, c: jnp.swapaxes(g.reshape(g.shape[0], N_DEV, c), 0, 1)
    early_flags = [True] * 4
    early_names = ("w_down", "w_out", "w_glu", "w_up")
    g_sems = _exchange_start(
        [gw_down.reshape(N_DEV, -1, D_MODEL), gw_out.reshape(N_DEV, -1, D_MODEL), gw_glu.reshape(N_DEV, -1, D_SSM), gw_up],
        early_flags, dyc, "grad_early_start", 1)
    started = g_sems[4][0, 0]

    du, dbbr, dbbi, dcr, dci, dar, dai, ddk = _s5_bwd(u, dyc, st_r, st_i, a_re, a_im, bbr, bbi, cr, ci,
                                                      d_skip + started, n_seq)
    partial_early = {
        "ab_re": jnp.sum(dar, axis=1), "ab_im": jnp.sum(dai, axis=1),
        "bb_re": _block_diag_extract(dbbr, False), "bb_im": _block_diag_extract(dbbi, False),
        "c_re": _block_diag_extract(dcr, True), "c_im": _block_diag_extract(dci, True),
        "d_skip": jnp.sum(ddk, axis=1), "b_glu": d_bglu,
        "norm_out_ssm": d_gs, "norm_out_attn": d_ga, "norm_ffn": d_gf, "conv_b": g_conv[:, 3],
    }
    early_keys = tuple(partial_early)
    early_shapes = [partial_early[k].shape for k in early_keys]
    p_sems = _exchange_start([_pack([partial_early[k] for k in early_keys])], [False], du, "small_early_start", 2)
    started = started + p_sems[4][0, 0]

    dqn, dkn, dv, dcq, dck = _attn_bwd(qkv, cum, ck + started, ya, dya, lse, n_seq)
    dcum8 = dcq[:, :, :, 0] + dck.reshape(n_seq, N_HEADS, seq_len)
    dcum = _pad_to(jnp.swapaxes(dcum8, 1, 2).reshape(n, N_HEADS), 1, LANES)
    dfl, dbf = _fprep_bwd(dcum, fl, bf, n_seq)
    dx, dproj, d_gmix, d_qg, d_kg = _inproj_bwd(xf, norm_mix, w_in_p, avg, qg, kg, raw, du, dqn, dkn, dv, dfl, dh1)

    gw_in = _tn_matmul(dproj, hn, "dw_in", D_IN_PAD, D_MODEL, out_rows=D_IN, out_dtype=BF16)
    partial_late = {
        "norm_mix": d_gmix, "b_forget": jnp.sum(dbf, axis=(0, 1))[:N_HEADS],
        "q_norm": jnp.sum(d_qg.reshape(N_HEADS, HEAD_DIM), axis=0),
        "k_norm": jnp.sum(d_kg.reshape(N_HEADS, HEAD_DIM), axis=0), "loss": loss_local.reshape(1),
    }
    late_keys = tuple(partial_late)
    late_shapes = [partial_late[k].shape for k in late_keys]

    late_flags = [True, True, False]
    l_sems = _exchange_start(
        [gw_in.reshape(N_DEV, D_IN // N_DEV, D_MODEL).astype(BF16), g_conv[:, :3],
         _pack([partial_late[k] for k in late_keys])],
        late_flags, dx, "grad_late_start", 4)
    early_src, early_land = _exchange_wait(g_sems[0], g_sems[1], g_sems[2], g_sems[3], early_flags, l_sems[4],
                                           "grad_early_wait")
    land = dict(zip(early_names, early_land))
    land_up = land["w_up"]
    own = {k: lax.dynamic_index_in_dim(s, me_idx, 0, keepdims=False) for k, s in zip(early_names, early_src)}
    grads, deltas, new_m, new_v = {}, {}, {}, {}

    def adam_shard(name):
        flip = (lambda a: jnp.swapaxes(a, 1, 2)) if name in ("w_in", "w_up") else (lambda a: a)
        outs = _adam_sharded(land[name], own[name], flip(weights[name]), flip(mom1[name]), flip(mom2[name]),
                             "adam_" + name, ADAM_TILE[name])
        grads[name], deltas[name], new_m[name], new_v[name] = [flip(o) for o in outs]

    for name in ("w_up", "w_down", "w_out", "w_glu"):
        adam_shard(name)
    (own_pack,), (early_parts,) = _exchange_wait(p_sems[0], p_sems[1], p_sems[2], p_sems[3], [False], land_up,
                                                 "small_early_wait")
    early_sum = _sum_partials(early_parts, own_pack, "sum_early_partials")
    (src_in, src_cw, own_late), (land["w_in"], land["conv_w"], late_parts) = _exchange_wait(
        l_sems[0], l_sems[1], l_sems[2], l_sems[3], late_flags,
        _after_all(early_sum, *[new_v[k] for k in ("w_up", "w_down", "w_out", "w_glu")]), "grad_late_wait")
    own["w_in"] = lax.dynamic_index_in_dim(src_in, me_idx, 0, keepdims=False)
    own["conv_w"] = lax.dynamic_index_in_dim(src_cw, me_idx, 0, keepdims=False)
    for name in ("w_in", "conv_w"):
        adam_shard(name)

    summed = dict(zip(late_keys, _unpack(_sum_partials(late_parts, own_late, "sum_late_partials"), late_shapes)))
    summed.update(zip(early_keys, _unpack(early_sum, early_shapes)))
    dlr, dli, dldt, dbr_t, dbi_t = _s5_param_bwd(
        lr3, li3, ldt3, br_t, bi_t, summed["ab_re"].reshape(lr3.shape), summed["ab_im"].reshape(lr3.shape),
        summed["bb_re"], summed["bb_im"])
    small_grads = {
        "norm_mix": summed["norm_mix"], "b_forget": summed["b_forget"], "lam_re": dlr, "lam_im": dli,
        "b_re": dbr_t, "b_im": dbi_t, "c_re": summed["c_re"], "c_im": summed["c_im"],
        "d_skip": summed["d_skip"], "log_dt": dldt, "b_glu": summed["b_glu"], "q_norm": summed["q_norm"],
        "k_norm": summed["k_norm"], "norm_out_ssm": summed["norm_out_ssm"], "norm_out_attn": summed["norm_out_attn"],
        "norm_ffn": summed["norm_ffn"], "conv_b": summed["conv_b"],
    }
    repl = tuple(k for k in WEIGHT_NAMES if k not in SHARDED)
    turn = lambda k, a: jnp.swapaxes(a, 2, 3) if k in ("b_re", "b_im") else a
    w_list = [turn(k, weights[k]) for k in repl]
    g_list = [small_grads[k].reshape(w.shape) for k, w in zip(repl, w_list)]
    d_list, m_list, v_list = _adam_replicated(g_list, w_list, [turn(k, mom1[k]) for k in repl],
                                              [turn(k, mom2[k]) for k in repl], "adam_replicated")
    for k, g, d, nm, nv in zip(repl, g_list, d_list, m_list, v_list):
        grads[k], deltas[k], new_m[k], new_v[k] = turn(k, g), turn(k, d), turn(k, nm), turn(k, nv)

    grad_x = dx.reshape(x.shape)
    loss = summed["loss"].reshape(())
    return (loss, grad_x, *[grads[k] for k in WEIGHT_NAMES], *[deltas[k] for k in WEIGHT_NAMES],
            *[new_m[k] for k in WEIGHT_NAMES], *[new_v[k] for k in WEIGHT_NAMES])
```

```python
import functools
import math

import jax
import jax.numpy as jnp
from jax import lax
from jax.experimental import pallas as pl
from jax.experimental.pallas import tpu as pltpu

F32 = jnp.float32
BF16 = jnp.bfloat16
HIGHEST = lax.Precision.HIGHEST

N_DEV = 8
D_MODEL = 1024
D_SSM = 512
D_ATTN = 512
N_HEADS = 8
HEAD_DIM = 64
N_GROUPS = 32
SSM_GROUP = 16
SSM_STATE = 64
D_FF = 2752
D_FF_PAD = 2816
D_IN = 2056
D_IN_PAD = 2176
EPS = 1e-6
LANES = 128
SUBLANES = 8
VMEM_LIMIT = 56 * 1024 * 1024

ADAM_LR = 0.001
ADAM_B1 = 0.9
ADAM_B2 = 0.999
ADAM_EPS = 1e-08
ADAM_WD = 0.01
ADAM_STEP = 10


def _cparams(*sem):
    return pltpu.CompilerParams(dimension_semantics=sem, vmem_limit_bytes=VMEM_LIMIT)


def _dot(a, b, **kw):
    return jnp.dot(a, b, preferred_element_type=F32, **kw)


def _dot_nt(a, b):
    return lax.dot_general(a, b, (((1,), (1,)), ((), ())), preferred_element_type=F32)


def _dot_tn(a, b):
    return lax.dot_general(a, b, (((0,), (0,)), ((), ())), preferred_element_type=F32)


def _rms(x, g):
    return x * lax.rsqrt(jnp.mean(x * x, axis=-1, keepdims=True) + EPS) * g


def _split_dot(x, avg):
    hi = x.astype(BF16)
    lo = (x - hi.astype(F32)).astype(BF16)
    return _dot(hi, avg) + _dot(lo, avg)


@jax.custom_vjp
def _group_mean(x, avg):
    return _split_dot(x, avg)


def _group_mean_fwd(x, avg):
    return _split_dot(x, avg), avg


def _group_mean_bwd(avg, ct):
    return _split_dot(ct, avg), jnp.zeros_like(avg)


_group_mean.defvjp(_group_mean_fwd, _group_mean_bwd)


def _headnorm(q, avg, g):
    return q * lax.rsqrt(_group_mean(q * q, avg) + EPS) * g


def _exchange(srcs, scatter_flags, name):
    n = len(srcs)
    out_shape = []
    for s, sc in zip(srcs, scatter_flags):
        shp = s.shape if sc else (N_DEV,) + s.shape
        out_shape.append(jax.ShapeDtypeStruct(shp, s.dtype))

    def body(*refs):
        src = refs[:n]
        dst = refs[n:2 * n]
        send_sems, recv_sems, loc_sems = refs[2 * n:]
        x, y, c = lax.axis_index("x"), lax.axis_index("y"), lax.axis_index("c")
        me = 4 * x + 2 * y + c
        peers = []
        for j in range(1, N_DEV):
            px = 1 - x if (j >> 2) & 1 else x
            py = 1 - y if (j >> 1) & 1 else y
            pc = 1 - c if j & 1 else c
            peers.append(((px, py, pc), 4 * px + 2 * py + pc))
        local, sends = [], []
        for k in range(n):
            own = src[k].at[me] if scatter_flags[k] else src[k]
            lc = pltpu.make_async_copy(own, dst[k].at[me], loc_sems.at[k])
            lc.start()
            local.append(lc)
            for j, (pid, pidx) in enumerate(peers):
                s = src[k].at[pidx] if scatter_flags[k] else src[k]
                cp = pltpu.make_async_remote_copy(
                    src_ref=s, dst_ref=dst[k].at[me], send_sem=send_sems.at[k, j], recv_sem=recv_sems.at[k, j],
                    device_id=pid, device_id_type=pl.DeviceIdType.MESH)
                cp.start()
                sends.append(cp)
        for k in range(n):
            for j, (pid, pidx) in enumerate(peers):
                s = src[k].at[pidx] if scatter_flags[k] else src[k]
                pltpu.make_async_remote_copy(
                    src_ref=s, dst_ref=dst[k].at[pidx], send_sem=send_sems.at[k, j], recv_sem=recv_sems.at[k, j],
                    device_id=pid, device_id_type=pl.DeviceIdType.MESH).wait_recv()
        for cp in sends:
            cp.wait_send()
        for lc in local:
            lc.wait()

    any_spec = pl.BlockSpec(memory_space=pl.ANY)
    return pl.pallas_call(
        body, name=name, out_shape=tuple(out_shape),
        in_specs=[any_spec] * n, out_specs=tuple([any_spec] * n),
        scratch_shapes=[pltpu.SemaphoreType.DMA((n, N_DEV - 1)), pltpu.SemaphoreType.DMA((n, N_DEV - 1)),
                        pltpu.SemaphoreType.DMA((n,))],
        compiler_params=pltpu.CompilerParams(has_side_effects=True),
    )(*srcs)


def _peer_list():
    x, y, c = lax.axis_index("x"), lax.axis_index("y"), lax.axis_index("c")
    peers = []
    for j in range(1, N_DEV):
        px = 1 - x if (j >> 2) & 1 else x
        py = 1 - y if (j >> 1) & 1 else y
        pc = 1 - c if j & 1 else c
        peers.append(((px, py, pc), 4 * px + 2 * py + pc))
    return 4 * x + 2 * y + c, peers


def _split_copies(src, land, send_sems, recv_sems, scatter_flags, me, peers, incoming):
    copies = []
    for k in range(len(src)):
        for j, (pid, pidx) in enumerate(peers):
            s = src[k].at[pidx] if scatter_flags[k] else src[k]
            i = k * (N_DEV - 1) + j
            copies.append(pltpu.make_async_remote_copy(
                src_ref=s, dst_ref=land[k].at[pidx if incoming else me], send_sem=send_sems[i],
                recv_sem=recv_sems[i], device_id=pid, device_id_type=pl.DeviceIdType.MESH))
    return copies


def _exchange_start(srcs, scatter_flags, after, name, collective_id):
    n = len(srcs)
    ns = n * (N_DEV - 1)
    hbm = pl.BlockSpec(memory_space=pltpu.HBM)
    sem = pl.BlockSpec(memory_space=pltpu.SEMAPHORE)
    land_shapes = [s.shape if sc else (N_DEV,) + s.shape for s, sc in zip(srcs, scatter_flags)]

    def body(*refs):
        src, land = refs[:n], refs[n:2 * n]
        send_sems = refs[2 * n + 1:2 * n + 1 + ns]
        recv_sems = refs[2 * n + 1 + ns:2 * n + 1 + 2 * ns]
        token = refs[4 * n + 1 + 2 * ns]
        me, peers = _peer_list()
        barrier = pltpu.get_barrier_semaphore()
        for pid, _ in peers:
            pl.semaphore_signal(barrier, inc=1, device_id=pid, device_id_type=pl.DeviceIdType.MESH)
        pl.semaphore_wait(barrier, N_DEV - 1)
        for cp in _split_copies(src, land, send_sems, recv_sems, scatter_flags, me, peers, False):
            cp.start()
        token[...] = jnp.zeros_like(token)

    outs = pl.pallas_call(
        body, name=name,
        out_shape=(*[pltpu.SemaphoreType.DMA(())] * (2 * ns), *[pltpu.HBM(s.shape, s.dtype) for s in srcs],
                   *[pltpu.HBM(shp, s.dtype) for shp, s in zip(land_shapes, srcs)],
                   jax.ShapeDtypeStruct((SUBLANES, LANES), F32)),
        in_specs=[hbm] * (2 * n) + [pl.BlockSpec(memory_space=pl.ANY)],
        out_specs=(*[sem] * (2 * ns), *[hbm] * (2 * n), pl.BlockSpec(memory_space=pltpu.VMEM)),
        input_output_aliases={i: 2 * ns + i for i in range(2 * n)},
        compiler_params=pltpu.CompilerParams(has_side_effects=pltpu.SideEffectType.DATAFLOW_SIDE_EFFECTING,
                                             collective_id=collective_id),
    )(*[pltpu.with_memory_space_constraint(s, pltpu.HBM) for s in srcs],
      *[pltpu.with_memory_space_constraint(lax.empty(shp, s.dtype), pltpu.HBM) for shp, s in zip(land_shapes, srcs)],
      after)
    return (outs[:ns], outs[ns:2 * ns], outs[2 * ns:2 * ns + n], outs[2 * ns + n:2 * ns + 2 * n], outs[2 * ns + 2 * n])


def _exchange_wait(send_sems, recv_sems, srcs, lands, scatter_flags, after, name):
    n = len(srcs)
    ns = n * (N_DEV - 1)
    hbm = pl.BlockSpec(memory_space=pltpu.HBM)
    sem = pl.BlockSpec(memory_space=pltpu.SEMAPHORE)

    def body(*refs):
        src, land = refs[:n], refs[n:2 * n]
        s_sems = refs[2 * n:2 * n + ns]
        r_sems = refs[2 * n + ns:2 * n + 2 * ns]
        me, peers = _peer_list()
        for cp in _split_copies(src, land, s_sems, r_sems, scatter_flags, me, peers, True):
            cp.wait_send()
            cp.wait_recv()

    outs = pl.pallas_call(
        body, name=name,
        out_shape=tuple(pltpu.HBM(a.shape, a.dtype) for a in (*srcs, *lands)),
        in_specs=[hbm] * (2 * n) + [sem] * (2 * ns) + [pl.BlockSpec(memory_space=pl.ANY)],
        out_specs=tuple([hbm] * (2 * n)),
        input_output_aliases={i: i for i in range(2 * n)},
        compiler_params=pltpu.CompilerParams(has_side_effects=pltpu.SideEffectType.DATAFLOW_SIDE_EFFECTING),
    )(*srcs, *lands, *send_sems, *recv_sems, after)
    return outs[:n], outs[n:]


def _tn_matmul(a, b, name, tk, tm, out_rows=None, out_cols=None, out_dtype=F32, tn=512):
    n_tok, k_dim = a.shape
    m_dim = b.shape[1]
    grid = (k_dim // tk, m_dim // tm, n_tok // tn)

    def body(a_ref, b_ref, o_ref, acc):
        k = pl.program_id(2)
        part = _dot_tn(a_ref[...].astype(BF16), b_ref[...].astype(BF16))

        @pl.when(k == 0)
        def _():
            acc[...] = part

        @pl.when(k > 0)
        def _():
            acc[...] += part

        @pl.when(k == grid[2] - 1)
        def _():
            o_ref[...] = acc[...].astype(out_dtype)

    return pl.pallas_call(
        body, name=name, grid=grid,
        in_specs=[pl.BlockSpec((tn, tk), lambda i, j, k: (k, i)), pl.BlockSpec((tn, tm), lambda i, j, k: (k, j))],
        out_specs=pl.BlockSpec((tk, tm), lambda i, j, k: (i, j)),
        out_shape=jax.ShapeDtypeStruct((out_rows or k_dim, out_cols or m_dim), out_dtype),
        scratch_shapes=[pltpu.VMEM((tk, tm), F32)],
        compiler_params=_cparams("parallel", "parallel", "arbitrary"),
    )(a, b)


def _adam_math(g, w, m, v):
    m = ADAM_B1 * m + (1.0 - ADAM_B1) * g
    v = ADAM_B2 * v + (1.0 - ADAM_B2) * (g * g)
    m_hat = m / (1.0 - ADAM_B1 ** ADAM_STEP)
    v_hat = v / (1.0 - ADAM_B2 ** ADAM_STEP)
    delta = -ADAM_LR * (m_hat / (jnp.sqrt(v_hat) + ADAM_EPS) + ADAM_WD * w)
    return delta, m, v


def _adam_sharded(land, own, w, m, v, name, tile):
    _, r, c = w.shape

    def body(*refs):
        l_ref = refs[0]
        own_ref = refs[1] if own is not None else None
        w_ref, m_ref, v_ref, g_ref, d_ref, nm_ref, nv_ref = [ref.at[0] for ref in refs[-7:]]
        if own_ref is not None:
            x, y, z = lax.axis_index("x"), lax.axis_index("y"), lax.axis_index("c")
            me = 4 * x + 2 * y + z
            mine = own_ref[...].astype(F32)
        g = None
        for s in range(N_DEV):
            part = l_ref[s].astype(F32)
            if own_ref is not None:
                part = jnp.where(me == s, mine, part)
            g = part if g is None else g + part
        d, nm, nv = _adam_math(g, w_ref[...], m_ref[...], v_ref[...])
        g_ref[...] = g
        d_ref[...] = d
        nm_ref[...] = nm
        nv_ref[...] = nv

    tr, tc = tile
    spec = pl.BlockSpec((1, tr, tc), lambda i, j: (0, i, j))
    own_specs, own_args = ([pl.BlockSpec((tr, tc), lambda i, j: (i, j))], [own]) if own is not None else ([], [])
    return pl.pallas_call(
        body, name=name, grid=(r // tr, c // tc),
        in_specs=[pl.BlockSpec((N_DEV, tr, tc), lambda i, j: (0, i, j)), *own_specs, spec, spec, spec],
        out_specs=(spec, spec, spec, spec),
        out_shape=tuple(jax.ShapeDtypeStruct((1, r, c), F32) for _ in range(4)),
        compiler_params=_cparams("parallel", "parallel"),
    )(land, *own_args, w, m, v)


def _sum_partials(parts, own, name):
    _, r, c = parts.shape

    def body(*refs):
        p_ref, o_ref = refs[0], refs[-1]
        if own is not None:
            x, y, z = lax.axis_index("x"), lax.axis_index("y"), lax.axis_index("c")
            me = 4 * x + 2 * y + z
            mine = refs[1][...]
        g = None
        for s in range(N_DEV):
            part = p_ref[s]
            if own is not None:
                part = jnp.where(me == s, mine, part)
            g = part if g is None else g + part
        o_ref[...] = g

    args = (parts,) if own is None else (parts, own)
    return pl.pallas_call(body, name=name, out_shape=jax.ShapeDtypeStruct((r, c), F32),
                          compiler_params=pltpu.CompilerParams(vmem_limit_bytes=VMEM_LIMIT))(*args)


def _adam_replicated(gs, ws, ms, vs, name):
    k = len(ws)

    def body(*refs):
        outs = refs[4 * k:]
        for i in range(k):
            d, nm, nv = _adam_math(refs[i][...], refs[k + i][...], refs[2 * k + i][...], refs[3 * k + i][...])
            outs[i][...] = d
            outs[k + i][...] = nm
            outs[2 * k + i][...] = nv

    outs = pl.pallas_call(body, name=name, out_shape=tuple(jax.ShapeDtypeStruct(w.shape, F32) for w in ws) * 3,
                          compiler_params=pltpu.CompilerParams(vmem_limit_bytes=VMEM_LIMIT))(*gs, *ws, *ms, *vs)
    return outs[:k], outs[k:2 * k], outs[2 * k:]


def _inproj_fwd(x, g, w_in, avg, qg, kg, tm=512):
    n = x.shape[0]

    def body(x_ref, g_ref, w_ref, a_ref, qg_ref, kg_ref, hn_ref, u_ref, qkv_ref, raw_ref, fl_ref):
        hn = _rms(x_ref[...], g_ref[...]).astype(BF16)
        hn_ref[...] = hn
        proj = _dot_nt(hn, w_ref[...])
        u_ref[...] = proj[:, 0:512]
        q = proj[:, 512:1024]
        k = proj[:, 1024:1536]
        raw_ref[:, 0:512] = q
        raw_ref[:, 512:1024] = k
        qkv_ref[:, 0:512] = _headnorm(q, a_ref[...], qg_ref[...])
        qkv_ref[:, 512:1024] = _headnorm(k, a_ref[...], kg_ref[...])
        qkv_ref[:, 1024:1536] = proj[:, 1536:2048]
        fl_ref[...] = proj[:, 2048:D_IN_PAD]

    row = lambda w: pl.BlockSpec((tm, w), lambda i: (i, 0))
    full = lambda a: pl.BlockSpec(a.shape, lambda i: (0,) * a.ndim)
    return pl.pallas_call(
        body, name="inproj_fwd", grid=(n // tm,),
        in_specs=[row(D_MODEL), full(g), full(w_in), full(avg), full(qg), full(kg)],
        out_specs=(row(D_MODEL), row(512), row(1536), row(1024), row(LANES)),
        out_shape=(jax.ShapeDtypeStruct((n, D_MODEL), BF16), jax.ShapeDtypeStruct((n, 512), F32),
                   jax.ShapeDtypeStruct((n, 1536), F32), jax.ShapeDtypeStruct((n, 1024), F32),
                   jax.ShapeDtypeStruct((n, LANES), F32)),
        compiler_params=_cparams("parallel"),
    )(x, g, w_in, avg, qg, kg)


def _inproj_bwd(x, g, w_in, avg, qg, kg, raw, du, dqn, dkn, dv, dfl, dres, tm=512):
    n = x.shape[0]

    def body(x_ref, g_ref, w_ref, a_ref, qg_ref, kg_ref, raw_ref, du_ref, dqn_ref, dkn_ref, dv_ref, dfl_ref, dres_ref,
             dx_ref, dproj_ref, dg_ref, dqg_ref, dkg_ref):
        @pl.when(pl.program_id(0) == 0)
        def _():
            dg_ref[...] = jnp.zeros_like(dg_ref)
            dqg_ref[...] = jnp.zeros_like(dqg_ref)
            dkg_ref[...] = jnp.zeros_like(dkg_ref)
        avg_m = a_ref[...]
        _, vjp_q = jax.vjp(lambda q, gg: _headnorm(q, avg_m, gg), raw_ref[:, 0:512], qg_ref[...])
        dq, dqg = vjp_q(dqn_ref[...])
        _, vjp_k = jax.vjp(lambda k, gg: _headnorm(k, avg_m, gg), raw_ref[:, 512:1024], kg_ref[...])
        dk, dkg = vjp_k(dkn_ref[...])
        dproj = jnp.concatenate([du_ref[...], dq, dk, dv_ref[...], dfl_ref[...]], axis=1).astype(BF16)
        dproj_ref[...] = dproj
        dhn = _dot(dproj, w_ref[...])
        _, vjp_x = jax.vjp(_rms, x_ref[...], g_ref[...])
        dxn, dg = vjp_x(dhn)
        dx_ref[...] = dxn + dres_ref[...]
        dg_ref[...] += dg
        dqg_ref[...] += dqg
        dkg_ref[...] += dkg

    row = lambda w: pl.BlockSpec((tm, w), lambda i: (i, 0))
    full = lambda a: pl.BlockSpec(a.shape, lambda i: (0,) * a.ndim)
    vec = lambda w: pl.BlockSpec((1, w), lambda i: (0, 0))
    return pl.pallas_call(
        body, name="inproj_bwd", grid=(n // tm,),
        in_specs=[row(D_MODEL), full(g), full(w_in), full(avg), full(qg), full(kg), row(1024), row(512), row(512),
                  row(512), row(512), row(LANES), row(D_MODEL)],
        out_specs=(row(D_MODEL), row(D_IN_PAD), vec(D_MODEL), vec(512), vec(512)),
        out_shape=(jax.ShapeDtypeStruct((n, D_MODEL), F32), jax.ShapeDtypeStruct((n, D_IN_PAD), BF16),
                   jax.ShapeDtypeStruct((1, D_MODEL), F32), jax.ShapeDtypeStruct((1, 512), F32),
                   jax.ShapeDtypeStruct((1, 512), F32)),
        compiler_params=_cparams("arbitrary"),
    )(x, g, w_in, avg, qg, kg, raw, du, dqn, dkn, dv, dfl, dres)


def _glu_fwd(yc, wg, bg, tm=512):
    n = yc.shape[0]

    def body(yc_ref, w_ref, b_ref, ys_ref):
        gl = jax.nn.gelu(yc_ref[...])
        z = _dot(gl.astype(BF16), w_ref[...]) + b_ref[...]
        ys_ref[...] = gl * jax.nn.sigmoid(z)

    row = pl.BlockSpec((tm, 512), lambda i: (i, 0))
    full = lambda a: pl.BlockSpec(a.shape, lambda i: (0,) * a.ndim)
    return pl.pallas_call(
        body, name="glu_fwd", grid=(n // tm,), in_specs=[row, full(wg), full(bg)], out_specs=row,
        out_shape=jax.ShapeDtypeStruct((n, 512), F32), compiler_params=_cparams("parallel"),
    )(yc, wg, bg)


def _glu_bwd(yc, dys, wg, bg, tm=512):
    n = yc.shape[0]

    def body(yc_ref, dys_ref, w_ref, b_ref, dyc_ref, gl_ref, dz_ref, db_ref):
        @pl.when(pl.program_id(0) == 0)
        def _():
            db_ref[...] = jnp.zeros_like(db_ref)
        gl, vjp_gelu = jax.vjp(jax.nn.gelu, yc_ref[...])
        glb = gl.astype(BF16)
        z = _dot(glb, w_ref[...]) + b_ref[...]
        s = jax.nn.sigmoid(z)
        dys = dys_ref[...]
        dz = dys * gl * s * (1.0 - s)
        dzb = dz.astype(BF16)
        dgl = dys * s + _dot_nt(dzb, w_ref[...])
        dyc_ref[...] = vjp_gelu(dgl)[0]
        gl_ref[...] = glb
        dz_ref[...] = dzb
        db_ref[...] += jnp.sum(dz, axis=0, keepdims=True)

    row = pl.BlockSpec((tm, 512), lambda i: (i, 0))
    full = lambda a: pl.BlockSpec(a.shape, lambda i: (0,) * a.ndim)
    return pl.pallas_call(
        body, name="glu_bwd", grid=(n // tm,), in_specs=[row, row, full(wg), full(bg)],
        out_specs=(row, row, row, pl.BlockSpec((1, 512), lambda i: (0, 0))),
        out_shape=(jax.ShapeDtypeStruct((n, 512), F32), jax.ShapeDtypeStruct((n, 512), BF16),
                   jax.ShapeDtypeStruct((n, 512), BF16), jax.ShapeDtypeStruct((1, 512), F32)),
        compiler_params=_cparams("arbitrary"),
    )(yc, dys, wg, bg)


def _mix_fwd(x, ys, ya, gs, ga, wout, gf, tm=512):
    n = x.shape[0]

    def body(x_ref, ys_ref, ya_ref, gs_ref, ga_ref, w_ref, gf_ref, h1_ref, hn2_ref, mixed_ref):
        mixed = jnp.concatenate([_rms(ys_ref[...], gs_ref[...]), _rms(ya_ref[...], ga_ref[...])], axis=1).astype(BF16)
        mixed_ref[...] = mixed
        h1 = x_ref[...] + _dot(mixed, w_ref[...])
        h1_ref[...] = h1
        hn2_ref[...] = _rms(h1, gf_ref[...]).astype(BF16)

    row = lambda w: pl.BlockSpec((tm, w), lambda i: (i, 0))
    full = lambda a: pl.BlockSpec(a.shape, lambda i: (0,) * a.ndim)
    return pl.pallas_call(
        body, name="mix_fwd", grid=(n // tm,),
        in_specs=[row(D_MODEL), row(512), row(512), full(gs), full(ga), full(wout), full(gf)],
        out_specs=(row(D_MODEL), row(D_MODEL), row(D_MODEL)),
        out_shape=(jax.ShapeDtypeStruct((n, D_MODEL), F32), jax.ShapeDtypeStruct((n, D_MODEL), BF16),
                   jax.ShapeDtypeStruct((n, D_MODEL), BF16)),
        compiler_params=_cparams("parallel"),
    )(x, ys, ya, gs, ga, wout, gf)


def _mix_bwd(dy, dhn2_parts, h1, ys, ya, gs, ga, wout, gf, tm=512):
    n = dy.shape[0]
    n_parts = dhn2_parts.shape[0]

    def body(dy_ref, dp_ref, h1_ref, ys_ref, ya_ref, gs_ref, ga_ref, w_ref, gf_ref,
             dh1_ref, dys_ref, dya_ref, dgs_ref, dga_ref, dgf_ref):
        @pl.when(pl.program_id(0) == 0)
        def _():
            dgs_ref[...] = jnp.zeros_like(dgs_ref)
            dga_ref[...] = jnp.zeros_like(dga_ref)
            dgf_ref[...] = jnp.zeros_like(dgf_ref)
        dhn2 = dp_ref[0]
        for p in range(1, n_parts):
            dhn2 = dhn2 + dp_ref[p]
        _, vjp_f = jax.vjp(_rms, h1_ref[...], gf_ref[...])
        dh1n, dgf = vjp_f(dhn2)
        dh1 = dy_ref[...] + dh1n
        dh1_ref[...] = dh1
        dmixed = _dot_nt(dh1.astype(BF16), w_ref[...])
        _, vjp_s = jax.vjp(_rms, ys_ref[...], gs_ref[...])
        dys, dgs = vjp_s(dmixed[:, 0:512])
        _, vjp_a = jax.vjp(_rms, ya_ref[...], ga_ref[...])
        dya, dga = vjp_a(dmixed[:, 512:1024])
        dys_ref[...] = dys
        dya_ref[...] = dya
        dgs_ref[...] += dgs
        dga_ref[...] += dga
        dgf_ref[...] += dgf

    row = lambda w: pl.BlockSpec((tm, w), lambda i: (i, 0))
    full = lambda a: pl.BlockSpec(a.shape, lambda i: (0,) * a.ndim)
    vec = lambda w: pl.BlockSpec((1, w), lambda i: (0, 0))
    return pl.pallas_call(
        body, name="mix_bwd", grid=(n // tm,),
        in_specs=[row(D_MODEL), pl.BlockSpec((n_parts, tm, D_MODEL), lambda i: (0, i, 0)), row(D_MODEL), row(512),
                  row(512), full(gs), full(ga), full(wout), full(gf)],
        out_specs=(row(D_MODEL), row(512), row(512), vec(512), vec(512), vec(D_MODEL)),
        out_shape=(jax.ShapeDtypeStruct((n, D_MODEL), F32), jax.ShapeDtypeStruct((n, 512), F32),
                   jax.ShapeDtypeStruct((n, 512), F32), jax.ShapeDtypeStruct((1, 512), F32),
                   jax.ShapeDtypeStruct((1, 512), F32), jax.ShapeDtypeStruct((1, D_MODEL), F32)),
        compiler_params=_cparams("arbitrary"),
    )(dy, dhn2_parts, h1, ys, ya, gs, ga, wout, gf)


HALO = 16
FFN_GROUPS = 4
FFN_GROUP = D_FF // FFN_GROUPS


def _conv3(ue, cw):
    return cw[2:3] * ue + cw[1:2] * pltpu.roll(ue, 1, 0) + cw[0:1] * pltpu.roll(ue, 2, 0) + cw[3:4]


def _ffn_weight_specs():
    gate = lambda i, j: (j, 0, 0)
    val = lambda i, j: (j + FFN_GROUPS, 0, 0)
    w_blk, c_blk = (1, FFN_GROUP, D_MODEL), (1, SUBLANES, FFN_GROUP)
    return [pl.BlockSpec(w_blk, gate), pl.BlockSpec(w_blk, val), pl.BlockSpec(c_blk, gate), pl.BlockSpec(c_blk, val),
            pl.BlockSpec((1, FFN_GROUP, D_MODEL), gate)]


def _ffn_fwd(hn2, h1, target, w_up, conv, w_down, seq_len, tm=512):
    n = hn2.shape[0]
    nj = FFN_GROUPS
    hb = tm // HALO

    def body(hn_ref, halo_ref, h1_ref, tgt_ref, wg_ref, wv_ref, cg_ref, cv_ref, wd_ref,
             ug_ref, uv_ref, pg_ref, pv_ref, dy_ref, loss_ref, acc):
        i, j = pl.program_id(0), pl.program_id(1)
        seq_start = (i * tm) % seq_len == 0
        halo = halo_ref[...]
        halo = jnp.where(seq_start, jnp.zeros_like(halo), halo)
        he = jnp.concatenate([halo, hn_ref[...]], axis=0)
        ueg = _dot_nt(he, wg_ref[0])
        uev = _dot_nt(he, wv_ref[0])
        ug_ref[0] = ueg[HALO:].astype(BF16)
        uv_ref[0] = uev[HALO:].astype(BF16)
        cg = _conv3(ueg, cg_ref[0])[HALO:]
        cv = _conv3(uev, cv_ref[0])[HALO:]
        pg_ref[0] = cg.astype(BF16)
        pv_ref[0] = cv.astype(BF16)
        act = (jax.nn.silu(cg) * cv).astype(BF16)
        part = _dot(act, wd_ref[0])

        @pl.when(j == 0)
        def _():
            acc[...] = part

        @pl.when(j > 0)
        def _():
            acc[...] += part

        @pl.when(j == nj - 1)
        def _():
            err = h1_ref[...] + acc[...] - tgt_ref[...]
            dy_ref[...] = err * (1.0 / D_MODEL)
            loss_ref[0] = jnp.sum(err * err, axis=0, keepdims=True)

    row = pl.BlockSpec((tm, D_MODEL), lambda i, j: (i, 0))
    u_main = pl.BlockSpec((1, tm, FFN_GROUP), lambda i, j: (j, i, 0))
    u_shape = jax.ShapeDtypeStruct((FFN_GROUPS, n, FFN_GROUP), BF16)
    return pl.pallas_call(
        body, name="ffn_fwd", grid=(n // tm, nj),
        in_specs=[row, pl.BlockSpec((HALO, D_MODEL), lambda i, j: (jnp.maximum(i * hb - 1, 0), 0)), row, row,
                  *_ffn_weight_specs()],
        out_specs=(u_main, u_main, u_main, u_main, row, pl.BlockSpec((1, 1, D_MODEL), lambda i, j: (i, 0, 0))),
        out_shape=(u_shape, u_shape, u_shape, u_shape, jax.ShapeDtypeStruct((n, D_MODEL), F32),
                   jax.ShapeDtypeStruct((n // tm, 1, D_MODEL), F32)),
        scratch_shapes=[pltpu.VMEM((tm, D_MODEL), F32)],
        compiler_params=_cparams("parallel", "arbitrary"),
    )(hn2, hn2, h1, target, w_up, w_up, conv, conv, w_down)


def _ffn_bwd(dy, ug, uv, pg, pv, w_up, conv, w_down, seq_len, tm=512):
    n = dy.shape[0]
    nj = FFN_GROUPS
    fb = FFN_GROUP
    hb = tm // HALO
    last_hb = n // HALO - 1
    rows = tm + HALO

    def body(dy_ref, dyn_ref, ug_ref, uv_ref, pgm_ref, pgn_ref, pvm_ref, pvn_ref, wg_ref, wv_ref, cg_ref, cv_ref,
             wd_ref, dug_ref, duv_ref, act_ref, dhn_ref, dcg_ref, dcv_ref, acc):
        i, j = pl.program_id(0), pl.program_id(1)
        seq_end = ((i + 1) * tm) % seq_len == 0
        dyn = dyn_ref[...]
        dyn = jnp.where(seq_end, jnp.zeros_like(dyn), dyn)
        d_out = jnp.concatenate([dy_ref[...], dyn], axis=0).astype(BF16)
        d_act = _dot_nt(d_out, wd_ref[0])
        cge = jnp.concatenate([pgm_ref[0], pgn_ref[0]], axis=0).astype(F32)
        cve = jnp.concatenate([pvm_ref[0], pvn_ref[0]], axis=0).astype(F32)
        act, vjp_act = jax.vjp(lambda g, v: jax.nn.silu(g) * v, cge, cve)
        dcge, dcve = vjp_act(d_act)
        act_ref[0] = act[:tm].astype(BF16)

        def conv_t(dc, u_ref, cw):
            ahead1 = pltpu.roll(dc, rows - 1, 0)[:tm]
            ahead2 = pltpu.roll(dc, rows - 2, 0)[:tm]
            here = dc[:tm]
            du = cw[2:3] * here + cw[1:2] * ahead1 + cw[0:1] * ahead2
            u = u_ref[0].astype(F32)
            col = lambda x: jnp.sum(x, axis=0, keepdims=True)
            grad = jnp.concatenate([col(ahead2 * u), col(ahead1 * u), col(here * u), col(here),
                                    jnp.zeros((4, fb), F32)], axis=0)
            return du.astype(BF16), grad

        cwg, cwv = cg_ref[0], cv_ref[0]
        dug, grad_g = conv_t(dcge, ug_ref, cwg)
        duv, grad_v = conv_t(dcve, uv_ref, cwv)
        dug_ref[0] = dug
        duv_ref[0] = duv
        part = _dot(dug, wg_ref[0]) + _dot(duv, wv_ref[0])

        @pl.when(j == 0)
        def _():
            acc[...] = part

        @pl.when(j > 0)
        def _():
            acc[...] += part

        @pl.when(j == nj - 1)
        def _():
            dhn_ref[...] = acc[...]

        @pl.when(i == 0)
        def _():
            dcg_ref[j] = jnp.zeros((8, fb), F32)
            dcv_ref[j] = jnp.zeros((8, fb), F32)

        dcg_ref[j] += grad_g
        dcv_ref[j] += grad_v

    row = pl.BlockSpec((tm, D_MODEL), lambda i, j: (i, 0))
    u_main = pl.BlockSpec((1, tm, fb), lambda i, j: (j, i, 0))
    u_next = pl.BlockSpec((1, HALO, fb), lambda i, j: (j, jnp.minimum((i + 1) * hb, last_hb), 0))
    dc_spec = pl.BlockSpec((nj, 8, fb), lambda i, j: (0, 0, 0))
    u_shape = jax.ShapeDtypeStruct((FFN_GROUPS, n, fb), BF16)
    return pl.pallas_call(
        body, name="ffn_bwd", grid=(n // tm, nj),
        in_specs=[row, pl.BlockSpec((HALO, D_MODEL), lambda i, j: (jnp.minimum((i + 1) * hb, last_hb), 0)),
                  u_main, u_main, u_main, u_next, u_main, u_next, *_ffn_weight_specs()],
        out_specs=(u_main, u_main, u_main, row, dc_spec, dc_spec),
        out_shape=(u_shape, u_shape, u_shape, jax.ShapeDtypeStruct((n, D_MODEL), F32),
                   jax.ShapeDtypeStruct((nj, 8, fb), F32), jax.ShapeDtypeStruct((nj, 8, fb), F32)),
        scratch_shapes=[pltpu.VMEM((tm, D_MODEL), F32)],
        compiler_params=_cparams("arbitrary", "arbitrary"),
    )(dy, dy, ug, uv, pg, pg, pv, pv, w_up, w_up, conv, conv, w_down)


def _tn_grouped(a, b, name, shared_a, out_dtype=F32, tn=1024):
    groups = b.shape[0] if shared_a else a.shape[0]
    n_tok = a.shape[0] if shared_a else b.shape[0]
    k_dim, m_dim = a.shape[-1], b.shape[-1]

    def body(a_ref, b_ref, o_ref, acc):
        k = pl.program_id(1)
        a_t = a_ref[...] if shared_a else a_ref[0]
        b_t = b_ref[0] if shared_a else b_ref[...]
        part = _dot_tn(a_t.astype(BF16), b_t.astype(BF16))

        @pl.when(k == 0)
        def _():
            acc[...] = part

        @pl.when(k > 0)
        def _():
            acc[...] += part

        @pl.when(k == n_tok // tn - 1)
        def _():
            o_ref[0] = acc[...].astype(out_dtype)

    plain = lambda w: pl.BlockSpec((tn, w), lambda g, k: (k, 0))
    grouped = lambda w: pl.BlockSpec((1, tn, w), lambda g, k: (g, k, 0))
    return pl.pallas_call(
        body, name=name, grid=(groups, n_tok // tn),
        in_specs=[plain(k_dim), grouped(m_dim)] if shared_a else [grouped(k_dim), plain(m_dim)],
        out_specs=pl.BlockSpec((1, k_dim, m_dim), lambda g, k: (g, 0, 0)),
        out_shape=jax.ShapeDtypeStruct((groups, k_dim, m_dim), out_dtype),
        scratch_shapes=[pltpu.VMEM((k_dim, m_dim), F32)],
        compiler_params=_cparams("parallel", "arbitrary"),
    )(a, b)


def _s5_param_fn(lr, li, ldt, br, bi):
    dt = jnp.exp(ldt)
    mag = jnp.exp(lr * dt)
    ab_re = mag * jnp.cos(li * dt)
    ab_im = mag * jnp.sin(li * dt)
    nr = ab_re - 1.0
    ni = ab_im
    den = lr * lr + li * li
    q_re = (nr * lr + ni * li) / den
    q_im = (ni * lr - nr * li) / den
    bb_re = q_re * br - q_im * bi
    bb_im = q_re * bi + q_im * br
    return ab_re, ab_im, bb_re, bb_im


def _s5_param_fwd(lr, li, ldt, br, bi):
    def body(lr_ref, li_ref, ldt_ref, br_ref, bi_ref, ar_ref, ai_ref, bbr_ref, bbi_ref):
        ar, ai, bbr, bbi = _s5_param_fn(lr_ref[...], li_ref[...], ldt_ref[...], br_ref[...], bi_ref[...])
        ar_ref[...] = ar
        ai_ref[...] = ai
        bbr_ref[...] = bbr
        bbi_ref[...] = bbi

    return pl.pallas_call(
        body, name="s5_param_fwd",
        out_shape=(jax.ShapeDtypeStruct(lr.shape, F32), jax.ShapeDtypeStruct(lr.shape, F32),
                   jax.ShapeDtypeStruct(br.shape, F32), jax.ShapeDtypeStruct(br.shape, F32)),
    )(lr, li, ldt, br, bi)


def _s5_param_bwd(lr, li, ldt, br, bi, dar, dai, dbbr, dbbi):
    def body(lr_ref, li_ref, ldt_ref, br_ref, bi_ref, dar_ref, dai_ref, dbbr_ref, dbbi_ref,
             dlr_ref, dli_ref, dldt_ref, dbr_ref, dbi_ref):
        _, vjp = jax.vjp(_s5_param_fn, lr_ref[...], li_ref[...], ldt_ref[...], br_ref[...], bi_ref[...])
        dlr, dli, dldt, dbr, dbi = vjp((dar_ref[...], dai_ref[...], dbbr_ref[...], dbbi_ref[...]))
        dlr_ref[...] = dlr
        dli_ref[...] = dli
        dldt_ref[...] = dldt
        dbr_ref[...] = dbr
        dbi_ref[...] = dbi

    return pl.pallas_call(
        body, name="s5_param_bwd",
        out_shape=(jax.ShapeDtypeStruct(lr.shape, F32), jax.ShapeDtypeStruct(lr.shape, F32),
                   jax.ShapeDtypeStruct(ldt.shape, F32), jax.ShapeDtypeStruct(br.shape, F32),
                   jax.ShapeDtypeStruct(br.shape, F32)),
    )(lr, li, ldt, br, bi, dar, dai, dbbr, dbbi)


S5_CHUNK = 256
S5_STATES = 512
S5_BLOCKS = 4


def _cpow_rows(ar, ai, count):
    rs, im = [ar], [ai]
    for _ in range(count - 1):
        pr, pi = rs[-1], im[-1]
        rs.append(pr * ar - pi * ai)
        im.append(pr * ai + pi * ar)
    return rs, im


def _scan_in_groups(vr, vi, pr, pi, rm, reverse):
    n, width = vr.shape
    vr = vr.reshape(n // SUBLANES, SUBLANES, width)
    vi = vi.reshape(n // SUBLANES, SUBLANES, width)
    row = rm[0:SUBLANES]
    for k in (1, 2, 4):
        shift = SUBLANES - k if reverse else k
        keep = row < SUBLANES - k if reverse else row >= k
        kr = jnp.where(keep, pr[k - 1], 0.0)
        ki = jnp.where(keep, pi[k - 1], 0.0)
        sr, si = pltpu.roll(vr, shift, 1), pltpu.roll(vi, shift, 1)
        vr, vi = vr + kr * sr - ki * si, vi + kr * si + ki * sr
    return vr.reshape(n, width), vi.reshape(n, width)


def _carry_over_groups(xr_s, xi_s, wr, wi, c0r, c0i, reverse):
    groups = xr_s.shape[0] // SUBLANES
    pick = 0 if reverse else SUBLANES - 1

    def step(q, carry):
        cr, ci = carry
        r = groups - 1 - q if reverse else q
        o = pl.multiple_of(r * SUBLANES, SUBLANES)
        vr = xr_s[pl.ds(o, SUBLANES), :]
        vi = xi_s[pl.ds(o, SUBLANES), :]
        nr = vr + wr * cr - wi * ci
        ni = vi + wr * ci + wi * cr
        xr_s[pl.ds(o, SUBLANES), :] = nr
        xi_s[pl.ds(o, SUBLANES), :] = ni
        return (jnp.broadcast_to(nr[pick:pick + 1], nr.shape), jnp.broadcast_to(ni[pick:pick + 1], ni.shape))

    return lax.fori_loop(0, groups, step, (c0r, c0i), unroll=4)


def _s5_state_scan(u_b, bbr, bbi, pr, pi, rm, xr_s, xi_s, c0r, c0i):
    bur = _dot(u_b, bbr)
    bui = _dot(u_b, bbi)
    bur, bui = _scan_in_groups(bur, bui, pr, pi, rm, False)
    xr_s[...] = bur
    xi_s[...] = bui
    w8r = jnp.concatenate(pr, axis=0)
    w8i = jnp.concatenate(pi, axis=0)
    return _carry_over_groups(xr_s, xi_s, w8r, w8i, c0r, c0i, False)


def _s5_fwd(u, a_re, a_im, bbr, bbi, cr, ci, d_skip, n_seq):
    n = u.shape[0]
    seq_len = n // n_seq
    nt = seq_len // S5_CHUNK
    tc = S5_CHUNK

    def body(u_ref, ar_ref, ai_ref, bbr_ref, bbi_ref, cr_ref, ci_ref, d_ref, y_ref, str_ref, sti_ref, xrb_ref, xib_ref,
             xr_s, xi_s, car_r, car_i):
        t = pl.program_id(2)

        @pl.when(t == 0)
        def _():
            car_r[...] = jnp.zeros_like(car_r)
            car_i[...] = jnp.zeros_like(car_i)
        pr, pi = _cpow_rows(ar_ref[0], ai_ref[0], SUBLANES)
        rm = lax.broadcasted_iota(jnp.int32, (tc, S5_STATES), 0) & (SUBLANES - 1)
        str_ref[0, 0] = car_r[...]
        sti_ref[0, 0] = car_i[...]
        u_t = u_ref[...]
        cfr, cfi = _s5_state_scan(u_t.astype(BF16), bbr_ref[0], bbi_ref[0], pr, pi, rm, xr_s, xi_s,
                                  car_r[...], car_i[...])
        car_r[...] = cfr
        car_i[...] = cfi
        xr_b = xr_s[...].astype(BF16)
        xi_b = xi_s[...].astype(BF16)
        xrb_ref[...] = xr_b
        xib_ref[...] = xi_b
        y_ref[...] = _dot(xr_b, cr_ref[0]) - _dot(xi_b, ci_ref[0]) + d_ref[...] * u_t

    x_spec = pl.BlockSpec((tc, S5_STATES), lambda cb, b, t: (b * nt + t, cb))
    x_shape = jax.ShapeDtypeStruct((n, S5_BLOCKS * S5_STATES), BF16)
    u_spec = pl.BlockSpec((tc, LANES), lambda cb, b, t: (b * nt + t, cb))
    a_spec = pl.BlockSpec((1, 1, S5_STATES), lambda cb, b, t: (cb, 0, 0))
    bb_spec = pl.BlockSpec((1, LANES, S5_STATES), lambda cb, b, t: (cb, 0, 0))
    c_spec = pl.BlockSpec((1, S5_STATES, LANES), lambda cb, b, t: (cb, 0, 0))
    st_spec = pl.BlockSpec((1, 1, SUBLANES, S5_STATES), lambda cb, b, t: (cb, b * nt + t, 0, 0))
    st_shape = jax.ShapeDtypeStruct((S5_BLOCKS, n_seq * nt, SUBLANES, S5_STATES), F32)
    return pl.pallas_call(
        body, name="s5_fwd", grid=(S5_BLOCKS, n_seq, nt),
        in_specs=[u_spec, a_spec, a_spec, bb_spec, bb_spec, c_spec, c_spec,
                  pl.BlockSpec((1, LANES), lambda cb, b, t: (0, cb))],
        out_specs=(u_spec, st_spec, st_spec, x_spec, x_spec),
        out_shape=(jax.ShapeDtypeStruct((n, D_SSM), F32), st_shape, st_shape, x_shape, x_shape),
        scratch_shapes=[pltpu.VMEM((tc, S5_STATES), F32), pltpu.VMEM((tc, S5_STATES), F32),
                        pltpu.VMEM((SUBLANES, S5_STATES), F32), pltpu.VMEM((SUBLANES, S5_STATES), F32)],
        compiler_params=_cparams("parallel", "arbitrary", "arbitrary"),
    )(u, a_re, a_im, bbr, bbi, cr, ci, d_skip)


def _s5_bwd(u, dy, xs_r, xs_i, st_r, st_i, a_re, a_im, bbr, bbi, cr, ci, d_skip, n_seq):
    n = u.shape[0]
    seq_len = n // n_seq
    nt = seq_len // S5_CHUNK
    tc = S5_CHUNK

    def body(u_ref, dy_ref, xrb_ref, xib_ref, str_ref, sti_ref, ar_ref, ai_ref, bbr_ref, bbi_ref, cr_ref, ci_ref, d_ref,
             du_ref, dbbr_ref, dbbi_ref, dcr_ref, dci_ref, dar_ref, dai_ref, dd_ref,
             gr_s, gi_s, car_r, car_i):
        b, t = pl.program_id(1), pl.program_id(2)

        @pl.when((b == 0) & (t == 0))
        def _():
            for ref in (dbbr_ref, dbbi_ref, dcr_ref, dci_ref, dar_ref, dai_ref, dd_ref):
                ref[...] = jnp.zeros_like(ref)

        @pl.when(t == 0)
        def _():
            car_r[...] = jnp.zeros_like(car_r)
            car_i[...] = jnp.zeros_like(car_i)
        ar, ai = ar_ref[0], ai_ref[0]
        pr, pi = _cpow_rows(ar, ai, SUBLANES)
        row = lax.broadcasted_iota(jnp.int32, (tc, S5_STATES), 0)
        rm = row & (SUBLANES - 1)
        u_t = u_ref[...]
        u_b = u_t.astype(BF16)
        dy_t = dy_ref[...]
        dy_b = dy_t.astype(BF16)
        s0r, s0i = str_ref[0, 0], sti_ref[0, 0]
        xr_b, xi_b = xrb_ref[...], xib_ref[...]
        xr, xi = xr_b.astype(F32), xi_b.astype(F32)
        gr = _dot_nt(dy_b, cr_ref[0])
        gi = -_dot_nt(dy_b, ci_ref[0])
        npi = [-v for v in pi]
        gr, gi = _scan_in_groups(gr, gi, pr, npi, rm, True)
        gr_s[...] = gr
        gi_s[...] = gi
        w8r = jnp.concatenate(pr[::-1], axis=0)
        w8i = jnp.concatenate(npi[::-1], axis=0)
        cfr, cfi = _carry_over_groups(gr_s, gi_s, w8r, w8i, car_r[...], car_i[...], True)
        car_r[...] = cfr
        car_i[...] = cfi
        gr, gi = gr_s[...], gi_s[...]
        gr_b, gi_b = gr.astype(BF16), gi.astype(BF16)
        du_ref[...] = _dot_nt(gr_b, bbr_ref[0]) + _dot_nt(gi_b, bbi_ref[0]) + d_ref[...] * dy_t
        dbbr_ref[0] += _dot_tn(u_b, gr_b)
        dbbi_ref[0] += _dot_tn(u_b, gi_b)
        dcr_ref[0] += _dot_tn(xr_b, dy_b)
        dci_ref[0] -= _dot_tn(xi_b, dy_b)
        dd_ref[0] += jnp.sum((dy_t * u_t).reshape(tc // SUBLANES, SUBLANES, LANES), axis=0)
        first = row == 0
        xpr = jnp.where(first, jnp.broadcast_to(s0r[0:1], xr.shape), pltpu.roll(xr, 1, 0))
        xpi = jnp.where(first, jnp.broadcast_to(s0i[0:1], xi.shape), pltpu.roll(xi, 1, 0))
        shp = (tc // SUBLANES, SUBLANES, S5_STATES)
        dar_ref[0] += jnp.sum((gr * xpr + gi * xpi).reshape(shp), axis=0)
        dai_ref[0] += jnp.sum((gi * xpr - gr * xpi).reshape(shp), axis=0)

    u_spec = pl.BlockSpec((tc, LANES), lambda cb, b, t: (b * nt + nt - 1 - t, cb))
    a_spec = pl.BlockSpec((1, 1, S5_STATES), lambda cb, b, t: (cb, 0, 0))
    bb_spec = pl.BlockSpec((1, LANES, S5_STATES), lambda cb, b, t: (cb, 0, 0))
    c_spec = pl.BlockSpec((1, S5_STATES, LANES), lambda cb, b, t: (cb, 0, 0))
    st_spec = pl.BlockSpec((1, 1, SUBLANES, S5_STATES), lambda cb, b, t: (cb, b * nt + nt - 1 - t, 0, 0))
    da_spec = pl.BlockSpec((1, SUBLANES, S5_STATES), lambda cb, b, t: (cb, 0, 0))
    dd_spec = pl.BlockSpec((1, SUBLANES, LANES), lambda cb, b, t: (cb, 0, 0))
    big = pltpu.VMEM((tc, S5_STATES), F32)
    small = pltpu.VMEM((SUBLANES, S5_STATES), F32)
    x_spec = pl.BlockSpec((tc, S5_STATES), lambda cb, b, t: (b * nt + nt - 1 - t, cb))
    return pl.pallas_call(
        body, name="s5_bwd", grid=(S5_BLOCKS, n_seq, nt),
        in_specs=[u_spec, u_spec, x_spec, x_spec, st_spec, st_spec, a_spec, a_spec, bb_spec, bb_spec, c_spec, c_spec,
                  pl.BlockSpec((1, LANES), lambda cb, b, t: (0, cb))],
        out_specs=(u_spec, bb_spec, bb_spec, c_spec, c_spec, da_spec, da_spec, dd_spec),
        out_shape=(jax.ShapeDtypeStruct((n, D_SSM), F32),
                   jax.ShapeDtypeStruct((S5_BLOCKS, LANES, S5_STATES), F32),
                   jax.ShapeDtypeStruct((S5_BLOCKS, LANES, S5_STATES), F32),
                   jax.ShapeDtypeStruct((S5_BLOCKS, S5_STATES, LANES), F32),
                   jax.ShapeDtypeStruct((S5_BLOCKS, S5_STATES, LANES), F32),
                   jax.ShapeDtypeStruct((S5_BLOCKS, SUBLANES, S5_STATES), F32),
                   jax.ShapeDtypeStruct((S5_BLOCKS, SUBLANES, S5_STATES), F32),
                   jax.ShapeDtypeStruct((S5_BLOCKS, SUBLANES, LANES), F32)),
        scratch_shapes=[big, big, small, small],
        compiler_params=_cparams("parallel", "arbitrary", "arbitrary"),
    )(u, dy, xs_r, xs_i, st_r, st_i, a_re, a_im, bbr, bbi, cr, ci, d_skip)


CUM_BLOCK = 128


def _tri(lower):
    r = lax.broadcasted_iota(jnp.int32, (CUM_BLOCK, CUM_BLOCK), 0)
    c = lax.broadcasted_iota(jnp.int32, (CUM_BLOCK, CUM_BLOCK), 1)
    return jnp.where(r >= c if lower else r <= c, 1.0, 0.0).astype(F32)


def _fprep_fwd(fl, bf, n_seq):
    n = fl.shape[0]
    seq_len = n // n_seq
    nb = seq_len // CUM_BLOCK

    def body(fl_ref, bf_ref, cum_ref):
        tril = _tri(True)
        carry = jnp.zeros((1, LANES), F32)
        for blk in range(nb):
            rows = slice(blk * CUM_BLOCK, (blk + 1) * CUM_BLOCK)
            lf = jax.nn.log_sigmoid(fl_ref[rows, :] + bf_ref[...])
            cs = jnp.dot(tril, lf, preferred_element_type=F32, precision=HIGHEST) + carry
            cum_ref[rows, :] = cs
            carry = cs[CUM_BLOCK - 1:CUM_BLOCK, :]

    spec = pl.BlockSpec((seq_len, LANES), lambda b: (b, 0))
    return pl.pallas_call(
        body, name="fprep_fwd", grid=(n_seq,), in_specs=[spec, pl.BlockSpec((1, LANES), lambda b: (0, 0))],
        out_specs=spec, out_shape=jax.ShapeDtypeStruct((n, LANES), F32), compiler_params=_cparams("parallel"),
    )(fl, bf)


def _fprep_bwd(dcum, fl, bf, n_seq):
    n = fl.shape[0]
    seq_len = n // n_seq
    nb = seq_len // CUM_BLOCK

    def body(dcum_ref, fl_ref, bf_ref, dfl_ref, dbf_ref):
        triu = _tri(False)
        lane = lax.broadcasted_iota(jnp.int32, (CUM_BLOCK, LANES), 1)
        carry = jnp.zeros((1, LANES), F32)
        total = jnp.zeros((1, LANES), F32)
        for blk in reversed(range(nb)):
            rows = slice(blk * CUM_BLOCK, (blk + 1) * CUM_BLOCK)
            rs = jnp.dot(triu, dcum_ref[rows, :], preferred_element_type=F32, precision=HIGHEST) + carry
            carry = rs[0:1, :]
            _, vjp = jax.vjp(jax.nn.log_sigmoid, fl_ref[rows, :] + bf_ref[...])
            dz = jnp.where(lane < N_HEADS, vjp(rs)[0], 0.0)
            dfl_ref[rows, :] = dz
            total = total + jnp.sum(dz, axis=0, keepdims=True)
        dbf_ref[0] = total

    spec = pl.BlockSpec((seq_len, LANES), lambda b: (b, 0))
    return pl.pallas_call(
        body, name="fprep_bwd", grid=(n_seq,), in_specs=[spec, spec, pl.BlockSpec((1, LANES), lambda b: (0, 0))],
        out_specs=(spec, pl.BlockSpec((1, 1, LANES), lambda b: (b, 0, 0))),
        out_shape=(jax.ShapeDtypeStruct((n, LANES), F32), jax.ShapeDtypeStruct((n_seq, 1, LANES), F32)),
        compiler_params=_cparams("parallel"),
    )(dcum, fl, bf)


ATT_TQ = 256
ATT_KSTEP = 256
ATT_SCALE = HEAD_DIM ** -0.5
NEG_BIG = -1e30


assert ATT_KSTEP == ATT_TQ


def _scores(q_scaled, kb, row_bias, ck, kend):
    s = _dot_nt(q_scaled, kb) - ck
    if row_bias is not None:
        s = s + row_bias
    r = lax.broadcasted_iota(jnp.int32, (ATT_TQ, ATT_TQ), 0)
    c = lax.broadcasted_iota(jnp.int32, (ATT_TQ, ATT_TQ), 1)
    diag = jnp.where(r >= c, s[:, kend - ATT_TQ:], NEG_BIG)
    return diag if kend == ATT_TQ else jnp.concatenate([s[:, :kend - ATT_TQ], diag], axis=1)


def _attn_specs(n_seq, seq_len):
    nq = seq_len // ATT_TQ
    q_spec = pl.BlockSpec((ATT_TQ, LANES), lambda b, h, q: (b * nq + q, h))
    k_spec = pl.BlockSpec((seq_len, LANES), lambda b, h, q: (b, N_HEADS // 2 + h))
    v_spec = pl.BlockSpec((seq_len, LANES), lambda b, h, q: (b, N_HEADS + h))
    cq_spec = pl.BlockSpec((1, 2, ATT_TQ, 1), lambda b, h, q: (b, h, q, 0))
    ck_spec = pl.BlockSpec((1, 2, 1, seq_len), lambda b, h, q: (b, h, 0, 0))
    return nq, q_spec, k_spec, v_spec, cq_spec, ck_spec


def _own_cum(cum_ref, e):
    lane = lax.broadcasted_iota(jnp.int32, (1, LANES), 1)
    return jnp.sum(jnp.where(lane == 2 * pl.program_id(1) + e, cum_ref[...], 0.0), axis=1, keepdims=True)


def _head_selectors():
    head0 = lax.broadcasted_iota(jnp.int32, (1, LANES), 1) < HEAD_DIM
    return head0, (head0, jnp.logical_not(head0))


def _for_key_range(qi, seq_len, run):
    per = ATT_KSTEP // ATT_TQ
    for g in range(seq_len // ATT_KSTEP):
        pl.when(qi // per == g)(functools.partial(run, (g + 1) * ATT_KSTEP))


def _attn_fwd(qkv, cum, ck, n_seq):
    n = qkv.shape[0]
    seq_len = n // n_seq
    nq, q_spec, k_spec, v_spec, cq_spec, ck_spec = _attn_specs(n_seq, seq_len)
    cum_spec = pl.BlockSpec((ATT_TQ, LANES), lambda b, h, q: (b * nq + q, 0))

    def body(q_ref, k_ref, v_ref, cum_ref, ck_ref, o_ref, lse_ref):
        qi = pl.program_id(2)
        q2 = q_ref[...]
        head0, sels = _head_selectors()
        qe = [jnp.where(sel, q2 * ATT_SCALE, 0.0).astype(BF16) for sel in sels]

        def run(kend):
            kb = k_ref[0:kend, :].astype(BF16)
            vb = v_ref[0:kend, :].astype(BF16)
            outs = []
            for e in range(2):
                s = _scores(qe[e], kb, None, ck_ref[0, e, :, 0:kend], kend)
                mx = jnp.max(s, axis=1, keepdims=True)
                p = jnp.exp(s - mx)
                den = jnp.sum(p, axis=1, keepdims=True)
                outs.append(_dot(p.astype(BF16), vb) / den)
                lse_ref[0, e] = _own_cum(cum_ref, e) + mx + jnp.log(den)
            o_ref[...] = jnp.where(head0, outs[0], outs[1])

        _for_key_range(qi, seq_len, run)

    return pl.pallas_call(
        body, name="attn_fwd", grid=(n_seq, N_HEADS // 2, nq),
        in_specs=[q_spec, k_spec, v_spec, cum_spec, ck_spec],
        out_specs=(q_spec, cq_spec),
        out_shape=(jax.ShapeDtypeStruct((n, D_ATTN), F32), jax.ShapeDtypeStruct((n_seq, N_HEADS, seq_len, 1), F32)),
        compiler_params=_cparams("parallel", "parallel", "parallel"),
    )(qkv, qkv, qkv, cum, ck)


def _attn_bwd(qkv, cum, ck, o, do, lse, n_seq):
    n = qkv.shape[0]
    seq_len = n // n_seq
    nq, q_spec, k_spec, v_spec, cq_spec, ck_spec = _attn_specs(n_seq, seq_len)
    kv_out = pl.BlockSpec((seq_len, LANES), lambda b, h, q: (b, h))
    cum_spec = pl.BlockSpec((ATT_TQ, LANES), lambda b, h, q: (b * nq + q, 0))

    def body(q_ref, k_ref, v_ref, cum_ref, ck_ref, o_ref, do_ref, lse_ref, dq_ref, dk_ref, dv_ref, dcq_ref, dck_ref):
        qi = pl.program_id(2)

        @pl.when(qi == 0)
        def _():
            dk_ref[...] = jnp.zeros_like(dk_ref)
            dv_ref[...] = jnp.zeros_like(dv_ref)
            dck_ref[...] = jnp.zeros_like(dck_ref)
        q2 = q_ref[...]
        do2 = do_ref[...]
        o2 = o_ref[...]
        head0, sels = _head_selectors()
        qe = [jnp.where(sel, q2 * ATT_SCALE, 0.0).astype(BF16) for sel in sels]
        doe = [jnp.where(sel, do2, 0.0) for sel in sels]
        doe_b = [d.astype(BF16) for d in doe]
        delta = [jnp.sum(d * o2, axis=1, keepdims=True) for d in doe]

        def run(kend):
            kb = k_ref[0:kend, :].astype(BF16)
            vb = v_ref[0:kend, :].astype(BF16)
            dqs = []
            dk = jnp.zeros((kend, LANES), F32)
            dv = jnp.zeros((kend, LANES), F32)
            for e in range(2):
                p = jnp.exp(_scores(qe[e], kb, _own_cum(cum_ref, e) - lse_ref[0, e], ck_ref[0, e, :, 0:kend], kend))
                ds = p * (_dot_nt(doe_b[e], vb) - delta[e])
                ds_b = ds.astype(BF16)
                dqs.append(_dot(ds_b, kb))
                dk = dk + _dot_tn(ds_b, qe[e])
                dv = dv + _dot_tn(p.astype(BF16), doe_b[e])
                dcq_ref[0, e] = jnp.sum(ds, axis=1, keepdims=True)
                dck_ref[0, e, :, 0:kend] -= jnp.sum(ds, axis=0, keepdims=True)
            dk_ref[0:kend, :] += dk
            dv_ref[0:kend, :] += dv
            dq_ref[...] = jnp.where(head0, dqs[0], dqs[1]) * ATT_SCALE

        _for_key_range(qi, seq_len, run)

    return pl.pallas_call(
        body, name="attn_bwd", grid=(n_seq, N_HEADS // 2, nq),
        in_specs=[q_spec, k_spec, v_spec, cum_spec, ck_spec, q_spec, q_spec, cq_spec],
        out_specs=(q_spec, kv_out, kv_out, cq_spec, ck_spec),
        out_shape=(jax.ShapeDtypeStruct((n, D_ATTN), F32), jax.ShapeDtypeStruct((n, D_ATTN), F32),
                   jax.ShapeDtypeStruct((n, D_ATTN), F32),
                   jax.ShapeDtypeStruct((n_seq, N_HEADS, seq_len, 1), F32),
                   jax.ShapeDtypeStruct((n_seq, N_HEADS, 1, seq_len), F32)),
        compiler_params=_cparams("parallel", "parallel", "arbitrary"),
    )(qkv, qkv, qkv, cum, ck, o, do, lse)


WEIGHT_NAMES = ("norm_mix", "w_in", "b_forget", "lam_re", "lam_im", "b_re", "b_im", "c_re", "c_im", "d_skip", "log_dt",
                "w_glu", "b_glu", "q_norm", "k_norm", "norm_out_ssm", "norm_out_attn", "w_out", "norm_ffn", "w_up",
                "conv_w", "conv_b", "w_down")
SHARDED = ("w_in", "w_glu", "w_out", "w_up", "conv_w", "w_down")
ADAM_TILE = {"w_in": (257, 256), "w_glu": (64, 512), "w_out": (128, 1024), "w_up": (688, 256), "conv_w": (3, 688),
             "w_down": (344, 1024)}
PACK_ROWS = SUBLANES * LANES


def _after_all(*arrays):
    return sum(a[(0,) * a.ndim].astype(F32) for a in arrays).reshape(1, 1)


def _pad_to(a, axis, size):
    pad = [(0, 0)] * a.ndim
    pad[axis] = (0, size - a.shape[axis])
    return jnp.pad(a, pad)


def _block_diag(t, transpose):
    t4 = t.reshape(S5_BLOCKS, 8, SSM_GROUP, SSM_STATE)
    eye = jnp.eye(8, dtype=t.dtype)
    if transpose:
        e = jnp.swapaxes(t4, 2, 3)[:, :, :, None, :] * eye[None, :, None, :, None]
        return e.reshape(S5_BLOCKS, S5_STATES, LANES)
    e = t4[:, :, :, None, :] * eye[None, :, None, :, None]
    return e.reshape(S5_BLOCKS, LANES, S5_STATES)


def _block_diag_extract(m, transpose):
    if transpose:
        m5 = m.reshape(S5_BLOCKS, 8, SSM_STATE, 8, SSM_GROUP)
        d = jnp.stack([m5[:, i, :, i, :] for i in range(8)], axis=1)
        return jnp.swapaxes(d, 2, 3).reshape(N_GROUPS, SSM_GROUP, SSM_STATE)
    m5 = m.reshape(S5_BLOCKS, 8, SSM_GROUP, 8, SSM_STATE)
    d = jnp.stack([m5[:, i, :, i, :] for i in range(8)], axis=1)
    return d.reshape(N_GROUPS, SSM_GROUP, SSM_STATE)


def _pack(pieces):
    flat = jnp.concatenate([p.reshape(-1).astype(F32) for p in pieces])
    size = -(-flat.shape[0] // PACK_ROWS) * PACK_ROWS
    return _pad_to(flat, 0, size).reshape(-1, LANES)


def _unpack(packed, shapes):
    flat = packed.reshape(-1)
    out, off = [], 0
    for shp in shapes:
        size = math.prod(shp)
        out.append(flat[off:off + size].reshape(shp))
        off += size
    return out


def kernel(x, norm_mix, w_in, b_forget, lam_re, lam_im, b_re, b_im, c_re, c_im, d_skip, log_dt, w_glu, b_glu, q_norm, k_norm, norm_out_ssm, norm_out_attn, w_out, norm_ffn, w_up, conv_w, conv_b, w_down, loss_target, m_norm_mix, m_w_in, m_b_forget, m_lam_re, m_lam_im, m_b_re, m_b_im, m_c_re, m_c_im, m_d_skip, m_log_dt, m_w_glu, m_b_glu, m_q_norm, m_k_norm, m_norm_out_ssm, m_norm_out_attn, m_w_out, m_norm_ffn, m_w_up, m_conv_w, m_conv_b, m_w_down, v_norm_mix, v_w_in, v_b_forget, v_lam_re, v_lam_im, v_b_re, v_b_im, v_c_re, v_c_im, v_d_skip, v_log_dt, v_w_glu, v_b_glu, v_q_norm, v_k_norm, v_norm_out_ssm, v_norm_out_attn, v_w_out, v_norm_ffn, v_w_up, v_conv_w, v_conv_b, v_w_down):
    given = dict(locals())
    weights = {k: given[k] for k in WEIGHT_NAMES}
    mom1 = {k: given["m_" + k] for k in WEIGHT_NAMES}
    mom2 = {k: given["v_" + k] for k in WEIGHT_NAMES}
    n_seq, seq_len, _ = x.shape
    n = n_seq * seq_len
    xf = x.reshape(n, D_MODEL)
    target = loss_target.reshape(n, D_MODEL)
    me_idx = 4 * lax.axis_index("x") + 2 * lax.axis_index("y") + lax.axis_index("c")

    in_flags = [False] * 2
    in_sems = _exchange_start([jnp.swapaxes(w_in[0], 0, 1).astype(BF16), conv_w[0]], in_flags, norm_mix,
                              "gather_in_start", 3)
    fill_own = lambda got, mine: lax.dynamic_update_index_in_dim(got, mine, me_idx, 0)

    lr3 = lam_re[0].reshape(N_GROUPS, 1, SSM_STATE)
    li3 = lam_im[0].reshape(N_GROUPS, 1, SSM_STATE)
    ldt3 = log_dt[0].reshape(N_GROUPS, 1, 1)
    br_t = jnp.swapaxes(b_re[0], 1, 2)
    bi_t = jnp.swapaxes(b_im[0], 1, 2)
    ab_re, ab_im, bb_re, bb_im = _s5_param_fwd(lr3, li3, ldt3, br_t, bi_t)
    a_re = ab_re.reshape(S5_BLOCKS, 1, S5_STATES)
    a_im = ab_im.reshape(S5_BLOCKS, 1, S5_STATES)
    bbr = _block_diag(bb_re, False).astype(BF16)
    bbi = _block_diag(bb_im, False).astype(BF16)
    cr = _block_diag(c_re[0], True).astype(BF16)
    ci = _block_diag(c_im[0], True).astype(BF16)
    avg = jnp.kron(jnp.eye(N_HEADS, dtype=F32), jnp.full((HEAD_DIM, HEAD_DIM), 1.0 / HEAD_DIM, F32)).astype(BF16)
    qg = jnp.tile(q_norm, (1, N_HEADS))
    kg = jnp.tile(k_norm, (1, N_HEADS))
    row_shards = [w_down[0].astype(BF16), w_out[0].astype(BF16), w_glu[0].astype(BF16)]

    (own_in, own_cw), (g_in, g_cw) = _exchange_wait(in_sems[0], in_sems[1], in_sems[2], in_sems[3], in_flags,
                                                    _after_all(a_re, a_im, bbr, bbi, cr, ci, avg, qg, kg, *row_shards),
                                                    "gather_in_wait")
    g_in = fill_own(g_in, own_in)
    g_cw = fill_own(g_cw, own_cw)
    row_flags = [False] * 3
    r_sems = _exchange_start(row_shards, row_flags, g_in, "gather_rows_start", 0)
    u_sems = _exchange_start([jnp.swapaxes(w_up[0], 0, 1).astype(BF16)], [False], r_sems[4], "gather_up_start", 5)
    norm_mix = norm_mix + u_sems[4][0, 0]
    w_in_p = _pad_to(g_in.reshape(D_IN, D_MODEL), 0, D_IN_PAD)

    hn, u, qkv, raw, fl = _inproj_fwd(xf, norm_mix, w_in_p, avg, qg, kg)
    yc, st_r, st_i, xs_r, xs_i = _s5_fwd(u, a_re, a_im, bbr, bbi, cr, ci, d_skip, n_seq)
    bf = _pad_to(b_forget, 1, LANES)
    cum = _fprep_fwd(fl, bf, n_seq)
    cum8 = jnp.swapaxes(cum[:, :N_HEADS].reshape(n_seq, seq_len, N_HEADS), 1, 2)
    ck = cum8[:, :, None, :]
    ya, lse = _attn_fwd(qkv, cum, ck, n_seq)
    own_rows, got_rows = _exchange_wait(r_sems[0], r_sems[1], r_sems[2], r_sems[3], row_flags, ya, "gather_rows_wait")
    g_down, g_out, g_glu = [fill_own(g, o) for o, g in zip(own_rows, got_rows)]
    w_glu_f = g_glu.reshape(D_SSM, D_SSM)
    w_out_f = g_out.reshape(D_MODEL, D_MODEL)
    conv_st = _pad_to(jnp.concatenate([g_cw, conv_b.reshape(N_DEV, 1, -1)], axis=1), 1, SUBLANES)
    w_down4 = g_down.reshape(FFN_GROUPS, FFN_GROUP, D_MODEL)
    ys = _glu_fwd(yc, w_glu_f, b_glu)
    h1, hn2, mixed = _mix_fwd(xf, ys, ya, norm_out_ssm, norm_out_attn, w_out_f, norm_ffn)
    (own_up,), (g_up,) = _exchange_wait(u_sems[0], u_sems[1], u_sems[2], u_sems[3], [False], _after_all(h1, hn2),
                                        "gather_up_wait")
    g_up = fill_own(g_up, own_up)
    ug, uv, pg, pv, dy, loss_part = _ffn_fwd(hn2, h1, target, g_up, conv_st, w_down4, seq_len)
    loss_local = 0.5 * jnp.sum(loss_part) / D_MODEL

    dug, duv, act, dhn2, dcg, dcv = _ffn_bwd(dy, ug, uv, pg, pv, g_up, conv_st, w_down4, seq_len)
    dh1, dys, dya, d_gs, d_ga, d_gf = _mix_bwd(dy, dhn2[None], h1, ys, ya, norm_out_ssm, norm_out_attn, w_out_f, norm_ffn)
    dyc, gl_b, dz_b, d_bglu = _glu_bwd(yc, dys, w_glu_f, b_glu)

    gw_glu = _tn_matmul(gl_b, dz_b, "dw_glu", D_SSM, D_SSM, out_dtype=BF16)
    gw_out = _tn_matmul(mixed, dh1, "dw_out", D_MODEL, D_MODEL, out_dtype=BF16)
    gw_up = jnp.concatenate([_tn_grouped(dug, hn2, "dw_up_gate", False, BF16),
                             _tn_grouped(duv, hn2, "dw_up_val", False, BF16)], axis=0)
    gw_down = _tn_grouped(act, dy, "dw_down", False, BF16)
    g_conv = jnp.concatenate([dcg, dcv], axis=0)
    by_cols = lambda g, c: jnp.swapaxes(g.reshape(g.shape[0], N_DEV, c), 0, 1)
    early_flags = [True] * 4
    early_names = ("w_down", "w_out", "w_glu", "w_up")
    g_sems = _exchange_start(
        [gw_down.reshape(N_DEV, -1, D_MODEL), gw_out.reshape(N_DEV, -1, D_MODEL), gw_glu.reshape(N_DEV, -1, D_SSM), gw_up],
        early_flags, dyc, "grad_early_start", 1)
    started = g_sems[4][0, 0]

    du, dbbr, dbbi, dcr, dci, dar, dai, ddk = _s5_bwd(u, dyc, xs_r, xs_i, st_r, st_i, a_re, a_im, bbr, bbi, cr, ci,
                                                      d_skip + started, n_seq)
    partial_early = {
        "ab_re": jnp.sum(dar, axis=1), "ab_im": jnp.sum(dai, axis=1),
        "bb_re": _block_diag_extract(dbbr, False), "bb_im": _block_diag_extract(dbbi, False),
        "c_re": _block_diag_extract(dcr, True), "c_im": _block_diag_extract(dci, True),
        "d_skip": jnp.sum(ddk, axis=1), "b_glu": d_bglu,
        "norm_out_ssm": d_gs, "norm_out_attn": d_ga, "norm_ffn": d_gf, "conv_b": g_conv[:, 3],
    }
    early_keys = tuple(partial_early)
    early_shapes = [partial_early[k].shape for k in early_keys]
    p_sems = _exchange_start([_pack([partial_early[k] for k in early_keys])], [False], du, "small_early_start", 2)
    started = started + p_sems[4][0, 0]

    dqn, dkn, dv, dcq, dck = _attn_bwd(qkv, cum, ck + started, ya, dya, lse, n_seq)
    dcum8 = dcq[:, :, :, 0] + dck.reshape(n_seq, N_HEADS, seq_len)
    dcum = _pad_to(jnp.swapaxes(dcum8, 1, 2).reshape(n, N_HEADS), 1, LANES)
    dfl, dbf = _fprep_bwd(dcum, fl, bf, n_seq)
    dx, dproj, d_gmix, d_qg, d_kg = _inproj_bwd(xf, norm_mix, w_in_p, avg, qg, kg, raw, du, dqn, dkn, dv, dfl, dh1)

    gw_in = _tn_matmul(dproj, hn, "dw_in", D_IN_PAD, D_MODEL, out_rows=D_IN, out_dtype=BF16)
    partial_late = {
        "norm_mix": d_gmix, "b_forget": jnp.sum(dbf, axis=(0, 1))[:N_HEADS],
        "q_norm": jnp.sum(d_qg.reshape(N_HEADS, HEAD_DIM), axis=0),
        "k_norm": jnp.sum(d_kg.reshape(N_HEADS, HEAD_DIM), axis=0), "loss": loss_local.reshape(1),
    }
    late_keys = tuple(partial_late)
    late_shapes = [partial_late[k].shape for k in late_keys]

    late_flags = [True, True, False]
    l_sems = _exchange_start(
        [gw_in.reshape(N_DEV, D_IN // N_DEV, D_MODEL).astype(BF16), g_conv[:, :3],
         _pack([partial_late[k] for k in late_keys])],
        late_flags, dx, "grad_late_start", 4)
    early_src, early_land = _exchange_wait(g_sems[0], g_sems[1], g_sems[2], g_sems[3], early_flags, l_sems[4],
                                           "grad_early_wait")
    land = dict(zip(early_names, early_land))
    land_up = land["w_up"]
    own = {k: lax.dynamic_index_in_dim(s, me_idx, 0, keepdims=False) for k, s in zip(early_names, early_src)}
    grads, deltas, new_m, new_v = {}, {}, {}, {}

    def adam_shard(name):
        flip = (lambda a: jnp.swapaxes(a, 1, 2)) if name in ("w_in", "w_up") else (lambda a: a)
        outs = _adam_sharded(land[name], own[name], flip(weights[name]), flip(mom1[name]), flip(mom2[name]),
                             "adam_" + name, ADAM_TILE[name])
        grads[name], deltas[name], new_m[name], new_v[name] = [flip(o) for o in outs]

    for name in ("w_up", "w_down", "w_out", "w_glu"):
        adam_shard(name)
    (own_pack,), (early_parts,) = _exchange_wait(p_sems[0], p_sems[1], p_sems[2], p_sems[3], [False], land_up,
                                                 "small_early_wait")
    early_sum = _sum_partials(early_parts, own_pack, "sum_early_partials")
    (src_in, src_cw, own_late), (land["w_in"], land["conv_w"], late_parts) = _exchange_wait(
        l_sems[0], l_sems[1], l_sems[2], l_sems[3], late_flags,
        _after_all(early_sum, *[new_v[k] for k in ("w_up", "w_down", "w_out", "w_glu")]), "grad_late_wait")
    own["w_in"] = lax.dynamic_index_in_dim(src_in, me_idx, 0, keepdims=False)
    own["conv_w"] = lax.dynamic_index_in_dim(src_cw, me_idx, 0, keepdims=False)
    for name in ("w_in", "conv_w"):
        adam_shard(name)

    summed = dict(zip(late_keys, _unpack(_sum_partials(late_parts, own_late, "sum_late_partials"), late_shapes)))
    summed.update(zip(early_keys, _unpack(early_sum, early_shapes)))
    dlr, dli, dldt, dbr_t, dbi_t = _s5_param_bwd(
        lr3, li3, ldt3, br_t, bi_t, summed["ab_re"].reshape(lr3.shape), summed["ab_im"].reshape(lr3.shape),
        summed["bb_re"], summed["bb_im"])
    small_grads = {
        "norm_mix": summed["norm_mix"], "b_forget": summed["b_forget"], "lam_re": dlr, "lam_im": dli,
        "b_re": dbr_t, "b_im": dbi_t, "c_re": summed["c_re"], "c_im": summed["c_im"],
        "d_skip": summed["d_skip"], "log_dt": dldt, "b_glu": summed["b_glu"], "q_norm": summed["q_norm"],
        "k_norm": summed["k_norm"], "norm_out_ssm": summed["norm_out_ssm"], "norm_out_attn": summed["norm_out_attn"],
        "norm_ffn": summed["norm_ffn"], "conv_b": summed["conv_b"],
    }
    repl = tuple(k for k in WEIGHT_NAMES if k not in SHARDED)
    turn = lambda k, a: jnp.swapaxes(a, 2, 3) if k in ("b_re", "b_im") else a
    w_list = [turn(k, weights[k]) for k in repl]
    g_list = [small_grads[k].reshape(w.shape) for k, w in zip(repl, w_list)]
    d_list, m_list, v_list = _adam_replicated(g_list, w_list, [turn(k, mom1[k]) for k in repl],
                                              [turn(k, mom2[k]) for k in repl], "adam_replicated")
    for k, g, d, nm, nv in zip(repl, g_list, d_list, m_list, v_list):
        grads[k], deltas[k], new_m[k], new_v[k] = turn(k, g), turn(k, d), turn(k, nm), turn(k, nv)

    grad_x = dx.reshape(x.shape)
    loss = summed["loss"].reshape(())
    return (loss, grad_x, *[grads[k] for k in WEIGHT_NAMES], *[deltas[k] for k in WEIGHT_NAMES],
            *[new_m[k] for k in WEIGHT_NAMES], *[new_v[k] for k in WEIGHT_NAMES])
```

```python
import functools
import math

import jax
import jax.numpy as jnp
from jax import lax
from jax.experimental import pallas as pl
from jax.experimental.pallas import tpu as pltpu

F32 = jnp.float32
BF16 = jnp.bfloat16
HIGHEST = lax.Precision.HIGHEST

N_DEV = 8
D_MODEL = 1024
D_SSM = 512
D_ATTN = 512
N_HEADS = 8
HEAD_DIM = 64
N_GROUPS = 32
SSM_GROUP = 16
SSM_STATE = 64
D_FF = 2752
D_FF_PAD = 2816
D_IN = 2056
D_IN_PAD = 2176
EPS = 1e-6
LANES = 128
SUBLANES = 8
VMEM_LIMIT = 56 * 1024 * 1024

ADAM_LR = 0.001
ADAM_B1 = 0.9
ADAM_B2 = 0.999
ADAM_EPS = 1e-08
ADAM_WD = 0.01
ADAM_STEP = 10


def _cparams(*sem):
    return pltpu.CompilerParams(dimension_semantics=sem, vmem_limit_bytes=VMEM_LIMIT)


def _dot(a, b, **kw):
    return jnp.dot(a, b, preferred_element_type=F32, **kw)


def _dot_nt(a, b):
    return lax.dot_general(a, b, (((1,), (1,)), ((), ())), preferred_element_type=F32)


def _dot_tn(a, b):
    return lax.dot_general(a, b, (((0,), (0,)), ((), ())), preferred_element_type=F32)


def _rms(x, g):
    return x * lax.rsqrt(jnp.mean(x * x, axis=-1, keepdims=True) + EPS) * g


def _split_dot(x, avg):
    hi = x.astype(BF16)
    lo = (x - hi.astype(F32)).astype(BF16)
    return _dot(hi, avg) + _dot(lo, avg)


@jax.custom_vjp
def _group_mean(x, avg):
    return _split_dot(x, avg)


def _group_mean_fwd(x, avg):
    return _split_dot(x, avg), avg


def _group_mean_bwd(avg, ct):
    return _split_dot(ct, avg), jnp.zeros_like(avg)


_group_mean.defvjp(_group_mean_fwd, _group_mean_bwd)


def _headnorm(q, avg, g):
    return q * lax.rsqrt(_group_mean(q * q, avg) + EPS) * g


def _exchange(srcs, scatter_flags, name):
    n = len(srcs)
    out_shape = []
    for s, sc in zip(srcs, scatter_flags):
        shp = s.shape if sc else (N_DEV,) + s.shape
        out_shape.append(jax.ShapeDtypeStruct(shp, s.dtype))

    def body(*refs):
        src = refs[:n]
        dst = refs[n:2 * n]
        send_sems, recv_sems, loc_sems = refs[2 * n:]
        x, y, c = lax.axis_index("x"), lax.axis_index("y"), lax.axis_index("c")
        me = 4 * x + 2 * y + c
        peers = []
        for j in range(1, N_DEV):
            px = 1 - x if (j >> 2) & 1 else x
            py = 1 - y if (j >> 1) & 1 else y
            pc = 1 - c if j & 1 else c
            peers.append(((px, py, pc), 4 * px + 2 * py + pc))
        local, sends = [], []
        for k in range(n):
            own = src[k].at[me] if scatter_flags[k] else src[k]
            lc = pltpu.make_async_copy(own, dst[k].at[me], loc_sems.at[k])
            lc.start()
            local.append(lc)
            for j, (pid, pidx) in enumerate(peers):
                s = src[k].at[pidx] if scatter_flags[k] else src[k]
                cp = pltpu.make_async_remote_copy(
                    src_ref=s, dst_ref=dst[k].at[me], send_sem=send_sems.at[k, j], recv_sem=recv_sems.at[k, j],
                    device_id=pid, device_id_type=pl.DeviceIdType.MESH)
                cp.start()
                sends.append(cp)
        for k in range(n):
            for j, (pid, pidx) in enumerate(peers):
                s = src[k].at[pidx] if scatter_flags[k] else src[k]
                pltpu.make_async_remote_copy(
                    src_ref=s, dst_ref=dst[k].at[pidx], send_sem=send_sems.at[k, j], recv_sem=recv_sems.at[k, j],
                    device_id=pid, device_id_type=pl.DeviceIdType.MESH).wait_recv()
        for cp in sends:
            cp.wait_send()
        for lc in local:
            lc.wait()

    any_spec = pl.BlockSpec(memory_space=pl.ANY)
    return pl.pallas_call(
        body, name=name, out_shape=tuple(out_shape),
        in_specs=[any_spec] * n, out_specs=tuple([any_spec] * n),
        scratch_shapes=[pltpu.SemaphoreType.DMA((n, N_DEV - 1)), pltpu.SemaphoreType.DMA((n, N_DEV - 1)),
                        pltpu.SemaphoreType.DMA((n,))],
        compiler_params=pltpu.CompilerParams(has_side_effects=True),
    )(*srcs)


def _peer_list():
    x, y, c = lax.axis_index("x"), lax.axis_index("y"), lax.axis_index("c")
    peers = []
    for j in range(1, N_DEV):
        px = 1 - x if (j >> 2) & 1 else x
        py = 1 - y if (j >> 1) & 1 else y
        pc = 1 - c if j & 1 else c
        peers.append(((px, py, pc), 4 * px + 2 * py + pc))
    return 4 * x + 2 * y + c, peers


def _split_copies(src, land, send_sems, recv_sems, scatter_flags, me, peers, incoming):
    copies = []
    for k in range(len(src)):
        for j, (pid, pidx) in enumerate(peers):
            s = src[k].at[pidx] if scatter_flags[k] else src[k]
            i = k * (N_DEV - 1) + j
            copies.append(pltpu.make_async_remote_copy(
                src_ref=s, dst_ref=land[k].at[pidx if incoming else me], send_sem=send_sems[i],
                recv_sem=recv_sems[i], device_id=pid, device_id_type=pl.DeviceIdType.MESH))
    return copies


def _exchange_start(srcs, scatter_flags, after, name, collective_id):
    n = len(srcs)
    ns = n * (N_DEV - 1)
    hbm = pl.BlockSpec(memory_space=pltpu.HBM)
    sem = pl.BlockSpec(memory_space=pltpu.SEMAPHORE)
    land_shapes = [s.shape if sc else (N_DEV,) + s.shape for s, sc in zip(srcs, scatter_flags)]

    def body(*refs):
        src, land = refs[:n], refs[n:2 * n]
        send_sems = refs[2 * n + 1:2 * n + 1 + ns]
        recv_sems = refs[2 * n + 1 + ns:2 * n + 1 + 2 * ns]
        token = refs[4 * n + 1 + 2 * ns]
        me, peers = _peer_list()
        barrier = pltpu.get_barrier_semaphore()
        for pid, _ in peers:
            pl.semaphore_signal(barrier, inc=1, device_id=pid, device_id_type=pl.DeviceIdType.MESH)
        pl.semaphore_wait(barrier, N_DEV - 1)
        for cp in _split_copies(src, land, send_sems, recv_sems, scatter_flags, me, peers, False):
            cp.start()
        token[...] = jnp.zeros_like(token)

    outs = pl.pallas_call(
        body, name=name,
        out_shape=(*[pltpu.SemaphoreType.DMA(())] * (2 * ns), *[pltpu.HBM(s.shape, s.dtype) for s in srcs],
                   *[pltpu.HBM(shp, s.dtype) for shp, s in zip(land_shapes, srcs)],
                   jax.ShapeDtypeStruct((SUBLANES, LANES), F32)),
        in_specs=[hbm] * (2 * n) + [pl.BlockSpec(memory_space=pl.ANY)],
        out_specs=(*[sem] * (2 * ns), *[hbm] * (2 * n), pl.BlockSpec(memory_space=pltpu.VMEM)),
        input_output_aliases={i: 2 * ns + i for i in range(2 * n)},
        compiler_params=pltpu.CompilerParams(has_side_effects=pltpu.SideEffectType.DATAFLOW_SIDE_EFFECTING,
                                             collective_id=collective_id),
    )(*[pltpu.with_memory_space_constraint(s, pltpu.HBM) for s in srcs],
      *[pltpu.with_memory_space_constraint(lax.empty(shp, s.dtype), pltpu.HBM) for shp, s in zip(land_shapes, srcs)],
      after)
    return (outs[:ns], outs[ns:2 * ns], outs[2 * ns:2 * ns + n], outs[2 * ns + n:2 * ns + 2 * n], outs[2 * ns + 2 * n])


def _exchange_wait(send_sems, recv_sems, srcs, lands, scatter_flags, after, name):
    n = len(srcs)
    ns = n * (N_DEV - 1)
    hbm = pl.BlockSpec(memory_space=pltpu.HBM)
    sem = pl.BlockSpec(memory_space=pltpu.SEMAPHORE)

    def body(*refs):
        src, land = refs[:n], refs[n:2 * n]
        s_sems = refs[2 * n:2 * n + ns]
        r_sems = refs[2 * n + ns:2 * n + 2 * ns]
        me, peers = _peer_list()
        for cp in _split_copies(src, land, s_sems, r_sems, scatter_flags, me, peers, True):
            cp.wait_send()
            cp.wait_recv()

    outs = pl.pallas_call(
        body, name=name,
        out_shape=tuple(pltpu.HBM(a.shape, a.dtype) for a in (*srcs, *lands)),
        in_specs=[hbm] * (2 * n) + [sem] * (2 * ns) + [pl.BlockSpec(memory_space=pl.ANY)],
        out_specs=tuple([hbm] * (2 * n)),
        input_output_aliases={i: i for i in range(2 * n)},
        compiler_params=pltpu.CompilerParams(has_side_effects=pltpu.SideEffectType.DATAFLOW_SIDE_EFFECTING),
    )(*srcs, *lands, *send_sems, *recv_sems, after)
    return outs[:n], outs[n:]


def _tn_matmul(a, b, name, tk, tm, out_rows=None, out_cols=None, out_dtype=F32, tn=512):
    n_tok, k_dim = a.shape
    m_dim = b.shape[1]
    grid = (k_dim // tk, m_dim // tm, n_tok // tn)

    def body(a_ref, b_ref, o_ref, acc):
        k = pl.program_id(2)
        part = _dot_tn(a_ref[...].astype(BF16), b_ref[...].astype(BF16))

        @pl.when(k == 0)
        def _():
            acc[...] = part

        @pl.when(k > 0)
        def _():
            acc[...] += part

        @pl.when(k == grid[2] - 1)
        def _():
            o_ref[...] = acc[...].astype(out_dtype)

    return pl.pallas_call(
        body, name=name, grid=grid,
        in_specs=[pl.BlockSpec((tn, tk), lambda i, j, k: (k, i)), pl.BlockSpec((tn, tm), lambda i, j, k: (k, j))],
        out_specs=pl.BlockSpec((tk, tm), lambda i, j, k: (i, j)),
        out_shape=jax.ShapeDtypeStruct((out_rows or k_dim, out_cols or m_dim), out_dtype),
        scratch_shapes=[pltpu.VMEM((tk, tm), F32)],
        compiler_params=_cparams("parallel", "parallel", "arbitrary"),
    )(a, b)


def _adam_math(g, w, m, v):
    m = ADAM_B1 * m + (1.0 - ADAM_B1) * g
    v = ADAM_B2 * v + (1.0 - ADAM_B2) * (g * g)
    m_hat = m / (1.0 - ADAM_B1 ** ADAM_STEP)
    v_hat = v / (1.0 - ADAM_B2 ** ADAM_STEP)
    delta = -ADAM_LR * (m_hat / (jnp.sqrt(v_hat) + ADAM_EPS) + ADAM_WD * w)
    return delta, m, v


def _adam_sharded(land, own, w, m, v, name, tile):
    _, r, c = w.shape

    def body(*refs):
        l_ref = refs[0]
        own_ref = refs[1] if own is not None else None
        w_ref, m_ref, v_ref, g_ref, d_ref, nm_ref, nv_ref = [ref.at[0] for ref in refs[-7:]]
        if own_ref is not None:
            x, y, z = lax.axis_index("x"), lax.axis_index("y"), lax.axis_index("c")
            me = 4 * x + 2 * y + z
            mine = own_ref[...].astype(F32)
        g = None
        for s in range(N_DEV):
            part = l_ref[s].astype(F32)
            if own_ref is not None:
                part = jnp.where(me == s, mine, part)
            g = part if g is None else g + part
        d, nm, nv = _adam_math(g, w_ref[...], m_ref[...], v_ref[...])
        g_ref[...] = g
        d_ref[...] = d
        nm_ref[...] = nm
        nv_ref[...] = nv

    tr, tc = tile
    spec = pl.BlockSpec((1, tr, tc), lambda i, j: (0, i, j))
    own_specs, own_args = ([pl.BlockSpec((tr, tc), lambda i, j: (i, j))], [own]) if own is not None else ([], [])
    return pl.pallas_call(
        body, name=name, grid=(r // tr, c // tc),
        in_specs=[pl.BlockSpec((N_DEV, tr, tc), lambda i, j: (0, i, j)), *own_specs, spec, spec, spec],
        out_specs=(spec, spec, spec, spec),
        out_shape=tuple(jax.ShapeDtypeStruct((1, r, c), F32) for _ in range(4)),
        compiler_params=_cparams("parallel", "parallel"),
    )(land, *own_args, w, m, v)


def _sum_partials(parts, own, name):
    _, r, c = parts.shape

    def body(*refs):
        p_ref, o_ref = refs[0], refs[-1]
        if own is not None:
            x, y, z = lax.axis_index("x"), lax.axis_index("y"), lax.axis_index("c")
            me = 4 * x + 2 * y + z
            mine = refs[1][...]
        g = None
        for s in range(N_DEV):
            part = p_ref[s]
            if own is not None:
                part = jnp.where(me == s, mine, part)
            g = part if g is None else g + part
        o_ref[...] = g

    args = (parts,) if own is None else (parts, own)
    return pl.pallas_call(body, name=name, out_shape=jax.ShapeDtypeStruct((r, c), F32),
                          compiler_params=pltpu.CompilerParams(vmem_limit_bytes=VMEM_LIMIT))(*args)


def _adam_replicated(gs, ws, ms, vs, name):
    k = len(ws)

    def body(*refs):
        outs = refs[4 * k:]
        for i in range(k):
            d, nm, nv = _adam_math(refs[i][...], refs[k + i][...], refs[2 * k + i][...], refs[3 * k + i][...])
            outs[i][...] = d
            outs[k + i][...] = nm
            outs[2 * k + i][...] = nv

    outs = pl.pallas_call(body, name=name, out_shape=tuple(jax.ShapeDtypeStruct(w.shape, F32) for w in ws) * 3,
                          compiler_params=pltpu.CompilerParams(vmem_limit_bytes=VMEM_LIMIT))(*gs, *ws, *ms, *vs)
    return outs[:k], outs[k:2 * k], outs[2 * k:]


def _inproj_fwd(x, g, w_in, avg, qg, kg, tm=512):
    n = x.shape[0]

    def body(x_ref, g_ref, w_ref, a_ref, qg_ref, kg_ref, hn_ref, u_ref, qkv_ref, raw_ref, fl_ref):
        hn = _rms(x_ref[...], g_ref[...]).astype(BF16)
        hn_ref[...] = hn
        proj = _dot_nt(hn, w_ref[...])
        u_ref[...] = proj[:, 0:512]
        q = proj[:, 512:1024]
        k = proj[:, 1024:1536]
        raw_ref[:, 0:512] = q
        raw_ref[:, 512:1024] = k
        qkv_ref[:, 0:512] = _headnorm(q, a_ref[...], qg_ref[...]).astype(BF16)
        qkv_ref[:, 512:1024] = _headnorm(k, a_ref[...], kg_ref[...]).astype(BF16)
        qkv_ref[:, 1024:1536] = proj[:, 1536:2048].astype(BF16)
        fl_ref[...] = proj[:, 2048:D_IN_PAD]

    row = lambda w: pl.BlockSpec((tm, w), lambda i: (i, 0))
    full = lambda a: pl.BlockSpec(a.shape, lambda i: (0,) * a.ndim)
    return pl.pallas_call(
        body, name="inproj_fwd", grid=(n // tm,),
        in_specs=[row(D_MODEL), full(g), full(w_in), full(avg), full(qg), full(kg)],
        out_specs=(row(D_MODEL), row(512), row(1536), row(1024), row(LANES)),
        out_shape=(jax.ShapeDtypeStruct((n, D_MODEL), BF16), jax.ShapeDtypeStruct((n, 512), F32),
                   jax.ShapeDtypeStruct((n, 1536), BF16), jax.ShapeDtypeStruct((n, 1024), F32),
                   jax.ShapeDtypeStruct((n, LANES), F32)),
        compiler_params=_cparams("parallel"),
    )(x, g, w_in, avg, qg, kg)


def _inproj_bwd(x, g, w_in, avg, qg, kg, raw, du, dqn, dkn, dv, dfl, dres, tm=512):
    n = x.shape[0]

    def body(x_ref, g_ref, w_ref, a_ref, qg_ref, kg_ref, raw_ref, du_ref, dqn_ref, dkn_ref, dv_ref, dfl_ref, dres_ref,
             dx_ref, dproj_ref, dg_ref, dqg_ref, dkg_ref):
        @pl.when(pl.program_id(0) == 0)
        def _():
            dg_ref[...] = jnp.zeros_like(dg_ref)
            dqg_ref[...] = jnp.zeros_like(dqg_ref)
            dkg_ref[...] = jnp.zeros_like(dkg_ref)
        avg_m = a_ref[...]
        _, vjp_q = jax.vjp(lambda q, gg: _headnorm(q, avg_m, gg), raw_ref[:, 0:512], qg_ref[...])
        dq, dqg = vjp_q(dqn_ref[...])
        _, vjp_k = jax.vjp(lambda k, gg: _headnorm(k, avg_m, gg), raw_ref[:, 512:1024], kg_ref[...])
        dk, dkg = vjp_k(dkn_ref[...])
        dproj = jnp.concatenate([du_ref[...], dq, dk, dv_ref[...], dfl_ref[...]], axis=1).astype(BF16)
        dproj_ref[...] = dproj
        dhn = _dot(dproj, w_ref[...])
        _, vjp_x = jax.vjp(_rms, x_ref[...], g_ref[...])
        dxn, dg = vjp_x(dhn)
        dx_ref[...] = dxn + dres_ref[...]
        dg_ref[...] += dg
        dqg_ref[...] += dqg
        dkg_ref[...] += dkg

    row = lambda w: pl.BlockSpec((tm, w), lambda i: (i, 0))
    full = lambda a: pl.BlockSpec(a.shape, lambda i: (0,) * a.ndim)
    vec = lambda w: pl.BlockSpec((1, w), lambda i: (0, 0))
    return pl.pallas_call(
        body, name="inproj_bwd", grid=(n // tm,),
        in_specs=[row(D_MODEL), full(g), full(w_in), full(avg), full(qg), full(kg), row(1024), row(512), row(512),
                  row(512), row(512), row(LANES), row(D_MODEL)],
        out_specs=(row(D_MODEL), row(D_IN_PAD), vec(D_MODEL), vec(512), vec(512)),
        out_shape=(jax.ShapeDtypeStruct((n, D_MODEL), F32), jax.ShapeDtypeStruct((n, D_IN_PAD), BF16),
                   jax.ShapeDtypeStruct((1, D_MODEL), F32), jax.ShapeDtypeStruct((1, 512), F32),
                   jax.ShapeDtypeStruct((1, 512), F32)),
        compiler_params=_cparams("arbitrary"),
    )(x, g, w_in, avg, qg, kg, raw, du, dqn, dkn, dv, dfl, dres)


def _glu_fwd(yc, wg, bg, tm=512):
    n = yc.shape[0]

    def body(yc_ref, w_ref, b_ref, ys_ref):
        gl = jax.nn.gelu(yc_ref[...])
        z = _dot(gl.astype(BF16), w_ref[...]) + b_ref[...]
        ys_ref[...] = gl * jax.nn.sigmoid(z)

    row = pl.BlockSpec((tm, 512), lambda i: (i, 0))
    full = lambda a: pl.BlockSpec(a.shape, lambda i: (0,) * a.ndim)
    return pl.pallas_call(
        body, name="glu_fwd", grid=(n // tm,), in_specs=[row, full(wg), full(bg)], out_specs=row,
        out_shape=jax.ShapeDtypeStruct((n, 512), F32), compiler_params=_cparams("parallel"),
    )(yc, wg, bg)


def _glu_bwd(yc, dys, wg, bg, tm=512):
    n = yc.shape[0]

    def body(yc_ref, dys_ref, w_ref, b_ref, dyc_ref, gl_ref, dz_ref, db_ref):
        @pl.when(pl.program_id(0) == 0)
        def _():
            db_ref[...] = jnp.zeros_like(db_ref)
        gl, vjp_gelu = jax.vjp(jax.nn.gelu, yc_ref[...])
        glb = gl.astype(BF16)
        z = _dot(glb, w_ref[...]) + b_ref[...]
        s = jax.nn.sigmoid(z)
        dys = dys_ref[...]
        dz = dys * gl * s * (1.0 - s)
        dzb = dz.astype(BF16)
        dgl = dys * s + _dot_nt(dzb, w_ref[...])
        dyc_ref[...] = vjp_gelu(dgl)[0]
        gl_ref[...] = glb
        dz_ref[...] = dzb
        db_ref[...] += jnp.sum(dz, axis=0, keepdims=True)

    row = pl.BlockSpec((tm, 512), lambda i: (i, 0))
    full = lambda a: pl.BlockSpec(a.shape, lambda i: (0,) * a.ndim)
    return pl.pallas_call(
        body, name="glu_bwd", grid=(n // tm,), in_specs=[row, row, full(wg), full(bg)],
        out_specs=(row, row, row, pl.BlockSpec((1, 512), lambda i: (0, 0))),
        out_shape=(jax.ShapeDtypeStruct((n, 512), F32), jax.ShapeDtypeStruct((n, 512), BF16),
                   jax.ShapeDtypeStruct((n, 512), BF16), jax.ShapeDtypeStruct((1, 512), F32)),
        compiler_params=_cparams("arbitrary"),
    )(yc, dys, wg, bg)


def _mix_fwd(x, ys, ya, gs, ga, wout, gf, tm=512):
    n = x.shape[0]

    def body(x_ref, ys_ref, ya_ref, gs_ref, ga_ref, w_ref, gf_ref, h1_ref, hn2_ref, mixed_ref):
        mixed = jnp.concatenate([_rms(ys_ref[...], gs_ref[...]), _rms(ya_ref[...], ga_ref[...])], axis=1).astype(BF16)
        mixed_ref[...] = mixed
        h1 = x_ref[...] + _dot(mixed, w_ref[...])
        h1_ref[...] = h1
        hn2_ref[...] = _rms(h1, gf_ref[...]).astype(BF16)

    row = lambda w: pl.BlockSpec((tm, w), lambda i: (i, 0))
    full = lambda a: pl.BlockSpec(a.shape, lambda i: (0,) * a.ndim)
    return pl.pallas_call(
        body, name="mix_fwd", grid=(n // tm,),
        in_specs=[row(D_MODEL), row(512), row(512), full(gs), full(ga), full(wout), full(gf)],
        out_specs=(row(D_MODEL), row(D_MODEL), row(D_MODEL)),
        out_shape=(jax.ShapeDtypeStruct((n, D_MODEL), F32), jax.ShapeDtypeStruct((n, D_MODEL), BF16),
                   jax.ShapeDtypeStruct((n, D_MODEL), BF16)),
        compiler_params=_cparams("parallel"),
    )(x, ys, ya, gs, ga, wout, gf)


def _mix_bwd(dy, dhn2_parts, h1, ys, ya, gs, ga, wout, gf, tm=512):
    n = dy.shape[0]
    n_parts = dhn2_parts.shape[0]

    def body(dy_ref, dp_ref, h1_ref, ys_ref, ya_ref, gs_ref, ga_ref, w_ref, gf_ref,
             dh1_ref, dys_ref, dya_ref, dgs_ref, dga_ref, dgf_ref):
        @pl.when(pl.program_id(0) == 0)
        def _():
            dgs_ref[...] = jnp.zeros_like(dgs_ref)
            dga_ref[...] = jnp.zeros_like(dga_ref)
            dgf_ref[...] = jnp.zeros_like(dgf_ref)
        dhn2 = dp_ref[0]
        for p in range(1, n_parts):
            dhn2 = dhn2 + dp_ref[p]
        _, vjp_f = jax.vjp(_rms, h1_ref[...], gf_ref[...])
        dh1n, dgf = vjp_f(dhn2)
        dh1 = dy_ref[...] + dh1n
        dh1_ref[...] = dh1
        dmixed = _dot_nt(dh1.astype(BF16), w_ref[...])
        _, vjp_s = jax.vjp(_rms, ys_ref[...], gs_ref[...])
        dys, dgs = vjp_s(dmixed[:, 0:512])
        _, vjp_a = jax.vjp(_rms, ya_ref[...], ga_ref[...])
        dya, dga = vjp_a(dmixed[:, 512:1024])
        dys_ref[...] = dys
        dya_ref[...] = dya
        dgs_ref[...] += dgs
        dga_ref[...] += dga
        dgf_ref[...] += dgf

    row = lambda w: pl.BlockSpec((tm, w), lambda i: (i, 0))
    full = lambda a: pl.BlockSpec(a.shape, lambda i: (0,) * a.ndim)
    vec = lambda w: pl.BlockSpec((1, w), lambda i: (0, 0))
    return pl.pallas_call(
        body, name="mix_bwd", grid=(n // tm,),
        in_specs=[row(D_MODEL), pl.BlockSpec((n_parts, tm, D_MODEL), lambda i: (0, i, 0)), row(D_MODEL), row(512),
                  row(512), full(gs), full(ga), full(wout), full(gf)],
        out_specs=(row(D_MODEL), row(512), row(512), vec(512), vec(512), vec(D_MODEL)),
        out_shape=(jax.ShapeDtypeStruct((n, D_MODEL), F32), jax.ShapeDtypeStruct((n, 512), F32),
                   jax.ShapeDtypeStruct((n, 512), F32), jax.ShapeDtypeStruct((1, 512), F32),
                   jax.ShapeDtypeStruct((1, 512), F32), jax.ShapeDtypeStruct((1, D_MODEL), F32)),
        compiler_params=_cparams("arbitrary"),
    )(dy, dhn2_parts, h1, ys, ya, gs, ga, wout, gf)


HALO = 16
FFN_GROUPS = 4
FFN_GROUP = D_FF // FFN_GROUPS


def _conv3(ue, cw):
    return cw[2:3] * ue + cw[1:2] * pltpu.roll(ue, 1, 0) + cw[0:1] * pltpu.roll(ue, 2, 0) + cw[3:4]


def _ffn_weight_specs():
    gate = lambda i, j: (j, 0, 0)
    val = lambda i, j: (j + FFN_GROUPS, 0, 0)
    w_blk, c_blk = (1, FFN_GROUP, D_MODEL), (1, SUBLANES, FFN_GROUP)
    return [pl.BlockSpec(w_blk, gate), pl.BlockSpec(w_blk, val), pl.BlockSpec(c_blk, gate), pl.BlockSpec(c_blk, val),
            pl.BlockSpec((1, FFN_GROUP, D_MODEL), gate)]


def _ffn_fwd(hn2, h1, target, w_up, conv, w_down, seq_len, tm=512):
    n = hn2.shape[0]
    nj = FFN_GROUPS
    hb = tm // HALO

    def body(hn_ref, halo_ref, h1_ref, tgt_ref, wg_ref, wv_ref, cg_ref, cv_ref, wd_ref,
             ug_ref, uv_ref, pg_ref, pv_ref, dy_ref, loss_ref, acc):
        i, j = pl.program_id(0), pl.program_id(1)
        seq_start = (i * tm) % seq_len == 0
        halo = halo_ref[...]
        halo = jnp.where(seq_start, jnp.zeros_like(halo), halo)
        he = jnp.concatenate([halo, hn_ref[...]], axis=0)
        ueg = _dot_nt(he, wg_ref[0])
        uev = _dot_nt(he, wv_ref[0])
        ug_ref[0] = ueg[HALO:].astype(BF16)
        uv_ref[0] = uev[HALO:].astype(BF16)
        cg = _conv3(ueg, cg_ref[0])[HALO:]
        cv = _conv3(uev, cv_ref[0])[HALO:]
        pg_ref[0] = cg.astype(BF16)
        pv_ref[0] = cv.astype(BF16)
        act = (jax.nn.silu(cg) * cv).astype(BF16)
        part = _dot(act, wd_ref[0])

        @pl.when(j == 0)
        def _():
            acc[...] = part

        @pl.when(j > 0)
        def _():
            acc[...] += part

        @pl.when(j == nj - 1)
        def _():
            err = h1_ref[...] + acc[...] - tgt_ref[...]
            dy_ref[...] = err * (1.0 / D_MODEL)
            loss_ref[0] = jnp.sum(err * err, axis=0, keepdims=True)

    row = pl.BlockSpec((tm, D_MODEL), lambda i, j: (i, 0))
    u_main = pl.BlockSpec((1, tm, FFN_GROUP), lambda i, j: (j, i, 0))
    u_shape = jax.ShapeDtypeStruct((FFN_GROUPS, n, FFN_GROUP), BF16)
    return pl.pallas_call(
        body, name="ffn_fwd", grid=(n // tm, nj),
        in_specs=[row, pl.BlockSpec((HALO, D_MODEL), lambda i, j: (jnp.maximum(i * hb - 1, 0), 0)), row, row,
                  *_ffn_weight_specs()],
        out_specs=(u_main, u_main, u_main, u_main, row, pl.BlockSpec((1, 1, D_MODEL), lambda i, j: (i, 0, 0))),
        out_shape=(u_shape, u_shape, u_shape, u_shape, jax.ShapeDtypeStruct((n, D_MODEL), F32),
                   jax.ShapeDtypeStruct((n // tm, 1, D_MODEL), F32)),
        scratch_shapes=[pltpu.VMEM((tm, D_MODEL), F32)],
        compiler_params=_cparams("parallel", "arbitrary"),
    )(hn2, hn2, h1, target, w_up, w_up, conv, conv, w_down)


def _ffn_bwd(dy, ug, uv, pg, pv, w_up, conv, w_down, seq_len, tm=512):
    n = dy.shape[0]
    nj = FFN_GROUPS
    fb = FFN_GROUP
    hb = tm // HALO
    last_hb = n // HALO - 1
    rows = tm + HALO

    def body(dy_ref, dyn_ref, ug_ref, uv_ref, pgm_ref, pgn_ref, pvm_ref, pvn_ref, wg_ref, wv_ref, cg_ref, cv_ref,
             wd_ref, dug_ref, duv_ref, act_ref, dhn_ref, dcg_ref, dcv_ref, acc):
        i, j = pl.program_id(0), pl.program_id(1)
        seq_end = ((i + 1) * tm) % seq_len == 0
        dyn = dyn_ref[...]
        dyn = jnp.where(seq_end, jnp.zeros_like(dyn), dyn)
        d_out = jnp.concatenate([dy_ref[...], dyn], axis=0).astype(BF16)
        d_act = _dot_nt(d_out, wd_ref[0])
        cge = jnp.concatenate([pgm_ref[0], pgn_ref[0]], axis=0).astype(F32)
        cve = jnp.concatenate([pvm_ref[0], pvn_ref[0]], axis=0).astype(F32)
        act, vjp_act = jax.vjp(lambda g, v: jax.nn.silu(g) * v, cge, cve)
        dcge, dcve = vjp_act(d_act)
        act_ref[0] = act[:tm].astype(BF16)

        def conv_t(dc, u_ref, cw):
            ahead1 = pltpu.roll(dc, rows - 1, 0)[:tm]
            ahead2 = pltpu.roll(dc, rows - 2, 0)[:tm]
            here = dc[:tm]
            du = cw[2:3] * here + cw[1:2] * ahead1 + cw[0:1] * ahead2
            u = u_ref[0].astype(F32)
            col = lambda x: jnp.sum(x, axis=0, keepdims=True)
            grad = jnp.concatenate([col(ahead2 * u), col(ahead1 * u), col(here * u), col(here),
                                    jnp.zeros((4, fb), F32)], axis=0)
            return du.astype(BF16), grad

        cwg, cwv = cg_ref[0], cv_ref[0]
        dug, grad_g = conv_t(dcge, ug_ref, cwg)
        duv, grad_v = conv_t(dcve, uv_ref, cwv)
        dug_ref[0] = dug
        duv_ref[0] = duv
        part = _dot(dug, wg_ref[0]) + _dot(duv, wv_ref[0])

        @pl.when(j == 0)
        def _():
            acc[...] = part

        @pl.when(j > 0)
        def _():
            acc[...] += part

        @pl.when(j == nj - 1)
        def _():
            dhn_ref[...] = acc[...]

        @pl.when(i == 0)
        def _():
            dcg_ref[j] = jnp.zeros((8, fb), F32)
            dcv_ref[j] = jnp.zeros((8, fb), F32)

        dcg_ref[j] += grad_g
        dcv_ref[j] += grad_v

    row = pl.BlockSpec((tm, D_MODEL), lambda i, j: (i, 0))
    u_main = pl.BlockSpec((1, tm, fb), lambda i, j: (j, i, 0))
    u_next = pl.BlockSpec((1, HALO, fb), lambda i, j: (j, jnp.minimum((i + 1) * hb, last_hb), 0))
    dc_spec = pl.BlockSpec((nj, 8, fb), lambda i, j: (0, 0, 0))
    u_shape = jax.ShapeDtypeStruct((FFN_GROUPS, n, fb), BF16)
    return pl.pallas_call(
        body, name="ffn_bwd", grid=(n // tm, nj),
        in_specs=[row, pl.BlockSpec((HALO, D_MODEL), lambda i, j: (jnp.minimum((i + 1) * hb, last_hb), 0)),
                  u_main, u_main, u_main, u_next, u_main, u_next, *_ffn_weight_specs()],
        out_specs=(u_main, u_main, u_main, row, dc_spec, dc_spec),
        out_shape=(u_shape, u_shape, u_shape, jax.ShapeDtypeStruct((n, D_MODEL), F32),
                   jax.ShapeDtypeStruct((nj, 8, fb), F32), jax.ShapeDtypeStruct((nj, 8, fb), F32)),
        scratch_shapes=[pltpu.VMEM((tm, D_MODEL), F32)],
        compiler_params=_cparams("arbitrary", "arbitrary"),
    )(dy, dy, ug, uv, pg, pg, pv, pv, w_up, w_up, conv, conv, w_down)


def _tn_grouped(a, b, name, shared_a, out_dtype=F32, tn=1024):
    groups = b.shape[0] if shared_a else a.shape[0]
    n_tok = a.shape[0] if shared_a else b.shape[0]
    k_dim, m_dim = a.shape[-1], b.shape[-1]

    def body(a_ref, b_ref, o_ref, acc):
        k = pl.program_id(1)
        a_t = a_ref[...] if shared_a else a_ref[0]
        b_t = b_ref[0] if shared_a else b_ref[...]
        part = _dot_tn(a_t.astype(BF16), b_t.astype(BF16))

        @pl.when(k == 0)
        def _():
            acc[...] = part

        @pl.when(k > 0)
        def _():
            acc[...] += part

        @pl.when(k == n_tok // tn - 1)
        def _():
            o_ref[0] = acc[...].astype(out_dtype)

    plain = lambda w: pl.BlockSpec((tn, w), lambda g, k: (k, 0))
    grouped = lambda w: pl.BlockSpec((1, tn, w), lambda g, k: (g, k, 0))
    return pl.pallas_call(
        body, name=name, grid=(groups, n_tok // tn),
        in_specs=[plain(k_dim), grouped(m_dim)] if shared_a else [grouped(k_dim), plain(m_dim)],
        out_specs=pl.BlockSpec((1, k_dim, m_dim), lambda g, k: (g, 0, 0)),
        out_shape=jax.ShapeDtypeStruct((groups, k_dim, m_dim), out_dtype),
        scratch_shapes=[pltpu.VMEM((k_dim, m_dim), F32)],
        compiler_params=_cparams("parallel", "arbitrary"),
    )(a, b)


def _s5_param_fn(lr, li, ldt, br, bi):
    dt = jnp.exp(ldt)
    mag = jnp.exp(lr * dt)
    ab_re = mag * jnp.cos(li * dt)
    ab_im = mag * jnp.sin(li * dt)
    nr = ab_re - 1.0
    ni = ab_im
    den = lr * lr + li * li
    q_re = (nr * lr + ni * li) / den
    q_im = (ni * lr - nr * li) / den
    bb_re = q_re * br - q_im * bi
    bb_im = q_re * bi + q_im * br
    return ab_re, ab_im, bb_re, bb_im


def _s5_param_fwd(lr, li, ldt, br, bi):
    def body(lr_ref, li_ref, ldt_ref, br_ref, bi_ref, ar_ref, ai_ref, bbr_ref, bbi_ref):
        ar, ai, bbr, bbi = _s5_param_fn(lr_ref[...], li_ref[...], ldt_ref[...], br_ref[...], bi_ref[...])
        ar_ref[...] = ar
        ai_ref[...] = ai
        bbr_ref[...] = bbr
        bbi_ref[...] = bbi

    return pl.pallas_call(
        body, name="s5_param_fwd",
        out_shape=(jax.ShapeDtypeStruct(lr.shape, F32), jax.ShapeDtypeStruct(lr.shape, F32),
                   jax.ShapeDtypeStruct(br.shape, F32), jax.ShapeDtypeStruct(br.shape, F32)),
    )(lr, li, ldt, br, bi)


def _s5_param_bwd(lr, li, ldt, br, bi, dar, dai, dbbr, dbbi):
    def body(lr_ref, li_ref, ldt_ref, br_ref, bi_ref, dar_ref, dai_ref, dbbr_ref, dbbi_ref,
             dlr_ref, dli_ref, dldt_ref, dbr_ref, dbi_ref):
        _, vjp = jax.vjp(_s5_param_fn, lr_ref[...], li_ref[...], ldt_ref[...], br_ref[...], bi_ref[...])
        dlr, dli, dldt, dbr, dbi = vjp((dar_ref[...], dai_ref[...], dbbr_ref[...], dbbi_ref[...]))
        dlr_ref[...] = dlr
        dli_ref[...] = dli
        dldt_ref[...] = dldt
        dbr_ref[...] = dbr
        dbi_ref[...] = dbi

    return pl.pallas_call(
        body, name="s5_param_bwd",
        out_shape=(jax.ShapeDtypeStruct(lr.shape, F32), jax.ShapeDtypeStruct(lr.shape, F32),
                   jax.ShapeDtypeStruct(ldt.shape, F32), jax.ShapeDtypeStruct(br.shape, F32),
                   jax.ShapeDtypeStruct(br.shape, F32)),
    )(lr, li, ldt, br, bi, dar, dai, dbbr, dbbi)


S5_CHUNK = 256
S5_STATES = 512
S5_BLOCKS = 4


def _cpow_rows(ar, ai, count):
    rs, im = [ar], [ai]
    for _ in range(count - 1):
        pr, pi = rs[-1], im[-1]
        rs.append(pr * ar - pi * ai)
        im.append(pr * ai + pi * ar)
    return rs, im


def _scan_in_groups(vr, vi, pr, pi, rm, reverse):
    n, width = vr.shape
    vr = vr.reshape(n // SUBLANES, SUBLANES, width)
    vi = vi.reshape(n // SUBLANES, SUBLANES, width)
    row = rm[0:SUBLANES]
    for k in (1, 2, 4):
        shift = SUBLANES - k if reverse else k
        keep = row < SUBLANES - k if reverse else row >= k
        kr = jnp.where(keep, pr[k - 1], 0.0)
        ki = jnp.where(keep, pi[k - 1], 0.0)
        sr, si = pltpu.roll(vr, shift, 1), pltpu.roll(vi, shift, 1)
        vr, vi = vr + kr * sr - ki * si, vi + kr * si + ki * sr
    return vr.reshape(n, width), vi.reshape(n, width)


def _carry_over_groups(xr_s, xi_s, wr, wi, c0r, c0i, reverse):
    groups = xr_s.shape[0] // SUBLANES
    pick = 0 if reverse else SUBLANES - 1

    def step(q, carry):
        cr, ci = carry
        r = groups - 1 - q if reverse else q
        o = pl.multiple_of(r * SUBLANES, SUBLANES)
        vr = xr_s[pl.ds(o, SUBLANES), :]
        vi = xi_s[pl.ds(o, SUBLANES), :]
        nr = vr + wr * cr - wi * ci
        ni = vi + wr * ci + wi * cr
        xr_s[pl.ds(o, SUBLANES), :] = nr
        xi_s[pl.ds(o, SUBLANES), :] = ni
        return (jnp.broadcast_to(nr[pick:pick + 1], nr.shape), jnp.broadcast_to(ni[pick:pick + 1], ni.shape))

    return lax.fori_loop(0, groups, step, (c0r, c0i), unroll=4)


def _s5_state_scan(u_b, bbr, bbi, pr, pi, rm, xr_s, xi_s, c0r, c0i):
    bur = _dot(u_b, bbr)
    bui = _dot(u_b, bbi)
    bur, bui = _scan_in_groups(bur, bui, pr, pi, rm, False)
    xr_s[...] = bur
    xi_s[...] = bui
    w8r = jnp.concatenate(pr, axis=0)
    w8i = jnp.concatenate(pi, axis=0)
    return _carry_over_groups(xr_s, xi_s, w8r, w8i, c0r, c0i, False)


def _s5_fwd(u, a_re, a_im, bbr, bbi, cr, ci, d_skip, n_seq):
    n = u.shape[0]
    seq_len = n // n_seq
    nt = seq_len // S5_CHUNK
    tc = S5_CHUNK

    def body(u_ref, ar_ref, ai_ref, bbr_ref, bbi_ref, cr_ref, ci_ref, d_ref, y_ref, str_ref, sti_ref, xrb_ref, xib_ref,
             xr_s, xi_s, car_r, car_i):
        t = pl.program_id(2)

        @pl.when(t == 0)
        def _():
            car_r[...] = jnp.zeros_like(car_r)
            car_i[...] = jnp.zeros_like(car_i)
        pr, pi = _cpow_rows(ar_ref[0], ai_ref[0], SUBLANES)
        rm = lax.broadcasted_iota(jnp.int32, (tc, S5_STATES), 0) & (SUBLANES - 1)
        str_ref[0, 0] = car_r[...]
        sti_ref[0, 0] = car_i[...]
        u_t = u_ref[...]
        cfr, cfi = _s5_state_scan(u_t.astype(BF16), bbr_ref[0], bbi_ref[0], pr, pi, rm, xr_s, xi_s,
                                  car_r[...], car_i[...])
        car_r[...] = cfr
        car_i[...] = cfi
        xr_b = xr_s[...].astype(BF16)
        xi_b = xi_s[...].astype(BF16)
        xrb_ref[...] = xr_b
        xib_ref[...] = xi_b
        y_ref[...] = _dot(xr_b, cr_ref[0]) - _dot(xi_b, ci_ref[0]) + d_ref[...] * u_t

    x_spec = pl.BlockSpec((tc, S5_STATES), lambda cb, b, t: (b * nt + t, cb))
    x_shape = jax.ShapeDtypeStruct((n, S5_BLOCKS * S5_STATES), BF16)
    u_spec = pl.BlockSpec((tc, LANES), lambda cb, b, t: (b * nt + t, cb))
    a_spec = pl.BlockSpec((1, 1, S5_STATES), lambda cb, b, t: (cb, 0, 0))
    bb_spec = pl.BlockSpec((1, LANES, S5_STATES), lambda cb, b, t: (cb, 0, 0))
    c_spec = pl.BlockSpec((1, S5_STATES, LANES), lambda cb, b, t: (cb, 0, 0))
    st_spec = pl.BlockSpec((1, 1, SUBLANES, S5_STATES), lambda cb, b, t: (cb, b * nt + t, 0, 0))
    st_shape = jax.ShapeDtypeStruct((S5_BLOCKS, n_seq * nt, SUBLANES, S5_STATES), F32)
    return pl.pallas_call(
        body, name="s5_fwd", grid=(S5_BLOCKS, n_seq, nt),
        in_specs=[u_spec, a_spec, a_spec, bb_spec, bb_spec, c_spec, c_spec,
                  pl.BlockSpec((1, LANES), lambda cb, b, t: (0, cb))],
        out_specs=(u_spec, st_spec, st_spec, x_spec, x_spec),
        out_shape=(jax.ShapeDtypeStruct((n, D_SSM), F32), st_shape, st_shape, x_shape, x_shape),
        scratch_shapes=[pltpu.VMEM((tc, S5_STATES), F32), pltpu.VMEM((tc, S5_STATES), F32),
                        pltpu.VMEM((SUBLANES, S5_STATES), F32), pltpu.VMEM((SUBLANES, S5_STATES), F32)],
        compiler_params=_cparams("parallel", "arbitrary", "arbitrary"),
    )(u, a_re, a_im, bbr, bbi, cr, ci, d_skip)


def _s5_bwd(u, dy, xs_r, xs_i, st_r, st_i, a_re, a_im, bbr, bbi, cr, ci, d_skip, n_seq):
    n = u.shape[0]
    seq_len = n // n_seq
    nt = seq_len // S5_CHUNK
    tc = S5_CHUNK

    def body(u_ref, dy_ref, xrb_ref, xib_ref, str_ref, sti_ref, ar_ref, ai_ref, bbr_ref, bbi_ref, cr_ref, ci_ref, d_ref,
             du_ref, dbbr_ref, dbbi_ref, dcr_ref, dci_ref, dar_ref, dai_ref, dd_ref,
             gr_s, gi_s, car_r, car_i):
        b, t = pl.program_id(1), pl.program_id(2)

        @pl.when((b == 0) & (t == 0))
        def _():
            for ref in (dbbr_ref, dbbi_ref, dcr_ref, dci_ref, dar_ref, dai_ref, dd_ref):
                ref[...] = jnp.zeros_like(ref)

        @pl.when(t == 0)
        def _():
            car_r[...] = jnp.zeros_like(car_r)
            car_i[...] = jnp.zeros_like(car_i)
        ar, ai = ar_ref[0], ai_ref[0]
        pr, pi = _cpow_rows(ar, ai, SUBLANES)
        row = lax.broadcasted_iota(jnp.int32, (tc, S5_STATES), 0)
        rm = row & (SUBLANES - 1)
        u_t = u_ref[...]
        u_b = u_t.astype(BF16)
        dy_t = dy_ref[...]
        dy_b = dy_t.astype(BF16)
        s0r, s0i = str_ref[0, 0], sti_ref[0, 0]
        xr_b, xi_b = xrb_ref[...], xib_ref[...]
        xr, xi = xr_b.astype(F32), xi_b.astype(F32)
        gr = _dot_nt(dy_b, cr_ref[0])
        gi = -_dot_nt(dy_b, ci_ref[0])
        npi = [-v for v in pi]
        gr, gi = _scan_in_groups(gr, gi, pr, npi, rm, True)
        gr_s[...] = gr
        gi_s[...] = gi
        w8r = jnp.concatenate(pr[::-1], axis=0)
        w8i = jnp.concatenate(npi[::-1], axis=0)
        cfr, cfi = _carry_over_groups(gr_s, gi_s, w8r, w8i, car_r[...], car_i[...], True)
        car_r[...] = cfr
        car_i[...] = cfi
        gr, gi = gr_s[...], gi_s[...]
        gr_b, gi_b = gr.astype(BF16), gi.astype(BF16)
        du_ref[...] = _dot_nt(gr_b, bbr_ref[0]) + _dot_nt(gi_b, bbi_ref[0]) + d_ref[...] * dy_t
        dbbr_ref[0] += _dot_tn(u_b, gr_b)
        dbbi_ref[0] += _dot_tn(u_b, gi_b)
        dcr_ref[0] += _dot_tn(xr_b, dy_b)
        dci_ref[0] -= _dot_tn(xi_b, dy_b)
        dd_ref[0] += jnp.sum((dy_t * u_t).reshape(tc // SUBLANES, SUBLANES, LANES), axis=0)
        first = row == 0
        xpr = jnp.where(first, jnp.broadcast_to(s0r[0:1], xr.shape), pltpu.roll(xr, 1, 0))
        xpi = jnp.where(first, jnp.broadcast_to(s0i[0:1], xi.shape), pltpu.roll(xi, 1, 0))
        shp = (tc // SUBLANES, SUBLANES, S5_STATES)
        dar_ref[0] += jnp.sum((gr * xpr + gi * xpi).reshape(shp), axis=0)
        dai_ref[0] += jnp.sum((gi * xpr - gr * xpi).reshape(shp), axis=0)

    u_spec = pl.BlockSpec((tc, LANES), lambda cb, b, t: (b * nt + nt - 1 - t, cb))
    a_spec = pl.BlockSpec((1, 1, S5_STATES), lambda cb, b, t: (cb, 0, 0))
    bb_spec = pl.BlockSpec((1, LANES, S5_STATES), lambda cb, b, t: (cb, 0, 0))
    c_spec = pl.BlockSpec((1, S5_STATES, LANES), lambda cb, b, t: (cb, 0, 0))
    st_spec = pl.BlockSpec((1, 1, SUBLANES, S5_STATES), lambda cb, b, t: (cb, b * nt + nt - 1 - t, 0, 0))
    da_spec = pl.BlockSpec((1, SUBLANES, S5_STATES), lambda cb, b, t: (cb, 0, 0))
    dd_spec = pl.BlockSpec((1, SUBLANES, LANES), lambda cb, b, t: (cb, 0, 0))
    big = pltpu.VMEM((tc, S5_STATES), F32)
    small = pltpu.VMEM((SUBLANES, S5_STATES), F32)
    x_spec = pl.BlockSpec((tc, S5_STATES), lambda cb, b, t: (b * nt + nt - 1 - t, cb))
    return pl.pallas_call(
        body, name="s5_bwd", grid=(S5_BLOCKS, n_seq, nt),
        in_specs=[u_spec, u_spec, x_spec, x_spec, st_spec, st_spec, a_spec, a_spec, bb_spec, bb_spec, c_spec, c_spec,
                  pl.BlockSpec((1, LANES), lambda cb, b, t: (0, cb))],
        out_specs=(u_spec, bb_spec, bb_spec, c_spec, c_spec, da_spec, da_spec, dd_spec),
        out_shape=(jax.ShapeDtypeStruct((n, D_SSM), F32),
                   jax.ShapeDtypeStruct((S5_BLOCKS, LANES, S5_STATES), F32),
                   jax.ShapeDtypeStruct((S5_BLOCKS, LANES, S5_STATES), F32),
                   jax.ShapeDtypeStruct((S5_BLOCKS, S5_STATES, LANES), F32),
                   jax.ShapeDtypeStruct((S5_BLOCKS, S5_STATES, LANES), F32),
                   jax.ShapeDtypeStruct((S5_BLOCKS, SUBLANES, S5_STATES), F32),
                   jax.ShapeDtypeStruct((S5_BLOCKS, SUBLANES, S5_STATES), F32),
                   jax.ShapeDtypeStruct((S5_BLOCKS, SUBLANES, LANES), F32)),
        scratch_shapes=[big, big, small, small],
        compiler_params=_cparams("parallel", "arbitrary", "arbitrary"),
    )(u, dy, xs_r, xs_i, st_r, st_i, a_re, a_im, bbr, bbi, cr, ci, d_skip)


CUM_BLOCK = 128


def _tri(lower):
    r = lax.broadcasted_iota(jnp.int32, (CUM_BLOCK, CUM_BLOCK), 0)
    c = lax.broadcasted_iota(jnp.int32, (CUM_BLOCK, CUM_BLOCK), 1)
    return jnp.where(r >= c if lower else r <= c, 1.0, 0.0).astype(F32)


def _fprep_fwd(fl, bf, n_seq):
    n = fl.shape[0]
    seq_len = n // n_seq
    nb = seq_len // CUM_BLOCK

    def body(fl_ref, bf_ref, cum_ref):
        tril = _tri(True)
        carry = jnp.zeros((1, LANES), F32)
        for blk in range(nb):
            rows = slice(blk * CUM_BLOCK, (blk + 1) * CUM_BLOCK)
            lf = jax.nn.log_sigmoid(fl_ref[rows, :] + bf_ref[...])
            cs = jnp.dot(tril, lf, preferred_element_type=F32, precision=HIGHEST) + carry
            cum_ref[rows, :] = cs
            carry = cs[CUM_BLOCK - 1:CUM_BLOCK, :]

    spec = pl.BlockSpec((seq_len, LANES), lambda b: (b, 0))
    return pl.pallas_call(
        body, name="fprep_fwd", grid=(n_seq,), in_specs=[spec, pl.BlockSpec((1, LANES), lambda b: (0, 0))],
        out_specs=spec, out_shape=jax.ShapeDtypeStruct((n, LANES), F32), compiler_params=_cparams("parallel"),
    )(fl, bf)


def _fprep_bwd(dcum, fl, bf, n_seq):
    n = fl.shape[0]
    seq_len = n // n_seq
    nb = seq_len // CUM_BLOCK

    def body(dcum_ref, fl_ref, bf_ref, dfl_ref, dbf_ref):
        triu = _tri(False)
        lane = lax.broadcasted_iota(jnp.int32, (CUM_BLOCK, LANES), 1)
        carry = jnp.zeros((1, LANES), F32)
        total = jnp.zeros((1, LANES), F32)
        for blk in reversed(range(nb)):
            rows = slice(blk * CUM_BLOCK, (blk + 1) * CUM_BLOCK)
            rs = jnp.dot(triu, dcum_ref[rows, :], preferred_element_type=F32, precision=HIGHEST) + carry
            carry = rs[0:1, :]
            _, vjp = jax.vjp(jax.nn.log_sigmoid, fl_ref[rows, :] + bf_ref[...])
            dz = jnp.where(lane < N_HEADS, vjp(rs)[0], 0.0)
            dfl_ref[rows, :] = dz
            total = total + jnp.sum(dz, axis=0, keepdims=True)
        dbf_ref[0] = total

    spec = pl.BlockSpec((seq_len, LANES), lambda b: (b, 0))
    return pl.pallas_call(
        body, name="fprep_bwd", grid=(n_seq,), in_specs=[spec, spec, pl.BlockSpec((1, LANES), lambda b: (0, 0))],
        out_specs=(spec, pl.BlockSpec((1, 1, LANES), lambda b: (b, 0, 0))),
        out_shape=(jax.ShapeDtypeStruct((n, LANES), F32), jax.ShapeDtypeStruct((n_seq, 1, LANES), F32)),
        compiler_params=_cparams("parallel"),
    )(dcum, fl, bf)


ATT_TQ = 256
ATT_KSTEP = 256
ATT_SCALE = HEAD_DIM ** -0.5
NEG_BIG = -1e30


assert ATT_KSTEP == ATT_TQ


def _scores(q_scaled, kb, row_bias, ck, kend):
    s = _dot_nt(q_scaled, kb) - ck
    if row_bias is not None:
        s = s + row_bias
    r = lax.broadcasted_iota(jnp.int32, (ATT_TQ, ATT_TQ), 0)
    c = lax.broadcasted_iota(jnp.int32, (ATT_TQ, ATT_TQ), 1)
    diag = jnp.where(r >= c, s[:, kend - ATT_TQ:], NEG_BIG)
    return diag if kend == ATT_TQ else jnp.concatenate([s[:, :kend - ATT_TQ], diag], axis=1)


def _attn_specs(n_seq, seq_len):
    nq = seq_len // ATT_TQ
    q_spec = pl.BlockSpec((ATT_TQ, LANES), lambda b, h, q: (b * nq + q, h))
    k_spec = pl.BlockSpec((seq_len, LANES), lambda b, h, q: (b, N_HEADS // 2 + h))
    v_spec = pl.BlockSpec((seq_len, LANES), lambda b, h, q: (b, N_HEADS + h))
    cq_spec = pl.BlockSpec((1, 2, ATT_TQ, 1), lambda b, h, q: (b, h, q, 0))
    ck_spec = pl.BlockSpec((1, 2, 1, seq_len), lambda b, h, q: (b, h, 0, 0))
    return nq, q_spec, k_spec, v_spec, cq_spec, ck_spec


def _own_cum(cum_ref, e):
    lane = lax.broadcasted_iota(jnp.int32, (1, LANES), 1)
    return jnp.sum(jnp.where(lane == 2 * pl.program_id(1) + e, cum_ref[...], 0.0), axis=1, keepdims=True)


def _head_selectors():
    head0 = lax.broadcasted_iota(jnp.int32, (1, LANES), 1) < HEAD_DIM
    return head0, (head0, jnp.logical_not(head0))


def _for_key_range(qi, seq_len, run):
    per = ATT_KSTEP // ATT_TQ
    for g in range(seq_len // ATT_KSTEP):
        pl.when(qi // per == g)(functools.partial(run, (g + 1) * ATT_KSTEP))


def _attn_fwd(qkv, cum, ck, n_seq):
    n = qkv.shape[0]
    seq_len = n // n_seq
    nq, q_spec, k_spec, v_spec, cq_spec, ck_spec = _attn_specs(n_seq, seq_len)
    cum_spec = pl.BlockSpec((ATT_TQ, LANES), lambda b, h, q: (b * nq + q, 0))

    def body(q_ref, k_ref, v_ref, cum_ref, ck_ref, o_ref, lse_ref):
        qi = pl.program_id(2)
        q2 = q_ref[...]
        head0, sels = _head_selectors()
        qe = [jnp.where(sel, q2 * ATT_SCALE, 0.0).astype(BF16) for sel in sels]

        def run(kend):
            kb = k_ref[0:kend, :].astype(BF16)
            vb = v_ref[0:kend, :].astype(BF16)
            outs = []
            for e in range(2):
                s = _scores(qe[e], kb, None, ck_ref[0, e, :, 0:kend], kend)
                mx = jnp.max(s, axis=1, keepdims=True)
                p = jnp.exp(s - mx)
                den = jnp.sum(p, axis=1, keepdims=True)
                outs.append(_dot(p.astype(BF16), vb) / den)
                lse_ref[0, e] = _own_cum(cum_ref, e) + mx + jnp.log(den)
            o_ref[...] = jnp.where(head0, outs[0], outs[1])

        _for_key_range(qi, seq_len, run)

    return pl.pallas_call(
        body, name="attn_fwd", grid=(n_seq, N_HEADS // 2, nq),
        in_specs=[q_spec, k_spec, v_spec, cum_spec, ck_spec],
        out_specs=(q_spec, cq_spec),
        out_shape=(jax.ShapeDtypeStruct((n, D_ATTN), F32), jax.ShapeDtypeStruct((n_seq, N_HEADS, seq_len, 1), F32)),
        compiler_params=_cparams("parallel", "parallel", "parallel"),
    )(qkv, qkv, qkv, cum, ck)


def _attn_bwd(qkv, cum, ck, o, do, lse, n_seq):
    n = qkv.shape[0]
    seq_len = n // n_seq
    nq, q_spec, k_spec, v_spec, cq_spec, ck_spec = _attn_specs(n_seq, seq_len)
    kv_out = pl.BlockSpec((seq_len, LANES), lambda b, h, q: (b, h))
    cum_spec = pl.BlockSpec((ATT_TQ, LANES), lambda b, h, q: (b * nq + q, 0))

    def body(q_ref, k_ref, v_ref, cum_ref, ck_ref, o_ref, do_ref, lse_ref, dq_ref, dk_ref, dv_ref, dcq_ref, dck_ref):
        qi = pl.program_id(2)

        @pl.when(qi == 0)
        def _():
            dk_ref[...] = jnp.zeros_like(dk_ref)
            dv_ref[...] = jnp.zeros_like(dv_ref)
            dck_ref[...] = jnp.zeros_like(dck_ref)
        q2 = q_ref[...]
        do2 = do_ref[...]
        o2 = o_ref[...]
        head0, sels = _head_selectors()
        qe = [jnp.where(sel, q2 * ATT_SCALE, 0.0).astype(BF16) for sel in sels]
        doe = [jnp.where(sel, do2, 0.0) for sel in sels]
        doe_b = [d.astype(BF16) for d in doe]
        delta = [jnp.sum(d * o2, axis=1, keepdims=True) for d in doe]

        def run(kend):
            kb = k_ref[0:kend, :].astype(BF16)
            vb = v_ref[0:kend, :].astype(BF16)
            dqs = []
            dk = jnp.zeros((kend, LANES), F32)
            dv = jnp.zeros((kend, LANES), F32)
            for e in range(2):
                p = jnp.exp(_scores(qe[e], kb, _own_cum(cum_ref, e) - lse_ref[0, e], ck_ref[0, e, :, 0:kend], kend))
                ds = p * (_dot_nt(doe_b[e], vb) - delta[e])
                ds_b = ds.astype(BF16)
                dqs.append(_dot(ds_b, kb))
                dk = dk + _dot_tn(ds_b, qe[e])
                dv = dv + _dot_tn(p.astype(BF16), doe_b[e])
                dcq_ref[0, e] = jnp.sum(ds, axis=1, keepdims=True)
                dck_ref[0, e, :, 0:kend] -= jnp.sum(ds, axis=0, keepdims=True)
            dk_ref[0:kend, :] += dk
            dv_ref[0:kend, :] += dv
            dq_ref[...] = jnp.where(head0, dqs[0], dqs[1]) * ATT_SCALE

        _for_key_range(qi, seq_len, run)

    return pl.pallas_call(
        body, name="attn_bwd", grid=(n_seq, N_HEADS // 2, nq),
        in_specs=[q_spec, k_spec, v_spec, cum_spec, ck_spec, q_spec, q_spec, cq_spec],
        out_specs=(q_spec, kv_out, kv_out, cq_spec, ck_spec),
        out_shape=(jax.ShapeDtypeStruct((n, D_ATTN), F32), jax.ShapeDtypeStruct((n, D_ATTN), F32),
                   jax.ShapeDtypeStruct((n, D_ATTN), F32),
                   jax.ShapeDtypeStruct((n_seq, N_HEADS, seq_len, 1), F32),
                   jax.ShapeDtypeStruct((n_seq, N_HEADS, 1, seq_len), F32)),
        compiler_params=_cparams("parallel", "parallel", "arbitrary"),
    )(qkv, qkv, qkv, cum, ck, o, do, lse)


WEIGHT_NAMES = ("norm_mix", "w_in", "b_forget", "lam_re", "lam_im", "b_re", "b_im", "c_re", "c_im", "d_skip", "log_dt",
                "w_glu", "b_glu", "q_norm", "k_norm", "norm_out_ssm", "norm_out_attn", "w_out", "norm_ffn", "w_up",
                "conv_w", "conv_b", "w_down")
SHARDED = ("w_in", "w_glu", "w_out", "w_up", "conv_w", "w_down")
ADAM_TILE = {"w_in": (257, 256), "w_glu": (64, 512), "w_out": (128, 1024), "w_up": (688, 256), "conv_w": (3, 688),
             "w_down": (344, 1024)}
PACK_ROWS = SUBLANES * LANES


def _after_all(*arrays):
    return sum(a[(0,) * a.ndim].astype(F32) for a in arrays).reshape(1, 1)


def _pad_to(a, axis, size):
    pad = [(0, 0)] * a.ndim
    pad[axis] = (0, size - a.shape[axis])
    return jnp.pad(a, pad)


def _block_diag(t, transpose):
    t4 = t.reshape(S5_BLOCKS, 8, SSM_GROUP, SSM_STATE)
    eye = jnp.eye(8, dtype=t.dtype)
    if transpose:
        e = jnp.swapaxes(t4, 2, 3)[:, :, :, None, :] * eye[None, :, None, :, None]
        return e.reshape(S5_BLOCKS, S5_STATES, LANES)
    e = t4[:, :, :, None, :] * eye[None, :, None, :, None]
    return e.reshape(S5_BLOCKS, LANES, S5_STATES)


def _block_diag_extract(m, transpose):
    if transpose:
        m5 = m.reshape(S5_BLOCKS, 8, SSM_STATE, 8, SSM_GROUP)
        d = jnp.stack([m5[:, i, :, i, :] for i in range(8)], axis=1)
        return jnp.swapaxes(d, 2, 3).reshape(N_GROUPS, SSM_GROUP, SSM_STATE)
    m5 = m.reshape(S5_BLOCKS, 8, SSM_GROUP, 8, SSM_STATE)
    d = jnp.stack([m5[:, i, :, i, :] for i in range(8)], axis=1)
    return d.reshape(N_GROUPS, SSM_GROUP, SSM_STATE)


def _pack(pieces):
    flat = jnp.concatenate([p.reshape(-1).astype(F32) for p in pieces])
    size = -(-flat.shape[0] // PACK_ROWS) * PACK_ROWS
    return _pad_to(flat, 0, size).reshape(-1, LANES)


def _unpack(packed, shapes):
    flat = packed.reshape(-1)
    out, off = [], 0
    for shp in shapes:
        size = math.prod(shp)
        out.append(flat[off:off + size].reshape(shp))
        off += size
    return out


def kernel(x, norm_mix, w_in, b_forget, lam_re, lam_im, b_re, b_im, c_re, c_im, d_skip, log_dt, w_glu, b_glu, q_norm, k_norm, norm_out_ssm, norm_out_attn, w_out, norm_ffn, w_up, conv_w, conv_b, w_down, loss_target, m_norm_mix, m_w_in, m_b_forget, m_lam_re, m_lam_im, m_b_re, m_b_im, m_c_re, m_c_im, m_d_skip, m_log_dt, m_w_glu, m_b_glu, m_q_norm, m_k_norm, m_norm_out_ssm, m_norm_out_attn, m_w_out, m_norm_ffn, m_w_up, m_conv_w, m_conv_b, m_w_down, v_norm_mix, v_w_in, v_b_forget, v_lam_re, v_lam_im, v_b_re, v_b_im, v_c_re, v_c_im, v_d_skip, v_log_dt, v_w_glu, v_b_glu, v_q_norm, v_k_norm, v_norm_out_ssm, v_norm_out_attn, v_w_out, v_norm_ffn, v_w_up, v_conv_w, v_conv_b, v_w_down):
    given = dict(locals())
    weights = {k: given[k] for k in WEIGHT_NAMES}
    mom1 = {k: given["m_" + k] for k in WEIGHT_NAMES}
    mom2 = {k: given["v_" + k] for k in WEIGHT_NAMES}
    n_seq, seq_len, _ = x.shape
    n = n_seq * seq_len
    xf = x.reshape(n, D_MODEL)
    target = loss_target.reshape(n, D_MODEL)
    me_idx = 4 * lax.axis_index("x") + 2 * lax.axis_index("y") + lax.axis_index("c")

    in_flags = [False] * 2
    in_sems = _exchange_start([jnp.swapaxes(w_in[0], 0, 1).astype(BF16), conv_w[0]], in_flags, norm_mix,
                              "gather_in_start", 3)
    fill_own = lambda got, mine: lax.dynamic_update_index_in_dim(got, mine, me_idx, 0)

    lr3 = lam_re[0].reshape(N_GROUPS, 1, SSM_STATE)
    li3 = lam_im[0].reshape(N_GROUPS, 1, SSM_STATE)
    ldt3 = log_dt[0].reshape(N_GROUPS, 1, 1)
    br_t = jnp.swapaxes(b_re[0], 1, 2)
    bi_t = jnp.swapaxes(b_im[0], 1, 2)
    ab_re, ab_im, bb_re, bb_im = _s5_param_fwd(lr3, li3, ldt3, br_t, bi_t)
    a_re = ab_re.reshape(S5_BLOCKS, 1, S5_STATES)
    a_im = ab_im.reshape(S5_BLOCKS, 1, S5_STATES)
    bbr = _block_diag(bb_re, False).astype(BF16)
    bbi = _block_diag(bb_im, False).astype(BF16)
    cr = _block_diag(c_re[0], True).astype(BF16)
    ci = _block_diag(c_im[0], True).astype(BF16)
    avg = jnp.kron(jnp.eye(N_HEADS, dtype=F32), jnp.full((HEAD_DIM, HEAD_DIM), 1.0 / HEAD_DIM, F32)).astype(BF16)
    qg = jnp.tile(q_norm, (1, N_HEADS))
    kg = jnp.tile(k_norm, (1, N_HEADS))
    row_shards = [w_down[0].astype(BF16), w_out[0].astype(BF16), w_glu[0].astype(BF16)]

    (own_in, own_cw), (g_in, g_cw) = _exchange_wait(in_sems[0], in_sems[1], in_sems[2], in_sems[3], in_flags,
                                                    _after_all(a_re, a_im, bbr, bbi, cr, ci, avg, qg, kg, *row_shards),
                                                    "gather_in_wait")
    g_in = fill_own(g_in, own_in)
    g_cw = fill_own(g_cw, own_cw)
    row_flags = [False] * 3
    r_sems = _exchange_start(row_shards, row_flags, g_in, "gather_rows_start", 0)
    u_sems = _exchange_start([jnp.swapaxes(w_up[0], 0, 1).astype(BF16)], [False], r_sems[4], "gather_up_start", 5)
    norm_mix = norm_mix + u_sems[4][0, 0]
    w_in_p = _pad_to(g_in.reshape(D_IN, D_MODEL), 0, D_IN_PAD)

    hn, u, qkv, raw, fl = _inproj_fwd(xf, norm_mix, w_in_p, avg, qg, kg)
    yc, st_r, st_i, xs_r, xs_i = _s5_fwd(u, a_re, a_im, bbr, bbi, cr, ci, d_skip, n_seq)
    bf = _pad_to(b_forget, 1, LANES)
    cum = _fprep_fwd(fl, bf, n_seq)
    cum8 = jnp.swapaxes(cum[:, :N_HEADS].reshape(n_seq, seq_len, N_HEADS), 1, 2)
    ck = cum8[:, :, None, :]
    ya, lse = _attn_fwd(qkv, cum, ck, n_seq)
    own_rows, got_rows = _exchange_wait(r_sems[0], r_sems[1], r_sems[2], r_sems[3], row_flags, ya, "gather_rows_wait")
    g_down, g_out, g_glu = [fill_own(g, o) for o, g in zip(own_rows, got_rows)]
    w_glu_f = g_glu.reshape(D_SSM, D_SSM)
    w_out_f = g_out.reshape(D_MODEL, D_MODEL)
    conv_st = _pad_to(jnp.concatenate([g_cw, conv_b.reshape(N_DEV, 1, -1)], axis=1), 1, SUBLANES)
    w_down4 = g_down.reshape(FFN_GROUPS, FFN_GROUP, D_MODEL)
    ys = _glu_fwd(yc, w_glu_f, b_glu)
    h1, hn2, mixed = _mix_fwd(xf, ys, ya, norm_out_ssm, norm_out_attn, w_out_f, norm_ffn)
    (own_up,), (g_up,) = _exchange_wait(u_sems[0], u_sems[1], u_sems[2], u_sems[3], [False], _after_all(h1, hn2),
                                        "gather_up_wait")
    g_up = fill_own(g_up, own_up)
    ug, uv, pg, pv, dy, loss_part = _ffn_fwd(hn2, h1, target, g_up, conv_st, w_down4, seq_len)
    loss_local = 0.5 * jnp.sum(loss_part) / D_MODEL

    dug, duv, act, dhn2, dcg, dcv = _ffn_bwd(dy, ug, uv, pg, pv, g_up, conv_st, w_down4, seq_len)
    dh1, dys, dya, d_gs, d_ga, d_gf = _mix_bwd(dy, dhn2[None], h1, ys, ya, norm_out_ssm, norm_out_attn, w_out_f, norm_ffn)
    dyc, gl_b, dz_b, d_bglu = _glu_bwd(yc, dys, w_glu_f, b_glu)

    gw_glu = _tn_matmul(gl_b, dz_b, "dw_glu", D_SSM, D_SSM, out_dtype=BF16)
    gw_out = _tn_matmul(mixed, dh1, "dw_out", D_MODEL, D_MODEL, out_dtype=BF16)
    gw_up = jnp.concatenate([_tn_grouped(dug, hn2, "dw_up_gate", False, BF16),
                             _tn_grouped(duv, hn2, "dw_up_val", False, BF16)], axis=0)
    gw_down = _tn_grouped(act, dy, "dw_down", False, BF16)
    g_conv = jnp.concatenate([dcg, dcv], axis=0)
    by_cols = lambda g, c: jnp.swapaxes(g.reshape(g.shape[0], N_DEV, c), 0, 1)
    early_flags = [True] * 4
    early_names = ("w_down", "w_out", "w_glu", "w_up")
    g_sems = _exchange_start(
        [gw_down.reshape(N_DEV, -1, D_MODEL), gw_out.reshape(N_DEV, -1, D_MODEL), gw_glu.reshape(N_DEV, -1, D_SSM), gw_up],
        early_flags, dyc, "grad_early_start", 1)
    started = g_sems[4][0, 0]

    du, dbbr, dbbi, dcr, dci, dar, dai, ddk = _s5_bwd(u, dyc, xs_r, xs_i, st_r, st_i, a_re, a_im, bbr, bbi, cr, ci,
                                                      d_skip + started, n_seq)
    partial_early = {
        "ab_re": jnp.sum(dar, axis=1), "ab_im": jnp.sum(dai, axis=1),
        "bb_re": _block_diag_extract(dbbr, False), "bb_im": _block_diag_extract(dbbi, False),
        "c_re": _block_diag_extract(dcr, True), "c_im": _block_diag_extract(dci, True),
        "d_skip": jnp.sum(ddk, axis=1), "b_glu": d_bglu,
        "norm_out_ssm": d_gs, "norm_out_attn": d_ga, "norm_ffn": d_gf, "conv_b": g_conv[:, 3],
    }
    early_keys = tuple(partial_early)
    early_shapes = [partial_early[k].shape for k in early_keys]
    p_sems = _exchange_start([_pack([partial_early[k] for k in early_keys])], [False], du, "small_early_start", 2)
    started = started + p_sems[4][0, 0]

    dqn, dkn, dv, dcq, dck = _attn_bwd(qkv, cum, ck + started, ya, dya, lse, n_seq)
    dcum8 = dcq[:, :, :, 0] + dck.reshape(n_seq, N_HEADS, seq_len)
    dcum = _pad_to(jnp.swapaxes(dcum8, 1, 2).reshape(n, N_HEADS), 1, LANES)
    dfl, dbf = _fprep_bwd(dcum, fl, bf, n_seq)
    dx, dproj, d_gmix, d_qg, d_kg = _inproj_bwd(xf, norm_mix, w_in_p, avg, qg, kg, raw, du, dqn, dkn, dv, dfl, dh1)

    gw_in = _tn_matmul(dproj, hn, "dw_in", D_IN_PAD, D_MODEL, out_rows=D_IN, out_dtype=BF16)
    partial_late = {
        "norm_mix": d_gmix, "b_forget": jnp.sum(dbf, axis=(0, 1))[:N_HEADS],
        "q_norm": jnp.sum(d_qg.reshape(N_HEADS, HEAD_DIM), axis=0),
        "k_norm": jnp.sum(d_kg.reshape(N_HEADS, HEAD_DIM), axis=0), "loss": loss_local.reshape(1),
    }
    late_keys = tuple(partial_late)
    late_shapes = [partial_late[k].shape for k in late_keys]

    late_flags = [True, True, False]
    l_sems = _exchange_start(
        [gw_in.reshape(N_DEV, D_IN // N_DEV, D_MODEL).astype(BF16), g_conv[:, :3],
         _pack([partial_late[k] for k in late_keys])],
        late_flags, dx, "grad_late_start", 4)
    early_src, early_land = _exchange_wait(g_sems[0], g_sems[1], g_sems[2], g_sems[3], early_flags, l_sems[4],
                                           "grad_early_wait")
    land = dict(zip(early_names, early_land))
    land_up = land["w_up"]
    own = {k: lax.dynamic_index_in_dim(s, me_idx, 0, keepdims=False) for k, s in zip(early_names, early_src)}
    grads, deltas, new_m, new_v = {}, {}, {}, {}

    def adam_shard(name):
        flip = (lambda a: jnp.swapaxes(a, 1, 2)) if name in ("w_in", "w_up") else (lambda a: a)
        outs = _adam_sharded(land[name], own[name], flip(weights[name]), flip(mom1[name]), flip(mom2[name]),
                             "adam_" + name, ADAM_TILE[name])
        grads[name], deltas[name], new_m[name], new_v[name] = [flip(o) for o in outs]

    for name in ("w_up", "w_down", "w_out", "w_glu"):
        adam_shard(name)
    (own_pack,), (early_parts,) = _exchange_wait(p_sems[0], p_sems[1], p_sems[2], p_sems[3], [False], land_up,
                                                 "small_early_wait")
    early_sum = _sum_partials(early_parts, own_pack, "sum_early_partials")
    (src_in, src_cw, own_late), (land["w_in"], land["conv_w"], late_parts) = _exchange_wait(
        l_sems[0], l_sems[1], l_sems[2], l_sems[3], late_flags,
        _after_all(early_sum, *[new_v[k] for k in ("w_up", "w_down", "w_out", "w_glu")]), "grad_late_wait")
    own["w_in"] = lax.dynamic_index_in_dim(src_in, me_idx, 0, keepdims=False)
    own["conv_w"] = lax.dynamic_index_in_dim(src_cw, me_idx, 0, keepdims=False)
    for name in ("w_in", "conv_w"):
        adam_shard(name)

    summed = dict(zip(late_keys, _unpack(_sum_partials(late_parts, own_late, "sum_late_partials"), late_shapes)))
    summed.update(zip(early_keys, _unpack(early_sum, early_shapes)))
    dlr, dli, dldt, dbr_t, dbi_t = _s5_param_bwd(
        lr3, li3, ldt3, br_t, bi_t, summed["ab_re"].reshape(lr3.shape), summed["ab_im"].reshape(lr3.shape),
        summed["bb_re"], summed["bb_im"])
    small_grads = {
        "norm_mix": summed["norm_mix"], "b_forget": summed["b_forget"], "lam_re": dlr, "lam_im": dli,
        "b_re": dbr_t, "b_im": dbi_t, "c_re": summed["c_re"], "c_im": summed["c_im"],
        "d_skip": summed["d_skip"], "log_dt": dldt, "b_glu": summed["b_glu"], "q_norm": summed["q_norm"],
        "k_norm": summed["k_norm"], "norm_out_ssm": summed["norm_out_ssm"], "norm_out_attn": summed["norm_out_attn"],
        "norm_ffn": summed["norm_ffn"], "conv_b": summed["conv_b"],
    }
    repl = tuple(k for k in WEIGHT_NAMES if k not in SHARDED)
    turn = lambda k, a: jnp.swapaxes(a, 2, 3) if k in ("b_re", "b_im") else a
    w_list = [turn(k, weights[k]) for k in repl]
    g_list = [small_grads[k].reshape(w.shape) for k, w in zip(repl, w_list)]
    d_list, m_list, v_list = _adam_replicated(g_list, w_list, [turn(k, mom1[k]) for k in repl],
                                              [turn(k, mom2[k]) for k in repl], "adam_replicated")
    for k, g, d, nm, nv in zip(repl, g_list, d_list, m_list, v_list):
        grads[k], deltas[k], new_m[k], new_v[k] = turn(k, g), turn(k, d), turn(k, nm), turn(k, nv)

    grad_x = dx.reshape(x.shape)
    loss = summed["loss"].reshape(())
    return (loss, grad_x, *[grads[k] for k in WEIGHT_NAMES], *[deltas[k] for k in WEIGHT_NAMES],
            *[new_m[k] for k in WEIGHT_NAMES], *[new_v[k] for k in WEIGHT_NAMES])
```

```python
import functools
import math

import jax
import jax.numpy as jnp
from jax import lax
from jax.experimental import pallas as pl
from jax.experimental.pallas import tpu as pltpu

F32 = jnp.float32
BF16 = jnp.bfloat16
HIGHEST = lax.Precision.HIGHEST

N_DEV = 8
D_MODEL = 1024
D_SSM = 512
D_ATTN = 512
N_HEADS = 8
HEAD_DIM = 64
N_GROUPS = 32
SSM_GROUP = 16
SSM_STATE = 64
D_FF = 2752
D_FF_PAD = 2816
D_IN = 2056
D_IN_PAD = 2176
EPS = 1e-6
LANES = 128
SUBLANES = 8
VMEM_LIMIT = 56 * 1024 * 1024

ADAM_LR = 0.001
ADAM_B1 = 0.9
ADAM_B2 = 0.999
ADAM_EPS = 1e-08
ADAM_WD = 0.01
ADAM_STEP = 10


def _cparams(*sem):
    return pltpu.CompilerParams(dimension_semantics=sem, vmem_limit_bytes=VMEM_LIMIT)


def _dot(a, b, **kw):
    return jnp.dot(a, b, preferred_element_type=F32, **kw)


def _dot_nt(a, b):
    return lax.dot_general(a, b, (((1,), (1,)), ((), ())), preferred_element_type=F32)


def _dot_tn(a, b):
    return lax.dot_general(a, b, (((0,), (0,)), ((), ())), preferred_element_type=F32)


def _rms(x, g):
    return x * lax.rsqrt(jnp.mean(x * x, axis=-1, keepdims=True) + EPS) * g


def _split_dot(x, avg):
    hi = x.astype(BF16)
    lo = (x - hi.astype(F32)).astype(BF16)
    return _dot(hi, avg) + _dot(lo, avg)


@jax.custom_vjp
def _group_mean(x, avg):
    return _split_dot(x, avg)


def _group_mean_fwd(x, avg):
    return _split_dot(x, avg), avg


def _group_mean_bwd(avg, ct):
    return _split_dot(ct, avg), jnp.zeros_like(avg)


_group_mean.defvjp(_group_mean_fwd, _group_mean_bwd)


def _headnorm(q, avg, g):
    return q * lax.rsqrt(_group_mean(q * q, avg) + EPS) * g


ALL_PEERS = tuple(range(1, N_DEV))
NEAR_PEERS = (1, 2, 4, 6)
RELAYED = (2, 4, 6)


def _peer_list(js=ALL_PEERS):
    x, y, c = lax.axis_index("x"), lax.axis_index("y"), lax.axis_index("c")
    peers = []
    for j in js:
        px = 1 - x if (j >> 2) & 1 else x
        py = 1 - y if (j >> 1) & 1 else y
        pc = 1 - c if j & 1 else c
        peers.append(((px, py, pc), 4 * px + 2 * py + pc))
    return 4 * x + 2 * y + c, peers


def _split_copies(src, land, send_sems, recv_sems, scatter_flags, me, peers, incoming):
    copies = []
    for k in range(len(src)):
        for j, (pid, pidx) in enumerate(peers):
            s = src[k].at[pidx] if scatter_flags[k] else src[k]
            i = k * len(peers) + j
            copies.append(pltpu.make_async_remote_copy(
                src_ref=s, dst_ref=land[k].at[pidx if incoming else me], send_sem=send_sems[i],
                recv_sem=recv_sems[i], device_id=pid, device_id_type=pl.DeviceIdType.MESH))
    return copies


def _handshake(peers):
    barrier = pltpu.get_barrier_semaphore()
    for pid, _ in peers:
        pl.semaphore_signal(barrier, inc=1, device_id=pid, device_id_type=pl.DeviceIdType.MESH)
    pl.semaphore_wait(barrier, len(peers))


def _exchange_start(srcs, scatter_flags, after, name, collective_id, peer_js=ALL_PEERS):
    n = len(srcs)
    ns = n * len(peer_js)
    hbm = pl.BlockSpec(memory_space=pltpu.HBM)
    sem = pl.BlockSpec(memory_space=pltpu.SEMAPHORE)
    land_shapes = [s.shape if sc else (N_DEV,) + s.shape for s, sc in zip(srcs, scatter_flags)]

    def body(*refs):
        src, land = refs[:n], refs[n:2 * n]
        send_sems = refs[2 * n + 1:2 * n + 1 + ns]
        recv_sems = refs[2 * n + 1 + ns:2 * n + 1 + 2 * ns]
        token = refs[4 * n + 1 + 2 * ns]
        me, peers = _peer_list(peer_js)
        _handshake(peers)
        for cp in _split_copies(src, land, send_sems, recv_sems, scatter_flags, me, peers, False):
            cp.start()
        token[...] = jnp.zeros_like(token)

    outs = pl.pallas_call(
        body, name=name,
        out_shape=(*[pltpu.SemaphoreType.DMA(())] * (2 * ns), *[pltpu.HBM(s.shape, s.dtype) for s in srcs],
                   *[pltpu.HBM(shp, s.dtype) for shp, s in zip(land_shapes, srcs)],
                   jax.ShapeDtypeStruct((SUBLANES, LANES), F32)),
        in_specs=[hbm] * (2 * n) + [pl.BlockSpec(memory_space=pl.ANY)],
        out_specs=(*[sem] * (2 * ns), *[hbm] * (2 * n), pl.BlockSpec(memory_space=pltpu.VMEM)),
        input_output_aliases={i: 2 * ns + i for i in range(2 * n)},
        compiler_params=pltpu.CompilerParams(has_side_effects=pltpu.SideEffectType.DATAFLOW_SIDE_EFFECTING,
                                             collective_id=collective_id),
    )(*[pltpu.with_memory_space_constraint(s, pltpu.HBM) for s in srcs],
      *[pltpu.with_memory_space_constraint(lax.empty(shp, s.dtype), pltpu.HBM) for shp, s in zip(land_shapes, srcs)],
      after)
    return (outs[:ns], outs[ns:2 * ns], outs[2 * ns:2 * ns + n], outs[2 * ns + n:2 * ns + 2 * n], outs[2 * ns + 2 * n])


def _exchange_wait(send_sems, recv_sems, srcs, lands, scatter_flags, after, name, peer_js=ALL_PEERS):
    n = len(srcs)
    ns = n * len(peer_js)
    hbm = pl.BlockSpec(memory_space=pltpu.HBM)
    sem = pl.BlockSpec(memory_space=pltpu.SEMAPHORE)

    def body(*refs):
        src, land = refs[:n], refs[n:2 * n]
        s_sems = refs[2 * n:2 * n + ns]
        r_sems = refs[2 * n + ns:2 * n + 2 * ns]
        me, peers = _peer_list(peer_js)
        for cp in _split_copies(src, land, s_sems, r_sems, scatter_flags, me, peers, True):
            cp.wait_send()
            cp.wait_recv()

    outs = pl.pallas_call(
        body, name=name,
        out_shape=tuple(pltpu.HBM(a.shape, a.dtype) for a in (*srcs, *lands)),
        in_specs=[hbm] * (2 * n) + [sem] * (2 * ns) + [pl.BlockSpec(memory_space=pl.ANY)],
        out_specs=tuple([hbm] * (2 * n)),
        input_output_aliases={i: i for i in range(2 * n)},
        compiler_params=pltpu.CompilerParams(has_side_effects=pltpu.SideEffectType.DATAFLOW_SIDE_EFFECTING),
    )(*srcs, *lands, *send_sems, *recv_sems, after)
    return outs[:n], outs[n:]


def _relay_copies(land, send_sems, recv_sems, incoming):
    _, ((sibling, _),) = _peer_list((1,))
    _, heard = _peer_list(RELAYED)
    _, sibling_heard = _peer_list(tuple(j ^ 1 for j in RELAYED))
    copies = []
    for k in range(len(land)):
        for j in range(len(RELAYED)):
            slot = (sibling_heard if incoming else heard)[j][1]
            i = k * len(RELAYED) + j
            copies.append(pltpu.make_async_remote_copy(
                src_ref=land[k].at[slot], dst_ref=land[k].at[slot], send_sem=send_sems[i], recv_sem=recv_sems[i],
                device_id=sibling, device_id_type=pl.DeviceIdType.MESH))
    return copies


def _relay_start(lands, name, collective_id):
    n = len(lands)
    ns = n * len(RELAYED)
    hbm = pl.BlockSpec(memory_space=pltpu.HBM)
    sem = pl.BlockSpec(memory_space=pltpu.SEMAPHORE)

    def body(*refs):
        land = refs[:n]
        send_sems = refs[n:n + ns]
        recv_sems = refs[n + ns:n + 2 * ns]
        token = refs[2 * n + 2 * ns]
        _handshake(_peer_list((1,))[1])
        for cp in _relay_copies(land, send_sems, recv_sems, False):
            cp.start()
        token[...] = jnp.zeros_like(token)

    outs = pl.pallas_call(
        body, name=name,
        out_shape=(*[pltpu.SemaphoreType.DMA(())] * (2 * ns), *[pltpu.HBM(a.shape, a.dtype) for a in lands],
                   jax.ShapeDtypeStruct((SUBLANES, LANES), F32)),
        in_specs=[hbm] * n,
        out_specs=(*[sem] * (2 * ns), *[hbm] * n, pl.BlockSpec(memory_space=pltpu.VMEM)),
        input_output_aliases={i: 2 * ns + i for i in range(n)},
        compiler_params=pltpu.CompilerParams(has_side_effects=pltpu.SideEffectType.DATAFLOW_SIDE_EFFECTING,
                                             collective_id=collective_id),
    )(*lands)
    return outs[:ns], outs[ns:2 * ns], outs[2 * ns:2 * ns + n], outs[2 * ns + n]


def _relay_wait(send_sems, recv_sems, lands, after, name):
    n = len(lands)
    ns = n * len(RELAYED)
    hbm = pl.BlockSpec(memory_space=pltpu.HBM)
    sem = pl.BlockSpec(memory_space=pltpu.SEMAPHORE)

    def body(*refs):
        land = refs[:n]
        for cp in _relay_copies(land, refs[n:n + ns], refs[n + ns:n + 2 * ns], True):
            cp.wait_send()
            cp.wait_recv()

    return pl.pallas_call(
        body, name=name,
        out_shape=tuple(pltpu.HBM(a.shape, a.dtype) for a in lands),
        in_specs=[hbm] * n + [sem] * (2 * ns) + [pl.BlockSpec(memory_space=pl.ANY)],
        out_specs=tuple([hbm] * n),
        input_output_aliases={i: i for i in range(n)},
        compiler_params=pltpu.CompilerParams(has_side_effects=pltpu.SideEffectType.DATAFLOW_SIDE_EFFECTING),
    )(*lands, *send_sems, *recv_sems, after)


def _tn_matmul(a, b, name, tk, tm, out_rows=None, out_cols=None, out_dtype=F32, tn=512):
    n_tok, k_dim = a.shape
    m_dim = b.shape[1]
    grid = (k_dim // tk, m_dim // tm, n_tok // tn)

    def body(a_ref, b_ref, o_ref, acc):
        k = pl.program_id(2)
        part = _dot_tn(a_ref[...].astype(BF16), b_ref[...].astype(BF16))

        @pl.when(k == 0)
        def _():
            acc[...] = part

        @pl.when(k > 0)
        def _():
            acc[...] += part

        @pl.when(k == grid[2] - 1)
        def _():
            o_ref[...] = acc[...].astype(out_dtype)

    return pl.pallas_call(
        body, name=name, grid=grid,
        in_specs=[pl.BlockSpec((tn, tk), lambda i, j, k: (k, i)), pl.BlockSpec((tn, tm), lambda i, j, k: (k, j))],
        out_specs=pl.BlockSpec((tk, tm), lambda i, j, k: (i, j)),
        out_shape=jax.ShapeDtypeStruct((out_rows or k_dim, out_cols or m_dim), out_dtype),
        scratch_shapes=[pltpu.VMEM((tk, tm), F32)],
        compiler_params=_cparams("parallel", "parallel", "arbitrary"),
    )(a, b)


def _adam_math(g, w, m, v):
    m = ADAM_B1 * m + (1.0 - ADAM_B1) * g
    v = ADAM_B2 * v + (1.0 - ADAM_B2) * (g * g)
    m_hat = m / (1.0 - ADAM_B1 ** ADAM_STEP)
    v_hat = v / (1.0 - ADAM_B2 ** ADAM_STEP)
    delta = -ADAM_LR * (m_hat / (jnp.sqrt(v_hat) + ADAM_EPS) + ADAM_WD * w)
    return delta, m, v


def _adam_sharded(land, own, w, m, v, name, tile):
    _, r, c = w.shape

    def body(*refs):
        l_ref = refs[0]
        own_ref = refs[1] if own is not None else None
        w_ref, m_ref, v_ref, g_ref, d_ref, nm_ref, nv_ref = [ref.at[0] for ref in refs[-7:]]
        if own_ref is not None:
            x, y, z = lax.axis_index("x"), lax.axis_index("y"), lax.axis_index("c")
            me = 4 * x + 2 * y + z
            mine = own_ref[...].astype(F32)
        g = None
        for s in range(N_DEV):
            part = l_ref[s].astype(F32)
            if own_ref is not None:
                part = jnp.where(me == s, mine, part)
            g = part if g is None else g + part
        d, nm, nv = _adam_math(g, w_ref[...], m_ref[...], v_ref[...])
        g_ref[...] = g
        d_ref[...] = d
        nm_ref[...] = nm
        nv_ref[...] = nv

    tr, tc = tile
    spec = pl.BlockSpec((1, tr, tc), lambda i, j: (0, i, j))
    own_specs, own_args = ([pl.BlockSpec((tr, tc), lambda i, j: (i, j))], [own]) if own is not None else ([], [])
    return pl.pallas_call(
        body, name=name, grid=(r // tr, c // tc),
        in_specs=[pl.BlockSpec((N_DEV, tr, tc), lambda i, j: (0, i, j)), *own_specs, spec, spec, spec],
        out_specs=(spec, spec, spec, spec),
        out_shape=tuple(jax.ShapeDtypeStruct((1, r, c), F32) for _ in range(4)),
        compiler_params=_cparams("parallel", "parallel"),
    )(land, *own_args, w, m, v)


def _sum_partials(parts, own, name):
    _, r, c = parts.shape

    def body(*refs):
        p_ref, o_ref = refs[0], refs[-1]
        if own is not None:
            x, y, z = lax.axis_index("x"), lax.axis_index("y"), lax.axis_index("c")
            me = 4 * x + 2 * y + z
            mine = refs[1][...]
        g = None
        for s in range(N_DEV):
            part = p_ref[s]
            if own is not None:
                part = jnp.where(me == s, mine, part)
            g = part if g is None else g + part
        o_ref[...] = g

    args = (parts,) if own is None else (parts, own)
    return pl.pallas_call(body, name=name, out_shape=jax.ShapeDtypeStruct((r, c), F32),
                          compiler_params=pltpu.CompilerParams(vmem_limit_bytes=VMEM_LIMIT))(*args)


def _adam_replicated(gs, ws, ms, vs, name):
    k = len(ws)

    def body(*refs):
        outs = refs[4 * k:]
        for i in range(k):
            d, nm, nv = _adam_math(refs[i][...], refs[k + i][...], refs[2 * k + i][...], refs[3 * k + i][...])
            outs[i][...] = d
            outs[k + i][...] = nm
            outs[2 * k + i][...] = nv

    outs = pl.pallas_call(body, name=name, out_shape=tuple(jax.ShapeDtypeStruct(w.shape, F32) for w in ws) * 3,
                          compiler_params=pltpu.CompilerParams(vmem_limit_bytes=VMEM_LIMIT))(*gs, *ws, *ms, *vs)
    return outs[:k], outs[k:2 * k], outs[2 * k:]


def _inproj_fwd(x, g, w_in, avg, qg, kg, tm=512):
    n = x.shape[0]

    def body(x_ref, g_ref, w_ref, a_ref, qg_ref, kg_ref, hn_ref, u_ref, qkv_ref, raw_ref, fl_ref):
        hn = _rms(x_ref[...], g_ref[...]).astype(BF16)
        hn_ref[...] = hn
        proj = _dot_nt(hn, w_ref[...])
        u_ref[...] = proj[:, 0:512]
        q = proj[:, 512:1024]
        k = proj[:, 1024:1536]
        raw_ref[:, 0:512] = q
        raw_ref[:, 512:1024] = k
        qkv_ref[:, 0:512] = _headnorm(q, a_ref[...], qg_ref[...]).astype(BF16)
        qkv_ref[:, 512:1024] = _headnorm(k, a_ref[...], kg_ref[...]).astype(BF16)
        qkv_ref[:, 1024:1536] = proj[:, 1536:2048].astype(BF16)
        fl_ref[...] = proj[:, 2048:D_IN_PAD]

    row = lambda w: pl.BlockSpec((tm, w), lambda i: (i, 0))
    full = lambda a: pl.BlockSpec(a.shape, lambda i: (0,) * a.ndim)
    return pl.pallas_call(
        body, name="inproj_fwd", grid=(n // tm,),
        in_specs=[row(D_MODEL), full(g), full(w_in), full(avg), full(qg), full(kg)],
        out_specs=(row(D_MODEL), row(512), row(1536), row(1024), row(LANES)),
        out_shape=(jax.ShapeDtypeStruct((n, D_MODEL), BF16), jax.ShapeDtypeStruct((n, 512), F32),
                   jax.ShapeDtypeStruct((n, 1536), BF16), jax.ShapeDtypeStruct((n, 1024), F32),
                   jax.ShapeDtypeStruct((n, LANES), F32)),
        compiler_params=_cparams("parallel"),
    )(x, g, w_in, avg, qg, kg)


def _inproj_bwd(x, g, w_in, avg, qg, kg, raw, du, dqn, dkn, dv, dfl, dres, tm=512):
    n = x.shape[0]

    def body(x_ref, g_ref, w_ref, a_ref, qg_ref, kg_ref, raw_ref, du_ref, dqn_ref, dkn_ref, dv_ref, dfl_ref, dres_ref,
             dx_ref, dproj_ref, dg_ref, dqg_ref, dkg_ref):
        @pl.when(pl.program_id(0) == 0)
        def _():
            dg_ref[...] = jnp.zeros_like(dg_ref)
            dqg_ref[...] = jnp.zeros_like(dqg_ref)
            dkg_ref[...] = jnp.zeros_like(dkg_ref)
        avg_m = a_ref[...]
        _, vjp_q = jax.vjp(lambda q, gg: _headnorm(q, avg_m, gg), raw_ref[:, 0:512], qg_ref[...])
        dq, dqg = vjp_q(dqn_ref[...])
        _, vjp_k = jax.vjp(lambda k, gg: _headnorm(k, avg_m, gg), raw_ref[:, 512:1024], kg_ref[...])
        dk, dkg = vjp_k(dkn_ref[...])
        dproj = jnp.concatenate([du_ref[...], dq, dk, dv_ref[...], dfl_ref[...]], axis=1).astype(BF16)
        dproj_ref[...] = dproj
        dhn = _dot(dproj, w_ref[...])
        _, vjp_x = jax.vjp(_rms, x_ref[...], g_ref[...])
        dxn, dg = vjp_x(dhn)
        dx_ref[...] = dxn + dres_ref[...]
        dg_ref[...] += dg
        dqg_ref[...] += dqg
        dkg_ref[...] += dkg

    row = lambda w: pl.BlockSpec((tm, w), lambda i: (i, 0))
    full = lambda a: pl.BlockSpec(a.shape, lambda i: (0,) * a.ndim)
    vec = lambda w: pl.BlockSpec((1, w), lambda i: (0, 0))
    return pl.pallas_call(
        body, name="inproj_bwd", grid=(n // tm,),
        in_specs=[row(D_MODEL), full(g), full(w_in), full(avg), full(qg), full(kg), row(1024), row(512), row(512),
                  row(512), row(512), row(LANES), row(D_MODEL)],
        out_specs=(row(D_MODEL), row(D_IN_PAD), vec(D_MODEL), vec(512), vec(512)),
        out_shape=(jax.ShapeDtypeStruct((n, D_MODEL), F32), jax.ShapeDtypeStruct((n, D_IN_PAD), BF16),
                   jax.ShapeDtypeStruct((1, D_MODEL), F32), jax.ShapeDtypeStruct((1, 512), F32),
                   jax.ShapeDtypeStruct((1, 512), F32)),
        compiler_params=_cparams("arbitrary"),
    )(x, g, w_in, avg, qg, kg, raw, du, dqn, dkn, dv, dfl, dres)


def _glu_fwd(yc, wg, bg, tm=512):
    n = yc.shape[0]

    def body(yc_ref, w_ref, b_ref, ys_ref):
        gl = jax.nn.gelu(yc_ref[...])
        z = _dot(gl.astype(BF16), w_ref[...]) + b_ref[...]
        ys_ref[...] = gl * jax.nn.sigmoid(z)

    row = pl.BlockSpec((tm, 512), lambda i: (i, 0))
    full = lambda a: pl.BlockSpec(a.shape, lambda i: (0,) * a.ndim)
    return pl.pallas_call(
        body, name="glu_fwd", grid=(n // tm,), in_specs=[row, full(wg), full(bg)], out_specs=row,
        out_shape=jax.ShapeDtypeStruct((n, 512), F32), compiler_params=_cparams("parallel"),
    )(yc, wg, bg)


def _glu_bwd(yc, dys, wg, bg, tm=512):
    n = yc.shape[0]

    def body(yc_ref, dys_ref, w_ref, b_ref, dyc_ref, gl_ref, dz_ref, db_ref):
        @pl.when(pl.program_id(0) == 0)
        def _():
            db_ref[...] = jnp.zeros_like(db_ref)
        gl, vjp_gelu = jax.vjp(jax.nn.gelu, yc_ref[...])
        glb = gl.astype(BF16)
        z = _dot(glb, w_ref[...]) + b_ref[...]
        s = jax.nn.sigmoid(z)
        dys = dys_ref[...]
        dz = dys * gl * s * (1.0 - s)
        dzb = dz.astype(BF16)
        dgl = dys * s + _dot_nt(dzb, w_ref[...])
        dyc_ref[...] = vjp_gelu(dgl)[0]
        gl_ref[...] = glb
        dz_ref[...] = dzb
        db_ref[...] += jnp.sum(dz, axis=0, keepdims=True)

    row = pl.BlockSpec((tm, 512), lambda i: (i, 0))
    full = lambda a: pl.BlockSpec(a.shape, lambda i: (0,) * a.ndim)
    return pl.pallas_call(
        body, name="glu_bwd", grid=(n // tm,), in_specs=[row, row, full(wg), full(bg)],
        out_specs=(row, row, row, pl.BlockSpec((1, 512), lambda i: (0, 0))),
        out_shape=(jax.ShapeDtypeStruct((n, 512), F32), jax.ShapeDtypeStruct((n, 512), BF16),
                   jax.ShapeDtypeStruct((n, 512), BF16), jax.ShapeDtypeStruct((1, 512), F32)),
        compiler_params=_cparams("arbitrary"),
    )(yc, dys, wg, bg)


def _mix_fwd(x, ys, ya, gs, ga, wout, gf, tm=512):
    n = x.shape[0]

    def body(x_ref, ys_ref, ya_ref, gs_ref, ga_ref, w_ref, gf_ref, h1_ref, hn2_ref, mixed_ref):
        mixed = jnp.concatenate([_rms(ys_ref[...], gs_ref[...]), _rms(ya_ref[...], ga_ref[...])], axis=1).astype(BF16)
        mixed_ref[...] = mixed
        h1 = x_ref[...] + _dot(mixed, w_ref[...])
        h1_ref[...] = h1
        hn2_ref[...] = _rms(h1, gf_ref[...]).astype(BF16)

    row = lambda w: pl.BlockSpec((tm, w), lambda i: (i, 0))
    full = lambda a: pl.BlockSpec(a.shape, lambda i: (0,) * a.ndim)
    return pl.pallas_call(
        body, name="mix_fwd", grid=(n // tm,),
        in_specs=[row(D_MODEL), row(512), row(512), full(gs), full(ga), full(wout), full(gf)],
        out_specs=(row(D_MODEL), row(D_MODEL), row(D_MODEL)),
        out_shape=(jax.ShapeDtypeStruct((n, D_MODEL), F32), jax.ShapeDtypeStruct((n, D_MODEL), BF16),
                   jax.ShapeDtypeStruct((n, D_MODEL), BF16)),
        compiler_params=_cparams("parallel"),
    )(x, ys, ya, gs, ga, wout, gf)


def _mix_bwd(dy, dhn2_parts, h1, ys, ya, gs, ga, wout, gf, tm=512):
    n = dy.shape[0]
    n_parts = dhn2_parts.shape[0]

    def body(dy_ref, dp_ref, h1_ref, ys_ref, ya_ref, gs_ref, ga_ref, w_ref, gf_ref,
             dh1_ref, dys_ref, dya_ref, dgs_ref, dga_ref, dgf_ref):
        @pl.when(pl.program_id(0) == 0)
        def _():
            dgs_ref[...] = jnp.zeros_like(dgs_ref)
            dga_ref[...] = jnp.zeros_like(dga_ref)
            dgf_ref[...] = jnp.zeros_like(dgf_ref)
        dhn2 = dp_ref[0]
        for p in range(1, n_parts):
            dhn2 = dhn2 + dp_ref[p]
        _, vjp_f = jax.vjp(_rms, h1_ref[...], gf_ref[...])
        dh1n, dgf = vjp_f(dhn2)
        dh1 = dy_ref[...] + dh1n
        dh1_ref[...] = dh1
        dmixed = _dot_nt(dh1.astype(BF16), w_ref[...])
        _, vjp_s = jax.vjp(_rms, ys_ref[...], gs_ref[...])
        dys, dgs = vjp_s(dmixed[:, 0:512])
        _, vjp_a = jax.vjp(_rms, ya_ref[...], ga_ref[...])
        dya, dga = vjp_a(dmixed[:, 512:1024])
        dys_ref[...] = dys
        dya_ref[...] = dya
        dgs_ref[...] += dgs
        dga_ref[...] += dga
        dgf_ref[...] += dgf

    row = lambda w: pl.BlockSpec((tm, w), lambda i: (i, 0))
    full = lambda a: pl.BlockSpec(a.shape, lambda i: (0,) * a.ndim)
    vec = lambda w: pl.BlockSpec((1, w), lambda i: (0, 0))
    return pl.pallas_call(
        body, name="mix_bwd", grid=(n // tm,),
        in_specs=[row(D_MODEL), pl.BlockSpec((n_parts, tm, D_MODEL), lambda i: (0, i, 0)), row(D_MODEL), row(512),
                  row(512), full(gs), full(ga), full(wout), full(gf)],
        out_specs=(row(D_MODEL), row(512), row(512), vec(512), vec(512), vec(D_MODEL)),
        out_shape=(jax.ShapeDtypeStruct((n, D_MODEL), F32), jax.ShapeDtypeStruct((n, 512), F32),
                   jax.ShapeDtypeStruct((n, 512), F32), jax.ShapeDtypeStruct((1, 512), F32),
                   jax.ShapeDtypeStruct((1, 512), F32), jax.ShapeDtypeStruct((1, D_MODEL), F32)),
        compiler_params=_cparams("arbitrary"),
    )(dy, dhn2_parts, h1, ys, ya, gs, ga, wout, gf)


HALO = 16
FFN_GROUPS = 4
FFN_GROUP = D_FF // FFN_GROUPS


def _conv3(ue, cw):
    return cw[2:3] * ue + cw[1:2] * pltpu.roll(ue, 1, 0) + cw[0:1] * pltpu.roll(ue, 2, 0) + cw[3:4]


def _ffn_weight_specs():
    gate = lambda i, j: (j, 0, 0)
    val = lambda i, j: (j + FFN_GROUPS, 0, 0)
    w_blk, c_blk = (1, FFN_GROUP, D_MODEL), (1, SUBLANES, FFN_GROUP)
    return [pl.BlockSpec(w_blk, gate), pl.BlockSpec(w_blk, val), pl.BlockSpec(c_blk, gate), pl.BlockSpec(c_blk, val),
            pl.BlockSpec((1, FFN_GROUP, D_MODEL), gate)]


def _ffn_fwd(hn2, h1, target, w_up, conv, w_down, seq_len, tm=512):
    n = hn2.shape[0]
    nj = FFN_GROUPS
    hb = tm // HALO

    def body(hn_ref, halo_ref, h1_ref, tgt_ref, wg_ref, wv_ref, cg_ref, cv_ref, wd_ref,
             ug_ref, uv_ref, pg_ref, pv_ref, dy_ref, loss_ref, acc):
        i, j = pl.program_id(0), pl.program_id(1)
        seq_start = (i * tm) % seq_len == 0
        halo = halo_ref[...]
        halo = jnp.where(seq_start, jnp.zeros_like(halo), halo)
        he = jnp.concatenate([halo, hn_ref[...]], axis=0)
        ueg = _dot_nt(he, wg_ref[0])
        uev = _dot_nt(he, wv_ref[0])
        ug_ref[0] = ueg[HALO:].astype(BF16)
        uv_ref[0] = uev[HALO:].astype(BF16)
        cg = _conv3(ueg, cg_ref[0])[HALO:]
        cv = _conv3(uev, cv_ref[0])[HALO:]
        pg_ref[0] = cg.astype(BF16)
        pv_ref[0] = cv.astype(BF16)
        act = (jax.nn.silu(cg) * cv).astype(BF16)
        part = _dot(act, wd_ref[0])

        @pl.when(j == 0)
        def _():
            acc[...] = part

        @pl.when(j > 0)
        def _():
            acc[...] += part

        @pl.when(j == nj - 1)
        def _():
            err = h1_ref[...] + acc[...] - tgt_ref[...]
            dy_ref[...] = err * (1.0 / D_MODEL)
            loss_ref[0] = jnp.sum(err * err, axis=0, keepdims=True)

    row = pl.BlockSpec((tm, D_MODEL), lambda i, j: (i, 0))
    u_main = pl.BlockSpec((1, tm, FFN_GROUP), lambda i, j: (j, i, 0))
    u_shape = jax.ShapeDtypeStruct((FFN_GROUPS, n, FFN_GROUP), BF16)
    return pl.pallas_call(
        body, name="ffn_fwd", grid=(n // tm, nj),
        in_specs=[row, pl.BlockSpec((HALO, D_MODEL), lambda i, j: (jnp.maximum(i * hb - 1, 0), 0)), row, row,
                  *_ffn_weight_specs()],
        out_specs=(u_main, u_main, u_main, u_main, row, pl.BlockSpec((1, 1, D_MODEL), lambda i, j: (i, 0, 0))),
        out_shape=(u_shape, u_shape, u_shape, u_shape, jax.ShapeDtypeStruct((n, D_MODEL), F32),
                   jax.ShapeDtypeStruct((n // tm, 1, D_MODEL), F32)),
        scratch_shapes=[pltpu.VMEM((tm, D_MODEL), F32)],
        compiler_params=_cparams("parallel", "arbitrary"),
    )(hn2, hn2, h1, target, w_up, w_up, conv, conv, w_down)


def _ffn_bwd(dy, ug, uv, pg, pv, w_up, conv, w_down, seq_len, tm=512):
    n = dy.shape[0]
    nj = FFN_GROUPS
    fb = FFN_GROUP
    hb = tm // HALO
    last_hb = n // HALO - 1
    rows = tm + HALO

    def body(dy_ref, dyn_ref, ug_ref, uv_ref, pgm_ref, pgn_ref, pvm_ref, pvn_ref, wg_ref, wv_ref, cg_ref, cv_ref,
             wd_ref, dug_ref, duv_ref, act_ref, dhn_ref, dcg_ref, dcv_ref, acc):
        i, j = pl.program_id(0), pl.program_id(1)
        seq_end = ((i + 1) * tm) % seq_len == 0
        dyn = dyn_ref[...]
        dyn = jnp.where(seq_end, jnp.zeros_like(dyn), dyn)
        d_out = jnp.concatenate([dy_ref[...], dyn], axis=0).astype(BF16)
        d_act = _dot_nt(d_out, wd_ref[0])
        cge = jnp.concatenate([pgm_ref[0], pgn_ref[0]], axis=0).astype(F32)
        cve = jnp.concatenate([pvm_ref[0], pvn_ref[0]], axis=0).astype(F32)
        act, vjp_act = jax.vjp(lambda g, v: jax.nn.silu(g) * v, cge, cve)
        dcge, dcve = vjp_act(d_act)
        act_ref[0] = act[:tm].astype(BF16)

        def conv_t(dc, u_ref, cw):
            ahead1 = pltpu.roll(dc, rows - 1, 0)[:tm]
            ahead2 = pltpu.roll(dc, rows - 2, 0)[:tm]
            here = dc[:tm]
            du = cw[2:3] * here + cw[1:2] * ahead1 + cw[0:1] * ahead2
            u = u_ref[0].astype(F32)
            col = lambda x: jnp.sum(x, axis=0, keepdims=True)
            grad = jnp.concatenate([col(ahead2 * u), col(ahead1 * u), col(here * u), col(here),
                                    jnp.zeros((4, fb), F32)], axis=0)
            return du.astype(BF16), grad

        cwg, cwv = cg_ref[0], cv_ref[0]
        dug, grad_g = conv_t(dcge, ug_ref, cwg)
        duv, grad_v = conv_t(dcve, uv_ref, cwv)
        dug_ref[0] = dug
        duv_ref[0] = duv
        part = _dot(dug, wg_ref[0]) + _dot(duv, wv_ref[0])

        @pl.when(j == 0)
        def _():
            acc[...] = part

        @pl.when(j > 0)
        def _():
            acc[...] += part

        @pl.when(j == nj - 1)
        def _():
            dhn_ref[...] = acc[...]

        @pl.when(i == 0)
        def _():
            dcg_ref[j] = jnp.zeros((8, fb), F32)
            dcv_ref[j] = jnp.zeros((8, fb), F32)

        dcg_ref[j] += grad_g
        dcv_ref[j] += grad_v

    row = pl.BlockSpec((tm, D_MODEL), lambda i, j: (i, 0))
    u_main = pl.BlockSpec((1, tm, fb), lambda i, j: (j, i, 0))
    u_next = pl.BlockSpec((1, HALO, fb), lambda i, j: (j, jnp.minimum((i + 1) * hb, last_hb), 0))
    dc_spec = pl.BlockSpec((nj, 8, fb), lambda i, j: (0, 0, 0))
    u_shape = jax.ShapeDtypeStruct((FFN_GROUPS, n, fb), BF16)
    return pl.pallas_call(
        body, name="ffn_bwd", grid=(n // tm, nj),
        in_specs=[row, pl.BlockSpec((HALO, D_MODEL), lambda i, j: (jnp.minimum((i + 1) * hb, last_hb), 0)),
                  u_main, u_main, u_main, u_next, u_main, u_next, *_ffn_weight_specs()],
        out_specs=(u_main, u_main, u_main, row, dc_spec, dc_spec),
        out_shape=(u_shape, u_shape, u_shape, jax.ShapeDtypeStruct((n, D_MODEL), F32),
                   jax.ShapeDtypeStruct((nj, 8, fb), F32), jax.ShapeDtypeStruct((nj, 8, fb), F32)),
        scratch_shapes=[pltpu.VMEM((tm, D_MODEL), F32)],
        compiler_params=_cparams("arbitrary", "arbitrary"),
    )(dy, dy, ug, uv, pg, pg, pv, pv, w_up, w_up, conv, conv, w_down)


def _tn_grouped(a, b, name, shared_a, out_dtype=F32, tn=1024):
    groups = b.shape[0] if shared_a else a.shape[0]
    n_tok = a.shape[0] if shared_a else b.shape[0]
    k_dim, m_dim = a.shape[-1], b.shape[-1]

    def body(a_ref, b_ref, o_ref, acc):
        k = pl.program_id(1)
        a_t = a_ref[...] if shared_a else a_ref[0]
        b_t = b_ref[0] if shared_a else b_ref[...]
        part = _dot_tn(a_t.astype(BF16), b_t.astype(BF16))

        @pl.when(k == 0)
        def _():
            acc[...] = part

        @pl.when(k > 0)
        def _():
            acc[...] += part

        @pl.when(k == n_tok // tn - 1)
        def _():
            o_ref[0] = acc[...].astype(out_dtype)

    plain = lambda w: pl.BlockSpec((tn, w), lambda g, k: (k, 0))
    grouped = lambda w: pl.BlockSpec((1, tn, w), lambda g, k: (g, k, 0))
    return pl.pallas_call(
        body, name=name, grid=(groups, n_tok // tn),
        in_specs=[plain(k_dim), grouped(m_dim)] if shared_a else [grouped(k_dim), plain(m_dim)],
        out_specs=pl.BlockSpec((1, k_dim, m_dim), lambda g, k: (g, 0, 0)),
        out_shape=jax.ShapeDtypeStruct((groups, k_dim, m_dim), out_dtype),
        scratch_shapes=[pltpu.VMEM((k_dim, m_dim), F32)],
        compiler_params=_cparams("parallel", "arbitrary"),
    )(a, b)


def _s5_param_fn(lr, li, ldt, br, bi):
    dt = jnp.exp(ldt)
    mag = jnp.exp(lr * dt)
    ab_re = mag * jnp.cos(li * dt)
    ab_im = mag * jnp.sin(li * dt)
    nr = ab_re - 1.0
    ni = ab_im
    den = lr * lr + li * li
    q_re = (nr * lr + ni * li) / den
    q_im = (ni * lr - nr * li) / den
    bb_re = q_re * br - q_im * bi
    bb_im = q_re * bi + q_im * br
    return ab_re, ab_im, bb_re, bb_im


def _s5_param_fwd(lr, li, ldt, br, bi):
    def body(lr_ref, li_ref, ldt_ref, br_ref, bi_ref, ar_ref, ai_ref, bbr_ref, bbi_ref):
        ar, ai, bbr, bbi = _s5_param_fn(lr_ref[...], li_ref[...], ldt_ref[...], br_ref[...], bi_ref[...])
        ar_ref[...] = ar
        ai_ref[...] = ai
        bbr_ref[...] = bbr
        bbi_ref[...] = bbi

    return pl.pallas_call(
        body, name="s5_param_fwd",
        out_shape=(jax.ShapeDtypeStruct(lr.shape, F32), jax.ShapeDtypeStruct(lr.shape, F32),
                   jax.ShapeDtypeStruct(br.shape, F32), jax.ShapeDtypeStruct(br.shape, F32)),
    )(lr, li, ldt, br, bi)


def _s5_param_bwd(lr, li, ldt, br, bi, dar, dai, dbbr, dbbi):
    def body(lr_ref, li_ref, ldt_ref, br_ref, bi_ref, dar_ref, dai_ref, dbbr_ref, dbbi_ref,
             dlr_ref, dli_ref, dldt_ref, dbr_ref, dbi_ref):
        _, vjp = jax.vjp(_s5_param_fn, lr_ref[...], li_ref[...], ldt_ref[...], br_ref[...], bi_ref[...])
        dlr, dli, dldt, dbr, dbi = vjp((dar_ref[...], dai_ref[...], dbbr_ref[...], dbbi_ref[...]))
        dlr_ref[...] = dlr
        dli_ref[...] = dli
        dldt_ref[...] = dldt
        dbr_ref[...] = dbr
        dbi_ref[...] = dbi

    return pl.pallas_call(
        body, name="s5_param_bwd",
        out_shape=(jax.ShapeDtypeStruct(lr.shape, F32), jax.ShapeDtypeStruct(lr.shape, F32),
                   jax.ShapeDtypeStruct(ldt.shape, F32), jax.ShapeDtypeStruct(br.shape, F32),
                   jax.ShapeDtypeStruct(br.shape, F32)),
    )(lr, li, ldt, br, bi, dar, dai, dbbr, dbbi)


S5_CHUNK = 256
S5_STATES = 512
S5_BLOCKS = 4


def _cpow_rows(ar, ai, count):
    rs, im = [ar], [ai]
    for _ in range(count - 1):
        pr, pi = rs[-1], im[-1]
        rs.append(pr * ar - pi * ai)
        im.append(pr * ai + pi * ar)
    return rs, im


def _scan_in_groups(vr, vi, pr, pi, rm, reverse):
    n, width = vr.shape
    vr = vr.reshape(n // SUBLANES, SUBLANES, width)
    vi = vi.reshape(n // SUBLANES, SUBLANES, width)
    row = rm[0:SUBLANES]
    for k in (1, 2, 4):
        shift = SUBLANES - k if reverse else k
        keep = row < SUBLANES - k if reverse else row >= k
        kr = jnp.where(keep, pr[k - 1], 0.0)
        ki = jnp.where(keep, pi[k - 1], 0.0)
        sr, si = pltpu.roll(vr, shift, 1), pltpu.roll(vi, shift, 1)
        vr, vi = vr + kr * sr - ki * si, vi + kr * si + ki * sr
    return vr.reshape(n, width), vi.reshape(n, width)


def _carry_over_groups(xr_s, xi_s, wr, wi, c0r, c0i, reverse):
    groups = xr_s.shape[0] // SUBLANES
    pick = 0 if reverse else SUBLANES - 1

    def step(q, carry):
        cr, ci = carry
        r = groups - 1 - q if reverse else q
        o = pl.multiple_of(r * SUBLANES, SUBLANES)
        vr = xr_s[pl.ds(o, SUBLANES), :]
        vi = xi_s[pl.ds(o, SUBLANES), :]
        nr = vr + wr * cr - wi * ci
        ni = vi + wr * ci + wi * cr
        xr_s[pl.ds(o, SUBLANES), :] = nr
        xi_s[pl.ds(o, SUBLANES), :] = ni
        return (jnp.broadcast_to(nr[pick:pick + 1], nr.shape), jnp.broadcast_to(ni[pick:pick + 1], ni.shape))

    return lax.fori_loop(0, groups, step, (c0r, c0i), unroll=4)


def _s5_state_scan(u_b, bbr, bbi, pr, pi, rm, xr_s, xi_s, c0r, c0i):
    bur = _dot(u_b, bbr)
    bui = _dot(u_b, bbi)
    bur, bui = _scan_in_groups(bur, bui, pr, pi, rm, False)
    xr_s[...] = bur
    xi_s[...] = bui
    w8r = jnp.concatenate(pr, axis=0)
    w8i = jnp.concatenate(pi, axis=0)
    return _carry_over_groups(xr_s, xi_s, w8r, w8i, c0r, c0i, False)


def _s5_fwd(u, a_re, a_im, bbr, bbi, cr, ci, d_skip, n_seq):
    n = u.shape[0]
    seq_len = n // n_seq
    nt = seq_len // S5_CHUNK
    tc = S5_CHUNK

    def body(u_ref, ar_ref, ai_ref, bbr_ref, bbi_ref, cr_ref, ci_ref, d_ref, y_ref, str_ref, sti_ref, xrb_ref, xib_ref,
             xr_s, xi_s, car_r, car_i):
        t = pl.program_id(2)

        @pl.when(t == 0)
        def _():
            car_r[...] = jnp.zeros_like(car_r)
            car_i[...] = jnp.zeros_like(car_i)
        pr, pi = _cpow_rows(ar_ref[0], ai_ref[0], SUBLANES)
        rm = lax.broadcasted_iota(jnp.int32, (tc, S5_STATES), 0) & (SUBLANES - 1)
        str_ref[0, 0] = car_r[...]
        sti_ref[0, 0] = car_i[...]
        u_t = u_ref[...]
        cfr, cfi = _s5_state_scan(u_t.astype(BF16), bbr_ref[0], bbi_ref[0], pr, pi, rm, xr_s, xi_s,
                                  car_r[...], car_i[...])
        car_r[...] = cfr
        car_i[...] = cfi
        xr_b = xr_s[...].astype(BF16)
        xi_b = xi_s[...].astype(BF16)
        xrb_ref[...] = xr_b
        xib_ref[...] = xi_b
        y_ref[...] = _dot(xr_b, cr_ref[0]) - _dot(xi_b, ci_ref[0]) + d_ref[...] * u_t

    x_spec = pl.BlockSpec((tc, S5_STATES), lambda cb, b, t: (b * nt + t, cb))
    x_shape = jax.ShapeDtypeStruct((n, S5_BLOCKS * S5_STATES), BF16)
    u_spec = pl.BlockSpec((tc, LANES), lambda cb, b, t: (b * nt + t, cb))
    a_spec = pl.BlockSpec((1, 1, S5_STATES), lambda cb, b, t: (cb, 0, 0))
    bb_spec = pl.BlockSpec((1, LANES, S5_STATES), lambda cb, b, t: (cb, 0, 0))
    c_spec = pl.BlockSpec((1, S5_STATES, LANES), lambda cb, b, t: (cb, 0, 0))
    st_spec = pl.BlockSpec((1, 1, SUBLANES, S5_STATES), lambda cb, b, t: (cb, b * nt + t, 0, 0))
    st_shape = jax.ShapeDtypeStruct((S5_BLOCKS, n_seq * nt, SUBLANES, S5_STATES), F32)
    return pl.pallas_call(
        body, name="s5_fwd", grid=(S5_BLOCKS, n_seq, nt),
        in_specs=[u_spec, a_spec, a_spec, bb_spec, bb_spec, c_spec, c_spec,
                  pl.BlockSpec((1, LANES), lambda cb, b, t: (0, cb))],
        out_specs=(u_spec, st_spec, st_spec, x_spec, x_spec),
        out_shape=(jax.ShapeDtypeStruct((n, D_SSM), F32), st_shape, st_shape, x_shape, x_shape),
        scratch_shapes=[pltpu.VMEM((tc, S5_STATES), F32), pltpu.VMEM((tc, S5_STATES), F32),
                        pltpu.VMEM((SUBLANES, S5_STATES), F32), pltpu.VMEM((SUBLANES, S5_STATES), F32)],
        compiler_params=_cparams("parallel", "arbitrary", "arbitrary"),
    )(u, a_re, a_im, bbr, bbi, cr, ci, d_skip)


def _s5_bwd(u, dy, xs_r, xs_i, st_r, st_i, a_re, a_im, bbr, bbi, cr, ci, d_skip, n_seq):
    n = u.shape[0]
    seq_len = n // n_seq
    nt = seq_len // S5_CHUNK
    tc = S5_CHUNK

    def body(u_ref, dy_ref, xrb_ref, xib_ref, str_ref, sti_ref, ar_ref, ai_ref, bbr_ref, bbi_ref, cr_ref, ci_ref, d_ref,
             du_ref, dbbr_ref, dbbi_ref, dcr_ref, dci_ref, dar_ref, dai_ref, dd_ref,
             gr_s, gi_s, car_r, car_i):
        b, t = pl.program_id(1), pl.program_id(2)

        @pl.when((b == 0) & (t == 0))
        def _():
            for ref in (dbbr_ref, dbbi_ref, dcr_ref, dci_ref, dar_ref, dai_ref, dd_ref):
                ref[...] = jnp.zeros_like(ref)

        @pl.when(t == 0)
        def _():
            car_r[...] = jnp.zeros_like(car_r)
            car_i[...] = jnp.zeros_like(car_i)
        ar, ai = ar_ref[0], ai_ref[0]
        pr, pi = _cpow_rows(ar, ai, SUBLANES)
        row = lax.broadcasted_iota(jnp.int32, (tc, S5_STATES), 0)
        rm = row & (SUBLANES - 1)
        u_t = u_ref[...]
        u_b = u_t.astype(BF16)
        dy_t = dy_ref[...]
        dy_b = dy_t.astype(BF16)
        s0r, s0i = str_ref[0, 0], sti_ref[0, 0]
        xr_b, xi_b = xrb_ref[...], xib_ref[...]
        xr, xi = xr_b.astype(F32), xi_b.astype(F32)
        gr = _dot_nt(dy_b, cr_ref[0])
        gi = -_dot_nt(dy_b, ci_ref[0])
        npi = [-v for v in pi]
        gr, gi = _scan_in_groups(gr, gi, pr, npi, rm, True)
        gr_s[...] = gr
        gi_s[...] = gi
        w8r = jnp.concatenate(pr[::-1], axis=0)
        w8i = jnp.concatenate(npi[::-1], axis=0)
        cfr, cfi = _carry_over_groups(gr_s, gi_s, w8r, w8i, car_r[...], car_i[...], True)
        car_r[...] = cfr
        car_i[...] = cfi
        gr, gi = gr_s[...], gi_s[...]
        gr_b, gi_b = gr.astype(BF16), gi.astype(BF16)
        du_ref[...] = _dot_nt(gr_b, bbr_ref[0]) + _dot_nt(gi_b, bbi_ref[0]) + d_ref[...] * dy_t
        dbbr_ref[0] += _dot_tn(u_b, gr_b)
        dbbi_ref[0] += _dot_tn(u_b, gi_b)
        dcr_ref[0] += _dot_tn(xr_b, dy_b)
        dci_ref[0] -= _dot_tn(xi_b, dy_b)
        dd_ref[0] += jnp.sum((dy_t * u_t).reshape(tc // SUBLANES, SUBLANES, LANES), axis=0)
        first = row == 0
        xpr = jnp.where(first, jnp.broadcast_to(s0r[0:1], xr.shape), pltpu.roll(xr, 1, 0))
        xpi = jnp.where(first, jnp.broadcast_to(s0i[0:1], xi.shape), pltpu.roll(xi, 1, 0))
        shp = (tc // SUBLANES, SUBLANES, S5_STATES)
        dar_ref[0] += jnp.sum((gr * xpr + gi * xpi).reshape(shp), axis=0)
        dai_ref[0] += jnp.sum((gi * xpr - gr * xpi).reshape(shp), axis=0)

    u_spec = pl.BlockSpec((tc, LANES), lambda cb, b, t: (b * nt + nt - 1 - t, cb))
    a_spec = pl.BlockSpec((1, 1, S5_STATES), lambda cb, b, t: (cb, 0, 0))
    bb_spec = pl.BlockSpec((1, LANES, S5_STATES), lambda cb, b, t: (cb, 0, 0))
    c_spec = pl.BlockSpec((1, S5_STATES, LANES), lambda cb, b, t: (cb, 0, 0))
    st_spec = pl.BlockSpec((1, 1, SUBLANES, S5_STATES), lambda cb, b, t: (cb, b * nt + nt - 1 - t, 0, 0))
    da_spec = pl.BlockSpec((1, SUBLANES, S5_STATES), lambda cb, b, t: (cb, 0, 0))
    dd_spec = pl.BlockSpec((1, SUBLANES, LANES), lambda cb, b, t: (cb, 0, 0))
    big = pltpu.VMEM((tc, S5_STATES), F32)
    small = pltpu.VMEM((SUBLANES, S5_STATES), F32)
    x_spec = pl.BlockSpec((tc, S5_STATES), lambda cb, b, t: (b * nt + nt - 1 - t, cb))
    return pl.pallas_call(
        body, name="s5_bwd", grid=(S5_BLOCKS, n_seq, nt),
        in_specs=[u_spec, u_spec, x_spec, x_spec, st_spec, st_spec, a_spec, a_spec, bb_spec, bb_spec, c_spec, c_spec,
                  pl.BlockSpec((1, LANES), lambda cb, b, t: (0, cb))],
        out_specs=(u_spec, bb_spec, bb_spec, c_spec, c_spec, da_spec, da_spec, dd_spec),
        out_shape=(jax.ShapeDtypeStruct((n, D_SSM), F32),
                   jax.ShapeDtypeStruct((S5_BLOCKS, LANES, S5_STATES), F32),
                   jax.ShapeDtypeStruct((S5_BLOCKS, LANES, S5_STATES), F32),
                   jax.ShapeDtypeStruct((S5_BLOCKS, S5_STATES, LANES), F32),
                   jax.ShapeDtypeStruct((S5_BLOCKS, S5_STATES, LANES), F32),
                   jax.ShapeDtypeStruct((S5_BLOCKS, SUBLANES, S5_STATES), F32),
                   jax.ShapeDtypeStruct((S5_BLOCKS, SUBLANES, S5_STATES), F32),
                   jax.ShapeDtypeStruct((S5_BLOCKS, SUBLANES, LANES), F32)),
        scratch_shapes=[big, big, small, small],
        compiler_params=_cparams("parallel", "arbitrary", "arbitrary"),
    )(u, dy, xs_r, xs_i, st_r, st_i, a_re, a_im, bbr, bbi, cr, ci, d_skip)


CUM_BLOCK = 128


def _tri(lower):
    r = lax.broadcasted_iota(jnp.int32, (CUM_BLOCK, CUM_BLOCK), 0)
    c = lax.broadcasted_iota(jnp.int32, (CUM_BLOCK, CUM_BLOCK), 1)
    return jnp.where(r >= c if lower else r <= c, 1.0, 0.0).astype(F32)


def _fprep_fwd(fl, bf, n_seq):
    n = fl.shape[0]
    seq_len = n // n_seq
    nb = seq_len // CUM_BLOCK

    def body(fl_ref, bf_ref, cum_ref):
        tril = _tri(True)
        carry = jnp.zeros((1, LANES), F32)
        for blk in range(nb):
            rows = slice(blk * CUM_BLOCK, (blk + 1) * CUM_BLOCK)
            lf = jax.nn.log_sigmoid(fl_ref[rows, :] + bf_ref[...])
            cs = jnp.dot(tril, lf, preferred_element_type=F32, precision=HIGHEST) + carry
            cum_ref[rows, :] = cs
            carry = cs[CUM_BLOCK - 1:CUM_BLOCK, :]

    spec = pl.BlockSpec((seq_len, LANES), lambda b: (b, 0))
    return pl.pallas_call(
        body, name="fprep_fwd", grid=(n_seq,), in_specs=[spec, pl.BlockSpec((1, LANES), lambda b: (0, 0))],
        out_specs=spec, out_shape=jax.ShapeDtypeStruct((n, LANES), F32), compiler_params=_cparams("parallel"),
    )(fl, bf)


def _fprep_bwd(dcum, fl, bf, n_seq):
    n = fl.shape[0]
    seq_len = n // n_seq
    nb = seq_len // CUM_BLOCK

    def body(dcum_ref, fl_ref, bf_ref, dfl_ref, dbf_ref):
        triu = _tri(False)
        lane = lax.broadcasted_iota(jnp.int32, (CUM_BLOCK, LANES), 1)
        carry = jnp.zeros((1, LANES), F32)
        total = jnp.zeros((1, LANES), F32)
        for blk in reversed(range(nb)):
            rows = slice(blk * CUM_BLOCK, (blk + 1) * CUM_BLOCK)
            rs = jnp.dot(triu, dcum_ref[rows, :], preferred_element_type=F32, precision=HIGHEST) + carry
            carry = rs[0:1, :]
            _, vjp = jax.vjp(jax.nn.log_sigmoid, fl_ref[rows, :] + bf_ref[...])
            dz = jnp.where(lane < N_HEADS, vjp(rs)[0], 0.0)
            dfl_ref[rows, :] = dz
            total = total + jnp.sum(dz, axis=0, keepdims=True)
        dbf_ref[0] = total

    spec = pl.BlockSpec((seq_len, LANES), lambda b: (b, 0))
    return pl.pallas_call(
        body, name="fprep_bwd", grid=(n_seq,), in_specs=[spec, spec, pl.BlockSpec((1, LANES), lambda b: (0, 0))],
        out_specs=(spec, pl.BlockSpec((1, 1, LANES), lambda b: (b, 0, 0))),
        out_shape=(jax.ShapeDtypeStruct((n, LANES), F32), jax.ShapeDtypeStruct((n_seq, 1, LANES), F32)),
        compiler_params=_cparams("parallel"),
    )(dcum, fl, bf)


ATT_TQ = 256
ATT_KSTEP = 256
ATT_SCALE = HEAD_DIM ** -0.5
NEG_BIG = -1e30


assert ATT_KSTEP == ATT_TQ


def _scores(q_scaled, kb, row_bias, ck, kend):
    s = _dot_nt(q_scaled, kb) - ck
    if row_bias is not None:
        s = s + row_bias
    r = lax.broadcasted_iota(jnp.int32, (ATT_TQ, ATT_TQ), 0)
    c = lax.broadcasted_iota(jnp.int32, (ATT_TQ, ATT_TQ), 1)
    diag = jnp.where(r >= c, s[:, kend - ATT_TQ:], NEG_BIG)
    return diag if kend == ATT_TQ else jnp.concatenate([s[:, :kend - ATT_TQ], diag], axis=1)


def _attn_specs(n_seq, seq_len):
    nq = seq_len // ATT_TQ
    q_spec = pl.BlockSpec((ATT_TQ, LANES), lambda b, h, q: (b * nq + q, h))
    k_spec = pl.BlockSpec((seq_len, LANES), lambda b, h, q: (b, N_HEADS // 2 + h))
    v_spec = pl.BlockSpec((seq_len, LANES), lambda b, h, q: (b, N_HEADS + h))
    cq_spec = pl.BlockSpec((1, 2, ATT_TQ, 1), lambda b, h, q: (b, h, q, 0))
    ck_spec = pl.BlockSpec((1, 2, 1, seq_len), lambda b, h, q: (b, h, 0, 0))
    return nq, q_spec, k_spec, v_spec, cq_spec, ck_spec


def _own_cum(cum_ref, e):
    lane = lax.broadcasted_iota(jnp.int32, (1, LANES), 1)
    return jnp.sum(jnp.where(lane == 2 * pl.program_id(1) + e, cum_ref[...], 0.0), axis=1, keepdims=True)


def _head_selectors():
    head0 = lax.broadcasted_iota(jnp.int32, (1, LANES), 1) < HEAD_DIM
    return head0, (head0, jnp.logical_not(head0))


def _for_key_range(qi, seq_len, run):
    per = ATT_KSTEP // ATT_TQ
    for g in range(seq_len // ATT_KSTEP):
        pl.when(qi // per == g)(functools.partial(run, (g + 1) * ATT_KSTEP))


def _attn_fwd(qkv, cum, ck, n_seq):
    n = qkv.shape[0]
    seq_len = n // n_seq
    nq, q_spec, k_spec, v_spec, cq_spec, ck_spec = _attn_specs(n_seq, seq_len)
    cum_spec = pl.BlockSpec((ATT_TQ, LANES), lambda b, h, q: (b * nq + q, 0))

    def body(q_ref, k_ref, v_ref, cum_ref, ck_ref, o_ref, lse_ref):
        qi = pl.program_id(2)
        q2 = q_ref[...]
        head0, sels = _head_selectors()
        qe = [jnp.where(sel, q2 * ATT_SCALE, 0.0).astype(BF16) for sel in sels]

        def run(kend):
            kb = k_ref[0:kend, :].astype(BF16)
            vb = v_ref[0:kend, :].astype(BF16)
            outs = []
            for e in range(2):
                s = _scores(qe[e], kb, None, ck_ref[0, e, :, 0:kend], kend)
                mx = jnp.max(s, axis=1, keepdims=True)
                p = jnp.exp(s - mx)
                den = jnp.sum(p, axis=1, keepdims=True)
                outs.append(_dot(p.astype(BF16), vb) / den)
                lse_ref[0, e] = _own_cum(cum_ref, e) + mx + jnp.log(den)
            o_ref[...] = jnp.where(head0, outs[0], outs[1])

        _for_key_range(qi, seq_len, run)

    return pl.pallas_call(
        body, name="attn_fwd", grid=(n_seq, N_HEADS // 2, nq),
        in_specs=[q_spec, k_spec, v_spec, cum_spec, ck_spec],
        out_specs=(q_spec, cq_spec),
        out_shape=(jax.ShapeDtypeStruct((n, D_ATTN), F32), jax.ShapeDtypeStruct((n_seq, N_HEADS, seq_len, 1), F32)),
        compiler_params=_cparams("parallel", "parallel", "parallel"),
    )(qkv, qkv, qkv, cum, ck)


def _attn_bwd(qkv, cum, ck, o, do, lse, n_seq):
    n = qkv.shape[0]
    seq_len = n // n_seq
    nq, q_spec, k_spec, v_spec, cq_spec, ck_spec = _attn_specs(n_seq, seq_len)
    kv_out = pl.BlockSpec((seq_len, LANES), lambda b, h, q: (b, h))
    cum_spec = pl.BlockSpec((ATT_TQ, LANES), lambda b, h, q: (b * nq + q, 0))

    def body(q_ref, k_ref, v_ref, cum_ref, ck_ref, o_ref, do_ref, lse_ref, dq_ref, dk_ref, dv_ref, dcq_ref, dck_ref):
        qi = pl.program_id(2)

        @pl.when(qi == 0)
        def _():
            dk_ref[...] = jnp.zeros_like(dk_ref)
            dv_ref[...] = jnp.zeros_like(dv_ref)
            dck_ref[...] = jnp.zeros_like(dck_ref)
        q2 = q_ref[...]
        do2 = do_ref[...]
        o2 = o_ref[...]
        head0, sels = _head_selectors()
        qe = [jnp.where(sel, q2 * ATT_SCALE, 0.0).astype(BF16) for sel in sels]
        doe = [jnp.where(sel, do2, 0.0) for sel in sels]
        doe_b = [d.astype(BF16) for d in doe]
        delta = [jnp.sum(d * o2, axis=1, keepdims=True) for d in doe]

        def run(kend):
            kb = k_ref[0:kend, :].astype(BF16)
            vb = v_ref[0:kend, :].astype(BF16)
            dqs = []
            dk = jnp.zeros((kend, LANES), F32)
            dv = jnp.zeros((kend, LANES), F32)
            for e in range(2):
                p = jnp.exp(_scores(qe[e], kb, _own_cum(cum_ref, e) - lse_ref[0, e], ck_ref[0, e, :, 0:kend], kend))
                ds = p * (_dot_nt(doe_b[e], vb) - delta[e])
                ds_b = ds.astype(BF16)
                dqs.append(_dot(ds_b, kb))
                dk = dk + _dot_tn(ds_b, qe[e])
                dv = dv + _dot_tn(p.astype(BF16), doe_b[e])
                dcq_ref[0, e] = jnp.sum(ds, axis=1, keepdims=True)
                dck_ref[0, e, :, 0:kend] -= jnp.sum(ds, axis=0, keepdims=True)
            dk_ref[0:kend, :] += dk
            dv_ref[0:kend, :] += dv
            dq_ref[...] = jnp.where(head0, dqs[0], dqs[1]) * ATT_SCALE

        _for_key_range(qi, seq_len, run)

    return pl.pallas_call(
        body, name="attn_bwd", grid=(n_seq, N_HEADS // 2, nq),
        in_specs=[q_spec, k_spec, v_spec, cum_spec, ck_spec, q_spec, q_spec, cq_spec],
        out_specs=(q_spec, kv_out, kv_out, cq_spec, ck_spec),
        out_shape=(jax.ShapeDtypeStruct((n, D_ATTN), F32), jax.ShapeDtypeStruct((n, D_ATTN), F32),
                   jax.ShapeDtypeStruct((n, D_ATTN), F32),
                   jax.ShapeDtypeStruct((n_seq, N_HEADS, seq_len, 1), F32),
                   jax.ShapeDtypeStruct((n_seq, N_HEADS, 1, seq_len), F32)),
        compiler_params=_cparams("parallel", "parallel", "arbitrary"),
    )(qkv, qkv, qkv, cum, ck, o, do, lse)


WEIGHT_NAMES = ("norm_mix", "w_in", "b_forget", "lam_re", "lam_im", "b_re", "b_im", "c_re", "c_im", "d_skip", "log_dt",
                "w_glu", "b_glu", "q_norm", "k_norm", "norm_out_ssm", "norm_out_attn", "w_out", "norm_ffn", "w_up",
                "conv_w", "conv_b", "w_down")
SHARDED = ("w_in", "w_glu", "w_out", "w_up", "conv_w", "w_down")
ADAM_TILE = {"w_in": (257, 256), "w_glu": (64, 512), "w_out": (128, 1024), "w_up": (688, 256), "conv_w": (3, 688),
             "w_down": (344, 1024)}
PACK_ROWS = SUBLANES * LANES


def _after_all(*arrays):
    return sum(a[(0,) * a.ndim].astype(F32) for a in arrays).reshape(1, 1)


def _pad_to(a, axis, size):
    pad = [(0, 0)] * a.ndim
    pad[axis] = (0, size - a.shape[axis])
    return jnp.pad(a, pad)


def _block_diag(t, transpose):
    t4 = t.reshape(S5_BLOCKS, 8, SSM_GROUP, SSM_STATE)
    eye = jnp.eye(8, dtype=t.dtype)
    if transpose:
        e = jnp.swapaxes(t4, 2, 3)[:, :, :, None, :] * eye[None, :, None, :, None]
        return e.reshape(S5_BLOCKS, S5_STATES, LANES)
    e = t4[:, :, :, None, :] * eye[None, :, None, :, None]
    return e.reshape(S5_BLOCKS, LANES, S5_STATES)


def _block_diag_extract(m, transpose):
    if transpose:
        m5 = m.reshape(S5_BLOCKS, 8, SSM_STATE, 8, SSM_GROUP)
        d = jnp.stack([m5[:, i, :, i, :] for i in range(8)], axis=1)
        return jnp.swapaxes(d, 2, 3).reshape(N_GROUPS, SSM_GROUP, SSM_STATE)
    m5 = m.reshape(S5_BLOCKS, 8, SSM_GROUP, 8, SSM_STATE)
    d = jnp.stack([m5[:, i, :, i, :] for i in range(8)], axis=1)
    return d.reshape(N_GROUPS, SSM_GROUP, SSM_STATE)


def _pack(pieces):
    flat = jnp.concatenate([p.reshape(-1).astype(F32) for p in pieces])
    size = -(-flat.shape[0] // PACK_ROWS) * PACK_ROWS
    return _pad_to(flat, 0, size).reshape(-1, LANES)


def _unpack(packed, shapes):
    flat = packed.reshape(-1)
    out, off = [], 0
    for shp in shapes:
        size = math.prod(shp)
        out.append(flat[off:off + size].reshape(shp))
        off += size
    return out


def kernel(x, norm_mix, w_in, b_forget, lam_re, lam_im, b_re, b_im, c_re, c_im, d_skip, log_dt, w_glu, b_glu, q_norm, k_norm, norm_out_ssm, norm_out_attn, w_out, norm_ffn, w_up, conv_w, conv_b, w_down, loss_target, m_norm_mix, m_w_in, m_b_forget, m_lam_re, m_lam_im, m_b_re, m_b_im, m_c_re, m_c_im, m_d_skip, m_log_dt, m_w_glu, m_b_glu, m_q_norm, m_k_norm, m_norm_out_ssm, m_norm_out_attn, m_w_out, m_norm_ffn, m_w_up, m_conv_w, m_conv_b, m_w_down, v_norm_mix, v_w_in, v_b_forget, v_lam_re, v_lam_im, v_b_re, v_b_im, v_c_re, v_c_im, v_d_skip, v_log_dt, v_w_glu, v_b_glu, v_q_norm, v_k_norm, v_norm_out_ssm, v_norm_out_attn, v_w_out, v_norm_ffn, v_w_up, v_conv_w, v_conv_b, v_w_down):
    given = dict(locals())
    weights = {k: given[k] for k in WEIGHT_NAMES}
    mom1 = {k: given["m_" + k] for k in WEIGHT_NAMES}
    mom2 = {k: given["v_" + k] for k in WEIGHT_NAMES}
    n_seq, seq_len, _ = x.shape
    n = n_seq * seq_len
    xf = x.reshape(n, D_MODEL)
    target = loss_target.reshape(n, D_MODEL)
    me_idx = 4 * lax.axis_index("x") + 2 * lax.axis_index("y") + lax.axis_index("c")

    in_flags = [False] * 2
    in_sems = _exchange_start([jnp.swapaxes(w_in[0], 0, 1).astype(BF16), conv_w[0]], in_flags, norm_mix,
                              "gather_in_start", 3, NEAR_PEERS)
    fill_own = lambda got, mine: lax.dynamic_update_index_in_dim(got, mine, me_idx, 0)

    lr3 = lam_re[0].reshape(N_GROUPS, 1, SSM_STATE)
    li3 = lam_im[0].reshape(N_GROUPS, 1, SSM_STATE)
    ldt3 = log_dt[0].reshape(N_GROUPS, 1, 1)
    br_t = jnp.swapaxes(b_re[0], 1, 2)
    bi_t = jnp.swapaxes(b_im[0], 1, 2)
    ab_re, ab_im, bb_re, bb_im = _s5_param_fwd(lr3, li3, ldt3, br_t, bi_t)
    a_re = ab_re.reshape(S5_BLOCKS, 1, S5_STATES)
    a_im = ab_im.reshape(S5_BLOCKS, 1, S5_STATES)
    bbr = _block_diag(bb_re, False).astype(BF16)
    bbi = _block_diag(bb_im, False).astype(BF16)
    cr = _block_diag(c_re[0], True).astype(BF16)
    ci = _block_diag(c_im[0], True).astype(BF16)
    avg = jnp.kron(jnp.eye(N_HEADS, dtype=F32), jnp.full((HEAD_DIM, HEAD_DIM), 1.0 / HEAD_DIM, F32)).astype(BF16)
    qg = jnp.tile(q_norm, (1, N_HEADS))
    kg = jnp.tile(k_norm, (1, N_HEADS))
    row_shards = [w_down[0].astype(BF16), w_out[0].astype(BF16), w_glu[0].astype(BF16)]

    (own_in, own_cw), near = _exchange_wait(in_sems[0], in_sems[1], in_sems[2], in_sems[3], in_flags,
                                            _after_all(a_re, a_im, bbr, bbi, cr, ci, avg, qg, kg, *row_shards),
                                            "gather_in_wait", NEAR_PEERS)
    relay = _relay_start(list(near), "gather_in_relay_start", 6)
    g_in, g_cw = _relay_wait(relay[0], relay[1], relay[2], relay[3], "gather_in_relay_wait")
    g_in = fill_own(g_in, own_in)
    g_cw = fill_own(g_cw, own_cw)
    row_flags = [False] * 3
    r_sems = _exchange_start(row_shards, row_flags, g_in, "gather_rows_start", 0)
    u_sems = _exchange_start([jnp.swapaxes(w_up[0], 0, 1).astype(BF16)], [False], r_sems[4], "gather_up_start", 5)
    norm_mix = norm_mix + u_sems[4][0, 0]
    w_in_p = _pad_to(g_in.reshape(D_IN, D_MODEL), 0, D_IN_PAD)

    hn, u, qkv, raw, fl = _inproj_fwd(xf, norm_mix, w_in_p, avg, qg, kg)
    yc, st_r, st_i, xs_r, xs_i = _s5_fwd(u, a_re, a_im, bbr, bbi, cr, ci, d_skip, n_seq)
    bf = _pad_to(b_forget, 1, LANES)
    cum = _fprep_fwd(fl, bf, n_seq)
    cum8 = jnp.swapaxes(cum[:, :N_HEADS].reshape(n_seq, seq_len, N_HEADS), 1, 2)
    ck = cum8[:, :, None, :]
    ya, lse = _attn_fwd(qkv, cum, ck, n_seq)
    own_rows, got_rows = _exchange_wait(r_sems[0], r_sems[1], r_sems[2], r_sems[3], row_flags, ya, "gather_rows_wait")
    g_down, g_out, g_glu = [fill_own(g, o) for o, g in zip(own_rows, got_rows)]
    w_glu_f = g_glu.reshape(D_SSM, D_SSM)
    w_out_f = g_out.reshape(D_MODEL, D_MODEL)
    conv_st = _pad_to(jnp.concatenate([g_cw, conv_b.reshape(N_DEV, 1, -1)], axis=1), 1, SUBLANES)
    w_down4 = g_down.reshape(FFN_GROUPS, FFN_GROUP, D_MODEL)
    ys = _glu_fwd(yc, w_glu_f, b_glu)
    h1, hn2, mixed = _mix_fwd(xf, ys, ya, norm_out_ssm, norm_out_attn, w_out_f, norm_ffn)
    (own_up,), (g_up,) = _exchange_wait(u_sems[0], u_sems[1], u_sems[2], u_sems[3], [False], _after_all(h1, hn2),
                                        "gather_up_wait")
    g_up = fill_own(g_up, own_up)
    ug, uv, pg, pv, dy, loss_part = _ffn_fwd(hn2, h1, target, g_up, conv_st, w_down4, seq_len)
    loss_local = 0.5 * jnp.sum(loss_part) / D_MODEL

    dug, duv, act, dhn2, dcg, dcv = _ffn_bwd(dy, ug, uv, pg, pv, g_up, conv_st, w_down4, seq_len)
    dh1, dys, dya, d_gs, d_ga, d_gf = _mix_bwd(dy, dhn2[None], h1, ys, ya, norm_out_ssm, norm_out_attn, w_out_f, norm_ffn)
    dyc, gl_b, dz_b, d_bglu = _glu_bwd(yc, dys, w_glu_f, b_glu)

    gw_glu = _tn_matmul(gl_b, dz_b, "dw_glu", D_SSM, D_SSM, out_dtype=BF16)
    gw_out = _tn_matmul(mixed, dh1, "dw_out", D_MODEL, D_MODEL, out_dtype=BF16)
    gw_up = jnp.concatenate([_tn_grouped(dug, hn2, "dw_up_gate", False, BF16),
                             _tn_grouped(duv, hn2, "dw_up_val", False, BF16)], axis=0)
    gw_down = _tn_grouped(act, dy, "dw_down", False, BF16)
    g_conv = jnp.concatenate([dcg, dcv], axis=0)
    by_cols = lambda g, c: jnp.swapaxes(g.reshape(g.shape[0], N_DEV, c), 0, 1)
    early_flags = [True] * 4
    early_names = ("w_down", "w_out", "w_glu", "w_up")
    g_sems = _exchange_start(
        [gw_down.reshape(N_DEV, -1, D_MODEL), gw_out.reshape(N_DEV, -1, D_MODEL), gw_glu.reshape(N_DEV, -1, D_SSM), gw_up],
        early_flags, dyc, "grad_early_start", 1)
    started = g_sems[4][0, 0]

    du, dbbr, dbbi, dcr, dci, dar, dai, ddk = _s5_bwd(u, dyc, xs_r, xs_i, st_r, st_i, a_re, a_im, bbr, bbi, cr, ci,
                                                      d_skip + started, n_seq)
    partial_early = {
        "ab_re": jnp.sum(dar, axis=1), "ab_im": jnp.sum(dai, axis=1),
        "bb_re": _block_diag_extract(dbbr, False), "bb_im": _block_diag_extract(dbbi, False),
        "c_re": _block_diag_extract(dcr, True), "c_im": _block_diag_extract(dci, True),
        "d_skip": jnp.sum(ddk, axis=1), "b_glu": d_bglu,
        "norm_out_ssm": d_gs, "norm_out_attn": d_ga, "norm_ffn": d_gf, "conv_b": g_conv[:, 3],
    }
    early_keys = tuple(partial_early)
    early_shapes = [partial_early[k].shape for k in early_keys]
    p_sems = _exchange_start([_pack([partial_early[k] for k in early_keys])], [False], du, "small_early_start", 2)
    started = started + p_sems[4][0, 0]

    dqn, dkn, dv, dcq, dck = _attn_bwd(qkv, cum, ck + started, ya, dya, lse, n_seq)
    dcum8 = dcq[:, :, :, 0] + dck.reshape(n_seq, N_HEADS, seq_len)
    dcum = _pad_to(jnp.swapaxes(dcum8, 1, 2).reshape(n, N_HEADS), 1, LANES)
    dfl, dbf = _fprep_bwd(dcum, fl, bf, n_seq)
    dx, dproj, d_gmix, d_qg, d_kg = _inproj_bwd(xf, norm_mix, w_in_p, avg, qg, kg, raw, du, dqn, dkn, dv, dfl, dh1)

    gw_in = _tn_matmul(dproj, hn, "dw_in", D_IN_PAD, D_MODEL, out_rows=D_IN, out_dtype=BF16)
    partial_late = {
        "norm_mix": d_gmix, "b_forget": jnp.sum(dbf, axis=(0, 1))[:N_HEADS],
        "q_norm": jnp.sum(d_qg.reshape(N_HEADS, HEAD_DIM), axis=0),
        "k_norm": jnp.sum(d_kg.reshape(N_HEADS, HEAD_DIM), axis=0), "loss": loss_local.reshape(1),
    }
    late_keys = tuple(partial_late)
    late_shapes = [partial_late[k].shape for k in late_keys]

    late_flags = [True, True, False]
    l_sems = _exchange_start(
        [gw_in.reshape(N_DEV, D_IN // N_DEV, D_MODEL).astype(BF16), g_conv[:, :3],
         _pack([partial_late[k] for k in late_keys])],
        late_flags, dx, "grad_late_start", 4)
    early_src, early_land = _exchange_wait(g_sems[0], g_sems[1], g_sems[2], g_sems[3], early_flags, l_sems[4],
                                           "grad_early_wait")
    land = dict(zip(early_names, early_land))
    land_up = land["w_up"]
    own = {k: lax.dynamic_index_in_dim(s, me_idx, 0, keepdims=False) for k, s in zip(early_names, early_src)}
    grads, deltas, new_m, new_v = {}, {}, {}, {}

    def adam_shard(name):
        flip = (lambda a: jnp.swapaxes(a, 1, 2)) if name in ("w_in", "w_up") else (lambda a: a)
        outs = _adam_sharded(land[name], own[name], flip(weights[name]), flip(mom1[name]), flip(mom2[name]),
                             "adam_" + name, ADAM_TILE[name])
        grads[name], deltas[name], new_m[name], new_v[name] = [flip(o) for o in outs]

    for name in ("w_up", "w_down", "w_out", "w_glu"):
        adam_shard(name)
    (own_pack,), (early_parts,) = _exchange_wait(p_sems[0], p_sems[1], p_sems[2], p_sems[3], [False], land_up,
                                                 "small_early_wait")
    early_sum = _sum_partials(early_parts, own_pack, "sum_early_partials")
    (src_in, src_cw, own_late), (land["w_in"], land["conv_w"], late_parts) = _exchange_wait(
        l_sems[0], l_sems[1], l_sems[2], l_sems[3], late_flags,
        _after_all(early_sum, *[new_v[k] for k in ("w_up", "w_down", "w_out", "w_glu")]), "grad_late_wait")
    own["w_in"] = lax.dynamic_index_in_dim(src_in, me_idx, 0, keepdims=False)
    own["conv_w"] = lax.dynamic_index_in_dim(src_cw, me_idx, 0, keepdims=False)
    for name in ("w_in", "conv_w"):
        adam_shard(name)

    summed = dict(zip(late_keys, _unpack(_sum_partials(late_parts, own_late, "sum_late_partials"), late_shapes)))
    summed.update(zip(early_keys, _unpack(early_sum, early_shapes)))
    dlr, dli, dldt, dbr_t, dbi_t = _s5_param_bwd(
        lr3, li3, ldt3, br_t, bi_t, summed["ab_re"].reshape(lr3.shape), summed["ab_im"].reshape(lr3.shape),
        summed["bb_re"], summed["bb_im"])
    small_grads = {
        "norm_mix": summed["norm_mix"], "b_forget": summed["b_forget"], "lam_re": dlr, "lam_im": dli,
        "b_re": dbr_t, "b_im": dbi_t, "c_re": summed["c_re"], "c_im": summed["c_im"],
        "d_skip": summed["d_skip"], "log_dt": dldt, "b_glu": summed["b_glu"], "q_norm": summed["q_norm"],
        "k_norm": summed["k_norm"], "norm_out_ssm": summed["norm_out_ssm"], "norm_out_attn": summed["norm_out_attn"],
        "norm_ffn": summed["norm_ffn"], "conv_b": summed["conv_b"],
    }
    repl = tuple(k for k in WEIGHT_NAMES if k not in SHARDED)
    turn = lambda k, a: jnp.swapaxes(a, 2, 3) if k in ("b_re", "b_im") else a
    w_list = [turn(k, weights[k]) for k in repl]
    g_list = [small_grads[k].reshape(w.shape) for k, w in zip(repl, w_list)]
    d_list, m_list, v_list = _adam_replicated(g_list, w_list, [turn(k, mom1[k]) for k in repl],
                                              [turn(k, mom2[k]) for k in repl], "adam_replicated")
    for k, g, d, nm, nv in zip(repl, g_list, d_list, m_list, v_list):
        grads[k], deltas[k], new_m[k], new_v[k] = turn(k, g), turn(k, d), turn(k, nm), turn(k, nv)

    grad_x = dx.reshape(x.shape)
    loss = summed["loss"].reshape(())
    return (loss, grad_x, *[grads[k] for k in WEIGHT_NAMES], *[deltas[k] for k in WEIGHT_NAMES],
            *[new_m[k] for k in WEIGHT_NAMES], *[new_v[k] for k in WEIGHT_NAMES])
```

```python
import functools
import math

import jax
import jax.numpy as jnp
from jax import lax
from jax.experimental import pallas as pl
from jax.experimental.pallas import tpu as pltpu

F32 = jnp.float32
BF16 = jnp.bfloat16
HIGHEST = lax.Precision.HIGHEST

N_DEV = 8
D_MODEL = 1024
D_SSM = 512
D_ATTN = 512
N_HEADS = 8
HEAD_DIM = 64
N_GROUPS = 32
SSM_GROUP = 16
SSM_STATE = 64
D_FF = 2752
D_IN = 2056
D_IN_PAD = 2176
EPS = 1e-6
LANES = 128
SUBLANES = 8
VMEM_LIMIT = 56 * 1024 * 1024

ADAM_LR = 0.001
ADAM_B1 = 0.9
ADAM_B2 = 0.999
ADAM_EPS = 1e-08
ADAM_WD = 0.01
ADAM_STEP = 10


def _cparams(*sem):
    return pltpu.CompilerParams(dimension_semantics=sem, vmem_limit_bytes=VMEM_LIMIT)


def _dot(a, b, **kw):
    return jnp.dot(a, b, preferred_element_type=F32, **kw)


def _dot_nt(a, b):
    return lax.dot_general(a, b, (((1,), (1,)), ((), ())), preferred_element_type=F32)


def _dot_tn(a, b):
    return lax.dot_general(a, b, (((0,), (0,)), ((), ())), preferred_element_type=F32)


def _rms(x, g):
    return x * lax.rsqrt(jnp.mean(x * x, axis=-1, keepdims=True) + EPS) * g


def _split_dot(x, avg):
    hi = x.astype(BF16)
    lo = (x - hi.astype(F32)).astype(BF16)
    return _dot(hi, avg) + _dot(lo, avg)


@jax.custom_vjp
def _group_mean(x, avg):
    return _split_dot(x, avg)


def _group_mean_fwd(x, avg):
    return _split_dot(x, avg), avg


def _group_mean_bwd(avg, ct):
    return _split_dot(ct, avg), jnp.zeros_like(avg)


_group_mean.defvjp(_group_mean_fwd, _group_mean_bwd)


def _headnorm(q, avg, g):
    return q * lax.rsqrt(_group_mean(q * q, avg) + EPS) * g


ALL_PEERS = tuple(range(1, N_DEV))
NEAR_PEERS = (1, 2, 4, 6)
RELAYED = (2, 4, 6)


def _peer_list(js=ALL_PEERS):
    x, y, c = lax.axis_index("x"), lax.axis_index("y"), lax.axis_index("c")
    peers = []
    for j in js:
        px = 1 - x if (j >> 2) & 1 else x
        py = 1 - y if (j >> 1) & 1 else y
        pc = 1 - c if j & 1 else c
        peers.append(((px, py, pc), 4 * px + 2 * py + pc))
    return 4 * x + 2 * y + c, peers


def _split_copies(src, land, send_sems, recv_sems, scatter_flags, me, peers, incoming):
    copies = []
    for k in range(len(src)):
        for j, (pid, pidx) in enumerate(peers):
            s = src[k].at[pidx] if scatter_flags[k] else src[k]
            i = k * len(peers) + j
            copies.append(pltpu.make_async_remote_copy(
                src_ref=s, dst_ref=land[k].at[pidx if incoming else me], send_sem=send_sems[i],
                recv_sem=recv_sems[i], device_id=pid, device_id_type=pl.DeviceIdType.MESH))
    return copies


def _handshake(peers):
    barrier = pltpu.get_barrier_semaphore()
    for pid, _ in peers:
        pl.semaphore_signal(barrier, inc=1, device_id=pid, device_id_type=pl.DeviceIdType.MESH)
    pl.semaphore_wait(barrier, len(peers))


def _exchange_start(srcs, scatter_flags, after, name, collective_id, peer_js=ALL_PEERS):
    n = len(srcs)
    ns = n * len(peer_js)
    hbm = pl.BlockSpec(memory_space=pltpu.HBM)
    sem = pl.BlockSpec(memory_space=pltpu.SEMAPHORE)
    land_shapes = [s.shape if sc else (N_DEV,) + s.shape for s, sc in zip(srcs, scatter_flags)]

    def body(*refs):
        src, land = refs[:n], refs[n:2 * n]
        send_sems = refs[2 * n + 1:2 * n + 1 + ns]
        recv_sems = refs[2 * n + 1 + ns:2 * n + 1 + 2 * ns]
        token = refs[4 * n + 1 + 2 * ns]
        me, peers = _peer_list(peer_js)
        _handshake(peers)
        for cp in _split_copies(src, land, send_sems, recv_sems, scatter_flags, me, peers, False):
            cp.start()
        token[...] = jnp.zeros_like(token)

    outs = pl.pallas_call(
        body, name=name,
        out_shape=(*[pltpu.SemaphoreType.DMA(())] * (2 * ns), *[pltpu.HBM(s.shape, s.dtype) for s in srcs],
                   *[pltpu.HBM(shp, s.dtype) for shp, s in zip(land_shapes, srcs)],
                   jax.ShapeDtypeStruct((SUBLANES, LANES), F32)),
        in_specs=[hbm] * (2 * n) + [pl.BlockSpec(memory_space=pl.ANY)],
        out_specs=(*[sem] * (2 * ns), *[hbm] * (2 * n), pl.BlockSpec(memory_space=pltpu.VMEM)),
        input_output_aliases={i: 2 * ns + i for i in range(2 * n)},
        compiler_params=pltpu.CompilerParams(has_side_effects=pltpu.SideEffectType.DATAFLOW_SIDE_EFFECTING,
                                             collective_id=collective_id),
    )(*[pltpu.with_memory_space_constraint(s, pltpu.HBM) for s in srcs],
      *[pltpu.with_memory_space_constraint(lax.empty(shp, s.dtype), pltpu.HBM) for shp, s in zip(land_shapes, srcs)],
      after)
    return (outs[:ns], outs[ns:2 * ns], outs[2 * ns:2 * ns + n], outs[2 * ns + n:2 * ns + 2 * n], outs[2 * ns + 2 * n])


def _exchange_wait(send_sems, recv_sems, srcs, lands, scatter_flags, after, name, peer_js=ALL_PEERS):
    n = len(srcs)
    ns = n * len(peer_js)
    hbm = pl.BlockSpec(memory_space=pltpu.HBM)
    sem = pl.BlockSpec(memory_space=pltpu.SEMAPHORE)

    def body(*refs):
        src, land = refs[:n], refs[n:2 * n]
        s_sems = refs[2 * n:2 * n + ns]
        r_sems = refs[2 * n + ns:2 * n + 2 * ns]
        me, peers = _peer_list(peer_js)
        for cp in _split_copies(src, land, s_sems, r_sems, scatter_flags, me, peers, True):
            cp.wait_send()
            cp.wait_recv()

    outs = pl.pallas_call(
        body, name=name,
        out_shape=tuple(pltpu.HBM(a.shape, a.dtype) for a in (*srcs, *lands)),
        in_specs=[hbm] * (2 * n) + [sem] * (2 * ns) + [pl.BlockSpec(memory_space=pl.ANY)],
        out_specs=tuple([hbm] * (2 * n)),
        input_output_aliases={i: i for i in range(2 * n)},
        compiler_params=pltpu.CompilerParams(has_side_effects=pltpu.SideEffectType.DATAFLOW_SIDE_EFFECTING),
    )(*srcs, *lands, *send_sems, *recv_sems, after)
    return outs[:n], outs[n:]


def _relay_copies(land, send_sems, recv_sems, incoming):
    _, ((sibling, _),) = _peer_list((1,))
    _, heard = _peer_list(RELAYED)
    _, sibling_heard = _peer_list(tuple(j ^ 1 for j in RELAYED))
    copies = []
    for k in range(len(land)):
        for j in range(len(RELAYED)):
            slot = (sibling_heard if incoming else heard)[j][1]
            i = k * len(RELAYED) + j
            copies.append(pltpu.make_async_remote_copy(
                src_ref=land[k].at[slot], dst_ref=land[k].at[slot], send_sem=send_sems[i], recv_sem=recv_sems[i],
                device_id=sibling, device_id_type=pl.DeviceIdType.MESH))
    return copies


def _relay_start(lands, name, collective_id):
    n = len(lands)
    ns = n * len(RELAYED)
    hbm = pl.BlockSpec(memory_space=pltpu.HBM)
    sem = pl.BlockSpec(memory_space=pltpu.SEMAPHORE)

    def body(*refs):
        land = refs[:n]
        send_sems = refs[n:n + ns]
        recv_sems = refs[n + ns:n + 2 * ns]
        token = refs[2 * n + 2 * ns]
        _handshake(_peer_list((1,))[1])
        for cp in _relay_copies(land, send_sems, recv_sems, False):
            cp.start()
        token[...] = jnp.zeros_like(token)

    outs = pl.pallas_call(
        body, name=name,
        out_shape=(*[pltpu.SemaphoreType.DMA(())] * (2 * ns), *[pltpu.HBM(a.shape, a.dtype) for a in lands],
                   jax.ShapeDtypeStruct((SUBLANES, LANES), F32)),
        in_specs=[hbm] * n,
        out_specs=(*[sem] * (2 * ns), *[hbm] * n, pl.BlockSpec(memory_space=pltpu.VMEM)),
        input_output_aliases={i: 2 * ns + i for i in range(n)},
        compiler_params=pltpu.CompilerParams(has_side_effects=pltpu.SideEffectType.DATAFLOW_SIDE_EFFECTING,
                                             collective_id=collective_id),
    )(*lands)
    return outs[:ns], outs[ns:2 * ns], outs[2 * ns:2 * ns + n], outs[2 * ns + n]


def _relay_wait(send_sems, recv_sems, lands, after, name):
    n = len(lands)
    ns = n * len(RELAYED)
    hbm = pl.BlockSpec(memory_space=pltpu.HBM)
    sem = pl.BlockSpec(memory_space=pltpu.SEMAPHORE)

    def body(*refs):
        land = refs[:n]
        for cp in _relay_copies(land, refs[n:n + ns], refs[n + ns:n + 2 * ns], True):
            cp.wait_send()
            cp.wait_recv()

    return pl.pallas_call(
        body, name=name,
        out_shape=tuple(pltpu.HBM(a.shape, a.dtype) for a in lands),
        in_specs=[hbm] * n + [sem] * (2 * ns) + [pl.BlockSpec(memory_space=pl.ANY)],
        out_specs=tuple([hbm] * n),
        input_output_aliases={i: i for i in range(n)},
        compiler_params=pltpu.CompilerParams(has_side_effects=pltpu.SideEffectType.DATAFLOW_SIDE_EFFECTING),
    )(*lands, *send_sems, *recv_sems, after)


def _tn_matmul(a, b, name, tk, tm, out_rows=None, out_cols=None, out_dtype=F32, tn=512):
    n_tok, k_dim = a.shape
    m_dim = b.shape[1]
    grid = (k_dim // tk, m_dim // tm, n_tok // tn)

    def body(a_ref, b_ref, o_ref, acc):
        k = pl.program_id(2)
        part = _dot_tn(a_ref[...].astype(BF16), b_ref[...].astype(BF16))

        @pl.when(k == 0)
        def _():
            acc[...] = part

        @pl.when(k > 0)
        def _():
            acc[...] += part

        @pl.when(k == grid[2] - 1)
        def _():
            o_ref[...] = acc[...].astype(out_dtype)

    return pl.pallas_call(
        body, name=name, grid=grid,
        in_specs=[pl.BlockSpec((tn, tk), lambda i, j, k: (k, i)), pl.BlockSpec((tn, tm), lambda i, j, k: (k, j))],
        out_specs=pl.BlockSpec((tk, tm), lambda i, j, k: (i, j)),
        out_shape=jax.ShapeDtypeStruct((out_rows or k_dim, out_cols or m_dim), out_dtype),
        scratch_shapes=[pltpu.VMEM((tk, tm), F32)],
        compiler_params=_cparams("parallel", "parallel", "arbitrary"),
    )(a, b)


def _adam_math(g, w, m, v):
    m = ADAM_B1 * m + (1.0 - ADAM_B1) * g
    v = ADAM_B2 * v + (1.0 - ADAM_B2) * (g * g)
    m_hat = m / (1.0 - ADAM_B1 ** ADAM_STEP)
    v_hat = v / (1.0 - ADAM_B2 ** ADAM_STEP)
    delta = -ADAM_LR * (m_hat / (jnp.sqrt(v_hat) + ADAM_EPS) + ADAM_WD * w)
    return delta, m, v


def _adam_sharded(land, own, w, m, v, name, tile):
    _, r, c = w.shape

    def body(*refs):
        l_ref = refs[0]
        own_ref = refs[1] if own is not None else None
        w_ref, m_ref, v_ref, g_ref, d_ref, nm_ref, nv_ref = [ref.at[0] for ref in refs[-7:]]
        if own_ref is not None:
            x, y, z = lax.axis_index("x"), lax.axis_index("y"), lax.axis_index("c")
            me = 4 * x + 2 * y + z
            mine = own_ref[...].astype(F32)
        g = None
        for s in range(N_DEV):
            part = l_ref[s].astype(F32)
            if own_ref is not None:
                part = jnp.where(me == s, mine, part)
            g = part if g is None else g + part
        d, nm, nv = _adam_math(g, w_ref[...], m_ref[...], v_ref[...])
        g_ref[...] = g
        d_ref[...] = d
        nm_ref[...] = nm
        nv_ref[...] = nv

    tr, tc = tile
    spec = pl.BlockSpec((1, tr, tc), lambda i, j: (0, i, j))
    own_specs, own_args = ([pl.BlockSpec((tr, tc), lambda i, j: (i, j))], [own]) if own is not None else ([], [])
    return pl.pallas_call(
        body, name=name, grid=(r // tr, c // tc),
        in_specs=[pl.BlockSpec((N_DEV, tr, tc), lambda i, j: (0, i, j)), *own_specs, spec, spec, spec],
        out_specs=(spec, spec, spec, spec),
        out_shape=tuple(jax.ShapeDtypeStruct((1, r, c), F32) for _ in range(4)),
        compiler_params=_cparams("parallel", "parallel"),
    )(land, *own_args, w, m, v)


def _sum_partials(parts, own, name):
    _, r, c = parts.shape

    def body(*refs):
        p_ref, o_ref = refs[0], refs[-1]
        if own is not None:
            x, y, z = lax.axis_index("x"), lax.axis_index("y"), lax.axis_index("c")
            me = 4 * x + 2 * y + z
            mine = refs[1][...]
        g = None
        for s in range(N_DEV):
            part = p_ref[s]
            if own is not None:
                part = jnp.where(me == s, mine, part)
            g = part if g is None else g + part
        o_ref[...] = g

    args = (parts,) if own is None else (parts, own)
    return pl.pallas_call(body, name=name, out_shape=jax.ShapeDtypeStruct((r, c), F32),
                          compiler_params=pltpu.CompilerParams(vmem_limit_bytes=VMEM_LIMIT))(*args)


def _adam_replicated(gs, ws, ms, vs, name):
    k = len(ws)

    def body(*refs):
        outs = refs[4 * k:]
        for i in range(k):
            d, nm, nv = _adam_math(refs[i][...], refs[k + i][...], refs[2 * k + i][...], refs[3 * k + i][...])
            outs[i][...] = d
            outs[k + i][...] = nm
            outs[2 * k + i][...] = nv

    outs = pl.pallas_call(body, name=name, out_shape=tuple(jax.ShapeDtypeStruct(w.shape, F32) for w in ws) * 3,
                          compiler_params=pltpu.CompilerParams(vmem_limit_bytes=VMEM_LIMIT))(*gs, *ws, *ms, *vs)
    return outs[:k], outs[k:2 * k], outs[2 * k:]


def _inproj_fwd(x, g, w_in, avg, qg, kg, tm=512):
    n = x.shape[0]

    def body(x_ref, g_ref, w_ref, a_ref, qg_ref, kg_ref, hn_ref, u_ref, qkv_ref, raw_ref, fl_ref):
        hn = _rms(x_ref[...], g_ref[...]).astype(BF16)
        hn_ref[...] = hn
        proj = _dot_nt(hn, w_ref[...])
        u_ref[...] = proj[:, 0:512]
        q = proj[:, 512:1024]
        k = proj[:, 1024:1536]
        raw_ref[:, 0:512] = q
        raw_ref[:, 512:1024] = k
        qkv_ref[:, 0:512] = _headnorm(q, a_ref[...], qg_ref[...]).astype(BF16)
        qkv_ref[:, 512:1024] = _headnorm(k, a_ref[...], kg_ref[...]).astype(BF16)
        qkv_ref[:, 1024:1536] = proj[:, 1536:2048].astype(BF16)
        fl_ref[...] = proj[:, 2048:D_IN_PAD]

    row = lambda w: pl.BlockSpec((tm, w), lambda i: (i, 0))
    full = lambda a: pl.BlockSpec(a.shape, lambda i: (0,) * a.ndim)
    return pl.pallas_call(
        body, name="inproj_fwd", grid=(n // tm,),
        in_specs=[row(D_MODEL), full(g), full(w_in), full(avg), full(qg), full(kg)],
        out_specs=(row(D_MODEL), row(512), row(1536), row(1024), row(LANES)),
        out_shape=(jax.ShapeDtypeStruct((n, D_MODEL), BF16), jax.ShapeDtypeStruct((n, 512), F32),
                   jax.ShapeDtypeStruct((n, 1536), BF16), jax.ShapeDtypeStruct((n, 1024), F32),
                   jax.ShapeDtypeStruct((n, LANES), F32)),
        compiler_params=_cparams("parallel"),
    )(x, g, w_in, avg, qg, kg)


def _inproj_bwd(x, g, w_in, avg, qg, kg, raw, du, dqn, dkn, dv, dfl, dres, tm=512):
    n = x.shape[0]

    def body(x_ref, g_ref, w_ref, a_ref, qg_ref, kg_ref, raw_ref, du_ref, dqn_ref, dkn_ref, dv_ref, dfl_ref, dres_ref,
             dx_ref, dproj_ref, dg_ref, dqg_ref, dkg_ref):
        @pl.when(pl.program_id(0) == 0)
        def _():
            dg_ref[...] = jnp.zeros_like(dg_ref)
            dqg_ref[...] = jnp.zeros_like(dqg_ref)
            dkg_ref[...] = jnp.zeros_like(dkg_ref)
        avg_m = a_ref[...]
        _, vjp_q = jax.vjp(lambda q, gg: _headnorm(q, avg_m, gg), raw_ref[:, 0:512], qg_ref[...])
        dq, dqg = vjp_q(dqn_ref[...])
        _, vjp_k = jax.vjp(lambda k, gg: _headnorm(k, avg_m, gg), raw_ref[:, 512:1024], kg_ref[...])
        dk, dkg = vjp_k(dkn_ref[...])
        dproj = jnp.concatenate([du_ref[...], dq, dk, dv_ref[...], dfl_ref[...]], axis=1).astype(BF16)
        dproj_ref[...] = dproj
        dhn = _dot(dproj, w_ref[...])
        _, vjp_x = jax.vjp(_rms, x_ref[...], g_ref[...])
        dxn, dg = vjp_x(dhn)
        dx_ref[...] = dxn + dres_ref[...]
        dg_ref[...] += dg
        dqg_ref[...] += dqg
        dkg_ref[...] += dkg

    row = lambda w: pl.BlockSpec((tm, w), lambda i: (i, 0))
    full = lambda a: pl.BlockSpec(a.shape, lambda i: (0,) * a.ndim)
    vec = lambda w: pl.BlockSpec((1, w), lambda i: (0, 0))
    return pl.pallas_call(
        body, name="inproj_bwd", grid=(n // tm,),
        in_specs=[row(D_MODEL), full(g), full(w_in), full(avg), full(qg), full(kg), row(1024), row(512), row(512),
                  row(512), row(512), row(LANES), row(D_MODEL)],
        out_specs=(row(D_MODEL), row(D_IN_PAD), vec(D_MODEL), vec(512), vec(512)),
        out_shape=(jax.ShapeDtypeStruct((n, D_MODEL), F32), jax.ShapeDtypeStruct((n, D_IN_PAD), BF16),
                   jax.ShapeDtypeStruct((1, D_MODEL), F32), jax.ShapeDtypeStruct((1, 512), F32),
                   jax.ShapeDtypeStruct((1, 512), F32)),
        compiler_params=_cparams("arbitrary"),
    )(x, g, w_in, avg, qg, kg, raw, du, dqn, dkn, dv, dfl, dres)


def _glu_fwd(yc, wg, bg, tm=512):
    n = yc.shape[0]

    def body(yc_ref, w_ref, b_ref, ys_ref):
        gl = jax.nn.gelu(yc_ref[...])
        z = _dot(gl.astype(BF16), w_ref[...]) + b_ref[...]
        ys_ref[...] = gl * jax.nn.sigmoid(z)

    row = pl.BlockSpec((tm, 512), lambda i: (i, 0))
    full = lambda a: pl.BlockSpec(a.shape, lambda i: (0,) * a.ndim)
    return pl.pallas_call(
        body, name="glu_fwd", grid=(n // tm,), in_specs=[row, full(wg), full(bg)], out_specs=row,
        out_shape=jax.ShapeDtypeStruct((n, 512), F32), compiler_params=_cparams("parallel"),
    )(yc, wg, bg)


def _glu_bwd(yc, dys, wg, bg, tm=512):
    n = yc.shape[0]

    def body(yc_ref, dys_ref, w_ref, b_ref, dyc_ref, gl_ref, dz_ref, db_ref):
        @pl.when(pl.program_id(0) == 0)
        def _():
            db_ref[...] = jnp.zeros_like(db_ref)
        gl, vjp_gelu = jax.vjp(jax.nn.gelu, yc_ref[...])
        glb = gl.astype(BF16)
        z = _dot(glb, w_ref[...]) + b_ref[...]
        s = jax.nn.sigmoid(z)
        dys = dys_ref[...]
        dz = dys * gl * s * (1.0 - s)
        dzb = dz.astype(BF16)
        dgl = dys * s + _dot_nt(dzb, w_ref[...])
        dyc_ref[...] = vjp_gelu(dgl)[0]
        gl_ref[...] = glb
        dz_ref[...] = dzb
        db_ref[...] += jnp.sum(dz, axis=0, keepdims=True)

    row = pl.BlockSpec((tm, 512), lambda i: (i, 0))
    full = lambda a: pl.BlockSpec(a.shape, lambda i: (0,) * a.ndim)
    return pl.pallas_call(
        body, name="glu_bwd", grid=(n // tm,), in_specs=[row, row, full(wg), full(bg)],
        out_specs=(row, row, row, pl.BlockSpec((1, 512), lambda i: (0, 0))),
        out_shape=(jax.ShapeDtypeStruct((n, 512), F32), jax.ShapeDtypeStruct((n, 512), BF16),
                   jax.ShapeDtypeStruct((n, 512), BF16), jax.ShapeDtypeStruct((1, 512), F32)),
        compiler_params=_cparams("arbitrary"),
    )(yc, dys, wg, bg)


def _mix_fwd(x, ys, ya, gs, ga, wout, gf, tm=512):
    n = x.shape[0]

    def body(x_ref, ys_ref, ya_ref, gs_ref, ga_ref, w_ref, gf_ref, h1_ref, hn2_ref, mixed_ref):
        mixed = jnp.concatenate([_rms(ys_ref[...], gs_ref[...]), _rms(ya_ref[...], ga_ref[...])], axis=1).astype(BF16)
        mixed_ref[...] = mixed
        h1 = x_ref[...] + _dot(mixed, w_ref[...])
        h1_ref[...] = h1
        hn2_ref[...] = _rms(h1, gf_ref[...]).astype(BF16)

    row = lambda w: pl.BlockSpec((tm, w), lambda i: (i, 0))
    full = lambda a: pl.BlockSpec(a.shape, lambda i: (0,) * a.ndim)
    return pl.pallas_call(
        body, name="mix_fwd", grid=(n // tm,),
        in_specs=[row(D_MODEL), row(512), row(512), full(gs), full(ga), full(wout), full(gf)],
        out_specs=(row(D_MODEL), row(D_MODEL), row(D_MODEL)),
        out_shape=(jax.ShapeDtypeStruct((n, D_MODEL), F32), jax.ShapeDtypeStruct((n, D_MODEL), BF16),
                   jax.ShapeDtypeStruct((n, D_MODEL), BF16)),
        compiler_params=_cparams("parallel"),
    )(x, ys, ya, gs, ga, wout, gf)


def _mix_bwd(dy, dhn2_parts, h1, ys, ya, gs, ga, wout, gf, tm=512):
    n = dy.shape[0]
    n_parts = dhn2_parts.shape[0]

    def body(dy_ref, dp_ref, h1_ref, ys_ref, ya_ref, gs_ref, ga_ref, w_ref, gf_ref,
             dh1_ref, dys_ref, dya_ref, dgs_ref, dga_ref, dgf_ref):
        @pl.when(pl.program_id(0) == 0)
        def _():
            dgs_ref[...] = jnp.zeros_like(dgs_ref)
            dga_ref[...] = jnp.zeros_like(dga_ref)
            dgf_ref[...] = jnp.zeros_like(dgf_ref)
        dhn2 = dp_ref[0]
        for p in range(1, n_parts):
            dhn2 = dhn2 + dp_ref[p]
        _, vjp_f = jax.vjp(_rms, h1_ref[...], gf_ref[...])
        dh1n, dgf = vjp_f(dhn2)
        dh1 = dy_ref[...] + dh1n
        dh1_ref[...] = dh1
        dmixed = _dot_nt(dh1.astype(BF16), w_ref[...])
        _, vjp_s = jax.vjp(_rms, ys_ref[...], gs_ref[...])
        dys, dgs = vjp_s(dmixed[:, 0:512])
        _, vjp_a = jax.vjp(_rms, ya_ref[...], ga_ref[...])
        dya, dga = vjp_a(dmixed[:, 512:1024])
        dys_ref[...] = dys
        dya_ref[...] = dya
        dgs_ref[...] += dgs
        dga_ref[...] += dga
        dgf_ref[...] += dgf

    row = lambda w: pl.BlockSpec((tm, w), lambda i: (i, 0))
    full = lambda a: pl.BlockSpec(a.shape, lambda i: (0,) * a.ndim)
    vec = lambda w: pl.BlockSpec((1, w), lambda i: (0, 0))
    return pl.pallas_call(
        body, name="mix_bwd", grid=(n // tm,),
        in_specs=[row(D_MODEL), pl.BlockSpec((n_parts, tm, D_MODEL), lambda i: (0, i, 0)), row(D_MODEL), row(512),
                  row(512), full(gs), full(ga), full(wout), full(gf)],
        out_specs=(row(D_MODEL), row(512), row(512), vec(512), vec(512), vec(D_MODEL)),
        out_shape=(jax.ShapeDtypeStruct((n, D_MODEL), F32), jax.ShapeDtypeStruct((n, 512), F32),
                   jax.ShapeDtypeStruct((n, 512), F32), jax.ShapeDtypeStruct((1, 512), F32),
                   jax.ShapeDtypeStruct((1, 512), F32), jax.ShapeDtypeStruct((1, D_MODEL), F32)),
        compiler_params=_cparams("arbitrary"),
    )(dy, dhn2_parts, h1, ys, ya, gs, ga, wout, gf)


HALO = 16
FFN_GROUPS = 4
FFN_GROUP = D_FF // FFN_GROUPS


def _conv3(ue, cw):
    return cw[2:3] * ue + cw[1:2] * pltpu.roll(ue, 1, 0) + cw[0:1] * pltpu.roll(ue, 2, 0) + cw[3:4]


def _ffn_weight_specs():
    gate = lambda i, j: (j, 0, 0)
    val = lambda i, j: (j + FFN_GROUPS, 0, 0)
    w_blk, c_blk = (1, FFN_GROUP, D_MODEL), (1, SUBLANES, FFN_GROUP)
    return [pl.BlockSpec(w_blk, gate), pl.BlockSpec(w_blk, val), pl.BlockSpec(c_blk, gate), pl.BlockSpec(c_blk, val),
            pl.BlockSpec((1, FFN_GROUP, D_MODEL), gate)]


def _ffn_fwd(hn2, h1, target, w_up, conv, w_down, seq_len, tm=512):
    n = hn2.shape[0]
    nj = FFN_GROUPS
    hb = tm // HALO

    def body(hn_ref, halo_ref, h1_ref, tgt_ref, wg_ref, wv_ref, cg_ref, cv_ref, wd_ref,
             ug_ref, uv_ref, pg_ref, pv_ref, dy_ref, loss_ref, acc):
        i, j = pl.program_id(0), pl.program_id(1)
        seq_start = (i * tm) % seq_len == 0
        halo = halo_ref[...]
        halo = jnp.where(seq_start, jnp.zeros_like(halo), halo)
        he = jnp.concatenate([halo, hn_ref[...]], axis=0)
        ueg = _dot_nt(he, wg_ref[0])
        uev = _dot_nt(he, wv_ref[0])
        ug_ref[0] = ueg[HALO:].astype(BF16)
        uv_ref[0] = uev[HALO:].astype(BF16)
        cg = _conv3(ueg, cg_ref[0])[HALO:]
        cv = _conv3(uev, cv_ref[0])[HALO:]
        pg_ref[0] = cg.astype(BF16)
        pv_ref[0] = cv.astype(BF16)
        act = (jax.nn.silu(cg) * cv).astype(BF16)
        part = _dot(act, wd_ref[0])

        @pl.when(j == 0)
        def _():
            acc[...] = part

        @pl.when(j > 0)
        def _():
            acc[...] += part

        @pl.when(j == nj - 1)
        def _():
            err = h1_ref[...] + acc[...] - tgt_ref[...]
            dy_ref[...] = err * (1.0 / D_MODEL)
            loss_ref[0] = jnp.sum(err * err, axis=0, keepdims=True)

    row = pl.BlockSpec((tm, D_MODEL), lambda i, j: (i, 0))
    u_main = pl.BlockSpec((1, tm, FFN_GROUP), lambda i, j: (j, i, 0))
    u_shape = jax.ShapeDtypeStruct((FFN_GROUPS, n, FFN_GROUP), BF16)
    return pl.pallas_call(
        body, name="ffn_fwd", grid=(n // tm, nj),
        in_specs=[row, pl.BlockSpec((HALO, D_MODEL), lambda i, j: (jnp.maximum(i * hb - 1, 0), 0)), row, row,
                  *_ffn_weight_specs()],
        out_specs=(u_main, u_main, u_main, u_main, row, pl.BlockSpec((1, 1, D_MODEL), lambda i, j: (i, 0, 0))),
        out_shape=(u_shape, u_shape, u_shape, u_shape, jax.ShapeDtypeStruct((n, D_MODEL), F32),
                   jax.ShapeDtypeStruct((n // tm, 1, D_MODEL), F32)),
        scratch_shapes=[pltpu.VMEM((tm, D_MODEL), F32)],
        compiler_params=_cparams("parallel", "arbitrary"),
    )(hn2, hn2, h1, target, w_up, w_up, conv, conv, w_down)


def _ffn_bwd(dy, ug, uv, pg, pv, w_up, conv, w_down, seq_len, tm=512):
    n = dy.shape[0]
    nj = FFN_GROUPS
    fb = FFN_GROUP
    hb = tm // HALO
    last_hb = n // HALO - 1
    rows = tm + HALO

    def body(dy_ref, dyn_ref, ug_ref, uv_ref, pgm_ref, pgn_ref, pvm_ref, pvn_ref, wg_ref, wv_ref, cg_ref, cv_ref,
             wd_ref, dug_ref, duv_ref, act_ref, dhn_ref, dcg_ref, dcv_ref, acc):
        i, j = pl.program_id(0), pl.program_id(1)
        seq_end = ((i + 1) * tm) % seq_len == 0
        dyn = dyn_ref[...]
        dyn = jnp.where(seq_end, jnp.zeros_like(dyn), dyn)
        d_out = jnp.concatenate([dy_ref[...], dyn], axis=0).astype(BF16)
        d_act = _dot_nt(d_out, wd_ref[0])
        cge = jnp.concatenate([pgm_ref[0], pgn_ref[0]], axis=0).astype(F32)
        cve = jnp.concatenate([pvm_ref[0], pvn_ref[0]], axis=0).astype(F32)
        act, vjp_act = jax.vjp(lambda g, v: jax.nn.silu(g) * v, cge, cve)
        dcge, dcve = vjp_act(d_act)
        act_ref[0] = act[:tm].astype(BF16)

        def conv_t(dc, u_ref, cw):
            ahead1 = pltpu.roll(dc, rows - 1, 0)[:tm]
            ahead2 = pltpu.roll(dc, rows - 2, 0)[:tm]
            here = dc[:tm]
            du = cw[2:3] * here + cw[1:2] * ahead1 + cw[0:1] * ahead2
            u = u_ref[0].astype(F32)
            col = lambda x: jnp.sum(x, axis=0, keepdims=True)
            grad = jnp.concatenate([col(ahead2 * u), col(ahead1 * u), col(here * u), col(here),
                                    jnp.zeros((4, fb), F32)], axis=0)
            return du.astype(BF16), grad

        cwg, cwv = cg_ref[0], cv_ref[0]
        dug, grad_g = conv_t(dcge, ug_ref, cwg)
        duv, grad_v = conv_t(dcve, uv_ref, cwv)
        dug_ref[0] = dug
        duv_ref[0] = duv
        part = _dot(dug, wg_ref[0]) + _dot(duv, wv_ref[0])

        @pl.when(j == 0)
        def _():
            acc[...] = part

        @pl.when(j > 0)
        def _():
            acc[...] += part

        @pl.when(j == nj - 1)
        def _():
            dhn_ref[...] = acc[...]

        @pl.when(i == 0)
        def _():
            dcg_ref[j] = jnp.zeros((8, fb), F32)
            dcv_ref[j] = jnp.zeros((8, fb), F32)

        dcg_ref[j] += grad_g
        dcv_ref[j] += grad_v

    row = pl.BlockSpec((tm, D_MODEL), lambda i, j: (i, 0))
    u_main = pl.BlockSpec((1, tm, fb), lambda i, j: (j, i, 0))
    u_next = pl.BlockSpec((1, HALO, fb), lambda i, j: (j, jnp.minimum((i + 1) * hb, last_hb), 0))
    dc_spec = pl.BlockSpec((nj, 8, fb), lambda i, j: (0, 0, 0))
    u_shape = jax.ShapeDtypeStruct((FFN_GROUPS, n, fb), BF16)
    return pl.pallas_call(
        body, name="ffn_bwd", grid=(n // tm, nj),
        in_specs=[row, pl.BlockSpec((HALO, D_MODEL), lambda i, j: (jnp.minimum((i + 1) * hb, last_hb), 0)),
                  u_main, u_main, u_main, u_next, u_main, u_next, *_ffn_weight_specs()],
        out_specs=(u_main, u_main, u_main, row, dc_spec, dc_spec),
        out_shape=(u_shape, u_shape, u_shape, jax.ShapeDtypeStruct((n, D_MODEL), F32),
                   jax.ShapeDtypeStruct((nj, 8, fb), F32), jax.ShapeDtypeStruct((nj, 8, fb), F32)),
        scratch_shapes=[pltpu.VMEM((tm, D_MODEL), F32)],
        compiler_params=_cparams("arbitrary", "arbitrary"),
    )(dy, dy, ug, uv, pg, pg, pv, pv, w_up, w_up, conv, conv, w_down)


def _tn_grouped(a, b, name, shared_a, out_dtype=F32, tn=1024):
    groups = b.shape[0] if shared_a else a.shape[0]
    n_tok = a.shape[0] if shared_a else b.shape[0]
    k_dim, m_dim = a.shape[-1], b.shape[-1]

    def body(a_ref, b_ref, o_ref, acc):
        k = pl.program_id(1)
        a_t = a_ref[...] if shared_a else a_ref[0]
        b_t = b_ref[0] if shared_a else b_ref[...]
        part = _dot_tn(a_t.astype(BF16), b_t.astype(BF16))

        @pl.when(k == 0)
        def _():
            acc[...] = part

        @pl.when(k > 0)
        def _():
            acc[...] += part

        @pl.when(k == n_tok // tn - 1)
        def _():
            o_ref[0] = acc[...].astype(out_dtype)

    plain = lambda w: pl.BlockSpec((tn, w), lambda g, k: (k, 0))
    grouped = lambda w: pl.BlockSpec((1, tn, w), lambda g, k: (g, k, 0))
    return pl.pallas_call(
        body, name=name, grid=(groups, n_tok // tn),
        in_specs=[plain(k_dim), grouped(m_dim)] if shared_a else [grouped(k_dim), plain(m_dim)],
        out_specs=pl.BlockSpec((1, k_dim, m_dim), lambda g, k: (g, 0, 0)),
        out_shape=jax.ShapeDtypeStruct((groups, k_dim, m_dim), out_dtype),
        scratch_shapes=[pltpu.VMEM((k_dim, m_dim), F32)],
        compiler_params=_cparams("parallel", "arbitrary"),
    )(a, b)


def _s5_param_fn(lr, li, ldt, br, bi):
    dt = jnp.exp(ldt)
    mag = jnp.exp(lr * dt)
    ab_re = mag * jnp.cos(li * dt)
    ab_im = mag * jnp.sin(li * dt)
    nr = ab_re - 1.0
    ni = ab_im
    den = lr * lr + li * li
    q_re = (nr * lr + ni * li) / den
    q_im = (ni * lr - nr * li) / den
    bb_re = q_re * br - q_im * bi
    bb_im = q_re * bi + q_im * br
    return ab_re, ab_im, bb_re, bb_im


def _s5_param_fwd(lr, li, ldt, br, bi):
    def body(lr_ref, li_ref, ldt_ref, br_ref, bi_ref, ar_ref, ai_ref, bbr_ref, bbi_ref):
        ar, ai, bbr, bbi = _s5_param_fn(lr_ref[...], li_ref[...], ldt_ref[...], br_ref[...], bi_ref[...])
        ar_ref[...] = ar
        ai_ref[...] = ai
        bbr_ref[...] = bbr
        bbi_ref[...] = bbi

    return pl.pallas_call(
        body, name="s5_param_fwd",
        out_shape=(jax.ShapeDtypeStruct(lr.shape, F32), jax.ShapeDtypeStruct(lr.shape, F32),
                   jax.ShapeDtypeStruct(br.shape, F32), jax.ShapeDtypeStruct(br.shape, F32)),
    )(lr, li, ldt, br, bi)


def _s5_param_bwd(lr, li, ldt, br, bi, dar, dai, dbbr, dbbi):
    def body(lr_ref, li_ref, ldt_ref, br_ref, bi_ref, dar_ref, dai_ref, dbbr_ref, dbbi_ref,
             dlr_ref, dli_ref, dldt_ref, dbr_ref, dbi_ref):
        _, vjp = jax.vjp(_s5_param_fn, lr_ref[...], li_ref[...], ldt_ref[...], br_ref[...], bi_ref[...])
        dlr, dli, dldt, dbr, dbi = vjp((dar_ref[...], dai_ref[...], dbbr_ref[...], dbbi_ref[...]))
        dlr_ref[...] = dlr
        dli_ref[...] = dli
        dldt_ref[...] = dldt
        dbr_ref[...] = dbr
        dbi_ref[...] = dbi

    return pl.pallas_call(
        body, name="s5_param_bwd",
        out_shape=(jax.ShapeDtypeStruct(lr.shape, F32), jax.ShapeDtypeStruct(lr.shape, F32),
                   jax.ShapeDtypeStruct(ldt.shape, F32), jax.ShapeDtypeStruct(br.shape, F32),
                   jax.ShapeDtypeStruct(br.shape, F32)),
    )(lr, li, ldt, br, bi, dar, dai, dbbr, dbbi)


S5_CHUNK = 512
S5_STATES = 512
S5_BLOCKS = 4


def _cpow_rows(ar, ai, count):
    rs, im = [ar], [ai]
    for _ in range(count - 1):
        pr, pi = rs[-1], im[-1]
        rs.append(pr * ar - pi * ai)
        im.append(pr * ai + pi * ar)
    return rs, im


def _scan_in_groups(vr, vi, pr, pi, rm, reverse):
    n, width = vr.shape
    vr = vr.reshape(n // SUBLANES, SUBLANES, width)
    vi = vi.reshape(n // SUBLANES, SUBLANES, width)
    row = rm[0:SUBLANES]
    for k in (1, 2, 4):
        shift = SUBLANES - k if reverse else k
        keep = row < SUBLANES - k if reverse else row >= k
        kr = jnp.where(keep, pr[k - 1], 0.0)
        ki = jnp.where(keep, pi[k - 1], 0.0)
        sr, si = pltpu.roll(vr, shift, 1), pltpu.roll(vi, shift, 1)
        vr, vi = vr + kr * sr - ki * si, vi + kr * si + ki * sr
    return vr.reshape(n, width), vi.reshape(n, width)


def _carry_over_groups(xr_s, xi_s, wr, wi, c0r, c0i, reverse):
    groups = xr_s.shape[0] // SUBLANES
    pick = 0 if reverse else SUBLANES - 1

    def step(q, carry):
        cr, ci = carry
        r = groups - 1 - q if reverse else q
        o = pl.multiple_of(r * SUBLANES, SUBLANES)
        vr = xr_s[pl.ds(o, SUBLANES), :]
        vi = xi_s[pl.ds(o, SUBLANES), :]
        nr = vr + wr * cr - wi * ci
        ni = vi + wr * ci + wi * cr
        xr_s[pl.ds(o, SUBLANES), :] = nr
        xi_s[pl.ds(o, SUBLANES), :] = ni
        return (jnp.broadcast_to(nr[pick:pick + 1], nr.shape), jnp.broadcast_to(ni[pick:pick + 1], ni.shape))

    return lax.fori_loop(0, groups, step, (c0r, c0i), unroll=4)


def _s5_state_scan(u_b, bbr, bbi, pr, pi, rm, xr_s, xi_s, c0r, c0i):
    bur = _dot(u_b, bbr)
    bui = _dot(u_b, bbi)
    bur, bui = _scan_in_groups(bur, bui, pr, pi, rm, False)
    xr_s[...] = bur
    xi_s[...] = bui
    w8r = jnp.concatenate(pr, axis=0)
    w8i = jnp.concatenate(pi, axis=0)
    return _carry_over_groups(xr_s, xi_s, w8r, w8i, c0r, c0i, False)


def _s5_fwd(u, a_re, a_im, bbr, bbi, cr, ci, d_skip, n_seq):
    n = u.shape[0]
    seq_len = n // n_seq
    nt = seq_len // S5_CHUNK
    tc = S5_CHUNK

    def body(u_ref, ar_ref, ai_ref, bbr_ref, bbi_ref, cr_ref, ci_ref, d_ref, y_ref, str_ref, sti_ref, xrb_ref, xib_ref,
             xr_s, xi_s, car_r, car_i):
        t = pl.program_id(2)

        @pl.when(t == 0)
        def _():
            car_r[...] = jnp.zeros_like(car_r)
            car_i[...] = jnp.zeros_like(car_i)
        pr, pi = _cpow_rows(ar_ref[0], ai_ref[0], SUBLANES)
        rm = lax.broadcasted_iota(jnp.int32, (tc, S5_STATES), 0) & (SUBLANES - 1)
        str_ref[0, 0] = car_r[...]
        sti_ref[0, 0] = car_i[...]
        u_t = u_ref[...]
        cfr, cfi = _s5_state_scan(u_t.astype(BF16), bbr_ref[0], bbi_ref[0], pr, pi, rm, xr_s, xi_s,
                                  car_r[...], car_i[...])
        car_r[...] = cfr
        car_i[...] = cfi
        xr_b = xr_s[...].astype(BF16)
        xi_b = xi_s[...].astype(BF16)
        xrb_ref[...] = xr_b
        xib_ref[...] = xi_b
        y_ref[...] = _dot(xr_b, cr_ref[0]) - _dot(xi_b, ci_ref[0]) + d_ref[...] * u_t

    x_spec = pl.BlockSpec((tc, S5_STATES), lambda cb, b, t: (b * nt + t, cb))
    x_shape = jax.ShapeDtypeStruct((n, S5_BLOCKS * S5_STATES), BF16)
    u_spec = pl.BlockSpec((tc, LANES), lambda cb, b, t: (b * nt + t, cb))
    a_spec = pl.BlockSpec((1, 1, S5_STATES), lambda cb, b, t: (cb, 0, 0))
    bb_spec = pl.BlockSpec((1, LANES, S5_STATES), lambda cb, b, t: (cb, 0, 0))
    c_spec = pl.BlockSpec((1, S5_STATES, LANES), lambda cb, b, t: (cb, 0, 0))
    st_spec = pl.BlockSpec((1, 1, SUBLANES, S5_STATES), lambda cb, b, t: (cb, b * nt + t, 0, 0))
    st_shape = jax.ShapeDtypeStruct((S5_BLOCKS, n_seq * nt, SUBLANES, S5_STATES), F32)
    return pl.pallas_call(
        body, name="s5_fwd", grid=(S5_BLOCKS, n_seq, nt),
        in_specs=[u_spec, a_spec, a_spec, bb_spec, bb_spec, c_spec, c_spec,
                  pl.BlockSpec((1, LANES), lambda cb, b, t: (0, cb))],
        out_specs=(u_spec, st_spec, st_spec, x_spec, x_spec),
        out_shape=(jax.ShapeDtypeStruct((n, D_SSM), F32), st_shape, st_shape, x_shape, x_shape),
        scratch_shapes=[pltpu.VMEM((tc, S5_STATES), F32), pltpu.VMEM((tc, S5_STATES), F32),
                        pltpu.VMEM((SUBLANES, S5_STATES), F32), pltpu.VMEM((SUBLANES, S5_STATES), F32)],
        compiler_params=_cparams("parallel", "arbitrary", "arbitrary"),
    )(u, a_re, a_im, bbr, bbi, cr, ci, d_skip)


def _s5_bwd(u, dy, xs_r, xs_i, st_r, st_i, a_re, a_im, bbr, bbi, cr, ci, d_skip, n_seq):
    n = u.shape[0]
    seq_len = n // n_seq
    nt = seq_len // S5_CHUNK
    tc = S5_CHUNK

    def body(u_ref, dy_ref, xrb_ref, xib_ref, str_ref, sti_ref, ar_ref, ai_ref, bbr_ref, bbi_ref, cr_ref, ci_ref, d_ref,
             du_ref, dbbr_ref, dbbi_ref, dcr_ref, dci_ref, dar_ref, dai_ref, dd_ref,
             gr_s, gi_s, car_r, car_i):
        b, t = pl.program_id(1), pl.program_id(2)

        @pl.when((b == 0) & (t == 0))
        def _():
            for ref in (dbbr_ref, dbbi_ref, dcr_ref, dci_ref, dar_ref, dai_ref, dd_ref):
                ref[...] = jnp.zeros_like(ref)

        @pl.when(t == 0)
        def _():
            car_r[...] = jnp.zeros_like(car_r)
            car_i[...] = jnp.zeros_like(car_i)
        ar, ai = ar_ref[0], ai_ref[0]
        pr, pi = _cpow_rows(ar, ai, SUBLANES)
        row = lax.broadcasted_iota(jnp.int32, (tc, S5_STATES), 0)
        rm = row & (SUBLANES - 1)
        u_t = u_ref[...]
        u_b = u_t.astype(BF16)
        dy_t = dy_ref[...]
        dy_b = dy_t.astype(BF16)
        s0r, s0i = str_ref[0, 0], sti_ref[0, 0]
        xr_b, xi_b = xrb_ref[...], xib_ref[...]
        xr, xi = xr_b.astype(F32), xi_b.astype(F32)
        gr = _dot_nt(dy_b, cr_ref[0])
        gi = -_dot_nt(dy_b, ci_ref[0])
        npi = [-v for v in pi]
        gr, gi = _scan_in_groups(gr, gi, pr, npi, rm, True)
        gr_s[...] = gr
        gi_s[...] = gi
        w8r = jnp.concatenate(pr[::-1], axis=0)
        w8i = jnp.concatenate(npi[::-1], axis=0)
        cfr, cfi = _carry_over_groups(gr_s, gi_s, w8r, w8i, car_r[...], car_i[...], True)
        car_r[...] = cfr
        car_i[...] = cfi
        gr, gi = gr_s[...], gi_s[...]
        gr_b, gi_b = gr.astype(BF16), gi.astype(BF16)
        du_ref[...] = _dot_nt(gr_b, bbr_ref[0]) + _dot_nt(gi_b, bbi_ref[0]) + d_ref[...] * dy_t
        dbbr_ref[0] += _dot_tn(u_b, gr_b)
        dbbi_ref[0] += _dot_tn(u_b, gi_b)
        dcr_ref[0] += _dot_tn(xr_b, dy_b)
        dci_ref[0] -= _dot_tn(xi_b, dy_b)
        dd_ref[0] += jnp.sum((dy_t * u_t).reshape(tc // SUBLANES, SUBLANES, LANES), axis=0)
        first = row == 0
        xpr = jnp.where(first, jnp.broadcast_to(s0r[0:1], xr.shape), pltpu.roll(xr, 1, 0))
        xpi = jnp.where(first, jnp.broadcast_to(s0i[0:1], xi.shape), pltpu.roll(xi, 1, 0))
        shp = (tc // SUBLANES, SUBLANES, S5_STATES)
        dar_ref[0] += jnp.sum((gr * xpr + gi * xpi).reshape(shp), axis=0)
        dai_ref[0] += jnp.sum((gi * xpr - gr * xpi).reshape(shp), axis=0)

    u_spec = pl.BlockSpec((tc, LANES), lambda cb, b, t: (b * nt + nt - 1 - t, cb))
    a_spec = pl.BlockSpec((1, 1, S5_STATES), lambda cb, b, t: (cb, 0, 0))
    bb_spec = pl.BlockSpec((1, LANES, S5_STATES), lambda cb, b, t: (cb, 0, 0))
    c_spec = pl.BlockSpec((1, S5_STATES, LANES), lambda cb, b, t: (cb, 0, 0))
    st_spec = pl.BlockSpec((1, 1, SUBLANES, S5_STATES), lambda cb, b, t: (cb, b * nt + nt - 1 - t, 0, 0))
    da_spec = pl.BlockSpec((1, SUBLANES, S5_STATES), lambda cb, b, t: (cb, 0, 0))
    dd_spec = pl.BlockSpec((1, SUBLANES, LANES), lambda cb, b, t: (cb, 0, 0))
    big = pltpu.VMEM((tc, S5_STATES), F32)
    small = pltpu.VMEM((SUBLANES, S5_STATES), F32)
    x_spec = pl.BlockSpec((tc, S5_STATES), lambda cb, b, t: (b * nt + nt - 1 - t, cb))
    return pl.pallas_call(
        body, name="s5_bwd", grid=(S5_BLOCKS, n_seq, nt),
        in_specs=[u_spec, u_spec, x_spec, x_spec, st_spec, st_spec, a_spec, a_spec, bb_spec, bb_spec, c_spec, c_spec,
                  pl.BlockSpec((1, LANES), lambda cb, b, t: (0, cb))],
        out_specs=(u_spec, bb_spec, bb_spec, c_spec, c_spec, da_spec, da_spec, dd_spec),
        out_shape=(jax.ShapeDtypeStruct((n, D_SSM), F32),
                   jax.ShapeDtypeStruct((S5_BLOCKS, LANES, S5_STATES), F32),
                   jax.ShapeDtypeStruct((S5_BLOCKS, LANES, S5_STATES), F32),
                   jax.ShapeDtypeStruct((S5_BLOCKS, S5_STATES, LANES), F32),
                   jax.ShapeDtypeStruct((S5_BLOCKS, S5_STATES, LANES), F32),
                   jax.ShapeDtypeStruct((S5_BLOCKS, SUBLANES, S5_STATES), F32),
                   jax.ShapeDtypeStruct((S5_BLOCKS, SUBLANES, S5_STATES), F32),
                   jax.ShapeDtypeStruct((S5_BLOCKS, SUBLANES, LANES), F32)),
        scratch_shapes=[big, big, small, small],
        compiler_params=_cparams("parallel", "arbitrary", "arbitrary"),
    )(u, dy, xs_r, xs_i, st_r, st_i, a_re, a_im, bbr, bbi, cr, ci, d_skip)


CUM_BLOCK = 128


def _tri(lower):
    r = lax.broadcasted_iota(jnp.int32, (CUM_BLOCK, CUM_BLOCK), 0)
    c = lax.broadcasted_iota(jnp.int32, (CUM_BLOCK, CUM_BLOCK), 1)
    return jnp.where(r >= c if lower else r <= c, 1.0, 0.0).astype(F32)


def _fprep_fwd(fl, bf, n_seq):
    n = fl.shape[0]
    seq_len = n // n_seq
    nb = seq_len // CUM_BLOCK

    def body(fl_ref, bf_ref, cum_ref):
        tril = _tri(True)
        carry = jnp.zeros((1, LANES), F32)
        for blk in range(nb):
            rows = slice(blk * CUM_BLOCK, (blk + 1) * CUM_BLOCK)
            lf = jax.nn.log_sigmoid(fl_ref[rows, :] + bf_ref[...])
            cs = jnp.dot(tril, lf, preferred_element_type=F32, precision=HIGHEST) + carry
            cum_ref[rows, :] = cs
            carry = cs[CUM_BLOCK - 1:CUM_BLOCK, :]

    spec = pl.BlockSpec((seq_len, LANES), lambda b: (b, 0))
    return pl.pallas_call(
        body, name="fprep_fwd", grid=(n_seq,), in_specs=[spec, pl.BlockSpec((1, LANES), lambda b: (0, 0))],
        out_specs=spec, out_shape=jax.ShapeDtypeStruct((n, LANES), F32), compiler_params=_cparams("parallel"),
    )(fl, bf)


def _fprep_bwd(dcum, fl, bf, n_seq):
    n = fl.shape[0]
    seq_len = n // n_seq
    nb = seq_len // CUM_BLOCK

    def body(dcum_ref, fl_ref, bf_ref, dfl_ref, dbf_ref):
        triu = _tri(False)
        lane = lax.broadcasted_iota(jnp.int32, (CUM_BLOCK, LANES), 1)
        carry = jnp.zeros((1, LANES), F32)
        total = jnp.zeros((1, LANES), F32)
        for blk in reversed(range(nb)):
            rows = slice(blk * CUM_BLOCK, (blk + 1) * CUM_BLOCK)
            rs = jnp.dot(triu, dcum_ref[rows, :], preferred_element_type=F32, precision=HIGHEST) + carry
            carry = rs[0:1, :]
            _, vjp = jax.vjp(jax.nn.log_sigmoid, fl_ref[rows, :] + bf_ref[...])
            dz = jnp.where(lane < N_HEADS, vjp(rs)[0], 0.0)
            dfl_ref[rows, :] = dz
            total = total + jnp.sum(dz, axis=0, keepdims=True)
        dbf_ref[0] = total

    spec = pl.BlockSpec((seq_len, LANES), lambda b: (b, 0))
    return pl.pallas_call(
        body, name="fprep_bwd", grid=(n_seq,), in_specs=[spec, spec, pl.BlockSpec((1, LANES), lambda b: (0, 0))],
        out_specs=(spec, pl.BlockSpec((1, 1, LANES), lambda b: (b, 0, 0))),
        out_shape=(jax.ShapeDtypeStruct((n, LANES), F32), jax.ShapeDtypeStruct((n_seq, 1, LANES), F32)),
        compiler_params=_cparams("parallel"),
    )(dcum, fl, bf)


ATT_TQ = 256
ATT_KSTEP = 256
ATT_SCALE = HEAD_DIM ** -0.5
NEG_BIG = -1e30


assert ATT_KSTEP == ATT_TQ


def _scores(q_scaled, kb, row_bias, ck, kend):
    s = _dot_nt(q_scaled, kb) - ck
    if row_bias is not None:
        s = s + row_bias
    r = lax.broadcasted_iota(jnp.int32, (ATT_TQ, ATT_TQ), 0)
    c = lax.broadcasted_iota(jnp.int32, (ATT_TQ, ATT_TQ), 1)
    diag = jnp.where(r >= c, s[:, kend - ATT_TQ:], NEG_BIG)
    return diag if kend == ATT_TQ else jnp.concatenate([s[:, :kend - ATT_TQ], diag], axis=1)


def _attn_specs(n_seq, seq_len):
    nq = seq_len // ATT_TQ
    q_spec = pl.BlockSpec((ATT_TQ, LANES), lambda b, h, q: (b * nq + q, h))
    k_spec = pl.BlockSpec((seq_len, LANES), lambda b, h, q: (b, N_HEADS // 2 + h))
    v_spec = pl.BlockSpec((seq_len, LANES), lambda b, h, q: (b, N_HEADS + h))
    cq_spec = pl.BlockSpec((1, 2, ATT_TQ, 1), lambda b, h, q: (b, h, q, 0))
    ck_spec = pl.BlockSpec((1, 2, 1, seq_len), lambda b, h, q: (b, h, 0, 0))
    return nq, q_spec, k_spec, v_spec, cq_spec, ck_spec


def _own_cum(cum_ref, e):
    lane = lax.broadcasted_iota(jnp.int32, (1, LANES), 1)
    return jnp.sum(jnp.where(lane == 2 * pl.program_id(1) + e, cum_ref[...], 0.0), axis=1, keepdims=True)


def _head_selectors():
    head0 = lax.broadcasted_iota(jnp.int32, (1, LANES), 1) < HEAD_DIM
    return head0, (head0, jnp.logical_not(head0))


def _for_key_range(qi, seq_len, run):
    per = ATT_KSTEP // ATT_TQ
    for g in range(seq_len // ATT_KSTEP):
        pl.when(qi // per == g)(functools.partial(run, (g + 1) * ATT_KSTEP))


def _attn_fwd(qkv, cum, ck, n_seq):
    n = qkv.shape[0]
    seq_len = n // n_seq
    nq, q_spec, k_spec, v_spec, cq_spec, ck_spec = _attn_specs(n_seq, seq_len)
    cum_spec = pl.BlockSpec((ATT_TQ, LANES), lambda b, h, q: (b * nq + q, 0))

    def body(q_ref, k_ref, v_ref, cum_ref, ck_ref, o_ref, lse_ref):
        qi = pl.program_id(2)
        q2 = q_ref[...]
        head0, sels = _head_selectors()
        qe = [jnp.where(sel, q2 * ATT_SCALE, 0.0).astype(BF16) for sel in sels]

        def run(kend):
            kb = k_ref[0:kend, :].astype(BF16)
            vb = v_ref[0:kend, :].astype(BF16)
            outs = []
            for e in range(2):
                s = _scores(qe[e], kb, None, ck_ref[0, e, :, 0:kend], kend)
                mx = jnp.max(s, axis=1, keepdims=True)
                p = jnp.exp(s - mx)
                den = jnp.sum(p, axis=1, keepdims=True)
                outs.append(_dot(p.astype(BF16), vb) / den)
                lse_ref[0, e] = _own_cum(cum_ref, e) + mx + jnp.log(den)
            o_ref[...] = jnp.where(head0, outs[0], outs[1])

        _for_key_range(qi, seq_len, run)

    return pl.pallas_call(
        body, name="attn_fwd", grid=(n_seq, N_HEADS // 2, nq),
        in_specs=[q_spec, k_spec, v_spec, cum_spec, ck_spec],
        out_specs=(q_spec, cq_spec),
        out_shape=(jax.ShapeDtypeStruct((n, D_ATTN), F32), jax.ShapeDtypeStruct((n_seq, N_HEADS, seq_len, 1), F32)),
        compiler_params=_cparams("parallel", "parallel", "parallel"),
    )(qkv, qkv, qkv, cum, ck)


def _attn_bwd(qkv, cum, ck, o, do, lse, n_seq):
    n = qkv.shape[0]
    seq_len = n // n_seq
    nq, q_spec, k_spec, v_spec, cq_spec, ck_spec = _attn_specs(n_seq, seq_len)
    kv_out = pl.BlockSpec((seq_len, LANES), lambda b, h, q: (b, h))
    cum_spec = pl.BlockSpec((ATT_TQ, LANES), lambda b, h, q: (b * nq + q, 0))

    def body(q_ref, k_ref, v_ref, cum_ref, ck_ref, o_ref, do_ref, lse_ref, dq_ref, dk_ref, dv_ref, dcq_ref, dck_ref):
        qi = pl.program_id(2)

        @pl.when(qi == 0)
        def _():
            dk_ref[...] = jnp.zeros_like(dk_ref)
            dv_ref[...] = jnp.zeros_like(dv_ref)
            dck_ref[...] = jnp.zeros_like(dck_ref)
        q2 = q_ref[...]
        do2 = do_ref[...]
        o2 = o_ref[...]
        head0, sels = _head_selectors()
        qe = [jnp.where(sel, q2 * ATT_SCALE, 0.0).astype(BF16) for sel in sels]
        doe = [jnp.where(sel, do2, 0.0) for sel in sels]
        doe_b = [d.astype(BF16) for d in doe]
        delta = [jnp.sum(d * o2, axis=1, keepdims=True) for d in doe]

        def run(kend):
            kb = k_ref[0:kend, :].astype(BF16)
            vb = v_ref[0:kend, :].astype(BF16)
            dqs = []
            dk = jnp.zeros((kend, LANES), F32)
            dv = jnp.zeros((kend, LANES), F32)
            for e in range(2):
                p = jnp.exp(_scores(qe[e], kb, _own_cum(cum_ref, e) - lse_ref[0, e], ck_ref[0, e, :, 0:kend], kend))
                ds = p * (_dot_nt(doe_b[e], vb) - delta[e])
                ds_b = ds.astype(BF16)
                dqs.append(_dot(ds_b, kb))
                dk = dk + _dot_tn(ds_b, qe[e])
                dv = dv + _dot_tn(p.astype(BF16), doe_b[e])
                dcq_ref[0, e] = jnp.sum(ds, axis=1, keepdims=True)
                dck_ref[0, e, :, 0:kend] -= jnp.sum(ds, axis=0, keepdims=True)
            dk_ref[0:kend, :] += dk
            dv_ref[0:kend, :] += dv
            dq_ref[...] = jnp.where(head0, dqs[0], dqs[1]) * ATT_SCALE

        _for_key_range(qi, seq_len, run)

    return pl.pallas_call(
        body, name="attn_bwd", grid=(n_seq, N_HEADS // 2, nq),
        in_specs=[q_spec, k_spec, v_spec, cum_spec, ck_spec, q_spec, q_spec, cq_spec],
        out_specs=(q_spec, kv_out, kv_out, cq_spec, ck_spec),
        out_shape=(jax.ShapeDtypeStruct((n, D_ATTN), F32), jax.ShapeDtypeStruct((n, D_ATTN), F32),
                   jax.ShapeDtypeStruct((n, D_ATTN), F32),
                   jax.ShapeDtypeStruct((n_seq, N_HEADS, seq_len, 1), F32),
                   jax.ShapeDtypeStruct((n_seq, N_HEADS, 1, seq_len), F32)),
        compiler_params=_cparams("parallel", "parallel", "arbitrary"),
    )(qkv, qkv, qkv, cum, ck, o, do, lse)


WEIGHT_NAMES = ("norm_mix", "w_in", "b_forget", "lam_re", "lam_im", "b_re", "b_im", "c_re", "c_im", "d_skip", "log_dt",
                "w_glu", "b_glu", "q_norm", "k_norm", "norm_out_ssm", "norm_out_attn", "w_out", "norm_ffn", "w_up",
                "conv_w", "conv_b", "w_down")
SHARDED = ("w_in", "w_glu", "w_out", "w_up", "conv_w", "w_down")
ADAM_TILE = {"w_in": (257, 256), "w_glu": (64, 512), "w_out": (128, 1024), "w_up": (688, 256), "conv_w": (3, 688),
             "w_down": (344, 1024)}
PACK_ROWS = SUBLANES * LANES


def _after_all(*arrays):
    return sum(a[(0,) * a.ndim].astype(F32) for a in arrays).reshape(1, 1)


def _pad_to(a, axis, size):
    pad = [(0, 0)] * a.ndim
    pad[axis] = (0, size - a.shape[axis])
    return jnp.pad(a, pad)


def _block_diag(t, transpose):
    t4 = t.reshape(S5_BLOCKS, 8, SSM_GROUP, SSM_STATE)
    eye = jnp.eye(8, dtype=t.dtype)
    if transpose:
        e = jnp.swapaxes(t4, 2, 3)[:, :, :, None, :] * eye[None, :, None, :, None]
        return e.reshape(S5_BLOCKS, S5_STATES, LANES)
    e = t4[:, :, :, None, :] * eye[None, :, None, :, None]
    return e.reshape(S5_BLOCKS, LANES, S5_STATES)


def _block_diag_extract(m, transpose):
    if transpose:
        m5 = m.reshape(S5_BLOCKS, 8, SSM_STATE, 8, SSM_GROUP)
        d = jnp.stack([m5[:, i, :, i, :] for i in range(8)], axis=1)
        return jnp.swapaxes(d, 2, 3).reshape(N_GROUPS, SSM_GROUP, SSM_STATE)
    m5 = m.reshape(S5_BLOCKS, 8, SSM_GROUP, 8, SSM_STATE)
    d = jnp.stack([m5[:, i, :, i, :] for i in range(8)], axis=1)
    return d.reshape(N_GROUPS, SSM_GROUP, SSM_STATE)


def _pack(pieces):
    flat = jnp.concatenate([p.reshape(-1).astype(F32) for p in pieces])
    size = -(-flat.shape[0] // PACK_ROWS) * PACK_ROWS
    return _pad_to(flat, 0, size).reshape(-1, LANES)


def _unpack(packed, shapes):
    flat = packed.reshape(-1)
    out, off = [], 0
    for shp in shapes:
        size = math.prod(shp)
        out.append(flat[off:off + size].reshape(shp))
        off += size
    return out


def kernel(x, norm_mix, w_in, b_forget, lam_re, lam_im, b_re, b_im, c_re, c_im, d_skip, log_dt, w_glu, b_glu, q_norm, k_norm, norm_out_ssm, norm_out_attn, w_out, norm_ffn, w_up, conv_w, conv_b, w_down, loss_target, m_norm_mix, m_w_in, m_b_forget, m_lam_re, m_lam_im, m_b_re, m_b_im, m_c_re, m_c_im, m_d_skip, m_log_dt, m_w_glu, m_b_glu, m_q_norm, m_k_norm, m_norm_out_ssm, m_norm_out_attn, m_w_out, m_norm_ffn, m_w_up, m_conv_w, m_conv_b, m_w_down, v_norm_mix, v_w_in, v_b_forget, v_lam_re, v_lam_im, v_b_re, v_b_im, v_c_re, v_c_im, v_d_skip, v_log_dt, v_w_glu, v_b_glu, v_q_norm, v_k_norm, v_norm_out_ssm, v_norm_out_attn, v_w_out, v_norm_ffn, v_w_up, v_conv_w, v_conv_b, v_w_down):
    given = dict(locals())
    weights = {k: given[k] for k in WEIGHT_NAMES}
    mom1 = {k: given["m_" + k] for k in WEIGHT_NAMES}
    mom2 = {k: given["v_" + k] for k in WEIGHT_NAMES}
    n_seq, seq_len, _ = x.shape
    n = n_seq * seq_len
    xf = x.reshape(n, D_MODEL)
    target = loss_target.reshape(n, D_MODEL)
    me_idx = 4 * lax.axis_index("x") + 2 * lax.axis_index("y") + lax.axis_index("c")

    in_flags = [False] * 2
    in_sems = _exchange_start([jnp.swapaxes(w_in[0], 0, 1).astype(BF16), conv_w[0]], in_flags, norm_mix,
                              "gather_in_start", 3, NEAR_PEERS)
    fill_own = lambda got, mine: lax.dynamic_update_index_in_dim(got, mine, me_idx, 0)

    lr3 = lam_re[0].reshape(N_GROUPS, 1, SSM_STATE)
    li3 = lam_im[0].reshape(N_GROUPS, 1, SSM_STATE)
    ldt3 = log_dt[0].reshape(N_GROUPS, 1, 1)
    br_t = jnp.swapaxes(b_re[0], 1, 2)
    bi_t = jnp.swapaxes(b_im[0], 1, 2)
    ab_re, ab_im, bb_re, bb_im = _s5_param_fwd(lr3, li3, ldt3, br_t, bi_t)
    a_re = ab_re.reshape(S5_BLOCKS, 1, S5_STATES)
    a_im = ab_im.reshape(S5_BLOCKS, 1, S5_STATES)
    bbr = _block_diag(bb_re, False).astype(BF16)
    bbi = _block_diag(bb_im, False).astype(BF16)
    cr = _block_diag(c_re[0], True).astype(BF16)
    ci = _block_diag(c_im[0], True).astype(BF16)
    avg = jnp.kron(jnp.eye(N_HEADS, dtype=F32), jnp.full((HEAD_DIM, HEAD_DIM), 1.0 / HEAD_DIM, F32)).astype(BF16)
    qg = jnp.tile(q_norm, (1, N_HEADS))
    kg = jnp.tile(k_norm, (1, N_HEADS))
    row_shards = [w_down[0].astype(BF16), w_out[0].astype(BF16), w_glu[0].astype(BF16)]

    (own_in, own_cw), near = _exchange_wait(in_sems[0], in_sems[1], in_sems[2], in_sems[3], in_flags,
                                            _after_all(a_re, a_im, bbr, bbi, cr, ci, avg, qg, kg, *row_shards),
                                            "gather_in_wait", NEAR_PEERS)
    relay = _relay_start(list(near), "gather_in_relay_start", 6)
    g_in, g_cw = _relay_wait(relay[0], relay[1], relay[2], relay[3], "gather_in_relay_wait")
    g_in = fill_own(g_in, own_in)
    g_cw = fill_own(g_cw, own_cw)
    row_flags = [False] * 3
    r_sems = _exchange_start(row_shards, row_flags, g_in, "gather_rows_start", 0)
    u_sems = _exchange_start([jnp.swapaxes(w_up[0], 0, 1).astype(BF16)], [False], r_sems[4], "gather_up_start", 5)
    norm_mix = norm_mix + u_sems[4][0, 0]
    w_in_p = _pad_to(g_in.reshape(D_IN, D_MODEL), 0, D_IN_PAD)

    hn, u, qkv, raw, fl = _inproj_fwd(xf, norm_mix, w_in_p, avg, qg, kg)
    yc, st_r, st_i, xs_r, xs_i = _s5_fwd(u, a_re, a_im, bbr, bbi, cr, ci, d_skip, n_seq)
    bf = _pad_to(b_forget, 1, LANES)
    cum = _fprep_fwd(fl, bf, n_seq)
    cum8 = jnp.swapaxes(cum[:, :N_HEADS].reshape(n_seq, seq_len, N_HEADS), 1, 2)
    ck = cum8[:, :, None, :]
    ya, lse = _attn_fwd(qkv, cum, ck, n_seq)
    own_rows, got_rows = _exchange_wait(r_sems[0], r_sems[1], r_sems[2], r_sems[3], row_flags, ya, "gather_rows_wait")
    g_down, g_out, g_glu = [fill_own(g, o) for o, g in zip(own_rows, got_rows)]
    w_glu_f = g_glu.reshape(D_SSM, D_SSM)
    w_out_f = g_out.reshape(D_MODEL, D_MODEL)
    conv_st = _pad_to(jnp.concatenate([g_cw, conv_b.reshape(N_DEV, 1, -1)], axis=1), 1, SUBLANES)
    w_down4 = g_down.reshape(FFN_GROUPS, FFN_GROUP, D_MODEL)
    ys = _glu_fwd(yc, w_glu_f, b_glu)
    h1, hn2, mixed = _mix_fwd(xf, ys, ya, norm_out_ssm, norm_out_attn, w_out_f, norm_ffn)
    (own_up,), (g_up,) = _exchange_wait(u_sems[0], u_sems[1], u_sems[2], u_sems[3], [False], _after_all(h1, hn2),
                                        "gather_up_wait")
    g_up = fill_own(g_up, own_up)
    ug, uv, pg, pv, dy, loss_part = _ffn_fwd(hn2, h1, target, g_up, conv_st, w_down4, seq_len)
    loss_local = 0.5 * jnp.sum(loss_part) / D_MODEL

    dug, duv, act, dhn2, dcg, dcv = _ffn_bwd(dy, ug, uv, pg, pv, g_up, conv_st, w_down4, seq_len)
    dh1, dys, dya, d_gs, d_ga, d_gf = _mix_bwd(dy, dhn2[None], h1, ys, ya, norm_out_ssm, norm_out_attn, w_out_f, norm_ffn)
    dyc, gl_b, dz_b, d_bglu = _glu_bwd(yc, dys, w_glu_f, b_glu)

    gw_glu = _tn_matmul(gl_b, dz_b, "dw_glu", D_SSM, D_SSM, out_dtype=BF16)
    gw_out = _tn_matmul(mixed, dh1, "dw_out", D_MODEL, D_MODEL, out_dtype=BF16)
    gw_up = jnp.concatenate([_tn_grouped(dug, hn2, "dw_up_gate", False, BF16),
                             _tn_grouped(duv, hn2, "dw_up_val", False, BF16)], axis=0)
    gw_down = _tn_grouped(act, dy, "dw_down", False, BF16)
    g_conv = jnp.concatenate([dcg, dcv], axis=0)
    by_cols = lambda g, c: jnp.swapaxes(g.reshape(g.shape[0], N_DEV, c), 0, 1)
    early_flags = [True] * 4
    early_names = ("w_down", "w_out", "w_glu", "w_up")
    g_sems = _exchange_start(
        [gw_down.reshape(N_DEV, -1, D_MODEL), gw_out.reshape(N_DEV, -1, D_MODEL), gw_glu.reshape(N_DEV, -1, D_SSM), gw_up],
        early_flags, dyc, "grad_early_start", 1)
    started = g_sems[4][0, 0]

    du, dbbr, dbbi, dcr, dci, dar, dai, ddk = _s5_bwd(u, dyc, xs_r, xs_i, st_r, st_i, a_re, a_im, bbr, bbi, cr, ci,
                                                      d_skip + started, n_seq)
    partial_early = {
        "ab_re": jnp.sum(dar, axis=1), "ab_im": jnp.sum(dai, axis=1),
        "bb_re": _block_diag_extract(dbbr, False), "bb_im": _block_diag_extract(dbbi, False),
        "c_re": _block_diag_extract(dcr, True), "c_im": _block_diag_extract(dci, True),
        "d_skip": jnp.sum(ddk, axis=1), "b_glu": d_bglu,
        "norm_out_ssm": d_gs, "norm_out_attn": d_ga, "norm_ffn": d_gf, "conv_b": g_conv[:, 3],
    }
    early_keys = tuple(partial_early)
    early_shapes = [partial_early[k].shape for k in early_keys]
    p_sems = _exchange_start([_pack([partial_early[k] for k in early_keys])], [False], du, "small_early_start", 2)
    started = started + p_sems[4][0, 0]

    dqn, dkn, dv, dcq, dck = _attn_bwd(qkv, cum, ck + started, ya, dya, lse, n_seq)
    dcum8 = dcq[:, :, :, 0] + dck.reshape(n_seq, N_HEADS, seq_len)
    dcum = _pad_to(jnp.swapaxes(dcum8, 1, 2).reshape(n, N_HEADS), 1, LANES)
    dfl, dbf = _fprep_bwd(dcum, fl, bf, n_seq)
    dx, dproj, d_gmix, d_qg, d_kg = _inproj_bwd(xf, norm_mix, w_in_p, avg, qg, kg, raw, du, dqn, dkn, dv, dfl, dh1)

    gw_in = _tn_matmul(dproj, hn, "dw_in", D_IN_PAD, D_MODEL, out_rows=D_IN, out_dtype=BF16)
    partial_late = {
        "norm_mix": d_gmix, "b_forget": jnp.sum(dbf, axis=(0, 1))[:N_HEADS],
        "q_norm": jnp.sum(d_qg.reshape(N_HEADS, HEAD_DIM), axis=0),
        "k_norm": jnp.sum(d_kg.reshape(N_HEADS, HEAD_DIM), axis=0), "loss": loss_local.reshape(1),
    }
    late_keys = tuple(partial_late)
    late_shapes = [partial_late[k].shape for k in late_keys]

    late_flags = [True, True, False]
    l_sems = _exchange_start(
        [gw_in.reshape(N_DEV, D_IN // N_DEV, D_MODEL).astype(BF16), g_conv[:, :3],
         _pack([partial_late[k] for k in late_keys])],
        late_flags, dx, "grad_late_start", 4)
    early_src, early_land = _exchange_wait(g_sems[0], g_sems[1], g_sems[2], g_sems[3], early_flags, l_sems[4],
                                           "grad_early_wait")
    land = dict(zip(early_names, early_land))
    land_up = land["w_up"]
    own = {k: lax.dynamic_index_in_dim(s, me_idx, 0, keepdims=False) for k, s in zip(early_names, early_src)}
    grads, deltas, new_m, new_v = {}, {}, {}, {}

    def adam_shard(name):
        flip = (lambda a: jnp.swapaxes(a, 1, 2)) if name in ("w_in", "w_up") else (lambda a: a)
        outs = _adam_sharded(land[name], own[name], flip(weights[name]), flip(mom1[name]), flip(mom2[name]),
                             "adam_" + name, ADAM_TILE[name])
        grads[name], deltas[name], new_m[name], new_v[name] = [flip(o) for o in outs]

    for name in ("w_up", "w_down", "w_out", "w_glu"):
        adam_shard(name)
    (own_pack,), (early_parts,) = _exchange_wait(p_sems[0], p_sems[1], p_sems[2], p_sems[3], [False], land_up,
                                                 "small_early_wait")
    early_sum = _sum_partials(early_parts, own_pack, "sum_early_partials")
    (src_in, src_cw, own_late), (land["w_in"], land["conv_w"], late_parts) = _exchange_wait(
        l_sems[0], l_sems[1], l_sems[2], l_sems[3], late_flags,
        _after_all(early_sum, *[new_v[k] for k in ("w_up", "w_down", "w_out", "w_glu")]), "grad_late_wait")
    own["w_in"] = lax.dynamic_index_in_dim(src_in, me_idx, 0, keepdims=False)
    own["conv_w"] = lax.dynamic_index_in_dim(src_cw, me_idx, 0, keepdims=False)
    for name in ("w_in", "conv_w"):
        adam_shard(name)

    summed = dict(zip(late_keys, _unpack(_sum_partials(late_parts, own_late, "sum_late_partials"), late_shapes)))
    summed.update(zip(early_keys, _unpack(early_sum, early_shapes)))
    dlr, dli, dldt, dbr_t, dbi_t = _s5_param_bwd(
        lr3, li3, ldt3, br_t, bi_t, summed["ab_re"].reshape(lr3.shape), summed["ab_im"].reshape(lr3.shape),
        summed["bb_re"], summed["bb_im"])
    small_grads = {
        "norm_mix": summed["norm_mix"], "b_forget": summed["b_forget"], "lam_re": dlr, "lam_im": dli,
        "b_re": dbr_t, "b_im": dbi_t, "c_re": summed["c_re"], "c_im": summed["c_im"],
        "d_skip": summed["d_skip"], "log_dt": dldt, "b_glu": summed["b_glu"], "q_norm": summed["q_norm"],
        "k_norm": summed["k_norm"], "norm_out_ssm": summed["norm_out_ssm"], "norm_out_attn": summed["norm_out_attn"],
        "norm_ffn": summed["norm_ffn"], "conv_b": summed["conv_b"],
    }
    repl = tuple(k for k in WEIGHT_NAMES if k not in SHARDED)
    turn = lambda k, a: jnp.swapaxes(a, 2, 3) if k in ("b_re", "b_im") else a
    w_list = [turn(k, weights[k]) for k in repl]
    g_list = [small_grads[k].reshape(w.shape) for k, w in zip(repl, w_list)]
    d_list, m_list, v_list = _adam_replicated(g_list, w_list, [turn(k, mom1[k]) for k in repl],
                                              [turn(k, mom2[k]) for k in repl], "adam_replicated")
    for k, g, d, nm, nv in zip(repl, g_list, d_list, m_list, v_list):
        grads[k], deltas[k], new_m[k], new_v[k] = turn(k, g), turn(k, d), turn(k, nm), turn(k, nv)

    grad_x = dx.reshape(x.shape)
    loss = summed["loss"].reshape(())
    return (loss, grad_x, *[grads[k] for k in WEIGHT_NAMES], *[deltas[k] for k in WEIGHT_NAMES],
            *[new_m[k] for k in WEIGHT_NAMES], *[new_v[k] for k in WEIGHT_NAMES])
```

```python
import functools
import math

import jax
import jax.numpy as jnp
from jax import lax
from jax.experimental import pallas as pl
from jax.experimental.pallas import tpu as pltpu

F32 = jnp.float32
BF16 = jnp.bfloat16
HIGHEST = lax.Precision.HIGHEST

N_DEV = 8
D_MODEL = 1024
D_SSM = 512
D_ATTN = 512
N_HEADS = 8
HEAD_DIM = 64
N_GROUPS = 32
SSM_GROUP = 16
SSM_STATE = 64
D_FF = 2752
D_IN = 2056
D_IN_PAD = 2176
EPS = 1e-6
LANES = 128
SUBLANES = 8
VMEM_LIMIT = 56 * 1024 * 1024

ADAM_LR = 0.001
ADAM_B1 = 0.9
ADAM_B2 = 0.999
ADAM_EPS = 1e-08
ADAM_WD = 0.01
ADAM_STEP = 10


def _cparams(*sem):
    return pltpu.CompilerParams(dimension_semantics=sem, vmem_limit_bytes=VMEM_LIMIT)


def _dot(a, b, **kw):
    return jnp.dot(a, b, preferred_element_type=F32, **kw)


def _dot_nt(a, b):
    return lax.dot_general(a, b, (((1,), (1,)), ((), ())), preferred_element_type=F32)


def _dot_tn(a, b):
    return lax.dot_general(a, b, (((0,), (0,)), ((), ())), preferred_element_type=F32)


def _rms(x, g):
    return x * lax.rsqrt(jnp.mean(x * x, axis=-1, keepdims=True) + EPS) * g


def _split_dot(x, avg):
    hi = x.astype(BF16)
    lo = (x - hi.astype(F32)).astype(BF16)
    return _dot(hi, avg) + _dot(lo, avg)


@jax.custom_vjp
def _group_mean(x, avg):
    return _split_dot(x, avg)


def _group_mean_fwd(x, avg):
    return _split_dot(x, avg), avg


def _group_mean_bwd(avg, ct):
    return _split_dot(ct, avg), jnp.zeros_like(avg)


_group_mean.defvjp(_group_mean_fwd, _group_mean_bwd)


def _headnorm(q, avg, g):
    return q * lax.rsqrt(_group_mean(q * q, avg) + EPS) * g


ALL_PEERS = tuple(range(1, N_DEV))
NEAR_PEERS = (1, 2, 4, 6)
RELAYED = (2, 4, 6)


def _peer_list(js=ALL_PEERS):
    x, y, c = lax.axis_index("x"), lax.axis_index("y"), lax.axis_index("c")
    peers = []
    for j in js:
        px = 1 - x if (j >> 2) & 1 else x
        py = 1 - y if (j >> 1) & 1 else y
        pc = 1 - c if j & 1 else c
        peers.append(((px, py, pc), 4 * px + 2 * py + pc))
    return 4 * x + 2 * y + c, peers


def _split_copies(src, land, send_sems, recv_sems, scatter_flags, me, peers, incoming):
    copies = []
    for k in range(len(src)):
        for j, (pid, pidx) in enumerate(peers):
            s = src[k].at[pidx] if scatter_flags[k] else src[k]
            i = k * len(peers) + j
            copies.append(pltpu.make_async_remote_copy(
                src_ref=s, dst_ref=land[k].at[pidx if incoming else me], send_sem=send_sems[i],
                recv_sem=recv_sems[i], device_id=pid, device_id_type=pl.DeviceIdType.MESH))
    return copies


def _handshake(peers):
    barrier = pltpu.get_barrier_semaphore()
    for pid, _ in peers:
        pl.semaphore_signal(barrier, inc=1, device_id=pid, device_id_type=pl.DeviceIdType.MESH)
    pl.semaphore_wait(barrier, len(peers))


def _exchange_start(srcs, scatter_flags, after, name, collective_id, peer_js=ALL_PEERS):
    n = len(srcs)
    ns = n * len(peer_js)
    hbm = pl.BlockSpec(memory_space=pltpu.HBM)
    sem = pl.BlockSpec(memory_space=pltpu.SEMAPHORE)
    land_shapes = [s.shape if sc else (N_DEV,) + s.shape for s, sc in zip(srcs, scatter_flags)]

    def body(*refs):
        src, land = refs[:n], refs[n:2 * n]
        send_sems = refs[2 * n + 1:2 * n + 1 + ns]
        recv_sems = refs[2 * n + 1 + ns:2 * n + 1 + 2 * ns]
        token = refs[4 * n + 1 + 2 * ns]
        me, peers = _peer_list(peer_js)
        _handshake(peers)
        for cp in _split_copies(src, land, send_sems, recv_sems, scatter_flags, me, peers, False):
            cp.start()
        token[...] = jnp.zeros_like(token)

    outs = pl.pallas_call(
        body, name=name,
        out_shape=(*[pltpu.SemaphoreType.DMA(())] * (2 * ns), *[pltpu.HBM(s.shape, s.dtype) for s in srcs],
                   *[pltpu.HBM(shp, s.dtype) for shp, s in zip(land_shapes, srcs)],
                   jax.ShapeDtypeStruct((SUBLANES, LANES), F32)),
        in_specs=[hbm] * (2 * n) + [pl.BlockSpec(memory_space=pl.ANY)],
        out_specs=(*[sem] * (2 * ns), *[hbm] * (2 * n), pl.BlockSpec(memory_space=pltpu.VMEM)),
        input_output_aliases={i: 2 * ns + i for i in range(2 * n)},
        compiler_params=pltpu.CompilerParams(has_side_effects=pltpu.SideEffectType.DATAFLOW_SIDE_EFFECTING,
                                             collective_id=collective_id),
    )(*[pltpu.with_memory_space_constraint(s, pltpu.HBM) for s in srcs],
      *[pltpu.with_memory_space_constraint(lax.empty(shp, s.dtype), pltpu.HBM) for shp, s in zip(land_shapes, srcs)],
      after)
    return (outs[:ns], outs[ns:2 * ns], outs[2 * ns:2 * ns + n], outs[2 * ns + n:2 * ns + 2 * n], outs[2 * ns + 2 * n])


def _exchange_wait(send_sems, recv_sems, srcs, lands, scatter_flags, after, name, peer_js=ALL_PEERS):
    n = len(srcs)
    ns = n * len(peer_js)
    hbm = pl.BlockSpec(memory_space=pltpu.HBM)
    sem = pl.BlockSpec(memory_space=pltpu.SEMAPHORE)

    def body(*refs):
        src, land = refs[:n], refs[n:2 * n]
        s_sems = refs[2 * n:2 * n + ns]
        r_sems = refs[2 * n + ns:2 * n + 2 * ns]
        me, peers = _peer_list(peer_js)
        for cp in _split_copies(src, land, s_sems, r_sems, scatter_flags, me, peers, True):
            cp.wait_send()
            cp.wait_recv()

    outs = pl.pallas_call(
        body, name=name,
        out_shape=tuple(pltpu.HBM(a.shape, a.dtype) for a in (*srcs, *lands)),
        in_specs=[hbm] * (2 * n) + [sem] * (2 * ns) + [pl.BlockSpec(memory_space=pl.ANY)],
        out_specs=tuple([hbm] * (2 * n)),
        input_output_aliases={i: i for i in range(2 * n)},
        compiler_params=pltpu.CompilerParams(has_side_effects=pltpu.SideEffectType.DATAFLOW_SIDE_EFFECTING),
    )(*srcs, *lands, *send_sems, *recv_sems, after)
    return outs[:n], outs[n:]


def _relay_copies(land, send_sems, recv_sems, incoming):
    _, ((sibling, _),) = _peer_list((1,))
    _, heard = _peer_list(RELAYED)
    _, sibling_heard = _peer_list(tuple(j ^ 1 for j in RELAYED))
    copies = []
    for k in range(len(land)):
        for j in range(len(RELAYED)):
            slot = (sibling_heard if incoming else heard)[j][1]
            i = k * len(RELAYED) + j
            copies.append(pltpu.make_async_remote_copy(
                src_ref=land[k].at[slot], dst_ref=land[k].at[slot], send_sem=send_sems[i], recv_sem=recv_sems[i],
                device_id=sibling, device_id_type=pl.DeviceIdType.MESH))
    return copies


def _relay_start(lands, name, collective_id):
    n = len(lands)
    ns = n * len(RELAYED)
    hbm = pl.BlockSpec(memory_space=pltpu.HBM)
    sem = pl.BlockSpec(memory_space=pltpu.SEMAPHORE)

    def body(*refs):
        land = refs[:n]
        send_sems = refs[n:n + ns]
        recv_sems = refs[n + ns:n + 2 * ns]
        token = refs[2 * n + 2 * ns]
        _handshake(_peer_list((1,))[1])
        for cp in _relay_copies(land, send_sems, recv_sems, False):
            cp.start()
        token[...] = jnp.zeros_like(token)

    outs = pl.pallas_call(
        body, name=name,
        out_shape=(*[pltpu.SemaphoreType.DMA(())] * (2 * ns), *[pltpu.HBM(a.shape, a.dtype) for a in lands],
                   jax.ShapeDtypeStruct((SUBLANES, LANES), F32)),
        in_specs=[hbm] * n,
        out_specs=(*[sem] * (2 * ns), *[hbm] * n, pl.BlockSpec(memory_space=pltpu.VMEM)),
        input_output_aliases={i: 2 * ns + i for i in range(n)},
        compiler_params=pltpu.CompilerParams(has_side_effects=pltpu.SideEffectType.DATAFLOW_SIDE_EFFECTING,
                                             collective_id=collective_id),
    )(*lands)
    return outs[:ns], outs[ns:2 * ns], outs[2 * ns:2 * ns + n], outs[2 * ns + n]


def _relay_wait(send_sems, recv_sems, lands, after, name):
    n = len(lands)
    ns = n * len(RELAYED)
    hbm = pl.BlockSpec(memory_space=pltpu.HBM)
    sem = pl.BlockSpec(memory_space=pltpu.SEMAPHORE)

    def body(*refs):
        land = refs[:n]
        for cp in _relay_copies(land, refs[n:n + ns], refs[n + ns:n + 2 * ns], True):
            cp.wait_send()
            cp.wait_recv()

    return pl.pallas_call(
        body, name=name,
        out_shape=tuple(pltpu.HBM(a.shape, a.dtype) for a in lands),
        in_specs=[hbm] * n + [sem] * (2 * ns) + [pl.BlockSpec(memory_space=pl.ANY)],
        out_specs=tuple([hbm] * n),
        input_output_aliases={i: i for i in range(n)},
        compiler_params=pltpu.CompilerParams(has_side_effects=pltpu.SideEffectType.DATAFLOW_SIDE_EFFECTING),
    )(*lands, *send_sems, *recv_sems, after)


def _tn_matmul(a, b, name, tk, tm, out_rows=None, out_cols=None, out_dtype=F32, tn=512):
    n_tok, k_dim = a.shape
    m_dim = b.shape[1]
    grid = (k_dim // tk, m_dim // tm, n_tok // tn)

    def body(a_ref, b_ref, o_ref, acc):
        k = pl.program_id(2)
        part = _dot_tn(a_ref[...].astype(BF16), b_ref[...].astype(BF16))

        @pl.when(k == 0)
        def _():
            acc[...] = part

        @pl.when(k > 0)
        def _():
            acc[...] += part

        @pl.when(k == grid[2] - 1)
        def _():
            o_ref[...] = acc[...].astype(out_dtype)

    return pl.pallas_call(
        body, name=name, grid=grid,
        in_specs=[pl.BlockSpec((tn, tk), lambda i, j, k: (k, i)), pl.BlockSpec((tn, tm), lambda i, j, k: (k, j))],
        out_specs=pl.BlockSpec((tk, tm), lambda i, j, k: (i, j)),
        out_shape=jax.ShapeDtypeStruct((out_rows or k_dim, out_cols or m_dim), out_dtype),
        scratch_shapes=[pltpu.VMEM((tk, tm), F32)],
        compiler_params=_cparams("parallel", "parallel", "arbitrary"),
    )(a, b)


def _adam_math(g, w, m, v):
    m = ADAM_B1 * m + (1.0 - ADAM_B1) * g
    v = ADAM_B2 * v + (1.0 - ADAM_B2) * (g * g)
    m_hat = m / (1.0 - ADAM_B1 ** ADAM_STEP)
    v_hat = v / (1.0 - ADAM_B2 ** ADAM_STEP)
    delta = -ADAM_LR * (m_hat / (jnp.sqrt(v_hat) + ADAM_EPS) + ADAM_WD * w)
    return delta, m, v


def _adam_sharded(land, own, w, m, v, name, tile):
    _, r, c = w.shape

    def body(*refs):
        l_ref = refs[0]
        own_ref = refs[1] if own is not None else None
        w_ref, m_ref, v_ref, g_ref, d_ref, nm_ref, nv_ref = [ref.at[0] for ref in refs[-7:]]
        if own_ref is not None:
            x, y, z = lax.axis_index("x"), lax.axis_index("y"), lax.axis_index("c")
            me = 4 * x + 2 * y + z
            mine = own_ref[...].astype(F32)
        g = None
        for s in range(N_DEV):
            part = l_ref[s].astype(F32)
            if own_ref is not None:
                part = jnp.where(me == s, mine, part)
            g = part if g is None else g + part
        d, nm, nv = _adam_math(g, w_ref[...], m_ref[...], v_ref[...])
        g_ref[...] = g
        d_ref[...] = d
        nm_ref[...] = nm
        nv_ref[...] = nv

    tr, tc = tile
    spec = pl.BlockSpec((1, tr, tc), lambda i, j: (0, i, j))
    own_specs, own_args = ([pl.BlockSpec((tr, tc), lambda i, j: (i, j))], [own]) if own is not None else ([], [])
    return pl.pallas_call(
        body, name=name, grid=(r // tr, c // tc),
        in_specs=[pl.BlockSpec((N_DEV, tr, tc), lambda i, j: (0, i, j)), *own_specs, spec, spec, spec],
        out_specs=(spec, spec, spec, spec),
        out_shape=tuple(jax.ShapeDtypeStruct((1, r, c), F32) for _ in range(4)),
        compiler_params=_cparams("parallel", "parallel"),
    )(land, *own_args, w, m, v)


def _sum_partials(parts, own, name):
    _, r, c = parts.shape

    def body(*refs):
        p_ref, o_ref = refs[0], refs[-1]
        if own is not None:
            x, y, z = lax.axis_index("x"), lax.axis_index("y"), lax.axis_index("c")
            me = 4 * x + 2 * y + z
            mine = refs[1][...]
        g = None
        for s in range(N_DEV):
            part = p_ref[s]
            if own is not None:
                part = jnp.where(me == s, mine, part)
            g = part if g is None else g + part
        o_ref[...] = g

    args = (parts,) if own is None else (parts, own)
    return pl.pallas_call(body, name=name, out_shape=jax.ShapeDtypeStruct((r, c), F32),
                          compiler_params=pltpu.CompilerParams(vmem_limit_bytes=VMEM_LIMIT))(*args)


def _adam_replicated(gs, ws, ms, vs, name):
    k = len(ws)

    def body(*refs):
        outs = refs[4 * k:]
        for i in range(k):
            d, nm, nv = _adam_math(refs[i][...], refs[k + i][...], refs[2 * k + i][...], refs[3 * k + i][...])
            outs[i][...] = d
            outs[k + i][...] = nm
            outs[2 * k + i][...] = nv

    outs = pl.pallas_call(body, name=name, out_shape=tuple(jax.ShapeDtypeStruct(w.shape, F32) for w in ws) * 3,
                          compiler_params=pltpu.CompilerParams(vmem_limit_bytes=VMEM_LIMIT))(*gs, *ws, *ms, *vs)
    return outs[:k], outs[k:2 * k], outs[2 * k:]


def _inproj_fwd(x, g, w_in, avg, qg, kg, tm=512):
    n = x.shape[0]

    def body(x_ref, g_ref, w_ref, a_ref, qg_ref, kg_ref, hn_ref, u_ref, qkv_ref, raw_ref, fl_ref):
        hn = _rms(x_ref[...], g_ref[...]).astype(BF16)
        hn_ref[...] = hn
        proj = _dot_nt(hn, w_ref[...])
        u_ref[...] = proj[:, 0:512]
        q = proj[:, 512:1024]
        k = proj[:, 1024:1536]
        raw_ref[:, 0:512] = q
        raw_ref[:, 512:1024] = k
        qkv_ref[:, 0:512] = _headnorm(q, a_ref[...], qg_ref[...]).astype(BF16)
        qkv_ref[:, 512:1024] = _headnorm(k, a_ref[...], kg_ref[...]).astype(BF16)
        qkv_ref[:, 1024:1536] = proj[:, 1536:2048].astype(BF16)
        fl_ref[...] = proj[:, 2048:D_IN_PAD]

    row = lambda w: pl.BlockSpec((tm, w), lambda i: (i, 0))
    full = lambda a: pl.BlockSpec(a.shape, lambda i: (0,) * a.ndim)
    return pl.pallas_call(
        body, name="inproj_fwd", grid=(n // tm,),
        in_specs=[row(D_MODEL), full(g), full(w_in), full(avg), full(qg), full(kg)],
        out_specs=(row(D_MODEL), row(512), row(1536), row(1024), row(LANES)),
        out_shape=(jax.ShapeDtypeStruct((n, D_MODEL), BF16), jax.ShapeDtypeStruct((n, 512), F32),
                   jax.ShapeDtypeStruct((n, 1536), BF16), jax.ShapeDtypeStruct((n, 1024), F32),
                   jax.ShapeDtypeStruct((n, LANES), F32)),
        compiler_params=_cparams("parallel"),
    )(x, g, w_in, avg, qg, kg)


def _inproj_bwd(x, g, w_in, avg, qg, kg, raw, du, dqn, dkn, dv, dfl, dres, tm=512):
    n = x.shape[0]

    def body(x_ref, g_ref, w_ref, a_ref, qg_ref, kg_ref, raw_ref, du_ref, dqn_ref, dkn_ref, dv_ref, dfl_ref, dres_ref,
             dx_ref, dproj_ref, dg_ref, dqg_ref, dkg_ref):
        @pl.when(pl.program_id(0) == 0)
        def _():
            dg_ref[...] = jnp.zeros_like(dg_ref)
            dqg_ref[...] = jnp.zeros_like(dqg_ref)
            dkg_ref[...] = jnp.zeros_like(dkg_ref)
        avg_m = a_ref[...]
        _, vjp_q = jax.vjp(lambda q, gg: _headnorm(q, avg_m, gg), raw_ref[:, 0:512], qg_ref[...])
        dq, dqg = vjp_q(dqn_ref[...])
        _, vjp_k = jax.vjp(lambda k, gg: _headnorm(k, avg_m, gg), raw_ref[:, 512:1024], kg_ref[...])
        dk, dkg = vjp_k(dkn_ref[...])
        dproj = jnp.concatenate([du_ref[...], dq, dk, dv_ref[...], dfl_ref[...]], axis=1).astype(BF16)
        dproj_ref[...] = dproj
        dhn = _dot(dproj, w_ref[...])
        _, vjp_x = jax.vjp(_rms, x_ref[...], g_ref[...])
        dxn, dg = vjp_x(dhn)
        dx_ref[...] = dxn + dres_ref[...]
        dg_ref[...] += dg
        dqg_ref[...] += dqg
        dkg_ref[...] += dkg

    row = lambda w: pl.BlockSpec((tm, w), lambda i: (i, 0))
    full = lambda a: pl.BlockSpec(a.shape, lambda i: (0,) * a.ndim)
    vec = lambda w: pl.BlockSpec((1, w), lambda i: (0, 0))
    return pl.pallas_call(
        body, name="inproj_bwd", grid=(n // tm,),
        in_specs=[row(D_MODEL), full(g), full(w_in), full(avg), full(qg), full(kg), row(1024), row(512), row(512),
                  row(512), row(512), row(LANES), row(D_MODEL)],
        out_specs=(row(D_MODEL), row(D_IN_PAD), vec(D_MODEL), vec(512), vec(512)),
        out_shape=(jax.ShapeDtypeStruct((n, D_MODEL), F32), jax.ShapeDtypeStruct((n, D_IN_PAD), BF16),
                   jax.ShapeDtypeStruct((1, D_MODEL), F32), jax.ShapeDtypeStruct((1, 512), F32),
                   jax.ShapeDtypeStruct((1, 512), F32)),
        compiler_params=_cparams("arbitrary"),
    )(x, g, w_in, avg, qg, kg, raw, du, dqn, dkn, dv, dfl, dres)


def _glu_fwd(yc, wg, bg, tm=512):
    n = yc.shape[0]

    def body(yc_ref, w_ref, b_ref, ys_ref):
        gl = jax.nn.gelu(yc_ref[...])
        z = _dot(gl.astype(BF16), w_ref[...]) + b_ref[...]
        ys_ref[...] = gl * jax.nn.sigmoid(z)

    row = pl.BlockSpec((tm, 512), lambda i: (i, 0))
    full = lambda a: pl.BlockSpec(a.shape, lambda i: (0,) * a.ndim)
    return pl.pallas_call(
        body, name="glu_fwd", grid=(n // tm,), in_specs=[row, full(wg), full(bg)], out_specs=row,
        out_shape=jax.ShapeDtypeStruct((n, 512), F32), compiler_params=_cparams("parallel"),
    )(yc, wg, bg)


def _glu_bwd(yc, dys, wg, bg, tm=512):
    n = yc.shape[0]

    def body(yc_ref, dys_ref, w_ref, b_ref, dyc_ref, gl_ref, dz_ref, db_ref):
        @pl.when(pl.program_id(0) == 0)
        def _():
            db_ref[...] = jnp.zeros_like(db_ref)
        gl, vjp_gelu = jax.vjp(jax.nn.gelu, yc_ref[...])
        glb = gl.astype(BF16)
        z = _dot(glb, w_ref[...]) + b_ref[...]
        s = jax.nn.sigmoid(z)
        dys = dys_ref[...]
        dz = dys * gl * s * (1.0 - s)
        dzb = dz.astype(BF16)
        dgl = dys * s + _dot_nt(dzb, w_ref[...])
        dyc_ref[...] = vjp_gelu(dgl)[0]
        gl_ref[...] = glb
        dz_ref[...] = dzb
        db_ref[...] += jnp.sum(dz, axis=0, keepdims=True)

    row = pl.BlockSpec((tm, 512), lambda i: (i, 0))
    full = lambda a: pl.BlockSpec(a.shape, lambda i: (0,) * a.ndim)
    return pl.pallas_call(
        body, name="glu_bwd", grid=(n // tm,), in_specs=[row, row, full(wg), full(bg)],
        out_specs=(row, row, row, pl.BlockSpec((1, 512), lambda i: (0, 0))),
        out_shape=(jax.ShapeDtypeStruct((n, 512), F32), jax.ShapeDtypeStruct((n, 512), BF16),
                   jax.ShapeDtypeStruct((n, 512), BF16), jax.ShapeDtypeStruct((1, 512), F32)),
        compiler_params=_cparams("arbitrary"),
    )(yc, dys, wg, bg)


def _mix_fwd(x, ys, ya, gs, ga, wout, gf, tm=512):
    n = x.shape[0]

    def body(x_ref, ys_ref, ya_ref, gs_ref, ga_ref, w_ref, gf_ref, h1_ref, hn2_ref, mixed_ref):
        mixed = jnp.concatenate([_rms(ys_ref[...], gs_ref[...]), _rms(ya_ref[...], ga_ref[...])], axis=1).astype(BF16)
        mixed_ref[...] = mixed
        h1 = x_ref[...] + _dot(mixed, w_ref[...])
        h1_ref[...] = h1
        hn2_ref[...] = _rms(h1, gf_ref[...]).astype(BF16)

    row = lambda w: pl.BlockSpec((tm, w), lambda i: (i, 0))
    full = lambda a: pl.BlockSpec(a.shape, lambda i: (0,) * a.ndim)
    return pl.pallas_call(
        body, name="mix_fwd", grid=(n // tm,),
        in_specs=[row(D_MODEL), row(512), row(512), full(gs), full(ga), full(wout), full(gf)],
        out_specs=(row(D_MODEL), row(D_MODEL), row(D_MODEL)),
        out_shape=(jax.ShapeDtypeStruct((n, D_MODEL), F32), jax.ShapeDtypeStruct((n, D_MODEL), BF16),
                   jax.ShapeDtypeStruct((n, D_MODEL), BF16)),
        compiler_params=_cparams("parallel"),
    )(x, ys, ya, gs, ga, wout, gf)


def _mix_bwd(dy, dhn2_parts, h1, ys, ya, gs, ga, wout, gf, tm=512):
    n = dy.shape[0]
    n_parts = dhn2_parts.shape[0]

    def body(dy_ref, dp_ref, h1_ref, ys_ref, ya_ref, gs_ref, ga_ref, w_ref, gf_ref,
             dh1_ref, dys_ref, dya_ref, dgs_ref, dga_ref, dgf_ref):
        @pl.when(pl.program_id(0) == 0)
        def _():
            dgs_ref[...] = jnp.zeros_like(dgs_ref)
            dga_ref[...] = jnp.zeros_like(dga_ref)
            dgf_ref[...] = jnp.zeros_like(dgf_ref)
        dhn2 = dp_ref[0]
        for p in range(1, n_parts):
            dhn2 = dhn2 + dp_ref[p]
        _, vjp_f = jax.vjp(_rms, h1_ref[...], gf_ref[...])
        dh1n, dgf = vjp_f(dhn2)
        dh1 = dy_ref[...] + dh1n
        dh1_ref[...] = dh1
        dmixed = _dot_nt(dh1.astype(BF16), w_ref[...])
        _, vjp_s = jax.vjp(_rms, ys_ref[...], gs_ref[...])
        dys, dgs = vjp_s(dmixed[:, 0:512])
        _, vjp_a = jax.vjp(_rms, ya_ref[...], ga_ref[...])
        dya, dga = vjp_a(dmixed[:, 512:1024])
        dys_ref[...] = dys
        dya_ref[...] = dya
        dgs_ref[...] += dgs
        dga_ref[...] += dga
        dgf_ref[...] += dgf

    row = lambda w: pl.BlockSpec((tm, w), lambda i: (i, 0))
    full = lambda a: pl.BlockSpec(a.shape, lambda i: (0,) * a.ndim)
    vec = lambda w: pl.BlockSpec((1, w), lambda i: (0, 0))
    return pl.pallas_call(
        body, name="mix_bwd", grid=(n // tm,),
        in_specs=[row(D_MODEL), pl.BlockSpec((n_parts, tm, D_MODEL), lambda i: (0, i, 0)), row(D_MODEL), row(512),
                  row(512), full(gs), full(ga), full(wout), full(gf)],
        out_specs=(row(D_MODEL), row(512), row(512), vec(512), vec(512), vec(D_MODEL)),
        out_shape=(jax.ShapeDtypeStruct((n, D_MODEL), F32), jax.ShapeDtypeStruct((n, 512), F32),
                   jax.ShapeDtypeStruct((n, 512), F32), jax.ShapeDtypeStruct((1, 512), F32),
                   jax.ShapeDtypeStruct((1, 512), F32), jax.ShapeDtypeStruct((1, D_MODEL), F32)),
        compiler_params=_cparams("arbitrary"),
    )(dy, dhn2_parts, h1, ys, ya, gs, ga, wout, gf)


HALO = 16
FFN_GROUPS = 4
FFN_GROUP = D_FF // FFN_GROUPS


def _conv3(ue, cw):
    return cw[2:3] * ue + cw[1:2] * pltpu.roll(ue, 1, 0) + cw[0:1] * pltpu.roll(ue, 2, 0) + cw[3:4]


def _ffn_weight_specs():
    gate = lambda i, j: (j, 0, 0)
    val = lambda i, j: (j + FFN_GROUPS, 0, 0)
    w_blk, c_blk = (1, FFN_GROUP, D_MODEL), (1, SUBLANES, FFN_GROUP)
    return [pl.BlockSpec(w_blk, gate), pl.BlockSpec(w_blk, val), pl.BlockSpec(c_blk, gate), pl.BlockSpec(c_blk, val),
            pl.BlockSpec((1, FFN_GROUP, D_MODEL), gate)]


def _ffn_fwd(hn2, h1, target, w_up, conv, w_down, seq_len, tm=512):
    n = hn2.shape[0]
    nj = FFN_GROUPS
    hb = tm // HALO

    def body(hn_ref, halo_ref, h1_ref, tgt_ref, wg_ref, wv_ref, cg_ref, cv_ref, wd_ref,
             ug_ref, uv_ref, pg_ref, pv_ref, dy_ref, loss_ref, acc):
        i, j = pl.program_id(0), pl.program_id(1)
        seq_start = (i * tm) % seq_len == 0
        halo = halo_ref[...]
        halo = jnp.where(seq_start, jnp.zeros_like(halo), halo)
        he = jnp.concatenate([halo, hn_ref[...]], axis=0)
        ueg = _dot_nt(he, wg_ref[0])
        uev = _dot_nt(he, wv_ref[0])
        ug_ref[0] = ueg[HALO:].astype(BF16)
        uv_ref[0] = uev[HALO:].astype(BF16)
        cg = _conv3(ueg, cg_ref[0])[HALO:]
        cv = _conv3(uev, cv_ref[0])[HALO:]
        pg_ref[0] = cg.astype(BF16)
        pv_ref[0] = cv.astype(BF16)
        act = (jax.nn.silu(cg) * cv).astype(BF16)
        part = _dot(act, wd_ref[0])

        @pl.when(j == 0)
        def _():
            acc[...] = part

        @pl.when(j > 0)
        def _():
            acc[...] += part

        @pl.when(j == nj - 1)
        def _():
            err = h1_ref[...] + acc[...] - tgt_ref[...]
            dy_ref[...] = err * (1.0 / D_MODEL)
            loss_ref[0] = jnp.sum(err * err, axis=0, keepdims=True)

    row = pl.BlockSpec((tm, D_MODEL), lambda i, j: (i, 0))
    u_main = pl.BlockSpec((1, tm, FFN_GROUP), lambda i, j: (j, i, 0))
    u_shape = jax.ShapeDtypeStruct((FFN_GROUPS, n, FFN_GROUP), BF16)
    return pl.pallas_call(
        body, name="ffn_fwd", grid=(n // tm, nj),
        in_specs=[row, pl.BlockSpec((HALO, D_MODEL), lambda i, j: (jnp.maximum(i * hb - 1, 0), 0)), row, row,
                  *_ffn_weight_specs()],
        out_specs=(u_main, u_main, u_main, u_main, row, pl.BlockSpec((1, 1, D_MODEL), lambda i, j: (i, 0, 0))),
        out_shape=(u_shape, u_shape, u_shape, u_shape, jax.ShapeDtypeStruct((n, D_MODEL), F32),
                   jax.ShapeDtypeStruct((n // tm, 1, D_MODEL), F32)),
        scratch_shapes=[pltpu.VMEM((tm, D_MODEL), F32)],
        compiler_params=_cparams("parallel", "arbitrary"),
    )(hn2, hn2, h1, target, w_up, w_up, conv, conv, w_down)


def _ffn_bwd(dy, ug, uv, pg, pv, w_up, conv, w_down, seq_len, tm=512):
    n = dy.shape[0]
    nj = FFN_GROUPS
    fb = FFN_GROUP
    hb = tm // HALO
    last_hb = n // HALO - 1
    rows = tm + HALO

    def body(dy_ref, dyn_ref, ug_ref, uv_ref, pgm_ref, pgn_ref, pvm_ref, pvn_ref, wg_ref, wv_ref, cg_ref, cv_ref,
             wd_ref, dug_ref, duv_ref, act_ref, dhn_ref, dcg_ref, dcv_ref, acc):
        i, j = pl.program_id(0), pl.program_id(1)
        seq_end = ((i + 1) * tm) % seq_len == 0
        dyn = dyn_ref[...]
        dyn = jnp.where(seq_end, jnp.zeros_like(dyn), dyn)
        d_out = jnp.concatenate([dy_ref[...], dyn], axis=0).astype(BF16)
        d_act = _dot_nt(d_out, wd_ref[0])
        cge = jnp.concatenate([pgm_ref[0], pgn_ref[0]], axis=0).astype(F32)
        cve = jnp.concatenate([pvm_ref[0], pvn_ref[0]], axis=0).astype(F32)
        act, vjp_act = jax.vjp(lambda g, v: jax.nn.silu(g) * v, cge, cve)
        dcge, dcve = vjp_act(d_act)
        act_ref[0] = act[:tm].astype(BF16)

        def conv_t(dc, u_ref, cw):
            ahead1 = pltpu.roll(dc, rows - 1, 0)[:tm]
            ahead2 = pltpu.roll(dc, rows - 2, 0)[:tm]
            here = dc[:tm]
            du = cw[2:3] * here + cw[1:2] * ahead1 + cw[0:1] * ahead2
            u = u_ref[0].astype(F32)
            col = lambda x: jnp.sum(x, axis=0, keepdims=True)
            grad = jnp.concatenate([col(ahead2 * u), col(ahead1 * u), col(here * u), col(here),
                                    jnp.zeros((4, fb), F32)], axis=0)
            return du.astype(BF16), grad

        cwg, cwv = cg_ref[0], cv_ref[0]
        dug, grad_g = conv_t(dcge, ug_ref, cwg)
        duv, grad_v = conv_t(dcve, uv_ref, cwv)
        dug_ref[0] = dug
        duv_ref[0] = duv
        part = _dot(dug, wg_ref[0]) + _dot(duv, wv_ref[0])

        @pl.when(j == 0)
        def _():
            acc[...] = part

        @pl.when(j > 0)
        def _():
            acc[...] += part

        @pl.when(j == nj - 1)
        def _():
            dhn_ref[...] = acc[...]

        @pl.when(i == 0)
        def _():
            dcg_ref[j] = jnp.zeros((8, fb), F32)
            dcv_ref[j] = jnp.zeros((8, fb), F32)

        dcg_ref[j] += grad_g
        dcv_ref[j] += grad_v

    row = pl.BlockSpec((tm, D_MODEL), lambda i, j: (i, 0))
    u_main = pl.BlockSpec((1, tm, fb), lambda i, j: (j, i, 0))
    u_next = pl.BlockSpec((1, HALO, fb), lambda i, j: (j, jnp.minimum((i + 1) * hb, last_hb), 0))
    dc_spec = pl.BlockSpec((nj, 8, fb), lambda i, j: (0, 0, 0))
    u_shape = jax.ShapeDtypeStruct((FFN_GROUPS, n, fb), BF16)
    return pl.pallas_call(
        body, name="ffn_bwd", grid=(n // tm, nj),
        in_specs=[row, pl.BlockSpec((HALO, D_MODEL), lambda i, j: (jnp.minimum((i + 1) * hb, last_hb), 0)),
                  u_main, u_main, u_main, u_next, u_main, u_next, *_ffn_weight_specs()],
        out_specs=(u_main, u_main, u_main, row, dc_spec, dc_spec),
        out_shape=(u_shape, u_shape, u_shape, jax.ShapeDtypeStruct((n, D_MODEL), F32),
                   jax.ShapeDtypeStruct((nj, 8, fb), F32), jax.ShapeDtypeStruct((nj, 8, fb), F32)),
        scratch_shapes=[pltpu.VMEM((tm, D_MODEL), F32)],
        compiler_params=_cparams("arbitrary", "arbitrary"),
    )(dy, dy, ug, uv, pg, pg, pv, pv, w_up, w_up, conv, conv, w_down)


def _tn_grouped(a, b, name, shared_a, out_dtype=F32, tn=1024):
    groups = b.shape[0] if shared_a else a.shape[0]
    n_tok = a.shape[0] if shared_a else b.shape[0]
    k_dim, m_dim = a.shape[-1], b.shape[-1]

    def body(a_ref, b_ref, o_ref, acc):
        k = pl.program_id(1)
        a_t = a_ref[...] if shared_a else a_ref[0]
        b_t = b_ref[0] if shared_a else b_ref[...]
        part = _dot_tn(a_t.astype(BF16), b_t.astype(BF16))

        @pl.when(k == 0)
        def _():
            acc[...] = part

        @pl.when(k > 0)
        def _():
            acc[...] += part

        @pl.when(k == n_tok // tn - 1)
        def _():
            o_ref[0] = acc[...].astype(out_dtype)

    plain = lambda w: pl.BlockSpec((tn, w), lambda g, k: (k, 0))
    grouped = lambda w: pl.BlockSpec((1, tn, w), lambda g, k: (g, k, 0))
    return pl.pallas_call(
        body, name=name, grid=(groups, n_tok // tn),
        in_specs=[plain(k_dim), grouped(m_dim)] if shared_a else [grouped(k_dim), plain(m_dim)],
        out_specs=pl.BlockSpec((1, k_dim, m_dim), lambda g, k: (g, 0, 0)),
        out_shape=jax.ShapeDtypeStruct((groups, k_dim, m_dim), out_dtype),
        scratch_shapes=[pltpu.VMEM((k_dim, m_dim), F32)],
        compiler_params=_cparams("parallel", "arbitrary"),
    )(a, b)


def _s5_param_fn(lr, li, ldt, br, bi):
    dt = jnp.exp(ldt)
    mag = jnp.exp(lr * dt)
    ab_re = mag * jnp.cos(li * dt)
    ab_im = mag * jnp.sin(li * dt)
    nr = ab_re - 1.0
    ni = ab_im
    den = lr * lr + li * li
    q_re = (nr * lr + ni * li) / den
    q_im = (ni * lr - nr * li) / den
    bb_re = q_re * br - q_im * bi
    bb_im = q_re * bi + q_im * br
    return ab_re, ab_im, bb_re, bb_im


def _s5_param_fwd(lr, li, ldt, br, bi):
    def body(lr_ref, li_ref, ldt_ref, br_ref, bi_ref, ar_ref, ai_ref, bbr_ref, bbi_ref):
        ar, ai, bbr, bbi = _s5_param_fn(lr_ref[...], li_ref[...], ldt_ref[...], br_ref[...], bi_ref[...])
        ar_ref[...] = ar
        ai_ref[...] = ai
        bbr_ref[...] = bbr
        bbi_ref[...] = bbi

    return pl.pallas_call(
        body, name="s5_param_fwd",
        out_shape=(jax.ShapeDtypeStruct(lr.shape, F32), jax.ShapeDtypeStruct(lr.shape, F32),
                   jax.ShapeDtypeStruct(br.shape, F32), jax.ShapeDtypeStruct(br.shape, F32)),
    )(lr, li, ldt, br, bi)


def _s5_param_bwd(lr, li, ldt, br, bi, dar, dai, dbbr, dbbi):
    def body(lr_ref, li_ref, ldt_ref, br_ref, bi_ref, dar_ref, dai_ref, dbbr_ref, dbbi_ref,
             dlr_ref, dli_ref, dldt_ref, dbr_ref, dbi_ref):
        _, vjp = jax.vjp(_s5_param_fn, lr_ref[...], li_ref[...], ldt_ref[...], br_ref[...], bi_ref[...])
        dlr, dli, dldt, dbr, dbi = vjp((dar_ref[...], dai_ref[...], dbbr_ref[...], dbbi_ref[...]))
        dlr_ref[...] = dlr
        dli_ref[...] = dli
        dldt_ref[...] = dldt
        dbr_ref[...] = dbr
        dbi_ref[...] = dbi

    return pl.pallas_call(
        body, name="s5_param_bwd",
        out_shape=(jax.ShapeDtypeStruct(lr.shape, F32), jax.ShapeDtypeStruct(lr.shape, F32),
                   jax.ShapeDtypeStruct(ldt.shape, F32), jax.ShapeDtypeStruct(br.shape, F32),
                   jax.ShapeDtypeStruct(br.shape, F32)),
    )(lr, li, ldt, br, bi, dar, dai, dbbr, dbbi)


S5_CHUNK = 1024
S5_STATES = 512
S5_BLOCKS = 4


def _cpow_rows(ar, ai, count):
    rs, im = [ar], [ai]
    for _ in range(count - 1):
        pr, pi = rs[-1], im[-1]
        rs.append(pr * ar - pi * ai)
        im.append(pr * ai + pi * ar)
    return rs, im


def _scan_in_groups(vr, vi, pr, pi, rm, reverse):
    n, width = vr.shape
    vr = vr.reshape(n // SUBLANES, SUBLANES, width)
    vi = vi.reshape(n // SUBLANES, SUBLANES, width)
    row = rm[0:SUBLANES]
    for k in (1, 2, 4):
        shift = SUBLANES - k if reverse else k
        keep = row < SUBLANES - k if reverse else row >= k
        kr = jnp.where(keep, pr[k - 1], 0.0)
        ki = jnp.where(keep, pi[k - 1], 0.0)
        sr, si = pltpu.roll(vr, shift, 1), pltpu.roll(vi, shift, 1)
        vr, vi = vr + kr * sr - ki * si, vi + kr * si + ki * sr
    return vr.reshape(n, width), vi.reshape(n, width)


def _carry_over_groups(xr_s, xi_s, wr, wi, c0r, c0i, reverse):
    groups = xr_s.shape[0] // SUBLANES
    pick = 0 if reverse else SUBLANES - 1

    def step(q, carry):
        cr, ci = carry
        r = groups - 1 - q if reverse else q
        o = pl.multiple_of(r * SUBLANES, SUBLANES)
        vr = xr_s[pl.ds(o, SUBLANES), :]
        vi = xi_s[pl.ds(o, SUBLANES), :]
        nr = vr + wr * cr - wi * ci
        ni = vi + wr * ci + wi * cr
        xr_s[pl.ds(o, SUBLANES), :] = nr
        xi_s[pl.ds(o, SUBLANES), :] = ni
        return (jnp.broadcast_to(nr[pick:pick + 1], nr.shape), jnp.broadcast_to(ni[pick:pick + 1], ni.shape))

    return lax.fori_loop(0, groups, step, (c0r, c0i), unroll=4)


def _s5_state_scan(u_b, bbr, bbi, pr, pi, rm, xr_s, xi_s, c0r, c0i):
    bur = _dot(u_b, bbr)
    bui = _dot(u_b, bbi)
    bur, bui = _scan_in_groups(bur, bui, pr, pi, rm, False)
    xr_s[...] = bur
    xi_s[...] = bui
    w8r = jnp.concatenate(pr, axis=0)
    w8i = jnp.concatenate(pi, axis=0)
    return _carry_over_groups(xr_s, xi_s, w8r, w8i, c0r, c0i, False)


def _s5_fwd(u, a_re, a_im, bbr, bbi, cr, ci, d_skip, n_seq):
    n = u.shape[0]
    seq_len = n // n_seq
    nt = seq_len // S5_CHUNK
    tc = S5_CHUNK

    def body(u_ref, ar_ref, ai_ref, bbr_ref, bbi_ref, cr_ref, ci_ref, d_ref, y_ref, str_ref, sti_ref, xrb_ref, xib_ref,
             xr_s, xi_s, car_r, car_i):
        t = pl.program_id(2)

        @pl.when(t == 0)
        def _():
            car_r[...] = jnp.zeros_like(car_r)
            car_i[...] = jnp.zeros_like(car_i)
        pr, pi = _cpow_rows(ar_ref[0], ai_ref[0], SUBLANES)
        rm = lax.broadcasted_iota(jnp.int32, (tc, S5_STATES), 0) & (SUBLANES - 1)
        str_ref[0, 0] = car_r[...]
        sti_ref[0, 0] = car_i[...]
        u_t = u_ref[...]
        cfr, cfi = _s5_state_scan(u_t.astype(BF16), bbr_ref[0], bbi_ref[0], pr, pi, rm, xr_s, xi_s,
                                  car_r[...], car_i[...])
        car_r[...] = cfr
        car_i[...] = cfi
        xr_b = xr_s[...].astype(BF16)
        xi_b = xi_s[...].astype(BF16)
        xrb_ref[...] = xr_b
        xib_ref[...] = xi_b
        y_ref[...] = _dot(xr_b, cr_ref[0]) - _dot(xi_b, ci_ref[0]) + d_ref[...] * u_t

    x_spec = pl.BlockSpec((tc, S5_STATES), lambda cb, b, t: (b * nt + t, cb))
    x_shape = jax.ShapeDtypeStruct((n, S5_BLOCKS * S5_STATES), BF16)
    u_spec = pl.BlockSpec((tc, LANES), lambda cb, b, t: (b * nt + t, cb))
    a_spec = pl.BlockSpec((1, 1, S5_STATES), lambda cb, b, t: (cb, 0, 0))
    bb_spec = pl.BlockSpec((1, LANES, S5_STATES), lambda cb, b, t: (cb, 0, 0))
    c_spec = pl.BlockSpec((1, S5_STATES, LANES), lambda cb, b, t: (cb, 0, 0))
    st_spec = pl.BlockSpec((1, 1, SUBLANES, S5_STATES), lambda cb, b, t: (cb, b * nt + t, 0, 0))
    st_shape = jax.ShapeDtypeStruct((S5_BLOCKS, n_seq * nt, SUBLANES, S5_STATES), F32)
    return pl.pallas_call(
        body, name="s5_fwd", grid=(S5_BLOCKS, n_seq, nt),
        in_specs=[u_spec, a_spec, a_spec, bb_spec, bb_spec, c_spec, c_spec,
                  pl.BlockSpec((1, LANES), lambda cb, b, t: (0, cb))],
        out_specs=(u_spec, st_spec, st_spec, x_spec, x_spec),
        out_shape=(jax.ShapeDtypeStruct((n, D_SSM), F32), st_shape, st_shape, x_shape, x_shape),
        scratch_shapes=[pltpu.VMEM((tc, S5_STATES), F32), pltpu.VMEM((tc, S5_STATES), F32),
                        pltpu.VMEM((SUBLANES, S5_STATES), F32), pltpu.VMEM((SUBLANES, S5_STATES), F32)],
        compiler_params=_cparams("parallel", "arbitrary", "arbitrary"),
    )(u, a_re, a_im, bbr, bbi, cr, ci, d_skip)


def _s5_bwd(u, dy, xs_r, xs_i, st_r, st_i, a_re, a_im, bbr, bbi, cr, ci, d_skip, n_seq):
    n = u.shape[0]
    seq_len = n // n_seq
    nt = seq_len // S5_CHUNK
    tc = S5_CHUNK

    def body(u_ref, dy_ref, xrb_ref, xib_ref, str_ref, sti_ref, ar_ref, ai_ref, bbr_ref, bbi_ref, cr_ref, ci_ref, d_ref,
             du_ref, dbbr_ref, dbbi_ref, dcr_ref, dci_ref, dar_ref, dai_ref, dd_ref,
             gr_s, gi_s, car_r, car_i):
        b, t = pl.program_id(1), pl.program_id(2)

        @pl.when((b == 0) & (t == 0))
        def _():
            for ref in (dbbr_ref, dbbi_ref, dcr_ref, dci_ref, dar_ref, dai_ref, dd_ref):
                ref[...] = jnp.zeros_like(ref)

        @pl.when(t == 0)
        def _():
            car_r[...] = jnp.zeros_like(car_r)
            car_i[...] = jnp.zeros_like(car_i)
        ar, ai = ar_ref[0], ai_ref[0]
        pr, pi = _cpow_rows(ar, ai, SUBLANES)
        row = lax.broadcasted_iota(jnp.int32, (tc, S5_STATES), 0)
        rm = row & (SUBLANES - 1)
        u_t = u_ref[...]
        u_b = u_t.astype(BF16)
        dy_t = dy_ref[...]
        dy_b = dy_t.astype(BF16)
        s0r, s0i = str_ref[0, 0], sti_ref[0, 0]
        xr_b, xi_b = xrb_ref[...], xib_ref[...]
        xr, xi = xr_b.astype(F32), xi_b.astype(F32)
        gr = _dot_nt(dy_b, cr_ref[0])
        gi = -_dot_nt(dy_b, ci_ref[0])
        npi = [-v for v in pi]
        gr, gi = _scan_in_groups(gr, gi, pr, npi, rm, True)
        gr_s[...] = gr
        gi_s[...] = gi
        w8r = jnp.concatenate(pr[::-1], axis=0)
        w8i = jnp.concatenate(npi[::-1], axis=0)
        cfr, cfi = _carry_over_groups(gr_s, gi_s, w8r, w8i, car_r[...], car_i[...], True)
        car_r[...] = cfr
        car_i[...] = cfi
        gr, gi = gr_s[...], gi_s[...]
        gr_b, gi_b = gr.astype(BF16), gi.astype(BF16)
        du_ref[...] = _dot_nt(gr_b, bbr_ref[0]) + _dot_nt(gi_b, bbi_ref[0]) + d_ref[...] * dy_t
        dbbr_ref[0] += _dot_tn(u_b, gr_b)
        dbbi_ref[0] += _dot_tn(u_b, gi_b)
        dcr_ref[0] += _dot_tn(xr_b, dy_b)
        dci_ref[0] -= _dot_tn(xi_b, dy_b)
        dd_ref[0] += jnp.sum((dy_t * u_t).reshape(tc // SUBLANES, SUBLANES, LANES), axis=0)
        first = row == 0
        xpr = jnp.where(first, jnp.broadcast_to(s0r[0:1], xr.shape), pltpu.roll(xr, 1, 0))
        xpi = jnp.where(first, jnp.broadcast_to(s0i[0:1], xi.shape), pltpu.roll(xi, 1, 0))
        shp = (tc // SUBLANES, SUBLANES, S5_STATES)
        dar_ref[0] += jnp.sum((gr * xpr + gi * xpi).reshape(shp), axis=0)
        dai_ref[0] += jnp.sum((gi * xpr - gr * xpi).reshape(shp), axis=0)

    u_spec = pl.BlockSpec((tc, LANES), lambda cb, b, t: (b * nt + nt - 1 - t, cb))
    a_spec = pl.BlockSpec((1, 1, S5_STATES), lambda cb, b, t: (cb, 0, 0))
    bb_spec = pl.BlockSpec((1, LANES, S5_STATES), lambda cb, b, t: (cb, 0, 0))
    c_spec = pl.BlockSpec((1, S5_STATES, LANES), lambda cb, b, t: (cb, 0, 0))
    st_spec = pl.BlockSpec((1, 1, SUBLANES, S5_STATES), lambda cb, b, t: (cb, b * nt + nt - 1 - t, 0, 0))
    da_spec = pl.BlockSpec((1, SUBLANES, S5_STATES), lambda cb, b, t: (cb, 0, 0))
    dd_spec = pl.BlockSpec((1, SUBLANES, LANES), lambda cb, b, t: (cb, 0, 0))
    big = pltpu.VMEM((tc, S5_STATES), F32)
    small = pltpu.VMEM((SUBLANES, S5_STATES), F32)
    x_spec = pl.BlockSpec((tc, S5_STATES), lambda cb, b, t: (b * nt + nt - 1 - t, cb))
    return pl.pallas_call(
        body, name="s5_bwd", grid=(S5_BLOCKS, n_seq, nt),
        in_specs=[u_spec, u_spec, x_spec, x_spec, st_spec, st_spec, a_spec, a_spec, bb_spec, bb_spec, c_spec, c_spec,
                  pl.BlockSpec((1, LANES), lambda cb, b, t: (0, cb))],
        out_specs=(u_spec, bb_spec, bb_spec, c_spec, c_spec, da_spec, da_spec, dd_spec),
        out_shape=(jax.ShapeDtypeStruct((n, D_SSM), F32),
                   jax.ShapeDtypeStruct((S5_BLOCKS, LANES, S5_STATES), F32),
                   jax.ShapeDtypeStruct((S5_BLOCKS, LANES, S5_STATES), F32),
                   jax.ShapeDtypeStruct((S5_BLOCKS, S5_STATES, LANES), F32),
                   jax.ShapeDtypeStruct((S5_BLOCKS, S5_STATES, LANES), F32),
                   jax.ShapeDtypeStruct((S5_BLOCKS, SUBLANES, S5_STATES), F32),
                   jax.ShapeDtypeStruct((S5_BLOCKS, SUBLANES, S5_STATES), F32),
                   jax.ShapeDtypeStruct((S5_BLOCKS, SUBLANES, LANES), F32)),
        scratch_shapes=[big, big, small, small],
        compiler_params=_cparams("parallel", "arbitrary", "arbitrary"),
    )(u, dy, xs_r, xs_i, st_r, st_i, a_re, a_im, bbr, bbi, cr, ci, d_skip)


CUM_BLOCK = 128


def _tri(lower):
    r = lax.broadcasted_iota(jnp.int32, (CUM_BLOCK, CUM_BLOCK), 0)
    c = lax.broadcasted_iota(jnp.int32, (CUM_BLOCK, CUM_BLOCK), 1)
    return jnp.where(r >= c if lower else r <= c, 1.0, 0.0).astype(F32)


def _fprep_fwd(fl, bf, n_seq):
    n = fl.shape[0]
    seq_len = n // n_seq
    nb = seq_len // CUM_BLOCK

    def body(fl_ref, bf_ref, cum_ref):
        tril = _tri(True)
        carry = jnp.zeros((1, LANES), F32)
        for blk in range(nb):
            rows = slice(blk * CUM_BLOCK, (blk + 1) * CUM_BLOCK)
            lf = jax.nn.log_sigmoid(fl_ref[rows, :] + bf_ref[...])
            cs = jnp.dot(tril, lf, preferred_element_type=F32, precision=HIGHEST) + carry
            cum_ref[rows, :] = cs
            carry = cs[CUM_BLOCK - 1:CUM_BLOCK, :]

    spec = pl.BlockSpec((seq_len, LANES), lambda b: (b, 0))
    return pl.pallas_call(
        body, name="fprep_fwd", grid=(n_seq,), in_specs=[spec, pl.BlockSpec((1, LANES), lambda b: (0, 0))],
        out_specs=spec, out_shape=jax.ShapeDtypeStruct((n, LANES), F32), compiler_params=_cparams("parallel"),
    )(fl, bf)


def _fprep_bwd(dcum, fl, bf, n_seq):
    n = fl.shape[0]
    seq_len = n // n_seq
    nb = seq_len // CUM_BLOCK

    def body(dcum_ref, fl_ref, bf_ref, dfl_ref, dbf_ref):
        triu = _tri(False)
        lane = lax.broadcasted_iota(jnp.int32, (CUM_BLOCK, LANES), 1)
        carry = jnp.zeros((1, LANES), F32)
        total = jnp.zeros((1, LANES), F32)
        for blk in reversed(range(nb)):
            rows = slice(blk * CUM_BLOCK, (blk + 1) * CUM_BLOCK)
            rs = jnp.dot(triu, dcum_ref[rows, :], preferred_element_type=F32, precision=HIGHEST) + carry
            carry = rs[0:1, :]
            _, vjp = jax.vjp(jax.nn.log_sigmoid, fl_ref[rows, :] + bf_ref[...])
            dz = jnp.where(lane < N_HEADS, vjp(rs)[0], 0.0)
            dfl_ref[rows, :] = dz
            total = total + jnp.sum(dz, axis=0, keepdims=True)
        dbf_ref[0] = total

    spec = pl.BlockSpec((seq_len, LANES), lambda b: (b, 0))
    return pl.pallas_call(
        body, name="fprep_bwd", grid=(n_seq,), in_specs=[spec, spec, pl.BlockSpec((1, LANES), lambda b: (0, 0))],
        out_specs=(spec, pl.BlockSpec((1, 1, LANES), lambda b: (b, 0, 0))),
        out_shape=(jax.ShapeDtypeStruct((n, LANES), F32), jax.ShapeDtypeStruct((n_seq, 1, LANES), F32)),
        compiler_params=_cparams("parallel"),
    )(dcum, fl, bf)


ATT_TQ = 256
ATT_KSTEP = 256
ATT_PAIRS = 2
ATT_SCALE = HEAD_DIM ** -0.5
NEG_BIG = -1e30


assert ATT_KSTEP == ATT_TQ


def _scores(q_scaled, kb, row_bias, ck, kend):
    s = _dot_nt(q_scaled, kb) - ck
    if row_bias is not None:
        s = s + row_bias
    r = lax.broadcasted_iota(jnp.int32, (ATT_TQ, ATT_TQ), 0)
    c = lax.broadcasted_iota(jnp.int32, (ATT_TQ, ATT_TQ), 1)
    diag = jnp.where(r >= c, s[:, kend - ATT_TQ:], NEG_BIG)
    return diag if kend == ATT_TQ else jnp.concatenate([s[:, :kend - ATT_TQ], diag], axis=1)


def _attn_specs(n_seq, seq_len):
    nq = seq_len // ATT_TQ
    width = ATT_PAIRS * LANES
    per = D_ATTN // width
    q_spec = pl.BlockSpec((ATT_TQ, width), lambda b, h, q: (b * nq + q, h))
    k_spec = pl.BlockSpec((seq_len, width), lambda b, h, q: (b, per + h))
    v_spec = pl.BlockSpec((seq_len, width), lambda b, h, q: (b, 2 * per + h))
    cq_spec = pl.BlockSpec((1, 2 * ATT_PAIRS, ATT_TQ, 1), lambda b, h, q: (b, h, q, 0))
    ck_spec = pl.BlockSpec((1, 2 * ATT_PAIRS, 1, seq_len), lambda b, h, q: (b, h, 0, 0))
    return nq, q_spec, k_spec, v_spec, cq_spec, ck_spec


def _own_cum(cum_ref, head):
    lane = lax.broadcasted_iota(jnp.int32, (1, LANES), 1)
    mine = lane == 2 * ATT_PAIRS * pl.program_id(1) + head
    return jnp.sum(jnp.where(mine, cum_ref[...], 0.0), axis=1, keepdims=True)


def _head_selectors():
    head0 = lax.broadcasted_iota(jnp.int32, (1, LANES), 1) < HEAD_DIM
    return head0, (head0, jnp.logical_not(head0))


def _for_key_range(qi, seq_len, run):
    per = ATT_KSTEP // ATT_TQ
    for g in range(seq_len // ATT_KSTEP):
        pl.when(qi // per == g)(functools.partial(run, (g + 1) * ATT_KSTEP))


def _attn_fwd(qkv, cum, ck, n_seq):
    n = qkv.shape[0]
    seq_len = n // n_seq
    nq, q_spec, k_spec, v_spec, cq_spec, ck_spec = _attn_specs(n_seq, seq_len)
    cum_spec = pl.BlockSpec((ATT_TQ, LANES), lambda b, h, q: (b * nq + q, 0))

    def body(q_ref, k_ref, v_ref, cum_ref, ck_ref, o_ref, lse_ref):
        qi = pl.program_id(2)
        head0, sels = _head_selectors()

        def run(kend):
            for pair in range(ATT_PAIRS):
                win = slice(pair * LANES, (pair + 1) * LANES)
                q2 = q_ref[:, win]
                kb = k_ref[0:kend, win]
                vb = v_ref[0:kend, win]
                outs = []
                for e in range(2):
                    head = 2 * pair + e
                    qe = jnp.where(sels[e], q2 * ATT_SCALE, 0.0).astype(BF16)
                    s = _scores(qe, kb, None, ck_ref[0, head, :, 0:kend], kend)
                    mx = jnp.max(s, axis=1, keepdims=True)
                    p = jnp.exp(s - mx)
                    den = jnp.sum(p, axis=1, keepdims=True)
                    outs.append(_dot(p.astype(BF16), vb) / den)
                    lse_ref[0, head] = _own_cum(cum_ref, head) + mx + jnp.log(den)
                o_ref[:, win] = jnp.where(head0, outs[0], outs[1])

        _for_key_range(qi, seq_len, run)

    return pl.pallas_call(
        body, name="attn_fwd", grid=(n_seq, N_HEADS // 2 // ATT_PAIRS, nq),
        in_specs=[q_spec, k_spec, v_spec, cum_spec, ck_spec],
        out_specs=(q_spec, cq_spec),
        out_shape=(jax.ShapeDtypeStruct((n, D_ATTN), F32), jax.ShapeDtypeStruct((n_seq, N_HEADS, seq_len, 1), F32)),
        compiler_params=_cparams("parallel", "parallel", "parallel"),
    )(qkv, qkv, qkv, cum, ck)


def _attn_bwd(qkv, cum, ck, o, do, lse, n_seq):
    n = qkv.shape[0]
    seq_len = n // n_seq
    nq, q_spec, k_spec, v_spec, cq_spec, ck_spec = _attn_specs(n_seq, seq_len)
    kv_out = pl.BlockSpec((seq_len, ATT_PAIRS * LANES), lambda b, h, q: (b, h))
    cum_spec = pl.BlockSpec((ATT_TQ, LANES), lambda b, h, q: (b * nq + q, 0))

    def body(q_ref, k_ref, v_ref, cum_ref, ck_ref, o_ref, do_ref, lse_ref, dq_ref, dk_ref, dv_ref, dcq_ref, dck_ref):
        qi = pl.program_id(2)

        @pl.when(qi == 0)
        def _():
            dk_ref[...] = jnp.zeros_like(dk_ref)
            dv_ref[...] = jnp.zeros_like(dv_ref)
            dck_ref[...] = jnp.zeros_like(dck_ref)
        head0, sels = _head_selectors()

        def run(kend):
            for pair in range(ATT_PAIRS):
                win = slice(pair * LANES, (pair + 1) * LANES)
                q2 = q_ref[:, win]
                do2 = do_ref[:, win]
                o2 = o_ref[:, win]
                kb = k_ref[0:kend, win]
                vb = v_ref[0:kend, win]
                dqs = []
                dk = jnp.zeros((kend, LANES), F32)
                dv = jnp.zeros((kend, LANES), F32)
                for e in range(2):
                    head = 2 * pair + e
                    qe = jnp.where(sels[e], q2 * ATT_SCALE, 0.0).astype(BF16)
                    doe = jnp.where(sels[e], do2, 0.0)
                    doe_b = doe.astype(BF16)
                    delta = jnp.sum(doe * o2, axis=1, keepdims=True)
                    bias = _own_cum(cum_ref, head) - lse_ref[0, head]
                    p = jnp.exp(_scores(qe, kb, bias, ck_ref[0, head, :, 0:kend], kend))
                    ds = p * (_dot_nt(doe_b, vb) - delta)
                    ds_b = ds.astype(BF16)
                    dqs.append(_dot(ds_b, kb))
                    dk = dk + _dot_tn(ds_b, qe)
                    dv = dv + _dot_tn(p.astype(BF16), doe_b)
                    dcq_ref[0, head] = jnp.sum(ds, axis=1, keepdims=True)
                    dck_ref[0, head, :, 0:kend] -= jnp.sum(ds, axis=0, keepdims=True)
                dk_ref[0:kend, win] += dk
                dv_ref[0:kend, win] += dv
                dq_ref[:, win] = jnp.where(head0, dqs[0], dqs[1]) * ATT_SCALE

        _for_key_range(qi, seq_len, run)

    return pl.pallas_call(
        body, name="attn_bwd", grid=(n_seq, N_HEADS // 2 // ATT_PAIRS, nq),
        in_specs=[q_spec, k_spec, v_spec, cum_spec, ck_spec, q_spec, q_spec, cq_spec],
        out_specs=(q_spec, kv_out, kv_out, cq_spec, ck_spec),
        out_shape=(jax.ShapeDtypeStruct((n, D_ATTN), F32), jax.ShapeDtypeStruct((n, D_ATTN), F32),
                   jax.ShapeDtypeStruct((n, D_ATTN), F32),
                   jax.ShapeDtypeStruct((n_seq, N_HEADS, seq_len, 1), F32),
                   jax.ShapeDtypeStruct((n_seq, N_HEADS, 1, seq_len), F32)),
        compiler_params=_cparams("parallel", "parallel", "arbitrary"),
    )(qkv, qkv, qkv, cum, ck, o, do, lse)


WEIGHT_NAMES = ("norm_mix", "w_in", "b_forget", "lam_re", "lam_im", "b_re", "b_im", "c_re", "c_im", "d_skip", "log_dt",
                "w_glu", "b_glu", "q_norm", "k_norm", "norm_out_ssm", "norm_out_attn", "w_out", "norm_ffn", "w_up",
                "conv_w", "conv_b", "w_down")
SHARDED = ("w_in", "w_glu", "w_out", "w_up", "conv_w", "w_down")
ADAM_TILE = {"w_in": (257, 256), "w_glu": (64, 512), "w_out": (128, 1024), "w_up": (688, 256), "conv_w": (3, 688),
             "w_down": (344, 1024)}
PACK_ROWS = SUBLANES * LANES


def _after_all(*arrays):
    return sum(a[(0,) * a.ndim].astype(F32) for a in arrays).reshape(1, 1)


def _pad_to(a, axis, size):
    pad = [(0, 0)] * a.ndim
    pad[axis] = (0, size - a.shape[axis])
    return jnp.pad(a, pad)


def _block_diag(t, transpose):
    t4 = t.reshape(S5_BLOCKS, 8, SSM_GROUP, SSM_STATE)
    eye = jnp.eye(8, dtype=t.dtype)
    if transpose:
        e = jnp.swapaxes(t4, 2, 3)[:, :, :, None, :] * eye[None, :, None, :, None]
        return e.reshape(S5_BLOCKS, S5_STATES, LANES)
    e = t4[:, :, :, None, :] * eye[None, :, None, :, None]
    return e.reshape(S5_BLOCKS, LANES, S5_STATES)


def _block_diag_extract(m, transpose):
    if transpose:
        m5 = m.reshape(S5_BLOCKS, 8, SSM_STATE, 8, SSM_GROUP)
        d = jnp.stack([m5[:, i, :, i, :] for i in range(8)], axis=1)
        return jnp.swapaxes(d, 2, 3).reshape(N_GROUPS, SSM_GROUP, SSM_STATE)
    m5 = m.reshape(S5_BLOCKS, 8, SSM_GROUP, 8, SSM_STATE)
    d = jnp.stack([m5[:, i, :, i, :] for i in range(8)], axis=1)
    return d.reshape(N_GROUPS, SSM_GROUP, SSM_STATE)


def _pack(pieces):
    flat = jnp.concatenate([p.reshape(-1).astype(F32) for p in pieces])
    size = -(-flat.shape[0] // PACK_ROWS) * PACK_ROWS
    return _pad_to(flat, 0, size).reshape(-1, LANES)


def _unpack(packed, shapes):
    flat = packed.reshape(-1)
    out, off = [], 0
    for shp in shapes:
        size = math.prod(shp)
        out.append(flat[off:off + size].reshape(shp))
        off += size
    return out


def kernel(x, norm_mix, w_in, b_forget, lam_re, lam_im, b_re, b_im, c_re, c_im, d_skip, log_dt, w_glu, b_glu, q_norm, k_norm, norm_out_ssm, norm_out_attn, w_out, norm_ffn, w_up, conv_w, conv_b, w_down, loss_target, m_norm_mix, m_w_in, m_b_forget, m_lam_re, m_lam_im, m_b_re, m_b_im, m_c_re, m_c_im, m_d_skip, m_log_dt, m_w_glu, m_b_glu, m_q_norm, m_k_norm, m_norm_out_ssm, m_norm_out_attn, m_w_out, m_norm_ffn, m_w_up, m_conv_w, m_conv_b, m_w_down, v_norm_mix, v_w_in, v_b_forget, v_lam_re, v_lam_im, v_b_re, v_b_im, v_c_re, v_c_im, v_d_skip, v_log_dt, v_w_glu, v_b_glu, v_q_norm, v_k_norm, v_norm_out_ssm, v_norm_out_attn, v_w_out, v_norm_ffn, v_w_up, v_conv_w, v_conv_b, v_w_down):
    given = dict(locals())
    weights = {k: given[k] for k in WEIGHT_NAMES}
    mom1 = {k: given["m_" + k] for k in WEIGHT_NAMES}
    mom2 = {k: given["v_" + k] for k in WEIGHT_NAMES}
    n_seq, seq_len, _ = x.shape
    n = n_seq * seq_len
    xf = x.reshape(n, D_MODEL)
    target = loss_target.reshape(n, D_MODEL)
    me_idx = 4 * lax.axis_index("x") + 2 * lax.axis_index("y") + lax.axis_index("c")

    in_flags = [False] * 2
    in_sems = _exchange_start([jnp.swapaxes(w_in[0], 0, 1).astype(BF16), conv_w[0]], in_flags, norm_mix,
                              "gather_in_start", 3, NEAR_PEERS)
    fill_own = lambda got, mine: lax.dynamic_update_index_in_dim(got, mine, me_idx, 0)

    lr3 = lam_re[0].reshape(N_GROUPS, 1, SSM_STATE)
    li3 = lam_im[0].reshape(N_GROUPS, 1, SSM_STATE)
    ldt3 = log_dt[0].reshape(N_GROUPS, 1, 1)
    br_t = jnp.swapaxes(b_re[0], 1, 2)
    bi_t = jnp.swapaxes(b_im[0], 1, 2)
    ab_re, ab_im, bb_re, bb_im = _s5_param_fwd(lr3, li3, ldt3, br_t, bi_t)
    a_re = ab_re.reshape(S5_BLOCKS, 1, S5_STATES)
    a_im = ab_im.reshape(S5_BLOCKS, 1, S5_STATES)
    bbr = _block_diag(bb_re, False).astype(BF16)
    bbi = _block_diag(bb_im, False).astype(BF16)
    cr = _block_diag(c_re[0], True).astype(BF16)
    ci = _block_diag(c_im[0], True).astype(BF16)
    avg = jnp.kron(jnp.eye(N_HEADS, dtype=F32), jnp.full((HEAD_DIM, HEAD_DIM), 1.0 / HEAD_DIM, F32)).astype(BF16)
    qg = jnp.tile(q_norm, (1, N_HEADS))
    kg = jnp.tile(k_norm, (1, N_HEADS))
    row_shards = [w_down[0].astype(BF16), w_out[0].astype(BF16), w_glu[0].astype(BF16)]

    (own_in, own_cw), near = _exchange_wait(in_sems[0], in_sems[1], in_sems[2], in_sems[3], in_flags,
                                            _after_all(a_re, a_im, bbr, bbi, cr, ci, avg, qg, kg, *row_shards),
                                            "gather_in_wait", NEAR_PEERS)
    relay = _relay_start(list(near), "gather_in_relay_start", 6)
    g_in, g_cw = _relay_wait(relay[0], relay[1], relay[2], relay[3], "gather_in_relay_wait")
    g_in = fill_own(g_in, own_in)
    g_cw = fill_own(g_cw, own_cw)
    row_flags = [False] * 3
    r_sems = _exchange_start(row_shards, row_flags, g_in, "gather_rows_start", 0)
    u_sems = _exchange_start([jnp.swapaxes(w_up[0], 0, 1).astype(BF16)], [False], r_sems[4], "gather_up_start", 5)
    norm_mix = norm_mix + u_sems[4][0, 0]
    w_in_p = _pad_to(g_in.reshape(D_IN, D_MODEL), 0, D_IN_PAD)

    hn, u, qkv, raw, fl = _inproj_fwd(xf, norm_mix, w_in_p, avg, qg, kg)
    yc, st_r, st_i, xs_r, xs_i = _s5_fwd(u, a_re, a_im, bbr, bbi, cr, ci, d_skip, n_seq)
    bf = _pad_to(b_forget, 1, LANES)
    cum = _fprep_fwd(fl, bf, n_seq)
    cum8 = jnp.swapaxes(cum[:, :N_HEADS].reshape(n_seq, seq_len, N_HEADS), 1, 2)
    ck = cum8[:, :, None, :]
    ya, lse = _attn_fwd(qkv, cum, ck, n_seq)
    own_rows, got_rows = _exchange_wait(r_sems[0], r_sems[1], r_sems[2], r_sems[3], row_flags, ya, "gather_rows_wait")
    g_down, g_out, g_glu = [fill_own(g, o) for o, g in zip(own_rows, got_rows)]
    w_glu_f = g_glu.reshape(D_SSM, D_SSM)
    w_out_f = g_out.reshape(D_MODEL, D_MODEL)
    conv_st = _pad_to(jnp.concatenate([g_cw, conv_b.reshape(N_DEV, 1, -1)], axis=1), 1, SUBLANES)
    w_down4 = g_down.reshape(FFN_GROUPS, FFN_GROUP, D_MODEL)
    ys = _glu_fwd(yc, w_glu_f, b_glu)
    h1, hn2, mixed = _mix_fwd(xf, ys, ya, norm_out_ssm, norm_out_attn, w_out_f, norm_ffn)
    (own_up,), (g_up,) = _exchange_wait(u_sems[0], u_sems[1], u_sems[2], u_sems[3], [False], _after_all(h1, hn2),
                                        "gather_up_wait")
    g_up = fill_own(g_up, own_up)
    ug, uv, pg, pv, dy, loss_part = _ffn_fwd(hn2, h1, target, g_up, conv_st, w_down4, seq_len)
    loss_local = 0.5 * jnp.sum(loss_part) / D_MODEL

    dug, duv, act, dhn2, dcg, dcv = _ffn_bwd(dy, ug, uv, pg, pv, g_up, conv_st, w_down4, seq_len)
    dh1, dys, dya, d_gs, d_ga, d_gf = _mix_bwd(dy, dhn2[None], h1, ys, ya, norm_out_ssm, norm_out_attn, w_out_f, norm_ffn)
    dyc, gl_b, dz_b, d_bglu = _glu_bwd(yc, dys, w_glu_f, b_glu)

    gw_glu = _tn_matmul(gl_b, dz_b, "dw_glu", D_SSM, D_SSM, out_dtype=BF16)
    gw_out = _tn_matmul(mixed, dh1, "dw_out", D_MODEL, D_MODEL, out_dtype=BF16)
    gw_up = jnp.concatenate([_tn_grouped(dug, hn2, "dw_up_gate", False, BF16),
                             _tn_grouped(duv, hn2, "dw_up_val", False, BF16)], axis=0)
    gw_down = _tn_grouped(act, dy, "dw_down", False, BF16)
    g_conv = jnp.concatenate([dcg, dcv], axis=0)
    by_cols = lambda g, c: jnp.swapaxes(g.reshape(g.shape[0], N_DEV, c), 0, 1)
    early_flags = [True] * 4
    early_names = ("w_down", "w_out", "w_glu", "w_up")
    g_sems = _exchange_start(
        [gw_down.reshape(N_DEV, -1, D_MODEL), gw_out.reshape(N_DEV, -1, D_MODEL), gw_glu.reshape(N_DEV, -1, D_SSM), gw_up],
        early_flags, dyc, "grad_early_start", 1)
    started = g_sems[4][0, 0]

    du, dbbr, dbbi, dcr, dci, dar, dai, ddk = _s5_bwd(u, dyc, xs_r, xs_i, st_r, st_i, a_re, a_im, bbr, bbi, cr, ci,
                                                      d_skip + started, n_seq)
    partial_early = {
        "ab_re": jnp.sum(dar, axis=1), "ab_im": jnp.sum(dai, axis=1),
        "bb_re": _block_diag_extract(dbbr, False), "bb_im": _block_diag_extract(dbbi, False),
        "c_re": _block_diag_extract(dcr, True), "c_im": _block_diag_extract(dci, True),
        "d_skip": jnp.sum(ddk, axis=1), "b_glu": d_bglu,
        "norm_out_ssm": d_gs, "norm_out_attn": d_ga, "norm_ffn": d_gf, "conv_b": g_conv[:, 3],
    }
    early_keys = tuple(partial_early)
    early_shapes = [partial_early[k].shape for k in early_keys]
    p_sems = _exchange_start([_pack([partial_early[k] for k in early_keys])], [False], du, "small_early_start", 2)
    started = started + p_sems[4][0, 0]

    dqn, dkn, dv, dcq, dck = _attn_bwd(qkv, cum, ck + started, ya, dya, lse, n_seq)
    dcum8 = dcq[:, :, :, 0] + dck.reshape(n_seq, N_HEADS, seq_len)
    dcum = _pad_to(jnp.swapaxes(dcum8, 1, 2).reshape(n, N_HEADS), 1, LANES)
    dfl, dbf = _fprep_bwd(dcum, fl, bf, n_seq)
    dx, dproj, d_gmix, d_qg, d_kg = _inproj_bwd(xf, norm_mix, w_in_p, avg, qg, kg, raw, du, dqn, dkn, dv, dfl, dh1)

    gw_in = _tn_matmul(dproj, hn, "dw_in", D_IN_PAD, D_MODEL, out_rows=D_IN, out_dtype=BF16)
    partial_late = {
        "norm_mix": d_gmix, "b_forget": jnp.sum(dbf, axis=(0, 1))[:N_HEADS],
        "q_norm": jnp.sum(d_qg.reshape(N_HEADS, HEAD_DIM), axis=0),
        "k_norm": jnp.sum(d_kg.reshape(N_HEADS, HEAD_DIM), axis=0), "loss": loss_local.reshape(1),
    }
    late_keys = tuple(partial_late)
    late_shapes = [partial_late[k].shape for k in late_keys]

    late_flags = [True, True, False]
    l_sems = _exchange_start(
        [gw_in.reshape(N_DEV, D_IN // N_DEV, D_MODEL).astype(BF16), g_conv[:, :3],
         _pack([partial_late[k] for k in late_keys])],
        late_flags, dx, "grad_late_start", 4)
    early_src, early_land = _exchange_wait(g_sems[0], g_sems[1], g_sems[2], g_sems[3], early_flags, l_sems[4],
                                           "grad_early_wait")
    land = dict(zip(early_names, early_land))
    land_up = land["w_up"]
    own = {k: lax.dynamic_index_in_dim(s, me_idx, 0, keepdims=False) for k, s in zip(early_names, early_src)}
    grads, deltas, new_m, new_v = {}, {}, {}, {}

    def adam_shard(name):
        flip = (lambda a: jnp.swapaxes(a, 1, 2)) if name in ("w_in", "w_up") else (lambda a: a)
        outs = _adam_sharded(land[name], own[name], flip(weights[name]), flip(mom1[name]), flip(mom2[name]),
                             "adam_" + name, ADAM_TILE[name])
        grads[name], deltas[name], new_m[name], new_v[name] = [flip(o) for o in outs]

    for name in ("w_up", "w_down", "w_out", "w_glu"):
        adam_shard(name)
    (own_pack,), (early_parts,) = _exchange_wait(p_sems[0], p_sems[1], p_sems[2], p_sems[3], [False], land_up,
                                                 "small_early_wait")
    early_sum = _sum_partials(early_parts, own_pack, "sum_early_partials")
    (src_in, src_cw, own_late), (land["w_in"], land["conv_w"], late_parts) = _exchange_wait(
        l_sems[0], l_sems[1], l_sems[2], l_sems[3], late_flags,
        _after_all(early_sum, *[new_v[k] for k in ("w_up", "w_down", "w_out", "w_glu")]), "grad_late_wait")
    own["w_in"] = lax.dynamic_index_in_dim(src_in, me_idx, 0, keepdims=False)
    own["conv_w"] = lax.dynamic_index_in_dim(src_cw, me_idx, 0, keepdims=False)
    for name in ("w_in", "conv_w"):
        adam_shard(name)

    summed = dict(zip(late_keys, _unpack(_sum_partials(late_parts, own_late, "sum_late_partials"), late_shapes)))
    summed.update(zip(early_keys, _unpack(early_sum, early_shapes)))
    dlr, dli, dldt, dbr_t, dbi_t = _s5_param_bwd(
        lr3, li3, ldt3, br_t, bi_t, summed["ab_re"].reshape(lr3.shape), summed["ab_im"].reshape(lr3.shape),
        summed["bb_re"], summed["bb_im"])
    small_grads = {
        "norm_mix": summed["norm_mix"], "b_forget": summed["b_forget"], "lam_re": dlr, "lam_im": dli,
        "b_re": dbr_t, "b_im": dbi_t, "c_re": summed["c_re"], "c_im": summed["c_im"],
        "d_skip": summed["d_skip"], "log_dt": dldt, "b_glu": summed["b_glu"], "q_norm": summed["q_norm"],
        "k_norm": summed["k_norm"], "norm_out_ssm": summed["norm_out_ssm"], "norm_out_attn": summed["norm_out_attn"],
        "norm_ffn": summed["norm_ffn"], "conv_b": summed["conv_b"],
    }
    repl = tuple(k for k in WEIGHT_NAMES if k not in SHARDED)
    turn = lambda k, a: jnp.swapaxes(a, 2, 3) if k in ("b_re", "b_im") else a
    w_list = [turn(k, weights[k]) for k in repl]
    g_list = [small_grads[k].reshape(w.shape) for k, w in zip(repl, w_list)]
    d_list, m_list, v_list = _adam_replicated(g_list, w_list, [turn(k, mom1[k]) for k in repl],
                                              [turn(k, mom2[k]) for k in repl], "adam_replicated")
    for k, g, d, nm, nv in zip(repl, g_list, d_list, m_list, v_list):
        grads[k], deltas[k], new_m[k], new_v[k] = turn(k, g), turn(k, d), turn(k, nm), turn(k, nv)

    grad_x = dx.reshape(x.shape)
    loss = summed["loss"].reshape(())
    return (loss, grad_x, *[grads[k] for k in WEIGHT_NAMES], *[deltas[k] for k in WEIGHT_NAMES],
            *[new_m[k] for k in WEIGHT_NAMES], *[new_v[k] for k in WEIGHT_NAMES])
```

```python
import functools
import math

import jax
import jax.numpy as jnp
from jax import lax
from jax.experimental import pallas as pl
from jax.experimental.pallas import tpu as pltpu

F32 = jnp.float32
BF16 = jnp.bfloat16
HIGHEST = lax.Precision.HIGHEST

N_DEV = 8
D_MODEL = 1024
D_SSM = 512
D_ATTN = 512
N_HEADS = 8
HEAD_DIM = 64
N_GROUPS = 32
SSM_GROUP = 16
SSM_STATE = 64
D_FF = 2752
D_IN = 2056
D_IN_PAD = 2176
EPS = 1e-6
LANES = 128
SUBLANES = 8
VMEM_LIMIT = 56 * 1024 * 1024

ADAM_LR = 0.001
ADAM_B1 = 0.9
ADAM_B2 = 0.999
ADAM_EPS = 1e-08
ADAM_WD = 0.01
ADAM_STEP = 10


def _cparams(*sem):
    return pltpu.CompilerParams(dimension_semantics=sem, vmem_limit_bytes=VMEM_LIMIT)


def _dot(a, b, **kw):
    return jnp.dot(a, b, preferred_element_type=F32, **kw)


def _dot_nt(a, b):
    return lax.dot_general(a, b, (((1,), (1,)), ((), ())), preferred_element_type=F32)


def _dot_tn(a, b):
    return lax.dot_general(a, b, (((0,), (0,)), ((), ())), preferred_element_type=F32)


def _rms(x, g):
    return x * lax.rsqrt(jnp.mean(x * x, axis=-1, keepdims=True) + EPS) * g


def _split_dot(x, avg):
    hi = x.astype(BF16)
    lo = (x - hi.astype(F32)).astype(BF16)
    return _dot(hi, avg) + _dot(lo, avg)


@jax.custom_vjp
def _group_mean(x, avg):
    return _split_dot(x, avg)


def _group_mean_fwd(x, avg):
    return _split_dot(x, avg), avg


def _group_mean_bwd(avg, ct):
    return _split_dot(ct, avg), jnp.zeros_like(avg)


_group_mean.defvjp(_group_mean_fwd, _group_mean_bwd)


def _headnorm(q, avg, g):
    return q * lax.rsqrt(_group_mean(q * q, avg) + EPS) * g


ALL_PEERS = tuple(range(1, N_DEV))
NEAR_PEERS = (1, 2, 4, 6)
RELAYED = (2, 4, 6)


def _peer_list(js=ALL_PEERS):
    x, y, c = lax.axis_index("x"), lax.axis_index("y"), lax.axis_index("c")
    peers = []
    for j in js:
        px = 1 - x if (j >> 2) & 1 else x
        py = 1 - y if (j >> 1) & 1 else y
        pc = 1 - c if j & 1 else c
        peers.append(((px, py, pc), 4 * px + 2 * py + pc))
    return 4 * x + 2 * y + c, peers


def _split_copies(src, land, send_sems, recv_sems, scatter_flags, me, peers, incoming):
    copies = []
    for k in range(len(src)):
        for j, (pid, pidx) in enumerate(peers):
            s = src[k].at[pidx] if scatter_flags[k] else src[k]
            i = k * len(peers) + j
            copies.append(pltpu.make_async_remote_copy(
                src_ref=s, dst_ref=land[k].at[pidx if incoming else me], send_sem=send_sems[i],
                recv_sem=recv_sems[i], device_id=pid, device_id_type=pl.DeviceIdType.MESH))
    return copies


def _handshake(peers):
    barrier = pltpu.get_barrier_semaphore()
    for pid, _ in peers:
        pl.semaphore_signal(barrier, inc=1, device_id=pid, device_id_type=pl.DeviceIdType.MESH)
    pl.semaphore_wait(barrier, len(peers))


def _exchange_start(srcs, scatter_flags, after, name, collective_id, peer_js=ALL_PEERS):
    n = len(srcs)
    ns = n * len(peer_js)
    hbm = pl.BlockSpec(memory_space=pltpu.HBM)
    sem = pl.BlockSpec(memory_space=pltpu.SEMAPHORE)
    land_shapes = [s.shape if sc else (N_DEV,) + s.shape for s, sc in zip(srcs, scatter_flags)]

    def body(*refs):
        src, land = refs[:n], refs[n:2 * n]
        send_sems = refs[2 * n + 1:2 * n + 1 + ns]
        recv_sems = refs[2 * n + 1 + ns:2 * n + 1 + 2 * ns]
        token = refs[4 * n + 1 + 2 * ns]
        me, peers = _peer_list(peer_js)
        _handshake(peers)
        for cp in _split_copies(src, land, send_sems, recv_sems, scatter_flags, me, peers, False):
            cp.start()
        token[...] = jnp.zeros_like(token)

    outs = pl.pallas_call(
        body, name=name,
        out_shape=(*[pltpu.SemaphoreType.DMA(())] * (2 * ns), *[pltpu.HBM(s.shape, s.dtype) for s in srcs],
                   *[pltpu.HBM(shp, s.dtype) for shp, s in zip(land_shapes, srcs)],
                   jax.ShapeDtypeStruct((SUBLANES, LANES), F32)),
        in_specs=[hbm] * (2 * n) + [pl.BlockSpec(memory_space=pl.ANY)],
        out_specs=(*[sem] * (2 * ns), *[hbm] * (2 * n), pl.BlockSpec(memory_space=pltpu.VMEM)),
        input_output_aliases={i: 2 * ns + i for i in range(2 * n)},
        compiler_params=pltpu.CompilerParams(has_side_effects=pltpu.SideEffectType.DATAFLOW_SIDE_EFFECTING,
                                             collective_id=collective_id),
    )(*[pltpu.with_memory_space_constraint(s, pltpu.HBM) for s in srcs],
      *[pltpu.with_memory_space_constraint(lax.empty(shp, s.dtype), pltpu.HBM) for shp, s in zip(land_shapes, srcs)],
      after)
    return (outs[:ns], outs[ns:2 * ns], outs[2 * ns:2 * ns + n], outs[2 * ns + n:2 * ns + 2 * n], outs[2 * ns + 2 * n])


def _exchange_wait(send_sems, recv_sems, srcs, lands, scatter_flags, after, name, peer_js=ALL_PEERS):
    n = len(srcs)
    ns = n * len(peer_js)
    hbm = pl.BlockSpec(memory_space=pltpu.HBM)
    sem = pl.BlockSpec(memory_space=pltpu.SEMAPHORE)

    def body(*refs):
        src, land = refs[:n], refs[n:2 * n]
        s_sems = refs[2 * n:2 * n + ns]
        r_sems = refs[2 * n + ns:2 * n + 2 * ns]
        me, peers = _peer_list(peer_js)
        for cp in _split_copies(src, land, s_sems, r_sems, scatter_flags, me, peers, True):
            cp.wait_send()
            cp.wait_recv()

    outs = pl.pallas_call(
        body, name=name,
        out_shape=tuple(pltpu.HBM(a.shape, a.dtype) for a in (*srcs, *lands)),
        in_specs=[hbm] * (2 * n) + [sem] * (2 * ns) + [pl.BlockSpec(memory_space=pl.ANY)],
        out_specs=tuple([hbm] * (2 * n)),
        input_output_aliases={i: i for i in range(2 * n)},
        compiler_params=pltpu.CompilerParams(has_side_effects=pltpu.SideEffectType.DATAFLOW_SIDE_EFFECTING),
    )(*srcs, *lands, *send_sems, *recv_sems, after)
    return outs[:n], outs[n:]


def _relay_copies(land, send_sems, recv_sems, incoming):
    _, ((sibling, _),) = _peer_list((1,))
    _, heard = _peer_list(RELAYED)
    _, sibling_heard = _peer_list(tuple(j ^ 1 for j in RELAYED))
    copies = []
    for k in range(len(land)):
        for j in range(len(RELAYED)):
            slot = (sibling_heard if incoming else heard)[j][1]
            i = k * len(RELAYED) + j
            copies.append(pltpu.make_async_remote_copy(
                src_ref=land[k].at[slot], dst_ref=land[k].at[slot], send_sem=send_sems[i], recv_sem=recv_sems[i],
                device_id=sibling, device_id_type=pl.DeviceIdType.MESH))
    return copies


def _relay_start(lands, name, collective_id):
    n = len(lands)
    ns = n * len(RELAYED)
    hbm = pl.BlockSpec(memory_space=pltpu.HBM)
    sem = pl.BlockSpec(memory_space=pltpu.SEMAPHORE)

    def body(*refs):
        land = refs[:n]
        send_sems = refs[n:n + ns]
        recv_sems = refs[n + ns:n + 2 * ns]
        token = refs[2 * n + 2 * ns]
        _handshake(_peer_list((1,))[1])
        for cp in _relay_copies(land, send_sems, recv_sems, False):
            cp.start()
        token[...] = jnp.zeros_like(token)

    outs = pl.pallas_call(
        body, name=name,
        out_shape=(*[pltpu.SemaphoreType.DMA(())] * (2 * ns), *[pltpu.HBM(a.shape, a.dtype) for a in lands],
                   jax.ShapeDtypeStruct((SUBLANES, LANES), F32)),
        in_specs=[hbm] * n,
        out_specs=(*[sem] * (2 * ns), *[hbm] * n, pl.BlockSpec(memory_space=pltpu.VMEM)),
        input_output_aliases={i: 2 * ns + i for i in range(n)},
        compiler_params=pltpu.CompilerParams(has_side_effects=pltpu.SideEffectType.DATAFLOW_SIDE_EFFECTING,
                                             collective_id=collective_id),
    )(*lands)
    return outs[:ns], outs[ns:2 * ns], outs[2 * ns:2 * ns + n], outs[2 * ns + n]


def _relay_wait(send_sems, recv_sems, lands, after, name):
    n = len(lands)
    ns = n * len(RELAYED)
    hbm = pl.BlockSpec(memory_space=pltpu.HBM)
    sem = pl.BlockSpec(memory_space=pltpu.SEMAPHORE)

    def body(*refs):
        land = refs[:n]
        for cp in _relay_copies(land, refs[n:n + ns], refs[n + ns:n + 2 * ns], True):
            cp.wait_send()
            cp.wait_recv()

    return pl.pallas_call(
        body, name=name,
        out_shape=tuple(pltpu.HBM(a.shape, a.dtype) for a in lands),
        in_specs=[hbm] * n + [sem] * (2 * ns) + [pl.BlockSpec(memory_space=pl.ANY)],
        out_specs=tuple([hbm] * n),
        input_output_aliases={i: i for i in range(n)},
        compiler_params=pltpu.CompilerParams(has_side_effects=pltpu.SideEffectType.DATAFLOW_SIDE_EFFECTING),
    )(*lands, *send_sems, *recv_sems, after)


def _tn_matmul(a, b, name, tk, tm, out_rows=None, out_cols=None, out_dtype=F32, tn=512):
    n_tok, k_dim = a.shape
    m_dim = b.shape[1]
    grid = (k_dim // tk, m_dim // tm, n_tok // tn)

    def body(a_ref, b_ref, o_ref, acc):
        k = pl.program_id(2)
        part = _dot_tn(a_ref[...].astype(BF16), b_ref[...].astype(BF16))

        @pl.when(k == 0)
        def _():
            acc[...] = part

        @pl.when(k > 0)
        def _():
            acc[...] += part

        @pl.when(k == grid[2] - 1)
        def _():
            o_ref[...] = acc[...].astype(out_dtype)

    return pl.pallas_call(
        body, name=name, grid=grid,
        in_specs=[pl.BlockSpec((tn, tk), lambda i, j, k: (k, i)), pl.BlockSpec((tn, tm), lambda i, j, k: (k, j))],
        out_specs=pl.BlockSpec((tk, tm), lambda i, j, k: (i, j)),
        out_shape=jax.ShapeDtypeStruct((out_rows or k_dim, out_cols or m_dim), out_dtype),
        scratch_shapes=[pltpu.VMEM((tk, tm), F32)],
        compiler_params=_cparams("parallel", "parallel", "arbitrary"),
    )(a, b)


def _adam_math(g, w, m, v):
    m = ADAM_B1 * m + (1.0 - ADAM_B1) * g
    v = ADAM_B2 * v + (1.0 - ADAM_B2) * (g * g)
    m_hat = m / (1.0 - ADAM_B1 ** ADAM_STEP)
    v_hat = v / (1.0 - ADAM_B2 ** ADAM_STEP)
    delta = -ADAM_LR * (m_hat / (jnp.sqrt(v_hat) + ADAM_EPS) + ADAM_WD * w)
    return delta, m, v


def _adam_sharded(land, own, w, m, v, name, tile):
    _, r, c = w.shape

    def body(*refs):
        l_ref = refs[0]
        own_ref = refs[1] if own is not None else None
        w_ref, m_ref, v_ref, g_ref, d_ref, nm_ref, nv_ref = [ref.at[0] for ref in refs[-7:]]
        if own_ref is not None:
            x, y, z = lax.axis_index("x"), lax.axis_index("y"), lax.axis_index("c")
            me = 4 * x + 2 * y + z
            mine = own_ref[...].astype(F32)
        g = None
        for s in range(N_DEV):
            part = l_ref[s].astype(F32)
            if own_ref is not None:
                part = jnp.where(me == s, mine, part)
            g = part if g is None else g + part
        d, nm, nv = _adam_math(g, w_ref[...], m_ref[...], v_ref[...])
        g_ref[...] = g
        d_ref[...] = d
        nm_ref[...] = nm
        nv_ref[...] = nv

    tr, tc = tile
    spec = pl.BlockSpec((1, tr, tc), lambda i, j: (0, i, j))
    own_specs, own_args = ([pl.BlockSpec((tr, tc), lambda i, j: (i, j))], [own]) if own is not None else ([], [])
    return pl.pallas_call(
        body, name=name, grid=(r // tr, c // tc),
        in_specs=[pl.BlockSpec((N_DEV, tr, tc), lambda i, j: (0, i, j)), *own_specs, spec, spec, spec],
        out_specs=(spec, spec, spec, spec),
        out_shape=tuple(jax.ShapeDtypeStruct((1, r, c), F32) for _ in range(4)),
        compiler_params=_cparams("parallel", "parallel"),
    )(land, *own_args, w, m, v)


def _sum_partials(parts, own, name):
    _, r, c = parts.shape

    def body(*refs):
        p_ref, o_ref = refs[0], refs[-1]
        if own is not None:
            x, y, z = lax.axis_index("x"), lax.axis_index("y"), lax.axis_index("c")
            me = 4 * x + 2 * y + z
            mine = refs[1][...]
        g = None
        for s in range(N_DEV):
            part = p_ref[s]
            if own is not None:
                part = jnp.where(me == s, mine, part)
            g = part if g is None else g + part
        o_ref[...] = g

    args = (parts,) if own is None else (parts, own)
    return pl.pallas_call(body, name=name, out_shape=jax.ShapeDtypeStruct((r, c), F32),
                          compiler_params=pltpu.CompilerParams(vmem_limit_bytes=VMEM_LIMIT))(*args)


def _adam_replicated(gs, ws, ms, vs, name):
    k = len(ws)

    def body(*refs):
        outs = refs[4 * k:]
        for i in range(k):
            d, nm, nv = _adam_math(refs[i][...], refs[k + i][...], refs[2 * k + i][...], refs[3 * k + i][...])
            outs[i][...] = d
            outs[k + i][...] = nm
            outs[2 * k + i][...] = nv

    outs = pl.pallas_call(body, name=name, out_shape=tuple(jax.ShapeDtypeStruct(w.shape, F32) for w in ws) * 3,
                          compiler_params=pltpu.CompilerParams(vmem_limit_bytes=VMEM_LIMIT))(*gs, *ws, *ms, *vs)
    return outs[:k], outs[k:2 * k], outs[2 * k:]


def _inproj_fwd(x, g, w_in, avg, qg, kg, tm=512):
    n = x.shape[0]

    def body(x_ref, g_ref, w_ref, a_ref, qg_ref, kg_ref, hn_ref, u_ref, qkv_ref, raw_ref, fl_ref):
        hn = _rms(x_ref[...], g_ref[...]).astype(BF16)
        hn_ref[...] = hn
        proj = _dot_nt(hn, w_ref[...])
        u_ref[...] = proj[:, 0:512]
        q = proj[:, 512:1024]
        k = proj[:, 1024:1536]
        raw_ref[:, 0:512] = q
        raw_ref[:, 512:1024] = k
        qkv_ref[:, 0:512] = _headnorm(q, a_ref[...], qg_ref[...]).astype(BF16)
        qkv_ref[:, 512:1024] = _headnorm(k, a_ref[...], kg_ref[...]).astype(BF16)
        qkv_ref[:, 1024:1536] = proj[:, 1536:2048].astype(BF16)
        fl_ref[...] = proj[:, 2048:D_IN_PAD]

    row = lambda w: pl.BlockSpec((tm, w), lambda i: (i, 0))
    full = lambda a: pl.BlockSpec(a.shape, lambda i: (0,) * a.ndim)
    return pl.pallas_call(
        body, name="inproj_fwd", grid=(n // tm,),
        in_specs=[row(D_MODEL), full(g), full(w_in), full(avg), full(qg), full(kg)],
        out_specs=(row(D_MODEL), row(512), row(1536), row(1024), row(LANES)),
        out_shape=(jax.ShapeDtypeStruct((n, D_MODEL), BF16), jax.ShapeDtypeStruct((n, 512), F32),
                   jax.ShapeDtypeStruct((n, 1536), BF16), jax.ShapeDtypeStruct((n, 1024), F32),
                   jax.ShapeDtypeStruct((n, LANES), F32)),
        compiler_params=_cparams("parallel"),
    )(x, g, w_in, avg, qg, kg)


def _inproj_bwd(x, g, w_in, avg, qg, kg, raw, du, dqn, dkn, dv, dfl, dres, tm=512):
    n = x.shape[0]

    def body(x_ref, g_ref, w_ref, a_ref, qg_ref, kg_ref, raw_ref, du_ref, dqn_ref, dkn_ref, dv_ref, dfl_ref, dres_ref,
             dx_ref, dproj_ref, dg_ref, dqg_ref, dkg_ref):
        @pl.when(pl.program_id(0) == 0)
        def _():
            dg_ref[...] = jnp.zeros_like(dg_ref)
            dqg_ref[...] = jnp.zeros_like(dqg_ref)
            dkg_ref[...] = jnp.zeros_like(dkg_ref)
        avg_m = a_ref[...]
        _, vjp_q = jax.vjp(lambda q, gg: _headnorm(q, avg_m, gg), raw_ref[:, 0:512], qg_ref[...])
        dq, dqg = vjp_q(dqn_ref[...])
        _, vjp_k = jax.vjp(lambda k, gg: _headnorm(k, avg_m, gg), raw_ref[:, 512:1024], kg_ref[...])
        dk, dkg = vjp_k(dkn_ref[...])
        dproj = jnp.concatenate([du_ref[...], dq, dk, dv_ref[...], dfl_ref[...]], axis=1).astype(BF16)
        dproj_ref[...] = dproj
        dhn = _dot(dproj, w_ref[...])
        _, vjp_x = jax.vjp(_rms, x_ref[...], g_ref[...])
        dxn, dg = vjp_x(dhn)
        dx_ref[...] = dxn + dres_ref[...]
        dg_ref[...] += dg
        dqg_ref[...] += dqg
        dkg_ref[...] += dkg

    row = lambda w: pl.BlockSpec((tm, w), lambda i: (i, 0))
    full = lambda a: pl.BlockSpec(a.shape, lambda i: (0,) * a.ndim)
    vec = lambda w: pl.BlockSpec((1, w), lambda i: (0, 0))
    return pl.pallas_call(
        body, name="inproj_bwd", grid=(n // tm,),
        in_specs=[row(D_MODEL), full(g), full(w_in), full(avg), full(qg), full(kg), row(1024), row(512), row(512),
                  row(512), row(512), row(LANES), row(D_MODEL)],
        out_specs=(row(D_MODEL), row(D_IN_PAD), vec(D_MODEL), vec(512), vec(512)),
        out_shape=(jax.ShapeDtypeStruct((n, D_MODEL), F32), jax.ShapeDtypeStruct((n, D_IN_PAD), BF16),
                   jax.ShapeDtypeStruct((1, D_MODEL), F32), jax.ShapeDtypeStruct((1, 512), F32),
                   jax.ShapeDtypeStruct((1, 512), F32)),
        compiler_params=_cparams("arbitrary"),
    )(x, g, w_in, avg, qg, kg, raw, du, dqn, dkn, dv, dfl, dres)


def _glu_fwd(yc, wg, bg, tm=512):
    n = yc.shape[0]

    def body(yc_ref, w_ref, b_ref, ys_ref):
        gl = jax.nn.gelu(yc_ref[...])
        z = _dot(gl.astype(BF16), w_ref[...]) + b_ref[...]
        ys_ref[...] = gl * jax.nn.sigmoid(z)

    row = pl.BlockSpec((tm, 512), lambda i: (i, 0))
    full = lambda a: pl.BlockSpec(a.shape, lambda i: (0,) * a.ndim)
    return pl.pallas_call(
        body, name="glu_fwd", grid=(n // tm,), in_specs=[row, full(wg), full(bg)], out_specs=row,
        out_shape=jax.ShapeDtypeStruct((n, 512), F32), compiler_params=_cparams("parallel"),
    )(yc, wg, bg)


def _glu_bwd(yc, dys, wg, bg, tm=512):
    n = yc.shape[0]

    def body(yc_ref, dys_ref, w_ref, b_ref, dyc_ref, gl_ref, dz_ref, db_ref):
        @pl.when(pl.program_id(0) == 0)
        def _():
            db_ref[...] = jnp.zeros_like(db_ref)
        gl, vjp_gelu = jax.vjp(jax.nn.gelu, yc_ref[...])
        glb = gl.astype(BF16)
        z = _dot(glb, w_ref[...]) + b_ref[...]
        s = jax.nn.sigmoid(z)
        dys = dys_ref[...]
        dz = dys * gl * s * (1.0 - s)
        dzb = dz.astype(BF16)
        dgl = dys * s + _dot_nt(dzb, w_ref[...])
        dyc_ref[...] = vjp_gelu(dgl)[0]
        gl_ref[...] = glb
        dz_ref[...] = dzb
        db_ref[...] += jnp.sum(dz, axis=0, keepdims=True)

    row = pl.BlockSpec((tm, 512), lambda i: (i, 0))
    full = lambda a: pl.BlockSpec(a.shape, lambda i: (0,) * a.ndim)
    return pl.pallas_call(
        body, name="glu_bwd", grid=(n // tm,), in_specs=[row, row, full(wg), full(bg)],
        out_specs=(row, row, row, pl.BlockSpec((1, 512), lambda i: (0, 0))),
        out_shape=(jax.ShapeDtypeStruct((n, 512), F32), jax.ShapeDtypeStruct((n, 512), BF16),
                   jax.ShapeDtypeStruct((n, 512), BF16), jax.ShapeDtypeStruct((1, 512), F32)),
        compiler_params=_cparams("arbitrary"),
    )(yc, dys, wg, bg)


def _mix_fwd(x, ys, ya, gs, ga, wout, gf, tm=512):
    n = x.shape[0]

    def body(x_ref, ys_ref, ya_ref, gs_ref, ga_ref, w_ref, gf_ref, h1_ref, hn2_ref, mixed_ref):
        mixed = jnp.concatenate([_rms(ys_ref[...], gs_ref[...]), _rms(ya_ref[...], ga_ref[...])], axis=1).astype(BF16)
        mixed_ref[...] = mixed
        h1 = x_ref[...] + _dot(mixed, w_ref[...])
        h1_ref[...] = h1
        hn2_ref[...] = _rms(h1, gf_ref[...]).astype(BF16)

    row = lambda w: pl.BlockSpec((tm, w), lambda i: (i, 0))
    full = lambda a: pl.BlockSpec(a.shape, lambda i: (0,) * a.ndim)
    return pl.pallas_call(
        body, name="mix_fwd", grid=(n // tm,),
        in_specs=[row(D_MODEL), row(512), row(512), full(gs), full(ga), full(wout), full(gf)],
        out_specs=(row(D_MODEL), row(D_MODEL), row(D_MODEL)),
        out_shape=(jax.ShapeDtypeStruct((n, D_MODEL), F32), jax.ShapeDtypeStruct((n, D_MODEL), BF16),
                   jax.ShapeDtypeStruct((n, D_MODEL), BF16)),
        compiler_params=_cparams("parallel"),
    )(x, ys, ya, gs, ga, wout, gf)


def _mix_bwd(dy, dhn2_parts, h1, ys, ya, gs, ga, wout, gf, tm=512):
    n = dy.shape[0]
    n_parts = dhn2_parts.shape[0]

    def body(dy_ref, dp_ref, h1_ref, ys_ref, ya_ref, gs_ref, ga_ref, w_ref, gf_ref,
             dh1_ref, dys_ref, dya_ref, dgs_ref, dga_ref, dgf_ref):
        @pl.when(pl.program_id(0) == 0)
        def _():
            dgs_ref[...] = jnp.zeros_like(dgs_ref)
            dga_ref[...] = jnp.zeros_like(dga_ref)
            dgf_ref[...] = jnp.zeros_like(dgf_ref)
        dhn2 = dp_ref[0]
        for p in range(1, n_parts):
            dhn2 = dhn2 + dp_ref[p]
        _, vjp_f = jax.vjp(_rms, h1_ref[...], gf_ref[...])
        dh1n, dgf = vjp_f(dhn2)
        dh1 = dy_ref[...] + dh1n
        dh1_ref[...] = dh1
        dmixed = _dot_nt(dh1.astype(BF16), w_ref[...])
        _, vjp_s = jax.vjp(_rms, ys_ref[...], gs_ref[...])
        dys, dgs = vjp_s(dmixed[:, 0:512])
        _, vjp_a = jax.vjp(_rms, ya_ref[...], ga_ref[...])
        dya, dga = vjp_a(dmixed[:, 512:1024])
        dys_ref[...] = dys
        dya_ref[...] = dya
        dgs_ref[...] += dgs
        dga_ref[...] += dga
        dgf_ref[...] += dgf

    row = lambda w: pl.BlockSpec((tm, w), lambda i: (i, 0))
    full = lambda a: pl.BlockSpec(a.shape, lambda i: (0,) * a.ndim)
    vec = lambda w: pl.BlockSpec((1, w), lambda i: (0, 0))
    return pl.pallas_call(
        body, name="mix_bwd", grid=(n // tm,),
        in_specs=[row(D_MODEL), pl.BlockSpec((n_parts, tm, D_MODEL), lambda i: (0, i, 0)), row(D_MODEL), row(512),
                  row(512), full(gs), full(ga), full(wout), full(gf)],
        out_specs=(row(D_MODEL), row(512), row(512), vec(512), vec(512), vec(D_MODEL)),
        out_shape=(jax.ShapeDtypeStruct((n, D_MODEL), F32), jax.ShapeDtypeStruct((n, 512), F32),
                   jax.ShapeDtypeStruct((n, 512), F32), jax.ShapeDtypeStruct((1, 512), F32),
                   jax.ShapeDtypeStruct((1, 512), F32), jax.ShapeDtypeStruct((1, D_MODEL), F32)),
        compiler_params=_cparams("arbitrary"),
    )(dy, dhn2_parts, h1, ys, ya, gs, ga, wout, gf)


HALO = 16
FFN_GROUPS = 4
FFN_GROUP = D_FF // FFN_GROUPS


def _conv3(ue, cw):
    return cw[2:3] * ue + cw[1:2] * pltpu.roll(ue, 1, 0) + cw[0:1] * pltpu.roll(ue, 2, 0) + cw[3:4]


def _ffn_weight_specs():
    gate = lambda i, j: (j, 0, 0)
    val = lambda i, j: (j + FFN_GROUPS, 0, 0)
    w_blk, c_blk = (1, FFN_GROUP, D_MODEL), (1, SUBLANES, FFN_GROUP)
    return [pl.BlockSpec(w_blk, gate), pl.BlockSpec(w_blk, val), pl.BlockSpec(c_blk, gate), pl.BlockSpec(c_blk, val),
            pl.BlockSpec((1, FFN_GROUP, D_MODEL), gate)]


def _ffn_fwd(hn2, h1, target, w_up, conv, w_down, seq_len, tm=512):
    n = hn2.shape[0]
    nj = FFN_GROUPS
    hb = tm // HALO

    def body(hn_ref, halo_ref, h1_ref, tgt_ref, wg_ref, wv_ref, cg_ref, cv_ref, wd_ref,
             ug_ref, uv_ref, pg_ref, pv_ref, dy_ref, loss_ref, acc):
        i, j = pl.program_id(0), pl.program_id(1)
        seq_start = (i * tm) % seq_len == 0
        halo = halo_ref[...]
        halo = jnp.where(seq_start, jnp.zeros_like(halo), halo)
        he = jnp.concatenate([halo, hn_ref[...]], axis=0)
        ueg = _dot_nt(he, wg_ref[0])
        uev = _dot_nt(he, wv_ref[0])
        ug_ref[0] = ueg[HALO:].astype(BF16)
        uv_ref[0] = uev[HALO:].astype(BF16)
        cg = _conv3(ueg, cg_ref[0])[HALO:]
        cv = _conv3(uev, cv_ref[0])[HALO:]
        pg_ref[0] = cg.astype(BF16)
        pv_ref[0] = cv.astype(BF16)
        act = (jax.nn.silu(cg) * cv).astype(BF16)
        part = _dot(act, wd_ref[0])

        @pl.when(j == 0)
        def _():
            acc[...] = part

        @pl.when(j > 0)
        def _():
            acc[...] += part

        @pl.when(j == nj - 1)
        def _():
            err = h1_ref[...] + acc[...] - tgt_ref[...]
            dy_ref[...] = err * (1.0 / D_MODEL)
            loss_ref[0] = jnp.sum(err * err, axis=0, keepdims=True)

    row = pl.BlockSpec((tm, D_MODEL), lambda i, j: (i, 0))
    u_main = pl.BlockSpec((1, tm, FFN_GROUP), lambda i, j: (j, i, 0))
    u_shape = jax.ShapeDtypeStruct((FFN_GROUPS, n, FFN_GROUP), BF16)
    return pl.pallas_call(
        body, name="ffn_fwd", grid=(n // tm, nj),
        in_specs=[row, pl.BlockSpec((HALO, D_MODEL), lambda i, j: (jnp.maximum(i * hb - 1, 0), 0)), row, row,
                  *_ffn_weight_specs()],
        out_specs=(u_main, u_main, u_main, u_main, row, pl.BlockSpec((1, 1, D_MODEL), lambda i, j: (i, 0, 0))),
        out_shape=(u_shape, u_shape, u_shape, u_shape, jax.ShapeDtypeStruct((n, D_MODEL), F32),
                   jax.ShapeDtypeStruct((n // tm, 1, D_MODEL), F32)),
        scratch_shapes=[pltpu.VMEM((tm, D_MODEL), F32)],
        compiler_params=_cparams("parallel", "arbitrary"),
    )(hn2, hn2, h1, target, w_up, w_up, conv, conv, w_down)


def _ffn_bwd(dy, ug, uv, pg, pv, w_up, conv, w_down, seq_len, tm=512):
    n = dy.shape[0]
    nj = FFN_GROUPS
    fb = FFN_GROUP
    hb = tm // HALO
    last_hb = n // HALO - 1
    rows = tm + HALO

    def body(dy_ref, dyn_ref, ug_ref, uv_ref, pgm_ref, pgn_ref, pvm_ref, pvn_ref, wg_ref, wv_ref, cg_ref, cv_ref,
             wd_ref, dug_ref, duv_ref, act_ref, dhn_ref, dcg_ref, dcv_ref, acc):
        i, j = pl.program_id(0), pl.program_id(1)
        seq_end = ((i + 1) * tm) % seq_len == 0
        dyn = dyn_ref[...]
        dyn = jnp.where(seq_end, jnp.zeros_like(dyn), dyn)
        d_out = jnp.concatenate([dy_ref[...], dyn], axis=0).astype(BF16)
        d_act = _dot_nt(d_out, wd_ref[0])
        cge = jnp.concatenate([pgm_ref[0], pgn_ref[0]], axis=0).astype(F32)
        cve = jnp.concatenate([pvm_ref[0], pvn_ref[0]], axis=0).astype(F32)
        act, vjp_act = jax.vjp(lambda g, v: jax.nn.silu(g) * v, cge, cve)
        dcge, dcve = vjp_act(d_act)
        act_ref[0] = act[:tm].astype(BF16)

        def conv_t(dc, u_ref, cw):
            ahead1 = pltpu.roll(dc, rows - 1, 0)[:tm]
            ahead2 = pltpu.roll(dc, rows - 2, 0)[:tm]
            here = dc[:tm]
            du = cw[2:3] * here + cw[1:2] * ahead1 + cw[0:1] * ahead2
            u = u_ref[0].astype(F32)
            col = lambda x: jnp.sum(x, axis=0, keepdims=True)
            grad = jnp.concatenate([col(ahead2 * u), col(ahead1 * u), col(here * u), col(here),
                                    jnp.zeros((4, fb), F32)], axis=0)
            return du.astype(BF16), grad

        cwg, cwv = cg_ref[0], cv_ref[0]
        dug, grad_g = conv_t(dcge, ug_ref, cwg)
        duv, grad_v = conv_t(dcve, uv_ref, cwv)
        dug_ref[0] = dug
        duv_ref[0] = duv
        part = _dot(dug, wg_ref[0]) + _dot(duv, wv_ref[0])

        @pl.when(j == 0)
        def _():
            acc[...] = part

        @pl.when(j > 0)
        def _():
            acc[...] += part

        @pl.when(j == nj - 1)
        def _():
            dhn_ref[...] = acc[...]

        @pl.when(i == 0)
        def _():
            dcg_ref[j] = jnp.zeros((8, fb), F32)
            dcv_ref[j] = jnp.zeros((8, fb), F32)

        dcg_ref[j] += grad_g
        dcv_ref[j] += grad_v

    row = pl.BlockSpec((tm, D_MODEL), lambda i, j: (i, 0))
    u_main = pl.BlockSpec((1, tm, fb), lambda i, j: (j, i, 0))
    u_next = pl.BlockSpec((1, HALO, fb), lambda i, j: (j, jnp.minimum((i + 1) * hb, last_hb), 0))
    dc_spec = pl.BlockSpec((nj, 8, fb), lambda i, j: (0, 0, 0))
    u_shape = jax.ShapeDtypeStruct((FFN_GROUPS, n, fb), BF16)
    return pl.pallas_call(
        body, name="ffn_bwd", grid=(n // tm, nj),
        in_specs=[row, pl.BlockSpec((HALO, D_MODEL), lambda i, j: (jnp.minimum((i + 1) * hb, last_hb), 0)),
                  u_main, u_main, u_main, u_next, u_main, u_next, *_ffn_weight_specs()],
        out_specs=(u_main, u_main, u_main, row, dc_spec, dc_spec),
        out_shape=(u_shape, u_shape, u_shape, jax.ShapeDtypeStruct((n, D_MODEL), F32),
                   jax.ShapeDtypeStruct((nj, 8, fb), F32), jax.ShapeDtypeStruct((nj, 8, fb), F32)),
        scratch_shapes=[pltpu.VMEM((tm, D_MODEL), F32)],
        compiler_params=_cparams("arbitrary", "arbitrary"),
    )(dy, dy, ug, uv, pg, pg, pv, pv, w_up, w_up, conv, conv, w_down)


def _tn_grouped(a, b, name, shared_a, out_dtype=F32, tn=1024):
    groups = b.shape[0] if shared_a else a.shape[0]
    n_tok = a.shape[0] if shared_a else b.shape[0]
    k_dim, m_dim = a.shape[-1], b.shape[-1]

    def body(a_ref, b_ref, o_ref, acc):
        k = pl.program_id(1)
        a_t = a_ref[...] if shared_a else a_ref[0]
        b_t = b_ref[0] if shared_a else b_ref[...]
        part = _dot_tn(a_t.astype(BF16), b_t.astype(BF16))

        @pl.when(k == 0)
        def _():
            acc[...] = part

        @pl.when(k > 0)
        def _():
            acc[...] += part

        @pl.when(k == n_tok // tn - 1)
        def _():
            o_ref[0] = acc[...].astype(out_dtype)

    plain = lambda w: pl.BlockSpec((tn, w), lambda g, k: (k, 0))
    grouped = lambda w: pl.BlockSpec((1, tn, w), lambda g, k: (g, k, 0))
    return pl.pallas_call(
        body, name=name, grid=(groups, n_tok // tn),
        in_specs=[plain(k_dim), grouped(m_dim)] if shared_a else [grouped(k_dim), plain(m_dim)],
        out_specs=pl.BlockSpec((1, k_dim, m_dim), lambda g, k: (g, 0, 0)),
        out_shape=jax.ShapeDtypeStruct((groups, k_dim, m_dim), out_dtype),
        scratch_shapes=[pltpu.VMEM((k_dim, m_dim), F32)],
        compiler_params=_cparams("parallel", "arbitrary"),
    )(a, b)


def _s5_param_fn(lr, li, ldt, br, bi):
    dt = jnp.exp(ldt)
    mag = jnp.exp(lr * dt)
    ab_re = mag * jnp.cos(li * dt)
    ab_im = mag * jnp.sin(li * dt)
    nr = ab_re - 1.0
    ni = ab_im
    den = lr * lr + li * li
    q_re = (nr * lr + ni * li) / den
    q_im = (ni * lr - nr * li) / den
    bb_re = q_re * br - q_im * bi
    bb_im = q_re * bi + q_im * br
    return ab_re, ab_im, bb_re, bb_im


def _s5_param_fwd(lr, li, ldt, br, bi):
    def body(lr_ref, li_ref, ldt_ref, br_ref, bi_ref, ar_ref, ai_ref, bbr_ref, bbi_ref):
        ar, ai, bbr, bbi = _s5_param_fn(lr_ref[...], li_ref[...], ldt_ref[...], br_ref[...], bi_ref[...])
        ar_ref[...] = ar
        ai_ref[...] = ai
        bbr_ref[...] = bbr
        bbi_ref[...] = bbi

    return pl.pallas_call(
        body, name="s5_param_fwd",
        out_shape=(jax.ShapeDtypeStruct(lr.shape, F32), jax.ShapeDtypeStruct(lr.shape, F32),
                   jax.ShapeDtypeStruct(br.shape, F32), jax.ShapeDtypeStruct(br.shape, F32)),
    )(lr, li, ldt, br, bi)


def _s5_param_bwd(lr, li, ldt, br, bi, dar, dai, dbbr, dbbi):
    def body(lr_ref, li_ref, ldt_ref, br_ref, bi_ref, dar_ref, dai_ref, dbbr_ref, dbbi_ref,
             dlr_ref, dli_ref, dldt_ref, dbr_ref, dbi_ref):
        _, vjp = jax.vjp(_s5_param_fn, lr_ref[...], li_ref[...], ldt_ref[...], br_ref[...], bi_ref[...])
        dlr, dli, dldt, dbr, dbi = vjp((dar_ref[...], dai_ref[...], dbbr_ref[...], dbbi_ref[...]))
        dlr_ref[...] = dlr
        dli_ref[...] = dli
        dldt_ref[...] = dldt
        dbr_ref[...] = dbr
        dbi_ref[...] = dbi

    return pl.pallas_call(
        body, name="s5_param_bwd",
        out_shape=(jax.ShapeDtypeStruct(lr.shape, F32), jax.ShapeDtypeStruct(lr.shape, F32),
                   jax.ShapeDtypeStruct(ldt.shape, F32), jax.ShapeDtypeStruct(br.shape, F32),
                   jax.ShapeDtypeStruct(br.shape, F32)),
    )(lr, li, ldt, br, bi, dar, dai, dbbr, dbbi)


S5_CHUNK = 1024
S5_STATES = 512
S5_BLOCKS = 4


def _cpow_rows(ar, ai, count):
    rs, im = [ar], [ai]
    for _ in range(count - 1):
        pr, pi = rs[-1], im[-1]
        rs.append(pr * ar - pi * ai)
        im.append(pr * ai + pi * ar)
    return rs, im


def _scan_in_groups(vr, vi, pr, pi, rm, reverse):
    n, width = vr.shape
    vr = vr.reshape(n // SUBLANES, SUBLANES, width)
    vi = vi.reshape(n // SUBLANES, SUBLANES, width)
    row = rm[0:SUBLANES]
    for k in (1, 2, 4):
        shift = SUBLANES - k if reverse else k
        keep = row < SUBLANES - k if reverse else row >= k
        kr = jnp.where(keep, pr[k - 1], 0.0)
        ki = jnp.where(keep, pi[k - 1], 0.0)
        sr, si = pltpu.roll(vr, shift, 1), pltpu.roll(vi, shift, 1)
        vr, vi = vr + kr * sr - ki * si, vi + kr * si + ki * sr
    return vr.reshape(n, width), vi.reshape(n, width)


def _carry_over_groups(xr_s, xi_s, wr, wi, c0r, c0i, reverse):
    groups = xr_s.shape[0] // SUBLANES
    pick = 0 if reverse else SUBLANES - 1

    def step(q, carry):
        cr, ci = carry
        r = groups - 1 - q if reverse else q
        o = pl.multiple_of(r * SUBLANES, SUBLANES)
        vr = xr_s[pl.ds(o, SUBLANES), :]
        vi = xi_s[pl.ds(o, SUBLANES), :]
        nr = vr + wr * cr - wi * ci
        ni = vi + wr * ci + wi * cr
        xr_s[pl.ds(o, SUBLANES), :] = nr
        xi_s[pl.ds(o, SUBLANES), :] = ni
        return (jnp.broadcast_to(nr[pick:pick + 1], nr.shape), jnp.broadcast_to(ni[pick:pick + 1], ni.shape))

    return lax.fori_loop(0, groups, step, (c0r, c0i), unroll=4)


def _s5_state_scan(u_b, bbr, bbi, pr, pi, rm, xr_s, xi_s, c0r, c0i):
    bur = _dot(u_b, bbr)
    bui = _dot(u_b, bbi)
    bur, bui = _scan_in_groups(bur, bui, pr, pi, rm, False)
    xr_s[...] = bur
    xi_s[...] = bui
    w8r = jnp.concatenate(pr, axis=0)
    w8i = jnp.concatenate(pi, axis=0)
    return _carry_over_groups(xr_s, xi_s, w8r, w8i, c0r, c0i, False)


def _s5_fwd(u, a_re, a_im, bbr, bbi, cr, ci, d_skip, n_seq):
    n = u.shape[0]
    seq_len = n // n_seq
    nt = seq_len // S5_CHUNK
    tc = S5_CHUNK

    def body(u_ref, ar_ref, ai_ref, bbr_ref, bbi_ref, cr_ref, ci_ref, d_ref, y_ref, str_ref, sti_ref, xrb_ref, xib_ref,
             xr_s, xi_s, car_r, car_i):
        t = pl.program_id(2)

        @pl.when(t == 0)
        def _():
            car_r[...] = jnp.zeros_like(car_r)
            car_i[...] = jnp.zeros_like(car_i)
        pr, pi = _cpow_rows(ar_ref[0], ai_ref[0], SUBLANES)
        rm = lax.broadcasted_iota(jnp.int32, (tc, S5_STATES), 0) & (SUBLANES - 1)
        str_ref[0, 0] = car_r[...]
        sti_ref[0, 0] = car_i[...]
        u_t = u_ref[...]
        cfr, cfi = _s5_state_scan(u_t.astype(BF16), bbr_ref[0], bbi_ref[0], pr, pi, rm, xr_s, xi_s,
                                  car_r[...], car_i[...])
        car_r[...] = cfr
        car_i[...] = cfi
        xr_b = xr_s[...].astype(BF16)
        xi_b = xi_s[...].astype(BF16)
        xrb_ref[...] = xr_b
        xib_ref[...] = xi_b
        y_ref[...] = _dot(xr_b, cr_ref[0]) - _dot(xi_b, ci_ref[0]) + d_ref[...] * u_t

    x_spec = pl.BlockSpec((tc, S5_STATES), lambda cb, b, t: (b * nt + t, cb))
    x_shape = jax.ShapeDtypeStruct((n, S5_BLOCKS * S5_STATES), BF16)
    u_spec = pl.BlockSpec((tc, LANES), lambda cb, b, t: (b * nt + t, cb))
    a_spec = pl.BlockSpec((1, 1, S5_STATES), lambda cb, b, t: (cb, 0, 0))
    bb_spec = pl.BlockSpec((1, LANES, S5_STATES), lambda cb, b, t: (cb, 0, 0))
    c_spec = pl.BlockSpec((1, S5_STATES, LANES), lambda cb, b, t: (cb, 0, 0))
    st_spec = pl.BlockSpec((1, 1, SUBLANES, S5_STATES), lambda cb, b, t: (cb, b * nt + t, 0, 0))
    st_shape = jax.ShapeDtypeStruct((S5_BLOCKS, n_seq * nt, SUBLANES, S5_STATES), F32)
    return pl.pallas_call(
        body, name="s5_fwd", grid=(S5_BLOCKS, n_seq, nt),
        in_specs=[u_spec, a_spec, a_spec, bb_spec, bb_spec, c_spec, c_spec,
                  pl.BlockSpec((1, LANES), lambda cb, b, t: (0, cb))],
        out_specs=(u_spec, st_spec, st_spec, x_spec, x_spec),
        out_shape=(jax.ShapeDtypeStruct((n, D_SSM), F32), st_shape, st_shape, x_shape, x_shape),
        scratch_shapes=[pltpu.VMEM((tc, S5_STATES), F32), pltpu.VMEM((tc, S5_STATES), F32),
                        pltpu.VMEM((SUBLANES, S5_STATES), F32), pltpu.VMEM((SUBLANES, S5_STATES), F32)],
        compiler_params=_cparams("parallel", "arbitrary", "arbitrary"),
    )(u, a_re, a_im, bbr, bbi, cr, ci, d_skip)


def _s5_bwd(u, dy, xs_r, xs_i, st_r, st_i, a_re, a_im, bbr, bbi, cr, ci, d_skip, n_seq):
    n = u.shape[0]
    seq_len = n // n_seq
    nt = seq_len // S5_CHUNK
    tc = S5_CHUNK

    def body(u_ref, dy_ref, xrb_ref, xib_ref, str_ref, sti_ref, ar_ref, ai_ref, bbr_ref, bbi_ref, cr_ref, ci_ref, d_ref,
             du_ref, dbbr_ref, dbbi_ref, dcr_ref, dci_ref, dar_ref, dai_ref, dd_ref,
             gr_s, gi_s, car_r, car_i):
        b, t = pl.program_id(1), pl.program_id(2)

        @pl.when((b == 0) & (t == 0))
        def _():
            for ref in (dbbr_ref, dbbi_ref, dcr_ref, dci_ref, dar_ref, dai_ref, dd_ref):
                ref[...] = jnp.zeros_like(ref)

        @pl.when(t == 0)
        def _():
            car_r[...] = jnp.zeros_like(car_r)
            car_i[...] = jnp.zeros_like(car_i)
        ar, ai = ar_ref[0], ai_ref[0]
        pr, pi = _cpow_rows(ar, ai, SUBLANES)
        row = lax.broadcasted_iota(jnp.int32, (tc, S5_STATES), 0)
        rm = row & (SUBLANES - 1)
        u_t = u_ref[...]
        u_b = u_t.astype(BF16)
        dy_t = dy_ref[...]
        dy_b = dy_t.astype(BF16)
        s0r, s0i = str_ref[0, 0], sti_ref[0, 0]
        xr_b, xi_b = xrb_ref[...], xib_ref[...]
        xr, xi = xr_b.astype(F32), xi_b.astype(F32)
        gr = _dot_nt(dy_b, cr_ref[0])
        gi = -_dot_nt(dy_b, ci_ref[0])
        npi = [-v for v in pi]
        gr, gi = _scan_in_groups(gr, gi, pr, npi, rm, True)
        gr_s[...] = gr
        gi_s[...] = gi
        w8r = jnp.concatenate(pr[::-1], axis=0)
        w8i = jnp.concatenate(npi[::-1], axis=0)
        cfr, cfi = _carry_over_groups(gr_s, gi_s, w8r, w8i, car_r[...], car_i[...], True)
        car_r[...] = cfr
        car_i[...] = cfi
        gr, gi = gr_s[...], gi_s[...]
        gr_b, gi_b = gr.astype(BF16), gi.astype(BF16)
        du_ref[...] = _dot_nt(gr_b, bbr_ref[0]) + _dot_nt(gi_b, bbi_ref[0]) + d_ref[...] * dy_t
        dbbr_ref[0] += _dot_tn(u_b, gr_b)
        dbbi_ref[0] += _dot_tn(u_b, gi_b)
        dcr_ref[0] += _dot_tn(xr_b, dy_b)
        dci_ref[0] -= _dot_tn(xi_b, dy_b)
        dd_ref[0] += jnp.sum((dy_t * u_t).reshape(tc // SUBLANES, SUBLANES, LANES), axis=0)
        first = row == 0
        xpr = jnp.where(first, jnp.broadcast_to(s0r[0:1], xr.shape), pltpu.roll(xr, 1, 0))
        xpi = jnp.where(first, jnp.broadcast_to(s0i[0:1], xi.shape), pltpu.roll(xi, 1, 0))
        shp = (tc // SUBLANES, SUBLANES, S5_STATES)
        dar_ref[0] += jnp.sum((gr * xpr + gi * xpi).reshape(shp), axis=0)
        dai_ref[0] += jnp.sum((gi * xpr - gr * xpi).reshape(shp), axis=0)

    u_spec = pl.BlockSpec((tc, LANES), lambda cb, b, t: (b * nt + nt - 1 - t, cb))
    a_spec = pl.BlockSpec((1, 1, S5_STATES), lambda cb, b, t: (cb, 0, 0))
    bb_spec = pl.BlockSpec((1, LANES, S5_STATES), lambda cb, b, t: (cb, 0, 0))
    c_spec = pl.BlockSpec((1, S5_STATES, LANES), lambda cb, b, t: (cb, 0, 0))
    st_spec = pl.BlockSpec((1, 1, SUBLANES, S5_STATES), lambda cb, b, t: (cb, b * nt + nt - 1 - t, 0, 0))
    da_spec = pl.BlockSpec((1, SUBLANES, S5_STATES), lambda cb, b, t: (cb, 0, 0))
    dd_spec = pl.BlockSpec((1, SUBLANES, LANES), lambda cb, b, t: (cb, 0, 0))
    big = pltpu.VMEM((tc, S5_STATES), F32)
    small = pltpu.VMEM((SUBLANES, S5_STATES), F32)
    x_spec = pl.BlockSpec((tc, S5_STATES), lambda cb, b, t: (b * nt + nt - 1 - t, cb))
    return pl.pallas_call(
        body, name="s5_bwd", grid=(S5_BLOCKS, n_seq, nt),
        in_specs=[u_spec, u_spec, x_spec, x_spec, st_spec, st_spec, a_spec, a_spec, bb_spec, bb_spec, c_spec, c_spec,
                  pl.BlockSpec((1, LANES), lambda cb, b, t: (0, cb))],
        out_specs=(u_spec, bb_spec, bb_spec, c_spec, c_spec, da_spec, da_spec, dd_spec),
        out_shape=(jax.ShapeDtypeStruct((n, D_SSM), F32),
                   jax.ShapeDtypeStruct((S5_BLOCKS, LANES, S5_STATES), F32),
                   jax.ShapeDtypeStruct((S5_BLOCKS, LANES, S5_STATES), F32),
                   jax.ShapeDtypeStruct((S5_BLOCKS, S5_STATES, LANES), F32),
                   jax.ShapeDtypeStruct((S5_BLOCKS, S5_STATES, LANES), F32),
                   jax.ShapeDtypeStruct((S5_BLOCKS, SUBLANES, S5_STATES), F32),
                   jax.ShapeDtypeStruct((S5_BLOCKS, SUBLANES, S5_STATES), F32),
                   jax.ShapeDtypeStruct((S5_BLOCKS, SUBLANES, LANES), F32)),
        scratch_shapes=[big, big, small, small],
        compiler_params=_cparams("parallel", "arbitrary", "arbitrary"),
    )(u, dy, xs_r, xs_i, st_r, st_i, a_re, a_im, bbr, bbi, cr, ci, d_skip)


CUM_BLOCK = 128


def _tri(lower):
    r = lax.broadcasted_iota(jnp.int32, (CUM_BLOCK, CUM_BLOCK), 0)
    c = lax.broadcasted_iota(jnp.int32, (CUM_BLOCK, CUM_BLOCK), 1)
    return jnp.where(r >= c if lower else r <= c, 1.0, 0.0).astype(F32)


def _fprep_fwd(fl, bf, n_seq):
    n = fl.shape[0]
    seq_len = n // n_seq
    nb = seq_len // CUM_BLOCK

    def body(fl_ref, bf_ref, cum_ref):
        tril = _tri(True)
        carry = jnp.zeros((1, LANES), F32)
        for blk in range(nb):
            rows = slice(blk * CUM_BLOCK, (blk + 1) * CUM_BLOCK)
            lf = jax.nn.log_sigmoid(fl_ref[rows, :] + bf_ref[...])
            cs = jnp.dot(tril, lf, preferred_element_type=F32, precision=HIGHEST) + carry
            cum_ref[rows, :] = cs
            carry = cs[CUM_BLOCK - 1:CUM_BLOCK, :]

    spec = pl.BlockSpec((seq_len, LANES), lambda b: (b, 0))
    return pl.pallas_call(
        body, name="fprep_fwd", grid=(n_seq,), in_specs=[spec, pl.BlockSpec((1, LANES), lambda b: (0, 0))],
        out_specs=spec, out_shape=jax.ShapeDtypeStruct((n, LANES), F32), compiler_params=_cparams("parallel"),
    )(fl, bf)


def _fprep_bwd(dcum, fl, bf, n_seq):
    n = fl.shape[0]
    seq_len = n // n_seq
    nb = seq_len // CUM_BLOCK

    def body(dcum_ref, fl_ref, bf_ref, dfl_ref, dbf_ref):
        triu = _tri(False)
        lane = lax.broadcasted_iota(jnp.int32, (CUM_BLOCK, LANES), 1)
        carry = jnp.zeros((1, LANES), F32)
        total = jnp.zeros((1, LANES), F32)
        for blk in reversed(range(nb)):
            rows = slice(blk * CUM_BLOCK, (blk + 1) * CUM_BLOCK)
            rs = jnp.dot(triu, dcum_ref[rows, :], preferred_element_type=F32, precision=HIGHEST) + carry
            carry = rs[0:1, :]
            _, vjp = jax.vjp(jax.nn.log_sigmoid, fl_ref[rows, :] + bf_ref[...])
            dz = jnp.where(lane < N_HEADS, vjp(rs)[0], 0.0)
            dfl_ref[rows, :] = dz
            total = total + jnp.sum(dz, axis=0, keepdims=True)
        dbf_ref[0] = total

    spec = pl.BlockSpec((seq_len, LANES), lambda b: (b, 0))
    return pl.pallas_call(
        body, name="fprep_bwd", grid=(n_seq,), in_specs=[spec, spec, pl.BlockSpec((1, LANES), lambda b: (0, 0))],
        out_specs=(spec, pl.BlockSpec((1, 1, LANES), lambda b: (b, 0, 0))),
        out_shape=(jax.ShapeDtypeStruct((n, LANES), F32), jax.ShapeDtypeStruct((n_seq, 1, LANES), F32)),
        compiler_params=_cparams("parallel"),
    )(dcum, fl, bf)


ATT_TQ = 256
ATT_KSTEP = 256
ATT_PAIRS = 2
ATT_SCALE = HEAD_DIM ** -0.5
NEG_BIG = -1e30


assert ATT_KSTEP == ATT_TQ


def _scores(q_scaled, kb, row_bias, ck, kend):
    s = _dot_nt(q_scaled, kb) - ck
    if row_bias is not None:
        s = s + row_bias
    r = lax.broadcasted_iota(jnp.int32, (ATT_TQ, ATT_TQ), 0)
    c = lax.broadcasted_iota(jnp.int32, (ATT_TQ, ATT_TQ), 1)
    diag = jnp.where(r >= c, s[:, kend - ATT_TQ:], NEG_BIG)
    return diag if kend == ATT_TQ else jnp.concatenate([s[:, :kend - ATT_TQ], diag], axis=1)


def _attn_specs(n_seq, seq_len):
    nq = seq_len // ATT_TQ
    width = ATT_PAIRS * LANES
    per = D_ATTN // width
    q_spec = pl.BlockSpec((ATT_TQ, width), lambda b, h, q: (b * nq + q, h))
    k_spec = pl.BlockSpec((seq_len, width), lambda b, h, q: (b, per + h))
    v_spec = pl.BlockSpec((seq_len, width), lambda b, h, q: (b, 2 * per + h))
    cq_spec = pl.BlockSpec((1, 2 * ATT_PAIRS, ATT_TQ, 1), lambda b, h, q: (b, h, q, 0))
    ck_spec = pl.BlockSpec((1, 2 * ATT_PAIRS, 1, seq_len), lambda b, h, q: (b, h, 0, 0))
    return nq, q_spec, k_spec, v_spec, cq_spec, ck_spec


def _own_cum(cum_ref, head):
    lane = lax.broadcasted_iota(jnp.int32, (1, LANES), 1)
    mine = lane == 2 * ATT_PAIRS * pl.program_id(1) + head
    return jnp.sum(jnp.where(mine, cum_ref[...], 0.0), axis=1, keepdims=True)


def _head_selectors():
    head0 = lax.broadcasted_iota(jnp.int32, (1, LANES), 1) < HEAD_DIM
    return head0, (head0, jnp.logical_not(head0))


def _for_key_range(qi, seq_len, run):
    per = ATT_KSTEP // ATT_TQ
    for g in range(seq_len // ATT_KSTEP):
        pl.when(qi // per == g)(functools.partial(run, (g + 1) * ATT_KSTEP))


def _attn_fwd(qkv, cum, ck, n_seq):
    n = qkv.shape[0]
    seq_len = n // n_seq
    nq, q_spec, k_spec, v_spec, cq_spec, ck_spec = _attn_specs(n_seq, seq_len)
    cum_spec = pl.BlockSpec((ATT_TQ, LANES), lambda b, h, q: (b * nq + q, 0))

    def body(q_ref, k_ref, v_ref, cum_ref, ck_ref, o_ref, lse_ref):
        qi = pl.program_id(2)
        head0, sels = _head_selectors()

        def run(kend):
            for pair in range(ATT_PAIRS):
                win = slice(pair * LANES, (pair + 1) * LANES)
                q2 = q_ref[:, win]
                kb = k_ref[0:kend, win]
                vb = v_ref[0:kend, win]
                outs = []
                for e in range(2):
                    head = 2 * pair + e
                    qe = jnp.where(sels[e], q2 * ATT_SCALE, 0.0).astype(BF16)
                    s = _scores(qe, kb, None, ck_ref[0, head, :, 0:kend], kend)
                    mx = jnp.max(s, axis=1, keepdims=True)
                    p = jnp.exp(s - mx)
                    den = jnp.sum(p, axis=1, keepdims=True)
                    outs.append(_dot(p.astype(BF16), vb) / den)
                    lse_ref[0, head] = _own_cum(cum_ref, head) + mx + jnp.log(den)
                o_ref[:, win] = jnp.where(head0, outs[0], outs[1])

        _for_key_range(qi, seq_len, run)

    return pl.pallas_call(
        body, name="attn_fwd", grid=(n_seq, N_HEADS // 2 // ATT_PAIRS, nq),
        in_specs=[q_spec, k_spec, v_spec, cum_spec, ck_spec],
        out_specs=(q_spec, cq_spec),
        out_shape=(jax.ShapeDtypeStruct((n, D_ATTN), F32), jax.ShapeDtypeStruct((n_seq, N_HEADS, seq_len, 1), F32)),
        compiler_params=_cparams("parallel", "parallel", "parallel"),
    )(qkv, qkv, qkv, cum, ck)


def _attn_bwd(qkv, cum, ck, o, do, lse, n_seq):
    n = qkv.shape[0]
    seq_len = n // n_seq
    nq, q_spec, k_spec, v_spec, cq_spec, ck_spec = _attn_specs(n_seq, seq_len)
    kv_out = pl.BlockSpec((seq_len, ATT_PAIRS * LANES), lambda b, h, q: (b, h))
    cum_spec = pl.BlockSpec((ATT_TQ, LANES), lambda b, h, q: (b * nq + q, 0))

    def body(q_ref, k_ref, v_ref, cum_ref, ck_ref, o_ref, do_ref, lse_ref, dq_ref, dk_ref, dv_ref, dcq_ref, dck_ref):
        qi = pl.program_id(2)

        @pl.when(qi == 0)
        def _():
            dk_ref[...] = jnp.zeros_like(dk_ref)
            dv_ref[...] = jnp.zeros_like(dv_ref)
            dck_ref[...] = jnp.zeros_like(dck_ref)
        head0, sels = _head_selectors()

        def run(kend):
            for pair in range(ATT_PAIRS):
                win = slice(pair * LANES, (pair + 1) * LANES)
                q2 = q_ref[:, win]
                do2 = do_ref[:, win]
                o2 = o_ref[:, win]
                kb = k_ref[0:kend, win]
                vb = v_ref[0:kend, win]
                dqs = []
                dk = jnp.zeros((kend, LANES), F32)
                dv = jnp.zeros((kend, LANES), F32)
                for e in range(2):
                    head = 2 * pair + e
                    qe = jnp.where(sels[e], q2 * ATT_SCALE, 0.0).astype(BF16)
                    doe = jnp.where(sels[e], do2, 0.0)
                    doe_b = doe.astype(BF16)
                    delta = jnp.sum(doe * o2, axis=1, keepdims=True)
                    bias = _own_cum(cum_ref, head) - lse_ref[0, head]
                    p = jnp.exp(_scores(qe, kb, bias, ck_ref[0, head, :, 0:kend], kend))
                    ds = p * (_dot_nt(doe_b, vb) - delta)
                    ds_b = ds.astype(BF16)
                    dqs.append(_dot(ds_b, kb))
                    dk = dk + _dot_tn(ds_b, qe)
                    dv = dv + _dot_tn(p.astype(BF16), doe_b)
                    dcq_ref[0, head] = jnp.sum(ds, axis=1, keepdims=True)
                    dck_ref[0, head, :, 0:kend] -= jnp.sum(ds, axis=0, keepdims=True)
                dk_ref[0:kend, win] += dk
                dv_ref[0:kend, win] += dv
                dq_ref[:, win] = jnp.where(head0, dqs[0], dqs[1]) * ATT_SCALE

        _for_key_range(qi, seq_len, run)

    return pl.pallas_call(
        body, name="attn_bwd", grid=(n_seq, N_HEADS // 2 // ATT_PAIRS, nq),
        in_specs=[q_spec, k_spec, v_spec, cum_spec, ck_spec, q_spec, q_spec, cq_spec],
        out_specs=(q_spec, kv_out, kv_out, cq_spec, ck_spec),
        out_shape=(jax.ShapeDtypeStruct((n, D_ATTN), F32), jax.ShapeDtypeStruct((n, D_ATTN), F32),
                   jax.ShapeDtypeStruct((n, D_ATTN), F32),
                   jax.ShapeDtypeStruct((n_seq, N_HEADS, seq_len, 1), F32),
                   jax.ShapeDtypeStruct((n_seq, N_HEADS, 1, seq_len), F32)),
        compiler_params=_cparams("parallel", "parallel", "arbitrary"),
    )(qkv, qkv, qkv, cum, ck, o, do, lse)


WEIGHT_NAMES = ("norm_mix", "w_in", "b_forget", "lam_re", "lam_im", "b_re", "b_im", "c_re", "c_im", "d_skip", "log_dt",
                "w_glu", "b_glu", "q_norm", "k_norm", "norm_out_ssm", "norm_out_attn", "w_out", "norm_ffn", "w_up",
                "conv_w", "conv_b", "w_down")
SHARDED = ("w_in", "w_glu", "w_out", "w_up", "conv_w", "w_down")
ADAM_TILE = {"w_in": (257, 256), "w_glu": (64, 512), "w_out": (128, 1024), "w_up": (688, 256), "conv_w": (3, 688),
             "w_down": (344, 1024)}
PACK_ROWS = SUBLANES * LANES


def _after_all(*arrays):
    return sum(a[(0,) * a.ndim].astype(F32) for a in arrays).reshape(1, 1)


def _pad_to(a, axis, size):
    pad = [(0, 0)] * a.ndim
    pad[axis] = (0, size - a.shape[axis])
    return jnp.pad(a, pad)


def _block_diag(t, transpose):
    t4 = t.reshape(S5_BLOCKS, 8, SSM_GROUP, SSM_STATE)
    eye = jnp.eye(8, dtype=t.dtype)
    if transpose:
        e = jnp.swapaxes(t4, 2, 3)[:, :, :, None, :] * eye[None, :, None, :, None]
        return e.reshape(S5_BLOCKS, S5_STATES, LANES)
    e = t4[:, :, :, None, :] * eye[None, :, None, :, None]
    return e.reshape(S5_BLOCKS, LANES, S5_STATES)


def _block_diag_extract(m, transpose):
    if transpose:
        m5 = m.reshape(S5_BLOCKS, 8, SSM_STATE, 8, SSM_GROUP)
        d = jnp.stack([m5[:, i, :, i, :] for i in range(8)], axis=1)
        return jnp.swapaxes(d, 2, 3).reshape(N_GROUPS, SSM_GROUP, SSM_STATE)
    m5 = m.reshape(S5_BLOCKS, 8, SSM_GROUP, 8, SSM_STATE)
    d = jnp.stack([m5[:, i, :, i, :] for i in range(8)], axis=1)
    return d.reshape(N_GROUPS, SSM_GROUP, SSM_STATE)


def _pack(pieces):
    flat = jnp.concatenate([p.reshape(-1).astype(F32) for p in pieces])
    size = -(-flat.shape[0] // PACK_ROWS) * PACK_ROWS
    return _pad_to(flat, 0, size).reshape(-1, LANES)


def _unpack(packed, shapes):
    flat = packed.reshape(-1)
    out, off = [], 0
    for shp in shapes:
        size = math.prod(shp)
        out.append(flat[off:off + size].reshape(shp))
        off += size
    return out


def kernel(x, norm_mix, w_in, b_forget, lam_re, lam_im, b_re, b_im, c_re, c_im, d_skip, log_dt, w_glu, b_glu, q_norm, k_norm, norm_out_ssm, norm_out_attn, w_out, norm_ffn, w_up, conv_w, conv_b, w_down, loss_target, m_norm_mix, m_w_in, m_b_forget, m_lam_re, m_lam_im, m_b_re, m_b_im, m_c_re, m_c_im, m_d_skip, m_log_dt, m_w_glu, m_b_glu, m_q_norm, m_k_norm, m_norm_out_ssm, m_norm_out_attn, m_w_out, m_norm_ffn, m_w_up, m_conv_w, m_conv_b, m_w_down, v_norm_mix, v_w_in, v_b_forget, v_lam_re, v_lam_im, v_b_re, v_b_im, v_c_re, v_c_im, v_d_skip, v_log_dt, v_w_glu, v_b_glu, v_q_norm, v_k_norm, v_norm_out_ssm, v_norm_out_attn, v_w_out, v_norm_ffn, v_w_up, v_conv_w, v_conv_b, v_w_down):
    given = dict(locals())
    weights = {k: given[k] for k in WEIGHT_NAMES}
    mom1 = {k: given["m_" + k] for k in WEIGHT_NAMES}
    mom2 = {k: given["v_" + k] for k in WEIGHT_NAMES}
    n_seq, seq_len, _ = x.shape
    n = n_seq * seq_len
    xf = x.reshape(n, D_MODEL)
    target = loss_target.reshape(n, D_MODEL)
    me_idx = 4 * lax.axis_index("x") + 2 * lax.axis_index("y") + lax.axis_index("c")

    in_flags = [False] * 2
    in_sems = _exchange_start([jnp.swapaxes(w_in[0], 0, 1).astype(BF16), conv_w[0]], in_flags, norm_mix,
                              "gather_in_start", 3, NEAR_PEERS)
    fill_own = lambda got, mine: lax.dynamic_update_index_in_dim(got, mine, me_idx, 0)

    lr3 = lam_re[0].reshape(N_GROUPS, 1, SSM_STATE)
    li3 = lam_im[0].reshape(N_GROUPS, 1, SSM_STATE)
    ldt3 = log_dt[0].reshape(N_GROUPS, 1, 1)
    br_t = jnp.swapaxes(b_re[0], 1, 2)
    bi_t = jnp.swapaxes(b_im[0], 1, 2)
    ab_re, ab_im, bb_re, bb_im = _s5_param_fwd(lr3, li3, ldt3, br_t, bi_t)
    a_re = ab_re.reshape(S5_BLOCKS, 1, S5_STATES)
    a_im = ab_im.reshape(S5_BLOCKS, 1, S5_STATES)
    bbr = _block_diag(bb_re, False).astype(BF16)
    bbi = _block_diag(bb_im, False).astype(BF16)
    cr = _block_diag(c_re[0], True).astype(BF16)
    ci = _block_diag(c_im[0], True).astype(BF16)
    avg = jnp.kron(jnp.eye(N_HEADS, dtype=F32), jnp.full((HEAD_DIM, HEAD_DIM), 1.0 / HEAD_DIM, F32)).astype(BF16)
    qg = jnp.tile(q_norm, (1, N_HEADS))
    kg = jnp.tile(k_norm, (1, N_HEADS))
    row_shards = [w_down[0].astype(BF16), w_out[0].astype(BF16), w_glu[0].astype(BF16)]

    (own_in, own_cw), near = _exchange_wait(in_sems[0], in_sems[1], in_sems[2], in_sems[3], in_flags,
                                            _after_all(a_re, a_im, bbr, bbi, cr, ci, avg, qg, kg, *row_shards),
                                            "gather_in_wait", NEAR_PEERS)
    relay = _relay_start(list(near), "gather_in_relay_start", 6)
    g_in, g_cw = _relay_wait(relay[0], relay[1], relay[2], relay[3], "gather_in_relay_wait")
    g_in = fill_own(g_in, own_in)
    g_cw = fill_own(g_cw, own_cw)
    row_flags = [False] * 3
    r_sems = _exchange_start(row_shards, row_flags, g_in, "gather_rows_start", 0)
    u_sems = _exchange_start([jnp.swapaxes(w_up[0], 0, 1).astype(BF16)], [False], r_sems[4], "gather_up_start", 5)
    norm_mix = norm_mix + u_sems[4][0, 0]
    w_in_p = _pad_to(g_in.reshape(D_IN, D_MODEL), 0, D_IN_PAD)

    hn, u, qkv, raw, fl = _inproj_fwd(xf, norm_mix, w_in_p, avg, qg, kg)
    yc, st_r, st_i, xs_r, xs_i = _s5_fwd(u, a_re, a_im, bbr, bbi, cr, ci, d_skip, n_seq)
    bf = _pad_to(b_forget, 1, LANES)
    cum = _fprep_fwd(fl, bf, n_seq)
    cum8 = jnp.swapaxes(cum[:, :N_HEADS].reshape(n_seq, seq_len, N_HEADS), 1, 2)
    ck = cum8[:, :, None, :]
    ya, lse = _attn_fwd(qkv, cum, ck, n_seq)
    own_rows, got_rows = _exchange_wait(r_sems[0], r_sems[1], r_sems[2], r_sems[3], row_flags, ya, "gather_rows_wait")
    g_down, g_out, g_glu = [fill_own(g, o) for o, g in zip(own_rows, got_rows)]
    w_glu_f = g_glu.reshape(D_SSM, D_SSM)
    w_out_f = g_out.reshape(D_MODEL, D_MODEL)
    conv_st = _pad_to(jnp.concatenate([g_cw, conv_b.reshape(N_DEV, 1, -1)], axis=1), 1, SUBLANES)
    w_down4 = g_down.reshape(FFN_GROUPS, FFN_GROUP, D_MODEL)
    ys = _glu_fwd(yc, w_glu_f, b_glu)
    h1, hn2, mixed = _mix_fwd(xf, ys, ya, norm_out_ssm, norm_out_attn, w_out_f, norm_ffn)
    (own_up,), (g_up,) = _exchange_wait(u_sems[0], u_sems[1], u_sems[2], u_sems[3], [False], _after_all(h1, hn2),
                                        "gather_up_wait")
    g_up = fill_own(g_up, own_up)
    ug, uv, pg, pv, dy, loss_part = _ffn_fwd(hn2, h1, target, g_up, conv_st, w_down4, seq_len)
    loss_local = 0.5 * jnp.sum(loss_part) / D_MODEL

    dug, duv, act, dhn2, dcg, dcv = _ffn_bwd(dy, ug, uv, pg, pv, g_up, conv_st, w_down4, seq_len)
    dh1, dys, dya, d_gs, d_ga, d_gf = _mix_bwd(dy, dhn2[None], h1, ys, ya, norm_out_ssm, norm_out_attn, w_out_f, norm_ffn)
    dyc, gl_b, dz_b, d_bglu = _glu_bwd(yc, dys, w_glu_f, b_glu)

    gw_glu = _tn_matmul(gl_b, dz_b, "dw_glu", D_SSM, D_SSM, out_dtype=BF16, tn=n)
    gw_out = _tn_matmul(mixed, dh1, "dw_out", D_MODEL, D_MODEL, out_dtype=BF16, tn=n // 2)
    gw_up = jnp.concatenate([_tn_grouped(dug, hn2, "dw_up_gate", False, BF16, tn=n),
                             _tn_grouped(duv, hn2, "dw_up_val", False, BF16, tn=n)], axis=0)
    gw_down = _tn_grouped(act, dy, "dw_down", False, BF16, tn=n // 2)
    g_conv = jnp.concatenate([dcg, dcv], axis=0)
    by_cols = lambda g, c: jnp.swapaxes(g.reshape(g.shape[0], N_DEV, c), 0, 1)
    early_flags = [True] * 4
    early_names = ("w_down", "w_out", "w_glu", "w_up")
    g_sems = _exchange_start(
        [gw_down.reshape(N_DEV, -1, D_MODEL), gw_out.reshape(N_DEV, -1, D_MODEL), gw_glu.reshape(N_DEV, -1, D_SSM), gw_up],
        early_flags, dyc, "grad_early_start", 1)
    started = g_sems[4][0, 0]

    du, dbbr, dbbi, dcr, dci, dar, dai, ddk = _s5_bwd(u, dyc, xs_r, xs_i, st_r, st_i, a_re, a_im, bbr, bbi, cr, ci,
                                                      d_skip + started, n_seq)
    partial_early = {
        "ab_re": jnp.sum(dar, axis=1), "ab_im": jnp.sum(dai, axis=1),
        "bb_re": _block_diag_extract(dbbr, False), "bb_im": _block_diag_extract(dbbi, False),
        "c_re": _block_diag_extract(dcr, True), "c_im": _block_diag_extract(dci, True),
        "d_skip": jnp.sum(ddk, axis=1), "b_glu": d_bglu,
        "norm_out_ssm": d_gs, "norm_out_attn": d_ga, "norm_ffn": d_gf, "conv_b": g_conv[:, 3],
    }
    early_keys = tuple(partial_early)
    early_shapes = [partial_early[k].shape for k in early_keys]
    p_sems = _exchange_start([_pack([partial_early[k] for k in early_keys])], [False], du, "small_early_start", 2)
    started = started + p_sems[4][0, 0]

    dqn, dkn, dv, dcq, dck = _attn_bwd(qkv, cum, ck + started, ya, dya, lse, n_seq)
    dcum8 = dcq[:, :, :, 0] + dck.reshape(n_seq, N_HEADS, seq_len)
    dcum = _pad_to(jnp.swapaxes(dcum8, 1, 2).reshape(n, N_HEADS), 1, LANES)
    dfl, dbf = _fprep_bwd(dcum, fl, bf, n_seq)
    dx, dproj, d_gmix, d_qg, d_kg = _inproj_bwd(xf, norm_mix, w_in_p, avg, qg, kg, raw, du, dqn, dkn, dv, dfl, dh1)

    gw_in = _tn_matmul(dproj, hn, "dw_in", D_IN_PAD, D_MODEL, out_rows=D_IN, out_dtype=BF16, tn=n // 4)
    partial_late = {
        "norm_mix": d_gmix, "b_forget": jnp.sum(dbf, axis=(0, 1))[:N_HEADS],
        "q_norm": jnp.sum(d_qg.reshape(N_HEADS, HEAD_DIM), axis=0),
        "k_norm": jnp.sum(d_kg.reshape(N_HEADS, HEAD_DIM), axis=0), "loss": loss_local.reshape(1),
    }
    late_keys = tuple(partial_late)
    late_shapes = [partial_late[k].shape for k in late_keys]

    late_flags = [True, True, False]
    l_sems = _exchange_start(
        [gw_in.reshape(N_DEV, D_IN // N_DEV, D_MODEL).astype(BF16), g_conv[:, :3],
         _pack([partial_late[k] for k in late_keys])],
        late_flags, dx, "grad_late_start", 4)
    early_src, early_land = _exchange_wait(g_sems[0], g_sems[1], g_sems[2], g_sems[3], early_flags, l_sems[4],
                                           "grad_early_wait")
    land = dict(zip(early_names, early_land))
    land_up = land["w_up"]
    own = {k: lax.dynamic_index_in_dim(s, me_idx, 0, keepdims=False) for k, s in zip(early_names, early_src)}
    grads, deltas, new_m, new_v = {}, {}, {}, {}

    def adam_shard(name):
        flip = (lambda a: jnp.swapaxes(a, 1, 2)) if name in ("w_in", "w_up") else (lambda a: a)
        outs = _adam_sharded(land[name], own[name], flip(weights[name]), flip(mom1[name]), flip(mom2[name]),
                             "adam_" + name, ADAM_TILE[name])
        grads[name], deltas[name], new_m[name], new_v[name] = [flip(o) for o in outs]

    for name in ("w_up", "w_down", "w_out", "w_glu"):
        adam_shard(name)
    (own_pack,), (early_parts,) = _exchange_wait(p_sems[0], p_sems[1], p_sems[2], p_sems[3], [False], land_up,
                                                 "small_early_wait")
    early_sum = _sum_partials(early_parts, own_pack, "sum_early_partials")
    (src_in, src_cw, own_late), (land["w_in"], land["conv_w"], late_parts) = _exchange_wait(
        l_sems[0], l_sems[1], l_sems[2], l_sems[3], late_flags,
        _after_all(early_sum, *[new_v[k] for k in ("w_up", "w_down", "w_out", "w_glu")]), "grad_late_wait")
    own["w_in"] = lax.dynamic_index_in_dim(src_in, me_idx, 0, keepdims=False)
    own["conv_w"] = lax.dynamic_index_in_dim(src_cw, me_idx, 0, keepdims=False)
    for name in ("w_in", "conv_w"):
        adam_shard(name)

    summed = dict(zip(late_keys, _unpack(_sum_partials(late_parts, own_late, "sum_late_partials"), late_shapes)))
    summed.update(zip(early_keys, _unpack(early_sum, early_shapes)))
    dlr, dli, dldt, dbr_t, dbi_t = _s5_param_bwd(
        lr3, li3, ldt3, br_t, bi_t, summed["ab_re"].reshape(lr3.shape), summed["ab_im"].reshape(lr3.shape),
        summed["bb_re"], summed["bb_im"])
    small_grads = {
        "norm_mix": summed["norm_mix"], "b_forget": summed["b_forget"], "lam_re": dlr, "lam_im": dli,
        "b_re": dbr_t, "b_im": dbi_t, "c_re": summed["c_re"], "c_im": summed["c_im"],
        "d_skip": summed["d_skip"], "log_dt": dldt, "b_glu": summed["b_glu"], "q_norm": summed["q_norm"],
        "k_norm": summed["k_norm"], "norm_out_ssm": summed["norm_out_ssm"], "norm_out_attn": summed["norm_out_attn"],
        "norm_ffn": summed["norm_ffn"], "conv_b": summed["conv_b"],
    }
    repl = tuple(k for k in WEIGHT_NAMES if k not in SHARDED)
    turn = lambda k, a: jnp.swapaxes(a, 2, 3) if k in ("b_re", "b_im") else a
    w_list = [turn(k, weights[k]) for k in repl]
    g_list = [small_grads[k].reshape(w.shape) for k, w in zip(repl, w_list)]
    d_list, m_list, v_list = _adam_replicated(g_list, w_list, [turn(k, mom1[k]) for k in repl],
                                              [turn(k, mom2[k]) for k in repl], "adam_replicated")
    for k, g, d, nm, nv in zip(repl, g_list, d_list, m_list, v_list):
        grads[k], deltas[k], new_m[k], new_v[k] = turn(k, g), turn(k, d), turn(k, nm), turn(k, nv)

    grad_x = dx.reshape(x.shape)
    loss = summed["loss"].reshape(())
    return (loss, grad_x, *[grads[k] for k in WEIGHT_NAMES], *[deltas[k] for k in WEIGHT_NAMES],
            *[new_m[k] for k in WEIGHT_NAMES], *[new_v[k] for k in WEIGHT_NAMES])
```

```python
import functools
import math

import jax
import jax.numpy as jnp
from jax import lax
from jax.experimental import pallas as pl
from jax.experimental.pallas import tpu as pltpu

F32 = jnp.float32
BF16 = jnp.bfloat16
HIGHEST = lax.Precision.HIGHEST

N_DEV = 8
D_MODEL = 1024
D_SSM = 512
D_ATTN = 512
N_HEADS = 8
HEAD_DIM = 64
N_GROUPS = 32
SSM_GROUP = 16
SSM_STATE = 64
D_FF = 2752
D_IN = 2056
D_IN_PAD = 2176
EPS = 1e-6
LANES = 128
SUBLANES = 8
VMEM_LIMIT = 56 * 1024 * 1024

ADAM_LR = 0.001
ADAM_B1 = 0.9
ADAM_B2 = 0.999
ADAM_EPS = 1e-08
ADAM_WD = 0.01
ADAM_STEP = 10


def _cparams(*sem):
    return pltpu.CompilerParams(dimension_semantics=sem, vmem_limit_bytes=VMEM_LIMIT)


def _dot(a, b, **kw):
    return jnp.dot(a, b, preferred_element_type=F32, **kw)


def _dot_nt(a, b):
    return lax.dot_general(a, b, (((1,), (1,)), ((), ())), preferred_element_type=F32)


def _dot_tn(a, b):
    return lax.dot_general(a, b, (((0,), (0,)), ((), ())), preferred_element_type=F32)


def _rms(x, g):
    return x * lax.rsqrt(jnp.mean(x * x, axis=-1, keepdims=True) + EPS) * g


def _split_dot(x, avg):
    hi = x.astype(BF16)
    lo = (x - hi.astype(F32)).astype(BF16)
    return _dot(hi, avg) + _dot(lo, avg)


@jax.custom_vjp
def _group_mean(x, avg):
    return _split_dot(x, avg)


def _group_mean_fwd(x, avg):
    return _split_dot(x, avg), avg


def _group_mean_bwd(avg, ct):
    return _split_dot(ct, avg), jnp.zeros_like(avg)


_group_mean.defvjp(_group_mean_fwd, _group_mean_bwd)


def _headnorm(q, avg, g):
    return q * lax.rsqrt(_group_mean(q * q, avg) + EPS) * g


ALL_PEERS = tuple(range(1, N_DEV))
NEAR_PEERS = (1, 2, 4, 6)
RELAYED = (2, 4, 6)


def _peer_list(js=ALL_PEERS):
    x, y, c = lax.axis_index("x"), lax.axis_index("y"), lax.axis_index("c")
    peers = []
    for j in js:
        px = 1 - x if (j >> 2) & 1 else x
        py = 1 - y if (j >> 1) & 1 else y
        pc = 1 - c if j & 1 else c
        peers.append(((px, py, pc), 4 * px + 2 * py + pc))
    return 4 * x + 2 * y + c, peers


def _split_copies(src, land, send_sems, recv_sems, scatter_flags, me, peers, incoming):
    copies = []
    for k in range(len(src)):
        for j, (pid, pidx) in enumerate(peers):
            s = src[k].at[pidx] if scatter_flags[k] else src[k]
            i = k * len(peers) + j
            copies.append(pltpu.make_async_remote_copy(
                src_ref=s, dst_ref=land[k].at[pidx if incoming else me], send_sem=send_sems[i],
                recv_sem=recv_sems[i], device_id=pid, device_id_type=pl.DeviceIdType.MESH))
    return copies


def _handshake(peers):
    barrier = pltpu.get_barrier_semaphore()
    for pid, _ in peers:
        pl.semaphore_signal(barrier, inc=1, device_id=pid, device_id_type=pl.DeviceIdType.MESH)
    pl.semaphore_wait(barrier, len(peers))


def _exchange_start(srcs, scatter_flags, after, name, collective_id, peer_js=ALL_PEERS):
    n = len(srcs)
    ns = n * len(peer_js)
    hbm = pl.BlockSpec(memory_space=pltpu.HBM)
    sem = pl.BlockSpec(memory_space=pltpu.SEMAPHORE)
    land_shapes = [s.shape if sc else (N_DEV,) + s.shape for s, sc in zip(srcs, scatter_flags)]

    def body(*refs):
        src, land = refs[:n], refs[n:2 * n]
        send_sems = refs[2 * n + 1:2 * n + 1 + ns]
        recv_sems = refs[2 * n + 1 + ns:2 * n + 1 + 2 * ns]
        token = refs[4 * n + 1 + 2 * ns]
        me, peers = _peer_list(peer_js)
        _handshake(peers)
        for cp in _split_copies(src, land, send_sems, recv_sems, scatter_flags, me, peers, False):
            cp.start()
        token[...] = jnp.zeros_like(token)

    outs = pl.pallas_call(
        body, name=name,
        out_shape=(*[pltpu.SemaphoreType.DMA(())] * (2 * ns), *[pltpu.HBM(s.shape, s.dtype) for s in srcs],
                   *[pltpu.HBM(shp, s.dtype) for shp, s in zip(land_shapes, srcs)],
                   jax.ShapeDtypeStruct((SUBLANES, LANES), F32)),
        in_specs=[hbm] * (2 * n) + [pl.BlockSpec(memory_space=pl.ANY)],
        out_specs=(*[sem] * (2 * ns), *[hbm] * (2 * n), pl.BlockSpec(memory_space=pltpu.VMEM)),
        input_output_aliases={i: 2 * ns + i for i in range(2 * n)},
        compiler_params=pltpu.CompilerParams(has_side_effects=pltpu.SideEffectType.DATAFLOW_SIDE_EFFECTING,
                                             collective_id=collective_id),
    )(*[pltpu.with_memory_space_constraint(s, pltpu.HBM) for s in srcs],
      *[pltpu.with_memory_space_constraint(lax.empty(shp, s.dtype), pltpu.HBM) for shp, s in zip(land_shapes, srcs)],
      after)
    return (outs[:ns], outs[ns:2 * ns], outs[2 * ns:2 * ns + n], outs[2 * ns + n:2 * ns + 2 * n], outs[2 * ns + 2 * n])


def _exchange_wait(send_sems, recv_sems, srcs, lands, scatter_flags, after, name, peer_js=ALL_PEERS):
    n = len(srcs)
    ns = n * len(peer_js)
    hbm = pl.BlockSpec(memory_space=pltpu.HBM)
    sem = pl.BlockSpec(memory_space=pltpu.SEMAPHORE)

    def body(*refs):
        src, land = refs[:n], refs[n:2 * n]
        s_sems = refs[2 * n:2 * n + ns]
        r_sems = refs[2 * n + ns:2 * n + 2 * ns]
        me, peers = _peer_list(peer_js)
        for cp in _split_copies(src, land, s_sems, r_sems, scatter_flags, me, peers, True):
            cp.wait_send()
            cp.wait_recv()

    outs = pl.pallas_call(
        body, name=name,
        out_shape=tuple(pltpu.HBM(a.shape, a.dtype) for a in (*srcs, *lands)),
        in_specs=[hbm] * (2 * n) + [sem] * (2 * ns) + [pl.BlockSpec(memory_space=pl.ANY)],
        out_specs=tuple([hbm] * (2 * n)),
        input_output_aliases={i: i for i in range(2 * n)},
        compiler_params=pltpu.CompilerParams(has_side_effects=pltpu.SideEffectType.DATAFLOW_SIDE_EFFECTING),
    )(*srcs, *lands, *send_sems, *recv_sems, after)
    return outs[:n], outs[n:]


def _relay_copies(land, send_sems, recv_sems, incoming):
    _, ((sibling, _),) = _peer_list((1,))
    _, heard = _peer_list(RELAYED)
    _, sibling_heard = _peer_list(tuple(j ^ 1 for j in RELAYED))
    copies = []
    for k in range(len(land)):
        for j in range(len(RELAYED)):
            slot = (sibling_heard if incoming else heard)[j][1]
            i = k * len(RELAYED) + j
            copies.append(pltpu.make_async_remote_copy(
                src_ref=land[k].at[slot], dst_ref=land[k].at[slot], send_sem=send_sems[i], recv_sem=recv_sems[i],
                device_id=sibling, device_id_type=pl.DeviceIdType.MESH))
    return copies


def _relay_start(lands, name, collective_id):
    n = len(lands)
    ns = n * len(RELAYED)
    hbm = pl.BlockSpec(memory_space=pltpu.HBM)
    sem = pl.BlockSpec(memory_space=pltpu.SEMAPHORE)

    def body(*refs):
        land = refs[:n]
        send_sems = refs[n:n + ns]
        recv_sems = refs[n + ns:n + 2 * ns]
        token = refs[2 * n + 2 * ns]
        _handshake(_peer_list((1,))[1])
        for cp in _relay_copies(land, send_sems, recv_sems, False):
            cp.start()
        token[...] = jnp.zeros_like(token)

    outs = pl.pallas_call(
        body, name=name,
        out_shape=(*[pltpu.SemaphoreType.DMA(())] * (2 * ns), *[pltpu.HBM(a.shape, a.dtype) for a in lands],
                   jax.ShapeDtypeStruct((SUBLANES, LANES), F32)),
        in_specs=[hbm] * n,
        out_specs=(*[sem] * (2 * ns), *[hbm] * n, pl.BlockSpec(memory_space=pltpu.VMEM)),
        input_output_aliases={i: 2 * ns + i for i in range(n)},
        compiler_params=pltpu.CompilerParams(has_side_effects=pltpu.SideEffectType.DATAFLOW_SIDE_EFFECTING,
                                             collective_id=collective_id),
    )(*lands)
    return outs[:ns], outs[ns:2 * ns], outs[2 * ns:2 * ns + n], outs[2 * ns + n]


def _relay_wait(send_sems, recv_sems, lands, after, name):
    n = len(lands)
    ns = n * len(RELAYED)
    hbm = pl.BlockSpec(memory_space=pltpu.HBM)
    sem = pl.BlockSpec(memory_space=pltpu.SEMAPHORE)

    def body(*refs):
        land = refs[:n]
        for cp in _relay_copies(land, refs[n:n + ns], refs[n + ns:n + 2 * ns], True):
            cp.wait_send()
            cp.wait_recv()

    return pl.pallas_call(
        body, name=name,
        out_shape=tuple(pltpu.HBM(a.shape, a.dtype) for a in lands),
        in_specs=[hbm] * n + [sem] * (2 * ns) + [pl.BlockSpec(memory_space=pl.ANY)],
        out_specs=tuple([hbm] * n),
        input_output_aliases={i: i for i in range(n)},
        compiler_params=pltpu.CompilerParams(has_side_effects=pltpu.SideEffectType.DATAFLOW_SIDE_EFFECTING),
    )(*lands, *send_sems, *recv_sems, after)


def _tn_matmul(a, b, name, tk, tm, out_rows=None, out_cols=None, out_dtype=F32, tn=512):
    n_tok, k_dim = a.shape
    m_dim = b.shape[1]
    grid = (k_dim // tk, m_dim // tm, n_tok // tn)

    def body(a_ref, b_ref, o_ref, acc):
        k = pl.program_id(2)
        part = _dot_tn(a_ref[...].astype(BF16), b_ref[...].astype(BF16))

        @pl.when(k == 0)
        def _():
            acc[...] = part

        @pl.when(k > 0)
        def _():
            acc[...] += part

        @pl.when(k == grid[2] - 1)
        def _():
            o_ref[...] = acc[...].astype(out_dtype)

    return pl.pallas_call(
        body, name=name, grid=grid,
        in_specs=[pl.BlockSpec((tn, tk), lambda i, j, k: (k, i)), pl.BlockSpec((tn, tm), lambda i, j, k: (k, j))],
        out_specs=pl.BlockSpec((tk, tm), lambda i, j, k: (i, j)),
        out_shape=jax.ShapeDtypeStruct((out_rows or k_dim, out_cols or m_dim), out_dtype),
        scratch_shapes=[pltpu.VMEM((tk, tm), F32)],
        compiler_params=_cparams("parallel", "parallel", "arbitrary"),
    )(a, b)


def _adam_math(g, w, m, v):
    m = ADAM_B1 * m + (1.0 - ADAM_B1) * g
    v = ADAM_B2 * v + (1.0 - ADAM_B2) * (g * g)
    m_hat = m / (1.0 - ADAM_B1 ** ADAM_STEP)
    v_hat = v / (1.0 - ADAM_B2 ** ADAM_STEP)
    delta = -ADAM_LR * (m_hat / (jnp.sqrt(v_hat) + ADAM_EPS) + ADAM_WD * w)
    return delta, m, v


def _adam_sharded(land, own, w, m, v, name, tile):
    _, r, c = w.shape

    def body(*refs):
        l_ref = refs[0]
        own_ref = refs[1] if own is not None else None
        w_ref, m_ref, v_ref, g_ref, d_ref, nm_ref, nv_ref = [ref.at[0] for ref in refs[-7:]]
        if own_ref is not None:
            x, y, z = lax.axis_index("x"), lax.axis_index("y"), lax.axis_index("c")
            me = 4 * x + 2 * y + z
            mine = own_ref[...].astype(F32)
        g = None
        for s in range(N_DEV):
            part = l_ref[s].astype(F32)
            if own_ref is not None:
                part = jnp.where(me == s, mine, part)
            g = part if g is None else g + part
        d, nm, nv = _adam_math(g, w_ref[...], m_ref[...], v_ref[...])
        g_ref[...] = g
        d_ref[...] = d
        nm_ref[...] = nm
        nv_ref[...] = nv

    tr, tc = tile
    spec = pl.BlockSpec((1, tr, tc), lambda i, j: (0, i, j))
    own_specs, own_args = ([pl.BlockSpec((tr, tc), lambda i, j: (i, j))], [own]) if own is not None else ([], [])
    return pl.pallas_call(
        body, name=name, grid=(r // tr, c // tc),
        in_specs=[pl.BlockSpec((N_DEV, tr, tc), lambda i, j: (0, i, j)), *own_specs, spec, spec, spec],
        out_specs=(spec, spec, spec, spec),
        out_shape=tuple(jax.ShapeDtypeStruct((1, r, c), F32) for _ in range(4)),
        compiler_params=_cparams("parallel", "parallel"),
    )(land, *own_args, w, m, v)


def _sum_partials(parts, own, name):
    _, r, c = parts.shape

    def body(*refs):
        p_ref, o_ref = refs[0], refs[-1]
        if own is not None:
            x, y, z = lax.axis_index("x"), lax.axis_index("y"), lax.axis_index("c")
            me = 4 * x + 2 * y + z
            mine = refs[1][...]
        g = None
        for s in range(N_DEV):
            part = p_ref[s]
            if own is not None:
                part = jnp.where(me == s, mine, part)
            g = part if g is None else g + part
        o_ref[...] = g

    args = (parts,) if own is None else (parts, own)
    return pl.pallas_call(body, name=name, out_shape=jax.ShapeDtypeStruct((r, c), F32),
                          compiler_params=pltpu.CompilerParams(vmem_limit_bytes=VMEM_LIMIT))(*args)


def _adam_replicated(gs, ws, ms, vs, name):
    k = len(ws)

    def body(*refs):
        outs = refs[4 * k:]
        for i in range(k):
            d, nm, nv = _adam_math(refs[i][...], refs[k + i][...], refs[2 * k + i][...], refs[3 * k + i][...])
            outs[i][...] = d
            outs[k + i][...] = nm
            outs[2 * k + i][...] = nv

    outs = pl.pallas_call(body, name=name, out_shape=tuple(jax.ShapeDtypeStruct(w.shape, F32) for w in ws) * 3,
                          compiler_params=pltpu.CompilerParams(vmem_limit_bytes=VMEM_LIMIT))(*gs, *ws, *ms, *vs)
    return outs[:k], outs[k:2 * k], outs[2 * k:]


def _inproj_fwd(x, g, w_in, avg, qg, kg, tm=512):
    n = x.shape[0]

    def body(x_ref, g_ref, w_ref, a_ref, qg_ref, kg_ref, hn_ref, u_ref, qkv_ref, raw_ref, fl_ref):
        hn = _rms(x_ref[...], g_ref[...]).astype(BF16)
        hn_ref[...] = hn
        proj = _dot_nt(hn, w_ref[...])
        u_ref[...] = proj[:, 0:512]
        q = proj[:, 512:1024]
        k = proj[:, 1024:1536]
        raw_ref[:, 0:512] = q
        raw_ref[:, 512:1024] = k
        qkv_ref[:, 0:512] = _headnorm(q, a_ref[...], qg_ref[...]).astype(BF16)
        qkv_ref[:, 512:1024] = _headnorm(k, a_ref[...], kg_ref[...]).astype(BF16)
        qkv_ref[:, 1024:1536] = proj[:, 1536:2048].astype(BF16)
        fl_ref[...] = proj[:, 2048:D_IN_PAD]

    row = lambda w: pl.BlockSpec((tm, w), lambda i: (i, 0))
    full = lambda a: pl.BlockSpec(a.shape, lambda i: (0,) * a.ndim)
    return pl.pallas_call(
        body, name="inproj_fwd", grid=(n // tm,),
        in_specs=[row(D_MODEL), full(g), full(w_in), full(avg), full(qg), full(kg)],
        out_specs=(row(D_MODEL), row(512), row(1536), row(1024), row(LANES)),
        out_shape=(jax.ShapeDtypeStruct((n, D_MODEL), BF16), jax.ShapeDtypeStruct((n, 512), F32),
                   jax.ShapeDtypeStruct((n, 1536), BF16), jax.ShapeDtypeStruct((n, 1024), F32),
                   jax.ShapeDtypeStruct((n, LANES), F32)),
        compiler_params=_cparams("parallel"),
    )(x, g, w_in, avg, qg, kg)


def _inproj_bwd(x, g, w_in, avg, qg, kg, raw, du, dqn, dkn, dv, dfl, dres, tm=512):
    n = x.shape[0]

    def body(x_ref, g_ref, w_ref, a_ref, qg_ref, kg_ref, raw_ref, du_ref, dqn_ref, dkn_ref, dv_ref, dfl_ref, dres_ref,
             dx_ref, dproj_ref, dg_ref, dqg_ref, dkg_ref):
        @pl.when(pl.program_id(0) == 0)
        def _():
            dg_ref[...] = jnp.zeros_like(dg_ref)
            dqg_ref[...] = jnp.zeros_like(dqg_ref)
            dkg_ref[...] = jnp.zeros_like(dkg_ref)
        avg_m = a_ref[...]
        _, vjp_q = jax.vjp(lambda q, gg: _headnorm(q, avg_m, gg), raw_ref[:, 0:512], qg_ref[...])
        dq, dqg = vjp_q(dqn_ref[...])
        _, vjp_k = jax.vjp(lambda k, gg: _headnorm(k, avg_m, gg), raw_ref[:, 512:1024], kg_ref[...])
        dk, dkg = vjp_k(dkn_ref[...])
        dproj = jnp.concatenate([du_ref[...], dq, dk, dv_ref[...], dfl_ref[...]], axis=1).astype(BF16)
        dproj_ref[...] = dproj
        dhn = _dot(dproj, w_ref[...])
        _, vjp_x = jax.vjp(_rms, x_ref[...], g_ref[...])
        dxn, dg = vjp_x(dhn)
        dx_ref[...] = dxn + dres_ref[...]
        dg_ref[...] += dg
        dqg_ref[...] += dqg
        dkg_ref[...] += dkg

    row = lambda w: pl.BlockSpec((tm, w), lambda i: (i, 0))
    full = lambda a: pl.BlockSpec(a.shape, lambda i: (0,) * a.ndim)
    vec = lambda w: pl.BlockSpec((1, w), lambda i: (0, 0))
    return pl.pallas_call(
        body, name="inproj_bwd", grid=(n // tm,),
        in_specs=[row(D_MODEL), full(g), full(w_in), full(avg), full(qg), full(kg), row(1024), row(512), row(512),
                  row(512), row(512), row(LANES), row(D_MODEL)],
        out_specs=(row(D_MODEL), row(D_IN_PAD), vec(D_MODEL), vec(512), vec(512)),
        out_shape=(jax.ShapeDtypeStruct((n, D_MODEL), F32), jax.ShapeDtypeStruct((n, D_IN_PAD), BF16),
                   jax.ShapeDtypeStruct((1, D_MODEL), F32), jax.ShapeDtypeStruct((1, 512), F32),
                   jax.ShapeDtypeStruct((1, 512), F32)),
        compiler_params=_cparams("arbitrary"),
    )(x, g, w_in, avg, qg, kg, raw, du, dqn, dkn, dv, dfl, dres)


def _glu_fwd(yc, wg, bg, tm=512):
    n = yc.shape[0]

    def body(yc_ref, w_ref, b_ref, ys_ref):
        gl = jax.nn.gelu(yc_ref[...])
        z = _dot(gl.astype(BF16), w_ref[...]) + b_ref[...]
        ys_ref[...] = gl * jax.nn.sigmoid(z)

    row = pl.BlockSpec((tm, 512), lambda i: (i, 0))
    full = lambda a: pl.BlockSpec(a.shape, lambda i: (0,) * a.ndim)
    return pl.pallas_call(
        body, name="glu_fwd", grid=(n // tm,), in_specs=[row, full(wg), full(bg)], out_specs=row,
        out_shape=jax.ShapeDtypeStruct((n, 512), F32), compiler_params=_cparams("parallel"),
    )(yc, wg, bg)


def _glu_bwd(yc, dys, wg, bg, tm=512):
    n = yc.shape[0]

    def body(yc_ref, dys_ref, w_ref, b_ref, dyc_ref, gl_ref, dz_ref, db_ref):
        @pl.when(pl.program_id(0) == 0)
        def _():
            db_ref[...] = jnp.zeros_like(db_ref)
        gl, vjp_gelu = jax.vjp(jax.nn.gelu, yc_ref[...])
        glb = gl.astype(BF16)
        z = _dot(glb, w_ref[...]) + b_ref[...]
        s = jax.nn.sigmoid(z)
        dys = dys_ref[...]
        dz = dys * gl * s * (1.0 - s)
        dzb = dz.astype(BF16)
        dgl = dys * s + _dot_nt(dzb, w_ref[...])
        dyc_ref[...] = vjp_gelu(dgl)[0]
        gl_ref[...] = glb
        dz_ref[...] = dzb
        db_ref[...] += jnp.sum(dz, axis=0, keepdims=True)

    row = pl.BlockSpec((tm, 512), lambda i: (i, 0))
    full = lambda a: pl.BlockSpec(a.shape, lambda i: (0,) * a.ndim)
    return pl.pallas_call(
        body, name="glu_bwd", grid=(n // tm,), in_specs=[row, row, full(wg), full(bg)],
        out_specs=(row, row, row, pl.BlockSpec((1, 512), lambda i: (0, 0))),
        out_shape=(jax.ShapeDtypeStruct((n, 512), F32), jax.ShapeDtypeStruct((n, 512), BF16),
                   jax.ShapeDtypeStruct((n, 512), BF16), jax.ShapeDtypeStruct((1, 512), F32)),
        compiler_params=_cparams("arbitrary"),
    )(yc, dys, wg, bg)


def _mix_fwd(x, ys, ya, gs, ga, wout, gf, tm=512):
    n = x.shape[0]

    def body(x_ref, ys_ref, ya_ref, gs_ref, ga_ref, w_ref, gf_ref, h1_ref, hn2_ref, mixed_ref):
        mixed = jnp.concatenate([_rms(ys_ref[...], gs_ref[...]), _rms(ya_ref[...], ga_ref[...])], axis=1).astype(BF16)
        mixed_ref[...] = mixed
        h1 = x_ref[...] + _dot(mixed, w_ref[...])
        h1_ref[...] = h1
        hn2_ref[...] = _rms(h1, gf_ref[...]).astype(BF16)

    row = lambda w: pl.BlockSpec((tm, w), lambda i: (i, 0))
    full = lambda a: pl.BlockSpec(a.shape, lambda i: (0,) * a.ndim)
    return pl.pallas_call(
        body, name="mix_fwd", grid=(n // tm,),
        in_specs=[row(D_MODEL), row(512), row(512), full(gs), full(ga), full(wout), full(gf)],
        out_specs=(row(D_MODEL), row(D_MODEL), row(D_MODEL)),
        out_shape=(jax.ShapeDtypeStruct((n, D_MODEL), F32), jax.ShapeDtypeStruct((n, D_MODEL), BF16),
                   jax.ShapeDtypeStruct((n, D_MODEL), BF16)),
        compiler_params=_cparams("parallel"),
    )(x, ys, ya, gs, ga, wout, gf)


def _mix_bwd(dy, dhn2_parts, h1, ys, ya, gs, ga, wout, gf, tm=512):
    n = dy.shape[0]
    n_parts = dhn2_parts.shape[0]

    def body(dy_ref, dp_ref, h1_ref, ys_ref, ya_ref, gs_ref, ga_ref, w_ref, gf_ref,
             dh1_ref, dys_ref, dya_ref, dgs_ref, dga_ref, dgf_ref):
        @pl.when(pl.program_id(0) == 0)
        def _():
            dgs_ref[...] = jnp.zeros_like(dgs_ref)
            dga_ref[...] = jnp.zeros_like(dga_ref)
            dgf_ref[...] = jnp.zeros_like(dgf_ref)
        dhn2 = dp_ref[0]
        for p in range(1, n_parts):
            dhn2 = dhn2 + dp_ref[p]
        _, vjp_f = jax.vjp(_rms, h1_ref[...], gf_ref[...])
        dh1n, dgf = vjp_f(dhn2)
        dh1 = dy_ref[...] + dh1n
        dh1_ref[...] = dh1
        dmixed = _dot_nt(dh1.astype(BF16), w_ref[...])
        _, vjp_s = jax.vjp(_rms, ys_ref[...], gs_ref[...])
        dys, dgs = vjp_s(dmixed[:, 0:512])
        _, vjp_a = jax.vjp(_rms, ya_ref[...], ga_ref[...])
        dya, dga = vjp_a(dmixed[:, 512:1024])
        dys_ref[...] = dys
        dya_ref[...] = dya
        dgs_ref[...] += dgs
        dga_ref[...] += dga
        dgf_ref[...] += dgf

    row = lambda w: pl.BlockSpec((tm, w), lambda i: (i, 0))
    full = lambda a: pl.BlockSpec(a.shape, lambda i: (0,) * a.ndim)
    vec = lambda w: pl.BlockSpec((1, w), lambda i: (0, 0))
    return pl.pallas_call(
        body, name="mix_bwd", grid=(n // tm,),
        in_specs=[row(D_MODEL), pl.BlockSpec((n_parts, tm, D_MODEL), lambda i: (0, i, 0)), row(D_MODEL), row(512),
                  row(512), full(gs), full(ga), full(wout), full(gf)],
        out_specs=(row(D_MODEL), row(512), row(512), vec(512), vec(512), vec(D_MODEL)),
        out_shape=(jax.ShapeDtypeStruct((n, D_MODEL), F32), jax.ShapeDtypeStruct((n, 512), F32),
                   jax.ShapeDtypeStruct((n, 512), F32), jax.ShapeDtypeStruct((1, 512), F32),
                   jax.ShapeDtypeStruct((1, 512), F32), jax.ShapeDtypeStruct((1, D_MODEL), F32)),
        compiler_params=_cparams("arbitrary"),
    )(dy, dhn2_parts, h1, ys, ya, gs, ga, wout, gf)


HALO = 16
FFN_GROUPS = 4
FFN_GROUP = D_FF // FFN_GROUPS


def _conv3(ue, cw):
    return cw[2:3] * ue + cw[1:2] * pltpu.roll(ue, 1, 0) + cw[0:1] * pltpu.roll(ue, 2, 0) + cw[3:4]


def _ffn_weight_specs():
    gate = lambda i, j: (j, 0, 0)
    val = lambda i, j: (j + FFN_GROUPS, 0, 0)
    w_blk, c_blk = (1, FFN_GROUP, D_MODEL), (1, SUBLANES, FFN_GROUP)
    return [pl.BlockSpec(w_blk, gate), pl.BlockSpec(w_blk, val), pl.BlockSpec(c_blk, gate), pl.BlockSpec(c_blk, val),
            pl.BlockSpec((1, FFN_GROUP, D_MODEL), gate)]


def _ffn_fwd(hn2, h1, target, w_up, conv, w_down, seq_len, tm=512):
    n = hn2.shape[0]
    nj = FFN_GROUPS
    hb = tm // HALO

    def body(hn_ref, halo_ref, h1_ref, tgt_ref, wg_ref, wv_ref, cg_ref, cv_ref, wd_ref,
             ug_ref, uv_ref, pg_ref, pv_ref, dy_ref, dyb_ref, loss_ref, acc):
        i, j = pl.program_id(0), pl.program_id(1)
        seq_start = (i * tm) % seq_len == 0
        halo = halo_ref[...]
        halo = jnp.where(seq_start, jnp.zeros_like(halo), halo)
        he = jnp.concatenate([halo, hn_ref[...]], axis=0)
        ueg = _dot_nt(he, wg_ref[0])
        uev = _dot_nt(he, wv_ref[0])
        ug_ref[0] = ueg[HALO:].astype(BF16)
        uv_ref[0] = uev[HALO:].astype(BF16)
        cg = _conv3(ueg, cg_ref[0])[HALO:]
        cv = _conv3(uev, cv_ref[0])[HALO:]
        pg_ref[0] = cg.astype(BF16)
        pv_ref[0] = cv.astype(BF16)
        act = (jax.nn.silu(cg) * cv).astype(BF16)
        part = _dot(act, wd_ref[0])

        @pl.when(j == 0)
        def _():
            acc[...] = part

        @pl.when(j > 0)
        def _():
            acc[...] += part

        @pl.when(j == nj - 1)
        def _():
            err = h1_ref[...] + acc[...] - tgt_ref[...]
            dy = err * (1.0 / D_MODEL)
            dy_ref[...] = dy
            dyb_ref[...] = dy.astype(BF16)
            loss_ref[0] = jnp.sum(err * err, axis=0, keepdims=True)

    row = pl.BlockSpec((tm, D_MODEL), lambda i, j: (i, 0))
    u_main = pl.BlockSpec((1, tm, FFN_GROUP), lambda i, j: (j, i, 0))
    u_shape = jax.ShapeDtypeStruct((FFN_GROUPS, n, FFN_GROUP), BF16)
    return pl.pallas_call(
        body, name="ffn_fwd", grid=(n // tm, nj),
        in_specs=[row, pl.BlockSpec((HALO, D_MODEL), lambda i, j: (jnp.maximum(i * hb - 1, 0), 0)), row, row,
                  *_ffn_weight_specs()],
        out_specs=(u_main, u_main, u_main, u_main, row, row, pl.BlockSpec((1, 1, D_MODEL), lambda i, j: (i, 0, 0))),
        out_shape=(u_shape, u_shape, u_shape, u_shape, jax.ShapeDtypeStruct((n, D_MODEL), F32),
                   jax.ShapeDtypeStruct((n, D_MODEL), BF16), jax.ShapeDtypeStruct((n // tm, 1, D_MODEL), F32)),
        scratch_shapes=[pltpu.VMEM((tm, D_MODEL), F32)],
        compiler_params=_cparams("parallel", "arbitrary"),
    )(hn2, hn2, h1, target, w_up, w_up, conv, conv, w_down)


def _ffn_bwd(dy, ug, uv, pg, pv, w_up, conv, w_down, seq_len, tm=512):
    n = dy.shape[0]
    nj = FFN_GROUPS
    fb = FFN_GROUP
    hb = tm // HALO
    last_hb = n // HALO - 1
    rows = tm + HALO

    def body(dy_ref, dyn_ref, ug_ref, uv_ref, pgm_ref, pgn_ref, pvm_ref, pvn_ref, wg_ref, wv_ref, cg_ref, cv_ref,
             wd_ref, dug_ref, duv_ref, act_ref, dhn_ref, dcg_ref, dcv_ref, acc):
        i, j = pl.program_id(0), pl.program_id(1)
        seq_end = ((i + 1) * tm) % seq_len == 0
        dyn = dyn_ref[...]
        dyn = jnp.where(seq_end, jnp.zeros_like(dyn), dyn)
        d_out = jnp.concatenate([dy_ref[...], dyn], axis=0)
        d_act = _dot_nt(d_out, wd_ref[0])
        cge = jnp.concatenate([pgm_ref[0], pgn_ref[0]], axis=0).astype(F32)
        cve = jnp.concatenate([pvm_ref[0], pvn_ref[0]], axis=0).astype(F32)
        act, vjp_act = jax.vjp(lambda g, v: jax.nn.silu(g) * v, cge, cve)
        dcge, dcve = vjp_act(d_act)
        act_ref[0] = act[:tm].astype(BF16)

        def conv_t(dc, u_ref, cw):
            ahead1 = pltpu.roll(dc, rows - 1, 0)[:tm]
            ahead2 = pltpu.roll(dc, rows - 2, 0)[:tm]
            here = dc[:tm]
            du = cw[2:3] * here + cw[1:2] * ahead1 + cw[0:1] * ahead2
            u = u_ref[0].astype(F32)
            col = lambda x: jnp.sum(x, axis=0, keepdims=True)
            grad = jnp.concatenate([col(ahead2 * u), col(ahead1 * u), col(here * u), col(here),
                                    jnp.zeros((4, fb), F32)], axis=0)
            return du.astype(BF16), grad

        cwg, cwv = cg_ref[0], cv_ref[0]
        dug, grad_g = conv_t(dcge, ug_ref, cwg)
        duv, grad_v = conv_t(dcve, uv_ref, cwv)
        dug_ref[0] = dug
        duv_ref[0] = duv
        part = _dot(dug, wg_ref[0]) + _dot(duv, wv_ref[0])

        @pl.when(j == 0)
        def _():
            acc[...] = part

        @pl.when(j > 0)
        def _():
            acc[...] += part

        @pl.when(j == nj - 1)
        def _():
            dhn_ref[...] = acc[...]

        @pl.when(i == 0)
        def _():
            dcg_ref[j] = jnp.zeros((8, fb), F32)
            dcv_ref[j] = jnp.zeros((8, fb), F32)

        dcg_ref[j] += grad_g
        dcv_ref[j] += grad_v

    row = pl.BlockSpec((tm, D_MODEL), lambda i, j: (i, 0))
    u_main = pl.BlockSpec((1, tm, fb), lambda i, j: (j, i, 0))
    u_next = pl.BlockSpec((1, HALO, fb), lambda i, j: (j, jnp.minimum((i + 1) * hb, last_hb), 0))
    dc_spec = pl.BlockSpec((nj, 8, fb), lambda i, j: (0, 0, 0))
    u_shape = jax.ShapeDtypeStruct((FFN_GROUPS, n, fb), BF16)
    return pl.pallas_call(
        body, name="ffn_bwd", grid=(n // tm, nj),
        in_specs=[row, pl.BlockSpec((HALO, D_MODEL), lambda i, j: (jnp.minimum((i + 1) * hb, last_hb), 0)),
                  u_main, u_main, u_main, u_next, u_main, u_next, *_ffn_weight_specs()],
        out_specs=(u_main, u_main, u_main, row, dc_spec, dc_spec),
        out_shape=(u_shape, u_shape, u_shape, jax.ShapeDtypeStruct((n, D_MODEL), F32),
                   jax.ShapeDtypeStruct((nj, 8, fb), F32), jax.ShapeDtypeStruct((nj, 8, fb), F32)),
        scratch_shapes=[pltpu.VMEM((tm, D_MODEL), F32)],
        compiler_params=_cparams("arbitrary", "arbitrary"),
    )(dy, dy, ug, uv, pg, pg, pv, pv, w_up, w_up, conv, conv, w_down)


def _tn_grouped(a, b, name, shared_a, out_dtype=F32, tn=1024):
    groups = b.shape[0] if shared_a else a.shape[0]
    n_tok = a.shape[0] if shared_a else b.shape[0]
    k_dim, m_dim = a.shape[-1], b.shape[-1]

    def body(a_ref, b_ref, o_ref, acc):
        k = pl.program_id(1)
        a_t = a_ref[...] if shared_a else a_ref[0]
        b_t = b_ref[0] if shared_a else b_ref[...]
        part = _dot_tn(a_t.astype(BF16), b_t.astype(BF16))

        @pl.when(k == 0)
        def _():
            acc[...] = part

        @pl.when(k > 0)
        def _():
            acc[...] += part

        @pl.when(k == n_tok // tn - 1)
        def _():
            o_ref[0] = acc[...].astype(out_dtype)

    plain = lambda w: pl.BlockSpec((tn, w), lambda g, k: (k, 0))
    grouped = lambda w: pl.BlockSpec((1, tn, w), lambda g, k: (g, k, 0))
    return pl.pallas_call(
        body, name=name, grid=(groups, n_tok // tn),
        in_specs=[plain(k_dim), grouped(m_dim)] if shared_a else [grouped(k_dim), plain(m_dim)],
        out_specs=pl.BlockSpec((1, k_dim, m_dim), lambda g, k: (g, 0, 0)),
        out_shape=jax.ShapeDtypeStruct((groups, k_dim, m_dim), out_dtype),
        scratch_shapes=[pltpu.VMEM((k_dim, m_dim), F32)],
        compiler_params=_cparams("parallel", "arbitrary"),
    )(a, b)


def _s5_param_fn(lr, li, ldt, br, bi):
    dt = jnp.exp(ldt)
    mag = jnp.exp(lr * dt)
    ab_re = mag * jnp.cos(li * dt)
    ab_im = mag * jnp.sin(li * dt)
    nr = ab_re - 1.0
    ni = ab_im
    den = lr * lr + li * li
    q_re = (nr * lr + ni * li) / den
    q_im = (ni * lr - nr * li) / den
    bb_re = q_re * br - q_im * bi
    bb_im = q_re * bi + q_im * br
    return ab_re, ab_im, bb_re, bb_im


def _s5_param_fwd(lr, li, ldt, br, bi):
    def body(lr_ref, li_ref, ldt_ref, br_ref, bi_ref, ar_ref, ai_ref, bbr_ref, bbi_ref):
        ar, ai, bbr, bbi = _s5_param_fn(lr_ref[...], li_ref[...], ldt_ref[...], br_ref[...], bi_ref[...])
        ar_ref[...] = ar
        ai_ref[...] = ai
        bbr_ref[...] = bbr
        bbi_ref[...] = bbi

    return pl.pallas_call(
        body, name="s5_param_fwd",
        out_shape=(jax.ShapeDtypeStruct(lr.shape, F32), jax.ShapeDtypeStruct(lr.shape, F32),
                   jax.ShapeDtypeStruct(br.shape, F32), jax.ShapeDtypeStruct(br.shape, F32)),
    )(lr, li, ldt, br, bi)


def _s5_param_bwd(lr, li, ldt, br, bi, dar, dai, dbbr, dbbi):
    def body(lr_ref, li_ref, ldt_ref, br_ref, bi_ref, dar_ref, dai_ref, dbbr_ref, dbbi_ref,
             dlr_ref, dli_ref, dldt_ref, dbr_ref, dbi_ref):
        _, vjp = jax.vjp(_s5_param_fn, lr_ref[...], li_ref[...], ldt_ref[...], br_ref[...], bi_ref[...])
        dlr, dli, dldt, dbr, dbi = vjp((dar_ref[...], dai_ref[...], dbbr_ref[...], dbbi_ref[...]))
        dlr_ref[...] = dlr
        dli_ref[...] = dli
        dldt_ref[...] = dldt
        dbr_ref[...] = dbr
        dbi_ref[...] = dbi

    return pl.pallas_call(
        body, name="s5_param_bwd",
        out_shape=(jax.ShapeDtypeStruct(lr.shape, F32), jax.ShapeDtypeStruct(lr.shape, F32),
                   jax.ShapeDtypeStruct(ldt.shape, F32), jax.ShapeDtypeStruct(br.shape, F32),
                   jax.ShapeDtypeStruct(br.shape, F32)),
    )(lr, li, ldt, br, bi, dar, dai, dbbr, dbbi)


S5_CHUNK = 1024
S5_STATES = 512
S5_BLOCKS = 4


def _cpow_rows(ar, ai, count):
    rs, im = [ar], [ai]
    for _ in range(count - 1):
        pr, pi = rs[-1], im[-1]
        rs.append(pr * ar - pi * ai)
        im.append(pr * ai + pi * ar)
    return rs, im


def _scan_in_groups(vr, vi, pr, pi, rm, reverse):
    n, width = vr.shape
    vr = vr.reshape(n // SUBLANES, SUBLANES, width)
    vi = vi.reshape(n // SUBLANES, SUBLANES, width)
    row = rm[0:SUBLANES]
    for k in (1, 2, 4):
        shift = SUBLANES - k if reverse else k
        keep = row < SUBLANES - k if reverse else row >= k
        kr = jnp.where(keep, pr[k - 1], 0.0)
        ki = jnp.where(keep, pi[k - 1], 0.0)
        sr, si = pltpu.roll(vr, shift, 1), pltpu.roll(vi, shift, 1)
        vr, vi = vr + kr * sr - ki * si, vi + kr * si + ki * sr
    return vr.reshape(n, width), vi.reshape(n, width)


def _carry_over_groups(xr_s, xi_s, wr, wi, c0r, c0i, reverse):
    groups = xr_s.shape[0] // SUBLANES
    pick = 0 if reverse else SUBLANES - 1

    def step(q, carry):
        cr, ci = carry
        r = groups - 1 - q if reverse else q
        o = pl.multiple_of(r * SUBLANES, SUBLANES)
        vr = xr_s[pl.ds(o, SUBLANES), :]
        vi = xi_s[pl.ds(o, SUBLANES), :]
        nr = vr + wr * cr - wi * ci
        ni = vi + wr * ci + wi * cr
        xr_s[pl.ds(o, SUBLANES), :] = nr
        xi_s[pl.ds(o, SUBLANES), :] = ni
        return (jnp.broadcast_to(nr[pick:pick + 1], nr.shape), jnp.broadcast_to(ni[pick:pick + 1], ni.shape))

    return lax.fori_loop(0, groups, step, (c0r, c0i), unroll=4)


def _s5_state_scan(u_b, bbr, bbi, pr, pi, rm, xr_s, xi_s, c0r, c0i):
    bur = _dot(u_b, bbr)
    bui = _dot(u_b, bbi)
    bur, bui = _scan_in_groups(bur, bui, pr, pi, rm, False)
    xr_s[...] = bur
    xi_s[...] = bui
    w8r = jnp.concatenate(pr, axis=0)
    w8i = jnp.concatenate(pi, axis=0)
    return _carry_over_groups(xr_s, xi_s, w8r, w8i, c0r, c0i, False)


def _s5_fwd(u, a_re, a_im, bbr, bbi, cr, ci, d_skip, n_seq):
    n = u.shape[0]
    seq_len = n // n_seq
    nt = seq_len // S5_CHUNK
    tc = S5_CHUNK

    def body(u_ref, ar_ref, ai_ref, bbr_ref, bbi_ref, cr_ref, ci_ref, d_ref, y_ref, str_ref, sti_ref, xrb_ref, xib_ref,
             xr_s, xi_s, car_r, car_i):
        t = pl.program_id(2)

        @pl.when(t == 0)
        def _():
            car_r[...] = jnp.zeros_like(car_r)
            car_i[...] = jnp.zeros_like(car_i)
        pr, pi = _cpow_rows(ar_ref[0], ai_ref[0], SUBLANES)
        rm = lax.broadcasted_iota(jnp.int32, (tc, S5_STATES), 0) & (SUBLANES - 1)
        str_ref[0, 0] = car_r[...]
        sti_ref[0, 0] = car_i[...]
        u_t = u_ref[...]
        cfr, cfi = _s5_state_scan(u_t.astype(BF16), bbr_ref[0], bbi_ref[0], pr, pi, rm, xr_s, xi_s,
                                  car_r[...], car_i[...])
        car_r[...] = cfr
        car_i[...] = cfi
        xr_b = xr_s[...].astype(BF16)
        xi_b = xi_s[...].astype(BF16)
        xrb_ref[...] = xr_b
        xib_ref[...] = xi_b
        y_ref[...] = _dot(xr_b, cr_ref[0]) - _dot(xi_b, ci_ref[0]) + d_ref[...] * u_t

    x_spec = pl.BlockSpec((tc, S5_STATES), lambda cb, b, t: (b * nt + t, cb))
    x_shape = jax.ShapeDtypeStruct((n, S5_BLOCKS * S5_STATES), BF16)
    u_spec = pl.BlockSpec((tc, LANES), lambda cb, b, t: (b * nt + t, cb))
    a_spec = pl.BlockSpec((1, 1, S5_STATES), lambda cb, b, t: (cb, 0, 0))
    bb_spec = pl.BlockSpec((1, LANES, S5_STATES), lambda cb, b, t: (cb, 0, 0))
    c_spec = pl.BlockSpec((1, S5_STATES, LANES), lambda cb, b, t: (cb, 0, 0))
    st_spec = pl.BlockSpec((1, 1, SUBLANES, S5_STATES), lambda cb, b, t: (cb, b * nt + t, 0, 0))
    st_shape = jax.ShapeDtypeStruct((S5_BLOCKS, n_seq * nt, SUBLANES, S5_STATES), F32)
    return pl.pallas_call(
        body, name="s5_fwd", grid=(S5_BLOCKS, n_seq, nt),
        in_specs=[u_spec, a_spec, a_spec, bb_spec, bb_spec, c_spec, c_spec,
                  pl.BlockSpec((1, LANES), lambda cb, b, t: (0, cb))],
        out_specs=(u_spec, st_spec, st_spec, x_spec, x_spec),
        out_shape=(jax.ShapeDtypeStruct((n, D_SSM), F32), st_shape, st_shape, x_shape, x_shape),
        scratch_shapes=[pltpu.VMEM((tc, S5_STATES), F32), pltpu.VMEM((tc, S5_STATES), F32),
                        pltpu.VMEM((SUBLANES, S5_STATES), F32), pltpu.VMEM((SUBLANES, S5_STATES), F32)],
        compiler_params=_cparams("parallel", "arbitrary", "arbitrary"),
    )(u, a_re, a_im, bbr, bbi, cr, ci, d_skip)


def _s5_bwd(u, dy, xs_r, xs_i, st_r, st_i, a_re, a_im, bbr, bbi, cr, ci, d_skip, n_seq):
    n = u.shape[0]
    seq_len = n // n_seq
    nt = seq_len // S5_CHUNK
    tc = S5_CHUNK

    def body(u_ref, dy_ref, xrb_ref, xib_ref, str_ref, sti_ref, ar_ref, ai_ref, bbr_ref, bbi_ref, cr_ref, ci_ref, d_ref,
             du_ref, dbbr_ref, dbbi_ref, dcr_ref, dci_ref, dar_ref, dai_ref, dd_ref,
             gr_s, gi_s, car_r, car_i):
        b, t = pl.program_id(1), pl.program_id(2)

        @pl.when((b == 0) & (t == 0))
        def _():
            for ref in (dbbr_ref, dbbi_ref, dcr_ref, dci_ref, dar_ref, dai_ref, dd_ref):
                ref[...] = jnp.zeros_like(ref)

        @pl.when(t == 0)
        def _():
            car_r[...] = jnp.zeros_like(car_r)
            car_i[...] = jnp.zeros_like(car_i)
        ar, ai = ar_ref[0], ai_ref[0]
        pr, pi = _cpow_rows(ar, ai, SUBLANES)
        row = lax.broadcasted_iota(jnp.int32, (tc, S5_STATES), 0)
        rm = row & (SUBLANES - 1)
        u_t = u_ref[...]
        u_b = u_t.astype(BF16)
        dy_t = dy_ref[...]
        dy_b = dy_t.astype(BF16)
        s0r, s0i = str_ref[0, 0], sti_ref[0, 0]
        xr_b, xi_b = xrb_ref[...], xib_ref[...]
        xr, xi = xr_b.astype(F32), xi_b.astype(F32)
        gr = _dot_nt(dy_b, cr_ref[0])
        gi = -_dot_nt(dy_b, ci_ref[0])
        npi = [-v for v in pi]
        gr, gi = _scan_in_groups(gr, gi, pr, npi, rm, True)
        gr_s[...] = gr
        gi_s[...] = gi
        w8r = jnp.concatenate(pr[::-1], axis=0)
        w8i = jnp.concatenate(npi[::-1], axis=0)
        cfr, cfi = _carry_over_groups(gr_s, gi_s, w8r, w8i, car_r[...], car_i[...], True)
        car_r[...] = cfr
        car_i[...] = cfi
        gr, gi = gr_s[...], gi_s[...]
        gr_b, gi_b = gr.astype(BF16), gi.astype(BF16)
        du_ref[...] = _dot_nt(gr_b, bbr_ref[0]) + _dot_nt(gi_b, bbi_ref[0]) + d_ref[...] * dy_t
        dbbr_ref[0] += _dot_tn(u_b, gr_b)
        dbbi_ref[0] += _dot_tn(u_b, gi_b)
        dcr_ref[0] += _dot_tn(xr_b, dy_b)
        dci_ref[0] -= _dot_tn(xi_b, dy_b)
        dd_ref[0] += jnp.sum((dy_t * u_t).reshape(tc // SUBLANES, SUBLANES, LANES), axis=0)
        first = row == 0
        xpr = jnp.where(first, jnp.broadcast_to(s0r[0:1], xr.shape), pltpu.roll(xr, 1, 0))
        xpi = jnp.where(first, jnp.broadcast_to(s0i[0:1], xi.shape), pltpu.roll(xi, 1, 0))
        shp = (tc // SUBLANES, SUBLANES, S5_STATES)
        dar_ref[0] += jnp.sum((gr * xpr + gi * xpi).reshape(shp), axis=0)
        dai_ref[0] += jnp.sum((gi * xpr - gr * xpi).reshape(shp), axis=0)

    u_spec = pl.BlockSpec((tc, LANES), lambda cb, b, t: (b * nt + nt - 1 - t, cb))
    a_spec = pl.BlockSpec((1, 1, S5_STATES), lambda cb, b, t: (cb, 0, 0))
    bb_spec = pl.BlockSpec((1, LANES, S5_STATES), lambda cb, b, t: (cb, 0, 0))
    c_spec = pl.BlockSpec((1, S5_STATES, LANES), lambda cb, b, t: (cb, 0, 0))
    st_spec = pl.BlockSpec((1, 1, SUBLANES, S5_STATES), lambda cb, b, t: (cb, b * nt + nt - 1 - t, 0, 0))
    da_spec = pl.BlockSpec((1, SUBLANES, S5_STATES), lambda cb, b, t: (cb, 0, 0))
    dd_spec = pl.BlockSpec((1, SUBLANES, LANES), lambda cb, b, t: (cb, 0, 0))
    big = pltpu.VMEM((tc, S5_STATES), F32)
    small = pltpu.VMEM((SUBLANES, S5_STATES), F32)
    x_spec = pl.BlockSpec((tc, S5_STATES), lambda cb, b, t: (b * nt + nt - 1 - t, cb))
    return pl.pallas_call(
        body, name="s5_bwd", grid=(S5_BLOCKS, n_seq, nt),
        in_specs=[u_spec, u_spec, x_spec, x_spec, st_spec, st_spec, a_spec, a_spec, bb_spec, bb_spec, c_spec, c_spec,
                  pl.BlockSpec((1, LANES), lambda cb, b, t: (0, cb))],
        out_specs=(u_spec, bb_spec, bb_spec, c_spec, c_spec, da_spec, da_spec, dd_spec),
        out_shape=(jax.ShapeDtypeStruct((n, D_SSM), F32),
                   jax.ShapeDtypeStruct((S5_BLOCKS, LANES, S5_STATES), F32),
                   jax.ShapeDtypeStruct((S5_BLOCKS, LANES, S5_STATES), F32),
                   jax.ShapeDtypeStruct((S5_BLOCKS, S5_STATES, LANES), F32),
                   jax.ShapeDtypeStruct((S5_BLOCKS, S5_STATES, LANES), F32),
                   jax.ShapeDtypeStruct((S5_BLOCKS, SUBLANES, S5_STATES), F32),
                   jax.ShapeDtypeStruct((S5_BLOCKS, SUBLANES, S5_STATES), F32),
                   jax.ShapeDtypeStruct((S5_BLOCKS, SUBLANES, LANES), F32)),
        scratch_shapes=[big, big, small, small],
        compiler_params=_cparams("parallel", "arbitrary", "arbitrary"),
    )(u, dy, xs_r, xs_i, st_r, st_i, a_re, a_im, bbr, bbi, cr, ci, d_skip)


CUM_BLOCK = 128


def _tri(lower):
    r = lax.broadcasted_iota(jnp.int32, (CUM_BLOCK, CUM_BLOCK), 0)
    c = lax.broadcasted_iota(jnp.int32, (CUM_BLOCK, CUM_BLOCK), 1)
    return jnp.where(r >= c if lower else r <= c, 1.0, 0.0).astype(F32)


def _fprep_fwd(fl, bf, n_seq):
    n = fl.shape[0]
    seq_len = n // n_seq
    nb = seq_len // CUM_BLOCK

    def body(fl_ref, bf_ref, cum_ref):
        tril = _tri(True)
        carry = jnp.zeros((1, LANES), F32)
        for blk in range(nb):
            rows = slice(blk * CUM_BLOCK, (blk + 1) * CUM_BLOCK)
            lf = jax.nn.log_sigmoid(fl_ref[rows, :] + bf_ref[...])
            cs = jnp.dot(tril, lf, preferred_element_type=F32, precision=HIGHEST) + carry
            cum_ref[rows, :] = cs
            carry = cs[CUM_BLOCK - 1:CUM_BLOCK, :]

    spec = pl.BlockSpec((seq_len, LANES), lambda b: (b, 0))
    return pl.pallas_call(
        body, name="fprep_fwd", grid=(n_seq,), in_specs=[spec, pl.BlockSpec((1, LANES), lambda b: (0, 0))],
        out_specs=spec, out_shape=jax.ShapeDtypeStruct((n, LANES), F32), compiler_params=_cparams("parallel"),
    )(fl, bf)


def _fprep_bwd(dcum, fl, bf, n_seq):
    n = fl.shape[0]
    seq_len = n // n_seq
    nb = seq_len // CUM_BLOCK

    def body(dcum_ref, fl_ref, bf_ref, dfl_ref, dbf_ref):
        triu = _tri(False)
        lane = lax.broadcasted_iota(jnp.int32, (CUM_BLOCK, LANES), 1)
        carry = jnp.zeros((1, LANES), F32)
        total = jnp.zeros((1, LANES), F32)
        for blk in reversed(range(nb)):
            rows = slice(blk * CUM_BLOCK, (blk + 1) * CUM_BLOCK)
            rs = jnp.dot(triu, dcum_ref[rows, :], preferred_element_type=F32, precision=HIGHEST) + carry
            carry = rs[0:1, :]
            _, vjp = jax.vjp(jax.nn.log_sigmoid, fl_ref[rows, :] + bf_ref[...])
            dz = jnp.where(lane < N_HEADS, vjp(rs)[0], 0.0)
            dfl_ref[rows, :] = dz
            total = total + jnp.sum(dz, axis=0, keepdims=True)
        dbf_ref[0] = total

    spec = pl.BlockSpec((seq_len, LANES), lambda b: (b, 0))
    return pl.pallas_call(
        body, name="fprep_bwd", grid=(n_seq,), in_specs=[spec, spec, pl.BlockSpec((1, LANES), lambda b: (0, 0))],
        out_specs=(spec, pl.BlockSpec((1, 1, LANES), lambda b: (b, 0, 0))),
        out_shape=(jax.ShapeDtypeStruct((n, LANES), F32), jax.ShapeDtypeStruct((n_seq, 1, LANES), F32)),
        compiler_params=_cparams("parallel"),
    )(dcum, fl, bf)


ATT_TQ = 256
ATT_KSTEP = 256
ATT_PAIRS = 2
ATT_SCALE = HEAD_DIM ** -0.5
NEG_BIG = -1e30


assert ATT_KSTEP == ATT_TQ


def _scores(q_scaled, kb, row_bias, ck, kend):
    s = _dot_nt(q_scaled, kb) - ck
    if row_bias is not None:
        s = s + row_bias
    r = lax.broadcasted_iota(jnp.int32, (ATT_TQ, ATT_TQ), 0)
    c = lax.broadcasted_iota(jnp.int32, (ATT_TQ, ATT_TQ), 1)
    diag = jnp.where(r >= c, s[:, kend - ATT_TQ:], NEG_BIG)
    return diag if kend == ATT_TQ else jnp.concatenate([s[:, :kend - ATT_TQ], diag], axis=1)


def _attn_specs(n_seq, seq_len):
    nq = seq_len // ATT_TQ
    width = ATT_PAIRS * LANES
    per = D_ATTN // width
    q_spec = pl.BlockSpec((ATT_TQ, width), lambda b, h, q: (b * nq + q, h))
    k_spec = pl.BlockSpec((seq_len, width), lambda b, h, q: (b, per + h))
    v_spec = pl.BlockSpec((seq_len, width), lambda b, h, q: (b, 2 * per + h))
    cq_spec = pl.BlockSpec((1, 2 * ATT_PAIRS, ATT_TQ, 1), lambda b, h, q: (b, h, q, 0))
    ck_spec = pl.BlockSpec((1, 2 * ATT_PAIRS, 1, seq_len), lambda b, h, q: (b, h, 0, 0))
    return nq, q_spec, k_spec, v_spec, cq_spec, ck_spec


def _own_cum(cum_ref, head):
    lane = lax.broadcasted_iota(jnp.int32, (1, LANES), 1)
    mine = lane == 2 * ATT_PAIRS * pl.program_id(1) + head
    return jnp.sum(jnp.where(mine, cum_ref[...], 0.0), axis=1, keepdims=True)


def _head_selectors():
    head0 = lax.broadcasted_iota(jnp.int32, (1, LANES), 1) < HEAD_DIM
    return head0, (head0, jnp.logical_not(head0))


def _for_key_range(qi, seq_len, run):
    per = ATT_KSTEP // ATT_TQ
    for g in range(seq_len // ATT_KSTEP):
        pl.when(qi // per == g)(functools.partial(run, (g + 1) * ATT_KSTEP))


def _attn_fwd(qkv, cum, ck, n_seq):
    n = qkv.shape[0]
    seq_len = n // n_seq
    nq, q_spec, k_spec, v_spec, cq_spec, ck_spec = _attn_specs(n_seq, seq_len)
    cum_spec = pl.BlockSpec((ATT_TQ, LANES), lambda b, h, q: (b * nq + q, 0))

    def body(q_ref, k_ref, v_ref, cum_ref, ck_ref, o_ref, lse_ref):
        qi = pl.program_id(2)
        head0, sels = _head_selectors()

        def run(kend):
            for pair in range(ATT_PAIRS):
                win = slice(pair * LANES, (pair + 1) * LANES)
                q2 = q_ref[:, win]
                kb = k_ref[0:kend, win]
                vb = v_ref[0:kend, win]
                outs = []
                for e in range(2):
                    head = 2 * pair + e
                    qe = jnp.where(sels[e], q2 * ATT_SCALE, 0.0).astype(BF16)
                    s = _scores(qe, kb, None, ck_ref[0, head, :, 0:kend], kend)
                    mx = jnp.max(s, axis=1, keepdims=True)
                    p = jnp.exp(s - mx)
                    den = jnp.sum(p, axis=1, keepdims=True)
                    outs.append(_dot(p.astype(BF16), vb) / den)
                    lse_ref[0, head] = _own_cum(cum_ref, head) + mx + jnp.log(den)
                o_ref[:, win] = jnp.where(head0, outs[0], outs[1])

        _for_key_range(qi, seq_len, run)

    return pl.pallas_call(
        body, name="attn_fwd", grid=(n_seq, N_HEADS // 2 // ATT_PAIRS, nq),
        in_specs=[q_spec, k_spec, v_spec, cum_spec, ck_spec],
        out_specs=(q_spec, cq_spec),
        out_shape=(jax.ShapeDtypeStruct((n, D_ATTN), F32), jax.ShapeDtypeStruct((n_seq, N_HEADS, seq_len, 1), F32)),
        compiler_params=_cparams("parallel", "parallel", "parallel"),
    )(qkv, qkv, qkv, cum, ck)


def _attn_bwd(qkv, cum, ck, o, do, lse, n_seq):
    n = qkv.shape[0]
    seq_len = n // n_seq
    nq, q_spec, k_spec, v_spec, cq_spec, ck_spec = _attn_specs(n_seq, seq_len)
    kv_out = pl.BlockSpec((seq_len, ATT_PAIRS * LANES), lambda b, h, q: (b, h))
    cum_spec = pl.BlockSpec((ATT_TQ, LANES), lambda b, h, q: (b * nq + q, 0))

    def body(q_ref, k_ref, v_ref, cum_ref, ck_ref, o_ref, do_ref, lse_ref, dq_ref, dk_ref, dv_ref, dcq_ref, dck_ref):
        qi = pl.program_id(2)

        @pl.when(qi == 0)
        def _():
            dk_ref[...] = jnp.zeros_like(dk_ref)
            dv_ref[...] = jnp.zeros_like(dv_ref)
            dck_ref[...] = jnp.zeros_like(dck_ref)
        head0, sels = _head_selectors()

        def run(kend):
            for pair in range(ATT_PAIRS):
                win = slice(pair * LANES, (pair + 1) * LANES)
                q2 = q_ref[:, win]
                do2 = do_ref[:, win]
                o2 = o_ref[:, win]
                kb = k_ref[0:kend, win]
                vb = v_ref[0:kend, win]
                dqs = []
                dk = jnp.zeros((kend, LANES), F32)
                dv = jnp.zeros((kend, LANES), F32)
                for e in range(2):
                    head = 2 * pair + e
                    qe = jnp.where(sels[e], q2 * ATT_SCALE, 0.0).astype(BF16)
                    doe = jnp.where(sels[e], do2, 0.0)
                    doe_b = doe.astype(BF16)
                    delta = jnp.sum(doe * o2, axis=1, keepdims=True)
                    bias = _own_cum(cum_ref, head) - lse_ref[0, head]
                    p = jnp.exp(_scores(qe, kb, bias, ck_ref[0, head, :, 0:kend], kend))
                    ds = p * (_dot_nt(doe_b, vb) - delta)
                    ds_b = ds.astype(BF16)
                    dqs.append(_dot(ds_b, kb))
                    dk = dk + _dot_tn(ds_b, qe)
                    dv = dv + _dot_tn(p.astype(BF16), doe_b)
                    dcq_ref[0, head] = jnp.sum(ds, axis=1, keepdims=True)
                    dck_ref[0, head, :, 0:kend] -= jnp.sum(ds, axis=0, keepdims=True)
                dk_ref[0:kend, win] += dk
                dv_ref[0:kend, win] += dv
                dq_ref[:, win] = jnp.where(head0, dqs[0], dqs[1]) * ATT_SCALE

        _for_key_range(qi, seq_len, run)

    return pl.pallas_call(
        body, name="attn_bwd", grid=(n_seq, N_HEADS // 2 // ATT_PAIRS, nq),
        in_specs=[q_spec, k_spec, v_spec, cum_spec, ck_spec, q_spec, q_spec, cq_spec],
        out_specs=(q_spec, kv_out, kv_out, cq_spec, ck_spec),
        out_shape=(jax.ShapeDtypeStruct((n, D_ATTN), F32), jax.ShapeDtypeStruct((n, D_ATTN), F32),
                   jax.ShapeDtypeStruct((n, D_ATTN), F32),
                   jax.ShapeDtypeStruct((n_seq, N_HEADS, seq_len, 1), F32),
                   jax.ShapeDtypeStruct((n_seq, N_HEADS, 1, seq_len), F32)),
        compiler_params=_cparams("parallel", "parallel", "arbitrary"),
    )(qkv, qkv, qkv, cum, ck, o, do, lse)


WEIGHT_NAMES = ("norm_mix", "w_in", "b_forget", "lam_re", "lam_im", "b_re", "b_im", "c_re", "c_im", "d_skip", "log_dt",
                "w_glu", "b_glu", "q_norm", "k_norm", "norm_out_ssm", "norm_out_attn", "w_out", "norm_ffn", "w_up",
                "conv_w", "conv_b", "w_down")
SHARDED = ("w_in", "w_glu", "w_out", "w_up", "conv_w", "w_down")
ADAM_TILE = {"w_in": (257, 256), "w_glu": (64, 512), "w_out": (128, 1024), "w_up": (688, 256), "conv_w": (3, 688),
             "w_down": (344, 1024)}
PACK_ROWS = SUBLANES * LANES


def _after_all(*arrays):
    return sum(a[(0,) * a.ndim].astype(F32) for a in arrays).reshape(1, 1)


def _pad_to(a, axis, size):
    pad = [(0, 0)] * a.ndim
    pad[axis] = (0, size - a.shape[axis])
    return jnp.pad(a, pad)


def _block_diag(t, transpose):
    t4 = t.reshape(S5_BLOCKS, 8, SSM_GROUP, SSM_STATE)
    eye = jnp.eye(8, dtype=t.dtype)
    if transpose:
        e = jnp.swapaxes(t4, 2, 3)[:, :, :, None, :] * eye[None, :, None, :, None]
        return e.reshape(S5_BLOCKS, S5_STATES, LANES)
    e = t4[:, :, :, None, :] * eye[None, :, None, :, None]
    return e.reshape(S5_BLOCKS, LANES, S5_STATES)


def _block_diag_extract(m, transpose):
    if transpose:
        m5 = m.reshape(S5_BLOCKS, 8, SSM_STATE, 8, SSM_GROUP)
        d = jnp.stack([m5[:, i, :, i, :] for i in range(8)], axis=1)
        return jnp.swapaxes(d, 2, 3).reshape(N_GROUPS, SSM_GROUP, SSM_STATE)
    m5 = m.reshape(S5_BLOCKS, 8, SSM_GROUP, 8, SSM_STATE)
    d = jnp.stack([m5[:, i, :, i, :] for i in range(8)], axis=1)
    return d.reshape(N_GROUPS, SSM_GROUP, SSM_STATE)


def _pack(pieces):
    flat = jnp.concatenate([p.reshape(-1).astype(F32) for p in pieces])
    size = -(-flat.shape[0] // PACK_ROWS) * PACK_ROWS
    return _pad_to(flat, 0, size).reshape(-1, LANES)


def _unpack(packed, shapes):
    flat = packed.reshape(-1)
    out, off = [], 0
    for shp in shapes:
        size = math.prod(shp)
        out.append(flat[off:off + size].reshape(shp))
        off += size
    return out


def kernel(x, norm_mix, w_in, b_forget, lam_re, lam_im, b_re, b_im, c_re, c_im, d_skip, log_dt, w_glu, b_glu, q_norm, k_norm, norm_out_ssm, norm_out_attn, w_out, norm_ffn, w_up, conv_w, conv_b, w_down, loss_target, m_norm_mix, m_w_in, m_b_forget, m_lam_re, m_lam_im, m_b_re, m_b_im, m_c_re, m_c_im, m_d_skip, m_log_dt, m_w_glu, m_b_glu, m_q_norm, m_k_norm, m_norm_out_ssm, m_norm_out_attn, m_w_out, m_norm_ffn, m_w_up, m_conv_w, m_conv_b, m_w_down, v_norm_mix, v_w_in, v_b_forget, v_lam_re, v_lam_im, v_b_re, v_b_im, v_c_re, v_c_im, v_d_skip, v_log_dt, v_w_glu, v_b_glu, v_q_norm, v_k_norm, v_norm_out_ssm, v_norm_out_attn, v_w_out, v_norm_ffn, v_w_up, v_conv_w, v_conv_b, v_w_down):
    given = dict(locals())
    weights = {k: given[k] for k in WEIGHT_NAMES}
    mom1 = {k: given["m_" + k] for k in WEIGHT_NAMES}
    mom2 = {k: given["v_" + k] for k in WEIGHT_NAMES}
    n_seq, seq_len, _ = x.shape
    n = n_seq * seq_len
    xf = x.reshape(n, D_MODEL)
    target = loss_target.reshape(n, D_MODEL)
    me_idx = 4 * lax.axis_index("x") + 2 * lax.axis_index("y") + lax.axis_index("c")

    in_flags = [False] * 2
    in_sems = _exchange_start([jnp.swapaxes(w_in[0], 0, 1).astype(BF16), conv_w[0]], in_flags, norm_mix,
                              "gather_in_start", 3, NEAR_PEERS)
    fill_own = lambda got, mine: lax.dynamic_update_index_in_dim(got, mine, me_idx, 0)

    lr3 = lam_re[0].reshape(N_GROUPS, 1, SSM_STATE)
    li3 = lam_im[0].reshape(N_GROUPS, 1, SSM_STATE)
    ldt3 = log_dt[0].reshape(N_GROUPS, 1, 1)
    br_t = jnp.swapaxes(b_re[0], 1, 2)
    bi_t = jnp.swapaxes(b_im[0], 1, 2)
    ab_re, ab_im, bb_re, bb_im = _s5_param_fwd(lr3, li3, ldt3, br_t, bi_t)
    a_re = ab_re.reshape(S5_BLOCKS, 1, S5_STATES)
    a_im = ab_im.reshape(S5_BLOCKS, 1, S5_STATES)
    bbr = _block_diag(bb_re, False).astype(BF16)
    bbi = _block_diag(bb_im, False).astype(BF16)
    cr = _block_diag(c_re[0], True).astype(BF16)
    ci = _block_diag(c_im[0], True).astype(BF16)
    avg = jnp.kron(jnp.eye(N_HEADS, dtype=F32), jnp.full((HEAD_DIM, HEAD_DIM), 1.0 / HEAD_DIM, F32)).astype(BF16)
    qg = jnp.tile(q_norm, (1, N_HEADS))
    kg = jnp.tile(k_norm, (1, N_HEADS))
    row_shards = [w_down[0].astype(BF16), w_out[0].astype(BF16), w_glu[0].astype(BF16)]

    (own_in, own_cw), near = _exchange_wait(in_sems[0], in_sems[1], in_sems[2], in_sems[3], in_flags,
                                            _after_all(a_re, a_im, bbr, bbi, cr, ci, avg, qg, kg, *row_shards),
                                            "gather_in_wait", NEAR_PEERS)
    relay = _relay_start(list(near), "gather_in_relay_start", 6)
    g_in, g_cw = _relay_wait(relay[0], relay[1], relay[2], relay[3], "gather_in_relay_wait")
    g_in = fill_own(g_in, own_in)
    g_cw = fill_own(g_cw, own_cw)
    row_flags = [False] * 3
    r_sems = _exchange_start(row_shards, row_flags, g_in, "gather_rows_start", 0)
    u_sems = _exchange_start([jnp.swapaxes(w_up[0], 0, 1).astype(BF16)], [False], r_sems[4], "gather_up_start", 5)
    norm_mix = norm_mix + u_sems[4][0, 0]
    w_in_p = _pad_to(g_in.reshape(D_IN, D_MODEL), 0, D_IN_PAD)

    hn, u, qkv, raw, fl = _inproj_fwd(xf, norm_mix, w_in_p, avg, qg, kg)
    yc, st_r, st_i, xs_r, xs_i = _s5_fwd(u, a_re, a_im, bbr, bbi, cr, ci, d_skip, n_seq)
    bf = _pad_to(b_forget, 1, LANES)
    cum = _fprep_fwd(fl, bf, n_seq)
    cum8 = jnp.swapaxes(cum[:, :N_HEADS].reshape(n_seq, seq_len, N_HEADS), 1, 2)
    ck = cum8[:, :, None, :]
    ya, lse = _attn_fwd(qkv, cum, ck, n_seq)
    own_rows, got_rows = _exchange_wait(r_sems[0], r_sems[1], r_sems[2], r_sems[3], row_flags, ya, "gather_rows_wait")
    g_down, g_out, g_glu = [fill_own(g, o) for o, g in zip(own_rows, got_rows)]
    w_glu_f = g_glu.reshape(D_SSM, D_SSM)
    w_out_f = g_out.reshape(D_MODEL, D_MODEL)
    conv_st = _pad_to(jnp.concatenate([g_cw, conv_b.reshape(N_DEV, 1, -1)], axis=1), 1, SUBLANES)
    w_down4 = g_down.reshape(FFN_GROUPS, FFN_GROUP, D_MODEL)
    ys = _glu_fwd(yc, w_glu_f, b_glu)
    h1, hn2, mixed = _mix_fwd(xf, ys, ya, norm_out_ssm, norm_out_attn, w_out_f, norm_ffn)
    (own_up,), (g_up,) = _exchange_wait(u_sems[0], u_sems[1], u_sems[2], u_sems[3], [False], _after_all(h1, hn2),
                                        "gather_up_wait")
    g_up = fill_own(g_up, own_up)
    ug, uv, pg, pv, dy, dy_b, loss_part = _ffn_fwd(hn2, h1, target, g_up, conv_st, w_down4, seq_len)
    loss_local = 0.5 * jnp.sum(loss_part) / D_MODEL

    dug, duv, act, dhn2, dcg, dcv = _ffn_bwd(dy_b, ug, uv, pg, pv, g_up, conv_st, w_down4, seq_len)
    dh1, dys, dya, d_gs, d_ga, d_gf = _mix_bwd(dy, dhn2[None], h1, ys, ya, norm_out_ssm, norm_out_attn, w_out_f, norm_ffn)
    dyc, gl_b, dz_b, d_bglu = _glu_bwd(yc, dys, w_glu_f, b_glu)

    gw_glu = _tn_matmul(gl_b, dz_b, "dw_glu", D_SSM, D_SSM, out_dtype=BF16, tn=n)
    gw_out = _tn_matmul(mixed, dh1, "dw_out", D_MODEL, D_MODEL, out_dtype=BF16, tn=n // 2)
    gw_up = jnp.concatenate([_tn_grouped(dug, hn2, "dw_up_gate", False, BF16, tn=n),
                             _tn_grouped(duv, hn2, "dw_up_val", False, BF16, tn=n)], axis=0)
    gw_down = _tn_grouped(act, dy_b, "dw_down", False, BF16, tn=n)
    g_conv = jnp.concatenate([dcg, dcv], axis=0)
    by_cols = lambda g, c: jnp.swapaxes(g.reshape(g.shape[0], N_DEV, c), 0, 1)
    early_flags = [True] * 4
    early_names = ("w_down", "w_out", "w_glu", "w_up")
    g_sems = _exchange_start(
        [gw_down.reshape(N_DEV, -1, D_MODEL), gw_out.reshape(N_DEV, -1, D_MODEL), gw_glu.reshape(N_DEV, -1, D_SSM), gw_up],
        early_flags, dyc, "grad_early_start", 1)
    started = g_sems[4][0, 0]

    du, dbbr, dbbi, dcr, dci, dar, dai, ddk = _s5_bwd(u, dyc, xs_r, xs_i, st_r, st_i, a_re, a_im, bbr, bbi, cr, ci,
                                                      d_skip + started, n_seq)
    partial_early = {
        "ab_re": jnp.sum(dar, axis=1), "ab_im": jnp.sum(dai, axis=1),
        "bb_re": _block_diag_extract(dbbr, False), "bb_im": _block_diag_extract(dbbi, False),
        "c_re": _block_diag_extract(dcr, True), "c_im": _block_diag_extract(dci, True),
        "d_skip": jnp.sum(ddk, axis=1), "b_glu": d_bglu,
        "norm_out_ssm": d_gs, "norm_out_attn": d_ga, "norm_ffn": d_gf, "conv_b": g_conv[:, 3],
    }
    early_keys = tuple(partial_early)
    early_shapes = [partial_early[k].shape for k in early_keys]
    p_sems = _exchange_start([_pack([partial_early[k] for k in early_keys])], [False], du, "small_early_start", 2)
    started = started + p_sems[4][0, 0]

    dqn, dkn, dv, dcq, dck = _attn_bwd(qkv, cum, ck + started, ya, dya, lse, n_seq)
    dcum8 = dcq[:, :, :, 0] + dck.reshape(n_seq, N_HEADS, seq_len)
    dcum = _pad_to(jnp.swapaxes(dcum8, 1, 2).reshape(n, N_HEADS), 1, LANES)
    dfl, dbf = _fprep_bwd(dcum, fl, bf, n_seq)
    dx, dproj, d_gmix, d_qg, d_kg = _inproj_bwd(xf, norm_mix, w_in_p, avg, qg, kg, raw, du, dqn, dkn, dv, dfl, dh1)

    gw_in = _tn_matmul(dproj, hn, "dw_in", D_IN_PAD, D_MODEL, out_rows=D_IN, out_dtype=BF16, tn=n // 4)
    partial_late = {
        "norm_mix": d_gmix, "b_forget": jnp.sum(dbf, axis=(0, 1))[:N_HEADS],
        "q_norm": jnp.sum(d_qg.reshape(N_HEADS, HEAD_DIM), axis=0),
        "k_norm": jnp.sum(d_kg.reshape(N_HEADS, HEAD_DIM), axis=0), "loss": loss_local.reshape(1),
    }
    late_keys = tuple(partial_late)
    late_shapes = [partial_late[k].shape for k in late_keys]

    late_flags = [True, True, False]
    l_sems = _exchange_start(
        [gw_in.reshape(N_DEV, D_IN // N_DEV, D_MODEL).astype(BF16), g_conv[:, :3],
         _pack([partial_late[k] for k in late_keys])],
        late_flags, dx, "grad_late_start", 4)
    early_src, early_land = _exchange_wait(g_sems[0], g_sems[1], g_sems[2], g_sems[3], early_flags, l_sems[4],
                                           "grad_early_wait")
    land = dict(zip(early_names, early_land))
    land_up = land["w_up"]
    own = {k: lax.dynamic_index_in_dim(s, me_idx, 0, keepdims=False) for k, s in zip(early_names, early_src)}
    grads, deltas, new_m, new_v = {}, {}, {}, {}

    def adam_shard(name):
        flip = (lambda a: jnp.swapaxes(a, 1, 2)) if name in ("w_in", "w_up") else (lambda a: a)
        outs = _adam_sharded(land[name], own[name], flip(weights[name]), flip(mom1[name]), flip(mom2[name]),
                             "adam_" + name, ADAM_TILE[name])
        grads[name], deltas[name], new_m[name], new_v[name] = [flip(o) for o in outs]

    for name in ("w_up", "w_down", "w_out", "w_glu"):
        adam_shard(name)
    (own_pack,), (early_parts,) = _exchange_wait(p_sems[0], p_sems[1], p_sems[2], p_sems[3], [False], land_up,
                                                 "small_early_wait")
    early_sum = _sum_partials(early_parts, own_pack, "sum_early_partials")
    (src_in, src_cw, own_late), (land["w_in"], land["conv_w"], late_parts) = _exchange_wait(
        l_sems[0], l_sems[1], l_sems[2], l_sems[3], late_flags,
        _after_all(early_sum, *[new_v[k] for k in ("w_up", "w_down", "w_out", "w_glu")]), "grad_late_wait")
    own["w_in"] = lax.dynamic_index_in_dim(src_in, me_idx, 0, keepdims=False)
    own["conv_w"] = lax.dynamic_index_in_dim(src_cw, me_idx, 0, keepdims=False)
    for name in ("w_in", "conv_w"):
        adam_shard(name)

    summed = dict(zip(late_keys, _unpack(_sum_partials(late_parts, own_late, "sum_late_partials"), late_shapes)))
    summed.update(zip(early_keys, _unpack(early_sum, early_shapes)))
    dlr, dli, dldt, dbr_t, dbi_t = _s5_param_bwd(
        lr3, li3, ldt3, br_t, bi_t, summed["ab_re"].reshape(lr3.shape), summed["ab_im"].reshape(lr3.shape),
        summed["bb_re"], summed["bb_im"])
    small_grads = {
        "norm_mix": summed["norm_mix"], "b_forget": summed["b_forget"], "lam_re": dlr, "lam_im": dli,
        "b_re": dbr_t, "b_im": dbi_t, "c_re": summed["c_re"], "c_im": summed["c_im"],
        "d_skip": summed["d_skip"], "log_dt": dldt, "b_glu": summed["b_glu"], "q_norm": summed["q_norm"],
        "k_norm": summed["k_norm"], "norm_out_ssm": summed["norm_out_ssm"], "norm_out_attn": summed["norm_out_attn"],
        "norm_ffn": summed["norm_ffn"], "conv_b": summed["conv_b"],
    }
    repl = tuple(k for k in WEIGHT_NAMES if k not in SHARDED)
    turn = lambda k, a: jnp.swapaxes(a, 2, 3) if k in ("b_re", "b_im") else a
    w_list = [turn(k, weights[k]) for k in repl]
    g_list = [small_grads[k].reshape(w.shape) for k, w in zip(repl, w_list)]
    d_list, m_list, v_list = _adam_replicated(g_list, w_list, [turn(k, mom1[k]) for k in repl],
                                              [turn(k, mom2[k]) for k in repl], "adam_replicated")
    for k, g, d, nm, nv in zip(repl, g_list, d_list, m_list, v_list):
        grads[k], deltas[k], new_m[k], new_v[k] = turn(k, g), turn(k, d), turn(k, nm), turn(k, nv)

    grad_x = dx.reshape(x.shape)
    loss = summed["loss"].reshape(())
    return (loss, grad_x, *[grads[k] for k in WEIGHT_NAMES], *[deltas[k] for k in WEIGHT_NAMES],
            *[new_m[k] for k in WEIGHT_NAMES], *[new_v[k] for k in WEIGHT_NAMES])
```

```python
import functools
import math

import jax
import jax.numpy as jnp
from jax import lax
from jax.experimental import pallas as pl
from jax.experimental.pallas import tpu as pltpu

F32 = jnp.float32
BF16 = jnp.bfloat16
HIGHEST = lax.Precision.HIGHEST

N_DEV = 8
D_MODEL = 1024
D_SSM = 512
D_ATTN = 512
N_HEADS = 8
HEAD_DIM = 64
N_GROUPS = 32
SSM_GROUP = 16
SSM_STATE = 64
D_FF = 2752
D_IN = 2056
D_IN_PAD = 2176
EPS = 1e-6
LANES = 128
SUBLANES = 8
VMEM_LIMIT = 56 * 1024 * 1024

ADAM_LR = 0.001
ADAM_B1 = 0.9
ADAM_B2 = 0.999
ADAM_EPS = 1e-08
ADAM_WD = 0.01
ADAM_STEP = 10


def _cparams(*sem):
    return pltpu.CompilerParams(dimension_semantics=sem, vmem_limit_bytes=VMEM_LIMIT)


def _dot(a, b, **kw):
    return jnp.dot(a, b, preferred_element_type=F32, **kw)


def _dot_nt(a, b):
    return lax.dot_general(a, b, (((1,), (1,)), ((), ())), preferred_element_type=F32)


def _dot_tn(a, b):
    return lax.dot_general(a, b, (((0,), (0,)), ((), ())), preferred_element_type=F32)


def _rms(x, g):
    return x * lax.rsqrt(jnp.mean(x * x, axis=-1, keepdims=True) + EPS) * g


def _split_dot(x, avg):
    hi = x.astype(BF16)
    lo = (x - hi.astype(F32)).astype(BF16)
    return _dot(hi, avg) + _dot(lo, avg)


@jax.custom_vjp
def _group_mean(x, avg):
    return _split_dot(x, avg)


def _group_mean_fwd(x, avg):
    return _split_dot(x, avg), avg


def _group_mean_bwd(avg, ct):
    return _split_dot(ct, avg), jnp.zeros_like(avg)


_group_mean.defvjp(_group_mean_fwd, _group_mean_bwd)


def _headnorm(q, avg, g):
    return q * lax.rsqrt(_group_mean(q * q, avg) + EPS) * g


ALL_PEERS = tuple(range(1, N_DEV))
NEAR_PEERS = (1, 2, 4, 6)
RELAYED = (2, 4, 6)


def _peer_list(js=ALL_PEERS):
    x, y, c = lax.axis_index("x"), lax.axis_index("y"), lax.axis_index("c")
    peers = []
    for j in js:
        px = 1 - x if (j >> 2) & 1 else x
        py = 1 - y if (j >> 1) & 1 else y
        pc = 1 - c if j & 1 else c
        peers.append(((px, py, pc), 4 * px + 2 * py + pc))
    return 4 * x + 2 * y + c, peers


def _split_copies(src, land, send_sems, recv_sems, scatter_flags, me, peers, incoming):
    copies = []
    for k in range(len(src)):
        for j, (pid, pidx) in enumerate(peers):
            s = src[k].at[pidx] if scatter_flags[k] else src[k]
            i = k * len(peers) + j
            copies.append(pltpu.make_async_remote_copy(
                src_ref=s, dst_ref=land[k].at[pidx if incoming else me], send_sem=send_sems[i],
                recv_sem=recv_sems[i], device_id=pid, device_id_type=pl.DeviceIdType.MESH))
    return copies


def _handshake(peers):
    barrier = pltpu.get_barrier_semaphore()
    for pid, _ in peers:
        pl.semaphore_signal(barrier, inc=1, device_id=pid, device_id_type=pl.DeviceIdType.MESH)
    pl.semaphore_wait(barrier, len(peers))


def _exchange_start(srcs, scatter_flags, after, name, collective_id, peer_js=ALL_PEERS):
    n = len(srcs)
    ns = n * len(peer_js)
    hbm = pl.BlockSpec(memory_space=pltpu.HBM)
    sem = pl.BlockSpec(memory_space=pltpu.SEMAPHORE)
    land_shapes = [s.shape if sc else (N_DEV,) + s.shape for s, sc in zip(srcs, scatter_flags)]

    def body(*refs):
        src, land = refs[:n], refs[n:2 * n]
        send_sems = refs[2 * n + 1:2 * n + 1 + ns]
        recv_sems = refs[2 * n + 1 + ns:2 * n + 1 + 2 * ns]
        token = refs[4 * n + 1 + 2 * ns]
        me, peers = _peer_list(peer_js)
        _handshake(peers)
        for cp in _split_copies(src, land, send_sems, recv_sems, scatter_flags, me, peers, False):
            cp.start()
        token[...] = jnp.zeros_like(token)

    outs = pl.pallas_call(
        body, name=name,
        out_shape=(*[pltpu.SemaphoreType.DMA(())] * (2 * ns), *[pltpu.HBM(s.shape, s.dtype) for s in srcs],
                   *[pltpu.HBM(shp, s.dtype) for shp, s in zip(land_shapes, srcs)],
                   jax.ShapeDtypeStruct((SUBLANES, LANES), F32)),
        in_specs=[hbm] * (2 * n) + [pl.BlockSpec(memory_space=pl.ANY)],
        out_specs=(*[sem] * (2 * ns), *[hbm] * (2 * n), pl.BlockSpec(memory_space=pltpu.VMEM)),
        input_output_aliases={i: 2 * ns + i for i in range(2 * n)},
        compiler_params=pltpu.CompilerParams(has_side_effects=pltpu.SideEffectType.DATAFLOW_SIDE_EFFECTING,
                                             collective_id=collective_id),
    )(*[pltpu.with_memory_space_constraint(s, pltpu.HBM) for s in srcs],
      *[pltpu.with_memory_space_constraint(lax.empty(shp, s.dtype), pltpu.HBM) for shp, s in zip(land_shapes, srcs)],
      after)
    return (outs[:ns], outs[ns:2 * ns], outs[2 * ns:2 * ns + n], outs[2 * ns + n:2 * ns + 2 * n], outs[2 * ns + 2 * n])


def _exchange_wait(send_sems, recv_sems, srcs, lands, scatter_flags, after, name, peer_js=ALL_PEERS):
    n = len(srcs)
    ns = n * len(peer_js)
    hbm = pl.BlockSpec(memory_space=pltpu.HBM)
    sem = pl.BlockSpec(memory_space=pltpu.SEMAPHORE)

    def body(*refs):
        src, land = refs[:n], refs[n:2 * n]
        s_sems = refs[2 * n:2 * n + ns]
        r_sems = refs[2 * n + ns:2 * n + 2 * ns]
        me, peers = _peer_list(peer_js)
        for cp in _split_copies(src, land, s_sems, r_sems, scatter_flags, me, peers, True):
            cp.wait_send()
            cp.wait_recv()

    outs = pl.pallas_call(
        body, name=name,
        out_shape=tuple(pltpu.HBM(a.shape, a.dtype) for a in (*srcs, *lands)),
        in_specs=[hbm] * (2 * n) + [sem] * (2 * ns) + [pl.BlockSpec(memory_space=pl.ANY)],
        out_specs=tuple([hbm] * (2 * n)),
        input_output_aliases={i: i for i in range(2 * n)},
        compiler_params=pltpu.CompilerParams(has_side_effects=pltpu.SideEffectType.DATAFLOW_SIDE_EFFECTING),
    )(*srcs, *lands, *send_sems, *recv_sems, after)
    return outs[:n], outs[n:]


def _relay_copies(land, send_sems, recv_sems, incoming):
    _, ((sibling, _),) = _peer_list((1,))
    _, heard = _peer_list(RELAYED)
    _, sibling_heard = _peer_list(tuple(j ^ 1 for j in RELAYED))
    copies = []
    for k in range(len(land)):
        for j in range(len(RELAYED)):
            slot = (sibling_heard if incoming else heard)[j][1]
            i = k * len(RELAYED) + j
            copies.append(pltpu.make_async_remote_copy(
                src_ref=land[k].at[slot], dst_ref=land[k].at[slot], send_sem=send_sems[i], recv_sem=recv_sems[i],
                device_id=sibling, device_id_type=pl.DeviceIdType.MESH))
    return copies


def _relay_start(lands, name, collective_id):
    n = len(lands)
    ns = n * len(RELAYED)
    hbm = pl.BlockSpec(memory_space=pltpu.HBM)
    sem = pl.BlockSpec(memory_space=pltpu.SEMAPHORE)

    def body(*refs):
        land = refs[:n]
        send_sems = refs[n:n + ns]
        recv_sems = refs[n + ns:n + 2 * ns]
        token = refs[2 * n + 2 * ns]
        _handshake(_peer_list((1,))[1])
        for cp in _relay_copies(land, send_sems, recv_sems, False):
            cp.start()
        token[...] = jnp.zeros_like(token)

    outs = pl.pallas_call(
        body, name=name,
        out_shape=(*[pltpu.SemaphoreType.DMA(())] * (2 * ns), *[pltpu.HBM(a.shape, a.dtype) for a in lands],
                   jax.ShapeDtypeStruct((SUBLANES, LANES), F32)),
        in_specs=[hbm] * n,
        out_specs=(*[sem] * (2 * ns), *[hbm] * n, pl.BlockSpec(memory_space=pltpu.VMEM)),
        input_output_aliases={i: 2 * ns + i for i in range(n)},
        compiler_params=pltpu.CompilerParams(has_side_effects=pltpu.SideEffectType.DATAFLOW_SIDE_EFFECTING,
                                             collective_id=collective_id),
    )(*lands)
    return outs[:ns], outs[ns:2 * ns], outs[2 * ns:2 * ns + n], outs[2 * ns + n]


def _relay_wait(send_sems, recv_sems, lands, after, name):
    n = len(lands)
    ns = n * len(RELAYED)
    hbm = pl.BlockSpec(memory_space=pltpu.HBM)
    sem = pl.BlockSpec(memory_space=pltpu.SEMAPHORE)

    def body(*refs):
        land = refs[:n]
        for cp in _relay_copies(land, refs[n:n + ns], refs[n + ns:n + 2 * ns], True):
            cp.wait_send()
            cp.wait_recv()

    return pl.pallas_call(
        body, name=name,
        out_shape=tuple(pltpu.HBM(a.shape, a.dtype) for a in lands),
        in_specs=[hbm] * n + [sem] * (2 * ns) + [pl.BlockSpec(memory_space=pl.ANY)],
        out_specs=tuple([hbm] * n),
        input_output_aliases={i: i for i in range(n)},
        compiler_params=pltpu.CompilerParams(has_side_effects=pltpu.SideEffectType.DATAFLOW_SIDE_EFFECTING),
    )(*lands, *send_sems, *recv_sems, after)


def _tn_matmul(a, b, name, tk, tm, out_rows=None, out_cols=None, out_dtype=F32, tn=512):
    n_tok, k_dim = a.shape
    m_dim = b.shape[1]
    grid = (k_dim // tk, m_dim // tm, n_tok // tn)

    def body(a_ref, b_ref, o_ref, acc):
        k = pl.program_id(2)
        part = _dot_tn(a_ref[...].astype(BF16), b_ref[...].astype(BF16))

        @pl.when(k == 0)
        def _():
            acc[...] = part

        @pl.when(k > 0)
        def _():
            acc[...] += part

        @pl.when(k == grid[2] - 1)
        def _():
            o_ref[...] = acc[...].astype(out_dtype)

    return pl.pallas_call(
        body, name=name, grid=grid,
        in_specs=[pl.BlockSpec((tn, tk), lambda i, j, k: (k, i)), pl.BlockSpec((tn, tm), lambda i, j, k: (k, j))],
        out_specs=pl.BlockSpec((tk, tm), lambda i, j, k: (i, j)),
        out_shape=jax.ShapeDtypeStruct((out_rows or k_dim, out_cols or m_dim), out_dtype),
        scratch_shapes=[pltpu.VMEM((tk, tm), F32)],
        compiler_params=_cparams("parallel", "parallel", "arbitrary"),
    )(a, b)


def _adam_math(g, w, m, v):
    m = ADAM_B1 * m + (1.0 - ADAM_B1) * g
    v = ADAM_B2 * v + (1.0 - ADAM_B2) * (g * g)
    m_hat = m / (1.0 - ADAM_B1 ** ADAM_STEP)
    v_hat = v / (1.0 - ADAM_B2 ** ADAM_STEP)
    delta = -ADAM_LR * (m_hat / (jnp.sqrt(v_hat) + ADAM_EPS) + ADAM_WD * w)
    return delta, m, v


def _adam_sharded(land, own, w, m, v, name, tile):
    _, r, c = w.shape

    def body(*refs):
        l_ref = refs[0]
        own_ref = refs[1] if own is not None else None
        w_ref, m_ref, v_ref, g_ref, d_ref, nm_ref, nv_ref = [ref.at[0] for ref in refs[-7:]]
        if own_ref is not None:
            x, y, z = lax.axis_index("x"), lax.axis_index("y"), lax.axis_index("c")
            me = 4 * x + 2 * y + z
            mine = own_ref[...].astype(F32)
        g = None
        for s in range(N_DEV):
            part = l_ref[s].astype(F32)
            if own_ref is not None:
                part = jnp.where(me == s, mine, part)
            g = part if g is None else g + part
        d, nm, nv = _adam_math(g, w_ref[...], m_ref[...], v_ref[...])
        g_ref[...] = g
        d_ref[...] = d
        nm_ref[...] = nm
        nv_ref[...] = nv

    tr, tc = tile
    spec = pl.BlockSpec((1, tr, tc), lambda i, j: (0, i, j))
    own_specs, own_args = ([pl.BlockSpec((tr, tc), lambda i, j: (i, j))], [own]) if own is not None else ([], [])
    return pl.pallas_call(
        body, name=name, grid=(r // tr, c // tc),
        in_specs=[pl.BlockSpec((N_DEV, tr, tc), lambda i, j: (0, i, j)), *own_specs, spec, spec, spec],
        out_specs=(spec, spec, spec, spec),
        out_shape=tuple(jax.ShapeDtypeStruct((1, r, c), F32) for _ in range(4)),
        compiler_params=_cparams("parallel", "parallel"),
    )(land, *own_args, w, m, v)


def _sum_partials(parts, own, name):
    _, r, c = parts.shape

    def body(*refs):
        p_ref, o_ref = refs[0], refs[-1]
        if own is not None:
            x, y, z = lax.axis_index("x"), lax.axis_index("y"), lax.axis_index("c")
            me = 4 * x + 2 * y + z
            mine = refs[1][...]
        g = None
        for s in range(N_DEV):
            part = p_ref[s]
            if own is not None:
                part = jnp.where(me == s, mine, part)
            g = part if g is None else g + part
        o_ref[...] = g

    args = (parts,) if own is None else (parts, own)
    return pl.pallas_call(body, name=name, out_shape=jax.ShapeDtypeStruct((r, c), F32),
                          compiler_params=pltpu.CompilerParams(vmem_limit_bytes=VMEM_LIMIT))(*args)


def _adam_replicated(gs, ws, ms, vs, name):
    k = len(ws)

    def body(*refs):
        outs = refs[4 * k:]
        for i in range(k):
            d, nm, nv = _adam_math(refs[i][...], refs[k + i][...], refs[2 * k + i][...], refs[3 * k + i][...])
            outs[i][...] = d
            outs[k + i][...] = nm
            outs[2 * k + i][...] = nv

    outs = pl.pallas_call(body, name=name, out_shape=tuple(jax.ShapeDtypeStruct(w.shape, F32) for w in ws) * 3,
                          compiler_params=pltpu.CompilerParams(vmem_limit_bytes=VMEM_LIMIT))(*gs, *ws, *ms, *vs)
    return outs[:k], outs[k:2 * k], outs[2 * k:]


def _inproj_fwd(x, g, w_in, avg, qg, kg, tm=512):
    n = x.shape[0]

    def body(x_ref, g_ref, w_ref, a_ref, qg_ref, kg_ref, hn_ref, u_ref, qkv_ref, raw_ref, fl_ref):
        hn = _rms(x_ref[...], g_ref[...]).astype(BF16)
        hn_ref[...] = hn
        proj = _dot_nt(hn, w_ref[...])
        u_ref[...] = proj[:, 0:512]
        q = proj[:, 512:1024]
        k = proj[:, 1024:1536]
        raw_ref[:, 0:512] = q
        raw_ref[:, 512:1024] = k
        qkv_ref[:, 0:512] = _headnorm(q, a_ref[...], qg_ref[...]).astype(BF16)
        qkv_ref[:, 512:1024] = _headnorm(k, a_ref[...], kg_ref[...]).astype(BF16)
        qkv_ref[:, 1024:1536] = proj[:, 1536:2048].astype(BF16)
        fl_ref[...] = proj[:, 2048:D_IN_PAD]

    row = lambda w: pl.BlockSpec((tm, w), lambda i: (i, 0))
    full = lambda a: pl.BlockSpec(a.shape, lambda i: (0,) * a.ndim)
    return pl.pallas_call(
        body, name="inproj_fwd", grid=(n // tm,),
        in_specs=[row(D_MODEL), full(g), full(w_in), full(avg), full(qg), full(kg)],
        out_specs=(row(D_MODEL), row(512), row(1536), row(1024), row(LANES)),
        out_shape=(jax.ShapeDtypeStruct((n, D_MODEL), BF16), jax.ShapeDtypeStruct((n, 512), F32),
                   jax.ShapeDtypeStruct((n, 1536), BF16), jax.ShapeDtypeStruct((n, 1024), F32),
                   jax.ShapeDtypeStruct((n, LANES), F32)),
        compiler_params=_cparams("parallel"),
    )(x, g, w_in, avg, qg, kg)


def _inproj_bwd(x, g, w_in, avg, qg, kg, raw, du, dqn, dkn, dv, dfl, dres, tm=512):
    n = x.shape[0]

    def body(x_ref, g_ref, w_ref, a_ref, qg_ref, kg_ref, raw_ref, du_ref, dqn_ref, dkn_ref, dv_ref, dfl_ref, dres_ref,
             dx_ref, dproj_ref, dg_ref, dqg_ref, dkg_ref):
        @pl.when(pl.program_id(0) == 0)
        def _():
            dg_ref[...] = jnp.zeros_like(dg_ref)
            dqg_ref[...] = jnp.zeros_like(dqg_ref)
            dkg_ref[...] = jnp.zeros_like(dkg_ref)
        avg_m = a_ref[...]
        _, vjp_q = jax.vjp(lambda q, gg: _headnorm(q, avg_m, gg), raw_ref[:, 0:512], qg_ref[...])
        dq, dqg = vjp_q(dqn_ref[...])
        _, vjp_k = jax.vjp(lambda k, gg: _headnorm(k, avg_m, gg), raw_ref[:, 512:1024], kg_ref[...])
        dk, dkg = vjp_k(dkn_ref[...])
        dproj = jnp.concatenate([du_ref[...], dq, dk, dv_ref[...], dfl_ref[...]], axis=1).astype(BF16)
        dproj_ref[...] = dproj
        dhn = _dot(dproj, w_ref[...])
        _, vjp_x = jax.vjp(_rms, x_ref[...], g_ref[...])
        dxn, dg = vjp_x(dhn)
        dx_ref[...] = dxn + dres_ref[...]
        dg_ref[...] += dg
        dqg_ref[...] += dqg
        dkg_ref[...] += dkg

    row = lambda w: pl.BlockSpec((tm, w), lambda i: (i, 0))
    full = lambda a: pl.BlockSpec(a.shape, lambda i: (0,) * a.ndim)
    vec = lambda w: pl.BlockSpec((1, w), lambda i: (0, 0))
    return pl.pallas_call(
        body, name="inproj_bwd", grid=(n // tm,),
        in_specs=[row(D_MODEL), full(g), full(w_in), full(avg), full(qg), full(kg), row(1024), row(512), row(512),
                  row(512), row(512), row(LANES), row(D_MODEL)],
        out_specs=(row(D_MODEL), row(D_IN_PAD), vec(D_MODEL), vec(512), vec(512)),
        out_shape=(jax.ShapeDtypeStruct((n, D_MODEL), F32), jax.ShapeDtypeStruct((n, D_IN_PAD), BF16),
                   jax.ShapeDtypeStruct((1, D_MODEL), F32), jax.ShapeDtypeStruct((1, 512), F32),
                   jax.ShapeDtypeStruct((1, 512), F32)),
        compiler_params=_cparams("arbitrary"),
    )(x, g, w_in, avg, qg, kg, raw, du, dqn, dkn, dv, dfl, dres)


def _glu_fwd(yc, wg, bg, tm=512):
    n = yc.shape[0]

    def body(yc_ref, w_ref, b_ref, ys_ref):
        gl = jax.nn.gelu(yc_ref[...])
        z = _dot(gl.astype(BF16), w_ref[...]) + b_ref[...]
        ys_ref[...] = gl * jax.nn.sigmoid(z)

    row = pl.BlockSpec((tm, 512), lambda i: (i, 0))
    full = lambda a: pl.BlockSpec(a.shape, lambda i: (0,) * a.ndim)
    return pl.pallas_call(
        body, name="glu_fwd", grid=(n // tm,), in_specs=[row, full(wg), full(bg)], out_specs=row,
        out_shape=jax.ShapeDtypeStruct((n, 512), F32), compiler_params=_cparams("parallel"),
    )(yc, wg, bg)


def _glu_bwd(yc, dys, wg, bg, tm=512):
    n = yc.shape[0]

    def body(yc_ref, dys_ref, w_ref, b_ref, dyc_ref, gl_ref, dz_ref, db_ref):
        @pl.when(pl.program_id(0) == 0)
        def _():
            db_ref[...] = jnp.zeros_like(db_ref)
        gl, vjp_gelu = jax.vjp(jax.nn.gelu, yc_ref[...])
        glb = gl.astype(BF16)
        z = _dot(glb, w_ref[...]) + b_ref[...]
        s = jax.nn.sigmoid(z)
        dys = dys_ref[...]
        dz = dys * gl * s * (1.0 - s)
        dzb = dz.astype(BF16)
        dgl = dys * s + _dot_nt(dzb, w_ref[...])
        dyc_ref[...] = vjp_gelu(dgl)[0]
        gl_ref[...] = glb
        dz_ref[...] = dzb
        db_ref[...] += jnp.sum(dz, axis=0, keepdims=True)

    row = pl.BlockSpec((tm, 512), lambda i: (i, 0))
    full = lambda a: pl.BlockSpec(a.shape, lambda i: (0,) * a.ndim)
    return pl.pallas_call(
        body, name="glu_bwd", grid=(n // tm,), in_specs=[row, row, full(wg), full(bg)],
        out_specs=(row, row, row, pl.BlockSpec((1, 512), lambda i: (0, 0))),
        out_shape=(jax.ShapeDtypeStruct((n, 512), F32), jax.ShapeDtypeStruct((n, 512), BF16),
                   jax.ShapeDtypeStruct((n, 512), BF16), jax.ShapeDtypeStruct((1, 512), F32)),
        compiler_params=_cparams("arbitrary"),
    )(yc, dys, wg, bg)


def _mix_fwd(x, ys, ya, gs, ga, wout, gf, tm=512):
    n = x.shape[0]

    def body(x_ref, ys_ref, ya_ref, gs_ref, ga_ref, w_ref, gf_ref, h1_ref, hn2_ref, mixed_ref):
        mixed = jnp.concatenate([_rms(ys_ref[...], gs_ref[...]), _rms(ya_ref[...], ga_ref[...])], axis=1).astype(BF16)
        mixed_ref[...] = mixed
        h1 = x_ref[...] + _dot(mixed, w_ref[...])
        h1_ref[...] = h1
        hn2_ref[...] = _rms(h1, gf_ref[...]).astype(BF16)

    row = lambda w: pl.BlockSpec((tm, w), lambda i: (i, 0))
    full = lambda a: pl.BlockSpec(a.shape, lambda i: (0,) * a.ndim)
    return pl.pallas_call(
        body, name="mix_fwd", grid=(n // tm,),
        in_specs=[row(D_MODEL), row(512), row(512), full(gs), full(ga), full(wout), full(gf)],
        out_specs=(row(D_MODEL), row(D_MODEL), row(D_MODEL)),
        out_shape=(jax.ShapeDtypeStruct((n, D_MODEL), F32), jax.ShapeDtypeStruct((n, D_MODEL), BF16),
                   jax.ShapeDtypeStruct((n, D_MODEL), BF16)),
        compiler_params=_cparams("parallel"),
    )(x, ys, ya, gs, ga, wout, gf)


def _mix_bwd(dy, dhn2_parts, h1, ys, ya, gs, ga, wout, gf, tm=512):
    n = dy.shape[0]
    n_parts = dhn2_parts.shape[0]

    def body(dy_ref, dp_ref, h1_ref, ys_ref, ya_ref, gs_ref, ga_ref, w_ref, gf_ref,
             dh1_ref, dys_ref, dya_ref, dgs_ref, dga_ref, dgf_ref):
        @pl.when(pl.program_id(0) == 0)
        def _():
            dgs_ref[...] = jnp.zeros_like(dgs_ref)
            dga_ref[...] = jnp.zeros_like(dga_ref)
            dgf_ref[...] = jnp.zeros_like(dgf_ref)
        dhn2 = dp_ref[0]
        for p in range(1, n_parts):
            dhn2 = dhn2 + dp_ref[p]
        _, vjp_f = jax.vjp(_rms, h1_ref[...], gf_ref[...])
        dh1n, dgf = vjp_f(dhn2)
        dh1 = dy_ref[...] + dh1n
        dh1_ref[...] = dh1
        dmixed = _dot_nt(dh1.astype(BF16), w_ref[...])
        _, vjp_s = jax.vjp(_rms, ys_ref[...], gs_ref[...])
        dys, dgs = vjp_s(dmixed[:, 0:512])
        _, vjp_a = jax.vjp(_rms, ya_ref[...], ga_ref[...])
        dya, dga = vjp_a(dmixed[:, 512:1024])
        dys_ref[...] = dys
        dya_ref[...] = dya
        dgs_ref[...] += dgs
        dga_ref[...] += dga
        dgf_ref[...] += dgf

    row = lambda w: pl.BlockSpec((tm, w), lambda i: (i, 0))
    full = lambda a: pl.BlockSpec(a.shape, lambda i: (0,) * a.ndim)
    vec = lambda w: pl.BlockSpec((1, w), lambda i: (0, 0))
    return pl.pallas_call(
        body, name="mix_bwd", grid=(n // tm,),
        in_specs=[row(D_MODEL), pl.BlockSpec((n_parts, tm, D_MODEL), lambda i: (0, i, 0)), row(D_MODEL), row(512),
                  row(512), full(gs), full(ga), full(wout), full(gf)],
        out_specs=(row(D_MODEL), row(512), row(512), vec(512), vec(512), vec(D_MODEL)),
        out_shape=(jax.ShapeDtypeStruct((n, D_MODEL), F32), jax.ShapeDtypeStruct((n, 512), F32),
                   jax.ShapeDtypeStruct((n, 512), F32), jax.ShapeDtypeStruct((1, 512), F32),
                   jax.ShapeDtypeStruct((1, 512), F32), jax.ShapeDtypeStruct((1, D_MODEL), F32)),
        compiler_params=_cparams("arbitrary"),
    )(dy, dhn2_parts, h1, ys, ya, gs, ga, wout, gf)


HALO = 16
FFN_GROUPS = 4
FFN_GROUP = D_FF // FFN_GROUPS


def _conv3(ue, cw):
    return cw[2:3] * ue + cw[1:2] * pltpu.roll(ue, 1, 0) + cw[0:1] * pltpu.roll(ue, 2, 0) + cw[3:4]


def _ffn_weight_specs():
    gate = lambda i, j: (j, 0, 0)
    val = lambda i, j: (j + FFN_GROUPS, 0, 0)
    w_blk, c_blk = (1, FFN_GROUP, D_MODEL), (1, SUBLANES, FFN_GROUP)
    return [pl.BlockSpec(w_blk, gate), pl.BlockSpec(w_blk, val), pl.BlockSpec(c_blk, gate), pl.BlockSpec(c_blk, val),
            pl.BlockSpec((1, FFN_GROUP, D_MODEL), gate)]


def _ffn_fwd(hn2, h1, target, w_up, conv, w_down, seq_len, tm=512):
    n = hn2.shape[0]
    nj = FFN_GROUPS
    hb = tm // HALO

    def body(hn_ref, halo_ref, h1_ref, tgt_ref, wg_ref, wv_ref, cg_ref, cv_ref, wd_ref,
             ug_ref, uv_ref, pg_ref, pv_ref, dy_ref, loss_ref, acc):
        i, j = pl.program_id(0), pl.program_id(1)
        seq_start = (i * tm) % seq_len == 0
        halo = halo_ref[...]
        halo = jnp.where(seq_start, jnp.zeros_like(halo), halo)
        he = jnp.concatenate([halo, hn_ref[...]], axis=0)
        ueg = _dot_nt(he, wg_ref[0])
        uev = _dot_nt(he, wv_ref[0])
        ug_ref[0] = ueg[HALO:].astype(BF16)
        uv_ref[0] = uev[HALO:].astype(BF16)
        cg = _conv3(ueg, cg_ref[0])[HALO:]
        cv = _conv3(uev, cv_ref[0])[HALO:]
        pg_ref[0] = cg.astype(BF16)
        pv_ref[0] = cv.astype(BF16)
        act = (jax.nn.silu(cg) * cv).astype(BF16)
        part = _dot(act, wd_ref[0])

        @pl.when(j == 0)
        def _():
            acc[...] = part

        @pl.when(j > 0)
        def _():
            acc[...] += part

        @pl.when(j == nj - 1)
        def _():
            err = h1_ref[...] + acc[...] - tgt_ref[...]
            dy_ref[...] = err * (1.0 / D_MODEL)
            loss_ref[0] = jnp.sum(err * err, axis=0, keepdims=True)

    row = pl.BlockSpec((tm, D_MODEL), lambda i, j: (i, 0))
    u_main = pl.BlockSpec((1, tm, FFN_GROUP), lambda i, j: (j, i, 0))
    u_shape = jax.ShapeDtypeStruct((FFN_GROUPS, n, FFN_GROUP), BF16)
    return pl.pallas_call(
        body, name="ffn_fwd", grid=(n // tm, nj),
        in_specs=[row, pl.BlockSpec((HALO, D_MODEL), lambda i, j: (jnp.maximum(i * hb - 1, 0), 0)), row, row,
                  *_ffn_weight_specs()],
        out_specs=(u_main, u_main, u_main, u_main, row, pl.BlockSpec((1, 1, D_MODEL), lambda i, j: (i, 0, 0))),
        out_shape=(u_shape, u_shape, u_shape, u_shape, jax.ShapeDtypeStruct((n, D_MODEL), F32),
                   jax.ShapeDtypeStruct((n // tm, 1, D_MODEL), F32)),
        scratch_shapes=[pltpu.VMEM((tm, D_MODEL), F32)],
        compiler_params=_cparams("parallel", "arbitrary"),
    )(hn2, hn2, h1, target, w_up, w_up, conv, conv, w_down)


def _ffn_bwd(dy, ug, uv, pg, pv, w_up, conv, w_down, seq_len, tm=512):
    n = dy.shape[0]
    nj = FFN_GROUPS
    fb = FFN_GROUP
    hb = tm // HALO
    last_hb = n // HALO - 1
    rows = tm + HALO

    def body(dy_ref, dyn_ref, ug_ref, uv_ref, pgm_ref, pgn_ref, pvm_ref, pvn_ref, wg_ref, wv_ref, cg_ref, cv_ref,
             wd_ref, dug_ref, duv_ref, act_ref, dhn_ref, dcg_ref, dcv_ref, acc):
        i, j = pl.program_id(0), pl.program_id(1)
        seq_end = ((i + 1) * tm) % seq_len == 0
        dyn = dyn_ref[...]
        dyn = jnp.where(seq_end, jnp.zeros_like(dyn), dyn)
        d_out = jnp.concatenate([dy_ref[...], dyn], axis=0).astype(BF16)
        d_act = _dot_nt(d_out, wd_ref[0])
        cge = jnp.concatenate([pgm_ref[0], pgn_ref[0]], axis=0).astype(F32)
        cve = jnp.concatenate([pvm_ref[0], pvn_ref[0]], axis=0).astype(F32)
        act, vjp_act = jax.vjp(lambda g, v: jax.nn.silu(g) * v, cge, cve)
        dcge, dcve = vjp_act(d_act)
        act_ref[0] = act[:tm].astype(BF16)

        def conv_t(dc, u_ref, cw):
            ahead1 = pltpu.roll(dc, rows - 1, 0)[:tm]
            ahead2 = pltpu.roll(dc, rows - 2, 0)[:tm]
            here = dc[:tm]
            du = cw[2:3] * here + cw[1:2] * ahead1 + cw[0:1] * ahead2
            u = u_ref[0].astype(F32)
            col = lambda x: jnp.sum(x, axis=0, keepdims=True)
            grad = jnp.concatenate([col(ahead2 * u), col(ahead1 * u), col(here * u), col(here),
                                    jnp.zeros((4, fb), F32)], axis=0)
            return du.astype(BF16), grad

        cwg, cwv = cg_ref[0], cv_ref[0]
        dug, grad_g = conv_t(dcge, ug_ref, cwg)
        duv, grad_v = conv_t(dcve, uv_ref, cwv)
        dug_ref[0] = dug
        duv_ref[0] = duv
        part = _dot(dug, wg_ref[0]) + _dot(duv, wv_ref[0])

        @pl.when(j == 0)
        def _():
            acc[...] = part

        @pl.when(j > 0)
        def _():
            acc[...] += part

        @pl.when(j == nj - 1)
        def _():
            dhn_ref[...] = acc[...]

        @pl.when(i == 0)
        def _():
            dcg_ref[j] = jnp.zeros((8, fb), F32)
            dcv_ref[j] = jnp.zeros((8, fb), F32)

        dcg_ref[j] += grad_g
        dcv_ref[j] += grad_v

    row = pl.BlockSpec((tm, D_MODEL), lambda i, j: (i, 0))
    u_main = pl.BlockSpec((1, tm, fb), lambda i, j: (j, i, 0))
    u_next = pl.BlockSpec((1, HALO, fb), lambda i, j: (j, jnp.minimum((i + 1) * hb, last_hb), 0))
    dc_spec = pl.BlockSpec((nj, 8, fb), lambda i, j: (0, 0, 0))
    u_shape = jax.ShapeDtypeStruct((FFN_GROUPS, n, fb), BF16)
    return pl.pallas_call(
        body, name="ffn_bwd", grid=(n // tm, nj),
        in_specs=[row, pl.BlockSpec((HALO, D_MODEL), lambda i, j: (jnp.minimum((i + 1) * hb, last_hb), 0)),
                  u_main, u_main, u_main, u_next, u_main, u_next, *_ffn_weight_specs()],
        out_specs=(u_main, u_main, u_main, row, dc_spec, dc_spec),
        out_shape=(u_shape, u_shape, u_shape, jax.ShapeDtypeStruct((n, D_MODEL), F32),
                   jax.ShapeDtypeStruct((nj, 8, fb), F32), jax.ShapeDtypeStruct((nj, 8, fb), F32)),
        scratch_shapes=[pltpu.VMEM((tm, D_MODEL), F32)],
        compiler_params=_cparams("arbitrary", "arbitrary"),
    )(dy, dy, ug, uv, pg, pg, pv, pv, w_up, w_up, conv, conv, w_down)


def _tn_grouped(a, b, name, shared_a, out_dtype=F32, tn=1024):
    groups = b.shape[0] if shared_a else a.shape[0]
    n_tok = a.shape[0] if shared_a else b.shape[0]
    k_dim, m_dim = a.shape[-1], b.shape[-1]

    def body(a_ref, b_ref, o_ref, acc):
        k = pl.program_id(1)
        a_t = a_ref[...] if shared_a else a_ref[0]
        b_t = b_ref[0] if shared_a else b_ref[...]
        part = _dot_tn(a_t.astype(BF16), b_t.astype(BF16))

        @pl.when(k == 0)
        def _():
            acc[...] = part

        @pl.when(k > 0)
        def _():
            acc[...] += part

        @pl.when(k == n_tok // tn - 1)
        def _():
            o_ref[0] = acc[...].astype(out_dtype)

    plain = lambda w: pl.BlockSpec((tn, w), lambda g, k: (k, 0))
    grouped = lambda w: pl.BlockSpec((1, tn, w), lambda g, k: (g, k, 0))
    return pl.pallas_call(
        body, name=name, grid=(groups, n_tok // tn),
        in_specs=[plain(k_dim), grouped(m_dim)] if shared_a else [grouped(k_dim), plain(m_dim)],
        out_specs=pl.BlockSpec((1, k_dim, m_dim), lambda g, k: (g, 0, 0)),
        out_shape=jax.ShapeDtypeStruct((groups, k_dim, m_dim), out_dtype),
        scratch_shapes=[pltpu.VMEM((k_dim, m_dim), F32)],
        compiler_params=_cparams("parallel", "arbitrary"),
    )(a, b)


def _s5_param_fn(lr, li, ldt, br, bi):
    dt = jnp.exp(ldt)
    mag = jnp.exp(lr * dt)
    ab_re = mag * jnp.cos(li * dt)
    ab_im = mag * jnp.sin(li * dt)
    nr = ab_re - 1.0
    ni = ab_im
    den = lr * lr + li * li
    q_re = (nr * lr + ni * li) / den
    q_im = (ni * lr - nr * li) / den
    bb_re = q_re * br - q_im * bi
    bb_im = q_re * bi + q_im * br
    return ab_re, ab_im, bb_re, bb_im


def _s5_param_fwd(lr, li, ldt, br, bi):
    def body(lr_ref, li_ref, ldt_ref, br_ref, bi_ref, ar_ref, ai_ref, bbr_ref, bbi_ref):
        ar, ai, bbr, bbi = _s5_param_fn(lr_ref[...], li_ref[...], ldt_ref[...], br_ref[...], bi_ref[...])
        ar_ref[...] = ar
        ai_ref[...] = ai
        bbr_ref[...] = bbr
        bbi_ref[...] = bbi

    return pl.pallas_call(
        body, name="s5_param_fwd",
        out_shape=(jax.ShapeDtypeStruct(lr.shape, F32), jax.ShapeDtypeStruct(lr.shape, F32),
                   jax.ShapeDtypeStruct(br.shape, F32), jax.ShapeDtypeStruct(br.shape, F32)),
    )(lr, li, ldt, br, bi)


def _s5_param_bwd(lr, li, ldt, br, bi, dar, dai, dbbr, dbbi):
    def body(lr_ref, li_ref, ldt_ref, br_ref, bi_ref, dar_ref, dai_ref, dbbr_ref, dbbi_ref,
             dlr_ref, dli_ref, dldt_ref, dbr_ref, dbi_ref):
        _, vjp = jax.vjp(_s5_param_fn, lr_ref[...], li_ref[...], ldt_ref[...], br_ref[...], bi_ref[...])
        dlr, dli, dldt, dbr, dbi = vjp((dar_ref[...], dai_ref[...], dbbr_ref[...], dbbi_ref[...]))
        dlr_ref[...] = dlr
        dli_ref[...] = dli
        dldt_ref[...] = dldt
        dbr_ref[...] = dbr
        dbi_ref[...] = dbi

    return pl.pallas_call(
        body, name="s5_param_bwd",
        out_shape=(jax.ShapeDtypeStruct(lr.shape, F32), jax.ShapeDtypeStruct(lr.shape, F32),
                   jax.ShapeDtypeStruct(ldt.shape, F32), jax.ShapeDtypeStruct(br.shape, F32),
                   jax.ShapeDtypeStruct(br.shape, F32)),
    )(lr, li, ldt, br, bi, dar, dai, dbbr, dbbi)


S5_CHUNK = 1024
S5_STATES = 512
S5_BLOCKS = 4


def _cpow_rows(ar, ai, count):
    rs, im = [ar], [ai]
    for _ in range(count - 1):
        pr, pi = rs[-1], im[-1]
        rs.append(pr * ar - pi * ai)
        im.append(pr * ai + pi * ar)
    return rs, im


def _scan_in_groups(vr, vi, pr, pi, rm, reverse):
    n, width = vr.shape
    vr = vr.reshape(n // SUBLANES, SUBLANES, width)
    vi = vi.reshape(n // SUBLANES, SUBLANES, width)
    row = rm[0:SUBLANES]
    for k in (1, 2, 4):
        shift = SUBLANES - k if reverse else k
        keep = row < SUBLANES - k if reverse else row >= k
        kr = jnp.where(keep, pr[k - 1], 0.0)
        ki = jnp.where(keep, pi[k - 1], 0.0)
        sr, si = pltpu.roll(vr, shift, 1), pltpu.roll(vi, shift, 1)
        vr, vi = vr + kr * sr - ki * si, vi + kr * si + ki * sr
    return vr.reshape(n, width), vi.reshape(n, width)


def _carry_over_groups(xr_s, xi_s, wr, wi, c0r, c0i, reverse):
    groups = xr_s.shape[0] // SUBLANES
    pick = 0 if reverse else SUBLANES - 1

    def step(q, carry):
        cr, ci = carry
        r = groups - 1 - q if reverse else q
        o = pl.multiple_of(r * SUBLANES, SUBLANES)
        vr = xr_s[pl.ds(o, SUBLANES), :]
        vi = xi_s[pl.ds(o, SUBLANES), :]
        nr = vr + wr * cr - wi * ci
        ni = vi + wr * ci + wi * cr
        xr_s[pl.ds(o, SUBLANES), :] = nr
        xi_s[pl.ds(o, SUBLANES), :] = ni
        return (jnp.broadcast_to(nr[pick:pick + 1], nr.shape), jnp.broadcast_to(ni[pick:pick + 1], ni.shape))

    return lax.fori_loop(0, groups, step, (c0r, c0i), unroll=4)


def _s5_state_scan(u_b, bbr, bbi, pr, pi, rm, xr_s, xi_s, c0r, c0i):
    bur = _dot(u_b, bbr)
    bui = _dot(u_b, bbi)
    bur, bui = _scan_in_groups(bur, bui, pr, pi, rm, False)
    xr_s[...] = bur
    xi_s[...] = bui
    w8r = jnp.concatenate(pr, axis=0)
    w8i = jnp.concatenate(pi, axis=0)
    return _carry_over_groups(xr_s, xi_s, w8r, w8i, c0r, c0i, False)


def _s5_fwd(u, a_re, a_im, bbr, bbi, cr, ci, d_skip, n_seq):
    n = u.shape[0]
    seq_len = n // n_seq
    nt = seq_len // S5_CHUNK
    tc = S5_CHUNK

    def body(u_ref, ar_ref, ai_ref, bbr_ref, bbi_ref, cr_ref, ci_ref, d_ref, y_ref, str_ref, sti_ref, xrb_ref, xib_ref,
             xr_s, xi_s, car_r, car_i):
        t = pl.program_id(2)

        @pl.when(t == 0)
        def _():
            car_r[...] = jnp.zeros_like(car_r)
            car_i[...] = jnp.zeros_like(car_i)
        pr, pi = _cpow_rows(ar_ref[0], ai_ref[0], SUBLANES)
        rm = lax.broadcasted_iota(jnp.int32, (tc, S5_STATES), 0) & (SUBLANES - 1)
        str_ref[0, 0] = car_r[...]
        sti_ref[0, 0] = car_i[...]
        u_t = u_ref[...]
        cfr, cfi = _s5_state_scan(u_t.astype(BF16), bbr_ref[0], bbi_ref[0], pr, pi, rm, xr_s, xi_s,
                                  car_r[...], car_i[...])
        car_r[...] = cfr
        car_i[...] = cfi
        xr_b = xr_s[...].astype(BF16)
        xi_b = xi_s[...].astype(BF16)
        xrb_ref[...] = xr_b
        xib_ref[...] = xi_b
        y_ref[...] = _dot(xr_b, cr_ref[0]) - _dot(xi_b, ci_ref[0]) + d_ref[...] * u_t

    x_spec = pl.BlockSpec((tc, S5_STATES), lambda cb, b, t: (b * nt + t, cb))
    x_shape = jax.ShapeDtypeStruct((n, S5_BLOCKS * S5_STATES), BF16)
    u_spec = pl.BlockSpec((tc, LANES), lambda cb, b, t: (b * nt + t, cb))
    a_spec = pl.BlockSpec((1, 1, S5_STATES), lambda cb, b, t: (cb, 0, 0))
    bb_spec = pl.BlockSpec((1, LANES, S5_STATES), lambda cb, b, t: (cb, 0, 0))
    c_spec = pl.BlockSpec((1, S5_STATES, LANES), lambda cb, b, t: (cb, 0, 0))
    st_spec = pl.BlockSpec((1, 1, SUBLANES, S5_STATES), lambda cb, b, t: (cb, b * nt + t, 0, 0))
    st_shape = jax.ShapeDtypeStruct((S5_BLOCKS, n_seq * nt, SUBLANES, S5_STATES), F32)
    return pl.pallas_call(
        body, name="s5_fwd", grid=(S5_BLOCKS, n_seq, nt),
        in_specs=[u_spec, a_spec, a_spec, bb_spec, bb_spec, c_spec, c_spec,
                  pl.BlockSpec((1, LANES), lambda cb, b, t: (0, cb))],
        out_specs=(u_spec, st_spec, st_spec, x_spec, x_spec),
        out_shape=(jax.ShapeDtypeStruct((n, D_SSM), F32), st_shape, st_shape, x_shape, x_shape),
        scratch_shapes=[pltpu.VMEM((tc, S5_STATES), F32), pltpu.VMEM((tc, S5_STATES), F32),
                        pltpu.VMEM((SUBLANES, S5_STATES), F32), pltpu.VMEM((SUBLANES, S5_STATES), F32)],
        compiler_params=_cparams("parallel", "arbitrary", "arbitrary"),
    )(u, a_re, a_im, bbr, bbi, cr, ci, d_skip)


def _s5_bwd(u, dy, xs_r, xs_i, st_r, st_i, a_re, a_im, bbr, bbi, cr, ci, d_skip, n_seq):
    n = u.shape[0]
    seq_len = n // n_seq
    nt = seq_len // S5_CHUNK
    tc = S5_CHUNK

    def body(u_ref, dy_ref, xrb_ref, xib_ref, str_ref, sti_ref, ar_ref, ai_ref, bbr_ref, bbi_ref, cr_ref, ci_ref, d_ref,
             du_ref, dbbr_ref, dbbi_ref, dcr_ref, dci_ref, dar_ref, dai_ref, dd_ref,
             gr_s, gi_s, car_r, car_i):
        b, t = pl.program_id(1), pl.program_id(2)

        @pl.when((b == 0) & (t == 0))
        def _():
            for ref in (dbbr_ref, dbbi_ref, dcr_ref, dci_ref, dar_ref, dai_ref, dd_ref):
                ref[...] = jnp.zeros_like(ref)

        @pl.when(t == 0)
        def _():
            car_r[...] = jnp.zeros_like(car_r)
            car_i[...] = jnp.zeros_like(car_i)
        ar, ai = ar_ref[0], ai_ref[0]
        pr, pi = _cpow_rows(ar, ai, SUBLANES)
        row = lax.broadcasted_iota(jnp.int32, (tc, S5_STATES), 0)
        rm = row & (SUBLANES - 1)
        u_t = u_ref[...]
        u_b = u_t.astype(BF16)
        dy_t = dy_ref[...]
        dy_b = dy_t.astype(BF16)
        s0r, s0i = str_ref[0, 0], sti_ref[0, 0]
        xr_b, xi_b = xrb_ref[...], xib_ref[...]
        xr, xi = xr_b.astype(F32), xi_b.astype(F32)
        gr = _dot_nt(dy_b, cr_ref[0])
        gi = -_dot_nt(dy_b, ci_ref[0])
        npi = [-v for v in pi]
        gr, gi = _scan_in_groups(gr, gi, pr, npi, rm, True)
        gr_s[...] = gr
        gi_s[...] = gi
        w8r = jnp.concatenate(pr[::-1], axis=0)
        w8i = jnp.concatenate(npi[::-1], axis=0)
        cfr, cfi = _carry_over_groups(gr_s, gi_s, w8r, w8i, car_r[...], car_i[...], True)
        car_r[...] = cfr
        car_i[...] = cfi
        gr, gi = gr_s[...], gi_s[...]
        gr_b, gi_b = gr.astype(BF16), gi.astype(BF16)
        du_ref[...] = _dot_nt(gr_b, bbr_ref[0]) + _dot_nt(gi_b, bbi_ref[0]) + d_ref[...] * dy_t
        dbbr_ref[0] += _dot_tn(u_b, gr_b)
        dbbi_ref[0] += _dot_tn(u_b, gi_b)
        dcr_ref[0] += _dot_tn(xr_b, dy_b)
        dci_ref[0] -= _dot_tn(xi_b, dy_b)
        dd_ref[0] += jnp.sum((dy_t * u_t).reshape(tc // SUBLANES, SUBLANES, LANES), axis=0)
        first = row == 0
        xpr = jnp.where(first, jnp.broadcast_to(s0r[0:1], xr.shape), pltpu.roll(xr, 1, 0))
        xpi = jnp.where(first, jnp.broadcast_to(s0i[0:1], xi.shape), pltpu.roll(xi, 1, 0))
        shp = (tc // SUBLANES, SUBLANES, S5_STATES)
        dar_ref[0] += jnp.sum((gr * xpr + gi * xpi).reshape(shp), axis=0)
        dai_ref[0] += jnp.sum((gi * xpr - gr * xpi).reshape(shp), axis=0)

    u_spec = pl.BlockSpec((tc, LANES), lambda cb, b, t: (b * nt + nt - 1 - t, cb))
    a_spec = pl.BlockSpec((1, 1, S5_STATES), lambda cb, b, t: (cb, 0, 0))
    bb_spec = pl.BlockSpec((1, LANES, S5_STATES), lambda cb, b, t: (cb, 0, 0))
    c_spec = pl.BlockSpec((1, S5_STATES, LANES), lambda cb, b, t: (cb, 0, 0))
    st_spec = pl.BlockSpec((1, 1, SUBLANES, S5_STATES), lambda cb, b, t: (cb, b * nt + nt - 1 - t, 0, 0))
    da_spec = pl.BlockSpec((1, SUBLANES, S5_STATES), lambda cb, b, t: (cb, 0, 0))
    dd_spec = pl.BlockSpec((1, SUBLANES, LANES), lambda cb, b, t: (cb, 0, 0))
    big = pltpu.VMEM((tc, S5_STATES), F32)
    small = pltpu.VMEM((SUBLANES, S5_STATES), F32)
    x_spec = pl.BlockSpec((tc, S5_STATES), lambda cb, b, t: (b * nt + nt - 1 - t, cb))
    return pl.pallas_call(
        body, name="s5_bwd", grid=(S5_BLOCKS, n_seq, nt),
        in_specs=[u_spec, u_spec, x_spec, x_spec, st_spec, st_spec, a_spec, a_spec, bb_spec, bb_spec, c_spec, c_spec,
                  pl.BlockSpec((1, LANES), lambda cb, b, t: (0, cb))],
        out_specs=(u_spec, bb_spec, bb_spec, c_spec, c_spec, da_spec, da_spec, dd_spec),
        out_shape=(jax.ShapeDtypeStruct((n, D_SSM), F32),
                   jax.ShapeDtypeStruct((S5_BLOCKS, LANES, S5_STATES), F32),
                   jax.ShapeDtypeStruct((S5_BLOCKS, LANES, S5_STATES), F32),
                   jax.ShapeDtypeStruct((S5_BLOCKS, S5_STATES, LANES), F32),
                   jax.ShapeDtypeStruct((S5_BLOCKS, S5_STATES, LANES), F32),
                   jax.ShapeDtypeStruct((S5_BLOCKS, SUBLANES, S5_STATES), F32),
                   jax.ShapeDtypeStruct((S5_BLOCKS, SUBLANES, S5_STATES), F32),
                   jax.ShapeDtypeStruct((S5_BLOCKS, SUBLANES, LANES), F32)),
        scratch_shapes=[big, big, small, small],
        compiler_params=_cparams("parallel", "arbitrary", "arbitrary"),
    )(u, dy, xs_r, xs_i, st_r, st_i, a_re, a_im, bbr, bbi, cr, ci, d_skip)


CUM_BLOCK = 128


def _tri(lower):
    r = lax.broadcasted_iota(jnp.int32, (CUM_BLOCK, CUM_BLOCK), 0)
    c = lax.broadcasted_iota(jnp.int32, (CUM_BLOCK, CUM_BLOCK), 1)
    return jnp.where(r >= c if lower else r <= c, 1.0, 0.0).astype(F32)


def _fprep_fwd(fl, bf, n_seq):
    n = fl.shape[0]
    seq_len = n // n_seq
    nb = seq_len // CUM_BLOCK

    def body(fl_ref, bf_ref, cum_ref):
        tril = _tri(True)
        carry = jnp.zeros((1, LANES), F32)
        for blk in range(nb):
            rows = slice(blk * CUM_BLOCK, (blk + 1) * CUM_BLOCK)
            lf = jax.nn.log_sigmoid(fl_ref[rows, :] + bf_ref[...])
            cs = jnp.dot(tril, lf, preferred_element_type=F32, precision=HIGHEST) + carry
            cum_ref[rows, :] = cs
            carry = cs[CUM_BLOCK - 1:CUM_BLOCK, :]

    spec = pl.BlockSpec((seq_len, LANES), lambda b: (b, 0))
    return pl.pallas_call(
        body, name="fprep_fwd", grid=(n_seq,), in_specs=[spec, pl.BlockSpec((1, LANES), lambda b: (0, 0))],
        out_specs=spec, out_shape=jax.ShapeDtypeStruct((n, LANES), F32), compiler_params=_cparams("parallel"),
    )(fl, bf)


def _fprep_bwd(dcum, fl, bf, n_seq):
    n = fl.shape[0]
    seq_len = n // n_seq
    nb = seq_len // CUM_BLOCK

    def body(dcum_ref, fl_ref, bf_ref, dfl_ref, dbf_ref):
        triu = _tri(False)
        lane = lax.broadcasted_iota(jnp.int32, (CUM_BLOCK, LANES), 1)
        carry = jnp.zeros((1, LANES), F32)
        total = jnp.zeros((1, LANES), F32)
        for blk in reversed(range(nb)):
            rows = slice(blk * CUM_BLOCK, (blk + 1) * CUM_BLOCK)
            rs = jnp.dot(triu, dcum_ref[rows, :], preferred_element_type=F32, precision=HIGHEST) + carry
            carry = rs[0:1, :]
            _, vjp = jax.vjp(jax.nn.log_sigmoid, fl_ref[rows, :] + bf_ref[...])
            dz = jnp.where(lane < N_HEADS, vjp(rs)[0], 0.0)
            dfl_ref[rows, :] = dz
            total = total + jnp.sum(dz, axis=0, keepdims=True)
        dbf_ref[0] = total

    spec = pl.BlockSpec((seq_len, LANES), lambda b: (b, 0))
    return pl.pallas_call(
        body, name="fprep_bwd", grid=(n_seq,), in_specs=[spec, spec, pl.BlockSpec((1, LANES), lambda b: (0, 0))],
        out_specs=(spec, pl.BlockSpec((1, 1, LANES), lambda b: (b, 0, 0))),
        out_shape=(jax.ShapeDtypeStruct((n, LANES), F32), jax.ShapeDtypeStruct((n_seq, 1, LANES), F32)),
        compiler_params=_cparams("parallel"),
    )(dcum, fl, bf)


ATT_TQ = 256
ATT_KSTEP = 256
ATT_PAIRS = 2
ATT_PAIRS_FWD = 4
ATT_SCALE = HEAD_DIM ** -0.5
NEG_BIG = -1e30


assert ATT_KSTEP == ATT_TQ


def _scores(q_scaled, kb, row_bias, ck, kend):
    s = _dot_nt(q_scaled, kb) - ck
    if row_bias is not None:
        s = s + row_bias
    r = lax.broadcasted_iota(jnp.int32, (ATT_TQ, ATT_TQ), 0)
    c = lax.broadcasted_iota(jnp.int32, (ATT_TQ, ATT_TQ), 1)
    diag = jnp.where(r >= c, s[:, kend - ATT_TQ:], NEG_BIG)
    return diag if kend == ATT_TQ else jnp.concatenate([s[:, :kend - ATT_TQ], diag], axis=1)


def _attn_specs(n_seq, seq_len, pairs=ATT_PAIRS):
    nq = seq_len // ATT_TQ
    width = pairs * LANES
    per = D_ATTN // width
    q_spec = pl.BlockSpec((ATT_TQ, width), lambda b, h, q: (b * nq + q, h))
    k_spec = pl.BlockSpec((seq_len, width), lambda b, h, q: (b, per + h))
    v_spec = pl.BlockSpec((seq_len, width), lambda b, h, q: (b, 2 * per + h))
    cq_spec = pl.BlockSpec((1, 2 * pairs, ATT_TQ, 1), lambda b, h, q: (b, h, q, 0))
    ck_spec = pl.BlockSpec((1, 2 * pairs, 1, seq_len), lambda b, h, q: (b, h, 0, 0))
    return nq, q_spec, k_spec, v_spec, cq_spec, ck_spec


def _own_cum(cum_ref, head, pairs=ATT_PAIRS):
    lane = lax.broadcasted_iota(jnp.int32, (1, LANES), 1)
    mine = lane == 2 * pairs * pl.program_id(1) + head
    return jnp.sum(jnp.where(mine, cum_ref[...], 0.0), axis=1, keepdims=True)


def _head_selectors():
    head0 = lax.broadcasted_iota(jnp.int32, (1, LANES), 1) < HEAD_DIM
    return head0, (head0, jnp.logical_not(head0))


def _for_key_range(qi, seq_len, run):
    per = ATT_KSTEP // ATT_TQ
    for g in range(seq_len // ATT_KSTEP):
        pl.when(qi // per == g)(functools.partial(run, (g + 1) * ATT_KSTEP))


def _attn_fwd(qkv, cum, ck, n_seq):
    n = qkv.shape[0]
    seq_len = n // n_seq
    nq, q_spec, k_spec, v_spec, cq_spec, ck_spec = _attn_specs(n_seq, seq_len, ATT_PAIRS_FWD)
    cum_spec = pl.BlockSpec((ATT_TQ, LANES), lambda b, h, q: (b * nq + q, 0))

    def body(q_ref, k_ref, v_ref, cum_ref, ck_ref, o_ref, lse_ref):
        qi = pl.program_id(2)
        head0, sels = _head_selectors()

        def run(kend):
            for pair in range(ATT_PAIRS_FWD):
                win = slice(pair * LANES, (pair + 1) * LANES)
                q2 = q_ref[:, win]
                kb = k_ref[0:kend, win]
                vb = v_ref[0:kend, win]
                outs = []
                for e in range(2):
                    head = 2 * pair + e
                    qe = jnp.where(sels[e], q2 * ATT_SCALE, 0.0).astype(BF16)
                    s = _scores(qe, kb, None, ck_ref[0, head, :, 0:kend], kend)
                    mx = jnp.max(s, axis=1, keepdims=True)
                    p = jnp.exp(s - mx)
                    den = jnp.sum(p, axis=1, keepdims=True)
                    outs.append(_dot(p.astype(BF16), vb) / den)
                    lse_ref[0, head] = _own_cum(cum_ref, head, ATT_PAIRS_FWD) + mx + jnp.log(den)
                o_ref[:, win] = jnp.where(head0, outs[0], outs[1])

        _for_key_range(qi, seq_len, run)

    return pl.pallas_call(
        body, name="attn_fwd", grid=(n_seq, N_HEADS // 2 // ATT_PAIRS_FWD, nq),
        in_specs=[q_spec, k_spec, v_spec, cum_spec, ck_spec],
        out_specs=(q_spec, cq_spec),
        out_shape=(jax.ShapeDtypeStruct((n, D_ATTN), F32), jax.ShapeDtypeStruct((n_seq, N_HEADS, seq_len, 1), F32)),
        compiler_params=_cparams("parallel", "parallel", "parallel"),
    )(qkv, qkv, qkv, cum, ck)


def _attn_bwd(qkv, cum, ck, o, do, lse, n_seq):
    n = qkv.shape[0]
    seq_len = n // n_seq
    nq, q_spec, k_spec, v_spec, cq_spec, ck_spec = _attn_specs(n_seq, seq_len)
    kv_out = pl.BlockSpec((seq_len, ATT_PAIRS * LANES), lambda b, h, q: (b, h))
    cum_spec = pl.BlockSpec((ATT_TQ, LANES), lambda b, h, q: (b * nq + q, 0))

    def body(q_ref, k_ref, v_ref, cum_ref, ck_ref, o_ref, do_ref, lse_ref, dq_ref, dk_ref, dv_ref, dcq_ref, dck_ref):
        qi = pl.program_id(2)

        @pl.when(qi == 0)
        def _():
            dk_ref[...] = jnp.zeros_like(dk_ref)
            dv_ref[...] = jnp.zeros_like(dv_ref)
            dck_ref[...] = jnp.zeros_like(dck_ref)
        head0, sels = _head_selectors()

        def run(kend):
            for pair in range(ATT_PAIRS):
                win = slice(pair * LANES, (pair + 1) * LANES)
                q2 = q_ref[:, win]
                do2 = do_ref[:, win]
                o2 = o_ref[:, win]
                kb = k_ref[0:kend, win]
                vb = v_ref[0:kend, win]
                dqs = []
                dk = jnp.zeros((kend, LANES), F32)
                dv = jnp.zeros((kend, LANES), F32)
                for e in range(2):
                    head = 2 * pair + e
                    qe = jnp.where(sels[e], q2 * ATT_SCALE, 0.0).astype(BF16)
                    doe = jnp.where(sels[e], do2, 0.0)
                    doe_b = doe.astype(BF16)
                    delta = jnp.sum(doe * o2, axis=1, keepdims=True)
                    bias = _own_cum(cum_ref, head) - lse_ref[0, head]
                    p = jnp.exp(_scores(qe, kb, bias, ck_ref[0, head, :, 0:kend], kend))
                    ds = p * (_dot_nt(doe_b, vb) - delta)
                    ds_b = ds.astype(BF16)
                    dqs.append(_dot(ds_b, kb))
                    dk = dk + _dot_tn(ds_b, qe)
                    dv = dv + _dot_tn(p.astype(BF16), doe_b)
                    dcq_ref[0, head] = jnp.sum(ds, axis=1, keepdims=True)
                    dck_ref[0, head, :, 0:kend] -= jnp.sum(ds, axis=0, keepdims=True)
                dk_ref[0:kend, win] += dk
                dv_ref[0:kend, win] += dv
                dq_ref[:, win] = jnp.where(head0, dqs[0], dqs[1]) * ATT_SCALE

        _for_key_range(qi, seq_len, run)

    return pl.pallas_call(
        body, name="attn_bwd", grid=(n_seq, N_HEADS // 2 // ATT_PAIRS, nq),
        in_specs=[q_spec, k_spec, v_spec, cum_spec, ck_spec, q_spec, q_spec, cq_spec],
        out_specs=(q_spec, kv_out, kv_out, cq_spec, ck_spec),
        out_shape=(jax.ShapeDtypeStruct((n, D_ATTN), F32), jax.ShapeDtypeStruct((n, D_ATTN), F32),
                   jax.ShapeDtypeStruct((n, D_ATTN), F32),
                   jax.ShapeDtypeStruct((n_seq, N_HEADS, seq_len, 1), F32),
                   jax.ShapeDtypeStruct((n_seq, N_HEADS, 1, seq_len), F32)),
        compiler_params=_cparams("parallel", "parallel", "arbitrary"),
    )(qkv, qkv, qkv, cum, ck, o, do, lse)


WEIGHT_NAMES = ("norm_mix", "w_in", "b_forget", "lam_re", "lam_im", "b_re", "b_im", "c_re", "c_im", "d_skip", "log_dt",
                "w_glu", "b_glu", "q_norm", "k_norm", "norm_out_ssm", "norm_out_attn", "w_out", "norm_ffn", "w_up",
                "conv_w", "conv_b", "w_down")
SHARDED = ("w_in", "w_glu", "w_out", "w_up", "conv_w", "w_down")
ADAM_TILE = {"w_in": (257, 256), "w_glu": (64, 512), "w_out": (128, 1024), "w_up": (688, 256), "conv_w": (3, 688),
             "w_down": (344, 1024)}
PACK_ROWS = SUBLANES * LANES


def _after_all(*arrays):
    return sum(a[(0,) * a.ndim].astype(F32) for a in arrays).reshape(1, 1)


def _pad_to(a, axis, size):
    pad = [(0, 0)] * a.ndim
    pad[axis] = (0, size - a.shape[axis])
    return jnp.pad(a, pad)


def _block_diag(t, transpose):
    t4 = t.reshape(S5_BLOCKS, 8, SSM_GROUP, SSM_STATE)
    eye = jnp.eye(8, dtype=t.dtype)
    if transpose:
        e = jnp.swapaxes(t4, 2, 3)[:, :, :, None, :] * eye[None, :, None, :, None]
        return e.reshape(S5_BLOCKS, S5_STATES, LANES)
    e = t4[:, :, :, None, :] * eye[None, :, None, :, None]
    return e.reshape(S5_BLOCKS, LANES, S5_STATES)


def _block_diag_extract(m, transpose):
    if transpose:
        m5 = m.reshape(S5_BLOCKS, 8, SSM_STATE, 8, SSM_GROUP)
        d = jnp.stack([m5[:, i, :, i, :] for i in range(8)], axis=1)
        return jnp.swapaxes(d, 2, 3).reshape(N_GROUPS, SSM_GROUP, SSM_STATE)
    m5 = m.reshape(S5_BLOCKS, 8, SSM_GROUP, 8, SSM_STATE)
    d = jnp.stack([m5[:, i, :, i, :] for i in range(8)], axis=1)
    return d.reshape(N_GROUPS, SSM_GROUP, SSM_STATE)


def _pack(pieces):
    flat = jnp.concatenate([p.reshape(-1).astype(F32) for p in pieces])
    size = -(-flat.shape[0] // PACK_ROWS) * PACK_ROWS
    return _pad_to(flat, 0, size).reshape(-1, LANES)


def _unpack(packed, shapes):
    flat = packed.reshape(-1)
    out, off = [], 0
    for shp in shapes:
        size = math.prod(shp)
        out.append(flat[off:off + size].reshape(shp))
        off += size
    return out


def kernel(x, norm_mix, w_in, b_forget, lam_re, lam_im, b_re, b_im, c_re, c_im, d_skip, log_dt, w_glu, b_glu, q_norm, k_norm, norm_out_ssm, norm_out_attn, w_out, norm_ffn, w_up, conv_w, conv_b, w_down, loss_target, m_norm_mix, m_w_in, m_b_forget, m_lam_re, m_lam_im, m_b_re, m_b_im, m_c_re, m_c_im, m_d_skip, m_log_dt, m_w_glu, m_b_glu, m_q_norm, m_k_norm, m_norm_out_ssm, m_norm_out_attn, m_w_out, m_norm_ffn, m_w_up, m_conv_w, m_conv_b, m_w_down, v_norm_mix, v_w_in, v_b_forget, v_lam_re, v_lam_im, v_b_re, v_b_im, v_c_re, v_c_im, v_d_skip, v_log_dt, v_w_glu, v_b_glu, v_q_norm, v_k_norm, v_norm_out_ssm, v_norm_out_attn, v_w_out, v_norm_ffn, v_w_up, v_conv_w, v_conv_b, v_w_down):
    given = dict(locals())
    weights = {k: given[k] for k in WEIGHT_NAMES}
    mom1 = {k: given["m_" + k] for k in WEIGHT_NAMES}
    mom2 = {k: given["v_" + k] for k in WEIGHT_NAMES}
    n_seq, seq_len, _ = x.shape
    n = n_seq * seq_len
    xf = x.reshape(n, D_MODEL)
    target = loss_target.reshape(n, D_MODEL)
    me_idx = 4 * lax.axis_index("x") + 2 * lax.axis_index("y") + lax.axis_index("c")

    in_flags = [False] * 2
    in_sems = _exchange_start([jnp.swapaxes(w_in[0], 0, 1).astype(BF16), conv_w[0]], in_flags, norm_mix,
                              "gather_in_start", 3, NEAR_PEERS)
    fill_own = lambda got, mine: lax.dynamic_update_index_in_dim(got, mine, me_idx, 0)

    lr3 = lam_re[0].reshape(N_GROUPS, 1, SSM_STATE)
    li3 = lam_im[0].reshape(N_GROUPS, 1, SSM_STATE)
    ldt3 = log_dt[0].reshape(N_GROUPS, 1, 1)
    br_t = jnp.swapaxes(b_re[0], 1, 2)
    bi_t = jnp.swapaxes(b_im[0], 1, 2)
    ab_re, ab_im, bb_re, bb_im = _s5_param_fwd(lr3, li3, ldt3, br_t, bi_t)
    a_re = ab_re.reshape(S5_BLOCKS, 1, S5_STATES)
    a_im = ab_im.reshape(S5_BLOCKS, 1, S5_STATES)
    bbr = _block_diag(bb_re, False).astype(BF16)
    bbi = _block_diag(bb_im, False).astype(BF16)
    cr = _block_diag(c_re[0], True).astype(BF16)
    ci = _block_diag(c_im[0], True).astype(BF16)
    avg = jnp.kron(jnp.eye(N_HEADS, dtype=F32), jnp.full((HEAD_DIM, HEAD_DIM), 1.0 / HEAD_DIM, F32)).astype(BF16)
    qg = jnp.tile(q_norm, (1, N_HEADS))
    kg = jnp.tile(k_norm, (1, N_HEADS))
    row_shards = [w_down[0].astype(BF16), w_out[0].astype(BF16), w_glu[0].astype(BF16)]

    (own_in, own_cw), near = _exchange_wait(in_sems[0], in_sems[1], in_sems[2], in_sems[3], in_flags,
                                            _after_all(a_re, a_im, bbr, bbi, cr, ci, avg, qg, kg, *row_shards),
                                            "gather_in_wait", NEAR_PEERS)
    relay = _relay_start(list(near), "gather_in_relay_start", 6)
    g_in, g_cw = _relay_wait(relay[0], relay[1], relay[2], relay[3], "gather_in_relay_wait")
    g_in = fill_own(g_in, own_in)
    g_cw = fill_own(g_cw, own_cw)
    row_flags = [False] * 3
    r_sems = _exchange_start(row_shards, row_flags, g_in, "gather_rows_start", 0)
    u_sems = _exchange_start([jnp.swapaxes(w_up[0], 0, 1).astype(BF16)], [False], r_sems[4], "gather_up_start", 5)
    norm_mix = norm_mix + u_sems[4][0, 0]
    w_in_p = _pad_to(g_in.reshape(D_IN, D_MODEL), 0, D_IN_PAD)

    hn, u, qkv, raw, fl = _inproj_fwd(xf, norm_mix, w_in_p, avg, qg, kg)
    yc, st_r, st_i, xs_r, xs_i = _s5_fwd(u, a_re, a_im, bbr, bbi, cr, ci, d_skip, n_seq)
    bf = _pad_to(b_forget, 1, LANES)
    cum = _fprep_fwd(fl, bf, n_seq)
    cum8 = jnp.swapaxes(cum[:, :N_HEADS].reshape(n_seq, seq_len, N_HEADS), 1, 2)
    ck = cum8[:, :, None, :]
    ya, lse = _attn_fwd(qkv, cum, ck, n_seq)
    own_rows, got_rows = _exchange_wait(r_sems[0], r_sems[1], r_sems[2], r_sems[3], row_flags, ya, "gather_rows_wait")
    g_down, g_out, g_glu = [fill_own(g, o) for o, g in zip(own_rows, got_rows)]
    w_glu_f = g_glu.reshape(D_SSM, D_SSM)
    w_out_f = g_out.reshape(D_MODEL, D_MODEL)
    conv_st = _pad_to(jnp.concatenate([g_cw, conv_b.reshape(N_DEV, 1, -1)], axis=1), 1, SUBLANES)
    w_down4 = g_down.reshape(FFN_GROUPS, FFN_GROUP, D_MODEL)
    ys = _glu_fwd(yc, w_glu_f, b_glu)
    h1, hn2, mixed = _mix_fwd(xf, ys, ya, norm_out_ssm, norm_out_attn, w_out_f, norm_ffn)
    (own_up,), (g_up,) = _exchange_wait(u_sems[0], u_sems[1], u_sems[2], u_sems[3], [False], _after_all(h1, hn2),
                                        "gather_up_wait")
    g_up = fill_own(g_up, own_up)
    ug, uv, pg, pv, dy, loss_part = _ffn_fwd(hn2, h1, target, g_up, conv_st, w_down4, seq_len)
    loss_local = 0.5 * jnp.sum(loss_part) / D_MODEL

    dug, duv, act, dhn2, dcg, dcv = _ffn_bwd(dy, ug, uv, pg, pv, g_up, conv_st, w_down4, seq_len)
    dh1, dys, dya, d_gs, d_ga, d_gf = _mix_bwd(dy, dhn2[None], h1, ys, ya, norm_out_ssm, norm_out_attn, w_out_f, norm_ffn)
    dyc, gl_b, dz_b, d_bglu = _glu_bwd(yc, dys, w_glu_f, b_glu)

    gw_glu = _tn_matmul(gl_b, dz_b, "dw_glu", D_SSM, D_SSM, out_dtype=BF16, tn=n)
    gw_out = _tn_matmul(mixed, dh1, "dw_out", D_MODEL, D_MODEL, out_dtype=BF16, tn=n // 2)
    gw_up = jnp.concatenate([_tn_grouped(dug, hn2, "dw_up_gate", False, BF16, tn=n),
                             _tn_grouped(duv, hn2, "dw_up_val", False, BF16, tn=n)], axis=0)
    gw_down = _tn_grouped(act, dy, "dw_down", False, BF16, tn=n // 2)
    g_conv = jnp.concatenate([dcg, dcv], axis=0)
    by_cols = lambda g, c: jnp.swapaxes(g.reshape(g.shape[0], N_DEV, c), 0, 1)
    early_flags = [True] * 4
    early_names = ("w_down", "w_out", "w_glu", "w_up")
    g_sems = _exchange_start(
        [gw_down.reshape(N_DEV, -1, D_MODEL), gw_out.reshape(N_DEV, -1, D_MODEL), gw_glu.reshape(N_DEV, -1, D_SSM), gw_up],
        early_flags, dyc, "grad_early_start", 1)
    started = g_sems[4][0, 0]

    du, dbbr, dbbi, dcr, dci, dar, dai, ddk = _s5_bwd(u, dyc, xs_r, xs_i, st_r, st_i, a_re, a_im, bbr, bbi, cr, ci,
                                                      d_skip + started, n_seq)
    partial_early = {
        "ab_re": jnp.sum(dar, axis=1), "ab_im": jnp.sum(dai, axis=1),
        "bb_re": _block_diag_extract(dbbr, False), "bb_im": _block_diag_extract(dbbi, False),
        "c_re": _block_diag_extract(dcr, True), "c_im": _block_diag_extract(dci, True),
        "d_skip": jnp.sum(ddk, axis=1), "b_glu": d_bglu,
        "norm_out_ssm": d_gs, "norm_out_attn": d_ga, "norm_ffn": d_gf, "conv_b": g_conv[:, 3],
    }
    early_keys = tuple(partial_early)
    early_shapes = [partial_early[k].shape for k in early_keys]
    p_sems = _exchange_start([_pack([partial_early[k] for k in early_keys])], [False], du, "small_early_start", 2)
    started = started + p_sems[4][0, 0]

    dqn, dkn, dv, dcq, dck = _attn_bwd(qkv, cum, ck + started, ya, dya, lse, n_seq)
    dcum8 = dcq[:, :, :, 0] + dck.reshape(n_seq, N_HEADS, seq_len)
    dcum = _pad_to(jnp.swapaxes(dcum8, 1, 2).reshape(n, N_HEADS), 1, LANES)
    dfl, dbf = _fprep_bwd(dcum, fl, bf, n_seq)
    dx, dproj, d_gmix, d_qg, d_kg = _inproj_bwd(xf, norm_mix, w_in_p, avg, qg, kg, raw, du, dqn, dkn, dv, dfl, dh1)

    gw_in = _tn_matmul(dproj, hn, "dw_in", D_IN_PAD, D_MODEL, out_rows=D_IN, out_dtype=BF16, tn=n // 4)
    partial_late = {
        "norm_mix": d_gmix, "b_forget": jnp.sum(dbf, axis=(0, 1))[:N_HEADS],
        "q_norm": jnp.sum(d_qg.reshape(N_HEADS, HEAD_DIM), axis=0),
        "k_norm": jnp.sum(d_kg.reshape(N_HEADS, HEAD_DIM), axis=0), "loss": loss_local.reshape(1),
    }
    late_keys = tuple(partial_late)
    late_shapes = [partial_late[k].shape for k in late_keys]

    late_flags = [True, True, False]
    l_sems = _exchange_start(
        [gw_in.reshape(N_DEV, D_IN // N_DEV, D_MODEL).astype(BF16), g_conv[:, :3],
         _pack([partial_late[k] for k in late_keys])],
        late_flags, dx, "grad_late_start", 4)
    early_src, early_land = _exchange_wait(g_sems[0], g_sems[1], g_sems[2], g_sems[3], early_flags, l_sems[4],
                                           "grad_early_wait")
    land = dict(zip(early_names, early_land))
    land_up = land["w_up"]
    own = {k: lax.dynamic_index_in_dim(s, me_idx, 0, keepdims=False) for k, s in zip(early_names, early_src)}
    grads, deltas, new_m, new_v = {}, {}, {}, {}

    def adam_shard(name):
        flip = (lambda a: jnp.swapaxes(a, 1, 2)) if name in ("w_in", "w_up") else (lambda a: a)
        outs = _adam_sharded(land[name], own[name], flip(weights[name]), flip(mom1[name]), flip(mom2[name]),
                             "adam_" + name, ADAM_TILE[name])
        grads[name], deltas[name], new_m[name], new_v[name] = [flip(o) for o in outs]

    for name in ("w_up", "w_down", "w_out", "w_glu"):
        adam_shard(name)
    (own_pack,), (early_parts,) = _exchange_wait(p_sems[0], p_sems[1], p_sems[2], p_sems[3], [False], land_up,
                                                 "small_early_wait")
    early_sum = _sum_partials(early_parts, own_pack, "sum_early_partials")
    (src_in, src_cw, own_late), (land["w_in"], land["conv_w"], late_parts) = _exchange_wait(
        l_sems[0], l_sems[1], l_sems[2], l_sems[3], late_flags,
        _after_all(early_sum, *[new_v[k] for k in ("w_up", "w_down", "w_out", "w_glu")]), "grad_late_wait")
    own["w_in"] = lax.dynamic_index_in_dim(src_in, me_idx, 0, keepdims=False)
    own["conv_w"] = lax.dynamic_index_in_dim(src_cw, me_idx, 0, keepdims=False)
    for name in ("w_in", "conv_w"):
        adam_shard(name)

    summed = dict(zip(late_keys, _unpack(_sum_partials(late_parts, own_late, "sum_late_partials"), late_shapes)))
    summed.update(zip(early_keys, _unpack(early_sum, early_shapes)))
    dlr, dli, dldt, dbr_t, dbi_t = _s5_param_bwd(
        lr3, li3, ldt3, br_t, bi_t, summed["ab_re"].reshape(lr3.shape), summed["ab_im"].reshape(lr3.shape),
        summed["bb_re"], summed["bb_im"])
    small_grads = {
        "norm_mix": summed["norm_mix"], "b_forget": summed["b_forget"], "lam_re": dlr, "lam_im": dli,
        "b_re": dbr_t, "b_im": dbi_t, "c_re": summed["c_re"], "c_im": summed["c_im"],
        "d_skip": summed["d_skip"], "log_dt": dldt, "b_glu": summed["b_glu"], "q_norm": summed["q_norm"],
        "k_norm": summed["k_norm"], "norm_out_ssm": summed["norm_out_ssm"], "norm_out_attn": summed["norm_out_attn"],
        "norm_ffn": summed["norm_ffn"], "conv_b": summed["conv_b"],
    }
    repl = tuple(k for k in WEIGHT_NAMES if k not in SHARDED)
    turn = lambda k, a: jnp.swapaxes(a, 2, 3) if k in ("b_re", "b_im") else a
    w_list = [turn(k, weights[k]) for k in repl]
    g_list = [small_grads[k].reshape(w.shape) for k, w in zip(repl, w_list)]
    d_list, m_list, v_list = _adam_replicated(g_list, w_list, [turn(k, mom1[k]) for k in repl],
                                              [turn(k, mom2[k]) for k in repl], "adam_replicated")
    for k, g, d, nm, nv in zip(repl, g_list, d_list, m_list, v_list):
        grads[k], deltas[k], new_m[k], new_v[k] = turn(k, g), turn(k, d), turn(k, nm), turn(k, nv)

    grad_x = dx.reshape(x.shape)
    loss = summed["loss"].reshape(())
    return (loss, grad_x, *[grads[k] for k in WEIGHT_NAMES], *[deltas[k] for k in WEIGHT_NAMES],
            *[new_m[k] for k in WEIGHT_NAMES], *[new_v[k] for k in WEIGHT_NAMES])
```

```python
import functools
import math

import jax
import jax.numpy as jnp
from jax import lax
from jax.experimental import pallas as pl
from jax.experimental.pallas import tpu as pltpu

F32 = jnp.float32
BF16 = jnp.bfloat16
HIGHEST = lax.Precision.HIGHEST

N_DEV = 8
D_MODEL = 1024
D_SSM = 512
D_ATTN = 512
N_HEADS = 8
HEAD_DIM = 64
N_GROUPS = 32
SSM_GROUP = 16
SSM_STATE = 64
D_FF = 2752
D_IN = 2056
D_IN_PAD = 2176
EPS = 1e-6
LANES = 128
SUBLANES = 8
VMEM_LIMIT = 56 * 1024 * 1024

ADAM_LR = 0.001
ADAM_B1 = 0.9
ADAM_B2 = 0.999
ADAM_EPS = 1e-08
ADAM_WD = 0.01
ADAM_STEP = 10


def _cparams(*sem):
    return pltpu.CompilerParams(dimension_semantics=sem, vmem_limit_bytes=VMEM_LIMIT)


def _dot(a, b, **kw):
    return jnp.dot(a, b, preferred_element_type=F32, **kw)


def _dot_nt(a, b):
    return lax.dot_general(a, b, (((1,), (1,)), ((), ())), preferred_element_type=F32)


def _dot_tn(a, b):
    return lax.dot_general(a, b, (((0,), (0,)), ((), ())), preferred_element_type=F32)


def _rms(x, g):
    return x * lax.rsqrt(jnp.mean(x * x, axis=-1, keepdims=True) + EPS) * g


def _split_dot(x, avg):
    hi = x.astype(BF16)
    lo = (x - hi.astype(F32)).astype(BF16)
    return _dot(hi, avg) + _dot(lo, avg)


@jax.custom_vjp
def _group_mean(x, avg):
    return _split_dot(x, avg)


def _group_mean_fwd(x, avg):
    return _split_dot(x, avg), avg


def _group_mean_bwd(avg, ct):
    return _split_dot(ct, avg), jnp.zeros_like(avg)


_group_mean.defvjp(_group_mean_fwd, _group_mean_bwd)


def _headnorm(q, avg, g):
    return q * lax.rsqrt(_group_mean(q * q, avg) + EPS) * g


ALL_PEERS = tuple(range(1, N_DEV))
NEAR_PEERS = (1, 2, 4, 6)
RELAYED = (2, 4, 6)


def _peer_list(js=ALL_PEERS):
    x, y, c = lax.axis_index("x"), lax.axis_index("y"), lax.axis_index("c")
    peers = []
    for j in js:
        px = 1 - x if (j >> 2) & 1 else x
        py = 1 - y if (j >> 1) & 1 else y
        pc = 1 - c if j & 1 else c
        peers.append(((px, py, pc), 4 * px + 2 * py + pc))
    return 4 * x + 2 * y + c, peers


def _split_copies(src, land, send_sems, recv_sems, scatter_flags, me, peers, incoming):
    copies = []
    for k in range(len(src)):
        for j, (pid, pidx) in enumerate(peers):
            s = src[k].at[pidx] if scatter_flags[k] else src[k]
            i = k * len(peers) + j
            copies.append(pltpu.make_async_remote_copy(
                src_ref=s, dst_ref=land[k].at[pidx if incoming else me], send_sem=send_sems[i],
                recv_sem=recv_sems[i], device_id=pid, device_id_type=pl.DeviceIdType.MESH))
    return copies


def _handshake(peers):
    barrier = pltpu.get_barrier_semaphore()
    for pid, _ in peers:
        pl.semaphore_signal(barrier, inc=1, device_id=pid, device_id_type=pl.DeviceIdType.MESH)
    pl.semaphore_wait(barrier, len(peers))


def _exchange_start(srcs, scatter_flags, after, name, collective_id, peer_js=ALL_PEERS):
    n = len(srcs)
    ns = n * len(peer_js)
    hbm = pl.BlockSpec(memory_space=pltpu.HBM)
    sem = pl.BlockSpec(memory_space=pltpu.SEMAPHORE)
    land_shapes = [s.shape if sc else (N_DEV,) + s.shape for s, sc in zip(srcs, scatter_flags)]

    def body(*refs):
        src, land = refs[:n], refs[n:2 * n]
        send_sems = refs[2 * n + 1:2 * n + 1 + ns]
        recv_sems = refs[2 * n + 1 + ns:2 * n + 1 + 2 * ns]
        token = refs[4 * n + 1 + 2 * ns]
        me, peers = _peer_list(peer_js)
        _handshake(peers)
        for cp in _split_copies(src, land, send_sems, recv_sems, scatter_flags, me, peers, False):
            cp.start()
        token[...] = jnp.zeros_like(token)

    outs = pl.pallas_call(
        body, name=name,
        out_shape=(*[pltpu.SemaphoreType.DMA(())] * (2 * ns), *[pltpu.HBM(s.shape, s.dtype) for s in srcs],
                   *[pltpu.HBM(shp, s.dtype) for shp, s in zip(land_shapes, srcs)],
                   jax.ShapeDtypeStruct((SUBLANES, LANES), F32)),
        in_specs=[hbm] * (2 * n) + [pl.BlockSpec(memory_space=pl.ANY)],
        out_specs=(*[sem] * (2 * ns), *[hbm] * (2 * n), pl.BlockSpec(memory_space=pltpu.VMEM)),
        input_output_aliases={i: 2 * ns + i for i in range(2 * n)},
        compiler_params=pltpu.CompilerParams(has_side_effects=pltpu.SideEffectType.DATAFLOW_SIDE_EFFECTING,
                                             collective_id=collective_id),
    )(*[pltpu.with_memory_space_constraint(s, pltpu.HBM) for s in srcs],
      *[pltpu.with_memory_space_constraint(lax.empty(shp, s.dtype), pltpu.HBM) for shp, s in zip(land_shapes, srcs)],
      after)
    return (outs[:ns], outs[ns:2 * ns], outs[2 * ns:2 * ns + n], outs[2 * ns + n:2 * ns + 2 * n], outs[2 * ns + 2 * n])


def _exchange_wait(send_sems, recv_sems, srcs, lands, scatter_flags, after, name, peer_js=ALL_PEERS):
    n = len(srcs)
    ns = n * len(peer_js)
    hbm = pl.BlockSpec(memory_space=pltpu.HBM)
    sem = pl.BlockSpec(memory_space=pltpu.SEMAPHORE)

    def body(*refs):
        src, land = refs[:n], refs[n:2 * n]
        s_sems = refs[2 * n:2 * n + ns]
        r_sems = refs[2 * n + ns:2 * n + 2 * ns]
        me, peers = _peer_list(peer_js)
        for cp in _split_copies(src, land, s_sems, r_sems, scatter_flags, me, peers, True):
            cp.wait_send()
            cp.wait_recv()

    outs = pl.pallas_call(
        body, name=name,
        out_shape=tuple(pltpu.HBM(a.shape, a.dtype) for a in (*srcs, *lands)),
        in_specs=[hbm] * (2 * n) + [sem] * (2 * ns) + [pl.BlockSpec(memory_space=pl.ANY)],
        out_specs=tuple([hbm] * (2 * n)),
        input_output_aliases={i: i for i in range(2 * n)},
        compiler_params=pltpu.CompilerParams(has_side_effects=pltpu.SideEffectType.DATAFLOW_SIDE_EFFECTING),
    )(*srcs, *lands, *send_sems, *recv_sems, after)
    return outs[:n], outs[n:]


def _relay_copies(land, send_sems, recv_sems, incoming):
    _, ((sibling, _),) = _peer_list((1,))
    _, heard = _peer_list(RELAYED)
    _, sibling_heard = _peer_list(tuple(j ^ 1 for j in RELAYED))
    copies = []
    for k in range(len(land)):
        for j in range(len(RELAYED)):
            slot = (sibling_heard if incoming else heard)[j][1]
            i = k * len(RELAYED) + j
            copies.append(pltpu.make_async_remote_copy(
                src_ref=land[k].at[slot], dst_ref=land[k].at[slot], send_sem=send_sems[i], recv_sem=recv_sems[i],
                device_id=sibling, device_id_type=pl.DeviceIdType.MESH))
    return copies


def _relay_start(lands, name, collective_id):
    n = len(lands)
    ns = n * len(RELAYED)
    hbm = pl.BlockSpec(memory_space=pltpu.HBM)
    sem = pl.BlockSpec(memory_space=pltpu.SEMAPHORE)

    def body(*refs):
        land = refs[:n]
        send_sems = refs[n:n + ns]
        recv_sems = refs[n + ns:n + 2 * ns]
        token = refs[2 * n + 2 * ns]
        _handshake(_peer_list((1,))[1])
        for cp in _relay_copies(land, send_sems, recv_sems, False):
            cp.start()
        token[...] = jnp.zeros_like(token)

    outs = pl.pallas_call(
        body, name=name,
        out_shape=(*[pltpu.SemaphoreType.DMA(())] * (2 * ns), *[pltpu.HBM(a.shape, a.dtype) for a in lands],
                   jax.ShapeDtypeStruct((SUBLANES, LANES), F32)),
        in_specs=[hbm] * n,
        out_specs=(*[sem] * (2 * ns), *[hbm] * n, pl.BlockSpec(memory_space=pltpu.VMEM)),
        input_output_aliases={i: 2 * ns + i for i in range(n)},
        compiler_params=pltpu.CompilerParams(has_side_effects=pltpu.SideEffectType.DATAFLOW_SIDE_EFFECTING,
                                             collective_id=collective_id),
    )(*lands)
    return outs[:ns], outs[ns:2 * ns], outs[2 * ns:2 * ns + n], outs[2 * ns + n]


def _relay_wait(send_sems, recv_sems, lands, after, name):
    n = len(lands)
    ns = n * len(RELAYED)
    hbm = pl.BlockSpec(memory_space=pltpu.HBM)
    sem = pl.BlockSpec(memory_space=pltpu.SEMAPHORE)

    def body(*refs):
        land = refs[:n]
        for cp in _relay_copies(land, refs[n:n + ns], refs[n + ns:n + 2 * ns], True):
            cp.wait_send()
            cp.wait_recv()

    return pl.pallas_call(
        body, name=name,
        out_shape=tuple(pltpu.HBM(a.shape, a.dtype) for a in lands),
        in_specs=[hbm] * n + [sem] * (2 * ns) + [pl.BlockSpec(memory_space=pl.ANY)],
        out_specs=tuple([hbm] * n),
        input_output_aliases={i: i for i in range(n)},
        compiler_params=pltpu.CompilerParams(has_side_effects=pltpu.SideEffectType.DATAFLOW_SIDE_EFFECTING),
    )(*lands, *send_sems, *recv_sems, after)


def _tn_matmul(a, b, name, tk, tm, out_rows=None, out_cols=None, out_dtype=F32, tn=512):
    n_tok, k_dim = a.shape
    m_dim = b.shape[1]
    grid = (k_dim // tk, m_dim // tm, n_tok // tn)

    def body(a_ref, b_ref, o_ref, acc):
        k = pl.program_id(2)
        part = _dot_tn(a_ref[...].astype(BF16), b_ref[...].astype(BF16))

        @pl.when(k == 0)
        def _():
            acc[...] = part

        @pl.when(k > 0)
        def _():
            acc[...] += part

        @pl.when(k == grid[2] - 1)
        def _():
            o_ref[...] = acc[...].astype(out_dtype)

    return pl.pallas_call(
        body, name=name, grid=grid,
        in_specs=[pl.BlockSpec((tn, tk), lambda i, j, k: (k, i)), pl.BlockSpec((tn, tm), lambda i, j, k: (k, j))],
        out_specs=pl.BlockSpec((tk, tm), lambda i, j, k: (i, j)),
        out_shape=jax.ShapeDtypeStruct((out_rows or k_dim, out_cols or m_dim), out_dtype),
        scratch_shapes=[pltpu.VMEM((tk, tm), F32)],
        compiler_params=_cparams("parallel", "parallel", "arbitrary"),
    )(a, b)


def _adam_math(g, w, m, v):
    m = ADAM_B1 * m + (1.0 - ADAM_B1) * g
    v = ADAM_B2 * v + (1.0 - ADAM_B2) * (g * g)
    m_hat = m / (1.0 - ADAM_B1 ** ADAM_STEP)
    v_hat = v / (1.0 - ADAM_B2 ** ADAM_STEP)
    delta = -ADAM_LR * (m_hat / (jnp.sqrt(v_hat) + ADAM_EPS) + ADAM_WD * w)
    return delta, m, v


def _adam_sharded(land, own, w, m, v, name, tile):
    _, r, c = w.shape

    def body(*refs):
        l_ref = refs[0]
        own_ref = refs[1] if own is not None else None
        w_ref, m_ref, v_ref, g_ref, d_ref, nm_ref, nv_ref = [ref.at[0] for ref in refs[-7:]]
        if own_ref is not None:
            x, y, z = lax.axis_index("x"), lax.axis_index("y"), lax.axis_index("c")
            me = 4 * x + 2 * y + z
            mine = own_ref[...].astype(F32)
        g = None
        for s in range(N_DEV):
            part = l_ref[s].astype(F32)
            if own_ref is not None:
                part = jnp.where(me == s, mine, part)
            g = part if g is None else g + part
        d, nm, nv = _adam_math(g, w_ref[...], m_ref[...], v_ref[...])
        g_ref[...] = g
        d_ref[...] = d
        nm_ref[...] = nm
        nv_ref[...] = nv

    tr, tc = tile
    spec = pl.BlockSpec((1, tr, tc), lambda i, j: (0, i, j))
    own_specs, own_args = ([pl.BlockSpec((tr, tc), lambda i, j: (i, j))], [own]) if own is not None else ([], [])
    return pl.pallas_call(
        body, name=name, grid=(r // tr, c // tc),
        in_specs=[pl.BlockSpec((N_DEV, tr, tc), lambda i, j: (0, i, j)), *own_specs, spec, spec, spec],
        out_specs=(spec, spec, spec, spec),
        out_shape=tuple(jax.ShapeDtypeStruct((1, r, c), F32) for _ in range(4)),
        compiler_params=_cparams("parallel", "parallel"),
    )(land, *own_args, w, m, v)


def _sum_partials(parts, own, name):
    _, r, c = parts.shape

    def body(*refs):
        p_ref, o_ref = refs[0], refs[-1]
        if own is not None:
            x, y, z = lax.axis_index("x"), lax.axis_index("y"), lax.axis_index("c")
            me = 4 * x + 2 * y + z
            mine = refs[1][...]
        g = None
        for s in range(N_DEV):
            part = p_ref[s]
            if own is not None:
                part = jnp.where(me == s, mine, part)
            g = part if g is None else g + part
        o_ref[...] = g

    args = (parts,) if own is None else (parts, own)
    return pl.pallas_call(body, name=name, out_shape=jax.ShapeDtypeStruct((r, c), F32),
                          compiler_params=pltpu.CompilerParams(vmem_limit_bytes=VMEM_LIMIT))(*args)


def _adam_replicated(gs, ws, ms, vs, name):
    k = len(ws)

    def body(*refs):
        outs = refs[4 * k:]
        for i in range(k):
            d, nm, nv = _adam_math(refs[i][...], refs[k + i][...], refs[2 * k + i][...], refs[3 * k + i][...])
            outs[i][...] = d
            outs[k + i][...] = nm
            outs[2 * k + i][...] = nv

    outs = pl.pallas_call(body, name=name, out_shape=tuple(jax.ShapeDtypeStruct(w.shape, F32) for w in ws) * 3,
                          compiler_params=pltpu.CompilerParams(vmem_limit_bytes=VMEM_LIMIT))(*gs, *ws, *ms, *vs)
    return outs[:k], outs[k:2 * k], outs[2 * k:]


def _inproj_fwd(x, g, w_in, avg, qg, kg, tm=512):
    n = x.shape[0]

    def body(x_ref, g_ref, w_ref, a_ref, qg_ref, kg_ref, hn_ref, u_ref, qkv_ref, raw_ref, fl_ref):
        hn = _rms(x_ref[...], g_ref[...]).astype(BF16)
        hn_ref[...] = hn
        proj = _dot_nt(hn, w_ref[...])
        u_ref[...] = proj[:, 0:512]
        q = proj[:, 512:1024]
        k = proj[:, 1024:1536]
        raw_ref[:, 0:512] = q
        raw_ref[:, 512:1024] = k
        qkv_ref[:, 0:512] = _headnorm(q, a_ref[...], qg_ref[...]).astype(BF16)
        qkv_ref[:, 512:1024] = _headnorm(k, a_ref[...], kg_ref[...]).astype(BF16)
        qkv_ref[:, 1024:1536] = proj[:, 1536:2048].astype(BF16)
        fl_ref[...] = proj[:, 2048:D_IN_PAD]

    row = lambda w: pl.BlockSpec((tm, w), lambda i: (i, 0))
    full = lambda a: pl.BlockSpec(a.shape, lambda i: (0,) * a.ndim)
    return pl.pallas_call(
        body, name="inproj_fwd", grid=(n // tm,),
        in_specs=[row(D_MODEL), full(g), full(w_in), full(avg), full(qg), full(kg)],
        out_specs=(row(D_MODEL), row(512), row(1536), row(1024), row(LANES)),
        out_shape=(jax.ShapeDtypeStruct((n, D_MODEL), BF16), jax.ShapeDtypeStruct((n, 512), F32),
                   jax.ShapeDtypeStruct((n, 1536), BF16), jax.ShapeDtypeStruct((n, 1024), F32),
                   jax.ShapeDtypeStruct((n, LANES), F32)),
        compiler_params=_cparams("parallel"),
    )(x, g, w_in, avg, qg, kg)


def _inproj_bwd(x, g, w_in, avg, qg, kg, raw, du, dqn, dkn, dv, dfl, dres, tm=512):
    n = x.shape[0]

    def body(x_ref, g_ref, w_ref, a_ref, qg_ref, kg_ref, raw_ref, du_ref, dqn_ref, dkn_ref, dv_ref, dfl_ref, dres_ref,
             dx_ref, dproj_ref, dg_ref, dqg_ref, dkg_ref):
        @pl.when(pl.program_id(0) == 0)
        def _():
            dg_ref[...] = jnp.zeros_like(dg_ref)
            dqg_ref[...] = jnp.zeros_like(dqg_ref)
            dkg_ref[...] = jnp.zeros_like(dkg_ref)
        avg_m = a_ref[...]
        _, vjp_q = jax.vjp(lambda q, gg: _headnorm(q, avg_m, gg), raw_ref[:, 0:512], qg_ref[...])
        dq, dqg = vjp_q(dqn_ref[...])
        _, vjp_k = jax.vjp(lambda k, gg: _headnorm(k, avg_m, gg), raw_ref[:, 512:1024], kg_ref[...])
        dk, dkg = vjp_k(dkn_ref[...])
        dproj = jnp.concatenate([du_ref[...], dq, dk, dv_ref[...], dfl_ref[...]], axis=1).astype(BF16)
        dproj_ref[...] = dproj
        dhn = _dot(dproj, w_ref[...])
        _, vjp_x = jax.vjp(_rms, x_ref[...], g_ref[...])
        dxn, dg = vjp_x(dhn)
        dx_ref[...] = dxn + dres_ref[...]
        dg_ref[...] += dg
        dqg_ref[...] += dqg
        dkg_ref[...] += dkg

    row = lambda w: pl.BlockSpec((tm, w), lambda i: (i, 0))
    full = lambda a: pl.BlockSpec(a.shape, lambda i: (0,) * a.ndim)
    vec = lambda w: pl.BlockSpec((1, w), lambda i: (0, 0))
    return pl.pallas_call(
        body, name="inproj_bwd", grid=(n // tm,),
        in_specs=[row(D_MODEL), full(g), full(w_in), full(avg), full(qg), full(kg), row(1024), row(512), row(512),
                  row(512), row(512), row(LANES), row(D_MODEL)],
        out_specs=(row(D_MODEL), row(D_IN_PAD), vec(D_MODEL), vec(512), vec(512)),
        out_shape=(jax.ShapeDtypeStruct((n, D_MODEL), F32), jax.ShapeDtypeStruct((n, D_IN_PAD), BF16),
                   jax.ShapeDtypeStruct((1, D_MODEL), F32), jax.ShapeDtypeStruct((1, 512), F32),
                   jax.ShapeDtypeStruct((1, 512), F32)),
        compiler_params=_cparams("arbitrary"),
    )(x, g, w_in, avg, qg, kg, raw, du, dqn, dkn, dv, dfl, dres)


def _mix_fwd(x, yc, ya, wg, bg, gs, ga, wout, gf, tm=512):
    n = x.shape[0]

    def body(x_ref, yc_ref, ya_ref, wg_ref, bg_ref, gs_ref, ga_ref, w_ref, gf_ref, h1_ref, hn2_ref, mixed_ref):
        gl = jax.nn.gelu(yc_ref[...])
        ys = gl * jax.nn.sigmoid(_dot(gl.astype(BF16), wg_ref[...]) + bg_ref[...])
        mixed = jnp.concatenate([_rms(ys, gs_ref[...]), _rms(ya_ref[...], ga_ref[...])], axis=1).astype(BF16)
        mixed_ref[...] = mixed
        h1 = x_ref[...] + _dot(mixed, w_ref[...])
        h1_ref[...] = h1
        hn2_ref[...] = _rms(h1, gf_ref[...]).astype(BF16)

    row = lambda w: pl.BlockSpec((tm, w), lambda i: (i, 0))
    full = lambda a: pl.BlockSpec(a.shape, lambda i: (0,) * a.ndim)
    return pl.pallas_call(
        body, name="mix_fwd", grid=(n // tm,),
        in_specs=[row(D_MODEL), row(512), row(512), full(wg), full(bg), full(gs), full(ga), full(wout), full(gf)],
        out_specs=(row(D_MODEL), row(D_MODEL), row(D_MODEL)),
        out_shape=(jax.ShapeDtypeStruct((n, D_MODEL), F32), jax.ShapeDtypeStruct((n, D_MODEL), BF16),
                   jax.ShapeDtypeStruct((n, D_MODEL), BF16)),
        compiler_params=_cparams("parallel"),
    )(x, yc, ya, wg, bg, gs, ga, wout, gf)


def _mix_bwd(dy, dhn2_parts, h1, yc, ya, wg, bg, gs, ga, wout, gf, tm=512):
    n = dy.shape[0]
    n_parts = dhn2_parts.shape[0]

    def body(dy_ref, dp_ref, h1_ref, yc_ref, ya_ref, wg_ref, bg_ref, gs_ref, ga_ref, w_ref, gf_ref,
             dh1_ref, dyc_ref, dya_ref, gl_ref, dz_ref, dgs_ref, dga_ref, dgf_ref, dbg_ref):
        @pl.when(pl.program_id(0) == 0)
        def _():
            for ref in (dgs_ref, dga_ref, dgf_ref, dbg_ref):
                ref[...] = jnp.zeros_like(ref)
        dhn2 = dp_ref[0]
        for p in range(1, n_parts):
            dhn2 = dhn2 + dp_ref[p]
        _, vjp_f = jax.vjp(_rms, h1_ref[...], gf_ref[...])
        dh1n, dgf = vjp_f(dhn2)
        dh1 = dy_ref[...] + dh1n
        dh1_ref[...] = dh1
        dmixed = _dot_nt(dh1.astype(BF16), w_ref[...])
        gl, vjp_gelu = jax.vjp(jax.nn.gelu, yc_ref[...])
        glb = gl.astype(BF16)
        s = jax.nn.sigmoid(_dot(glb, wg_ref[...]) + bg_ref[...])
        _, vjp_s = jax.vjp(_rms, gl * s, gs_ref[...])
        dys, dgs = vjp_s(dmixed[:, 0:512])
        _, vjp_a = jax.vjp(_rms, ya_ref[...], ga_ref[...])
        dya, dga = vjp_a(dmixed[:, 512:1024])
        dz = dys * gl * s * (1.0 - s)
        dzb = dz.astype(BF16)
        dyc_ref[...] = vjp_gelu(dys * s + _dot_nt(dzb, wg_ref[...]))[0]
        dya_ref[...] = dya
        gl_ref[...] = glb
        dz_ref[...] = dzb
        dgs_ref[...] += dgs
        dga_ref[...] += dga
        dgf_ref[...] += dgf
        dbg_ref[...] += jnp.sum(dz, axis=0, keepdims=True)

    row = lambda w: pl.BlockSpec((tm, w), lambda i: (i, 0))
    full = lambda a: pl.BlockSpec(a.shape, lambda i: (0,) * a.ndim)
    vec = lambda w: pl.BlockSpec((1, w), lambda i: (0, 0))
    return pl.pallas_call(
        body, name="mix_bwd", grid=(n // tm,),
        in_specs=[row(D_MODEL), pl.BlockSpec((n_parts, tm, D_MODEL), lambda i: (0, i, 0)), row(D_MODEL), row(512),
                  row(512), full(wg), full(bg), full(gs), full(ga), full(wout), full(gf)],
        out_specs=(row(D_MODEL), row(512), row(512), row(512), row(512), vec(512), vec(512), vec(D_MODEL), vec(512)),
        out_shape=(jax.ShapeDtypeStruct((n, D_MODEL), F32), jax.ShapeDtypeStruct((n, 512), F32),
                   jax.ShapeDtypeStruct((n, 512), F32), jax.ShapeDtypeStruct((n, 512), BF16),
                   jax.ShapeDtypeStruct((n, 512), BF16), jax.ShapeDtypeStruct((1, 512), F32),
                   jax.ShapeDtypeStruct((1, 512), F32), jax.ShapeDtypeStruct((1, D_MODEL), F32),
                   jax.ShapeDtypeStruct((1, 512), F32)),
        compiler_params=_cparams("arbitrary"),
    )(dy, dhn2_parts, h1, yc, ya, wg, bg, gs, ga, wout, gf)


HALO = 16
FFN_GROUPS = 4
FFN_GROUP = D_FF // FFN_GROUPS


def _conv3(ue, cw):
    return cw[2:3] * ue + cw[1:2] * pltpu.roll(ue, 1, 0) + cw[0:1] * pltpu.roll(ue, 2, 0) + cw[3:4]


def _ffn_weight_specs():
    gate = lambda i, j: (j, 0, 0)
    val = lambda i, j: (j + FFN_GROUPS, 0, 0)
    w_blk, c_blk = (1, FFN_GROUP, D_MODEL), (1, SUBLANES, FFN_GROUP)
    return [pl.BlockSpec(w_blk, gate), pl.BlockSpec(w_blk, val), pl.BlockSpec(c_blk, gate), pl.BlockSpec(c_blk, val),
            pl.BlockSpec((1, FFN_GROUP, D_MODEL), gate)]


def _ffn_fwd(hn2, h1, target, w_up, conv, w_down, seq_len, tm=512):
    n = hn2.shape[0]
    nj = FFN_GROUPS
    hb = tm // HALO

    def body(hn_ref, halo_ref, h1_ref, tgt_ref, wg_ref, wv_ref, cg_ref, cv_ref, wd_ref,
             ug_ref, uv_ref, pg_ref, pv_ref, dy_ref, loss_ref, acc):
        i, j = pl.program_id(0), pl.program_id(1)
        seq_start = (i * tm) % seq_len == 0
        halo = halo_ref[...]
        halo = jnp.where(seq_start, jnp.zeros_like(halo), halo)
        he = jnp.concatenate([halo, hn_ref[...]], axis=0)
        ueg = _dot_nt(he, wg_ref[0])
        uev = _dot_nt(he, wv_ref[0])
        ug_ref[0] = ueg[HALO:].astype(BF16)
        uv_ref[0] = uev[HALO:].astype(BF16)
        cg = _conv3(ueg, cg_ref[0])[HALO:]
        cv = _conv3(uev, cv_ref[0])[HALO:]
        pg_ref[0] = cg.astype(BF16)
        pv_ref[0] = cv.astype(BF16)
        act = (jax.nn.silu(cg) * cv).astype(BF16)
        part = _dot(act, wd_ref[0])

        @pl.when(j == 0)
        def _():
            acc[...] = part

        @pl.when(j > 0)
        def _():
            acc[...] += part

        @pl.when(j == nj - 1)
        def _():
            err = h1_ref[...] + acc[...] - tgt_ref[...]
            dy_ref[...] = err * (1.0 / D_MODEL)
            loss_ref[0] = jnp.sum(err * err, axis=0, keepdims=True)

    row = pl.BlockSpec((tm, D_MODEL), lambda i, j: (i, 0))
    u_main = pl.BlockSpec((1, tm, FFN_GROUP), lambda i, j: (j, i, 0))
    u_shape = jax.ShapeDtypeStruct((FFN_GROUPS, n, FFN_GROUP), BF16)
    return pl.pallas_call(
        body, name="ffn_fwd", grid=(n // tm, nj),
        in_specs=[row, pl.BlockSpec((HALO, D_MODEL), lambda i, j: (jnp.maximum(i * hb - 1, 0), 0)), row, row,
                  *_ffn_weight_specs()],
        out_specs=(u_main, u_main, u_main, u_main, row, pl.BlockSpec((1, 1, D_MODEL), lambda i, j: (i, 0, 0))),
        out_shape=(u_shape, u_shape, u_shape, u_shape, jax.ShapeDtypeStruct((n, D_MODEL), F32),
                   jax.ShapeDtypeStruct((n // tm, 1, D_MODEL), F32)),
        scratch_shapes=[pltpu.VMEM((tm, D_MODEL), F32)],
        compiler_params=_cparams("parallel", "arbitrary"),
    )(hn2, hn2, h1, target, w_up, w_up, conv, conv, w_down)


def _ffn_bwd(dy, ug, uv, pg, pv, w_up, conv, w_down, seq_len, tm=512):
    n = dy.shape[0]
    nj = FFN_GROUPS
    fb = FFN_GROUP
    hb = tm // HALO
    last_hb = n // HALO - 1
    rows = tm + HALO

    def body(dy_ref, dyn_ref, ug_ref, uv_ref, pgm_ref, pgn_ref, pvm_ref, pvn_ref, wg_ref, wv_ref, cg_ref, cv_ref,
             wd_ref, dug_ref, duv_ref, act_ref, dhn_ref, dcg_ref, dcv_ref, acc):
        i, j = pl.program_id(0), pl.program_id(1)
        seq_end = ((i + 1) * tm) % seq_len == 0
        dyn = dyn_ref[...]
        dyn = jnp.where(seq_end, jnp.zeros_like(dyn), dyn)
        d_out = jnp.concatenate([dy_ref[...], dyn], axis=0).astype(BF16)
        d_act = _dot_nt(d_out, wd_ref[0])
        cge = jnp.concatenate([pgm_ref[0], pgn_ref[0]], axis=0).astype(F32)
        cve = jnp.concatenate([pvm_ref[0], pvn_ref[0]], axis=0).astype(F32)
        act, vjp_act = jax.vjp(lambda g, v: jax.nn.silu(g) * v, cge, cve)
        dcge, dcve = vjp_act(d_act)
        act_ref[0] = act[:tm].astype(BF16)

        def conv_t(dc, u_ref, cw):
            ahead1 = pltpu.roll(dc, rows - 1, 0)[:tm]
            ahead2 = pltpu.roll(dc, rows - 2, 0)[:tm]
            here = dc[:tm]
            du = cw[2:3] * here + cw[1:2] * ahead1 + cw[0:1] * ahead2
            u = u_ref[0].astype(F32)
            col = lambda x: jnp.sum(x, axis=0, keepdims=True)
            grad = jnp.concatenate([col(ahead2 * u), col(ahead1 * u), col(here * u), col(here),
                                    jnp.zeros((4, fb), F32)], axis=0)
            return du.astype(BF16), grad

        cwg, cwv = cg_ref[0], cv_ref[0]
        dug, grad_g = conv_t(dcge, ug_ref, cwg)
        duv, grad_v = conv_t(dcve, uv_ref, cwv)
        dug_ref[0] = dug
        duv_ref[0] = duv
        part = _dot(dug, wg_ref[0]) + _dot(duv, wv_ref[0])

        @pl.when(j == 0)
        def _():
            acc[...] = part

        @pl.when(j > 0)
        def _():
            acc[...] += part

        @pl.when(j == nj - 1)
        def _():
            dhn_ref[...] = acc[...]

        @pl.when(i == 0)
        def _():
            dcg_ref[j] = jnp.zeros((8, fb), F32)
            dcv_ref[j] = jnp.zeros((8, fb), F32)

        dcg_ref[j] += grad_g
        dcv_ref[j] += grad_v

    row = pl.BlockSpec((tm, D_MODEL), lambda i, j: (i, 0))
    u_main = pl.BlockSpec((1, tm, fb), lambda i, j: (j, i, 0))
    u_next = pl.BlockSpec((1, HALO, fb), lambda i, j: (j, jnp.minimum((i + 1) * hb, last_hb), 0))
    dc_spec = pl.BlockSpec((nj, 8, fb), lambda i, j: (0, 0, 0))
    u_shape = jax.ShapeDtypeStruct((FFN_GROUPS, n, fb), BF16)
    return pl.pallas_call(
        body, name="ffn_bwd", grid=(n // tm, nj),
        in_specs=[row, pl.BlockSpec((HALO, D_MODEL), lambda i, j: (jnp.minimum((i + 1) * hb, last_hb), 0)),
                  u_main, u_main, u_main, u_next, u_main, u_next, *_ffn_weight_specs()],
        out_specs=(u_main, u_main, u_main, row, dc_spec, dc_spec),
        out_shape=(u_shape, u_shape, u_shape, jax.ShapeDtypeStruct((n, D_MODEL), F32),
                   jax.ShapeDtypeStruct((nj, 8, fb), F32), jax.ShapeDtypeStruct((nj, 8, fb), F32)),
        scratch_shapes=[pltpu.VMEM((tm, D_MODEL), F32)],
        compiler_params=_cparams("arbitrary", "arbitrary"),
    )(dy, dy, ug, uv, pg, pg, pv, pv, w_up, w_up, conv, conv, w_down)


def _tn_grouped(a, b, name, shared_a, out_dtype=F32, tn=1024):
    groups = b.shape[0] if shared_a else a.shape[0]
    n_tok = a.shape[0] if shared_a else b.shape[0]
    k_dim, m_dim = a.shape[-1], b.shape[-1]

    def body(a_ref, b_ref, o_ref, acc):
        k = pl.program_id(1)
        a_t = a_ref[...] if shared_a else a_ref[0]
        b_t = b_ref[0] if shared_a else b_ref[...]
        part = _dot_tn(a_t.astype(BF16), b_t.astype(BF16))

        @pl.when(k == 0)
        def _():
            acc[...] = part

        @pl.when(k > 0)
        def _():
            acc[...] += part

        @pl.when(k == n_tok // tn - 1)
        def _():
            o_ref[0] = acc[...].astype(out_dtype)

    plain = lambda w: pl.BlockSpec((tn, w), lambda g, k: (k, 0))
    grouped = lambda w: pl.BlockSpec((1, tn, w), lambda g, k: (g, k, 0))
    return pl.pallas_call(
        body, name=name, grid=(groups, n_tok // tn),
        in_specs=[plain(k_dim), grouped(m_dim)] if shared_a else [grouped(k_dim), plain(m_dim)],
        out_specs=pl.BlockSpec((1, k_dim, m_dim), lambda g, k: (g, 0, 0)),
        out_shape=jax.ShapeDtypeStruct((groups, k_dim, m_dim), out_dtype),
        scratch_shapes=[pltpu.VMEM((k_dim, m_dim), F32)],
        compiler_params=_cparams("parallel", "arbitrary"),
    )(a, b)


def _s5_param_fn(lr, li, ldt, br, bi):
    dt = jnp.exp(ldt)
    mag = jnp.exp(lr * dt)
    ab_re = mag * jnp.cos(li * dt)
    ab_im = mag * jnp.sin(li * dt)
    nr = ab_re - 1.0
    ni = ab_im
    den = lr * lr + li * li
    q_re = (nr * lr + ni * li) / den
    q_im = (ni * lr - nr * li) / den
    bb_re = q_re * br - q_im * bi
    bb_im = q_re * bi + q_im * br
    return ab_re, ab_im, bb_re, bb_im


def _s5_param_fwd(lr, li, ldt, br, bi):
    def body(lr_ref, li_ref, ldt_ref, br_ref, bi_ref, ar_ref, ai_ref, bbr_ref, bbi_ref):
        ar, ai, bbr, bbi = _s5_param_fn(lr_ref[...], li_ref[...], ldt_ref[...], br_ref[...], bi_ref[...])
        ar_ref[...] = ar
        ai_ref[...] = ai
        bbr_ref[...] = bbr
        bbi_ref[...] = bbi

    return pl.pallas_call(
        body, name="s5_param_fwd",
        out_shape=(jax.ShapeDtypeStruct(lr.shape, F32), jax.ShapeDtypeStruct(lr.shape, F32),
                   jax.ShapeDtypeStruct(br.shape, F32), jax.ShapeDtypeStruct(br.shape, F32)),
    )(lr, li, ldt, br, bi)


def _s5_param_bwd(lr, li, ldt, br, bi, dar, dai, dbbr, dbbi):
    def body(lr_ref, li_ref, ldt_ref, br_ref, bi_ref, dar_ref, dai_ref, dbbr_ref, dbbi_ref,
             dlr_ref, dli_ref, dldt_ref, dbr_ref, dbi_ref):
        _, vjp = jax.vjp(_s5_param_fn, lr_ref[...], li_ref[...], ldt_ref[...], br_ref[...], bi_ref[...])
        dlr, dli, dldt, dbr, dbi = vjp((dar_ref[...], dai_ref[...], dbbr_ref[...], dbbi_ref[...]))
        dlr_ref[...] = dlr
        dli_ref[...] = dli
        dldt_ref[...] = dldt
        dbr_ref[...] = dbr
        dbi_ref[...] = dbi

    return pl.pallas_call(
        body, name="s5_param_bwd",
        out_shape=(jax.ShapeDtypeStruct(lr.shape, F32), jax.ShapeDtypeStruct(lr.shape, F32),
                   jax.ShapeDtypeStruct(ldt.shape, F32), jax.ShapeDtypeStruct(br.shape, F32),
                   jax.ShapeDtypeStruct(br.shape, F32)),
    )(lr, li, ldt, br, bi, dar, dai, dbbr, dbbi)


S5_CHUNK = 1024
S5_STATES = 512
S5_BLOCKS = 4


def _cpow_rows(ar, ai, count):
    rs, im = [ar], [ai]
    for _ in range(count - 1):
        pr, pi = rs[-1], im[-1]
        rs.append(pr * ar - pi * ai)
        im.append(pr * ai + pi * ar)
    return rs, im


def _scan_in_groups(vr, vi, pr, pi, rm, reverse):
    n, width = vr.shape
    vr = vr.reshape(n // SUBLANES, SUBLANES, width)
    vi = vi.reshape(n // SUBLANES, SUBLANES, width)
    row = rm[0:SUBLANES]
    for k in (1, 2, 4):
        shift = SUBLANES - k if reverse else k
        keep = row < SUBLANES - k if reverse else row >= k
        kr = jnp.where(keep, pr[k - 1], 0.0)
        ki = jnp.where(keep, pi[k - 1], 0.0)
        sr, si = pltpu.roll(vr, shift, 1), pltpu.roll(vi, shift, 1)
        vr, vi = vr + kr * sr - ki * si, vi + kr * si + ki * sr
    return vr.reshape(n, width), vi.reshape(n, width)


def _carry_over_groups(xr_s, xi_s, wr, wi, c0r, c0i, reverse):
    groups = xr_s.shape[0] // SUBLANES
    pick = 0 if reverse else SUBLANES - 1

    def step(q, carry):
        cr, ci = carry
        r = groups - 1 - q if reverse else q
        o = pl.multiple_of(r * SUBLANES, SUBLANES)
        vr = xr_s[pl.ds(o, SUBLANES), :]
        vi = xi_s[pl.ds(o, SUBLANES), :]
        nr = vr + wr * cr - wi * ci
        ni = vi + wr * ci + wi * cr
        xr_s[pl.ds(o, SUBLANES), :] = nr
        xi_s[pl.ds(o, SUBLANES), :] = ni
        return (jnp.broadcast_to(nr[pick:pick + 1], nr.shape), jnp.broadcast_to(ni[pick:pick + 1], ni.shape))

    return lax.fori_loop(0, groups, step, (c0r, c0i), unroll=4)


def _s5_state_scan(u_b, bbr, bbi, pr, pi, rm, xr_s, xi_s, c0r, c0i):
    bur = _dot(u_b, bbr)
    bui = _dot(u_b, bbi)
    bur, bui = _scan_in_groups(bur, bui, pr, pi, rm, False)
    xr_s[...] = bur
    xi_s[...] = bui
    w8r = jnp.concatenate(pr, axis=0)
    w8i = jnp.concatenate(pi, axis=0)
    return _carry_over_groups(xr_s, xi_s, w8r, w8i, c0r, c0i, False)


def _s5_fwd(u, a_re, a_im, bbr, bbi, cr, ci, d_skip, n_seq):
    n = u.shape[0]
    seq_len = n // n_seq
    nt = seq_len // S5_CHUNK
    tc = S5_CHUNK

    def body(u_ref, ar_ref, ai_ref, bbr_ref, bbi_ref, cr_ref, ci_ref, d_ref, y_ref, str_ref, sti_ref, xrb_ref, xib_ref,
             xr_s, xi_s, car_r, car_i):
        t = pl.program_id(2)

        @pl.when(t == 0)
        def _():
            car_r[...] = jnp.zeros_like(car_r)
            car_i[...] = jnp.zeros_like(car_i)
        pr, pi = _cpow_rows(ar_ref[0], ai_ref[0], SUBLANES)
        rm = lax.broadcasted_iota(jnp.int32, (tc, S5_STATES), 0) & (SUBLANES - 1)
        str_ref[0, 0] = car_r[...]
        sti_ref[0, 0] = car_i[...]
        u_t = u_ref[...]
        cfr, cfi = _s5_state_scan(u_t.astype(BF16), bbr_ref[0], bbi_ref[0], pr, pi, rm, xr_s, xi_s,
                                  car_r[...], car_i[...])
        car_r[...] = cfr
        car_i[...] = cfi
        xr_b = xr_s[...].astype(BF16)
        xi_b = xi_s[...].astype(BF16)
        xrb_ref[...] = xr_b
        xib_ref[...] = xi_b
        y_ref[...] = _dot(xr_b, cr_ref[0]) - _dot(xi_b, ci_ref[0]) + d_ref[...] * u_t

    x_spec = pl.BlockSpec((tc, S5_STATES), lambda cb, b, t: (b * nt + t, cb))
    x_shape = jax.ShapeDtypeStruct((n, S5_BLOCKS * S5_STATES), BF16)
    u_spec = pl.BlockSpec((tc, LANES), lambda cb, b, t: (b * nt + t, cb))
    a_spec = pl.BlockSpec((1, 1, S5_STATES), lambda cb, b, t: (cb, 0, 0))
    bb_spec = pl.BlockSpec((1, LANES, S5_STATES), lambda cb, b, t: (cb, 0, 0))
    c_spec = pl.BlockSpec((1, S5_STATES, LANES), lambda cb, b, t: (cb, 0, 0))
    st_spec = pl.BlockSpec((1, 1, SUBLANES, S5_STATES), lambda cb, b, t: (cb, b * nt + t, 0, 0))
    st_shape = jax.ShapeDtypeStruct((S5_BLOCKS, n_seq * nt, SUBLANES, S5_STATES), F32)
    return pl.pallas_call(
        body, name="s5_fwd", grid=(S5_BLOCKS, n_seq, nt),
        in_specs=[u_spec, a_spec, a_spec, bb_spec, bb_spec, c_spec, c_spec,
                  pl.BlockSpec((1, LANES), lambda cb, b, t: (0, cb))],
        out_specs=(u_spec, st_spec, st_spec, x_spec, x_spec),
        out_shape=(jax.ShapeDtypeStruct((n, D_SSM), F32), st_shape, st_shape, x_shape, x_shape),
        scratch_shapes=[pltpu.VMEM((tc, S5_STATES), F32), pltpu.VMEM((tc, S5_STATES), F32),
                        pltpu.VMEM((SUBLANES, S5_STATES), F32), pltpu.VMEM((SUBLANES, S5_STATES), F32)],
        compiler_params=_cparams("parallel", "arbitrary", "arbitrary"),
    )(u, a_re, a_im, bbr, bbi, cr, ci, d_skip)


def _s5_bwd(u, dy, xs_r, xs_i, st_r, st_i, a_re, a_im, bbr, bbi, cr, ci, d_skip, n_seq):
    n = u.shape[0]
    seq_len = n // n_seq
    nt = seq_len // S5_CHUNK
    tc = S5_CHUNK

    def body(u_ref, dy_ref, xrb_ref, xib_ref, str_ref, sti_ref, ar_ref, ai_ref, bbr_ref, bbi_ref, cr_ref, ci_ref, d_ref,
             du_ref, dbbr_ref, dbbi_ref, dcr_ref, dci_ref, dar_ref, dai_ref, dd_ref,
             gr_s, gi_s, car_r, car_i):
        b, t = pl.program_id(1), pl.program_id(2)

        @pl.when((b == 0) & (t == 0))
        def _():
            for ref in (dbbr_ref, dbbi_ref, dcr_ref, dci_ref, dar_ref, dai_ref, dd_ref):
                ref[...] = jnp.zeros_like(ref)

        @pl.when(t == 0)
        def _():
            car_r[...] = jnp.zeros_like(car_r)
            car_i[...] = jnp.zeros_like(car_i)
        ar, ai = ar_ref[0], ai_ref[0]
        pr, pi = _cpow_rows(ar, ai, SUBLANES)
        row = lax.broadcasted_iota(jnp.int32, (tc, S5_STATES), 0)
        rm = row & (SUBLANES - 1)
        u_t = u_ref[...]
        u_b = u_t.astype(BF16)
        dy_t = dy_ref[...]
        dy_b = dy_t.astype(BF16)
        s0r, s0i = str_ref[0, 0], sti_ref[0, 0]
        xr_b, xi_b = xrb_ref[...], xib_ref[...]
        xr, xi = xr_b.astype(F32), xi_b.astype(F32)
        gr = _dot_nt(dy_b, cr_ref[0])
        gi = -_dot_nt(dy_b, ci_ref[0])
        npi = [-v for v in pi]
        gr, gi = _scan_in_groups(gr, gi, pr, npi, rm, True)
        gr_s[...] = gr
        gi_s[...] = gi
        w8r = jnp.concatenate(pr[::-1], axis=0)
        w8i = jnp.concatenate(npi[::-1], axis=0)
        cfr, cfi = _carry_over_groups(gr_s, gi_s, w8r, w8i, car_r[...], car_i[...], True)
        car_r[...] = cfr
        car_i[...] = cfi
        gr, gi = gr_s[...], gi_s[...]
        gr_b, gi_b = gr.astype(BF16), gi.astype(BF16)
        du_ref[...] = _dot_nt(gr_b, bbr_ref[0]) + _dot_nt(gi_b, bbi_ref[0]) + d_ref[...] * dy_t
        dbbr_ref[0] += _dot_tn(u_b, gr_b)
        dbbi_ref[0] += _dot_tn(u_b, gi_b)
        dcr_ref[0] += _dot_tn(xr_b, dy_b)
        dci_ref[0] -= _dot_tn(xi_b, dy_b)
        dd_ref[0] += jnp.sum((dy_t * u_t).reshape(tc // SUBLANES, SUBLANES, LANES), axis=0)
        first = row == 0
        xpr = jnp.where(first, jnp.broadcast_to(s0r[0:1], xr.shape), pltpu.roll(xr, 1, 0))
        xpi = jnp.where(first, jnp.broadcast_to(s0i[0:1], xi.shape), pltpu.roll(xi, 1, 0))
        shp = (tc // SUBLANES, SUBLANES, S5_STATES)
        dar_ref[0] += jnp.sum((gr * xpr + gi * xpi).reshape(shp), axis=0)
        dai_ref[0] += jnp.sum((gi * xpr - gr * xpi).reshape(shp), axis=0)

    u_spec = pl.BlockSpec((tc, LANES), lambda cb, b, t: (b * nt + nt - 1 - t, cb))
    a_spec = pl.BlockSpec((1, 1, S5_STATES), lambda cb, b, t: (cb, 0, 0))
    bb_spec = pl.BlockSpec((1, LANES, S5_STATES), lambda cb, b, t: (cb, 0, 0))
    c_spec = pl.BlockSpec((1, S5_STATES, LANES), lambda cb, b, t: (cb, 0, 0))
    st_spec = pl.BlockSpec((1, 1, SUBLANES, S5_STATES), lambda cb, b, t: (cb, b * nt + nt - 1 - t, 0, 0))
    da_spec = pl.BlockSpec((1, SUBLANES, S5_STATES), lambda cb, b, t: (cb, 0, 0))
    dd_spec = pl.BlockSpec((1, SUBLANES, LANES), lambda cb, b, t: (cb, 0, 0))
    big = pltpu.VMEM((tc, S5_STATES), F32)
    small = pltpu.VMEM((SUBLANES, S5_STATES), F32)
    x_spec = pl.BlockSpec((tc, S5_STATES), lambda cb, b, t: (b * nt + nt - 1 - t, cb))
    return pl.pallas_call(
        body, name="s5_bwd", grid=(S5_BLOCKS, n_seq, nt),
        in_specs=[u_spec, u_spec, x_spec, x_spec, st_spec, st_spec, a_spec, a_spec, bb_spec, bb_spec, c_spec, c_spec,
                  pl.BlockSpec((1, LANES), lambda cb, b, t: (0, cb))],
        out_specs=(u_spec, bb_spec, bb_spec, c_spec, c_spec, da_spec, da_spec, dd_spec),
        out_shape=(jax.ShapeDtypeStruct((n, D_SSM), F32),
                   jax.ShapeDtypeStruct((S5_BLOCKS, LANES, S5_STATES), F32),
                   jax.ShapeDtypeStruct((S5_BLOCKS, LANES, S5_STATES), F32),
                   jax.ShapeDtypeStruct((S5_BLOCKS, S5_STATES, LANES), F32),
                   jax.ShapeDtypeStruct((S5_BLOCKS, S5_STATES, LANES), F32),
                   jax.ShapeDtypeStruct((S5_BLOCKS, SUBLANES, S5_STATES), F32),
                   jax.ShapeDtypeStruct((S5_BLOCKS, SUBLANES, S5_STATES), F32),
                   jax.ShapeDtypeStruct((S5_BLOCKS, SUBLANES, LANES), F32)),
        scratch_shapes=[big, big, small, small],
        compiler_params=_cparams("parallel", "arbitrary", "arbitrary"),
    )(u, dy, xs_r, xs_i, st_r, st_i, a_re, a_im, bbr, bbi, cr, ci, d_skip)


CUM_BLOCK = 128


def _tri(lower):
    r = lax.broadcasted_iota(jnp.int32, (CUM_BLOCK, CUM_BLOCK), 0)
    c = lax.broadcasted_iota(jnp.int32, (CUM_BLOCK, CUM_BLOCK), 1)
    return jnp.where(r >= c if lower else r <= c, 1.0, 0.0).astype(F32)


def _fprep_fwd(fl, bf, n_seq):
    n = fl.shape[0]
    seq_len = n // n_seq
    nb = seq_len // CUM_BLOCK

    def body(fl_ref, bf_ref, cum_ref):
        tril = _tri(True)
        carry = jnp.zeros((1, LANES), F32)
        for blk in range(nb):
            rows = slice(blk * CUM_BLOCK, (blk + 1) * CUM_BLOCK)
            lf = jax.nn.log_sigmoid(fl_ref[rows, :] + bf_ref[...])
            cs = jnp.dot(tril, lf, preferred_element_type=F32, precision=HIGHEST) + carry
            cum_ref[rows, :] = cs
            carry = cs[CUM_BLOCK - 1:CUM_BLOCK, :]

    spec = pl.BlockSpec((seq_len, LANES), lambda b: (b, 0))
    return pl.pallas_call(
        body, name="fprep_fwd", grid=(n_seq,), in_specs=[spec, pl.BlockSpec((1, LANES), lambda b: (0, 0))],
        out_specs=spec, out_shape=jax.ShapeDtypeStruct((n, LANES), F32), compiler_params=_cparams("parallel"),
    )(fl, bf)


def _fprep_bwd(dcum, fl, bf, n_seq):
    n = fl.shape[0]
    seq_len = n // n_seq
    nb = seq_len // CUM_BLOCK

    def body(dcum_ref, fl_ref, bf_ref, dfl_ref, dbf_ref):
        triu = _tri(False)
        lane = lax.broadcasted_iota(jnp.int32, (CUM_BLOCK, LANES), 1)
        carry = jnp.zeros((1, LANES), F32)
        total = jnp.zeros((1, LANES), F32)
        for blk in reversed(range(nb)):
            rows = slice(blk * CUM_BLOCK, (blk + 1) * CUM_BLOCK)
            rs = jnp.dot(triu, dcum_ref[rows, :], preferred_element_type=F32, precision=HIGHEST) + carry
            carry = rs[0:1, :]
            _, vjp = jax.vjp(jax.nn.log_sigmoid, fl_ref[rows, :] + bf_ref[...])
            dz = jnp.where(lane < N_HEADS, vjp(rs)[0], 0.0)
            dfl_ref[rows, :] = dz
            total = total + jnp.sum(dz, axis=0, keepdims=True)
        dbf_ref[0] = total

    spec = pl.BlockSpec((seq_len, LANES), lambda b: (b, 0))
    return pl.pallas_call(
        body, name="fprep_bwd", grid=(n_seq,), in_specs=[spec, spec, pl.BlockSpec((1, LANES), lambda b: (0, 0))],
        out_specs=(spec, pl.BlockSpec((1, 1, LANES), lambda b: (b, 0, 0))),
        out_shape=(jax.ShapeDtypeStruct((n, LANES), F32), jax.ShapeDtypeStruct((n_seq, 1, LANES), F32)),
        compiler_params=_cparams("parallel"),
    )(dcum, fl, bf)


ATT_TQ = 256
ATT_KSTEP = 256
ATT_PAIRS = 2
ATT_SCALE = HEAD_DIM ** -0.5
NEG_BIG = -1e30


assert ATT_KSTEP == ATT_TQ


def _scores(q_scaled, kb, row_bias, ck, kend):
    s = _dot_nt(q_scaled, kb) - ck
    if row_bias is not None:
        s = s + row_bias
    r = lax.broadcasted_iota(jnp.int32, (ATT_TQ, ATT_TQ), 0)
    c = lax.broadcasted_iota(jnp.int32, (ATT_TQ, ATT_TQ), 1)
    diag = jnp.where(r >= c, s[:, kend - ATT_TQ:], NEG_BIG)
    return diag if kend == ATT_TQ else jnp.concatenate([s[:, :kend - ATT_TQ], diag], axis=1)


def _attn_specs(n_seq, seq_len):
    nq = seq_len // ATT_TQ
    width = ATT_PAIRS * LANES
    per = D_ATTN // width
    q_spec = pl.BlockSpec((ATT_TQ, width), lambda b, h, q: (b * nq + q, h))
    k_spec = pl.BlockSpec((seq_len, width), lambda b, h, q: (b, per + h))
    v_spec = pl.BlockSpec((seq_len, width), lambda b, h, q: (b, 2 * per + h))
    cq_spec = pl.BlockSpec((1, 2 * ATT_PAIRS, ATT_TQ, 1), lambda b, h, q: (b, h, q, 0))
    ck_spec = pl.BlockSpec((1, 2 * ATT_PAIRS, 1, seq_len), lambda b, h, q: (b, h, 0, 0))
    return nq, q_spec, k_spec, v_spec, cq_spec, ck_spec


def _own_cum(cum_ref, head):
    lane = lax.broadcasted_iota(jnp.int32, (1, LANES), 1)
    mine = lane == 2 * ATT_PAIRS * pl.program_id(1) + head
    return jnp.sum(jnp.where(mine, cum_ref[...], 0.0), axis=1, keepdims=True)


def _head_selectors():
    head0 = lax.broadcasted_iota(jnp.int32, (1, LANES), 1) < HEAD_DIM
    return head0, (head0, jnp.logical_not(head0))


def _for_key_range(qi, seq_len, run):
    per = ATT_KSTEP // ATT_TQ
    for g in range(seq_len // ATT_KSTEP):
        pl.when(qi // per == g)(functools.partial(run, (g + 1) * ATT_KSTEP))


def _attn_fwd(qkv, cum, ck, n_seq):
    n = qkv.shape[0]
    seq_len = n // n_seq
    nq, q_spec, k_spec, v_spec, cq_spec, ck_spec = _attn_specs(n_seq, seq_len)
    cum_spec = pl.BlockSpec((ATT_TQ, LANES), lambda b, h, q: (b * nq + q, 0))

    def body(q_ref, k_ref, v_ref, cum_ref, ck_ref, o_ref, lse_ref):
        qi = pl.program_id(2)
        head0, sels = _head_selectors()

        def run(kend):
            for pair in range(ATT_PAIRS):
                win = slice(pair * LANES, (pair + 1) * LANES)
                q2 = q_ref[:, win]
                kb = k_ref[0:kend, win]
                vb = v_ref[0:kend, win]
                outs = []
                for e in range(2):
                    head = 2 * pair + e
                    qe = jnp.where(sels[e], q2 * ATT_SCALE, 0.0).astype(BF16)
                    s = _scores(qe, kb, None, ck_ref[0, head, :, 0:kend], kend)
                    mx = jnp.max(s, axis=1, keepdims=True)
                    p = jnp.exp(s - mx)
                    den = jnp.sum(p, axis=1, keepdims=True)
                    outs.append(_dot(p.astype(BF16), vb) / den)
                    lse_ref[0, head] = _own_cum(cum_ref, head) + mx + jnp.log(den)
                o_ref[:, win] = jnp.where(head0, outs[0], outs[1])

        _for_key_range(qi, seq_len, run)

    return pl.pallas_call(
        body, name="attn_fwd", grid=(n_seq, N_HEADS // 2 // ATT_PAIRS, nq),
        in_specs=[q_spec, k_spec, v_spec, cum_spec, ck_spec],
        out_specs=(q_spec, cq_spec),
        out_shape=(jax.ShapeDtypeStruct((n, D_ATTN), F32), jax.ShapeDtypeStruct((n_seq, N_HEADS, seq_len, 1), F32)),
        compiler_params=_cparams("parallel", "parallel", "parallel"),
    )(qkv, qkv, qkv, cum, ck)


def _attn_bwd(qkv, cum, ck, o, do, lse, n_seq):
    n = qkv.shape[0]
    seq_len = n // n_seq
    nq, q_spec, k_spec, v_spec, cq_spec, ck_spec = _attn_specs(n_seq, seq_len)
    kv_out = pl.BlockSpec((seq_len, ATT_PAIRS * LANES), lambda b, h, q: (b, h))
    cum_spec = pl.BlockSpec((ATT_TQ, LANES), lambda b, h, q: (b * nq + q, 0))

    def body(q_ref, k_ref, v_ref, cum_ref, ck_ref, o_ref, do_ref, lse_ref, dq_ref, dk_ref, dv_ref, dcq_ref, dck_ref):
        qi = pl.program_id(2)

        @pl.when(qi == 0)
        def _():
            dk_ref[...] = jnp.zeros_like(dk_ref)
            dv_ref[...] = jnp.zeros_like(dv_ref)
            dck_ref[...] = jnp.zeros_like(dck_ref)
        head0, sels = _head_selectors()

        def run(kend):
            for pair in range(ATT_PAIRS):
                win = slice(pair * LANES, (pair + 1) * LANES)
                q2 = q_ref[:, win]
                do2 = do_ref[:, win]
                o2 = o_ref[:, win]
                kb = k_ref[0:kend, win]
                vb = v_ref[0:kend, win]
                dqs = []
                dk = jnp.zeros((kend, LANES), F32)
                dv = jnp.zeros((kend, LANES), F32)
                for e in range(2):
                    head = 2 * pair + e
                    qe = jnp.where(sels[e], q2 * ATT_SCALE, 0.0).astype(BF16)
                    doe = jnp.where(sels[e], do2, 0.0)
                    doe_b = doe.astype(BF16)
                    delta = jnp.sum(doe * o2, axis=1, keepdims=True)
                    bias = _own_cum(cum_ref, head) - lse_ref[0, head]
                    p = jnp.exp(_scores(qe, kb, bias, ck_ref[0, head, :, 0:kend], kend))
                    ds = p * (_dot_nt(doe_b, vb) - delta)
                    ds_b = ds.astype(BF16)
                    dqs.append(_dot(ds_b, kb))
                    dk = dk + _dot_tn(ds_b, qe)
                    dv = dv + _dot_tn(p.astype(BF16), doe_b)
                    dcq_ref[0, head] = jnp.sum(ds, axis=1, keepdims=True)
                    dck_ref[0, head, :, 0:kend] -= jnp.sum(ds, axis=0, keepdims=True)
                dk_ref[0:kend, win] += dk
                dv_ref[0:kend, win] += dv
                dq_ref[:, win] = jnp.where(head0, dqs[0], dqs[1]) * ATT_SCALE

        _for_key_range(qi, seq_len, run)

    return pl.pallas_call(
        body, name="attn_bwd", grid=(n_seq, N_HEADS // 2 // ATT_PAIRS, nq),
        in_specs=[q_spec, k_spec, v_spec, cum_spec, ck_spec, q_spec, q_spec, cq_spec],
        out_specs=(q_spec, kv_out, kv_out, cq_spec, ck_spec),
        out_shape=(jax.ShapeDtypeStruct((n, D_ATTN), F32), jax.ShapeDtypeStruct((n, D_ATTN), F32),
                   jax.ShapeDtypeStruct((n, D_ATTN), F32),
                   jax.ShapeDtypeStruct((n_seq, N_HEADS, seq_len, 1), F32),
                   jax.ShapeDtypeStruct((n_seq, N_HEADS, 1, seq_len), F32)),
        compiler_params=_cparams("parallel", "parallel", "arbitrary"),
    )(qkv, qkv, qkv, cum, ck, o, do, lse)


WEIGHT_NAMES = ("norm_mix", "w_in", "b_forget", "lam_re", "lam_im", "b_re", "b_im", "c_re", "c_im", "d_skip", "log_dt",
                "w_glu", "b_glu", "q_norm", "k_norm", "norm_out_ssm", "norm_out_attn", "w_out", "norm_ffn", "w_up",
                "conv_w", "conv_b", "w_down")
SHARDED = ("w_in", "w_glu", "w_out", "w_up", "conv_w", "w_down")
ADAM_TILE = {"w_in": (257, 256), "w_glu": (64, 512), "w_out": (128, 1024), "w_up": (688, 256), "conv_w": (3, 688),
             "w_down": (344, 1024)}
PACK_ROWS = SUBLANES * LANES


def _after_all(*arrays):
    return sum(a[(0,) * a.ndim].astype(F32) for a in arrays).reshape(1, 1)


def _pad_to(a, axis, size):
    pad = [(0, 0)] * a.ndim
    pad[axis] = (0, size - a.shape[axis])
    return jnp.pad(a, pad)


def _block_diag(t, transpose):
    t4 = t.reshape(S5_BLOCKS, 8, SSM_GROUP, SSM_STATE)
    eye = jnp.eye(8, dtype=t.dtype)
    if transpose:
        e = jnp.swapaxes(t4, 2, 3)[:, :, :, None, :] * eye[None, :, None, :, None]
        return e.reshape(S5_BLOCKS, S5_STATES, LANES)
    e = t4[:, :, :, None, :] * eye[None, :, None, :, None]
    return e.reshape(S5_BLOCKS, LANES, S5_STATES)


def _block_diag_extract(m, transpose):
    if transpose:
        m5 = m.reshape(S5_BLOCKS, 8, SSM_STATE, 8, SSM_GROUP)
        d = jnp.stack([m5[:, i, :, i, :] for i in range(8)], axis=1)
        return jnp.swapaxes(d, 2, 3).reshape(N_GROUPS, SSM_GROUP, SSM_STATE)
    m5 = m.reshape(S5_BLOCKS, 8, SSM_GROUP, 8, SSM_STATE)
    d = jnp.stack([m5[:, i, :, i, :] for i in range(8)], axis=1)
    return d.reshape(N_GROUPS, SSM_GROUP, SSM_STATE)


def _pack(pieces):
    flat = jnp.concatenate([p.reshape(-1).astype(F32) for p in pieces])
    size = -(-flat.shape[0] // PACK_ROWS) * PACK_ROWS
    return _pad_to(flat, 0, size).reshape(-1, LANES)


def _unpack(packed, shapes):
    flat = packed.reshape(-1)
    out, off = [], 0
    for shp in shapes:
        size = math.prod(shp)
        out.append(flat[off:off + size].reshape(shp))
        off += size
    return out


def kernel(x, norm_mix, w_in, b_forget, lam_re, lam_im, b_re, b_im, c_re, c_im, d_skip, log_dt, w_glu, b_glu, q_norm, k_norm, norm_out_ssm, norm_out_attn, w_out, norm_ffn, w_up, conv_w, conv_b, w_down, loss_target, m_norm_mix, m_w_in, m_b_forget, m_lam_re, m_lam_im, m_b_re, m_b_im, m_c_re, m_c_im, m_d_skip, m_log_dt, m_w_glu, m_b_glu, m_q_norm, m_k_norm, m_norm_out_ssm, m_norm_out_attn, m_w_out, m_norm_ffn, m_w_up, m_conv_w, m_conv_b, m_w_down, v_norm_mix, v_w_in, v_b_forget, v_lam_re, v_lam_im, v_b_re, v_b_im, v_c_re, v_c_im, v_d_skip, v_log_dt, v_w_glu, v_b_glu, v_q_norm, v_k_norm, v_norm_out_ssm, v_norm_out_attn, v_w_out, v_norm_ffn, v_w_up, v_conv_w, v_conv_b, v_w_down):
    given = dict(locals())
    weights = {k: given[k] for k in WEIGHT_NAMES}
    mom1 = {k: given["m_" + k] for k in WEIGHT_NAMES}
    mom2 = {k: given["v_" + k] for k in WEIGHT_NAMES}
    n_seq, seq_len, _ = x.shape
    n = n_seq * seq_len
    xf = x.reshape(n, D_MODEL)
    target = loss_target.reshape(n, D_MODEL)
    me_idx = 4 * lax.axis_index("x") + 2 * lax.axis_index("y") + lax.axis_index("c")

    in_flags = [False] * 2
    in_sems = _exchange_start([jnp.swapaxes(w_in[0], 0, 1).astype(BF16), conv_w[0]], in_flags, norm_mix,
                              "gather_in_start", 3, NEAR_PEERS)
    fill_own = lambda got, mine: lax.dynamic_update_index_in_dim(got, mine, me_idx, 0)

    lr3 = lam_re[0].reshape(N_GROUPS, 1, SSM_STATE)
    li3 = lam_im[0].reshape(N_GROUPS, 1, SSM_STATE)
    ldt3 = log_dt[0].reshape(N_GROUPS, 1, 1)
    br_t = jnp.swapaxes(b_re[0], 1, 2)
    bi_t = jnp.swapaxes(b_im[0], 1, 2)
    ab_re, ab_im, bb_re, bb_im = _s5_param_fwd(lr3, li3, ldt3, br_t, bi_t)
    a_re = ab_re.reshape(S5_BLOCKS, 1, S5_STATES)
    a_im = ab_im.reshape(S5_BLOCKS, 1, S5_STATES)
    bbr = _block_diag(bb_re, False).astype(BF16)
    bbi = _block_diag(bb_im, False).astype(BF16)
    cr = _block_diag(c_re[0], True).astype(BF16)
    ci = _block_diag(c_im[0], True).astype(BF16)
    avg = jnp.kron(jnp.eye(N_HEADS, dtype=F32), jnp.full((HEAD_DIM, HEAD_DIM), 1.0 / HEAD_DIM, F32)).astype(BF16)
    qg = jnp.tile(q_norm, (1, N_HEADS))
    kg = jnp.tile(k_norm, (1, N_HEADS))
    row_shards = [w_down[0].astype(BF16), w_out[0].astype(BF16), w_glu[0].astype(BF16)]

    (own_in, own_cw), near = _exchange_wait(in_sems[0], in_sems[1], in_sems[2], in_sems[3], in_flags,
                                            _after_all(a_re, a_im, bbr, bbi, cr, ci, avg, qg, kg, *row_shards),
                                            "gather_in_wait", NEAR_PEERS)
    relay = _relay_start(list(near), "gather_in_relay_start", 6)
    g_in, g_cw = _relay_wait(relay[0], relay[1], relay[2], relay[3], "gather_in_relay_wait")
    g_in = fill_own(g_in, own_in)
    g_cw = fill_own(g_cw, own_cw)
    row_flags = [False] * 3
    r_sems = _exchange_start(row_shards, row_flags, g_in, "gather_rows_start", 0)
    u_sems = _exchange_start([jnp.swapaxes(w_up[0], 0, 1).astype(BF16)], [False], r_sems[4], "gather_up_start", 5)
    norm_mix = norm_mix + u_sems[4][0, 0]
    w_in_p = _pad_to(g_in.reshape(D_IN, D_MODEL), 0, D_IN_PAD)

    hn, u, qkv, raw, fl = _inproj_fwd(xf, norm_mix, w_in_p, avg, qg, kg)
    yc, st_r, st_i, xs_r, xs_i = _s5_fwd(u, a_re, a_im, bbr, bbi, cr, ci, d_skip, n_seq)
    bf = _pad_to(b_forget, 1, LANES)
    cum = _fprep_fwd(fl, bf, n_seq)
    cum8 = jnp.swapaxes(cum[:, :N_HEADS].reshape(n_seq, seq_len, N_HEADS), 1, 2)
    ck = cum8[:, :, None, :]
    ya, lse = _attn_fwd(qkv, cum, ck, n_seq)
    own_rows, got_rows = _exchange_wait(r_sems[0], r_sems[1], r_sems[2], r_sems[3], row_flags, ya, "gather_rows_wait")
    g_down, g_out, g_glu = [fill_own(g, o) for o, g in zip(own_rows, got_rows)]
    w_glu_f = g_glu.reshape(D_SSM, D_SSM)
    w_out_f = g_out.reshape(D_MODEL, D_MODEL)
    conv_st = _pad_to(jnp.concatenate([g_cw, conv_b.reshape(N_DEV, 1, -1)], axis=1), 1, SUBLANES)
    w_down4 = g_down.reshape(FFN_GROUPS, FFN_GROUP, D_MODEL)
    h1, hn2, mixed = _mix_fwd(xf, yc, ya, w_glu_f, b_glu, norm_out_ssm, norm_out_attn, w_out_f, norm_ffn)
    (own_up,), (g_up,) = _exchange_wait(u_sems[0], u_sems[1], u_sems[2], u_sems[3], [False], _after_all(h1, hn2),
                                        "gather_up_wait")
    g_up = fill_own(g_up, own_up)
    ug, uv, pg, pv, dy, loss_part = _ffn_fwd(hn2, h1, target, g_up, conv_st, w_down4, seq_len)
    loss_local = 0.5 * jnp.sum(loss_part) / D_MODEL

    dug, duv, act, dhn2, dcg, dcv = _ffn_bwd(dy, ug, uv, pg, pv, g_up, conv_st, w_down4, seq_len)
    dh1, dyc, dya, gl_b, dz_b, d_gs, d_ga, d_gf, d_bglu = _mix_bwd(
        dy, dhn2[None], h1, yc, ya, w_glu_f, b_glu, norm_out_ssm, norm_out_attn, w_out_f, norm_ffn)

    gw_glu = _tn_matmul(gl_b, dz_b, "dw_glu", D_SSM, D_SSM, out_dtype=BF16, tn=n)
    gw_out = _tn_matmul(mixed, dh1, "dw_out", D_MODEL, D_MODEL, out_dtype=BF16, tn=n // 2)
    gw_up = jnp.concatenate([_tn_grouped(dug, hn2, "dw_up_gate", False, BF16, tn=n),
                             _tn_grouped(duv, hn2, "dw_up_val", False, BF16, tn=n)], axis=0)
    gw_down = _tn_grouped(act, dy, "dw_down", False, BF16, tn=n // 2)
    g_conv = jnp.concatenate([dcg, dcv], axis=0)
    by_cols = lambda g, c: jnp.swapaxes(g.reshape(g.shape[0], N_DEV, c), 0, 1)
    early_flags = [True] * 4
    early_names = ("w_down", "w_out", "w_glu", "w_up")
    g_sems = _exchange_start(
        [gw_down.reshape(N_DEV, -1, D_MODEL), gw_out.reshape(N_DEV, -1, D_MODEL), gw_glu.reshape(N_DEV, -1, D_SSM), gw_up],
        early_flags, dyc, "grad_early_start", 1)
    started = g_sems[4][0, 0]

    du, dbbr, dbbi, dcr, dci, dar, dai, ddk = _s5_bwd(u, dyc, xs_r, xs_i, st_r, st_i, a_re, a_im, bbr, bbi, cr, ci,
                                                      d_skip + started, n_seq)
    partial_early = {
        "ab_re": jnp.sum(dar, axis=1), "ab_im": jnp.sum(dai, axis=1),
        "bb_re": _block_diag_extract(dbbr, False), "bb_im": _block_diag_extract(dbbi, False),
        "c_re": _block_diag_extract(dcr, True), "c_im": _block_diag_extract(dci, True),
        "d_skip": jnp.sum(ddk, axis=1), "b_glu": d_bglu,
        "norm_out_ssm": d_gs, "norm_out_attn": d_ga, "norm_ffn": d_gf, "conv_b": g_conv[:, 3],
    }
    early_keys = tuple(partial_early)
    early_shapes = [partial_early[k].shape for k in early_keys]
    p_sems = _exchange_start([_pack([partial_early[k] for k in early_keys])], [False], du, "small_early_start", 2)
    started = started + p_sems[4][0, 0]

    dqn, dkn, dv, dcq, dck = _attn_bwd(qkv, cum, ck + started, ya, dya, lse, n_seq)
    dcum8 = dcq[:, :, :, 0] + dck.reshape(n_seq, N_HEADS, seq_len)
    dcum = _pad_to(jnp.swapaxes(dcum8, 1, 2).reshape(n, N_HEADS), 1, LANES)
    dfl, dbf = _fprep_bwd(dcum, fl, bf, n_seq)
    dx, dproj, d_gmix, d_qg, d_kg = _inproj_bwd(xf, norm_mix, w_in_p, avg, qg, kg, raw, du, dqn, dkn, dv, dfl, dh1)

    gw_in = _tn_matmul(dproj, hn, "dw_in", D_IN_PAD, D_MODEL, out_rows=D_IN, out_dtype=BF16, tn=n // 4)
    partial_late = {
        "norm_mix": d_gmix, "b_forget": jnp.sum(dbf, axis=(0, 1))[:N_HEADS],
        "q_norm": jnp.sum(d_qg.reshape(N_HEADS, HEAD_DIM), axis=0),
        "k_norm": jnp.sum(d_kg.reshape(N_HEADS, HEAD_DIM), axis=0), "loss": loss_local.reshape(1),
    }
    late_keys = tuple(partial_late)
    late_shapes = [partial_late[k].shape for k in late_keys]

    late_flags = [True, True, False]
    l_sems = _exchange_start(
        [gw_in.reshape(N_DEV, D_IN // N_DEV, D_MODEL).astype(BF16), g_conv[:, :3],
         _pack([partial_late[k] for k in late_keys])],
        late_flags, dx, "grad_late_start", 4)
    early_src, early_land = _exchange_wait(g_sems[0], g_sems[1], g_sems[2], g_sems[3], early_flags, l_sems[4],
                                           "grad_early_wait")
    land = dict(zip(early_names, early_land))
    land_up = land["w_up"]
    own = {k: lax.dynamic_index_in_dim(s, me_idx, 0, keepdims=False) for k, s in zip(early_names, early_src)}
    grads, deltas, new_m, new_v = {}, {}, {}, {}

    def adam_shard(name):
        flip = (lambda a: jnp.swapaxes(a, 1, 2)) if name in ("w_in", "w_up") else (lambda a: a)
        outs = _adam_sharded(land[name], own[name], flip(weights[name]), flip(mom1[name]), flip(mom2[name]),
                             "adam_" + name, ADAM_TILE[name])
        grads[name], deltas[name], new_m[name], new_v[name] = [flip(o) for o in outs]

    for name in ("w_up", "w_down", "w_out", "w_glu"):
        adam_shard(name)
    (own_pack,), (early_parts,) = _exchange_wait(p_sems[0], p_sems[1], p_sems[2], p_sems[3], [False], land_up,
                                                 "small_early_wait")
    early_sum = _sum_partials(early_parts, own_pack, "sum_early_partials")
    (src_in, src_cw, own_late), (land["w_in"], land["conv_w"], late_parts) = _exchange_wait(
        l_sems[0], l_sems[1], l_sems[2], l_sems[3], late_flags,
        _after_all(early_sum, *[new_v[k] for k in ("w_up", "w_down", "w_out", "w_glu")]), "grad_late_wait")
    own["w_in"] = lax.dynamic_index_in_dim(src_in, me_idx, 0, keepdims=False)
    own["conv_w"] = lax.dynamic_index_in_dim(src_cw, me_idx, 0, keepdims=False)
    for name in ("w_in", "conv_w"):
        adam_shard(name)

    summed = dict(zip(late_keys, _unpack(_sum_partials(late_parts, own_late, "sum_late_partials"), late_shapes)))
    summed.update(zip(early_keys, _unpack(early_sum, early_shapes)))
    dlr, dli, dldt, dbr_t, dbi_t = _s5_param_bwd(
        lr3, li3, ldt3, br_t, bi_t, summed["ab_re"].reshape(lr3.shape), summed["ab_im"].reshape(lr3.shape),
        summed["bb_re"], summed["bb_im"])
    small_grads = {
        "norm_mix": summed["norm_mix"], "b_forget": summed["b_forget"], "lam_re": dlr, "lam_im": dli,
        "b_re": dbr_t, "b_im": dbi_t, "c_re": summed["c_re"], "c_im": summed["c_im"],
        "d_skip": summed["d_skip"], "log_dt": dldt, "b_glu": summed["b_glu"], "q_norm": summed["q_norm"],
        "k_norm": summed["k_norm"], "norm_out_ssm": summed["norm_out_ssm"], "norm_out_attn": summed["norm_out_attn"],
        "norm_ffn": summed["norm_ffn"], "conv_b": summed["conv_b"],
    }
    repl = tuple(k for k in WEIGHT_NAMES if k not in SHARDED)
    turn = lambda k, a: jnp.swapaxes(a, 2, 3) if k in ("b_re", "b_im") else a
    w_list = [turn(k, weights[k]) for k in repl]
    g_list = [small_grads[k].reshape(w.shape) for k, w in zip(repl, w_list)]
    d_list, m_list, v_list = _adam_replicated(g_list, w_list, [turn(k, mom1[k]) for k in repl],
                                              [turn(k, mom2[k]) for k in repl], "adam_replicated")
    for k, g, d, nm, nv in zip(repl, g_list, d_list, m_list, v_list):
        grads[k], deltas[k], new_m[k], new_v[k] = turn(k, g), turn(k, d), turn(k, nm), turn(k, nv)

    grad_x = dx.reshape(x.shape)
    loss = summed["loss"].reshape(())
    return (loss, grad_x, *[grads[k] for k in WEIGHT_NAMES], *[deltas[k] for k in WEIGHT_NAMES],
            *[new_m[k] for k in WEIGHT_NAMES], *[new_v[k] for k in WEIGHT_NAMES])
```

```python
import functools
import math

import jax
import jax.numpy as jnp
from jax import lax
from jax.experimental import pallas as pl
from jax.experimental.pallas import tpu as pltpu

F32 = jnp.float32
BF16 = jnp.bfloat16
HIGHEST = lax.Precision.HIGHEST

N_DEV = 8
D_MODEL = 1024
D_SSM = 512
D_ATTN = 512
N_HEADS = 8
HEAD_DIM = 64
N_GROUPS = 32
SSM_GROUP = 16
SSM_STATE = 64
D_FF = 2752
D_IN = 2056
D_IN_PAD = 2176
EPS = 1e-6
LANES = 128
SUBLANES = 8
VMEM_LIMIT = 56 * 1024 * 1024

ADAM_LR = 0.001
ADAM_B1 = 0.9
ADAM_B2 = 0.999
ADAM_EPS = 1e-08
ADAM_WD = 0.01
ADAM_STEP = 10


def _cparams(*sem):
    return pltpu.CompilerParams(dimension_semantics=sem, vmem_limit_bytes=VMEM_LIMIT)


def _dot(a, b, **kw):
    return jnp.dot(a, b, preferred_element_type=F32, **kw)


def _dot_nt(a, b):
    return lax.dot_general(a, b, (((1,), (1,)), ((), ())), preferred_element_type=F32)


def _dot_tn(a, b):
    return lax.dot_general(a, b, (((0,), (0,)), ((), ())), preferred_element_type=F32)


def _rms(x, g):
    return x * lax.rsqrt(jnp.mean(x * x, axis=-1, keepdims=True) + EPS) * g


def _split_dot(x, avg):
    hi = x.astype(BF16)
    lo = (x - hi.astype(F32)).astype(BF16)
    return _dot(hi, avg) + _dot(lo, avg)


@jax.custom_vjp
def _group_mean(x, avg):
    return _split_dot(x, avg)


def _group_mean_fwd(x, avg):
    return _split_dot(x, avg), avg


def _group_mean_bwd(avg, ct):
    return _split_dot(ct, avg), jnp.zeros_like(avg)


_group_mean.defvjp(_group_mean_fwd, _group_mean_bwd)


def _headnorm(q, avg, g):
    return q * lax.rsqrt(_group_mean(q * q, avg) + EPS) * g


ALL_PEERS = tuple(range(1, N_DEV))
NEAR_PEERS = (1, 2, 4, 6)
RELAYED = (2, 4, 6)


def _peer_list(js=ALL_PEERS):
    x, y, c = lax.axis_index("x"), lax.axis_index("y"), lax.axis_index("c")
    peers = []
    for j in js:
        px = 1 - x if (j >> 2) & 1 else x
        py = 1 - y if (j >> 1) & 1 else y
        pc = 1 - c if j & 1 else c
        peers.append(((px, py, pc), 4 * px + 2 * py + pc))
    return 4 * x + 2 * y + c, peers


def _split_copies(src, land, send_sems, recv_sems, scatter_flags, me, peers, incoming):
    copies = []
    for k in range(len(src)):
        for j, (pid, pidx) in enumerate(peers):
            s = src[k].at[pidx] if scatter_flags[k] else src[k]
            i = k * len(peers) + j
            copies.append(pltpu.make_async_remote_copy(
                src_ref=s, dst_ref=land[k].at[pidx if incoming else me], send_sem=send_sems[i],
                recv_sem=recv_sems[i], device_id=pid, device_id_type=pl.DeviceIdType.MESH))
    return copies


def _handshake(peers):
    barrier = pltpu.get_barrier_semaphore()
    for pid, _ in peers:
        pl.semaphore_signal(barrier, inc=1, device_id=pid, device_id_type=pl.DeviceIdType.MESH)
    pl.semaphore_wait(barrier, len(peers))


def _exchange_start(srcs, scatter_flags, after, name, collective_id, peer_js=ALL_PEERS):
    n = len(srcs)
    ns = n * len(peer_js)
    hbm = pl.BlockSpec(memory_space=pltpu.HBM)
    sem = pl.BlockSpec(memory_space=pltpu.SEMAPHORE)
    land_shapes = [s.shape if sc else (N_DEV,) + s.shape for s, sc in zip(srcs, scatter_flags)]

    def body(*refs):
        src, land = refs[:n], refs[n:2 * n]
        send_sems = refs[2 * n + 1:2 * n + 1 + ns]
        recv_sems = refs[2 * n + 1 + ns:2 * n + 1 + 2 * ns]
        token = refs[4 * n + 1 + 2 * ns]
        me, peers = _peer_list(peer_js)
        _handshake(peers)
        for cp in _split_copies(src, land, send_sems, recv_sems, scatter_flags, me, peers, False):
            cp.start()
        token[...] = jnp.zeros_like(token)

    outs = pl.pallas_call(
        body, name=name,
        out_shape=(*[pltpu.SemaphoreType.DMA(())] * (2 * ns), *[pltpu.HBM(s.shape, s.dtype) for s in srcs],
                   *[pltpu.HBM(shp, s.dtype) for shp, s in zip(land_shapes, srcs)],
                   jax.ShapeDtypeStruct((SUBLANES, LANES), F32)),
        in_specs=[hbm] * (2 * n) + [pl.BlockSpec(memory_space=pl.ANY)],
        out_specs=(*[sem] * (2 * ns), *[hbm] * (2 * n), pl.BlockSpec(memory_space=pltpu.VMEM)),
        input_output_aliases={i: 2 * ns + i for i in range(2 * n)},
        compiler_params=pltpu.CompilerParams(has_side_effects=pltpu.SideEffectType.DATAFLOW_SIDE_EFFECTING,
                                             collective_id=collective_id),
    )(*[pltpu.with_memory_space_constraint(s, pltpu.HBM) for s in srcs],
      *[pltpu.with_memory_space_constraint(lax.empty(shp, s.dtype), pltpu.HBM) for shp, s in zip(land_shapes, srcs)],
      after)
    return (outs[:ns], outs[ns:2 * ns], outs[2 * ns:2 * ns + n], outs[2 * ns + n:2 * ns + 2 * n], outs[2 * ns + 2 * n])


def _exchange_wait(send_sems, recv_sems, srcs, lands, scatter_flags, after, name, peer_js=ALL_PEERS):
    n = len(srcs)
    ns = n * len(peer_js)
    hbm = pl.BlockSpec(memory_space=pltpu.HBM)
    sem = pl.BlockSpec(memory_space=pltpu.SEMAPHORE)

    def body(*refs):
        src, land = refs[:n], refs[n:2 * n]
        s_sems = refs[2 * n:2 * n + ns]
        r_sems = refs[2 * n + ns:2 * n + 2 * ns]
        me, peers = _peer_list(peer_js)
        for cp in _split_copies(src, land, s_sems, r_sems, scatter_flags, me, peers, True):
            cp.wait_send()
            cp.wait_recv()

    outs = pl.pallas_call(
        body, name=name,
        out_shape=tuple(pltpu.HBM(a.shape, a.dtype) for a in (*srcs, *lands)),
        in_specs=[hbm] * (2 * n) + [sem] * (2 * ns) + [pl.BlockSpec(memory_space=pl.ANY)],
        out_specs=tuple([hbm] * (2 * n)),
        input_output_aliases={i: i for i in range(2 * n)},
        compiler_params=pltpu.CompilerParams(has_side_effects=pltpu.SideEffectType.DATAFLOW_SIDE_EFFECTING),
    )(*srcs, *lands, *send_sems, *recv_sems, after)
    return outs[:n], outs[n:]


def _relay_copies(land, send_sems, recv_sems, incoming):
    _, ((sibling, _),) = _peer_list((1,))
    _, heard = _peer_list(RELAYED)
    _, sibling_heard = _peer_list(tuple(j ^ 1 for j in RELAYED))
    copies = []
    for k in range(len(land)):
        for j in range(len(RELAYED)):
            slot = (sibling_heard if incoming else heard)[j][1]
            i = k * len(RELAYED) + j
            copies.append(pltpu.make_async_remote_copy(
                src_ref=land[k].at[slot], dst_ref=land[k].at[slot], send_sem=send_sems[i], recv_sem=recv_sems[i],
                device_id=sibling, device_id_type=pl.DeviceIdType.MESH))
    return copies


def _relay_start(lands, name, collective_id):
    n = len(lands)
    ns = n * len(RELAYED)
    hbm = pl.BlockSpec(memory_space=pltpu.HBM)
    sem = pl.BlockSpec(memory_space=pltpu.SEMAPHORE)

    def body(*refs):
        land = refs[:n]
        send_sems = refs[n:n + ns]
        recv_sems = refs[n + ns:n + 2 * ns]
        token = refs[2 * n + 2 * ns]
        _handshake(_peer_list((1,))[1])
        for cp in _relay_copies(land, send_sems, recv_sems, False):
            cp.start()
        token[...] = jnp.zeros_like(token)

    outs = pl.pallas_call(
        body, name=name,
        out_shape=(*[pltpu.SemaphoreType.DMA(())] * (2 * ns), *[pltpu.HBM(a.shape, a.dtype) for a in lands],
                   jax.ShapeDtypeStruct((SUBLANES, LANES), F32)),
        in_specs=[hbm] * n,
        out_specs=(*[sem] * (2 * ns), *[hbm] * n, pl.BlockSpec(memory_space=pltpu.VMEM)),
        input_output_aliases={i: 2 * ns + i for i in range(n)},
        compiler_params=pltpu.CompilerParams(has_side_effects=pltpu.SideEffectType.DATAFLOW_SIDE_EFFECTING,
                                             collective_id=collective_id),
    )(*lands)
    return outs[:ns], outs[ns:2 * ns], outs[2 * ns:2 * ns + n], outs[2 * ns + n]


def _relay_wait(send_sems, recv_sems, lands, after, name):
    n = len(lands)
    ns = n * len(RELAYED)
    hbm = pl.BlockSpec(memory_space=pltpu.HBM)
    sem = pl.BlockSpec(memory_space=pltpu.SEMAPHORE)

    def body(*refs):
        land = refs[:n]
        for cp in _relay_copies(land, refs[n:n + ns], refs[n + ns:n + 2 * ns], True):
            cp.wait_send()
            cp.wait_recv()

    return pl.pallas_call(
        body, name=name,
        out_shape=tuple(pltpu.HBM(a.shape, a.dtype) for a in lands),
        in_specs=[hbm] * n + [sem] * (2 * ns) + [pl.BlockSpec(memory_space=pl.ANY)],
        out_specs=tuple([hbm] * n),
        input_output_aliases={i: i for i in range(n)},
        compiler_params=pltpu.CompilerParams(has_side_effects=pltpu.SideEffectType.DATAFLOW_SIDE_EFFECTING),
    )(*lands, *send_sems, *recv_sems, after)


def _tn_matmul(a, b, name, tk, tm, out_rows=None, out_cols=None, out_dtype=F32, tn=512):
    n_tok, k_dim = a.shape
    m_dim = b.shape[1]
    grid = (k_dim // tk, m_dim // tm, n_tok // tn)

    def body(a_ref, b_ref, o_ref, acc):
        k = pl.program_id(2)
        part = _dot_tn(a_ref[...].astype(BF16), b_ref[...].astype(BF16))

        @pl.when(k == 0)
        def _():
            acc[...] = part

        @pl.when(k > 0)
        def _():
            acc[...] += part

        @pl.when(k == grid[2] - 1)
        def _():
            o_ref[...] = acc[...].astype(out_dtype)

    return pl.pallas_call(
        body, name=name, grid=grid,
        in_specs=[pl.BlockSpec((tn, tk), lambda i, j, k: (k, i)), pl.BlockSpec((tn, tm), lambda i, j, k: (k, j))],
        out_specs=pl.BlockSpec((tk, tm), lambda i, j, k: (i, j)),
        out_shape=jax.ShapeDtypeStruct((out_rows or k_dim, out_cols or m_dim), out_dtype),
        scratch_shapes=[pltpu.VMEM((tk, tm), F32)],
        compiler_params=_cparams("parallel", "parallel", "arbitrary"),
    )(a, b)


def _adam_math(g, w, m, v):
    m = ADAM_B1 * m + (1.0 - ADAM_B1) * g
    v = ADAM_B2 * v + (1.0 - ADAM_B2) * (g * g)
    m_hat = m / (1.0 - ADAM_B1 ** ADAM_STEP)
    v_hat = v / (1.0 - ADAM_B2 ** ADAM_STEP)
    delta = -ADAM_LR * (m_hat / (jnp.sqrt(v_hat) + ADAM_EPS) + ADAM_WD * w)
    return delta, m, v


def _adam_sharded(land, own, w, m, v, name, tile):
    _, r, c = w.shape

    def body(*refs):
        l_ref = refs[0]
        own_ref = refs[1] if own is not None else None
        w_ref, m_ref, v_ref, g_ref, d_ref, nm_ref, nv_ref = [ref.at[0] for ref in refs[-7:]]
        if own_ref is not None:
            x, y, z = lax.axis_index("x"), lax.axis_index("y"), lax.axis_index("c")
            me = 4 * x + 2 * y + z
            mine = own_ref[...].astype(F32)
        g = None
        for s in range(N_DEV):
            part = l_ref[s].astype(F32)
            if own_ref is not None:
                part = jnp.where(me == s, mine, part)
            g = part if g is None else g + part
        d, nm, nv = _adam_math(g, w_ref[...], m_ref[...], v_ref[...])
        g_ref[...] = g
        d_ref[...] = d
        nm_ref[...] = nm
        nv_ref[...] = nv

    tr, tc = tile
    spec = pl.BlockSpec((1, tr, tc), lambda i, j: (0, i, j))
    own_specs, own_args = ([pl.BlockSpec((tr, tc), lambda i, j: (i, j))], [own]) if own is not None else ([], [])
    return pl.pallas_call(
        body, name=name, grid=(r // tr, c // tc),
        in_specs=[pl.BlockSpec((N_DEV, tr, tc), lambda i, j: (0, i, j)), *own_specs, spec, spec, spec],
        out_specs=(spec, spec, spec, spec),
        out_shape=tuple(jax.ShapeDtypeStruct((1, r, c), F32) for _ in range(4)),
        compiler_params=_cparams("parallel", "parallel"),
    )(land, *own_args, w, m, v)


def _sum_partials(parts, own, name):
    _, r, c = parts.shape

    def body(*refs):
        p_ref, o_ref = refs[0], refs[-1]
        if own is not None:
            x, y, z = lax.axis_index("x"), lax.axis_index("y"), lax.axis_index("c")
            me = 4 * x + 2 * y + z
            mine = refs[1][...]
        g = None
        for s in range(N_DEV):
            part = p_ref[s]
            if own is not None:
                part = jnp.where(me == s, mine, part)
            g = part if g is None else g + part
        o_ref[...] = g

    args = (parts,) if own is None else (parts, own)
    return pl.pallas_call(body, name=name, out_shape=jax.ShapeDtypeStruct((r, c), F32),
                          compiler_params=pltpu.CompilerParams(vmem_limit_bytes=VMEM_LIMIT))(*args)


def _adam_replicated(gs, ws, ms, vs, name):
    k = len(ws)

    def body(*refs):
        outs = refs[4 * k:]
        for i in range(k):
            d, nm, nv = _adam_math(refs[i][...], refs[k + i][...], refs[2 * k + i][...], refs[3 * k + i][...])
            outs[i][...] = d
            outs[k + i][...] = nm
            outs[2 * k + i][...] = nv

    outs = pl.pallas_call(body, name=name, out_shape=tuple(jax.ShapeDtypeStruct(w.shape, F32) for w in ws) * 3,
                          compiler_params=pltpu.CompilerParams(vmem_limit_bytes=VMEM_LIMIT))(*gs, *ws, *ms, *vs)
    return outs[:k], outs[k:2 * k], outs[2 * k:]


def _inproj_fwd(x, g, w_in, avg, qg, kg, tm=512):
    n = x.shape[0]

    def body(x_ref, g_ref, w_ref, a_ref, qg_ref, kg_ref, hn_ref, u_ref, qkv_ref, raw_ref, fl_ref):
        hn = _rms(x_ref[...], g_ref[...]).astype(BF16)
        hn_ref[...] = hn
        proj = _dot_nt(hn, w_ref[...])
        u_ref[...] = proj[:, 0:512]
        q = proj[:, 512:1024]
        k = proj[:, 1024:1536]
        raw_ref[:, 0:512] = q
        raw_ref[:, 512:1024] = k
        qkv_ref[:, 0:512] = _headnorm(q, a_ref[...], qg_ref[...]).astype(BF16)
        qkv_ref[:, 512:1024] = _headnorm(k, a_ref[...], kg_ref[...]).astype(BF16)
        qkv_ref[:, 1024:1536] = proj[:, 1536:2048].astype(BF16)
        fl_ref[...] = proj[:, 2048:D_IN_PAD]

    row = lambda w: pl.BlockSpec((tm, w), lambda i: (i, 0))
    full = lambda a: pl.BlockSpec(a.shape, lambda i: (0,) * a.ndim)
    return pl.pallas_call(
        body, name="inproj_fwd", grid=(n // tm,),
        in_specs=[row(D_MODEL), full(g), full(w_in), full(avg), full(qg), full(kg)],
        out_specs=(row(D_MODEL), row(512), row(1536), row(1024), row(LANES)),
        out_shape=(jax.ShapeDtypeStruct((n, D_MODEL), BF16), jax.ShapeDtypeStruct((n, 512), F32),
                   jax.ShapeDtypeStruct((n, 1536), BF16), jax.ShapeDtypeStruct((n, 1024), F32),
                   jax.ShapeDtypeStruct((n, LANES), F32)),
        compiler_params=_cparams("parallel"),
    )(x, g, w_in, avg, qg, kg)


def _inproj_bwd(x, g, w_in, avg, qg, kg, raw, du, dqn, dkn, dv, dfl, dres, tm=512):
    n = x.shape[0]

    def body(x_ref, g_ref, w_ref, a_ref, qg_ref, kg_ref, raw_ref, du_ref, dqn_ref, dkn_ref, dv_ref, dfl_ref, dres_ref,
             dx_ref, dproj_ref, dg_ref, dqg_ref, dkg_ref):
        @pl.when(pl.program_id(0) == 0)
        def _():
            dg_ref[...] = jnp.zeros_like(dg_ref)
            dqg_ref[...] = jnp.zeros_like(dqg_ref)
            dkg_ref[...] = jnp.zeros_like(dkg_ref)
        avg_m = a_ref[...]
        _, vjp_q = jax.vjp(lambda q, gg: _headnorm(q, avg_m, gg), raw_ref[:, 0:512], qg_ref[...])
        dq, dqg = vjp_q(dqn_ref[...])
        _, vjp_k = jax.vjp(lambda k, gg: _headnorm(k, avg_m, gg), raw_ref[:, 512:1024], kg_ref[...])
        dk, dkg = vjp_k(dkn_ref[...])
        dproj = jnp.concatenate([du_ref[...], dq, dk, dv_ref[...], dfl_ref[...]], axis=1).astype(BF16)
        dproj_ref[...] = dproj
        dhn = _dot(dproj, w_ref[...])
        _, vjp_x = jax.vjp(_rms, x_ref[...], g_ref[...])
        dxn, dg = vjp_x(dhn)
        dx_ref[...] = dxn + dres_ref[...]
        dg_ref[...] += dg
        dqg_ref[...] += dqg
        dkg_ref[...] += dkg

    row = lambda w: pl.BlockSpec((tm, w), lambda i: (i, 0))
    full = lambda a: pl.BlockSpec(a.shape, lambda i: (0,) * a.ndim)
    vec = lambda w: pl.BlockSpec((1, w), lambda i: (0, 0))
    return pl.pallas_call(
        body, name="inproj_bwd", grid=(n // tm,),
        in_specs=[row(D_MODEL), full(g), full(w_in), full(avg), full(qg), full(kg), row(1024), row(512), row(512),
                  row(512), row(512), row(LANES), row(D_MODEL)],
        out_specs=(row(D_MODEL), row(D_IN_PAD), vec(D_MODEL), vec(512), vec(512)),
        out_shape=(jax.ShapeDtypeStruct((n, D_MODEL), F32), jax.ShapeDtypeStruct((n, D_IN_PAD), BF16),
                   jax.ShapeDtypeStruct((1, D_MODEL), F32), jax.ShapeDtypeStruct((1, 512), F32),
                   jax.ShapeDtypeStruct((1, 512), F32)),
        compiler_params=_cparams("arbitrary"),
    )(x, g, w_in, avg, qg, kg, raw, du, dqn, dkn, dv, dfl, dres)


def _mix_fwd(x, yc, ya, wg, bg, gs, ga, wout, gf, tm=512):
    n = x.shape[0]

    def body(x_ref, yc_ref, ya_ref, wg_ref, bg_ref, gs_ref, ga_ref, w_ref, gf_ref, h1_ref, hn2_ref, mixed_ref):
        gl = jax.nn.gelu(yc_ref[...])
        ys = gl * jax.nn.sigmoid(_dot(gl.astype(BF16), wg_ref[...]) + bg_ref[...])
        mixed = jnp.concatenate([_rms(ys, gs_ref[...]), _rms(ya_ref[...], ga_ref[...])], axis=1).astype(BF16)
        mixed_ref[...] = mixed
        h1 = x_ref[...] + _dot(mixed, w_ref[...])
        h1_ref[...] = h1
        hn2_ref[...] = _rms(h1, gf_ref[...]).astype(BF16)

    row = lambda w: pl.BlockSpec((tm, w), lambda i: (i, 0))
    full = lambda a: pl.BlockSpec(a.shape, lambda i: (0,) * a.ndim)
    return pl.pallas_call(
        body, name="mix_fwd", grid=(n // tm,),
        in_specs=[row(D_MODEL), row(512), row(512), full(wg), full(bg), full(gs), full(ga), full(wout), full(gf)],
        out_specs=(row(D_MODEL), row(D_MODEL), row(D_MODEL)),
        out_shape=(jax.ShapeDtypeStruct((n, D_MODEL), F32), jax.ShapeDtypeStruct((n, D_MODEL), BF16),
                   jax.ShapeDtypeStruct((n, D_MODEL), BF16)),
        compiler_params=_cparams("parallel"),
    )(x, yc, ya, wg, bg, gs, ga, wout, gf)


def _mix_bwd(dy, dhn2_parts, h1, yc, ya, wg, bg, gs, ga, wout, gf, tm=512):
    n = dy.shape[0]
    n_parts = dhn2_parts.shape[0]

    def body(dy_ref, dp_ref, h1_ref, yc_ref, ya_ref, wg_ref, bg_ref, gs_ref, ga_ref, w_ref, gf_ref,
             dh1_ref, dyc_ref, dya_ref, gwo_ref, gwg_ref, dgs_ref, dga_ref, dgf_ref, dbg_ref):
        @pl.when(pl.program_id(0) == 0)
        def _():
            for ref in (gwo_ref, gwg_ref, dgs_ref, dga_ref, dgf_ref, dbg_ref):
                ref[...] = jnp.zeros_like(ref)
        dhn2 = dp_ref[0]
        for p in range(1, n_parts):
            dhn2 = dhn2 + dp_ref[p]
        _, vjp_f = jax.vjp(_rms, h1_ref[...], gf_ref[...])
        dh1n, dgf = vjp_f(dhn2)
        dh1 = dy_ref[...] + dh1n
        dh1_ref[...] = dh1
        dh1_b = dh1.astype(BF16)
        dmixed = _dot_nt(dh1_b, w_ref[...])
        gl, vjp_gelu = jax.vjp(jax.nn.gelu, yc_ref[...])
        glb = gl.astype(BF16)
        s = jax.nn.sigmoid(_dot(glb, wg_ref[...]) + bg_ref[...])
        rs, vjp_s = jax.vjp(_rms, gl * s, gs_ref[...])
        dys, dgs = vjp_s(dmixed[:, 0:512])
        ra, vjp_a = jax.vjp(_rms, ya_ref[...], ga_ref[...])
        dya, dga = vjp_a(dmixed[:, 512:1024])
        dz = dys * gl * s * (1.0 - s)
        dzb = dz.astype(BF16)
        dyc_ref[...] = vjp_gelu(dys * s + _dot_nt(dzb, wg_ref[...]))[0]
        dya_ref[...] = dya
        gwo_ref[...] += _dot_tn(jnp.concatenate([rs, ra], axis=1).astype(BF16), dh1_b)
        gwg_ref[...] += _dot_tn(glb, dzb)
        dgs_ref[...] += dgs
        dga_ref[...] += dga
        dgf_ref[...] += dgf
        dbg_ref[...] += jnp.sum(dz, axis=0, keepdims=True)

    row = lambda w: pl.BlockSpec((tm, w), lambda i: (i, 0))
    full = lambda a: pl.BlockSpec(a.shape, lambda i: (0,) * a.ndim)
    vec = lambda w: pl.BlockSpec((1, w), lambda i: (0, 0))
    return pl.pallas_call(
        body, name="mix_bwd", grid=(n // tm,),
        in_specs=[row(D_MODEL), pl.BlockSpec((n_parts, tm, D_MODEL), lambda i: (0, i, 0)), row(D_MODEL), row(512),
                  row(512), full(wg), full(bg), full(gs), full(ga), full(wout), full(gf)],
        out_specs=(row(D_MODEL), row(512), row(512), pl.BlockSpec((D_MODEL, D_MODEL), lambda i: (0, 0)),
                   pl.BlockSpec((D_SSM, D_SSM), lambda i: (0, 0)), vec(512), vec(512), vec(D_MODEL), vec(512)),
        out_shape=(jax.ShapeDtypeStruct((n, D_MODEL), F32), jax.ShapeDtypeStruct((n, 512), F32),
                   jax.ShapeDtypeStruct((n, 512), F32), jax.ShapeDtypeStruct((D_MODEL, D_MODEL), F32),
                   jax.ShapeDtypeStruct((D_SSM, D_SSM), F32), jax.ShapeDtypeStruct((1, 512), F32),
                   jax.ShapeDtypeStruct((1, 512), F32), jax.ShapeDtypeStruct((1, D_MODEL), F32),
                   jax.ShapeDtypeStruct((1, 512), F32)),
        compiler_params=_cparams("arbitrary"),
    )(dy, dhn2_parts, h1, yc, ya, wg, bg, gs, ga, wout, gf)


HALO = 16
FFN_GROUPS = 4
FFN_GROUP = D_FF // FFN_GROUPS


def _conv3(ue, cw):
    return cw[2:3] * ue + cw[1:2] * pltpu.roll(ue, 1, 0) + cw[0:1] * pltpu.roll(ue, 2, 0) + cw[3:4]


def _ffn_weight_specs():
    gate = lambda i, j: (j, 0, 0)
    val = lambda i, j: (j + FFN_GROUPS, 0, 0)
    w_blk, c_blk = (1, FFN_GROUP, D_MODEL), (1, SUBLANES, FFN_GROUP)
    return [pl.BlockSpec(w_blk, gate), pl.BlockSpec(w_blk, val), pl.BlockSpec(c_blk, gate), pl.BlockSpec(c_blk, val),
            pl.BlockSpec((1, FFN_GROUP, D_MODEL), gate)]


def _ffn_fwd(hn2, h1, target, w_up, conv, w_down, seq_len, tm=512):
    n = hn2.shape[0]
    nj = FFN_GROUPS
    hb = tm // HALO

    def body(hn_ref, halo_ref, h1_ref, tgt_ref, wg_ref, wv_ref, cg_ref, cv_ref, wd_ref,
             ug_ref, uv_ref, pg_ref, pv_ref, dy_ref, loss_ref, acc):
        i, j = pl.program_id(0), pl.program_id(1)
        seq_start = (i * tm) % seq_len == 0
        halo = halo_ref[...]
        halo = jnp.where(seq_start, jnp.zeros_like(halo), halo)
        he = jnp.concatenate([halo, hn_ref[...]], axis=0)
        ueg = _dot_nt(he, wg_ref[0])
        uev = _dot_nt(he, wv_ref[0])
        ug_ref[0] = ueg[HALO:].astype(BF16)
        uv_ref[0] = uev[HALO:].astype(BF16)
        cg = _conv3(ueg, cg_ref[0])[HALO:]
        cv = _conv3(uev, cv_ref[0])[HALO:]
        pg_ref[0] = cg.astype(BF16)
        pv_ref[0] = cv.astype(BF16)
        act = (jax.nn.silu(cg) * cv).astype(BF16)
        part = _dot(act, wd_ref[0])

        @pl.when(j == 0)
        def _():
            acc[...] = part

        @pl.when(j > 0)
        def _():
            acc[...] += part

        @pl.when(j == nj - 1)
        def _():
            err = h1_ref[...] + acc[...] - tgt_ref[...]
            dy_ref[...] = err * (1.0 / D_MODEL)
            loss_ref[0] = jnp.sum(err * err, axis=0, keepdims=True)

    row = pl.BlockSpec((tm, D_MODEL), lambda i, j: (i, 0))
    u_main = pl.BlockSpec((1, tm, FFN_GROUP), lambda i, j: (j, i, 0))
    u_shape = jax.ShapeDtypeStruct((FFN_GROUPS, n, FFN_GROUP), BF16)
    return pl.pallas_call(
        body, name="ffn_fwd", grid=(n // tm, nj),
        in_specs=[row, pl.BlockSpec((HALO, D_MODEL), lambda i, j: (jnp.maximum(i * hb - 1, 0), 0)), row, row,
                  *_ffn_weight_specs()],
        out_specs=(u_main, u_main, u_main, u_main, row, pl.BlockSpec((1, 1, D_MODEL), lambda i, j: (i, 0, 0))),
        out_shape=(u_shape, u_shape, u_shape, u_shape, jax.ShapeDtypeStruct((n, D_MODEL), F32),
                   jax.ShapeDtypeStruct((n // tm, 1, D_MODEL), F32)),
        scratch_shapes=[pltpu.VMEM((tm, D_MODEL), F32)],
        compiler_params=_cparams("parallel", "arbitrary"),
    )(hn2, hn2, h1, target, w_up, w_up, conv, conv, w_down)


def _ffn_bwd(dy, ug, uv, pg, pv, w_up, conv, w_down, seq_len, tm=512):
    n = dy.shape[0]
    nj = FFN_GROUPS
    fb = FFN_GROUP
    hb = tm // HALO
    last_hb = n // HALO - 1
    rows = tm + HALO

    def body(dy_ref, dyn_ref, ug_ref, uv_ref, pgm_ref, pgn_ref, pvm_ref, pvn_ref, wg_ref, wv_ref, cg_ref, cv_ref,
             wd_ref, dug_ref, duv_ref, act_ref, dhn_ref, dcg_ref, dcv_ref, acc):
        i, j = pl.program_id(0), pl.program_id(1)
        seq_end = ((i + 1) * tm) % seq_len == 0
        dyn = dyn_ref[...]
        dyn = jnp.where(seq_end, jnp.zeros_like(dyn), dyn)
        d_out = jnp.concatenate([dy_ref[...], dyn], axis=0).astype(BF16)
        d_act = _dot_nt(d_out, wd_ref[0])
        cge = jnp.concatenate([pgm_ref[0], pgn_ref[0]], axis=0).astype(F32)
        cve = jnp.concatenate([pvm_ref[0], pvn_ref[0]], axis=0).astype(F32)
        act, vjp_act = jax.vjp(lambda g, v: jax.nn.silu(g) * v, cge, cve)
        dcge, dcve = vjp_act(d_act)
        act_ref[0] = act[:tm].astype(BF16)

        def conv_t(dc, u_ref, cw):
            ahead1 = pltpu.roll(dc, rows - 1, 0)[:tm]
            ahead2 = pltpu.roll(dc, rows - 2, 0)[:tm]
            here = dc[:tm]
            du = cw[2:3] * here + cw[1:2] * ahead1 + cw[0:1] * ahead2
            u = u_ref[0].astype(F32)
            col = lambda x: jnp.sum(x, axis=0, keepdims=True)
            grad = jnp.concatenate([col(ahead2 * u), col(ahead1 * u), col(here * u), col(here),
                                    jnp.zeros((4, fb), F32)], axis=0)
            return du.astype(BF16), grad

        cwg, cwv = cg_ref[0], cv_ref[0]
        dug, grad_g = conv_t(dcge, ug_ref, cwg)
        duv, grad_v = conv_t(dcve, uv_ref, cwv)
        dug_ref[0] = dug
        duv_ref[0] = duv
        part = _dot(dug, wg_ref[0]) + _dot(duv, wv_ref[0])

        @pl.when(j == 0)
        def _():
            acc[...] = part

        @pl.when(j > 0)
        def _():
            acc[...] += part

        @pl.when(j == nj - 1)
        def _():
            dhn_ref[...] = acc[...]

        @pl.when(i == 0)
        def _():
            dcg_ref[j] = jnp.zeros((8, fb), F32)
            dcv_ref[j] = jnp.zeros((8, fb), F32)

        dcg_ref[j] += grad_g
        dcv_ref[j] += grad_v

    row = pl.BlockSpec((tm, D_MODEL), lambda i, j: (i, 0))
    u_main = pl.BlockSpec((1, tm, fb), lambda i, j: (j, i, 0))
    u_next = pl.BlockSpec((1, HALO, fb), lambda i, j: (j, jnp.minimum((i + 1) * hb, last_hb), 0))
    dc_spec = pl.BlockSpec((nj, 8, fb), lambda i, j: (0, 0, 0))
    u_shape = jax.ShapeDtypeStruct((FFN_GROUPS, n, fb), BF16)
    return pl.pallas_call(
        body, name="ffn_bwd", grid=(n // tm, nj),
        in_specs=[row, pl.BlockSpec((HALO, D_MODEL), lambda i, j: (jnp.minimum((i + 1) * hb, last_hb), 0)),
                  u_main, u_main, u_main, u_next, u_main, u_next, *_ffn_weight_specs()],
        out_specs=(u_main, u_main, u_main, row, dc_spec, dc_spec),
        out_shape=(u_shape, u_shape, u_shape, jax.ShapeDtypeStruct((n, D_MODEL), F32),
                   jax.ShapeDtypeStruct((nj, 8, fb), F32), jax.ShapeDtypeStruct((nj, 8, fb), F32)),
        scratch_shapes=[pltpu.VMEM((tm, D_MODEL), F32)],
        compiler_params=_cparams("arbitrary", "arbitrary"),
    )(dy, dy, ug, uv, pg, pg, pv, pv, w_up, w_up, conv, conv, w_down)


def _tn_grouped(a, b, name, shared_a, out_dtype=F32, tn=1024):
    groups = b.shape[0] if shared_a else a.shape[0]
    n_tok = a.shape[0] if shared_a else b.shape[0]
    k_dim, m_dim = a.shape[-1], b.shape[-1]

    def body(a_ref, b_ref, o_ref, acc):
        k = pl.program_id(1)
        a_t = a_ref[...] if shared_a else a_ref[0]
        b_t = b_ref[0] if shared_a else b_ref[...]
        part = _dot_tn(a_t.astype(BF16), b_t.astype(BF16))

        @pl.when(k == 0)
        def _():
            acc[...] = part

        @pl.when(k > 0)
        def _():
            acc[...] += part

        @pl.when(k == n_tok // tn - 1)
        def _():
            o_ref[0] = acc[...].astype(out_dtype)

    plain = lambda w: pl.BlockSpec((tn, w), lambda g, k: (k, 0))
    grouped = lambda w: pl.BlockSpec((1, tn, w), lambda g, k: (g, k, 0))
    return pl.pallas_call(
        body, name=name, grid=(groups, n_tok // tn),
        in_specs=[plain(k_dim), grouped(m_dim)] if shared_a else [grouped(k_dim), plain(m_dim)],
        out_specs=pl.BlockSpec((1, k_dim, m_dim), lambda g, k: (g, 0, 0)),
        out_shape=jax.ShapeDtypeStruct((groups, k_dim, m_dim), out_dtype),
        scratch_shapes=[pltpu.VMEM((k_dim, m_dim), F32)],
        compiler_params=_cparams("parallel", "arbitrary"),
    )(a, b)


def _s5_param_fn(lr, li, ldt, br, bi):
    dt = jnp.exp(ldt)
    mag = jnp.exp(lr * dt)
    ab_re = mag * jnp.cos(li * dt)
    ab_im = mag * jnp.sin(li * dt)
    nr = ab_re - 1.0
    ni = ab_im
    den = lr * lr + li * li
    q_re = (nr * lr + ni * li) / den
    q_im = (ni * lr - nr * li) / den
    bb_re = q_re * br - q_im * bi
    bb_im = q_re * bi + q_im * br
    return ab_re, ab_im, bb_re, bb_im


def _s5_param_fwd(lr, li, ldt, br, bi):
    def body(lr_ref, li_ref, ldt_ref, br_ref, bi_ref, ar_ref, ai_ref, bbr_ref, bbi_ref):
        ar, ai, bbr, bbi = _s5_param_fn(lr_ref[...], li_ref[...], ldt_ref[...], br_ref[...], bi_ref[...])
        ar_ref[...] = ar
        ai_ref[...] = ai
        bbr_ref[...] = bbr
        bbi_ref[...] = bbi

    return pl.pallas_call(
        body, name="s5_param_fwd",
        out_shape=(jax.ShapeDtypeStruct(lr.shape, F32), jax.ShapeDtypeStruct(lr.shape, F32),
                   jax.ShapeDtypeStruct(br.shape, F32), jax.ShapeDtypeStruct(br.shape, F32)),
    )(lr, li, ldt, br, bi)


def _s5_param_bwd(lr, li, ldt, br, bi, dar, dai, dbbr, dbbi):
    def body(lr_ref, li_ref, ldt_ref, br_ref, bi_ref, dar_ref, dai_ref, dbbr_ref, dbbi_ref,
             dlr_ref, dli_ref, dldt_ref, dbr_ref, dbi_ref):
        _, vjp = jax.vjp(_s5_param_fn, lr_ref[...], li_ref[...], ldt_ref[...], br_ref[...], bi_ref[...])
        dlr, dli, dldt, dbr, dbi = vjp((dar_ref[...], dai_ref[...], dbbr_ref[...], dbbi_ref[...]))
        dlr_ref[...] = dlr
        dli_ref[...] = dli
        dldt_ref[...] = dldt
        dbr_ref[...] = dbr
        dbi_ref[...] = dbi

    return pl.pallas_call(
        body, name="s5_param_bwd",
        out_shape=(jax.ShapeDtypeStruct(lr.shape, F32), jax.ShapeDtypeStruct(lr.shape, F32),
                   jax.ShapeDtypeStruct(ldt.shape, F32), jax.ShapeDtypeStruct(br.shape, F32),
                   jax.ShapeDtypeStruct(br.shape, F32)),
    )(lr, li, ldt, br, bi, dar, dai, dbbr, dbbi)


S5_CHUNK = 1024
S5_STATES = 512
S5_BLOCKS = 4


def _cpow_rows(ar, ai, count):
    rs, im = [ar], [ai]
    for _ in range(count - 1):
        pr, pi = rs[-1], im[-1]
        rs.append(pr * ar - pi * ai)
        im.append(pr * ai + pi * ar)
    return rs, im


def _scan_in_groups(vr, vi, pr, pi, rm, reverse):
    n, width = vr.shape
    vr = vr.reshape(n // SUBLANES, SUBLANES, width)
    vi = vi.reshape(n // SUBLANES, SUBLANES, width)
    row = rm[0:SUBLANES]
    for k in (1, 2, 4):
        shift = SUBLANES - k if reverse else k
        keep = row < SUBLANES - k if reverse else row >= k
        kr = jnp.where(keep, pr[k - 1], 0.0)
        ki = jnp.where(keep, pi[k - 1], 0.0)
        sr, si = pltpu.roll(vr, shift, 1), pltpu.roll(vi, shift, 1)
        vr, vi = vr + kr * sr - ki * si, vi + kr * si + ki * sr
    return vr.reshape(n, width), vi.reshape(n, width)


def _carry_over_groups(xr_s, xi_s, wr, wi, c0r, c0i, reverse):
    groups = xr_s.shape[0] // SUBLANES
    pick = 0 if reverse else SUBLANES - 1

    def step(q, carry):
        cr, ci = carry
        r = groups - 1 - q if reverse else q
        o = pl.multiple_of(r * SUBLANES, SUBLANES)
        vr = xr_s[pl.ds(o, SUBLANES), :]
        vi = xi_s[pl.ds(o, SUBLANES), :]
        nr = vr + wr * cr - wi * ci
        ni = vi + wr * ci + wi * cr
        xr_s[pl.ds(o, SUBLANES), :] = nr
        xi_s[pl.ds(o, SUBLANES), :] = ni
        return (jnp.broadcast_to(nr[pick:pick + 1], nr.shape), jnp.broadcast_to(ni[pick:pick + 1], ni.shape))

    return lax.fori_loop(0, groups, step, (c0r, c0i), unroll=4)


def _s5_state_scan(u_b, bbr, bbi, pr, pi, rm, xr_s, xi_s, c0r, c0i):
    bur = _dot(u_b, bbr)
    bui = _dot(u_b, bbi)
    bur, bui = _scan_in_groups(bur, bui, pr, pi, rm, False)
    xr_s[...] = bur
    xi_s[...] = bui
    w8r = jnp.concatenate(pr, axis=0)
    w8i = jnp.concatenate(pi, axis=0)
    return _carry_over_groups(xr_s, xi_s, w8r, w8i, c0r, c0i, False)


def _s5_fwd(u, a_re, a_im, bbr, bbi, cr, ci, d_skip, n_seq):
    n = u.shape[0]
    seq_len = n // n_seq
    nt = seq_len // S5_CHUNK
    tc = S5_CHUNK

    def body(u_ref, ar_ref, ai_ref, bbr_ref, bbi_ref, cr_ref, ci_ref, d_ref, y_ref, str_ref, sti_ref, xrb_ref, xib_ref,
             xr_s, xi_s, car_r, car_i):
        t = pl.program_id(2)

        @pl.when(t == 0)
        def _():
            car_r[...] = jnp.zeros_like(car_r)
            car_i[...] = jnp.zeros_like(car_i)
        pr, pi = _cpow_rows(ar_ref[0], ai_ref[0], SUBLANES)
        rm = lax.broadcasted_iota(jnp.int32, (tc, S5_STATES), 0) & (SUBLANES - 1)
        str_ref[0, 0] = car_r[...]
        sti_ref[0, 0] = car_i[...]
        u_t = u_ref[...]
        cfr, cfi = _s5_state_scan(u_t.astype(BF16), bbr_ref[0], bbi_ref[0], pr, pi, rm, xr_s, xi_s,
                                  car_r[...], car_i[...])
        car_r[...] = cfr
        car_i[...] = cfi
        xr_b = xr_s[...].astype(BF16)
        xi_b = xi_s[...].astype(BF16)
        xrb_ref[...] = xr_b
        xib_ref[...] = xi_b
        y_ref[...] = _dot(xr_b, cr_ref[0]) - _dot(xi_b, ci_ref[0]) + d_ref[...] * u_t

    x_spec = pl.BlockSpec((tc, S5_STATES), lambda cb, b, t: (b * nt + t, cb))
    x_shape = jax.ShapeDtypeStruct((n, S5_BLOCKS * S5_STATES), BF16)
    u_spec = pl.BlockSpec((tc, LANES), lambda cb, b, t: (b * nt + t, cb))
    a_spec = pl.BlockSpec((1, 1, S5_STATES), lambda cb, b, t: (cb, 0, 0))
    bb_spec = pl.BlockSpec((1, LANES, S5_STATES), lambda cb, b, t: (cb, 0, 0))
    c_spec = pl.BlockSpec((1, S5_STATES, LANES), lambda cb, b, t: (cb, 0, 0))
    st_spec = pl.BlockSpec((1, 1, SUBLANES, S5_STATES), lambda cb, b, t: (cb, b * nt + t, 0, 0))
    st_shape = jax.ShapeDtypeStruct((S5_BLOCKS, n_seq * nt, SUBLANES, S5_STATES), F32)
    return pl.pallas_call(
        body, name="s5_fwd", grid=(S5_BLOCKS, n_seq, nt),
        in_specs=[u_spec, a_spec, a_spec, bb_spec, bb_spec, c_spec, c_spec,
                  pl.BlockSpec((1, LANES), lambda cb, b, t: (0, cb))],
        out_specs=(u_spec, st_spec, st_spec, x_spec, x_spec),
        out_shape=(jax.ShapeDtypeStruct((n, D_SSM), F32), st_shape, st_shape, x_shape, x_shape),
        scratch_shapes=[pltpu.VMEM((tc, S5_STATES), F32), pltpu.VMEM((tc, S5_STATES), F32),
                        pltpu.VMEM((SUBLANES, S5_STATES), F32), pltpu.VMEM((SUBLANES, S5_STATES), F32)],
        compiler_params=_cparams("parallel", "arbitrary", "arbitrary"),
    )(u, a_re, a_im, bbr, bbi, cr, ci, d_skip)


def _s5_bwd(u, dy, xs_r, xs_i, st_r, st_i, a_re, a_im, bbr, bbi, cr, ci, d_skip, n_seq):
    n = u.shape[0]
    seq_len = n // n_seq
    nt = seq_len // S5_CHUNK
    tc = S5_CHUNK

    def body(u_ref, dy_ref, xrb_ref, xib_ref, str_ref, sti_ref, ar_ref, ai_ref, bbr_ref, bbi_ref, cr_ref, ci_ref, d_ref,
             du_ref, dbbr_ref, dbbi_ref, dcr_ref, dci_ref, dar_ref, dai_ref, dd_ref,
             gr_s, gi_s, car_r, car_i):
        b, t = pl.program_id(1), pl.program_id(2)

        @pl.when((b == 0) & (t == 0))
        def _():
            for ref in (dbbr_ref, dbbi_ref, dcr_ref, dci_ref, dar_ref, dai_ref, dd_ref):
                ref[...] = jnp.zeros_like(ref)

        @pl.when(t == 0)
        def _():
            car_r[...] = jnp.zeros_like(car_r)
            car_i[...] = jnp.zeros_like(car_i)
        ar, ai = ar_ref[0], ai_ref[0]
        pr, pi = _cpow_rows(ar, ai, SUBLANES)
        row = lax.broadcasted_iota(jnp.int32, (tc, S5_STATES), 0)
        rm = row & (SUBLANES - 1)
        u_t = u_ref[...]
        u_b = u_t.astype(BF16)
        dy_t = dy_ref[...]
        dy_b = dy_t.astype(BF16)
        s0r, s0i = str_ref[0, 0], sti_ref[0, 0]
        xr_b, xi_b = xrb_ref[...], xib_ref[...]
        xr, xi = xr_b.astype(F32), xi_b.astype(F32)
        gr = _dot_nt(dy_b, cr_ref[0])
        gi = -_dot_nt(dy_b, ci_ref[0])
        npi = [-v for v in pi]
        gr, gi = _scan_in_groups(gr, gi, pr, npi, rm, True)
        gr_s[...] = gr
        gi_s[...] = gi
        w8r = jnp.concatenate(pr[::-1], axis=0)
        w8i = jnp.concatenate(npi[::-1], axis=0)
        cfr, cfi = _carry_over_groups(gr_s, gi_s, w8r, w8i, car_r[...], car_i[...], True)
        car_r[...] = cfr
        car_i[...] = cfi
        gr, gi = gr_s[...], gi_s[...]
        gr_b, gi_b = gr.astype(BF16), gi.astype(BF16)
        du_ref[...] = _dot_nt(gr_b, bbr_ref[0]) + _dot_nt(gi_b, bbi_ref[0]) + d_ref[...] * dy_t
        dbbr_ref[0] += _dot_tn(u_b, gr_b)
        dbbi_ref[0] += _dot_tn(u_b, gi_b)
        dcr_ref[0] += _dot_tn(xr_b, dy_b)
        dci_ref[0] -= _dot_tn(xi_b, dy_b)
        dd_ref[0] += jnp.sum((dy_t * u_t).reshape(tc // SUBLANES, SUBLANES, LANES), axis=0)
        first = row == 0
        xpr = jnp.where(first, jnp.broadcast_to(s0r[0:1], xr.shape), pltpu.roll(xr, 1, 0))
        xpi = jnp.where(first, jnp.broadcast_to(s0i[0:1], xi.shape), pltpu.roll(xi, 1, 0))
        shp = (tc // SUBLANES, SUBLANES, S5_STATES)
        dar_ref[0] += jnp.sum((gr * xpr + gi * xpi).reshape(shp), axis=0)
        dai_ref[0] += jnp.sum((gi * xpr - gr * xpi).reshape(shp), axis=0)

    u_spec = pl.BlockSpec((tc, LANES), lambda cb, b, t: (b * nt + nt - 1 - t, cb))
    a_spec = pl.BlockSpec((1, 1, S5_STATES), lambda cb, b, t: (cb, 0, 0))
    bb_spec = pl.BlockSpec((1, LANES, S5_STATES), lambda cb, b, t: (cb, 0, 0))
    c_spec = pl.BlockSpec((1, S5_STATES, LANES), lambda cb, b, t: (cb, 0, 0))
    st_spec = pl.BlockSpec((1, 1, SUBLANES, S5_STATES), lambda cb, b, t: (cb, b * nt + nt - 1 - t, 0, 0))
    da_spec = pl.BlockSpec((1, SUBLANES, S5_STATES), lambda cb, b, t: (cb, 0, 0))
    dd_spec = pl.BlockSpec((1, SUBLANES, LANES), lambda cb, b, t: (cb, 0, 0))
    big = pltpu.VMEM((tc, S5_STATES), F32)
    small = pltpu.VMEM((SUBLANES, S5_STATES), F32)
    x_spec = pl.BlockSpec((tc, S5_STATES), lambda cb, b, t: (b * nt + nt - 1 - t, cb))
    return pl.pallas_call(
        body, name="s5_bwd", grid=(S5_BLOCKS, n_seq, nt),
        in_specs=[u_spec, u_spec, x_spec, x_spec, st_spec, st_spec, a_spec, a_spec, bb_spec, bb_spec, c_spec, c_spec,
                  pl.BlockSpec((1, LANES), lambda cb, b, t: (0, cb))],
        out_specs=(u_spec, bb_spec, bb_spec, c_spec, c_spec, da_spec, da_spec, dd_spec),
        out_shape=(jax.ShapeDtypeStruct((n, D_SSM), F32),
                   jax.ShapeDtypeStruct((S5_BLOCKS, LANES, S5_STATES), F32),
                   jax.ShapeDtypeStruct((S5_BLOCKS, LANES, S5_STATES), F32),
                   jax.ShapeDtypeStruct((S5_BLOCKS, S5_STATES, LANES), F32),
                   jax.ShapeDtypeStruct((S5_BLOCKS, S5_STATES, LANES), F32),
                   jax.ShapeDtypeStruct((S5_BLOCKS, SUBLANES, S5_STATES), F32),
                   jax.ShapeDtypeStruct((S5_BLOCKS, SUBLANES, S5_STATES), F32),
                   jax.ShapeDtypeStruct((S5_BLOCKS, SUBLANES, LANES), F32)),
        scratch_shapes=[big, big, small, small],
        compiler_params=_cparams("parallel", "arbitrary", "arbitrary"),
    )(u, dy, xs_r, xs_i, st_r, st_i, a_re, a_im, bbr, bbi, cr, ci, d_skip)


CUM_BLOCK = 128


def _tri(lower):
    r = lax.broadcasted_iota(jnp.int32, (CUM_BLOCK, CUM_BLOCK), 0)
    c = lax.broadcasted_iota(jnp.int32, (CUM_BLOCK, CUM_BLOCK), 1)
    return jnp.where(r >= c if lower else r <= c, 1.0, 0.0).astype(F32)


def _fprep_fwd(fl, bf, n_seq):
    n = fl.shape[0]
    seq_len = n // n_seq
    nb = seq_len // CUM_BLOCK

    def body(fl_ref, bf_ref, cum_ref):
        tril = _tri(True)
        carry = jnp.zeros((1, LANES), F32)
        for blk in range(nb):
            rows = slice(blk * CUM_BLOCK, (blk + 1) * CUM_BLOCK)
            lf = jax.nn.log_sigmoid(fl_ref[rows, :] + bf_ref[...])
            cs = jnp.dot(tril, lf, preferred_element_type=F32, precision=HIGHEST) + carry
            cum_ref[rows, :] = cs
            carry = cs[CUM_BLOCK - 1:CUM_BLOCK, :]

    spec = pl.BlockSpec((seq_len, LANES), lambda b: (b, 0))
    return pl.pallas_call(
        body, name="fprep_fwd", grid=(n_seq,), in_specs=[spec, pl.BlockSpec((1, LANES), lambda b: (0, 0))],
        out_specs=spec, out_shape=jax.ShapeDtypeStruct((n, LANES), F32), compiler_params=_cparams("parallel"),
    )(fl, bf)


def _fprep_bwd(dcum, fl, bf, n_seq):
    n = fl.shape[0]
    seq_len = n // n_seq
    nb = seq_len // CUM_BLOCK

    def body(dcum_ref, fl_ref, bf_ref, dfl_ref, dbf_ref):
        triu = _tri(False)
        lane = lax.broadcasted_iota(jnp.int32, (CUM_BLOCK, LANES), 1)
        carry = jnp.zeros((1, LANES), F32)
        total = jnp.zeros((1, LANES), F32)
        for blk in reversed(range(nb)):
            rows = slice(blk * CUM_BLOCK, (blk + 1) * CUM_BLOCK)
            rs = jnp.dot(triu, dcum_ref[rows, :], preferred_element_type=F32, precision=HIGHEST) + carry
            carry = rs[0:1, :]
            _, vjp = jax.vjp(jax.nn.log_sigmoid, fl_ref[rows, :] + bf_ref[...])
            dz = jnp.where(lane < N_HEADS, vjp(rs)[0], 0.0)
            dfl_ref[rows, :] = dz
            total = total + jnp.sum(dz, axis=0, keepdims=True)
        dbf_ref[0] = total

    spec = pl.BlockSpec((seq_len, LANES), lambda b: (b, 0))
    return pl.pallas_call(
        body, name="fprep_bwd", grid=(n_seq,), in_specs=[spec, spec, pl.BlockSpec((1, LANES), lambda b: (0, 0))],
        out_specs=(spec, pl.BlockSpec((1, 1, LANES), lambda b: (b, 0, 0))),
        out_shape=(jax.ShapeDtypeStruct((n, LANES), F32), jax.ShapeDtypeStruct((n_seq, 1, LANES), F32)),
        compiler_params=_cparams("parallel"),
    )(dcum, fl, bf)


ATT_TQ = 256
ATT_KSTEP = 256
ATT_PAIRS = 2
ATT_SCALE = HEAD_DIM ** -0.5
NEG_BIG = -1e30


assert ATT_KSTEP == ATT_TQ


def _scores(q_scaled, kb, row_bias, ck, kend):
    s = _dot_nt(q_scaled, kb) - ck
    if row_bias is not None:
        s = s + row_bias
    r = lax.broadcasted_iota(jnp.int32, (ATT_TQ, ATT_TQ), 0)
    c = lax.broadcasted_iota(jnp.int32, (ATT_TQ, ATT_TQ), 1)
    diag = jnp.where(r >= c, s[:, kend - ATT_TQ:], NEG_BIG)
    return diag if kend == ATT_TQ else jnp.concatenate([s[:, :kend - ATT_TQ], diag], axis=1)


def _attn_specs(n_seq, seq_len):
    nq = seq_len // ATT_TQ
    width = ATT_PAIRS * LANES
    per = D_ATTN // width
    q_spec = pl.BlockSpec((ATT_TQ, width), lambda b, h, q: (b * nq + q, h))
    k_spec = pl.BlockSpec((seq_len, width), lambda b, h, q: (b, per + h))
    v_spec = pl.BlockSpec((seq_len, width), lambda b, h, q: (b, 2 * per + h))
    cq_spec = pl.BlockSpec((1, 2 * ATT_PAIRS, ATT_TQ, 1), lambda b, h, q: (b, h, q, 0))
    ck_spec = pl.BlockSpec((1, 2 * ATT_PAIRS, 1, seq_len), lambda b, h, q: (b, h, 0, 0))
    return nq, q_spec, k_spec, v_spec, cq_spec, ck_spec


def _own_cum(cum_ref, head):
    lane = lax.broadcasted_iota(jnp.int32, (1, LANES), 1)
    mine = lane == 2 * ATT_PAIRS * pl.program_id(1) + head
    return jnp.sum(jnp.where(mine, cum_ref[...], 0.0), axis=1, keepdims=True)


def _head_selectors():
    head0 = lax.broadcasted_iota(jnp.int32, (1, LANES), 1) < HEAD_DIM
    return head0, (head0, jnp.logical_not(head0))


def _for_key_range(qi, seq_len, run):
    per = ATT_KSTEP // ATT_TQ
    for g in range(seq_len // ATT_KSTEP):
        pl.when(qi // per == g)(functools.partial(run, (g + 1) * ATT_KSTEP))


def _attn_fwd(qkv, cum, ck, n_seq):
    n = qkv.shape[0]
    seq_len = n // n_seq
    nq, q_spec, k_spec, v_spec, cq_spec, ck_spec = _attn_specs(n_seq, seq_len)
    cum_spec = pl.BlockSpec((ATT_TQ, LANES), lambda b, h, q: (b * nq + q, 0))

    def body(q_ref, k_ref, v_ref, cum_ref, ck_ref, o_ref, lse_ref):
        qi = pl.program_id(2)
        head0, sels = _head_selectors()

        def run(kend):
            for pair in range(ATT_PAIRS):
                win = slice(pair * LANES, (pair + 1) * LANES)
                q2 = q_ref[:, win]
                kb = k_ref[0:kend, win]
                vb = v_ref[0:kend, win]
                outs = []
                for e in range(2):
                    head = 2 * pair + e
                    qe = jnp.where(sels[e], q2 * ATT_SCALE, 0.0).astype(BF16)
                    s = _scores(qe, kb, None, ck_ref[0, head, :, 0:kend], kend)
                    mx = jnp.max(s, axis=1, keepdims=True)
                    p = jnp.exp(s - mx)
                    den = jnp.sum(p, axis=1, keepdims=True)
                    outs.append(_dot(p.astype(BF16), vb) / den)
                    lse_ref[0, head] = _own_cum(cum_ref, head) + mx + jnp.log(den)
                o_ref[:, win] = jnp.where(head0, outs[0], outs[1])

        _for_key_range(qi, seq_len, run)

    return pl.pallas_call(
        body, name="attn_fwd", grid=(n_seq, N_HEADS // 2 // ATT_PAIRS, nq),
        in_specs=[q_spec, k_spec, v_spec, cum_spec, ck_spec],
        out_specs=(q_spec, cq_spec),
        out_shape=(jax.ShapeDtypeStruct((n, D_ATTN), F32), jax.ShapeDtypeStruct((n_seq, N_HEADS, seq_len, 1), F32)),
        compiler_params=_cparams("parallel", "parallel", "parallel"),
    )(qkv, qkv, qkv, cum, ck)


def _attn_bwd(qkv, cum, ck, o, do, lse, n_seq):
    n = qkv.shape[0]
    seq_len = n // n_seq
    nq, q_spec, k_spec, v_spec, cq_spec, ck_spec = _attn_specs(n_seq, seq_len)
    kv_out = pl.BlockSpec((seq_len, ATT_PAIRS * LANES), lambda b, h, q: (b, h))
    cum_spec = pl.BlockSpec((ATT_TQ, LANES), lambda b, h, q: (b * nq + q, 0))

    def body(q_ref, k_ref, v_ref, cum_ref, ck_ref, o_ref, do_ref, lse_ref, dq_ref, dk_ref, dv_ref, dcq_ref, dck_ref):
        qi = pl.program_id(2)

        @pl.when(qi == 0)
        def _():
            dk_ref[...] = jnp.zeros_like(dk_ref)
            dv_ref[...] = jnp.zeros_like(dv_ref)
            dck_ref[...] = jnp.zeros_like(dck_ref)
        head0, sels = _head_selectors()

        def run(kend):
            for pair in range(ATT_PAIRS):
                win = slice(pair * LANES, (pair + 1) * LANES)
                q2 = q_ref[:, win]
                do2 = do_ref[:, win]
                o2 = o_ref[:, win]
                kb = k_ref[0:kend, win]
                vb = v_ref[0:kend, win]
                dqs = []
                dk = jnp.zeros((kend, LANES), F32)
                dv = jnp.zeros((kend, LANES), F32)
                for e in range(2):
                    head = 2 * pair + e
                    qe = jnp.where(sels[e], q2 * ATT_SCALE, 0.0).astype(BF16)
                    doe = jnp.where(sels[e], do2, 0.0)
                    doe_b = doe.astype(BF16)
                    delta = jnp.sum(doe * o2, axis=1, keepdims=True)
                    bias = _own_cum(cum_ref, head) - lse_ref[0, head]
                    p = jnp.exp(_scores(qe, kb, bias, ck_ref[0, head, :, 0:kend], kend))
                    ds = p * (_dot_nt(doe_b, vb) - delta)
                    ds_b = ds.astype(BF16)
                    dqs.append(_dot(ds_b, kb))
                    dk = dk + _dot_tn(ds_b, qe)
                    dv = dv + _dot_tn(p.astype(BF16), doe_b)
                    dcq_ref[0, head] = jnp.sum(ds, axis=1, keepdims=True)
                    dck_ref[0, head, :, 0:kend] -= jnp.sum(ds, axis=0, keepdims=True)
                dk_ref[0:kend, win] += dk
                dv_ref[0:kend, win] += dv
                dq_ref[:, win] = jnp.where(head0, dqs[0], dqs[1]) * ATT_SCALE

        _for_key_range(qi, seq_len, run)

    return pl.pallas_call(
        body, name="attn_bwd", grid=(n_seq, N_HEADS // 2 // ATT_PAIRS, nq),
        in_specs=[q_spec, k_spec, v_spec, cum_spec, ck_spec, q_spec, q_spec, cq_spec],
        out_specs=(q_spec, kv_out, kv_out, cq_spec, ck_spec),
        out_shape=(jax.ShapeDtypeStruct((n, D_ATTN), F32), jax.ShapeDtypeStruct((n, D_ATTN), F32),
                   jax.ShapeDtypeStruct((n, D_ATTN), F32),
                   jax.ShapeDtypeStruct((n_seq, N_HEADS, seq_len, 1), F32),
                   jax.ShapeDtypeStruct((n_seq, N_HEADS, 1, seq_len), F32)),
        compiler_params=_cparams("parallel", "parallel", "arbitrary"),
    )(qkv, qkv, qkv, cum, ck, o, do, lse)


WEIGHT_NAMES = ("norm_mix", "w_in", "b_forget", "lam_re", "lam_im", "b_re", "b_im", "c_re", "c_im", "d_skip", "log_dt",
                "w_glu", "b_glu", "q_norm", "k_norm", "norm_out_ssm", "norm_out_attn", "w_out", "norm_ffn", "w_up",
                "conv_w", "conv_b", "w_down")
SHARDED = ("w_in", "w_glu", "w_out", "w_up", "conv_w", "w_down")
ADAM_TILE = {"w_in": (257, 256), "w_glu": (64, 512), "w_out": (128, 1024), "w_up": (688, 256), "conv_w": (3, 688),
             "w_down": (344, 1024)}
PACK_ROWS = SUBLANES * LANES


def _after_all(*arrays):
    return sum(a[(0,) * a.ndim].astype(F32) for a in arrays).reshape(1, 1)


def _pad_to(a, axis, size):
    pad = [(0, 0)] * a.ndim
    pad[axis] = (0, size - a.shape[axis])
    return jnp.pad(a, pad)


def _block_diag(t, transpose):
    t4 = t.reshape(S5_BLOCKS, 8, SSM_GROUP, SSM_STATE)
    eye = jnp.eye(8, dtype=t.dtype)
    if transpose:
        e = jnp.swapaxes(t4, 2, 3)[:, :, :, None, :] * eye[None, :, None, :, None]
        return e.reshape(S5_BLOCKS, S5_STATES, LANES)
    e = t4[:, :, :, None, :] * eye[None, :, None, :, None]
    return e.reshape(S5_BLOCKS, LANES, S5_STATES)


def _block_diag_extract(m, transpose):
    if transpose:
        m5 = m.reshape(S5_BLOCKS, 8, SSM_STATE, 8, SSM_GROUP)
        d = jnp.stack([m5[:, i, :, i, :] for i in range(8)], axis=1)
        return jnp.swapaxes(d, 2, 3).reshape(N_GROUPS, SSM_GROUP, SSM_STATE)
    m5 = m.reshape(S5_BLOCKS, 8, SSM_GROUP, 8, SSM_STATE)
    d = jnp.stack([m5[:, i, :, i, :] for i in range(8)], axis=1)
    return d.reshape(N_GROUPS, SSM_GROUP, SSM_STATE)


def _pack(pieces):
    flat = jnp.concatenate([p.reshape(-1).astype(F32) for p in pieces])
    size = -(-flat.shape[0] // PACK_ROWS) * PACK_ROWS
    return _pad_to(flat, 0, size).reshape(-1, LANES)


def _unpack(packed, shapes):
    flat = packed.reshape(-1)
    out, off = [], 0
    for shp in shapes:
        size = math.prod(shp)
        out.append(flat[off:off + size].reshape(shp))
        off += size
    return out


def kernel(x, norm_mix, w_in, b_forget, lam_re, lam_im, b_re, b_im, c_re, c_im, d_skip, log_dt, w_glu, b_glu, q_norm, k_norm, norm_out_ssm, norm_out_attn, w_out, norm_ffn, w_up, conv_w, conv_b, w_down, loss_target, m_norm_mix, m_w_in, m_b_forget, m_lam_re, m_lam_im, m_b_re, m_b_im, m_c_re, m_c_im, m_d_skip, m_log_dt, m_w_glu, m_b_glu, m_q_norm, m_k_norm, m_norm_out_ssm, m_norm_out_attn, m_w_out, m_norm_ffn, m_w_up, m_conv_w, m_conv_b, m_w_down, v_norm_mix, v_w_in, v_b_forget, v_lam_re, v_lam_im, v_b_re, v_b_im, v_c_re, v_c_im, v_d_skip, v_log_dt, v_w_glu, v_b_glu, v_q_norm, v_k_norm, v_norm_out_ssm, v_norm_out_attn, v_w_out, v_norm_ffn, v_w_up, v_conv_w, v_conv_b, v_w_down):
    given = dict(locals())
    weights = {k: given[k] for k in WEIGHT_NAMES}
    mom1 = {k: given["m_" + k] for k in WEIGHT_NAMES}
    mom2 = {k: given["v_" + k] for k in WEIGHT_NAMES}
    n_seq, seq_len, _ = x.shape
    n = n_seq * seq_len
    xf = x.reshape(n, D_MODEL)
    target = loss_target.reshape(n, D_MODEL)
    me_idx = 4 * lax.axis_index("x") + 2 * lax.axis_index("y") + lax.axis_index("c")

    in_flags = [False] * 2
    in_sems = _exchange_start([jnp.swapaxes(w_in[0], 0, 1).astype(BF16), conv_w[0]], in_flags, norm_mix,
                              "gather_in_start", 3, NEAR_PEERS)
    fill_own = lambda got, mine: lax.dynamic_update_index_in_dim(got, mine, me_idx, 0)

    lr3 = lam_re[0].reshape(N_GROUPS, 1, SSM_STATE)
    li3 = lam_im[0].reshape(N_GROUPS, 1, SSM_STATE)
    ldt3 = log_dt[0].reshape(N_GROUPS, 1, 1)
    br_t = jnp.swapaxes(b_re[0], 1, 2)
    bi_t = jnp.swapaxes(b_im[0], 1, 2)
    ab_re, ab_im, bb_re, bb_im = _s5_param_fwd(lr3, li3, ldt3, br_t, bi_t)
    a_re = ab_re.reshape(S5_BLOCKS, 1, S5_STATES)
    a_im = ab_im.reshape(S5_BLOCKS, 1, S5_STATES)
    bbr = _block_diag(bb_re, False).astype(BF16)
    bbi = _block_diag(bb_im, False).astype(BF16)
    cr = _block_diag(c_re[0], True).astype(BF16)
    ci = _block_diag(c_im[0], True).astype(BF16)
    avg = jnp.kron(jnp.eye(N_HEADS, dtype=F32), jnp.full((HEAD_DIM, HEAD_DIM), 1.0 / HEAD_DIM, F32)).astype(BF16)
    qg = jnp.tile(q_norm, (1, N_HEADS))
    kg = jnp.tile(k_norm, (1, N_HEADS))
    row_shards = [w_down[0].astype(BF16), w_out[0].astype(BF16), w_glu[0].astype(BF16)]

    (own_in, own_cw), near = _exchange_wait(in_sems[0], in_sems[1], in_sems[2], in_sems[3], in_flags,
                                            _after_all(a_re, a_im, bbr, bbi, cr, ci, avg, qg, kg, *row_shards),
                                            "gather_in_wait", NEAR_PEERS)
    relay = _relay_start(list(near), "gather_in_relay_start", 6)
    g_in, g_cw = _relay_wait(relay[0], relay[1], relay[2], relay[3], "gather_in_relay_wait")
    g_in = fill_own(g_in, own_in)
    g_cw = fill_own(g_cw, own_cw)
    row_flags = [False] * 3
    r_sems = _exchange_start(row_shards, row_flags, g_in, "gather_rows_start", 0)
    u_sems = _exchange_start([jnp.swapaxes(w_up[0], 0, 1).astype(BF16)], [False], r_sems[4], "gather_up_start", 5)
    norm_mix = norm_mix + u_sems[4][0, 0]
    w_in_p = _pad_to(g_in.reshape(D_IN, D_MODEL), 0, D_IN_PAD)

    hn, u, qkv, raw, fl = _inproj_fwd(xf, norm_mix, w_in_p, avg, qg, kg)
    yc, st_r, st_i, xs_r, xs_i = _s5_fwd(u, a_re, a_im, bbr, bbi, cr, ci, d_skip, n_seq)
    bf = _pad_to(b_forget, 1, LANES)
    cum = _fprep_fwd(fl, bf, n_seq)
    cum8 = jnp.swapaxes(cum[:, :N_HEADS].reshape(n_seq, seq_len, N_HEADS), 1, 2)
    ck = cum8[:, :, None, :]
    ya, lse = _attn_fwd(qkv, cum, ck, n_seq)
    own_rows, got_rows = _exchange_wait(r_sems[0], r_sems[1], r_sems[2], r_sems[3], row_flags, ya, "gather_rows_wait")
    g_down, g_out, g_glu = [fill_own(g, o) for o, g in zip(own_rows, got_rows)]
    w_glu_f = g_glu.reshape(D_SSM, D_SSM)
    w_out_f = g_out.reshape(D_MODEL, D_MODEL)
    conv_st = _pad_to(jnp.concatenate([g_cw, conv_b.reshape(N_DEV, 1, -1)], axis=1), 1, SUBLANES)
    w_down4 = g_down.reshape(FFN_GROUPS, FFN_GROUP, D_MODEL)
    h1, hn2, mixed = _mix_fwd(xf, yc, ya, w_glu_f, b_glu, norm_out_ssm, norm_out_attn, w_out_f, norm_ffn)
    (own_up,), (g_up,) = _exchange_wait(u_sems[0], u_sems[1], u_sems[2], u_sems[3], [False], _after_all(h1, hn2),
                                        "gather_up_wait")
    g_up = fill_own(g_up, own_up)
    ug, uv, pg, pv, dy, loss_part = _ffn_fwd(hn2, h1, target, g_up, conv_st, w_down4, seq_len)
    loss_local = 0.5 * jnp.sum(loss_part) / D_MODEL

    dug, duv, act, dhn2, dcg, dcv = _ffn_bwd(dy, ug, uv, pg, pv, g_up, conv_st, w_down4, seq_len)
    dh1, dyc, dya, gw_out, gw_glu, d_gs, d_ga, d_gf, d_bglu = _mix_bwd(
        dy, dhn2[None], h1, yc, ya, w_glu_f, b_glu, norm_out_ssm, norm_out_attn, w_out_f, norm_ffn)

    gw_glu, gw_out = gw_glu.astype(BF16), gw_out.astype(BF16)
    gw_up = jnp.concatenate([_tn_grouped(dug, hn2, "dw_up_gate", False, BF16, tn=n),
                             _tn_grouped(duv, hn2, "dw_up_val", False, BF16, tn=n)], axis=0)
    gw_down = _tn_grouped(act, dy, "dw_down", False, BF16, tn=n // 2)
    g_conv = jnp.concatenate([dcg, dcv], axis=0)
    by_cols = lambda g, c: jnp.swapaxes(g.reshape(g.shape[0], N_DEV, c), 0, 1)
    early_flags = [True] * 4
    early_names = ("w_down", "w_out", "w_glu", "w_up")
    g_sems = _exchange_start(
        [gw_down.reshape(N_DEV, -1, D_MODEL), gw_out.reshape(N_DEV, -1, D_MODEL), gw_glu.reshape(N_DEV, -1, D_SSM), gw_up],
        early_flags, dyc, "grad_early_start", 1)
    started = g_sems[4][0, 0]

    du, dbbr, dbbi, dcr, dci, dar, dai, ddk = _s5_bwd(u, dyc, xs_r, xs_i, st_r, st_i, a_re, a_im, bbr, bbi, cr, ci,
                                                      d_skip + started, n_seq)
    partial_early = {
        "ab_re": jnp.sum(dar, axis=1), "ab_im": jnp.sum(dai, axis=1),
        "bb_re": _block_diag_extract(dbbr, False), "bb_im": _block_diag_extract(dbbi, False),
        "c_re": _block_diag_extract(dcr, True), "c_im": _block_diag_extract(dci, True),
        "d_skip": jnp.sum(ddk, axis=1), "b_glu": d_bglu,
        "norm_out_ssm": d_gs, "norm_out_attn": d_ga, "norm_ffn": d_gf, "conv_b": g_conv[:, 3],
    }
    early_keys = tuple(partial_early)
    early_shapes = [partial_early[k].shape for k in early_keys]
    p_sems = _exchange_start([_pack([partial_early[k] for k in early_keys])], [False], du, "small_early_start", 2)
    started = started + p_sems[4][0, 0]

    dqn, dkn, dv, dcq, dck = _attn_bwd(qkv, cum, ck + started, ya, dya, lse, n_seq)
    dcum8 = dcq[:, :, :, 0] + dck.reshape(n_seq, N_HEADS, seq_len)
    dcum = _pad_to(jnp.swapaxes(dcum8, 1, 2).reshape(n, N_HEADS), 1, LANES)
    dfl, dbf = _fprep_bwd(dcum, fl, bf, n_seq)
    dx, dproj, d_gmix, d_qg, d_kg = _inproj_bwd(xf, norm_mix, w_in_p, avg, qg, kg, raw, du, dqn, dkn, dv, dfl, dh1)

    gw_in = _tn_matmul(dproj, hn, "dw_in", D_IN_PAD, D_MODEL, out_rows=D_IN, out_dtype=BF16, tn=n // 4)
    partial_late = {
        "norm_mix": d_gmix, "b_forget": jnp.sum(dbf, axis=(0, 1))[:N_HEADS],
        "q_norm": jnp.sum(d_qg.reshape(N_HEADS, HEAD_DIM), axis=0),
        "k_norm": jnp.sum(d_kg.reshape(N_HEADS, HEAD_DIM), axis=0), "loss": loss_local.reshape(1),
    }
    late_keys = tuple(partial_late)
    late_shapes = [partial_late[k].shape for k in late_keys]

    late_flags = [True, True, False]
    l_sems = _exchange_start(
        [gw_in.reshape(N_DEV, D_IN // N_DEV, D_MODEL).astype(BF16), g_conv[:, :3],
         _pack([partial_late[k] for k in late_keys])],
        late_flags, dx, "grad_late_start", 4)
    early_src, early_land = _exchange_wait(g_sems[0], g_sems[1], g_sems[2], g_sems[3], early_flags, l_sems[4],
                                           "grad_early_wait")
    land = dict(zip(early_names, early_land))
    land_up = land["w_up"]
    own = {k: lax.dynamic_index_in_dim(s, me_idx, 0, keepdims=False) for k, s in zip(early_names, early_src)}
    grads, deltas, new_m, new_v = {}, {}, {}, {}

    def adam_shard(name):
        flip = (lambda a: jnp.swapaxes(a, 1, 2)) if name in ("w_in", "w_up") else (lambda a: a)
        outs = _adam_sharded(land[name], own[name], flip(weights[name]), flip(mom1[name]), flip(mom2[name]),
                             "adam_" + name, ADAM_TILE[name])
        grads[name], deltas[name], new_m[name], new_v[name] = [flip(o) for o in outs]

    for name in ("w_up", "w_down", "w_out", "w_glu"):
        adam_shard(name)
    (own_pack,), (early_parts,) = _exchange_wait(p_sems[0], p_sems[1], p_sems[2], p_sems[3], [False], land_up,
                                                 "small_early_wait")
    early_sum = _sum_partials(early_parts, own_pack, "sum_early_partials")
    (src_in, src_cw, own_late), (land["w_in"], land["conv_w"], late_parts) = _exchange_wait(
        l_sems[0], l_sems[1], l_sems[2], l_sems[3], late_flags,
        _after_all(early_sum, *[new_v[k] for k in ("w_up", "w_down", "w_out", "w_glu")]), "grad_late_wait")
    own["w_in"] = lax.dynamic_index_in_dim(src_in, me_idx, 0, keepdims=False)
    own["conv_w"] = lax.dynamic_index_in_dim(src_cw, me_idx, 0, keepdims=False)
    for name in ("w_in", "conv_w"):
        adam_shard(name)

    summed = dict(zip(late_keys, _unpack(_sum_partials(late_parts, own_late, "sum_late_partials"), late_shapes)))
    summed.update(zip(early_keys, _unpack(early_sum, early_shapes)))
    dlr, dli, dldt, dbr_t, dbi_t = _s5_param_bwd(
        lr3, li3, ldt3, br_t, bi_t, summed["ab_re"].reshape(lr3.shape), summed["ab_im"].reshape(lr3.shape),
        summed["bb_re"], summed["bb_im"])
    small_grads = {
        "norm_mix": summed["norm_mix"], "b_forget": summed["b_forget"], "lam_re": dlr, "lam_im": dli,
        "b_re": dbr_t, "b_im": dbi_t, "c_re": summed["c_re"], "c_im": summed["c_im"],
        "d_skip": summed["d_skip"], "log_dt": dldt, "b_glu": summed["b_glu"], "q_norm": summed["q_norm"],
        "k_norm": summed["k_norm"], "norm_out_ssm": summed["norm_out_ssm"], "norm_out_attn": summed["norm_out_attn"],
        "norm_ffn": summed["norm_ffn"], "conv_b": summed["conv_b"],
    }
    repl = tuple(k for k in WEIGHT_NAMES if k not in SHARDED)
    turn = lambda k, a: jnp.swapaxes(a, 2, 3) if k in ("b_re", "b_im") else a
    w_list = [turn(k, weights[k]) for k in repl]
    g_list = [small_grads[k].reshape(w.shape) for k, w in zip(repl, w_list)]
    d_list, m_list, v_list = _adam_replicated(g_list, w_list, [turn(k, mom1[k]) for k in repl],
                                              [turn(k, mom2[k]) for k in repl], "adam_replicated")
    for k, g, d, nm, nv in zip(repl, g_list, d_list, m_list, v_list):
        grads[k], deltas[k], new_m[k], new_v[k] = turn(k, g), turn(k, d), turn(k, nm), turn(k, nv)

    grad_x = dx.reshape(x.shape)
    loss = summed["loss"].reshape(())
    return (loss, grad_x, *[grads[k] for k in WEIGHT_NAMES], *[deltas[k] for k in WEIGHT_NAMES],
            *[new_m[k] for k in WEIGHT_NAMES], *[new_v[k] for k in WEIGHT_NAMES])
```
